```python
import jax, jax.numpy as jnp
from jax import lax
import numpy as np

D_MODEL = 1024
BATCH = 8
SEQ = 2048
DEPTH = 1

D_MIX = D_MODEL
SB_HEAD_DIM = 64
SB_WIDTH = D_MIX // 2
SB_HEADS = SB_WIDTH // SB_HEAD_DIM
GDN_HEAD_DIM = 128
GDN_WIDTH = D_MIX - SB_WIDTH
GDN_HEADS = GDN_WIDTH // GDN_HEAD_DIM
CONV_WIDTH = 4
CHUNK = 64
BLOCK_Q = 128
EPS = 1e-6
IN_SPLITS = (SB_WIDTH, SB_WIDTH, SB_WIDTH, SB_WIDTH,
             GDN_WIDTH, GDN_WIDTH, GDN_WIDTH, GDN_WIDTH,
             GDN_HEADS, GDN_HEADS)
D_IN = sum(IN_SPLITS)

kernel_name = "hybrid_stickbreak_gated_deltanet_block"


def rmsnorm(x, w):
    x32 = x.astype(jnp.float32)
    y = x32 * lax.rsqrt(jnp.mean(x32 * x32, axis=-1, keepdims=True) + EPS)
    return (y * w.astype(jnp.float32)).astype(x.dtype)


def l2norm(x):
    x32 = x.astype(jnp.float32)
    return x32 * lax.rsqrt(jnp.sum(x32 * x32, axis=-1, keepdims=True) + EPS)


def causal_depthwise_conv(x, w):
    k_taps, t_len = w.shape[0], x.shape[1]
    xp = jnp.pad(x, ((0, 0), (k_taps - 1, 0), (0, 0)))
    out = xp[:, 0:t_len] * w[0]
    for i in range(1, k_taps):
        out = out + xp[:, i:i + t_len] * w[i]
    return out


def stick_breaking_attention(q, k, v):
    t_len, d = q.shape[2], q.shape[3]
    scale = float(1.0 / np.sqrt(d))
    outs = []
    for blk in range(t_len // BLOCK_Q):
        start, end = blk * BLOCK_Q, (blk + 1) * BLOCK_Q
        qb = q[:, :, start:end]
        kb, vb = k[:, :, :end], v[:, :, :end]
        z = jnp.einsum('bhqd,bhkd->bhqk', qb, kb).astype(jnp.float32) * scale
        t_idx = start + jnp.arange(BLOCK_Q)
        s_idx = jnp.arange(end)
        causal = s_idx[None, :] < t_idx[:, None]
        sp = jnp.where(causal, jax.nn.softplus(z), 0.0)
        tail = lax.cumsum(sp, axis=sp.ndim - 1, reverse=True) - sp
        attn = jnp.where(causal, jnp.exp(jax.nn.log_sigmoid(z) - tail), 0.0)
        outs.append(jnp.einsum('bhqk,bhkd->bhqd', attn.astype(v.dtype), vb))
    return jnp.concatenate(outs, axis=2)


def gated_delta_chunked(q, k, v, g, beta):
    out_dtype = v.dtype
    b, h, t_len, dk = q.shape
    dv = v.shape[-1]
    n_chunks = t_len // CHUNK
    q = q.astype(jnp.float32) * float(dk ** -0.5)
    k = k.astype(jnp.float32)
    v = v.astype(jnp.float32)
    resh = lambda a: a.reshape((b, h, n_chunks, CHUNK) + a.shape[3:])
    q, k, v, g, beta = resh(q), resh(k), resh(v), resh(g.astype(jnp.float32)), resh(beta.astype(jnp.float32))
    g = lax.cumsum(g, axis=g.ndim - 1)
    incl = jnp.tril(jnp.ones((CHUNK, CHUNK), dtype=bool))
    strict = jnp.tril(jnp.ones((CHUNK, CHUNK), dtype=bool), k=-1)
    decay = jnp.exp(jnp.where(incl, g[..., :, None] - g[..., None, :], -jnp.inf))
    k_beta = k * beta[..., None]
    v_beta = v * beta[..., None]
    a_mat = jnp.where(strict, jnp.einsum('bhnid,bhnjd->bhnij', k_beta, k) * decay, 0.0)
    eye = jnp.eye(CHUNK, dtype=jnp.float32)
    rhs = jnp.concatenate([v_beta, k_beta * jnp.exp(g)[..., None]], axis=-1)
    sol = lax.linalg.triangular_solve(a_mat + eye, rhs, left_side=True, lower=True,
                                      unit_diagonal=True)
    u, w = sol[..., :dv], sol[..., dv:]
    attn_intra = jnp.where(incl, jnp.einsum('bhnid,bhnjd->bhnij', q, k) * decay, 0.0)

    def step(state, inp):
        q_c, k_c, u_c, w_c, g_c, a_c = inp
        v_new = u_c - jnp.einsum('bhcd,bhde->bhce', w_c, state)
        o_c = jnp.einsum('bhcd,bhde->bhce', q_c * jnp.exp(g_c)[..., None], state) \
            + jnp.einsum('bhij,bhje->bhie', a_c, v_new)
        g_last = g_c[..., -1]
        state = state * jnp.exp(g_last)[..., None, None] + jnp.einsum(
            'bhcd,bhce->bhde', k_c * jnp.exp(g_last[..., None] - g_c)[..., None], v_new)
        return state, o_c

    mv = lambda a: jnp.moveaxis(a, 2, 0)
    state0 = jnp.zeros((b, h, dk, dv), dtype=jnp.float32)
    _, o = lax.scan(step, state0, (mv(q), mv(k), mv(u), mv(w), mv(g), mv(attn_intra)))
    o = jnp.moveaxis(o, 0, 2).reshape(b, h, t_len, dv)
    return o.astype(out_dtype)


def _fwd_setup_inputs(seed: int = 0) -> dict:
    key = jax.random.key(seed)
    ks = jax.random.split(key, 10)
    f32 = jnp.float32
    x = jax.random.normal(ks[0], (BATCH, SEQ, D_MODEL), f32)
    norm1_w = 1.0 + 0.02 * jax.random.normal(ks[1], (DEPTH, D_MODEL), f32)
    w_in = jax.random.normal(ks[2], (DEPTH, D_MODEL, D_IN), f32) * D_MODEL ** -0.5
    sb_norm_w = 1.0 + 0.02 * jax.random.normal(ks[3], (DEPTH, SB_HEAD_DIM), f32)
    gdn_conv_w = jax.random.normal(ks[4], (DEPTH, CONV_WIDTH, 3 * GDN_WIDTH), f32) * CONV_WIDTH ** -0.5
    gdn_A_log = jnp.log(jax.random.uniform(ks[5], (DEPTH, GDN_HEADS), f32, 1.0, 16.0))
    dt = jnp.exp(jax.random.uniform(ks[6], (DEPTH, GDN_HEADS), f32, float(np.log(1e-3)), float(np.log(1e-1))))
    gdn_dt_bias = dt + jnp.log(-jnp.expm1(-dt))
    gdn_norm_w = 1.0 + 0.02 * jax.random.normal(ks[7], (DEPTH, GDN_HEAD_DIM), f32)
    w_out = jax.random.normal(ks[8], (DEPTH, D_MIX, D_MODEL), f32) * D_MIX ** -0.5
    final_norm_w = 1.0 + 0.02 * jax.random.normal(ks[9], (D_MODEL,), f32)
    return {"x": x, "norm1_w": norm1_w, "w_in": w_in, "sb_norm_w": sb_norm_w,
            "gdn_conv_w": gdn_conv_w, "gdn_A_log": gdn_A_log, "gdn_dt_bias": gdn_dt_bias,
            "gdn_norm_w": gdn_norm_w, "w_out": w_out, "final_norm_w": final_norm_w}


def _fwd_reference(x, norm1_w, w_in, sb_norm_w, gdn_conv_w, gdn_A_log, gdn_dt_bias,
              gdn_norm_w, w_out, final_norm_w):
    b, t_len, _ = x.shape
    split_idx = [int(s) for s in np.cumsum(IN_SPLITS)[:-1]]
    to_heads = lambda a, n, d: a.reshape(b, t_len, n, d).transpose(0, 2, 1, 3)
    for layer in range(DEPTH):
        h = rmsnorm(x, norm1_w[layer])
        proj = jnp.einsum('btd,de->bte', h, w_in[layer])
        sb_q, sb_k, sb_v, sb_z, g_q, g_k, g_v, g_z, g_b, g_a = jnp.split(proj, split_idx, axis=-1)

        o_sb = stick_breaking_attention(to_heads(sb_q, SB_HEADS, SB_HEAD_DIM),
                                        to_heads(sb_k, SB_HEADS, SB_HEAD_DIM),
                                        to_heads(sb_v, SB_HEADS, SB_HEAD_DIM))
        o_sb = rmsnorm(o_sb.transpose(0, 2, 1, 3), sb_norm_w[layer]).reshape(b, t_len, SB_WIDTH)
        o_sb = o_sb * jax.nn.silu(sb_z)

        qkv = jax.nn.silu(causal_depthwise_conv(jnp.concatenate([g_q, g_k, g_v], axis=-1),
                                                gdn_conv_w[layer]))
        gq, gk, gv = jnp.split(qkv, 3, axis=-1)
        gq = l2norm(to_heads(gq, GDN_HEADS, GDN_HEAD_DIM)).astype(x.dtype)
        gk = l2norm(to_heads(gk, GDN_HEADS, GDN_HEAD_DIM)).astype(x.dtype)
        gv = to_heads(gv, GDN_HEADS, GDN_HEAD_DIM)
        beta = jax.nn.sigmoid(g_b.astype(jnp.float32)).transpose(0, 2, 1)
        decay = (-jnp.exp(gdn_A_log[layer].astype(jnp.float32))
                 * jax.nn.softplus(g_a.astype(jnp.float32) + gdn_dt_bias[layer].astype(jnp.float32)))
        decay = decay.transpose(0, 2, 1)
        o_gdn = gated_delta_chunked(gq, gk, gv, decay, beta)
        o_gdn = rmsnorm(o_gdn.transpose(0, 2, 1, 3), gdn_norm_w[layer]).reshape(b, t_len, GDN_WIDTH)
        o_gdn = o_gdn * jax.nn.silu(g_z)

        mixed = jnp.concatenate([o_sb, o_gdn], axis=-1)
        x = x + jnp.einsum('bte,ed->btd', mixed, w_out[layer])
    return rmsnorm(x, final_norm_w)


import jax as _jax
import jax.numpy as _jnp

TWIN_FORMAT = 'train_step'
FWD_PARAMS = ['x', 'norm1_w', 'w_in', 'sb_norm_w', 'gdn_conv_w', 'gdn_A_log', 'gdn_dt_bias', 'gdn_norm_w', 'w_out', 'final_norm_w']
TWIN_WEIGHTS = ['norm1_w', 'w_in', 'sb_norm_w', 'gdn_conv_w', 'gdn_A_log', 'gdn_dt_bias', 'gdn_norm_w', 'w_out', 'final_norm_w']
TWIN_DIFF_INPUT = 'x'
TWIN_INPUTS = ['x', 'norm1_w', 'w_in', 'sb_norm_w', 'gdn_conv_w', 'gdn_A_log', 'gdn_dt_bias', 'gdn_norm_w', 'w_out', 'final_norm_w', 'loss_target', 'm_norm1_w', 'm_w_in', 'm_sb_norm_w', 'm_gdn_conv_w', 'm_gdn_A_log', 'm_gdn_dt_bias', 'm_gdn_norm_w', 'm_w_out', 'm_final_norm_w', 'v_norm1_w', 'v_w_in', 'v_sb_norm_w', 'v_gdn_conv_w', 'v_gdn_A_log', 'v_gdn_dt_bias', 'v_gdn_norm_w', 'v_w_out', 'v_final_norm_w']
TWIN_OUTPUTS = ['loss', 'grad_x', 'grad_norm1_w', 'grad_w_in', 'grad_sb_norm_w', 'grad_gdn_conv_w', 'grad_gdn_A_log', 'grad_gdn_dt_bias', 'grad_gdn_norm_w', 'grad_w_out', 'grad_final_norm_w', 'delta_norm1_w', 'delta_w_in', 'delta_sb_norm_w', 'delta_gdn_conv_w', 'delta_gdn_A_log', 'delta_gdn_dt_bias', 'delta_gdn_norm_w', 'delta_w_out', 'delta_final_norm_w', 'new_m_norm1_w', 'new_m_w_in', 'new_m_sb_norm_w', 'new_m_gdn_conv_w', 'new_m_gdn_A_log', 'new_m_gdn_dt_bias', 'new_m_gdn_norm_w', 'new_m_w_out', 'new_m_final_norm_w', 'new_v_norm1_w', 'new_v_w_in', 'new_v_sb_norm_w', 'new_v_gdn_conv_w', 'new_v_gdn_A_log', 'new_v_gdn_dt_bias', 'new_v_gdn_norm_w', 'new_v_w_out', 'new_v_final_norm_w']
TWIN_LEAF_KINDS = {'loss': 'loss', 'grad_x': 'grad_x', 'grad_norm1_w': 'grad_w', 'grad_w_in': 'grad_w', 'grad_sb_norm_w': 'grad_w', 'grad_gdn_conv_w': 'grad_w', 'grad_gdn_A_log': 'grad_w', 'grad_gdn_dt_bias': 'grad_w', 'grad_gdn_norm_w': 'grad_w', 'grad_w_out': 'grad_w', 'grad_final_norm_w': 'grad_w', 'delta_norm1_w': 'delta_w', 'delta_w_in': 'delta_w', 'delta_sb_norm_w': 'delta_w', 'delta_gdn_conv_w': 'delta_w', 'delta_gdn_A_log': 'delta_w', 'delta_gdn_dt_bias': 'delta_w', 'delta_gdn_norm_w': 'delta_w', 'delta_w_out': 'delta_w', 'delta_final_norm_w': 'delta_w', 'new_m_norm1_w': 'new_m', 'new_m_w_in': 'new_m', 'new_m_sb_norm_w': 'new_m', 'new_m_gdn_conv_w': 'new_m', 'new_m_gdn_A_log': 'new_m', 'new_m_gdn_dt_bias': 'new_m', 'new_m_gdn_norm_w': 'new_m', 'new_m_w_out': 'new_m', 'new_m_final_norm_w': 'new_m', 'new_v_norm1_w': 'new_v', 'new_v_w_in': 'new_v', 'new_v_sb_norm_w': 'new_v', 'new_v_gdn_conv_w': 'new_v', 'new_v_gdn_A_log': 'new_v', 'new_v_gdn_dt_bias': 'new_v', 'new_v_gdn_norm_w': 'new_v', 'new_v_w_out': 'new_v', 'new_v_final_norm_w': 'new_v'}


def _forward(args):
    return _fwd_reference(*[args[k] for k in FWD_PARAMS])


def _output_shape():
    out = _jax.eval_shape(lambda: _forward(_fwd_setup_inputs(0)))
    return out.shape, out.dtype

N_MICROBATCH = 1
ADAM_LR = 0.001
ADAM_B1 = 0.9
ADAM_B2 = 0.999
ADAM_EPS = 1e-08
ADAM_WD = 0.01
ADAM_STEP = 10
PER_EXAMPLE_BATCH_AXIS = {'x': 0, 'loss_target': 0}
SHARED_INPUTS = []
_WEIGHT_DTYPES = {'norm1_w': _jnp.float32, 'w_in': _jnp.float32, 'sb_norm_w': _jnp.float32, 'gdn_conv_w': _jnp.float32, 'gdn_A_log': _jnp.float32, 'gdn_dt_bias': _jnp.float32, 'gdn_norm_w': _jnp.float32, 'w_out': _jnp.float32, 'final_norm_w': _jnp.float32}
MOMENT_SCALE = {'norm1_w': 1.051825e-01, 'w_in': 5.277609e-02, 'sb_norm_w': 1.977610e-01, 'gdn_conv_w': 4.951464e-02, 'gdn_A_log': 2.841824e-01, 'gdn_dt_bias': 2.763123e-01, 'gdn_norm_w': 1.357792e-01, 'w_out': 6.327447e-02, 'final_norm_w': 1.599184e+01}


def _to_microbatches(a, axis):
    t = _jnp.moveaxis(a, axis, 0)
    t = t.reshape((N_MICROBATCH, t.shape[0] // N_MICROBATCH) + t.shape[1:])
    return _jnp.moveaxis(t, 1, axis + 1)


def setup_inputs(seed: int = 0) -> dict:
    inp = _fwd_setup_inputs(seed)
    key = _jax.random.fold_in(_jax.random.key(seed), 7919)
    shape, _ = _output_shape()
    out = dict(inp)
    out["loss_target"] = _jax.random.normal(_jax.random.fold_in(key, 0), shape, _jnp.float32)
    for i, name in enumerate(TWIN_WEIGHTS):
        w = inp[name].astype(_jnp.float32)
        if MOMENT_SCALE is None:
            s = _jnp.sqrt(_jnp.mean(_jnp.square(w)) + 1e-30)
        else:
            s = MOMENT_SCALE[name]
        km, kv = _jax.random.split(_jax.random.fold_in(key, i + 1))
        out[name] = w
        out["m_" + name] = s * _jax.random.normal(km, w.shape, _jnp.float32)
        out["v_" + name] = (s * s) * _jax.random.uniform(kv, w.shape, _jnp.float32, 0.5, 1.5)
    if N_MICROBATCH > 1:
        for name, axis in PER_EXAMPLE_BATCH_AXIS.items():
            out[name] = _to_microbatches(out[name], axis)
    return {'x': out['x'], 'norm1_w': out['norm1_w'], 'w_in': out['w_in'], 'sb_norm_w': out['sb_norm_w'], 'gdn_conv_w': out['gdn_conv_w'], 'gdn_A_log': out['gdn_A_log'], 'gdn_dt_bias': out['gdn_dt_bias'], 'gdn_norm_w': out['gdn_norm_w'], 'w_out': out['w_out'], 'final_norm_w': out['final_norm_w'], 'loss_target': out['loss_target'], 'm_norm1_w': out['m_norm1_w'], 'm_w_in': out['m_w_in'], 'm_sb_norm_w': out['m_sb_norm_w'], 'm_gdn_conv_w': out['m_gdn_conv_w'], 'm_gdn_A_log': out['m_gdn_A_log'], 'm_gdn_dt_bias': out['m_gdn_dt_bias'], 'm_gdn_norm_w': out['m_gdn_norm_w'], 'm_w_out': out['m_w_out'], 'm_final_norm_w': out['m_final_norm_w'], 'v_norm1_w': out['v_norm1_w'], 'v_w_in': out['v_w_in'], 'v_sb_norm_w': out['v_sb_norm_w'], 'v_gdn_conv_w': out['v_gdn_conv_w'], 'v_gdn_A_log': out['v_gdn_A_log'], 'v_gdn_dt_bias': out['v_gdn_dt_bias'], 'v_gdn_norm_w': out['v_gdn_norm_w'], 'v_w_out': out['v_w_out'], 'v_final_norm_w': out['v_final_norm_w']}


def _loss(weights, diff, rest, loss_target):
    with _jax.named_scope("forward"):
        args = {**rest, TWIN_DIFF_INPUT: diff, **{k: w.astype(_WEIGHT_DTYPES[k]) for k, w in weights.items()}}
        y = _forward(args)
    with _jax.named_scope("loss_head"):
        err = _jnp.square(y.astype(_jnp.float32) - loss_target)
        return 0.5 * _jnp.sum(_jnp.mean(err, axis=-1)) if err.ndim else 0.5 * err


def _adamw(w, g, m, v):
    m = ADAM_B1 * m + (1.0 - ADAM_B1) * g
    v = ADAM_B2 * v + (1.0 - ADAM_B2) * _jnp.square(g)
    m_hat = m / (1.0 - ADAM_B1 ** ADAM_STEP)
    v_hat = v / (1.0 - ADAM_B2 ** ADAM_STEP)
    delta = -ADAM_LR * (m_hat / (_jnp.sqrt(v_hat) + ADAM_EPS) + ADAM_WD * w)
    return delta, m, v


def reference(x, norm1_w, w_in, sb_norm_w, gdn_conv_w, gdn_A_log, gdn_dt_bias, gdn_norm_w, w_out, final_norm_w, loss_target, m_norm1_w, m_w_in, m_sb_norm_w, m_gdn_conv_w, m_gdn_A_log, m_gdn_dt_bias, m_gdn_norm_w, m_w_out, m_final_norm_w, v_norm1_w, v_w_in, v_sb_norm_w, v_gdn_conv_w, v_gdn_A_log, v_gdn_dt_bias, v_gdn_norm_w, v_w_out, v_final_norm_w):
    given = dict(x=x, norm1_w=norm1_w, w_in=w_in, sb_norm_w=sb_norm_w, gdn_conv_w=gdn_conv_w, gdn_A_log=gdn_A_log, gdn_dt_bias=gdn_dt_bias, gdn_norm_w=gdn_norm_w, w_out=w_out, final_norm_w=final_norm_w, loss_target=loss_target, m_norm1_w=m_norm1_w, m_w_in=m_w_in, m_sb_norm_w=m_sb_norm_w, m_gdn_conv_w=m_gdn_conv_w, m_gdn_A_log=m_gdn_A_log, m_gdn_dt_bias=m_gdn_dt_bias, m_gdn_norm_w=m_gdn_norm_w, m_w_out=m_w_out, m_final_norm_w=m_final_norm_w, v_norm1_w=v_norm1_w, v_w_in=v_w_in, v_sb_norm_w=v_sb_norm_w, v_gdn_conv_w=v_gdn_conv_w, v_gdn_A_log=v_gdn_A_log, v_gdn_dt_bias=v_gdn_dt_bias, v_gdn_norm_w=v_gdn_norm_w, v_w_out=v_w_out, v_final_norm_w=v_final_norm_w)
    weights = {n: given[n] for n in TWIN_WEIGHTS}
    shared = {n: given[n] for n in SHARED_INPUTS}
    per_example = {n: given[n] for n in ['x']}
    grad_fn = _jax.value_and_grad(_loss, argnums=(0, 1))

    def one_microbatch(ex, loss_target):
        ex = dict(ex)
        diff = ex.pop(TWIN_DIFF_INPUT)
        return grad_fn(weights, diff, {**shared, **ex}, loss_target)

    if N_MICROBATCH == 1:
        loss, (grad_w, grad_x) = one_microbatch(per_example, given["loss_target"])
    else:
        def body(carry, xs):
            loss_sum, grad_sum = carry
            l_k, (gw_k, gx_k) = one_microbatch(xs[0], xs[1])
            with _jax.named_scope("update"):
                return (loss_sum + l_k, _jax.tree.map(_jnp.add, grad_sum, gw_k)), gx_k

        init = (_jnp.zeros((), _jnp.float32), _jax.tree.map(_jnp.zeros_like, weights))
        (loss, grad_w), grad_x = _jax.lax.scan(body, init, (per_example, given["loss_target"]))
    with _jax.named_scope("update"):
        delta_w, new_m, new_v = {}, {}, {}
        for n in TWIN_WEIGHTS:
            delta_w[n], new_m[n], new_v[n] = _adamw(weights[n], grad_w[n], given["m_" + n], given["v_" + n])
    return (loss, grad_x, *[grad_w[n] for n in TWIN_WEIGHTS], *[delta_w[n] for n in TWIN_WEIGHTS],
            *[new_m[n] for n in TWIN_WEIGHTS], *[new_v[n] for n in TWIN_WEIGHTS])
```

```python
import functools

import jax
import jax.numpy as jnp
from jax import lax
from jax.experimental import pallas as pl
from jax.experimental.pallas import tpu as pltpu

F32 = jnp.float32
MXU_DTYPE = jnp.bfloat16
EXACT = lax.Precision.HIGHEST
EPS = 1e-6
N_DEV = 8
SB_HEAD_DIM = 64
GDN_HEAD_DIM = 128
GDN_HEADS = 4
CHUNK = 64
CONV_WIDTH = 4
LANES = 128
SB_BLOCK = 128
VMEM_LIMIT_BYTES = 56 * 1024 * 1024

ADAM_LR = 0.001
ADAM_B1 = 0.9
ADAM_B2 = 0.999
ADAM_EPS = 1e-08
ADAM_WD = 0.01
ADAM_STEP = 10

_NT = (((1,), (1,)), ((), ()))
_TN = (((0,), (0,)), ((), ()))


def _mm(a, b):
    return jnp.dot(a.astype(MXU_DTYPE), b.astype(MXU_DTYPE), preferred_element_type=F32)


def _mm_nt(a, b):
    return lax.dot_general(a.astype(MXU_DTYPE), b.astype(MXU_DTYPE), _NT, preferred_element_type=F32)


def _mm_tn(a, b):
    return lax.dot_general(a.astype(MXU_DTYPE), b.astype(MXU_DTYPE), _TN, preferred_element_type=F32)


def _mx(a, b):
    return jnp.dot(a, b, precision=EXACT, preferred_element_type=F32)


def _mx_nt(a, b):
    return lax.dot_general(a, b, _NT, precision=EXACT, preferred_element_type=F32)


def _mx_tn(a, b):
    return lax.dot_general(a, b, _TN, precision=EXACT, preferred_element_type=F32)


def _sigmoid(z):
    return 1.0 / (1.0 + jnp.exp(-z))


def _softplus(z):
    return jnp.maximum(z, 0.0) + jnp.log(1.0 + jnp.exp(-jnp.abs(z)))


def _params(*semantics):
    return pltpu.CompilerParams(dimension_semantics=semantics, vmem_limit_bytes=VMEM_LIMIT_BYTES)


def _inproj_call(x, norm_w, w_main, w_small, w_small_t, tm=256):
    t_len, d = x.shape
    n = w_main.shape[1]
    ns = w_small.shape[1]
    nst = w_small_t.shape[0]

    def body(x_ref, nw_ref, wm_ref, ws_ref, wst_ref, pm_ref, ps_ref, pst_ref, ht_ref, r_ref):
        xv = x_ref[...]
        r = lax.rsqrt(jnp.mean(xv * xv, axis=-1, keepdims=True) + EPS)
        h = xv * r * nw_ref[...]
        hb = h.astype(MXU_DTYPE)
        for n0 in range(0, n, 512):
            pm_ref[:, n0:n0 + 512] = jnp.dot(hb, wm_ref[:, n0:n0 + 512], preferred_element_type=F32)
        ps_ref[...] = jnp.dot(hb, ws_ref[...], preferred_element_type=F32)
        pst_ref[...] = lax.dot_general(wst_ref[...], hb, _NT, preferred_element_type=F32)
        ht_ref[...] = h.T.astype(MXU_DTYPE)
        r_ref[...] = r

    return pl.pallas_call(
        body, name="inproj",
        grid=(t_len // tm,),
        in_specs=[pl.BlockSpec((tm, d), lambda i: (i, 0)),
                  pl.BlockSpec((1, d), lambda i: (0, 0)),
                  pl.BlockSpec((d, n), lambda i: (0, 0)),
                  pl.BlockSpec((d, ns), lambda i: (0, 0)),
                  pl.BlockSpec((nst, d), lambda i: (0, 0))],
        out_specs=[pl.BlockSpec((tm, n), lambda i: (i, 0)),
                   pl.BlockSpec((tm, ns), lambda i: (i, 0)),
                   pl.BlockSpec((nst, tm), lambda i: (0, i)),
                   pl.BlockSpec((d, tm), lambda i: (0, i)),
                   pl.BlockSpec((tm, 1), lambda i: (i, 0))],
        out_shape=[jax.ShapeDtypeStruct((t_len, n), F32),
                   jax.ShapeDtypeStruct((t_len, ns), F32),
                   jax.ShapeDtypeStruct((nst, t_len), F32),
                   jax.ShapeDtypeStruct((d, t_len), MXU_DTYPE),
                   jax.ShapeDtypeStruct((t_len, 1), F32)],
        compiler_params=_params("arbitrary"),
    )(x, norm_w, w_main, w_small, w_small_t)


def _sb_scores(qm, k_ref, r0, s0, scale, row_i, col_i):
    kb = k_ref[pl.ds(s0, SB_BLOCK), :].astype(MXU_DTYPE)
    z = lax.dot_general(qm, kb, _NT, preferred_element_type=F32) * scale
    mask = (col_i + s0) < (row_i + r0)
    sp_all = _softplus(z)
    return z, mask, sp_all, jnp.where(mask, sp_all, 0.0)


def _sb_fwd_call(proj, t_len):
    nq = t_len // SB_BLOCK
    scale = float(SB_HEAD_DIM) ** -0.5
    n_pairs = 512 // LANES
    per_pair = LANES // SB_HEAD_DIM

    def body(q_ref, k_ref, v_ref, o_ref, st_ref):
        lane = lax.broadcasted_iota(jnp.int32, (1, LANES), 1)
        row_i = lax.broadcasted_iota(jnp.int32, (SB_BLOCK, SB_BLOCK), 0)
        col_i = lax.broadcasted_iota(jnp.int32, (SB_BLOCK, SB_BLOCK), 1)
        ge = (row_i >= col_i).astype(F32)
        for hh in range(per_pair):
            hm = ((lane // SB_HEAD_DIM) == hh).astype(F32)

            def q_loop(qi, carry, hm=hm, hh=hh):
                r0 = pl.multiple_of(qi * SB_BLOCK, SB_BLOCK)
                qm = (q_ref[pl.ds(r0, SB_BLOCK), :] * hm).astype(MXU_DTYPE)

                def k_loop(step, kc):
                    acc, c = kc
                    s0 = pl.multiple_of((qi - step) * SB_BLOCK, SB_BLOCK)
                    z, mask, _, sp = _sb_scores(qm, k_ref, r0, s0, scale, row_i, col_i)
                    a = jnp.where(mask, jnp.exp(z - (_mx(sp, ge) + c)), 0.0)
                    vm = (v_ref[pl.ds(s0, SB_BLOCK), :] * hm).astype(MXU_DTYPE)
                    acc = acc + jnp.dot(a.astype(MXU_DTYPE), vm, preferred_element_type=F32)
                    return acc, c + jnp.sum(sp, axis=-1, keepdims=True)

                acc, c = lax.fori_loop(0, qi + 1, k_loop,
                                       (jnp.zeros((SB_BLOCK, LANES), F32), jnp.zeros((SB_BLOCK, 1), F32)))
                if hh == 0:
                    o_ref[pl.ds(r0, SB_BLOCK), :] = acc
                else:
                    o_ref[pl.ds(r0, SB_BLOCK), :] += acc
                st_ref[hh, pl.ds(r0, SB_BLOCK), :] = c
                return carry

            lax.fori_loop(0, nq, q_loop, 0)

    return pl.pallas_call(
        body, name="sb_fwd",
        grid=(n_pairs,),
        in_specs=[pl.BlockSpec((t_len, LANES), lambda p: (0, p)),
                  pl.BlockSpec((t_len, LANES), lambda p: (0, n_pairs + p)),
                  pl.BlockSpec((t_len, LANES), lambda p: (0, 2 * n_pairs + p))],
        out_specs=[pl.BlockSpec((t_len, LANES), lambda p: (0, p)),
                   pl.BlockSpec((per_pair, t_len, 1), lambda p: (p, 0, 0))],
        out_shape=[jax.ShapeDtypeStruct((t_len, 512), F32),
                   jax.ShapeDtypeStruct((n_pairs * per_pair, t_len, 1), F32)],
        compiler_params=_params("arbitrary"),
    )(proj, proj, proj)


def _sb_bwd_call(proj, sp_total, d_o, t_len):
    nq = t_len // SB_BLOCK
    scale = float(SB_HEAD_DIM) ** -0.5
    n_pairs = 512 // LANES
    per_pair = LANES // SB_HEAD_DIM

    def body(q_ref, k_ref, v_ref, st_ref, do_ref, d_ref):
        lane = lax.broadcasted_iota(jnp.int32, (1, LANES), 1)
        row_i = lax.broadcasted_iota(jnp.int32, (SB_BLOCK, SB_BLOCK), 0)
        col_i = lax.broadcasted_iota(jnp.int32, (SB_BLOCK, SB_BLOCK), 1)
        lt = (row_i < col_i).astype(F32)
        le = (row_i <= col_i).astype(F32)
        d_ref[1] = jnp.zeros((t_len, LANES), F32)
        d_ref[2] = jnp.zeros((t_len, LANES), F32)
        for hh in range(per_pair):
            hm = ((lane // SB_HEAD_DIM) == hh).astype(F32)

            def q_loop(qi, carry, hm=hm, hh=hh):
                r0 = pl.multiple_of(qi * SB_BLOCK, SB_BLOCK)
                rows = pl.ds(r0, SB_BLOCK)
                qm = (q_ref[rows, :] * hm).astype(MXU_DTYPE)
                dom = (do_ref[rows, :] * hm).astype(MXU_DTYPE)
                total = st_ref[hh, rows, :]

                def k_loop(kj, kc):
                    dq, cl, gl = kc
                    s0 = pl.multiple_of(kj * SB_BLOCK, SB_BLOCK)
                    cols = pl.ds(s0, SB_BLOCK)
                    z, mask, sp_all, sp = _sb_scores(qm, k_ref, r0, s0, scale, row_i, col_i)
                    a = jnp.where(mask, jnp.exp(z - (total - cl - _mx(sp, lt))), 0.0)
                    vm = (v_ref[cols, :] * hm).astype(MXU_DTYPE)
                    km = (k_ref[cols, :] * hm).astype(MXU_DTYPE)
                    g = lax.dot_general(dom, vm, _NT, preferred_element_type=F32) * a
                    pm = gl + _mx(g, le)
                    sig = jnp.exp(z - sp_all)
                    dz = (jnp.where(mask, g - sig * pm, 0.0) * scale).astype(MXU_DTYPE)
                    dq = dq + jnp.dot(dz, km, preferred_element_type=F32)
                    d_ref[1, cols, :] += lax.dot_general(dz, qm, _TN, preferred_element_type=F32)
                    d_ref[2, cols, :] += lax.dot_general(a.astype(MXU_DTYPE), dom, _TN, preferred_element_type=F32)
                    return (dq, cl + jnp.sum(sp, axis=-1, keepdims=True), gl + jnp.sum(g, axis=-1, keepdims=True))

                zero_col = jnp.zeros((SB_BLOCK, 1), F32)
                dq, _, _ = lax.fori_loop(0, qi + 1, k_loop,
                                         (jnp.zeros((SB_BLOCK, LANES), F32), zero_col, zero_col))
                if hh == 0:
                    d_ref[0, rows, :] = dq
                else:
                    d_ref[0, rows, :] += dq
                return carry

            lax.fori_loop(0, nq, q_loop, 0)

    col = lambda off: pl.BlockSpec((t_len, LANES), lambda p: (0, off + p))
    return pl.pallas_call(
        body, name="sb_bwd",
        grid=(n_pairs,),
        in_specs=[col(0), col(n_pairs), col(2 * n_pairs),
                  pl.BlockSpec((per_pair, t_len, 1), lambda p: (p, 0, 0)), col(0)],
        out_specs=pl.BlockSpec((3, t_len, LANES), lambda p: (0, 0, p)),
        out_shape=jax.ShapeDtypeStruct((3, t_len, 512), F32),
        compiler_params=_params("arbitrary"),
    )(proj, proj, proj, sp_total, d_o)


def _conv_taps(xin, rows, t_len):
    taps = []
    for i in range(CONV_WIDTH):
        shift = CONV_WIDTH - 1 - i
        if shift == 0:
            taps.append(xin)
        else:
            taps.append(jnp.where(rows >= shift, pltpu.roll(xin, shift, axis=0), 0.0))
    return taps


def _gdn_prep_body_common(x_ref, w_ref, t_len):
    j = pl.program_id(0)
    xin = x_ref[...]
    rows = lax.broadcasted_iota(jnp.int32, (t_len, LANES), 0)
    taps = _conv_taps(xin, rows, t_len)
    pre = taps[0] * w_ref[0:1, :]
    for i in range(1, CONV_WIDTH):
        pre = pre + taps[i] * w_ref[i:i + 1, :]
    sg = _sigmoid(pre)
    act = pre * sg
    is_qk = j < 2 * GDN_HEADS
    nrm = jnp.where(is_qk, lax.rsqrt(jnp.sum(act * act, axis=-1, keepdims=True) + EPS), 1.0)
    sc = jnp.where(j < GDN_HEADS, float(GDN_HEAD_DIM) ** -0.5, 1.0)
    return j, rows, taps, pre, sg, act, is_qk, nrm, sc


def _gdn_prep_call(proj, conv_w, t_len):
    first = 2048 // LANES

    def body(x_ref, w_ref, out_ref):
        _, _, _, _, _, act, _, nrm, sc = _gdn_prep_body_common(x_ref, w_ref, t_len)
        out_ref[...] = act * nrm * sc

    return pl.pallas_call(
        body, name="gdn_prep",
        grid=(3 * GDN_HEADS,),
        in_specs=[pl.BlockSpec((t_len, LANES), lambda j: (0, first + j)),
                  pl.BlockSpec((CONV_WIDTH, LANES), lambda j: (0, j))],
        out_specs=pl.BlockSpec((t_len, LANES), lambda j: (0, j)),
        out_shape=jax.ShapeDtypeStruct((t_len, 3 * 512), F32),
        compiler_params=_params("arbitrary"),
    )(proj, conv_w)


def _gdn_prep_bwd_call(proj, conv_w, d_act3, t_len):
    first = 2048 // LANES

    def body(x_ref, w_ref, d_ref, dx_ref, dw_ref):
        _, rows, taps, pre, sg, act, is_qk, nrm, sc = _gdn_prep_body_common(x_ref, w_ref, t_len)
        d_out = d_ref[0]
        dn = d_out * sc
        d_norm = nrm * dn - act * (nrm * nrm * nrm) * jnp.sum(dn * act, axis=-1, keepdims=True)
        d_act = jnp.where(is_qk, d_norm, d_out)
        d_pre = d_act * sg * (1.0 + pre * (1.0 - sg))
        dx = d_pre * w_ref[CONV_WIDTH - 1:CONV_WIDTH, :]
        dw_ref[CONV_WIDTH - 1:CONV_WIDTH, :] = jnp.sum(d_pre * taps[CONV_WIDTH - 1], axis=0, keepdims=True)
        for i in range(CONV_WIDTH - 1):
            shift = CONV_WIDTH - 1 - i
            up = jnp.where(rows < t_len - shift, pltpu.roll(d_pre, t_len - shift, axis=0), 0.0)
            dx = dx + up * w_ref[i:i + 1, :]
            dw_ref[i:i + 1, :] = jnp.sum(d_pre * taps[i], axis=0, keepdims=True)
        dx_ref[0] = dx

    return pl.pallas_call(
        body, name="gdn_prep_bwd",
        grid=(3 * GDN_HEADS,),
        in_specs=[pl.BlockSpec((t_len, LANES), lambda j: (0, first + j)),
                  pl.BlockSpec((CONV_WIDTH, LANES), lambda j: (0, j)),
                  pl.BlockSpec((1, t_len, LANES), lambda j: (j // GDN_HEADS, 0, j % GDN_HEADS))],
        out_specs=[pl.BlockSpec((1, t_len, LANES), lambda j: (j // GDN_HEADS, 0, j % GDN_HEADS)),
                   pl.BlockSpec((CONV_WIDTH, LANES), lambda j: (0, j))],
        out_shape=[jax.ShapeDtypeStruct((3, t_len, 512), F32),
                   jax.ShapeDtypeStruct((CONV_WIDTH, 3 * 512), F32)],
        compiler_params=_params("arbitrary"),
    )(proj, conv_w, d_act3)


def _chunk_cumsum_matrix():
    r = lax.broadcasted_iota(jnp.int32, (LANES, LANES), 0)
    c = lax.broadcasted_iota(jnp.int32, (LANES, LANES), 1)
    return ((r <= c) & ((r // CHUNK) == (c // CHUNK))).astype(F32)


def _gdn_gates_call(ps, pst, alog_l, dtb_l, alog_c, dtb_c, t_len):
    def body(ps_ref, pst_ref, al_ref, dl_ref, ac_ref, dc_ref, beta_ref, gcol_ref, grow_ref):
        upper = _chunk_cumsum_matrix()
        lower = upper.T
        psv = ps_ref[...]
        beta_ref[...] = _sigmoid(psv)
        g_l = -jnp.exp(al_ref[...]) * _softplus(psv + dl_ref[...])
        g_r = -jnp.exp(ac_ref[...]) * _softplus(pst_ref[...] + dc_ref[...])
        for w in range(t_len // LANES):
            sl = slice(w * LANES, (w + 1) * LANES)
            gcol_ref[sl, :] = _mx(lower, g_l[sl, :])
            grow_ref[:, sl] = _mx(g_r[:, sl], upper)

    vm = pl.BlockSpec(memory_space=pltpu.VMEM)
    return pl.pallas_call(
        body, name="gdn_gates",
        in_specs=[vm] * 6, out_specs=[vm] * 3,
        out_shape=[jax.ShapeDtypeStruct((t_len, LANES), F32),
                   jax.ShapeDtypeStruct((t_len, LANES), F32),
                   jax.ShapeDtypeStruct((8, t_len), F32)],
        compiler_params=pltpu.CompilerParams(vmem_limit_bytes=VMEM_LIMIT_BYTES),
    )(ps, pst, alog_l, dtb_l, alog_c, dtb_c)


def _gdn_gates_bwd_call(ps, alog_l, dtb_l, d_l, t_len):
    def body(ps_ref, al_ref, dl_ref, d_ref, dps_ref, gal_ref, gdt_ref):
        lane = lax.broadcasted_iota(jnp.int32, (1, LANES), 1)
        psv = ps_ref[...]
        dv = d_ref[...]
        beta = _sigmoid(psv)
        ea = jnp.exp(al_ref[...])
        arg = psv + dl_ref[...]
        g = -ea * _softplus(arg)
        d_a = dv * (-ea) * _sigmoid(arg)
        is_a = (lane >= GDN_HEADS) & (lane < 2 * GDN_HEADS)
        dps_ref[...] = jnp.where(lane < GDN_HEADS, dv * beta * (1.0 - beta), jnp.where(is_a, d_a, 0.0))
        gdt_ref[...] = jnp.where(is_a, jnp.sum(d_a, axis=0, keepdims=True), 0.0)
        gal_ref[...] = jnp.where(is_a, jnp.sum(dv * g, axis=0, keepdims=True), 0.0)

    vm = pl.BlockSpec(memory_space=pltpu.VMEM)
    return pl.pallas_call(
        body, name="gdn_gates_bwd",
        in_specs=[vm] * 4, out_specs=[vm] * 3,
        out_shape=[jax.ShapeDtypeStruct((t_len, LANES), F32),
                   jax.ShapeDtypeStruct((1, LANES), F32),
                   jax.ShapeDtypeStruct((1, LANES), F32)],
        compiler_params=pltpu.CompilerParams(vmem_limit_bytes=VMEM_LIMIT_BYTES),
    )(ps, alog_l, dtb_l, d_l)


def _chunk_terms(q_ref, k_ref, v_ref, b_ref, gc_ref, gr_ref, i, incl, strict):
    r0 = pl.multiple_of(i * CHUNK, CHUNK)
    rows = pl.ds(r0, CHUNK)
    q, k, v = q_ref[rows, :], k_ref[rows, :], v_ref[rows, :]
    gc, gr, b = gc_ref[0, i], gr_ref[0, i], b_ref[0, i]
    dm = jnp.where(incl, jnp.exp(jnp.where(incl, gc - gr, 0.0)), 0.0)
    kb = k * b
    vb = v * b
    e = jnp.exp(gc)
    a = jnp.where(strict, _mx_nt(kb, k) * dm, 0.0)
    p = jnp.where(incl, _mx_nt(q, k) * dm, 0.0)
    gl = gc[CHUNK - 1:CHUNK, :]
    eg = jnp.exp(gl - gc)
    return rows, q, k, v, b, gc, dm, kb, vb, e, a, p, gl, eg


def _gdn_fwd_call(gact, beta_c, gam_c, gam_r, t_len):
    n_chunks = t_len // CHUNK
    dk = GDN_HEAD_DIM

    def body(q_ref, k_ref, v_ref, b_ref, gc_ref, gr_ref, o_ref, s_ref, t_ref):
        row = lax.broadcasted_iota(jnp.int32, (CHUNK, CHUNK), 0)
        col = lax.broadcasted_iota(jnp.int32, (CHUNK, CHUNK), 1)
        incl, strict = row >= col, row > col
        eye = (row == col).astype(F32)

        def chunk(i, s):
            rows, q, k, v, b, gc, dm, kb, vb, e, a, p, gl, eg = _chunk_terms(
                q_ref, k_ref, v_ref, b_ref, gc_ref, gr_ref, i, incl, strict)
            x = -a
            tm = eye + x
            xp = x
            for _ in range(5):
                xp = _mx(xp, xp)
                tm = tm + _mx(tm, xp)
            u = _mx(tm, vb)
            w = _mx(tm, kb * e)
            vn = u - _mx(w, s)
            o_ref[rows, :] = _mx(q * e, s) + _mx(p, vn)
            s_ref[0, i] = s
            t_ref[0, i] = tm
            return s * jnp.exp(gl) + _mx_tn(k * eg, vn)

        lax.fori_loop(0, n_chunks, chunk, jnp.zeros((dk, dk), F32))

    col_spec = lambda off: pl.BlockSpec((t_len, LANES), lambda h: (0, off + h))
    gate_c = pl.BlockSpec((1, n_chunks, CHUNK, 1), lambda h: (h, 0, 0, 0))
    gate_r = pl.BlockSpec((1, n_chunks, 1, CHUNK), lambda h: (h, 0, 0, 0))
    return pl.pallas_call(
        body, name="gdn_fwd",
        grid=(GDN_HEADS,),
        in_specs=[col_spec(0), col_spec(GDN_HEADS), col_spec(2 * GDN_HEADS), gate_c, gate_c, gate_r],
        out_specs=[pl.BlockSpec((t_len, LANES), lambda h: (0, h)),
                   pl.BlockSpec((1, n_chunks, dk, dk), lambda h: (h, 0, 0, 0)),
                   pl.BlockSpec((1, n_chunks, CHUNK, CHUNK), lambda h: (h, 0, 0, 0))],
        out_shape=[jax.ShapeDtypeStruct((t_len, 512), F32),
                   jax.ShapeDtypeStruct((GDN_HEADS, n_chunks, dk, dk), F32),
                   jax.ShapeDtypeStruct((GDN_HEADS, n_chunks, CHUNK, CHUNK), F32)],
        compiler_params=_params("arbitrary"),
    )(gact, gact, gact, beta_c, gam_c, gam_r)


def _gdn_bwd_call(gact, beta_c, gam_c, gam_r, s_all, t_all, d_o, t_len):
    n_chunks = t_len // CHUNK
    dk = GDN_HEAD_DIM

    def body(q_ref, k_ref, v_ref, b_ref, gc_ref, gr_ref, s_ref, t_ref, do_ref, d_ref, db_ref, dg_ref):
        row = lax.broadcasted_iota(jnp.int32, (CHUNK, CHUNK), 0)
        col = lax.broadcasted_iota(jnp.int32, (CHUNK, CHUNK), 1)
        incl, strict = row >= col, row > col
        upper = (row <= col).astype(F32)
        ones = jnp.ones((CHUNK, LANES), F32)
        last_row = lax.broadcasted_iota(jnp.int32, (CHUNK, 1), 0) == CHUNK - 1
        rsum = lambda m: jnp.sum(m, axis=-1, keepdims=True)

        def chunk(step, ds):
            i = n_chunks - 1 - step
            rows, q, k, v, b, gc, dm, kb, vb, e, a, p, gl, eg = _chunk_terms(
                q_ref, k_ref, v_ref, b_ref, gc_ref, gr_ref, i, incl, strict)
            s = s_ref[0, i]
            tm = t_ref[0, i]
            d_out = do_ref[rows, :]
            el = jnp.exp(gl)
            kbe = kb * e
            u = _mx(tm, vb)
            w = _mx(tm, kbe)
            vn = u - _mx(w, s)
            qe = q * e
            kd = k * eg

            d_vn = _mx_tn(p, d_out) + _mx(kd, ds)
            d_qe = _mx_nt(d_out, s)
            d_p = jnp.where(incl, _mx_nt(d_out, vn), 0.0)
            ds_new = el * ds + _mx_tn(qe, d_out) - _mx_tn(w, d_vn)
            d_kd = _mx_nt(vn, ds)
            d_w = -_mx_nt(d_vn, s)
            d_vb = _mx_tn(tm, d_vn)
            d_kbe = _mx_tn(tm, d_w)
            d_a = -jnp.where(strict, _mx_nt(d_vb, u) + _mx_nt(d_kbe, w), 0.0)
            m = d_a * dm
            n = d_p * dm
            d_kb = _mx(m, k) + d_kbe * e
            d_q = _mx(n, k) + d_qe * e
            d_k = _mx_tn(m, kb) + _mx_tn(n, q) + d_kd * eg + b * d_kb
            r = d_a * a + d_p * p
            kd_term = rsum(d_kd * kd)
            d_gl = jnp.sum(ds * s) * el + jnp.sum(kd_term)
            d_gam = (rsum(r) - _mx_tn(r, ones)[:, 0:1] + rsum(d_qe * qe) + rsum(d_kbe * kbe) - kd_term
                     + jnp.where(last_row, d_gl, 0.0))
            d_ref[0, rows, :] = d_q
            d_ref[1, rows, :] = d_k
            d_ref[2, rows, :] = b * d_vb
            db_ref[0, i] = rsum(d_kb * k) + rsum(d_vb * v)
            dg_ref[0, i] = _mx(upper, d_gam * ones)[:, 0:1]
            return ds_new

        lax.fori_loop(0, n_chunks, chunk, jnp.zeros((dk, dk), F32))

    col_spec = lambda off: pl.BlockSpec((t_len, LANES), lambda h: (0, off + h))
    gate_c = pl.BlockSpec((1, n_chunks, CHUNK, 1), lambda h: (h, 0, 0, 0))
    gate_r = pl.BlockSpec((1, n_chunks, 1, CHUNK), lambda h: (h, 0, 0, 0))
    return pl.pallas_call(
        body, name="gdn_bwd",
        grid=(GDN_HEADS,),
        in_specs=[col_spec(0), col_spec(GDN_HEADS), col_spec(2 * GDN_HEADS), gate_c, gate_c, gate_r,
                  pl.BlockSpec((1, n_chunks, dk, dk), lambda h: (h, 0, 0, 0)),
                  pl.BlockSpec((1, n_chunks, CHUNK, CHUNK), lambda h: (h, 0, 0, 0)),
                  pl.BlockSpec((t_len, LANES), lambda h: (0, h))],
        out_specs=[pl.BlockSpec((3, t_len, LANES), lambda h: (0, 0, h)), gate_c, gate_c],
        out_shape=[jax.ShapeDtypeStruct((3, t_len, 512), F32),
                   jax.ShapeDtypeStruct((GDN_HEADS, n_chunks, CHUNK, 1), F32),
                   jax.ShapeDtypeStruct((GDN_HEADS, n_chunks, CHUNK, 1), F32)],
        compiler_params=_params("arbitrary"),
    )(gact, gact, gact, beta_c, gam_c, gam_r, s_all, t_all, d_o)


def _group_matrix(width, group):
    r = lax.broadcasted_iota(jnp.int32, (width, width), 0)
    c = lax.broadcasted_iota(jnp.int32, (width, width), 1)
    return ((r // group) == (c // group)).astype(F32)


def _post_call(o_sb, o_gd, proj, x, target, w_out, sbw, gdw, fw, tm=256):
    t_len, d = x.shape
    half = 512
    zsb_blk = 1536 // half
    zgd_blk = 3584 // half

    def body(osb_ref, ogd_ref, zsb_ref, zgd_ref, x_ref, tg_ref, wo_ref, sbw_ref, gdw_ref, fw_ref,
             dx2_ref, dosb_ref, dzsb_ref, dogd_ref, dzgd_ref, loss_ref, gfw_ref, gsb_ref, ggd_ref, gwo_ref):
        step = pl.program_id(0)

        @pl.when(step == 0)
        def _():
            loss_ref[...] = jnp.zeros_like(loss_ref)
            gfw_ref[...] = jnp.zeros_like(gfw_ref)
            gsb_ref[...] = jnp.zeros_like(gsb_ref)
            ggd_ref[...] = jnp.zeros_like(ggd_ref)
            gwo_ref[...] = jnp.zeros_like(gwo_ref)

        def head_forward(o, z, w, gmat, inv):
            r = lax.rsqrt(_mx(o * o, gmat) * inv + EPS)
            nrm = o * r * w
            sg = _sigmoid(z)
            return r, nrm, sg, nrm * (z * sg)

        def head_backward(d_m, o, z, w, gmat, inv, r, nrm, sg):
            d_n = d_m * (z * sg)
            d_z = d_m * nrm * (sg * (1.0 + z * (1.0 - sg)))
            dnw = d_n * w
            d_o = r * dnw - o * (r * r * r) * (_mx(dnw * o, gmat) * inv)
            return d_o, d_z, jnp.sum(d_n * o * r, axis=0, keepdims=True)

        g_sb = _group_matrix(half, SB_HEAD_DIM)
        g_gd = _group_matrix(half, GDN_HEAD_DIM)
        osb, ogd, zsb, zgd = osb_ref[...], ogd_ref[...], zsb_ref[...], zgd_ref[...]
        sbw_v, gdw_v = sbw_ref[...], gdw_ref[...]
        r_sb, n_sb, sg_sb, m_sb = head_forward(osb, zsb, sbw_v, g_sb, 1.0 / SB_HEAD_DIM)
        r_gd, n_gd, sg_gd, m_gd = head_forward(ogd, zgd, gdw_v, g_gd, 1.0 / GDN_HEAD_DIM)
        mixed = jnp.concatenate([m_sb, m_gd], axis=1).astype(MXU_DTYPE)
        wo = wo_ref[...]
        x2 = x_ref[...] + jnp.dot(mixed, wo, preferred_element_type=F32)
        r2 = lax.rsqrt(jnp.mean(x2 * x2, axis=-1, keepdims=True) + EPS)
        fw_v = fw_ref[...]
        err = x2 * r2 * fw_v - tg_ref[...]
        loss_ref[...] += 0.5 * jnp.sum(jnp.sum(err * err, axis=-1, keepdims=True) * (1.0 / d))
        dy = err * (1.0 / d)
        gg = dy * fw_v
        dx2 = r2 * gg - x2 * ((r2 * r2 * r2) * jnp.mean(gg * x2, axis=-1, keepdims=True))
        gfw_ref[...] += jnp.sum(dy * x2 * r2, axis=0, keepdims=True)
        dx2_ref[...] = dx2
        dx2b = dx2.astype(MXU_DTYPE)
        d_mixed = lax.dot_general(dx2b, wo, _NT, preferred_element_type=F32)
        gwo_ref[...] += lax.dot_general(mixed, dx2b, _TN, preferred_element_type=F32)
        d_osb, d_zsb, gsb = head_backward(d_mixed[:, :half], osb, zsb, sbw_v, g_sb, 1.0 / SB_HEAD_DIM, r_sb, n_sb, sg_sb)
        d_ogd, d_zgd, ggd = head_backward(d_mixed[:, half:], ogd, zgd, gdw_v, g_gd, 1.0 / GDN_HEAD_DIM, r_gd, n_gd, sg_gd)
        dosb_ref[...] = d_osb
        dzsb_ref[...] = d_zsb
        dogd_ref[...] = d_ogd
        dzgd_ref[...] = d_zgd
        gsb_ref[...] += gsb
        ggd_ref[...] += ggd

    row_blk = lambda w: pl.BlockSpec((tm, w), lambda i: (i, 0))
    fixed = lambda r, w: pl.BlockSpec((r, w), lambda i: (0, 0))
    return pl.pallas_call(
        body, name="post",
        grid=(t_len // tm,),
        in_specs=[row_blk(half), row_blk(half),
                  pl.BlockSpec((tm, half), lambda i: (i, zsb_blk)),
                  pl.BlockSpec((tm, half), lambda i: (i, zgd_blk)),
                  row_blk(d), row_blk(d), fixed(d, d), fixed(1, half), fixed(1, half), fixed(1, d)],
        out_specs=[row_blk(d), row_blk(half), row_blk(half), row_blk(half), row_blk(half),
                   fixed(1, LANES), fixed(1, d), fixed(1, half), fixed(1, half), fixed(d, d)],
        out_shape=[jax.ShapeDtypeStruct((t_len, d), F32)] + [jax.ShapeDtypeStruct((t_len, half), F32)] * 4
                  + [jax.ShapeDtypeStruct((1, LANES), F32), jax.ShapeDtypeStruct((1, d), F32),
                     jax.ShapeDtypeStruct((1, half), F32), jax.ShapeDtypeStruct((1, half), F32),
                     jax.ShapeDtypeStruct((d, d), F32)],
        compiler_params=_params("arbitrary"),
    )(o_sb, o_gd, proj, proj, x, target, w_out, sbw, gdw, fw)


def _gw_in_call(h_t, dproj8, tm=512):
    d, t_len = h_t.shape
    n_piece, _, pw = dproj8.shape

    def body(ht_ref, dp_ref, gw_ref):
        @pl.when(pl.program_id(1) == 0)
        def _():
            gw_ref[...] = jnp.zeros_like(gw_ref)

        gw_ref[...] += jnp.dot(ht_ref[...], dp_ref[0].astype(MXU_DTYPE), preferred_element_type=F32)

    return pl.pallas_call(
        body, name="gw_in",
        grid=(n_piece, t_len // tm),
        in_specs=[pl.BlockSpec((d, tm), lambda p, t: (0, t)),
                  pl.BlockSpec((1, tm, pw), lambda p, t: (p, t, 0))],
        out_specs=pl.BlockSpec((d, pw), lambda p, t: (0, p)),
        out_shape=jax.ShapeDtypeStruct((d, n_piece * pw), F32),
        compiler_params=_params("arbitrary", "arbitrary"),
    )(h_t, dproj8)


def _gw_small_call(h_t, dsmall, tm=512):
    d, t_len = h_t.shape
    ns = dsmall.shape[1]

    def body(ht_ref, dp_ref, gw_ref):
        @pl.when(pl.program_id(0) == 0)
        def _():
            gw_ref[...] = jnp.zeros_like(gw_ref)

        gw_ref[...] += jnp.dot(ht_ref[...], dp_ref[...].astype(MXU_DTYPE), preferred_element_type=F32)

    return pl.pallas_call(
        body, name="gw_small",
        grid=(t_len // tm,),
        in_specs=[pl.BlockSpec((d, tm), lambda t: (0, t)),
                  pl.BlockSpec((tm, ns), lambda t: (t, 0))],
        out_specs=pl.BlockSpec((d, ns), lambda t: (0, 0)),
        out_shape=jax.ShapeDtypeStruct((d, ns), F32),
        compiler_params=_params("arbitrary"),
    )(h_t, dsmall)


def _dx_call(dproj8, dsmall, w_main, w_small, x, r, dx2, norm_w, tm=256):
    t_len, d = x.shape
    n_piece, _, pw = dproj8.shape
    ns = dsmall.shape[1]

    def body(dp_ref, ds_ref, wm_ref, ws_ref, x_ref, r_ref, dx2_ref, nw_ref, gx_ref, gnw_ref):
        @pl.when(pl.program_id(0) == 0)
        def _():
            gnw_ref[...] = jnp.zeros_like(gnw_ref)

        dh = lax.dot_general(ds_ref[...].astype(MXU_DTYPE), ws_ref[...], _NT, preferred_element_type=F32)
        for p in range(n_piece):
            dh = dh + lax.dot_general(dp_ref[p].astype(MXU_DTYPE), wm_ref[:, p * pw:(p + 1) * pw], _NT,
                                      preferred_element_type=F32)
        xv, rv = x_ref[...], r_ref[...]
        dn = dh * nw_ref[...]
        gx_ref[...] = dx2_ref[...] + rv * dn - xv * ((rv * rv * rv) * jnp.mean(dn * xv, axis=-1, keepdims=True))
        gnw_ref[...] += jnp.sum(dh * xv * rv, axis=0, keepdims=True)

    return pl.pallas_call(
        body, name="dx",
        grid=(t_len // tm,),
        in_specs=[pl.BlockSpec((n_piece, tm, pw), lambda i: (0, i, 0)),
                  pl.BlockSpec((tm, ns), lambda i: (i, 0)),
                  pl.BlockSpec((d, n_piece * pw), lambda i: (0, 0)),
                  pl.BlockSpec((d, ns), lambda i: (0, 0)),
                  pl.BlockSpec((tm, d), lambda i: (i, 0)),
                  pl.BlockSpec((tm, 1), lambda i: (i, 0)),
                  pl.BlockSpec((tm, d), lambda i: (i, 0)),
                  pl.BlockSpec((1, d), lambda i: (0, 0))],
        out_specs=[pl.BlockSpec((tm, d), lambda i: (i, 0)),
                   pl.BlockSpec((1, d), lambda i: (0, 0))],
        out_shape=[jax.ShapeDtypeStruct((t_len, d), F32), jax.ShapeDtypeStruct((1, d), F32)],
        compiler_params=_params("arbitrary"),
    )(dproj8, dsmall, w_main, w_small, x, r, dx2, norm_w)


def _exchange_call(name, srcs, per_peer):
    n = len(srcs)
    out_shapes = [jax.ShapeDtypeStruct(s.shape if pp else (N_DEV,) + s.shape, s.dtype) for s, pp in zip(srcs, per_peer)]

    def body(*refs):
        src_refs, out_refs = refs[:n], refs[n:2 * n]
        send_sems, recv_sems, local_sems = refs[2 * n:]
        x, y, c = lax.axis_index("x"), lax.axis_index("y"), lax.axis_index("c")
        me = 4 * x + 2 * y + c
        copies = []
        for a in range(n):
            mine = src_refs[a].at[me] if per_peer[a] else src_refs[a]
            local = pltpu.make_async_copy(mine, out_refs[a].at[me], local_sems.at[a])
            local.start()
            copies.append(local)
        remote = []
        for k in range(1, N_DEV):
            kx, ky, kc = (k >> 2) & 1, (k >> 1) & 1, k & 1
            px = 1 - x if kx else x
            py = 1 - y if ky else y
            pc = 1 - c if kc else c
            peer = 4 * px + 2 * py + pc
            for a in range(n):
                sem = a * (N_DEV - 1) + (k - 1)
                src = src_refs[a].at[peer] if per_peer[a] else src_refs[a]
                cp = pltpu.make_async_remote_copy(
                    src_ref=src, dst_ref=out_refs[a].at[me],
                    send_sem=send_sems.at[sem], recv_sem=recv_sems.at[sem],
                    device_id=(px, py, pc), device_id_type=pl.DeviceIdType.MESH)
                cp.start()
                remote.append(cp)
        for cp in remote:
            cp.wait_send()
        for cp in remote:
            cp.wait_recv()
        for cp in copies:
            cp.wait()

    hbm = pl.BlockSpec(memory_space=pl.ANY)
    return pl.pallas_call(
        body, name=name,
        in_specs=[hbm] * n, out_specs=[hbm] * n, out_shape=out_shapes,
        scratch_shapes=[pltpu.SemaphoreType.DMA((n * (N_DEV - 1),)),
                        pltpu.SemaphoreType.DMA((n * (N_DEV - 1),)),
                        pltpu.SemaphoreType.DMA((n,))],
    )(*srcs)


def _adam_call(name, parts, w, m, v, tr):
    rows, cols = w.shape

    def body(p_ref, w_ref, m_ref, v_ref, g_ref, d_ref, nm_ref, nv_ref):
        g = p_ref[0]
        for s in range(1, N_DEV):
            g = g + p_ref[s]
        m_new = ADAM_B1 * m_ref[...] + (1.0 - ADAM_B1) * g
        v_new = ADAM_B2 * v_ref[...] + (1.0 - ADAM_B2) * (g * g)
        m_hat = m_new / (1.0 - ADAM_B1 ** ADAM_STEP)
        v_hat = v_new / (1.0 - ADAM_B2 ** ADAM_STEP)
        g_ref[...] = g
        d_ref[...] = -ADAM_LR * (m_hat / (jnp.sqrt(v_hat) + ADAM_EPS) + ADAM_WD * w_ref[...])
        nm_ref[...] = m_new
        nv_ref[...] = v_new

    blk = pl.BlockSpec((tr, cols), lambda i: (i, 0))
    return pl.pallas_call(
        body, name=name,
        grid=(rows // tr,),
        in_specs=[pl.BlockSpec((N_DEV, tr, cols), lambda i: (0, i, 0)), blk, blk, blk],
        out_specs=[blk] * 4,
        out_shape=[jax.ShapeDtypeStruct((rows, cols), F32)] * 4,
        compiler_params=_params("arbitrary"),
    )(parts, w, m, v)


_SMALL_ROWS = ("norm1_w", "final_norm_w", "sb_norm_w", "gdn_norm_w", "gdn_A_log", "gdn_dt_bias", "loss")


def _pack_small(vals, width):
    rows = [jnp.pad(a.reshape(1, -1).astype(F32), ((0, 0), (0, width - a.size))) for a in vals]
    rows += [jnp.zeros((1, width), F32)] * (8 - len(rows))
    return jnp.concatenate(rows, axis=0)


def _device_step(x2d, tgt, w_full, w_out_f32, conv_full, norm1_w, sb_norm_w, gdn_A_log, gdn_dt_bias, gdn_norm_w,
                 final_norm_w):
    t_len, d = x2d.shape
    n_chunks = t_len // CHUNK
    n_main = 8 * 512
    n_small = w_full.shape[1] - n_main
    w_main = w_full[:, :n_main].astype(MXU_DTYPE)
    w_small = jnp.pad(w_full[:, n_main:], ((0, 0), (0, LANES - n_small))).astype(MXU_DTYPE)
    w_small_t = w_full[:, n_main:].T.astype(MXU_DTYPE)
    w_out_full = w_out_f32.astype(MXU_DTYPE)

    pad_lanes = lambda a, lo: jnp.pad(a.reshape(1, -1), ((0, 0), (lo, LANES - lo - a.size)))
    alog_l, dtb_l = pad_lanes(gdn_A_log, GDN_HEADS), pad_lanes(gdn_dt_bias, GDN_HEADS)
    alog_c, dtb_c = alog_l[:, :8].T, dtb_l[:, :8].T
    sbw = jnp.tile(sb_norm_w, (1, 512 // SB_HEAD_DIM))
    gdw = jnp.tile(gdn_norm_w, (1, 512 // GDN_HEAD_DIM))
    fw = final_norm_w.reshape(1, d)

    proj, ps, pst, h_t, r1 = _inproj_call(x2d, norm1_w, w_main, w_small, w_small_t)
    o_sb, sp_total = _sb_fwd_call(proj, t_len)
    gact = _gdn_prep_call(proj, conv_full, t_len)
    beta_l, gcol_l, grow = _gdn_gates_call(ps, pst, alog_l, dtb_l, alog_c, dtb_c, t_len)
    to_cols = lambda a: a.T.reshape(GDN_HEADS, n_chunks, CHUNK, 1)
    beta_c = to_cols(beta_l[:, :GDN_HEADS])
    gam_c = to_cols(gcol_l[:, GDN_HEADS:2 * GDN_HEADS])
    gam_r = grow[GDN_HEADS:2 * GDN_HEADS].reshape(GDN_HEADS, n_chunks, 1, CHUNK)
    o_gd, s_all, t_all = _gdn_fwd_call(gact, beta_c, gam_c, gam_r, t_len)

    (dx2, d_osb, d_zsb, d_ogd, d_zgd, loss_p, g_fw, g_sbw, g_gdw, g_wout) = _post_call(
        o_sb, o_gd, proj, x2d, tgt, w_out_full, sbw, gdw, fw)

    d_sb3 = _sb_bwd_call(proj, sp_total, d_osb, t_len)
    d_gact3, d_beta_c, d_g_c = _gdn_bwd_call(gact, beta_c, gam_c, gam_r, s_all, t_all, d_ogd, t_len)
    d_gd3, g_conv = _gdn_prep_bwd_call(proj, conv_full, d_gact3, t_len)
    from_cols = lambda a: a.reshape(GDN_HEADS, t_len).T
    d_gates = jnp.pad(jnp.concatenate([from_cols(d_beta_c), from_cols(d_g_c)], axis=1),
                      ((0, 0), (0, LANES - 2 * GDN_HEADS)))
    dsmall, g_alog, g_dtb = _gdn_gates_bwd_call(ps, alog_l, dtb_l, d_gates, t_len)

    dproj8 = jnp.concatenate([d_sb3, d_zsb[None], d_gd3, d_zgd[None]], axis=0)
    g_w_main = _gw_in_call(h_t, dproj8)
    g_w_small = _gw_small_call(h_t, dsmall)
    grad_x, g_n1 = _dx_call(dproj8, dsmall, w_main, w_small, x2d, r1, dx2, norm1_w)
    g_w_in_full = jnp.concatenate([g_w_main, g_w_small[:, :n_small]], axis=1)
    return (loss_p, grad_x, g_n1, g_w_in_full, g_sbw, g_conv, g_alog, g_dtb, g_gdw, g_wout, g_fw)


def kernel(x, norm1_w, w_in, sb_norm_w, gdn_conv_w, gdn_A_log, gdn_dt_bias, gdn_norm_w, w_out, final_norm_w, loss_target, m_norm1_w, m_w_in, m_sb_norm_w, m_gdn_conv_w, m_gdn_A_log, m_gdn_dt_bias, m_gdn_norm_w, m_w_out, m_final_norm_w, v_norm1_w, v_w_in, v_sb_norm_w, v_gdn_conv_w, v_gdn_A_log, v_gdn_dt_bias, v_gdn_norm_w, v_w_out, v_final_norm_w):
    d = x.shape[2]
    shard_cols = w_in.shape[2]
    conv_cols = gdn_conv_w.shape[2]

    w_in_g, w_out_g, conv_g = _exchange_call(
        "gather_weights", [w_in[0], w_out[0], gdn_conv_w[0]], [False, False, False])
    w_full = w_in_g.transpose(1, 0, 2).reshape(d, N_DEV * shard_cols)
    conv_full = conv_g.transpose(1, 0, 2).reshape(CONV_WIDTH, N_DEV * conv_cols)

    (loss_p, grad_x, g_n1, g_w_in_full, g_sbw, g_conv, g_alog, g_dtb, g_gdw, g_wout, g_fw) = _device_step(
        x[0], loss_target[0], w_full, w_out_g.reshape(d, d), conv_full, norm1_w, sb_norm_w, gdn_A_log, gdn_dt_bias,
        gdn_norm_w, final_norm_w)

    g_w_in_parts = g_w_in_full.reshape(d, N_DEV, shard_cols).transpose(1, 0, 2)
    g_wout_parts = g_wout.reshape(N_DEV, d // N_DEV, d)
    g_conv_parts = g_conv.reshape(CONV_WIDTH, N_DEV, conv_cols).transpose(1, 0, 2)
    fold = lambda a, group: a.reshape(-1, group).sum(axis=0)
    small_g = _pack_small([g_n1, g_fw, fold(g_sbw, SB_HEAD_DIM), fold(g_gdw, GDN_HEAD_DIM),
                           g_alog[0, GDN_HEADS:2 * GDN_HEADS], g_dtb[0, GDN_HEADS:2 * GDN_HEADS],
                           loss_p[0, :1]], d)
    p_w_in, p_wout, p_conv, p_small = _exchange_call(
        "exchange_grads", [g_w_in_parts, g_wout_parts, g_conv_parts, small_g], [True, True, True, False])

    small_w = _pack_small([norm1_w, final_norm_w, sb_norm_w, gdn_norm_w, gdn_A_log, gdn_dt_bias], d)
    small_m = _pack_small([m_norm1_w, m_final_norm_w, m_sb_norm_w, m_gdn_norm_w, m_gdn_A_log, m_gdn_dt_bias], d)
    small_v = _pack_small([v_norm1_w, v_final_norm_w, v_sb_norm_w, v_gdn_norm_w, v_gdn_A_log, v_gdn_dt_bias], d)

    r_w_in = _adam_call("adam_w_in", p_w_in, w_in[0], m_w_in[0], v_w_in[0], 256)
    r_wout = _adam_call("adam_w_out", p_wout, w_out[0], m_w_out[0], v_w_out[0], d // N_DEV)
    r_conv = _adam_call("adam_conv", p_conv, gdn_conv_w[0], m_gdn_conv_w[0], v_gdn_conv_w[0], CONV_WIDTH)
    r_small = _adam_call("adam_small", p_small, small_w, small_m, small_v, 8)

    shapes = {"norm1_w": norm1_w.shape, "final_norm_w": final_norm_w.shape, "sb_norm_w": sb_norm_w.shape,
              "gdn_norm_w": gdn_norm_w.shape, "gdn_A_log": gdn_A_log.shape, "gdn_dt_bias": gdn_dt_bias.shape}

    def small_out(kind, name):
        row = _SMALL_ROWS.index(name)
        shp = shapes[name]
        size = 1
        for s in shp:
            size *= s
        return r_small[kind][row, :size].reshape(shp)

    def outputs(kind):
        return (small_out(kind, "norm1_w"), r_w_in[kind][None], small_out(kind, "sb_norm_w"), r_conv[kind][None],
                small_out(kind, "gdn_A_log"), small_out(kind, "gdn_dt_bias"), small_out(kind, "gdn_norm_w"),
                r_wout[kind][None], small_out(kind, "final_norm_w"))

    loss = r_small[0][_SMALL_ROWS.index("loss"), 0]
    return (loss, grad_x[None], *outputs(0), *outputs(1), *outputs(2), *outputs(3))
```

```python
import functools

import jax
import jax.numpy as jnp
from jax import lax
from jax.experimental import pallas as pl
from jax.experimental.pallas import tpu as pltpu

F32 = jnp.float32
MXU_DTYPE = jnp.bfloat16
EXACT = lax.Precision.HIGHEST
EPS = 1e-6
N_DEV = 8
SB_HEAD_DIM = 64
GDN_HEAD_DIM = 128
GDN_HEADS = 4
CHUNK = 64
CONV_WIDTH = 4
LANES = 128
SB_BLOCK = 128
SB_BQ = 256
VMEM_LIMIT_BYTES = 56 * 1024 * 1024

ADAM_LR = 0.001
ADAM_B1 = 0.9
ADAM_B2 = 0.999
ADAM_EPS = 1e-08
ADAM_WD = 0.01
ADAM_STEP = 10

_NT = (((1,), (1,)), ((), ()))
_TN = (((0,), (0,)), ((), ()))


def _mm(a, b):
    return jnp.dot(a.astype(MXU_DTYPE), b.astype(MXU_DTYPE), preferred_element_type=F32)


def _mm_nt(a, b):
    return lax.dot_general(a.astype(MXU_DTYPE), b.astype(MXU_DTYPE), _NT, preferred_element_type=F32)


def _mm_tn(a, b):
    return lax.dot_general(a.astype(MXU_DTYPE), b.astype(MXU_DTYPE), _TN, preferred_element_type=F32)


def _mx(a, b):
    return jnp.dot(a, b, precision=EXACT, preferred_element_type=F32)


def _mx_nt(a, b):
    return lax.dot_general(a, b, _NT, precision=EXACT, preferred_element_type=F32)


def _mx_tn(a, b):
    return lax.dot_general(a, b, _TN, precision=EXACT, preferred_element_type=F32)


def _sigmoid(z):
    return 1.0 / (1.0 + jnp.exp(-z))


def _softplus(z):
    return jnp.maximum(z, 0.0) + jnp.log(1.0 + jnp.exp(-jnp.abs(z)))


def _params(*semantics):
    return pltpu.CompilerParams(dimension_semantics=semantics, vmem_limit_bytes=VMEM_LIMIT_BYTES)


def _inproj_call(x, norm_w, w_main, w_small, w_small_t, tm=256):
    t_len, d = x.shape
    n = w_main.shape[1]
    ns = w_small.shape[1]
    nst = w_small_t.shape[0]

    def body(x_ref, nw_ref, wm_ref, ws_ref, wst_ref, pm_ref, ps_ref, pst_ref, ht_ref, r_ref):
        xv = x_ref[...]
        r = lax.rsqrt(jnp.mean(xv * xv, axis=-1, keepdims=True) + EPS)
        h = xv * r * nw_ref[...]
        hb = h.astype(MXU_DTYPE)
        for n0 in range(0, n, 512):
            pm_ref[:, n0:n0 + 512] = jnp.dot(hb, wm_ref[:, n0:n0 + 512], preferred_element_type=F32)
        ps_ref[...] = jnp.dot(hb, ws_ref[...], preferred_element_type=F32)
        pst_ref[...] = lax.dot_general(wst_ref[...], hb, _NT, preferred_element_type=F32)
        ht_ref[...] = h.T.astype(MXU_DTYPE)
        r_ref[...] = r

    return pl.pallas_call(
        body, name="inproj",
        grid=(t_len // tm,),
        in_specs=[pl.BlockSpec((tm, d), lambda i: (i, 0)),
                  pl.BlockSpec((1, d), lambda i: (0, 0)),
                  pl.BlockSpec((d, n), lambda i: (0, 0)),
                  pl.BlockSpec((d, ns), lambda i: (0, 0)),
                  pl.BlockSpec((nst, d), lambda i: (0, 0))],
        out_specs=[pl.BlockSpec((tm, n), lambda i: (i, 0)),
                   pl.BlockSpec((tm, ns), lambda i: (i, 0)),
                   pl.BlockSpec((nst, tm), lambda i: (0, i)),
                   pl.BlockSpec((d, tm), lambda i: (0, i)),
                   pl.BlockSpec((tm, 1), lambda i: (i, 0))],
        out_shape=[jax.ShapeDtypeStruct((t_len, n), F32),
                   jax.ShapeDtypeStruct((t_len, ns), F32),
                   jax.ShapeDtypeStruct((nst, t_len), F32),
                   jax.ShapeDtypeStruct((d, t_len), MXU_DTYPE),
                   jax.ShapeDtypeStruct((t_len, 1), F32)],
        compiler_params=_params("arbitrary"),
    )(x, norm_w, w_main, w_small, w_small_t)


def _running_sum_mm(x, tri):
    hi = x.astype(MXU_DTYPE)
    lo = (x - hi.astype(F32)).astype(MXU_DTYPE)
    return jnp.dot(hi, tri, preferred_element_type=F32) + jnp.dot(lo, tri, preferred_element_type=F32)


def _sb_iotas():
    row_i = lax.broadcasted_iota(jnp.int32, (SB_BQ, SB_BLOCK), 0)
    col_i = lax.broadcasted_iota(jnp.int32, (SB_BQ, SB_BLOCK), 1)
    sq_r = lax.broadcasted_iota(jnp.int32, (SB_BLOCK, SB_BLOCK), 0)
    sq_c = lax.broadcasted_iota(jnp.int32, (SB_BLOCK, SB_BLOCK), 1)
    return row_i, col_i, sq_r, sq_c


def _sb_key_order(qi, tile, carry, descending):
    n_diag = SB_BQ // SB_BLOCK
    n_free = n_diag * qi
    diag = range(n_diag - 1, -1, -1) if descending else range(n_diag)
    if descending:
        for j in diag:
            carry = tile(n_free + j, True, carry)
        return lax.fori_loop(0, n_free, lambda s, c: tile(n_free - 1 - s, False, c), carry)
    carry = lax.fori_loop(0, n_free, lambda s, c: tile(s, False, c), carry)
    for j in diag:
        carry = tile(n_free + j, True, carry)
    return carry


def _sb_fwd_call(proj, t_len):
    nq = t_len // SB_BQ
    scale = float(SB_HEAD_DIM) ** -0.5
    n_pairs = 512 // LANES
    per_pair = LANES // SB_HEAD_DIM

    def body(q_ref, k_ref, v_ref, o_ref, st_ref):
        lane = lax.broadcasted_iota(jnp.int32, (1, LANES), 1)
        row_i, col_i, sq_r, sq_c = _sb_iotas()
        ge = (sq_r >= sq_c).astype(MXU_DTYPE)
        hms = [((lane // SB_HEAD_DIM) == hh).astype(F32) for hh in range(per_pair)]

        def q_loop(qi, carry):
            r0 = pl.multiple_of(qi * SB_BQ, SB_BQ)
            rows = pl.ds(r0, SB_BQ)
            q_all = q_ref[rows, :]
            qms = [(q_all * hm).astype(MXU_DTYPE) for hm in hms]

            def tile(kj, masked, kc):
                acc, cs = kc[0], list(kc[1:])
                s0 = pl.multiple_of(kj * SB_BLOCK, SB_BLOCK)
                cols = pl.ds(s0, SB_BLOCK)
                kb = k_ref[cols, :].astype(MXU_DTYPE)
                v_all = v_ref[cols, :]
                mask = (col_i + s0) < (row_i + r0) if masked else None
                for hh in range(per_pair):
                    z = lax.dot_general(qms[hh], kb, _NT, preferred_element_type=F32) * scale
                    sp = _softplus(z)
                    if masked:
                        sp = jnp.where(mask, sp, 0.0)
                    a = jnp.exp(z - (_running_sum_mm(sp, ge) + cs[hh]))
                    if masked:
                        a = jnp.where(mask, a, 0.0)
                    vm = (v_all * hms[hh]).astype(MXU_DTYPE)
                    acc = acc + jnp.dot(a.astype(MXU_DTYPE), vm, preferred_element_type=F32)
                    cs[hh] = cs[hh] + jnp.sum(sp, axis=-1, keepdims=True)
                return (acc, *cs)

            zero_col = jnp.zeros((SB_BQ, 1), F32)
            out = _sb_key_order(qi, tile, (jnp.zeros((SB_BQ, LANES), F32),) + (zero_col,) * per_pair, True)
            o_ref[rows, :] = out[0]
            for hh in range(per_pair):
                st_ref[hh, rows, :] = out[1 + hh]
            return carry

        lax.fori_loop(0, nq, q_loop, 0)

    return pl.pallas_call(
        body, name="sb_fwd",
        grid=(n_pairs,),
        in_specs=[pl.BlockSpec((t_len, LANES), lambda p: (0, p)),
                  pl.BlockSpec((t_len, LANES), lambda p: (0, n_pairs + p)),
                  pl.BlockSpec((t_len, LANES), lambda p: (0, 2 * n_pairs + p))],
        out_specs=[pl.BlockSpec((t_len, LANES), lambda p: (0, p)),
                   pl.BlockSpec((per_pair, t_len, 1), lambda p: (p, 0, 0))],
        out_shape=[jax.ShapeDtypeStruct((t_len, 512), F32),
                   jax.ShapeDtypeStruct((n_pairs * per_pair, t_len, 1), F32)],
        compiler_params=_params("arbitrary"),
    )(proj, proj, proj)


def _sb_bwd_call(proj, sp_total, d_o, t_len):
    nq = t_len // SB_BQ
    scale = float(SB_HEAD_DIM) ** -0.5
    n_pairs = 512 // LANES
    per_pair = LANES // SB_HEAD_DIM

    def body(q_ref, k_ref, v_ref, st_ref, do_ref, d_ref):
        lane = lax.broadcasted_iota(jnp.int32, (1, LANES), 1)
        row_i, col_i, sq_r, sq_c = _sb_iotas()
        lt = (sq_r < sq_c).astype(MXU_DTYPE)
        le = (sq_r <= sq_c).astype(MXU_DTYPE)
        hms = [((lane // SB_HEAD_DIM) == hh).astype(F32) for hh in range(per_pair)]
        d_ref[1] = jnp.zeros((t_len, LANES), F32)
        d_ref[2] = jnp.zeros((t_len, LANES), F32)

        def q_loop(qi, carry):
            r0 = pl.multiple_of(qi * SB_BQ, SB_BQ)
            rows = pl.ds(r0, SB_BQ)
            q_all, do_all = q_ref[rows, :], do_ref[rows, :]
            qms = [(q_all * hm).astype(MXU_DTYPE) for hm in hms]
            doms = [(do_all * hm).astype(MXU_DTYPE) for hm in hms]
            totals = [st_ref[hh, rows, :] for hh in range(per_pair)]

            def tile(kj, masked, kc):
                dq, cls, gls = kc[0], list(kc[1:1 + per_pair]), list(kc[1 + per_pair:])
                s0 = pl.multiple_of(kj * SB_BLOCK, SB_BLOCK)
                cols = pl.ds(s0, SB_BLOCK)
                k_all, v_all = k_ref[cols, :], v_ref[cols, :]
                kb = k_all.astype(MXU_DTYPE)
                mask = (col_i + s0) < (row_i + r0) if masked else None
                dk_t = jnp.zeros((SB_BLOCK, LANES), F32)
                dv_t = jnp.zeros((SB_BLOCK, LANES), F32)
                for hh in range(per_pair):
                    z = lax.dot_general(qms[hh], kb, _NT, preferred_element_type=F32) * scale
                    sp_all = _softplus(z)
                    sp = jnp.where(mask, sp_all, 0.0) if masked else sp_all
                    a = jnp.exp(z - (totals[hh] - cls[hh] - _running_sum_mm(sp, lt)))
                    if masked:
                        a = jnp.where(mask, a, 0.0)
                    vm = (v_all * hms[hh]).astype(MXU_DTYPE)
                    km = (k_all * hms[hh]).astype(MXU_DTYPE)
                    g = lax.dot_general(doms[hh], vm, _NT, preferred_element_type=F32) * a
                    dz = g - jnp.exp(z - sp_all) * (gls[hh] + _running_sum_mm(g, le))
                    if masked:
                        dz = jnp.where(mask, dz, 0.0)
                    dz = (dz * scale).astype(MXU_DTYPE)
                    dq = dq + jnp.dot(dz, km, preferred_element_type=F32)
                    dk_t = dk_t + lax.dot_general(dz, qms[hh], _TN, preferred_element_type=F32)
                    dv_t = dv_t + lax.dot_general(a.astype(MXU_DTYPE), doms[hh], _TN, preferred_element_type=F32)
                    cls[hh] = cls[hh] + jnp.sum(sp, axis=-1, keepdims=True)
                    gls[hh] = gls[hh] + jnp.sum(g, axis=-1, keepdims=True)
                d_ref[1, cols, :] += dk_t
                d_ref[2, cols, :] += dv_t
                return (dq, *cls, *gls)

            zero_col = jnp.zeros((SB_BQ, 1), F32)
            out = _sb_key_order(qi, tile, (jnp.zeros((SB_BQ, LANES), F32),) + (zero_col,) * (2 * per_pair), False)
            d_ref[0, rows, :] = out[0]
            return carry

        lax.fori_loop(0, nq, q_loop, 0)

    col = lambda off: pl.BlockSpec((t_len, LANES), lambda p: (0, off + p))
    return pl.pallas_call(
        body, name="sb_bwd",
        grid=(n_pairs,),
        in_specs=[col(0), col(n_pairs), col(2 * n_pairs),
                  pl.BlockSpec((per_pair, t_len, 1), lambda p: (p, 0, 0)), col(0)],
        out_specs=pl.BlockSpec((3, t_len, LANES), lambda p: (0, 0, p)),
        out_shape=jax.ShapeDtypeStruct((3, t_len, 512), F32),
        compiler_params=_params("arbitrary"),
    )(proj, proj, proj, sp_total, d_o)


def _conv_taps(xin, rows, t_len):
    taps = []
    for i in range(CONV_WIDTH):
        shift = CONV_WIDTH - 1 - i
        if shift == 0:
            taps.append(xin)
        else:
            taps.append(jnp.where(rows >= shift, pltpu.roll(xin, shift, axis=0), 0.0))
    return taps


def _gdn_prep_body_common(x_ref, w_ref, t_len):
    j = pl.program_id(0)
    xin = x_ref[...]
    rows = lax.broadcasted_iota(jnp.int32, (t_len, LANES), 0)
    taps = _conv_taps(xin, rows, t_len)
    pre = taps[0] * w_ref[0:1, :]
    for i in range(1, CONV_WIDTH):
        pre = pre + taps[i] * w_ref[i:i + 1, :]
    sg = _sigmoid(pre)
    act = pre * sg
    is_qk = j < 2 * GDN_HEADS
    nrm = jnp.where(is_qk, lax.rsqrt(jnp.sum(act * act, axis=-1, keepdims=True) + EPS), 1.0)
    sc = jnp.where(j < GDN_HEADS, float(GDN_HEAD_DIM) ** -0.5, 1.0)
    return j, rows, taps, pre, sg, act, is_qk, nrm, sc


def _gdn_prep_call(proj, conv_w, t_len):
    first = 2048 // LANES

    def body(x_ref, w_ref, out_ref):
        _, _, _, _, _, act, _, nrm, sc = _gdn_prep_body_common(x_ref, w_ref, t_len)
        out_ref[...] = act * nrm * sc

    return pl.pallas_call(
        body, name="gdn_prep",
        grid=(3 * GDN_HEADS,),
        in_specs=[pl.BlockSpec((t_len, LANES), lambda j: (0, first + j)),
                  pl.BlockSpec((CONV_WIDTH, LANES), lambda j: (0, j))],
        out_specs=pl.BlockSpec((t_len, LANES), lambda j: (0, j)),
        out_shape=jax.ShapeDtypeStruct((t_len, 3 * 512), F32),
        compiler_params=_params("arbitrary"),
    )(proj, conv_w)


def _gdn_prep_bwd_call(proj, conv_w, d_act3, t_len):
    first = 2048 // LANES

    def body(x_ref, w_ref, d_ref, dx_ref, dw_ref):
        _, rows, taps, pre, sg, act, is_qk, nrm, sc = _gdn_prep_body_common(x_ref, w_ref, t_len)
        d_out = d_ref[0]
        dn = d_out * sc
        d_norm = nrm * dn - act * (nrm * nrm * nrm) * jnp.sum(dn * act, axis=-1, keepdims=True)
        d_act = jnp.where(is_qk, d_norm, d_out)
        d_pre = d_act * sg * (1.0 + pre * (1.0 - sg))
        dx = d_pre * w_ref[CONV_WIDTH - 1:CONV_WIDTH, :]
        dw_ref[CONV_WIDTH - 1:CONV_WIDTH, :] = jnp.sum(d_pre * taps[CONV_WIDTH - 1], axis=0, keepdims=True)
        for i in range(CONV_WIDTH - 1):
            shift = CONV_WIDTH - 1 - i
            up = jnp.where(rows < t_len - shift, pltpu.roll(d_pre, t_len - shift, axis=0), 0.0)
            dx = dx + up * w_ref[i:i + 1, :]
            dw_ref[i:i + 1, :] = jnp.sum(d_pre * taps[i], axis=0, keepdims=True)
        dx_ref[0] = dx

    return pl.pallas_call(
        body, name="gdn_prep_bwd",
        grid=(3 * GDN_HEADS,),
        in_specs=[pl.BlockSpec((t_len, LANES), lambda j: (0, first + j)),
                  pl.BlockSpec((CONV_WIDTH, LANES), lambda j: (0, j)),
                  pl.BlockSpec((1, t_len, LANES), lambda j: (j // GDN_HEADS, 0, j % GDN_HEADS))],
        out_specs=[pl.BlockSpec((1, t_len, LANES), lambda j: (j // GDN_HEADS, 0, j % GDN_HEADS)),
                   pl.BlockSpec((CONV_WIDTH, LANES), lambda j: (0, j))],
        out_shape=[jax.ShapeDtypeStruct((3, t_len, 512), F32),
                   jax.ShapeDtypeStruct((CONV_WIDTH, 3 * 512), F32)],
        compiler_params=_params("arbitrary"),
    )(proj, conv_w, d_act3)


def _chunk_cumsum_matrix():
    r = lax.broadcasted_iota(jnp.int32, (LANES, LANES), 0)
    c = lax.broadcasted_iota(jnp.int32, (LANES, LANES), 1)
    return ((r <= c) & ((r // CHUNK) == (c // CHUNK))).astype(F32)


def _gdn_gates_call(ps, pst, alog_l, dtb_l, alog_c, dtb_c, t_len):
    def body(ps_ref, pst_ref, al_ref, dl_ref, ac_ref, dc_ref, beta_ref, gcol_ref, grow_ref):
        upper = _chunk_cumsum_matrix()
        lower = upper.T
        psv = ps_ref[...]
        beta_ref[...] = _sigmoid(psv)
        g_l = -jnp.exp(al_ref[...]) * _softplus(psv + dl_ref[...])
        g_r = -jnp.exp(ac_ref[...]) * _softplus(pst_ref[...] + dc_ref[...])
        for w in range(t_len // LANES):
            sl = slice(w * LANES, (w + 1) * LANES)
            gcol_ref[sl, :] = _mx(lower, g_l[sl, :])
            grow_ref[:, sl] = _mx(g_r[:, sl], upper)

    vm = pl.BlockSpec(memory_space=pltpu.VMEM)
    return pl.pallas_call(
        body, name="gdn_gates",
        in_specs=[vm] * 6, out_specs=[vm] * 3,
        out_shape=[jax.ShapeDtypeStruct((t_len, LANES), F32),
                   jax.ShapeDtypeStruct((t_len, LANES), F32),
                   jax.ShapeDtypeStruct((8, t_len), F32)],
        compiler_params=pltpu.CompilerParams(vmem_limit_bytes=VMEM_LIMIT_BYTES),
    )(ps, pst, alog_l, dtb_l, alog_c, dtb_c)


def _gdn_gates_bwd_call(ps, alog_l, dtb_l, d_l, t_len):
    def body(ps_ref, al_ref, dl_ref, d_ref, dps_ref, gal_ref, gdt_ref):
        lane = lax.broadcasted_iota(jnp.int32, (1, LANES), 1)
        psv = ps_ref[...]
        dv = d_ref[...]
        beta = _sigmoid(psv)
        ea = jnp.exp(al_ref[...])
        arg = psv + dl_ref[...]
        g = -ea * _softplus(arg)
        d_a = dv * (-ea) * _sigmoid(arg)
        is_a = (lane >= GDN_HEADS) & (lane < 2 * GDN_HEADS)
        dps_ref[...] = jnp.where(lane < GDN_HEADS, dv * beta * (1.0 - beta), jnp.where(is_a, d_a, 0.0))
        gdt_ref[...] = jnp.where(is_a, jnp.sum(d_a, axis=0, keepdims=True), 0.0)
        gal_ref[...] = jnp.where(is_a, jnp.sum(dv * g, axis=0, keepdims=True), 0.0)

    vm = pl.BlockSpec(memory_space=pltpu.VMEM)
    return pl.pallas_call(
        body, name="gdn_gates_bwd",
        in_specs=[vm] * 4, out_specs=[vm] * 3,
        out_shape=[jax.ShapeDtypeStruct((t_len, LANES), F32),
                   jax.ShapeDtypeStruct((1, LANES), F32),
                   jax.ShapeDtypeStruct((1, LANES), F32)],
        compiler_params=pltpu.CompilerParams(vmem_limit_bytes=VMEM_LIMIT_BYTES),
    )(ps, alog_l, dtb_l, d_l)


def _chunk_terms(q_ref, k_ref, v_ref, b_ref, gc_ref, gr_ref, i, incl, strict):
    r0 = pl.multiple_of(i * CHUNK, CHUNK)
    rows = pl.ds(r0, CHUNK)
    q, k, v = q_ref[rows, :], k_ref[rows, :], v_ref[rows, :]
    gc, gr, b = gc_ref[0, i], gr_ref[0, i], b_ref[0, i]
    dm = jnp.where(incl, jnp.exp(jnp.where(incl, gc - gr, 0.0)), 0.0)
    kb = k * b
    vb = v * b
    e = jnp.exp(gc)
    a = jnp.where(strict, _mx_nt(kb, k) * dm, 0.0)
    p = jnp.where(incl, _mx_nt(q, k) * dm, 0.0)
    gl = gc[CHUNK - 1:CHUNK, :]
    eg = jnp.exp(gl - gc)
    return rows, q, k, v, b, gc, dm, kb, vb, e, a, p, gl, eg


def _gdn_fwd_call(gact, beta_c, gam_c, gam_r, t_len):
    n_chunks = t_len // CHUNK
    dk = GDN_HEAD_DIM

    def body(q_ref, k_ref, v_ref, b_ref, gc_ref, gr_ref, o_ref, s_ref, t_ref):
        row = lax.broadcasted_iota(jnp.int32, (CHUNK, CHUNK), 0)
        col = lax.broadcasted_iota(jnp.int32, (CHUNK, CHUNK), 1)
        incl, strict = row >= col, row > col
        eye = (row == col).astype(F32)

        def chunk(i, s):
            rows, q, k, v, b, gc, dm, kb, vb, e, a, p, gl, eg = _chunk_terms(
                q_ref, k_ref, v_ref, b_ref, gc_ref, gr_ref, i, incl, strict)
            x = -a
            tm = eye + x
            xp = x
            for _ in range(5):
                xp = _mx(xp, xp)
                tm = tm + _mx(tm, xp)
            u = _mx(tm, vb)
            w = _mx(tm, kb * e)
            vn = u - _mx(w, s)
            o_ref[rows, :] = _mx(q * e, s) + _mx(p, vn)
            s_ref[0, i] = s
            t_ref[0, i] = tm
            return s * jnp.exp(gl) + _mx_tn(k * eg, vn)

        lax.fori_loop(0, n_chunks, chunk, jnp.zeros((dk, dk), F32))

    col_spec = lambda off: pl.BlockSpec((t_len, LANES), lambda h: (0, off + h))
    gate_c = pl.BlockSpec((1, n_chunks, CHUNK, 1), lambda h: (h, 0, 0, 0))
    gate_r = pl.BlockSpec((1, n_chunks, 1, CHUNK), lambda h: (h, 0, 0, 0))
    return pl.pallas_call(
        body, name="gdn_fwd",
        grid=(GDN_HEADS,),
        in_specs=[col_spec(0), col_spec(GDN_HEADS), col_spec(2 * GDN_HEADS), gate_c, gate_c, gate_r],
        out_specs=[pl.BlockSpec((t_len, LANES), lambda h: (0, h)),
                   pl.BlockSpec((1, n_chunks, dk, dk), lambda h: (h, 0, 0, 0)),
                   pl.BlockSpec((1, n_chunks, CHUNK, CHUNK), lambda h: (h, 0, 0, 0))],
        out_shape=[jax.ShapeDtypeStruct((t_len, 512), F32),
                   jax.ShapeDtypeStruct((GDN_HEADS, n_chunks, dk, dk), F32),
                   jax.ShapeDtypeStruct((GDN_HEADS, n_chunks, CHUNK, CHUNK), F32)],
        compiler_params=_params("arbitrary"),
    )(gact, gact, gact, beta_c, gam_c, gam_r)


def _gdn_bwd_call(gact, beta_c, gam_c, gam_r, s_all, t_all, d_o, t_len):
    n_chunks = t_len // CHUNK
    dk = GDN_HEAD_DIM

    def body(q_ref, k_ref, v_ref, b_ref, gc_ref, gr_ref, s_ref, t_ref, do_ref, d_ref, db_ref, dg_ref):
        row = lax.broadcasted_iota(jnp.int32, (CHUNK, CHUNK), 0)
        col = lax.broadcasted_iota(jnp.int32, (CHUNK, CHUNK), 1)
        incl, strict = row >= col, row > col
        upper = (row <= col).astype(F32)
        ones = jnp.ones((CHUNK, LANES), F32)
        last_row = lax.broadcasted_iota(jnp.int32, (CHUNK, 1), 0) == CHUNK - 1
        rsum = lambda m: jnp.sum(m, axis=-1, keepdims=True)

        def chunk(step, ds):
            i = n_chunks - 1 - step
            rows, q, k, v, b, gc, dm, kb, vb, e, a, p, gl, eg = _chunk_terms(
                q_ref, k_ref, v_ref, b_ref, gc_ref, gr_ref, i, incl, strict)
            s = s_ref[0, i]
            tm = t_ref[0, i]
            d_out = do_ref[rows, :]
            el = jnp.exp(gl)
            kbe = kb * e
            u = _mx(tm, vb)
            w = _mx(tm, kbe)
            vn = u - _mx(w, s)
            qe = q * e
            kd = k * eg

            d_vn = _mx_tn(p, d_out) + _mx(kd, ds)
            d_qe = _mx_nt(d_out, s)
            d_p = jnp.where(incl, _mx_nt(d_out, vn), 0.0)
            ds_new = el * ds + _mx_tn(qe, d_out) - _mx_tn(w, d_vn)
            d_kd = _mx_nt(vn, ds)
            d_w = -_mx_nt(d_vn, s)
            d_vb = _mx_tn(tm, d_vn)
            d_kbe = _mx_tn(tm, d_w)
            d_a = -jnp.where(strict, _mx_nt(d_vb, u) + _mx_nt(d_kbe, w), 0.0)
            m = d_a * dm
            n = d_p * dm
            d_kb = _mx(m, k) + d_kbe * e
            d_q = _mx(n, k) + d_qe * e
            d_k = _mx_tn(m, kb) + _mx_tn(n, q) + d_kd * eg + b * d_kb
            r = d_a * a + d_p * p
            kd_term = rsum(d_kd * kd)
            d_gl = jnp.sum(ds * s) * el + jnp.sum(kd_term)
            d_gam = (rsum(r) - _mx_tn(r, ones)[:, 0:1] + rsum(d_qe * qe) + rsum(d_kbe * kbe) - kd_term
                     + jnp.where(last_row, d_gl, 0.0))
            d_ref[0, rows, :] = d_q
            d_ref[1, rows, :] = d_k
            d_ref[2, rows, :] = b * d_vb
            db_ref[0, i] = rsum(d_kb * k) + rsum(d_vb * v)
            dg_ref[0, i] = _mx(upper, d_gam * ones)[:, 0:1]
            return ds_new

        lax.fori_loop(0, n_chunks, chunk, jnp.zeros((dk, dk), F32))

    col_spec = lambda off: pl.BlockSpec((t_len, LANES), lambda h: (0, off + h))
    gate_c = pl.BlockSpec((1, n_chunks, CHUNK, 1), lambda h: (h, 0, 0, 0))
    gate_r = pl.BlockSpec((1, n_chunks, 1, CHUNK), lambda h: (h, 0, 0, 0))
    return pl.pallas_call(
        body, name="gdn_bwd",
        grid=(GDN_HEADS,),
        in_specs=[col_spec(0), col_spec(GDN_HEADS), col_spec(2 * GDN_HEADS), gate_c, gate_c, gate_r,
                  pl.BlockSpec((1, n_chunks, dk, dk), lambda h: (h, 0, 0, 0)),
                  pl.BlockSpec((1, n_chunks, CHUNK, CHUNK), lambda h: (h, 0, 0, 0)),
                  pl.BlockSpec((t_len, LANES), lambda h: (0, h))],
        out_specs=[pl.BlockSpec((3, t_len, LANES), lambda h: (0, 0, h)), gate_c, gate_c],
        out_shape=[jax.ShapeDtypeStruct((3, t_len, 512), F32),
                   jax.ShapeDtypeStruct((GDN_HEADS, n_chunks, CHUNK, 1), F32),
                   jax.ShapeDtypeStruct((GDN_HEADS, n_chunks, CHUNK, 1), F32)],
        compiler_params=_params("arbitrary"),
    )(gact, gact, gact, beta_c, gam_c, gam_r, s_all, t_all, d_o)


def _group_matrix(width, group):
    r = lax.broadcasted_iota(jnp.int32, (width, width), 0)
    c = lax.broadcasted_iota(jnp.int32, (width, width), 1)
    return ((r // group) == (c // group)).astype(F32)


def _post_call(o_sb, o_gd, proj, x, target, w_out, sbw, gdw, fw, tm=256):
    t_len, d = x.shape
    half = 512
    zsb_blk = 1536 // half
    zgd_blk = 3584 // half

    def body(osb_ref, ogd_ref, zsb_ref, zgd_ref, x_ref, tg_ref, wo_ref, sbw_ref, gdw_ref, fw_ref,
             dx2_ref, dosb_ref, dzsb_ref, dogd_ref, dzgd_ref, loss_ref, gfw_ref, gsb_ref, ggd_ref, gwo_ref):
        step = pl.program_id(0)

        @pl.when(step == 0)
        def _():
            loss_ref[...] = jnp.zeros_like(loss_ref)
            gfw_ref[...] = jnp.zeros_like(gfw_ref)
            gsb_ref[...] = jnp.zeros_like(gsb_ref)
            ggd_ref[...] = jnp.zeros_like(ggd_ref)
            gwo_ref[...] = jnp.zeros_like(gwo_ref)

        def head_forward(o, z, w, gmat, inv):
            r = lax.rsqrt(_mx(o * o, gmat) * inv + EPS)
            nrm = o * r * w
            sg = _sigmoid(z)
            return r, nrm, sg, nrm * (z * sg)

        def head_backward(d_m, o, z, w, gmat, inv, r, nrm, sg):
            d_n = d_m * (z * sg)
            d_z = d_m * nrm * (sg * (1.0 + z * (1.0 - sg)))
            dnw = d_n * w
            d_o = r * dnw - o * (r * r * r) * (_mx(dnw * o, gmat) * inv)
            return d_o, d_z, jnp.sum(d_n * o * r, axis=0, keepdims=True)

        g_sb = _group_matrix(half, SB_HEAD_DIM)
        g_gd = _group_matrix(half, GDN_HEAD_DIM)
        osb, ogd, zsb, zgd = osb_ref[...], ogd_ref[...], zsb_ref[...], zgd_ref[...]
        sbw_v, gdw_v = sbw_ref[...], gdw_ref[...]
        r_sb, n_sb, sg_sb, m_sb = head_forward(osb, zsb, sbw_v, g_sb, 1.0 / SB_HEAD_DIM)
        r_gd, n_gd, sg_gd, m_gd = head_forward(ogd, zgd, gdw_v, g_gd, 1.0 / GDN_HEAD_DIM)
        mixed = jnp.concatenate([m_sb, m_gd], axis=1).astype(MXU_DTYPE)
        wo = wo_ref[...]
        x2 = x_ref[...] + jnp.dot(mixed, wo, preferred_element_type=F32)
        r2 = lax.rsqrt(jnp.mean(x2 * x2, axis=-1, keepdims=True) + EPS)
        fw_v = fw_ref[...]
        err = x2 * r2 * fw_v - tg_ref[...]
        loss_ref[...] += 0.5 * jnp.sum(jnp.sum(err * err, axis=-1, keepdims=True) * (1.0 / d))
        dy = err * (1.0 / d)
        gg = dy * fw_v
        dx2 = r2 * gg - x2 * ((r2 * r2 * r2) * jnp.mean(gg * x2, axis=-1, keepdims=True))
        gfw_ref[...] += jnp.sum(dy * x2 * r2, axis=0, keepdims=True)
        dx2_ref[...] = dx2
        dx2b = dx2.astype(MXU_DTYPE)
        d_mixed = lax.dot_general(dx2b, wo, _NT, preferred_element_type=F32)
        gwo_ref[...] += lax.dot_general(mixed, dx2b, _TN, preferred_element_type=F32)
        d_osb, d_zsb, gsb = head_backward(d_mixed[:, :half], osb, zsb, sbw_v, g_sb, 1.0 / SB_HEAD_DIM, r_sb, n_sb, sg_sb)
        d_ogd, d_zgd, ggd = head_backward(d_mixed[:, half:], ogd, zgd, gdw_v, g_gd, 1.0 / GDN_HEAD_DIM, r_gd, n_gd, sg_gd)
        dosb_ref[...] = d_osb
        dzsb_ref[...] = d_zsb
        dogd_ref[...] = d_ogd
        dzgd_ref[...] = d_zgd
        gsb_ref[...] += gsb
        ggd_ref[...] += ggd

    row_blk = lambda w: pl.BlockSpec((tm, w), lambda i: (i, 0))
    fixed = lambda r, w: pl.BlockSpec((r, w), lambda i: (0, 0))
    return pl.pallas_call(
        body, name="post",
        grid=(t_len // tm,),
        in_specs=[row_blk(half), row_blk(half),
                  pl.BlockSpec((tm, half), lambda i: (i, zsb_blk)),
                  pl.BlockSpec((tm, half), lambda i: (i, zgd_blk)),
                  row_blk(d), row_blk(d), fixed(d, d), fixed(1, half), fixed(1, half), fixed(1, d)],
        out_specs=[row_blk(d), row_blk(half), row_blk(half), row_blk(half), row_blk(half),
                   fixed(1, LANES), fixed(1, d), fixed(1, half), fixed(1, half), fixed(d, d)],
        out_shape=[jax.ShapeDtypeStruct((t_len, d), F32)] + [jax.ShapeDtypeStruct((t_len, half), F32)] * 4
                  + [jax.ShapeDtypeStruct((1, LANES), F32), jax.ShapeDtypeStruct((1, d), F32),
                     jax.ShapeDtypeStruct((1, half), F32), jax.ShapeDtypeStruct((1, half), F32),
                     jax.ShapeDtypeStruct((d, d), F32)],
        compiler_params=_params("arbitrary"),
    )(o_sb, o_gd, proj, proj, x, target, w_out, sbw, gdw, fw)


def _gw_in_call(h_t, dproj8, tm=512):
    d, t_len = h_t.shape
    n_piece, _, pw = dproj8.shape

    def body(ht_ref, dp_ref, gw_ref):
        @pl.when(pl.program_id(1) == 0)
        def _():
            gw_ref[...] = jnp.zeros_like(gw_ref)

        gw_ref[...] += jnp.dot(ht_ref[...], dp_ref[0].astype(MXU_DTYPE), preferred_element_type=F32)

    return pl.pallas_call(
        body, name="gw_in",
        grid=(n_piece, t_len // tm),
        in_specs=[pl.BlockSpec((d, tm), lambda p, t: (0, t)),
                  pl.BlockSpec((1, tm, pw), lambda p, t: (p, t, 0))],
        out_specs=pl.BlockSpec((d, pw), lambda p, t: (0, p)),
        out_shape=jax.ShapeDtypeStruct((d, n_piece * pw), F32),
        compiler_params=_params("arbitrary", "arbitrary"),
    )(h_t, dproj8)


def _gw_small_call(h_t, dsmall, tm=512):
    d, t_len = h_t.shape
    ns = dsmall.shape[1]

    def body(ht_ref, dp_ref, gw_ref):
        @pl.when(pl.program_id(0) == 0)
        def _():
            gw_ref[...] = jnp.zeros_like(gw_ref)

        gw_ref[...] += jnp.dot(ht_ref[...], dp_ref[...].astype(MXU_DTYPE), preferred_element_type=F32)

    return pl.pallas_call(
        body, name="gw_small",
        grid=(t_len // tm,),
        in_specs=[pl.BlockSpec((d, tm), lambda t: (0, t)),
                  pl.BlockSpec((tm, ns), lambda t: (t, 0))],
        out_specs=pl.BlockSpec((d, ns), lambda t: (0, 0)),
        out_shape=jax.ShapeDtypeStruct((d, ns), F32),
        compiler_params=_params("arbitrary"),
    )(h_t, dsmall)


def _dx_call(dproj8, dsmall, w_main, w_small, x, r, dx2, norm_w, tm=256):
    t_len, d = x.shape
    n_piece, _, pw = dproj8.shape
    ns = dsmall.shape[1]

    def body(dp_ref, ds_ref, wm_ref, ws_ref, x_ref, r_ref, dx2_ref, nw_ref, gx_ref, gnw_ref):
        @pl.when(pl.program_id(0) == 0)
        def _():
            gnw_ref[...] = jnp.zeros_like(gnw_ref)

        dh = lax.dot_general(ds_ref[...].astype(MXU_DTYPE), ws_ref[...], _NT, preferred_element_type=F32)
        for p in range(n_piece):
            dh = dh + lax.dot_general(dp_ref[p].astype(MXU_DTYPE), wm_ref[:, p * pw:(p + 1) * pw], _NT,
                                      preferred_element_type=F32)
        xv, rv = x_ref[...], r_ref[...]
        dn = dh * nw_ref[...]
        gx_ref[...] = dx2_ref[...] + rv * dn - xv * ((rv * rv * rv) * jnp.mean(dn * xv, axis=-1, keepdims=True))
        gnw_ref[...] += jnp.sum(dh * xv * rv, axis=0, keepdims=True)

    return pl.pallas_call(
        body, name="dx",
        grid=(t_len // tm,),
        in_specs=[pl.BlockSpec((n_piece, tm, pw), lambda i: (0, i, 0)),
                  pl.BlockSpec((tm, ns), lambda i: (i, 0)),
                  pl.BlockSpec((d, n_piece * pw), lambda i: (0, 0)),
                  pl.BlockSpec((d, ns), lambda i: (0, 0)),
                  pl.BlockSpec((tm, d), lambda i: (i, 0)),
                  pl.BlockSpec((tm, 1), lambda i: (i, 0)),
                  pl.BlockSpec((tm, d), lambda i: (i, 0)),
                  pl.BlockSpec((1, d), lambda i: (0, 0))],
        out_specs=[pl.BlockSpec((tm, d), lambda i: (i, 0)),
                   pl.BlockSpec((1, d), lambda i: (0, 0))],
        out_shape=[jax.ShapeDtypeStruct((t_len, d), F32), jax.ShapeDtypeStruct((1, d), F32)],
        compiler_params=_params("arbitrary"),
    )(dproj8, dsmall, w_main, w_small, x, r, dx2, norm_w)


def _exchange_call(name, srcs, per_peer):
    n = len(srcs)
    out_shapes = [jax.ShapeDtypeStruct(s.shape if pp else (N_DEV,) + s.shape, s.dtype) for s, pp in zip(srcs, per_peer)]

    def body(*refs):
        src_refs, out_refs = refs[:n], refs[n:2 * n]
        send_sems, recv_sems, local_sems = refs[2 * n:]
        x, y, c = lax.axis_index("x"), lax.axis_index("y"), lax.axis_index("c")
        me = 4 * x + 2 * y + c
        copies = []
        for a in range(n):
            mine = src_refs[a].at[me] if per_peer[a] else src_refs[a]
            local = pltpu.make_async_copy(mine, out_refs[a].at[me], local_sems.at[a])
            local.start()
            copies.append(local)
        remote = []
        for k in range(1, N_DEV):
            kx, ky, kc = (k >> 2) & 1, (k >> 1) & 1, k & 1
            px = 1 - x if kx else x
            py = 1 - y if ky else y
            pc = 1 - c if kc else c
            peer = 4 * px + 2 * py + pc
            for a in range(n):
                sem = a * (N_DEV - 1) + (k - 1)
                src = src_refs[a].at[peer] if per_peer[a] else src_refs[a]
                cp = pltpu.make_async_remote_copy(
                    src_ref=src, dst_ref=out_refs[a].at[me],
                    send_sem=send_sems.at[sem], recv_sem=recv_sems.at[sem],
                    device_id=(px, py, pc), device_id_type=pl.DeviceIdType.MESH)
                cp.start()
                remote.append(cp)
        for cp in remote:
            cp.wait_send()
        for cp in remote:
            cp.wait_recv()
        for cp in copies:
            cp.wait()

    hbm = pl.BlockSpec(memory_space=pl.ANY)
    return pl.pallas_call(
        body, name=name,
        in_specs=[hbm] * n, out_specs=[hbm] * n, out_shape=out_shapes,
        scratch_shapes=[pltpu.SemaphoreType.DMA((n * (N_DEV - 1),)),
                        pltpu.SemaphoreType.DMA((n * (N_DEV - 1),)),
                        pltpu.SemaphoreType.DMA((n,))],
    )(*srcs)


def _adam_call(name, parts, w, m, v, tr):
    rows, cols = w.shape

    def body(p_ref, w_ref, m_ref, v_ref, g_ref, d_ref, nm_ref, nv_ref):
        g = p_ref[0]
        for s in range(1, N_DEV):
            g = g + p_ref[s]
        m_new = ADAM_B1 * m_ref[...] + (1.0 - ADAM_B1) * g
        v_new = ADAM_B2 * v_ref[...] + (1.0 - ADAM_B2) * (g * g)
        m_hat = m_new / (1.0 - ADAM_B1 ** ADAM_STEP)
        v_hat = v_new / (1.0 - ADAM_B2 ** ADAM_STEP)
        g_ref[...] = g
        d_ref[...] = -ADAM_LR * (m_hat / (jnp.sqrt(v_hat) + ADAM_EPS) + ADAM_WD * w_ref[...])
        nm_ref[...] = m_new
        nv_ref[...] = v_new

    blk = pl.BlockSpec((tr, cols), lambda i: (i, 0))
    return pl.pallas_call(
        body, name=name,
        grid=(rows // tr,),
        in_specs=[pl.BlockSpec((N_DEV, tr, cols), lambda i: (0, i, 0)), blk, blk, blk],
        out_specs=[blk] * 4,
        out_shape=[jax.ShapeDtypeStruct((rows, cols), F32)] * 4,
        compiler_params=_params("arbitrary"),
    )(parts, w, m, v)


_SMALL_ROWS = ("norm1_w", "final_norm_w", "sb_norm_w", "gdn_norm_w", "gdn_A_log", "gdn_dt_bias", "loss")


def _pack_small(vals, width):
    rows = [jnp.pad(a.reshape(1, -1).astype(F32), ((0, 0), (0, width - a.size))) for a in vals]
    rows += [jnp.zeros((1, width), F32)] * (8 - len(rows))
    return jnp.concatenate(rows, axis=0)


def _device_step(x2d, tgt, w_full, w_out_f32, conv_full, norm1_w, sb_norm_w, gdn_A_log, gdn_dt_bias, gdn_norm_w,
                 final_norm_w):
    t_len, d = x2d.shape
    n_chunks = t_len // CHUNK
    n_main = 8 * 512
    n_small = w_full.shape[1] - n_main
    w_main = w_full[:, :n_main].astype(MXU_DTYPE)
    w_small = jnp.pad(w_full[:, n_main:], ((0, 0), (0, LANES - n_small))).astype(MXU_DTYPE)
    w_small_t = w_full[:, n_main:].T.astype(MXU_DTYPE)
    w_out_full = w_out_f32.astype(MXU_DTYPE)

    pad_lanes = lambda a, lo: jnp.pad(a.reshape(1, -1), ((0, 0), (lo, LANES - lo - a.size)))
    alog_l, dtb_l = pad_lanes(gdn_A_log, GDN_HEADS), pad_lanes(gdn_dt_bias, GDN_HEADS)
    alog_c, dtb_c = alog_l[:, :8].T, dtb_l[:, :8].T
    sbw = jnp.tile(sb_norm_w, (1, 512 // SB_HEAD_DIM))
    gdw = jnp.tile(gdn_norm_w, (1, 512 // GDN_HEAD_DIM))
    fw = final_norm_w.reshape(1, d)

    proj, ps, pst, h_t, r1 = _inproj_call(x2d, norm1_w, w_main, w_small, w_small_t)
    o_sb, sp_total = _sb_fwd_call(proj, t_len)
    gact = _gdn_prep_call(proj, conv_full, t_len)
    beta_l, gcol_l, grow = _gdn_gates_call(ps, pst, alog_l, dtb_l, alog_c, dtb_c, t_len)
    to_cols = lambda a: a.T.reshape(GDN_HEADS, n_chunks, CHUNK, 1)
    beta_c = to_cols(beta_l[:, :GDN_HEADS])
    gam_c = to_cols(gcol_l[:, GDN_HEADS:2 * GDN_HEADS])
    gam_r = grow[GDN_HEADS:2 * GDN_HEADS].reshape(GDN_HEADS, n_chunks, 1, CHUNK)
    o_gd, s_all, t_all = _gdn_fwd_call(gact, beta_c, gam_c, gam_r, t_len)

    (dx2, d_osb, d_zsb, d_ogd, d_zgd, loss_p, g_fw, g_sbw, g_gdw, g_wout) = _post_call(
        o_sb, o_gd, proj, x2d, tgt, w_out_full, sbw, gdw, fw)

    d_sb3 = _sb_bwd_call(proj, sp_total, d_osb, t_len)
    d_gact3, d_beta_c, d_g_c = _gdn_bwd_call(gact, beta_c, gam_c, gam_r, s_all, t_all, d_ogd, t_len)
    d_gd3, g_conv = _gdn_prep_bwd_call(proj, conv_full, d_gact3, t_len)
    from_cols = lambda a: a.reshape(GDN_HEADS, t_len).T
    d_gates = jnp.pad(jnp.concatenate([from_cols(d_beta_c), from_cols(d_g_c)], axis=1),
                      ((0, 0), (0, LANES - 2 * GDN_HEADS)))
    dsmall, g_alog, g_dtb = _gdn_gates_bwd_call(ps, alog_l, dtb_l, d_gates, t_len)

    dproj8 = jnp.concatenate([d_sb3, d_zsb[None], d_gd3, d_zgd[None]], axis=0)
    g_w_main = _gw_in_call(h_t, dproj8)
    g_w_small = _gw_small_call(h_t, dsmall)
    grad_x, g_n1 = _dx_call(dproj8, dsmall, w_main, w_small, x2d, r1, dx2, norm1_w)
    g_w_in_full = jnp.concatenate([g_w_main, g_w_small[:, :n_small]], axis=1)
    return (loss_p, grad_x, g_n1, g_w_in_full, g_sbw, g_conv, g_alog, g_dtb, g_gdw, g_wout, g_fw)


def kernel(x, norm1_w, w_in, sb_norm_w, gdn_conv_w, gdn_A_log, gdn_dt_bias, gdn_norm_w, w_out, final_norm_w, loss_target, m_norm1_w, m_w_in, m_sb_norm_w, m_gdn_conv_w, m_gdn_A_log, m_gdn_dt_bias, m_gdn_norm_w, m_w_out, m_final_norm_w, v_norm1_w, v_w_in, v_sb_norm_w, v_gdn_conv_w, v_gdn_A_log, v_gdn_dt_bias, v_gdn_norm_w, v_w_out, v_final_norm_w):
    d = x.shape[2]
    shard_cols = w_in.shape[2]
    conv_cols = gdn_conv_w.shape[2]

    w_in_g, w_out_g, conv_g = _exchange_call(
        "gather_weights", [w_in[0], w_out[0], gdn_conv_w[0]], [False, False, False])
    w_full = w_in_g.transpose(1, 0, 2).reshape(d, N_DEV * shard_cols)
    conv_full = conv_g.transpose(1, 0, 2).reshape(CONV_WIDTH, N_DEV * conv_cols)

    (loss_p, grad_x, g_n1, g_w_in_full, g_sbw, g_conv, g_alog, g_dtb, g_gdw, g_wout, g_fw) = _device_step(
        x[0], loss_target[0], w_full, w_out_g.reshape(d, d), conv_full, norm1_w, sb_norm_w, gdn_A_log, gdn_dt_bias,
        gdn_norm_w, final_norm_w)

    g_w_in_parts = g_w_in_full.reshape(d, N_DEV, shard_cols).transpose(1, 0, 2)
    g_wout_parts = g_wout.reshape(N_DEV, d // N_DEV, d)
    g_conv_parts = g_conv.reshape(CONV_WIDTH, N_DEV, conv_cols).transpose(1, 0, 2)
    fold = lambda a, group: a.reshape(-1, group).sum(axis=0)
    small_g = _pack_small([g_n1, g_fw, fold(g_sbw, SB_HEAD_DIM), fold(g_gdw, GDN_HEAD_DIM),
                           g_alog[0, GDN_HEADS:2 * GDN_HEADS], g_dtb[0, GDN_HEADS:2 * GDN_HEADS],
                           loss_p[0, :1]], d)
    p_w_in, p_wout, p_conv, p_small = _exchange_call(
        "exchange_grads", [g_w_in_parts, g_wout_parts, g_conv_parts, small_g], [True, True, True, False])

    small_w = _pack_small([norm1_w, final_norm_w, sb_norm_w, gdn_norm_w, gdn_A_log, gdn_dt_bias], d)
    small_m = _pack_small([m_norm1_w, m_final_norm_w, m_sb_norm_w, m_gdn_norm_w, m_gdn_A_log, m_gdn_dt_bias], d)
    small_v = _pack_small([v_norm1_w, v_final_norm_w, v_sb_norm_w, v_gdn_norm_w, v_gdn_A_log, v_gdn_dt_bias], d)

    r_w_in = _adam_call("adam_w_in", p_w_in, w_in[0], m_w_in[0], v_w_in[0], 256)
    r_wout = _adam_call("adam_w_out", p_wout, w_out[0], m_w_out[0], v_w_out[0], d // N_DEV)
    r_conv = _adam_call("adam_conv", p_conv, gdn_conv_w[0], m_gdn_conv_w[0], v_gdn_conv_w[0], CONV_WIDTH)
    r_small = _adam_call("adam_small", p_small, small_w, small_m, small_v, 8)

    shapes = {"norm1_w": norm1_w.shape, "final_norm_w": final_norm_w.shape, "sb_norm_w": sb_norm_w.shape,
              "gdn_norm_w": gdn_norm_w.shape, "gdn_A_log": gdn_A_log.shape, "gdn_dt_bias": gdn_dt_bias.shape}

    def small_out(kind, name):
        row = _SMALL_ROWS.index(name)
        shp = shapes[name]
        size = 1
        for s in shp:
            size *= s
        return r_small[kind][row, :size].reshape(shp)

    def outputs(kind):
        return (small_out(kind, "norm1_w"), r_w_in[kind][None], small_out(kind, "sb_norm_w"), r_conv[kind][None],
                small_out(kind, "gdn_A_log"), small_out(kind, "gdn_dt_bias"), small_out(kind, "gdn_norm_w"),
                r_wout[kind][None], small_out(kind, "final_norm_w"))

    loss = r_small[0][_SMALL_ROWS.index("loss"), 0]
    return (loss, grad_x[None], *outputs(0), *outputs(1), *outputs(2), *outputs(3))
```

```python
import functools

import jax
import jax.numpy as jnp
from jax import lax
from jax.experimental import pallas as pl
from jax.experimental.pallas import tpu as pltpu

F32 = jnp.float32
MXU_DTYPE = jnp.bfloat16
EXACT = lax.Precision.HIGHEST
EPS = 1e-6
N_DEV = 8
SB_HEAD_DIM = 64
GDN_HEAD_DIM = 128
GDN_HEADS = 4
GDN_HEADS_PER_STEP = 2
CHUNK = 64
CONV_WIDTH = 4
LANES = 128
SB_BLOCK = 128
SB_BQ = 256
VMEM_LIMIT_BYTES = 56 * 1024 * 1024

ADAM_LR = 0.001
ADAM_B1 = 0.9
ADAM_B2 = 0.999
ADAM_EPS = 1e-08
ADAM_WD = 0.01
ADAM_STEP = 10

_NN = (((1,), (0,)), ((), ()))
_NT = (((1,), (1,)), ((), ()))
_TN = (((0,), (0,)), ((), ()))


def _mm(a, b):
    return jnp.dot(a.astype(MXU_DTYPE), b.astype(MXU_DTYPE), preferred_element_type=F32)


def _mm_nt(a, b):
    return lax.dot_general(a.astype(MXU_DTYPE), b.astype(MXU_DTYPE), _NT, preferred_element_type=F32)


def _mm_tn(a, b):
    return lax.dot_general(a.astype(MXU_DTYPE), b.astype(MXU_DTYPE), _TN, preferred_element_type=F32)


def _mx(a, b):
    return jnp.dot(a, b, precision=EXACT, preferred_element_type=F32)


def _mx_nt(a, b):
    return lax.dot_general(a, b, _NT, precision=EXACT, preferred_element_type=F32)


def _mx_tn(a, b):
    return lax.dot_general(a, b, _TN, precision=EXACT, preferred_element_type=F32)


def _split(x):
    hi = x.astype(MXU_DTYPE)
    return hi, (x - hi.astype(F32)).astype(MXU_DTYPE)


def _m3_general(a, b, dims):
    ah, al = _split(a)
    bh, bl = _split(b)
    dot = lambda x, y: lax.dot_general(x, y, dims, preferred_element_type=F32)
    return dot(ah, bh) + (dot(ah, bl) + dot(al, bh))


def _m3(a, b):
    return _m3_general(a, b, _NN)


def _m3_nt(a, b):
    return _m3_general(a, b, _NT)


def _m3_tn(a, b):
    return _m3_general(a, b, _TN)


def _sigmoid(z):
    return 1.0 / (1.0 + jnp.exp(-z))


def _softplus(z):
    return jnp.maximum(z, 0.0) + jnp.log(1.0 + jnp.exp(-jnp.abs(z)))


def _params(*semantics):
    return pltpu.CompilerParams(dimension_semantics=semantics, vmem_limit_bytes=VMEM_LIMIT_BYTES)


def _inproj_call(x, norm_w, w_main, w_small, w_small_t, tm=256):
    t_len, d = x.shape
    n = w_main.shape[1]
    ns = w_small.shape[1]
    nst = w_small_t.shape[0]

    def body(x_ref, nw_ref, wm_ref, ws_ref, wst_ref, pm_ref, ps_ref, pst_ref, ht_ref, r_ref):
        xv = x_ref[...]
        r = lax.rsqrt(jnp.mean(xv * xv, axis=-1, keepdims=True) + EPS)
        h = xv * r * nw_ref[...]
        hb = h.astype(MXU_DTYPE)
        for n0 in range(0, n, 512):
            pm_ref[:, n0:n0 + 512] = jnp.dot(hb, wm_ref[:, n0:n0 + 512], preferred_element_type=F32)
        ps_ref[...] = jnp.dot(hb, ws_ref[...], preferred_element_type=F32)
        pst_ref[...] = lax.dot_general(wst_ref[...], hb, _NT, preferred_element_type=F32)
        ht_ref[...] = h.T.astype(MXU_DTYPE)
        r_ref[...] = r

    return pl.pallas_call(
        body, name="inproj",
        grid=(t_len // tm,),
        in_specs=[pl.BlockSpec((tm, d), lambda i: (i, 0)),
                  pl.BlockSpec((1, d), lambda i: (0, 0)),
                  pl.BlockSpec((d, n), lambda i: (0, 0)),
                  pl.BlockSpec((d, ns), lambda i: (0, 0)),
                  pl.BlockSpec((nst, d), lambda i: (0, 0))],
        out_specs=[pl.BlockSpec((tm, n), lambda i: (i, 0)),
                   pl.BlockSpec((tm, ns), lambda i: (i, 0)),
                   pl.BlockSpec((nst, tm), lambda i: (0, i)),
                   pl.BlockSpec((d, tm), lambda i: (0, i)),
                   pl.BlockSpec((tm, 1), lambda i: (i, 0))],
        out_shape=[jax.ShapeDtypeStruct((t_len, n), F32),
                   jax.ShapeDtypeStruct((t_len, ns), F32),
                   jax.ShapeDtypeStruct((nst, t_len), F32),
                   jax.ShapeDtypeStruct((d, t_len), MXU_DTYPE),
                   jax.ShapeDtypeStruct((t_len, 1), F32)],
        compiler_params=_params("arbitrary"),
    )(x, norm_w, w_main, w_small, w_small_t)


def _running_sum_mm(x, tri):
    hi = x.astype(MXU_DTYPE)
    lo = (x - hi.astype(F32)).astype(MXU_DTYPE)
    return jnp.dot(hi, tri, preferred_element_type=F32) + jnp.dot(lo, tri, preferred_element_type=F32)


def _sb_iotas():
    row_i = lax.broadcasted_iota(jnp.int32, (SB_BQ, SB_BLOCK), 0)
    col_i = lax.broadcasted_iota(jnp.int32, (SB_BQ, SB_BLOCK), 1)
    sq_r = lax.broadcasted_iota(jnp.int32, (SB_BLOCK, SB_BLOCK), 0)
    sq_c = lax.broadcasted_iota(jnp.int32, (SB_BLOCK, SB_BLOCK), 1)
    return row_i, col_i, sq_r, sq_c


def _sb_key_order(qi, tile, carry, descending):
    n_diag = SB_BQ // SB_BLOCK
    n_free = n_diag * qi
    diag = range(n_diag - 1, -1, -1) if descending else range(n_diag)
    if descending:
        for j in diag:
            carry = tile(n_free + j, True, carry)
        return lax.fori_loop(0, n_free, lambda s, c: tile(n_free - 1 - s, False, c), carry)
    carry = lax.fori_loop(0, n_free, lambda s, c: tile(s, False, c), carry)
    for j in diag:
        carry = tile(n_free + j, True, carry)
    return carry


def _sb_fwd_call(proj, t_len):
    nq = t_len // SB_BQ
    scale = float(SB_HEAD_DIM) ** -0.5
    n_pairs = 512 // LANES
    per_pair = LANES // SB_HEAD_DIM

    def body(q_ref, k_ref, v_ref, o_ref, st_ref):
        lane = lax.broadcasted_iota(jnp.int32, (1, LANES), 1)
        row_i, col_i, sq_r, sq_c = _sb_iotas()
        ge = (sq_r >= sq_c).astype(MXU_DTYPE)
        hms = [((lane // SB_HEAD_DIM) == hh).astype(F32) for hh in range(per_pair)]

        def q_loop(qi, carry):
            r0 = pl.multiple_of(qi * SB_BQ, SB_BQ)
            rows = pl.ds(r0, SB_BQ)
            q_all = q_ref[rows, :]
            qms = [(q_all * hm).astype(MXU_DTYPE) for hm in hms]

            def tile(kj, masked, kc):
                acc, cs = kc[0], list(kc[1:])
                s0 = pl.multiple_of(kj * SB_BLOCK, SB_BLOCK)
                cols = pl.ds(s0, SB_BLOCK)
                kb = k_ref[cols, :].astype(MXU_DTYPE)
                v_all = v_ref[cols, :]
                mask = (col_i + s0) < (row_i + r0) if masked else None
                for hh in range(per_pair):
                    z = lax.dot_general(qms[hh], kb, _NT, preferred_element_type=F32) * scale
                    sp = _softplus(z)
                    if masked:
                        sp = jnp.where(mask, sp, 0.0)
                    a = jnp.exp(z - (_running_sum_mm(sp, ge) + cs[hh]))
                    if masked:
                        a = jnp.where(mask, a, 0.0)
                    vm = (v_all * hms[hh]).astype(MXU_DTYPE)
                    acc = acc + jnp.dot(a.astype(MXU_DTYPE), vm, preferred_element_type=F32)
                    cs[hh] = cs[hh] + jnp.sum(sp, axis=-1, keepdims=True)
                return (acc, *cs)

            zero_col = jnp.zeros((SB_BQ, 1), F32)
            out = _sb_key_order(qi, tile, (jnp.zeros((SB_BQ, LANES), F32),) + (zero_col,) * per_pair, True)
            o_ref[rows, :] = out[0]
            for hh in range(per_pair):
                st_ref[hh, rows, :] = out[1 + hh]
            return carry

        lax.fori_loop(0, nq, q_loop, 0)

    return pl.pallas_call(
        body, name="sb_fwd",
        grid=(n_pairs,),
        in_specs=[pl.BlockSpec((t_len, LANES), lambda p: (0, p)),
                  pl.BlockSpec((t_len, LANES), lambda p: (0, n_pairs + p)),
                  pl.BlockSpec((t_len, LANES), lambda p: (0, 2 * n_pairs + p))],
        out_specs=[pl.BlockSpec((t_len, LANES), lambda p: (0, p)),
                   pl.BlockSpec((per_pair, t_len, 1), lambda p: (p, 0, 0))],
        out_shape=[jax.ShapeDtypeStruct((t_len, 512), F32),
                   jax.ShapeDtypeStruct((n_pairs * per_pair, t_len, 1), F32)],
        compiler_params=_params("arbitrary"),
    )(proj, proj, proj)


def _sb_bwd_call(proj, sp_total, d_o, t_len):
    nq = t_len // SB_BQ
    scale = float(SB_HEAD_DIM) ** -0.5
    n_pairs = 512 // LANES
    per_pair = LANES // SB_HEAD_DIM

    def body(q_ref, k_ref, v_ref, st_ref, do_ref, d_ref):
        lane = lax.broadcasted_iota(jnp.int32, (1, LANES), 1)
        row_i, col_i, sq_r, sq_c = _sb_iotas()
        lt = (sq_r < sq_c).astype(MXU_DTYPE)
        le = (sq_r <= sq_c).astype(MXU_DTYPE)
        hms = [((lane // SB_HEAD_DIM) == hh).astype(F32) for hh in range(per_pair)]
        d_ref[1] = jnp.zeros((t_len, LANES), F32)
        d_ref[2] = jnp.zeros((t_len, LANES), F32)

        def q_loop(qi, carry):
            r0 = pl.multiple_of(qi * SB_BQ, SB_BQ)
            rows = pl.ds(r0, SB_BQ)
            q_all, do_all = q_ref[rows, :], do_ref[rows, :]
            qms = [(q_all * hm).astype(MXU_DTYPE) for hm in hms]
            doms = [(do_all * hm).astype(MXU_DTYPE) for hm in hms]
            totals = [st_ref[hh, rows, :] for hh in range(per_pair)]

            def tile(kj, masked, kc):
                dq, cls, gls = kc[0], list(kc[1:1 + per_pair]), list(kc[1 + per_pair:])
                s0 = pl.multiple_of(kj * SB_BLOCK, SB_BLOCK)
                cols = pl.ds(s0, SB_BLOCK)
                k_all, v_all = k_ref[cols, :], v_ref[cols, :]
                kb = k_all.astype(MXU_DTYPE)
                mask = (col_i + s0) < (row_i + r0) if masked else None
                dk_t = jnp.zeros((SB_BLOCK, LANES), F32)
                dv_t = jnp.zeros((SB_BLOCK, LANES), F32)
                for hh in range(per_pair):
                    z = lax.dot_general(qms[hh], kb, _NT, preferred_element_type=F32) * scale
                    sp_all = _softplus(z)
                    sp = jnp.where(mask, sp_all, 0.0) if masked else sp_all
                    a = jnp.exp(z - (totals[hh] - cls[hh] - _running_sum_mm(sp, lt)))
                    if masked:
                        a = jnp.where(mask, a, 0.0)
                    vm = (v_all * hms[hh]).astype(MXU_DTYPE)
                    km = (k_all * hms[hh]).astype(MXU_DTYPE)
                    g = lax.dot_general(doms[hh], vm, _NT, preferred_element_type=F32) * a
                    dz = g - jnp.exp(z - sp_all) * (gls[hh] + _running_sum_mm(g, le))
                    if masked:
                        dz = jnp.where(mask, dz, 0.0)
                    dz = (dz * scale).astype(MXU_DTYPE)
                    dq = dq + jnp.dot(dz, km, preferred_element_type=F32)
                    dk_t = dk_t + lax.dot_general(dz, qms[hh], _TN, preferred_element_type=F32)
                    dv_t = dv_t + lax.dot_general(a.astype(MXU_DTYPE), doms[hh], _TN, preferred_element_type=F32)
                    cls[hh] = cls[hh] + jnp.sum(sp, axis=-1, keepdims=True)
                    gls[hh] = gls[hh] + jnp.sum(g, axis=-1, keepdims=True)
                d_ref[1, cols, :] += dk_t
                d_ref[2, cols, :] += dv_t
                return (dq, *cls, *gls)

            zero_col = jnp.zeros((SB_BQ, 1), F32)
            out = _sb_key_order(qi, tile, (jnp.zeros((SB_BQ, LANES), F32),) + (zero_col,) * (2 * per_pair), False)
            d_ref[0, rows, :] = out[0]
            return carry

        lax.fori_loop(0, nq, q_loop, 0)

    col = lambda off: pl.BlockSpec((t_len, LANES), lambda p: (0, off + p))
    return pl.pallas_call(
        body, name="sb_bwd",
        grid=(n_pairs,),
        in_specs=[col(0), col(n_pairs), col(2 * n_pairs),
                  pl.BlockSpec((per_pair, t_len, 1), lambda p: (p, 0, 0)), col(0)],
        out_specs=pl.BlockSpec((3, t_len, LANES), lambda p: (0, 0, p)),
        out_shape=jax.ShapeDtypeStruct((3, t_len, 512), F32),
        compiler_params=_params("arbitrary"),
    )(proj, proj, proj, sp_total, d_o)


def _conv_taps(xin, rows, t_len):
    taps = []
    for i in range(CONV_WIDTH):
        shift = CONV_WIDTH - 1 - i
        if shift == 0:
            taps.append(xin)
        else:
            taps.append(jnp.where(rows >= shift, pltpu.roll(xin, shift, axis=0), 0.0))
    return taps


def _gdn_prep_body_common(x_ref, w_ref, t_len):
    j = pl.program_id(0)
    xin = x_ref[...]
    rows = lax.broadcasted_iota(jnp.int32, (t_len, LANES), 0)
    taps = _conv_taps(xin, rows, t_len)
    pre = taps[0] * w_ref[0:1, :]
    for i in range(1, CONV_WIDTH):
        pre = pre + taps[i] * w_ref[i:i + 1, :]
    sg = _sigmoid(pre)
    act = pre * sg
    is_qk = j < 2 * GDN_HEADS
    nrm = jnp.where(is_qk, lax.rsqrt(jnp.sum(act * act, axis=-1, keepdims=True) + EPS), 1.0)
    sc = jnp.where(j < GDN_HEADS, float(GDN_HEAD_DIM) ** -0.5, 1.0)
    return j, rows, taps, pre, sg, act, is_qk, nrm, sc


def _gdn_prep_call(proj, conv_w, t_len):
    first = 2048 // LANES

    def body(x_ref, w_ref, out_ref):
        _, _, _, _, _, act, _, nrm, sc = _gdn_prep_body_common(x_ref, w_ref, t_len)
        out_ref[...] = act * nrm * sc

    return pl.pallas_call(
        body, name="gdn_prep",
        grid=(3 * GDN_HEADS,),
        in_specs=[pl.BlockSpec((t_len, LANES), lambda j: (0, first + j)),
                  pl.BlockSpec((CONV_WIDTH, LANES), lambda j: (0, j))],
        out_specs=pl.BlockSpec((t_len, LANES), lambda j: (0, j)),
        out_shape=jax.ShapeDtypeStruct((t_len, 3 * 512), F32),
        compiler_params=_params("arbitrary"),
    )(proj, conv_w)


def _gdn_prep_bwd_call(proj, conv_w, d_act3, t_len):
    first = 2048 // LANES

    def body(x_ref, w_ref, d_ref, dx_ref, dw_ref):
        _, rows, taps, pre, sg, act, is_qk, nrm, sc = _gdn_prep_body_common(x_ref, w_ref, t_len)
        d_out = d_ref[0]
        dn = d_out * sc
        d_norm = nrm * dn - act * (nrm * nrm * nrm) * jnp.sum(dn * act, axis=-1, keepdims=True)
        d_act = jnp.where(is_qk, d_norm, d_out)
        d_pre = d_act * sg * (1.0 + pre * (1.0 - sg))
        dx = d_pre * w_ref[CONV_WIDTH - 1:CONV_WIDTH, :]
        dw_ref[CONV_WIDTH - 1:CONV_WIDTH, :] = jnp.sum(d_pre * taps[CONV_WIDTH - 1], axis=0, keepdims=True)
        for i in range(CONV_WIDTH - 1):
            shift = CONV_WIDTH - 1 - i
            up = jnp.where(rows < t_len - shift, pltpu.roll(d_pre, t_len - shift, axis=0), 0.0)
            dx = dx + up * w_ref[i:i + 1, :]
            dw_ref[i:i + 1, :] = jnp.sum(d_pre * taps[i], axis=0, keepdims=True)
        dx_ref[0] = dx

    return pl.pallas_call(
        body, name="gdn_prep_bwd",
        grid=(3 * GDN_HEADS,),
        in_specs=[pl.BlockSpec((t_len, LANES), lambda j: (0, first + j)),
                  pl.BlockSpec((CONV_WIDTH, LANES), lambda j: (0, j)),
                  pl.BlockSpec((1, t_len, LANES), lambda j: (j // GDN_HEADS, 0, j % GDN_HEADS))],
        out_specs=[pl.BlockSpec((1, t_len, LANES), lambda j: (j // GDN_HEADS, 0, j % GDN_HEADS)),
                   pl.BlockSpec((CONV_WIDTH, LANES), lambda j: (0, j))],
        out_shape=[jax.ShapeDtypeStruct((3, t_len, 512), F32),
                   jax.ShapeDtypeStruct((CONV_WIDTH, 3 * 512), F32)],
        compiler_params=_params("arbitrary"),
    )(proj, conv_w, d_act3)


def _chunk_cumsum_matrix():
    r = lax.broadcasted_iota(jnp.int32, (LANES, LANES), 0)
    c = lax.broadcasted_iota(jnp.int32, (LANES, LANES), 1)
    return ((r <= c) & ((r // CHUNK) == (c // CHUNK))).astype(F32)


def _gdn_gates_call(ps, pst, alog_l, dtb_l, alog_c, dtb_c, t_len):
    def body(ps_ref, pst_ref, al_ref, dl_ref, ac_ref, dc_ref, beta_ref, gcol_ref, grow_ref):
        upper = _chunk_cumsum_matrix()
        lower = upper.T
        psv = ps_ref[...]
        beta_ref[...] = _sigmoid(psv)
        g_l = -jnp.exp(al_ref[...]) * _softplus(psv + dl_ref[...])
        g_r = -jnp.exp(ac_ref[...]) * _softplus(pst_ref[...] + dc_ref[...])
        for w in range(t_len // LANES):
            sl = slice(w * LANES, (w + 1) * LANES)
            gcol_ref[sl, :] = _mx(lower, g_l[sl, :])
            grow_ref[:, sl] = _mx(g_r[:, sl], upper)

    vm = pl.BlockSpec(memory_space=pltpu.VMEM)
    return pl.pallas_call(
        body, name="gdn_gates",
        in_specs=[vm] * 6, out_specs=[vm] * 3,
        out_shape=[jax.ShapeDtypeStruct((t_len, LANES), F32),
                   jax.ShapeDtypeStruct((t_len, LANES), F32),
                   jax.ShapeDtypeStruct((8, t_len), F32)],
        compiler_params=pltpu.CompilerParams(vmem_limit_bytes=VMEM_LIMIT_BYTES),
    )(ps, pst, alog_l, dtb_l, alog_c, dtb_c)


def _gdn_gates_bwd_call(ps, alog_l, dtb_l, d_l, t_len):
    def body(ps_ref, al_ref, dl_ref, d_ref, dps_ref, gal_ref, gdt_ref):
        lane = lax.broadcasted_iota(jnp.int32, (1, LANES), 1)
        psv = ps_ref[...]
        dv = d_ref[...]
        beta = _sigmoid(psv)
        ea = jnp.exp(al_ref[...])
        arg = psv + dl_ref[...]
        g = -ea * _softplus(arg)
        d_a = dv * (-ea) * _sigmoid(arg)
        is_a = (lane >= GDN_HEADS) & (lane < 2 * GDN_HEADS)
        dps_ref[...] = jnp.where(lane < GDN_HEADS, dv * beta * (1.0 - beta), jnp.where(is_a, d_a, 0.0))
        gdt_ref[...] = jnp.where(is_a, jnp.sum(d_a, axis=0, keepdims=True), 0.0)
        gal_ref[...] = jnp.where(is_a, jnp.sum(dv * g, axis=0, keepdims=True), 0.0)

    vm = pl.BlockSpec(memory_space=pltpu.VMEM)
    return pl.pallas_call(
        body, name="gdn_gates_bwd",
        in_specs=[vm] * 4, out_specs=[vm] * 3,
        out_shape=[jax.ShapeDtypeStruct((t_len, LANES), F32),
                   jax.ShapeDtypeStruct((1, LANES), F32),
                   jax.ShapeDtypeStruct((1, LANES), F32)],
        compiler_params=pltpu.CompilerParams(vmem_limit_bytes=VMEM_LIMIT_BYTES),
    )(ps, alog_l, dtb_l, d_l)


def _chunk_terms(q_ref, k_ref, v_ref, b_ref, gc_ref, gr_ref, i, j, incl, strict):
    r0 = pl.multiple_of(i * CHUNK, CHUNK)
    rows = pl.ds(r0, CHUNK)
    lanes = slice(j * GDN_HEAD_DIM, (j + 1) * GDN_HEAD_DIM)
    q, k, v = q_ref[rows, lanes], k_ref[rows, lanes], v_ref[rows, lanes]
    head = pl.program_id(0) * GDN_HEADS_PER_STEP + j
    lane_ids = lax.broadcasted_iota(jnp.int32, (1, LANES), 1)
    pick = lambda slab, lane: jnp.sum(jnp.where(lane_ids == lane, slab, 0.0), axis=-1, keepdims=True)
    b = pick(b_ref[rows, :], head)
    gc = pick(gc_ref[rows, :], GDN_HEADS + head)
    gr = gr_ref[j, i]
    dm = jnp.where(incl, jnp.exp(jnp.where(incl, gc - gr, 0.0)), 0.0)
    kb = k * b
    vb = v * b
    e = jnp.exp(gc)
    a = jnp.where(strict, _m3_nt(kb, k) * dm, 0.0)
    p = jnp.where(incl, _m3_nt(q, k) * dm, 0.0)
    gl = gc[CHUNK - 1:CHUNK, :]
    eg = jnp.exp(gl - gc)
    return rows, lanes, q, k, v, b, gc, dm, kb, vb, e, a, p, gl, eg


def _gdn_specs(t_len, n_chunks):
    hb = GDN_HEADS_PER_STEP
    groups = GDN_HEADS // hb
    col_spec = lambda part: pl.BlockSpec((t_len, hb * GDN_HEAD_DIM), lambda g: (0, part * groups + g))
    gate_c = pl.BlockSpec((t_len, LANES), lambda g: (0, 0))
    gate_r = pl.BlockSpec((hb, n_chunks, 1, CHUNK), lambda g: (g, 0, 0, 0))
    per_chunk = lambda r, c: pl.BlockSpec((hb, n_chunks, r, c), lambda g: (g, 0, 0, 0))
    return hb, groups, col_spec, gate_c, gate_r, per_chunk


def _gdn_fwd_call(gact, beta_c, gam_c, gam_r, t_len):
    n_chunks = t_len // CHUNK
    dk = GDN_HEAD_DIM
    hb, groups, col_spec, gate_c, gate_r, per_chunk = _gdn_specs(t_len, n_chunks)

    def body(q_ref, k_ref, v_ref, b_ref, gc_ref, gr_ref, o_ref, s_ref, t_ref):
        row = lax.broadcasted_iota(jnp.int32, (CHUNK, CHUNK), 0)
        col = lax.broadcasted_iota(jnp.int32, (CHUNK, CHUNK), 1)
        incl, strict = row >= col, row > col
        eye = (row == col).astype(F32)

        def chunk(i, states):
            new_states = []
            for j in range(hb):
                s = states[j]
                rows, lanes, q, k, v, b, gc, dm, kb, vb, e, a, p, gl, eg = _chunk_terms(
                    q_ref, k_ref, v_ref, b_ref, gc_ref, gr_ref, i, j, incl, strict)
                x = -a
                tm = eye + x
                xp = x
                for _ in range(5):
                    xp = _m3(xp, xp)
                    tm = tm + _m3(tm, xp)
                u = _m3(tm, vb)
                w = _m3(tm, kb * e)
                vn = u - _m3(w, s)
                o_ref[rows, lanes] = _m3(q * e, s) + _m3(p, vn)
                s_ref[j, i] = s
                t_ref[j, i] = tm
                new_states.append(s * jnp.exp(gl) + _m3_tn(k * eg, vn))
            return tuple(new_states)

        lax.fori_loop(0, n_chunks, chunk, (jnp.zeros((dk, dk), F32),) * hb, unroll=2)

    return pl.pallas_call(
        body, name="gdn_fwd",
        grid=(groups,),
        in_specs=[col_spec(0), col_spec(1), col_spec(2), gate_c, gate_c, gate_r],
        out_specs=[pl.BlockSpec((t_len, hb * dk), lambda g: (0, g)), per_chunk(dk, dk), per_chunk(CHUNK, CHUNK)],
        out_shape=[jax.ShapeDtypeStruct((t_len, 512), F32),
                   jax.ShapeDtypeStruct((GDN_HEADS, n_chunks, dk, dk), F32),
                   jax.ShapeDtypeStruct((GDN_HEADS, n_chunks, CHUNK, CHUNK), F32)],
        compiler_params=_params("arbitrary"),
    )(gact, gact, gact, beta_c, gam_c, gam_r)


def _gdn_bwd_call(gact, beta_c, gam_c, gam_r, s_all, t_all, d_o, t_len):
    n_chunks = t_len // CHUNK
    dk = GDN_HEAD_DIM
    hb, groups, col_spec, gate_c, gate_r, per_chunk = _gdn_specs(t_len, n_chunks)

    def body(q_ref, k_ref, v_ref, b_ref, gc_ref, gr_ref, s_ref, t_ref, do_ref, d_ref, dgate_ref):
        row = lax.broadcasted_iota(jnp.int32, (CHUNK, CHUNK), 0)
        col = lax.broadcasted_iota(jnp.int32, (CHUNK, CHUNK), 1)
        incl, strict = row >= col, row > col
        upper = (row <= col).astype(F32)
        ones = jnp.ones((CHUNK, LANES), F32)
        last_row = lax.broadcasted_iota(jnp.int32, (CHUNK, 1), 0) == CHUNK - 1
        lane_ids = lax.broadcasted_iota(jnp.int32, (1, LANES), 1)
        rsum = lambda m: jnp.sum(m, axis=-1, keepdims=True)

        @pl.when(pl.program_id(0) == 0)
        def _():
            dgate_ref[...] = jnp.zeros_like(dgate_ref)

        def chunk(step, d_states):
            i = n_chunks - 1 - step
            new_states = []
            for j in range(hb):
                ds = d_states[j]
                rows, lanes, q, k, v, b, gc, dm, kb, vb, e, a, p, gl, eg = _chunk_terms(
                    q_ref, k_ref, v_ref, b_ref, gc_ref, gr_ref, i, j, incl, strict)
                s = s_ref[j, i]
                tm = t_ref[j, i]
                d_out = do_ref[rows, lanes]
                el = jnp.exp(gl)
                kbe = kb * e
                u = _m3(tm, vb)
                w = _m3(tm, kbe)
                vn = u - _m3(w, s)
                qe = q * e
                kd = k * eg

                d_vn = _m3_tn(p, d_out) + _m3(kd, ds)
                d_qe = _m3_nt(d_out, s)
                d_p = jnp.where(incl, _m3_nt(d_out, vn), 0.0)
                new_states.append(el * ds + _m3_tn(qe, d_out) - _m3_tn(w, d_vn))
                d_kd = _m3_nt(vn, ds)
                d_w = -_m3_nt(d_vn, s)
                d_vb = _m3_tn(tm, d_vn)
                d_kbe = _m3_tn(tm, d_w)
                d_a = -jnp.where(strict, _m3_nt(d_vb, u) + _m3_nt(d_kbe, w), 0.0)
                m = d_a * dm
                n = d_p * dm
                d_kb = _m3(m, k) + d_kbe * e
                d_q = _m3(n, k) + d_qe * e
                d_k = _m3_tn(m, kb) + _m3_tn(n, q) + d_kd * eg + b * d_kb
                r = d_a * a + d_p * p
                kd_term = rsum(d_kd * kd)
                d_gl = jnp.sum(ds * s) * el + jnp.sum(kd_term)
                d_gam = (rsum(r) - _mx_tn(r, ones)[:, 0:1] + rsum(d_qe * qe) + rsum(d_kbe * kbe) - kd_term
                         + jnp.where(last_row, d_gl, 0.0))
                d_ref[0, rows, lanes] = d_q
                d_ref[1, rows, lanes] = d_k
                d_ref[2, rows, lanes] = b * d_vb
                head = pl.program_id(0) * hb + j
                d_beta = rsum(d_kb * k) + rsum(d_vb * v)
                d_g = _mx(upper, d_gam * ones)[:, 0:1]
                dgate_ref[rows, :] += (jnp.where(lane_ids == head, d_beta, 0.0)
                                       + jnp.where(lane_ids == GDN_HEADS + head, d_g, 0.0))
            return tuple(new_states)

        lax.fori_loop(0, n_chunks, chunk, (jnp.zeros((dk, dk), F32),) * hb)

    return pl.pallas_call(
        body, name="gdn_bwd",
        grid=(groups,),
        in_specs=[col_spec(0), col_spec(1), col_spec(2), gate_c, gate_c, gate_r,
                  per_chunk(dk, dk), per_chunk(CHUNK, CHUNK),
                  pl.BlockSpec((t_len, hb * dk), lambda g: (0, g))],
        out_specs=[pl.BlockSpec((3, t_len, hb * dk), lambda g: (0, 0, g)), gate_c],
        out_shape=[jax.ShapeDtypeStruct((3, t_len, 512), F32),
                   jax.ShapeDtypeStruct((t_len, LANES), F32)],
        compiler_params=_params("arbitrary"),
    )(gact, gact, gact, beta_c, gam_c, gam_r, s_all, t_all, d_o)


def _group_matrix(width, group):
    r = lax.broadcasted_iota(jnp.int32, (width, width), 0)
    c = lax.broadcasted_iota(jnp.int32, (width, width), 1)
    return ((r // group) == (c // group)).astype(F32)


def _post_call(o_sb, o_gd, proj, x, target, w_out, sbw, gdw, fw, tm=256):
    t_len, d = x.shape
    half = 512
    zsb_blk = 1536 // half
    zgd_blk = 3584 // half

    def body(osb_ref, ogd_ref, zsb_ref, zgd_ref, x_ref, tg_ref, wo_ref, sbw_ref, gdw_ref, fw_ref,
             dx2_ref, dosb_ref, dzsb_ref, dogd_ref, dzgd_ref, loss_ref, gfw_ref, gsb_ref, ggd_ref, gwo_ref):
        step = pl.program_id(0)

        @pl.when(step == 0)
        def _():
            loss_ref[...] = jnp.zeros_like(loss_ref)
            gfw_ref[...] = jnp.zeros_like(gfw_ref)
            gsb_ref[...] = jnp.zeros_like(gsb_ref)
            ggd_ref[...] = jnp.zeros_like(ggd_ref)
            gwo_ref[...] = jnp.zeros_like(gwo_ref)

        def head_forward(o, z, w, gmat, inv):
            r = lax.rsqrt(_mx(o * o, gmat) * inv + EPS)
            nrm = o * r * w
            sg = _sigmoid(z)
            return r, nrm, sg, nrm * (z * sg)

        def head_backward(d_m, o, z, w, gmat, inv, r, nrm, sg):
            d_n = d_m * (z * sg)
            d_z = d_m * nrm * (sg * (1.0 + z * (1.0 - sg)))
            dnw = d_n * w
            d_o = r * dnw - o * (r * r * r) * (_mx(dnw * o, gmat) * inv)
            return d_o, d_z, jnp.sum(d_n * o * r, axis=0, keepdims=True)

        g_sb = _group_matrix(half, SB_HEAD_DIM)
        g_gd = _group_matrix(half, GDN_HEAD_DIM)
        osb, ogd, zsb, zgd = osb_ref[...], ogd_ref[...], zsb_ref[...], zgd_ref[...]
        sbw_v, gdw_v = sbw_ref[...], gdw_ref[...]
        r_sb, n_sb, sg_sb, m_sb = head_forward(osb, zsb, sbw_v, g_sb, 1.0 / SB_HEAD_DIM)
        r_gd, n_gd, sg_gd, m_gd = head_forward(ogd, zgd, gdw_v, g_gd, 1.0 / GDN_HEAD_DIM)
        mixed = jnp.concatenate([m_sb, m_gd], axis=1).astype(MXU_DTYPE)
        wo = wo_ref[...]
        x2 = x_ref[...] + jnp.dot(mixed, wo, preferred_element_type=F32)
        r2 = lax.rsqrt(jnp.mean(x2 * x2, axis=-1, keepdims=True) + EPS)
        fw_v = fw_ref[...]
        err = x2 * r2 * fw_v - tg_ref[...]
        loss_ref[...] += 0.5 * jnp.sum(jnp.sum(err * err, axis=-1, keepdims=True) * (1.0 / d))
        dy = err * (1.0 / d)
        gg = dy * fw_v
        dx2 = r2 * gg - x2 * ((r2 * r2 * r2) * jnp.mean(gg * x2, axis=-1, keepdims=True))
        gfw_ref[...] += jnp.sum(dy * x2 * r2, axis=0, keepdims=True)
        dx2_ref[...] = dx2
        dx2b = dx2.astype(MXU_DTYPE)
        d_mixed = lax.dot_general(dx2b, wo, _NT, preferred_element_type=F32)
        gwo_ref[...] += lax.dot_general(mixed, dx2b, _TN, preferred_element_type=F32)
        d_osb, d_zsb, gsb = head_backward(d_mixed[:, :half], osb, zsb, sbw_v, g_sb, 1.0 / SB_HEAD_DIM, r_sb, n_sb, sg_sb)
        d_ogd, d_zgd, ggd = head_backward(d_mixed[:, half:], ogd, zgd, gdw_v, g_gd, 1.0 / GDN_HEAD_DIM, r_gd, n_gd, sg_gd)
        dosb_ref[...] = d_osb
        dzsb_ref[...] = d_zsb
        dogd_ref[...] = d_ogd
        dzgd_ref[...] = d_zgd
        gsb_ref[...] += gsb
        ggd_ref[...] += ggd

    row_blk = lambda w: pl.BlockSpec((tm, w), lambda i: (i, 0))
    fixed = lambda r, w: pl.BlockSpec((r, w), lambda i: (0, 0))
    return pl.pallas_call(
        body, name="post",
        grid=(t_len // tm,),
        in_specs=[row_blk(half), row_blk(half),
                  pl.BlockSpec((tm, half), lambda i: (i, zsb_blk)),
                  pl.BlockSpec((tm, half), lambda i: (i, zgd_blk)),
                  row_blk(d), row_blk(d), fixed(d, d), fixed(1, half), fixed(1, half), fixed(1, d)],
        out_specs=[row_blk(d), row_blk(half), row_blk(half), row_blk(half), row_blk(half),
                   fixed(1, LANES), fixed(1, d), fixed(1, half), fixed(1, half), fixed(d, d)],
        out_shape=[jax.ShapeDtypeStruct((t_len, d), F32)] + [jax.ShapeDtypeStruct((t_len, half), F32)] * 4
                  + [jax.ShapeDtypeStruct((1, LANES), F32), jax.ShapeDtypeStruct((1, d), F32),
                     jax.ShapeDtypeStruct((1, half), F32), jax.ShapeDtypeStruct((1, half), F32),
                     jax.ShapeDtypeStruct((d, d), F32)],
        compiler_params=_params("arbitrary"),
    )(o_sb, o_gd, proj, proj, x, target, w_out, sbw, gdw, fw)


def _gw_in_call(h_t, dproj8, tm=512):
    d, t_len = h_t.shape
    n_piece, _, pw = dproj8.shape

    def body(ht_ref, dp_ref, gw_ref):
        @pl.when(pl.program_id(1) == 0)
        def _():
            gw_ref[...] = jnp.zeros_like(gw_ref)

        gw_ref[...] += jnp.dot(ht_ref[...], dp_ref[0].astype(MXU_DTYPE), preferred_element_type=F32)

    return pl.pallas_call(
        body, name="gw_in",
        grid=(n_piece, t_len // tm),
        in_specs=[pl.BlockSpec((d, tm), lambda p, t: (0, t)),
                  pl.BlockSpec((1, tm, pw), lambda p, t: (p, t, 0))],
        out_specs=pl.BlockSpec((d, pw), lambda p, t: (0, p)),
        out_shape=jax.ShapeDtypeStruct((d, n_piece * pw), F32),
        compiler_params=_params("arbitrary", "arbitrary"),
    )(h_t, dproj8)


def _gw_small_call(h_t, dsmall, tm=512):
    d, t_len = h_t.shape
    ns = dsmall.shape[1]

    def body(ht_ref, dp_ref, gw_ref):
        @pl.when(pl.program_id(0) == 0)
        def _():
            gw_ref[...] = jnp.zeros_like(gw_ref)

        gw_ref[...] += jnp.dot(ht_ref[...], dp_ref[...].astype(MXU_DTYPE), preferred_element_type=F32)

    return pl.pallas_call(
        body, name="gw_small",
        grid=(t_len // tm,),
        in_specs=[pl.BlockSpec((d, tm), lambda t: (0, t)),
                  pl.BlockSpec((tm, ns), lambda t: (t, 0))],
        out_specs=pl.BlockSpec((d, ns), lambda t: (0, 0)),
        out_shape=jax.ShapeDtypeStruct((d, ns), F32),
        compiler_params=_params("arbitrary"),
    )(h_t, dsmall)


def _dx_call(dproj8, dsmall, w_main, w_small, x, r, dx2, norm_w, tm=256):
    t_len, d = x.shape
    n_piece, _, pw = dproj8.shape
    ns = dsmall.shape[1]

    def body(dp_ref, ds_ref, wm_ref, ws_ref, x_ref, r_ref, dx2_ref, nw_ref, gx_ref, gnw_ref):
        @pl.when(pl.program_id(0) == 0)
        def _():
            gnw_ref[...] = jnp.zeros_like(gnw_ref)

        dh = lax.dot_general(ds_ref[...].astype(MXU_DTYPE), ws_ref[...], _NT, preferred_element_type=F32)
        for p in range(n_piece):
            dh = dh + lax.dot_general(dp_ref[p].astype(MXU_DTYPE), wm_ref[:, p * pw:(p + 1) * pw], _NT,
                                      preferred_element_type=F32)
        xv, rv = x_ref[...], r_ref[...]
        dn = dh * nw_ref[...]
        gx_ref[...] = dx2_ref[...] + rv * dn - xv * ((rv * rv * rv) * jnp.mean(dn * xv, axis=-1, keepdims=True))
        gnw_ref[...] += jnp.sum(dh * xv * rv, axis=0, keepdims=True)

    return pl.pallas_call(
        body, name="dx",
        grid=(t_len // tm,),
        in_specs=[pl.BlockSpec((n_piece, tm, pw), lambda i: (0, i, 0)),
                  pl.BlockSpec((tm, ns), lambda i: (i, 0)),
                  pl.BlockSpec((d, n_piece * pw), lambda i: (0, 0)),
                  pl.BlockSpec((d, ns), lambda i: (0, 0)),
                  pl.BlockSpec((tm, d), lambda i: (i, 0)),
                  pl.BlockSpec((tm, 1), lambda i: (i, 0)),
                  pl.BlockSpec((tm, d), lambda i: (i, 0)),
                  pl.BlockSpec((1, d), lambda i: (0, 0))],
        out_specs=[pl.BlockSpec((tm, d), lambda i: (i, 0)),
                   pl.BlockSpec((1, d), lambda i: (0, 0))],
        out_shape=[jax.ShapeDtypeStruct((t_len, d), F32), jax.ShapeDtypeStruct((1, d), F32)],
        compiler_params=_params("arbitrary"),
    )(dproj8, dsmall, w_main, w_small, x, r, dx2, norm_w)


def _exchange_call(name, srcs, per_peer):
    n = len(srcs)
    out_shapes = [jax.ShapeDtypeStruct(s.shape if pp else (N_DEV,) + s.shape, s.dtype) for s, pp in zip(srcs, per_peer)]

    def body(*refs):
        src_refs, out_refs = refs[:n], refs[n:2 * n]
        send_sems, recv_sems, local_sems = refs[2 * n:]
        x, y, c = lax.axis_index("x"), lax.axis_index("y"), lax.axis_index("c")
        me = 4 * x + 2 * y + c
        copies = []
        for a in range(n):
            mine = src_refs[a].at[me] if per_peer[a] else src_refs[a]
            local = pltpu.make_async_copy(mine, out_refs[a].at[me], local_sems.at[a])
            local.start()
            copies.append(local)
        remote = []
        for k in range(1, N_DEV):
            kx, ky, kc = (k >> 2) & 1, (k >> 1) & 1, k & 1
            px = 1 - x if kx else x
            py = 1 - y if ky else y
            pc = 1 - c if kc else c
            peer = 4 * px + 2 * py + pc
            for a in range(n):
                sem = a * (N_DEV - 1) + (k - 1)
                src = src_refs[a].at[peer] if per_peer[a] else src_refs[a]
                cp = pltpu.make_async_remote_copy(
                    src_ref=src, dst_ref=out_refs[a].at[me],
                    send_sem=send_sems.at[sem], recv_sem=recv_sems.at[sem],
                    device_id=(px, py, pc), device_id_type=pl.DeviceIdType.MESH)
                cp.start()
                remote.append(cp)
        for cp in remote:
            cp.wait_send()
        for cp in remote:
            cp.wait_recv()
        for cp in copies:
            cp.wait()

    hbm = pl.BlockSpec(memory_space=pl.ANY)
    return pl.pallas_call(
        body, name=name,
        in_specs=[hbm] * n, out_specs=[hbm] * n, out_shape=out_shapes,
        scratch_shapes=[pltpu.SemaphoreType.DMA((n * (N_DEV - 1),)),
                        pltpu.SemaphoreType.DMA((n * (N_DEV - 1),)),
                        pltpu.SemaphoreType.DMA((n,))],
    )(*srcs)


def _adam_call(name, parts, w, m, v, tr):
    rows, cols = w.shape

    def body(p_ref, w_ref, m_ref, v_ref, g_ref, d_ref, nm_ref, nv_ref):
        g = p_ref[0]
        for s in range(1, N_DEV):
            g = g + p_ref[s]
        m_new = ADAM_B1 * m_ref[...] + (1.0 - ADAM_B1) * g
        v_new = ADAM_B2 * v_ref[...] + (1.0 - ADAM_B2) * (g * g)
        m_hat = m_new / (1.0 - ADAM_B1 ** ADAM_STEP)
        v_hat = v_new / (1.0 - ADAM_B2 ** ADAM_STEP)
        g_ref[...] = g
        d_ref[...] = -ADAM_LR * (m_hat / (jnp.sqrt(v_hat) + ADAM_EPS) + ADAM_WD * w_ref[...])
        nm_ref[...] = m_new
        nv_ref[...] = v_new

    blk = pl.BlockSpec((tr, cols), lambda i: (i, 0))
    return pl.pallas_call(
        body, name=name,
        grid=(rows // tr,),
        in_specs=[pl.BlockSpec((N_DEV, tr, cols), lambda i: (0, i, 0)), blk, blk, blk],
        out_specs=[blk] * 4,
        out_shape=[jax.ShapeDtypeStruct((rows, cols), F32)] * 4,
        compiler_params=_params("arbitrary"),
    )(parts, w, m, v)


_SMALL_ROWS = ("norm1_w", "final_norm_w", "sb_norm_w", "gdn_norm_w", "gdn_A_log", "gdn_dt_bias", "loss")


def _pack_small(vals, width):
    rows = [jnp.pad(a.reshape(1, -1).astype(F32), ((0, 0), (0, width - a.size))) for a in vals]
    rows += [jnp.zeros((1, width), F32)] * (8 - len(rows))
    return jnp.concatenate(rows, axis=0)


def _device_step(x2d, tgt, w_full, w_out_f32, conv_full, norm1_w, sb_norm_w, gdn_A_log, gdn_dt_bias, gdn_norm_w,
                 final_norm_w):
    t_len, d = x2d.shape
    n_chunks = t_len // CHUNK
    n_main = 8 * 512
    n_small = w_full.shape[1] - n_main
    w_main = w_full[:, :n_main].astype(MXU_DTYPE)
    w_small = jnp.pad(w_full[:, n_main:], ((0, 0), (0, LANES - n_small))).astype(MXU_DTYPE)
    w_small_t = w_full[:, n_main:].T.astype(MXU_DTYPE)
    w_out_full = w_out_f32.astype(MXU_DTYPE)

    pad_lanes = lambda a, lo: jnp.pad(a.reshape(1, -1), ((0, 0), (lo, LANES - lo - a.size)))
    alog_l, dtb_l = pad_lanes(gdn_A_log, GDN_HEADS), pad_lanes(gdn_dt_bias, GDN_HEADS)
    alog_c, dtb_c = alog_l[:, :8].T, dtb_l[:, :8].T
    sbw = jnp.tile(sb_norm_w, (1, 512 // SB_HEAD_DIM))
    gdw = jnp.tile(gdn_norm_w, (1, 512 // GDN_HEAD_DIM))
    fw = final_norm_w.reshape(1, d)

    proj, ps, pst, h_t, r1 = _inproj_call(x2d, norm1_w, w_main, w_small, w_small_t)
    o_sb, sp_total = _sb_fwd_call(proj, t_len)
    gact = _gdn_prep_call(proj, conv_full, t_len)
    beta_l, gcol_l, grow = _gdn_gates_call(ps, pst, alog_l, dtb_l, alog_c, dtb_c, t_len)
    gam_r = grow[GDN_HEADS:2 * GDN_HEADS].reshape(GDN_HEADS, n_chunks, 1, CHUNK)
    o_gd, s_all, t_all = _gdn_fwd_call(gact, beta_l, gcol_l, gam_r, t_len)

    (dx2, d_osb, d_zsb, d_ogd, d_zgd, loss_p, g_fw, g_sbw, g_gdw, g_wout) = _post_call(
        o_sb, o_gd, proj, x2d, tgt, w_out_full, sbw, gdw, fw)

    d_sb3 = _sb_bwd_call(proj, sp_total, d_osb, t_len)
    d_gact3, d_gates = _gdn_bwd_call(gact, beta_l, gcol_l, gam_r, s_all, t_all, d_ogd, t_len)
    d_gd3, g_conv = _gdn_prep_bwd_call(proj, conv_full, d_gact3, t_len)
    dsmall, g_alog, g_dtb = _gdn_gates_bwd_call(ps, alog_l, dtb_l, d_gates, t_len)

    dproj8 = jnp.concatenate([d_sb3, d_zsb[None], d_gd3, d_zgd[None]], axis=0)
    g_w_main = _gw_in_call(h_t, dproj8)
    g_w_small = _gw_small_call(h_t, dsmall)
    grad_x, g_n1 = _dx_call(dproj8, dsmall, w_main, w_small, x2d, r1, dx2, norm1_w)
    g_w_in_full = jnp.concatenate([g_w_main, g_w_small[:, :n_small]], axis=1)
    return (loss_p, grad_x, g_n1, g_w_in_full, g_sbw, g_conv, g_alog, g_dtb, g_gdw, g_wout, g_fw)


def kernel(x, norm1_w, w_in, sb_norm_w, gdn_conv_w, gdn_A_log, gdn_dt_bias, gdn_norm_w, w_out, final_norm_w, loss_target, m_norm1_w, m_w_in, m_sb_norm_w, m_gdn_conv_w, m_gdn_A_log, m_gdn_dt_bias, m_gdn_norm_w, m_w_out, m_final_norm_w, v_norm1_w, v_w_in, v_sb_norm_w, v_gdn_conv_w, v_gdn_A_log, v_gdn_dt_bias, v_gdn_norm_w, v_w_out, v_final_norm_w):
    d = x.shape[2]
    shard_cols = w_in.shape[2]
    conv_cols = gdn_conv_w.shape[2]

    w_in_g, w_out_g, conv_g = _exchange_call(
        "gather_weights", [w_in[0], w_out[0], gdn_conv_w[0]], [False, False, False])
    w_full = w_in_g.transpose(1, 0, 2).reshape(d, N_DEV * shard_cols)
    conv_full = conv_g.transpose(1, 0, 2).reshape(CONV_WIDTH, N_DEV * conv_cols)

    (loss_p, grad_x, g_n1, g_w_in_full, g_sbw, g_conv, g_alog, g_dtb, g_gdw, g_wout, g_fw) = _device_step(
        x[0], loss_target[0], w_full, w_out_g.reshape(d, d), conv_full, norm1_w, sb_norm_w, gdn_A_log, gdn_dt_bias,
        gdn_norm_w, final_norm_w)

    g_w_in_parts = g_w_in_full.reshape(d, N_DEV, shard_cols).transpose(1, 0, 2)
    g_wout_parts = g_wout.reshape(N_DEV, d // N_DEV, d)
    g_conv_parts = g_conv.reshape(CONV_WIDTH, N_DEV, conv_cols).transpose(1, 0, 2)
    fold = lambda a, group: a.reshape(-1, group).sum(axis=0)
    small_g = _pack_small([g_n1, g_fw, fold(g_sbw, SB_HEAD_DIM), fold(g_gdw, GDN_HEAD_DIM),
                           g_alog[0, GDN_HEADS:2 * GDN_HEADS], g_dtb[0, GDN_HEADS:2 * GDN_HEADS],
                           loss_p[0, :1]], d)
    p_w_in, p_wout, p_conv, p_small = _exchange_call(
        "exchange_grads", [g_w_in_parts, g_wout_parts, g_conv_parts, small_g], [True, True, True, False])

    small_w = _pack_small([norm1_w, final_norm_w, sb_norm_w, gdn_norm_w, gdn_A_log, gdn_dt_bias], d)
    small_m = _pack_small([m_norm1_w, m_final_norm_w, m_sb_norm_w, m_gdn_norm_w, m_gdn_A_log, m_gdn_dt_bias], d)
    small_v = _pack_small([v_norm1_w, v_final_norm_w, v_sb_norm_w, v_gdn_norm_w, v_gdn_A_log, v_gdn_dt_bias], d)

    r_w_in = _adam_call("adam_w_in", p_w_in, w_in[0], m_w_in[0], v_w_in[0], 256)
    r_wout = _adam_call("adam_w_out", p_wout, w_out[0], m_w_out[0], v_w_out[0], d // N_DEV)
    r_conv = _adam_call("adam_conv", p_conv, gdn_conv_w[0], m_gdn_conv_w[0], v_gdn_conv_w[0], CONV_WIDTH)
    r_small = _adam_call("adam_small", p_small, small_w, small_m, small_v, 8)

    shapes = {"norm1_w": norm1_w.shape, "final_norm_w": final_norm_w.shape, "sb_norm_w": sb_norm_w.shape,
              "gdn_norm_w": gdn_norm_w.shape, "gdn_A_log": gdn_A_log.shape, "gdn_dt_bias": gdn_dt_bias.shape}

    def small_out(kind, name):
        row = _SMALL_ROWS.index(name)
        shp = shapes[name]
        size = 1
        for s in shp:
            size *= s
        return r_small[kind][row, :size].reshape(shp)

    def outputs(kind):
        return (small_out(kind, "norm1_w"), r_w_in[kind][None], small_out(kind, "sb_norm_w"), r_conv[kind][None],
                small_out(kind, "gdn_A_log"), small_out(kind, "gdn_dt_bias"), small_out(kind, "gdn_norm_w"),
                r_wout[kind][None], small_out(kind, "final_norm_w"))

    loss = r_small[0][_SMALL_ROWS.index("loss"), 0]
    return (loss, grad_x[None], *outputs(0), *outputs(1), *outputs(2), *outputs(3))
```

```python
import functools

import jax
import jax.numpy as jnp
from jax import lax
from jax.experimental import pallas as pl
from jax.experimental.pallas import tpu as pltpu

F32 = jnp.float32
MXU_DTYPE = jnp.bfloat16
WIRE_DTYPE = jnp.bfloat16
EXACT = lax.Precision.HIGHEST
EPS = 1e-6
N_DEV = 8
SB_HEAD_DIM = 64
GDN_HEAD_DIM = 128
GDN_HEADS = 4
GDN_HEADS_PER_STEP = 2
CHUNK = 64
CONV_WIDTH = 4
LANES = 128
SB_BLOCK = 128
SB_BQ = 256
VMEM_LIMIT_BYTES = 56 * 1024 * 1024

ADAM_LR = 0.001
ADAM_B1 = 0.9
ADAM_B2 = 0.999
ADAM_EPS = 1e-08
ADAM_WD = 0.01
ADAM_STEP = 10

_NN = (((1,), (0,)), ((), ()))
_NT = (((1,), (1,)), ((), ()))
_TN = (((0,), (0,)), ((), ()))


def _mm(a, b):
    return jnp.dot(a.astype(MXU_DTYPE), b.astype(MXU_DTYPE), preferred_element_type=F32)


def _mm_nt(a, b):
    return lax.dot_general(a.astype(MXU_DTYPE), b.astype(MXU_DTYPE), _NT, preferred_element_type=F32)


def _mm_tn(a, b):
    return lax.dot_general(a.astype(MXU_DTYPE), b.astype(MXU_DTYPE), _TN, preferred_element_type=F32)


def _mx(a, b):
    return jnp.dot(a, b, precision=EXACT, preferred_element_type=F32)


def _mx_nt(a, b):
    return lax.dot_general(a, b, _NT, precision=EXACT, preferred_element_type=F32)


def _mx_tn(a, b):
    return lax.dot_general(a, b, _TN, precision=EXACT, preferred_element_type=F32)


def _split(x):
    hi = x.astype(MXU_DTYPE)
    return hi, (x - hi.astype(F32)).astype(MXU_DTYPE)


def _m3_general(a, b, dims):
    ah, al = _split(a)
    bh, bl = _split(b)
    dot = lambda x, y: lax.dot_general(x, y, dims, preferred_element_type=F32)
    return dot(ah, bh) + (dot(ah, bl) + dot(al, bh))


def _m3(a, b):
    return _m3_general(a, b, _NN)


def _m3_nt(a, b):
    return _m3_general(a, b, _NT)


def _m3_tn(a, b):
    return _m3_general(a, b, _TN)


def _sigmoid(z):
    return 1.0 / (1.0 + jnp.exp(-z))


def _softplus(z):
    return jnp.maximum(z, 0.0) + jnp.log(1.0 + jnp.exp(-jnp.abs(z)))


def _params(*semantics):
    return pltpu.CompilerParams(dimension_semantics=semantics, vmem_limit_bytes=VMEM_LIMIT_BYTES)


def _inproj_call(x, norm_w, w_main, w_small, w_small_t, tm=256):
    t_len, d = x.shape
    n = w_main.shape[1]
    ns = w_small.shape[1]
    nst = w_small_t.shape[0]

    def body(x_ref, nw_ref, wm_ref, ws_ref, wst_ref, pm_ref, ps_ref, pst_ref, ht_ref, r_ref):
        xv = x_ref[...]
        r = lax.rsqrt(jnp.mean(xv * xv, axis=-1, keepdims=True) + EPS)
        h = xv * r * nw_ref[...]
        hb = h.astype(MXU_DTYPE)
        for n0 in range(0, n, 512):
            pm_ref[:, n0:n0 + 512] = jnp.dot(hb, wm_ref[:, n0:n0 + 512], preferred_element_type=F32)
        ps_ref[...] = jnp.dot(hb, ws_ref[...], preferred_element_type=F32)
        pst_ref[...] = lax.dot_general(wst_ref[...], hb, _NT, preferred_element_type=F32)
        ht_ref[...] = h.T.astype(MXU_DTYPE)
        r_ref[...] = r

    return pl.pallas_call(
        body, name="inproj",
        grid=(t_len // tm,),
        in_specs=[pl.BlockSpec((tm, d), lambda i: (i, 0)),
                  pl.BlockSpec((1, d), lambda i: (0, 0)),
                  pl.BlockSpec((d, n), lambda i: (0, 0)),
                  pl.BlockSpec((d, ns), lambda i: (0, 0)),
                  pl.BlockSpec((nst, d), lambda i: (0, 0))],
        out_specs=[pl.BlockSpec((tm, n), lambda i: (i, 0)),
                   pl.BlockSpec((tm, ns), lambda i: (i, 0)),
                   pl.BlockSpec((nst, tm), lambda i: (0, i)),
                   pl.BlockSpec((d, tm), lambda i: (0, i)),
                   pl.BlockSpec((tm, 1), lambda i: (i, 0))],
        out_shape=[jax.ShapeDtypeStruct((t_len, n), F32),
                   jax.ShapeDtypeStruct((t_len, ns), F32),
                   jax.ShapeDtypeStruct((nst, t_len), F32),
                   jax.ShapeDtypeStruct((d, t_len), MXU_DTYPE),
                   jax.ShapeDtypeStruct((t_len, 1), F32)],
        compiler_params=_params("arbitrary"),
    )(x, norm_w, w_main, w_small, w_small_t)


def _running_sum_mm(x, tri):
    hi = x.astype(MXU_DTYPE)
    lo = (x - hi.astype(F32)).astype(MXU_DTYPE)
    return jnp.dot(hi, tri, preferred_element_type=F32) + jnp.dot(lo, tri, preferred_element_type=F32)


def _sb_iotas():
    row_i = lax.broadcasted_iota(jnp.int32, (SB_BQ, SB_BLOCK), 0)
    col_i = lax.broadcasted_iota(jnp.int32, (SB_BQ, SB_BLOCK), 1)
    sq_r = lax.broadcasted_iota(jnp.int32, (SB_BLOCK, SB_BLOCK), 0)
    sq_c = lax.broadcasted_iota(jnp.int32, (SB_BLOCK, SB_BLOCK), 1)
    return row_i, col_i, sq_r, sq_c


def _sb_key_order(qi, tile, carry, descending):
    n_diag = SB_BQ // SB_BLOCK
    n_free = n_diag * qi
    diag = range(n_diag - 1, -1, -1) if descending else range(n_diag)
    if descending:
        for j in diag:
            carry = tile(n_free + j, True, carry)
        return lax.fori_loop(0, n_free, lambda s, c: tile(n_free - 1 - s, False, c), carry)
    carry = lax.fori_loop(0, n_free, lambda s, c: tile(s, False, c), carry)
    for j in diag:
        carry = tile(n_free + j, True, carry)
    return carry


def _sb_fwd_call(proj, t_len):
    nq = t_len // SB_BQ
    scale = float(SB_HEAD_DIM) ** -0.5
    n_pairs = 512 // LANES
    per_pair = LANES // SB_HEAD_DIM

    def body(q_ref, k_ref, v_ref, o_ref, st_ref):
        lane = lax.broadcasted_iota(jnp.int32, (1, LANES), 1)
        row_i, col_i, sq_r, sq_c = _sb_iotas()
        ge = (sq_r >= sq_c).astype(MXU_DTYPE)
        hms = [((lane // SB_HEAD_DIM) == hh).astype(F32) for hh in range(per_pair)]

        def q_loop(qi, carry):
            r0 = pl.multiple_of(qi * SB_BQ, SB_BQ)
            rows = pl.ds(r0, SB_BQ)
            q_all = q_ref[rows, :]
            qms = [(q_all * hm).astype(MXU_DTYPE) for hm in hms]

            def tile(kj, masked, kc):
                acc, cs = kc[0], list(kc[1:])
                s0 = pl.multiple_of(kj * SB_BLOCK, SB_BLOCK)
                cols = pl.ds(s0, SB_BLOCK)
                kb = k_ref[cols, :].astype(MXU_DTYPE)
                v_all = v_ref[cols, :]
                mask = (col_i + s0) < (row_i + r0) if masked else None
                for hh in range(per_pair):
                    z = lax.dot_general(qms[hh], kb, _NT, preferred_element_type=F32) * scale
                    sp = _softplus(z)
                    if masked:
                        sp = jnp.where(mask, sp, 0.0)
                    a = jnp.exp(z - (_running_sum_mm(sp, ge) + cs[hh]))
                    if masked:
                        a = jnp.where(mask, a, 0.0)
                    vm = (v_all * hms[hh]).astype(MXU_DTYPE)
                    acc = acc + jnp.dot(a.astype(MXU_DTYPE), vm, preferred_element_type=F32)
                    cs[hh] = cs[hh] + jnp.sum(sp, axis=-1, keepdims=True)
                return (acc, *cs)

            zero_col = jnp.zeros((SB_BQ, 1), F32)
            out = _sb_key_order(qi, tile, (jnp.zeros((SB_BQ, LANES), F32),) + (zero_col,) * per_pair, True)
            o_ref[rows, :] = out[0]
            for hh in range(per_pair):
                st_ref[hh, rows, :] = out[1 + hh]
            return carry

        lax.fori_loop(0, nq, q_loop, 0)

    return pl.pallas_call(
        body, name="sb_fwd",
        grid=(n_pairs,),
        in_specs=[pl.BlockSpec((t_len, LANES), lambda p: (0, p)),
                  pl.BlockSpec((t_len, LANES), lambda p: (0, n_pairs + p)),
                  pl.BlockSpec((t_len, LANES), lambda p: (0, 2 * n_pairs + p))],
        out_specs=[pl.BlockSpec((t_len, LANES), lambda p: (0, p)),
                   pl.BlockSpec((per_pair, t_len, 1), lambda p: (p, 0, 0))],
        out_shape=[jax.ShapeDtypeStruct((t_len, 512), F32),
                   jax.ShapeDtypeStruct((n_pairs * per_pair, t_len, 1), F32)],
        compiler_params=_params("arbitrary"),
    )(proj, proj, proj)


def _sb_bwd_call(proj, sp_total, d_o, t_len):
    nq = t_len // SB_BQ
    scale = float(SB_HEAD_DIM) ** -0.5
    n_pairs = 512 // LANES
    per_pair = LANES // SB_HEAD_DIM

    def body(q_ref, k_ref, v_ref, st_ref, do_ref, d_ref):
        lane = lax.broadcasted_iota(jnp.int32, (1, LANES), 1)
        row_i, col_i, sq_r, sq_c = _sb_iotas()
        lt = (sq_r < sq_c).astype(MXU_DTYPE)
        le = (sq_r <= sq_c).astype(MXU_DTYPE)
        hms = [((lane // SB_HEAD_DIM) == hh).astype(F32) for hh in range(per_pair)]
        d_ref[1] = jnp.zeros((t_len, LANES), F32)
        d_ref[2] = jnp.zeros((t_len, LANES), F32)

        def q_loop(qi, carry):
            r0 = pl.multiple_of(qi * SB_BQ, SB_BQ)
            rows = pl.ds(r0, SB_BQ)
            q_all, do_all = q_ref[rows, :], do_ref[rows, :]
            qms = [(q_all * hm).astype(MXU_DTYPE) for hm in hms]
            doms = [(do_all * hm).astype(MXU_DTYPE) for hm in hms]
            totals = [st_ref[hh, rows, :] for hh in range(per_pair)]

            def tile(kj, masked, kc):
                dq, cls, gls = kc[0], list(kc[1:1 + per_pair]), list(kc[1 + per_pair:])
                s0 = pl.multiple_of(kj * SB_BLOCK, SB_BLOCK)
                cols = pl.ds(s0, SB_BLOCK)
                k_all, v_all = k_ref[cols, :], v_ref[cols, :]
                kb = k_all.astype(MXU_DTYPE)
                mask = (col_i + s0) < (row_i + r0) if masked else None
                dk_t = jnp.zeros((SB_BLOCK, LANES), F32)
                dv_t = jnp.zeros((SB_BLOCK, LANES), F32)
                for hh in range(per_pair):
                    z = lax.dot_general(qms[hh], kb, _NT, preferred_element_type=F32) * scale
                    sp_all = _softplus(z)
                    sp = jnp.where(mask, sp_all, 0.0) if masked else sp_all
                    a = jnp.exp(z - (totals[hh] - cls[hh] - _running_sum_mm(sp, lt)))
                    if masked:
                        a = jnp.where(mask, a, 0.0)
                    vm = (v_all * hms[hh]).astype(MXU_DTYPE)
                    km = (k_all * hms[hh]).astype(MXU_DTYPE)
                    g = lax.dot_general(doms[hh], vm, _NT, preferred_element_type=F32) * a
                    dz = g - jnp.exp(z - sp_all) * (gls[hh] + _running_sum_mm(g, le))
                    if masked:
                        dz = jnp.where(mask, dz, 0.0)
                    dz = (dz * scale).astype(MXU_DTYPE)
                    dq = dq + jnp.dot(dz, km, preferred_element_type=F32)
                    dk_t = dk_t + lax.dot_general(dz, qms[hh], _TN, preferred_element_type=F32)
                    dv_t = dv_t + lax.dot_general(a.astype(MXU_DTYPE), doms[hh], _TN, preferred_element_type=F32)
                    cls[hh] = cls[hh] + jnp.sum(sp, axis=-1, keepdims=True)
                    gls[hh] = gls[hh] + jnp.sum(g, axis=-1, keepdims=True)
                d_ref[1, cols, :] += dk_t
                d_ref[2, cols, :] += dv_t
                return (dq, *cls, *gls)

            zero_col = jnp.zeros((SB_BQ, 1), F32)
            out = _sb_key_order(qi, tile, (jnp.zeros((SB_BQ, LANES), F32),) + (zero_col,) * (2 * per_pair), False)
            d_ref[0, rows, :] = out[0]
            return carry

        lax.fori_loop(0, nq, q_loop, 0)

    col = lambda off: pl.BlockSpec((t_len, LANES), lambda p: (0, off + p))
    return pl.pallas_call(
        body, name="sb_bwd",
        grid=(n_pairs,),
        in_specs=[col(0), col(n_pairs), col(2 * n_pairs),
                  pl.BlockSpec((per_pair, t_len, 1), lambda p: (p, 0, 0)), col(0)],
        out_specs=pl.BlockSpec((3, t_len, LANES), lambda p: (0, 0, p)),
        out_shape=jax.ShapeDtypeStruct((3, t_len, 512), F32),
        compiler_params=_params("arbitrary"),
    )(proj, proj, proj, sp_total, d_o)


def _conv_taps(xin, rows, t_len):
    taps = []
    for i in range(CONV_WIDTH):
        shift = CONV_WIDTH - 1 - i
        if shift == 0:
            taps.append(xin)
        else:
            taps.append(jnp.where(rows >= shift, pltpu.roll(xin, shift, axis=0), 0.0))
    return taps


def _gdn_prep_body_common(x_ref, w_ref, t_len):
    j = pl.program_id(0)
    xin = x_ref[...]
    rows = lax.broadcasted_iota(jnp.int32, (t_len, LANES), 0)
    taps = _conv_taps(xin, rows, t_len)
    pre = taps[0] * w_ref[0:1, :]
    for i in range(1, CONV_WIDTH):
        pre = pre + taps[i] * w_ref[i:i + 1, :]
    sg = _sigmoid(pre)
    act = pre * sg
    is_qk = j < 2 * GDN_HEADS
    nrm = jnp.where(is_qk, lax.rsqrt(jnp.sum(act * act, axis=-1, keepdims=True) + EPS), 1.0)
    sc = jnp.where(j < GDN_HEADS, float(GDN_HEAD_DIM) ** -0.5, 1.0)
    return j, rows, taps, pre, sg, act, is_qk, nrm, sc


def _gdn_prep_call(proj, conv_w, t_len):
    first = 2048 // LANES

    def body(x_ref, w_ref, out_ref):
        _, _, _, _, _, act, _, nrm, sc = _gdn_prep_body_common(x_ref, w_ref, t_len)
        out_ref[...] = act * nrm * sc

    return pl.pallas_call(
        body, name="gdn_prep",
        grid=(3 * GDN_HEADS,),
        in_specs=[pl.BlockSpec((t_len, LANES), lambda j: (0, first + j)),
                  pl.BlockSpec((CONV_WIDTH, LANES), lambda j: (0, j))],
        out_specs=pl.BlockSpec((t_len, LANES), lambda j: (0, j)),
        out_shape=jax.ShapeDtypeStruct((t_len, 3 * 512), F32),
        compiler_params=_params("arbitrary"),
    )(proj, conv_w)


def _gdn_prep_bwd_call(proj, conv_w, d_act3, t_len):
    first = 2048 // LANES

    def body(x_ref, w_ref, d_ref, dx_ref, dw_ref):
        _, rows, taps, pre, sg, act, is_qk, nrm, sc = _gdn_prep_body_common(x_ref, w_ref, t_len)
        d_out = d_ref[0]
        dn = d_out * sc
        d_norm = nrm * dn - act * (nrm * nrm * nrm) * jnp.sum(dn * act, axis=-1, keepdims=True)
        d_act = jnp.where(is_qk, d_norm, d_out)
        d_pre = d_act * sg * (1.0 + pre * (1.0 - sg))
        dx = d_pre * w_ref[CONV_WIDTH - 1:CONV_WIDTH, :]
        dw_ref[CONV_WIDTH - 1:CONV_WIDTH, :] = jnp.sum(d_pre * taps[CONV_WIDTH - 1], axis=0, keepdims=True)
        for i in range(CONV_WIDTH - 1):
            shift = CONV_WIDTH - 1 - i
            up = jnp.where(rows < t_len - shift, pltpu.roll(d_pre, t_len - shift, axis=0), 0.0)
            dx = dx + up * w_ref[i:i + 1, :]
            dw_ref[i:i + 1, :] = jnp.sum(d_pre * taps[i], axis=0, keepdims=True)
        dx_ref[0] = dx

    return pl.pallas_call(
        body, name="gdn_prep_bwd",
        grid=(3 * GDN_HEADS,),
        in_specs=[pl.BlockSpec((t_len, LANES), lambda j: (0, first + j)),
                  pl.BlockSpec((CONV_WIDTH, LANES), lambda j: (0, j)),
                  pl.BlockSpec((1, t_len, LANES), lambda j: (j // GDN_HEADS, 0, j % GDN_HEADS))],
        out_specs=[pl.BlockSpec((1, t_len, LANES), lambda j: (j // GDN_HEADS, 0, j % GDN_HEADS)),
                   pl.BlockSpec((CONV_WIDTH, LANES), lambda j: (0, j))],
        out_shape=[jax.ShapeDtypeStruct((3, t_len, 512), F32),
                   jax.ShapeDtypeStruct((CONV_WIDTH, 3 * 512), F32)],
        compiler_params=_params("arbitrary"),
    )(proj, conv_w, d_act3)


def _chunk_cumsum_matrix():
    r = lax.broadcasted_iota(jnp.int32, (LANES, LANES), 0)
    c = lax.broadcasted_iota(jnp.int32, (LANES, LANES), 1)
    return ((r <= c) & ((r // CHUNK) == (c // CHUNK))).astype(F32)


def _gdn_gates_call(ps, pst, alog_l, dtb_l, alog_c, dtb_c, t_len):
    def body(ps_ref, pst_ref, al_ref, dl_ref, ac_ref, dc_ref, beta_ref, gcol_ref, grow_ref):
        upper = _chunk_cumsum_matrix()
        lower = upper.T
        psv = ps_ref[...]
        beta_ref[...] = _sigmoid(psv)
        g_l = -jnp.exp(al_ref[...]) * _softplus(psv + dl_ref[...])
        g_r = -jnp.exp(ac_ref[...]) * _softplus(pst_ref[...] + dc_ref[...])
        for w in range(t_len // LANES):
            sl = slice(w * LANES, (w + 1) * LANES)
            gcol_ref[sl, :] = _mx(lower, g_l[sl, :])
            grow_ref[:, sl] = _mx(g_r[:, sl], upper)

    vm = pl.BlockSpec(memory_space=pltpu.VMEM)
    return pl.pallas_call(
        body, name="gdn_gates",
        in_specs=[vm] * 6, out_specs=[vm] * 3,
        out_shape=[jax.ShapeDtypeStruct((t_len, LANES), F32),
                   jax.ShapeDtypeStruct((t_len, LANES), F32),
                   jax.ShapeDtypeStruct((8, t_len), F32)],
        compiler_params=pltpu.CompilerParams(vmem_limit_bytes=VMEM_LIMIT_BYTES),
    )(ps, pst, alog_l, dtb_l, alog_c, dtb_c)


def _gdn_gates_bwd_call(ps, alog_l, dtb_l, d_l, t_len):
    def body(ps_ref, al_ref, dl_ref, d_ref, dps_ref, gal_ref, gdt_ref):
        lane = lax.broadcasted_iota(jnp.int32, (1, LANES), 1)
        psv = ps_ref[...]
        dv = d_ref[...]
        beta = _sigmoid(psv)
        ea = jnp.exp(al_ref[...])
        arg = psv + dl_ref[...]
        g = -ea * _softplus(arg)
        d_a = dv * (-ea) * _sigmoid(arg)
        is_a = (lane >= GDN_HEADS) & (lane < 2 * GDN_HEADS)
        dps_ref[...] = jnp.where(lane < GDN_HEADS, dv * beta * (1.0 - beta), jnp.where(is_a, d_a, 0.0))
        gdt_ref[...] = jnp.where(is_a, jnp.sum(d_a, axis=0, keepdims=True), 0.0)
        gal_ref[...] = jnp.where(is_a, jnp.sum(dv * g, axis=0, keepdims=True), 0.0)

    vm = pl.BlockSpec(memory_space=pltpu.VMEM)
    return pl.pallas_call(
        body, name="gdn_gates_bwd",
        in_specs=[vm] * 4, out_specs=[vm] * 3,
        out_shape=[jax.ShapeDtypeStruct((t_len, LANES), F32),
                   jax.ShapeDtypeStruct((1, LANES), F32),
                   jax.ShapeDtypeStruct((1, LANES), F32)],
        compiler_params=pltpu.CompilerParams(vmem_limit_bytes=VMEM_LIMIT_BYTES),
    )(ps, alog_l, dtb_l, d_l)


def _chunk_terms(q_ref, k_ref, v_ref, b_ref, gc_ref, gr_ref, i, j, incl, strict):
    r0 = pl.multiple_of(i * CHUNK, CHUNK)
    rows = pl.ds(r0, CHUNK)
    lanes = slice(j * GDN_HEAD_DIM, (j + 1) * GDN_HEAD_DIM)
    q, k, v = q_ref[rows, lanes], k_ref[rows, lanes], v_ref[rows, lanes]
    head = pl.program_id(0) * GDN_HEADS_PER_STEP + j
    lane_ids = lax.broadcasted_iota(jnp.int32, (1, LANES), 1)
    pick = lambda slab, lane: jnp.sum(jnp.where(lane_ids == lane, slab, 0.0), axis=-1, keepdims=True)
    b = pick(b_ref[rows, :], head)
    gc = pick(gc_ref[rows, :], GDN_HEADS + head)
    gr = gr_ref[j, i]
    dm = jnp.where(incl, jnp.exp(jnp.where(incl, gc - gr, 0.0)), 0.0)
    kb = k * b
    vb = v * b
    e = jnp.exp(gc)
    a = jnp.where(strict, _m3_nt(kb, k) * dm, 0.0)
    p = jnp.where(incl, _m3_nt(q, k) * dm, 0.0)
    gl = gc[CHUNK - 1:CHUNK, :]
    eg = jnp.exp(gl - gc)
    return rows, lanes, q, k, v, b, gc, dm, kb, vb, e, a, p, gl, eg


def _gdn_specs(t_len, n_chunks):
    hb = GDN_HEADS_PER_STEP
    groups = GDN_HEADS // hb
    col_spec = lambda part: pl.BlockSpec((t_len, hb * GDN_HEAD_DIM), lambda g: (0, part * groups + g))
    gate_c = pl.BlockSpec((t_len, LANES), lambda g: (0, 0))
    gate_r = pl.BlockSpec((hb, n_chunks, 1, CHUNK), lambda g: (g, 0, 0, 0))
    per_chunk = lambda r, c: pl.BlockSpec((hb, n_chunks, r, c), lambda g: (g, 0, 0, 0))
    return hb, groups, col_spec, gate_c, gate_r, per_chunk


def _gdn_fwd_call(gact, beta_c, gam_c, gam_r, t_len):
    n_chunks = t_len // CHUNK
    dk = GDN_HEAD_DIM
    hb, groups, col_spec, gate_c, gate_r, per_chunk = _gdn_specs(t_len, n_chunks)

    def body(q_ref, k_ref, v_ref, b_ref, gc_ref, gr_ref, o_ref, s_ref, t_ref):
        row = lax.broadcasted_iota(jnp.int32, (CHUNK, CHUNK), 0)
        col = lax.broadcasted_iota(jnp.int32, (CHUNK, CHUNK), 1)
        incl, strict = row >= col, row > col
        eye = (row == col).astype(F32)

        def chunk(i, states):
            new_states = []
            for j in range(hb):
                s = states[j]
                rows, lanes, q, k, v, b, gc, dm, kb, vb, e, a, p, gl, eg = _chunk_terms(
                    q_ref, k_ref, v_ref, b_ref, gc_ref, gr_ref, i, j, incl, strict)
                x = -a
                tm = eye + x
                xp = x
                for _ in range(5):
                    xp = _m3(xp, xp)
                    tm = tm + _m3(tm, xp)
                u = _m3(tm, vb)
                w = _m3(tm, kb * e)
                vn = u - _m3(w, s)
                o_ref[rows, lanes] = _m3(q * e, s) + _m3(p, vn)
                s_ref[j, i] = s
                t_ref[j, i] = tm
                new_states.append(s * jnp.exp(gl) + _m3_tn(k * eg, vn))
            return tuple(new_states)

        lax.fori_loop(0, n_chunks, chunk, (jnp.zeros((dk, dk), F32),) * hb, unroll=2)

    return pl.pallas_call(
        body, name="gdn_fwd",
        grid=(groups,),
        in_specs=[col_spec(0), col_spec(1), col_spec(2), gate_c, gate_c, gate_r],
        out_specs=[pl.BlockSpec((t_len, hb * dk), lambda g: (0, g)), per_chunk(dk, dk), per_chunk(CHUNK, CHUNK)],
        out_shape=[jax.ShapeDtypeStruct((t_len, 512), F32),
                   jax.ShapeDtypeStruct((GDN_HEADS, n_chunks, dk, dk), F32),
                   jax.ShapeDtypeStruct((GDN_HEADS, n_chunks, CHUNK, CHUNK), F32)],
        compiler_params=_params("arbitrary"),
    )(gact, gact, gact, beta_c, gam_c, gam_r)


def _gdn_bwd_call(gact, beta_c, gam_c, gam_r, s_all, t_all, d_o, t_len):
    n_chunks = t_len // CHUNK
    dk = GDN_HEAD_DIM
    hb, groups, col_spec, gate_c, gate_r, per_chunk = _gdn_specs(t_len, n_chunks)

    def body(q_ref, k_ref, v_ref, b_ref, gc_ref, gr_ref, s_ref, t_ref, do_ref, d_ref, dgate_ref):
        row = lax.broadcasted_iota(jnp.int32, (CHUNK, CHUNK), 0)
        col = lax.broadcasted_iota(jnp.int32, (CHUNK, CHUNK), 1)
        incl, strict = row >= col, row > col
        upper = (row <= col).astype(F32)
        ones = jnp.ones((CHUNK, LANES), F32)
        last_row = lax.broadcasted_iota(jnp.int32, (CHUNK, 1), 0) == CHUNK - 1
        lane_ids = lax.broadcasted_iota(jnp.int32, (1, LANES), 1)
        rsum = lambda m: jnp.sum(m, axis=-1, keepdims=True)

        @pl.when(pl.program_id(0) == 0)
        def _():
            dgate_ref[...] = jnp.zeros_like(dgate_ref)

        def chunk(step, d_states):
            i = n_chunks - 1 - step
            new_states = []
            for j in range(hb):
                ds = d_states[j]
                rows, lanes, q, k, v, b, gc, dm, kb, vb, e, a, p, gl, eg = _chunk_terms(
                    q_ref, k_ref, v_ref, b_ref, gc_ref, gr_ref, i, j, incl, strict)
                s = s_ref[j, i]
                tm = t_ref[j, i]
                d_out = do_ref[rows, lanes]
                el = jnp.exp(gl)
                kbe = kb * e
                u = _m3(tm, vb)
                w = _m3(tm, kbe)
                vn = u - _m3(w, s)
                qe = q * e
                kd = k * eg

                d_vn = _m3_tn(p, d_out) + _m3(kd, ds)
                d_qe = _m3_nt(d_out, s)
                d_p = jnp.where(incl, _m3_nt(d_out, vn), 0.0)
                new_states.append(el * ds + _m3_tn(qe, d_out) - _m3_tn(w, d_vn))
                d_kd = _m3_nt(vn, ds)
                d_w = -_m3_nt(d_vn, s)
                d_vb = _m3_tn(tm, d_vn)
                d_kbe = _m3_tn(tm, d_w)
                d_a = -jnp.where(strict, _m3_nt(d_vb, u) + _m3_nt(d_kbe, w), 0.0)
                m = d_a * dm
                n = d_p * dm
                d_kb = _m3(m, k) + d_kbe * e
                d_q = _m3(n, k) + d_qe * e
                d_k = _m3_tn(m, kb) + _m3_tn(n, q) + d_kd * eg + b * d_kb
                r = d_a * a + d_p * p
                kd_term = rsum(d_kd * kd)
                d_gl = jnp.sum(ds * s) * el + jnp.sum(kd_term)
                d_gam = (rsum(r) - _mx_tn(r, ones)[:, 0:1] + rsum(d_qe * qe) + rsum(d_kbe * kbe) - kd_term
                         + jnp.where(last_row, d_gl, 0.0))
                d_ref[0, rows, lanes] = d_q
                d_ref[1, rows, lanes] = d_k
                d_ref[2, rows, lanes] = b * d_vb
                head = pl.program_id(0) * hb + j
                d_beta = rsum(d_kb * k) + rsum(d_vb * v)
                d_g = _mx(upper, d_gam * ones)[:, 0:1]
                dgate_ref[rows, :] += (jnp.where(lane_ids == head, d_beta, 0.0)
                                       + jnp.where(lane_ids == GDN_HEADS + head, d_g, 0.0))
            return tuple(new_states)

        lax.fori_loop(0, n_chunks, chunk, (jnp.zeros((dk, dk), F32),) * hb)

    return pl.pallas_call(
        body, name="gdn_bwd",
        grid=(groups,),
        in_specs=[col_spec(0), col_spec(1), col_spec(2), gate_c, gate_c, gate_r,
                  per_chunk(dk, dk), per_chunk(CHUNK, CHUNK),
                  pl.BlockSpec((t_len, hb * dk), lambda g: (0, g))],
        out_specs=[pl.BlockSpec((3, t_len, hb * dk), lambda g: (0, 0, g)), gate_c],
        out_shape=[jax.ShapeDtypeStruct((3, t_len, 512), F32),
                   jax.ShapeDtypeStruct((t_len, LANES), F32)],
        compiler_params=_params("arbitrary"),
    )(gact, gact, gact, beta_c, gam_c, gam_r, s_all, t_all, d_o)


def _group_matrix(width, group):
    r = lax.broadcasted_iota(jnp.int32, (width, width), 0)
    c = lax.broadcasted_iota(jnp.int32, (width, width), 1)
    return ((r // group) == (c // group)).astype(F32)


def _post_call(o_sb, o_gd, proj, x, target, w_out, sbw, gdw, fw, tm=256):
    t_len, d = x.shape
    half = 512
    zsb_blk = 1536 // half
    zgd_blk = 3584 // half

    def body(osb_ref, ogd_ref, zsb_ref, zgd_ref, x_ref, tg_ref, wo_ref, sbw_ref, gdw_ref, fw_ref,
             dx2_ref, dosb_ref, dzsb_ref, dogd_ref, dzgd_ref, loss_ref, gfw_ref, gsb_ref, ggd_ref, gwo_ref):
        step = pl.program_id(0)

        @pl.when(step == 0)
        def _():
            loss_ref[...] = jnp.zeros_like(loss_ref)
            gfw_ref[...] = jnp.zeros_like(gfw_ref)
            gsb_ref[...] = jnp.zeros_like(gsb_ref)
            ggd_ref[...] = jnp.zeros_like(ggd_ref)
            gwo_ref[...] = jnp.zeros_like(gwo_ref)

        def head_forward(o, z, w, gmat, inv):
            r = lax.rsqrt(_mx(o * o, gmat) * inv + EPS)
            nrm = o * r * w
            sg = _sigmoid(z)
            return r, nrm, sg, nrm * (z * sg)

        def head_backward(d_m, o, z, w, gmat, inv, r, nrm, sg):
            d_n = d_m * (z * sg)
            d_z = d_m * nrm * (sg * (1.0 + z * (1.0 - sg)))
            dnw = d_n * w
            d_o = r * dnw - o * (r * r * r) * (_mx(dnw * o, gmat) * inv)
            return d_o, d_z, jnp.sum(d_n * o * r, axis=0, keepdims=True)

        g_sb = _group_matrix(half, SB_HEAD_DIM)
        g_gd = _group_matrix(half, GDN_HEAD_DIM)
        osb, ogd, zsb, zgd = osb_ref[...], ogd_ref[...], zsb_ref[...], zgd_ref[...]
        sbw_v, gdw_v = sbw_ref[...], gdw_ref[...]
        r_sb, n_sb, sg_sb, m_sb = head_forward(osb, zsb, sbw_v, g_sb, 1.0 / SB_HEAD_DIM)
        r_gd, n_gd, sg_gd, m_gd = head_forward(ogd, zgd, gdw_v, g_gd, 1.0 / GDN_HEAD_DIM)
        mixed = jnp.concatenate([m_sb, m_gd], axis=1).astype(MXU_DTYPE)
        wo = wo_ref[...]
        x2 = x_ref[...] + jnp.dot(mixed, wo, preferred_element_type=F32)
        r2 = lax.rsqrt(jnp.mean(x2 * x2, axis=-1, keepdims=True) + EPS)
        fw_v = fw_ref[...]
        err = x2 * r2 * fw_v - tg_ref[...]
        loss_ref[...] += 0.5 * jnp.sum(jnp.sum(err * err, axis=-1, keepdims=True) * (1.0 / d))
        dy = err * (1.0 / d)
        gg = dy * fw_v
        dx2 = r2 * gg - x2 * ((r2 * r2 * r2) * jnp.mean(gg * x2, axis=-1, keepdims=True))
        gfw_ref[...] += jnp.sum(dy * x2 * r2, axis=0, keepdims=True)
        dx2_ref[...] = dx2
        dx2b = dx2.astype(MXU_DTYPE)
        d_mixed = lax.dot_general(dx2b, wo, _NT, preferred_element_type=F32)
        gwo_ref[...] += lax.dot_general(mixed, dx2b, _TN, preferred_element_type=F32)
        d_osb, d_zsb, gsb = head_backward(d_mixed[:, :half], osb, zsb, sbw_v, g_sb, 1.0 / SB_HEAD_DIM, r_sb, n_sb, sg_sb)
        d_ogd, d_zgd, ggd = head_backward(d_mixed[:, half:], ogd, zgd, gdw_v, g_gd, 1.0 / GDN_HEAD_DIM, r_gd, n_gd, sg_gd)
        dosb_ref[...] = d_osb
        dzsb_ref[...] = d_zsb
        dogd_ref[...] = d_ogd
        dzgd_ref[...] = d_zgd
        gsb_ref[...] += gsb
        ggd_ref[...] += ggd

    row_blk = lambda w: pl.BlockSpec((tm, w), lambda i: (i, 0))
    fixed = lambda r, w: pl.BlockSpec((r, w), lambda i: (0, 0))
    return pl.pallas_call(
        body, name="post",
        grid=(t_len // tm,),
        in_specs=[row_blk(half), row_blk(half),
                  pl.BlockSpec((tm, half), lambda i: (i, zsb_blk)),
                  pl.BlockSpec((tm, half), lambda i: (i, zgd_blk)),
                  row_blk(d), row_blk(d), fixed(d, d), fixed(1, half), fixed(1, half), fixed(1, d)],
        out_specs=[row_blk(d), row_blk(half), row_blk(half), row_blk(half), row_blk(half),
                   fixed(1, LANES), fixed(1, d), fixed(1, half), fixed(1, half), fixed(d, d)],
        out_shape=[jax.ShapeDtypeStruct((t_len, d), F32)] + [jax.ShapeDtypeStruct((t_len, half), F32)] * 4
                  + [jax.ShapeDtypeStruct((1, LANES), F32), jax.ShapeDtypeStruct((1, d), F32),
                     jax.ShapeDtypeStruct((1, half), F32), jax.ShapeDtypeStruct((1, half), F32),
                     jax.ShapeDtypeStruct((d, d), F32)],
        compiler_params=_params("arbitrary"),
    )(o_sb, o_gd, proj, proj, x, target, w_out, sbw, gdw, fw)


def _gw_in_call(h_t, dproj8, tm=512):
    d, t_len = h_t.shape
    n_piece, _, pw = dproj8.shape

    def body(ht_ref, dp_ref, gw_ref):
        @pl.when(pl.program_id(1) == 0)
        def _():
            gw_ref[...] = jnp.zeros_like(gw_ref)

        gw_ref[...] += jnp.dot(ht_ref[...], dp_ref[0].astype(MXU_DTYPE), preferred_element_type=F32)

    return pl.pallas_call(
        body, name="gw_in",
        grid=(n_piece, t_len // tm),
        in_specs=[pl.BlockSpec((d, tm), lambda p, t: (0, t)),
                  pl.BlockSpec((1, tm, pw), lambda p, t: (p, t, 0))],
        out_specs=pl.BlockSpec((d, pw), lambda p, t: (0, p)),
        out_shape=jax.ShapeDtypeStruct((d, n_piece * pw), F32),
        compiler_params=_params("arbitrary", "arbitrary"),
    )(h_t, dproj8)


def _gw_small_call(h_t, dsmall, tm=512):
    d, t_len = h_t.shape
    ns = dsmall.shape[1]

    def body(ht_ref, dp_ref, gw_ref):
        @pl.when(pl.program_id(0) == 0)
        def _():
            gw_ref[...] = jnp.zeros_like(gw_ref)

        gw_ref[...] += jnp.dot(ht_ref[...], dp_ref[...].astype(MXU_DTYPE), preferred_element_type=F32)

    return pl.pallas_call(
        body, name="gw_small",
        grid=(t_len // tm,),
        in_specs=[pl.BlockSpec((d, tm), lambda t: (0, t)),
                  pl.BlockSpec((tm, ns), lambda t: (t, 0))],
        out_specs=pl.BlockSpec((d, ns), lambda t: (0, 0)),
        out_shape=jax.ShapeDtypeStruct((d, ns), F32),
        compiler_params=_params("arbitrary"),
    )(h_t, dsmall)


def _dx_call(dproj8, dsmall, w_main, w_small, x, r, dx2, norm_w, tm=256):
    t_len, d = x.shape
    n_piece, _, pw = dproj8.shape
    ns = dsmall.shape[1]

    def body(dp_ref, ds_ref, wm_ref, ws_ref, x_ref, r_ref, dx2_ref, nw_ref, gx_ref, gnw_ref):
        @pl.when(pl.program_id(0) == 0)
        def _():
            gnw_ref[...] = jnp.zeros_like(gnw_ref)

        dh = lax.dot_general(ds_ref[...].astype(MXU_DTYPE), ws_ref[...], _NT, preferred_element_type=F32)
        for p in range(n_piece):
            dh = dh + lax.dot_general(dp_ref[p].astype(MXU_DTYPE), wm_ref[:, p * pw:(p + 1) * pw], _NT,
                                      preferred_element_type=F32)
        xv, rv = x_ref[...], r_ref[...]
        dn = dh * nw_ref[...]
        gx_ref[...] = dx2_ref[...] + rv * dn - xv * ((rv * rv * rv) * jnp.mean(dn * xv, axis=-1, keepdims=True))
        gnw_ref[...] += jnp.sum(dh * xv * rv, axis=0, keepdims=True)

    return pl.pallas_call(
        body, name="dx",
        grid=(t_len // tm,),
        in_specs=[pl.BlockSpec((n_piece, tm, pw), lambda i: (0, i, 0)),
                  pl.BlockSpec((tm, ns), lambda i: (i, 0)),
                  pl.BlockSpec((d, n_piece * pw), lambda i: (0, 0)),
                  pl.BlockSpec((d, ns), lambda i: (0, 0)),
                  pl.BlockSpec((tm, d), lambda i: (i, 0)),
                  pl.BlockSpec((tm, 1), lambda i: (i, 0)),
                  pl.BlockSpec((tm, d), lambda i: (i, 0)),
                  pl.BlockSpec((1, d), lambda i: (0, 0))],
        out_specs=[pl.BlockSpec((tm, d), lambda i: (i, 0)),
                   pl.BlockSpec((1, d), lambda i: (0, 0))],
        out_shape=[jax.ShapeDtypeStruct((t_len, d), F32), jax.ShapeDtypeStruct((1, d), F32)],
        compiler_params=_params("arbitrary"),
    )(dproj8, dsmall, w_main, w_small, x, r, dx2, norm_w)


def _exchange_call(name, srcs, per_peer):
    n = len(srcs)
    out_shapes = [jax.ShapeDtypeStruct(s.shape if pp else (N_DEV,) + s.shape, s.dtype) for s, pp in zip(srcs, per_peer)]

    def body(*refs):
        src_refs, out_refs = refs[:n], refs[n:2 * n]
        send_sems, recv_sems, local_sems = refs[2 * n:]
        x, y, c = lax.axis_index("x"), lax.axis_index("y"), lax.axis_index("c")
        me = 4 * x + 2 * y + c
        copies = []
        for a in range(n):
            mine = src_refs[a].at[me] if per_peer[a] else src_refs[a]
            local = pltpu.make_async_copy(mine, out_refs[a].at[me], local_sems.at[a])
            local.start()
            copies.append(local)
        remote = []
        for k in range(1, N_DEV):
            kx, ky, kc = (k >> 2) & 1, (k >> 1) & 1, k & 1
            px = 1 - x if kx else x
            py = 1 - y if ky else y
            pc = 1 - c if kc else c
            peer = 4 * px + 2 * py + pc
            for a in range(n):
                sem = a * (N_DEV - 1) + (k - 1)
                src = src_refs[a].at[peer] if per_peer[a] else src_refs[a]
                cp = pltpu.make_async_remote_copy(
                    src_ref=src, dst_ref=out_refs[a].at[me],
                    send_sem=send_sems.at[sem], recv_sem=recv_sems.at[sem],
                    device_id=(px, py, pc), device_id_type=pl.DeviceIdType.MESH)
                cp.start()
                remote.append(cp)
        for cp in remote:
            cp.wait_send()
        for cp in remote:
            cp.wait_recv()
        for cp in copies:
            cp.wait()

    hbm = pl.BlockSpec(memory_space=pl.ANY)
    return pl.pallas_call(
        body, name=name,
        in_specs=[hbm] * n, out_specs=[hbm] * n, out_shape=out_shapes,
        scratch_shapes=[pltpu.SemaphoreType.DMA((n * (N_DEV - 1),)),
                        pltpu.SemaphoreType.DMA((n * (N_DEV - 1),)),
                        pltpu.SemaphoreType.DMA((n,))],
    )(*srcs)


N_CHIPS = 4
_HBM = pl.BlockSpec(memory_space=pl.ANY)
_MESH = pl.DeviceIdType.MESH


def _gather_call(name, srcs):
    n = len(srcs)
    per = N_DEV - 1

    def body(*refs):
        src_refs, out_refs = refs[:n], refs[n:2 * n]
        send_sems, recv_sems, local_sems = refs[2 * n:]
        x, y, c = lax.axis_index("x"), lax.axis_index("y"), lax.axis_index("c")
        me, sibling = (x, y, c), (x, y, 1 - c)
        chips = [(1 - x, y), (x, 1 - y), (1 - x, 1 - y)]
        slot = lambda px, py, pc: 4 * px + 2 * py + pc

        def copy(a, k, block, to, from_src=False):
            rows = out_refs[a].at[slot(*block)]
            return pltpu.make_async_remote_copy(
                src_ref=src_refs[a] if from_src else rows, dst_ref=rows,
                send_sem=send_sems.at[a * per + k], recv_sem=recv_sems.at[a * per + k],
                device_id=to, device_id_type=_MESH)

        local = [pltpu.make_async_copy(src_refs[a], out_refs[a].at[slot(*me)], local_sems.at[a]) for a in range(n)]
        for cp in local:
            cp.start()
        first = []
        for a in range(n):
            first.append(copy(a, 0, me, sibling, True))
            first += [copy(a, 1 + j, me, (*chip, c), True) for j, chip in enumerate(chips)]
        for cp in first:
            cp.start()
        passed = []
        for j, chip in enumerate(chips):
            for a in range(n):
                copy(a, 1 + j, (*chip, c), me).wait_recv()
                fwd = copy(a, 4 + j, (*chip, c), sibling)
                fwd.start()
                passed.append(fwd)
        for a in range(n):
            copy(a, 0, sibling, me).wait_recv()
            for j, chip in enumerate(chips):
                copy(a, 4 + j, (*chip, 1 - c), me).wait_recv()
        for cp in first + passed:
            cp.wait_send()
        for cp in local:
            cp.wait()

    return pl.pallas_call(
        body, name=name,
        in_specs=[_HBM] * n, out_specs=[_HBM] * n,
        out_shape=[jax.ShapeDtypeStruct((N_DEV,) + s.shape, s.dtype) for s in srcs],
        scratch_shapes=[pltpu.SemaphoreType.DMA((n * per,)), pltpu.SemaphoreType.DMA((n * per,)),
                        pltpu.SemaphoreType.DMA((n,))],
    )(*srcs)


def _sibling_send_call(name, srcs):
    n = len(srcs)

    def body(*refs):
        src_refs, out_refs = refs[:n], refs[n:2 * n]
        send_sems, recv_sems = refs[2 * n:]
        x, y, c = lax.axis_index("x"), lax.axis_index("y"), lax.axis_index("c")
        copies = []
        for a in range(n):
            for ch in range(N_CHIPS):
                copies.append(pltpu.make_async_remote_copy(
                    src_ref=src_refs[a].at[2 * ch + (1 - c)], dst_ref=out_refs[a].at[ch],
                    send_sem=send_sems.at[a * N_CHIPS + ch], recv_sem=recv_sems.at[a * N_CHIPS + ch],
                    device_id=(x, y, 1 - c), device_id_type=_MESH))
        for cp in copies:
            cp.start()
        for cp in copies:
            cp.wait_send()
        for cp in copies:
            cp.wait_recv()

    return pl.pallas_call(
        body, name=name,
        in_specs=[_HBM] * n, out_specs=[_HBM] * n,
        out_shape=[jax.ShapeDtypeStruct((N_CHIPS,) + s.shape[1:], s.dtype) for s in srcs],
        scratch_shapes=[pltpu.SemaphoreType.DMA((n * N_CHIPS,)), pltpu.SemaphoreType.DMA((n * N_CHIPS,))],
    )(*srcs)


def _pair_sum_call(name, parts, from_sibling, tr):
    _, rows, cols = parts.shape

    def body(p_ref, s_ref, o_ref):
        o_ref[...] = (p_ref[...] + s_ref[...]).astype(o_ref.dtype)

    return pl.pallas_call(
        body, name=name,
        grid=(N_CHIPS, rows // tr),
        in_specs=[pl.BlockSpec((1, tr, cols), lambda ch, i: (2 * ch + lax.axis_index("c"), i, 0)),
                  pl.BlockSpec((1, tr, cols), lambda ch, i: (ch, i, 0))],
        out_specs=pl.BlockSpec((1, tr, cols), lambda ch, i: (ch, i, 0)),
        out_shape=jax.ShapeDtypeStruct((N_CHIPS, rows, cols), WIRE_DTYPE),
        compiler_params=_params("arbitrary", "arbitrary"),
    )(parts, from_sibling)


def _chip_exchange_call(name, srcs):
    n = len(srcs)
    per = N_CHIPS - 1

    def body(*refs):
        src_refs, out_refs = refs[:n], refs[n:2 * n]
        send_sems, recv_sems, local_sems = refs[2 * n:]
        x, y, c = lax.axis_index("x"), lax.axis_index("y"), lax.axis_index("c")
        mine = 2 * x + y
        chips = [(1 - x, y), (x, 1 - y), (1 - x, 1 - y)]
        local = [pltpu.make_async_copy(src_refs[a].at[mine], out_refs[a].at[mine], local_sems.at[a]) for a in range(n)]
        for cp in local:
            cp.start()
        remote = []
        for a in range(n):
            for j, (px, py) in enumerate(chips):
                remote.append(pltpu.make_async_remote_copy(
                    src_ref=src_refs[a].at[2 * px + py], dst_ref=out_refs[a].at[mine],
                    send_sem=send_sems.at[a * per + j], recv_sem=recv_sems.at[a * per + j],
                    device_id=(px, py, c), device_id_type=_MESH))
        for cp in remote:
            cp.start()
        for cp in remote:
            cp.wait_send()
        for cp in remote:
            cp.wait_recv()
        for cp in local:
            cp.wait()

    return pl.pallas_call(
        body, name=name,
        in_specs=[_HBM] * n, out_specs=[_HBM] * n,
        out_shape=[jax.ShapeDtypeStruct(s.shape, s.dtype) for s in srcs],
        scratch_shapes=[pltpu.SemaphoreType.DMA((n * per,)), pltpu.SemaphoreType.DMA((n * per,)),
                        pltpu.SemaphoreType.DMA((n,))],
    )(*srcs)


def _adam_call(name, parts, w, m, v, tr):
    rows, cols = w.shape
    n_slots = parts.shape[0]

    def body(p_ref, w_ref, m_ref, v_ref, g_ref, d_ref, nm_ref, nv_ref):
        g = p_ref[0].astype(F32)
        for s in range(1, n_slots):
            g = g + p_ref[s].astype(F32)
        m_new = ADAM_B1 * m_ref[...] + (1.0 - ADAM_B1) * g
        v_new = ADAM_B2 * v_ref[...] + (1.0 - ADAM_B2) * (g * g)
        m_hat = m_new / (1.0 - ADAM_B1 ** ADAM_STEP)
        v_hat = v_new / (1.0 - ADAM_B2 ** ADAM_STEP)
        g_ref[...] = g
        d_ref[...] = -ADAM_LR * (m_hat / (jnp.sqrt(v_hat) + ADAM_EPS) + ADAM_WD * w_ref[...])
        nm_ref[...] = m_new
        nv_ref[...] = v_new

    blk = pl.BlockSpec((tr, cols), lambda i: (i, 0))
    return pl.pallas_call(
        body, name=name,
        grid=(rows // tr,),
        in_specs=[pl.BlockSpec((n_slots, tr, cols), lambda i: (0, i, 0)), blk, blk, blk],
        out_specs=[blk] * 4,
        out_shape=[jax.ShapeDtypeStruct((rows, cols), F32)] * 4,
        compiler_params=_params("arbitrary"),
    )(parts, w, m, v)


_SMALL_ROWS = ("norm1_w", "final_norm_w", "sb_norm_w", "gdn_norm_w", "gdn_A_log", "gdn_dt_bias", "loss")


def _pack_small(vals, width):
    rows = [jnp.pad(a.reshape(1, -1).astype(F32), ((0, 0), (0, width - a.size))) for a in vals]
    rows += [jnp.zeros((1, width), F32)] * (8 - len(rows))
    return jnp.concatenate(rows, axis=0)


def _device_step(x2d, tgt, w_full, w_out_f32, conv_full, norm1_w, sb_norm_w, gdn_A_log, gdn_dt_bias, gdn_norm_w,
                 final_norm_w):
    t_len, d = x2d.shape
    n_chunks = t_len // CHUNK
    n_main = 8 * 512
    n_small = w_full.shape[1] - n_main
    w_main = w_full[:, :n_main].astype(MXU_DTYPE)
    w_small = jnp.pad(w_full[:, n_main:], ((0, 0), (0, LANES - n_small))).astype(MXU_DTYPE)
    w_small_t = w_full[:, n_main:].T.astype(MXU_DTYPE)
    w_out_full = w_out_f32.astype(MXU_DTYPE)

    pad_lanes = lambda a, lo: jnp.pad(a.reshape(1, -1), ((0, 0), (lo, LANES - lo - a.size)))
    alog_l, dtb_l = pad_lanes(gdn_A_log, GDN_HEADS), pad_lanes(gdn_dt_bias, GDN_HEADS)
    alog_c, dtb_c = alog_l[:, :8].T, dtb_l[:, :8].T
    sbw = jnp.tile(sb_norm_w, (1, 512 // SB_HEAD_DIM))
    gdw = jnp.tile(gdn_norm_w, (1, 512 // GDN_HEAD_DIM))
    fw = final_norm_w.reshape(1, d)

    proj, ps, pst, h_t, r1 = _inproj_call(x2d, norm1_w, w_main, w_small, w_small_t)
    o_sb, sp_total = _sb_fwd_call(proj, t_len)
    gact = _gdn_prep_call(proj, conv_full, t_len)
    beta_l, gcol_l, grow = _gdn_gates_call(ps, pst, alog_l, dtb_l, alog_c, dtb_c, t_len)
    gam_r = grow[GDN_HEADS:2 * GDN_HEADS].reshape(GDN_HEADS, n_chunks, 1, CHUNK)
    o_gd, s_all, t_all = _gdn_fwd_call(gact, beta_l, gcol_l, gam_r, t_len)

    (dx2, d_osb, d_zsb, d_ogd, d_zgd, loss_p, g_fw, g_sbw, g_gdw, g_wout) = _post_call(
        o_sb, o_gd, proj, x2d, tgt, w_out_full, sbw, gdw, fw)

    d_sb3 = _sb_bwd_call(proj, sp_total, d_osb, t_len)
    d_gact3, d_gates = _gdn_bwd_call(gact, beta_l, gcol_l, gam_r, s_all, t_all, d_ogd, t_len)
    d_gd3, g_conv = _gdn_prep_bwd_call(proj, conv_full, d_gact3, t_len)
    dsmall, g_alog, g_dtb = _gdn_gates_bwd_call(ps, alog_l, dtb_l, d_gates, t_len)

    dproj8 = jnp.concatenate([d_sb3, d_zsb[None], d_gd3, d_zgd[None]], axis=0)
    g_w_main = _gw_in_call(h_t, dproj8)
    g_w_small = _gw_small_call(h_t, dsmall)
    grad_x, g_n1 = _dx_call(dproj8, dsmall, w_main, w_small, x2d, r1, dx2, norm1_w)
    g_w_in_full = jnp.concatenate([g_w_main, g_w_small[:, :n_small]], axis=1)
    return (loss_p, grad_x, g_n1, g_w_in_full, g_sbw, g_conv, g_alog, g_dtb, g_gdw, g_wout, g_fw)


def kernel(x, norm1_w, w_in, sb_norm_w, gdn_conv_w, gdn_A_log, gdn_dt_bias, gdn_norm_w, w_out, final_norm_w, loss_target, m_norm1_w, m_w_in, m_sb_norm_w, m_gdn_conv_w, m_gdn_A_log, m_gdn_dt_bias, m_gdn_norm_w, m_w_out, m_final_norm_w, v_norm1_w, v_w_in, v_sb_norm_w, v_gdn_conv_w, v_gdn_A_log, v_gdn_dt_bias, v_gdn_norm_w, v_w_out, v_final_norm_w):
    d = x.shape[2]
    shard_cols = w_in.shape[2]
    conv_cols = gdn_conv_w.shape[2]

    w_in_g, w_out_g, conv_g = _gather_call(
        "gather_weights", [w_in[0].astype(WIRE_DTYPE), w_out[0].astype(WIRE_DTYPE), gdn_conv_w[0]])
    w_full = w_in_g.transpose(1, 0, 2).reshape(d, N_DEV * shard_cols)
    conv_full = conv_g.transpose(1, 0, 2).reshape(CONV_WIDTH, N_DEV * conv_cols)

    (loss_p, grad_x, g_n1, g_w_in_full, g_sbw, g_conv, g_alog, g_dtb, g_gdw, g_wout, g_fw) = _device_step(
        x[0], loss_target[0], w_full, w_out_g.reshape(d, d), conv_full, norm1_w, sb_norm_w, gdn_A_log, gdn_dt_bias,
        gdn_norm_w, final_norm_w)

    g_w_in_parts = g_w_in_full.reshape(d, N_DEV, shard_cols).transpose(1, 0, 2)
    g_wout_parts = g_wout.reshape(N_DEV, d // N_DEV, d)
    g_conv_parts = g_conv.reshape(CONV_WIDTH, N_DEV, conv_cols).transpose(1, 0, 2)
    fold = lambda a, group: a.reshape(-1, group).sum(axis=0)
    small_g = _pack_small([g_n1, g_fw, fold(g_sbw, SB_HEAD_DIM), fold(g_gdw, GDN_HEAD_DIM),
                           g_alog[0, GDN_HEADS:2 * GDN_HEADS], g_dtb[0, GDN_HEADS:2 * GDN_HEADS],
                           loss_p[0, :1]], d)
    sib_w_in, sib_wout, sib_conv = _sibling_send_call("grads_to_sibling", [g_w_in_parts, g_wout_parts, g_conv_parts])
    c_w_in = _pair_sum_call("pair_sum_w_in", g_w_in_parts, sib_w_in, 256)
    c_wout = _pair_sum_call("pair_sum_w_out", g_wout_parts, sib_wout, d // N_DEV)
    c_conv = _pair_sum_call("pair_sum_conv", g_conv_parts, sib_conv, CONV_WIDTH)
    p_w_in, p_wout, p_conv = _chip_exchange_call("grads_to_chips", [c_w_in, c_wout, c_conv])
    (p_small,) = _exchange_call("exchange_small", [small_g], [False])

    small_w = _pack_small([norm1_w, final_norm_w, sb_norm_w, gdn_norm_w, gdn_A_log, gdn_dt_bias], d)
    small_m = _pack_small([m_norm1_w, m_final_norm_w, m_sb_norm_w, m_gdn_norm_w, m_gdn_A_log, m_gdn_dt_bias], d)
    small_v = _pack_small([v_norm1_w, v_final_norm_w, v_sb_norm_w, v_gdn_norm_w, v_gdn_A_log, v_gdn_dt_bias], d)

    r_w_in = _adam_call("adam_w_in", p_w_in, w_in[0], m_w_in[0], v_w_in[0], 256)
    r_wout = _adam_call("adam_w_out", p_wout, w_out[0], m_w_out[0], v_w_out[0], d // N_DEV)
    r_conv = _adam_call("adam_conv", p_conv, gdn_conv_w[0], m_gdn_conv_w[0], v_gdn_conv_w[0], CONV_WIDTH)
    r_small = _adam_call("adam_small", p_small, small_w, small_m, small_v, 8)

    shapes = {"norm1_w": norm1_w.shape, "final_norm_w": final_norm_w.shape, "sb_norm_w": sb_norm_w.shape,
              "gdn_norm_w": gdn_norm_w.shape, "gdn_A_log": gdn_A_log.shape, "gdn_dt_bias": gdn_dt_bias.shape}

    def small_out(kind, name):
        row = _SMALL_ROWS.index(name)
        shp = shapes[name]
        size = 1
        for s in shp:
            size *= s
        return r_small[kind][row, :size].reshape(shp)

    def outputs(kind):
        return (small_out(kind, "norm1_w"), r_w_in[kind][None], small_out(kind, "sb_norm_w"), r_conv[kind][None],
                small_out(kind, "gdn_A_log"), small_out(kind, "gdn_dt_bias"), small_out(kind, "gdn_norm_w"),
                r_wout[kind][None], small_out(kind, "final_norm_w"))

    loss = r_small[0][_SMALL_ROWS.index("loss"), 0]
    return (loss, grad_x[None], *outputs(0), *outputs(1), *outputs(2), *outputs(3))
```

```python
import functools

import jax
import jax.numpy as jnp
from jax import lax
from jax.experimental import pallas as pl
from jax.experimental.pallas import tpu as pltpu

F32 = jnp.float32
MXU_DTYPE = jnp.bfloat16
WIRE_DTYPE = jnp.bfloat16
EXACT = lax.Precision.HIGHEST
EPS = 1e-6
N_DEV = 8
SB_HEAD_DIM = 64
GDN_HEAD_DIM = 128
GDN_HEADS = 4
GDN_HEADS_PER_STEP = 2
CHUNK = 64
CONV_WIDTH = 4
LANES = 128
SB_BLOCK = 128
SB_BQ = 256
VMEM_LIMIT_BYTES = 56 * 1024 * 1024

ADAM_LR = 0.001
ADAM_B1 = 0.9
ADAM_B2 = 0.999
ADAM_EPS = 1e-08
ADAM_WD = 0.01
ADAM_STEP = 10

_NN = (((1,), (0,)), ((), ()))
_NT = (((1,), (1,)), ((), ()))
_TN = (((0,), (0,)), ((), ()))


def _mm(a, b):
    return jnp.dot(a.astype(MXU_DTYPE), b.astype(MXU_DTYPE), preferred_element_type=F32)


def _mm_nt(a, b):
    return lax.dot_general(a.astype(MXU_DTYPE), b.astype(MXU_DTYPE), _NT, preferred_element_type=F32)


def _mm_tn(a, b):
    return lax.dot_general(a.astype(MXU_DTYPE), b.astype(MXU_DTYPE), _TN, preferred_element_type=F32)


def _mx(a, b):
    return jnp.dot(a, b, precision=EXACT, preferred_element_type=F32)


def _mx_nt(a, b):
    return lax.dot_general(a, b, _NT, precision=EXACT, preferred_element_type=F32)


def _mx_tn(a, b):
    return lax.dot_general(a, b, _TN, precision=EXACT, preferred_element_type=F32)


def _split(x):
    hi = x.astype(MXU_DTYPE)
    return hi, (x - hi.astype(F32)).astype(MXU_DTYPE)


def _m3_general(a, b, dims):
    ah, al = _split(a)
    bh, bl = _split(b)
    dot = lambda x, y: lax.dot_general(x, y, dims, preferred_element_type=F32)
    return dot(ah, bh) + (dot(ah, bl) + dot(al, bh))


def _m3(a, b):
    return _m3_general(a, b, _NN)


def _m3_nt(a, b):
    return _m3_general(a, b, _NT)


def _m3_tn(a, b):
    return _m3_general(a, b, _TN)


def _sigmoid(z):
    return 1.0 / (1.0 + jnp.exp(-z))


def _softplus(z):
    return jnp.maximum(z, 0.0) + jnp.log(1.0 + jnp.exp(-jnp.abs(z)))


def _params(*semantics):
    return pltpu.CompilerParams(dimension_semantics=semantics, vmem_limit_bytes=VMEM_LIMIT_BYTES)


def _inproj_call(x, norm_w, w_main, w_small, w_small_t, tm=256):
    t_len, d = x.shape
    n = w_main.shape[1]
    ns = w_small.shape[1]
    nst = w_small_t.shape[0]

    def body(x_ref, nw_ref, wm_ref, ws_ref, wst_ref, pm_ref, ps_ref, pst_ref, ht_ref, r_ref):
        xv = x_ref[...]
        r = lax.rsqrt(jnp.mean(xv * xv, axis=-1, keepdims=True) + EPS)
        h = xv * r * nw_ref[...]
        hb = h.astype(MXU_DTYPE)
        for n0 in range(0, n, 512):
            pm_ref[:, n0:n0 + 512] = jnp.dot(hb, wm_ref[:, n0:n0 + 512], preferred_element_type=F32)
        ps_ref[...] = jnp.dot(hb, ws_ref[...], preferred_element_type=F32)
        pst_ref[...] = lax.dot_general(wst_ref[...], hb, _NT, preferred_element_type=F32)
        ht_ref[...] = h.T.astype(MXU_DTYPE)
        r_ref[...] = r

    return pl.pallas_call(
        body, name="inproj",
        grid=(t_len // tm,),
        in_specs=[pl.BlockSpec((tm, d), lambda i: (i, 0)),
                  pl.BlockSpec((1, d), lambda i: (0, 0)),
                  pl.BlockSpec((d, n), lambda i: (0, 0)),
                  pl.BlockSpec((d, ns), lambda i: (0, 0)),
                  pl.BlockSpec((nst, d), lambda i: (0, 0))],
        out_specs=[pl.BlockSpec((tm, n), lambda i: (i, 0)),
                   pl.BlockSpec((tm, ns), lambda i: (i, 0)),
                   pl.BlockSpec((nst, tm), lambda i: (0, i)),
                   pl.BlockSpec((d, tm), lambda i: (0, i)),
                   pl.BlockSpec((tm, 1), lambda i: (i, 0))],
        out_shape=[jax.ShapeDtypeStruct((t_len, n), F32),
                   jax.ShapeDtypeStruct((t_len, ns), F32),
                   jax.ShapeDtypeStruct((nst, t_len), F32),
                   jax.ShapeDtypeStruct((d, t_len), MXU_DTYPE),
                   jax.ShapeDtypeStruct((t_len, 1), F32)],
        compiler_params=_params("arbitrary"),
    )(x, norm_w, w_main, w_small, w_small_t)


def _running_sum_mm(x, tri):
    hi = x.astype(MXU_DTYPE)
    lo = (x - hi.astype(F32)).astype(MXU_DTYPE)
    return jnp.dot(hi, tri, preferred_element_type=F32) + jnp.dot(lo, tri, preferred_element_type=F32)


def _sb_iotas():
    row_i = lax.broadcasted_iota(jnp.int32, (SB_BQ, SB_BLOCK), 0)
    col_i = lax.broadcasted_iota(jnp.int32, (SB_BQ, SB_BLOCK), 1)
    sq_r = lax.broadcasted_iota(jnp.int32, (SB_BLOCK, SB_BLOCK), 0)
    sq_c = lax.broadcasted_iota(jnp.int32, (SB_BLOCK, SB_BLOCK), 1)
    return row_i, col_i, sq_r, sq_c


SB_DIAG_BLOCKS = SB_BQ // SB_BLOCK
SB_EXP_FLOOR = -110.0


def _sb_keys_descending(qi, tile, carry, z_bounds, n_heads):
    n_free = SB_DIAG_BLOCKS * qi
    for j in range(SB_DIAG_BLOCKS - 1, -1, -1):
        carry = tile(n_free + j, True, carry)

    def largest_exponent(c):
        worst = jnp.max(z_bounds[0] - c[1])
        for h in range(1, n_heads):
            worst = jnp.maximum(worst, jnp.max(z_bounds[h] - c[1 + h]))
        return worst

    def cond(state):
        return (state[0] < n_free) & (state[1] > SB_EXP_FLOOR)

    def body(state):
        c = tile(n_free - 1 - state[0], False, state[2:])
        return (state[0] + 1, largest_exponent(c), *c)

    out = lax.while_loop(cond, body, (jnp.int32(0), largest_exponent(carry), *carry))
    return out[2:], out[0]


def _sb_keys_ascending(qi, n_run, tile, carry):
    n_free = SB_DIAG_BLOCKS * qi
    carry = lax.fori_loop(0, n_run, lambda s, c: tile(n_free - n_run + s, False, c), carry)
    for j in range(SB_DIAG_BLOCKS):
        carry = tile(n_free + j, True, carry)
    return carry


def _sb_fwd_call(proj, t_len):
    nq = t_len // SB_BQ
    scale = float(SB_HEAD_DIM) ** -0.5
    n_pairs = 512 // LANES
    per_pair = LANES // SB_HEAD_DIM

    def body(q_ref, k_ref, v_ref, o_ref, st_ref, nrun_ref):
        lane = lax.broadcasted_iota(jnp.int32, (1, LANES), 1)
        row_i, col_i, sq_r, sq_c = _sb_iotas()
        ge = (sq_r >= sq_c).astype(MXU_DTYPE)
        hms = [((lane // SB_HEAD_DIM) == hh).astype(F32) for hh in range(per_pair)]
        k_sq = k_ref[...] * k_ref[...]
        k_norms = [jnp.sqrt(jnp.max(jnp.sum(k_sq * hm, axis=-1, keepdims=True))) * (1.02 * scale) for hm in hms]

        def q_loop(qi, carry):
            r0 = pl.multiple_of(qi * SB_BQ, SB_BQ)
            rows = pl.ds(r0, SB_BQ)
            q_all = q_ref[rows, :]
            qms = [(q_all * hm).astype(MXU_DTYPE) for hm in hms]
            z_bounds = [jnp.sqrt(jnp.sum(q_all * q_all * hm, axis=-1, keepdims=True)) * kn
                        for hm, kn in zip(hms, k_norms)]

            def tile(kj, masked, kc):
                acc, cs = kc[0], list(kc[1:])
                s0 = pl.multiple_of(kj * SB_BLOCK, SB_BLOCK)
                cols = pl.ds(s0, SB_BLOCK)
                kb = k_ref[cols, :].astype(MXU_DTYPE)
                v_all = v_ref[cols, :]
                mask = (col_i + s0) < (row_i + r0) if masked else None
                for hh in range(per_pair):
                    z = lax.dot_general(qms[hh], kb, _NT, preferred_element_type=F32) * scale
                    sp = _softplus(z)
                    if masked:
                        sp = jnp.where(mask, sp, 0.0)
                    a = jnp.exp(z - (_running_sum_mm(sp, ge) + cs[hh]))
                    if masked:
                        a = jnp.where(mask, a, 0.0)
                    vm = (v_all * hms[hh]).astype(MXU_DTYPE)
                    acc = acc + jnp.dot(a.astype(MXU_DTYPE), vm, preferred_element_type=F32)
                    cs[hh] = cs[hh] + jnp.sum(sp, axis=-1, keepdims=True)
                return (acc, *cs)

            zero_col = jnp.zeros((SB_BQ, 1), F32)
            out, n_run = _sb_keys_descending(
                qi, tile, (jnp.zeros((SB_BQ, LANES), F32),) + (zero_col,) * per_pair, z_bounds, per_pair)
            o_ref[rows, :] = out[0]
            for hh in range(per_pair):
                st_ref[hh, rows, :] = out[1 + hh]
            nrun_ref[pl.program_id(0), qi] = n_run
            return carry

        lax.fori_loop(0, nq, q_loop, 0)

    return pl.pallas_call(
        body, name="sb_fwd",
        grid=(n_pairs,),
        in_specs=[pl.BlockSpec((t_len, LANES), lambda p: (0, p)),
                  pl.BlockSpec((t_len, LANES), lambda p: (0, n_pairs + p)),
                  pl.BlockSpec((t_len, LANES), lambda p: (0, 2 * n_pairs + p))],
        out_specs=[pl.BlockSpec((t_len, LANES), lambda p: (0, p)),
                   pl.BlockSpec((per_pair, t_len, 1), lambda p: (p, 0, 0)),
                   pl.BlockSpec(memory_space=pltpu.SMEM)],
        out_shape=[jax.ShapeDtypeStruct((t_len, 512), F32),
                   jax.ShapeDtypeStruct((n_pairs * per_pair, t_len, 1), F32),
                   jax.ShapeDtypeStruct((n_pairs, nq), jnp.int32)],
        compiler_params=_params("arbitrary"),
    )(proj, proj, proj)


def _sb_bwd_call(proj, sp_total, n_run_all, d_o, t_len):
    nq = t_len // SB_BQ
    scale = float(SB_HEAD_DIM) ** -0.5
    n_pairs = 512 // LANES
    per_pair = LANES // SB_HEAD_DIM

    def body(q_ref, k_ref, v_ref, st_ref, nrun_ref, do_ref, d_ref):
        lane = lax.broadcasted_iota(jnp.int32, (1, LANES), 1)
        row_i, col_i, sq_r, sq_c = _sb_iotas()
        lt = (sq_r < sq_c).astype(MXU_DTYPE)
        le = (sq_r <= sq_c).astype(MXU_DTYPE)
        hms = [((lane // SB_HEAD_DIM) == hh).astype(F32) for hh in range(per_pair)]
        d_ref[1] = jnp.zeros((t_len, LANES), F32)
        d_ref[2] = jnp.zeros((t_len, LANES), F32)

        def q_loop(qi, carry):
            r0 = pl.multiple_of(qi * SB_BQ, SB_BQ)
            rows = pl.ds(r0, SB_BQ)
            q_all, do_all = q_ref[rows, :], do_ref[rows, :]
            qms = [(q_all * hm).astype(MXU_DTYPE) for hm in hms]
            doms = [(do_all * hm).astype(MXU_DTYPE) for hm in hms]
            totals = [st_ref[hh, rows, :] for hh in range(per_pair)]

            def tile(kj, masked, kc):
                dq, cls, gls = kc[0], list(kc[1:1 + per_pair]), list(kc[1 + per_pair:])
                s0 = pl.multiple_of(kj * SB_BLOCK, SB_BLOCK)
                cols = pl.ds(s0, SB_BLOCK)
                k_all, v_all = k_ref[cols, :], v_ref[cols, :]
                kb = k_all.astype(MXU_DTYPE)
                mask = (col_i + s0) < (row_i + r0) if masked else None
                dk_t = jnp.zeros((SB_BLOCK, LANES), F32)
                dv_t = jnp.zeros((SB_BLOCK, LANES), F32)
                for hh in range(per_pair):
                    z = lax.dot_general(qms[hh], kb, _NT, preferred_element_type=F32) * scale
                    sp_all = _softplus(z)
                    sp = jnp.where(mask, sp_all, 0.0) if masked else sp_all
                    a = jnp.exp(z - (totals[hh] - cls[hh] - _running_sum_mm(sp, lt)))
                    if masked:
                        a = jnp.where(mask, a, 0.0)
                    vm = (v_all * hms[hh]).astype(MXU_DTYPE)
                    km = (k_all * hms[hh]).astype(MXU_DTYPE)
                    g = lax.dot_general(doms[hh], vm, _NT, preferred_element_type=F32) * a
                    dz = g - jnp.exp(z - sp_all) * (gls[hh] + _running_sum_mm(g, le))
                    if masked:
                        dz = jnp.where(mask, dz, 0.0)
                    dz = (dz * scale).astype(MXU_DTYPE)
                    dq = dq + jnp.dot(dz, km, preferred_element_type=F32)
                    dk_t = dk_t + lax.dot_general(dz, qms[hh], _TN, preferred_element_type=F32)
                    dv_t = dv_t + lax.dot_general(a.astype(MXU_DTYPE), doms[hh], _TN, preferred_element_type=F32)
                    cls[hh] = cls[hh] + jnp.sum(sp, axis=-1, keepdims=True)
                    gls[hh] = gls[hh] + jnp.sum(g, axis=-1, keepdims=True)
                d_ref[1, cols, :] += dk_t
                d_ref[2, cols, :] += dv_t
                return (dq, *cls, *gls)

            zero_col = jnp.zeros((SB_BQ, 1), F32)
            out = _sb_keys_ascending(qi, nrun_ref[pl.program_id(0), qi], tile,
                                     (jnp.zeros((SB_BQ, LANES), F32),) + (zero_col,) * (2 * per_pair))
            d_ref[0, rows, :] = out[0]
            return carry

        lax.fori_loop(0, nq, q_loop, 0)

    col = lambda off: pl.BlockSpec((t_len, LANES), lambda p: (0, off + p))
    return pl.pallas_call(
        body, name="sb_bwd",
        grid=(n_pairs,),
        in_specs=[col(0), col(n_pairs), col(2 * n_pairs),
                  pl.BlockSpec((per_pair, t_len, 1), lambda p: (p, 0, 0)),
                  pl.BlockSpec(memory_space=pltpu.SMEM), col(0)],
        out_specs=pl.BlockSpec((3, t_len, LANES), lambda p: (0, 0, p)),
        out_shape=jax.ShapeDtypeStruct((3, t_len, 512), F32),
        compiler_params=_params("arbitrary"),
    )(proj, proj, proj, sp_total, n_run_all, d_o)


def _conv_taps(xin, rows, t_len):
    taps = []
    for i in range(CONV_WIDTH):
        shift = CONV_WIDTH - 1 - i
        if shift == 0:
            taps.append(xin)
        else:
            taps.append(jnp.where(rows >= shift, pltpu.roll(xin, shift, axis=0), 0.0))
    return taps


def _gdn_prep_body_common(x_ref, w_ref, t_len):
    j = pl.program_id(0)
    xin = x_ref[...]
    rows = lax.broadcasted_iota(jnp.int32, (t_len, LANES), 0)
    taps = _conv_taps(xin, rows, t_len)
    pre = taps[0] * w_ref[0:1, :]
    for i in range(1, CONV_WIDTH):
        pre = pre + taps[i] * w_ref[i:i + 1, :]
    sg = _sigmoid(pre)
    act = pre * sg
    is_qk = j < 2 * GDN_HEADS
    nrm = jnp.where(is_qk, lax.rsqrt(jnp.sum(act * act, axis=-1, keepdims=True) + EPS), 1.0)
    sc = jnp.where(j < GDN_HEADS, float(GDN_HEAD_DIM) ** -0.5, 1.0)
    return j, rows, taps, pre, sg, act, is_qk, nrm, sc


def _gdn_prep_call(proj, conv_w, t_len):
    first = 2048 // LANES

    def body(x_ref, w_ref, out_ref):
        _, _, _, _, _, act, _, nrm, sc = _gdn_prep_body_common(x_ref, w_ref, t_len)
        out_ref[...] = act * nrm * sc

    return pl.pallas_call(
        body, name="gdn_prep",
        grid=(3 * GDN_HEADS,),
        in_specs=[pl.BlockSpec((t_len, LANES), lambda j: (0, first + j)),
                  pl.BlockSpec((CONV_WIDTH, LANES), lambda j: (0, j))],
        out_specs=pl.BlockSpec((t_len, LANES), lambda j: (0, j)),
        out_shape=jax.ShapeDtypeStruct((t_len, 3 * 512), F32),
        compiler_params=_params("arbitrary"),
    )(proj, conv_w)


def _gdn_prep_bwd_call(proj, conv_w, d_act3, t_len):
    first = 2048 // LANES

    def body(x_ref, w_ref, d_ref, dx_ref, dw_ref):
        _, rows, taps, pre, sg, act, is_qk, nrm, sc = _gdn_prep_body_common(x_ref, w_ref, t_len)
        d_out = d_ref[0]
        dn = d_out * sc
        d_norm = nrm * dn - act * (nrm * nrm * nrm) * jnp.sum(dn * act, axis=-1, keepdims=True)
        d_act = jnp.where(is_qk, d_norm, d_out)
        d_pre = d_act * sg * (1.0 + pre * (1.0 - sg))
        dx = d_pre * w_ref[CONV_WIDTH - 1:CONV_WIDTH, :]
        dw_ref[CONV_WIDTH - 1:CONV_WIDTH, :] = jnp.sum(d_pre * taps[CONV_WIDTH - 1], axis=0, keepdims=True)
        for i in range(CONV_WIDTH - 1):
            shift = CONV_WIDTH - 1 - i
            up = jnp.where(rows < t_len - shift, pltpu.roll(d_pre, t_len - shift, axis=0), 0.0)
            dx = dx + up * w_ref[i:i + 1, :]
            dw_ref[i:i + 1, :] = jnp.sum(d_pre * taps[i], axis=0, keepdims=True)
        dx_ref[0] = dx

    return pl.pallas_call(
        body, name="gdn_prep_bwd",
        grid=(3 * GDN_HEADS,),
        in_specs=[pl.BlockSpec((t_len, LANES), lambda j: (0, first + j)),
                  pl.BlockSpec((CONV_WIDTH, LANES), lambda j: (0, j)),
                  pl.BlockSpec((1, t_len, LANES), lambda j: (j // GDN_HEADS, 0, j % GDN_HEADS))],
        out_specs=[pl.BlockSpec((1, t_len, LANES), lambda j: (j // GDN_HEADS, 0, j % GDN_HEADS)),
                   pl.BlockSpec((CONV_WIDTH, LANES), lambda j: (0, j))],
        out_shape=[jax.ShapeDtypeStruct((3, t_len, 512), F32),
                   jax.ShapeDtypeStruct((CONV_WIDTH, 3 * 512), F32)],
        compiler_params=_params("arbitrary"),
    )(proj, conv_w, d_act3)


def _chunk_cumsum_matrix():
    r = lax.broadcasted_iota(jnp.int32, (LANES, LANES), 0)
    c = lax.broadcasted_iota(jnp.int32, (LANES, LANES), 1)
    return ((r <= c) & ((r // CHUNK) == (c // CHUNK))).astype(F32)


def _gdn_gates_call(ps, pst, alog_l, dtb_l, alog_c, dtb_c, t_len):
    def body(ps_ref, pst_ref, al_ref, dl_ref, ac_ref, dc_ref, beta_ref, gcol_ref, grow_ref):
        upper = _chunk_cumsum_matrix()
        lower = upper.T
        psv = ps_ref[...]
        beta_ref[...] = _sigmoid(psv)
        g_l = -jnp.exp(al_ref[...]) * _softplus(psv + dl_ref[...])
        g_r = -jnp.exp(ac_ref[...]) * _softplus(pst_ref[...] + dc_ref[...])
        for w in range(t_len // LANES):
            sl = slice(w * LANES, (w + 1) * LANES)
            gcol_ref[sl, :] = _mx(lower, g_l[sl, :])
            grow_ref[:, sl] = _mx(g_r[:, sl], upper)

    vm = pl.BlockSpec(memory_space=pltpu.VMEM)
    return pl.pallas_call(
        body, name="gdn_gates",
        in_specs=[vm] * 6, out_specs=[vm] * 3,
        out_shape=[jax.ShapeDtypeStruct((t_len, LANES), F32),
                   jax.ShapeDtypeStruct((t_len, LANES), F32),
                   jax.ShapeDtypeStruct((8, t_len), F32)],
        compiler_params=pltpu.CompilerParams(vmem_limit_bytes=VMEM_LIMIT_BYTES),
    )(ps, pst, alog_l, dtb_l, alog_c, dtb_c)


def _gdn_gates_bwd_call(ps, alog_l, dtb_l, d_l, t_len):
    def body(ps_ref, al_ref, dl_ref, d_ref, dps_ref, gal_ref, gdt_ref):
        lane = lax.broadcasted_iota(jnp.int32, (1, LANES), 1)
        psv = ps_ref[...]
        dv = d_ref[...]
        beta = _sigmoid(psv)
        ea = jnp.exp(al_ref[...])
        arg = psv + dl_ref[...]
        g = -ea * _softplus(arg)
        d_a = dv * (-ea) * _sigmoid(arg)
        is_a = (lane >= GDN_HEADS) & (lane < 2 * GDN_HEADS)
        dps_ref[...] = jnp.where(lane < GDN_HEADS, dv * beta * (1.0 - beta), jnp.where(is_a, d_a, 0.0))
        gdt_ref[...] = jnp.where(is_a, jnp.sum(d_a, axis=0, keepdims=True), 0.0)
        gal_ref[...] = jnp.where(is_a, jnp.sum(dv * g, axis=0, keepdims=True), 0.0)

    vm = pl.BlockSpec(memory_space=pltpu.VMEM)
    return pl.pallas_call(
        body, name="gdn_gates_bwd",
        in_specs=[vm] * 4, out_specs=[vm] * 3,
        out_shape=[jax.ShapeDtypeStruct((t_len, LANES), F32),
                   jax.ShapeDtypeStruct((1, LANES), F32),
                   jax.ShapeDtypeStruct((1, LANES), F32)],
        compiler_params=pltpu.CompilerParams(vmem_limit_bytes=VMEM_LIMIT_BYTES),
    )(ps, alog_l, dtb_l, d_l)


def _chunk_terms(q_ref, k_ref, v_ref, b_ref, gc_ref, gr_ref, i, j, incl, strict):
    r0 = pl.multiple_of(i * CHUNK, CHUNK)
    rows = pl.ds(r0, CHUNK)
    lanes = slice(j * GDN_HEAD_DIM, (j + 1) * GDN_HEAD_DIM)
    q, k, v = q_ref[rows, lanes], k_ref[rows, lanes], v_ref[rows, lanes]
    head = pl.program_id(0) * GDN_HEADS_PER_STEP + j
    lane_ids = lax.broadcasted_iota(jnp.int32, (1, LANES), 1)
    pick = lambda slab, lane: jnp.sum(jnp.where(lane_ids == lane, slab, 0.0), axis=-1, keepdims=True)
    b = pick(b_ref[rows, :], head)
    gc = pick(gc_ref[rows, :], GDN_HEADS + head)
    gr = gr_ref[j, i]
    dm = jnp.where(incl, jnp.exp(jnp.where(incl, gc - gr, 0.0)), 0.0)
    kb = k * b
    vb = v * b
    e = jnp.exp(gc)
    a = jnp.where(strict, _m3_nt(kb, k) * dm, 0.0)
    p = jnp.where(incl, _m3_nt(q, k) * dm, 0.0)
    gl = gc[CHUNK - 1:CHUNK, :]
    eg = jnp.exp(gl - gc)
    return rows, lanes, q, k, v, b, gc, dm, kb, vb, e, a, p, gl, eg


def _gdn_specs(t_len, n_chunks):
    hb = GDN_HEADS_PER_STEP
    groups = GDN_HEADS // hb
    col_spec = lambda part: pl.BlockSpec((t_len, hb * GDN_HEAD_DIM), lambda g: (0, part * groups + g))
    gate_c = pl.BlockSpec((t_len, LANES), lambda g: (0, 0))
    gate_r = pl.BlockSpec((hb, n_chunks, 1, CHUNK), lambda g: (g, 0, 0, 0))
    per_chunk = lambda r, c: pl.BlockSpec((hb, n_chunks, r, c), lambda g: (g, 0, 0, 0))
    return hb, groups, col_spec, gate_c, gate_r, per_chunk


def _gdn_fwd_call(gact, beta_c, gam_c, gam_r, t_len):
    n_chunks = t_len // CHUNK
    dk = GDN_HEAD_DIM
    hb, groups, col_spec, gate_c, gate_r, per_chunk = _gdn_specs(t_len, n_chunks)

    def body(q_ref, k_ref, v_ref, b_ref, gc_ref, gr_ref, o_ref, s_ref, t_ref):
        row = lax.broadcasted_iota(jnp.int32, (CHUNK, CHUNK), 0)
        col = lax.broadcasted_iota(jnp.int32, (CHUNK, CHUNK), 1)
        incl, strict = row >= col, row > col
        eye = (row == col).astype(F32)

        def chunk(i, states):
            new_states = []
            for j in range(hb):
                s = states[j]
                rows, lanes, q, k, v, b, gc, dm, kb, vb, e, a, p, gl, eg = _chunk_terms(
                    q_ref, k_ref, v_ref, b_ref, gc_ref, gr_ref, i, j, incl, strict)
                x = -a
                tm = eye + x
                xp = x
                for _ in range(5):
                    xp = _m3(xp, xp)
                    tm = tm + _m3(tm, xp)
                u = _m3(tm, vb)
                w = _m3(tm, kb * e)
                vn = u - _m3(w, s)
                o_ref[rows, lanes] = _m3(q * e, s) + _m3(p, vn)
                s_ref[j, i] = s
                t_ref[j, i] = tm
                new_states.append(s * jnp.exp(gl) + _m3_tn(k * eg, vn))
            return tuple(new_states)

        lax.fori_loop(0, n_chunks, chunk, (jnp.zeros((dk, dk), F32),) * hb, unroll=2)

    return pl.pallas_call(
        body, name="gdn_fwd",
        grid=(groups,),
        in_specs=[col_spec(0), col_spec(1), col_spec(2), gate_c, gate_c, gate_r],
        out_specs=[pl.BlockSpec((t_len, hb * dk), lambda g: (0, g)), per_chunk(dk, dk), per_chunk(CHUNK, CHUNK)],
        out_shape=[jax.ShapeDtypeStruct((t_len, 512), F32),
                   jax.ShapeDtypeStruct((GDN_HEADS, n_chunks, dk, dk), F32),
                   jax.ShapeDtypeStruct((GDN_HEADS, n_chunks, CHUNK, CHUNK), F32)],
        compiler_params=_params("arbitrary"),
    )(gact, gact, gact, beta_c, gam_c, gam_r)


def _gdn_bwd_call(gact, beta_c, gam_c, gam_r, s_all, t_all, d_o, t_len):
    n_chunks = t_len // CHUNK
    dk = GDN_HEAD_DIM
    hb, groups, col_spec, gate_c, gate_r, per_chunk = _gdn_specs(t_len, n_chunks)

    def body(q_ref, k_ref, v_ref, b_ref, gc_ref, gr_ref, s_ref, t_ref, do_ref, d_ref, dgate_ref):
        row = lax.broadcasted_iota(jnp.int32, (CHUNK, CHUNK), 0)
        col = lax.broadcasted_iota(jnp.int32, (CHUNK, CHUNK), 1)
        incl, strict = row >= col, row > col
        upper = (row <= col).astype(F32)
        ones = jnp.ones((CHUNK, LANES), F32)
        last_row = lax.broadcasted_iota(jnp.int32, (CHUNK, 1), 0) == CHUNK - 1
        lane_ids = lax.broadcasted_iota(jnp.int32, (1, LANES), 1)
        rsum = lambda m: jnp.sum(m, axis=-1, keepdims=True)

        @pl.when(pl.program_id(0) == 0)
        def _():
            dgate_ref[...] = jnp.zeros_like(dgate_ref)

        def chunk(step, d_states):
            i = n_chunks - 1 - step
            new_states = []
            for j in range(hb):
                ds = d_states[j]
                rows, lanes, q, k, v, b, gc, dm, kb, vb, e, a, p, gl, eg = _chunk_terms(
                    q_ref, k_ref, v_ref, b_ref, gc_ref, gr_ref, i, j, incl, strict)
                s = s_ref[j, i]
                tm = t_ref[j, i]
                d_out = do_ref[rows, lanes]
                el = jnp.exp(gl)
                kbe = kb * e
                u = _m3(tm, vb)
                w = _m3(tm, kbe)
                vn = u - _m3(w, s)
                qe = q * e
                kd = k * eg

                d_vn = _m3_tn(p, d_out) + _m3(kd, ds)
                d_qe = _m3_nt(d_out, s)
                d_p = jnp.where(incl, _m3_nt(d_out, vn), 0.0)
                new_states.append(el * ds + _m3_tn(qe, d_out) - _m3_tn(w, d_vn))
                d_kd = _m3_nt(vn, ds)
                d_w = -_m3_nt(d_vn, s)
                d_vb = _m3_tn(tm, d_vn)
                d_kbe = _m3_tn(tm, d_w)
                d_a = -jnp.where(strict, _m3_nt(d_vb, u) + _m3_nt(d_kbe, w), 0.0)
                m = d_a * dm
                n = d_p * dm
                d_kb = _m3(m, k) + d_kbe * e
                d_q = _m3(n, k) + d_qe * e
                d_k = _m3_tn(m, kb) + _m3_tn(n, q) + d_kd * eg + b * d_kb
                r = d_a * a + d_p * p
                kd_term = rsum(d_kd * kd)
                d_gl = jnp.sum(ds * s) * el + jnp.sum(kd_term)
                d_gam = (rsum(r) - _mx_tn(r, ones)[:, 0:1] + rsum(d_qe * qe) + rsum(d_kbe * kbe) - kd_term
                         + jnp.where(last_row, d_gl, 0.0))
                d_ref[0, rows, lanes] = d_q
                d_ref[1, rows, lanes] = d_k
                d_ref[2, rows, lanes] = b * d_vb
                head = pl.program_id(0) * hb + j
                d_beta = rsum(d_kb * k) + rsum(d_vb * v)
                d_g = _mx(upper, d_gam * ones)[:, 0:1]
                dgate_ref[rows, :] += (jnp.where(lane_ids == head, d_beta, 0.0)
                                       + jnp.where(lane_ids == GDN_HEADS + head, d_g, 0.0))
            return tuple(new_states)

        lax.fori_loop(0, n_chunks, chunk, (jnp.zeros((dk, dk), F32),) * hb)

    return pl.pallas_call(
        body, name="gdn_bwd",
        grid=(groups,),
        in_specs=[col_spec(0), col_spec(1), col_spec(2), gate_c, gate_c, gate_r,
                  per_chunk(dk, dk), per_chunk(CHUNK, CHUNK),
                  pl.BlockSpec((t_len, hb * dk), lambda g: (0, g))],
        out_specs=[pl.BlockSpec((3, t_len, hb * dk), lambda g: (0, 0, g)), gate_c],
        out_shape=[jax.ShapeDtypeStruct((3, t_len, 512), F32),
                   jax.ShapeDtypeStruct((t_len, LANES), F32)],
        compiler_params=_params("arbitrary"),
    )(gact, gact, gact, beta_c, gam_c, gam_r, s_all, t_all, d_o)


def _group_matrix(width, group):
    r = lax.broadcasted_iota(jnp.int32, (width, width), 0)
    c = lax.broadcasted_iota(jnp.int32, (width, width), 1)
    return ((r // group) == (c // group)).astype(F32)


def _post_call(o_sb, o_gd, proj, x, target, w_out, sbw, gdw, fw, tm=256):
    t_len, d = x.shape
    half = 512
    zsb_blk = 1536 // half
    zgd_blk = 3584 // half

    def body(osb_ref, ogd_ref, zsb_ref, zgd_ref, x_ref, tg_ref, wo_ref, sbw_ref, gdw_ref, fw_ref,
             dx2_ref, dosb_ref, dzsb_ref, dogd_ref, dzgd_ref, loss_ref, gfw_ref, gsb_ref, ggd_ref, gwo_ref):
        step = pl.program_id(0)

        @pl.when(step == 0)
        def _():
            loss_ref[...] = jnp.zeros_like(loss_ref)
            gfw_ref[...] = jnp.zeros_like(gfw_ref)
            gsb_ref[...] = jnp.zeros_like(gsb_ref)
            ggd_ref[...] = jnp.zeros_like(ggd_ref)
            gwo_ref[...] = jnp.zeros_like(gwo_ref)

        def head_forward(o, z, w, gmat, inv):
            r = lax.rsqrt(_mx(o * o, gmat) * inv + EPS)
            nrm = o * r * w
            sg = _sigmoid(z)
            return r, nrm, sg, nrm * (z * sg)

        def head_backward(d_m, o, z, w, gmat, inv, r, nrm, sg):
            d_n = d_m * (z * sg)
            d_z = d_m * nrm * (sg * (1.0 + z * (1.0 - sg)))
            dnw = d_n * w
            d_o = r * dnw - o * (r * r * r) * (_mx(dnw * o, gmat) * inv)
            return d_o, d_z, jnp.sum(d_n * o * r, axis=0, keepdims=True)

        g_sb = _group_matrix(half, SB_HEAD_DIM)
        g_gd = _group_matrix(half, GDN_HEAD_DIM)
        osb, ogd, zsb, zgd = osb_ref[...], ogd_ref[...], zsb_ref[...], zgd_ref[...]
        sbw_v, gdw_v = sbw_ref[...], gdw_ref[...]
        r_sb, n_sb, sg_sb, m_sb = head_forward(osb, zsb, sbw_v, g_sb, 1.0 / SB_HEAD_DIM)
        r_gd, n_gd, sg_gd, m_gd = head_forward(ogd, zgd, gdw_v, g_gd, 1.0 / GDN_HEAD_DIM)
        mixed = jnp.concatenate([m_sb, m_gd], axis=1).astype(MXU_DTYPE)
        wo = wo_ref[...]
        x2 = x_ref[...] + jnp.dot(mixed, wo, preferred_element_type=F32)
        r2 = lax.rsqrt(jnp.mean(x2 * x2, axis=-1, keepdims=True) + EPS)
        fw_v = fw_ref[...]
        err = x2 * r2 * fw_v - tg_ref[...]
        loss_ref[...] += 0.5 * jnp.sum(jnp.sum(err * err, axis=-1, keepdims=True) * (1.0 / d))
        dy = err * (1.0 / d)
        gg = dy * fw_v
        dx2 = r2 * gg - x2 * ((r2 * r2 * r2) * jnp.mean(gg * x2, axis=-1, keepdims=True))
        gfw_ref[...] += jnp.sum(dy * x2 * r2, axis=0, keepdims=True)
        dx2_ref[...] = dx2
        dx2b = dx2.astype(MXU_DTYPE)
        d_mixed = lax.dot_general(dx2b, wo, _NT, preferred_element_type=F32)
        gwo_ref[...] += lax.dot_general(mixed, dx2b, _TN, preferred_element_type=F32)
        d_osb, d_zsb, gsb = head_backward(d_mixed[:, :half], osb, zsb, sbw_v, g_sb, 1.0 / SB_HEAD_DIM, r_sb, n_sb, sg_sb)
        d_ogd, d_zgd, ggd = head_backward(d_mixed[:, half:], ogd, zgd, gdw_v, g_gd, 1.0 / GDN_HEAD_DIM, r_gd, n_gd, sg_gd)
        dosb_ref[...] = d_osb
        dzsb_ref[...] = d_zsb
        dogd_ref[...] = d_ogd
        dzgd_ref[...] = d_zgd
        gsb_ref[...] += gsb
        ggd_ref[...] += ggd

    row_blk = lambda w: pl.BlockSpec((tm, w), lambda i: (i, 0))
    fixed = lambda r, w: pl.BlockSpec((r, w), lambda i: (0, 0))
    return pl.pallas_call(
        body, name="post",
        grid=(t_len // tm,),
        in_specs=[row_blk(half), row_blk(half),
                  pl.BlockSpec((tm, half), lambda i: (i, zsb_blk)),
                  pl.BlockSpec((tm, half), lambda i: (i, zgd_blk)),
                  row_blk(d), row_blk(d), fixed(d, d), fixed(1, half), fixed(1, half), fixed(1, d)],
        out_specs=[row_blk(d), row_blk(half), row_blk(half), row_blk(half), row_blk(half),
                   fixed(1, LANES), fixed(1, d), fixed(1, half), fixed(1, half), fixed(d, d)],
        out_shape=[jax.ShapeDtypeStruct((t_len, d), F32)] + [jax.ShapeDtypeStruct((t_len, half), F32)] * 4
                  + [jax.ShapeDtypeStruct((1, LANES), F32), jax.ShapeDtypeStruct((1, d), F32),
                     jax.ShapeDtypeStruct((1, half), F32), jax.ShapeDtypeStruct((1, half), F32),
                     jax.ShapeDtypeStruct((d, d), F32)],
        compiler_params=_params("arbitrary"),
    )(o_sb, o_gd, proj, proj, x, target, w_out, sbw, gdw, fw)


def _gw_in_call(h_t, dproj8, tm=512):
    d, t_len = h_t.shape
    n_piece, _, pw = dproj8.shape

    def body(ht_ref, dp_ref, gw_ref):
        @pl.when(pl.program_id(1) == 0)
        def _():
            gw_ref[...] = jnp.zeros_like(gw_ref)

        gw_ref[...] += jnp.dot(ht_ref[...], dp_ref[0].astype(MXU_DTYPE), preferred_element_type=F32)

    return pl.pallas_call(
        body, name="gw_in",
        grid=(n_piece, t_len // tm),
        in_specs=[pl.BlockSpec((d, tm), lambda p, t: (0, t)),
                  pl.BlockSpec((1, tm, pw), lambda p, t: (p, t, 0))],
        out_specs=pl.BlockSpec((d, pw), lambda p, t: (0, p)),
        out_shape=jax.ShapeDtypeStruct((d, n_piece * pw), F32),
        compiler_params=_params("arbitrary", "arbitrary"),
    )(h_t, dproj8)


def _gw_small_call(h_t, dsmall, tm=512):
    d, t_len = h_t.shape
    ns = dsmall.shape[1]

    def body(ht_ref, dp_ref, gw_ref):
        @pl.when(pl.program_id(0) == 0)
        def _():
            gw_ref[...] = jnp.zeros_like(gw_ref)

        gw_ref[...] += jnp.dot(ht_ref[...], dp_ref[...].astype(MXU_DTYPE), preferred_element_type=F32)

    return pl.pallas_call(
        body, name="gw_small",
        grid=(t_len // tm,),
        in_specs=[pl.BlockSpec((d, tm), lambda t: (0, t)),
                  pl.BlockSpec((tm, ns), lambda t: (t, 0))],
        out_specs=pl.BlockSpec((d, ns), lambda t: (0, 0)),
        out_shape=jax.ShapeDtypeStruct((d, ns), F32),
        compiler_params=_params("arbitrary"),
    )(h_t, dsmall)


def _dx_call(dproj8, dsmall, w_main, w_small, x, r, dx2, norm_w, tm=256):
    t_len, d = x.shape
    n_piece, _, pw = dproj8.shape
    ns = dsmall.shape[1]

    def body(dp_ref, ds_ref, wm_ref, ws_ref, x_ref, r_ref, dx2_ref, nw_ref, gx_ref, gnw_ref):
        @pl.when(pl.program_id(0) == 0)
        def _():
            gnw_ref[...] = jnp.zeros_like(gnw_ref)

        dh = lax.dot_general(ds_ref[...].astype(MXU_DTYPE), ws_ref[...], _NT, preferred_element_type=F32)
        for p in range(n_piece):
            dh = dh + lax.dot_general(dp_ref[p].astype(MXU_DTYPE), wm_ref[:, p * pw:(p + 1) * pw], _NT,
                                      preferred_element_type=F32)
        xv, rv = x_ref[...], r_ref[...]
        dn = dh * nw_ref[...]
        gx_ref[...] = dx2_ref[...] + rv * dn - xv * ((rv * rv * rv) * jnp.mean(dn * xv, axis=-1, keepdims=True))
        gnw_ref[...] += jnp.sum(dh * xv * rv, axis=0, keepdims=True)

    return pl.pallas_call(
        body, name="dx",
        grid=(t_len // tm,),
        in_specs=[pl.BlockSpec((n_piece, tm, pw), lambda i: (0, i, 0)),
                  pl.BlockSpec((tm, ns), lambda i: (i, 0)),
                  pl.BlockSpec((d, n_piece * pw), lambda i: (0, 0)),
                  pl.BlockSpec((d, ns), lambda i: (0, 0)),
                  pl.BlockSpec((tm, d), lambda i: (i, 0)),
                  pl.BlockSpec((tm, 1), lambda i: (i, 0)),
                  pl.BlockSpec((tm, d), lambda i: (i, 0)),
                  pl.BlockSpec((1, d), lambda i: (0, 0))],
        out_specs=[pl.BlockSpec((tm, d), lambda i: (i, 0)),
                   pl.BlockSpec((1, d), lambda i: (0, 0))],
        out_shape=[jax.ShapeDtypeStruct((t_len, d), F32), jax.ShapeDtypeStruct((1, d), F32)],
        compiler_params=_params("arbitrary"),
    )(dproj8, dsmall, w_main, w_small, x, r, dx2, norm_w)


def _exchange_call(name, srcs, per_peer):
    n = len(srcs)
    out_shapes = [jax.ShapeDtypeStruct(s.shape if pp else (N_DEV,) + s.shape, s.dtype) for s, pp in zip(srcs, per_peer)]

    def body(*refs):
        src_refs, out_refs = refs[:n], refs[n:2 * n]
        send_sems, recv_sems, local_sems = refs[2 * n:]
        x, y, c = lax.axis_index("x"), lax.axis_index("y"), lax.axis_index("c")
        me = 4 * x + 2 * y + c
        copies = []
        for a in range(n):
            mine = src_refs[a].at[me] if per_peer[a] else src_refs[a]
            local = pltpu.make_async_copy(mine, out_refs[a].at[me], local_sems.at[a])
            local.start()
            copies.append(local)
        remote = []
        for k in range(1, N_DEV):
            kx, ky, kc = (k >> 2) & 1, (k >> 1) & 1, k & 1
            px = 1 - x if kx else x
            py = 1 - y if ky else y
            pc = 1 - c if kc else c
            peer = 4 * px + 2 * py + pc
            for a in range(n):
                sem = a * (N_DEV - 1) + (k - 1)
                src = src_refs[a].at[peer] if per_peer[a] else src_refs[a]
                cp = pltpu.make_async_remote_copy(
                    src_ref=src, dst_ref=out_refs[a].at[me],
                    send_sem=send_sems.at[sem], recv_sem=recv_sems.at[sem],
                    device_id=(px, py, pc), device_id_type=pl.DeviceIdType.MESH)
                cp.start()
                remote.append(cp)
        for cp in remote:
            cp.wait_send()
        for cp in remote:
            cp.wait_recv()
        for cp in copies:
            cp.wait()

    hbm = pl.BlockSpec(memory_space=pl.ANY)
    return pl.pallas_call(
        body, name=name,
        in_specs=[hbm] * n, out_specs=[hbm] * n, out_shape=out_shapes,
        scratch_shapes=[pltpu.SemaphoreType.DMA((n * (N_DEV - 1),)),
                        pltpu.SemaphoreType.DMA((n * (N_DEV - 1),)),
                        pltpu.SemaphoreType.DMA((n,))],
    )(*srcs)


N_CHIPS = 4
_HBM = pl.BlockSpec(memory_space=pl.ANY)
_MESH = pl.DeviceIdType.MESH


def _gather_call(name, srcs):
    n = len(srcs)
    per = N_DEV - 1

    def body(*refs):
        src_refs, out_refs = refs[:n], refs[n:2 * n]
        send_sems, recv_sems, local_sems = refs[2 * n:]
        x, y, c = lax.axis_index("x"), lax.axis_index("y"), lax.axis_index("c")
        me, sibling = (x, y, c), (x, y, 1 - c)
        chips = [(1 - x, y), (x, 1 - y), (1 - x, 1 - y)]
        slot = lambda px, py, pc: 4 * px + 2 * py + pc

        def copy(a, k, block, to, from_src=False):
            rows = out_refs[a].at[slot(*block)]
            return pltpu.make_async_remote_copy(
                src_ref=src_refs[a] if from_src else rows, dst_ref=rows,
                send_sem=send_sems.at[a * per + k], recv_sem=recv_sems.at[a * per + k],
                device_id=to, device_id_type=_MESH)

        local = [pltpu.make_async_copy(src_refs[a], out_refs[a].at[slot(*me)], local_sems.at[a]) for a in range(n)]
        for cp in local:
            cp.start()
        first = []
        for a in range(n):
            first.append(copy(a, 0, me, sibling, True))
            first += [copy(a, 1 + j, me, (*chip, c), True) for j, chip in enumerate(chips)]
        for cp in first:
            cp.start()
        passed = []
        for j, chip in enumerate(chips):
            for a in range(n):
                copy(a, 1 + j, (*chip, c), me).wait_recv()
                fwd = copy(a, 4 + j, (*chip, c), sibling)
                fwd.start()
                passed.append(fwd)
        for a in range(n):
            copy(a, 0, sibling, me).wait_recv()
            for j, chip in enumerate(chips):
                copy(a, 4 + j, (*chip, 1 - c), me).wait_recv()
        for cp in first + passed:
            cp.wait_send()
        for cp in local:
            cp.wait()

    return pl.pallas_call(
        body, name=name,
        in_specs=[_HBM] * n, out_specs=[_HBM] * n,
        out_shape=[jax.ShapeDtypeStruct((N_DEV,) + s.shape, s.dtype) for s in srcs],
        scratch_shapes=[pltpu.SemaphoreType.DMA((n * per,)), pltpu.SemaphoreType.DMA((n * per,)),
                        pltpu.SemaphoreType.DMA((n,))],
    )(*srcs)


def _sibling_send_call(name, srcs):
    n = len(srcs)

    def body(*refs):
        src_refs, out_refs = refs[:n], refs[n:2 * n]
        send_sems, recv_sems = refs[2 * n:]
        x, y, c = lax.axis_index("x"), lax.axis_index("y"), lax.axis_index("c")
        copies = []
        for a in range(n):
            for ch in range(N_CHIPS):
                copies.append(pltpu.make_async_remote_copy(
                    src_ref=src_refs[a].at[2 * ch + (1 - c)], dst_ref=out_refs[a].at[ch],
                    send_sem=send_sems.at[a * N_CHIPS + ch], recv_sem=recv_sems.at[a * N_CHIPS + ch],
                    device_id=(x, y, 1 - c), device_id_type=_MESH))
        for cp in copies:
            cp.start()
        for cp in copies:
            cp.wait_send()
        for cp in copies:
            cp.wait_recv()

    return pl.pallas_call(
        body, name=name,
        in_specs=[_HBM] * n, out_specs=[_HBM] * n,
        out_shape=[jax.ShapeDtypeStruct((N_CHIPS,) + s.shape[1:], s.dtype) for s in srcs],
        scratch_shapes=[pltpu.SemaphoreType.DMA((n * N_CHIPS,)), pltpu.SemaphoreType.DMA((n * N_CHIPS,))],
    )(*srcs)


def _pair_sum_call(name, parts, from_sibling, tr):
    _, rows, cols = parts.shape

    def body(p_ref, s_ref, o_ref):
        o_ref[...] = (p_ref[...] + s_ref[...]).astype(o_ref.dtype)

    return pl.pallas_call(
        body, name=name,
        grid=(N_CHIPS, rows // tr),
        in_specs=[pl.BlockSpec((1, tr, cols), lambda ch, i: (2 * ch + lax.axis_index("c"), i, 0)),
                  pl.BlockSpec((1, tr, cols), lambda ch, i: (ch, i, 0))],
        out_specs=pl.BlockSpec((1, tr, cols), lambda ch, i: (ch, i, 0)),
        out_shape=jax.ShapeDtypeStruct((N_CHIPS, rows, cols), WIRE_DTYPE),
        compiler_params=_params("arbitrary", "arbitrary"),
    )(parts, from_sibling)


def _chip_exchange_call(name, srcs):
    n = len(srcs)
    per = N_CHIPS - 1

    def body(*refs):
        src_refs, out_refs = refs[:n], refs[n:2 * n]
        send_sems, recv_sems, local_sems = refs[2 * n:]
        x, y, c = lax.axis_index("x"), lax.axis_index("y"), lax.axis_index("c")
        mine = 2 * x + y
        chips = [(1 - x, y), (x, 1 - y), (1 - x, 1 - y)]
        local = [pltpu.make_async_copy(src_refs[a].at[mine], out_refs[a].at[mine], local_sems.at[a]) for a in range(n)]
        for cp in local:
            cp.start()
        remote = []
        for a in range(n):
            for j, (px, py) in enumerate(chips):
                remote.append(pltpu.make_async_remote_copy(
                    src_ref=src_refs[a].at[2 * px + py], dst_ref=out_refs[a].at[mine],
                    send_sem=send_sems.at[a * per + j], recv_sem=recv_sems.at[a * per + j],
                    device_id=(px, py, c), device_id_type=_MESH))
        for cp in remote:
            cp.start()
        for cp in remote:
            cp.wait_send()
        for cp in remote:
            cp.wait_recv()
        for cp in local:
            cp.wait()

    return pl.pallas_call(
        body, name=name,
        in_specs=[_HBM] * n, out_specs=[_HBM] * n,
        out_shape=[jax.ShapeDtypeStruct(s.shape, s.dtype) for s in srcs],
        scratch_shapes=[pltpu.SemaphoreType.DMA((n * per,)), pltpu.SemaphoreType.DMA((n * per,)),
                        pltpu.SemaphoreType.DMA((n,))],
    )(*srcs)


def _adam_call(name, parts, w, m, v, tr):
    rows, cols = w.shape
    n_slots = parts.shape[0]

    def body(p_ref, w_ref, m_ref, v_ref, g_ref, d_ref, nm_ref, nv_ref):
        g = p_ref[0].astype(F32)
        for s in range(1, n_slots):
            g = g + p_ref[s].astype(F32)
        m_new = ADAM_B1 * m_ref[...] + (1.0 - ADAM_B1) * g
        v_new = ADAM_B2 * v_ref[...] + (1.0 - ADAM_B2) * (g * g)
        m_hat = m_new / (1.0 - ADAM_B1 ** ADAM_STEP)
        v_hat = v_new / (1.0 - ADAM_B2 ** ADAM_STEP)
        g_ref[...] = g
        d_ref[...] = -ADAM_LR * (m_hat / (jnp.sqrt(v_hat) + ADAM_EPS) + ADAM_WD * w_ref[...])
        nm_ref[...] = m_new
        nv_ref[...] = v_new

    blk = pl.BlockSpec((tr, cols), lambda i: (i, 0))
    return pl.pallas_call(
        body, name=name,
        grid=(rows // tr,),
        in_specs=[pl.BlockSpec((n_slots, tr, cols), lambda i: (0, i, 0)), blk, blk, blk],
        out_specs=[blk] * 4,
        out_shape=[jax.ShapeDtypeStruct((rows, cols), F32)] * 4,
        compiler_params=_params("arbitrary"),
    )(parts, w, m, v)


_SMALL_ROWS = ("norm1_w", "final_norm_w", "sb_norm_w", "gdn_norm_w", "gdn_A_log", "gdn_dt_bias", "loss")


def _pack_small(vals, width):
    rows = [jnp.pad(a.reshape(1, -1).astype(F32), ((0, 0), (0, width - a.size))) for a in vals]
    rows += [jnp.zeros((1, width), F32)] * (8 - len(rows))
    return jnp.concatenate(rows, axis=0)


def _device_step(x2d, tgt, w_full, w_out_f32, conv_full, norm1_w, sb_norm_w, gdn_A_log, gdn_dt_bias, gdn_norm_w,
                 final_norm_w):
    t_len, d = x2d.shape
    n_chunks = t_len // CHUNK
    n_main = 8 * 512
    n_small = w_full.shape[1] - n_main
    w_main = w_full[:, :n_main].astype(MXU_DTYPE)
    w_small = jnp.pad(w_full[:, n_main:], ((0, 0), (0, LANES - n_small))).astype(MXU_DTYPE)
    w_small_t = w_full[:, n_main:].T.astype(MXU_DTYPE)
    w_out_full = w_out_f32.astype(MXU_DTYPE)

    pad_lanes = lambda a, lo: jnp.pad(a.reshape(1, -1), ((0, 0), (lo, LANES - lo - a.size)))
    alog_l, dtb_l = pad_lanes(gdn_A_log, GDN_HEADS), pad_lanes(gdn_dt_bias, GDN_HEADS)
    alog_c, dtb_c = alog_l[:, :8].T, dtb_l[:, :8].T
    sbw = jnp.tile(sb_norm_w, (1, 512 // SB_HEAD_DIM))
    gdw = jnp.tile(gdn_norm_w, (1, 512 // GDN_HEAD_DIM))
    fw = final_norm_w.reshape(1, d)

    proj, ps, pst, h_t, r1 = _inproj_call(x2d, norm1_w, w_main, w_small, w_small_t)
    o_sb, sp_total, sb_blocks_run = _sb_fwd_call(proj, t_len)
    gact = _gdn_prep_call(proj, conv_full, t_len)
    beta_l, gcol_l, grow = _gdn_gates_call(ps, pst, alog_l, dtb_l, alog_c, dtb_c, t_len)
    gam_r = grow[GDN_HEADS:2 * GDN_HEADS].reshape(GDN_HEADS, n_chunks, 1, CHUNK)
    o_gd, s_all, t_all = _gdn_fwd_call(gact, beta_l, gcol_l, gam_r, t_len)

    (dx2, d_osb, d_zsb, d_ogd, d_zgd, loss_p, g_fw, g_sbw, g_gdw, g_wout) = _post_call(
        o_sb, o_gd, proj, x2d, tgt, w_out_full, sbw, gdw, fw)

    d_sb3 = _sb_bwd_call(proj, sp_total, sb_blocks_run, d_osb, t_len)
    d_gact3, d_gates = _gdn_bwd_call(gact, beta_l, gcol_l, gam_r, s_all, t_all, d_ogd, t_len)
    d_gd3, g_conv = _gdn_prep_bwd_call(proj, conv_full, d_gact3, t_len)
    dsmall, g_alog, g_dtb = _gdn_gates_bwd_call(ps, alog_l, dtb_l, d_gates, t_len)

    dproj8 = jnp.concatenate([d_sb3, d_zsb[None], d_gd3, d_zgd[None]], axis=0)
    g_w_main = _gw_in_call(h_t, dproj8)
    g_w_small = _gw_small_call(h_t, dsmall)
    grad_x, g_n1 = _dx_call(dproj8, dsmall, w_main, w_small, x2d, r1, dx2, norm1_w)
    g_w_in_full = jnp.concatenate([g_w_main, g_w_small[:, :n_small]], axis=1)
    return (loss_p, grad_x, g_n1, g_w_in_full, g_sbw, g_conv, g_alog, g_dtb, g_gdw, g_wout, g_fw)


def kernel(x, norm1_w, w_in, sb_norm_w, gdn_conv_w, gdn_A_log, gdn_dt_bias, gdn_norm_w, w_out, final_norm_w, loss_target, m_norm1_w, m_w_in, m_sb_norm_w, m_gdn_conv_w, m_gdn_A_log, m_gdn_dt_bias, m_gdn_norm_w, m_w_out, m_final_norm_w, v_norm1_w, v_w_in, v_sb_norm_w, v_gdn_conv_w, v_gdn_A_log, v_gdn_dt_bias, v_gdn_norm_w, v_w_out, v_final_norm_w):
    d = x.shape[2]
    shard_cols = w_in.shape[2]
    conv_cols = gdn_conv_w.shape[2]

    w_in_g, w_out_g, conv_g = _gather_call(
        "gather_weights", [w_in[0].astype(WIRE_DTYPE), w_out[0].astype(WIRE_DTYPE), gdn_conv_w[0]])
    w_full = w_in_g.transpose(1, 0, 2).reshape(d, N_DEV * shard_cols)
    conv_full = conv_g.transpose(1, 0, 2).reshape(CONV_WIDTH, N_DEV * conv_cols)

    (loss_p, grad_x, g_n1, g_w_in_full, g_sbw, g_conv, g_alog, g_dtb, g_gdw, g_wout, g_fw) = _device_step(
        x[0], loss_target[0], w_full, w_out_g.reshape(d, d), conv_full, norm1_w, sb_norm_w, gdn_A_log, gdn_dt_bias,
        gdn_norm_w, final_norm_w)

    g_w_in_parts = g_w_in_full.reshape(d, N_DEV, shard_cols).transpose(1, 0, 2)
    g_wout_parts = g_wout.reshape(N_DEV, d // N_DEV, d)
    g_conv_parts = g_conv.reshape(CONV_WIDTH, N_DEV, conv_cols).transpose(1, 0, 2)
    fold = lambda a, group: a.reshape(-1, group).sum(axis=0)
    small_g = _pack_small([g_n1, g_fw, fold(g_sbw, SB_HEAD_DIM), fold(g_gdw, GDN_HEAD_DIM),
                           g_alog[0, GDN_HEADS:2 * GDN_HEADS], g_dtb[0, GDN_HEADS:2 * GDN_HEADS],
                           loss_p[0, :1]], d)
    sib_w_in, sib_wout, sib_conv = _sibling_send_call("grads_to_sibling", [g_w_in_parts, g_wout_parts, g_conv_parts])
    c_w_in = _pair_sum_call("pair_sum_w_in", g_w_in_parts, sib_w_in, 256)
    c_wout = _pair_sum_call("pair_sum_w_out", g_wout_parts, sib_wout, d // N_DEV)
    c_conv = _pair_sum_call("pair_sum_conv", g_conv_parts, sib_conv, CONV_WIDTH)
    p_w_in, p_wout, p_conv = _chip_exchange_call("grads_to_chips", [c_w_in, c_wout, c_conv])
    (p_small,) = _exchange_call("exchange_small", [small_g], [False])

    small_w = _pack_small([norm1_w, final_norm_w, sb_norm_w, gdn_norm_w, gdn_A_log, gdn_dt_bias], d)
    small_m = _pack_small([m_norm1_w, m_final_norm_w, m_sb_norm_w, m_gdn_norm_w, m_gdn_A_log, m_gdn_dt_bias], d)
    small_v = _pack_small([v_norm1_w, v_final_norm_w, v_sb_norm_w, v_gdn_norm_w, v_gdn_A_log, v_gdn_dt_bias], d)

    r_w_in = _adam_call("adam_w_in", p_w_in, w_in[0], m_w_in[0], v_w_in[0], 256)
    r_wout = _adam_call("adam_w_out", p_wout, w_out[0], m_w_out[0], v_w_out[0], d // N_DEV)
    r_conv = _adam_call("adam_conv", p_conv, gdn_conv_w[0], m_gdn_conv_w[0], v_gdn_conv_w[0], CONV_WIDTH)
    r_small = _adam_call("adam_small", p_small, small_w, small_m, small_v, 8)

    shapes = {"norm1_w": norm1_w.shape, "final_norm_w": final_norm_w.shape, "sb_norm_w": sb_norm_w.shape,
              "gdn_norm_w": gdn_norm_w.shape, "gdn_A_log": gdn_A_log.shape, "gdn_dt_bias": gdn_dt_bias.shape}

    def small_out(kind, name):
        row = _SMALL_ROWS.index(name)
        shp = shapes[name]
        size = 1
        for s in shp:
            size *= s
        return r_small[kind][row, :size].reshape(shp)

    def outputs(kind):
        return (small_out(kind, "norm1_w"), r_w_in[kind][None], small_out(kind, "sb_norm_w"), r_conv[kind][None],
                small_out(kind, "gdn_A_log"), small_out(kind, "gdn_dt_bias"), small_out(kind, "gdn_norm_w"),
                r_wout[kind][None], small_out(kind, "final_norm_w"))

    loss = r_small[0][_SMALL_ROWS.index("loss"), 0]
    return (loss, grad_x[None], *outputs(0), *outputs(1), *outputs(2), *outputs(3))
```

```python
import functools

import jax
import jax.numpy as jnp
from jax import lax
from jax.experimental import pallas as pl
from jax.experimental.pallas import tpu as pltpu

F32 = jnp.float32
MXU_DTYPE = jnp.bfloat16
WIRE_DTYPE = jnp.bfloat16
EXACT = lax.Precision.HIGHEST
EPS = 1e-6
N_DEV = 8
SB_HEAD_DIM = 64
GDN_HEAD_DIM = 128
GDN_HEADS = 4
GDN_CHUNKS_PER_STEP = 4
CHUNK = 64
CONV_WIDTH = 4
LANES = 128
SB_BLOCK = 128
SB_BQ = 256
VMEM_LIMIT_BYTES = 56 * 1024 * 1024

ADAM_LR = 0.001
ADAM_B1 = 0.9
ADAM_B2 = 0.999
ADAM_EPS = 1e-08
ADAM_WD = 0.01
ADAM_STEP = 10

_NN = (((1,), (0,)), ((), ()))
_NT = (((1,), (1,)), ((), ()))
_TN = (((0,), (0,)), ((), ()))
_BNN = (((2,), (1,)), ((0,), (0,)))
_BNT = (((2,), (2,)), ((0,), (0,)))
_BTN = (((1,), (1,)), ((0,), (0,)))


def _mm(a, b):
    return jnp.dot(a.astype(MXU_DTYPE), b.astype(MXU_DTYPE), preferred_element_type=F32)


def _mm_nt(a, b):
    return lax.dot_general(a.astype(MXU_DTYPE), b.astype(MXU_DTYPE), _NT, preferred_element_type=F32)


def _mm_tn(a, b):
    return lax.dot_general(a.astype(MXU_DTYPE), b.astype(MXU_DTYPE), _TN, preferred_element_type=F32)


def _mx(a, b):
    return jnp.dot(a, b, precision=EXACT, preferred_element_type=F32)


def _mx_nt(a, b):
    return lax.dot_general(a, b, _NT, precision=EXACT, preferred_element_type=F32)


def _mx_tn(a, b):
    return lax.dot_general(a, b, _TN, precision=EXACT, preferred_element_type=F32)


def _split(x):
    hi = x.astype(MXU_DTYPE)
    return hi, (x - hi.astype(F32)).astype(MXU_DTYPE)


def _m3_general(a, b, dims):
    ah, al = _split(a)
    bh, bl = _split(b)
    dot = lambda x, y: lax.dot_general(x, y, dims, preferred_element_type=F32)
    return dot(ah, bh) + (dot(ah, bl) + dot(al, bh))


def _m3(a, b):
    return _m3_general(a, b, _NN)


def _m3_nt(a, b):
    return _m3_general(a, b, _NT)


def _m3_tn(a, b):
    return _m3_general(a, b, _TN)


def _sigmoid(z):
    return 1.0 / (1.0 + jnp.exp(-z))


def _softplus(z):
    return jnp.maximum(z, 0.0) + jnp.log(1.0 + jnp.exp(-jnp.abs(z)))


def _params(*semantics):
    return pltpu.CompilerParams(dimension_semantics=semantics, vmem_limit_bytes=VMEM_LIMIT_BYTES)


def _inproj_call(x, norm_w, w_main, w_small, w_small_t, tm=256):
    t_len, d = x.shape
    n = w_main.shape[1]
    ns = w_small.shape[1]
    nst = w_small_t.shape[0]

    def body(x_ref, nw_ref, wm_ref, ws_ref, wst_ref, pm_ref, ps_ref, pst_ref, ht_ref, r_ref):
        xv = x_ref[...]
        r = lax.rsqrt(jnp.mean(xv * xv, axis=-1, keepdims=True) + EPS)
        h = xv * r * nw_ref[...]
        hb = h.astype(MXU_DTYPE)
        for n0 in range(0, n, 512):
            pm_ref[:, n0:n0 + 512] = jnp.dot(hb, wm_ref[:, n0:n0 + 512], preferred_element_type=F32)
        ps_ref[...] = jnp.dot(hb, ws_ref[...], preferred_element_type=F32)
        pst_ref[...] = lax.dot_general(wst_ref[...], hb, _NT, preferred_element_type=F32)
        ht_ref[...] = h.T.astype(MXU_DTYPE)
        r_ref[...] = r

    return pl.pallas_call(
        body, name="inproj",
        grid=(t_len // tm,),
        in_specs=[pl.BlockSpec((tm, d), lambda i: (i, 0)),
                  pl.BlockSpec((1, d), lambda i: (0, 0)),
                  pl.BlockSpec((d, n), lambda i: (0, 0)),
                  pl.BlockSpec((d, ns), lambda i: (0, 0)),
                  pl.BlockSpec((nst, d), lambda i: (0, 0))],
        out_specs=[pl.BlockSpec((tm, n), lambda i: (i, 0)),
                   pl.BlockSpec((tm, ns), lambda i: (i, 0)),
                   pl.BlockSpec((nst, tm), lambda i: (0, i)),
                   pl.BlockSpec((d, tm), lambda i: (0, i)),
                   pl.BlockSpec((tm, 1), lambda i: (i, 0))],
        out_shape=[jax.ShapeDtypeStruct((t_len, n), F32),
                   jax.ShapeDtypeStruct((t_len, ns), F32),
                   jax.ShapeDtypeStruct((nst, t_len), F32),
                   jax.ShapeDtypeStruct((d, t_len), MXU_DTYPE),
                   jax.ShapeDtypeStruct((t_len, 1), F32)],
        compiler_params=_params("arbitrary"),
    )(x, norm_w, w_main, w_small, w_small_t)


def _running_sum_mm(x, tri):
    hi = x.astype(MXU_DTYPE)
    lo = (x - hi.astype(F32)).astype(MXU_DTYPE)
    return jnp.dot(hi, tri, preferred_element_type=F32) + jnp.dot(lo, tri, preferred_element_type=F32)


def _sb_iotas():
    row_i = lax.broadcasted_iota(jnp.int32, (SB_BQ, SB_BLOCK), 0)
    col_i = lax.broadcasted_iota(jnp.int32, (SB_BQ, SB_BLOCK), 1)
    sq_r = lax.broadcasted_iota(jnp.int32, (SB_BLOCK, SB_BLOCK), 0)
    sq_c = lax.broadcasted_iota(jnp.int32, (SB_BLOCK, SB_BLOCK), 1)
    return row_i, col_i, sq_r, sq_c


SB_DIAG_BLOCKS = SB_BQ // SB_BLOCK
SB_EXP_FLOOR = -110.0


def _sb_keys_descending(qi, tile, carry, z_bounds, n_heads):
    n_free = SB_DIAG_BLOCKS * qi
    for j in range(SB_DIAG_BLOCKS - 1, -1, -1):
        carry = tile(n_free + j, True, carry)

    def largest_exponent(c):
        worst = jnp.max(z_bounds[0] - c[1])
        for h in range(1, n_heads):
            worst = jnp.maximum(worst, jnp.max(z_bounds[h] - c[1 + h]))
        return worst

    def cond(state):
        return (state[0] < n_free) & (state[1] > SB_EXP_FLOOR)

    def body(state):
        c = tile(n_free - 1 - state[0], False, state[2:])
        return (state[0] + 1, largest_exponent(c), *c)

    out = lax.while_loop(cond, body, (jnp.int32(0), largest_exponent(carry), *carry))
    return out[2:], out[0]


def _sb_keys_ascending(qi, n_run, tile, carry):
    n_free = SB_DIAG_BLOCKS * qi
    carry = lax.fori_loop(0, n_run, lambda s, c: tile(n_free - n_run + s, False, c), carry)
    for j in range(SB_DIAG_BLOCKS):
        carry = tile(n_free + j, True, carry)
    return carry


def _sb_fwd_call(proj, t_len):
    nq = t_len // SB_BQ
    scale = float(SB_HEAD_DIM) ** -0.5
    n_pairs = 512 // LANES
    per_pair = LANES // SB_HEAD_DIM

    def body(q_ref, k_ref, v_ref, o_ref, st_ref, nrun_ref):
        lane = lax.broadcasted_iota(jnp.int32, (1, LANES), 1)
        row_i, col_i, sq_r, sq_c = _sb_iotas()
        ge = (sq_r >= sq_c).astype(MXU_DTYPE)
        hms = [((lane // SB_HEAD_DIM) == hh).astype(F32) for hh in range(per_pair)]
        k_sq = k_ref[...] * k_ref[...]
        k_norms = [jnp.sqrt(jnp.max(jnp.sum(k_sq * hm, axis=-1, keepdims=True))) * (1.02 * scale) for hm in hms]

        def q_loop(qi, carry):
            r0 = pl.multiple_of(qi * SB_BQ, SB_BQ)
            rows = pl.ds(r0, SB_BQ)
            q_all = q_ref[rows, :]
            qms = [(q_all * hm).astype(MXU_DTYPE) for hm in hms]
            z_bounds = [jnp.sqrt(jnp.sum(q_all * q_all * hm, axis=-1, keepdims=True)) * kn
                        for hm, kn in zip(hms, k_norms)]

            def tile(kj, masked, kc):
                acc, cs = kc[0], list(kc[1:])
                s0 = pl.multiple_of(kj * SB_BLOCK, SB_BLOCK)
                cols = pl.ds(s0, SB_BLOCK)
                kb = k_ref[cols, :].astype(MXU_DTYPE)
                v_all = v_ref[cols, :]
                mask = (col_i + s0) < (row_i + r0) if masked else None
                for hh in range(per_pair):
                    z = lax.dot_general(qms[hh], kb, _NT, preferred_element_type=F32) * scale
                    sp = _softplus(z)
                    if masked:
                        sp = jnp.where(mask, sp, 0.0)
                    a = jnp.exp(z - (_running_sum_mm(sp, ge) + cs[hh]))
                    if masked:
                        a = jnp.where(mask, a, 0.0)
                    vm = (v_all * hms[hh]).astype(MXU_DTYPE)
                    acc = acc + jnp.dot(a.astype(MXU_DTYPE), vm, preferred_element_type=F32)
                    cs[hh] = cs[hh] + jnp.sum(sp, axis=-1, keepdims=True)
                return (acc, *cs)

            zero_col = jnp.zeros((SB_BQ, 1), F32)
            out, n_run = _sb_keys_descending(
                qi, tile, (jnp.zeros((SB_BQ, LANES), F32),) + (zero_col,) * per_pair, z_bounds, per_pair)
            o_ref[rows, :] = out[0]
            for hh in range(per_pair):
                st_ref[hh, rows, :] = out[1 + hh]
            nrun_ref[pl.program_id(0), qi] = n_run
            return carry

        lax.fori_loop(0, nq, q_loop, 0)

    return pl.pallas_call(
        body, name="sb_fwd",
        grid=(n_pairs,),
        in_specs=[pl.BlockSpec((t_len, LANES), lambda p: (0, p)),
                  pl.BlockSpec((t_len, LANES), lambda p: (0, n_pairs + p)),
                  pl.BlockSpec((t_len, LANES), lambda p: (0, 2 * n_pairs + p))],
        out_specs=[pl.BlockSpec((t_len, LANES), lambda p: (0, p)),
                   pl.BlockSpec((per_pair, t_len, 1), lambda p: (p, 0, 0)),
                   pl.BlockSpec(memory_space=pltpu.SMEM)],
        out_shape=[jax.ShapeDtypeStruct((t_len, 512), F32),
                   jax.ShapeDtypeStruct((n_pairs * per_pair, t_len, 1), F32),
                   jax.ShapeDtypeStruct((n_pairs, nq), jnp.int32)],
        compiler_params=_params("arbitrary"),
    )(proj, proj, proj)


def _sb_bwd_call(proj, sp_total, n_run_all, d_o, t_len):
    nq = t_len // SB_BQ
    scale = float(SB_HEAD_DIM) ** -0.5
    n_pairs = 512 // LANES
    per_pair = LANES // SB_HEAD_DIM

    def body(q_ref, k_ref, v_ref, st_ref, nrun_ref, do_ref, d_ref):
        lane = lax.broadcasted_iota(jnp.int32, (1, LANES), 1)
        row_i, col_i, sq_r, sq_c = _sb_iotas()
        lt = (sq_r < sq_c).astype(MXU_DTYPE)
        le = (sq_r <= sq_c).astype(MXU_DTYPE)
        hms = [((lane // SB_HEAD_DIM) == hh).astype(F32) for hh in range(per_pair)]
        d_ref[1] = jnp.zeros((t_len, LANES), F32)
        d_ref[2] = jnp.zeros((t_len, LANES), F32)

        def q_loop(qi, carry):
            r0 = pl.multiple_of(qi * SB_BQ, SB_BQ)
            rows = pl.ds(r0, SB_BQ)
            q_all, do_all = q_ref[rows, :], do_ref[rows, :]
            qms = [(q_all * hm).astype(MXU_DTYPE) for hm in hms]
            doms = [(do_all * hm).astype(MXU_DTYPE) for hm in hms]
            totals = [st_ref[hh, rows, :] for hh in range(per_pair)]

            def tile(kj, masked, kc):
                dq, cls, gls = kc[0], list(kc[1:1 + per_pair]), list(kc[1 + per_pair:])
                s0 = pl.multiple_of(kj * SB_BLOCK, SB_BLOCK)
                cols = pl.ds(s0, SB_BLOCK)
                k_all, v_all = k_ref[cols, :], v_ref[cols, :]
                kb = k_all.astype(MXU_DTYPE)
                mask = (col_i + s0) < (row_i + r0) if masked else None
                dk_t = jnp.zeros((SB_BLOCK, LANES), F32)
                dv_t = jnp.zeros((SB_BLOCK, LANES), F32)
                for hh in range(per_pair):
                    z = lax.dot_general(qms[hh], kb, _NT, preferred_element_type=F32) * scale
                    sp_all = _softplus(z)
                    sp = jnp.where(mask, sp_all, 0.0) if masked else sp_all
                    a = jnp.exp(z - (totals[hh] - cls[hh] - _running_sum_mm(sp, lt)))
                    if masked:
                        a = jnp.where(mask, a, 0.0)
                    vm = (v_all * hms[hh]).astype(MXU_DTYPE)
                    km = (k_all * hms[hh]).astype(MXU_DTYPE)
                    g = lax.dot_general(doms[hh], vm, _NT, preferred_element_type=F32) * a
                    dz = g - jnp.exp(z - sp_all) * (gls[hh] + _running_sum_mm(g, le))
                    if masked:
                        dz = jnp.where(mask, dz, 0.0)
                    dz = (dz * scale).astype(MXU_DTYPE)
                    dq = dq + jnp.dot(dz, km, preferred_element_type=F32)
                    dk_t = dk_t + lax.dot_general(dz, qms[hh], _TN, preferred_element_type=F32)
                    dv_t = dv_t + lax.dot_general(a.astype(MXU_DTYPE), doms[hh], _TN, preferred_element_type=F32)
                    cls[hh] = cls[hh] + jnp.sum(sp, axis=-1, keepdims=True)
                    gls[hh] = gls[hh] + jnp.sum(g, axis=-1, keepdims=True)
                d_ref[1, cols, :] += dk_t
                d_ref[2, cols, :] += dv_t
                return (dq, *cls, *gls)

            zero_col = jnp.zeros((SB_BQ, 1), F32)
            out = _sb_keys_ascending(qi, nrun_ref[pl.program_id(0), qi], tile,
                                     (jnp.zeros((SB_BQ, LANES), F32),) + (zero_col,) * (2 * per_pair))
            d_ref[0, rows, :] = out[0]
            return carry

        lax.fori_loop(0, nq, q_loop, 0)

    col = lambda off: pl.BlockSpec((t_len, LANES), lambda p: (0, off + p))
    return pl.pallas_call(
        body, name="sb_bwd",
        grid=(n_pairs,),
        in_specs=[col(0), col(n_pairs), col(2 * n_pairs),
                  pl.BlockSpec((per_pair, t_len, 1), lambda p: (p, 0, 0)),
                  pl.BlockSpec(memory_space=pltpu.SMEM), col(0)],
        out_specs=pl.BlockSpec((3, t_len, LANES), lambda p: (0, 0, p)),
        out_shape=jax.ShapeDtypeStruct((3, t_len, 512), F32),
        compiler_params=_params("arbitrary"),
    )(proj, proj, proj, sp_total, n_run_all, d_o)


def _conv_taps(xin, rows, t_len):
    taps = []
    for i in range(CONV_WIDTH):
        shift = CONV_WIDTH - 1 - i
        if shift == 0:
            taps.append(xin)
        else:
            taps.append(jnp.where(rows >= shift, pltpu.roll(xin, shift, axis=0), 0.0))
    return taps


def _gdn_prep_body_common(x_ref, w_ref, t_len):
    j = pl.program_id(0)
    xin = x_ref[...]
    rows = lax.broadcasted_iota(jnp.int32, (t_len, LANES), 0)
    taps = _conv_taps(xin, rows, t_len)
    pre = taps[0] * w_ref[0:1, :]
    for i in range(1, CONV_WIDTH):
        pre = pre + taps[i] * w_ref[i:i + 1, :]
    sg = _sigmoid(pre)
    act = pre * sg
    is_qk = j < 2 * GDN_HEADS
    nrm = jnp.where(is_qk, lax.rsqrt(jnp.sum(act * act, axis=-1, keepdims=True) + EPS), 1.0)
    sc = jnp.where(j < GDN_HEADS, float(GDN_HEAD_DIM) ** -0.5, 1.0)
    return j, rows, taps, pre, sg, act, is_qk, nrm, sc


def _gdn_prep_call(proj, conv_w, t_len):
    first = 2048 // LANES

    def body(x_ref, w_ref, out_ref):
        _, _, _, _, _, act, _, nrm, sc = _gdn_prep_body_common(x_ref, w_ref, t_len)
        out_ref[...] = act * nrm * sc

    return pl.pallas_call(
        body, name="gdn_prep",
        grid=(3 * GDN_HEADS,),
        in_specs=[pl.BlockSpec((t_len, LANES), lambda j: (0, first + j)),
                  pl.BlockSpec((CONV_WIDTH, LANES), lambda j: (0, j))],
        out_specs=pl.BlockSpec((t_len, LANES), lambda j: (0, j)),
        out_shape=jax.ShapeDtypeStruct((t_len, 3 * 512), F32),
        compiler_params=_params("arbitrary"),
    )(proj, conv_w)


def _gdn_prep_bwd_call(proj, conv_w, d_act3, t_len):
    first = 2048 // LANES

    def body(x_ref, w_ref, d_ref, dx_ref, dw_ref):
        _, rows, taps, pre, sg, act, is_qk, nrm, sc = _gdn_prep_body_common(x_ref, w_ref, t_len)
        d_out = d_ref[0]
        dn = d_out * sc
        d_norm = nrm * dn - act * (nrm * nrm * nrm) * jnp.sum(dn * act, axis=-1, keepdims=True)
        d_act = jnp.where(is_qk, d_norm, d_out)
        d_pre = d_act * sg * (1.0 + pre * (1.0 - sg))
        dx = d_pre * w_ref[CONV_WIDTH - 1:CONV_WIDTH, :]
        dw_ref[CONV_WIDTH - 1:CONV_WIDTH, :] = jnp.sum(d_pre * taps[CONV_WIDTH - 1], axis=0, keepdims=True)
        for i in range(CONV_WIDTH - 1):
            shift = CONV_WIDTH - 1 - i
            up = jnp.where(rows < t_len - shift, pltpu.roll(d_pre, t_len - shift, axis=0), 0.0)
            dx = dx + up * w_ref[i:i + 1, :]
            dw_ref[i:i + 1, :] = jnp.sum(d_pre * taps[i], axis=0, keepdims=True)
        dx_ref[0] = dx

    return pl.pallas_call(
        body, name="gdn_prep_bwd",
        grid=(3 * GDN_HEADS,),
        in_specs=[pl.BlockSpec((t_len, LANES), lambda j: (0, first + j)),
                  pl.BlockSpec((CONV_WIDTH, LANES), lambda j: (0, j)),
                  pl.BlockSpec((1, t_len, LANES), lambda j: (j // GDN_HEADS, 0, j % GDN_HEADS))],
        out_specs=[pl.BlockSpec((1, t_len, LANES), lambda j: (j // GDN_HEADS, 0, j % GDN_HEADS)),
                   pl.BlockSpec((CONV_WIDTH, LANES), lambda j: (0, j))],
        out_shape=[jax.ShapeDtypeStruct((3, t_len, 512), F32),
                   jax.ShapeDtypeStruct((CONV_WIDTH, 3 * 512), F32)],
        compiler_params=_params("arbitrary"),
    )(proj, conv_w, d_act3)


def _chunk_cumsum_matrix():
    r = lax.broadcasted_iota(jnp.int32, (LANES, LANES), 0)
    c = lax.broadcasted_iota(jnp.int32, (LANES, LANES), 1)
    return ((r <= c) & ((r // CHUNK) == (c // CHUNK))).astype(F32)


def _gdn_gates_call(ps, pst, alog_l, dtb_l, alog_c, dtb_c, t_len):
    def body(ps_ref, pst_ref, al_ref, dl_ref, ac_ref, dc_ref, beta_ref, gcol_ref, grow_ref):
        upper = _chunk_cumsum_matrix()
        lower = upper.T
        psv = ps_ref[...]
        beta_ref[...] = _sigmoid(psv)
        g_l = -jnp.exp(al_ref[...]) * _softplus(psv + dl_ref[...])
        g_r = -jnp.exp(ac_ref[...]) * _softplus(pst_ref[...] + dc_ref[...])
        for w in range(t_len // LANES):
            sl = slice(w * LANES, (w + 1) * LANES)
            gcol_ref[sl, :] = _mx(lower, g_l[sl, :])
            grow_ref[:, sl] = _mx(g_r[:, sl], upper)

    vm = pl.BlockSpec(memory_space=pltpu.VMEM)
    return pl.pallas_call(
        body, name="gdn_gates",
        in_specs=[vm] * 6, out_specs=[vm] * 3,
        out_shape=[jax.ShapeDtypeStruct((t_len, LANES), F32),
                   jax.ShapeDtypeStruct((t_len, LANES), F32),
                   jax.ShapeDtypeStruct((8, t_len), F32)],
        compiler_params=pltpu.CompilerParams(vmem_limit_bytes=VMEM_LIMIT_BYTES),
    )(ps, pst, alog_l, dtb_l, alog_c, dtb_c)


def _gdn_gates_bwd_call(ps, alog_l, dtb_l, d_l, t_len):
    def body(ps_ref, al_ref, dl_ref, d_ref, dps_ref, gal_ref, gdt_ref):
        lane = lax.broadcasted_iota(jnp.int32, (1, LANES), 1)
        psv = ps_ref[...]
        dv = d_ref[...]
        beta = _sigmoid(psv)
        ea = jnp.exp(al_ref[...])
        arg = psv + dl_ref[...]
        g = -ea * _softplus(arg)
        d_a = dv * (-ea) * _sigmoid(arg)
        is_a = (lane >= GDN_HEADS) & (lane < 2 * GDN_HEADS)
        dps_ref[...] = jnp.where(lane < GDN_HEADS, dv * beta * (1.0 - beta), jnp.where(is_a, d_a, 0.0))
        gdt_ref[...] = jnp.where(is_a, jnp.sum(d_a, axis=0, keepdims=True), 0.0)
        gal_ref[...] = jnp.where(is_a, jnp.sum(dv * g, axis=0, keepdims=True), 0.0)

    vm = pl.BlockSpec(memory_space=pltpu.VMEM)
    return pl.pallas_call(
        body, name="gdn_gates_bwd",
        in_specs=[vm] * 4, out_specs=[vm] * 3,
        out_shape=[jax.ShapeDtypeStruct((t_len, LANES), F32),
                   jax.ShapeDtypeStruct((1, LANES), F32),
                   jax.ShapeDtypeStruct((1, LANES), F32)],
        compiler_params=pltpu.CompilerParams(vmem_limit_bytes=VMEM_LIMIT_BYTES),
    )(ps, alog_l, dtb_l, d_l)


def _bm(a, b):
    return _m3_general(a, b, _BNN)


def _bm_nt(a, b):
    return _m3_general(a, b, _BNT)


def _bm_tn(a, b):
    return _m3_general(a, b, _BTN)


def _heads_of(ref, rows):
    return jnp.stack([ref[rows, h * GDN_HEAD_DIM:(h + 1) * GDN_HEAD_DIM] for h in range(GDN_HEADS)])


def _chunk_terms(q_ref, k_ref, v_ref, b_ref, gc_ref, gr_ref, c, incl, strict):
    r0 = pl.multiple_of(c * CHUNK, CHUNK)
    rows = pl.ds(r0, CHUNK)
    q, k, v = _heads_of(q_ref, rows), _heads_of(k_ref, rows), _heads_of(v_ref, rows)
    lane_ids = lax.broadcasted_iota(jnp.int32, (1, LANES), 1)
    pick = lambda slab, first: jnp.stack([jnp.sum(jnp.where(lane_ids == first + h, slab, 0.0), axis=-1, keepdims=True)
                                          for h in range(GDN_HEADS)])
    b = pick(b_ref[rows, :], 0)
    gc = pick(gc_ref[rows, :], GDN_HEADS)
    gr = gr_ref[:, c]
    dm = jnp.where(incl, jnp.exp(jnp.where(incl, gc - gr, 0.0)), 0.0)
    kb = k * b
    vb = v * b
    e = jnp.exp(gc)
    a = jnp.where(strict, _bm_nt(kb, k) * dm, 0.0)
    p = jnp.where(incl, _bm_nt(q, k) * dm, 0.0)
    gl = gc[:, CHUNK - 1:CHUNK, :]
    eg = jnp.exp(gl - gc)
    return rows, q, k, v, b, gc, dm, kb, vb, e, a, p, gl, eg


def _gdn_specs(t_len, n_chunks, reverse):
    cps = GDN_CHUNKS_PER_STEP
    steps = n_chunks // cps
    at = (lambda g: steps - 1 - g) if reverse else (lambda g: g)
    rows_blk = lambda width, part=0: pl.BlockSpec((cps * CHUNK, width), lambda g: (at(g), part))
    gate_r = pl.BlockSpec((GDN_HEADS, cps, 1, CHUNK), lambda g: (0, at(g), 0, 0))
    per_chunk = lambda r, c: pl.BlockSpec((GDN_HEADS, cps, r, c), lambda g: (0, at(g), 0, 0))
    return cps, steps, rows_blk, gate_r, per_chunk


def _gdn_fwd_call(gact, beta_c, gam_c, gam_r, t_len):
    n_chunks = t_len // CHUNK
    dk = GDN_HEAD_DIM
    width = GDN_HEADS * dk
    cps, steps, rows_blk, gate_r, per_chunk = _gdn_specs(t_len, n_chunks, False)

    def body(q_ref, k_ref, v_ref, b_ref, gc_ref, gr_ref, o_ref, s_ref, t_ref, state_ref):
        row = lax.broadcasted_iota(jnp.int32, (CHUNK, CHUNK), 0)
        col = lax.broadcasted_iota(jnp.int32, (CHUNK, CHUNK), 1)
        incl, strict = row >= col, row > col
        eye = (row == col).astype(F32)

        @pl.when(pl.program_id(0) == 0)
        def _():
            state_ref[...] = jnp.zeros_like(state_ref)

        def chunk(c, carry):
            rows, q, k, v, b, gc, dm, kb, vb, e, a, p, gl, eg = _chunk_terms(
                q_ref, k_ref, v_ref, b_ref, gc_ref, gr_ref, c, incl, strict)
            s = state_ref[...]
            x = -a
            tm = eye + x
            xp = x
            for _ in range(5):
                xp = _bm(xp, xp)
                tm = tm + _bm(tm, xp)
            u = _bm(tm, vb)
            w = _bm(tm, kb * e)
            vn = u - _bm(w, s)
            o = _bm(q * e, s) + _bm(p, vn)
            for h in range(GDN_HEADS):
                o_ref[rows, h * dk:(h + 1) * dk] = o[h]
            s_ref[:, c] = s
            t_ref[:, c] = tm
            state_ref[...] = s * jnp.exp(gl) + _bm_tn(k * eg, vn)
            return carry

        lax.fori_loop(0, cps, chunk, 0)

    return pl.pallas_call(
        body, name="gdn_fwd",
        grid=(steps,),
        in_specs=[rows_blk(width, 0), rows_blk(width, 1), rows_blk(width, 2), rows_blk(LANES), rows_blk(LANES), gate_r],
        out_specs=[rows_blk(width), per_chunk(dk, dk), per_chunk(CHUNK, CHUNK)],
        out_shape=[jax.ShapeDtypeStruct((t_len, width), F32),
                   jax.ShapeDtypeStruct((GDN_HEADS, n_chunks, dk, dk), F32),
                   jax.ShapeDtypeStruct((GDN_HEADS, n_chunks, CHUNK, CHUNK), F32)],
        scratch_shapes=[pltpu.VMEM((GDN_HEADS, dk, dk), F32)],
        compiler_params=_params("arbitrary"),
    )(gact, gact, gact, beta_c, gam_c, gam_r)


def _gdn_bwd_call(gact, beta_c, gam_c, gam_r, s_all, t_all, d_o, t_len):
    n_chunks = t_len // CHUNK
    dk = GDN_HEAD_DIM
    width = GDN_HEADS * dk
    cps, steps, rows_blk, gate_r, per_chunk = _gdn_specs(t_len, n_chunks, True)

    def body(q_ref, k_ref, v_ref, b_ref, gc_ref, gr_ref, s_ref, t_ref, do_ref, d_ref, dgate_ref, dstate_ref):
        row = lax.broadcasted_iota(jnp.int32, (CHUNK, CHUNK), 0)
        col = lax.broadcasted_iota(jnp.int32, (CHUNK, CHUNK), 1)
        incl, strict = row >= col, row > col
        upper = jnp.broadcast_to((row <= col).astype(F32), (GDN_HEADS, CHUNK, CHUNK))
        ones = jnp.ones((GDN_HEADS, CHUNK, LANES), F32)
        last_row = lax.broadcasted_iota(jnp.int32, (CHUNK, 1), 0) == CHUNK - 1
        lane_ids = lax.broadcasted_iota(jnp.int32, (1, LANES), 1)
        rsum = lambda m: jnp.sum(m, axis=-1, keepdims=True)
        total = lambda m: jnp.sum(rsum(m), axis=1, keepdims=True)

        @pl.when(pl.program_id(0) == 0)
        def _():
            dstate_ref[...] = jnp.zeros_like(dstate_ref)

        def chunk(step, carry):
            c = cps - 1 - step
            rows, q, k, v, b, gc, dm, kb, vb, e, a, p, gl, eg = _chunk_terms(
                q_ref, k_ref, v_ref, b_ref, gc_ref, gr_ref, c, incl, strict)
            ds = dstate_ref[...]
            s = s_ref[:, c]
            tm = t_ref[:, c]
            d_out = _heads_of(do_ref, rows)
            el = jnp.exp(gl)
            kbe = kb * e
            u = _bm(tm, vb)
            w = _bm(tm, kbe)
            vn = u - _bm(w, s)
            qe = q * e
            kd = k * eg

            d_vn = _bm_tn(p, d_out) + _bm(kd, ds)
            d_qe = _bm_nt(d_out, s)
            d_p = jnp.where(incl, _bm_nt(d_out, vn), 0.0)
            dstate_ref[...] = el * ds + _bm_tn(qe, d_out) - _bm_tn(w, d_vn)
            d_kd = _bm_nt(vn, ds)
            d_w = -_bm_nt(d_vn, s)
            d_vb = _bm_tn(tm, d_vn)
            d_kbe = _bm_tn(tm, d_w)
            d_a = -jnp.where(strict, _bm_nt(d_vb, u) + _bm_nt(d_kbe, w), 0.0)
            m = d_a * dm
            n = d_p * dm
            d_kb = _bm(m, k) + d_kbe * e
            d_q = _bm(n, k) + d_qe * e
            d_k = _bm_tn(m, kb) + _bm_tn(n, q) + d_kd * eg + b * d_kb
            d_v = b * d_vb
            r = d_a * a + d_p * p
            kd_term = rsum(d_kd * kd)
            d_gl = total(ds * s) * el + jnp.sum(kd_term, axis=1, keepdims=True)
            d_gam = (rsum(r) - _bm_tn(r, ones)[:, :, 0:1] + rsum(d_qe * qe) + rsum(d_kbe * kbe) - kd_term
                     + jnp.where(last_row, d_gl, 0.0))
            d_beta = rsum(d_kb * k) + rsum(d_vb * v)
            d_g = _bm(upper, d_gam * ones)[:, :, 0:1]
            gates = jnp.zeros((CHUNK, LANES), F32)
            for h in range(GDN_HEADS):
                lanes = slice(h * dk, (h + 1) * dk)
                d_ref[0, rows, lanes] = d_q[h]
                d_ref[1, rows, lanes] = d_k[h]
                d_ref[2, rows, lanes] = d_v[h]
                gates = gates + (jnp.where(lane_ids == h, d_beta[h], 0.0)
                                 + jnp.where(lane_ids == GDN_HEADS + h, d_g[h], 0.0))
            dgate_ref[rows, :] = gates
            return carry

        lax.fori_loop(0, cps, chunk, 0)

    d_spec = pl.BlockSpec((3, cps * CHUNK, width), lambda g: (0, steps - 1 - g, 0))
    return pl.pallas_call(
        body, name="gdn_bwd",
        grid=(steps,),
        in_specs=[rows_blk(width, 0), rows_blk(width, 1), rows_blk(width, 2), rows_blk(LANES), rows_blk(LANES), gate_r,
                  per_chunk(dk, dk), per_chunk(CHUNK, CHUNK), rows_blk(width)],
        out_specs=[d_spec, rows_blk(LANES)],
        out_shape=[jax.ShapeDtypeStruct((3, t_len, width), F32),
                   jax.ShapeDtypeStruct((t_len, LANES), F32)],
        scratch_shapes=[pltpu.VMEM((GDN_HEADS, dk, dk), F32)],
        compiler_params=_params("arbitrary"),
    )(gact, gact, gact, beta_c, gam_c, gam_r, s_all, t_all, d_o)


def _group_matrix(width, group):
    r = lax.broadcasted_iota(jnp.int32, (width, width), 0)
    c = lax.broadcasted_iota(jnp.int32, (width, width), 1)
    return ((r // group) == (c // group)).astype(F32)


def _post_call(o_sb, o_gd, proj, x, target, w_out, sbw, gdw, fw, tm=256):
    t_len, d = x.shape
    half = 512
    zsb_blk = 1536 // half
    zgd_blk = 3584 // half

    def body(osb_ref, ogd_ref, zsb_ref, zgd_ref, x_ref, tg_ref, wo_ref, sbw_ref, gdw_ref, fw_ref,
             dx2_ref, dosb_ref, dzsb_ref, dogd_ref, dzgd_ref, loss_ref, gfw_ref, gsb_ref, ggd_ref, gwo_ref):
        step = pl.program_id(0)

        @pl.when(step == 0)
        def _():
            loss_ref[...] = jnp.zeros_like(loss_ref)
            gfw_ref[...] = jnp.zeros_like(gfw_ref)
            gsb_ref[...] = jnp.zeros_like(gsb_ref)
            ggd_ref[...] = jnp.zeros_like(ggd_ref)
            gwo_ref[...] = jnp.zeros_like(gwo_ref)

        def head_forward(o, z, w, gmat, inv):
            r = lax.rsqrt(_mx(o * o, gmat) * inv + EPS)
            nrm = o * r * w
            sg = _sigmoid(z)
            return r, nrm, sg, nrm * (z * sg)

        def head_backward(d_m, o, z, w, gmat, inv, r, nrm, sg):
            d_n = d_m * (z * sg)
            d_z = d_m * nrm * (sg * (1.0 + z * (1.0 - sg)))
            dnw = d_n * w
            d_o = r * dnw - o * (r * r * r) * (_mx(dnw * o, gmat) * inv)
            return d_o, d_z, jnp.sum(d_n * o * r, axis=0, keepdims=True)

        g_sb = _group_matrix(half, SB_HEAD_DIM)
        g_gd = _group_matrix(half, GDN_HEAD_DIM)
        osb, ogd, zsb, zgd = osb_ref[...], ogd_ref[...], zsb_ref[...], zgd_ref[...]
        sbw_v, gdw_v = sbw_ref[...], gdw_ref[...]
        r_sb, n_sb, sg_sb, m_sb = head_forward(osb, zsb, sbw_v, g_sb, 1.0 / SB_HEAD_DIM)
        r_gd, n_gd, sg_gd, m_gd = head_forward(ogd, zgd, gdw_v, g_gd, 1.0 / GDN_HEAD_DIM)
        mixed = jnp.concatenate([m_sb, m_gd], axis=1).astype(MXU_DTYPE)
        wo = wo_ref[...]
        x2 = x_ref[...] + jnp.dot(mixed, wo, preferred_element_type=F32)
        r2 = lax.rsqrt(jnp.mean(x2 * x2, axis=-1, keepdims=True) + EPS)
        fw_v = fw_ref[...]
        err = x2 * r2 * fw_v - tg_ref[...]
        loss_ref[...] += 0.5 * jnp.sum(jnp.sum(err * err, axis=-1, keepdims=True) * (1.0 / d))
        dy = err * (1.0 / d)
        gg = dy * fw_v
        dx2 = r2 * gg - x2 * ((r2 * r2 * r2) * jnp.mean(gg * x2, axis=-1, keepdims=True))
        gfw_ref[...] += jnp.sum(dy * x2 * r2, axis=0, keepdims=True)
        dx2_ref[...] = dx2
        dx2b = dx2.astype(MXU_DTYPE)
        d_mixed = lax.dot_general(dx2b, wo, _NT, preferred_element_type=F32)
        gwo_ref[...] += lax.dot_general(mixed, dx2b, _TN, preferred_element_type=F32)
        d_osb, d_zsb, gsb = head_backward(d_mixed[:, :half], osb, zsb, sbw_v, g_sb, 1.0 / SB_HEAD_DIM, r_sb, n_sb, sg_sb)
        d_ogd, d_zgd, ggd = head_backward(d_mixed[:, half:], ogd, zgd, gdw_v, g_gd, 1.0 / GDN_HEAD_DIM, r_gd, n_gd, sg_gd)
        dosb_ref[...] = d_osb
        dzsb_ref[...] = d_zsb
        dogd_ref[...] = d_ogd
        dzgd_ref[...] = d_zgd
        gsb_ref[...] += gsb
        ggd_ref[...] += ggd

    row_blk = lambda w: pl.BlockSpec((tm, w), lambda i: (i, 0))
    fixed = lambda r, w: pl.BlockSpec((r, w), lambda i: (0, 0))
    return pl.pallas_call(
        body, name="post",
        grid=(t_len // tm,),
        in_specs=[row_blk(half), row_blk(half),
                  pl.BlockSpec((tm, half), lambda i: (i, zsb_blk)),
                  pl.BlockSpec((tm, half), lambda i: (i, zgd_blk)),
                  row_blk(d), row_blk(d), fixed(d, d), fixed(1, half), fixed(1, half), fixed(1, d)],
        out_specs=[row_blk(d), row_blk(half), row_blk(half), row_blk(half), row_blk(half),
                   fixed(1, LANES), fixed(1, d), fixed(1, half), fixed(1, half), fixed(d, d)],
        out_shape=[jax.ShapeDtypeStruct((t_len, d), F32)] + [jax.ShapeDtypeStruct((t_len, half), F32)] * 4
                  + [jax.ShapeDtypeStruct((1, LANES), F32), jax.ShapeDtypeStruct((1, d), F32),
                     jax.ShapeDtypeStruct((1, half), F32), jax.ShapeDtypeStruct((1, half), F32),
                     jax.ShapeDtypeStruct((d, d), F32)],
        compiler_params=_params("arbitrary"),
    )(o_sb, o_gd, proj, proj, x, target, w_out, sbw, gdw, fw)


def _gw_in_call(h_t, dproj8, tm=512):
    d, t_len = h_t.shape
    n_piece, _, pw = dproj8.shape

    def body(ht_ref, dp_ref, gw_ref):
        @pl.when(pl.program_id(1) == 0)
        def _():
            gw_ref[...] = jnp.zeros_like(gw_ref)

        gw_ref[...] += jnp.dot(ht_ref[...], dp_ref[0].astype(MXU_DTYPE), preferred_element_type=F32)

    return pl.pallas_call(
        body, name="gw_in",
        grid=(n_piece, t_len // tm),
        in_specs=[pl.BlockSpec((d, tm), lambda p, t: (0, t)),
                  pl.BlockSpec((1, tm, pw), lambda p, t: (p, t, 0))],
        out_specs=pl.BlockSpec((d, pw), lambda p, t: (0, p)),
        out_shape=jax.ShapeDtypeStruct((d, n_piece * pw), F32),
        compiler_params=_params("arbitrary", "arbitrary"),
    )(h_t, dproj8)


def _gw_small_call(h_t, dsmall, tm=512):
    d, t_len = h_t.shape
    ns = dsmall.shape[1]

    def body(ht_ref, dp_ref, gw_ref):
        @pl.when(pl.program_id(0) == 0)
        def _():
            gw_ref[...] = jnp.zeros_like(gw_ref)

        gw_ref[...] += jnp.dot(ht_ref[...], dp_ref[...].astype(MXU_DTYPE), preferred_element_type=F32)

    return pl.pallas_call(
        body, name="gw_small",
        grid=(t_len // tm,),
        in_specs=[pl.BlockSpec((d, tm), lambda t: (0, t)),
                  pl.BlockSpec((tm, ns), lambda t: (t, 0))],
        out_specs=pl.BlockSpec((d, ns), lambda t: (0, 0)),
        out_shape=jax.ShapeDtypeStruct((d, ns), F32),
        compiler_params=_params("arbitrary"),
    )(h_t, dsmall)


def _dx_call(dproj8, dsmall, w_main, w_small, x, r, dx2, norm_w, tm=256):
    t_len, d = x.shape
    n_piece, _, pw = dproj8.shape
    ns = dsmall.shape[1]

    def body(dp_ref, ds_ref, wm_ref, ws_ref, x_ref, r_ref, dx2_ref, nw_ref, gx_ref, gnw_ref):
        @pl.when(pl.program_id(0) == 0)
        def _():
            gnw_ref[...] = jnp.zeros_like(gnw_ref)

        dh = lax.dot_general(ds_ref[...].astype(MXU_DTYPE), ws_ref[...], _NT, preferred_element_type=F32)
        for p in range(n_piece):
            dh = dh + lax.dot_general(dp_ref[p].astype(MXU_DTYPE), wm_ref[:, p * pw:(p + 1) * pw], _NT,
                                      preferred_element_type=F32)
        xv, rv = x_ref[...], r_ref[...]
        dn = dh * nw_ref[...]
        gx_ref[...] = dx2_ref[...] + rv * dn - xv * ((rv * rv * rv) * jnp.mean(dn * xv, axis=-1, keepdims=True))
        gnw_ref[...] += jnp.sum(dh * xv * rv, axis=0, keepdims=True)

    return pl.pallas_call(
        body, name="dx",
        grid=(t_len // tm,),
        in_specs=[pl.BlockSpec((n_piece, tm, pw), lambda i: (0, i, 0)),
                  pl.BlockSpec((tm, ns), lambda i: (i, 0)),
                  pl.BlockSpec((d, n_piece * pw), lambda i: (0, 0)),
                  pl.BlockSpec((d, ns), lambda i: (0, 0)),
                  pl.BlockSpec((tm, d), lambda i: (i, 0)),
                  pl.BlockSpec((tm, 1), lambda i: (i, 0)),
                  pl.BlockSpec((tm, d), lambda i: (i, 0)),
                  pl.BlockSpec((1, d), lambda i: (0, 0))],
        out_specs=[pl.BlockSpec((tm, d), lambda i: (i, 0)),
                   pl.BlockSpec((1, d), lambda i: (0, 0))],
        out_shape=[jax.ShapeDtypeStruct((t_len, d), F32), jax.ShapeDtypeStruct((1, d), F32)],
        compiler_params=_params("arbitrary"),
    )(dproj8, dsmall, w_main, w_small, x, r, dx2, norm_w)


def _exchange_call(name, srcs, per_peer):
    n = len(srcs)
    out_shapes = [jax.ShapeDtypeStruct(s.shape if pp else (N_DEV,) + s.shape, s.dtype) for s, pp in zip(srcs, per_peer)]

    def body(*refs):
        src_refs, out_refs = refs[:n], refs[n:2 * n]
        send_sems, recv_sems, local_sems = refs[2 * n:]
        x, y, c = lax.axis_index("x"), lax.axis_index("y"), lax.axis_index("c")
        me = 4 * x + 2 * y + c
        copies = []
        for a in range(n):
            mine = src_refs[a].at[me] if per_peer[a] else src_refs[a]
            local = pltpu.make_async_copy(mine, out_refs[a].at[me], local_sems.at[a])
            local.start()
            copies.append(local)
        remote = []
        for k in range(1, N_DEV):
            kx, ky, kc = (k >> 2) & 1, (k >> 1) & 1, k & 1
            px = 1 - x if kx else x
            py = 1 - y if ky else y
            pc = 1 - c if kc else c
            peer = 4 * px + 2 * py + pc
            for a in range(n):
                sem = a * (N_DEV - 1) + (k - 1)
                src = src_refs[a].at[peer] if per_peer[a] else src_refs[a]
                cp = pltpu.make_async_remote_copy(
                    src_ref=src, dst_ref=out_refs[a].at[me],
                    send_sem=send_sems.at[sem], recv_sem=recv_sems.at[sem],
                    device_id=(px, py, pc), device_id_type=pl.DeviceIdType.MESH)
                cp.start()
                remote.append(cp)
        for cp in remote:
            cp.wait_send()
        for cp in remote:
            cp.wait_recv()
        for cp in copies:
            cp.wait()

    hbm = pl.BlockSpec(memory_space=pl.ANY)
    return pl.pallas_call(
        body, name=name,
        in_specs=[hbm] * n, out_specs=[hbm] * n, out_shape=out_shapes,
        scratch_shapes=[pltpu.SemaphoreType.DMA((n * (N_DEV - 1),)),
                        pltpu.SemaphoreType.DMA((n * (N_DEV - 1),)),
                        pltpu.SemaphoreType.DMA((n,))],
    )(*srcs)


N_CHIPS = 4
_HBM = pl.BlockSpec(memory_space=pl.ANY)
_MESH = pl.DeviceIdType.MESH


def _gather_call(name, srcs):
    n = len(srcs)
    per = N_DEV - 1

    def body(*refs):
        src_refs, out_refs = refs[:n], refs[n:2 * n]
        send_sems, recv_sems, local_sems = refs[2 * n:]
        x, y, c = lax.axis_index("x"), lax.axis_index("y"), lax.axis_index("c")
        me, sibling = (x, y, c), (x, y, 1 - c)
        chips = [(1 - x, y), (x, 1 - y), (1 - x, 1 - y)]
        slot = lambda px, py, pc: 4 * px + 2 * py + pc

        def copy(a, k, block, to, from_src=False):
            rows = out_refs[a].at[slot(*block)]
            return pltpu.make_async_remote_copy(
                src_ref=src_refs[a] if from_src else rows, dst_ref=rows,
                send_sem=send_sems.at[a * per + k], recv_sem=recv_sems.at[a * per + k],
                device_id=to, device_id_type=_MESH)

        local = [pltpu.make_async_copy(src_refs[a], out_refs[a].at[slot(*me)], local_sems.at[a]) for a in range(n)]
        for cp in local:
            cp.start()
        first = []
        for a in range(n):
            first.append(copy(a, 0, me, sibling, True))
            first += [copy(a, 1 + j, me, (*chip, c), True) for j, chip in enumerate(chips)]
        for cp in first:
            cp.start()
        passed = []
        for j, chip in enumerate(chips):
            for a in range(n):
                copy(a, 1 + j, (*chip, c), me).wait_recv()
                fwd = copy(a, 4 + j, (*chip, c), sibling)
                fwd.start()
                passed.append(fwd)
        for a in range(n):
            copy(a, 0, sibling, me).wait_recv()
            for j, chip in enumerate(chips):
                copy(a, 4 + j, (*chip, 1 - c), me).wait_recv()
        for cp in first + passed:
            cp.wait_send()
        for cp in local:
            cp.wait()

    return pl.pallas_call(
        body, name=name,
        in_specs=[_HBM] * n, out_specs=[_HBM] * n,
        out_shape=[jax.ShapeDtypeStruct((N_DEV,) + s.shape, s.dtype) for s in srcs],
        scratch_shapes=[pltpu.SemaphoreType.DMA((n * per,)), pltpu.SemaphoreType.DMA((n * per,)),
                        pltpu.SemaphoreType.DMA((n,))],
    )(*srcs)


def _sibling_send_call(name, srcs):
    n = len(srcs)

    def body(*refs):
        src_refs, out_refs = refs[:n], refs[n:2 * n]
        send_sems, recv_sems = refs[2 * n:]
        x, y, c = lax.axis_index("x"), lax.axis_index("y"), lax.axis_index("c")
        copies = []
        for a in range(n):
            for ch in range(N_CHIPS):
                copies.append(pltpu.make_async_remote_copy(
                    src_ref=src_refs[a].at[2 * ch + (1 - c)], dst_ref=out_refs[a].at[ch],
                    send_sem=send_sems.at[a * N_CHIPS + ch], recv_sem=recv_sems.at[a * N_CHIPS + ch],
                    device_id=(x, y, 1 - c), device_id_type=_MESH))
        for cp in copies:
            cp.start()
        for cp in copies:
            cp.wait_send()
        for cp in copies:
            cp.wait_recv()

    return pl.pallas_call(
        body, name=name,
        in_specs=[_HBM] * n, out_specs=[_HBM] * n,
        out_shape=[jax.ShapeDtypeStruct((N_CHIPS,) + s.shape[1:], s.dtype) for s in srcs],
        scratch_shapes=[pltpu.SemaphoreType.DMA((n * N_CHIPS,)), pltpu.SemaphoreType.DMA((n * N_CHIPS,))],
    )(*srcs)


def _pair_sum_call(name, parts, from_sibling, tr):
    _, rows, cols = parts.shape

    def body(p_ref, s_ref, o_ref):
        o_ref[...] = (p_ref[...] + s_ref[...]).astype(o_ref.dtype)

    return pl.pallas_call(
        body, name=name,
        grid=(N_CHIPS, rows // tr),
        in_specs=[pl.BlockSpec((1, tr, cols), lambda ch, i: (2 * ch + lax.axis_index("c"), i, 0)),
                  pl.BlockSpec((1, tr, cols), lambda ch, i: (ch, i, 0))],
        out_specs=pl.BlockSpec((1, tr, cols), lambda ch, i: (ch, i, 0)),
        out_shape=jax.ShapeDtypeStruct((N_CHIPS, rows, cols), WIRE_DTYPE),
        compiler_params=_params("arbitrary", "arbitrary"),
    )(parts, from_sibling)


def _chip_exchange_call(name, srcs):
    n = len(srcs)
    per = N_CHIPS - 1

    def body(*refs):
        src_refs, out_refs = refs[:n], refs[n:2 * n]
        send_sems, recv_sems, local_sems = refs[2 * n:]
        x, y, c = lax.axis_index("x"), lax.axis_index("y"), lax.axis_index("c")
        mine = 2 * x + y
        chips = [(1 - x, y), (x, 1 - y), (1 - x, 1 - y)]
        local = [pltpu.make_async_copy(src_refs[a].at[mine], out_refs[a].at[mine], local_sems.at[a]) for a in range(n)]
        for cp in local:
            cp.start()
        remote = []
        for a in range(n):
            for j, (px, py) in enumerate(chips):
                remote.append(pltpu.make_async_remote_copy(
                    src_ref=src_refs[a].at[2 * px + py], dst_ref=out_refs[a].at[mine],
                    send_sem=send_sems.at[a * per + j], recv_sem=recv_sems.at[a * per + j],
                    device_id=(px, py, c), device_id_type=_MESH))
        for cp in remote:
            cp.start()
        for cp in remote:
            cp.wait_send()
        for cp in remote:
            cp.wait_recv()
        for cp in local:
            cp.wait()

    return pl.pallas_call(
        body, name=name,
        in_specs=[_HBM] * n, out_specs=[_HBM] * n,
        out_shape=[jax.ShapeDtypeStruct(s.shape, s.dtype) for s in srcs],
        scratch_shapes=[pltpu.SemaphoreType.DMA((n * per,)), pltpu.SemaphoreType.DMA((n * per,)),
                        pltpu.SemaphoreType.DMA((n,))],
    )(*srcs)


def _adam_call(name, parts, w, m, v, tr):
    rows, cols = w.shape
    n_slots = parts.shape[0]

    def body(p_ref, w_ref, m_ref, v_ref, g_ref, d_ref, nm_ref, nv_ref):
        g = p_ref[0].astype(F32)
        for s in range(1, n_slots):
            g = g + p_ref[s].astype(F32)
        m_new = ADAM_B1 * m_ref[...] + (1.0 - ADAM_B1) * g
        v_new = ADAM_B2 * v_ref[...] + (1.0 - ADAM_B2) * (g * g)
        m_hat = m_new / (1.0 - ADAM_B1 ** ADAM_STEP)
        v_hat = v_new / (1.0 - ADAM_B2 ** ADAM_STEP)
        g_ref[...] = g
        d_ref[...] = -ADAM_LR * (m_hat / (jnp.sqrt(v_hat) + ADAM_EPS) + ADAM_WD * w_ref[...])
        nm_ref[...] = m_new
        nv_ref[...] = v_new

    blk = pl.BlockSpec((tr, cols), lambda i: (i, 0))
    return pl.pallas_call(
        body, name=name,
        grid=(rows // tr,),
        in_specs=[pl.BlockSpec((n_slots, tr, cols), lambda i: (0, i, 0)), blk, blk, blk],
        out_specs=[blk] * 4,
        out_shape=[jax.ShapeDtypeStruct((rows, cols), F32)] * 4,
        compiler_params=_params("arbitrary"),
    )(parts, w, m, v)


_SMALL_ROWS = ("norm1_w", "final_norm_w", "sb_norm_w", "gdn_norm_w", "gdn_A_log", "gdn_dt_bias", "loss")


def _pack_small(vals, width):
    rows = [jnp.pad(a.reshape(1, -1).astype(F32), ((0, 0), (0, width - a.size))) for a in vals]
    rows += [jnp.zeros((1, width), F32)] * (8 - len(rows))
    return jnp.concatenate(rows, axis=0)


def _device_step(x2d, tgt, w_full, w_out_f32, conv_full, norm1_w, sb_norm_w, gdn_A_log, gdn_dt_bias, gdn_norm_w,
                 final_norm_w):
    t_len, d = x2d.shape
    n_chunks = t_len // CHUNK
    n_main = 8 * 512
    n_small = w_full.shape[1] - n_main
    w_main = w_full[:, :n_main].astype(MXU_DTYPE)
    w_small = jnp.pad(w_full[:, n_main:], ((0, 0), (0, LANES - n_small))).astype(MXU_DTYPE)
    w_small_t = w_full[:, n_main:].T.astype(MXU_DTYPE)
    w_out_full = w_out_f32.astype(MXU_DTYPE)

    pad_lanes = lambda a, lo: jnp.pad(a.reshape(1, -1), ((0, 0), (lo, LANES - lo - a.size)))
    alog_l, dtb_l = pad_lanes(gdn_A_log, GDN_HEADS), pad_lanes(gdn_dt_bias, GDN_HEADS)
    alog_c, dtb_c = alog_l[:, :8].T, dtb_l[:, :8].T
    sbw = jnp.tile(sb_norm_w, (1, 512 // SB_HEAD_DIM))
    gdw = jnp.tile(gdn_norm_w, (1, 512 // GDN_HEAD_DIM))
    fw = final_norm_w.reshape(1, d)

    proj, ps, pst, h_t, r1 = _inproj_call(x2d, norm1_w, w_main, w_small, w_small_t)
    o_sb, sp_total, sb_blocks_run = _sb_fwd_call(proj, t_len)
    gact = _gdn_prep_call(proj, conv_full, t_len)
    beta_l, gcol_l, grow = _gdn_gates_call(ps, pst, alog_l, dtb_l, alog_c, dtb_c, t_len)
    gam_r = grow[GDN_HEADS:2 * GDN_HEADS].reshape(GDN_HEADS, n_chunks, 1, CHUNK)
    o_gd, s_all, t_all = _gdn_fwd_call(gact, beta_l, gcol_l, gam_r, t_len)

    (dx2, d_osb, d_zsb, d_ogd, d_zgd, loss_p, g_fw, g_sbw, g_gdw, g_wout) = _post_call(
        o_sb, o_gd, proj, x2d, tgt, w_out_full, sbw, gdw, fw)

    d_sb3 = _sb_bwd_call(proj, sp_total, sb_blocks_run, d_osb, t_len)
    d_gact3, d_gates = _gdn_bwd_call(gact, beta_l, gcol_l, gam_r, s_all, t_all, d_ogd, t_len)
    d_gd3, g_conv = _gdn_prep_bwd_call(proj, conv_full, d_gact3, t_len)
    dsmall, g_alog, g_dtb = _gdn_gates_bwd_call(ps, alog_l, dtb_l, d_gates, t_len)

    dproj8 = jnp.concatenate([d_sb3, d_zsb[None], d_gd3, d_zgd[None]], axis=0)
    g_w_main = _gw_in_call(h_t, dproj8)
    g_w_small = _gw_small_call(h_t, dsmall)
    grad_x, g_n1 = _dx_call(dproj8, dsmall, w_main, w_small, x2d, r1, dx2, norm1_w)
    g_w_in_full = jnp.concatenate([g_w_main, g_w_small[:, :n_small]], axis=1)
    return (loss_p, grad_x, g_n1, g_w_in_full, g_sbw, g_conv, g_alog, g_dtb, g_gdw, g_wout, g_fw)


def kernel(x, norm1_w, w_in, sb_norm_w, gdn_conv_w, gdn_A_log, gdn_dt_bias, gdn_norm_w, w_out, final_norm_w, loss_target, m_norm1_w, m_w_in, m_sb_norm_w, m_gdn_conv_w, m_gdn_A_log, m_gdn_dt_bias, m_gdn_norm_w, m_w_out, m_final_norm_w, v_norm1_w, v_w_in, v_sb_norm_w, v_gdn_conv_w, v_gdn_A_log, v_gdn_dt_bias, v_gdn_norm_w, v_w_out, v_final_norm_w):
    d = x.shape[2]
    shard_cols = w_in.shape[2]
    conv_cols = gdn_conv_w.shape[2]

    w_in_g, w_out_g, conv_g = _gather_call(
        "gather_weights", [w_in[0].astype(WIRE_DTYPE), w_out[0].astype(WIRE_DTYPE), gdn_conv_w[0]])
    w_full = w_in_g.transpose(1, 0, 2).reshape(d, N_DEV * shard_cols)
    conv_full = conv_g.transpose(1, 0, 2).reshape(CONV_WIDTH, N_DEV * conv_cols)

    (loss_p, grad_x, g_n1, g_w_in_full, g_sbw, g_conv, g_alog, g_dtb, g_gdw, g_wout, g_fw) = _device_step(
        x[0], loss_target[0], w_full, w_out_g.reshape(d, d), conv_full, norm1_w, sb_norm_w, gdn_A_log, gdn_dt_bias,
        gdn_norm_w, final_norm_w)

    g_w_in_parts = g_w_in_full.reshape(d, N_DEV, shard_cols).transpose(1, 0, 2)
    g_wout_parts = g_wout.reshape(N_DEV, d // N_DEV, d)
    g_conv_parts = g_conv.reshape(CONV_WIDTH, N_DEV, conv_cols).transpose(1, 0, 2)
    fold = lambda a, group: a.reshape(-1, group).sum(axis=0)
    small_g = _pack_small([g_n1, g_fw, fold(g_sbw, SB_HEAD_DIM), fold(g_gdw, GDN_HEAD_DIM),
                           g_alog[0, GDN_HEADS:2 * GDN_HEADS], g_dtb[0, GDN_HEADS:2 * GDN_HEADS],
                           loss_p[0, :1]], d)
    sib_w_in, sib_wout, sib_conv = _sibling_send_call("grads_to_sibling", [g_w_in_parts, g_wout_parts, g_conv_parts])
    c_w_in = _pair_sum_call("pair_sum_w_in", g_w_in_parts, sib_w_in, 256)
    c_wout = _pair_sum_call("pair_sum_w_out", g_wout_parts, sib_wout, d // N_DEV)
    c_conv = _pair_sum_call("pair_sum_conv", g_conv_parts, sib_conv, CONV_WIDTH)
    p_w_in, p_wout, p_conv = _chip_exchange_call("grads_to_chips", [c_w_in, c_wout, c_conv])
    (p_small,) = _exchange_call("exchange_small", [small_g], [False])

    small_w = _pack_small([norm1_w, final_norm_w, sb_norm_w, gdn_norm_w, gdn_A_log, gdn_dt_bias], d)
    small_m = _pack_small([m_norm1_w, m_final_norm_w, m_sb_norm_w, m_gdn_norm_w, m_gdn_A_log, m_gdn_dt_bias], d)
    small_v = _pack_small([v_norm1_w, v_final_norm_w, v_sb_norm_w, v_gdn_norm_w, v_gdn_A_log, v_gdn_dt_bias], d)

    r_w_in = _adam_call("adam_w_in", p_w_in, w_in[0], m_w_in[0], v_w_in[0], 256)
    r_wout = _adam_call("adam_w_out", p_wout, w_out[0], m_w_out[0], v_w_out[0], d // N_DEV)
    r_conv = _adam_call("adam_conv", p_conv, gdn_conv_w[0], m_gdn_conv_w[0], v_gdn_conv_w[0], CONV_WIDTH)
    r_small = _adam_call("adam_small", p_small, small_w, small_m, small_v, 8)

    shapes = {"norm1_w": norm1_w.shape, "final_norm_w": final_norm_w.shape, "sb_norm_w": sb_norm_w.shape,
              "gdn_norm_w": gdn_norm_w.shape, "gdn_A_log": gdn_A_log.shape, "gdn_dt_bias": gdn_dt_bias.shape}

    def small_out(kind, name):
        row = _SMALL_ROWS.index(name)
        shp = shapes[name]
        size = 1
        for s in shp:
            size *= s
        return r_small[kind][row, :size].reshape(shp)

    def outputs(kind):
        return (small_out(kind, "norm1_w"), r_w_in[kind][None], small_out(kind, "sb_norm_w"), r_conv[kind][None],
                small_out(kind, "gdn_A_log"), small_out(kind, "gdn_dt_bias"), small_out(kind, "gdn_norm_w"),
                r_wout[kind][None], small_out(kind, "final_norm_w"))

    loss = r_small[0][_SMALL_ROWS.index("loss"), 0]
    return (loss, grad_x[None], *outputs(0), *outputs(1), *outputs(2), *outputs(3))
```

```python
import functools

import jax
import jax.numpy as jnp
from jax import lax
from jax.experimental import pallas as pl
from jax.experimental.pallas import tpu as pltpu

F32 = jnp.float32
MXU_DTYPE = jnp.bfloat16
WIRE_DTYPE = jnp.bfloat16
EXACT = lax.Precision.HIGHEST
EPS = 1e-6
N_DEV = 8
SB_HEAD_DIM = 64
GDN_HEAD_DIM = 128
GDN_HEADS = 4
GDN_CHUNKS_PER_STEP = 4
CHUNK = 64
CONV_WIDTH = 4
LANES = 128
SB_BLOCK = 128
SB_BQ = 256
VMEM_LIMIT_BYTES = 56 * 1024 * 1024

DPROJ_PIECE_OF_SLOT = (0, 1, 2, 4, 5, 6, 3, 7)
DPROJ_SB_SLOT, DPROJ_GDN_SLOT, DPROJ_GATE_SLOT = 0, 3, 6

ADAM_LR = 0.001
ADAM_B1 = 0.9
ADAM_B2 = 0.999
ADAM_EPS = 1e-08
ADAM_WD = 0.01
ADAM_STEP = 10

_NN = (((1,), (0,)), ((), ()))
_NT = (((1,), (1,)), ((), ()))
_TN = (((0,), (0,)), ((), ()))
_BNN = (((2,), (1,)), ((0,), (0,)))
_BNT = (((2,), (2,)), ((0,), (0,)))
_BTN = (((1,), (1,)), ((0,), (0,)))


def _mm(a, b):
    return jnp.dot(a.astype(MXU_DTYPE), b.astype(MXU_DTYPE), preferred_element_type=F32)


def _mm_nt(a, b):
    return lax.dot_general(a.astype(MXU_DTYPE), b.astype(MXU_DTYPE), _NT, preferred_element_type=F32)


def _mm_tn(a, b):
    return lax.dot_general(a.astype(MXU_DTYPE), b.astype(MXU_DTYPE), _TN, preferred_element_type=F32)


def _mx(a, b):
    return jnp.dot(a, b, precision=EXACT, preferred_element_type=F32)


def _mx_nt(a, b):
    return lax.dot_general(a, b, _NT, precision=EXACT, preferred_element_type=F32)


def _mx_tn(a, b):
    return lax.dot_general(a, b, _TN, precision=EXACT, preferred_element_type=F32)


def _split(x):
    hi = x.astype(MXU_DTYPE)
    return hi, (x - hi.astype(F32)).astype(MXU_DTYPE)


def _m3_general(a, b, dims):
    ah, al = _split(a)
    bh, bl = _split(b)
    dot = lambda x, y: lax.dot_general(x, y, dims, preferred_element_type=F32)
    return dot(ah, bh) + (dot(ah, bl) + dot(al, bh))


def _m3(a, b):
    return _m3_general(a, b, _NN)


def _m3_nt(a, b):
    return _m3_general(a, b, _NT)


def _m3_tn(a, b):
    return _m3_general(a, b, _TN)


def _sigmoid(z):
    return 1.0 / (1.0 + jnp.exp(-z))


def _softplus(z):
    return jnp.maximum(z, 0.0) + jnp.log(1.0 + jnp.exp(-jnp.abs(z)))


def _params(*semantics):
    return pltpu.CompilerParams(dimension_semantics=semantics, vmem_limit_bytes=VMEM_LIMIT_BYTES)


def _inproj_call(x, norm_w, w_main, w_small, w_small_t, tm=256):
    t_len, d = x.shape
    n = w_main.shape[1]
    ns = w_small.shape[1]
    nst = w_small_t.shape[0]

    def body(x_ref, nw_ref, wm_ref, ws_ref, wst_ref, pm_ref, ps_ref, pst_ref, ht_ref, r_ref):
        xv = x_ref[...]
        r = lax.rsqrt(jnp.mean(xv * xv, axis=-1, keepdims=True) + EPS)
        h = xv * r * nw_ref[...]
        hb = h.astype(MXU_DTYPE)
        for n0 in range(0, n, 512):
            pm_ref[:, n0:n0 + 512] = jnp.dot(hb, wm_ref[:, n0:n0 + 512], preferred_element_type=F32)
        ps_ref[...] = jnp.dot(hb, ws_ref[...], preferred_element_type=F32)
        pst_ref[...] = lax.dot_general(wst_ref[...], hb, _NT, preferred_element_type=F32)
        ht_ref[...] = h.T.astype(MXU_DTYPE)
        r_ref[...] = r

    return pl.pallas_call(
        body, name="inproj",
        grid=(t_len // tm,),
        in_specs=[pl.BlockSpec((tm, d), lambda i: (i, 0)),
                  pl.BlockSpec((1, d), lambda i: (0, 0)),
                  pl.BlockSpec((d, n), lambda i: (0, 0)),
                  pl.BlockSpec((d, ns), lambda i: (0, 0)),
                  pl.BlockSpec((nst, d), lambda i: (0, 0))],
        out_specs=[pl.BlockSpec((tm, n), lambda i: (i, 0)),
                   pl.BlockSpec((tm, ns), lambda i: (i, 0)),
                   pl.BlockSpec((nst, tm), lambda i: (0, i)),
                   pl.BlockSpec((d, tm), lambda i: (0, i)),
                   pl.BlockSpec((tm, 1), lambda i: (i, 0))],
        out_shape=[jax.ShapeDtypeStruct((t_len, n), F32),
                   jax.ShapeDtypeStruct((t_len, ns), F32),
                   jax.ShapeDtypeStruct((nst, t_len), F32),
                   jax.ShapeDtypeStruct((d, t_len), MXU_DTYPE),
                   jax.ShapeDtypeStruct((t_len, 1), F32)],
        compiler_params=_params("arbitrary"),
    )(x, norm_w, w_main, w_small, w_small_t)


def _running_sum_mm(x, tri):
    hi = x.astype(MXU_DTYPE)
    lo = (x - hi.astype(F32)).astype(MXU_DTYPE)
    return jnp.dot(hi, tri, preferred_element_type=F32) + jnp.dot(lo, tri, preferred_element_type=F32)


def _sb_iotas():
    row_i = lax.broadcasted_iota(jnp.int32, (SB_BQ, SB_BLOCK), 0)
    col_i = lax.broadcasted_iota(jnp.int32, (SB_BQ, SB_BLOCK), 1)
    sq_r = lax.broadcasted_iota(jnp.int32, (SB_BLOCK, SB_BLOCK), 0)
    sq_c = lax.broadcasted_iota(jnp.int32, (SB_BLOCK, SB_BLOCK), 1)
    return row_i, col_i, sq_r, sq_c


SB_DIAG_BLOCKS = SB_BQ // SB_BLOCK
SB_EXP_FLOOR = -110.0


def _sb_keys_descending(qi, tile, carry, z_bounds, n_heads):
    n_free = SB_DIAG_BLOCKS * qi
    for j in range(SB_DIAG_BLOCKS - 1, -1, -1):
        carry = tile(n_free + j, True, carry)

    def largest_exponent(c):
        worst = jnp.max(z_bounds[0] - c[1])
        for h in range(1, n_heads):
            worst = jnp.maximum(worst, jnp.max(z_bounds[h] - c[1 + h]))
        return worst

    def cond(state):
        return (state[0] < n_free) & (state[1] > SB_EXP_FLOOR)

    def body(state):
        c = tile(n_free - 1 - state[0], False, state[2:])
        return (state[0] + 1, largest_exponent(c), *c)

    out = lax.while_loop(cond, body, (jnp.int32(0), largest_exponent(carry), *carry))
    return out[2:], out[0]


def _sb_keys_ascending(qi, n_run, tile, carry):
    n_free = SB_DIAG_BLOCKS * qi
    carry = lax.fori_loop(0, n_run, lambda s, c: tile(n_free - n_run + s, False, c), carry)
    for j in range(SB_DIAG_BLOCKS):
        carry = tile(n_free + j, True, carry)
    return carry


def _sb_fwd_call(proj, t_len):
    nq = t_len // SB_BQ
    scale = float(SB_HEAD_DIM) ** -0.5
    n_pairs = 512 // LANES
    per_pair = LANES // SB_HEAD_DIM

    def body(q_ref, k_ref, v_ref, o_ref, st_ref, nrun_ref):
        lane = lax.broadcasted_iota(jnp.int32, (1, LANES), 1)
        row_i, col_i, sq_r, sq_c = _sb_iotas()
        ge = (sq_r >= sq_c).astype(MXU_DTYPE)
        hms = [((lane // SB_HEAD_DIM) == hh).astype(F32) for hh in range(per_pair)]
        k_sq = k_ref[...] * k_ref[...]
        k_norms = [jnp.sqrt(jnp.max(jnp.sum(k_sq * hm, axis=-1, keepdims=True))) * (1.02 * scale) for hm in hms]

        def q_loop(qi, carry):
            r0 = pl.multiple_of(qi * SB_BQ, SB_BQ)
            rows = pl.ds(r0, SB_BQ)
            q_all = q_ref[rows, :]
            qms = [(q_all * hm).astype(MXU_DTYPE) for hm in hms]
            z_bounds = [jnp.sqrt(jnp.sum(q_all * q_all * hm, axis=-1, keepdims=True)) * kn
                        for hm, kn in zip(hms, k_norms)]

            def tile(kj, masked, kc):
                acc, cs = kc[0], list(kc[1:])
                s0 = pl.multiple_of(kj * SB_BLOCK, SB_BLOCK)
                cols = pl.ds(s0, SB_BLOCK)
                kb = k_ref[cols, :].astype(MXU_DTYPE)
                v_all = v_ref[cols, :]
                mask = (col_i + s0) < (row_i + r0) if masked else None
                for hh in range(per_pair):
                    z = lax.dot_general(qms[hh], kb, _NT, preferred_element_type=F32) * scale
                    sp = _softplus(z)
                    if masked:
                        sp = jnp.where(mask, sp, 0.0)
                    a = jnp.exp(z - (_running_sum_mm(sp, ge) + cs[hh]))
                    if masked:
                        a = jnp.where(mask, a, 0.0)
                    vm = (v_all * hms[hh]).astype(MXU_DTYPE)
                    acc = acc + jnp.dot(a.astype(MXU_DTYPE), vm, preferred_element_type=F32)
                    cs[hh] = cs[hh] + jnp.sum(sp, axis=-1, keepdims=True)
                return (acc, *cs)

            zero_col = jnp.zeros((SB_BQ, 1), F32)
            out, n_run = _sb_keys_descending(
                qi, tile, (jnp.zeros((SB_BQ, LANES), F32),) + (zero_col,) * per_pair, z_bounds, per_pair)
            o_ref[rows, :] = out[0]
            for hh in range(per_pair):
                st_ref[hh, rows, :] = out[1 + hh]
            nrun_ref[pl.program_id(0), qi] = n_run
            return carry

        lax.fori_loop(0, nq, q_loop, 0)

    return pl.pallas_call(
        body, name="sb_fwd",
        grid=(n_pairs,),
        in_specs=[pl.BlockSpec((t_len, LANES), lambda p: (0, p)),
                  pl.BlockSpec((t_len, LANES), lambda p: (0, n_pairs + p)),
                  pl.BlockSpec((t_len, LANES), lambda p: (0, 2 * n_pairs + p))],
        out_specs=[pl.BlockSpec((t_len, LANES), lambda p: (0, p)),
                   pl.BlockSpec((per_pair, t_len, 1), lambda p: (p, 0, 0)),
                   pl.BlockSpec(memory_space=pltpu.SMEM)],
        out_shape=[jax.ShapeDtypeStruct((t_len, 512), F32),
                   jax.ShapeDtypeStruct((n_pairs * per_pair, t_len, 1), F32),
                   jax.ShapeDtypeStruct((n_pairs, nq), jnp.int32)],
        compiler_params=_params("arbitrary"),
    )(proj, proj, proj)


def _sb_bwd_call(proj, sp_total, n_run_all, d_o, dproj, t_len):
    nq = t_len // SB_BQ
    scale = float(SB_HEAD_DIM) ** -0.5
    n_pairs = 512 // LANES
    per_pair = LANES // SB_HEAD_DIM

    def body(q_ref, k_ref, v_ref, st_ref, nrun_ref, do_ref, dproj_in_ref, d_ref):
        lane = lax.broadcasted_iota(jnp.int32, (1, LANES), 1)
        row_i, col_i, sq_r, sq_c = _sb_iotas()
        lt = (sq_r < sq_c).astype(MXU_DTYPE)
        le = (sq_r <= sq_c).astype(MXU_DTYPE)
        hms = [((lane // SB_HEAD_DIM) == hh).astype(F32) for hh in range(per_pair)]
        d_ref[1] = jnp.zeros((t_len, LANES), F32)
        d_ref[2] = jnp.zeros((t_len, LANES), F32)

        def q_loop(qi, carry):
            r0 = pl.multiple_of(qi * SB_BQ, SB_BQ)
            rows = pl.ds(r0, SB_BQ)
            q_all, do_all = q_ref[rows, :], do_ref[rows, :]
            qms = [(q_all * hm).astype(MXU_DTYPE) for hm in hms]
            doms = [(do_all * hm).astype(MXU_DTYPE) for hm in hms]
            totals = [st_ref[hh, rows, :] for hh in range(per_pair)]

            def tile(kj, masked, kc):
                dq, cls, gls = kc[0], list(kc[1:1 + per_pair]), list(kc[1 + per_pair:])
                s0 = pl.multiple_of(kj * SB_BLOCK, SB_BLOCK)
                cols = pl.ds(s0, SB_BLOCK)
                k_all, v_all = k_ref[cols, :], v_ref[cols, :]
                kb = k_all.astype(MXU_DTYPE)
                mask = (col_i + s0) < (row_i + r0) if masked else None
                dk_t = jnp.zeros((SB_BLOCK, LANES), F32)
                dv_t = jnp.zeros((SB_BLOCK, LANES), F32)
                for hh in range(per_pair):
                    z = lax.dot_general(qms[hh], kb, _NT, preferred_element_type=F32) * scale
                    sp_all = _softplus(z)
                    sp = jnp.where(mask, sp_all, 0.0) if masked else sp_all
                    a = jnp.exp(z - (totals[hh] - cls[hh] - _running_sum_mm(sp, lt)))
                    if masked:
                        a = jnp.where(mask, a, 0.0)
                    vm = (v_all * hms[hh]).astype(MXU_DTYPE)
                    km = (k_all * hms[hh]).astype(MXU_DTYPE)
                    g = lax.dot_general(doms[hh], vm, _NT, preferred_element_type=F32) * a
                    dz = g - jnp.exp(z - sp_all) * (gls[hh] + _running_sum_mm(g, le))
                    if masked:
                        dz = jnp.where(mask, dz, 0.0)
                    dz = (dz * scale).astype(MXU_DTYPE)
                    dq = dq + jnp.dot(dz, km, preferred_element_type=F32)
                    dk_t = dk_t + lax.dot_general(dz, qms[hh], _TN, preferred_element_type=F32)
                    dv_t = dv_t + lax.dot_general(a.astype(MXU_DTYPE), doms[hh], _TN, preferred_element_type=F32)
                    cls[hh] = cls[hh] + jnp.sum(sp, axis=-1, keepdims=True)
                    gls[hh] = gls[hh] + jnp.sum(g, axis=-1, keepdims=True)
                d_ref[1, cols, :] += dk_t
                d_ref[2, cols, :] += dv_t
                return (dq, *cls, *gls)

            zero_col = jnp.zeros((SB_BQ, 1), F32)
            out = _sb_keys_ascending(qi, nrun_ref[pl.program_id(0), qi], tile,
                                     (jnp.zeros((SB_BQ, LANES), F32),) + (zero_col,) * (2 * per_pair))
            d_ref[0, rows, :] = out[0]
            return carry

        lax.fori_loop(0, nq, q_loop, 0)

    col = lambda off: pl.BlockSpec((t_len, LANES), lambda p: (0, off + p))
    return pl.pallas_call(
        body, name="sb_bwd",
        grid=(n_pairs,),
        in_specs=[col(0), col(n_pairs), col(2 * n_pairs),
                  pl.BlockSpec((per_pair, t_len, 1), lambda p: (p, 0, 0)),
                  pl.BlockSpec(memory_space=pltpu.SMEM), col(0), _HBM],
        out_specs=pl.BlockSpec((3, t_len, LANES), lambda p: (DPROJ_SB_SLOT // 3, 0, p)),
        out_shape=jax.ShapeDtypeStruct(dproj.shape, dproj.dtype),
        input_output_aliases={6: 0},
        compiler_params=_params("arbitrary"),
    )(proj, proj, proj, sp_total, n_run_all, d_o, dproj)


def _conv_taps(xin, rows, t_len):
    taps = []
    for i in range(CONV_WIDTH):
        shift = CONV_WIDTH - 1 - i
        if shift == 0:
            taps.append(xin)
        else:
            taps.append(jnp.where(rows >= shift, pltpu.roll(xin, shift, axis=0), 0.0))
    return taps


def _gdn_prep_body_common(x_ref, w_ref, t_len):
    j = pl.program_id(0)
    xin = x_ref[...]
    rows = lax.broadcasted_iota(jnp.int32, (t_len, LANES), 0)
    taps = _conv_taps(xin, rows, t_len)
    pre = taps[0] * w_ref[0:1, :]
    for i in range(1, CONV_WIDTH):
        pre = pre + taps[i] * w_ref[i:i + 1, :]
    sg = _sigmoid(pre)
    act = pre * sg
    is_qk = j < 2 * GDN_HEADS
    nrm = jnp.where(is_qk, lax.rsqrt(jnp.sum(act * act, axis=-1, keepdims=True) + EPS), 1.0)
    sc = jnp.where(j < GDN_HEADS, float(GDN_HEAD_DIM) ** -0.5, 1.0)
    return j, rows, taps, pre, sg, act, is_qk, nrm, sc


def _gdn_prep_call(proj, conv_w, t_len):
    first = 2048 // LANES

    def body(x_ref, w_ref, out_ref):
        _, _, _, _, _, act, _, nrm, sc = _gdn_prep_body_common(x_ref, w_ref, t_len)
        out_ref[...] = act * nrm * sc

    return pl.pallas_call(
        body, name="gdn_prep",
        grid=(3 * GDN_HEADS,),
        in_specs=[pl.BlockSpec((t_len, LANES), lambda j: (0, first + j)),
                  pl.BlockSpec((CONV_WIDTH, LANES), lambda j: (0, j))],
        out_specs=pl.BlockSpec((t_len, LANES), lambda j: (0, j)),
        out_shape=jax.ShapeDtypeStruct((t_len, 3 * 512), F32),
        compiler_params=_params("arbitrary"),
    )(proj, conv_w)


def _gdn_prep_bwd_call(proj, conv_w, d_act3, dproj, t_len):
    first = 2048 // LANES

    def body(x_ref, w_ref, d_ref, dproj_in_ref, dx_ref, dw_ref):
        _, rows, taps, pre, sg, act, is_qk, nrm, sc = _gdn_prep_body_common(x_ref, w_ref, t_len)
        d_out = d_ref[0]
        dn = d_out * sc
        d_norm = nrm * dn - act * (nrm * nrm * nrm) * jnp.sum(dn * act, axis=-1, keepdims=True)
        d_act = jnp.where(is_qk, d_norm, d_out)
        d_pre = d_act * sg * (1.0 + pre * (1.0 - sg))
        dx = d_pre * w_ref[CONV_WIDTH - 1:CONV_WIDTH, :]
        dw_ref[CONV_WIDTH - 1:CONV_WIDTH, :] = jnp.sum(d_pre * taps[CONV_WIDTH - 1], axis=0, keepdims=True)
        for i in range(CONV_WIDTH - 1):
            shift = CONV_WIDTH - 1 - i
            up = jnp.where(rows < t_len - shift, pltpu.roll(d_pre, t_len - shift, axis=0), 0.0)
            dx = dx + up * w_ref[i:i + 1, :]
            dw_ref[i:i + 1, :] = jnp.sum(d_pre * taps[i], axis=0, keepdims=True)
        dx_ref[0] = dx

    return pl.pallas_call(
        body, name="gdn_prep_bwd",
        grid=(3 * GDN_HEADS,),
        in_specs=[pl.BlockSpec((t_len, LANES), lambda j: (0, first + j)),
                  pl.BlockSpec((CONV_WIDTH, LANES), lambda j: (0, j)),
                  pl.BlockSpec((1, t_len, LANES), lambda j: (j // GDN_HEADS, 0, j % GDN_HEADS)), _HBM],
        out_specs=[pl.BlockSpec((1, t_len, LANES), lambda j: (DPROJ_GDN_SLOT + j // GDN_HEADS, 0, j % GDN_HEADS)),
                   pl.BlockSpec((CONV_WIDTH, LANES), lambda j: (0, j))],
        out_shape=[jax.ShapeDtypeStruct(dproj.shape, dproj.dtype),
                   jax.ShapeDtypeStruct((CONV_WIDTH, 3 * 512), F32)],
        input_output_aliases={3: 0},
        compiler_params=_params("arbitrary"),
    )(proj, conv_w, d_act3, dproj)


def _chunk_cumsum_matrix():
    r = lax.broadcasted_iota(jnp.int32, (LANES, LANES), 0)
    c = lax.broadcasted_iota(jnp.int32, (LANES, LANES), 1)
    return ((r <= c) & ((r // CHUNK) == (c // CHUNK))).astype(F32)


def _gdn_gates_call(ps, pst, alog_l, dtb_l, alog_c, dtb_c, t_len):
    def body(ps_ref, pst_ref, al_ref, dl_ref, ac_ref, dc_ref, beta_ref, gcol_ref, grow_ref):
        upper = _chunk_cumsum_matrix()
        lower = upper.T
        psv = ps_ref[...]
        beta_ref[...] = _sigmoid(psv)
        g_l = -jnp.exp(al_ref[...]) * _softplus(psv + dl_ref[...])
        g_r = -jnp.exp(ac_ref[...]) * _softplus(pst_ref[...] + dc_ref[...])
        for w in range(t_len // LANES):
            sl = slice(w * LANES, (w + 1) * LANES)
            gcol_ref[sl, :] = _mx(lower, g_l[sl, :])
            grow_ref[:, sl] = _mx(g_r[:, sl], upper)

    vm = pl.BlockSpec(memory_space=pltpu.VMEM)
    return pl.pallas_call(
        body, name="gdn_gates",
        in_specs=[vm] * 6, out_specs=[vm] * 3,
        out_shape=[jax.ShapeDtypeStruct((t_len, LANES), F32),
                   jax.ShapeDtypeStruct((t_len, LANES), F32),
                   jax.ShapeDtypeStruct((8, t_len), F32)],
        compiler_params=pltpu.CompilerParams(vmem_limit_bytes=VMEM_LIMIT_BYTES),
    )(ps, pst, alog_l, dtb_l, alog_c, dtb_c)


def _gdn_gates_bwd_call(ps, alog_l, dtb_l, d_l, t_len):
    def body(ps_ref, al_ref, dl_ref, d_ref, dps_ref, gal_ref, gdt_ref):
        lane = lax.broadcasted_iota(jnp.int32, (1, LANES), 1)
        psv = ps_ref[...]
        dv = d_ref[...]
        beta = _sigmoid(psv)
        ea = jnp.exp(al_ref[...])
        arg = psv + dl_ref[...]
        g = -ea * _softplus(arg)
        d_a = dv * (-ea) * _sigmoid(arg)
        is_a = (lane >= GDN_HEADS) & (lane < 2 * GDN_HEADS)
        dps_ref[...] = jnp.where(lane < GDN_HEADS, dv * beta * (1.0 - beta), jnp.where(is_a, d_a, 0.0))
        gdt_ref[...] = jnp.where(is_a, jnp.sum(d_a, axis=0, keepdims=True), 0.0)
        gal_ref[...] = jnp.where(is_a, jnp.sum(dv * g, axis=0, keepdims=True), 0.0)

    vm = pl.BlockSpec(memory_space=pltpu.VMEM)
    return pl.pallas_call(
        body, name="gdn_gates_bwd",
        in_specs=[vm] * 4, out_specs=[vm] * 3,
        out_shape=[jax.ShapeDtypeStruct((t_len, LANES), F32),
                   jax.ShapeDtypeStruct((1, LANES), F32),
                   jax.ShapeDtypeStruct((1, LANES), F32)],
        compiler_params=pltpu.CompilerParams(vmem_limit_bytes=VMEM_LIMIT_BYTES),
    )(ps, alog_l, dtb_l, d_l)


def _bm(a, b):
    return _m3_general(a, b, _BNN)


def _bm_nt(a, b):
    return _m3_general(a, b, _BNT)


def _bm_tn(a, b):
    return _m3_general(a, b, _BTN)


def _heads_of(ref, rows):
    return jnp.stack([ref[rows, h * GDN_HEAD_DIM:(h + 1) * GDN_HEAD_DIM] for h in range(GDN_HEADS)])


def _chunk_terms(q_ref, k_ref, v_ref, b_ref, gc_ref, gr_ref, c, incl, strict):
    r0 = pl.multiple_of(c * CHUNK, CHUNK)
    rows = pl.ds(r0, CHUNK)
    q, k, v = _heads_of(q_ref, rows), _heads_of(k_ref, rows), _heads_of(v_ref, rows)
    lane_ids = lax.broadcasted_iota(jnp.int32, (1, LANES), 1)
    pick = lambda slab, first: jnp.stack([jnp.sum(jnp.where(lane_ids == first + h, slab, 0.0), axis=-1, keepdims=True)
                                          for h in range(GDN_HEADS)])
    b = pick(b_ref[rows, :], 0)
    gc = pick(gc_ref[rows, :], GDN_HEADS)
    gr = gr_ref[:, c]
    dm = jnp.where(incl, jnp.exp(jnp.where(incl, gc - gr, 0.0)), 0.0)
    kb = k * b
    vb = v * b
    e = jnp.exp(gc)
    a = jnp.where(strict, _bm_nt(kb, k) * dm, 0.0)
    p = jnp.where(incl, _bm_nt(q, k) * dm, 0.0)
    gl = gc[:, CHUNK - 1:CHUNK, :]
    eg = jnp.exp(gl - gc)
    return rows, q, k, v, b, gc, dm, kb, vb, e, a, p, gl, eg


def _gdn_specs(t_len, n_chunks, reverse):
    cps = GDN_CHUNKS_PER_STEP
    steps = n_chunks // cps
    at = (lambda g: steps - 1 - g) if reverse else (lambda g: g)
    rows_blk = lambda width, part=0: pl.BlockSpec((cps * CHUNK, width), lambda g: (at(g), part))
    gate_r = pl.BlockSpec((GDN_HEADS, cps, 1, CHUNK), lambda g: (0, at(g), 0, 0))
    per_chunk = lambda r, c: pl.BlockSpec((GDN_HEADS, cps, r, c), lambda g: (0, at(g), 0, 0))
    return cps, steps, rows_blk, gate_r, per_chunk


def _gdn_fwd_call(gact, beta_c, gam_c, gam_r, t_len):
    n_chunks = t_len // CHUNK
    dk = GDN_HEAD_DIM
    width = GDN_HEADS * dk
    cps, steps, rows_blk, gate_r, per_chunk = _gdn_specs(t_len, n_chunks, False)

    def body(q_ref, k_ref, v_ref, b_ref, gc_ref, gr_ref, o_ref, s_ref, t_ref, state_ref):
        row = lax.broadcasted_iota(jnp.int32, (CHUNK, CHUNK), 0)
        col = lax.broadcasted_iota(jnp.int32, (CHUNK, CHUNK), 1)
        incl, strict = row >= col, row > col
        eye = (row == col).astype(F32)

        @pl.when(pl.program_id(0) == 0)
        def _():
            state_ref[...] = jnp.zeros_like(state_ref)

        def chunk(c, carry):
            rows, q, k, v, b, gc, dm, kb, vb, e, a, p, gl, eg = _chunk_terms(
                q_ref, k_ref, v_ref, b_ref, gc_ref, gr_ref, c, incl, strict)
            s = state_ref[...]
            x = -a
            tm = eye + x
            xp = x
            for _ in range(5):
                xp = _bm(xp, xp)
                tm = tm + _bm(tm, xp)
            u = _bm(tm, vb)
            w = _bm(tm, kb * e)
            vn = u - _bm(w, s)
            o = _bm(q * e, s) + _bm(p, vn)
            for h in range(GDN_HEADS):
                o_ref[rows, h * dk:(h + 1) * dk] = o[h]
            s_ref[:, c] = s
            t_ref[:, c] = tm
            state_ref[...] = s * jnp.exp(gl) + _bm_tn(k * eg, vn)
            return carry

        lax.fori_loop(0, cps, chunk, 0)

    return pl.pallas_call(
        body, name="gdn_fwd",
        grid=(steps,),
        in_specs=[rows_blk(width, 0), rows_blk(width, 1), rows_blk(width, 2), rows_blk(LANES), rows_blk(LANES), gate_r],
        out_specs=[rows_blk(width), per_chunk(dk, dk), per_chunk(CHUNK, CHUNK)],
        out_shape=[jax.ShapeDtypeStruct((t_len, width), F32),
                   jax.ShapeDtypeStruct((GDN_HEADS, n_chunks, dk, dk), F32),
                   jax.ShapeDtypeStruct((GDN_HEADS, n_chunks, CHUNK, CHUNK), F32)],
        scratch_shapes=[pltpu.VMEM((GDN_HEADS, dk, dk), F32)],
        compiler_params=_params("arbitrary"),
    )(gact, gact, gact, beta_c, gam_c, gam_r)


def _gdn_bwd_call(gact, beta_c, gam_c, gam_r, s_all, t_all, d_o, t_len):
    n_chunks = t_len // CHUNK
    dk = GDN_HEAD_DIM
    width = GDN_HEADS * dk
    cps, steps, rows_blk, gate_r, per_chunk = _gdn_specs(t_len, n_chunks, True)

    def body(q_ref, k_ref, v_ref, b_ref, gc_ref, gr_ref, s_ref, t_ref, do_ref, d_ref, dgate_ref, dstate_ref):
        row = lax.broadcasted_iota(jnp.int32, (CHUNK, CHUNK), 0)
        col = lax.broadcasted_iota(jnp.int32, (CHUNK, CHUNK), 1)
        incl, strict = row >= col, row > col
        upper = jnp.broadcast_to((row <= col).astype(F32), (GDN_HEADS, CHUNK, CHUNK))
        ones = jnp.ones((GDN_HEADS, CHUNK, LANES), F32)
        last_row = lax.broadcasted_iota(jnp.int32, (CHUNK, 1), 0) == CHUNK - 1
        lane_ids = lax.broadcasted_iota(jnp.int32, (1, LANES), 1)
        rsum = lambda m: jnp.sum(m, axis=-1, keepdims=True)
        total = lambda m: jnp.sum(rsum(m), axis=1, keepdims=True)

        @pl.when(pl.program_id(0) == 0)
        def _():
            dstate_ref[...] = jnp.zeros_like(dstate_ref)

        def chunk(step, carry):
            c = cps - 1 - step
            rows, q, k, v, b, gc, dm, kb, vb, e, a, p, gl, eg = _chunk_terms(
                q_ref, k_ref, v_ref, b_ref, gc_ref, gr_ref, c, incl, strict)
            ds = dstate_ref[...]
            s = s_ref[:, c]
            tm = t_ref[:, c]
            d_out = _heads_of(do_ref, rows)
            el = jnp.exp(gl)
            kbe = kb * e
            u = _bm(tm, vb)
            w = _bm(tm, kbe)
            vn = u - _bm(w, s)
            qe = q * e
            kd = k * eg

            d_vn = _bm_tn(p, d_out) + _bm(kd, ds)
            d_qe = _bm_nt(d_out, s)
            d_p = jnp.where(incl, _bm_nt(d_out, vn), 0.0)
            dstate_ref[...] = el * ds + _bm_tn(qe, d_out) - _bm_tn(w, d_vn)
            d_kd = _bm_nt(vn, ds)
            d_w = -_bm_nt(d_vn, s)
            d_vb = _bm_tn(tm, d_vn)
            d_kbe = _bm_tn(tm, d_w)
            d_a = -jnp.where(strict, _bm_nt(d_vb, u) + _bm_nt(d_kbe, w), 0.0)
            m = d_a * dm
            n = d_p * dm
            d_kb = _bm(m, k) + d_kbe * e
            d_q = _bm(n, k) + d_qe * e
            d_k = _bm_tn(m, kb) + _bm_tn(n, q) + d_kd * eg + b * d_kb
            d_v = b * d_vb
            r = d_a * a + d_p * p
            kd_term = rsum(d_kd * kd)
            d_gl = total(ds * s) * el + jnp.sum(kd_term, axis=1, keepdims=True)
            d_gam = (rsum(r) - _bm_tn(r, ones)[:, :, 0:1] + rsum(d_qe * qe) + rsum(d_kbe * kbe) - kd_term
                     + jnp.where(last_row, d_gl, 0.0))
            d_beta = rsum(d_kb * k) + rsum(d_vb * v)
            d_g = _bm(upper, d_gam * ones)[:, :, 0:1]
            gates = jnp.zeros((CHUNK, LANES), F32)
            for h in range(GDN_HEADS):
                lanes = slice(h * dk, (h + 1) * dk)
                d_ref[0, rows, lanes] = d_q[h]
                d_ref[1, rows, lanes] = d_k[h]
                d_ref[2, rows, lanes] = d_v[h]
                gates = gates + (jnp.where(lane_ids == h, d_beta[h], 0.0)
                                 + jnp.where(lane_ids == GDN_HEADS + h, d_g[h], 0.0))
            dgate_ref[rows, :] = gates
            return carry

        lax.fori_loop(0, cps, chunk, 0)

    d_spec = pl.BlockSpec((3, cps * CHUNK, width), lambda g: (0, steps - 1 - g, 0))
    return pl.pallas_call(
        body, name="gdn_bwd",
        grid=(steps,),
        in_specs=[rows_blk(width, 0), rows_blk(width, 1), rows_blk(width, 2), rows_blk(LANES), rows_blk(LANES), gate_r,
                  per_chunk(dk, dk), per_chunk(CHUNK, CHUNK), rows_blk(width)],
        out_specs=[d_spec, rows_blk(LANES)],
        out_shape=[jax.ShapeDtypeStruct((3, t_len, width), F32),
                   jax.ShapeDtypeStruct((t_len, LANES), F32)],
        scratch_shapes=[pltpu.VMEM((GDN_HEADS, dk, dk), F32)],
        compiler_params=_params("arbitrary"),
    )(gact, gact, gact, beta_c, gam_c, gam_r, s_all, t_all, d_o)


def _group_matrix(width, group):
    r = lax.broadcasted_iota(jnp.int32, (width, width), 0)
    c = lax.broadcasted_iota(jnp.int32, (width, width), 1)
    return ((r // group) == (c // group)).astype(F32)


def _post_call(o_sb, o_gd, proj, x, target, w_out, sbw, gdw, fw, tm=256):
    t_len, d = x.shape
    half = 512
    zsb_blk = 1536 // half
    zgd_blk = 3584 // half

    def body(osb_ref, ogd_ref, zsb_ref, zgd_ref, x_ref, tg_ref, wo_ref, sbw_ref, gdw_ref, fw_ref,
             dx2_ref, dosb_ref, dogd_ref, dz_ref, loss_ref, gfw_ref, gsb_ref, ggd_ref, gwo_ref):
        step = pl.program_id(0)

        @pl.when(step == 0)
        def _():
            loss_ref[...] = jnp.zeros_like(loss_ref)
            gfw_ref[...] = jnp.zeros_like(gfw_ref)
            gsb_ref[...] = jnp.zeros_like(gsb_ref)
            ggd_ref[...] = jnp.zeros_like(ggd_ref)
            gwo_ref[...] = jnp.zeros_like(gwo_ref)

        def head_forward(o, z, w, gmat, inv):
            r = lax.rsqrt(_mx(o * o, gmat) * inv + EPS)
            nrm = o * r * w
            sg = _sigmoid(z)
            return r, nrm, sg, nrm * (z * sg)

        def head_backward(d_m, o, z, w, gmat, inv, r, nrm, sg):
            d_n = d_m * (z * sg)
            d_z = d_m * nrm * (sg * (1.0 + z * (1.0 - sg)))
            dnw = d_n * w
            d_o = r * dnw - o * (r * r * r) * (_mx(dnw * o, gmat) * inv)
            return d_o, d_z, jnp.sum(d_n * o * r, axis=0, keepdims=True)

        g_sb = _group_matrix(half, SB_HEAD_DIM)
        g_gd = _group_matrix(half, GDN_HEAD_DIM)
        osb, ogd, zsb, zgd = osb_ref[...], ogd_ref[...], zsb_ref[...], zgd_ref[...]
        sbw_v, gdw_v = sbw_ref[...], gdw_ref[...]
        r_sb, n_sb, sg_sb, m_sb = head_forward(osb, zsb, sbw_v, g_sb, 1.0 / SB_HEAD_DIM)
        r_gd, n_gd, sg_gd, m_gd = head_forward(ogd, zgd, gdw_v, g_gd, 1.0 / GDN_HEAD_DIM)
        mixed = jnp.concatenate([m_sb, m_gd], axis=1).astype(MXU_DTYPE)
        wo = wo_ref[...]
        x2 = x_ref[...] + jnp.dot(mixed, wo, preferred_element_type=F32)
        r2 = lax.rsqrt(jnp.mean(x2 * x2, axis=-1, keepdims=True) + EPS)
        fw_v = fw_ref[...]
        err = x2 * r2 * fw_v - tg_ref[...]
        loss_ref[...] += 0.5 * jnp.sum(jnp.sum(err * err, axis=-1, keepdims=True) * (1.0 / d))
        dy = err * (1.0 / d)
        gg = dy * fw_v
        dx2 = r2 * gg - x2 * ((r2 * r2 * r2) * jnp.mean(gg * x2, axis=-1, keepdims=True))
        gfw_ref[...] += jnp.sum(dy * x2 * r2, axis=0, keepdims=True)
        dx2_ref[...] = dx2
        dx2b = dx2.astype(MXU_DTYPE)
        d_mixed = lax.dot_general(dx2b, wo, _NT, preferred_element_type=F32)
        gwo_ref[...] += lax.dot_general(mixed, dx2b, _TN, preferred_element_type=F32)
        d_osb, d_zsb, gsb = head_backward(d_mixed[:, :half], osb, zsb, sbw_v, g_sb, 1.0 / SB_HEAD_DIM, r_sb, n_sb, sg_sb)
        d_ogd, d_zgd, ggd = head_backward(d_mixed[:, half:], ogd, zgd, gdw_v, g_gd, 1.0 / GDN_HEAD_DIM, r_gd, n_gd, sg_gd)
        dosb_ref[...] = d_osb
        dogd_ref[...] = d_ogd
        dz_ref[0] = d_zsb
        dz_ref[1] = d_zgd
        gsb_ref[...] += gsb
        ggd_ref[...] += ggd

    row_blk = lambda w: pl.BlockSpec((tm, w), lambda i: (i, 0))
    fixed = lambda r, w: pl.BlockSpec((r, w), lambda i: (0, 0))
    return pl.pallas_call(
        body, name="post",
        grid=(t_len // tm,),
        in_specs=[row_blk(half), row_blk(half),
                  pl.BlockSpec((tm, half), lambda i: (i, zsb_blk)),
                  pl.BlockSpec((tm, half), lambda i: (i, zgd_blk)),
                  row_blk(d), row_blk(d), fixed(d, d), fixed(1, half), fixed(1, half), fixed(1, d)],
        out_specs=[row_blk(d), row_blk(half), row_blk(half),
                   pl.BlockSpec((2, tm, half), lambda i: (DPROJ_GATE_SLOT // 2, i, 0)),
                   fixed(1, LANES), fixed(1, d), fixed(1, half), fixed(1, half), fixed(d, d)],
        out_shape=[jax.ShapeDtypeStruct((t_len, d), F32)] + [jax.ShapeDtypeStruct((t_len, half), F32)] * 2
                  + [jax.ShapeDtypeStruct((len(DPROJ_PIECE_OF_SLOT), t_len, half), F32),
                     jax.ShapeDtypeStruct((1, LANES), F32), jax.ShapeDtypeStruct((1, d), F32),
                     jax.ShapeDtypeStruct((1, half), F32), jax.ShapeDtypeStruct((1, half), F32),
                     jax.ShapeDtypeStruct((d, d), F32)],
        compiler_params=_params("arbitrary"),
    )(o_sb, o_gd, proj, proj, x, target, w_out, sbw, gdw, fw)


def _piece_of_slot(s):
    return jnp.where(s < DPROJ_GDN_SLOT, s, jnp.where(s < DPROJ_GATE_SLOT, s + 1,
                                                     jnp.where(s == DPROJ_GATE_SLOT, 3, 7)))


def _gw_in_call(h_t, dproj8):
    d, t_len = h_t.shape
    n_piece, _, pw = dproj8.shape

    def body(ht_ref, dp_ref, gw_ref):
        gw_ref[...] = jnp.dot(ht_ref[...], dp_ref[0].astype(MXU_DTYPE), preferred_element_type=F32)

    return pl.pallas_call(
        body, name="gw_in",
        grid=(n_piece,),
        in_specs=[pl.BlockSpec((d, t_len), lambda s: (0, 0)),
                  pl.BlockSpec((1, t_len, pw), lambda s: (s, 0, 0))],
        out_specs=pl.BlockSpec((d, pw), lambda s: (0, _piece_of_slot(s))),
        out_shape=jax.ShapeDtypeStruct((d, n_piece * pw), F32),
        compiler_params=_params("arbitrary"),
    )(h_t, dproj8)


def _gw_small_call(h_t, dsmall, tm=512):
    d, t_len = h_t.shape
    ns = dsmall.shape[1]

    def body(ht_ref, dp_ref, gw_ref):
        @pl.when(pl.program_id(0) == 0)
        def _():
            gw_ref[...] = jnp.zeros_like(gw_ref)

        gw_ref[...] += jnp.dot(ht_ref[...], dp_ref[...].astype(MXU_DTYPE), preferred_element_type=F32)

    return pl.pallas_call(
        body, name="gw_small",
        grid=(t_len // tm,),
        in_specs=[pl.BlockSpec((d, tm), lambda t: (0, t)),
                  pl.BlockSpec((tm, ns), lambda t: (t, 0))],
        out_specs=pl.BlockSpec((d, ns), lambda t: (0, 0)),
        out_shape=jax.ShapeDtypeStruct((d, ns), F32),
        compiler_params=_params("arbitrary"),
    )(h_t, dsmall)


def _dx_call(dproj8, dsmall, w_main, w_small, x, r, dx2, norm_w, tm=256):
    t_len, d = x.shape
    n_piece, _, pw = dproj8.shape
    ns = dsmall.shape[1]

    def body(dp_ref, ds_ref, wm_ref, ws_ref, x_ref, r_ref, dx2_ref, nw_ref, gx_ref, gnw_ref):
        @pl.when(pl.program_id(0) == 0)
        def _():
            gnw_ref[...] = jnp.zeros_like(gnw_ref)

        dh = lax.dot_general(ds_ref[...].astype(MXU_DTYPE), ws_ref[...], _NT, preferred_element_type=F32)
        for s, p in enumerate(DPROJ_PIECE_OF_SLOT):
            dh = dh + lax.dot_general(dp_ref[s].astype(MXU_DTYPE), wm_ref[:, p * pw:(p + 1) * pw], _NT,
                                      preferred_element_type=F32)
        xv, rv = x_ref[...], r_ref[...]
        dn = dh * nw_ref[...]
        gx_ref[...] = dx2_ref[...] + rv * dn - xv * ((rv * rv * rv) * jnp.mean(dn * xv, axis=-1, keepdims=True))
        gnw_ref[...] += jnp.sum(dh * xv * rv, axis=0, keepdims=True)

    return pl.pallas_call(
        body, name="dx",
        grid=(t_len // tm,),
        in_specs=[pl.BlockSpec((n_piece, tm, pw), lambda i: (0, i, 0)),
                  pl.BlockSpec((tm, ns), lambda i: (i, 0)),
                  pl.BlockSpec((d, n_piece * pw), lambda i: (0, 0)),
                  pl.BlockSpec((d, ns), lambda i: (0, 0)),
                  pl.BlockSpec((tm, d), lambda i: (i, 0)),
                  pl.BlockSpec((tm, 1), lambda i: (i, 0)),
                  pl.BlockSpec((tm, d), lambda i: (i, 0)),
                  pl.BlockSpec((1, d), lambda i: (0, 0))],
        out_specs=[pl.BlockSpec((tm, d), lambda i: (i, 0)),
                   pl.BlockSpec((1, d), lambda i: (0, 0))],
        out_shape=[jax.ShapeDtypeStruct((t_len, d), F32), jax.ShapeDtypeStruct((1, d), F32)],
        compiler_params=_params("arbitrary"),
    )(dproj8, dsmall, w_main, w_small, x, r, dx2, norm_w)


def _exchange_call(name, srcs, per_peer):
    n = len(srcs)
    out_shapes = [jax.ShapeDtypeStruct(s.shape if pp else (N_DEV,) + s.shape, s.dtype) for s, pp in zip(srcs, per_peer)]

    def body(*refs):
        src_refs, out_refs = refs[:n], refs[n:2 * n]
        send_sems, recv_sems, local_sems = refs[2 * n:]
        x, y, c = lax.axis_index("x"), lax.axis_index("y"), lax.axis_index("c")
        me = 4 * x + 2 * y + c
        copies = []
        for a in range(n):
            mine = src_refs[a].at[me] if per_peer[a] else src_refs[a]
            local = pltpu.make_async_copy(mine, out_refs[a].at[me], local_sems.at[a])
            local.start()
            copies.append(local)
        remote = []
        for k in range(1, N_DEV):
            kx, ky, kc = (k >> 2) & 1, (k >> 1) & 1, k & 1
            px = 1 - x if kx else x
            py = 1 - y if ky else y
            pc = 1 - c if kc else c
            peer = 4 * px + 2 * py + pc
            for a in range(n):
                sem = a * (N_DEV - 1) + (k - 1)
                src = src_refs[a].at[peer] if per_peer[a] else src_refs[a]
                cp = pltpu.make_async_remote_copy(
                    src_ref=src, dst_ref=out_refs[a].at[me],
                    send_sem=send_sems.at[sem], recv_sem=recv_sems.at[sem],
                    device_id=(px, py, pc), device_id_type=pl.DeviceIdType.MESH)
                cp.start()
                remote.append(cp)
        for cp in remote:
            cp.wait_send()
        for cp in remote:
            cp.wait_recv()
        for cp in copies:
            cp.wait()

    hbm = pl.BlockSpec(memory_space=pl.ANY)
    return pl.pallas_call(
        body, name=name,
        in_specs=[hbm] * n, out_specs=[hbm] * n, out_shape=out_shapes,
        scratch_shapes=[pltpu.SemaphoreType.DMA((n * (N_DEV - 1),)),
                        pltpu.SemaphoreType.DMA((n * (N_DEV - 1),)),
                        pltpu.SemaphoreType.DMA((n,))],
    )(*srcs)


N_CHIPS = 4
_HBM = pl.BlockSpec(memory_space=pl.ANY)
_MESH = pl.DeviceIdType.MESH


def _gather_call(name, srcs):
    n = len(srcs)
    per = N_DEV - 1

    def body(*refs):
        src_refs, out_refs = refs[:n], refs[n:2 * n]
        send_sems, recv_sems, local_sems = refs[2 * n:]
        x, y, c = lax.axis_index("x"), lax.axis_index("y"), lax.axis_index("c")
        me, sibling = (x, y, c), (x, y, 1 - c)
        chips = [(1 - x, y), (x, 1 - y), (1 - x, 1 - y)]
        slot = lambda px, py, pc: 4 * px + 2 * py + pc

        def copy(a, k, block, to, from_src=False):
            rows = out_refs[a].at[slot(*block)]
            return pltpu.make_async_remote_copy(
                src_ref=src_refs[a] if from_src else rows, dst_ref=rows,
                send_sem=send_sems.at[a * per + k], recv_sem=recv_sems.at[a * per + k],
                device_id=to, device_id_type=_MESH)

        local = [pltpu.make_async_copy(src_refs[a], out_refs[a].at[slot(*me)], local_sems.at[a]) for a in range(n)]
        for cp in local:
            cp.start()
        first = []
        for a in range(n):
            first.append(copy(a, 0, me, sibling, True))
            first += [copy(a, 1 + j, me, (*chip, c), True) for j, chip in enumerate(chips)]
        for cp in first:
            cp.start()
        passed = []
        for j, chip in enumerate(chips):
            for a in range(n):
                copy(a, 1 + j, (*chip, c), me).wait_recv()
                fwd = copy(a, 4 + j, (*chip, c), sibling)
                fwd.start()
                passed.append(fwd)
        for a in range(n):
            copy(a, 0, sibling, me).wait_recv()
            for j, chip in enumerate(chips):
                copy(a, 4 + j, (*chip, 1 - c), me).wait_recv()
        for cp in first + passed:
            cp.wait_send()
        for cp in local:
            cp.wait()

    return pl.pallas_call(
        body, name=name,
        in_specs=[_HBM] * n, out_specs=[_HBM] * n,
        out_shape=[jax.ShapeDtypeStruct((N_DEV,) + s.shape, s.dtype) for s in srcs],
        scratch_shapes=[pltpu.SemaphoreType.DMA((n * per,)), pltpu.SemaphoreType.DMA((n * per,)),
                        pltpu.SemaphoreType.DMA((n,))],
    )(*srcs)


def _sibling_send_call(name, srcs):
    n = len(srcs)

    def body(*refs):
        src_refs, out_refs = refs[:n], refs[n:2 * n]
        send_sems, recv_sems = refs[2 * n:]
        x, y, c = lax.axis_index("x"), lax.axis_index("y"), lax.axis_index("c")
        copies = []
        for a in range(n):
            for ch in range(N_CHIPS):
                copies.append(pltpu.make_async_remote_copy(
                    src_ref=src_refs[a].at[2 * ch + (1 - c)], dst_ref=out_refs[a].at[ch],
                    send_sem=send_sems.at[a * N_CHIPS + ch], recv_sem=recv_sems.at[a * N_CHIPS + ch],
                    device_id=(x, y, 1 - c), device_id_type=_MESH))
        for cp in copies:
            cp.start()
        for cp in copies:
            cp.wait_send()
        for cp in copies:
            cp.wait_recv()

    return pl.pallas_call(
        body, name=name,
        in_specs=[_HBM] * n, out_specs=[_HBM] * n,
        out_shape=[jax.ShapeDtypeStruct((N_CHIPS,) + s.shape[1:], s.dtype) for s in srcs],
        scratch_shapes=[pltpu.SemaphoreType.DMA((n * N_CHIPS,)), pltpu.SemaphoreType.DMA((n * N_CHIPS,))],
    )(*srcs)


def _pair_sum_call(name, parts, from_sibling, tr):
    _, rows, cols = parts.shape

    def body(p_ref, s_ref, o_ref):
        o_ref[...] = (p_ref[...] + s_ref[...]).astype(o_ref.dtype)

    return pl.pallas_call(
        body, name=name,
        grid=(N_CHIPS, rows // tr),
        in_specs=[pl.BlockSpec((1, tr, cols), lambda ch, i: (2 * ch + lax.axis_index("c"), i, 0)),
                  pl.BlockSpec((1, tr, cols), lambda ch, i: (ch, i, 0))],
        out_specs=pl.BlockSpec((1, tr, cols), lambda ch, i: (ch, i, 0)),
        out_shape=jax.ShapeDtypeStruct((N_CHIPS, rows, cols), WIRE_DTYPE),
        compiler_params=_params("arbitrary", "arbitrary"),
    )(parts, from_sibling)


def _chip_exchange_call(name, srcs):
    n = len(srcs)
    per = N_CHIPS - 1

    def body(*refs):
        src_refs, out_refs = refs[:n], refs[n:2 * n]
        send_sems, recv_sems, local_sems = refs[2 * n:]
        x, y, c = lax.axis_index("x"), lax.axis_index("y"), lax.axis_index("c")
        mine = 2 * x + y
        chips = [(1 - x, y), (x, 1 - y), (1 - x, 1 - y)]
        local = [pltpu.make_async_copy(src_refs[a].at[mine], out_refs[a].at[mine], local_sems.at[a]) for a in range(n)]
        for cp in local:
            cp.start()
        remote = []
        for a in range(n):
            for j, (px, py) in enumerate(chips):
                remote.append(pltpu.make_async_remote_copy(
                    src_ref=src_refs[a].at[2 * px + py], dst_ref=out_refs[a].at[mine],
                    send_sem=send_sems.at[a * per + j], recv_sem=recv_sems.at[a * per + j],
                    device_id=(px, py, c), device_id_type=_MESH))
        for cp in remote:
            cp.start()
        for cp in remote:
            cp.wait_send()
        for cp in remote:
            cp.wait_recv()
        for cp in local:
            cp.wait()

    return pl.pallas_call(
        body, name=name,
        in_specs=[_HBM] * n, out_specs=[_HBM] * n,
        out_shape=[jax.ShapeDtypeStruct(s.shape, s.dtype) for s in srcs],
        scratch_shapes=[pltpu.SemaphoreType.DMA((n * per,)), pltpu.SemaphoreType.DMA((n * per,)),
                        pltpu.SemaphoreType.DMA((n,))],
    )(*srcs)


def _adam_call(name, parts, w, m, v, tr):
    rows, cols = w.shape
    n_slots = parts.shape[0]

    def body(p_ref, w_ref, m_ref, v_ref, g_ref, d_ref, nm_ref, nv_ref):
        g = p_ref[0].astype(F32)
        for s in range(1, n_slots):
            g = g + p_ref[s].astype(F32)
        m_new = ADAM_B1 * m_ref[...] + (1.0 - ADAM_B1) * g
        v_new = ADAM_B2 * v_ref[...] + (1.0 - ADAM_B2) * (g * g)
        m_hat = m_new / (1.0 - ADAM_B1 ** ADAM_STEP)
        v_hat = v_new / (1.0 - ADAM_B2 ** ADAM_STEP)
        g_ref[...] = g
        d_ref[...] = -ADAM_LR * (m_hat / (jnp.sqrt(v_hat) + ADAM_EPS) + ADAM_WD * w_ref[...])
        nm_ref[...] = m_new
        nv_ref[...] = v_new

    blk = pl.BlockSpec((tr, cols), lambda i: (i, 0))
    return pl.pallas_call(
        body, name=name,
        grid=(rows // tr,),
        in_specs=[pl.BlockSpec((n_slots, tr, cols), lambda i: (0, i, 0)), blk, blk, blk],
        out_specs=[blk] * 4,
        out_shape=[jax.ShapeDtypeStruct((rows, cols), F32)] * 4,
        compiler_params=_params("arbitrary"),
    )(parts, w, m, v)


_SMALL_ROWS = ("norm1_w", "final_norm_w", "sb_norm_w", "gdn_norm_w", "gdn_A_log", "gdn_dt_bias", "loss")


def _pack_small(vals, width):
    rows = [jnp.pad(a.reshape(1, -1).astype(F32), ((0, 0), (0, width - a.size))) for a in vals]
    rows += [jnp.zeros((1, width), F32)] * (8 - len(rows))
    return jnp.concatenate(rows, axis=0)


def _device_step(x2d, tgt, w_full, w_out_f32, conv_full, norm1_w, sb_norm_w, gdn_A_log, gdn_dt_bias, gdn_norm_w,
                 final_norm_w):
    t_len, d = x2d.shape
    n_chunks = t_len // CHUNK
    n_main = 8 * 512
    n_small = w_full.shape[1] - n_main
    w_main = w_full[:, :n_main].astype(MXU_DTYPE)
    w_small = jnp.pad(w_full[:, n_main:], ((0, 0), (0, LANES - n_small))).astype(MXU_DTYPE)
    w_small_t = w_full[:, n_main:].T.astype(MXU_DTYPE)
    w_out_full = w_out_f32.astype(MXU_DTYPE)

    pad_lanes = lambda a, lo: jnp.pad(a.reshape(1, -1), ((0, 0), (lo, LANES - lo - a.size)))
    alog_l, dtb_l = pad_lanes(gdn_A_log, GDN_HEADS), pad_lanes(gdn_dt_bias, GDN_HEADS)
    alog_c, dtb_c = alog_l[:, :8].T, dtb_l[:, :8].T
    sbw = jnp.tile(sb_norm_w, (1, 512 // SB_HEAD_DIM))
    gdw = jnp.tile(gdn_norm_w, (1, 512 // GDN_HEAD_DIM))
    fw = final_norm_w.reshape(1, d)

    proj, ps, pst, h_t, r1 = _inproj_call(x2d, norm1_w, w_main, w_small, w_small_t)
    o_sb, sp_total, sb_blocks_run = _sb_fwd_call(proj, t_len)
    gact = _gdn_prep_call(proj, conv_full, t_len)
    beta_l, gcol_l, grow = _gdn_gates_call(ps, pst, alog_l, dtb_l, alog_c, dtb_c, t_len)
    gam_r = grow[GDN_HEADS:2 * GDN_HEADS].reshape(GDN_HEADS, n_chunks, 1, CHUNK)
    o_gd, s_all, t_all = _gdn_fwd_call(gact, beta_l, gcol_l, gam_r, t_len)

    (dx2, d_osb, d_ogd, dproj8, loss_p, g_fw, g_sbw, g_gdw, g_wout) = _post_call(
        o_sb, o_gd, proj, x2d, tgt, w_out_full, sbw, gdw, fw)

    dproj8 = _sb_bwd_call(proj, sp_total, sb_blocks_run, d_osb, dproj8, t_len)
    d_gact3, d_gates = _gdn_bwd_call(gact, beta_l, gcol_l, gam_r, s_all, t_all, d_ogd, t_len)
    dproj8, g_conv = _gdn_prep_bwd_call(proj, conv_full, d_gact3, dproj8, t_len)
    dsmall, g_alog, g_dtb = _gdn_gates_bwd_call(ps, alog_l, dtb_l, d_gates, t_len)

    g_w_main = _gw_in_call(h_t, dproj8)
    g_w_small = _gw_small_call(h_t, dsmall)
    grad_x, g_n1 = _dx_call(dproj8, dsmall, w_main, w_small, x2d, r1, dx2, norm1_w)
    g_w_in_full = jnp.concatenate([g_w_main, g_w_small[:, :n_small]], axis=1)
    return (loss_p, grad_x, g_n1, g_w_in_full, g_sbw, g_conv, g_alog, g_dtb, g_gdw, g_wout, g_fw)


def kernel(x, norm1_w, w_in, sb_norm_w, gdn_conv_w, gdn_A_log, gdn_dt_bias, gdn_norm_w, w_out, final_norm_w, loss_target, m_norm1_w, m_w_in, m_sb_norm_w, m_gdn_conv_w, m_gdn_A_log, m_gdn_dt_bias, m_gdn_norm_w, m_w_out, m_final_norm_w, v_norm1_w, v_w_in, v_sb_norm_w, v_gdn_conv_w, v_gdn_A_log, v_gdn_dt_bias, v_gdn_norm_w, v_w_out, v_final_norm_w):
    d = x.shape[2]
    shard_cols = w_in.shape[2]
    conv_cols = gdn_conv_w.shape[2]

    w_in_g, w_out_g, conv_g = _gather_call(
        "gather_weights", [w_in[0].astype(WIRE_DTYPE), w_out[0].astype(WIRE_DTYPE), gdn_conv_w[0]])
    w_full = w_in_g.transpose(1, 0, 2).reshape(d, N_DEV * shard_cols)
    conv_full = conv_g.transpose(1, 0, 2).reshape(CONV_WIDTH, N_DEV * conv_cols)

    (loss_p, grad_x, g_n1, g_w_in_full, g_sbw, g_conv, g_alog, g_dtb, g_gdw, g_wout, g_fw) = _device_step(
        x[0], loss_target[0], w_full, w_out_g.reshape(d, d), conv_full, norm1_w, sb_norm_w, gdn_A_log, gdn_dt_bias,
        gdn_norm_w, final_norm_w)

    g_w_in_parts = g_w_in_full.reshape(d, N_DEV, shard_cols).transpose(1, 0, 2)
    g_wout_parts = g_wout.reshape(N_DEV, d // N_DEV, d)
    g_conv_parts = g_conv.reshape(CONV_WIDTH, N_DEV, conv_cols).transpose(1, 0, 2)
    fold = lambda a, group: a.reshape(-1, group).sum(axis=0)
    small_g = _pack_small([g_n1, g_fw, fold(g_sbw, SB_HEAD_DIM), fold(g_gdw, GDN_HEAD_DIM),
                           g_alog[0, GDN_HEADS:2 * GDN_HEADS], g_dtb[0, GDN_HEADS:2 * GDN_HEADS],
                           loss_p[0, :1]], d)
    sib_w_in, sib_wout, sib_conv = _sibling_send_call("grads_to_sibling", [g_w_in_parts, g_wout_parts, g_conv_parts])
    c_w_in = _pair_sum_call("pair_sum_w_in", g_w_in_parts, sib_w_in, 256)
    c_wout = _pair_sum_call("pair_sum_w_out", g_wout_parts, sib_wout, d // N_DEV)
    c_conv = _pair_sum_call("pair_sum_conv", g_conv_parts, sib_conv, CONV_WIDTH)
    p_w_in, p_wout, p_conv = _chip_exchange_call("grads_to_chips", [c_w_in, c_wout, c_conv])
    (p_small,) = _exchange_call("exchange_small", [small_g], [False])

    small_w = _pack_small([norm1_w, final_norm_w, sb_norm_w, gdn_norm_w, gdn_A_log, gdn_dt_bias], d)
    small_m = _pack_small([m_norm1_w, m_final_norm_w, m_sb_norm_w, m_gdn_norm_w, m_gdn_A_log, m_gdn_dt_bias], d)
    small_v = _pack_small([v_norm1_w, v_final_norm_w, v_sb_norm_w, v_gdn_norm_w, v_gdn_A_log, v_gdn_dt_bias], d)

    r_w_in = _adam_call("adam_w_in", p_w_in, w_in[0], m_w_in[0], v_w_in[0], 256)
    r_wout = _adam_call("adam_w_out", p_wout, w_out[0], m_w_out[0], v_w_out[0], d // N_DEV)
    r_conv = _adam_call("adam_conv", p_conv, gdn_conv_w[0], m_gdn_conv_w[0], v_gdn_conv_w[0], CONV_WIDTH)
    r_small = _adam_call("adam_small", p_small, small_w, small_m, small_v, 8)

    shapes = {"norm1_w": norm1_w.shape, "final_norm_w": final_norm_w.shape, "sb_norm_w": sb_norm_w.shape,
              "gdn_norm_w": gdn_norm_w.shape, "gdn_A_log": gdn_A_log.shape, "gdn_dt_bias": gdn_dt_bias.shape}

    def small_out(kind, name):
        row = _SMALL_ROWS.index(name)
        shp = shapes[name]
        size = 1
        for s in shp:
            size *= s
        return r_small[kind][row, :size].reshape(shp)

    def outputs(kind):
        return (small_out(kind, "norm1_w"), r_w_in[kind][None], small_out(kind, "sb_norm_w"), r_conv[kind][None],
                small_out(kind, "gdn_A_log"), small_out(kind, "gdn_dt_bias"), small_out(kind, "gdn_norm_w"),
                r_wout[kind][None], small_out(kind, "final_norm_w"))

    loss = r_small[0][_SMALL_ROWS.index("loss"), 0]
    return (loss, grad_x[None], *outputs(0), *outputs(1), *outputs(2), *outputs(3))
```

```python
import functools

import jax
import jax.numpy as jnp
from jax import lax
from jax.experimental import pallas as pl
from jax.experimental.pallas import tpu as pltpu

F32 = jnp.float32
MXU_DTYPE = jnp.bfloat16
WIRE_DTYPE = jnp.bfloat16
EXACT = lax.Precision.HIGHEST
EPS = 1e-6
N_DEV = 8
SB_HEAD_DIM = 64
GDN_HEAD_DIM = 128
GDN_HEADS = 4
GDN_CHUNKS_PER_STEP = 4
CHUNK = 64
CONV_WIDTH = 4
LANES = 128
SB_BLOCK = 128
SB_BQ = 256
VMEM_LIMIT_BYTES = 56 * 1024 * 1024

DPROJ_PIECE_OF_SLOT = (0, 1, 2, 4, 5, 6, 3, 7)
DPROJ_SB_SLOT, DPROJ_GDN_SLOT, DPROJ_GATE_SLOT = 0, 3, 6

ADAM_LR = 0.001
ADAM_B1 = 0.9
ADAM_B2 = 0.999
ADAM_EPS = 1e-08
ADAM_WD = 0.01
ADAM_STEP = 10

_NN = (((1,), (0,)), ((), ()))
_NT = (((1,), (1,)), ((), ()))
_TN = (((0,), (0,)), ((), ()))
_BNN = (((2,), (1,)), ((0,), (0,)))
_BNT = (((2,), (2,)), ((0,), (0,)))
_BTN = (((1,), (1,)), ((0,), (0,)))


def _mm(a, b):
    return jnp.dot(a.astype(MXU_DTYPE), b.astype(MXU_DTYPE), preferred_element_type=F32)


def _mm_nt(a, b):
    return lax.dot_general(a.astype(MXU_DTYPE), b.astype(MXU_DTYPE), _NT, preferred_element_type=F32)


def _mm_tn(a, b):
    return lax.dot_general(a.astype(MXU_DTYPE), b.astype(MXU_DTYPE), _TN, preferred_element_type=F32)


def _mx(a, b):
    return jnp.dot(a, b, precision=EXACT, preferred_element_type=F32)


def _mx_nt(a, b):
    return lax.dot_general(a, b, _NT, precision=EXACT, preferred_element_type=F32)


def _mx_tn(a, b):
    return lax.dot_general(a, b, _TN, precision=EXACT, preferred_element_type=F32)


def _split(x):
    hi = x.astype(MXU_DTYPE)
    return hi, (x - hi.astype(F32)).astype(MXU_DTYPE)


def _m3_general(a, b, dims):
    ah, al = _split(a)
    bh, bl = _split(b)
    dot = lambda x, y: lax.dot_general(x, y, dims, preferred_element_type=F32)
    return dot(ah, bh) + (dot(ah, bl) + dot(al, bh))


def _m3(a, b):
    return _m3_general(a, b, _NN)


def _m3_nt(a, b):
    return _m3_general(a, b, _NT)


def _m3_tn(a, b):
    return _m3_general(a, b, _TN)


def _sigmoid(z):
    return 1.0 / (1.0 + jnp.exp(-z))


def _softplus(z):
    return jnp.maximum(z, 0.0) + jnp.log(1.0 + jnp.exp(-jnp.abs(z)))


def _params(*semantics):
    return pltpu.CompilerParams(dimension_semantics=semantics, vmem_limit_bytes=VMEM_LIMIT_BYTES)


def _inproj_call(x, norm_w, w_main, w_small, w_small_t, tm=256):
    t_len, d = x.shape
    n = w_main.shape[1]
    ns = w_small.shape[1]
    nst = w_small_t.shape[0]

    def body(x_ref, nw_ref, wm_ref, ws_ref, wst_ref, pm_ref, ps_ref, pst_ref, ht_ref, r_ref):
        xv = x_ref[...]
        r = lax.rsqrt(jnp.mean(xv * xv, axis=-1, keepdims=True) + EPS)
        h = xv * r * nw_ref[...]
        hb = h.astype(MXU_DTYPE)
        for n0 in range(0, n, 512):
            pm_ref[:, n0:n0 + 512] = jnp.dot(hb, wm_ref[:, n0:n0 + 512], preferred_element_type=F32)
        ps_ref[...] = jnp.dot(hb, ws_ref[...], preferred_element_type=F32)
        pst_ref[...] = lax.dot_general(wst_ref[...], hb, _NT, preferred_element_type=F32)
        ht_ref[...] = h.T.astype(MXU_DTYPE)
        r_ref[...] = r

    return pl.pallas_call(
        body, name="inproj",
        grid=(t_len // tm,),
        in_specs=[pl.BlockSpec((tm, d), lambda i: (i, 0)),
                  pl.BlockSpec((1, d), lambda i: (0, 0)),
                  pl.BlockSpec((d, n), lambda i: (0, 0)),
                  pl.BlockSpec((d, ns), lambda i: (0, 0)),
                  pl.BlockSpec((nst, d), lambda i: (0, 0))],
        out_specs=[pl.BlockSpec((tm, n), lambda i: (i, 0)),
                   pl.BlockSpec((tm, ns), lambda i: (i, 0)),
                   pl.BlockSpec((nst, tm), lambda i: (0, i)),
                   pl.BlockSpec((d, tm), lambda i: (0, i)),
                   pl.BlockSpec((tm, 1), lambda i: (i, 0))],
        out_shape=[jax.ShapeDtypeStruct((t_len, n), F32),
                   jax.ShapeDtypeStruct((t_len, ns), F32),
                   jax.ShapeDtypeStruct((nst, t_len), F32),
                   jax.ShapeDtypeStruct((d, t_len), MXU_DTYPE),
                   jax.ShapeDtypeStruct((t_len, 1), F32)],
        compiler_params=_params("arbitrary"),
    )(x, norm_w, w_main, w_small, w_small_t)


def _running_sum_mm(x, tri):
    hi = x.astype(MXU_DTYPE)
    lo = (x - hi.astype(F32)).astype(MXU_DTYPE)
    return jnp.dot(hi, tri, preferred_element_type=F32) + jnp.dot(lo, tri, preferred_element_type=F32)


def _sb_iotas():
    row_i = lax.broadcasted_iota(jnp.int32, (SB_BQ, SB_BLOCK), 0)
    col_i = lax.broadcasted_iota(jnp.int32, (SB_BQ, SB_BLOCK), 1)
    sq_r = lax.broadcasted_iota(jnp.int32, (SB_BLOCK, SB_BLOCK), 0)
    sq_c = lax.broadcasted_iota(jnp.int32, (SB_BLOCK, SB_BLOCK), 1)
    return row_i, col_i, sq_r, sq_c


SB_DIAG_BLOCKS = SB_BQ // SB_BLOCK
SB_EXP_FLOOR = -110.0


def _sb_keys_descending(qi, tile, carry, z_bounds, n_heads):
    n_free = SB_DIAG_BLOCKS * qi
    for j in range(SB_DIAG_BLOCKS - 1, -1, -1):
        carry = tile(n_free + j, True, carry)

    def largest_exponent(c):
        worst = jnp.max(z_bounds[0] - c[1])
        for h in range(1, n_heads):
            worst = jnp.maximum(worst, jnp.max(z_bounds[h] - c[1 + h]))
        return worst

    def cond(state):
        return (state[0] < n_free) & (state[1] > SB_EXP_FLOOR)

    def body(state):
        c = tile(n_free - 1 - state[0], False, state[2:])
        return (state[0] + 1, largest_exponent(c), *c)

    out = lax.while_loop(cond, body, (jnp.int32(0), largest_exponent(carry), *carry))
    return out[2:], out[0]


def _sb_keys_ascending(qi, n_run, tile, carry):
    n_free = SB_DIAG_BLOCKS * qi
    carry = lax.fori_loop(0, n_run, lambda s, c: tile(n_free - n_run + s, False, c), carry)
    for j in range(SB_DIAG_BLOCKS):
        carry = tile(n_free + j, True, carry)
    return carry


def _sb_fwd_call(proj, t_len):
    nq = t_len // SB_BQ
    scale = float(SB_HEAD_DIM) ** -0.5
    n_pairs = 512 // LANES
    per_pair = LANES // SB_HEAD_DIM

    def body(q_ref, k_ref, v_ref, o_ref, st_ref, nrun_ref):
        lane = lax.broadcasted_iota(jnp.int32, (1, LANES), 1)
        row_i, col_i, sq_r, sq_c = _sb_iotas()
        ge = (sq_r >= sq_c).astype(MXU_DTYPE)
        hms = [((lane // SB_HEAD_DIM) == hh).astype(F32) for hh in range(per_pair)]
        k_sq = k_ref[...] * k_ref[...]
        k_norms = [jnp.sqrt(jnp.max(jnp.sum(k_sq * hm, axis=-1, keepdims=True))) * (1.02 * scale) for hm in hms]

        def q_loop(qi, carry):
            r0 = pl.multiple_of(qi * SB_BQ, SB_BQ)
            rows = pl.ds(r0, SB_BQ)
            q_all = q_ref[rows, :]
            qms = [(q_all * (hm * scale)).astype(MXU_DTYPE) for hm in hms]
            z_bounds = [jnp.sqrt(jnp.sum(q_all * q_all * hm, axis=-1, keepdims=True)) * kn
                        for hm, kn in zip(hms, k_norms)]

            def tile(kj, masked, kc):
                acc, cs = kc[0], list(kc[1:])
                heads = range(per_pair)
                s0 = pl.multiple_of(kj * SB_BLOCK, SB_BLOCK)
                cols = pl.ds(s0, SB_BLOCK)
                kb = k_ref[cols, :].astype(MXU_DTYPE)
                v_all = v_ref[cols, :]
                vms = [(v_all * hms[h]).astype(MXU_DTYPE) for h in heads]
                zs = [lax.dot_general(qms[h], kb, _NT, preferred_element_type=F32) for h in heads]
                sps = [_softplus(z) for z in zs]
                if masked:
                    mask = (col_i + s0) < (row_i + r0)
                    sps = [jnp.where(mask, sp, 0.0) for sp in sps]
                sums = [_running_sum_mm(sp, ge) for sp in sps]
                ws = [jnp.exp(zs[h] - (sums[h] + cs[h])) for h in heads]
                if masked:
                    ws = [jnp.where(mask, w, 0.0) for w in ws]
                for h in heads:
                    acc = acc + jnp.dot(ws[h].astype(MXU_DTYPE), vms[h], preferred_element_type=F32)
                cs = [cs[h] + jnp.sum(sps[h], axis=-1, keepdims=True) for h in heads]
                return (acc, *cs)

            zero_col = jnp.zeros((SB_BQ, 1), F32)
            out, n_run = _sb_keys_descending(
                qi, tile, (jnp.zeros((SB_BQ, LANES), F32),) + (zero_col,) * per_pair, z_bounds, per_pair)
            o_ref[rows, :] = out[0]
            for hh in range(per_pair):
                st_ref[hh, rows, :] = out[1 + hh]
            nrun_ref[pl.program_id(0), qi] = n_run
            return carry

        lax.fori_loop(0, nq, q_loop, 0)

    return pl.pallas_call(
        body, name="sb_fwd",
        grid=(n_pairs,),
        in_specs=[pl.BlockSpec((t_len, LANES), lambda p: (0, p)),
                  pl.BlockSpec((t_len, LANES), lambda p: (0, n_pairs + p)),
                  pl.BlockSpec((t_len, LANES), lambda p: (0, 2 * n_pairs + p))],
        out_specs=[pl.BlockSpec((t_len, LANES), lambda p: (0, p)),
                   pl.BlockSpec((per_pair, t_len, 1), lambda p: (p, 0, 0)),
                   pl.BlockSpec(memory_space=pltpu.SMEM)],
        out_shape=[jax.ShapeDtypeStruct((t_len, 512), F32),
                   jax.ShapeDtypeStruct((n_pairs * per_pair, t_len, 1), F32),
                   jax.ShapeDtypeStruct((n_pairs, nq), jnp.int32)],
        compiler_params=_params("arbitrary"),
    )(proj, proj, proj)


def _sb_bwd_call(proj, sp_total, n_run_all, d_o, dproj, t_len):
    nq = t_len // SB_BQ
    scale = float(SB_HEAD_DIM) ** -0.5
    n_pairs = 512 // LANES
    per_pair = LANES // SB_HEAD_DIM

    def body(q_ref, k_ref, v_ref, st_ref, nrun_ref, do_ref, dproj_in_ref, d_ref):
        lane = lax.broadcasted_iota(jnp.int32, (1, LANES), 1)
        row_i, col_i, sq_r, sq_c = _sb_iotas()
        lt = (sq_r < sq_c).astype(MXU_DTYPE)
        le = (sq_r <= sq_c).astype(MXU_DTYPE)
        hms = [((lane // SB_HEAD_DIM) == hh).astype(F32) for hh in range(per_pair)]
        d_ref[1] = jnp.zeros((t_len, LANES), F32)
        d_ref[2] = jnp.zeros((t_len, LANES), F32)

        def q_loop(qi, carry):
            r0 = pl.multiple_of(qi * SB_BQ, SB_BQ)
            rows = pl.ds(r0, SB_BQ)
            q_all, do_all = q_ref[rows, :], do_ref[rows, :]
            qms = [(q_all * (hm * scale)).astype(MXU_DTYPE) for hm in hms]
            doms = [(do_all * hm).astype(MXU_DTYPE) for hm in hms]
            totals = [st_ref[hh, rows, :] for hh in range(per_pair)]

            def tile(kj, masked, kc):
                dq, cls, gls = kc[0], list(kc[1:1 + per_pair]), list(kc[1 + per_pair:])
                heads = range(per_pair)
                s0 = pl.multiple_of(kj * SB_BLOCK, SB_BLOCK)
                cols = pl.ds(s0, SB_BLOCK)
                k_all, v_all = k_ref[cols, :], v_ref[cols, :]
                kb = k_all.astype(MXU_DTYPE)
                vms = [(v_all * hms[h]).astype(MXU_DTYPE) for h in heads]
                kms = [(k_all * (hms[h] * scale)).astype(MXU_DTYPE) for h in heads]
                zs = [lax.dot_general(qms[h], kb, _NT, preferred_element_type=F32) for h in heads]
                das = [lax.dot_general(doms[h], vms[h], _NT, preferred_element_type=F32) for h in heads]
                sp_alls = [_softplus(z) for z in zs]
                sps = sp_alls
                if masked:
                    mask = (col_i + s0) < (row_i + r0)
                    sps = [jnp.where(mask, sp, 0.0) for sp in sp_alls]
                lefts = [_running_sum_mm(sp, lt) for sp in sps]
                ws = [jnp.exp(zs[h] - (totals[h] - cls[h] - lefts[h])) for h in heads]
                if masked:
                    ws = [jnp.where(mask, w, 0.0) for w in ws]
                gs = [das[h] * ws[h] for h in heads]
                g_sums = [_running_sum_mm(g, le) for g in gs]
                dzs = [gs[h] - jnp.exp(zs[h] - sp_alls[h]) * (gls[h] + g_sums[h]) for h in heads]
                if masked:
                    dzs = [jnp.where(mask, dz, 0.0) for dz in dzs]
                dzs = [dz.astype(MXU_DTYPE) for dz in dzs]
                dk_t = jnp.zeros((SB_BLOCK, LANES), F32)
                dv_t = jnp.zeros((SB_BLOCK, LANES), F32)
                for h in heads:
                    dq = dq + jnp.dot(dzs[h], kms[h], preferred_element_type=F32)
                    dk_t = dk_t + lax.dot_general(dzs[h], qms[h], _TN, preferred_element_type=F32)
                    dv_t = dv_t + lax.dot_general(ws[h].astype(MXU_DTYPE), doms[h], _TN, preferred_element_type=F32)
                d_ref[1, cols, :] += dk_t
                d_ref[2, cols, :] += dv_t
                cls = [cls[h] + jnp.sum(sps[h], axis=-1, keepdims=True) for h in heads]
                gls = [gls[h] + jnp.sum(gs[h], axis=-1, keepdims=True) for h in heads]
                return (dq, *cls, *gls)

            zero_col = jnp.zeros((SB_BQ, 1), F32)
            out = _sb_keys_ascending(qi, nrun_ref[pl.program_id(0), qi], tile,
                                     (jnp.zeros((SB_BQ, LANES), F32),) + (zero_col,) * (2 * per_pair))
            d_ref[0, rows, :] = out[0]
            return carry

        lax.fori_loop(0, nq, q_loop, 0)

    col = lambda off: pl.BlockSpec((t_len, LANES), lambda p: (0, off + p))
    return pl.pallas_call(
        body, name="sb_bwd",
        grid=(n_pairs,),
        in_specs=[col(0), col(n_pairs), col(2 * n_pairs),
                  pl.BlockSpec((per_pair, t_len, 1), lambda p: (p, 0, 0)),
                  pl.BlockSpec(memory_space=pltpu.SMEM), col(0), _HBM],
        out_specs=pl.BlockSpec((3, t_len, LANES), lambda p: (DPROJ_SB_SLOT // 3, 0, p)),
        out_shape=jax.ShapeDtypeStruct(dproj.shape, dproj.dtype),
        input_output_aliases={6: 0},
        compiler_params=_params("arbitrary"),
    )(proj, proj, proj, sp_total, n_run_all, d_o, dproj)


def _conv_taps(xin, rows, t_len):
    taps = []
    for i in range(CONV_WIDTH):
        shift = CONV_WIDTH - 1 - i
        if shift == 0:
            taps.append(xin)
        else:
            taps.append(jnp.where(rows >= shift, pltpu.roll(xin, shift, axis=0), 0.0))
    return taps


def _gdn_prep_body_common(x_ref, w_ref, t_len):
    j = pl.program_id(0)
    xin = x_ref[...]
    rows = lax.broadcasted_iota(jnp.int32, (t_len, LANES), 0)
    taps = _conv_taps(xin, rows, t_len)
    pre = taps[0] * w_ref[0:1, :]
    for i in range(1, CONV_WIDTH):
        pre = pre + taps[i] * w_ref[i:i + 1, :]
    sg = _sigmoid(pre)
    act = pre * sg
    is_qk = j < 2 * GDN_HEADS
    nrm = jnp.where(is_qk, lax.rsqrt(jnp.sum(act * act, axis=-1, keepdims=True) + EPS), 1.0)
    sc = jnp.where(j < GDN_HEADS, float(GDN_HEAD_DIM) ** -0.5, 1.0)
    return j, rows, taps, pre, sg, act, is_qk, nrm, sc


def _gdn_prep_call(proj, conv_w, t_len):
    first = 2048 // LANES

    def body(x_ref, w_ref, out_ref):
        _, _, _, _, _, act, _, nrm, sc = _gdn_prep_body_common(x_ref, w_ref, t_len)
        out_ref[...] = act * nrm * sc

    return pl.pallas_call(
        body, name="gdn_prep",
        grid=(3 * GDN_HEADS,),
        in_specs=[pl.BlockSpec((t_len, LANES), lambda j: (0, first + j)),
                  pl.BlockSpec((CONV_WIDTH, LANES), lambda j: (0, j))],
        out_specs=pl.BlockSpec((t_len, LANES), lambda j: (0, j)),
        out_shape=jax.ShapeDtypeStruct((t_len, 3 * 512), F32),
        compiler_params=_params("arbitrary"),
    )(proj, conv_w)


def _gdn_prep_bwd_call(proj, conv_w, d_act3, dproj, t_len):
    first = 2048 // LANES

    def body(x_ref, w_ref, d_ref, dproj_in_ref, dx_ref, dw_ref):
        _, rows, taps, pre, sg, act, is_qk, nrm, sc = _gdn_prep_body_common(x_ref, w_ref, t_len)
        d_out = d_ref[0]
        dn = d_out * sc
        d_norm = nrm * dn - act * (nrm * nrm * nrm) * jnp.sum(dn * act, axis=-1, keepdims=True)
        d_act = jnp.where(is_qk, d_norm, d_out)
        d_pre = d_act * sg * (1.0 + pre * (1.0 - sg))
        dx = d_pre * w_ref[CONV_WIDTH - 1:CONV_WIDTH, :]
        dw_ref[CONV_WIDTH - 1:CONV_WIDTH, :] = jnp.sum(d_pre * taps[CONV_WIDTH - 1], axis=0, keepdims=True)
        for i in range(CONV_WIDTH - 1):
            shift = CONV_WIDTH - 1 - i
            up = jnp.where(rows < t_len - shift, pltpu.roll(d_pre, t_len - shift, axis=0), 0.0)
            dx = dx + up * w_ref[i:i + 1, :]
            dw_ref[i:i + 1, :] = jnp.sum(d_pre * taps[i], axis=0, keepdims=True)
        dx_ref[0] = dx

    return pl.pallas_call(
        body, name="gdn_prep_bwd",
        grid=(3 * GDN_HEADS,),
        in_specs=[pl.BlockSpec((t_len, LANES), lambda j: (0, first + j)),
                  pl.BlockSpec((CONV_WIDTH, LANES), lambda j: (0, j)),
                  pl.BlockSpec((1, t_len, LANES), lambda j: (j // GDN_HEADS, 0, j % GDN_HEADS)), _HBM],
        out_specs=[pl.BlockSpec((1, t_len, LANES), lambda j: (DPROJ_GDN_SLOT + j // GDN_HEADS, 0, j % GDN_HEADS)),
                   pl.BlockSpec((CONV_WIDTH, LANES), lambda j: (0, j))],
        out_shape=[jax.ShapeDtypeStruct(dproj.shape, dproj.dtype),
                   jax.ShapeDtypeStruct((CONV_WIDTH, 3 * 512), F32)],
        input_output_aliases={3: 0},
        compiler_params=_params("arbitrary"),
    )(proj, conv_w, d_act3, dproj)


def _chunk_cumsum_matrix():
    r = lax.broadcasted_iota(jnp.int32, (LANES, LANES), 0)
    c = lax.broadcasted_iota(jnp.int32, (LANES, LANES), 1)
    return ((r <= c) & ((r // CHUNK) == (c // CHUNK))).astype(F32)


def _gdn_gates_call(ps, pst, alog_l, dtb_l, alog_c, dtb_c, t_len):
    def body(ps_ref, pst_ref, al_ref, dl_ref, ac_ref, dc_ref, beta_ref, gcol_ref, grow_ref):
        upper = _chunk_cumsum_matrix()
        lower = upper.T
        psv = ps_ref[...]
        beta_ref[...] = _sigmoid(psv)
        g_l = -jnp.exp(al_ref[...]) * _softplus(psv + dl_ref[...])
        g_r = -jnp.exp(ac_ref[...]) * _softplus(pst_ref[...] + dc_ref[...])
        for w in range(t_len // LANES):
            sl = slice(w * LANES, (w + 1) * LANES)
            gcol_ref[sl, :] = _mx(lower, g_l[sl, :])
            grow_ref[:, sl] = _mx(g_r[:, sl], upper)

    vm = pl.BlockSpec(memory_space=pltpu.VMEM)
    return pl.pallas_call(
        body, name="gdn_gates",
        in_specs=[vm] * 6, out_specs=[vm] * 3,
        out_shape=[jax.ShapeDtypeStruct((t_len, LANES), F32),
                   jax.ShapeDtypeStruct((t_len, LANES), F32),
                   jax.ShapeDtypeStruct((8, t_len), F32)],
        compiler_params=pltpu.CompilerParams(vmem_limit_bytes=VMEM_LIMIT_BYTES),
    )(ps, pst, alog_l, dtb_l, alog_c, dtb_c)


def _gdn_gates_bwd_call(ps, alog_l, dtb_l, d_l, t_len):
    def body(ps_ref, al_ref, dl_ref, d_ref, dps_ref, gal_ref, gdt_ref):
        lane = lax.broadcasted_iota(jnp.int32, (1, LANES), 1)
        psv = ps_ref[...]
        dv = d_ref[...]
        beta = _sigmoid(psv)
        ea = jnp.exp(al_ref[...])
        arg = psv + dl_ref[...]
        g = -ea * _softplus(arg)
        d_a = dv * (-ea) * _sigmoid(arg)
        is_a = (lane >= GDN_HEADS) & (lane < 2 * GDN_HEADS)
        dps_ref[...] = jnp.where(lane < GDN_HEADS, dv * beta * (1.0 - beta), jnp.where(is_a, d_a, 0.0))
        gdt_ref[...] = jnp.where(is_a, jnp.sum(d_a, axis=0, keepdims=True), 0.0)
        gal_ref[...] = jnp.where(is_a, jnp.sum(dv * g, axis=0, keepdims=True), 0.0)

    vm = pl.BlockSpec(memory_space=pltpu.VMEM)
    return pl.pallas_call(
        body, name="gdn_gates_bwd",
        in_specs=[vm] * 4, out_specs=[vm] * 3,
        out_shape=[jax.ShapeDtypeStruct((t_len, LANES), F32),
                   jax.ShapeDtypeStruct((1, LANES), F32),
                   jax.ShapeDtypeStruct((1, LANES), F32)],
        compiler_params=pltpu.CompilerParams(vmem_limit_bytes=VMEM_LIMIT_BYTES),
    )(ps, alog_l, dtb_l, d_l)


def _bm(a, b):
    return _m3_general(a, b, _BNN)


def _bm_nt(a, b):
    return _m3_general(a, b, _BNT)


def _bm_tn(a, b):
    return _m3_general(a, b, _BTN)


def _heads_of(ref, rows):
    return jnp.stack([ref[rows, h * GDN_HEAD_DIM:(h + 1) * GDN_HEAD_DIM] for h in range(GDN_HEADS)])


def _chunk_terms(q_ref, k_ref, v_ref, b_ref, gc_ref, gr_ref, c, incl, strict):
    r0 = pl.multiple_of(c * CHUNK, CHUNK)
    rows = pl.ds(r0, CHUNK)
    q, k, v = _heads_of(q_ref, rows), _heads_of(k_ref, rows), _heads_of(v_ref, rows)
    lane_ids = lax.broadcasted_iota(jnp.int32, (1, LANES), 1)
    pick = lambda slab, first: jnp.stack([jnp.sum(jnp.where(lane_ids == first + h, slab, 0.0), axis=-1, keepdims=True)
                                          for h in range(GDN_HEADS)])
    b = pick(b_ref[rows, :], 0)
    gc = pick(gc_ref[rows, :], GDN_HEADS)
    gr = gr_ref[:, c]
    dm = jnp.where(incl, jnp.exp(jnp.where(incl, gc - gr, 0.0)), 0.0)
    kb = k * b
    vb = v * b
    e = jnp.exp(gc)
    a = jnp.where(strict, _bm_nt(kb, k) * dm, 0.0)
    p = jnp.where(incl, _bm_nt(q, k) * dm, 0.0)
    gl = gc[:, CHUNK - 1:CHUNK, :]
    eg = jnp.exp(gl - gc)
    return rows, q, k, v, b, gc, dm, kb, vb, e, a, p, gl, eg


def _gdn_specs(t_len, n_chunks, reverse):
    cps = GDN_CHUNKS_PER_STEP
    steps = n_chunks // cps
    at = (lambda g: steps - 1 - g) if reverse else (lambda g: g)
    rows_blk = lambda width, part=0: pl.BlockSpec((cps * CHUNK, width), lambda g: (at(g), part))
    gate_r = pl.BlockSpec((GDN_HEADS, cps, 1, CHUNK), lambda g: (0, at(g), 0, 0))
    per_chunk = lambda r, c: pl.BlockSpec((GDN_HEADS, cps, r, c), lambda g: (0, at(g), 0, 0))
    return cps, steps, rows_blk, gate_r, per_chunk


def _gdn_fwd_call(gact, beta_c, gam_c, gam_r, t_len):
    n_chunks = t_len // CHUNK
    dk = GDN_HEAD_DIM
    width = GDN_HEADS * dk
    cps, steps, rows_blk, gate_r, per_chunk = _gdn_specs(t_len, n_chunks, False)

    def body(q_ref, k_ref, v_ref, b_ref, gc_ref, gr_ref, o_ref, s_ref, t_ref, state_ref):
        row = lax.broadcasted_iota(jnp.int32, (CHUNK, CHUNK), 0)
        col = lax.broadcasted_iota(jnp.int32, (CHUNK, CHUNK), 1)
        incl, strict = row >= col, row > col
        eye = (row == col).astype(F32)

        @pl.when(pl.program_id(0) == 0)
        def _():
            state_ref[...] = jnp.zeros_like(state_ref)

        def chunk(c, carry):
            rows, q, k, v, b, gc, dm, kb, vb, e, a, p, gl, eg = _chunk_terms(
                q_ref, k_ref, v_ref, b_ref, gc_ref, gr_ref, c, incl, strict)
            s = state_ref[...]
            x = -a
            tm = eye + x
            xp = x
            for _ in range(5):
                xp = _bm(xp, xp)
                tm = tm + _bm(tm, xp)
            u = _bm(tm, vb)
            w = _bm(tm, kb * e)
            vn = u - _bm(w, s)
            o = _bm(q * e, s) + _bm(p, vn)
            for h in range(GDN_HEADS):
                o_ref[rows, h * dk:(h + 1) * dk] = o[h]
            s_ref[:, c] = s
            t_ref[:, c] = tm
            state_ref[...] = s * jnp.exp(gl) + _bm_tn(k * eg, vn)
            return carry

        lax.fori_loop(0, cps, chunk, 0)

    return pl.pallas_call(
        body, name="gdn_fwd",
        grid=(steps,),
        in_specs=[rows_blk(width, 0), rows_blk(width, 1), rows_blk(width, 2), rows_blk(LANES), rows_blk(LANES), gate_r],
        out_specs=[rows_blk(width), per_chunk(dk, dk), per_chunk(CHUNK, CHUNK)],
        out_shape=[jax.ShapeDtypeStruct((t_len, width), F32),
                   jax.ShapeDtypeStruct((GDN_HEADS, n_chunks, dk, dk), F32),
                   jax.ShapeDtypeStruct((GDN_HEADS, n_chunks, CHUNK, CHUNK), F32)],
        scratch_shapes=[pltpu.VMEM((GDN_HEADS, dk, dk), F32)],
        compiler_params=_params("arbitrary"),
    )(gact, gact, gact, beta_c, gam_c, gam_r)


def _gdn_bwd_call(gact, beta_c, gam_c, gam_r, s_all, t_all, d_o, t_len):
    n_chunks = t_len // CHUNK
    dk = GDN_HEAD_DIM
    width = GDN_HEADS * dk
    cps, steps, rows_blk, gate_r, per_chunk = _gdn_specs(t_len, n_chunks, True)

    def body(q_ref, k_ref, v_ref, b_ref, gc_ref, gr_ref, s_ref, t_ref, do_ref, d_ref, dgate_ref, dstate_ref):
        row = lax.broadcasted_iota(jnp.int32, (CHUNK, CHUNK), 0)
        col = lax.broadcasted_iota(jnp.int32, (CHUNK, CHUNK), 1)
        incl, strict = row >= col, row > col
        upper = jnp.broadcast_to((row <= col).astype(F32), (GDN_HEADS, CHUNK, CHUNK))
        ones = jnp.ones((GDN_HEADS, CHUNK, LANES), F32)
        last_row = lax.broadcasted_iota(jnp.int32, (CHUNK, 1), 0) == CHUNK - 1
        lane_ids = lax.broadcasted_iota(jnp.int32, (1, LANES), 1)
        rsum = lambda m: jnp.sum(m, axis=-1, keepdims=True)
        total = lambda m: jnp.sum(rsum(m), axis=1, keepdims=True)

        @pl.when(pl.program_id(0) == 0)
        def _():
            dstate_ref[...] = jnp.zeros_like(dstate_ref)

        def chunk(step, carry):
            c = cps - 1 - step
            rows, q, k, v, b, gc, dm, kb, vb, e, a, p, gl, eg = _chunk_terms(
                q_ref, k_ref, v_ref, b_ref, gc_ref, gr_ref, c, incl, strict)
            ds = dstate_ref[...]
            s = s_ref[:, c]
            tm = t_ref[:, c]
            d_out = _heads_of(do_ref, rows)
            el = jnp.exp(gl)
            kbe = kb * e
            u = _bm(tm, vb)
            w = _bm(tm, kbe)
            vn = u - _bm(w, s)
            qe = q * e
            kd = k * eg

            d_vn = _bm_tn(p, d_out) + _bm(kd, ds)
            d_qe = _bm_nt(d_out, s)
            d_p = jnp.where(incl, _bm_nt(d_out, vn), 0.0)
            dstate_ref[...] = el * ds + _bm_tn(qe, d_out) - _bm_tn(w, d_vn)
            d_kd = _bm_nt(vn, ds)
            d_w = -_bm_nt(d_vn, s)
            d_vb = _bm_tn(tm, d_vn)
            d_kbe = _bm_tn(tm, d_w)
            d_a = -jnp.where(strict, _bm_nt(d_vb, u) + _bm_nt(d_kbe, w), 0.0)
            m = d_a * dm
            n = d_p * dm
            d_kb = _bm(m, k) + d_kbe * e
            d_q = _bm(n, k) + d_qe * e
            d_k = _bm_tn(m, kb) + _bm_tn(n, q) + d_kd * eg + b * d_kb
            d_v = b * d_vb
            r = d_a * a + d_p * p
            kd_term = rsum(d_kd * kd)
            d_gl = total(ds * s) * el + jnp.sum(kd_term, axis=1, keepdims=True)
            d_gam = (rsum(r) - _bm_tn(r, ones)[:, :, 0:1] + rsum(d_qe * qe) + rsum(d_kbe * kbe) - kd_term
                     + jnp.where(last_row, d_gl, 0.0))
            d_beta = rsum(d_kb * k) + rsum(d_vb * v)
            d_g = _bm(upper, d_gam * ones)[:, :, 0:1]
            gates = jnp.zeros((CHUNK, LANES), F32)
            for h in range(GDN_HEADS):
                lanes = slice(h * dk, (h + 1) * dk)
                d_ref[0, rows, lanes] = d_q[h]
                d_ref[1, rows, lanes] = d_k[h]
                d_ref[2, rows, lanes] = d_v[h]
                gates = gates + (jnp.where(lane_ids == h, d_beta[h], 0.0)
                                 + jnp.where(lane_ids == GDN_HEADS + h, d_g[h], 0.0))
            dgate_ref[rows, :] = gates
            return carry

        lax.fori_loop(0, cps, chunk, 0)

    d_spec = pl.BlockSpec((3, cps * CHUNK, width), lambda g: (0, steps - 1 - g, 0))
    return pl.pallas_call(
        body, name="gdn_bwd",
        grid=(steps,),
        in_specs=[rows_blk(width, 0), rows_blk(width, 1), rows_blk(width, 2), rows_blk(LANES), rows_blk(LANES), gate_r,
                  per_chunk(dk, dk), per_chunk(CHUNK, CHUNK), rows_blk(width)],
        out_specs=[d_spec, rows_blk(LANES)],
        out_shape=[jax.ShapeDtypeStruct((3, t_len, width), F32),
                   jax.ShapeDtypeStruct((t_len, LANES), F32)],
        scratch_shapes=[pltpu.VMEM((GDN_HEADS, dk, dk), F32)],
        compiler_params=_params("arbitrary"),
    )(gact, gact, gact, beta_c, gam_c, gam_r, s_all, t_all, d_o)


def _group_matrix(width, group):
    r = lax.broadcasted_iota(jnp.int32, (width, width), 0)
    c = lax.broadcasted_iota(jnp.int32, (width, width), 1)
    return ((r // group) == (c // group)).astype(F32)


def _post_call(o_sb, o_gd, proj, x, target, w_out, sbw, gdw, fw, tm=256):
    t_len, d = x.shape
    half = 512
    zsb_blk = 1536 // half
    zgd_blk = 3584 // half

    def body(osb_ref, ogd_ref, zsb_ref, zgd_ref, x_ref, tg_ref, wo_ref, sbw_ref, gdw_ref, fw_ref,
             dx2_ref, dosb_ref, dogd_ref, dz_ref, loss_ref, gfw_ref, gsb_ref, ggd_ref, gwo_ref):
        step = pl.program_id(0)

        @pl.when(step == 0)
        def _():
            loss_ref[...] = jnp.zeros_like(loss_ref)
            gfw_ref[...] = jnp.zeros_like(gfw_ref)
            gsb_ref[...] = jnp.zeros_like(gsb_ref)
            ggd_ref[...] = jnp.zeros_like(ggd_ref)
            gwo_ref[...] = jnp.zeros_like(gwo_ref)

        def head_forward(o, z, w, gmat, inv):
            r = lax.rsqrt(_mx(o * o, gmat) * inv + EPS)
            nrm = o * r * w
            sg = _sigmoid(z)
            return r, nrm, sg, nrm * (z * sg)

        def head_backward(d_m, o, z, w, gmat, inv, r, nrm, sg):
            d_n = d_m * (z * sg)
            d_z = d_m * nrm * (sg * (1.0 + z * (1.0 - sg)))
            dnw = d_n * w
            d_o = r * dnw - o * (r * r * r) * (_mx(dnw * o, gmat) * inv)
            return d_o, d_z, jnp.sum(d_n * o * r, axis=0, keepdims=True)

        g_sb = _group_matrix(half, SB_HEAD_DIM)
        g_gd = _group_matrix(half, GDN_HEAD_DIM)
        osb, ogd, zsb, zgd = osb_ref[...], ogd_ref[...], zsb_ref[...], zgd_ref[...]
        sbw_v, gdw_v = sbw_ref[...], gdw_ref[...]
        r_sb, n_sb, sg_sb, m_sb = head_forward(osb, zsb, sbw_v, g_sb, 1.0 / SB_HEAD_DIM)
        r_gd, n_gd, sg_gd, m_gd = head_forward(ogd, zgd, gdw_v, g_gd, 1.0 / GDN_HEAD_DIM)
        mixed = jnp.concatenate([m_sb, m_gd], axis=1).astype(MXU_DTYPE)
        wo = wo_ref[...]
        x2 = x_ref[...] + jnp.dot(mixed, wo, preferred_element_type=F32)
        r2 = lax.rsqrt(jnp.mean(x2 * x2, axis=-1, keepdims=True) + EPS)
        fw_v = fw_ref[...]
        err = x2 * r2 * fw_v - tg_ref[...]
        loss_ref[...] += 0.5 * jnp.sum(jnp.sum(err * err, axis=-1, keepdims=True) * (1.0 / d))
        dy = err * (1.0 / d)
        gg = dy * fw_v
        dx2 = r2 * gg - x2 * ((r2 * r2 * r2) * jnp.mean(gg * x2, axis=-1, keepdims=True))
        gfw_ref[...] += jnp.sum(dy * x2 * r2, axis=0, keepdims=True)
        dx2_ref[...] = dx2
        dx2b = dx2.astype(MXU_DTYPE)
        d_mixed = lax.dot_general(dx2b, wo, _NT, preferred_element_type=F32)
        gwo_ref[...] += lax.dot_general(mixed, dx2b, _TN, preferred_element_type=F32)
        d_osb, d_zsb, gsb = head_backward(d_mixed[:, :half], osb, zsb, sbw_v, g_sb, 1.0 / SB_HEAD_DIM, r_sb, n_sb, sg_sb)
        d_ogd, d_zgd, ggd = head_backward(d_mixed[:, half:], ogd, zgd, gdw_v, g_gd, 1.0 / GDN_HEAD_DIM, r_gd, n_gd, sg_gd)
        dosb_ref[...] = d_osb
        dogd_ref[...] = d_ogd
        dz_ref[0] = d_zsb
        dz_ref[1] = d_zgd
        gsb_ref[...] += gsb
        ggd_ref[...] += ggd

    row_blk = lambda w: pl.BlockSpec((tm, w), lambda i: (i, 0))
    fixed = lambda r, w: pl.BlockSpec((r, w), lambda i: (0, 0))
    return pl.pallas_call(
        body, name="post",
        grid=(t_len // tm,),
        in_specs=[row_blk(half), row_blk(half),
                  pl.BlockSpec((tm, half), lambda i: (i, zsb_blk)),
                  pl.BlockSpec((tm, half), lambda i: (i, zgd_blk)),
                  row_blk(d), row_blk(d), fixed(d, d), fixed(1, half), fixed(1, half), fixed(1, d)],
        out_specs=[row_blk(d), row_blk(half), row_blk(half),
                   pl.BlockSpec((2, tm, half), lambda i: (DPROJ_GATE_SLOT // 2, i, 0)),
                   fixed(1, LANES), fixed(1, d), fixed(1, half), fixed(1, half), fixed(d, d)],
        out_shape=[jax.ShapeDtypeStruct((t_len, d), F32)] + [jax.ShapeDtypeStruct((t_len, half), F32)] * 2
                  + [jax.ShapeDtypeStruct((len(DPROJ_PIECE_OF_SLOT), t_len, half), F32),
                     jax.ShapeDtypeStruct((1, LANES), F32), jax.ShapeDtypeStruct((1, d), F32),
                     jax.ShapeDtypeStruct((1, half), F32), jax.ShapeDtypeStruct((1, half), F32),
                     jax.ShapeDtypeStruct((d, d), F32)],
        compiler_params=_params("arbitrary"),
    )(o_sb, o_gd, proj, proj, x, target, w_out, sbw, gdw, fw)


def _piece_of_slot(s):
    return jnp.where(s < DPROJ_GDN_SLOT, s, jnp.where(s < DPROJ_GATE_SLOT, s + 1,
                                                     jnp.where(s == DPROJ_GATE_SLOT, 3, 7)))


def _gw_in_call(h_t, dproj8):
    d, t_len = h_t.shape
    n_piece, _, pw = dproj8.shape

    def body(ht_ref, dp_ref, gw_ref):
        gw_ref[...] = jnp.dot(ht_ref[...], dp_ref[0].astype(MXU_DTYPE), preferred_element_type=F32)

    return pl.pallas_call(
        body, name="gw_in",
        grid=(n_piece,),
        in_specs=[pl.BlockSpec((d, t_len), lambda s: (0, 0)),
                  pl.BlockSpec((1, t_len, pw), lambda s: (s, 0, 0))],
        out_specs=pl.BlockSpec((d, pw), lambda s: (0, _piece_of_slot(s))),
        out_shape=jax.ShapeDtypeStruct((d, n_piece * pw), F32),
        compiler_params=_params("arbitrary"),
    )(h_t, dproj8)


def _gw_small_call(h_t, dsmall, tm=512):
    d, t_len = h_t.shape
    ns = dsmall.shape[1]

    def body(ht_ref, dp_ref, gw_ref):
        @pl.when(pl.program_id(0) == 0)
        def _():
            gw_ref[...] = jnp.zeros_like(gw_ref)

        gw_ref[...] += jnp.dot(ht_ref[...], dp_ref[...].astype(MXU_DTYPE), preferred_element_type=F32)

    return pl.pallas_call(
        body, name="gw_small",
        grid=(t_len // tm,),
        in_specs=[pl.BlockSpec((d, tm), lambda t: (0, t)),
                  pl.BlockSpec((tm, ns), lambda t: (t, 0))],
        out_specs=pl.BlockSpec((d, ns), lambda t: (0, 0)),
        out_shape=jax.ShapeDtypeStruct((d, ns), F32),
        compiler_params=_params("arbitrary"),
    )(h_t, dsmall)


def _dx_call(dproj8, dsmall, w_main, w_small, x, r, dx2, norm_w, tm=256):
    t_len, d = x.shape
    n_piece, _, pw = dproj8.shape
    ns = dsmall.shape[1]

    def body(dp_ref, ds_ref, wm_ref, ws_ref, x_ref, r_ref, dx2_ref, nw_ref, gx_ref, gnw_ref):
        @pl.when(pl.program_id(0) == 0)
        def _():
            gnw_ref[...] = jnp.zeros_like(gnw_ref)

        dh = lax.dot_general(ds_ref[...].astype(MXU_DTYPE), ws_ref[...], _NT, preferred_element_type=F32)
        for s, p in enumerate(DPROJ_PIECE_OF_SLOT):
            dh = dh + lax.dot_general(dp_ref[s].astype(MXU_DTYPE), wm_ref[:, p * pw:(p + 1) * pw], _NT,
                                      preferred_element_type=F32)
        xv, rv = x_ref[...], r_ref[...]
        dn = dh * nw_ref[...]
        gx_ref[...] = dx2_ref[...] + rv * dn - xv * ((rv * rv * rv) * jnp.mean(dn * xv, axis=-1, keepdims=True))
        gnw_ref[...] += jnp.sum(dh * xv * rv, axis=0, keepdims=True)

    return pl.pallas_call(
        body, name="dx",
        grid=(t_len // tm,),
        in_specs=[pl.BlockSpec((n_piece, tm, pw), lambda i: (0, i, 0)),
                  pl.BlockSpec((tm, ns), lambda i: (i, 0)),
                  pl.BlockSpec((d, n_piece * pw), lambda i: (0, 0)),
                  pl.BlockSpec((d, ns), lambda i: (0, 0)),
                  pl.BlockSpec((tm, d), lambda i: (i, 0)),
                  pl.BlockSpec((tm, 1), lambda i: (i, 0)),
                  pl.BlockSpec((tm, d), lambda i: (i, 0)),
                  pl.BlockSpec((1, d), lambda i: (0, 0))],
        out_specs=[pl.BlockSpec((tm, d), lambda i: (i, 0)),
                   pl.BlockSpec((1, d), lambda i: (0, 0))],
        out_shape=[jax.ShapeDtypeStruct((t_len, d), F32), jax.ShapeDtypeStruct((1, d), F32)],
        compiler_params=_params("arbitrary"),
    )(dproj8, dsmall, w_main, w_small, x, r, dx2, norm_w)


def _exchange_call(name, srcs, per_peer):
    n = len(srcs)
    out_shapes = [jax.ShapeDtypeStruct(s.shape if pp else (N_DEV,) + s.shape, s.dtype) for s, pp in zip(srcs, per_peer)]

    def body(*refs):
        src_refs, out_refs = refs[:n], refs[n:2 * n]
        send_sems, recv_sems, local_sems = refs[2 * n:]
        x, y, c = lax.axis_index("x"), lax.axis_index("y"), lax.axis_index("c")
        me = 4 * x + 2 * y + c
        copies = []
        for a in range(n):
            mine = src_refs[a].at[me] if per_peer[a] else src_refs[a]
            local = pltpu.make_async_copy(mine, out_refs[a].at[me], local_sems.at[a])
            local.start()
            copies.append(local)
        remote = []
        for k in range(1, N_DEV):
            kx, ky, kc = (k >> 2) & 1, (k >> 1) & 1, k & 1
            px = 1 - x if kx else x
            py = 1 - y if ky else y
            pc = 1 - c if kc else c
            peer = 4 * px + 2 * py + pc
            for a in range(n):
                sem = a * (N_DEV - 1) + (k - 1)
                src = src_refs[a].at[peer] if per_peer[a] else src_refs[a]
                cp = pltpu.make_async_remote_copy(
                    src_ref=src, dst_ref=out_refs[a].at[me],
                    send_sem=send_sems.at[sem], recv_sem=recv_sems.at[sem],
                    device_id=(px, py, pc), device_id_type=pl.DeviceIdType.MESH)
                cp.start()
                remote.append(cp)
        for cp in remote:
            cp.wait_send()
        for cp in remote:
            cp.wait_recv()
        for cp in copies:
            cp.wait()

    hbm = pl.BlockSpec(memory_space=pl.ANY)
    return pl.pallas_call(
        body, name=name,
        in_specs=[hbm] * n, out_specs=[hbm] * n, out_shape=out_shapes,
        scratch_shapes=[pltpu.SemaphoreType.DMA((n * (N_DEV - 1),)),
                        pltpu.SemaphoreType.DMA((n * (N_DEV - 1),)),
                        pltpu.SemaphoreType.DMA((n,))],
    )(*srcs)


N_CHIPS = 4
_HBM = pl.BlockSpec(memory_space=pl.ANY)
_MESH = pl.DeviceIdType.MESH


def _gather_call(name, srcs):
    n = len(srcs)
    per = N_DEV - 1

    def body(*refs):
        src_refs, out_refs = refs[:n], refs[n:2 * n]
        send_sems, recv_sems, local_sems = refs[2 * n:]
        x, y, c = lax.axis_index("x"), lax.axis_index("y"), lax.axis_index("c")
        me, sibling = (x, y, c), (x, y, 1 - c)
        chips = [(1 - x, y), (x, 1 - y), (1 - x, 1 - y)]
        slot = lambda px, py, pc: 4 * px + 2 * py + pc

        def copy(a, k, block, to, from_src=False):
            rows = out_refs[a].at[slot(*block)]
            return pltpu.make_async_remote_copy(
                src_ref=src_refs[a] if from_src else rows, dst_ref=rows,
                send_sem=send_sems.at[a * per + k], recv_sem=recv_sems.at[a * per + k],
                device_id=to, device_id_type=_MESH)

        local = [pltpu.make_async_copy(src_refs[a], out_refs[a].at[slot(*me)], local_sems.at[a]) for a in range(n)]
        for cp in local:
            cp.start()
        first = []
        for a in range(n):
            first.append(copy(a, 0, me, sibling, True))
            first += [copy(a, 1 + j, me, (*chip, c), True) for j, chip in enumerate(chips)]
        for cp in first:
            cp.start()
        passed = []
        for j, chip in enumerate(chips):
            for a in range(n):
                copy(a, 1 + j, (*chip, c), me).wait_recv()
                fwd = copy(a, 4 + j, (*chip, c), sibling)
                fwd.start()
                passed.append(fwd)
        for a in range(n):
            copy(a, 0, sibling, me).wait_recv()
            for j, chip in enumerate(chips):
                copy(a, 4 + j, (*chip, 1 - c), me).wait_recv()
        for cp in first + passed:
            cp.wait_send()
        for cp in local:
            cp.wait()

    return pl.pallas_call(
        body, name=name,
        in_specs=[_HBM] * n, out_specs=[_HBM] * n,
        out_shape=[jax.ShapeDtypeStruct((N_DEV,) + s.shape, s.dtype) for s in srcs],
        scratch_shapes=[pltpu.SemaphoreType.DMA((n * per,)), pltpu.SemaphoreType.DMA((n * per,)),
                        pltpu.SemaphoreType.DMA((n,))],
    )(*srcs)


def _sibling_send_call(name, srcs):
    n = len(srcs)

    def body(*refs):
        src_refs, out_refs = refs[:n], refs[n:2 * n]
        send_sems, recv_sems = refs[2 * n:]
        x, y, c = lax.axis_index("x"), lax.axis_index("y"), lax.axis_index("c")
        copies = []
        for a in range(n):
            for ch in range(N_CHIPS):
                copies.append(pltpu.make_async_remote_copy(
                    src_ref=src_refs[a].at[2 * ch + (1 - c)], dst_ref=out_refs[a].at[ch],
                    send_sem=send_sems.at[a * N_CHIPS + ch], recv_sem=recv_sems.at[a * N_CHIPS + ch],
                    device_id=(x, y, 1 - c), device_id_type=_MESH))
        for cp in copies:
            cp.start()
        for cp in copies:
            cp.wait_send()
        for cp in copies:
            cp.wait_recv()

    return pl.pallas_call(
        body, name=name,
        in_specs=[_HBM] * n, out_specs=[_HBM] * n,
        out_shape=[jax.ShapeDtypeStruct((N_CHIPS,) + s.shape[1:], s.dtype) for s in srcs],
        scratch_shapes=[pltpu.SemaphoreType.DMA((n * N_CHIPS,)), pltpu.SemaphoreType.DMA((n * N_CHIPS,))],
    )(*srcs)


def _pair_sum_call(name, parts, from_sibling, tr):
    _, rows, cols = parts.shape

    def body(p_ref, s_ref, o_ref):
        o_ref[...] = (p_ref[...] + s_ref[...]).astype(o_ref.dtype)

    return pl.pallas_call(
        body, name=name,
        grid=(N_CHIPS, rows // tr),
        in_specs=[pl.BlockSpec((1, tr, cols), lambda ch, i: (2 * ch + lax.axis_index("c"), i, 0)),
                  pl.BlockSpec((1, tr, cols), lambda ch, i: (ch, i, 0))],
        out_specs=pl.BlockSpec((1, tr, cols), lambda ch, i: (ch, i, 0)),
        out_shape=jax.ShapeDtypeStruct((N_CHIPS, rows, cols), WIRE_DTYPE),
        compiler_params=_params("arbitrary", "arbitrary"),
    )(parts, from_sibling)


def _chip_exchange_call(name, srcs):
    n = len(srcs)
    per = N_CHIPS - 1

    def body(*refs):
        src_refs, out_refs = refs[:n], refs[n:2 * n]
        send_sems, recv_sems, local_sems = refs[2 * n:]
        x, y, c = lax.axis_index("x"), lax.axis_index("y"), lax.axis_index("c")
        mine = 2 * x + y
        chips = [(1 - x, y), (x, 1 - y), (1 - x, 1 - y)]
        local = [pltpu.make_async_copy(src_refs[a].at[mine], out_refs[a].at[mine], local_sems.at[a]) for a in range(n)]
        for cp in local:
            cp.start()
        remote = []
        for a in range(n):
            for j, (px, py) in enumerate(chips):
                remote.append(pltpu.make_async_remote_copy(
                    src_ref=src_refs[a].at[2 * px + py], dst_ref=out_refs[a].at[mine],
                    send_sem=send_sems.at[a * per + j], recv_sem=recv_sems.at[a * per + j],
                    device_id=(px, py, c), device_id_type=_MESH))
        for cp in remote:
            cp.start()
        for cp in remote:
            cp.wait_send()
        for cp in remote:
            cp.wait_recv()
        for cp in local:
            cp.wait()

    return pl.pallas_call(
        body, name=name,
        in_specs=[_HBM] * n, out_specs=[_HBM] * n,
        out_shape=[jax.ShapeDtypeStruct(s.shape, s.dtype) for s in srcs],
        scratch_shapes=[pltpu.SemaphoreType.DMA((n * per,)), pltpu.SemaphoreType.DMA((n * per,)),
                        pltpu.SemaphoreType.DMA((n,))],
    )(*srcs)


def _adam_call(name, parts, w, m, v, tr):
    rows, cols = w.shape
    n_slots = parts.shape[0]

    def body(p_ref, w_ref, m_ref, v_ref, g_ref, d_ref, nm_ref, nv_ref):
        g = p_ref[0].astype(F32)
        for s in range(1, n_slots):
            g = g + p_ref[s].astype(F32)
        m_new = ADAM_B1 * m_ref[...] + (1.0 - ADAM_B1) * g
        v_new = ADAM_B2 * v_ref[...] + (1.0 - ADAM_B2) * (g * g)
        m_hat = m_new / (1.0 - ADAM_B1 ** ADAM_STEP)
        v_hat = v_new / (1.0 - ADAM_B2 ** ADAM_STEP)
        g_ref[...] = g
        d_ref[...] = -ADAM_LR * (m_hat / (jnp.sqrt(v_hat) + ADAM_EPS) + ADAM_WD * w_ref[...])
        nm_ref[...] = m_new
        nv_ref[...] = v_new

    blk = pl.BlockSpec((tr, cols), lambda i: (i, 0))
    return pl.pallas_call(
        body, name=name,
        grid=(rows // tr,),
        in_specs=[pl.BlockSpec((n_slots, tr, cols), lambda i: (0, i, 0)), blk, blk, blk],
        out_specs=[blk] * 4,
        out_shape=[jax.ShapeDtypeStruct((rows, cols), F32)] * 4,
        compiler_params=_params("arbitrary"),
    )(parts, w, m, v)


_SMALL_ROWS = ("norm1_w", "final_norm_w", "sb_norm_w", "gdn_norm_w", "gdn_A_log", "gdn_dt_bias", "loss")


def _pack_small(vals, width):
    rows = [jnp.pad(a.reshape(1, -1).astype(F32), ((0, 0), (0, width - a.size))) for a in vals]
    rows += [jnp.zeros((1, width), F32)] * (8 - len(rows))
    return jnp.concatenate(rows, axis=0)


def _device_step(x2d, tgt, w_full, w_out_f32, conv_full, norm1_w, sb_norm_w, gdn_A_log, gdn_dt_bias, gdn_norm_w,
                 final_norm_w):
    t_len, d = x2d.shape
    n_chunks = t_len // CHUNK
    n_main = 8 * 512
    n_small = w_full.shape[1] - n_main
    w_main = w_full[:, :n_main].astype(MXU_DTYPE)
    w_small = jnp.pad(w_full[:, n_main:], ((0, 0), (0, LANES - n_small))).astype(MXU_DTYPE)
    w_small_t = w_full[:, n_main:].T.astype(MXU_DTYPE)
    w_out_full = w_out_f32.astype(MXU_DTYPE)

    pad_lanes = lambda a, lo: jnp.pad(a.reshape(1, -1), ((0, 0), (lo, LANES - lo - a.size)))
    alog_l, dtb_l = pad_lanes(gdn_A_log, GDN_HEADS), pad_lanes(gdn_dt_bias, GDN_HEADS)
    alog_c, dtb_c = alog_l[:, :8].T, dtb_l[:, :8].T
    sbw = jnp.tile(sb_norm_w, (1, 512 // SB_HEAD_DIM))
    gdw = jnp.tile(gdn_norm_w, (1, 512 // GDN_HEAD_DIM))
    fw = final_norm_w.reshape(1, d)

    proj, ps, pst, h_t, r1 = _inproj_call(x2d, norm1_w, w_main, w_small, w_small_t)
    o_sb, sp_total, sb_blocks_run = _sb_fwd_call(proj, t_len)
    gact = _gdn_prep_call(proj, conv_full, t_len)
    beta_l, gcol_l, grow = _gdn_gates_call(ps, pst, alog_l, dtb_l, alog_c, dtb_c, t_len)
    gam_r = grow[GDN_HEADS:2 * GDN_HEADS].reshape(GDN_HEADS, n_chunks, 1, CHUNK)
    o_gd, s_all, t_all = _gdn_fwd_call(gact, beta_l, gcol_l, gam_r, t_len)

    (dx2, d_osb, d_ogd, dproj8, loss_p, g_fw, g_sbw, g_gdw, g_wout) = _post_call(
        o_sb, o_gd, proj, x2d, tgt, w_out_full, sbw, gdw, fw)

    dproj8 = _sb_bwd_call(proj, sp_total, sb_blocks_run, d_osb, dproj8, t_len)
    d_gact3, d_gates = _gdn_bwd_call(gact, beta_l, gcol_l, gam_r, s_all, t_all, d_ogd, t_len)
    dproj8, g_conv = _gdn_prep_bwd_call(proj, conv_full, d_gact3, dproj8, t_len)
    dsmall, g_alog, g_dtb = _gdn_gates_bwd_call(ps, alog_l, dtb_l, d_gates, t_len)

    g_w_main = _gw_in_call(h_t, dproj8)
    g_w_small = _gw_small_call(h_t, dsmall)
    grad_x, g_n1 = _dx_call(dproj8, dsmall, w_main, w_small, x2d, r1, dx2, norm1_w)
    g_w_in_full = jnp.concatenate([g_w_main, g_w_small[:, :n_small]], axis=1)
    return (loss_p, grad_x, g_n1, g_w_in_full, g_sbw, g_conv, g_alog, g_dtb, g_gdw, g_wout, g_fw)


def kernel(x, norm1_w, w_in, sb_norm_w, gdn_conv_w, gdn_A_log, gdn_dt_bias, gdn_norm_w, w_out, final_norm_w, loss_target, m_norm1_w, m_w_in, m_sb_norm_w, m_gdn_conv_w, m_gdn_A_log, m_gdn_dt_bias, m_gdn_norm_w, m_w_out, m_final_norm_w, v_norm1_w, v_w_in, v_sb_norm_w, v_gdn_conv_w, v_gdn_A_log, v_gdn_dt_bias, v_gdn_norm_w, v_w_out, v_final_norm_w):
    d = x.shape[2]
    shard_cols = w_in.shape[2]
    conv_cols = gdn_conv_w.shape[2]

    w_in_g, w_out_g, conv_g = _gather_call(
        "gather_weights", [w_in[0].astype(WIRE_DTYPE), w_out[0].astype(WIRE_DTYPE), gdn_conv_w[0]])
    w_full = w_in_g.transpose(1, 0, 2).reshape(d, N_DEV * shard_cols)
    conv_full = conv_g.transpose(1, 0, 2).reshape(CONV_WIDTH, N_DEV * conv_cols)

    (loss_p, grad_x, g_n1, g_w_in_full, g_sbw, g_conv, g_alog, g_dtb, g_gdw, g_wout, g_fw) = _device_step(
        x[0], loss_target[0], w_full, w_out_g.reshape(d, d), conv_full, norm1_w, sb_norm_w, gdn_A_log, gdn_dt_bias,
        gdn_norm_w, final_norm_w)

    g_w_in_parts = g_w_in_full.reshape(d, N_DEV, shard_cols).transpose(1, 0, 2)
    g_wout_parts = g_wout.reshape(N_DEV, d // N_DEV, d)
    g_conv_parts = g_conv.reshape(CONV_WIDTH, N_DEV, conv_cols).transpose(1, 0, 2)
    fold = lambda a, group: a.reshape(-1, group).sum(axis=0)
    small_g = _pack_small([g_n1, g_fw, fold(g_sbw, SB_HEAD_DIM), fold(g_gdw, GDN_HEAD_DIM),
                           g_alog[0, GDN_HEADS:2 * GDN_HEADS], g_dtb[0, GDN_HEADS:2 * GDN_HEADS],
                           loss_p[0, :1]], d)
    sib_w_in, sib_wout, sib_conv = _sibling_send_call("grads_to_sibling", [g_w_in_parts, g_wout_parts, g_conv_parts])
    c_w_in = _pair_sum_call("pair_sum_w_in", g_w_in_parts, sib_w_in, 256)
    c_wout = _pair_sum_call("pair_sum_w_out", g_wout_parts, sib_wout, d // N_DEV)
    c_conv = _pair_sum_call("pair_sum_conv", g_conv_parts, sib_conv, CONV_WIDTH)
    p_w_in, p_wout, p_conv = _chip_exchange_call("grads_to_chips", [c_w_in, c_wout, c_conv])
    (p_small,) = _exchange_call("exchange_small", [small_g], [False])

    small_w = _pack_small([norm1_w, final_norm_w, sb_norm_w, gdn_norm_w, gdn_A_log, gdn_dt_bias], d)
    small_m = _pack_small([m_norm1_w, m_final_norm_w, m_sb_norm_w, m_gdn_norm_w, m_gdn_A_log, m_gdn_dt_bias], d)
    small_v = _pack_small([v_norm1_w, v_final_norm_w, v_sb_norm_w, v_gdn_norm_w, v_gdn_A_log, v_gdn_dt_bias], d)

    r_w_in = _adam_call("adam_w_in", p_w_in, w_in[0], m_w_in[0], v_w_in[0], 256)
    r_wout = _adam_call("adam_w_out", p_wout, w_out[0], m_w_out[0], v_w_out[0], d // N_DEV)
    r_conv = _adam_call("adam_conv", p_conv, gdn_conv_w[0], m_gdn_conv_w[0], v_gdn_conv_w[0], CONV_WIDTH)
    r_small = _adam_call("adam_small", p_small, small_w, small_m, small_v, 8)

    shapes = {"norm1_w": norm1_w.shape, "final_norm_w": final_norm_w.shape, "sb_norm_w": sb_norm_w.shape,
              "gdn_norm_w": gdn_norm_w.shape, "gdn_A_log": gdn_A_log.shape, "gdn_dt_bias": gdn_dt_bias.shape}

    def small_out(kind, name):
        row = _SMALL_ROWS.index(name)
        shp = shapes[name]
        size = 1
        for s in shp:
            size *= s
        return r_small[kind][row, :size].reshape(shp)

    def outputs(kind):
        return (small_out(kind, "norm1_w"), r_w_in[kind][None], small_out(kind, "sb_norm_w"), r_conv[kind][None],
                small_out(kind, "gdn_A_log"), small_out(kind, "gdn_dt_bias"), small_out(kind, "gdn_norm_w"),
                r_wout[kind][None], small_out(kind, "final_norm_w"))

    loss = r_small[0][_SMALL_ROWS.index("loss"), 0]
    return (loss, grad_x[None], *outputs(0), *outputs(1), *outputs(2), *outputs(3))
```

```python
import functools

import jax
import jax.numpy as jnp
from jax import lax
from jax.experimental import pallas as pl
from jax.experimental.pallas import tpu as pltpu

F32 = jnp.float32
MXU_DTYPE = jnp.bfloat16
WIRE_DTYPE = jnp.bfloat16
EXACT = lax.Precision.HIGHEST
EPS = 1e-6
N_DEV = 8
SB_HEAD_DIM = 64
GDN_HEAD_DIM = 128
GDN_HEADS = 4
GDN_CHUNKS_PER_STEP = 4
CHUNK = 64
CONV_WIDTH = 4
LANES = 128
SB_BLOCK = 128
SB_BQ = 256
VMEM_LIMIT_BYTES = 56 * 1024 * 1024

DPROJ_PIECE_OF_SLOT = (0, 1, 2, 4, 5, 6, 3, 7)
DPROJ_SB_SLOT, DPROJ_GDN_SLOT, DPROJ_GATE_SLOT = 0, 3, 6

ADAM_LR = 0.001
ADAM_B1 = 0.9
ADAM_B2 = 0.999
ADAM_EPS = 1e-08
ADAM_WD = 0.01
ADAM_STEP = 10

_NN = (((1,), (0,)), ((), ()))
_NT = (((1,), (1,)), ((), ()))
_TN = (((0,), (0,)), ((), ()))
_BNN = (((2,), (1,)), ((0,), (0,)))
_BNT = (((2,), (2,)), ((0,), (0,)))
_BTN = (((1,), (1,)), ((0,), (0,)))


def _mm(a, b):
    return jnp.dot(a.astype(MXU_DTYPE), b.astype(MXU_DTYPE), preferred_element_type=F32)


def _mm_nt(a, b):
    return lax.dot_general(a.astype(MXU_DTYPE), b.astype(MXU_DTYPE), _NT, preferred_element_type=F32)


def _mm_tn(a, b):
    return lax.dot_general(a.astype(MXU_DTYPE), b.astype(MXU_DTYPE), _TN, preferred_element_type=F32)


def _mx(a, b):
    return jnp.dot(a, b, precision=EXACT, preferred_element_type=F32)


def _mx_nt(a, b):
    return lax.dot_general(a, b, _NT, precision=EXACT, preferred_element_type=F32)


def _mx_tn(a, b):
    return lax.dot_general(a, b, _TN, precision=EXACT, preferred_element_type=F32)


def _split(x):
    hi = x.astype(MXU_DTYPE)
    return hi, (x - hi.astype(F32)).astype(MXU_DTYPE)


def _m3_general(a, b, dims):
    ah, al = _split(a)
    bh, bl = _split(b)
    dot = lambda x, y: lax.dot_general(x, y, dims, preferred_element_type=F32)
    (contract, _), (batch, _) = dims
    free = [ax for ax in range(a.ndim) if ax not in contract and ax not in batch][0]
    m = a.shape[free]
    both = dot(jnp.concatenate([ah, al], axis=free), bh)
    out_axis = len(batch)
    hi_part = lax.slice_in_dim(both, 0, m, axis=out_axis)
    lo_part = lax.slice_in_dim(both, m, 2 * m, axis=out_axis)
    return hi_part + (dot(ah, bl) + lo_part)


def _m3(a, b):
    return _m3_general(a, b, _NN)


def _m3_nt(a, b):
    return _m3_general(a, b, _NT)


def _m3_tn(a, b):
    return _m3_general(a, b, _TN)


def _sigmoid(z):
    return 1.0 / (1.0 + jnp.exp(-z))


def _softplus(z):
    return jnp.maximum(z, 0.0) + jnp.log(1.0 + jnp.exp(-jnp.abs(z)))


def _params(*semantics):
    return pltpu.CompilerParams(dimension_semantics=semantics, vmem_limit_bytes=VMEM_LIMIT_BYTES)


def _inproj_call(x, norm_w, w_main, w_small, w_small_t, tm=256):
    t_len, d = x.shape
    n = w_main.shape[1]
    ns = w_small.shape[1]
    nst = w_small_t.shape[0]

    def body(x_ref, nw_ref, wm_ref, ws_ref, wst_ref, pm_ref, ps_ref, pst_ref, ht_ref, r_ref):
        xv = x_ref[...]
        r = lax.rsqrt(jnp.mean(xv * xv, axis=-1, keepdims=True) + EPS)
        h = xv * r * nw_ref[...]
        hb = h.astype(MXU_DTYPE)
        for n0 in range(0, n, 512):
            pm_ref[:, n0:n0 + 512] = jnp.dot(hb, wm_ref[:, n0:n0 + 512], preferred_element_type=F32)
        ps_ref[...] = jnp.dot(hb, ws_ref[...], preferred_element_type=F32)
        pst_ref[...] = lax.dot_general(wst_ref[...], hb, _NT, preferred_element_type=F32)
        ht_ref[...] = h.T.astype(MXU_DTYPE)
        r_ref[...] = r

    return pl.pallas_call(
        body, name="inproj",
        grid=(t_len // tm,),
        in_specs=[pl.BlockSpec((tm, d), lambda i: (i, 0)),
                  pl.BlockSpec((1, d), lambda i: (0, 0)),
                  pl.BlockSpec((d, n), lambda i: (0, 0)),
                  pl.BlockSpec((d, ns), lambda i: (0, 0)),
                  pl.BlockSpec((nst, d), lambda i: (0, 0))],
        out_specs=[pl.BlockSpec((tm, n), lambda i: (i, 0)),
                   pl.BlockSpec((tm, ns), lambda i: (i, 0)),
                   pl.BlockSpec((nst, tm), lambda i: (0, i)),
                   pl.BlockSpec((d, tm), lambda i: (0, i)),
                   pl.BlockSpec((tm, 1), lambda i: (i, 0))],
        out_shape=[jax.ShapeDtypeStruct((t_len, n), F32),
                   jax.ShapeDtypeStruct((t_len, ns), F32),
                   jax.ShapeDtypeStruct((nst, t_len), F32),
                   jax.ShapeDtypeStruct((d, t_len), MXU_DTYPE),
                   jax.ShapeDtypeStruct((t_len, 1), F32)],
        compiler_params=_params("arbitrary"),
    )(x, norm_w, w_main, w_small, w_small_t)


def _running_sum_mm(x, tri):
    hi = x.astype(MXU_DTYPE)
    lo = (x - hi.astype(F32)).astype(MXU_DTYPE)
    return jnp.dot(hi, tri, preferred_element_type=F32) + jnp.dot(lo, tri, preferred_element_type=F32)


def _sb_iotas():
    row_i = lax.broadcasted_iota(jnp.int32, (SB_BQ, SB_BLOCK), 0)
    col_i = lax.broadcasted_iota(jnp.int32, (SB_BQ, SB_BLOCK), 1)
    sq_r = lax.broadcasted_iota(jnp.int32, (SB_BLOCK, SB_BLOCK), 0)
    sq_c = lax.broadcasted_iota(jnp.int32, (SB_BLOCK, SB_BLOCK), 1)
    return row_i, col_i, sq_r, sq_c


SB_DIAG_BLOCKS = SB_BQ // SB_BLOCK
SB_EXP_FLOOR = -110.0


def _sb_keys_descending(qi, tile, carry, z_bounds, n_heads):
    n_free = SB_DIAG_BLOCKS * qi
    for j in range(SB_DIAG_BLOCKS - 1, -1, -1):
        carry = tile(n_free + j, True, carry)

    def largest_exponent(c):
        worst = jnp.max(z_bounds[0] - c[1])
        for h in range(1, n_heads):
            worst = jnp.maximum(worst, jnp.max(z_bounds[h] - c[1 + h]))
        return worst

    def cond(state):
        return (state[0] < n_free) & (state[1] > SB_EXP_FLOOR)

    def body(state):
        c = tile(n_free - 1 - state[0], False, state[2:])
        return (state[0] + 1, largest_exponent(c), *c)

    out = lax.while_loop(cond, body, (jnp.int32(0), largest_exponent(carry), *carry))
    return out[2:], out[0]


def _sb_keys_ascending(qi, n_run, tile, carry):
    n_free = SB_DIAG_BLOCKS * qi
    carry = lax.fori_loop(0, n_run, lambda s, c: tile(n_free - n_run + s, False, c), carry)
    for j in range(SB_DIAG_BLOCKS):
        carry = tile(n_free + j, True, carry)
    return carry


def _sb_fwd_call(proj, t_len):
    nq = t_len // SB_BQ
    scale = float(SB_HEAD_DIM) ** -0.5
    n_pairs = 512 // LANES
    per_pair = LANES // SB_HEAD_DIM

    def body(q_ref, k_ref, v_ref, o_ref, st_ref, nrun_ref):
        lane = lax.broadcasted_iota(jnp.int32, (1, LANES), 1)
        row_i, col_i, sq_r, sq_c = _sb_iotas()
        ge = (sq_r >= sq_c).astype(MXU_DTYPE)
        hms = [((lane // SB_HEAD_DIM) == hh).astype(F32) for hh in range(per_pair)]
        k_sq = k_ref[...] * k_ref[...]
        k_norms = [jnp.sqrt(jnp.max(jnp.sum(k_sq * hm, axis=-1, keepdims=True))) * (1.02 * scale) for hm in hms]

        def q_loop(qi, carry):
            r0 = pl.multiple_of(qi * SB_BQ, SB_BQ)
            rows = pl.ds(r0, SB_BQ)
            q_all = q_ref[rows, :]
            qms = [(q_all * (hm * scale)).astype(MXU_DTYPE) for hm in hms]
            z_bounds = [jnp.sqrt(jnp.sum(q_all * q_all * hm, axis=-1, keepdims=True)) * kn
                        for hm, kn in zip(hms, k_norms)]

            def tile(kj, masked, kc):
                acc, cs = kc[0], list(kc[1:])
                heads = range(per_pair)
                s0 = pl.multiple_of(kj * SB_BLOCK, SB_BLOCK)
                cols = pl.ds(s0, SB_BLOCK)
                kb = k_ref[cols, :].astype(MXU_DTYPE)
                v_all = v_ref[cols, :]
                vms = [(v_all * hms[h]).astype(MXU_DTYPE) for h in heads]
                zs = [lax.dot_general(qms[h], kb, _NT, preferred_element_type=F32) for h in heads]
                sps = [_softplus(z) for z in zs]
                if masked:
                    mask = (col_i + s0) < (row_i + r0)
                    sps = [jnp.where(mask, sp, 0.0) for sp in sps]
                sums = [_running_sum_mm(sp, ge) for sp in sps]
                ws = [jnp.exp(zs[h] - (sums[h] + cs[h])) for h in heads]
                if masked:
                    ws = [jnp.where(mask, w, 0.0) for w in ws]
                for h in heads:
                    acc = acc + jnp.dot(ws[h].astype(MXU_DTYPE), vms[h], preferred_element_type=F32)
                cs = [cs[h] + jnp.sum(sps[h], axis=-1, keepdims=True) for h in heads]
                return (acc, *cs)

            zero_col = jnp.zeros((SB_BQ, 1), F32)
            out, n_run = _sb_keys_descending(
                qi, tile, (jnp.zeros((SB_BQ, LANES), F32),) + (zero_col,) * per_pair, z_bounds, per_pair)
            o_ref[rows, :] = out[0]
            for hh in range(per_pair):
                st_ref[hh, rows, :] = out[1 + hh]
            nrun_ref[pl.program_id(0), qi] = n_run
            return carry

        lax.fori_loop(0, nq, q_loop, 0)

    return pl.pallas_call(
        body, name="sb_fwd",
        grid=(n_pairs,),
        in_specs=[pl.BlockSpec((t_len, LANES), lambda p: (0, p)),
                  pl.BlockSpec((t_len, LANES), lambda p: (0, n_pairs + p)),
                  pl.BlockSpec((t_len, LANES), lambda p: (0, 2 * n_pairs + p))],
        out_specs=[pl.BlockSpec((t_len, LANES), lambda p: (0, p)),
                   pl.BlockSpec((per_pair, t_len, 1), lambda p: (p, 0, 0)),
                   pl.BlockSpec(memory_space=pltpu.SMEM)],
        out_shape=[jax.ShapeDtypeStruct((t_len, 512), F32),
                   jax.ShapeDtypeStruct((n_pairs * per_pair, t_len, 1), F32),
                   jax.ShapeDtypeStruct((n_pairs, nq), jnp.int32)],
        compiler_params=_params("arbitrary"),
    )(proj, proj, proj)


def _sb_bwd_call(proj, sp_total, n_run_all, d_o, dproj, t_len):
    nq = t_len // SB_BQ
    scale = float(SB_HEAD_DIM) ** -0.5
    n_pairs = 512 // LANES
    per_pair = LANES // SB_HEAD_DIM

    def body(q_ref, k_ref, v_ref, st_ref, nrun_ref, do_ref, dproj_in_ref, d_ref):
        lane = lax.broadcasted_iota(jnp.int32, (1, LANES), 1)
        row_i, col_i, sq_r, sq_c = _sb_iotas()
        lt = (sq_r < sq_c).astype(MXU_DTYPE)
        le = (sq_r <= sq_c).astype(MXU_DTYPE)
        hms = [((lane // SB_HEAD_DIM) == hh).astype(F32) for hh in range(per_pair)]
        d_ref[1] = jnp.zeros((t_len, LANES), F32)
        d_ref[2] = jnp.zeros((t_len, LANES), F32)

        def q_loop(qi, carry):
            r0 = pl.multiple_of(qi * SB_BQ, SB_BQ)
            rows = pl.ds(r0, SB_BQ)
            q_all, do_all = q_ref[rows, :], do_ref[rows, :]
            qms = [(q_all * (hm * scale)).astype(MXU_DTYPE) for hm in hms]
            doms = [(do_all * hm).astype(MXU_DTYPE) for hm in hms]
            totals = [st_ref[hh, rows, :] for hh in range(per_pair)]

            def tile(kj, masked, kc):
                dq, cls, gls = kc[0], list(kc[1:1 + per_pair]), list(kc[1 + per_pair:])
                heads = range(per_pair)
                s0 = pl.multiple_of(kj * SB_BLOCK, SB_BLOCK)
                cols = pl.ds(s0, SB_BLOCK)
                k_all, v_all = k_ref[cols, :], v_ref[cols, :]
                kb = k_all.astype(MXU_DTYPE)
                vms = [(v_all * hms[h]).astype(MXU_DTYPE) for h in heads]
                kms = [(k_all * (hms[h] * scale)).astype(MXU_DTYPE) for h in heads]
                zs = [lax.dot_general(qms[h], kb, _NT, preferred_element_type=F32) for h in heads]
                das = [lax.dot_general(doms[h], vms[h], _NT, preferred_element_type=F32) for h in heads]
                sp_alls = [_softplus(z) for z in zs]
                sps = sp_alls
                if masked:
                    mask = (col_i + s0) < (row_i + r0)
                    sps = [jnp.where(mask, sp, 0.0) for sp in sp_alls]
                lefts = [_running_sum_mm(sp, lt) for sp in sps]
                ws = [jnp.exp(zs[h] - (totals[h] - cls[h] - lefts[h])) for h in heads]
                if masked:
                    ws = [jnp.where(mask, w, 0.0) for w in ws]
                gs = [das[h] * ws[h] for h in heads]
                g_sums = [_running_sum_mm(g, le) for g in gs]
                dzs = [gs[h] - jnp.exp(zs[h] - sp_alls[h]) * (gls[h] + g_sums[h]) for h in heads]
                if masked:
                    dzs = [jnp.where(mask, dz, 0.0) for dz in dzs]
                dzs = [dz.astype(MXU_DTYPE) for dz in dzs]
                dk_t = jnp.zeros((SB_BLOCK, LANES), F32)
                dv_t = jnp.zeros((SB_BLOCK, LANES), F32)
                for h in heads:
                    dq = dq + jnp.dot(dzs[h], kms[h], preferred_element_type=F32)
                    dk_t = dk_t + lax.dot_general(dzs[h], qms[h], _TN, preferred_element_type=F32)
                    dv_t = dv_t + lax.dot_general(ws[h].astype(MXU_DTYPE), doms[h], _TN, preferred_element_type=F32)
                d_ref[1, cols, :] += dk_t
                d_ref[2, cols, :] += dv_t
                cls = [cls[h] + jnp.sum(sps[h], axis=-1, keepdims=True) for h in heads]
                gls = [gls[h] + jnp.sum(gs[h], axis=-1, keepdims=True) for h in heads]
                return (dq, *cls, *gls)

            zero_col = jnp.zeros((SB_BQ, 1), F32)
            out = _sb_keys_ascending(qi, nrun_ref[pl.program_id(0), qi], tile,
                                     (jnp.zeros((SB_BQ, LANES), F32),) + (zero_col,) * (2 * per_pair))
            d_ref[0, rows, :] = out[0]
            return carry

        lax.fori_loop(0, nq, q_loop, 0)

    col = lambda off: pl.BlockSpec((t_len, LANES), lambda p: (0, off + p))
    return pl.pallas_call(
        body, name="sb_bwd",
        grid=(n_pairs,),
        in_specs=[col(0), col(n_pairs), col(2 * n_pairs),
                  pl.BlockSpec((per_pair, t_len, 1), lambda p: (p, 0, 0)),
                  pl.BlockSpec(memory_space=pltpu.SMEM), col(0), _HBM],
        out_specs=pl.BlockSpec((3, t_len, LANES), lambda p: (DPROJ_SB_SLOT // 3, 0, p)),
        out_shape=jax.ShapeDtypeStruct(dproj.shape, dproj.dtype),
        input_output_aliases={6: 0},
        compiler_params=_params("arbitrary"),
    )(proj, proj, proj, sp_total, n_run_all, d_o, dproj)


def _conv_taps(xin, rows, t_len):
    taps = []
    for i in range(CONV_WIDTH):
        shift = CONV_WIDTH - 1 - i
        if shift == 0:
            taps.append(xin)
        else:
            taps.append(jnp.where(rows >= shift, pltpu.roll(xin, shift, axis=0), 0.0))
    return taps


def _gdn_prep_body_common(x_ref, w_ref, t_len):
    j = pl.program_id(0)
    xin = x_ref[...]
    rows = lax.broadcasted_iota(jnp.int32, (t_len, LANES), 0)
    taps = _conv_taps(xin, rows, t_len)
    pre = taps[0] * w_ref[0:1, :]
    for i in range(1, CONV_WIDTH):
        pre = pre + taps[i] * w_ref[i:i + 1, :]
    sg = _sigmoid(pre)
    act = pre * sg
    is_qk = j < 2 * GDN_HEADS
    nrm = jnp.where(is_qk, lax.rsqrt(jnp.sum(act * act, axis=-1, keepdims=True) + EPS), 1.0)
    sc = jnp.where(j < GDN_HEADS, float(GDN_HEAD_DIM) ** -0.5, 1.0)
    return j, rows, taps, pre, sg, act, is_qk, nrm, sc


def _gdn_prep_call(proj, conv_w, t_len):
    first = 2048 // LANES

    def body(x_ref, w_ref, out_ref):
        _, _, _, _, _, act, _, nrm, sc = _gdn_prep_body_common(x_ref, w_ref, t_len)
        out_ref[...] = act * nrm * sc

    return pl.pallas_call(
        body, name="gdn_prep",
        grid=(3 * GDN_HEADS,),
        in_specs=[pl.BlockSpec((t_len, LANES), lambda j: (0, first + j)),
                  pl.BlockSpec((CONV_WIDTH, LANES), lambda j: (0, j))],
        out_specs=pl.BlockSpec((t_len, LANES), lambda j: (0, j)),
        out_shape=jax.ShapeDtypeStruct((t_len, 3 * 512), F32),
        compiler_params=_params("arbitrary"),
    )(proj, conv_w)


def _gdn_prep_bwd_call(proj, conv_w, d_act3, dproj, t_len):
    first = 2048 // LANES

    def body(x_ref, w_ref, d_ref, dproj_in_ref, dx_ref, dw_ref):
        _, rows, taps, pre, sg, act, is_qk, nrm, sc = _gdn_prep_body_common(x_ref, w_ref, t_len)
        d_out = d_ref[0]
        dn = d_out * sc
        d_norm = nrm * dn - act * (nrm * nrm * nrm) * jnp.sum(dn * act, axis=-1, keepdims=True)
        d_act = jnp.where(is_qk, d_norm, d_out)
        d_pre = d_act * sg * (1.0 + pre * (1.0 - sg))
        dx = d_pre * w_ref[CONV_WIDTH - 1:CONV_WIDTH, :]
        dw_ref[CONV_WIDTH - 1:CONV_WIDTH, :] = jnp.sum(d_pre * taps[CONV_WIDTH - 1], axis=0, keepdims=True)
        for i in range(CONV_WIDTH - 1):
            shift = CONV_WIDTH - 1 - i
            up = jnp.where(rows < t_len - shift, pltpu.roll(d_pre, t_len - shift, axis=0), 0.0)
            dx = dx + up * w_ref[i:i + 1, :]
            dw_ref[i:i + 1, :] = jnp.sum(d_pre * taps[i], axis=0, keepdims=True)
        dx_ref[0] = dx

    return pl.pallas_call(
        body, name="gdn_prep_bwd",
        grid=(3 * GDN_HEADS,),
        in_specs=[pl.BlockSpec((t_len, LANES), lambda j: (0, first + j)),
                  pl.BlockSpec((CONV_WIDTH, LANES), lambda j: (0, j)),
                  pl.BlockSpec((1, t_len, LANES), lambda j: (j // GDN_HEADS, 0, j % GDN_HEADS)), _HBM],
        out_specs=[pl.BlockSpec((1, t_len, LANES), lambda j: (DPROJ_GDN_SLOT + j // GDN_HEADS, 0, j % GDN_HEADS)),
                   pl.BlockSpec((CONV_WIDTH, LANES), lambda j: (0, j))],
        out_shape=[jax.ShapeDtypeStruct(dproj.shape, dproj.dtype),
                   jax.ShapeDtypeStruct((CONV_WIDTH, 3 * 512), F32)],
        input_output_aliases={3: 0},
        compiler_params=_params("arbitrary"),
    )(proj, conv_w, d_act3, dproj)


def _chunk_cumsum_matrix():
    r = lax.broadcasted_iota(jnp.int32, (LANES, LANES), 0)
    c = lax.broadcasted_iota(jnp.int32, (LANES, LANES), 1)
    return ((r <= c) & ((r // CHUNK) == (c // CHUNK))).astype(F32)


def _gdn_gates_call(ps, pst, alog_l, dtb_l, alog_c, dtb_c, t_len):
    def body(ps_ref, pst_ref, al_ref, dl_ref, ac_ref, dc_ref, beta_ref, gcol_ref, grow_ref):
        upper = _chunk_cumsum_matrix()
        lower = upper.T
        psv = ps_ref[...]
        beta_ref[...] = _sigmoid(psv)
        g_l = -jnp.exp(al_ref[...]) * _softplus(psv + dl_ref[...])
        g_r = -jnp.exp(ac_ref[...]) * _softplus(pst_ref[...] + dc_ref[...])
        for w in range(t_len // LANES):
            sl = slice(w * LANES, (w + 1) * LANES)
            gcol_ref[sl, :] = _mx(lower, g_l[sl, :])
            grow_ref[:, sl] = _mx(g_r[:, sl], upper)

    vm = pl.BlockSpec(memory_space=pltpu.VMEM)
    return pl.pallas_call(
        body, name="gdn_gates",
        in_specs=[vm] * 6, out_specs=[vm] * 3,
        out_shape=[jax.ShapeDtypeStruct((t_len, LANES), F32),
                   jax.ShapeDtypeStruct((t_len, LANES), F32),
                   jax.ShapeDtypeStruct((8, t_len), F32)],
        compiler_params=pltpu.CompilerParams(vmem_limit_bytes=VMEM_LIMIT_BYTES),
    )(ps, pst, alog_l, dtb_l, alog_c, dtb_c)


def _gdn_gates_bwd_call(ps, alog_l, dtb_l, d_l, t_len):
    def body(ps_ref, al_ref, dl_ref, d_ref, dps_ref, gal_ref, gdt_ref):
        lane = lax.broadcasted_iota(jnp.int32, (1, LANES), 1)
        psv = ps_ref[...]
        dv = d_ref[...]
        beta = _sigmoid(psv)
        ea = jnp.exp(al_ref[...])
        arg = psv + dl_ref[...]
        g = -ea * _softplus(arg)
        d_a = dv * (-ea) * _sigmoid(arg)
        is_a = (lane >= GDN_HEADS) & (lane < 2 * GDN_HEADS)
        dps_ref[...] = jnp.where(lane < GDN_HEADS, dv * beta * (1.0 - beta), jnp.where(is_a, d_a, 0.0))
        gdt_ref[...] = jnp.where(is_a, jnp.sum(d_a, axis=0, keepdims=True), 0.0)
        gal_ref[...] = jnp.where(is_a, jnp.sum(dv * g, axis=0, keepdims=True), 0.0)

    vm = pl.BlockSpec(memory_space=pltpu.VMEM)
    return pl.pallas_call(
        body, name="gdn_gates_bwd",
        in_specs=[vm] * 4, out_specs=[vm] * 3,
        out_shape=[jax.ShapeDtypeStruct((t_len, LANES), F32),
                   jax.ShapeDtypeStruct((1, LANES), F32),
                   jax.ShapeDtypeStruct((1, LANES), F32)],
        compiler_params=pltpu.CompilerParams(vmem_limit_bytes=VMEM_LIMIT_BYTES),
    )(ps, alog_l, dtb_l, d_l)


def _bm(a, b):
    return _m3_general(a, b, _BNN)


def _bm_nt(a, b):
    return _m3_general(a, b, _BNT)


def _bm_tn(a, b):
    return _m3_general(a, b, _BTN)


def _heads_of(ref, rows):
    return jnp.stack([ref[rows, h * GDN_HEAD_DIM:(h + 1) * GDN_HEAD_DIM] for h in range(GDN_HEADS)])


def _chunk_terms(q_ref, k_ref, v_ref, b_ref, gc_ref, gr_ref, c, incl, strict):
    r0 = pl.multiple_of(c * CHUNK, CHUNK)
    rows = pl.ds(r0, CHUNK)
    q, k, v = _heads_of(q_ref, rows), _heads_of(k_ref, rows), _heads_of(v_ref, rows)
    lane_ids = lax.broadcasted_iota(jnp.int32, (1, LANES), 1)
    pick = lambda slab, first: jnp.stack([jnp.sum(jnp.where(lane_ids == first + h, slab, 0.0), axis=-1, keepdims=True)
                                          for h in range(GDN_HEADS)])
    b = pick(b_ref[rows, :], 0)
    gc = pick(gc_ref[rows, :], GDN_HEADS)
    gr = gr_ref[:, c]
    dm = jnp.where(incl, jnp.exp(jnp.where(incl, gc - gr, 0.0)), 0.0)
    kb = k * b
    vb = v * b
    e = jnp.exp(gc)
    kk_qk = _bm_nt(jnp.concatenate([kb, q], axis=1), k)
    a = jnp.where(strict, kk_qk[:, :CHUNK] * dm, 0.0)
    p = jnp.where(incl, kk_qk[:, CHUNK:] * dm, 0.0)
    gl = gc[:, CHUNK - 1:CHUNK, :]
    eg = jnp.exp(gl - gc)
    return rows, q, k, v, b, gc, dm, kb, vb, e, a, p, gl, eg


def _unit_lower_inverse(a, eye):
    x = -a
    tm = eye + x
    xp = _bm(x, x)
    for _ in range(4):
        both = _bm(jnp.concatenate([xp, tm], axis=1), xp)
        tm = tm + both[:, CHUNK:]
        xp = both[:, :CHUNK]
    return tm + _bm(tm, xp)


def _gdn_specs(t_len, n_chunks, reverse):
    cps = GDN_CHUNKS_PER_STEP
    steps = n_chunks // cps
    at = (lambda g: steps - 1 - g) if reverse else (lambda g: g)
    rows_blk = lambda width, part=0: pl.BlockSpec((cps * CHUNK, width), lambda g: (at(g), part))
    gate_r = pl.BlockSpec((GDN_HEADS, cps, 1, CHUNK), lambda g: (0, at(g), 0, 0))
    per_chunk = lambda r, c: pl.BlockSpec((GDN_HEADS, cps, r, c), lambda g: (0, at(g), 0, 0))
    return cps, steps, rows_blk, gate_r, per_chunk


def _gdn_fwd_call(gact, beta_c, gam_c, gam_r, t_len):
    n_chunks = t_len // CHUNK
    dk = GDN_HEAD_DIM
    width = GDN_HEADS * dk
    cps, steps, rows_blk, gate_r, per_chunk = _gdn_specs(t_len, n_chunks, False)

    def body(q_ref, k_ref, v_ref, b_ref, gc_ref, gr_ref, o_ref, s_ref, t_ref, state_ref):
        row = lax.broadcasted_iota(jnp.int32, (CHUNK, CHUNK), 0)
        col = lax.broadcasted_iota(jnp.int32, (CHUNK, CHUNK), 1)
        incl, strict = row >= col, row > col
        eye = (row == col).astype(F32)

        @pl.when(pl.program_id(0) == 0)
        def _():
            state_ref[...] = jnp.zeros_like(state_ref)

        def chunk(c, carry):
            rows, q, k, v, b, gc, dm, kb, vb, e, a, p, gl, eg = _chunk_terms(
                q_ref, k_ref, v_ref, b_ref, gc_ref, gr_ref, c, incl, strict)
            s = state_ref[...]
            tm = _unit_lower_inverse(a, eye)
            uw = _bm(tm, jnp.concatenate([vb, kb * e], axis=2))
            u, w = uw[:, :, :dk], uw[:, :, dk:]
            ws_qs = _bm(jnp.concatenate([w, q * e], axis=1), s)
            vn = u - ws_qs[:, :CHUNK]
            o = ws_qs[:, CHUNK:] + _bm(p, vn)
            for h in range(GDN_HEADS):
                o_ref[rows, h * dk:(h + 1) * dk] = o[h]
            s_ref[:, c] = s
            t_ref[:, c] = tm
            state_ref[...] = s * jnp.exp(gl) + _bm_tn(k * eg, vn)
            return carry

        lax.fori_loop(0, cps, chunk, 0)

    return pl.pallas_call(
        body, name="gdn_fwd",
        grid=(steps,),
        in_specs=[rows_blk(width, 0), rows_blk(width, 1), rows_blk(width, 2), rows_blk(LANES), rows_blk(LANES), gate_r],
        out_specs=[rows_blk(width), per_chunk(dk, dk), per_chunk(CHUNK, CHUNK)],
        out_shape=[jax.ShapeDtypeStruct((t_len, width), F32),
                   jax.ShapeDtypeStruct((GDN_HEADS, n_chunks, dk, dk), F32),
                   jax.ShapeDtypeStruct((GDN_HEADS, n_chunks, CHUNK, CHUNK), F32)],
        scratch_shapes=[pltpu.VMEM((GDN_HEADS, dk, dk), F32)],
        compiler_params=_params("arbitrary"),
    )(gact, gact, gact, beta_c, gam_c, gam_r)


def _gdn_bwd_call(gact, beta_c, gam_c, gam_r, s_all, t_all, d_o, t_len):
    n_chunks = t_len // CHUNK
    dk = GDN_HEAD_DIM
    width = GDN_HEADS * dk
    cps, steps, rows_blk, gate_r, per_chunk = _gdn_specs(t_len, n_chunks, True)

    def body(q_ref, k_ref, v_ref, b_ref, gc_ref, gr_ref, s_ref, t_ref, do_ref, d_ref, dgate_ref, dstate_ref):
        row = lax.broadcasted_iota(jnp.int32, (CHUNK, CHUNK), 0)
        col = lax.broadcasted_iota(jnp.int32, (CHUNK, CHUNK), 1)
        incl, strict = row >= col, row > col
        upper = jnp.broadcast_to((row <= col).astype(F32), (GDN_HEADS, CHUNK, CHUNK))
        ones = jnp.ones((GDN_HEADS, CHUNK, LANES), F32)
        last_row = lax.broadcasted_iota(jnp.int32, (CHUNK, 1), 0) == CHUNK - 1
        lane_ids = lax.broadcasted_iota(jnp.int32, (1, LANES), 1)
        rsum = lambda m: jnp.sum(m, axis=-1, keepdims=True)
        total = lambda m: jnp.sum(rsum(m), axis=1, keepdims=True)

        @pl.when(pl.program_id(0) == 0)
        def _():
            dstate_ref[...] = jnp.zeros_like(dstate_ref)

        def chunk(step, carry):
            c = cps - 1 - step
            rows, q, k, v, b, gc, dm, kb, vb, e, a, p, gl, eg = _chunk_terms(
                q_ref, k_ref, v_ref, b_ref, gc_ref, gr_ref, c, incl, strict)
            ds = dstate_ref[...]
            s = s_ref[:, c]
            tm = t_ref[:, c]
            d_out = _heads_of(do_ref, rows)
            el = jnp.exp(gl)
            kbe = kb * e
            uw = _bm(tm, jnp.concatenate([vb, kbe], axis=2))
            u, w = uw[:, :, :dk], uw[:, :, dk:]
            vn = u - _bm(w, s)
            qe = q * e
            kd = k * eg

            d_vn = _bm_tn(p, d_out) + _bm(kd, ds)
            on_s = _bm_nt(jnp.concatenate([d_out, d_vn], axis=1), s)
            d_qe, d_w = on_s[:, :CHUNK], -on_s[:, CHUNK:]
            d_p = jnp.where(incl, _bm_nt(d_out, vn), 0.0)
            dstate_ref[...] = el * ds + _bm_tn(jnp.concatenate([qe, -w], axis=1),
                                               jnp.concatenate([d_out, d_vn], axis=1))
            d_kd = _bm_nt(vn, ds)
            d_both = _bm_tn(tm, jnp.concatenate([d_vn, d_w], axis=2))
            d_vb, d_kbe = d_both[:, :, :dk], d_both[:, :, dk:]
            d_a = -jnp.where(strict, _bm_nt(d_both, uw), 0.0)
            m = d_a * dm
            n = d_p * dm
            on_k = _bm(jnp.concatenate([m, n], axis=1), k)
            d_kb = on_k[:, :CHUNK] + d_kbe * e
            d_q = on_k[:, CHUNK:] + d_qe * e
            d_k = (_bm_tn(jnp.concatenate([m, n], axis=1), jnp.concatenate([kb, q], axis=1))
                   + d_kd * eg + b * d_kb)
            d_v = b * d_vb
            r = d_a * a + d_p * p
            kd_term = rsum(d_kd * kd)
            d_gl = total(ds * s) * el + jnp.sum(kd_term, axis=1, keepdims=True)
            d_gam = (rsum(r) - _bm_tn(r, ones)[:, :, 0:1] + rsum(d_qe * qe) + rsum(d_kbe * kbe) - kd_term
                     + jnp.where(last_row, d_gl, 0.0))
            d_beta = rsum(d_kb * k) + rsum(d_vb * v)
            d_g = _bm(upper, d_gam * ones)[:, :, 0:1]
            gates = jnp.zeros((CHUNK, LANES), F32)
            for h in range(GDN_HEADS):
                lanes = slice(h * dk, (h + 1) * dk)
                d_ref[0, rows, lanes] = d_q[h]
                d_ref[1, rows, lanes] = d_k[h]
                d_ref[2, rows, lanes] = d_v[h]
                gates = gates + (jnp.where(lane_ids == h, d_beta[h], 0.0)
                                 + jnp.where(lane_ids == GDN_HEADS + h, d_g[h], 0.0))
            dgate_ref[rows, :] = gates
            return carry

        lax.fori_loop(0, cps, chunk, 0)

    d_spec = pl.BlockSpec((3, cps * CHUNK, width), lambda g: (0, steps - 1 - g, 0))
    return pl.pallas_call(
        body, name="gdn_bwd",
        grid=(steps,),
        in_specs=[rows_blk(width, 0), rows_blk(width, 1), rows_blk(width, 2), rows_blk(LANES), rows_blk(LANES), gate_r,
                  per_chunk(dk, dk), per_chunk(CHUNK, CHUNK), rows_blk(width)],
        out_specs=[d_spec, rows_blk(LANES)],
        out_shape=[jax.ShapeDtypeStruct((3, t_len, width), F32),
                   jax.ShapeDtypeStruct((t_len, LANES), F32)],
        scratch_shapes=[pltpu.VMEM((GDN_HEADS, dk, dk), F32)],
        compiler_params=_params("arbitrary"),
    )(gact, gact, gact, beta_c, gam_c, gam_r, s_all, t_all, d_o)


def _group_matrix(width, group):
    r = lax.broadcasted_iota(jnp.int32, (width, width), 0)
    c = lax.broadcasted_iota(jnp.int32, (width, width), 1)
    return ((r // group) == (c // group)).astype(F32)


def _post_call(o_sb, o_gd, proj, x, target, w_out, sbw, gdw, fw, tm=256):
    t_len, d = x.shape
    half = 512
    zsb_blk = 1536 // half
    zgd_blk = 3584 // half

    def body(osb_ref, ogd_ref, zsb_ref, zgd_ref, x_ref, tg_ref, wo_ref, sbw_ref, gdw_ref, fw_ref,
             dx2_ref, dosb_ref, dogd_ref, dz_ref, loss_ref, gfw_ref, gsb_ref, ggd_ref, gwo_ref):
        step = pl.program_id(0)

        @pl.when(step == 0)
        def _():
            loss_ref[...] = jnp.zeros_like(loss_ref)
            gfw_ref[...] = jnp.zeros_like(gfw_ref)
            gsb_ref[...] = jnp.zeros_like(gsb_ref)
            ggd_ref[...] = jnp.zeros_like(ggd_ref)
            gwo_ref[...] = jnp.zeros_like(gwo_ref)

        def head_forward(o, z, w, gmat, inv):
            r = lax.rsqrt(_running_sum_mm(o * o, gmat) * inv + EPS)
            nrm = o * r * w
            sg = _sigmoid(z)
            return r, nrm, sg, nrm * (z * sg)

        def head_backward(d_m, o, z, w, gmat, inv, r, nrm, sg):
            d_n = d_m * (z * sg)
            d_z = d_m * nrm * (sg * (1.0 + z * (1.0 - sg)))
            dnw = d_n * w
            d_o = r * dnw - o * (r * r * r) * (_running_sum_mm(dnw * o, gmat) * inv)
            return d_o, d_z, jnp.sum(d_n * o * r, axis=0, keepdims=True)

        g_sb = _group_matrix(half, SB_HEAD_DIM).astype(MXU_DTYPE)
        g_gd = _group_matrix(half, GDN_HEAD_DIM).astype(MXU_DTYPE)
        osb, ogd, zsb, zgd = osb_ref[...], ogd_ref[...], zsb_ref[...], zgd_ref[...]
        sbw_v, gdw_v = sbw_ref[...], gdw_ref[...]
        r_sb, n_sb, sg_sb, m_sb = head_forward(osb, zsb, sbw_v, g_sb, 1.0 / SB_HEAD_DIM)
        r_gd, n_gd, sg_gd, m_gd = head_forward(ogd, zgd, gdw_v, g_gd, 1.0 / GDN_HEAD_DIM)
        mixed = jnp.concatenate([m_sb, m_gd], axis=1).astype(MXU_DTYPE)
        wo = wo_ref[...]
        x2 = x_ref[...] + jnp.dot(mixed, wo, preferred_element_type=F32)
        r2 = lax.rsqrt(jnp.mean(x2 * x2, axis=-1, keepdims=True) + EPS)
        fw_v = fw_ref[...]
        err = x2 * r2 * fw_v - tg_ref[...]
        loss_ref[...] += 0.5 * jnp.sum(jnp.sum(err * err, axis=-1, keepdims=True) * (1.0 / d))
        dy = err * (1.0 / d)
        gg = dy * fw_v
        dx2 = r2 * gg - x2 * ((r2 * r2 * r2) * jnp.mean(gg * x2, axis=-1, keepdims=True))
        gfw_ref[...] += jnp.sum(dy * x2 * r2, axis=0, keepdims=True)
        dx2_ref[...] = dx2
        dx2b = dx2.astype(MXU_DTYPE)
        d_mixed = lax.dot_general(dx2b, wo, _NT, preferred_element_type=F32)
        gwo_ref[...] += lax.dot_general(mixed, dx2b, _TN, preferred_element_type=F32)
        d_osb, d_zsb, gsb = head_backward(d_mixed[:, :half], osb, zsb, sbw_v, g_sb, 1.0 / SB_HEAD_DIM, r_sb, n_sb, sg_sb)
        d_ogd, d_zgd, ggd = head_backward(d_mixed[:, half:], ogd, zgd, gdw_v, g_gd, 1.0 / GDN_HEAD_DIM, r_gd, n_gd, sg_gd)
        dosb_ref[...] = d_osb
        dogd_ref[...] = d_ogd
        dz_ref[0] = d_zsb
        dz_ref[1] = d_zgd
        gsb_ref[...] += gsb
        ggd_ref[...] += ggd

    row_blk = lambda w: pl.BlockSpec((tm, w), lambda i: (i, 0))
    fixed = lambda r, w: pl.BlockSpec((r, w), lambda i: (0, 0))
    return pl.pallas_call(
        body, name="post",
        grid=(t_len // tm,),
        in_specs=[row_blk(half), row_blk(half),
                  pl.BlockSpec((tm, half), lambda i: (i, zsb_blk)),
                  pl.BlockSpec((tm, half), lambda i: (i, zgd_blk)),
                  row_blk(d), row_blk(d), fixed(d, d), fixed(1, half), fixed(1, half), fixed(1, d)],
        out_specs=[row_blk(d), row_blk(half), row_blk(half),
                   pl.BlockSpec((2, tm, half), lambda i: (DPROJ_GATE_SLOT // 2, i, 0)),
                   fixed(1, LANES), fixed(1, d), fixed(1, half), fixed(1, half), fixed(d, d)],
        out_shape=[jax.ShapeDtypeStruct((t_len, d), F32)] + [jax.ShapeDtypeStruct((t_len, half), F32)] * 2
                  + [jax.ShapeDtypeStruct((len(DPROJ_PIECE_OF_SLOT), t_len, half), F32),
                     jax.ShapeDtypeStruct((1, LANES), F32), jax.ShapeDtypeStruct((1, d), F32),
                     jax.ShapeDtypeStruct((1, half), F32), jax.ShapeDtypeStruct((1, half), F32),
                     jax.ShapeDtypeStruct((d, d), F32)],
        compiler_params=_params("arbitrary"),
    )(o_sb, o_gd, proj, proj, x, target, w_out, sbw, gdw, fw)


def _piece_of_slot(s):
    return jnp.where(s < DPROJ_GDN_SLOT, s, jnp.where(s < DPROJ_GATE_SLOT, s + 1,
                                                     jnp.where(s == DPROJ_GATE_SLOT, 3, 7)))


def _gw_in_call(h_t, dproj8):
    d, t_len = h_t.shape
    n_piece, _, pw = dproj8.shape

    def body(ht_ref, dp_ref, gw_ref):
        gw_ref[...] = jnp.dot(ht_ref[...], dp_ref[0].astype(MXU_DTYPE), preferred_element_type=F32)

    return pl.pallas_call(
        body, name="gw_in",
        grid=(n_piece,),
        in_specs=[pl.BlockSpec((d, t_len), lambda s: (0, 0)),
                  pl.BlockSpec((1, t_len, pw), lambda s: (s, 0, 0))],
        out_specs=pl.BlockSpec((d, pw), lambda s: (0, _piece_of_slot(s))),
        out_shape=jax.ShapeDtypeStruct((d, n_piece * pw), F32),
        compiler_params=_params("arbitrary"),
    )(h_t, dproj8)


def _gw_small_call(h_t, dsmall, tm=512):
    d, t_len = h_t.shape
    ns = dsmall.shape[1]

    def body(ht_ref, dp_ref, gw_ref):
        @pl.when(pl.program_id(0) == 0)
        def _():
            gw_ref[...] = jnp.zeros_like(gw_ref)

        gw_ref[...] += jnp.dot(ht_ref[...], dp_ref[...].astype(MXU_DTYPE), preferred_element_type=F32)

    return pl.pallas_call(
        body, name="gw_small",
        grid=(t_len // tm,),
        in_specs=[pl.BlockSpec((d, tm), lambda t: (0, t)),
                  pl.BlockSpec((tm, ns), lambda t: (t, 0))],
        out_specs=pl.BlockSpec((d, ns), lambda t: (0, 0)),
        out_shape=jax.ShapeDtypeStruct((d, ns), F32),
        compiler_params=_params("arbitrary"),
    )(h_t, dsmall)


def _dx_call(dproj8, dsmall, w_main, w_small, x, r, dx2, norm_w, tm=256):
    t_len, d = x.shape
    n_piece, _, pw = dproj8.shape
    ns = dsmall.shape[1]

    def body(dp_ref, ds_ref, wm_ref, ws_ref, x_ref, r_ref, dx2_ref, nw_ref, gx_ref, gnw_ref):
        @pl.when(pl.program_id(0) == 0)
        def _():
            gnw_ref[...] = jnp.zeros_like(gnw_ref)

        dh = lax.dot_general(ds_ref[...].astype(MXU_DTYPE), ws_ref[...], _NT, preferred_element_type=F32)
        for s, p in enumerate(DPROJ_PIECE_OF_SLOT):
            dh = dh + lax.dot_general(dp_ref[s].astype(MXU_DTYPE), wm_ref[:, p * pw:(p + 1) * pw], _NT,
                                      preferred_element_type=F32)
        xv, rv = x_ref[...], r_ref[...]
        dn = dh * nw_ref[...]
        gx_ref[...] = dx2_ref[...] + rv * dn - xv * ((rv * rv * rv) * jnp.mean(dn * xv, axis=-1, keepdims=True))
        gnw_ref[...] += jnp.sum(dh * xv * rv, axis=0, keepdims=True)

    return pl.pallas_call(
        body, name="dx",
        grid=(t_len // tm,),
        in_specs=[pl.BlockSpec((n_piece, tm, pw), lambda i: (0, i, 0)),
                  pl.BlockSpec((tm, ns), lambda i: (i, 0)),
                  pl.BlockSpec((d, n_piece * pw), lambda i: (0, 0)),
                  pl.BlockSpec((d, ns), lambda i: (0, 0)),
                  pl.BlockSpec((tm, d), lambda i: (i, 0)),
                  pl.BlockSpec((tm, 1), lambda i: (i, 0)),
                  pl.BlockSpec((tm, d), lambda i: (i, 0)),
                  pl.BlockSpec((1, d), lambda i: (0, 0))],
        out_specs=[pl.BlockSpec((tm, d), lambda i: (i, 0)),
                   pl.BlockSpec((1, d), lambda i: (0, 0))],
        out_shape=[jax.ShapeDtypeStruct((t_len, d), F32), jax.ShapeDtypeStruct((1, d), F32)],
        compiler_params=_params("arbitrary"),
    )(dproj8, dsmall, w_main, w_small, x, r, dx2, norm_w)


def _exchange_call(name, srcs, per_peer):
    n = len(srcs)
    out_shapes = [jax.ShapeDtypeStruct(s.shape if pp else (N_DEV,) + s.shape, s.dtype) for s, pp in zip(srcs, per_peer)]

    def body(*refs):
        src_refs, out_refs = refs[:n], refs[n:2 * n]
        send_sems, recv_sems, local_sems = refs[2 * n:]
        x, y, c = lax.axis_index("x"), lax.axis_index("y"), lax.axis_index("c")
        me = 4 * x + 2 * y + c
        copies = []
        for a in range(n):
            mine = src_refs[a].at[me] if per_peer[a] else src_refs[a]
            local = pltpu.make_async_copy(mine, out_refs[a].at[me], local_sems.at[a])
            local.start()
            copies.append(local)
        remote = []
        for k in range(1, N_DEV):
            kx, ky, kc = (k >> 2) & 1, (k >> 1) & 1, k & 1
            px = 1 - x if kx else x
            py = 1 - y if ky else y
            pc = 1 - c if kc else c
            peer = 4 * px + 2 * py + pc
            for a in range(n):
                sem = a * (N_DEV - 1) + (k - 1)
                src = src_refs[a].at[peer] if per_peer[a] else src_refs[a]
                cp = pltpu.make_async_remote_copy(
                    src_ref=src, dst_ref=out_refs[a].at[me],
                    send_sem=send_sems.at[sem], recv_sem=recv_sems.at[sem],
                    device_id=(px, py, pc), device_id_type=pl.DeviceIdType.MESH)
                cp.start()
                remote.append(cp)
        for cp in remote:
            cp.wait_send()
        for cp in remote:
            cp.wait_recv()
        for cp in copies:
            cp.wait()

    hbm = pl.BlockSpec(memory_space=pl.ANY)
    return pl.pallas_call(
        body, name=name,
        in_specs=[hbm] * n, out_specs=[hbm] * n, out_shape=out_shapes,
        scratch_shapes=[pltpu.SemaphoreType.DMA((n * (N_DEV - 1),)),
                        pltpu.SemaphoreType.DMA((n * (N_DEV - 1),)),
                        pltpu.SemaphoreType.DMA((n,))],
    )(*srcs)


N_CHIPS = 4
_HBM = pl.BlockSpec(memory_space=pl.ANY)
_MESH = pl.DeviceIdType.MESH


def _gather_call(name, srcs):
    n = len(srcs)
    per = N_DEV - 1

    def body(*refs):
        src_refs, out_refs = refs[:n], refs[n:2 * n]
        send_sems, recv_sems, local_sems = refs[2 * n:]
        x, y, c = lax.axis_index("x"), lax.axis_index("y"), lax.axis_index("c")
        me, sibling = (x, y, c), (x, y, 1 - c)
        chips = [(1 - x, y), (x, 1 - y), (1 - x, 1 - y)]
        slot = lambda px, py, pc: 4 * px + 2 * py + pc

        def copy(a, k, block, to, from_src=False):
            rows = out_refs[a].at[slot(*block)]
            return pltpu.make_async_remote_copy(
                src_ref=src_refs[a] if from_src else rows, dst_ref=rows,
                send_sem=send_sems.at[a * per + k], recv_sem=recv_sems.at[a * per + k],
                device_id=to, device_id_type=_MESH)

        local = [pltpu.make_async_copy(src_refs[a], out_refs[a].at[slot(*me)], local_sems.at[a]) for a in range(n)]
        for cp in local:
            cp.start()
        first = []
        for a in range(n):
            first.append(copy(a, 0, me, sibling, True))
            first += [copy(a, 1 + j, me, (*chip, c), True) for j, chip in enumerate(chips)]
        for cp in first:
            cp.start()
        passed = []
        for j, chip in enumerate(chips):
            for a in range(n):
                copy(a, 1 + j, (*chip, c), me).wait_recv()
                fwd = copy(a, 4 + j, (*chip, c), sibling)
                fwd.start()
                passed.append(fwd)
        for a in range(n):
            copy(a, 0, sibling, me).wait_recv()
            for j, chip in enumerate(chips):
                copy(a, 4 + j, (*chip, 1 - c), me).wait_recv()
        for cp in first + passed:
            cp.wait_send()
        for cp in local:
            cp.wait()

    return pl.pallas_call(
        body, name=name,
        in_specs=[_HBM] * n, out_specs=[_HBM] * n,
        out_shape=[jax.ShapeDtypeStruct((N_DEV,) + s.shape, s.dtype) for s in srcs],
        scratch_shapes=[pltpu.SemaphoreType.DMA((n * per,)), pltpu.SemaphoreType.DMA((n * per,)),
                        pltpu.SemaphoreType.DMA((n,))],
    )(*srcs)


def _sibling_send_call(name, srcs):
    n = len(srcs)

    def body(*refs):
        src_refs, out_refs = refs[:n], refs[n:2 * n]
        send_sems, recv_sems = refs[2 * n:]
        x, y, c = lax.axis_index("x"), lax.axis_index("y"), lax.axis_index("c")
        copies = []
        for a in range(n):
            for ch in range(N_CHIPS):
                copies.append(pltpu.make_async_remote_copy(
                    src_ref=src_refs[a].at[2 * ch + (1 - c)], dst_ref=out_refs[a].at[ch],
                    send_sem=send_sems.at[a * N_CHIPS + ch], recv_sem=recv_sems.at[a * N_CHIPS + ch],
                    device_id=(x, y, 1 - c), device_id_type=_MESH))
        for cp in copies:
            cp.start()
        for cp in copies:
            cp.wait_send()
        for cp in copies:
            cp.wait_recv()

    return pl.pallas_call(
        body, name=name,
        in_specs=[_HBM] * n, out_specs=[_HBM] * n,
        out_shape=[jax.ShapeDtypeStruct((N_CHIPS,) + s.shape[1:], s.dtype) for s in srcs],
        scratch_shapes=[pltpu.SemaphoreType.DMA((n * N_CHIPS,)), pltpu.SemaphoreType.DMA((n * N_CHIPS,))],
    )(*srcs)


def _pair_sum_call(name, parts, from_sibling, tr):
    _, rows, cols = parts.shape

    def body(p_ref, s_ref, o_ref):
        o_ref[...] = (p_ref[...] + s_ref[...]).astype(o_ref.dtype)

    return pl.pallas_call(
        body, name=name,
        grid=(N_CHIPS, rows // tr),
        in_specs=[pl.BlockSpec((1, tr, cols), lambda ch, i: (2 * ch + lax.axis_index("c"), i, 0)),
                  pl.BlockSpec((1, tr, cols), lambda ch, i: (ch, i, 0))],
        out_specs=pl.BlockSpec((1, tr, cols), lambda ch, i: (ch, i, 0)),
        out_shape=jax.ShapeDtypeStruct((N_CHIPS, rows, cols), WIRE_DTYPE),
        compiler_params=_params("arbitrary", "arbitrary"),
    )(parts, from_sibling)


def _chip_exchange_call(name, srcs):
    n = len(srcs)
    per = N_CHIPS - 1

    def body(*refs):
        src_refs, out_refs = refs[:n], refs[n:2 * n]
        send_sems, recv_sems, local_sems = refs[2 * n:]
        x, y, c = lax.axis_index("x"), lax.axis_index("y"), lax.axis_index("c")
        mine = 2 * x + y
        chips = [(1 - x, y), (x, 1 - y), (1 - x, 1 - y)]
        local = [pltpu.make_async_copy(src_refs[a].at[mine], out_refs[a].at[mine], local_sems.at[a]) for a in range(n)]
        for cp in local:
            cp.start()
        remote = []
        for a in range(n):
            for j, (px, py) in enumerate(chips):
                remote.append(pltpu.make_async_remote_copy(
                    src_ref=src_refs[a].at[2 * px + py], dst_ref=out_refs[a].at[mine],
                    send_sem=send_sems.at[a * per + j], recv_sem=recv_sems.at[a * per + j],
                    device_id=(px, py, c), device_id_type=_MESH))
        for cp in remote:
            cp.start()
        for cp in remote:
            cp.wait_send()
        for cp in remote:
            cp.wait_recv()
        for cp in local:
            cp.wait()

    return pl.pallas_call(
        body, name=name,
        in_specs=[_HBM] * n, out_specs=[_HBM] * n,
        out_shape=[jax.ShapeDtypeStruct(s.shape, s.dtype) for s in srcs],
        scratch_shapes=[pltpu.SemaphoreType.DMA((n * per,)), pltpu.SemaphoreType.DMA((n * per,)),
                        pltpu.SemaphoreType.DMA((n,))],
    )(*srcs)


def _adam_call(name, parts, w, m, v, tr):
    rows, cols = w.shape
    n_slots = parts.shape[0]

    def body(p_ref, w_ref, m_ref, v_ref, g_ref, d_ref, nm_ref, nv_ref):
        g = p_ref[0].astype(F32)
        for s in range(1, n_slots):
            g = g + p_ref[s].astype(F32)
        m_new = ADAM_B1 * m_ref[...] + (1.0 - ADAM_B1) * g
        v_new = ADAM_B2 * v_ref[...] + (1.0 - ADAM_B2) * (g * g)
        m_hat = m_new / (1.0 - ADAM_B1 ** ADAM_STEP)
        v_hat = v_new / (1.0 - ADAM_B2 ** ADAM_STEP)
        g_ref[...] = g
        d_ref[...] = -ADAM_LR * (m_hat / (jnp.sqrt(v_hat) + ADAM_EPS) + ADAM_WD * w_ref[...])
        nm_ref[...] = m_new
        nv_ref[...] = v_new

    blk = pl.BlockSpec((tr, cols), lambda i: (i, 0))
    return pl.pallas_call(
        body, name=name,
        grid=(rows // tr,),
        in_specs=[pl.BlockSpec((n_slots, tr, cols), lambda i: (0, i, 0)), blk, blk, blk],
        out_specs=[blk] * 4,
        out_shape=[jax.ShapeDtypeStruct((rows, cols), F32)] * 4,
        compiler_params=_params("arbitrary"),
    )(parts, w, m, v)


_SMALL_ROWS = ("norm1_w", "final_norm_w", "sb_norm_w", "gdn_norm_w", "gdn_A_log", "gdn_dt_bias", "loss")


def _pack_small(vals, width):
    rows = [jnp.pad(a.reshape(1, -1).astype(F32), ((0, 0), (0, width - a.size))) for a in vals]
    rows += [jnp.zeros((1, width), F32)] * (8 - len(rows))
    return jnp.concatenate(rows, axis=0)


def _device_step(x2d, tgt, w_full, w_out_f32, conv_full, norm1_w, sb_norm_w, gdn_A_log, gdn_dt_bias, gdn_norm_w,
                 final_norm_w):
    t_len, d = x2d.shape
    n_chunks = t_len // CHUNK
    n_main = 8 * 512
    n_small = w_full.shape[1] - n_main
    w_main = w_full[:, :n_main].astype(MXU_DTYPE)
    w_small = jnp.pad(w_full[:, n_main:], ((0, 0), (0, LANES - n_small))).astype(MXU_DTYPE)
    w_small_t = w_full[:, n_main:].T.astype(MXU_DTYPE)
    w_out_full = w_out_f32.astype(MXU_DTYPE)

    pad_lanes = lambda a, lo: jnp.pad(a.reshape(1, -1), ((0, 0), (lo, LANES - lo - a.size)))
    alog_l, dtb_l = pad_lanes(gdn_A_log, GDN_HEADS), pad_lanes(gdn_dt_bias, GDN_HEADS)
    alog_c, dtb_c = alog_l[:, :8].T, dtb_l[:, :8].T
    sbw = jnp.tile(sb_norm_w, (1, 512 // SB_HEAD_DIM))
    gdw = jnp.tile(gdn_norm_w, (1, 512 // GDN_HEAD_DIM))
    fw = final_norm_w.reshape(1, d)

    proj, ps, pst, h_t, r1 = _inproj_call(x2d, norm1_w, w_main, w_small, w_small_t)
    o_sb, sp_total, sb_blocks_run = _sb_fwd_call(proj, t_len)
    gact = _gdn_prep_call(proj, conv_full, t_len)
    beta_l, gcol_l, grow = _gdn_gates_call(ps, pst, alog_l, dtb_l, alog_c, dtb_c, t_len)
    gam_r = grow[GDN_HEADS:2 * GDN_HEADS].reshape(GDN_HEADS, n_chunks, 1, CHUNK)
    o_gd, s_all, t_all = _gdn_fwd_call(gact, beta_l, gcol_l, gam_r, t_len)

    (dx2, d_osb, d_ogd, dproj8, loss_p, g_fw, g_sbw, g_gdw, g_wout) = _post_call(
        o_sb, o_gd, proj, x2d, tgt, w_out_full, sbw, gdw, fw)

    dproj8 = _sb_bwd_call(proj, sp_total, sb_blocks_run, d_osb, dproj8, t_len)
    d_gact3, d_gates = _gdn_bwd_call(gact, beta_l, gcol_l, gam_r, s_all, t_all, d_ogd, t_len)
    dproj8, g_conv = _gdn_prep_bwd_call(proj, conv_full, d_gact3, dproj8, t_len)
    dsmall, g_alog, g_dtb = _gdn_gates_bwd_call(ps, alog_l, dtb_l, d_gates, t_len)

    g_w_main = _gw_in_call(h_t, dproj8)
    g_w_small = _gw_small_call(h_t, dsmall)
    grad_x, g_n1 = _dx_call(dproj8, dsmall, w_main, w_small, x2d, r1, dx2, norm1_w)
    g_w_in_full = jnp.concatenate([g_w_main, g_w_small[:, :n_small]], axis=1)
    return (loss_p, grad_x, g_n1, g_w_in_full, g_sbw, g_conv, g_alog, g_dtb, g_gdw, g_wout, g_fw)


def kernel(x, norm1_w, w_in, sb_norm_w, gdn_conv_w, gdn_A_log, gdn_dt_bias, gdn_norm_w, w_out, final_norm_w, loss_target, m_norm1_w, m_w_in, m_sb_norm_w, m_gdn_conv_w, m_gdn_A_log, m_gdn_dt_bias, m_gdn_norm_w, m_w_out, m_final_norm_w, v_norm1_w, v_w_in, v_sb_norm_w, v_gdn_conv_w, v_gdn_A_log, v_gdn_dt_bias, v_gdn_norm_w, v_w_out, v_final_norm_w):
    d = x.shape[2]
    shard_cols = w_in.shape[2]
    conv_cols = gdn_conv_w.shape[2]

    w_in_g, w_out_g, conv_g = _gather_call(
        "gather_weights", [w_in[0].astype(WIRE_DTYPE), w_out[0].astype(WIRE_DTYPE), gdn_conv_w[0]])
    w_full = w_in_g.transpose(1, 0, 2).reshape(d, N_DEV * shard_cols)
    conv_full = conv_g.transpose(1, 0, 2).reshape(CONV_WIDTH, N_DEV * conv_cols)

    (loss_p, grad_x, g_n1, g_w_in_full, g_sbw, g_conv, g_alog, g_dtb, g_gdw, g_wout, g_fw) = _device_step(
        x[0], loss_target[0], w_full, w_out_g.reshape(d, d), conv_full, norm1_w, sb_norm_w, gdn_A_log, gdn_dt_bias,
        gdn_norm_w, final_norm_w)

    g_w_in_parts = g_w_in_full.reshape(d, N_DEV, shard_cols).transpose(1, 0, 2)
    g_wout_parts = g_wout.reshape(N_DEV, d // N_DEV, d)
    g_conv_parts = g_conv.reshape(CONV_WIDTH, N_DEV, conv_cols).transpose(1, 0, 2)
    fold = lambda a, group: a.reshape(-1, group).sum(axis=0)
    small_g = _pack_small([g_n1, g_fw, fold(g_sbw, SB_HEAD_DIM), fold(g_gdw, GDN_HEAD_DIM),
                           g_alog[0, GDN_HEADS:2 * GDN_HEADS], g_dtb[0, GDN_HEADS:2 * GDN_HEADS],
                           loss_p[0, :1]], d)
    sib_w_in, sib_wout, sib_conv = _sibling_send_call("grads_to_sibling", [g_w_in_parts, g_wout_parts, g_conv_parts])
    c_w_in = _pair_sum_call("pair_sum_w_in", g_w_in_parts, sib_w_in, 256)
    c_wout = _pair_sum_call("pair_sum_w_out", g_wout_parts, sib_wout, d // N_DEV)
    c_conv = _pair_sum_call("pair_sum_conv", g_conv_parts, sib_conv, CONV_WIDTH)
    p_w_in, p_wout, p_conv = _chip_exchange_call("grads_to_chips", [c_w_in, c_wout, c_conv])
    (p_small,) = _exchange_call("exchange_small", [small_g], [False])

    small_w = _pack_small([norm1_w, final_norm_w, sb_norm_w, gdn_norm_w, gdn_A_log, gdn_dt_bias], d)
    small_m = _pack_small([m_norm1_w, m_final_norm_w, m_sb_norm_w, m_gdn_norm_w, m_gdn_A_log, m_gdn_dt_bias], d)
    small_v = _pack_small([v_norm1_w, v_final_norm_w, v_sb_norm_w, v_gdn_norm_w, v_gdn_A_log, v_gdn_dt_bias], d)

    r_w_in = _adam_call("adam_w_in", p_w_in, w_in[0], m_w_in[0], v_w_in[0], 256)
    r_wout = _adam_call("adam_w_out", p_wout, w_out[0], m_w_out[0], v_w_out[0], d // N_DEV)
    r_conv = _adam_call("adam_conv", p_conv, gdn_conv_w[0], m_gdn_conv_w[0], v_gdn_conv_w[0], CONV_WIDTH)
    r_small = _adam_call("adam_small", p_small, small_w, small_m, small_v, 8)

    shapes = {"norm1_w": norm1_w.shape, "final_norm_w": final_norm_w.shape, "sb_norm_w": sb_norm_w.shape,
              "gdn_norm_w": gdn_norm_w.shape, "gdn_A_log": gdn_A_log.shape, "gdn_dt_bias": gdn_dt_bias.shape}

    def small_out(kind, name):
        row = _SMALL_ROWS.index(name)
        shp = shapes[name]
        size = 1
        for s in shp:
            size *= s
        return r_small[kind][row, :size].reshape(shp)

    def outputs(kind):
        return (small_out(kind, "norm1_w"), r_w_in[kind][None], small_out(kind, "sb_norm_w"), r_conv[kind][None],
                small_out(kind, "gdn_A_log"), small_out(kind, "gdn_dt_bias"), small_out(kind, "gdn_norm_w"),
                r_wout[kind][None], small_out(kind, "final_norm_w"))

    loss = r_small[0][_SMALL_ROWS.index("loss"), 0]
    return (loss, grad_x[None], *outputs(0), *outputs(1), *outputs(2), *outputs(3))
```

```python
import functools

import jax
import jax.numpy as jnp
from jax import lax
from jax.experimental import pallas as pl
from jax.experimental.pallas import tpu as pltpu

F32 = jnp.float32
MXU_DTYPE = jnp.bfloat16
WIRE_DTYPE = jnp.bfloat16
EXACT = lax.Precision.HIGHEST
EPS = 1e-6
N_DEV = 8
SB_HEAD_DIM = 64
GDN_HEAD_DIM = 128
GDN_HEADS = 4
GDN_CHUNKS_PER_STEP = 4
CHUNK = 64
CONV_WIDTH = 4
LANES = 128
SB_BLOCK = 128
SB_BQ = 256
VMEM_LIMIT_BYTES = 56 * 1024 * 1024

DPROJ_PIECE_OF_SLOT = (0, 1, 2, 4, 5, 6, 3, 7)
DPROJ_SB_SLOT, DPROJ_GDN_SLOT, DPROJ_GATE_SLOT = 0, 3, 6

ADAM_LR = 0.001
ADAM_B1 = 0.9
ADAM_B2 = 0.999
ADAM_EPS = 1e-08
ADAM_WD = 0.01
ADAM_STEP = 10

_NN = (((1,), (0,)), ((), ()))
_NT = (((1,), (1,)), ((), ()))
_TN = (((0,), (0,)), ((), ()))
_BNN = (((2,), (1,)), ((0,), (0,)))
_BNT = (((2,), (2,)), ((0,), (0,)))
_BTN = (((1,), (1,)), ((0,), (0,)))


def _mm(a, b):
    return jnp.dot(a.astype(MXU_DTYPE), b.astype(MXU_DTYPE), preferred_element_type=F32)


def _mm_nt(a, b):
    return lax.dot_general(a.astype(MXU_DTYPE), b.astype(MXU_DTYPE), _NT, preferred_element_type=F32)


def _mm_tn(a, b):
    return lax.dot_general(a.astype(MXU_DTYPE), b.astype(MXU_DTYPE), _TN, preferred_element_type=F32)


def _mx(a, b):
    return jnp.dot(a, b, precision=EXACT, preferred_element_type=F32)


def _mx_nt(a, b):
    return lax.dot_general(a, b, _NT, precision=EXACT, preferred_element_type=F32)


def _mx_tn(a, b):
    return lax.dot_general(a, b, _TN, precision=EXACT, preferred_element_type=F32)


def _split(x):
    hi = x.astype(MXU_DTYPE)
    return hi, (x - hi.astype(F32)).astype(MXU_DTYPE)


def _m3_general(a, b, dims):
    ah, al = _split(a)
    bh, bl = _split(b)
    dot = lambda x, y: lax.dot_general(x, y, dims, preferred_element_type=F32)
    (contract, _), (batch, _) = dims
    free = [ax for ax in range(a.ndim) if ax not in contract and ax not in batch][0]
    m = a.shape[free]
    both = dot(jnp.concatenate([ah, al], axis=free), bh)
    out_axis = len(batch)
    hi_part = lax.slice_in_dim(both, 0, m, axis=out_axis)
    lo_part = lax.slice_in_dim(both, m, 2 * m, axis=out_axis)
    return hi_part + (dot(ah, bl) + lo_part)


def _m3(a, b):
    return _m3_general(a, b, _NN)


def _m3_nt(a, b):
    return _m3_general(a, b, _NT)


def _m3_tn(a, b):
    return _m3_general(a, b, _TN)


def _sigmoid(z):
    return 1.0 / (1.0 + jnp.exp(-z))


def _softplus(z):
    return jnp.maximum(z, 0.0) + jnp.log(1.0 + jnp.exp(-jnp.abs(z)))


def _params(*semantics):
    return pltpu.CompilerParams(dimension_semantics=semantics, vmem_limit_bytes=VMEM_LIMIT_BYTES)


def _inproj_call(x, norm_w, w_main, w_small, w_small_t, tm=256):
    t_len, d = x.shape
    n = w_main.shape[1]
    ns = w_small.shape[1]
    nst = w_small_t.shape[0]

    def body(x_ref, nw_ref, wm_ref, ws_ref, wst_ref, pm_ref, ps_ref, pst_ref, ht_ref, r_ref):
        xv = x_ref[...]
        r = lax.rsqrt(jnp.mean(xv * xv, axis=-1, keepdims=True) + EPS)
        h = xv * r * nw_ref[...]
        hb = h.astype(MXU_DTYPE)
        for n0 in range(0, n, 512):
            pm_ref[:, n0:n0 + 512] = jnp.dot(hb, wm_ref[:, n0:n0 + 512], preferred_element_type=F32)
        ps_ref[...] = jnp.dot(hb, ws_ref[...], preferred_element_type=F32)
        pst_ref[...] = lax.dot_general(wst_ref[...], hb, _NT, preferred_element_type=F32)
        ht_ref[...] = h.T.astype(MXU_DTYPE)
        r_ref[...] = r

    return pl.pallas_call(
        body, name="inproj",
        grid=(t_len // tm,),
        in_specs=[pl.BlockSpec((tm, d), lambda i: (i, 0)),
                  pl.BlockSpec((1, d), lambda i: (0, 0)),
                  pl.BlockSpec((d, n), lambda i: (0, 0)),
                  pl.BlockSpec((d, ns), lambda i: (0, 0)),
                  pl.BlockSpec((nst, d), lambda i: (0, 0))],
        out_specs=[pl.BlockSpec((tm, n), lambda i: (i, 0)),
                   pl.BlockSpec((tm, ns), lambda i: (i, 0)),
                   pl.BlockSpec((nst, tm), lambda i: (0, i)),
                   pl.BlockSpec((d, tm), lambda i: (0, i)),
                   pl.BlockSpec((tm, 1), lambda i: (i, 0))],
        out_shape=[jax.ShapeDtypeStruct((t_len, n), F32),
                   jax.ShapeDtypeStruct((t_len, ns), F32),
                   jax.ShapeDtypeStruct((nst, t_len), F32),
                   jax.ShapeDtypeStruct((d, t_len), MXU_DTYPE),
                   jax.ShapeDtypeStruct((t_len, 1), F32)],
        compiler_params=_params("arbitrary"),
    )(x, norm_w, w_main, w_small, w_small_t)


def _running_sum_mm(x, tri):
    hi = x.astype(MXU_DTYPE)
    lo = (x - hi.astype(F32)).astype(MXU_DTYPE)
    return jnp.dot(hi, tri, preferred_element_type=F32) + jnp.dot(lo, tri, preferred_element_type=F32)


def _sb_iotas():
    row_i = lax.broadcasted_iota(jnp.int32, (SB_BQ, SB_BLOCK), 0)
    col_i = lax.broadcasted_iota(jnp.int32, (SB_BQ, SB_BLOCK), 1)
    sq_r = lax.broadcasted_iota(jnp.int32, (SB_BLOCK, SB_BLOCK), 0)
    sq_c = lax.broadcasted_iota(jnp.int32, (SB_BLOCK, SB_BLOCK), 1)
    return row_i, col_i, sq_r, sq_c


SB_DIAG_BLOCKS = SB_BQ // SB_BLOCK
SB_EXP_FLOOR = -110.0


def _sb_keys_descending(qi, tile, carry, z_bounds, n_heads):
    n_free = SB_DIAG_BLOCKS * qi
    for j in range(SB_DIAG_BLOCKS - 1, -1, -1):
        carry = tile(n_free + j, True, carry)

    def largest_exponent(c):
        worst = jnp.max(z_bounds[0] - c[1])
        for h in range(1, n_heads):
            worst = jnp.maximum(worst, jnp.max(z_bounds[h] - c[1 + h]))
        return worst

    def cond(state):
        return (state[0] < n_free) & (state[1] > SB_EXP_FLOOR)

    def body(state):
        c = tile(n_free - 1 - state[0], False, state[2:])
        return (state[0] + 1, largest_exponent(c), *c)

    out = lax.while_loop(cond, body, (jnp.int32(0), largest_exponent(carry), *carry))
    return out[2:], out[0]


def _sb_keys_ascending(qi, n_run, tile, carry):
    n_free = SB_DIAG_BLOCKS * qi
    carry = lax.fori_loop(0, n_run, lambda s, c: tile(n_free - n_run + s, False, c), carry)
    for j in range(SB_DIAG_BLOCKS):
        carry = tile(n_free + j, True, carry)
    return carry


def _sb_fwd_call(proj, t_len):
    nq = t_len // SB_BQ
    scale = float(SB_HEAD_DIM) ** -0.5
    n_pairs = 512 // LANES
    per_pair = LANES // SB_HEAD_DIM

    def body(q_ref, k_ref, v_ref, o_ref, st_ref, nrun_ref):
        lane = lax.broadcasted_iota(jnp.int32, (1, LANES), 1)
        row_i, col_i, sq_r, sq_c = _sb_iotas()
        ge = (sq_r >= sq_c).astype(MXU_DTYPE)
        hms = [((lane // SB_HEAD_DIM) == hh).astype(F32) for hh in range(per_pair)]
        k_sq = k_ref[...] * k_ref[...]
        k_norms = [jnp.sqrt(jnp.max(jnp.sum(k_sq * hm, axis=-1, keepdims=True))) * (1.02 * scale) for hm in hms]

        def q_loop(qi, carry):
            r0 = pl.multiple_of(qi * SB_BQ, SB_BQ)
            rows = pl.ds(r0, SB_BQ)
            q_all = q_ref[rows, :]
            qms = [(q_all * (hm * scale)).astype(MXU_DTYPE) for hm in hms]
            z_bounds = [jnp.sqrt(jnp.sum(q_all * q_all * hm, axis=-1, keepdims=True)) * kn
                        for hm, kn in zip(hms, k_norms)]

            def tile(kj, masked, kc):
                acc, cs = kc[0], list(kc[1:])
                heads = range(per_pair)
                s0 = pl.multiple_of(kj * SB_BLOCK, SB_BLOCK)
                cols = pl.ds(s0, SB_BLOCK)
                kb = k_ref[cols, :].astype(MXU_DTYPE)
                v_all = v_ref[cols, :]
                vms = [(v_all * hms[h]).astype(MXU_DTYPE) for h in heads]
                zs = [lax.dot_general(qms[h], kb, _NT, preferred_element_type=F32) for h in heads]
                sps = [_softplus(z) for z in zs]
                if masked:
                    mask = (col_i + s0) < (row_i + r0)
                    sps = [jnp.where(mask, sp, 0.0) for sp in sps]
                sums = [_running_sum_mm(sp, ge) for sp in sps]
                ws = [jnp.exp(zs[h] - (sums[h] + cs[h])) for h in heads]
                if masked:
                    ws = [jnp.where(mask, w, 0.0) for w in ws]
                for h in heads:
                    acc = acc + jnp.dot(ws[h].astype(MXU_DTYPE), vms[h], preferred_element_type=F32)
                cs = [cs[h] + jnp.sum(sps[h], axis=-1, keepdims=True) for h in heads]
                return (acc, *cs)

            zero_col = jnp.zeros((SB_BQ, 1), F32)
            out, n_run = _sb_keys_descending(
                qi, tile, (jnp.zeros((SB_BQ, LANES), F32),) + (zero_col,) * per_pair, z_bounds, per_pair)
            o_ref[rows, :] = out[0]
            for hh in range(per_pair):
                st_ref[hh, rows, :] = out[1 + hh]
            nrun_ref[pl.program_id(0), qi] = n_run
            return carry

        lax.fori_loop(0, nq, q_loop, 0)

    return pl.pallas_call(
        body, name="sb_fwd",
        grid=(n_pairs,),
        in_specs=[pl.BlockSpec((t_len, LANES), lambda p: (0, p)),
                  pl.BlockSpec((t_len, LANES), lambda p: (0, n_pairs + p)),
                  pl.BlockSpec((t_len, LANES), lambda p: (0, 2 * n_pairs + p))],
        out_specs=[pl.BlockSpec((t_len, LANES), lambda p: (0, p)),
                   pl.BlockSpec((per_pair, t_len, 1), lambda p: (p, 0, 0)),
                   pl.BlockSpec(memory_space=pltpu.SMEM)],
        out_shape=[jax.ShapeDtypeStruct((t_len, 512), F32),
                   jax.ShapeDtypeStruct((n_pairs * per_pair, t_len, 1), F32),
                   jax.ShapeDtypeStruct((n_pairs, nq), jnp.int32)],
        compiler_params=_params("arbitrary"),
    )(proj, proj, proj)


def _sb_bwd_call(proj, sp_total, n_run_all, d_o, dproj, t_len):
    nq = t_len // SB_BQ
    scale = float(SB_HEAD_DIM) ** -0.5
    n_pairs = 512 // LANES
    per_pair = LANES // SB_HEAD_DIM

    def body(q_ref, k_ref, v_ref, st_ref, nrun_ref, do_ref, dproj_in_ref, d_ref):
        lane = lax.broadcasted_iota(jnp.int32, (1, LANES), 1)
        row_i, col_i, sq_r, sq_c = _sb_iotas()
        lt = (sq_r < sq_c).astype(MXU_DTYPE)
        le = (sq_r <= sq_c).astype(MXU_DTYPE)
        hms = [((lane // SB_HEAD_DIM) == hh).astype(F32) for hh in range(per_pair)]
        d_ref[1] = jnp.zeros((t_len, LANES), F32)
        d_ref[2] = jnp.zeros((t_len, LANES), F32)

        def q_loop(qi, carry):
            r0 = pl.multiple_of(qi * SB_BQ, SB_BQ)
            rows = pl.ds(r0, SB_BQ)
            q_all, do_all = q_ref[rows, :], do_ref[rows, :]
            qms = [(q_all * (hm * scale)).astype(MXU_DTYPE) for hm in hms]
            doms = [(do_all * hm).astype(MXU_DTYPE) for hm in hms]
            totals = [st_ref[hh, rows, :] for hh in range(per_pair)]

            def tile(kj, masked, kc):
                dq, cls, gls = kc[0], list(kc[1:1 + per_pair]), list(kc[1 + per_pair:])
                heads = range(per_pair)
                s0 = pl.multiple_of(kj * SB_BLOCK, SB_BLOCK)
                cols = pl.ds(s0, SB_BLOCK)
                k_all, v_all = k_ref[cols, :], v_ref[cols, :]
                kb = k_all.astype(MXU_DTYPE)
                vms = [(v_all * hms[h]).astype(MXU_DTYPE) for h in heads]
                kms = [(k_all * (hms[h] * scale)).astype(MXU_DTYPE) for h in heads]
                zs = [lax.dot_general(qms[h], kb, _NT, preferred_element_type=F32) for h in heads]
                das = [lax.dot_general(doms[h], vms[h], _NT, preferred_element_type=F32) for h in heads]
                sp_alls = [_softplus(z) for z in zs]
                sps = sp_alls
                if masked:
                    mask = (col_i + s0) < (row_i + r0)
                    sps = [jnp.where(mask, sp, 0.0) for sp in sp_alls]
                lefts = [_running_sum_mm(sp, lt) for sp in sps]
                ws = [jnp.exp(zs[h] - (totals[h] - cls[h] - lefts[h])) for h in heads]
                if masked:
                    ws = [jnp.where(mask, w, 0.0) for w in ws]
                gs = [das[h] * ws[h] for h in heads]
                g_sums = [_running_sum_mm(g, le) for g in gs]
                dzs = [gs[h] - jnp.exp(zs[h] - sp_alls[h]) * (gls[h] + g_sums[h]) for h in heads]
                if masked:
                    dzs = [jnp.where(mask, dz, 0.0) for dz in dzs]
                dzs = [dz.astype(MXU_DTYPE) for dz in dzs]
                dk_t = jnp.zeros((SB_BLOCK, LANES), F32)
                dv_t = jnp.zeros((SB_BLOCK, LANES), F32)
                for h in heads:
                    dq = dq + jnp.dot(dzs[h], kms[h], preferred_element_type=F32)
                    dk_t = dk_t + lax.dot_general(dzs[h], qms[h], _TN, preferred_element_type=F32)
                    dv_t = dv_t + lax.dot_general(ws[h].astype(MXU_DTYPE), doms[h], _TN, preferred_element_type=F32)
                d_ref[1, cols, :] += dk_t
                d_ref[2, cols, :] += dv_t
                cls = [cls[h] + jnp.sum(sps[h], axis=-1, keepdims=True) for h in heads]
                gls = [gls[h] + jnp.sum(gs[h], axis=-1, keepdims=True) for h in heads]
                return (dq, *cls, *gls)

            zero_col = jnp.zeros((SB_BQ, 1), F32)
            out = _sb_keys_ascending(qi, nrun_ref[pl.program_id(0), qi], tile,
                                     (jnp.zeros((SB_BQ, LANES), F32),) + (zero_col,) * (2 * per_pair))
            d_ref[0, rows, :] = out[0]
            return carry

        lax.fori_loop(0, nq, q_loop, 0)

    col = lambda off: pl.BlockSpec((t_len, LANES), lambda p: (0, off + p))
    return pl.pallas_call(
        body, name="sb_bwd",
        grid=(n_pairs,),
        in_specs=[col(0), col(n_pairs), col(2 * n_pairs),
                  pl.BlockSpec((per_pair, t_len, 1), lambda p: (p, 0, 0)),
                  pl.BlockSpec(memory_space=pltpu.SMEM), col(0), _HBM],
        out_specs=pl.BlockSpec((3, t_len, LANES), lambda p: (DPROJ_SB_SLOT // 3, 0, p)),
        out_shape=jax.ShapeDtypeStruct(dproj.shape, dproj.dtype),
        input_output_aliases={6: 0},
        compiler_params=_params("arbitrary"),
    )(proj, proj, proj, sp_total, n_run_all, d_o, dproj)


def _conv_taps(xin, rows, t_len):
    taps = []
    for i in range(CONV_WIDTH):
        shift = CONV_WIDTH - 1 - i
        if shift == 0:
            taps.append(xin)
        else:
            taps.append(jnp.where(rows >= shift, pltpu.roll(xin, shift, axis=0), 0.0))
    return taps


def _gdn_prep_body_common(x_ref, w_ref, t_len):
    j = pl.program_id(0)
    xin = x_ref[...]
    rows = lax.broadcasted_iota(jnp.int32, (t_len, LANES), 0)
    taps = _conv_taps(xin, rows, t_len)
    pre = taps[0] * w_ref[0:1, :]
    for i in range(1, CONV_WIDTH):
        pre = pre + taps[i] * w_ref[i:i + 1, :]
    sg = _sigmoid(pre)
    act = pre * sg
    is_qk = j < 2 * GDN_HEADS
    nrm = jnp.where(is_qk, lax.rsqrt(jnp.sum(act * act, axis=-1, keepdims=True) + EPS), 1.0)
    sc = jnp.where(j < GDN_HEADS, float(GDN_HEAD_DIM) ** -0.5, 1.0)
    return j, rows, taps, pre, sg, act, is_qk, nrm, sc


def _gdn_prep_call(proj, conv_w, t_len):
    first = 2048 // LANES

    def body(x_ref, w_ref, out_ref):
        _, _, _, _, _, act, _, nrm, sc = _gdn_prep_body_common(x_ref, w_ref, t_len)
        out_ref[...] = act * nrm * sc

    return pl.pallas_call(
        body, name="gdn_prep",
        grid=(3 * GDN_HEADS,),
        in_specs=[pl.BlockSpec((t_len, LANES), lambda j: (0, first + j)),
                  pl.BlockSpec((CONV_WIDTH, LANES), lambda j: (0, j))],
        out_specs=pl.BlockSpec((t_len, LANES), lambda j: (0, j)),
        out_shape=jax.ShapeDtypeStruct((t_len, 3 * 512), F32),
        compiler_params=_params("arbitrary"),
    )(proj, conv_w)


def _gdn_prep_bwd_call(proj, conv_w, d_act3, dproj, t_len):
    first = 2048 // LANES

    def body(x_ref, w_ref, d_ref, dproj_in_ref, dx_ref, dw_ref):
        _, rows, taps, pre, sg, act, is_qk, nrm, sc = _gdn_prep_body_common(x_ref, w_ref, t_len)
        d_out = d_ref[0]
        dn = d_out * sc
        d_norm = nrm * dn - act * (nrm * nrm * nrm) * jnp.sum(dn * act, axis=-1, keepdims=True)
        d_act = jnp.where(is_qk, d_norm, d_out)
        d_pre = d_act * sg * (1.0 + pre * (1.0 - sg))
        dx = d_pre * w_ref[CONV_WIDTH - 1:CONV_WIDTH, :]
        dw_ref[CONV_WIDTH - 1:CONV_WIDTH, :] = jnp.sum(d_pre * taps[CONV_WIDTH - 1], axis=0, keepdims=True)
        for i in range(CONV_WIDTH - 1):
            shift = CONV_WIDTH - 1 - i
            up = jnp.where(rows < t_len - shift, pltpu.roll(d_pre, t_len - shift, axis=0), 0.0)
            dx = dx + up * w_ref[i:i + 1, :]
            dw_ref[i:i + 1, :] = jnp.sum(d_pre * taps[i], axis=0, keepdims=True)
        dx_ref[0] = dx

    return pl.pallas_call(
        body, name="gdn_prep_bwd",
        grid=(3 * GDN_HEADS,),
        in_specs=[pl.BlockSpec((t_len, LANES), lambda j: (0, first + j)),
                  pl.BlockSpec((CONV_WIDTH, LANES), lambda j: (0, j)),
                  pl.BlockSpec((1, t_len, LANES), lambda j: (j // GDN_HEADS, 0, j % GDN_HEADS)), _HBM],
        out_specs=[pl.BlockSpec((1, t_len, LANES), lambda j: (DPROJ_GDN_SLOT + j // GDN_HEADS, 0, j % GDN_HEADS)),
                   pl.BlockSpec((CONV_WIDTH, LANES), lambda j: (0, j))],
        out_shape=[jax.ShapeDtypeStruct(dproj.shape, dproj.dtype),
                   jax.ShapeDtypeStruct((CONV_WIDTH, 3 * 512), F32)],
        input_output_aliases={3: 0},
        compiler_params=_params("arbitrary"),
    )(proj, conv_w, d_act3, dproj)


def _chunk_cumsum_matrix():
    r = lax.broadcasted_iota(jnp.int32, (LANES, LANES), 0)
    c = lax.broadcasted_iota(jnp.int32, (LANES, LANES), 1)
    return ((r <= c) & ((r // CHUNK) == (c // CHUNK))).astype(F32)


def _gdn_gates_call(ps, pst, alog_l, dtb_l, alog_c, dtb_c, t_len):
    def body(ps_ref, pst_ref, al_ref, dl_ref, ac_ref, dc_ref, beta_ref, gcol_ref, grow_ref):
        upper = _chunk_cumsum_matrix()
        lower = upper.T
        psv = ps_ref[...]
        beta_ref[...] = _sigmoid(psv)
        g_l = -jnp.exp(al_ref[...]) * _softplus(psv + dl_ref[...])
        g_r = -jnp.exp(ac_ref[...]) * _softplus(pst_ref[...] + dc_ref[...])
        for w in range(t_len // LANES):
            sl = slice(w * LANES, (w + 1) * LANES)
            gcol_ref[sl, :] = _mx(lower, g_l[sl, :])
            grow_ref[:, sl] = _mx(g_r[:, sl], upper)

    vm = pl.BlockSpec(memory_space=pltpu.VMEM)
    return pl.pallas_call(
        body, name="gdn_gates",
        in_specs=[vm] * 6, out_specs=[vm] * 3,
        out_shape=[jax.ShapeDtypeStruct((t_len, LANES), F32),
                   jax.ShapeDtypeStruct((t_len, LANES), F32),
                   jax.ShapeDtypeStruct((8, t_len), F32)],
        compiler_params=pltpu.CompilerParams(vmem_limit_bytes=VMEM_LIMIT_BYTES),
    )(ps, pst, alog_l, dtb_l, alog_c, dtb_c)


def _gdn_gates_bwd_call(ps, alog_l, dtb_l, d_l, t_len):
    def body(ps_ref, al_ref, dl_ref, d_ref, dps_ref, gal_ref, gdt_ref):
        lane = lax.broadcasted_iota(jnp.int32, (1, LANES), 1)
        psv = ps_ref[...]
        dv = d_ref[...]
        beta = _sigmoid(psv)
        ea = jnp.exp(al_ref[...])
        arg = psv + dl_ref[...]
        g = -ea * _softplus(arg)
        d_a = dv * (-ea) * _sigmoid(arg)
        is_a = (lane >= GDN_HEADS) & (lane < 2 * GDN_HEADS)
        dps_ref[...] = jnp.where(lane < GDN_HEADS, dv * beta * (1.0 - beta), jnp.where(is_a, d_a, 0.0))
        gdt_ref[...] = jnp.where(is_a, jnp.sum(d_a, axis=0, keepdims=True), 0.0)
        gal_ref[...] = jnp.where(is_a, jnp.sum(dv * g, axis=0, keepdims=True), 0.0)

    vm = pl.BlockSpec(memory_space=pltpu.VMEM)
    return pl.pallas_call(
        body, name="gdn_gates_bwd",
        in_specs=[vm] * 4, out_specs=[vm] * 3,
        out_shape=[jax.ShapeDtypeStruct((t_len, LANES), F32),
                   jax.ShapeDtypeStruct((1, LANES), F32),
                   jax.ShapeDtypeStruct((1, LANES), F32)],
        compiler_params=pltpu.CompilerParams(vmem_limit_bytes=VMEM_LIMIT_BYTES),
    )(ps, alog_l, dtb_l, d_l)


def _bm(a, b):
    return _m3_general(a, b, _BNN)


def _bm_nt(a, b):
    return _m3_general(a, b, _BNT)


def _bm_tn(a, b):
    return _m3_general(a, b, _BTN)


def _heads_of(ref, rows):
    return jnp.stack([ref[rows, h * GDN_HEAD_DIM:(h + 1) * GDN_HEAD_DIM] for h in range(GDN_HEADS)])


def _chunk_terms(q_ref, k_ref, v_ref, b_ref, gc_ref, gr_ref, c, incl, strict):
    r0 = pl.multiple_of(c * CHUNK, CHUNK)
    rows = pl.ds(r0, CHUNK)
    q, k, v = _heads_of(q_ref, rows), _heads_of(k_ref, rows), _heads_of(v_ref, rows)
    lane_ids = lax.broadcasted_iota(jnp.int32, (1, LANES), 1)
    pick = lambda slab, first: jnp.stack([jnp.sum(jnp.where(lane_ids == first + h, slab, 0.0), axis=-1, keepdims=True)
                                          for h in range(GDN_HEADS)])
    b = pick(b_ref[rows, :], 0)
    gc = pick(gc_ref[rows, :], GDN_HEADS)
    gr = gr_ref[:, c]
    dm = jnp.where(incl, jnp.exp(jnp.where(incl, gc - gr, 0.0)), 0.0)
    kb = k * b
    vb = v * b
    e = jnp.exp(gc)
    kk_qk = _bm_nt(jnp.concatenate([kb, q], axis=1), k)
    a = jnp.where(strict, kk_qk[:, :CHUNK] * dm, 0.0)
    p = jnp.where(incl, kk_qk[:, CHUNK:] * dm, 0.0)
    gl = gc[:, CHUNK - 1:CHUNK, :]
    eg = jnp.exp(gl - gc)
    return rows, q, k, v, b, gc, dm, kb, vb, e, a, p, gl, eg


def _unit_lower_inverse(a, eye):
    x = -a
    tm = eye + x
    xp = _bm(x, x)
    for _ in range(4):
        both = _bm(jnp.concatenate([xp, tm], axis=1), xp)
        tm = tm + both[:, CHUNK:]
        xp = both[:, :CHUNK]
    return tm + _bm(tm, xp)


def _gdn_specs(t_len, n_chunks, reverse):
    cps = GDN_CHUNKS_PER_STEP
    steps = n_chunks // cps
    at = (lambda g: steps - 1 - g) if reverse else (lambda g: g)
    rows_blk = lambda width, part=0: pl.BlockSpec((cps * CHUNK, width), lambda g: (at(g), part))
    gate_r = pl.BlockSpec((GDN_HEADS, cps, 1, CHUNK), lambda g: (0, at(g), 0, 0))
    per_chunk = lambda r, c: pl.BlockSpec((GDN_HEADS, cps, r, c), lambda g: (0, at(g), 0, 0))
    return cps, steps, rows_blk, gate_r, per_chunk


def _gdn_fwd_call(gact, beta_c, gam_c, gam_r, t_len):
    n_chunks = t_len // CHUNK
    dk = GDN_HEAD_DIM
    width = GDN_HEADS * dk
    cps, steps, rows_blk, gate_r, per_chunk = _gdn_specs(t_len, n_chunks, False)

    def body(q_ref, k_ref, v_ref, b_ref, gc_ref, gr_ref, o_ref, s_ref, t_ref, state_ref):
        row = lax.broadcasted_iota(jnp.int32, (CHUNK, CHUNK), 0)
        col = lax.broadcasted_iota(jnp.int32, (CHUNK, CHUNK), 1)
        incl, strict = row >= col, row > col
        eye = (row == col).astype(F32)

        @pl.when(pl.program_id(0) == 0)
        def _():
            state_ref[...] = jnp.zeros_like(state_ref)

        def chunk(c, carry):
            rows, q, k, v, b, gc, dm, kb, vb, e, a, p, gl, eg = _chunk_terms(
                q_ref, k_ref, v_ref, b_ref, gc_ref, gr_ref, c, incl, strict)
            s = state_ref[...]
            tm = _unit_lower_inverse(a, eye)
            uw = _bm(tm, jnp.concatenate([vb, kb * e], axis=2))
            u, w = uw[:, :, :dk], uw[:, :, dk:]
            ws_qs = _bm(jnp.concatenate([w, q * e], axis=1), s)
            vn = u - ws_qs[:, :CHUNK]
            o = ws_qs[:, CHUNK:] + _bm(p, vn)
            for h in range(GDN_HEADS):
                o_ref[rows, h * dk:(h + 1) * dk] = o[h]
            s_ref[:, c] = s
            t_ref[:, c] = tm
            state_ref[...] = s * jnp.exp(gl) + _bm_tn(k * eg, vn)
            return carry

        lax.fori_loop(0, cps, chunk, 0)

    return pl.pallas_call(
        body, name="gdn_fwd",
        grid=(steps,),
        in_specs=[rows_blk(width, 0), rows_blk(width, 1), rows_blk(width, 2), rows_blk(LANES), rows_blk(LANES), gate_r],
        out_specs=[rows_blk(width), per_chunk(dk, dk), per_chunk(CHUNK, CHUNK)],
        out_shape=[jax.ShapeDtypeStruct((t_len, width), F32),
                   jax.ShapeDtypeStruct((GDN_HEADS, n_chunks, dk, dk), F32),
                   jax.ShapeDtypeStruct((GDN_HEADS, n_chunks, CHUNK, CHUNK), F32)],
        scratch_shapes=[pltpu.VMEM((GDN_HEADS, dk, dk), F32)],
        compiler_params=_params("arbitrary"),
    )(gact, gact, gact, beta_c, gam_c, gam_r)


def _gdn_bwd_call(gact, beta_c, gam_c, gam_r, s_all, t_all, d_o, t_len):
    n_chunks = t_len // CHUNK
    dk = GDN_HEAD_DIM
    width = GDN_HEADS * dk
    cps, steps, rows_blk, gate_r, per_chunk = _gdn_specs(t_len, n_chunks, True)

    def body(q_ref, k_ref, v_ref, b_ref, gc_ref, gr_ref, s_ref, t_ref, do_ref, d_ref, dgate_ref, dstate_ref):
        row = lax.broadcasted_iota(jnp.int32, (CHUNK, CHUNK), 0)
        col = lax.broadcasted_iota(jnp.int32, (CHUNK, CHUNK), 1)
        incl, strict = row >= col, row > col
        upper = jnp.broadcast_to((row <= col).astype(F32), (GDN_HEADS, CHUNK, CHUNK))
        ones = jnp.ones((GDN_HEADS, CHUNK, LANES), F32)
        last_row = lax.broadcasted_iota(jnp.int32, (CHUNK, 1), 0) == CHUNK - 1
        lane_ids = lax.broadcasted_iota(jnp.int32, (1, LANES), 1)
        rsum = lambda m: jnp.sum(m, axis=-1, keepdims=True)
        total = lambda m: jnp.sum(rsum(m), axis=1, keepdims=True)

        @pl.when(pl.program_id(0) == 0)
        def _():
            dstate_ref[...] = jnp.zeros_like(dstate_ref)

        def chunk(step, carry):
            c = cps - 1 - step
            rows, q, k, v, b, gc, dm, kb, vb, e, a, p, gl, eg = _chunk_terms(
                q_ref, k_ref, v_ref, b_ref, gc_ref, gr_ref, c, incl, strict)
            ds = dstate_ref[...]
            s = s_ref[:, c]
            tm = t_ref[:, c]
            d_out = _heads_of(do_ref, rows)
            el = jnp.exp(gl)
            kbe = kb * e
            uw = _bm(tm, jnp.concatenate([vb, kbe], axis=2))
            u, w = uw[:, :, :dk], uw[:, :, dk:]
            vn = u - _bm(w, s)
            qe = q * e
            kd = k * eg

            d_vn = _bm_tn(p, d_out) + _bm(kd, ds)
            on_s = _bm_nt(jnp.concatenate([d_out, d_vn], axis=1), s)
            d_qe, d_w = on_s[:, :CHUNK], -on_s[:, CHUNK:]
            d_p = jnp.where(incl, _bm_nt(d_out, vn), 0.0)
            dstate_ref[...] = el * ds + _bm_tn(jnp.concatenate([qe, -w], axis=1),
                                               jnp.concatenate([d_out, d_vn], axis=1))
            d_kd = _bm_nt(vn, ds)
            d_both = _bm_tn(tm, jnp.concatenate([d_vn, d_w], axis=2))
            d_vb, d_kbe = d_both[:, :, :dk], d_both[:, :, dk:]
            d_a = -jnp.where(strict, _bm_nt(d_both, uw), 0.0)
            m = d_a * dm
            n = d_p * dm
            on_k = _bm(jnp.concatenate([m, n], axis=1), k)
            d_kb = on_k[:, :CHUNK] + d_kbe * e
            d_q = on_k[:, CHUNK:] + d_qe * e
            d_k = (_bm_tn(jnp.concatenate([m, n], axis=1), jnp.concatenate([kb, q], axis=1))
                   + d_kd * eg + b * d_kb)
            d_v = b * d_vb
            r = d_a * a + d_p * p
            kd_term = rsum(d_kd * kd)
            d_gl = total(ds * s) * el + jnp.sum(kd_term, axis=1, keepdims=True)
            d_gam = (rsum(r) - _bm_tn(r, ones)[:, :, 0:1] + rsum(d_qe * qe) + rsum(d_kbe * kbe) - kd_term
                     + jnp.where(last_row, d_gl, 0.0))
            d_beta = rsum(d_kb * k) + rsum(d_vb * v)
            d_g = _bm(upper, d_gam * ones)[:, :, 0:1]
            gates = jnp.zeros((CHUNK, LANES), F32)
            for h in range(GDN_HEADS):
                lanes = slice(h * dk, (h + 1) * dk)
                d_ref[0, rows, lanes] = d_q[h]
                d_ref[1, rows, lanes] = d_k[h]
                d_ref[2, rows, lanes] = d_v[h]
                gates = gates + (jnp.where(lane_ids == h, d_beta[h], 0.0)
                                 + jnp.where(lane_ids == GDN_HEADS + h, d_g[h], 0.0))
            dgate_ref[rows, :] = gates
            return carry

        lax.fori_loop(0, cps, chunk, 0)

    d_spec = pl.BlockSpec((3, cps * CHUNK, width), lambda g: (0, steps - 1 - g, 0))
    return pl.pallas_call(
        body, name="gdn_bwd",
        grid=(steps,),
        in_specs=[rows_blk(width, 0), rows_blk(width, 1), rows_blk(width, 2), rows_blk(LANES), rows_blk(LANES), gate_r,
                  per_chunk(dk, dk), per_chunk(CHUNK, CHUNK), rows_blk(width)],
        out_specs=[d_spec, rows_blk(LANES)],
        out_shape=[jax.ShapeDtypeStruct((3, t_len, width), F32),
                   jax.ShapeDtypeStruct((t_len, LANES), F32)],
        scratch_shapes=[pltpu.VMEM((GDN_HEADS, dk, dk), F32)],
        compiler_params=_params("arbitrary"),
    )(gact, gact, gact, beta_c, gam_c, gam_r, s_all, t_all, d_o)


def _group_matrix(width, group):
    r = lax.broadcasted_iota(jnp.int32, (width, width), 0)
    c = lax.broadcasted_iota(jnp.int32, (width, width), 1)
    return ((r // group) == (c // group)).astype(F32)


def _post_call(o_sb, o_gd, proj, x, target, w_out, sbw, gdw, fw, tm=256):
    t_len, d = x.shape
    half = 512
    zsb_blk = 1536 // half
    zgd_blk = 3584 // half

    def body(osb_ref, ogd_ref, zsb_ref, zgd_ref, x_ref, tg_ref, wo_ref, sbw_ref, gdw_ref, fw_ref,
             dx2_ref, dosb_ref, dogd_ref, dz_ref, loss_ref, gfw_ref, gsb_ref, ggd_ref, gwo_ref):
        step = pl.program_id(0)

        @pl.when(step == 0)
        def _():
            loss_ref[...] = jnp.zeros_like(loss_ref)
            gfw_ref[...] = jnp.zeros_like(gfw_ref)
            gsb_ref[...] = jnp.zeros_like(gsb_ref)
            ggd_ref[...] = jnp.zeros_like(ggd_ref)
            gwo_ref[...] = jnp.zeros_like(gwo_ref)

        def head_forward(o, z, w, gmat, inv):
            r = lax.rsqrt(_running_sum_mm(o * o, gmat) * inv + EPS)
            nrm = o * r * w
            sg = _sigmoid(z)
            return r, nrm, sg, nrm * (z * sg)

        def head_backward(d_m, o, z, w, gmat, inv, r, nrm, sg):
            d_n = d_m * (z * sg)
            d_z = d_m * nrm * (sg * (1.0 + z * (1.0 - sg)))
            dnw = d_n * w
            d_o = r * dnw - o * (r * r * r) * (_running_sum_mm(dnw * o, gmat) * inv)
            return d_o, d_z, jnp.sum(d_n * o * r, axis=0, keepdims=True)

        g_sb = _group_matrix(half, SB_HEAD_DIM).astype(MXU_DTYPE)
        g_gd = _group_matrix(half, GDN_HEAD_DIM).astype(MXU_DTYPE)
        osb, ogd, zsb, zgd = osb_ref[...], ogd_ref[...], zsb_ref[...], zgd_ref[...]
        sbw_v, gdw_v = sbw_ref[...], gdw_ref[...]
        r_sb, n_sb, sg_sb, m_sb = head_forward(osb, zsb, sbw_v, g_sb, 1.0 / SB_HEAD_DIM)
        r_gd, n_gd, sg_gd, m_gd = head_forward(ogd, zgd, gdw_v, g_gd, 1.0 / GDN_HEAD_DIM)
        mixed = jnp.concatenate([m_sb, m_gd], axis=1).astype(MXU_DTYPE)
        wo = wo_ref[...]
        x2 = x_ref[...] + jnp.dot(mixed, wo, preferred_element_type=F32)
        r2 = lax.rsqrt(jnp.mean(x2 * x2, axis=-1, keepdims=True) + EPS)
        fw_v = fw_ref[...]
        err = x2 * r2 * fw_v - tg_ref[...]
        loss_ref[...] += 0.5 * jnp.sum(jnp.sum(err * err, axis=-1, keepdims=True) * (1.0 / d))
        dy = err * (1.0 / d)
        gg = dy * fw_v
        dx2 = r2 * gg - x2 * ((r2 * r2 * r2) * jnp.mean(gg * x2, axis=-1, keepdims=True))
        gfw_ref[...] += jnp.sum(dy * x2 * r2, axis=0, keepdims=True)
        dx2_ref[...] = dx2
        dx2b = dx2.astype(MXU_DTYPE)
        d_mixed = lax.dot_general(dx2b, wo, _NT, preferred_element_type=F32)
        gwo_ref[...] += lax.dot_general(mixed, dx2b, _TN, preferred_element_type=F32)
        d_osb, d_zsb, gsb = head_backward(d_mixed[:, :half], osb, zsb, sbw_v, g_sb, 1.0 / SB_HEAD_DIM, r_sb, n_sb, sg_sb)
        d_ogd, d_zgd, ggd = head_backward(d_mixed[:, half:], ogd, zgd, gdw_v, g_gd, 1.0 / GDN_HEAD_DIM, r_gd, n_gd, sg_gd)
        dosb_ref[...] = d_osb
        dogd_ref[...] = d_ogd
        dz_ref[0] = d_zsb
        dz_ref[1] = d_zgd
        gsb_ref[...] += gsb
        ggd_ref[...] += ggd

    row_blk = lambda w: pl.BlockSpec((tm, w), lambda i: (i, 0))
    fixed = lambda r, w: pl.BlockSpec((r, w), lambda i: (0, 0))
    return pl.pallas_call(
        body, name="post",
        grid=(t_len // tm,),
        in_specs=[row_blk(half), row_blk(half),
                  pl.BlockSpec((tm, half), lambda i: (i, zsb_blk)),
                  pl.BlockSpec((tm, half), lambda i: (i, zgd_blk)),
                  row_blk(d), row_blk(d), fixed(d, d), fixed(1, half), fixed(1, half), fixed(1, d)],
        out_specs=[row_blk(d), row_blk(half), row_blk(half),
                   pl.BlockSpec((2, tm, half), lambda i: (DPROJ_GATE_SLOT // 2, i, 0)),
                   fixed(1, LANES), fixed(1, d), fixed(1, half), fixed(1, half), fixed(d, d)],
        out_shape=[jax.ShapeDtypeStruct((t_len, d), F32)] + [jax.ShapeDtypeStruct((t_len, half), F32)] * 2
                  + [jax.ShapeDtypeStruct((len(DPROJ_PIECE_OF_SLOT), t_len, half), F32),
                     jax.ShapeDtypeStruct((1, LANES), F32), jax.ShapeDtypeStruct((1, d), F32),
                     jax.ShapeDtypeStruct((1, half), F32), jax.ShapeDtypeStruct((1, half), F32),
                     jax.ShapeDtypeStruct((d, d), F32)],
        compiler_params=_params("arbitrary"),
    )(o_sb, o_gd, proj, proj, x, target, w_out, sbw, gdw, fw)


def _piece_of_slot(s):
    return jnp.where(s < DPROJ_GDN_SLOT, s, jnp.where(s < DPROJ_GATE_SLOT, s + 1,
                                                     jnp.where(s == DPROJ_GATE_SLOT, 3, 7)))


def _gw_in_call(h_t, dproj8):
    d, t_len = h_t.shape
    n_piece, _, pw = dproj8.shape

    def body(ht_ref, dp_ref, gw_ref):
        gw_ref[...] = jnp.dot(ht_ref[...], dp_ref[0].astype(MXU_DTYPE), preferred_element_type=F32)

    return pl.pallas_call(
        body, name="gw_in",
        grid=(n_piece,),
        in_specs=[pl.BlockSpec((d, t_len), lambda s: (0, 0)),
                  pl.BlockSpec((1, t_len, pw), lambda s: (s, 0, 0))],
        out_specs=pl.BlockSpec((d, pw), lambda s: (0, _piece_of_slot(s))),
        out_shape=jax.ShapeDtypeStruct((d, n_piece * pw), F32),
        compiler_params=_params("arbitrary"),
    )(h_t, dproj8)


def _gw_small_call(h_t, dsmall, tm=512):
    d, t_len = h_t.shape
    ns = dsmall.shape[1]

    def body(ht_ref, dp_ref, gw_ref):
        @pl.when(pl.program_id(0) == 0)
        def _():
            gw_ref[...] = jnp.zeros_like(gw_ref)

        gw_ref[...] += jnp.dot(ht_ref[...], dp_ref[...].astype(MXU_DTYPE), preferred_element_type=F32)

    return pl.pallas_call(
        body, name="gw_small",
        grid=(t_len // tm,),
        in_specs=[pl.BlockSpec((d, tm), lambda t: (0, t)),
                  pl.BlockSpec((tm, ns), lambda t: (t, 0))],
        out_specs=pl.BlockSpec((d, ns), lambda t: (0, 0)),
        out_shape=jax.ShapeDtypeStruct((d, ns), F32),
        compiler_params=_params("arbitrary"),
    )(h_t, dsmall)


def _dx_call(dproj8, dsmall, w_main, w_small, x, r, dx2, norm_w, tm=256):
    t_len, d = x.shape
    n_piece, _, pw = dproj8.shape
    ns = dsmall.shape[1]

    def body(dp_ref, ds_ref, wm_ref, ws_ref, x_ref, r_ref, dx2_ref, nw_ref, gx_ref, gnw_ref):
        @pl.when(pl.program_id(0) == 0)
        def _():
            gnw_ref[...] = jnp.zeros_like(gnw_ref)

        dh = lax.dot_general(ds_ref[...].astype(MXU_DTYPE), ws_ref[...], _NT, preferred_element_type=F32)
        for s, p in enumerate(DPROJ_PIECE_OF_SLOT):
            dh = dh + lax.dot_general(dp_ref[s].astype(MXU_DTYPE), wm_ref[:, p * pw:(p + 1) * pw], _NT,
                                      preferred_element_type=F32)
        xv, rv = x_ref[...], r_ref[...]
        dn = dh * nw_ref[...]
        gx_ref[...] = dx2_ref[...] + rv * dn - xv * ((rv * rv * rv) * jnp.mean(dn * xv, axis=-1, keepdims=True))
        gnw_ref[...] += jnp.sum(dh * xv * rv, axis=0, keepdims=True)

    return pl.pallas_call(
        body, name="dx",
        grid=(t_len // tm,),
        in_specs=[pl.BlockSpec((n_piece, tm, pw), lambda i: (0, i, 0)),
                  pl.BlockSpec((tm, ns), lambda i: (i, 0)),
                  pl.BlockSpec((d, n_piece * pw), lambda i: (0, 0)),
                  pl.BlockSpec((d, ns), lambda i: (0, 0)),
                  pl.BlockSpec((tm, d), lambda i: (i, 0)),
                  pl.BlockSpec((tm, 1), lambda i: (i, 0)),
                  pl.BlockSpec((tm, d), lambda i: (i, 0)),
                  pl.BlockSpec((1, d), lambda i: (0, 0))],
        out_specs=[pl.BlockSpec((tm, d), lambda i: (i, 0)),
                   pl.BlockSpec((1, d), lambda i: (0, 0))],
        out_shape=[jax.ShapeDtypeStruct((t_len, d), F32), jax.ShapeDtypeStruct((1, d), F32)],
        compiler_params=_params("arbitrary"),
    )(dproj8, dsmall, w_main, w_small, x, r, dx2, norm_w)


def _exchange_call(name, srcs, per_peer):
    n = len(srcs)
    out_shapes = [jax.ShapeDtypeStruct(s.shape if pp else (N_DEV,) + s.shape, s.dtype) for s, pp in zip(srcs, per_peer)]

    def body(*refs):
        src_refs, out_refs = refs[:n], refs[n:2 * n]
        send_sems, recv_sems, local_sems = refs[2 * n:]
        x, y, c = lax.axis_index("x"), lax.axis_index("y"), lax.axis_index("c")
        me = 4 * x + 2 * y + c
        copies = []
        for a in range(n):
            mine = src_refs[a].at[me] if per_peer[a] else src_refs[a]
            local = pltpu.make_async_copy(mine, out_refs[a].at[me], local_sems.at[a])
            local.start()
            copies.append(local)
        remote = []
        for k in range(1, N_DEV):
            kx, ky, kc = (k >> 2) & 1, (k >> 1) & 1, k & 1
            px = 1 - x if kx else x
            py = 1 - y if ky else y
            pc = 1 - c if kc else c
            peer = 4 * px + 2 * py + pc
            for a in range(n):
                sem = a * (N_DEV - 1) + (k - 1)
                src = src_refs[a].at[peer] if per_peer[a] else src_refs[a]
                cp = pltpu.make_async_remote_copy(
                    src_ref=src, dst_ref=out_refs[a].at[me],
                    send_sem=send_sems.at[sem], recv_sem=recv_sems.at[sem],
                    device_id=(px, py, pc), device_id_type=pl.DeviceIdType.MESH)
                cp.start()
                remote.append(cp)
        for cp in remote:
            cp.wait_send()
        for cp in remote:
            cp.wait_recv()
        for cp in copies:
            cp.wait()

    hbm = pl.BlockSpec(memory_space=pl.ANY)
    return pl.pallas_call(
        body, name=name,
        in_specs=[hbm] * n, out_specs=[hbm] * n, out_shape=out_shapes,
        scratch_shapes=[pltpu.SemaphoreType.DMA((n * (N_DEV - 1),)),
                        pltpu.SemaphoreType.DMA((n * (N_DEV - 1),)),
                        pltpu.SemaphoreType.DMA((n,))],
    )(*srcs)


N_CHIPS = 4
_HBM = pl.BlockSpec(memory_space=pl.ANY)
_MESH = pl.DeviceIdType.MESH


def _gather_call(name, srcs):
    n = len(srcs)
    per = N_DEV - 1

    def body(*refs):
        src_refs, out_refs = refs[:n], refs[n:2 * n]
        send_sems, recv_sems, local_sems = refs[2 * n:]
        x, y, c = lax.axis_index("x"), lax.axis_index("y"), lax.axis_index("c")
        me, sibling = (x, y, c), (x, y, 1 - c)
        chips = [(1 - x, y), (x, 1 - y), (1 - x, 1 - y)]
        slot = lambda px, py, pc: 4 * px + 2 * py + pc

        def copy(a, k, block, to, from_src=False):
            rows = out_refs[a].at[slot(*block)]
            return pltpu.make_async_remote_copy(
                src_ref=src_refs[a] if from_src else rows, dst_ref=rows,
                send_sem=send_sems.at[a * per + k], recv_sem=recv_sems.at[a * per + k],
                device_id=to, device_id_type=_MESH)

        local = [pltpu.make_async_copy(src_refs[a], out_refs[a].at[slot(*me)], local_sems.at[a]) for a in range(n)]
        for cp in local:
            cp.start()
        first = []
        for a in range(n):
            first.append(copy(a, 0, me, sibling, True))
            first += [copy(a, 1 + j, me, (*chip, c), True) for j, chip in enumerate(chips)]
        for cp in first:
            cp.start()
        passed = []
        for j, chip in enumerate(chips):
            for a in range(n):
                copy(a, 1 + j, (*chip, c), me).wait_recv()
                fwd = copy(a, 4 + j, (*chip, c), sibling)
                fwd.start()
                passed.append(fwd)
        for a in range(n):
            copy(a, 0, sibling, me).wait_recv()
            for j, chip in enumerate(chips):
                copy(a, 4 + j, (*chip, 1 - c), me).wait_recv()
        for cp in first + passed:
            cp.wait_send()
        for cp in local:
            cp.wait()

    return pl.pallas_call(
        body, name=name,
        in_specs=[_HBM] * n, out_specs=[_HBM] * n,
        out_shape=[jax.ShapeDtypeStruct((N_DEV,) + s.shape, s.dtype) for s in srcs],
        scratch_shapes=[pltpu.SemaphoreType.DMA((n * per,)), pltpu.SemaphoreType.DMA((n * per,)),
                        pltpu.SemaphoreType.DMA((n,))],
    )(*srcs)


def _sibling_send_call(name, srcs):
    n = len(srcs)

    def body(*refs):
        src_refs, out_refs = refs[:n], refs[n:2 * n]
        send_sems, recv_sems = refs[2 * n:]
        x, y, c = lax.axis_index("x"), lax.axis_index("y"), lax.axis_index("c")
        copies = []
        for a in range(n):
            for ch in range(N_CHIPS):
                copies.append(pltpu.make_async_remote_copy(
                    src_ref=src_refs[a].at[2 * ch + (1 - c)], dst_ref=out_refs[a].at[ch],
                    send_sem=send_sems.at[a * N_CHIPS + ch], recv_sem=recv_sems.at[a * N_CHIPS + ch],
                    device_id=(x, y, 1 - c), device_id_type=_MESH))
        for cp in copies:
            cp.start()
        for cp in copies:
            cp.wait_send()
        for cp in copies:
            cp.wait_recv()

    return pl.pallas_call(
        body, name=name,
        in_specs=[_HBM] * n, out_specs=[_HBM] * n,
        out_shape=[jax.ShapeDtypeStruct((N_CHIPS,) + s.shape[1:], s.dtype) for s in srcs],
        scratch_shapes=[pltpu.SemaphoreType.DMA((n * N_CHIPS,)), pltpu.SemaphoreType.DMA((n * N_CHIPS,))],
    )(*srcs)


def _pair_sum_call(name, parts, from_sibling, tr):
    _, rows, cols = parts.shape

    def body(p_ref, s_ref, o_ref):
        o_ref[...] = (p_ref[...] + s_ref[...]).astype(o_ref.dtype)

    return pl.pallas_call(
        body, name=name,
        grid=(N_CHIPS, rows // tr),
        in_specs=[pl.BlockSpec((1, tr, cols), lambda ch, i: (2 * ch + lax.axis_index("c"), i, 0)),
                  pl.BlockSpec((1, tr, cols), lambda ch, i: (ch, i, 0))],
        out_specs=pl.BlockSpec((1, tr, cols), lambda ch, i: (ch, i, 0)),
        out_shape=jax.ShapeDtypeStruct((N_CHIPS, rows, cols), WIRE_DTYPE),
        compiler_params=_params("arbitrary", "arbitrary"),
    )(parts, from_sibling)


def _chip_exchange_call(name, srcs):
    n = len(srcs)
    per = N_CHIPS - 1

    def body(*refs):
        src_refs, out_refs = refs[:n], refs[n:2 * n]
        send_sems, recv_sems, local_sems = refs[2 * n:]
        x, y, c = lax.axis_index("x"), lax.axis_index("y"), lax.axis_index("c")
        mine = 2 * x + y
        chips = [(1 - x, y), (x, 1 - y), (1 - x, 1 - y)]
        local = [pltpu.make_async_copy(src_refs[a].at[mine], out_refs[a].at[mine], local_sems.at[a]) for a in range(n)]
        for cp in local:
            cp.start()
        remote = []
        for a in range(n):
            for j, (px, py) in enumerate(chips):
                remote.append(pltpu.make_async_remote_copy(
                    src_ref=src_refs[a].at[2 * px + py], dst_ref=out_refs[a].at[mine],
                    send_sem=send_sems.at[a * per + j], recv_sem=recv_sems.at[a * per + j],
                    device_id=(px, py, c), device_id_type=_MESH))
        for cp in remote:
            cp.start()
        for cp in remote:
            cp.wait_send()
        for cp in remote:
            cp.wait_recv()
        for cp in local:
            cp.wait()

    return pl.pallas_call(
        body, name=name,
        in_specs=[_HBM] * n, out_specs=[_HBM] * n,
        out_shape=[jax.ShapeDtypeStruct(s.shape, s.dtype) for s in srcs],
        scratch_shapes=[pltpu.SemaphoreType.DMA((n * per,)), pltpu.SemaphoreType.DMA((n * per,)),
                        pltpu.SemaphoreType.DMA((n,))],
    )(*srcs)


def _adam_call(name, parts, w, m, v, tr):
    rows, cols = w.shape
    n_slots = parts.shape[0]

    def body(p_ref, w_ref, m_ref, v_ref, g_ref, d_ref, nm_ref, nv_ref):
        g = p_ref[0].astype(F32)
        for s in range(1, n_slots):
            g = g + p_ref[s].astype(F32)
        m_new = ADAM_B1 * m_ref[...] + (1.0 - ADAM_B1) * g
        v_new = ADAM_B2 * v_ref[...] + (1.0 - ADAM_B2) * (g * g)
        m_hat = m_new / (1.0 - ADAM_B1 ** ADAM_STEP)
        v_hat = v_new / (1.0 - ADAM_B2 ** ADAM_STEP)
        g_ref[...] = g
        d_ref[...] = -ADAM_LR * (m_hat / (jnp.sqrt(v_hat) + ADAM_EPS) + ADAM_WD * w_ref[...])
        nm_ref[...] = m_new
        nv_ref[...] = v_new

    blk = pl.BlockSpec((tr, cols), lambda i: (i, 0))
    return pl.pallas_call(
        body, name=name,
        grid=(rows // tr,),
        in_specs=[pl.BlockSpec((n_slots, tr, cols), lambda i: (0, i, 0)), blk, blk, blk],
        out_specs=[blk] * 4,
        out_shape=[jax.ShapeDtypeStruct((rows, cols), F32)] * 4,
        compiler_params=_params("arbitrary"),
    )(parts, w, m, v)


def _adamw(g, w, m, v):
    m_new = ADAM_B1 * m + (1.0 - ADAM_B1) * g
    v_new = ADAM_B2 * v + (1.0 - ADAM_B2) * (g * g)
    m_hat = m_new / (1.0 - ADAM_B1 ** ADAM_STEP)
    v_hat = v_new / (1.0 - ADAM_B2 ** ADAM_STEP)
    return -ADAM_LR * (m_hat / (jnp.sqrt(v_hat) + ADAM_EPS) + ADAM_WD * w), m_new, v_new


def _adam_small_call(parts, ws, ms, vs):
    n = len(ws)
    n_slots = parts.shape[0]

    def body(*refs):
        p_ref = refs[0]
        w_refs, m_refs, v_refs = refs[1:1 + n], refs[1 + n:1 + 2 * n], refs[1 + 2 * n:1 + 3 * n]
        loss_ref = refs[1 + 3 * n]
        outs = refs[2 + 3 * n:]
        g_all = p_ref[0]
        for s in range(1, n_slots):
            g_all = g_all + p_ref[s]
        loss_ref[...] = g_all[n:n + 1, 0:1]
        for r in range(n):
            size = w_refs[r].shape[1]
            g = g_all[r:r + 1, :size]
            delta, m_new, v_new = _adamw(g, w_refs[r][...], m_refs[r][...], v_refs[r][...])
            for kind, val in enumerate((g, delta, m_new, v_new)):
                outs[kind * n + r][...] = val

    vm = pl.BlockSpec(memory_space=pltpu.VMEM)
    shapes = [jax.ShapeDtypeStruct(w.shape, F32) for w in ws]
    return pl.pallas_call(
        body, name="adam_small",
        in_specs=[vm] * (1 + 3 * n), out_specs=[vm] * (1 + 4 * n),
        out_shape=[jax.ShapeDtypeStruct((1, 1), F32)] + shapes * 4,
    )(parts, *ws, *ms, *vs)


_SMALL_ROWS = ("norm1_w", "final_norm_w", "sb_norm_w", "gdn_norm_w", "gdn_A_log", "gdn_dt_bias", "loss")


def _pack_small(vals, width):
    rows = [jnp.pad(a.reshape(1, -1).astype(F32), ((0, 0), (0, width - a.size))) for a in vals]
    rows += [jnp.zeros((1, width), F32)] * (8 - len(rows))
    return jnp.concatenate(rows, axis=0)


def _device_step(x2d, tgt, w_full, w_out_f32, conv_full, norm1_w, sb_norm_w, gdn_A_log, gdn_dt_bias, gdn_norm_w,
                 final_norm_w):
    t_len, d = x2d.shape
    n_chunks = t_len // CHUNK
    n_main = 8 * 512
    n_small = w_full.shape[1] - n_main
    w_main = w_full[:, :n_main].astype(MXU_DTYPE)
    w_small = jnp.pad(w_full[:, n_main:], ((0, 0), (0, LANES - n_small))).astype(MXU_DTYPE)
    w_small_t = w_full[:, n_main:].T.astype(MXU_DTYPE)
    w_out_full = w_out_f32.astype(MXU_DTYPE)

    pad_lanes = lambda a, lo: jnp.pad(a.reshape(1, -1), ((0, 0), (lo, LANES - lo - a.size)))
    alog_l, dtb_l = pad_lanes(gdn_A_log, GDN_HEADS), pad_lanes(gdn_dt_bias, GDN_HEADS)
    alog_c, dtb_c = alog_l[:, :8].T, dtb_l[:, :8].T
    sbw = jnp.tile(sb_norm_w, (1, 512 // SB_HEAD_DIM))
    gdw = jnp.tile(gdn_norm_w, (1, 512 // GDN_HEAD_DIM))
    fw = final_norm_w.reshape(1, d)

    proj, ps, pst, h_t, r1 = _inproj_call(x2d, norm1_w, w_main, w_small, w_small_t)
    o_sb, sp_total, sb_blocks_run = _sb_fwd_call(proj, t_len)
    gact = _gdn_prep_call(proj, conv_full, t_len)
    beta_l, gcol_l, grow = _gdn_gates_call(ps, pst, alog_l, dtb_l, alog_c, dtb_c, t_len)
    gam_r = grow[GDN_HEADS:2 * GDN_HEADS].reshape(GDN_HEADS, n_chunks, 1, CHUNK)
    o_gd, s_all, t_all = _gdn_fwd_call(gact, beta_l, gcol_l, gam_r, t_len)

    (dx2, d_osb, d_ogd, dproj8, loss_p, g_fw, g_sbw, g_gdw, g_wout) = _post_call(
        o_sb, o_gd, proj, x2d, tgt, w_out_full, sbw, gdw, fw)

    dproj8 = _sb_bwd_call(proj, sp_total, sb_blocks_run, d_osb, dproj8, t_len)
    d_gact3, d_gates = _gdn_bwd_call(gact, beta_l, gcol_l, gam_r, s_all, t_all, d_ogd, t_len)
    dproj8, g_conv = _gdn_prep_bwd_call(proj, conv_full, d_gact3, dproj8, t_len)
    dsmall, g_alog, g_dtb = _gdn_gates_bwd_call(ps, alog_l, dtb_l, d_gates, t_len)

    g_w_main = _gw_in_call(h_t, dproj8)
    g_w_small = _gw_small_call(h_t, dsmall)
    grad_x, g_n1 = _dx_call(dproj8, dsmall, w_main, w_small, x2d, r1, dx2, norm1_w)
    g_w_in_full = jnp.concatenate([g_w_main, g_w_small[:, :n_small]], axis=1)
    return (loss_p, grad_x, g_n1, g_w_in_full, g_sbw, g_conv, g_alog, g_dtb, g_gdw, g_wout, g_fw)


def kernel(x, norm1_w, w_in, sb_norm_w, gdn_conv_w, gdn_A_log, gdn_dt_bias, gdn_norm_w, w_out, final_norm_w, loss_target, m_norm1_w, m_w_in, m_sb_norm_w, m_gdn_conv_w, m_gdn_A_log, m_gdn_dt_bias, m_gdn_norm_w, m_w_out, m_final_norm_w, v_norm1_w, v_w_in, v_sb_norm_w, v_gdn_conv_w, v_gdn_A_log, v_gdn_dt_bias, v_gdn_norm_w, v_w_out, v_final_norm_w):
    d = x.shape[2]
    shard_cols = w_in.shape[2]
    conv_cols = gdn_conv_w.shape[2]

    w_in_g, w_out_g, conv_g = _gather_call(
        "gather_weights", [w_in[0].astype(WIRE_DTYPE), w_out[0].astype(WIRE_DTYPE), gdn_conv_w[0]])
    w_full = w_in_g.transpose(1, 0, 2).reshape(d, N_DEV * shard_cols)
    conv_full = conv_g.transpose(1, 0, 2).reshape(CONV_WIDTH, N_DEV * conv_cols)

    (loss_p, grad_x, g_n1, g_w_in_full, g_sbw, g_conv, g_alog, g_dtb, g_gdw, g_wout, g_fw) = _device_step(
        x[0], loss_target[0], w_full, w_out_g.reshape(d, d), conv_full, norm1_w, sb_norm_w, gdn_A_log, gdn_dt_bias,
        gdn_norm_w, final_norm_w)

    g_w_in_parts = g_w_in_full.reshape(d, N_DEV, shard_cols).transpose(1, 0, 2)
    g_wout_parts = g_wout.reshape(N_DEV, d // N_DEV, d)
    g_conv_parts = g_conv.reshape(CONV_WIDTH, N_DEV, conv_cols).transpose(1, 0, 2)
    fold = lambda a, group: a.reshape(-1, group).sum(axis=0)
    small_g = _pack_small([g_n1, g_fw, fold(g_sbw, SB_HEAD_DIM), fold(g_gdw, GDN_HEAD_DIM),
                           g_alog[0, GDN_HEADS:2 * GDN_HEADS], g_dtb[0, GDN_HEADS:2 * GDN_HEADS],
                           loss_p[0, :1]], d)
    sib_w_in, sib_wout, sib_conv = _sibling_send_call("grads_to_sibling", [g_w_in_parts, g_wout_parts, g_conv_parts])
    c_w_in = _pair_sum_call("pair_sum_w_in", g_w_in_parts, sib_w_in, 256)
    c_wout = _pair_sum_call("pair_sum_w_out", g_wout_parts, sib_wout, d // N_DEV)
    c_conv = _pair_sum_call("pair_sum_conv", g_conv_parts, sib_conv, CONV_WIDTH)
    p_w_in, p_wout, p_conv = _chip_exchange_call("grads_to_chips", [c_w_in, c_wout, c_conv])
    (p_small,) = _exchange_call("exchange_small", [small_g], [False])

    r_w_in = _adam_call("adam_w_in", p_w_in, w_in[0], m_w_in[0], v_w_in[0], 256)
    r_wout = _adam_call("adam_w_out", p_wout, w_out[0], m_w_out[0], v_w_out[0], d // N_DEV)
    r_conv = _adam_call("adam_conv", p_conv, gdn_conv_w[0], m_gdn_conv_w[0], v_gdn_conv_w[0], CONV_WIDTH)

    row = lambda a: a.reshape(1, -1)
    n_small = len(_SMALL_ROWS) - 1
    r_small = _adam_small_call(
        p_small,
        [norm1_w, row(final_norm_w), sb_norm_w, gdn_norm_w, gdn_A_log, gdn_dt_bias],
        [m_norm1_w, row(m_final_norm_w), m_sb_norm_w, m_gdn_norm_w, m_gdn_A_log, m_gdn_dt_bias],
        [v_norm1_w, row(v_final_norm_w), v_sb_norm_w, v_gdn_norm_w, v_gdn_A_log, v_gdn_dt_bias])

    def small_out(kind, name):
        out = r_small[1 + kind * n_small + _SMALL_ROWS.index(name)]
        return out.reshape(final_norm_w.shape) if name == "final_norm_w" else out

    def outputs(kind):
        return (small_out(kind, "norm1_w"), r_w_in[kind][None], small_out(kind, "sb_norm_w"), r_conv[kind][None],
                small_out(kind, "gdn_A_log"), small_out(kind, "gdn_dt_bias"), small_out(kind, "gdn_norm_w"),
                r_wout[kind][None], small_out(kind, "final_norm_w"))

    return (r_small[0][0, 0], grad_x[None], *outputs(0), *outputs(1), *outputs(2), *outputs(3))
```

```python
import functools

import jax
import jax.numpy as jnp
from jax import lax
from jax.experimental import pallas as pl
from jax.experimental.pallas import tpu as pltpu

F32 = jnp.float32
MXU_DTYPE = jnp.bfloat16
WIRE_DTYPE = jnp.bfloat16
EXACT = lax.Precision.HIGHEST
EPS = 1e-6
N_DEV = 8
SB_HEAD_DIM = 64
GDN_HEAD_DIM = 128
GDN_HEADS = 4
GDN_CHUNKS_PER_STEP = 4
CHUNK = 64
CONV_WIDTH = 4
LANES = 128
SB_BLOCK = 128
SB_BQ = 256
VMEM_LIMIT_BYTES = 56 * 1024 * 1024

DPROJ_PIECE_OF_SLOT = (0, 1, 2, 4, 5, 6, 3, 7)
DPROJ_SB_SLOT, DPROJ_GDN_SLOT, DPROJ_GATE_SLOT = 0, 3, 6

ADAM_LR = 0.001
ADAM_B1 = 0.9
ADAM_B2 = 0.999
ADAM_EPS = 1e-08
ADAM_WD = 0.01
ADAM_STEP = 10

_NN = (((1,), (0,)), ((), ()))
_NT = (((1,), (1,)), ((), ()))
_TN = (((0,), (0,)), ((), ()))
_BNN = (((2,), (1,)), ((0,), (0,)))
_BNT = (((2,), (2,)), ((0,), (0,)))
_BTN = (((1,), (1,)), ((0,), (0,)))


def _mm(a, b):
    return jnp.dot(a.astype(MXU_DTYPE), b.astype(MXU_DTYPE), preferred_element_type=F32)


def _mm_nt(a, b):
    return lax.dot_general(a.astype(MXU_DTYPE), b.astype(MXU_DTYPE), _NT, preferred_element_type=F32)


def _mm_tn(a, b):
    return lax.dot_general(a.astype(MXU_DTYPE), b.astype(MXU_DTYPE), _TN, preferred_element_type=F32)


def _mx(a, b):
    return jnp.dot(a, b, precision=EXACT, preferred_element_type=F32)


def _mx_nt(a, b):
    return lax.dot_general(a, b, _NT, precision=EXACT, preferred_element_type=F32)


def _mx_tn(a, b):
    return lax.dot_general(a, b, _TN, precision=EXACT, preferred_element_type=F32)


def _split(x):
    hi = x.astype(MXU_DTYPE)
    return hi, (x - hi.astype(F32)).astype(MXU_DTYPE)


def _m3_general(a, b, dims):
    ah, al = _split(a)
    bh, bl = _split(b)
    dot = lambda x, y: lax.dot_general(x, y, dims, preferred_element_type=F32)
    (contract, _), (batch, _) = dims
    free = [ax for ax in range(a.ndim) if ax not in contract and ax not in batch][0]
    m = a.shape[free]
    both = dot(jnp.concatenate([ah, al], axis=free), bh)
    out_axis = len(batch)
    hi_part = lax.slice_in_dim(both, 0, m, axis=out_axis)
    lo_part = lax.slice_in_dim(both, m, 2 * m, axis=out_axis)
    return hi_part + (dot(ah, bl) + lo_part)


def _m3(a, b):
    return _m3_general(a, b, _NN)


def _m3_nt(a, b):
    return _m3_general(a, b, _NT)


def _m3_tn(a, b):
    return _m3_general(a, b, _TN)


def _sigmoid(z):
    return 1.0 / (1.0 + jnp.exp(-z))


def _softplus(z):
    return jnp.maximum(z, 0.0) + jnp.log(1.0 + jnp.exp(-jnp.abs(z)))


def _params(*semantics):
    return pltpu.CompilerParams(dimension_semantics=semantics, vmem_limit_bytes=VMEM_LIMIT_BYTES)


def _inproj_call(x, norm_w, w_main, w_small, w_small_t, tm=256):
    t_len, d = x.shape
    n = w_main.shape[1]
    ns = w_small.shape[1]
    nst = w_small_t.shape[0]

    def body(x_ref, nw_ref, wm_ref, ws_ref, wst_ref, pm_ref, ps_ref, pst_ref, ht_ref, r_ref):
        xv = x_ref[...]
        r = lax.rsqrt(jnp.mean(xv * xv, axis=-1, keepdims=True) + EPS)
        h = xv * r * nw_ref[...]
        hb = h.astype(MXU_DTYPE)
        for n0 in range(0, n, 512):
            pm_ref[:, n0:n0 + 512] = jnp.dot(hb, wm_ref[:, n0:n0 + 512], preferred_element_type=F32)
        ps_ref[...] = jnp.dot(hb, ws_ref[...], preferred_element_type=F32)
        pst_ref[...] = lax.dot_general(wst_ref[...], hb, _NT, preferred_element_type=F32)
        ht_ref[...] = h.T.astype(MXU_DTYPE)
        r_ref[...] = r

    return pl.pallas_call(
        body, name="inproj",
        grid=(t_len // tm,),
        in_specs=[pl.BlockSpec((tm, d), lambda i: (i, 0)),
                  pl.BlockSpec((1, d), lambda i: (0, 0)),
                  pl.BlockSpec((d, n), lambda i: (0, 0)),
                  pl.BlockSpec((d, ns), lambda i: (0, 0)),
                  pl.BlockSpec((nst, d), lambda i: (0, 0))],
        out_specs=[pl.BlockSpec((tm, n), lambda i: (i, 0)),
                   pl.BlockSpec((tm, ns), lambda i: (i, 0)),
                   pl.BlockSpec((nst, tm), lambda i: (0, i)),
                   pl.BlockSpec((d, tm), lambda i: (0, i)),
                   pl.BlockSpec((tm, 1), lambda i: (i, 0))],
        out_shape=[jax.ShapeDtypeStruct((t_len, n), F32),
                   jax.ShapeDtypeStruct((t_len, ns), F32),
                   jax.ShapeDtypeStruct((nst, t_len), F32),
                   jax.ShapeDtypeStruct((d, t_len), MXU_DTYPE),
                   jax.ShapeDtypeStruct((t_len, 1), F32)],
        compiler_params=_params("arbitrary"),
    )(x, norm_w, w_main, w_small, w_small_t)


def _running_sum_mm(x, tri):
    hi = x.astype(MXU_DTYPE)
    lo = (x - hi.astype(F32)).astype(MXU_DTYPE)
    return jnp.dot(hi, tri, preferred_element_type=F32) + jnp.dot(lo, tri, preferred_element_type=F32)


def _sb_iotas():
    row_i = lax.broadcasted_iota(jnp.int32, (SB_BQ, SB_BLOCK), 0)
    col_i = lax.broadcasted_iota(jnp.int32, (SB_BQ, SB_BLOCK), 1)
    sq_r = lax.broadcasted_iota(jnp.int32, (SB_BLOCK, SB_BLOCK), 0)
    sq_c = lax.broadcasted_iota(jnp.int32, (SB_BLOCK, SB_BLOCK), 1)
    return row_i, col_i, sq_r, sq_c


SB_DIAG_BLOCKS = SB_BQ // SB_BLOCK
SB_EXP_FLOOR = -110.0


def _sb_keys_descending(qi, tile, carry, z_bounds, n_heads):
    n_free = SB_DIAG_BLOCKS * qi
    for j in range(SB_DIAG_BLOCKS - 1, -1, -1):
        carry = tile(n_free + j, True, carry)

    def largest_exponent(c):
        worst = jnp.max(z_bounds[0] - c[1])
        for h in range(1, n_heads):
            worst = jnp.maximum(worst, jnp.max(z_bounds[h] - c[1 + h]))
        return worst

    def cond(state):
        return (state[0] < n_free) & (state[1] > SB_EXP_FLOOR)

    def body(state):
        c = tile(n_free - 1 - state[0], False, state[2:])
        return (state[0] + 1, largest_exponent(c), *c)

    out = lax.while_loop(cond, body, (jnp.int32(0), largest_exponent(carry), *carry))
    return out[2:], out[0]


def _sb_keys_ascending(qi, n_run, tile, carry):
    n_free = SB_DIAG_BLOCKS * qi
    carry = lax.fori_loop(0, n_run, lambda s, c: tile(n_free - n_run + s, False, c), carry)
    for j in range(SB_DIAG_BLOCKS):
        carry = tile(n_free + j, True, carry)
    return carry


def _sb_fwd_call(proj, t_len):
    nq = t_len // SB_BQ
    scale = float(SB_HEAD_DIM) ** -0.5
    n_pairs = 512 // LANES
    per_pair = LANES // SB_HEAD_DIM

    def body(q_ref, k_ref, v_ref, o_ref, st_ref, nrun_ref):
        lane = lax.broadcasted_iota(jnp.int32, (1, LANES), 1)
        row_i, col_i, sq_r, sq_c = _sb_iotas()
        ge = (sq_r >= sq_c).astype(MXU_DTYPE)
        hms = [((lane // SB_HEAD_DIM) == hh).astype(F32) for hh in range(per_pair)]
        k_sq = k_ref[...] * k_ref[...]
        k_norms = [jnp.sqrt(jnp.max(jnp.sum(k_sq * hm, axis=-1, keepdims=True))) * (1.02 * scale) for hm in hms]

        def q_loop(qi, carry):
            r0 = pl.multiple_of(qi * SB_BQ, SB_BQ)
            rows = pl.ds(r0, SB_BQ)
            q_all = q_ref[rows, :]
            qms = [(q_all * (hm * scale)).astype(MXU_DTYPE) for hm in hms]
            z_bounds = [jnp.sqrt(jnp.sum(q_all * q_all * hm, axis=-1, keepdims=True)) * kn
                        for hm, kn in zip(hms, k_norms)]

            def tile(kj, masked, kc):
                acc, cs = kc[0], list(kc[1:])
                heads = range(per_pair)
                s0 = pl.multiple_of(kj * SB_BLOCK, SB_BLOCK)
                cols = pl.ds(s0, SB_BLOCK)
                kb = k_ref[cols, :].astype(MXU_DTYPE)
                v_all = v_ref[cols, :]
                vms = [(v_all * hms[h]).astype(MXU_DTYPE) for h in heads]
                zs = [lax.dot_general(qms[h], kb, _NT, preferred_element_type=F32) for h in heads]
                sps = [_softplus(z) for z in zs]
                if masked:
                    mask = (col_i + s0) < (row_i + r0)
                    sps = [jnp.where(mask, sp, 0.0) for sp in sps]
                sums = [_running_sum_mm(sp, ge) for sp in sps]
                ws = [jnp.exp(zs[h] - (sums[h] + cs[h])) for h in heads]
                if masked:
                    ws = [jnp.where(mask, w, 0.0) for w in ws]
                for h in heads:
                    acc = acc + jnp.dot(ws[h].astype(MXU_DTYPE), vms[h], preferred_element_type=F32)
                cs = [cs[h] + jnp.sum(sps[h], axis=-1, keepdims=True) for h in heads]
                return (acc, *cs)

            zero_col = jnp.zeros((SB_BQ, 1), F32)
            out, n_run = _sb_keys_descending(
                qi, tile, (jnp.zeros((SB_BQ, LANES), F32),) + (zero_col,) * per_pair, z_bounds, per_pair)
            o_ref[rows, :] = out[0]
            for hh in range(per_pair):
                st_ref[hh, rows, :] = out[1 + hh]
            nrun_ref[pl.program_id(0), qi] = n_run
            return carry

        lax.fori_loop(0, nq, q_loop, 0)

    return pl.pallas_call(
        body, name="sb_fwd",
        grid=(n_pairs,),
        in_specs=[pl.BlockSpec((t_len, LANES), lambda p: (0, p)),
                  pl.BlockSpec((t_len, LANES), lambda p: (0, n_pairs + p)),
                  pl.BlockSpec((t_len, LANES), lambda p: (0, 2 * n_pairs + p))],
        out_specs=[pl.BlockSpec((t_len, LANES), lambda p: (0, p)),
                   pl.BlockSpec((per_pair, t_len, 1), lambda p: (p, 0, 0)),
                   pl.BlockSpec(memory_space=pltpu.SMEM)],
        out_shape=[jax.ShapeDtypeStruct((t_len, 512), F32),
                   jax.ShapeDtypeStruct((n_pairs * per_pair, t_len, 1), F32),
                   jax.ShapeDtypeStruct((n_pairs, nq), jnp.int32)],
        compiler_params=_params("arbitrary"),
    )(proj, proj, proj)


def _sb_bwd_call(proj, sp_total, n_run_all, d_o, dproj, t_len):
    nq = t_len // SB_BQ
    scale = float(SB_HEAD_DIM) ** -0.5
    n_pairs = 512 // LANES
    per_pair = LANES // SB_HEAD_DIM

    def body(q_ref, k_ref, v_ref, st_ref, nrun_ref, do_ref, dproj_in_ref, d_ref):
        lane = lax.broadcasted_iota(jnp.int32, (1, LANES), 1)
        row_i, col_i, sq_r, sq_c = _sb_iotas()
        lt = (sq_r < sq_c).astype(MXU_DTYPE)
        le = (sq_r <= sq_c).astype(MXU_DTYPE)
        hms = [((lane // SB_HEAD_DIM) == hh).astype(F32) for hh in range(per_pair)]
        d_ref[1] = jnp.zeros((t_len, LANES), F32)
        d_ref[2] = jnp.zeros((t_len, LANES), F32)

        def q_loop(qi, carry):
            r0 = pl.multiple_of(qi * SB_BQ, SB_BQ)
            rows = pl.ds(r0, SB_BQ)
            q_all, do_all = q_ref[rows, :], do_ref[rows, :]
            qms = [(q_all * (hm * scale)).astype(MXU_DTYPE) for hm in hms]
            doms = [(do_all * hm).astype(MXU_DTYPE) for hm in hms]
            totals = [st_ref[hh, rows, :] for hh in range(per_pair)]

            def tile(kj, masked, kc):
                dq, cls, gls = kc[0], list(kc[1:1 + per_pair]), list(kc[1 + per_pair:])
                heads = range(per_pair)
                s0 = pl.multiple_of(kj * SB_BLOCK, SB_BLOCK)
                cols = pl.ds(s0, SB_BLOCK)
                k_all, v_all = k_ref[cols, :], v_ref[cols, :]
                kb = k_all.astype(MXU_DTYPE)
                vms = [(v_all * hms[h]).astype(MXU_DTYPE) for h in heads]
                kms = [(k_all * (hms[h] * scale)).astype(MXU_DTYPE) for h in heads]
                zs = [lax.dot_general(qms[h], kb, _NT, preferred_element_type=F32) for h in heads]
                das = [lax.dot_general(doms[h], vms[h], _NT, preferred_element_type=F32) for h in heads]
                sp_alls = [_softplus(z) for z in zs]
                sps = sp_alls
                if masked:
                    mask = (col_i + s0) < (row_i + r0)
                    sps = [jnp.where(mask, sp, 0.0) for sp in sp_alls]
                lefts = [_running_sum_mm(sp, lt) for sp in sps]
                ws = [jnp.exp(zs[h] - (totals[h] - cls[h] - lefts[h])) for h in heads]
                if masked:
                    ws = [jnp.where(mask, w, 0.0) for w in ws]
                gs = [das[h] * ws[h] for h in heads]
                g_sums = [_running_sum_mm(g, le) for g in gs]
                dzs = [gs[h] - jnp.exp(zs[h] - sp_alls[h]) * (gls[h] + g_sums[h]) for h in heads]
                if masked:
                    dzs = [jnp.where(mask, dz, 0.0) for dz in dzs]
                dzs = [dz.astype(MXU_DTYPE) for dz in dzs]
                dk_t = jnp.zeros((SB_BLOCK, LANES), F32)
                dv_t = jnp.zeros((SB_BLOCK, LANES), F32)
                for h in heads:
                    dq = dq + jnp.dot(dzs[h], kms[h], preferred_element_type=F32)
                    dk_t = dk_t + lax.dot_general(dzs[h], qms[h], _TN, preferred_element_type=F32)
                    dv_t = dv_t + lax.dot_general(ws[h].astype(MXU_DTYPE), doms[h], _TN, preferred_element_type=F32)
                d_ref[1, cols, :] += dk_t
                d_ref[2, cols, :] += dv_t
                cls = [cls[h] + jnp.sum(sps[h], axis=-1, keepdims=True) for h in heads]
                gls = [gls[h] + jnp.sum(gs[h], axis=-1, keepdims=True) for h in heads]
                return (dq, *cls, *gls)

            zero_col = jnp.zeros((SB_BQ, 1), F32)
            out = _sb_keys_ascending(qi, nrun_ref[pl.program_id(0), qi], tile,
                                     (jnp.zeros((SB_BQ, LANES), F32),) + (zero_col,) * (2 * per_pair))
            d_ref[0, rows, :] = out[0]
            return carry

        lax.fori_loop(0, nq, q_loop, 0)

    col = lambda off: pl.BlockSpec((t_len, LANES), lambda p: (0, off + p))
    return pl.pallas_call(
        body, name="sb_bwd",
        grid=(n_pairs,),
        in_specs=[col(0), col(n_pairs), col(2 * n_pairs),
                  pl.BlockSpec((per_pair, t_len, 1), lambda p: (p, 0, 0)),
                  pl.BlockSpec(memory_space=pltpu.SMEM), col(0), _HBM],
        out_specs=pl.BlockSpec((3, t_len, LANES), lambda p: (DPROJ_SB_SLOT // 3, 0, p)),
        out_shape=jax.ShapeDtypeStruct(dproj.shape, dproj.dtype),
        input_output_aliases={6: 0},
        compiler_params=_params("arbitrary"),
    )(proj, proj, proj, sp_total, n_run_all, d_o, dproj)


def _conv_taps(xin, rows, t_len):
    taps = []
    for i in range(CONV_WIDTH):
        shift = CONV_WIDTH - 1 - i
        if shift == 0:
            taps.append(xin)
        else:
            taps.append(jnp.where(rows >= shift, pltpu.roll(xin, shift, axis=0), 0.0))
    return taps


def _gdn_prep_body_common(x_ref, w_ref, t_len):
    j = pl.program_id(0)
    xin = x_ref[...]
    rows = lax.broadcasted_iota(jnp.int32, (t_len, LANES), 0)
    taps = _conv_taps(xin, rows, t_len)
    pre = taps[0] * w_ref[0:1, :]
    for i in range(1, CONV_WIDTH):
        pre = pre + taps[i] * w_ref[i:i + 1, :]
    sg = _sigmoid(pre)
    act = pre * sg
    is_qk = j < 2 * GDN_HEADS
    nrm = jnp.where(is_qk, lax.rsqrt(jnp.sum(act * act, axis=-1, keepdims=True) + EPS), 1.0)
    sc = jnp.where(j < GDN_HEADS, float(GDN_HEAD_DIM) ** -0.5, 1.0)
    return j, rows, taps, pre, sg, act, is_qk, nrm, sc


def _gdn_prep_call(proj, conv_w, t_len):
    first = 2048 // LANES

    def body(x_ref, w_ref, out_ref):
        _, _, _, _, _, act, _, nrm, sc = _gdn_prep_body_common(x_ref, w_ref, t_len)
        out_ref[...] = act * nrm * sc

    return pl.pallas_call(
        body, name="gdn_prep",
        grid=(3 * GDN_HEADS,),
        in_specs=[pl.BlockSpec((t_len, LANES), lambda j: (0, first + j)),
                  pl.BlockSpec((CONV_WIDTH, LANES), lambda j: (0, j))],
        out_specs=pl.BlockSpec((t_len, LANES), lambda j: (0, j)),
        out_shape=jax.ShapeDtypeStruct((t_len, 3 * 512), F32),
        compiler_params=_params("arbitrary"),
    )(proj, conv_w)


def _gdn_prep_bwd_call(proj, conv_w, d_act3, dproj, t_len):
    first = 2048 // LANES

    def body(x_ref, w_ref, d_ref, dproj_in_ref, dx_ref, dw_ref):
        _, rows, taps, pre, sg, act, is_qk, nrm, sc = _gdn_prep_body_common(x_ref, w_ref, t_len)
        d_out = d_ref[0]
        dn = d_out * sc
        d_norm = nrm * dn - act * (nrm * nrm * nrm) * jnp.sum(dn * act, axis=-1, keepdims=True)
        d_act = jnp.where(is_qk, d_norm, d_out)
        d_pre = d_act * sg * (1.0 + pre * (1.0 - sg))
        dx = d_pre * w_ref[CONV_WIDTH - 1:CONV_WIDTH, :]
        dw_ref[CONV_WIDTH - 1:CONV_WIDTH, :] = jnp.sum(d_pre * taps[CONV_WIDTH - 1], axis=0, keepdims=True)
        for i in range(CONV_WIDTH - 1):
            shift = CONV_WIDTH - 1 - i
            up = jnp.where(rows < t_len - shift, pltpu.roll(d_pre, t_len - shift, axis=0), 0.0)
            dx = dx + up * w_ref[i:i + 1, :]
            dw_ref[i:i + 1, :] = jnp.sum(d_pre * taps[i], axis=0, keepdims=True)
        dx_ref[0] = dx

    return pl.pallas_call(
        body, name="gdn_prep_bwd",
        grid=(3 * GDN_HEADS,),
        in_specs=[pl.BlockSpec((t_len, LANES), lambda j: (0, first + j)),
                  pl.BlockSpec((CONV_WIDTH, LANES), lambda j: (0, j)),
                  pl.BlockSpec((1, t_len, LANES), lambda j: (j // GDN_HEADS, 0, j % GDN_HEADS)), _HBM],
        out_specs=[pl.BlockSpec((1, t_len, LANES), lambda j: (DPROJ_GDN_SLOT + j // GDN_HEADS, 0, j % GDN_HEADS)),
                   pl.BlockSpec((CONV_WIDTH, LANES), lambda j: (0, j))],
        out_shape=[jax.ShapeDtypeStruct(dproj.shape, dproj.dtype),
                   jax.ShapeDtypeStruct((CONV_WIDTH, 3 * 512), F32)],
        input_output_aliases={3: 0},
        compiler_params=_params("arbitrary"),
    )(proj, conv_w, d_act3, dproj)


def _chunk_cumsum_matrix():
    r = lax.broadcasted_iota(jnp.int32, (LANES, LANES), 0)
    c = lax.broadcasted_iota(jnp.int32, (LANES, LANES), 1)
    return ((r <= c) & ((r // CHUNK) == (c // CHUNK))).astype(F32)


def _gdn_gates_call(ps, pst, alog_l, dtb_l, alog_c, dtb_c, t_len):
    def body(ps_ref, pst_ref, al_ref, dl_ref, ac_ref, dc_ref, beta_ref, gcol_ref, grow_ref):
        upper = _chunk_cumsum_matrix()
        lower = upper.T
        psv = ps_ref[...]
        beta_ref[...] = _sigmoid(psv)
        g_l = -jnp.exp(al_ref[...]) * _softplus(psv + dl_ref[...])
        g_r = -jnp.exp(ac_ref[...]) * _softplus(pst_ref[...] + dc_ref[...])
        for w in range(t_len // LANES):
            sl = slice(w * LANES, (w + 1) * LANES)
            gcol_ref[sl, :] = _mx(lower, g_l[sl, :])
            grow_ref[:, sl] = _mx(g_r[:, sl], upper)

    vm = pl.BlockSpec(memory_space=pltpu.VMEM)
    return pl.pallas_call(
        body, name="gdn_gates",
        in_specs=[vm] * 6, out_specs=[vm] * 3,
        out_shape=[jax.ShapeDtypeStruct((t_len, LANES), F32),
                   jax.ShapeDtypeStruct((t_len, LANES), F32),
                   jax.ShapeDtypeStruct((8, t_len), F32)],
        compiler_params=pltpu.CompilerParams(vmem_limit_bytes=VMEM_LIMIT_BYTES),
    )(ps, pst, alog_l, dtb_l, alog_c, dtb_c)


def _gdn_gates_bwd_call(ps, alog_l, dtb_l, d_l, t_len):
    def body(ps_ref, al_ref, dl_ref, d_ref, dps_ref, gal_ref, gdt_ref):
        lane = lax.broadcasted_iota(jnp.int32, (1, LANES), 1)
        psv = ps_ref[...]
        dv = d_ref[...]
        beta = _sigmoid(psv)
        ea = jnp.exp(al_ref[...])
        arg = psv + dl_ref[...]
        g = -ea * _softplus(arg)
        d_a = dv * (-ea) * _sigmoid(arg)
        is_a = (lane >= GDN_HEADS) & (lane < 2 * GDN_HEADS)
        dps_ref[...] = jnp.where(lane < GDN_HEADS, dv * beta * (1.0 - beta), jnp.where(is_a, d_a, 0.0))
        gdt_ref[...] = jnp.where(is_a, jnp.sum(d_a, axis=0, keepdims=True), 0.0)
        gal_ref[...] = jnp.where(is_a, jnp.sum(dv * g, axis=0, keepdims=True), 0.0)

    vm = pl.BlockSpec(memory_space=pltpu.VMEM)
    return pl.pallas_call(
        body, name="gdn_gates_bwd",
        in_specs=[vm] * 4, out_specs=[vm] * 3,
        out_shape=[jax.ShapeDtypeStruct((t_len, LANES), F32),
                   jax.ShapeDtypeStruct((1, LANES), F32),
                   jax.ShapeDtypeStruct((1, LANES), F32)],
        compiler_params=pltpu.CompilerParams(vmem_limit_bytes=VMEM_LIMIT_BYTES),
    )(ps, alog_l, dtb_l, d_l)


def _bm(a, b):
    return _m3_general(a, b, _BNN)


def _bm_nt(a, b):
    return _m3_general(a, b, _BNT)


def _bm_tn(a, b):
    return _m3_general(a, b, _BTN)


def _heads_of(ref, rows):
    return jnp.stack([ref[rows, h * GDN_HEAD_DIM:(h + 1) * GDN_HEAD_DIM] for h in range(GDN_HEADS)])


def _chunk_terms(q_ref, k_ref, v_ref, b_ref, gc_ref, gr_ref, c, incl, strict):
    r0 = pl.multiple_of(c * CHUNK, CHUNK)
    rows = pl.ds(r0, CHUNK)
    q, k, v = _heads_of(q_ref, rows), _heads_of(k_ref, rows), _heads_of(v_ref, rows)
    lane_ids = lax.broadcasted_iota(jnp.int32, (1, LANES), 1)
    pick = lambda slab, first: jnp.stack([jnp.sum(jnp.where(lane_ids == first + h, slab, 0.0), axis=-1, keepdims=True)
                                          for h in range(GDN_HEADS)])
    b = pick(b_ref[rows, :], 0)
    gc = pick(gc_ref[rows, :], GDN_HEADS)
    gr = gr_ref[:, c]
    dm = jnp.where(incl, jnp.exp(jnp.where(incl, gc - gr, 0.0)), 0.0)
    kb = k * b
    vb = v * b
    e = jnp.exp(gc)
    kk_qk = _bm_nt(jnp.concatenate([kb, q], axis=1), k)
    a = jnp.where(strict, kk_qk[:, :CHUNK] * dm, 0.0)
    p = jnp.where(incl, kk_qk[:, CHUNK:] * dm, 0.0)
    gl = gc[:, CHUNK - 1:CHUNK, :]
    eg = jnp.exp(gl - gc)
    return rows, q, k, v, b, gc, dm, kb, vb, e, a, p, gl, eg


def _unit_lower_inverse(a, eye):
    x = -a
    tm = eye + x
    xp = _bm(x, x)
    for _ in range(4):
        both = _bm(jnp.concatenate([xp, tm], axis=1), xp)
        tm = tm + both[:, CHUNK:]
        xp = both[:, :CHUNK]
    return tm + _bm(tm, xp)


def _gdn_specs(t_len, n_chunks, reverse):
    cps = GDN_CHUNKS_PER_STEP
    steps = n_chunks // cps
    at = (lambda g: steps - 1 - g) if reverse else (lambda g: g)
    rows_blk = lambda width, part=0: pl.BlockSpec((cps * CHUNK, width), lambda g: (at(g), part))
    gate_r = pl.BlockSpec((GDN_HEADS, cps, 1, CHUNK), lambda g: (0, at(g), 0, 0))
    per_chunk = lambda r, c: pl.BlockSpec((GDN_HEADS, cps, r, c), lambda g: (0, at(g), 0, 0))
    return cps, steps, rows_blk, gate_r, per_chunk


def _gdn_fwd_call(gact, beta_c, gam_c, gam_r, t_len):
    n_chunks = t_len // CHUNK
    dk = GDN_HEAD_DIM
    width = GDN_HEADS * dk
    cps, steps, rows_blk, gate_r, per_chunk = _gdn_specs(t_len, n_chunks, False)

    def body(q_ref, k_ref, v_ref, b_ref, gc_ref, gr_ref, o_ref, s_ref, t_ref, state_ref):
        row = lax.broadcasted_iota(jnp.int32, (CHUNK, CHUNK), 0)
        col = lax.broadcasted_iota(jnp.int32, (CHUNK, CHUNK), 1)
        incl, strict = row >= col, row > col
        eye = (row == col).astype(F32)

        @pl.when(pl.program_id(0) == 0)
        def _():
            state_ref[...] = jnp.zeros_like(state_ref)

        def chunk(c, carry):
            rows, q, k, v, b, gc, dm, kb, vb, e, a, p, gl, eg = _chunk_terms(
                q_ref, k_ref, v_ref, b_ref, gc_ref, gr_ref, c, incl, strict)
            s = state_ref[...]
            tm = _unit_lower_inverse(a, eye)
            uw = _bm(tm, jnp.concatenate([vb, kb * e], axis=2))
            u, w = uw[:, :, :dk], uw[:, :, dk:]
            ws_qs = _bm(jnp.concatenate([w, q * e], axis=1), s)
            vn = u - ws_qs[:, :CHUNK]
            o = ws_qs[:, CHUNK:] + _bm(p, vn)
            for h in range(GDN_HEADS):
                o_ref[rows, h * dk:(h + 1) * dk] = o[h]
            s_ref[:, c] = s
            t_ref[:, c] = tm
            state_ref[...] = s * jnp.exp(gl) + _bm_tn(k * eg, vn)
            return carry

        lax.fori_loop(0, cps, chunk, 0)

    return pl.pallas_call(
        body, name="gdn_fwd",
        grid=(steps,),
        in_specs=[rows_blk(width, 0), rows_blk(width, 1), rows_blk(width, 2), rows_blk(LANES), rows_blk(LANES), gate_r],
        out_specs=[rows_blk(width), per_chunk(dk, dk), per_chunk(CHUNK, CHUNK)],
        out_shape=[jax.ShapeDtypeStruct((t_len, width), F32),
                   jax.ShapeDtypeStruct((GDN_HEADS, n_chunks, dk, dk), F32),
                   jax.ShapeDtypeStruct((GDN_HEADS, n_chunks, CHUNK, CHUNK), F32)],
        scratch_shapes=[pltpu.VMEM((GDN_HEADS, dk, dk), F32)],
        compiler_params=_params("arbitrary"),
    )(gact, gact, gact, beta_c, gam_c, gam_r)


def _gdn_bwd_call(gact, beta_c, gam_c, gam_r, s_all, t_all, d_o, t_len):
    n_chunks = t_len // CHUNK
    dk = GDN_HEAD_DIM
    width = GDN_HEADS * dk
    cps, steps, rows_blk, gate_r, per_chunk = _gdn_specs(t_len, n_chunks, True)

    def body(q_ref, k_ref, v_ref, b_ref, gc_ref, gr_ref, s_ref, t_ref, do_ref, d_ref, dgate_ref, dstate_ref):
        row = lax.broadcasted_iota(jnp.int32, (CHUNK, CHUNK), 0)
        col = lax.broadcasted_iota(jnp.int32, (CHUNK, CHUNK), 1)
        incl, strict = row >= col, row > col
        upper = jnp.broadcast_to((row <= col).astype(F32), (GDN_HEADS, CHUNK, CHUNK))
        ones = jnp.ones((GDN_HEADS, CHUNK, LANES), F32)
        last_row = lax.broadcasted_iota(jnp.int32, (CHUNK, 1), 0) == CHUNK - 1
        lane_ids = lax.broadcasted_iota(jnp.int32, (1, LANES), 1)
        rsum = lambda m: jnp.sum(m, axis=-1, keepdims=True)
        total = lambda m: jnp.sum(rsum(m), axis=1, keepdims=True)

        @pl.when(pl.program_id(0) == 0)
        def _():
            dstate_ref[...] = jnp.zeros_like(dstate_ref)

        def chunk(step, carry):
            c = cps - 1 - step
            rows, q, k, v, b, gc, dm, kb, vb, e, a, p, gl, eg = _chunk_terms(
                q_ref, k_ref, v_ref, b_ref, gc_ref, gr_ref, c, incl, strict)
            ds = dstate_ref[...]
            s = s_ref[:, c]
            tm = t_ref[:, c]
            d_out = _heads_of(do_ref, rows)
            el = jnp.exp(gl)
            kbe = kb * e
            uw = _bm(tm, jnp.concatenate([vb, kbe], axis=2))
            u, w = uw[:, :, :dk], uw[:, :, dk:]
            vn = u - _bm(w, s)
            qe = q * e
            kd = k * eg

            d_vn = _bm_tn(p, d_out) + _bm(kd, ds)
            on_s = _bm_nt(jnp.concatenate([d_out, d_vn], axis=1), s)
            d_qe, d_w = on_s[:, :CHUNK], -on_s[:, CHUNK:]
            d_p = jnp.where(incl, _bm_nt(d_out, vn), 0.0)
            dstate_ref[...] = el * ds + _bm_tn(jnp.concatenate([qe, -w], axis=1),
                                               jnp.concatenate([d_out, d_vn], axis=1))
            d_kd = _bm_nt(vn, ds)
            d_both = _bm_tn(tm, jnp.concatenate([d_vn, d_w], axis=2))
            d_vb, d_kbe = d_both[:, :, :dk], d_both[:, :, dk:]
            d_a = -jnp.where(strict, _bm_nt(d_both, uw), 0.0)
            m = d_a * dm
            n = d_p * dm
            on_k = _bm(jnp.concatenate([m, n], axis=1), k)
            d_kb = on_k[:, :CHUNK] + d_kbe * e
            d_q = on_k[:, CHUNK:] + d_qe * e
            d_k = (_bm_tn(jnp.concatenate([m, n], axis=1), jnp.concatenate([kb, q], axis=1))
                   + d_kd * eg + b * d_kb)
            d_v = b * d_vb
            r = d_a * a + d_p * p
            kd_term = rsum(d_kd * kd)
            d_gl = total(ds * s) * el + jnp.sum(kd_term, axis=1, keepdims=True)
            d_gam = (rsum(r) - _bm_tn(r, ones)[:, :, 0:1] + rsum(d_qe * qe) + rsum(d_kbe * kbe) - kd_term
                     + jnp.where(last_row, d_gl, 0.0))
            d_beta = rsum(d_kb * k) + rsum(d_vb * v)
            d_g = _bm(upper, d_gam * ones)[:, :, 0:1]
            gates = jnp.zeros((CHUNK, LANES), F32)
            for h in range(GDN_HEADS):
                lanes = slice(h * dk, (h + 1) * dk)
                d_ref[0, rows, lanes] = d_q[h]
                d_ref[1, rows, lanes] = d_k[h]
                d_ref[2, rows, lanes] = d_v[h]
                gates = gates + (jnp.where(lane_ids == h, d_beta[h], 0.0)
                                 + jnp.where(lane_ids == GDN_HEADS + h, d_g[h], 0.0))
            dgate_ref[rows, :] = gates
            return carry

        lax.fori_loop(0, cps, chunk, 0)

    d_spec = pl.BlockSpec((3, cps * CHUNK, width), lambda g: (0, steps - 1 - g, 0))
    return pl.pallas_call(
        body, name="gdn_bwd",
        grid=(steps,),
        in_specs=[rows_blk(width, 0), rows_blk(width, 1), rows_blk(width, 2), rows_blk(LANES), rows_blk(LANES), gate_r,
                  per_chunk(dk, dk), per_chunk(CHUNK, CHUNK), rows_blk(width)],
        out_specs=[d_spec, rows_blk(LANES)],
        out_shape=[jax.ShapeDtypeStruct((3, t_len, width), F32),
                   jax.ShapeDtypeStruct((t_len, LANES), F32)],
        scratch_shapes=[pltpu.VMEM((GDN_HEADS, dk, dk), F32)],
        compiler_params=_params("arbitrary"),
    )(gact, gact, gact, beta_c, gam_c, gam_r, s_all, t_all, d_o)


def _group_matrix(width, group):
    r = lax.broadcasted_iota(jnp.int32, (width, width), 0)
    c = lax.broadcasted_iota(jnp.int32, (width, width), 1)
    return ((r // group) == (c // group)).astype(F32)


def _post_call(o_sb, o_gd, proj, x, target, w_out, sbw, gdw, fw, tm=256):
    t_len, d = x.shape
    half = 512
    zsb_blk = 1536 // half
    zgd_blk = 3584 // half

    def body(osb_ref, ogd_ref, zsb_ref, zgd_ref, x_ref, tg_ref, wo_ref, sbw_ref, gdw_ref, fw_ref,
             dx2_ref, dosb_ref, dogd_ref, dz_ref, loss_ref, gfw_ref, gsb_ref, ggd_ref, gwo_ref):
        step = pl.program_id(0)

        @pl.when(step == 0)
        def _():
            loss_ref[...] = jnp.zeros_like(loss_ref)
            gfw_ref[...] = jnp.zeros_like(gfw_ref)
            gsb_ref[...] = jnp.zeros_like(gsb_ref)
            ggd_ref[...] = jnp.zeros_like(ggd_ref)
            gwo_ref[...] = jnp.zeros_like(gwo_ref)

        def head_forward(o, z, w, gmat, inv):
            r = lax.rsqrt(_running_sum_mm(o * o, gmat) * inv + EPS)
            nrm = o * r * w
            sg = _sigmoid(z)
            return r, nrm, sg, nrm * (z * sg)

        def head_backward(d_m, o, z, w, gmat, inv, r, nrm, sg):
            d_n = d_m * (z * sg)
            d_z = d_m * nrm * (sg * (1.0 + z * (1.0 - sg)))
            dnw = d_n * w
            d_o = r * dnw - o * (r * r * r) * (_running_sum_mm(dnw * o, gmat) * inv)
            return d_o, d_z, jnp.sum(d_n * o * r, axis=0, keepdims=True)

        g_sb = _group_matrix(half, SB_HEAD_DIM).astype(MXU_DTYPE)
        g_gd = _group_matrix(half, GDN_HEAD_DIM).astype(MXU_DTYPE)
        osb, ogd, zsb, zgd = osb_ref[...], ogd_ref[...], zsb_ref[...], zgd_ref[...]
        sbw_v, gdw_v = sbw_ref[...], gdw_ref[...]
        r_sb, n_sb, sg_sb, m_sb = head_forward(osb, zsb, sbw_v, g_sb, 1.0 / SB_HEAD_DIM)
        r_gd, n_gd, sg_gd, m_gd = head_forward(ogd, zgd, gdw_v, g_gd, 1.0 / GDN_HEAD_DIM)
        mixed = jnp.concatenate([m_sb, m_gd], axis=1).astype(MXU_DTYPE)
        wo = wo_ref[...]
        x2 = x_ref[...] + jnp.dot(mixed, wo, preferred_element_type=F32)
        r2 = lax.rsqrt(jnp.mean(x2 * x2, axis=-1, keepdims=True) + EPS)
        fw_v = fw_ref[...]
        err = x2 * r2 * fw_v - tg_ref[...]
        loss_ref[...] += 0.5 * jnp.sum(jnp.sum(err * err, axis=-1, keepdims=True) * (1.0 / d))
        dy = err * (1.0 / d)
        gg = dy * fw_v
        dx2 = r2 * gg - x2 * ((r2 * r2 * r2) * jnp.mean(gg * x2, axis=-1, keepdims=True))
        gfw_ref[...] += jnp.sum(dy * x2 * r2, axis=0, keepdims=True)
        dx2_ref[...] = dx2
        dx2b = dx2.astype(MXU_DTYPE)
        d_mixed = lax.dot_general(dx2b, wo, _NT, preferred_element_type=F32)
        gwo_ref[...] += lax.dot_general(mixed, dx2b, _TN, preferred_element_type=F32)
        d_osb, d_zsb, gsb = head_backward(d_mixed[:, :half], osb, zsb, sbw_v, g_sb, 1.0 / SB_HEAD_DIM, r_sb, n_sb, sg_sb)
        d_ogd, d_zgd, ggd = head_backward(d_mixed[:, half:], ogd, zgd, gdw_v, g_gd, 1.0 / GDN_HEAD_DIM, r_gd, n_gd, sg_gd)
        dosb_ref[...] = d_osb
        dogd_ref[...] = d_ogd
        dz_ref[0] = d_zsb
        dz_ref[1] = d_zgd
        gsb_ref[...] += gsb
        ggd_ref[...] += ggd

    row_blk = lambda w: pl.BlockSpec((tm, w), lambda i: (i, 0))
    fixed = lambda r, w: pl.BlockSpec((r, w), lambda i: (0, 0))
    return pl.pallas_call(
        body, name="post",
        grid=(t_len // tm,),
        in_specs=[row_blk(half), row_blk(half),
                  pl.BlockSpec((tm, half), lambda i: (i, zsb_blk)),
                  pl.BlockSpec((tm, half), lambda i: (i, zgd_blk)),
                  row_blk(d), row_blk(d), fixed(d, d), fixed(1, half), fixed(1, half), fixed(1, d)],
        out_specs=[row_blk(d), row_blk(half), row_blk(half),
                   pl.BlockSpec((2, tm, half), lambda i: (DPROJ_GATE_SLOT // 2, i, 0)),
                   fixed(1, LANES), fixed(1, d), fixed(1, half), fixed(1, half), fixed(d, d)],
        out_shape=[jax.ShapeDtypeStruct((t_len, d), F32)] + [jax.ShapeDtypeStruct((t_len, half), F32)] * 2
                  + [jax.ShapeDtypeStruct((len(DPROJ_PIECE_OF_SLOT), t_len, half), F32),
                     jax.ShapeDtypeStruct((1, LANES), F32), jax.ShapeDtypeStruct((1, d), F32),
                     jax.ShapeDtypeStruct((1, half), F32), jax.ShapeDtypeStruct((1, half), F32),
                     jax.ShapeDtypeStruct((d, d), F32)],
        compiler_params=_params("arbitrary"),
    )(o_sb, o_gd, proj, proj, x, target, w_out, sbw, gdw, fw)


def _piece_of_slot(s):
    return jnp.where(s < DPROJ_GDN_SLOT, s, jnp.where(s < DPROJ_GATE_SLOT, s + 1,
                                                     jnp.where(s == DPROJ_GATE_SLOT, 3, 7)))


def _gw_in_call(h_t, dproj8):
    d, t_len = h_t.shape
    n_piece, _, pw = dproj8.shape

    def body(ht_ref, dp_ref, gw_ref):
        gw_ref[...] = jnp.dot(ht_ref[...], dp_ref[0].astype(MXU_DTYPE), preferred_element_type=F32)

    return pl.pallas_call(
        body, name="gw_in",
        grid=(n_piece,),
        in_specs=[pl.BlockSpec((d, t_len), lambda s: (0, 0)),
                  pl.BlockSpec((1, t_len, pw), lambda s: (s, 0, 0))],
        out_specs=pl.BlockSpec((d, pw), lambda s: (0, _piece_of_slot(s))),
        out_shape=jax.ShapeDtypeStruct((d, n_piece * pw), F32),
        compiler_params=_params("arbitrary"),
    )(h_t, dproj8)


def _gw_small_call(h_t, dsmall, tm=512):
    d, t_len = h_t.shape
    ns = dsmall.shape[1]

    def body(ht_ref, dp_ref, gw_ref):
        @pl.when(pl.program_id(0) == 0)
        def _():
            gw_ref[...] = jnp.zeros_like(gw_ref)

        gw_ref[...] += jnp.dot(ht_ref[...], dp_ref[...].astype(MXU_DTYPE), preferred_element_type=F32)

    return pl.pallas_call(
        body, name="gw_small",
        grid=(t_len // tm,),
        in_specs=[pl.BlockSpec((d, tm), lambda t: (0, t)),
                  pl.BlockSpec((tm, ns), lambda t: (t, 0))],
        out_specs=pl.BlockSpec((d, ns), lambda t: (0, 0)),
        out_shape=jax.ShapeDtypeStruct((d, ns), F32),
        compiler_params=_params("arbitrary"),
    )(h_t, dsmall)


def _dx_call(dproj8, dsmall, w_main, w_small, x, r, dx2, norm_w, tm=256):
    t_len, d = x.shape
    n_piece, _, pw = dproj8.shape
    ns = dsmall.shape[1]

    def body(dp_ref, ds_ref, wm_ref, ws_ref, x_ref, r_ref, dx2_ref, nw_ref, gx_ref, gnw_ref):
        @pl.when(pl.program_id(0) == 0)
        def _():
            gnw_ref[...] = jnp.zeros_like(gnw_ref)

        dh = lax.dot_general(ds_ref[...].astype(MXU_DTYPE), ws_ref[...], _NT, preferred_element_type=F32)
        for s, p in enumerate(DPROJ_PIECE_OF_SLOT):
            dh = dh + lax.dot_general(dp_ref[s].astype(MXU_DTYPE), wm_ref[:, p * pw:(p + 1) * pw], _NT,
                                      preferred_element_type=F32)
        xv, rv = x_ref[...], r_ref[...]
        dn = dh * nw_ref[...]
        gx_ref[...] = dx2_ref[...] + rv * dn - xv * ((rv * rv * rv) * jnp.mean(dn * xv, axis=-1, keepdims=True))
        gnw_ref[...] += jnp.sum(dh * xv * rv, axis=0, keepdims=True)

    return pl.pallas_call(
        body, name="dx",
        grid=(t_len // tm,),
        in_specs=[pl.BlockSpec((n_piece, tm, pw), lambda i: (0, i, 0)),
                  pl.BlockSpec((tm, ns), lambda i: (i, 0)),
                  pl.BlockSpec((d, n_piece * pw), lambda i: (0, 0)),
                  pl.BlockSpec((d, ns), lambda i: (0, 0)),
                  pl.BlockSpec((tm, d), lambda i: (i, 0)),
                  pl.BlockSpec((tm, 1), lambda i: (i, 0)),
                  pl.BlockSpec((tm, d), lambda i: (i, 0)),
                  pl.BlockSpec((1, d), lambda i: (0, 0))],
        out_specs=[pl.BlockSpec((tm, d), lambda i: (i, 0)),
                   pl.BlockSpec((1, d), lambda i: (0, 0))],
        out_shape=[jax.ShapeDtypeStruct((t_len, d), F32), jax.ShapeDtypeStruct((1, d), F32)],
        compiler_params=_params("arbitrary"),
    )(dproj8, dsmall, w_main, w_small, x, r, dx2, norm_w)


def _exchange_call(name, srcs, per_peer):
    n = len(srcs)
    out_shapes = [jax.ShapeDtypeStruct(s.shape if pp else (N_DEV,) + s.shape, s.dtype) for s, pp in zip(srcs, per_peer)]

    def body(*refs):
        src_refs, out_refs = refs[:n], refs[n:2 * n]
        send_sems, recv_sems, local_sems = refs[2 * n:]
        x, y, c = lax.axis_index("x"), lax.axis_index("y"), lax.axis_index("c")
        me = 4 * x + 2 * y + c
        copies = []
        for a in range(n):
            mine = src_refs[a].at[me] if per_peer[a] else src_refs[a]
            local = pltpu.make_async_copy(mine, out_refs[a].at[me], local_sems.at[a])
            local.start()
            copies.append(local)
        remote = []
        for k in range(1, N_DEV):
            kx, ky, kc = (k >> 2) & 1, (k >> 1) & 1, k & 1
            px = 1 - x if kx else x
            py = 1 - y if ky else y
            pc = 1 - c if kc else c
            peer = 4 * px + 2 * py + pc
            for a in range(n):
                sem = a * (N_DEV - 1) + (k - 1)
                src = src_refs[a].at[peer] if per_peer[a] else src_refs[a]
                cp = pltpu.make_async_remote_copy(
                    src_ref=src, dst_ref=out_refs[a].at[me],
                    send_sem=send_sems.at[sem], recv_sem=recv_sems.at[sem],
                    device_id=(px, py, pc), device_id_type=pl.DeviceIdType.MESH)
                cp.start()
                remote.append(cp)
        for cp in remote:
            cp.wait_send()
        for cp in remote:
            cp.wait_recv()
        for cp in copies:
            cp.wait()

    hbm = pl.BlockSpec(memory_space=pl.ANY)
    return pl.pallas_call(
        body, name=name,
        in_specs=[hbm] * n, out_specs=[hbm] * n, out_shape=out_shapes,
        scratch_shapes=[pltpu.SemaphoreType.DMA((n * (N_DEV - 1),)),
                        pltpu.SemaphoreType.DMA((n * (N_DEV - 1),)),
                        pltpu.SemaphoreType.DMA((n,))],
    )(*srcs)


N_CHIPS = 4
_HBM = pl.BlockSpec(memory_space=pl.ANY)
_MESH = pl.DeviceIdType.MESH


def _gather_call(name, srcs):
    n = len(srcs)
    per = N_DEV - 1

    def body(*refs):
        src_refs, out_refs = refs[:n], refs[n:2 * n]
        send_sems, recv_sems, local_sems = refs[2 * n:]
        x, y, c = lax.axis_index("x"), lax.axis_index("y"), lax.axis_index("c")
        me, sibling = (x, y, c), (x, y, 1 - c)
        chips = [(1 - x, y), (x, 1 - y), (1 - x, 1 - y)]
        slot = lambda px, py, pc: 4 * px + 2 * py + pc

        def copy(a, k, block, to, from_src=False):
            rows = out_refs[a].at[slot(*block)]
            return pltpu.make_async_remote_copy(
                src_ref=src_refs[a] if from_src else rows, dst_ref=rows,
                send_sem=send_sems.at[a * per + k], recv_sem=recv_sems.at[a * per + k],
                device_id=to, device_id_type=_MESH)

        local = [pltpu.make_async_copy(src_refs[a], out_refs[a].at[slot(*me)], local_sems.at[a]) for a in range(n)]
        for cp in local:
            cp.start()
        first = []
        for a in range(n):
            first.append(copy(a, 0, me, sibling, True))
            first += [copy(a, 1 + j, me, (*chip, c), True) for j, chip in enumerate(chips)]
        for cp in first:
            cp.start()
        passed = []
        for j, chip in enumerate(chips):
            for a in range(n):
                copy(a, 1 + j, (*chip, c), me).wait_recv()
                fwd = copy(a, 4 + j, (*chip, c), sibling)
                fwd.start()
                passed.append(fwd)
        for a in range(n):
            copy(a, 0, sibling, me).wait_recv()
            for j, chip in enumerate(chips):
                copy(a, 4 + j, (*chip, 1 - c), me).wait_recv()
        for cp in first + passed:
            cp.wait_send()
        for cp in local:
            cp.wait()

    return pl.pallas_call(
        body, name=name,
        in_specs=[_HBM] * n, out_specs=[_HBM] * n,
        out_shape=[jax.ShapeDtypeStruct((N_DEV,) + s.shape, s.dtype) for s in srcs],
        scratch_shapes=[pltpu.SemaphoreType.DMA((n * per,)), pltpu.SemaphoreType.DMA((n * per,)),
                        pltpu.SemaphoreType.DMA((n,))],
    )(*srcs)


def _sibling_send_call(name, srcs):
    n = len(srcs)

    def body(*refs):
        src_refs, out_refs = refs[:n], refs[n:2 * n]
        send_sems, recv_sems = refs[2 * n:]
        x, y, c = lax.axis_index("x"), lax.axis_index("y"), lax.axis_index("c")
        copies = []
        for a in range(n):
            for ch in range(N_CHIPS):
                copies.append(pltpu.make_async_remote_copy(
                    src_ref=src_refs[a].at[2 * ch + (1 - c)], dst_ref=out_refs[a].at[ch],
                    send_sem=send_sems.at[a * N_CHIPS + ch], recv_sem=recv_sems.at[a * N_CHIPS + ch],
                    device_id=(x, y, 1 - c), device_id_type=_MESH))
        for cp in copies:
            cp.start()
        for cp in copies:
            cp.wait_send()
        for cp in copies:
            cp.wait_recv()

    return pl.pallas_call(
        body, name=name,
        in_specs=[_HBM] * n, out_specs=[_HBM] * n,
        out_shape=[jax.ShapeDtypeStruct((N_CHIPS,) + s.shape[1:], s.dtype) for s in srcs],
        scratch_shapes=[pltpu.SemaphoreType.DMA((n * N_CHIPS,)), pltpu.SemaphoreType.DMA((n * N_CHIPS,))],
    )(*srcs)


def _pair_sum_call(name, parts, from_sibling, tr):
    _, rows, cols = parts.shape

    def body(p_ref, s_ref, o_ref):
        o_ref[...] = (p_ref[...] + s_ref[...]).astype(o_ref.dtype)

    return pl.pallas_call(
        body, name=name,
        grid=(N_CHIPS, rows // tr),
        in_specs=[pl.BlockSpec((1, tr, cols), lambda ch, i: (2 * ch + lax.axis_index("c"), i, 0)),
                  pl.BlockSpec((1, tr, cols), lambda ch, i: (ch, i, 0))],
        out_specs=pl.BlockSpec((1, tr, cols), lambda ch, i: (ch, i, 0)),
        out_shape=jax.ShapeDtypeStruct((N_CHIPS, rows, cols), WIRE_DTYPE),
        compiler_params=_params("arbitrary", "arbitrary"),
    )(parts, from_sibling)


def _chip_exchange_call(name, srcs):
    n = len(srcs)
    per = N_CHIPS - 1

    def body(*refs):
        src_refs, out_refs = refs[:n], refs[n:2 * n]
        send_sems, recv_sems, local_sems = refs[2 * n:]
        x, y, c = lax.axis_index("x"), lax.axis_index("y"), lax.axis_index("c")
        mine = 2 * x + y
        chips = [(1 - x, y), (x, 1 - y), (1 - x, 1 - y)]
        local = [pltpu.make_async_copy(src_refs[a].at[mine], out_refs[a].at[mine], local_sems.at[a]) for a in range(n)]
        for cp in local:
            cp.start()
        remote = []
        for a in range(n):
            for j, (px, py) in enumerate(chips):
                remote.append(pltpu.make_async_remote_copy(
                    src_ref=src_refs[a].at[2 * px + py], dst_ref=out_refs[a].at[mine],
                    send_sem=send_sems.at[a * per + j], recv_sem=recv_sems.at[a * per + j],
                    device_id=(px, py, c), device_id_type=_MESH))
        for cp in remote:
            cp.start()
        for cp in remote:
            cp.wait_send()
        for cp in remote:
            cp.wait_recv()
        for cp in local:
            cp.wait()

    return pl.pallas_call(
        body, name=name,
        in_specs=[_HBM] * n, out_specs=[_HBM] * n,
        out_shape=[jax.ShapeDtypeStruct(s.shape, s.dtype) for s in srcs],
        scratch_shapes=[pltpu.SemaphoreType.DMA((n * per,)), pltpu.SemaphoreType.DMA((n * per,)),
                        pltpu.SemaphoreType.DMA((n,))],
    )(*srcs)


def _adam_call(name, parts, w, m, v, tr):
    rows, cols = w.shape
    n_slots = parts.shape[0]

    def body(p_ref, w_ref, m_ref, v_ref, g_ref, d_ref, nm_ref, nv_ref):
        g = p_ref[0].astype(F32)
        for s in range(1, n_slots):
            g = g + p_ref[s].astype(F32)
        m_new = ADAM_B1 * m_ref[...] + (1.0 - ADAM_B1) * g
        v_new = ADAM_B2 * v_ref[...] + (1.0 - ADAM_B2) * (g * g)
        m_hat = m_new / (1.0 - ADAM_B1 ** ADAM_STEP)
        v_hat = v_new / (1.0 - ADAM_B2 ** ADAM_STEP)
        g_ref[...] = g
        d_ref[...] = -ADAM_LR * (m_hat / (jnp.sqrt(v_hat) + ADAM_EPS) + ADAM_WD * w_ref[...])
        nm_ref[...] = m_new
        nv_ref[...] = v_new

    blk = pl.BlockSpec((tr, cols), lambda i: (i, 0))
    return pl.pallas_call(
        body, name=name,
        grid=(rows // tr,),
        in_specs=[pl.BlockSpec((n_slots, tr, cols), lambda i: (0, i, 0)), blk, blk, blk],
        out_specs=[blk] * 4,
        out_shape=[jax.ShapeDtypeStruct((rows, cols), F32)] * 4,
        compiler_params=_params("arbitrary"),
    )(parts, w, m, v)


N_PIECES = 8
PIECE = 512
SHARD_COLS = 513
SHARD_PAD = 640
RELAYOUT_ROWS = 256


def _from_shards_call(shards):
    _, d, _ = shards.shape
    tr = RELAYOUT_ROWS

    def body(p_ref, m_ref, s_ref):
        lane = lax.broadcasted_iota(jnp.int32, (tr, SHARD_PAD), 1)
        pad = jnp.zeros((tr, SHARD_PAD - SHARD_COLS), F32)
        sh = [jnp.concatenate([p_ref[s].astype(F32), pad], axis=1) for s in range(N_DEV)]
        for p in range(N_PIECES):
            y = sh[p] if p == 0 else pltpu.roll(sh[p], p, axis=1)
            if p > 0:
                y = jnp.where(lane < p, pltpu.roll(sh[p - 1], SHARD_PAD - (SHARD_COLS - p), axis=1), y)
            m_ref[:, p * PIECE:(p + 1) * PIECE] = y[:, :PIECE].astype(m_ref.dtype)
        first_gate = N_PIECES * PIECE - (N_DEV - 1) * SHARD_COLS
        s_ref[...] = pltpu.roll(sh[N_DEV - 1], SHARD_PAD - first_gate, axis=1)[:, :LANES].astype(s_ref.dtype)

    return pl.pallas_call(
        body, name="w_in_from_shards",
        grid=(d // tr,),
        in_specs=[pl.BlockSpec((N_DEV, tr, SHARD_COLS), lambda i: (0, i, 0))],
        out_specs=[pl.BlockSpec((tr, N_PIECES * PIECE), lambda i: (i, 0)), pl.BlockSpec((tr, LANES), lambda i: (i, 0))],
        out_shape=[jax.ShapeDtypeStruct((d, N_PIECES * PIECE), shards.dtype),
                   jax.ShapeDtypeStruct((d, LANES), shards.dtype)],
        compiler_params=_params("arbitrary"),
    )(shards)


def _to_shards_call(main, gates):
    d = main.shape[0]
    tr = RELAYOUT_ROWS

    def body(m_ref, s_ref, o_ref):
        for s in range(N_DEV):
            if s < N_DEV - 1:
                x = m_ref[:, s * PIECE:s * PIECE + SHARD_PAD]
            else:
                x = jnp.concatenate([m_ref[:, s * PIECE:(s + 1) * PIECE], s_ref[...]], axis=1)
            y = x if s == 0 else pltpu.roll(x, SHARD_PAD - s, axis=1)
            o_ref[s] = y[:, :SHARD_COLS]

    return pl.pallas_call(
        body, name="w_in_to_shards",
        grid=(d // tr,),
        in_specs=[pl.BlockSpec((tr, N_PIECES * PIECE), lambda i: (i, 0)), pl.BlockSpec((tr, LANES), lambda i: (i, 0))],
        out_specs=pl.BlockSpec((N_DEV, tr, SHARD_COLS), lambda i: (0, i, 0)),
        out_shape=jax.ShapeDtypeStruct((N_DEV, d, SHARD_COLS), main.dtype),
        compiler_params=_params("arbitrary"),
    )(main, gates)


def _adamw(g, w, m, v):
    m_new = ADAM_B1 * m + (1.0 - ADAM_B1) * g
    v_new = ADAM_B2 * v + (1.0 - ADAM_B2) * (g * g)
    m_hat = m_new / (1.0 - ADAM_B1 ** ADAM_STEP)
    v_hat = v_new / (1.0 - ADAM_B2 ** ADAM_STEP)
    return -ADAM_LR * (m_hat / (jnp.sqrt(v_hat) + ADAM_EPS) + ADAM_WD * w), m_new, v_new


def _adam_small_call(parts, ws, ms, vs):
    n = len(ws)
    n_slots = parts.shape[0]

    def body(*refs):
        p_ref = refs[0]
        w_refs, m_refs, v_refs = refs[1:1 + n], refs[1 + n:1 + 2 * n], refs[1 + 2 * n:1 + 3 * n]
        loss_ref = refs[1 + 3 * n]
        outs = refs[2 + 3 * n:]
        g_all = p_ref[0]
        for s in range(1, n_slots):
            g_all = g_all + p_ref[s]
        loss_ref[...] = g_all[n:n + 1, 0:1]
        for r in range(n):
            size = w_refs[r].shape[1]
            g = g_all[r:r + 1, :size]
            delta, m_new, v_new = _adamw(g, w_refs[r][...], m_refs[r][...], v_refs[r][...])
            for kind, val in enumerate((g, delta, m_new, v_new)):
                outs[kind * n + r][...] = val

    vm = pl.BlockSpec(memory_space=pltpu.VMEM)
    shapes = [jax.ShapeDtypeStruct(w.shape, F32) for w in ws]
    return pl.pallas_call(
        body, name="adam_small",
        in_specs=[vm] * (1 + 3 * n), out_specs=[vm] * (1 + 4 * n),
        out_shape=[jax.ShapeDtypeStruct((1, 1), F32)] + shapes * 4,
    )(parts, *ws, *ms, *vs)


_SMALL_ROWS = ("norm1_w", "final_norm_w", "sb_norm_w", "gdn_norm_w", "gdn_A_log", "gdn_dt_bias", "loss")


def _pack_small(vals, width):
    rows = [jnp.pad(a.reshape(1, -1).astype(F32), ((0, 0), (0, width - a.size))) for a in vals]
    rows += [jnp.zeros((1, width), F32)] * (8 - len(rows))
    return jnp.concatenate(rows, axis=0)


def _device_step(x2d, tgt, w_main, w_small, w_out_full, conv_full, norm1_w, sb_norm_w, gdn_A_log, gdn_dt_bias,
                 gdn_norm_w, final_norm_w):
    t_len, d = x2d.shape
    n_chunks = t_len // CHUNK
    w_main, w_small, w_out_full = (a.astype(MXU_DTYPE) for a in (w_main, w_small, w_out_full))
    w_small_t = w_small[:, :2 * GDN_HEADS].T

    pad_lanes = lambda a, lo: jnp.pad(a.reshape(1, -1), ((0, 0), (lo, LANES - lo - a.size)))
    alog_l, dtb_l = pad_lanes(gdn_A_log, GDN_HEADS), pad_lanes(gdn_dt_bias, GDN_HEADS)
    alog_c, dtb_c = alog_l[:, :8].T, dtb_l[:, :8].T
    sbw = jnp.tile(sb_norm_w, (1, 512 // SB_HEAD_DIM))
    gdw = jnp.tile(gdn_norm_w, (1, 512 // GDN_HEAD_DIM))
    fw = final_norm_w.reshape(1, d)

    proj, ps, pst, h_t, r1 = _inproj_call(x2d, norm1_w, w_main, w_small, w_small_t)
    o_sb, sp_total, sb_blocks_run = _sb_fwd_call(proj, t_len)
    gact = _gdn_prep_call(proj, conv_full, t_len)
    beta_l, gcol_l, grow = _gdn_gates_call(ps, pst, alog_l, dtb_l, alog_c, dtb_c, t_len)
    gam_r = grow[GDN_HEADS:2 * GDN_HEADS].reshape(GDN_HEADS, n_chunks, 1, CHUNK)
    o_gd, s_all, t_all = _gdn_fwd_call(gact, beta_l, gcol_l, gam_r, t_len)

    (dx2, d_osb, d_ogd, dproj8, loss_p, g_fw, g_sbw, g_gdw, g_wout) = _post_call(
        o_sb, o_gd, proj, x2d, tgt, w_out_full, sbw, gdw, fw)

    dproj8 = _sb_bwd_call(proj, sp_total, sb_blocks_run, d_osb, dproj8, t_len)
    d_gact3, d_gates = _gdn_bwd_call(gact, beta_l, gcol_l, gam_r, s_all, t_all, d_ogd, t_len)
    dproj8, g_conv = _gdn_prep_bwd_call(proj, conv_full, d_gact3, dproj8, t_len)
    dsmall, g_alog, g_dtb = _gdn_gates_bwd_call(ps, alog_l, dtb_l, d_gates, t_len)

    g_w_main = _gw_in_call(h_t, dproj8)
    g_w_small = _gw_small_call(h_t, dsmall)
    grad_x, g_n1 = _dx_call(dproj8, dsmall, w_main, w_small, x2d, r1, dx2, norm1_w)
    return (loss_p, grad_x, g_n1, (g_w_main, g_w_small), g_sbw, g_conv, g_alog, g_dtb, g_gdw, g_wout, g_fw)


def kernel(x, norm1_w, w_in, sb_norm_w, gdn_conv_w, gdn_A_log, gdn_dt_bias, gdn_norm_w, w_out, final_norm_w, loss_target, m_norm1_w, m_w_in, m_sb_norm_w, m_gdn_conv_w, m_gdn_A_log, m_gdn_dt_bias, m_gdn_norm_w, m_w_out, m_final_norm_w, v_norm1_w, v_w_in, v_sb_norm_w, v_gdn_conv_w, v_gdn_A_log, v_gdn_dt_bias, v_gdn_norm_w, v_w_out, v_final_norm_w):
    d = x.shape[2]
    shard_cols = w_in.shape[2]
    conv_cols = gdn_conv_w.shape[2]

    w_in_g, w_out_g, conv_g = _gather_call(
        "gather_weights", [w_in[0].astype(WIRE_DTYPE), w_out[0].astype(WIRE_DTYPE), gdn_conv_w[0]])
    w_main, w_small = _from_shards_call(w_in_g)
    conv_full = conv_g.transpose(1, 0, 2).reshape(CONV_WIDTH, N_DEV * conv_cols)

    (loss_p, grad_x, g_n1, (g_w_main, g_w_small), g_sbw, g_conv, g_alog, g_dtb, g_gdw, g_wout, g_fw) = _device_step(
        x[0], loss_target[0], w_main, w_small, w_out_g.reshape(d, d), conv_full, norm1_w, sb_norm_w, gdn_A_log,
        gdn_dt_bias, gdn_norm_w, final_norm_w)

    g_w_in_parts = _to_shards_call(g_w_main, g_w_small)
    g_wout_parts = g_wout.reshape(N_DEV, d // N_DEV, d)
    g_conv_parts = g_conv.reshape(CONV_WIDTH, N_DEV, conv_cols).transpose(1, 0, 2)
    fold = lambda a, group: a.reshape(-1, group).sum(axis=0)
    small_g = _pack_small([g_n1, g_fw, fold(g_sbw, SB_HEAD_DIM), fold(g_gdw, GDN_HEAD_DIM),
                           g_alog[0, GDN_HEADS:2 * GDN_HEADS], g_dtb[0, GDN_HEADS:2 * GDN_HEADS],
                           loss_p[0, :1]], d)
    sib_w_in, sib_wout, sib_conv = _sibling_send_call("grads_to_sibling", [g_w_in_parts, g_wout_parts, g_conv_parts])
    c_w_in = _pair_sum_call("pair_sum_w_in", g_w_in_parts, sib_w_in, 256)
    c_wout = _pair_sum_call("pair_sum_w_out", g_wout_parts, sib_wout, d // N_DEV)
    c_conv = _pair_sum_call("pair_sum_conv", g_conv_parts, sib_conv, CONV_WIDTH)
    p_w_in, p_wout, p_conv = _chip_exchange_call("grads_to_chips", [c_w_in, c_wout, c_conv])
    (p_small,) = _exchange_call("exchange_small", [small_g], [False])

    r_w_in = _adam_call("adam_w_in", p_w_in, w_in[0], m_w_in[0], v_w_in[0], 256)
    r_wout = _adam_call("adam_w_out", p_wout, w_out[0], m_w_out[0], v_w_out[0], d // N_DEV)
    r_conv = _adam_call("adam_conv", p_conv, gdn_conv_w[0], m_gdn_conv_w[0], v_gdn_conv_w[0], CONV_WIDTH)

    row = lambda a: a.reshape(1, -1)
    n_small = len(_SMALL_ROWS) - 1
    r_small = _adam_small_call(
        p_small,
        [norm1_w, row(final_norm_w), sb_norm_w, gdn_norm_w, gdn_A_log, gdn_dt_bias],
        [m_norm1_w, row(m_final_norm_w), m_sb_norm_w, m_gdn_norm_w, m_gdn_A_log, m_gdn_dt_bias],
        [v_norm1_w, row(v_final_norm_w), v_sb_norm_w, v_gdn_norm_w, v_gdn_A_log, v_gdn_dt_bias])

    def small_out(kind, name):
        out = r_small[1 + kind * n_small + _SMALL_ROWS.index(name)]
        return out.reshape(final_norm_w.shape) if name == "final_norm_w" else out

    def outputs(kind):
        return (small_out(kind, "norm1_w"), r_w_in[kind][None], small_out(kind, "sb_norm_w"), r_conv[kind][None],
                small_out(kind, "gdn_A_log"), small_out(kind, "gdn_dt_bias"), small_out(kind, "gdn_norm_w"),
                r_wout[kind][None], small_out(kind, "final_norm_w"))

    return (r_small[0][0, 0], grad_x[None], *outputs(0), *outputs(1), *outputs(2), *outputs(3))
```

```python
import functools

import jax
import jax.numpy as jnp
from jax import lax
from jax.experimental import pallas as pl
from jax.experimental.pallas import tpu as pltpu

F32 = jnp.float32
MXU_DTYPE = jnp.bfloat16
WIRE_DTYPE = jnp.bfloat16
EXACT = lax.Precision.HIGHEST
EPS = 1e-6
N_DEV = 8
SB_HEAD_DIM = 64
GDN_HEAD_DIM = 128
GDN_HEADS = 4
GDN_CHUNKS_PER_STEP = 4
CHUNK = 64
CONV_WIDTH = 4
LANES = 128
SB_BLOCK = 128
SB_BQ = 256
VMEM_LIMIT_BYTES = 56 * 1024 * 1024

DPROJ_PIECE_OF_SLOT = (0, 1, 2, 4, 5, 6, 3, 7)
DPROJ_SB_SLOT, DPROJ_GDN_SLOT, DPROJ_GATE_SLOT = 0, 3, 6

ADAM_LR = 0.001
ADAM_B1 = 0.9
ADAM_B2 = 0.999
ADAM_EPS = 1e-08
ADAM_WD = 0.01
ADAM_STEP = 10

_NN = (((1,), (0,)), ((), ()))
_NT = (((1,), (1,)), ((), ()))
_TN = (((0,), (0,)), ((), ()))
_BNN = (((2,), (1,)), ((0,), (0,)))
_BNT = (((2,), (2,)), ((0,), (0,)))
_BTN = (((1,), (1,)), ((0,), (0,)))


def _mm(a, b):
    return jnp.dot(a.astype(MXU_DTYPE), b.astype(MXU_DTYPE), preferred_element_type=F32)


def _mm_nt(a, b):
    return lax.dot_general(a.astype(MXU_DTYPE), b.astype(MXU_DTYPE), _NT, preferred_element_type=F32)


def _mm_tn(a, b):
    return lax.dot_general(a.astype(MXU_DTYPE), b.astype(MXU_DTYPE), _TN, preferred_element_type=F32)


def _mx(a, b):
    return jnp.dot(a, b, precision=EXACT, preferred_element_type=F32)


def _mx_nt(a, b):
    return lax.dot_general(a, b, _NT, precision=EXACT, preferred_element_type=F32)


def _mx_tn(a, b):
    return lax.dot_general(a, b, _TN, precision=EXACT, preferred_element_type=F32)


def _split(x):
    hi = x.astype(MXU_DTYPE)
    return hi, (x - hi.astype(F32)).astype(MXU_DTYPE)


def _m3_general(a, b, dims):
    ah, al = _split(a)
    bh, bl = _split(b)
    dot = lambda x, y: lax.dot_general(x, y, dims, preferred_element_type=F32)
    (contract, _), (batch, _) = dims
    free = [ax for ax in range(a.ndim) if ax not in contract and ax not in batch][0]
    m = a.shape[free]
    both = dot(jnp.concatenate([ah, al], axis=free), bh)
    out_axis = len(batch)
    hi_part = lax.slice_in_dim(both, 0, m, axis=out_axis)
    lo_part = lax.slice_in_dim(both, m, 2 * m, axis=out_axis)
    return hi_part + (dot(ah, bl) + lo_part)


def _m3(a, b):
    return _m3_general(a, b, _NN)


def _m3_nt(a, b):
    return _m3_general(a, b, _NT)


def _m3_tn(a, b):
    return _m3_general(a, b, _TN)


def _sigmoid(z):
    return 1.0 / (1.0 + jnp.exp(-z))


def _softplus(z):
    return jnp.maximum(z, 0.0) + jnp.log(1.0 + jnp.exp(-jnp.abs(z)))


def _params(*semantics):
    return pltpu.CompilerParams(dimension_semantics=semantics, vmem_limit_bytes=VMEM_LIMIT_BYTES)


def _inproj_call(x, norm_w, w_main, w_small, w_small_t, tm=256):
    t_len, d = x.shape
    n = w_main.shape[1]
    ns = w_small.shape[1]
    nst = w_small_t.shape[0]

    def body(x_ref, nw_ref, wm_ref, ws_ref, wst_ref, pm_ref, ps_ref, pst_ref, ht_ref, r_ref):
        xv = x_ref[...]
        r = lax.rsqrt(jnp.mean(xv * xv, axis=-1, keepdims=True) + EPS)
        h = xv * r * nw_ref[...]
        hb = h.astype(MXU_DTYPE)
        for n0 in range(0, n, 512):
            pm_ref[:, n0:n0 + 512] = jnp.dot(hb, wm_ref[:, n0:n0 + 512], preferred_element_type=F32)
        ps_ref[...] = jnp.dot(hb, ws_ref[...], preferred_element_type=F32)
        pst_ref[...] = lax.dot_general(wst_ref[...], hb, _NT, preferred_element_type=F32)
        ht_ref[...] = h.T.astype(MXU_DTYPE)
        r_ref[...] = r

    return pl.pallas_call(
        body, name="inproj",
        grid=(t_len // tm,),
        in_specs=[pl.BlockSpec((tm, d), lambda i: (i, 0)),
                  pl.BlockSpec((1, d), lambda i: (0, 0)),
                  pl.BlockSpec((d, n), lambda i: (0, 0)),
                  pl.BlockSpec((d, ns), lambda i: (0, 0)),
                  pl.BlockSpec((nst, d), lambda i: (0, 0))],
        out_specs=[pl.BlockSpec((tm, n), lambda i: (i, 0)),
                   pl.BlockSpec((tm, ns), lambda i: (i, 0)),
                   pl.BlockSpec((nst, tm), lambda i: (0, i)),
                   pl.BlockSpec((d, tm), lambda i: (0, i)),
                   pl.BlockSpec((tm, 1), lambda i: (i, 0))],
        out_shape=[jax.ShapeDtypeStruct((t_len, n), F32),
                   jax.ShapeDtypeStruct((t_len, ns), F32),
                   jax.ShapeDtypeStruct((nst, t_len), F32),
                   jax.ShapeDtypeStruct((d, t_len), MXU_DTYPE),
                   jax.ShapeDtypeStruct((t_len, 1), F32)],
        compiler_params=_params("arbitrary"),
    )(x, norm_w, w_main, w_small, w_small_t)


def _running_sum_mm(x, tri):
    hi = x.astype(MXU_DTYPE)
    lo = (x - hi.astype(F32)).astype(MXU_DTYPE)
    return jnp.dot(hi, tri, preferred_element_type=F32) + jnp.dot(lo, tri, preferred_element_type=F32)


def _sb_iotas():
    row_i = lax.broadcasted_iota(jnp.int32, (SB_BQ, SB_BLOCK), 0)
    col_i = lax.broadcasted_iota(jnp.int32, (SB_BQ, SB_BLOCK), 1)
    sq_r = lax.broadcasted_iota(jnp.int32, (SB_BLOCK, SB_BLOCK), 0)
    sq_c = lax.broadcasted_iota(jnp.int32, (SB_BLOCK, SB_BLOCK), 1)
    return row_i, col_i, sq_r, sq_c


SB_DIAG_BLOCKS = SB_BQ // SB_BLOCK
SB_EXP_FLOOR = -110.0


def _sb_keys_descending(qi, tile, carry, z_bounds, n_heads):
    n_free = SB_DIAG_BLOCKS * qi
    for j in range(SB_DIAG_BLOCKS - 1, -1, -1):
        carry = tile(n_free + j, True, carry, j * SB_BLOCK)

    def largest_exponent(c):
        worst = jnp.max(z_bounds[0] - c[1])
        for h in range(1, n_heads):
            worst = jnp.maximum(worst, jnp.max(z_bounds[h] - c[1 + h]))
        return worst

    def cond(state):
        return (state[0] < n_free) & (state[1] > SB_EXP_FLOOR)

    def body(state):
        c = tile(n_free - 1 - state[0], False, state[2:])
        return (state[0] + 1, largest_exponent(c), *c)

    out = lax.while_loop(cond, body, (jnp.int32(0), largest_exponent(carry), *carry))
    return out[2:], out[0]


def _sb_keys_ascending(qi, n_run, tile, carry):
    n_free = SB_DIAG_BLOCKS * qi
    carry = lax.fori_loop(0, n_run, lambda s, c: tile(n_free - n_run + s, False, c), carry)
    for j in range(SB_DIAG_BLOCKS):
        carry = tile(n_free + j, True, carry, j * SB_BLOCK)
    return carry


def _sb_fwd_call(proj, t_len):
    nq = t_len // SB_BQ
    scale = float(SB_HEAD_DIM) ** -0.5
    n_pairs = 512 // LANES
    per_pair = LANES // SB_HEAD_DIM

    def body(q_ref, k_ref, v_ref, o_ref, st_ref, nrun_ref):
        lane = lax.broadcasted_iota(jnp.int32, (1, LANES), 1)
        row_i, col_i, sq_r, sq_c = _sb_iotas()
        ge = (sq_r >= sq_c).astype(MXU_DTYPE)
        hms = [((lane // SB_HEAD_DIM) == hh).astype(F32) for hh in range(per_pair)]
        k_sq = k_ref[...] * k_ref[...]
        k_norms = [jnp.sqrt(jnp.max(jnp.sum(k_sq * hm, axis=-1, keepdims=True))) * (1.02 * scale) for hm in hms]

        def q_loop(qi, carry):
            r0 = pl.multiple_of(qi * SB_BQ, SB_BQ)
            rows = pl.ds(r0, SB_BQ)
            q_all = q_ref[rows, :]
            qms = [(q_all * (hm * scale)).astype(MXU_DTYPE) for hm in hms]
            z_bounds = [jnp.sqrt(jnp.sum(q_all * q_all * hm, axis=-1, keepdims=True)) * kn
                        for hm, kn in zip(hms, k_norms)]

            def tile(kj, masked, kc, lo=0):
                heads = range(per_pair)
                live = lambda a: a[lo:]
                merge = lambda old, new: new if lo == 0 else jnp.concatenate([old[:lo], new], axis=0)
                acc, cs = live(kc[0]), [live(c) for c in kc[1:]]
                s0 = pl.multiple_of(kj * SB_BLOCK, SB_BLOCK)
                cols = pl.ds(s0, SB_BLOCK)
                kb = k_ref[cols, :].astype(MXU_DTYPE)
                v_all = v_ref[cols, :]
                vms = [(v_all * hms[h]).astype(MXU_DTYPE) for h in heads]
                zs = [lax.dot_general(live(qms[h]), kb, _NT, preferred_element_type=F32) for h in heads]
                sps = [_softplus(z) for z in zs]
                if masked:
                    mask = (live(col_i) + s0) < (live(row_i) + r0)
                    sps = [jnp.where(mask, sp, 0.0) for sp in sps]
                sums = [_running_sum_mm(sp, ge) for sp in sps]
                ws = [jnp.exp(zs[h] - (sums[h] + cs[h])) for h in heads]
                if masked:
                    ws = [jnp.where(mask, w, 0.0) for w in ws]
                for h in heads:
                    acc = acc + jnp.dot(ws[h].astype(MXU_DTYPE), vms[h], preferred_element_type=F32)
                cs = [cs[h] + jnp.sum(sps[h], axis=-1, keepdims=True) for h in heads]
                return tuple(merge(old, new) for old, new in zip(kc, (acc, *cs)))

            zero_col = jnp.zeros((SB_BQ, 1), F32)
            out, n_run = _sb_keys_descending(
                qi, tile, (jnp.zeros((SB_BQ, LANES), F32),) + (zero_col,) * per_pair, z_bounds, per_pair)
            o_ref[rows, :] = out[0]
            for hh in range(per_pair):
                st_ref[hh, rows, :] = out[1 + hh]
            nrun_ref[pl.program_id(0), qi] = n_run
            return carry

        lax.fori_loop(0, nq, q_loop, 0)

    return pl.pallas_call(
        body, name="sb_fwd",
        grid=(n_pairs,),
        in_specs=[pl.BlockSpec((t_len, LANES), lambda p: (0, p)),
                  pl.BlockSpec((t_len, LANES), lambda p: (0, n_pairs + p)),
                  pl.BlockSpec((t_len, LANES), lambda p: (0, 2 * n_pairs + p))],
        out_specs=[pl.BlockSpec((t_len, LANES), lambda p: (0, p)),
                   pl.BlockSpec((per_pair, t_len, 1), lambda p: (p, 0, 0)),
                   pl.BlockSpec(memory_space=pltpu.SMEM)],
        out_shape=[jax.ShapeDtypeStruct((t_len, 512), F32),
                   jax.ShapeDtypeStruct((n_pairs * per_pair, t_len, 1), F32),
                   jax.ShapeDtypeStruct((n_pairs, nq), jnp.int32)],
        compiler_params=_params("arbitrary"),
    )(proj, proj, proj)


def _sb_bwd_call(proj, sp_total, n_run_all, d_o, dproj, t_len):
    nq = t_len // SB_BQ
    scale = float(SB_HEAD_DIM) ** -0.5
    n_pairs = 512 // LANES
    per_pair = LANES // SB_HEAD_DIM

    def body(q_ref, k_ref, v_ref, st_ref, nrun_ref, do_ref, dproj_in_ref, d_ref):
        lane = lax.broadcasted_iota(jnp.int32, (1, LANES), 1)
        row_i, col_i, sq_r, sq_c = _sb_iotas()
        lt = (sq_r < sq_c).astype(MXU_DTYPE)
        le = (sq_r <= sq_c).astype(MXU_DTYPE)
        hms = [((lane // SB_HEAD_DIM) == hh).astype(F32) for hh in range(per_pair)]
        d_ref[1] = jnp.zeros((t_len, LANES), F32)
        d_ref[2] = jnp.zeros((t_len, LANES), F32)

        def q_loop(qi, carry):
            r0 = pl.multiple_of(qi * SB_BQ, SB_BQ)
            rows = pl.ds(r0, SB_BQ)
            q_all, do_all = q_ref[rows, :], do_ref[rows, :]
            qms = [(q_all * (hm * scale)).astype(MXU_DTYPE) for hm in hms]
            doms = [(do_all * hm).astype(MXU_DTYPE) for hm in hms]
            totals = [st_ref[hh, rows, :] for hh in range(per_pair)]

            def tile(kj, masked, kc, lo=0):
                heads = range(per_pair)
                live = lambda a: a[lo:]
                merge = lambda old, new: new if lo == 0 else jnp.concatenate([old[:lo], new], axis=0)
                dq = live(kc[0])
                cls, gls = [live(c) for c in kc[1:1 + per_pair]], [live(c) for c in kc[1 + per_pair:]]
                q_live, do_live = [live(a) for a in qms], [live(a) for a in doms]
                s0 = pl.multiple_of(kj * SB_BLOCK, SB_BLOCK)
                cols = pl.ds(s0, SB_BLOCK)
                k_all, v_all = k_ref[cols, :], v_ref[cols, :]
                kb = k_all.astype(MXU_DTYPE)
                vms = [(v_all * hms[h]).astype(MXU_DTYPE) for h in heads]
                kms = [(k_all * (hms[h] * scale)).astype(MXU_DTYPE) for h in heads]
                zs = [lax.dot_general(q_live[h], kb, _NT, preferred_element_type=F32) for h in heads]
                das = [lax.dot_general(do_live[h], vms[h], _NT, preferred_element_type=F32) for h in heads]
                sp_alls = [_softplus(z) for z in zs]
                sps = sp_alls
                if masked:
                    mask = (live(col_i) + s0) < (live(row_i) + r0)
                    sps = [jnp.where(mask, sp, 0.0) for sp in sp_alls]
                lefts = [_running_sum_mm(sp, lt) for sp in sps]
                ws = [jnp.exp(zs[h] - (live(totals[h]) - cls[h] - lefts[h])) for h in heads]
                if masked:
                    ws = [jnp.where(mask, w, 0.0) for w in ws]
                gs = [das[h] * ws[h] for h in heads]
                g_sums = [_running_sum_mm(g, le) for g in gs]
                dzs = [gs[h] - jnp.exp(zs[h] - sp_alls[h]) * (gls[h] + g_sums[h]) for h in heads]
                if masked:
                    dzs = [jnp.where(mask, dz, 0.0) for dz in dzs]
                dzs = [dz.astype(MXU_DTYPE) for dz in dzs]
                dk_t = jnp.zeros((SB_BLOCK, LANES), F32)
                dv_t = jnp.zeros((SB_BLOCK, LANES), F32)
                for h in heads:
                    dq = dq + jnp.dot(dzs[h], kms[h], preferred_element_type=F32)
                    dk_t = dk_t + lax.dot_general(dzs[h], q_live[h], _TN, preferred_element_type=F32)
                    dv_t = dv_t + lax.dot_general(ws[h].astype(MXU_DTYPE), do_live[h], _TN,
                                                  preferred_element_type=F32)
                d_ref[1, cols, :] += dk_t
                d_ref[2, cols, :] += dv_t
                cls = [cls[h] + jnp.sum(sps[h], axis=-1, keepdims=True) for h in heads]
                gls = [gls[h] + jnp.sum(gs[h], axis=-1, keepdims=True) for h in heads]
                return tuple(merge(old, new) for old, new in zip(kc, (dq, *cls, *gls)))

            zero_col = jnp.zeros((SB_BQ, 1), F32)
            out = _sb_keys_ascending(qi, nrun_ref[pl.program_id(0), qi], tile,
                                     (jnp.zeros((SB_BQ, LANES), F32),) + (zero_col,) * (2 * per_pair))
            d_ref[0, rows, :] = out[0]
            return carry

        lax.fori_loop(0, nq, q_loop, 0)

    col = lambda off: pl.BlockSpec((t_len, LANES), lambda p: (0, off + p))
    return pl.pallas_call(
        body, name="sb_bwd",
        grid=(n_pairs,),
        in_specs=[col(0), col(n_pairs), col(2 * n_pairs),
                  pl.BlockSpec((per_pair, t_len, 1), lambda p: (p, 0, 0)),
                  pl.BlockSpec(memory_space=pltpu.SMEM), col(0), _HBM],
        out_specs=pl.BlockSpec((3, t_len, LANES), lambda p: (DPROJ_SB_SLOT // 3, 0, p)),
        out_shape=jax.ShapeDtypeStruct(dproj.shape, dproj.dtype),
        input_output_aliases={6: 0},
        compiler_params=_params("arbitrary"),
    )(proj, proj, proj, sp_total, n_run_all, d_o, dproj)


def _conv_taps(xin, rows, t_len):
    taps = []
    for i in range(CONV_WIDTH):
        shift = CONV_WIDTH - 1 - i
        if shift == 0:
            taps.append(xin)
        else:
            taps.append(jnp.where(rows >= shift, pltpu.roll(xin, shift, axis=0), 0.0))
    return taps


def _gdn_prep_body_common(x_ref, w_ref, t_len):
    j = pl.program_id(0)
    xin = x_ref[...]
    rows = lax.broadcasted_iota(jnp.int32, (t_len, LANES), 0)
    taps = _conv_taps(xin, rows, t_len)
    pre = taps[0] * w_ref[0:1, :]
    for i in range(1, CONV_WIDTH):
        pre = pre + taps[i] * w_ref[i:i + 1, :]
    sg = _sigmoid(pre)
    act = pre * sg
    is_qk = j < 2 * GDN_HEADS
    nrm = jnp.where(is_qk, lax.rsqrt(jnp.sum(act * act, axis=-1, keepdims=True) + EPS), 1.0)
    sc = jnp.where(j < GDN_HEADS, float(GDN_HEAD_DIM) ** -0.5, 1.0)
    return j, rows, taps, pre, sg, act, is_qk, nrm, sc


def _gdn_prep_call(proj, conv_w, t_len):
    first = 2048 // LANES

    def body(x_ref, w_ref, out_ref):
        _, _, _, _, _, act, _, nrm, sc = _gdn_prep_body_common(x_ref, w_ref, t_len)
        out_ref[...] = act * nrm * sc

    return pl.pallas_call(
        body, name="gdn_prep",
        grid=(3 * GDN_HEADS,),
        in_specs=[pl.BlockSpec((t_len, LANES), lambda j: (0, first + j)),
                  pl.BlockSpec((CONV_WIDTH, LANES), lambda j: (0, j))],
        out_specs=pl.BlockSpec((t_len, LANES), lambda j: (0, j)),
        out_shape=jax.ShapeDtypeStruct((t_len, 3 * 512), F32),
        compiler_params=_params("arbitrary"),
    )(proj, conv_w)


def _gdn_prep_bwd_call(proj, conv_w, d_act3, dproj, t_len):
    first = 2048 // LANES

    def body(x_ref, w_ref, d_ref, dproj_in_ref, dx_ref, dw_ref):
        _, rows, taps, pre, sg, act, is_qk, nrm, sc = _gdn_prep_body_common(x_ref, w_ref, t_len)
        d_out = d_ref[0]
        dn = d_out * sc
        d_norm = nrm * dn - act * (nrm * nrm * nrm) * jnp.sum(dn * act, axis=-1, keepdims=True)
        d_act = jnp.where(is_qk, d_norm, d_out)
        d_pre = d_act * sg * (1.0 + pre * (1.0 - sg))
        dx = d_pre * w_ref[CONV_WIDTH - 1:CONV_WIDTH, :]
        dw_ref[CONV_WIDTH - 1:CONV_WIDTH, :] = jnp.sum(d_pre * taps[CONV_WIDTH - 1], axis=0, keepdims=True)
        for i in range(CONV_WIDTH - 1):
            shift = CONV_WIDTH - 1 - i
            up = jnp.where(rows < t_len - shift, pltpu.roll(d_pre, t_len - shift, axis=0), 0.0)
            dx = dx + up * w_ref[i:i + 1, :]
            dw_ref[i:i + 1, :] = jnp.sum(d_pre * taps[i], axis=0, keepdims=True)
        dx_ref[0] = dx

    return pl.pallas_call(
        body, name="gdn_prep_bwd",
        grid=(3 * GDN_HEADS,),
        in_specs=[pl.BlockSpec((t_len, LANES), lambda j: (0, first + j)),
                  pl.BlockSpec((CONV_WIDTH, LANES), lambda j: (0, j)),
                  pl.BlockSpec((1, t_len, LANES), lambda j: (j // GDN_HEADS, 0, j % GDN_HEADS)), _HBM],
        out_specs=[pl.BlockSpec((1, t_len, LANES), lambda j: (DPROJ_GDN_SLOT + j // GDN_HEADS, 0, j % GDN_HEADS)),
                   pl.BlockSpec((CONV_WIDTH, LANES), lambda j: (0, j))],
        out_shape=[jax.ShapeDtypeStruct(dproj.shape, dproj.dtype),
                   jax.ShapeDtypeStruct((CONV_WIDTH, 3 * 512), F32)],
        input_output_aliases={3: 0},
        compiler_params=_params("arbitrary"),
    )(proj, conv_w, d_act3, dproj)


def _chunk_cumsum_matrix():
    r = lax.broadcasted_iota(jnp.int32, (LANES, LANES), 0)
    c = lax.broadcasted_iota(jnp.int32, (LANES, LANES), 1)
    return ((r <= c) & ((r // CHUNK) == (c // CHUNK))).astype(F32)


def _gdn_gates_call(ps, pst, alog_l, dtb_l, alog_c, dtb_c, t_len):
    def body(ps_ref, pst_ref, al_ref, dl_ref, ac_ref, dc_ref, beta_ref, gcol_ref, grow_ref):
        upper = _chunk_cumsum_matrix()
        lower = upper.T
        psv = ps_ref[...]
        beta_ref[...] = _sigmoid(psv)
        g_l = -jnp.exp(al_ref[...]) * _softplus(psv + dl_ref[...])
        g_r = -jnp.exp(ac_ref[...]) * _softplus(pst_ref[...] + dc_ref[...])
        for w in range(t_len // LANES):
            sl = slice(w * LANES, (w + 1) * LANES)
            gcol_ref[sl, :] = _mx(lower, g_l[sl, :])
            grow_ref[:, sl] = _mx(g_r[:, sl], upper)

    vm = pl.BlockSpec(memory_space=pltpu.VMEM)
    return pl.pallas_call(
        body, name="gdn_gates",
        in_specs=[vm] * 6, out_specs=[vm] * 3,
        out_shape=[jax.ShapeDtypeStruct((t_len, LANES), F32),
                   jax.ShapeDtypeStruct((t_len, LANES), F32),
                   jax.ShapeDtypeStruct((8, t_len), F32)],
        compiler_params=pltpu.CompilerParams(vmem_limit_bytes=VMEM_LIMIT_BYTES),
    )(ps, pst, alog_l, dtb_l, alog_c, dtb_c)


def _gdn_gates_bwd_call(ps, alog_l, dtb_l, d_l, t_len):
    def body(ps_ref, al_ref, dl_ref, d_ref, dps_ref, gal_ref, gdt_ref):
        lane = lax.broadcasted_iota(jnp.int32, (1, LANES), 1)
        psv = ps_ref[...]
        dv = d_ref[...]
        beta = _sigmoid(psv)
        ea = jnp.exp(al_ref[...])
        arg = psv + dl_ref[...]
        g = -ea * _softplus(arg)
        d_a = dv * (-ea) * _sigmoid(arg)
        is_a = (lane >= GDN_HEADS) & (lane < 2 * GDN_HEADS)
        dps_ref[...] = jnp.where(lane < GDN_HEADS, dv * beta * (1.0 - beta), jnp.where(is_a, d_a, 0.0))
        gdt_ref[...] = jnp.where(is_a, jnp.sum(d_a, axis=0, keepdims=True), 0.0)
        gal_ref[...] = jnp.where(is_a, jnp.sum(dv * g, axis=0, keepdims=True), 0.0)

    vm = pl.BlockSpec(memory_space=pltpu.VMEM)
    return pl.pallas_call(
        body, name="gdn_gates_bwd",
        in_specs=[vm] * 4, out_specs=[vm] * 3,
        out_shape=[jax.ShapeDtypeStruct((t_len, LANES), F32),
                   jax.ShapeDtypeStruct((1, LANES), F32),
                   jax.ShapeDtypeStruct((1, LANES), F32)],
        compiler_params=pltpu.CompilerParams(vmem_limit_bytes=VMEM_LIMIT_BYTES),
    )(ps, alog_l, dtb_l, d_l)


def _bm(a, b):
    return _m3_general(a, b, _BNN)


def _bm_nt(a, b):
    return _m3_general(a, b, _BNT)


def _bm_tn(a, b):
    return _m3_general(a, b, _BTN)


def _heads_of(ref, rows):
    return jnp.stack([ref[rows, h * GDN_HEAD_DIM:(h + 1) * GDN_HEAD_DIM] for h in range(GDN_HEADS)])


def _chunk_terms(q_ref, k_ref, v_ref, b_ref, gc_ref, gr_ref, c, incl, strict):
    r0 = pl.multiple_of(c * CHUNK, CHUNK)
    rows = pl.ds(r0, CHUNK)
    q, k, v = _heads_of(q_ref, rows), _heads_of(k_ref, rows), _heads_of(v_ref, rows)
    lane_ids = lax.broadcasted_iota(jnp.int32, (1, LANES), 1)
    pick = lambda slab, first: jnp.stack([jnp.sum(jnp.where(lane_ids == first + h, slab, 0.0), axis=-1, keepdims=True)
                                          for h in range(GDN_HEADS)])
    b = pick(b_ref[rows, :], 0)
    gc = pick(gc_ref[rows, :], GDN_HEADS)
    gr = gr_ref[:, c]
    dm = jnp.where(incl, jnp.exp(jnp.where(incl, gc - gr, 0.0)), 0.0)
    kb = k * b
    vb = v * b
    e = jnp.exp(gc)
    kk_qk = _bm_nt(jnp.concatenate([kb, q], axis=1), k)
    a = jnp.where(strict, kk_qk[:, :CHUNK] * dm, 0.0)
    p = jnp.where(incl, kk_qk[:, CHUNK:] * dm, 0.0)
    gl = gc[:, CHUNK - 1:CHUNK, :]
    eg = jnp.exp(gl - gc)
    return rows, q, k, v, b, gc, dm, kb, vb, e, a, p, gl, eg


def _unit_lower_inverse(a, eye):
    x = -a
    tm = eye + x
    xp = _bm(x, x)
    for _ in range(4):
        both = _bm(jnp.concatenate([xp, tm], axis=1), xp)
        tm = tm + both[:, CHUNK:]
        xp = both[:, :CHUNK]
    return tm + _bm(tm, xp)


def _gdn_specs(t_len, n_chunks, reverse):
    cps = GDN_CHUNKS_PER_STEP
    steps = n_chunks // cps
    at = (lambda g: steps - 1 - g) if reverse else (lambda g: g)
    rows_blk = lambda width, part=0: pl.BlockSpec((cps * CHUNK, width), lambda g: (at(g), part))
    gate_r = pl.BlockSpec((GDN_HEADS, cps, 1, CHUNK), lambda g: (0, at(g), 0, 0))
    per_chunk = lambda r, c: pl.BlockSpec((GDN_HEADS, cps, r, c), lambda g: (0, at(g), 0, 0))
    return cps, steps, rows_blk, gate_r, per_chunk


def _gdn_fwd_call(gact, beta_c, gam_c, gam_r, t_len):
    n_chunks = t_len // CHUNK
    dk = GDN_HEAD_DIM
    width = GDN_HEADS * dk
    cps, steps, rows_blk, gate_r, per_chunk = _gdn_specs(t_len, n_chunks, False)

    def body(q_ref, k_ref, v_ref, b_ref, gc_ref, gr_ref, o_ref, s_ref, t_ref, state_ref):
        row = lax.broadcasted_iota(jnp.int32, (CHUNK, CHUNK), 0)
        col = lax.broadcasted_iota(jnp.int32, (CHUNK, CHUNK), 1)
        incl, strict = row >= col, row > col
        eye = (row == col).astype(F32)

        @pl.when(pl.program_id(0) == 0)
        def _():
            state_ref[...] = jnp.zeros_like(state_ref)

        def chunk(c, carry):
            rows, q, k, v, b, gc, dm, kb, vb, e, a, p, gl, eg = _chunk_terms(
                q_ref, k_ref, v_ref, b_ref, gc_ref, gr_ref, c, incl, strict)
            s = state_ref[...]
            tm = _unit_lower_inverse(a, eye)
            uw = _bm(tm, jnp.concatenate([vb, kb * e], axis=2))
            u, w = uw[:, :, :dk], uw[:, :, dk:]
            ws_qs = _bm(jnp.concatenate([w, q * e], axis=1), s)
            vn = u - ws_qs[:, :CHUNK]
            o = ws_qs[:, CHUNK:] + _bm(p, vn)
            for h in range(GDN_HEADS):
                o_ref[rows, h * dk:(h + 1) * dk] = o[h]
            s_ref[:, c] = s
            t_ref[:, c] = tm
            state_ref[...] = s * jnp.exp(gl) + _bm_tn(k * eg, vn)
            return carry

        lax.fori_loop(0, cps, chunk, 0)

    return pl.pallas_call(
        body, name="gdn_fwd",
        grid=(steps,),
        in_specs=[rows_blk(width, 0), rows_blk(width, 1), rows_blk(width, 2), rows_blk(LANES), rows_blk(LANES), gate_r],
        out_specs=[rows_blk(width), per_chunk(dk, dk), per_chunk(CHUNK, CHUNK)],
        out_shape=[jax.ShapeDtypeStruct((t_len, width), F32),
                   jax.ShapeDtypeStruct((GDN_HEADS, n_chunks, dk, dk), F32),
                   jax.ShapeDtypeStruct((GDN_HEADS, n_chunks, CHUNK, CHUNK), F32)],
        scratch_shapes=[pltpu.VMEM((GDN_HEADS, dk, dk), F32)],
        compiler_params=_params("arbitrary"),
    )(gact, gact, gact, beta_c, gam_c, gam_r)


def _gdn_bwd_call(gact, beta_c, gam_c, gam_r, s_all, t_all, d_o, t_len):
    n_chunks = t_len // CHUNK
    dk = GDN_HEAD_DIM
    width = GDN_HEADS * dk
    cps, steps, rows_blk, gate_r, per_chunk = _gdn_specs(t_len, n_chunks, True)

    def body(q_ref, k_ref, v_ref, b_ref, gc_ref, gr_ref, s_ref, t_ref, do_ref, d_ref, dgate_ref, dstate_ref):
        row = lax.broadcasted_iota(jnp.int32, (CHUNK, CHUNK), 0)
        col = lax.broadcasted_iota(jnp.int32, (CHUNK, CHUNK), 1)
        incl, strict = row >= col, row > col
        upper = jnp.broadcast_to((row <= col).astype(F32), (GDN_HEADS, CHUNK, CHUNK))
        ones = jnp.ones((GDN_HEADS, CHUNK, LANES), F32)
        last_row = lax.broadcasted_iota(jnp.int32, (CHUNK, 1), 0) == CHUNK - 1
        lane_ids = lax.broadcasted_iota(jnp.int32, (1, LANES), 1)
        rsum = lambda m: jnp.sum(m, axis=-1, keepdims=True)
        total = lambda m: jnp.sum(rsum(m), axis=1, keepdims=True)

        @pl.when(pl.program_id(0) == 0)
        def _():
            dstate_ref[...] = jnp.zeros_like(dstate_ref)

        def chunk(step, carry):
            c = cps - 1 - step
            rows, q, k, v, b, gc, dm, kb, vb, e, a, p, gl, eg = _chunk_terms(
                q_ref, k_ref, v_ref, b_ref, gc_ref, gr_ref, c, incl, strict)
            ds = dstate_ref[...]
            s = s_ref[:, c]
            tm = t_ref[:, c]
            d_out = _heads_of(do_ref, rows)
            el = jnp.exp(gl)
            kbe = kb * e
            uw = _bm(tm, jnp.concatenate([vb, kbe], axis=2))
            u, w = uw[:, :, :dk], uw[:, :, dk:]
            vn = u - _bm(w, s)
            qe = q * e
            kd = k * eg

            d_vn = _bm_tn(p, d_out) + _bm(kd, ds)
            on_s = _bm_nt(jnp.concatenate([d_out, d_vn], axis=1), s)
            d_qe, d_w = on_s[:, :CHUNK], -on_s[:, CHUNK:]
            d_p = jnp.where(incl, _bm_nt(d_out, vn), 0.0)
            dstate_ref[...] = el * ds + _bm_tn(jnp.concatenate([qe, -w], axis=1),
                                               jnp.concatenate([d_out, d_vn], axis=1))
            d_kd = _bm_nt(vn, ds)
            d_both = _bm_tn(tm, jnp.concatenate([d_vn, d_w], axis=2))
            d_vb, d_kbe = d_both[:, :, :dk], d_both[:, :, dk:]
            d_a = -jnp.where(strict, _bm_nt(d_both, uw), 0.0)
            m = d_a * dm
            n = d_p * dm
            on_k = _bm(jnp.concatenate([m, n], axis=1), k)
            d_kb = on_k[:, :CHUNK] + d_kbe * e
            d_q = on_k[:, CHUNK:] + d_qe * e
            d_k = (_bm_tn(jnp.concatenate([m, n], axis=1), jnp.concatenate([kb, q], axis=1))
                   + d_kd * eg + b * d_kb)
            d_v = b * d_vb
            r = d_a * a + d_p * p
            kd_term = rsum(d_kd * kd)
            d_gl = total(ds * s) * el + jnp.sum(kd_term, axis=1, keepdims=True)
            d_gam = (rsum(r) - _bm_tn(r, ones)[:, :, 0:1] + rsum(d_qe * qe) + rsum(d_kbe * kbe) - kd_term
                     + jnp.where(last_row, d_gl, 0.0))
            d_beta = rsum(d_kb * k) + rsum(d_vb * v)
            d_g = _bm(upper, d_gam * ones)[:, :, 0:1]
            gates = jnp.zeros((CHUNK, LANES), F32)
            for h in range(GDN_HEADS):
                lanes = slice(h * dk, (h + 1) * dk)
                d_ref[0, rows, lanes] = d_q[h]
                d_ref[1, rows, lanes] = d_k[h]
                d_ref[2, rows, lanes] = d_v[h]
                gates = gates + (jnp.where(lane_ids == h, d_beta[h], 0.0)
                                 + jnp.where(lane_ids == GDN_HEADS + h, d_g[h], 0.0))
            dgate_ref[rows, :] = gates
            return carry

        lax.fori_loop(0, cps, chunk, 0)

    d_spec = pl.BlockSpec((3, cps * CHUNK, width), lambda g: (0, steps - 1 - g, 0))
    return pl.pallas_call(
        body, name="gdn_bwd",
        grid=(steps,),
        in_specs=[rows_blk(width, 0), rows_blk(width, 1), rows_blk(width, 2), rows_blk(LANES), rows_blk(LANES), gate_r,
                  per_chunk(dk, dk), per_chunk(CHUNK, CHUNK), rows_blk(width)],
        out_specs=[d_spec, rows_blk(LANES)],
        out_shape=[jax.ShapeDtypeStruct((3, t_len, width), F32),
                   jax.ShapeDtypeStruct((t_len, LANES), F32)],
        scratch_shapes=[pltpu.VMEM((GDN_HEADS, dk, dk), F32)],
        compiler_params=_params("arbitrary"),
    )(gact, gact, gact, beta_c, gam_c, gam_r, s_all, t_all, d_o)


def _group_matrix(width, group):
    r = lax.broadcasted_iota(jnp.int32, (width, width), 0)
    c = lax.broadcasted_iota(jnp.int32, (width, width), 1)
    return ((r // group) == (c // group)).astype(F32)


def _post_call(o_sb, o_gd, proj, x, target, w_out, sbw, gdw, fw, tm=256):
    t_len, d = x.shape
    half = 512
    zsb_blk = 1536 // half
    zgd_blk = 3584 // half

    def body(osb_ref, ogd_ref, zsb_ref, zgd_ref, x_ref, tg_ref, wo_ref, sbw_ref, gdw_ref, fw_ref,
             dx2_ref, dosb_ref, dogd_ref, dz_ref, loss_ref, gfw_ref, gsb_ref, ggd_ref, gwo_ref):
        step = pl.program_id(0)

        @pl.when(step == 0)
        def _():
            loss_ref[...] = jnp.zeros_like(loss_ref)
            gfw_ref[...] = jnp.zeros_like(gfw_ref)
            gsb_ref[...] = jnp.zeros_like(gsb_ref)
            ggd_ref[...] = jnp.zeros_like(ggd_ref)
            gwo_ref[...] = jnp.zeros_like(gwo_ref)

        def head_forward(o, z, w, gmat, inv):
            r = lax.rsqrt(_running_sum_mm(o * o, gmat) * inv + EPS)
            nrm = o * r * w
            sg = _sigmoid(z)
            return r, nrm, sg, nrm * (z * sg)

        def head_backward(d_m, o, z, w, gmat, inv, r, nrm, sg):
            d_n = d_m * (z * sg)
            d_z = d_m * nrm * (sg * (1.0 + z * (1.0 - sg)))
            dnw = d_n * w
            d_o = r * dnw - o * (r * r * r) * (_running_sum_mm(dnw * o, gmat) * inv)
            return d_o, d_z, jnp.sum(d_n * o * r, axis=0, keepdims=True)

        g_sb = _group_matrix(half, SB_HEAD_DIM).astype(MXU_DTYPE)
        g_gd = _group_matrix(half, GDN_HEAD_DIM).astype(MXU_DTYPE)
        osb, ogd, zsb, zgd = osb_ref[...], ogd_ref[...], zsb_ref[...], zgd_ref[...]
        sbw_v, gdw_v = sbw_ref[...], gdw_ref[...]
        r_sb, n_sb, sg_sb, m_sb = head_forward(osb, zsb, sbw_v, g_sb, 1.0 / SB_HEAD_DIM)
        r_gd, n_gd, sg_gd, m_gd = head_forward(ogd, zgd, gdw_v, g_gd, 1.0 / GDN_HEAD_DIM)
        mixed = jnp.concatenate([m_sb, m_gd], axis=1).astype(MXU_DTYPE)
        wo = wo_ref[...]
        x2 = x_ref[...] + jnp.dot(mixed, wo, preferred_element_type=F32)
        r2 = lax.rsqrt(jnp.mean(x2 * x2, axis=-1, keepdims=True) + EPS)
        fw_v = fw_ref[...]
        err = x2 * r2 * fw_v - tg_ref[...]
        loss_ref[...] += 0.5 * jnp.sum(jnp.sum(err * err, axis=-1, keepdims=True) * (1.0 / d))
        dy = err * (1.0 / d)
        gg = dy * fw_v
        dx2 = r2 * gg - x2 * ((r2 * r2 * r2) * jnp.mean(gg * x2, axis=-1, keepdims=True))
        gfw_ref[...] += jnp.sum(dy * x2 * r2, axis=0, keepdims=True)
        dx2_ref[...] = dx2
        dx2b = dx2.astype(MXU_DTYPE)
        d_mixed = lax.dot_general(dx2b, wo, _NT, preferred_element_type=F32)
        gwo_ref[...] += lax.dot_general(mixed, dx2b, _TN, preferred_element_type=F32)
        d_osb, d_zsb, gsb = head_backward(d_mixed[:, :half], osb, zsb, sbw_v, g_sb, 1.0 / SB_HEAD_DIM, r_sb, n_sb, sg_sb)
        d_ogd, d_zgd, ggd = head_backward(d_mixed[:, half:], ogd, zgd, gdw_v, g_gd, 1.0 / GDN_HEAD_DIM, r_gd, n_gd, sg_gd)
        dosb_ref[...] = d_osb
        dogd_ref[...] = d_ogd
        dz_ref[0] = d_zsb
        dz_ref[1] = d_zgd
        gsb_ref[...] += gsb
        ggd_ref[...] += ggd

    row_blk = lambda w: pl.BlockSpec((tm, w), lambda i: (i, 0))
    fixed = lambda r, w: pl.BlockSpec((r, w), lambda i: (0, 0))
    return pl.pallas_call(
        body, name="post",
        grid=(t_len // tm,),
        in_specs=[row_blk(half), row_blk(half),
                  pl.BlockSpec((tm, half), lambda i: (i, zsb_blk)),
                  pl.BlockSpec((tm, half), lambda i: (i, zgd_blk)),
                  row_blk(d), row_blk(d), fixed(d, d), fixed(1, half), fixed(1, half), fixed(1, d)],
        out_specs=[row_blk(d), row_blk(half), row_blk(half),
                   pl.BlockSpec((2, tm, half), lambda i: (DPROJ_GATE_SLOT // 2, i, 0)),
                   fixed(1, LANES), fixed(1, d), fixed(1, half), fixed(1, half), fixed(d, d)],
        out_shape=[jax.ShapeDtypeStruct((t_len, d), F32)] + [jax.ShapeDtypeStruct((t_len, half), F32)] * 2
                  + [jax.ShapeDtypeStruct((len(DPROJ_PIECE_OF_SLOT), t_len, half), F32),
                     jax.ShapeDtypeStruct((1, LANES), F32), jax.ShapeDtypeStruct((1, d), F32),
                     jax.ShapeDtypeStruct((1, half), F32), jax.ShapeDtypeStruct((1, half), F32),
                     jax.ShapeDtypeStruct((d, d), F32)],
        compiler_params=_params("arbitrary"),
    )(o_sb, o_gd, proj, proj, x, target, w_out, sbw, gdw, fw)


def _piece_of_slot(s):
    return jnp.where(s < DPROJ_GDN_SLOT, s, jnp.where(s < DPROJ_GATE_SLOT, s + 1,
                                                     jnp.where(s == DPROJ_GATE_SLOT, 3, 7)))


def _gw_in_call(h_t, dproj8):
    d, t_len = h_t.shape
    n_piece, _, pw = dproj8.shape

    def body(ht_ref, dp_ref, gw_ref):
        gw_ref[...] = jnp.dot(ht_ref[...], dp_ref[0].astype(MXU_DTYPE), preferred_element_type=F32)

    return pl.pallas_call(
        body, name="gw_in",
        grid=(n_piece,),
        in_specs=[pl.BlockSpec((d, t_len), lambda s: (0, 0)),
                  pl.BlockSpec((1, t_len, pw), lambda s: (s, 0, 0))],
        out_specs=pl.BlockSpec((d, pw), lambda s: (0, _piece_of_slot(s))),
        out_shape=jax.ShapeDtypeStruct((d, n_piece * pw), F32),
        compiler_params=_params("arbitrary"),
    )(h_t, dproj8)


def _gw_small_call(h_t, dsmall, tm=512):
    d, t_len = h_t.shape
    ns = dsmall.shape[1]

    def body(ht_ref, dp_ref, gw_ref):
        @pl.when(pl.program_id(0) == 0)
        def _():
            gw_ref[...] = jnp.zeros_like(gw_ref)

        gw_ref[...] += jnp.dot(ht_ref[...], dp_ref[...].astype(MXU_DTYPE), preferred_element_type=F32)

    return pl.pallas_call(
        body, name="gw_small",
        grid=(t_len // tm,),
        in_specs=[pl.BlockSpec((d, tm), lambda t: (0, t)),
                  pl.BlockSpec((tm, ns), lambda t: (t, 0))],
        out_specs=pl.BlockSpec((d, ns), lambda t: (0, 0)),
        out_shape=jax.ShapeDtypeStruct((d, ns), F32),
        compiler_params=_params("arbitrary"),
    )(h_t, dsmall)


def _dx_call(dproj8, dsmall, w_main, w_small, x, r, dx2, norm_w, tm=256):
    t_len, d = x.shape
    n_piece, _, pw = dproj8.shape
    ns = dsmall.shape[1]

    def body(dp_ref, ds_ref, wm_ref, ws_ref, x_ref, r_ref, dx2_ref, nw_ref, gx_ref, gnw_ref):
        @pl.when(pl.program_id(0) == 0)
        def _():
            gnw_ref[...] = jnp.zeros_like(gnw_ref)

        dh = lax.dot_general(ds_ref[...].astype(MXU_DTYPE), ws_ref[...], _NT, preferred_element_type=F32)
        for s, p in enumerate(DPROJ_PIECE_OF_SLOT):
            dh = dh + lax.dot_general(dp_ref[s].astype(MXU_DTYPE), wm_ref[:, p * pw:(p + 1) * pw], _NT,
                                      preferred_element_type=F32)
        xv, rv = x_ref[...], r_ref[...]
        dn = dh * nw_ref[...]
        gx_ref[...] = dx2_ref[...] + rv * dn - xv * ((rv * rv * rv) * jnp.mean(dn * xv, axis=-1, keepdims=True))
        gnw_ref[...] += jnp.sum(dh * xv * rv, axis=0, keepdims=True)

    return pl.pallas_call(
        body, name="dx",
        grid=(t_len // tm,),
        in_specs=[pl.BlockSpec((n_piece, tm, pw), lambda i: (0, i, 0)),
                  pl.BlockSpec((tm, ns), lambda i: (i, 0)),
                  pl.BlockSpec((d, n_piece * pw), lambda i: (0, 0)),
                  pl.BlockSpec((d, ns), lambda i: (0, 0)),
                  pl.BlockSpec((tm, d), lambda i: (i, 0)),
                  pl.BlockSpec((tm, 1), lambda i: (i, 0)),
                  pl.BlockSpec((tm, d), lambda i: (i, 0)),
                  pl.BlockSpec((1, d), lambda i: (0, 0))],
        out_specs=[pl.BlockSpec((tm, d), lambda i: (i, 0)),
                   pl.BlockSpec((1, d), lambda i: (0, 0))],
        out_shape=[jax.ShapeDtypeStruct((t_len, d), F32), jax.ShapeDtypeStruct((1, d), F32)],
        compiler_params=_params("arbitrary"),
    )(dproj8, dsmall, w_main, w_small, x, r, dx2, norm_w)


def _exchange_call(name, srcs, per_peer):
    n = len(srcs)
    out_shapes = [jax.ShapeDtypeStruct(s.shape if pp else (N_DEV,) + s.shape, s.dtype) for s, pp in zip(srcs, per_peer)]

    def body(*refs):
        src_refs, out_refs = refs[:n], refs[n:2 * n]
        send_sems, recv_sems, local_sems = refs[2 * n:]
        x, y, c = lax.axis_index("x"), lax.axis_index("y"), lax.axis_index("c")
        me = 4 * x + 2 * y + c
        copies = []
        for a in range(n):
            mine = src_refs[a].at[me] if per_peer[a] else src_refs[a]
            local = pltpu.make_async_copy(mine, out_refs[a].at[me], local_sems.at[a])
            local.start()
            copies.append(local)
        remote = []
        for k in range(1, N_DEV):
            kx, ky, kc = (k >> 2) & 1, (k >> 1) & 1, k & 1
            px = 1 - x if kx else x
            py = 1 - y if ky else y
            pc = 1 - c if kc else c
            peer = 4 * px + 2 * py + pc
            for a in range(n):
                sem = a * (N_DEV - 1) + (k - 1)
                src = src_refs[a].at[peer] if per_peer[a] else src_refs[a]
                cp = pltpu.make_async_remote_copy(
                    src_ref=src, dst_ref=out_refs[a].at[me],
                    send_sem=send_sems.at[sem], recv_sem=recv_sems.at[sem],
                    device_id=(px, py, pc), device_id_type=pl.DeviceIdType.MESH)
                cp.start()
                remote.append(cp)
        for cp in remote:
            cp.wait_send()
        for cp in remote:
            cp.wait_recv()
        for cp in copies:
            cp.wait()

    hbm = pl.BlockSpec(memory_space=pl.ANY)
    return pl.pallas_call(
        body, name=name,
        in_specs=[hbm] * n, out_specs=[hbm] * n, out_shape=out_shapes,
        scratch_shapes=[pltpu.SemaphoreType.DMA((n * (N_DEV - 1),)),
                        pltpu.SemaphoreType.DMA((n * (N_DEV - 1),)),
                        pltpu.SemaphoreType.DMA((n,))],
    )(*srcs)


N_CHIPS = 4
_HBM = pl.BlockSpec(memory_space=pl.ANY)
_MESH = pl.DeviceIdType.MESH


def _gather_call(name, srcs):
    n = len(srcs)
    per = N_DEV - 1

    def body(*refs):
        src_refs, out_refs = refs[:n], refs[n:2 * n]
        send_sems, recv_sems, local_sems = refs[2 * n:]
        x, y, c = lax.axis_index("x"), lax.axis_index("y"), lax.axis_index("c")
        me, sibling = (x, y, c), (x, y, 1 - c)
        chips = [(1 - x, y), (x, 1 - y), (1 - x, 1 - y)]
        slot = lambda px, py, pc: 4 * px + 2 * py + pc

        def copy(a, k, block, to, from_src=False):
            rows = out_refs[a].at[slot(*block)]
            return pltpu.make_async_remote_copy(
                src_ref=src_refs[a] if from_src else rows, dst_ref=rows,
                send_sem=send_sems.at[a * per + k], recv_sem=recv_sems.at[a * per + k],
                device_id=to, device_id_type=_MESH)

        local = [pltpu.make_async_copy(src_refs[a], out_refs[a].at[slot(*me)], local_sems.at[a]) for a in range(n)]
        for cp in local:
            cp.start()
        first = []
        for a in range(n):
            first.append(copy(a, 0, me, sibling, True))
            first += [copy(a, 1 + j, me, (*chip, c), True) for j, chip in enumerate(chips)]
        for cp in first:
            cp.start()
        passed = []
        for j, chip in enumerate(chips):
            for a in range(n):
                copy(a, 1 + j, (*chip, c), me).wait_recv()
                fwd = copy(a, 4 + j, (*chip, c), sibling)
                fwd.start()
                passed.append(fwd)
        for a in range(n):
            copy(a, 0, sibling, me).wait_recv()
            for j, chip in enumerate(chips):
                copy(a, 4 + j, (*chip, 1 - c), me).wait_recv()
        for cp in first + passed:
            cp.wait_send()
        for cp in local:
            cp.wait()

    return pl.pallas_call(
        body, name=name,
        in_specs=[_HBM] * n, out_specs=[_HBM] * n,
        out_shape=[jax.ShapeDtypeStruct((N_DEV,) + s.shape, s.dtype) for s in srcs],
        scratch_shapes=[pltpu.SemaphoreType.DMA((n * per,)), pltpu.SemaphoreType.DMA((n * per,)),
                        pltpu.SemaphoreType.DMA((n,))],
    )(*srcs)


def _sibling_send_call(name, srcs):
    n = len(srcs)

    def body(*refs):
        src_refs, out_refs = refs[:n], refs[n:2 * n]
        send_sems, recv_sems = refs[2 * n:]
        x, y, c = lax.axis_index("x"), lax.axis_index("y"), lax.axis_index("c")
        copies = []
        for a in range(n):
            for ch in range(N_CHIPS):
                copies.append(pltpu.make_async_remote_copy(
                    src_ref=src_refs[a].at[2 * ch + (1 - c)], dst_ref=out_refs[a].at[ch],
                    send_sem=send_sems.at[a * N_CHIPS + ch], recv_sem=recv_sems.at[a * N_CHIPS + ch],
                    device_id=(x, y, 1 - c), device_id_type=_MESH))
        for cp in copies:
            cp.start()
        for cp in copies:
            cp.wait_send()
        for cp in copies:
            cp.wait_recv()

    return pl.pallas_call(
        body, name=name,
        in_specs=[_HBM] * n, out_specs=[_HBM] * n,
        out_shape=[jax.ShapeDtypeStruct((N_CHIPS,) + s.shape[1:], s.dtype) for s in srcs],
        scratch_shapes=[pltpu.SemaphoreType.DMA((n * N_CHIPS,)), pltpu.SemaphoreType.DMA((n * N_CHIPS,))],
    )(*srcs)


def _pair_sum_call(name, parts, from_sibling, tr):
    _, rows, cols = parts.shape

    def body(p_ref, s_ref, o_ref):
        o_ref[...] = (p_ref[...].astype(F32) + s_ref[...].astype(F32)).astype(o_ref.dtype)

    return pl.pallas_call(
        body, name=name,
        grid=(N_CHIPS, rows // tr),
        in_specs=[pl.BlockSpec((1, tr, cols), lambda ch, i: (2 * ch + lax.axis_index("c"), i, 0)),
                  pl.BlockSpec((1, tr, cols), lambda ch, i: (ch, i, 0))],
        out_specs=pl.BlockSpec((1, tr, cols), lambda ch, i: (ch, i, 0)),
        out_shape=jax.ShapeDtypeStruct((N_CHIPS, rows, cols), WIRE_DTYPE),
        compiler_params=_params("arbitrary", "arbitrary"),
    )(parts, from_sibling)


def _chip_exchange_call(name, srcs):
    n = len(srcs)
    per = N_CHIPS - 1

    def body(*refs):
        src_refs, out_refs = refs[:n], refs[n:2 * n]
        send_sems, recv_sems, local_sems = refs[2 * n:]
        x, y, c = lax.axis_index("x"), lax.axis_index("y"), lax.axis_index("c")
        mine = 2 * x + y
        chips = [(1 - x, y), (x, 1 - y), (1 - x, 1 - y)]
        local = [pltpu.make_async_copy(src_refs[a].at[mine], out_refs[a].at[mine], local_sems.at[a]) for a in range(n)]
        for cp in local:
            cp.start()
        remote = []
        for a in range(n):
            for j, (px, py) in enumerate(chips):
                remote.append(pltpu.make_async_remote_copy(
                    src_ref=src_refs[a].at[2 * px + py], dst_ref=out_refs[a].at[mine],
                    send_sem=send_sems.at[a * per + j], recv_sem=recv_sems.at[a * per + j],
                    device_id=(px, py, c), device_id_type=_MESH))
        for cp in remote:
            cp.start()
        for cp in remote:
            cp.wait_send()
        for cp in remote:
            cp.wait_recv()
        for cp in local:
            cp.wait()

    return pl.pallas_call(
        body, name=name,
        in_specs=[_HBM] * n, out_specs=[_HBM] * n,
        out_shape=[jax.ShapeDtypeStruct(s.shape, s.dtype) for s in srcs],
        scratch_shapes=[pltpu.SemaphoreType.DMA((n * per,)), pltpu.SemaphoreType.DMA((n * per,)),
                        pltpu.SemaphoreType.DMA((n,))],
    )(*srcs)


def _adam_call(name, parts, w, m, v, tr):
    rows, cols = w.shape
    n_slots = parts.shape[0]

    def body(p_ref, w_ref, m_ref, v_ref, g_ref, d_ref, nm_ref, nv_ref):
        g = p_ref[0].astype(F32)
        for s in range(1, n_slots):
            g = g + p_ref[s].astype(F32)
        m_new = ADAM_B1 * m_ref[...] + (1.0 - ADAM_B1) * g
        v_new = ADAM_B2 * v_ref[...] + (1.0 - ADAM_B2) * (g * g)
        m_hat = m_new / (1.0 - ADAM_B1 ** ADAM_STEP)
        v_hat = v_new / (1.0 - ADAM_B2 ** ADAM_STEP)
        g_ref[...] = g
        d_ref[...] = -ADAM_LR * (m_hat / (jnp.sqrt(v_hat) + ADAM_EPS) + ADAM_WD * w_ref[...])
        nm_ref[...] = m_new
        nv_ref[...] = v_new

    blk = pl.BlockSpec((tr, cols), lambda i: (i, 0))
    return pl.pallas_call(
        body, name=name,
        grid=(rows // tr,),
        in_specs=[pl.BlockSpec((n_slots, tr, cols), lambda i: (0, i, 0)), blk, blk, blk],
        out_specs=[blk] * 4,
        out_shape=[jax.ShapeDtypeStruct((rows, cols), F32)] * 4,
        compiler_params=_params("arbitrary"),
    )(parts, w, m, v)


N_PIECES = 8
PIECE = 512
SHARD_COLS = 513
SHARD_PAD = 640
RELAYOUT_ROWS = 256


def _from_shards_call(shards):
    _, d, _ = shards.shape
    tr = RELAYOUT_ROWS

    def body(p_ref, m_ref, s_ref):
        lane = lax.broadcasted_iota(jnp.int32, (tr, SHARD_PAD), 1)
        pad = jnp.zeros((tr, SHARD_PAD - SHARD_COLS), F32)
        sh = [jnp.concatenate([p_ref[s].astype(F32), pad], axis=1) for s in range(N_DEV)]
        for p in range(N_PIECES):
            y = sh[p] if p == 0 else pltpu.roll(sh[p], p, axis=1)
            if p > 0:
                y = jnp.where(lane < p, pltpu.roll(sh[p - 1], SHARD_PAD - (SHARD_COLS - p), axis=1), y)
            m_ref[:, p * PIECE:(p + 1) * PIECE] = y[:, :PIECE].astype(m_ref.dtype)
        first_gate = N_PIECES * PIECE - (N_DEV - 1) * SHARD_COLS
        s_ref[...] = pltpu.roll(sh[N_DEV - 1], SHARD_PAD - first_gate, axis=1)[:, :LANES].astype(s_ref.dtype)

    return pl.pallas_call(
        body, name="w_in_from_shards",
        grid=(d // tr,),
        in_specs=[pl.BlockSpec((N_DEV, tr, SHARD_COLS), lambda i: (0, i, 0))],
        out_specs=[pl.BlockSpec((tr, N_PIECES * PIECE), lambda i: (i, 0)), pl.BlockSpec((tr, LANES), lambda i: (i, 0))],
        out_shape=[jax.ShapeDtypeStruct((d, N_PIECES * PIECE), shards.dtype),
                   jax.ShapeDtypeStruct((d, LANES), shards.dtype)],
        compiler_params=_params("arbitrary"),
    )(shards)


def _to_shards_call(main, gates, out_dtype):
    d = main.shape[0]
    tr = RELAYOUT_ROWS

    def body(m_ref, s_ref, o_ref):
        for s in range(N_DEV):
            if s < N_DEV - 1:
                x = m_ref[:, s * PIECE:s * PIECE + SHARD_PAD]
            else:
                x = jnp.concatenate([m_ref[:, s * PIECE:(s + 1) * PIECE], s_ref[...]], axis=1)
            y = x if s == 0 else pltpu.roll(x, SHARD_PAD - s, axis=1)
            o_ref[s] = y[:, :SHARD_COLS].astype(out_dtype)

    return pl.pallas_call(
        body, name="w_in_to_shards",
        grid=(d // tr,),
        in_specs=[pl.BlockSpec((tr, N_PIECES * PIECE), lambda i: (i, 0)), pl.BlockSpec((tr, LANES), lambda i: (i, 0))],
        out_specs=pl.BlockSpec((N_DEV, tr, SHARD_COLS), lambda i: (0, i, 0)),
        out_shape=jax.ShapeDtypeStruct((N_DEV, d, SHARD_COLS), out_dtype),
        compiler_params=_params("arbitrary"),
    )(main, gates)


def _adamw(g, w, m, v):
    m_new = ADAM_B1 * m + (1.0 - ADAM_B1) * g
    v_new = ADAM_B2 * v + (1.0 - ADAM_B2) * (g * g)
    m_hat = m_new / (1.0 - ADAM_B1 ** ADAM_STEP)
    v_hat = v_new / (1.0 - ADAM_B2 ** ADAM_STEP)
    return -ADAM_LR * (m_hat / (jnp.sqrt(v_hat) + ADAM_EPS) + ADAM_WD * w), m_new, v_new


def _adam_small_call(parts, ws, ms, vs):
    n = len(ws)
    n_slots = parts.shape[0]

    def body(*refs):
        p_ref = refs[0]
        w_refs, m_refs, v_refs = refs[1:1 + n], refs[1 + n:1 + 2 * n], refs[1 + 2 * n:1 + 3 * n]
        loss_ref = refs[1 + 3 * n]
        outs = refs[2 + 3 * n:]
        g_all = p_ref[0]
        for s in range(1, n_slots):
            g_all = g_all + p_ref[s]
        loss_ref[...] = g_all[n:n + 1, 0:1]
        for r in range(n):
            size = w_refs[r].shape[1]
            g = g_all[r:r + 1, :size]
            delta, m_new, v_new = _adamw(g, w_refs[r][...], m_refs[r][...], v_refs[r][...])
            for kind, val in enumerate((g, delta, m_new, v_new)):
                outs[kind * n + r][...] = val

    vm = pl.BlockSpec(memory_space=pltpu.VMEM)
    shapes = [jax.ShapeDtypeStruct(w.shape, F32) for w in ws]
    return pl.pallas_call(
        body, name="adam_small",
        in_specs=[vm] * (1 + 3 * n), out_specs=[vm] * (1 + 4 * n),
        out_shape=[jax.ShapeDtypeStruct((1, 1), F32)] + shapes * 4,
    )(parts, *ws, *ms, *vs)


_SMALL_ROWS = ("norm1_w", "final_norm_w", "sb_norm_w", "gdn_norm_w", "gdn_A_log", "gdn_dt_bias", "loss")


def _pack_small(vals, width):
    rows = [jnp.pad(a.reshape(1, -1).astype(F32), ((0, 0), (0, width - a.size))) for a in vals]
    rows += [jnp.zeros((1, width), F32)] * (8 - len(rows))
    return jnp.concatenate(rows, axis=0)


def _device_step(x2d, tgt, w_main, w_small, w_out_full, conv_full, norm1_w, sb_norm_w, gdn_A_log, gdn_dt_bias,
                 gdn_norm_w, final_norm_w):
    t_len, d = x2d.shape
    n_chunks = t_len // CHUNK
    w_main, w_small, w_out_full = (a.astype(MXU_DTYPE) for a in (w_main, w_small, w_out_full))
    w_small_t = w_small[:, :2 * GDN_HEADS].T

    pad_lanes = lambda a, lo: jnp.pad(a.reshape(1, -1), ((0, 0), (lo, LANES - lo - a.size)))
    alog_l, dtb_l = pad_lanes(gdn_A_log, GDN_HEADS), pad_lanes(gdn_dt_bias, GDN_HEADS)
    alog_c, dtb_c = alog_l[:, :8].T, dtb_l[:, :8].T
    sbw = jnp.tile(sb_norm_w, (1, 512 // SB_HEAD_DIM))
    gdw = jnp.tile(gdn_norm_w, (1, 512 // GDN_HEAD_DIM))
    fw = final_norm_w.reshape(1, d)

    proj, ps, pst, h_t, r1 = _inproj_call(x2d, norm1_w, w_main, w_small, w_small_t)
    o_sb, sp_total, sb_blocks_run = _sb_fwd_call(proj, t_len)
    gact = _gdn_prep_call(proj, conv_full, t_len)
    beta_l, gcol_l, grow = _gdn_gates_call(ps, pst, alog_l, dtb_l, alog_c, dtb_c, t_len)
    gam_r = grow[GDN_HEADS:2 * GDN_HEADS].reshape(GDN_HEADS, n_chunks, 1, CHUNK)
    o_gd, s_all, t_all = _gdn_fwd_call(gact, beta_l, gcol_l, gam_r, t_len)

    (dx2, d_osb, d_ogd, dproj8, loss_p, g_fw, g_sbw, g_gdw, g_wout) = _post_call(
        o_sb, o_gd, proj, x2d, tgt, w_out_full, sbw, gdw, fw)

    dproj8 = _sb_bwd_call(proj, sp_total, sb_blocks_run, d_osb, dproj8, t_len)
    d_gact3, d_gates = _gdn_bwd_call(gact, beta_l, gcol_l, gam_r, s_all, t_all, d_ogd, t_len)
    dproj8, g_conv = _gdn_prep_bwd_call(proj, conv_full, d_gact3, dproj8, t_len)
    dsmall, g_alog, g_dtb = _gdn_gates_bwd_call(ps, alog_l, dtb_l, d_gates, t_len)

    g_w_main = _gw_in_call(h_t, dproj8)
    g_w_small = _gw_small_call(h_t, dsmall)
    grad_x, g_n1 = _dx_call(dproj8, dsmall, w_main, w_small, x2d, r1, dx2, norm1_w)
    return (loss_p, grad_x, g_n1, (g_w_main, g_w_small), g_sbw, g_conv, g_alog, g_dtb, g_gdw, g_wout, g_fw)


def kernel(x, norm1_w, w_in, sb_norm_w, gdn_conv_w, gdn_A_log, gdn_dt_bias, gdn_norm_w, w_out, final_norm_w, loss_target, m_norm1_w, m_w_in, m_sb_norm_w, m_gdn_conv_w, m_gdn_A_log, m_gdn_dt_bias, m_gdn_norm_w, m_w_out, m_final_norm_w, v_norm1_w, v_w_in, v_sb_norm_w, v_gdn_conv_w, v_gdn_A_log, v_gdn_dt_bias, v_gdn_norm_w, v_w_out, v_final_norm_w):
    d = x.shape[2]
    shard_cols = w_in.shape[2]
    conv_cols = gdn_conv_w.shape[2]

    w_in_g, w_out_g, conv_g = _gather_call(
        "gather_weights", [w_in[0].astype(WIRE_DTYPE), w_out[0].astype(WIRE_DTYPE), gdn_conv_w[0]])
    w_main, w_small = _from_shards_call(w_in_g)
    conv_full = conv_g.transpose(1, 0, 2).reshape(CONV_WIDTH, N_DEV * conv_cols)

    (loss_p, grad_x, g_n1, (g_w_main, g_w_small), g_sbw, g_conv, g_alog, g_dtb, g_gdw, g_wout, g_fw) = _device_step(
        x[0], loss_target[0], w_main, w_small, w_out_g.reshape(d, d), conv_full, norm1_w, sb_norm_w, gdn_A_log,
        gdn_dt_bias, gdn_norm_w, final_norm_w)

    g_w_in_parts = _to_shards_call(g_w_main, g_w_small, WIRE_DTYPE)
    g_wout_parts = g_wout.reshape(N_DEV, d // N_DEV, d)
    g_conv_parts = g_conv.reshape(CONV_WIDTH, N_DEV, conv_cols).transpose(1, 0, 2)
    fold = lambda a, group: a.reshape(-1, group).sum(axis=0)
    small_g = _pack_small([g_n1, g_fw, fold(g_sbw, SB_HEAD_DIM), fold(g_gdw, GDN_HEAD_DIM),
                           g_alog[0, GDN_HEADS:2 * GDN_HEADS], g_dtb[0, GDN_HEADS:2 * GDN_HEADS],
                           loss_p[0, :1]], d)
    sib_w_in, sib_wout, sib_conv = _sibling_send_call("grads_to_sibling", [g_w_in_parts, g_wout_parts, g_conv_parts])
    c_w_in = _pair_sum_call("pair_sum_w_in", g_w_in_parts, sib_w_in, 256)
    c_wout = _pair_sum_call("pair_sum_w_out", g_wout_parts, sib_wout, d // N_DEV)
    c_conv = _pair_sum_call("pair_sum_conv", g_conv_parts, sib_conv, CONV_WIDTH)
    p_w_in, p_wout, p_conv = _chip_exchange_call("grads_to_chips", [c_w_in, c_wout, c_conv])
    (p_small,) = _exchange_call("exchange_small", [small_g], [False])

    r_w_in = _adam_call("adam_w_in", p_w_in, w_in[0], m_w_in[0], v_w_in[0], 256)
    r_wout = _adam_call("adam_w_out", p_wout, w_out[0], m_w_out[0], v_w_out[0], d // N_DEV)
    r_conv = _adam_call("adam_conv", p_conv, gdn_conv_w[0], m_gdn_conv_w[0], v_gdn_conv_w[0], CONV_WIDTH)

    row = lambda a: a.reshape(1, -1)
    n_small = len(_SMALL_ROWS) - 1
    r_small = _adam_small_call(
        p_small,
        [norm1_w, row(final_norm_w), sb_norm_w, gdn_norm_w, gdn_A_log, gdn_dt_bias],
        [m_norm1_w, row(m_final_norm_w), m_sb_norm_w, m_gdn_norm_w, m_gdn_A_log, m_gdn_dt_bias],
        [v_norm1_w, row(v_final_norm_w), v_sb_norm_w, v_gdn_norm_w, v_gdn_A_log, v_gdn_dt_bias])

    def small_out(kind, name):
        out = r_small[1 + kind * n_small + _SMALL_ROWS.index(name)]
        return out.reshape(final_norm_w.shape) if name == "final_norm_w" else out

    def outputs(kind):
        return (small_out(kind, "norm1_w"), r_w_in[kind][None], small_out(kind, "sb_norm_w"), r_conv[kind][None],
                small_out(kind, "gdn_A_log"), small_out(kind, "gdn_dt_bias"), small_out(kind, "gdn_norm_w"),
                r_wout[kind][None], small_out(kind, "final_norm_w"))

    return (r_small[0][0, 0], grad_x[None], *outputs(0), *outputs(1), *outputs(2), *outputs(3))
```

```python
import functools

import jax
import jax.numpy as jnp
from jax import lax
from jax.experimental import pallas as pl
from jax.experimental.pallas import tpu as pltpu

F32 = jnp.float32
MXU_DTYPE = jnp.bfloat16
WIRE_DTYPE = jnp.bfloat16
EXACT = lax.Precision.HIGHEST
EPS = 1e-6
N_DEV = 8
SB_HEAD_DIM = 64
GDN_HEAD_DIM = 128
GDN_HEADS = 4
GDN_CHUNKS_PER_STEP = 4
GDN_BWD_GROUP = 1
CHUNK = 64
CONV_WIDTH = 4
LANES = 128
SB_BLOCK = 128
SB_BQ = 256
VMEM_LIMIT_BYTES = 56 * 1024 * 1024

DPROJ_PIECE_OF_SLOT = (0, 1, 2, 4, 5, 6, 3, 7)
DPROJ_SB_SLOT, DPROJ_GDN_SLOT, DPROJ_GATE_SLOT = 0, 3, 6

ADAM_LR = 0.001
ADAM_B1 = 0.9
ADAM_B2 = 0.999
ADAM_EPS = 1e-08
ADAM_WD = 0.01
ADAM_STEP = 10

_NN = (((1,), (0,)), ((), ()))
_NT = (((1,), (1,)), ((), ()))
_TN = (((0,), (0,)), ((), ()))
_BNN = (((2,), (1,)), ((0,), (0,)))
_BNT = (((2,), (2,)), ((0,), (0,)))
_BTN = (((1,), (1,)), ((0,), (0,)))


def _mm(a, b):
    return jnp.dot(a.astype(MXU_DTYPE), b.astype(MXU_DTYPE), preferred_element_type=F32)


def _mm_nt(a, b):
    return lax.dot_general(a.astype(MXU_DTYPE), b.astype(MXU_DTYPE), _NT, preferred_element_type=F32)


def _mm_tn(a, b):
    return lax.dot_general(a.astype(MXU_DTYPE), b.astype(MXU_DTYPE), _TN, preferred_element_type=F32)


def _mx(a, b):
    return jnp.dot(a, b, precision=EXACT, preferred_element_type=F32)


def _mx_nt(a, b):
    return lax.dot_general(a, b, _NT, precision=EXACT, preferred_element_type=F32)


def _mx_tn(a, b):
    return lax.dot_general(a, b, _TN, precision=EXACT, preferred_element_type=F32)


def _split(x):
    hi = x.astype(MXU_DTYPE)
    return hi, (x - hi.astype(F32)).astype(MXU_DTYPE)


def _m3_general(a, b, dims):
    ah, al = _split(a)
    bh, bl = _split(b)
    dot = lambda x, y: lax.dot_general(x, y, dims, preferred_element_type=F32)
    (contract, _), (batch, _) = dims
    free = [ax for ax in range(a.ndim) if ax not in contract and ax not in batch][0]
    m = a.shape[free]
    both = dot(jnp.concatenate([ah, al], axis=free), bh)
    out_axis = len(batch)
    hi_part = lax.slice_in_dim(both, 0, m, axis=out_axis)
    lo_part = lax.slice_in_dim(both, m, 2 * m, axis=out_axis)
    return hi_part + (dot(ah, bl) + lo_part)


def _m3(a, b):
    return _m3_general(a, b, _NN)


def _m3_nt(a, b):
    return _m3_general(a, b, _NT)


def _m3_tn(a, b):
    return _m3_general(a, b, _TN)


def _sigmoid(z):
    return 1.0 / (1.0 + jnp.exp(-z))


def _softplus(z):
    return jnp.maximum(z, 0.0) + jnp.log(1.0 + jnp.exp(-jnp.abs(z)))


def _params(*semantics):
    return pltpu.CompilerParams(dimension_semantics=semantics, vmem_limit_bytes=VMEM_LIMIT_BYTES)


def _inproj_call(x, norm_w, w_main, w_small, w_small_t, tm=256):
    t_len, d = x.shape
    n = w_main.shape[1]
    ns = w_small.shape[1]
    nst = w_small_t.shape[0]

    def body(x_ref, nw_ref, wm_ref, ws_ref, wst_ref, pm_ref, ps_ref, pst_ref, ht_ref, r_ref):
        xv = x_ref[...]
        r = lax.rsqrt(jnp.mean(xv * xv, axis=-1, keepdims=True) + EPS)
        h = xv * r * nw_ref[...]
        hb = h.astype(MXU_DTYPE)
        for n0 in range(0, n, 512):
            pm_ref[:, n0:n0 + 512] = jnp.dot(hb, wm_ref[:, n0:n0 + 512], preferred_element_type=F32)
        ps_ref[...] = jnp.dot(hb, ws_ref[...], preferred_element_type=F32)
        pst_ref[...] = lax.dot_general(wst_ref[...], hb, _NT, preferred_element_type=F32)
        ht_ref[...] = h.T.astype(MXU_DTYPE)
        r_ref[...] = r

    return pl.pallas_call(
        body, name="inproj",
        grid=(t_len // tm,),
        in_specs=[pl.BlockSpec((tm, d), lambda i: (i, 0)),
                  pl.BlockSpec((1, d), lambda i: (0, 0)),
                  pl.BlockSpec((d, n), lambda i: (0, 0)),
                  pl.BlockSpec((d, ns), lambda i: (0, 0)),
                  pl.BlockSpec((nst, d), lambda i: (0, 0))],
        out_specs=[pl.BlockSpec((tm, n), lambda i: (i, 0)),
                   pl.BlockSpec((tm, ns), lambda i: (i, 0)),
                   pl.BlockSpec((nst, tm), lambda i: (0, i)),
                   pl.BlockSpec((d, tm), lambda i: (0, i)),
                   pl.BlockSpec((tm, 1), lambda i: (i, 0))],
        out_shape=[jax.ShapeDtypeStruct((t_len, n), F32),
                   jax.ShapeDtypeStruct((t_len, ns), F32),
                   jax.ShapeDtypeStruct((nst, t_len), F32),
                   jax.ShapeDtypeStruct((d, t_len), MXU_DTYPE),
                   jax.ShapeDtypeStruct((t_len, 1), F32)],
        compiler_params=_params("arbitrary"),
    )(x, norm_w, w_main, w_small, w_small_t)


def _running_sum_mm(x, tri):
    hi = x.astype(MXU_DTYPE)
    lo = (x - hi.astype(F32)).astype(MXU_DTYPE)
    return jnp.dot(hi, tri, preferred_element_type=F32) + jnp.dot(lo, tri, preferred_element_type=F32)


def _sb_iotas():
    row_i = lax.broadcasted_iota(jnp.int32, (SB_BQ, SB_BLOCK), 0)
    col_i = lax.broadcasted_iota(jnp.int32, (SB_BQ, SB_BLOCK), 1)
    sq_r = lax.broadcasted_iota(jnp.int32, (SB_BLOCK, SB_BLOCK), 0)
    sq_c = lax.broadcasted_iota(jnp.int32, (SB_BLOCK, SB_BLOCK), 1)
    return row_i, col_i, sq_r, sq_c


SB_DIAG_BLOCKS = SB_BQ // SB_BLOCK
SB_EXP_FLOOR = -110.0


def _sb_keys_descending(qi, tile, carry, z_bounds, n_heads):
    n_free = SB_DIAG_BLOCKS * qi
    for j in range(SB_DIAG_BLOCKS - 1, -1, -1):
        carry = tile(n_free + j, True, carry, j * SB_BLOCK)

    def largest_exponent(c):
        worst = jnp.max(z_bounds[0] - c[1])
        for h in range(1, n_heads):
            worst = jnp.maximum(worst, jnp.max(z_bounds[h] - c[1 + h]))
        return worst

    def cond(state):
        return (state[0] < n_free) & (state[1] > SB_EXP_FLOOR)

    def body(state):
        c = tile(n_free - 1 - state[0], False, state[2:])
        return (state[0] + 1, largest_exponent(c), *c)

    out = lax.while_loop(cond, body, (jnp.int32(0), largest_exponent(carry), *carry))
    return out[2:], out[0]


def _sb_keys_ascending(qi, n_run, tile, carry):
    n_free = SB_DIAG_BLOCKS * qi
    carry = lax.fori_loop(0, n_run, lambda s, c: tile(n_free - n_run + s, False, c), carry)
    for j in range(SB_DIAG_BLOCKS):
        carry = tile(n_free + j, True, carry, j * SB_BLOCK)
    return carry


def _sb_fwd_call(proj, t_len):
    nq = t_len // SB_BQ
    scale = float(SB_HEAD_DIM) ** -0.5
    n_pairs = 512 // LANES
    per_pair = LANES // SB_HEAD_DIM

    def body(q_ref, k_ref, v_ref, o_ref, st_ref, nrun_ref):
        lane = lax.broadcasted_iota(jnp.int32, (1, LANES), 1)
        row_i, col_i, sq_r, sq_c = _sb_iotas()
        ge = (sq_r >= sq_c).astype(MXU_DTYPE)
        hms = [((lane // SB_HEAD_DIM) == hh).astype(F32) for hh in range(per_pair)]
        k_sq = k_ref[...] * k_ref[...]
        k_norms = [jnp.sqrt(jnp.max(jnp.sum(k_sq * hm, axis=-1, keepdims=True))) * (1.02 * scale) for hm in hms]

        def q_loop(qi, carry):
            r0 = pl.multiple_of(qi * SB_BQ, SB_BQ)
            rows = pl.ds(r0, SB_BQ)
            q_all = q_ref[rows, :]
            qms = [(q_all * (hm * scale)).astype(MXU_DTYPE) for hm in hms]
            z_bounds = [jnp.sqrt(jnp.sum(q_all * q_all * hm, axis=-1, keepdims=True)) * kn
                        for hm, kn in zip(hms, k_norms)]

            def tile(kj, masked, kc, lo=0):
                heads = range(per_pair)
                live = lambda a: a[lo:]
                merge = lambda old, new: new if lo == 0 else jnp.concatenate([old[:lo], new], axis=0)
                acc, cs = live(kc[0]), [live(c) for c in kc[1:]]
                s0 = pl.multiple_of(kj * SB_BLOCK, SB_BLOCK)
                cols = pl.ds(s0, SB_BLOCK)
                kb = k_ref[cols, :].astype(MXU_DTYPE)
                v_all = v_ref[cols, :]
                vms = [(v_all * hms[h]).astype(MXU_DTYPE) for h in heads]
                zs = [lax.dot_general(live(qms[h]), kb, _NT, preferred_element_type=F32) for h in heads]
                sps = [_softplus(z) for z in zs]
                if masked:
                    mask = (live(col_i) + s0) < (live(row_i) + r0)
                    sps = [jnp.where(mask, sp, 0.0) for sp in sps]
                sums = [_running_sum_mm(sp, ge) for sp in sps]
                ws = [jnp.exp(zs[h] - (sums[h] + cs[h])) for h in heads]
                if masked:
                    ws = [jnp.where(mask, w, 0.0) for w in ws]
                for h in heads:
                    acc = acc + jnp.dot(ws[h].astype(MXU_DTYPE), vms[h], preferred_element_type=F32)
                cs = [cs[h] + jnp.sum(sps[h], axis=-1, keepdims=True) for h in heads]
                return tuple(merge(old, new) for old, new in zip(kc, (acc, *cs)))

            zero_col = jnp.zeros((SB_BQ, 1), F32)
            out, n_run = _sb_keys_descending(
                qi, tile, (jnp.zeros((SB_BQ, LANES), F32),) + (zero_col,) * per_pair, z_bounds, per_pair)
            o_ref[rows, :] = out[0]
            for hh in range(per_pair):
                st_ref[hh, rows, :] = out[1 + hh]
            nrun_ref[pl.program_id(0), qi] = n_run
            return carry

        lax.fori_loop(0, nq, q_loop, 0)

    return pl.pallas_call(
        body, name="sb_fwd",
        grid=(n_pairs,),
        in_specs=[pl.BlockSpec((t_len, LANES), lambda p: (0, p)),
                  pl.BlockSpec((t_len, LANES), lambda p: (0, n_pairs + p)),
                  pl.BlockSpec((t_len, LANES), lambda p: (0, 2 * n_pairs + p))],
        out_specs=[pl.BlockSpec((t_len, LANES), lambda p: (0, p)),
                   pl.BlockSpec((per_pair, t_len, 1), lambda p: (p, 0, 0)),
                   pl.BlockSpec(memory_space=pltpu.SMEM)],
        out_shape=[jax.ShapeDtypeStruct((t_len, 512), F32),
                   jax.ShapeDtypeStruct((n_pairs * per_pair, t_len, 1), F32),
                   jax.ShapeDtypeStruct((n_pairs, nq), jnp.int32)],
        compiler_params=_params("arbitrary"),
    )(proj, proj, proj)


def _sb_bwd_call(proj, sp_total, n_run_all, d_o, dproj, t_len):
    nq = t_len // SB_BQ
    scale = float(SB_HEAD_DIM) ** -0.5
    n_pairs = 512 // LANES
    per_pair = LANES // SB_HEAD_DIM

    def body(q_ref, k_ref, v_ref, st_ref, nrun_ref, do_ref, dproj_in_ref, d_ref):
        lane = lax.broadcasted_iota(jnp.int32, (1, LANES), 1)
        row_i, col_i, sq_r, sq_c = _sb_iotas()
        lt = (sq_r < sq_c).astype(MXU_DTYPE)
        le = (sq_r <= sq_c).astype(MXU_DTYPE)
        hms = [((lane // SB_HEAD_DIM) == hh).astype(F32) for hh in range(per_pair)]
        d_ref[1] = jnp.zeros((t_len, LANES), F32)
        d_ref[2] = jnp.zeros((t_len, LANES), F32)

        def q_loop(qi, carry):
            r0 = pl.multiple_of(qi * SB_BQ, SB_BQ)
            rows = pl.ds(r0, SB_BQ)
            q_all, do_all = q_ref[rows, :], do_ref[rows, :]
            qms = [(q_all * (hm * scale)).astype(MXU_DTYPE) for hm in hms]
            doms = [(do_all * hm).astype(MXU_DTYPE) for hm in hms]
            totals = [st_ref[hh, rows, :] for hh in range(per_pair)]

            def tile(kj, masked, kc, lo=0):
                heads = range(per_pair)
                live = lambda a: a[lo:]
                merge = lambda old, new: new if lo == 0 else jnp.concatenate([old[:lo], new], axis=0)
                dq = live(kc[0])
                cls, gls = [live(c) for c in kc[1:1 + per_pair]], [live(c) for c in kc[1 + per_pair:]]
                q_live, do_live = [live(a) for a in qms], [live(a) for a in doms]
                s0 = pl.multiple_of(kj * SB_BLOCK, SB_BLOCK)
                cols = pl.ds(s0, SB_BLOCK)
                k_all, v_all = k_ref[cols, :], v_ref[cols, :]
                kb = k_all.astype(MXU_DTYPE)
                vms = [(v_all * hms[h]).astype(MXU_DTYPE) for h in heads]
                kms = [(k_all * (hms[h] * scale)).astype(MXU_DTYPE) for h in heads]
                zs = [lax.dot_general(q_live[h], kb, _NT, preferred_element_type=F32) for h in heads]
                das = [lax.dot_general(do_live[h], vms[h], _NT, preferred_element_type=F32) for h in heads]
                sp_alls = [_softplus(z) for z in zs]
                sps = sp_alls
                if masked:
                    mask = (live(col_i) + s0) < (live(row_i) + r0)
                    sps = [jnp.where(mask, sp, 0.0) for sp in sp_alls]
                lefts = [_running_sum_mm(sp, lt) for sp in sps]
                ws = [jnp.exp(zs[h] - (live(totals[h]) - cls[h] - lefts[h])) for h in heads]
                if masked:
                    ws = [jnp.where(mask, w, 0.0) for w in ws]
                gs = [das[h] * ws[h] for h in heads]
                g_sums = [_running_sum_mm(g, le) for g in gs]
                dzs = [gs[h] - jnp.exp(zs[h] - sp_alls[h]) * (gls[h] + g_sums[h]) for h in heads]
                if masked:
                    dzs = [jnp.where(mask, dz, 0.0) for dz in dzs]
                dzs = [dz.astype(MXU_DTYPE) for dz in dzs]
                dk_t = jnp.zeros((SB_BLOCK, LANES), F32)
                dv_t = jnp.zeros((SB_BLOCK, LANES), F32)
                for h in heads:
                    dq = dq + jnp.dot(dzs[h], kms[h], preferred_element_type=F32)
                    dk_t = dk_t + lax.dot_general(dzs[h], q_live[h], _TN, preferred_element_type=F32)
                    dv_t = dv_t + lax.dot_general(ws[h].astype(MXU_DTYPE), do_live[h], _TN,
                                                  preferred_element_type=F32)
                d_ref[1, cols, :] += dk_t
                d_ref[2, cols, :] += dv_t
                cls = [cls[h] + jnp.sum(sps[h], axis=-1, keepdims=True) for h in heads]
                gls = [gls[h] + jnp.sum(gs[h], axis=-1, keepdims=True) for h in heads]
                return tuple(merge(old, new) for old, new in zip(kc, (dq, *cls, *gls)))

            zero_col = jnp.zeros((SB_BQ, 1), F32)
            out = _sb_keys_ascending(qi, nrun_ref[pl.program_id(0), qi], tile,
                                     (jnp.zeros((SB_BQ, LANES), F32),) + (zero_col,) * (2 * per_pair))
            d_ref[0, rows, :] = out[0]
            return carry

        lax.fori_loop(0, nq, q_loop, 0)

    col = lambda off: pl.BlockSpec((t_len, LANES), lambda p: (0, off + p))
    return pl.pallas_call(
        body, name="sb_bwd",
        grid=(n_pairs,),
        in_specs=[col(0), col(n_pairs), col(2 * n_pairs),
                  pl.BlockSpec((per_pair, t_len, 1), lambda p: (p, 0, 0)),
                  pl.BlockSpec(memory_space=pltpu.SMEM), col(0), _HBM],
        out_specs=pl.BlockSpec((3, t_len, LANES), lambda p: (DPROJ_SB_SLOT // 3, 0, p)),
        out_shape=jax.ShapeDtypeStruct(dproj.shape, dproj.dtype),
        input_output_aliases={6: 0},
        compiler_params=_params("arbitrary"),
    )(proj, proj, proj, sp_total, n_run_all, d_o, dproj)


def _conv_taps(xin, rows, t_len):
    taps = []
    for i in range(CONV_WIDTH):
        shift = CONV_WIDTH - 1 - i
        if shift == 0:
            taps.append(xin)
        else:
            taps.append(jnp.where(rows >= shift, pltpu.roll(xin, shift, axis=0), 0.0))
    return taps


def _gdn_prep_body_common(x_ref, w_ref, t_len):
    j = pl.program_id(0)
    xin = x_ref[...]
    rows = lax.broadcasted_iota(jnp.int32, (t_len, LANES), 0)
    taps = _conv_taps(xin, rows, t_len)
    pre = taps[0] * w_ref[0:1, :]
    for i in range(1, CONV_WIDTH):
        pre = pre + taps[i] * w_ref[i:i + 1, :]
    sg = _sigmoid(pre)
    act = pre * sg
    is_qk = j < 2 * GDN_HEADS
    nrm = jnp.where(is_qk, lax.rsqrt(jnp.sum(act * act, axis=-1, keepdims=True) + EPS), 1.0)
    sc = jnp.where(j < GDN_HEADS, float(GDN_HEAD_DIM) ** -0.5, 1.0)
    return j, rows, taps, pre, sg, act, is_qk, nrm, sc


def _gdn_prep_call(proj, conv_w, t_len):
    first = 2048 // LANES

    def body(x_ref, w_ref, out_ref):
        _, _, _, _, _, act, _, nrm, sc = _gdn_prep_body_common(x_ref, w_ref, t_len)
        out_ref[...] = act * nrm * sc

    return pl.pallas_call(
        body, name="gdn_prep",
        grid=(3 * GDN_HEADS,),
        in_specs=[pl.BlockSpec((t_len, LANES), lambda j: (0, first + j)),
                  pl.BlockSpec((CONV_WIDTH, LANES), lambda j: (0, j))],
        out_specs=pl.BlockSpec((t_len, LANES), lambda j: (0, j)),
        out_shape=jax.ShapeDtypeStruct((t_len, 3 * 512), F32),
        compiler_params=_params("arbitrary"),
    )(proj, conv_w)


def _gdn_prep_bwd_call(proj, conv_w, d_act3, dproj, t_len):
    first = 2048 // LANES

    def body(x_ref, w_ref, d_ref, dproj_in_ref, dx_ref, dw_ref):
        _, rows, taps, pre, sg, act, is_qk, nrm, sc = _gdn_prep_body_common(x_ref, w_ref, t_len)
        d_out = d_ref[0]
        dn = d_out * sc
        d_norm = nrm * dn - act * (nrm * nrm * nrm) * jnp.sum(dn * act, axis=-1, keepdims=True)
        d_act = jnp.where(is_qk, d_norm, d_out)
        d_pre = d_act * sg * (1.0 + pre * (1.0 - sg))
        dx = d_pre * w_ref[CONV_WIDTH - 1:CONV_WIDTH, :]
        dw_ref[CONV_WIDTH - 1:CONV_WIDTH, :] = jnp.sum(d_pre * taps[CONV_WIDTH - 1], axis=0, keepdims=True)
        for i in range(CONV_WIDTH - 1):
            shift = CONV_WIDTH - 1 - i
            up = jnp.where(rows < t_len - shift, pltpu.roll(d_pre, t_len - shift, axis=0), 0.0)
            dx = dx + up * w_ref[i:i + 1, :]
            dw_ref[i:i + 1, :] = jnp.sum(d_pre * taps[i], axis=0, keepdims=True)
        dx_ref[0] = dx

    return pl.pallas_call(
        body, name="gdn_prep_bwd",
        grid=(3 * GDN_HEADS,),
        in_specs=[pl.BlockSpec((t_len, LANES), lambda j: (0, first + j)),
                  pl.BlockSpec((CONV_WIDTH, LANES), lambda j: (0, j)),
                  pl.BlockSpec((1, t_len, LANES), lambda j: (j // GDN_HEADS, 0, j % GDN_HEADS)), _HBM],
        out_specs=[pl.BlockSpec((1, t_len, LANES), lambda j: (DPROJ_GDN_SLOT + j // GDN_HEADS, 0, j % GDN_HEADS)),
                   pl.BlockSpec((CONV_WIDTH, LANES), lambda j: (0, j))],
        out_shape=[jax.ShapeDtypeStruct(dproj.shape, dproj.dtype),
                   jax.ShapeDtypeStruct((CONV_WIDTH, 3 * 512), F32)],
        input_output_aliases={3: 0},
        compiler_params=_params("arbitrary"),
    )(proj, conv_w, d_act3, dproj)


def _chunk_cumsum_matrix():
    r = lax.broadcasted_iota(jnp.int32, (LANES, LANES), 0)
    c = lax.broadcasted_iota(jnp.int32, (LANES, LANES), 1)
    return ((r <= c) & ((r // CHUNK) == (c // CHUNK))).astype(F32)


def _gdn_gates_call(ps, pst, alog_l, dtb_l, alog_c, dtb_c, t_len):
    def body(ps_ref, pst_ref, al_ref, dl_ref, ac_ref, dc_ref, beta_ref, gcol_ref, grow_ref):
        upper = _chunk_cumsum_matrix()
        lower = upper.T
        psv = ps_ref[...]
        beta_ref[...] = _sigmoid(psv)
        g_l = -jnp.exp(al_ref[...]) * _softplus(psv + dl_ref[...])
        g_r = -jnp.exp(ac_ref[...]) * _softplus(pst_ref[...] + dc_ref[...])
        for w in range(t_len // LANES):
            sl = slice(w * LANES, (w + 1) * LANES)
            gcol_ref[sl, :] = _mx(lower, g_l[sl, :])
            grow_ref[:, sl] = _mx(g_r[:, sl], upper)

    vm = pl.BlockSpec(memory_space=pltpu.VMEM)
    return pl.pallas_call(
        body, name="gdn_gates",
        in_specs=[vm] * 6, out_specs=[vm] * 3,
        out_shape=[jax.ShapeDtypeStruct((t_len, LANES), F32),
                   jax.ShapeDtypeStruct((t_len, LANES), F32),
                   jax.ShapeDtypeStruct((8, t_len), F32)],
        compiler_params=pltpu.CompilerParams(vmem_limit_bytes=VMEM_LIMIT_BYTES),
    )(ps, pst, alog_l, dtb_l, alog_c, dtb_c)


def _gdn_gates_bwd_call(ps, alog_l, dtb_l, d_l, t_len):
    def body(ps_ref, al_ref, dl_ref, d_ref, dps_ref, gal_ref, gdt_ref):
        lane = lax.broadcasted_iota(jnp.int32, (1, LANES), 1)
        psv = ps_ref[...]
        dv = d_ref[...]
        beta = _sigmoid(psv)
        ea = jnp.exp(al_ref[...])
        arg = psv + dl_ref[...]
        g = -ea * _softplus(arg)
        d_a = dv * (-ea) * _sigmoid(arg)
        is_a = (lane >= GDN_HEADS) & (lane < 2 * GDN_HEADS)
        dps_ref[...] = jnp.where(lane < GDN_HEADS, dv * beta * (1.0 - beta), jnp.where(is_a, d_a, 0.0))
        gdt_ref[...] = jnp.where(is_a, jnp.sum(d_a, axis=0, keepdims=True), 0.0)
        gal_ref[...] = jnp.where(is_a, jnp.sum(dv * g, axis=0, keepdims=True), 0.0)

    vm = pl.BlockSpec(memory_space=pltpu.VMEM)
    return pl.pallas_call(
        body, name="gdn_gates_bwd",
        in_specs=[vm] * 4, out_specs=[vm] * 3,
        out_shape=[jax.ShapeDtypeStruct((t_len, LANES), F32),
                   jax.ShapeDtypeStruct((1, LANES), F32),
                   jax.ShapeDtypeStruct((1, LANES), F32)],
        compiler_params=pltpu.CompilerParams(vmem_limit_bytes=VMEM_LIMIT_BYTES),
    )(ps, alog_l, dtb_l, d_l)


def _bm(a, b):
    return _m3_general(a, b, _BNN)


def _bm_nt(a, b):
    return _m3_general(a, b, _BNT)


def _bm_tn(a, b):
    return _m3_general(a, b, _BTN)


def _heads_of(ref, rows):
    return jnp.stack([ref[rows, h * GDN_HEAD_DIM:(h + 1) * GDN_HEAD_DIM] for h in range(GDN_HEADS)])


def _chunk_terms(q_ref, k_ref, v_ref, b_ref, gc_ref, gr_ref, c, incl, strict, n=1):
    r0 = c * CHUNK if isinstance(c, int) else pl.multiple_of(c * CHUNK, CHUNK)
    rows = pl.ds(r0, n * CHUNK)
    per_chunk = lambda x: x.reshape(GDN_HEADS * n, CHUNK, x.shape[-1])
    q, k, v = (per_chunk(_heads_of(ref, rows)) for ref in (q_ref, k_ref, v_ref))
    lane_ids = lax.broadcasted_iota(jnp.int32, (1, LANES), 1)
    pick = lambda slab, first: jnp.stack([jnp.sum(jnp.where(lane_ids == first + h, slab, 0.0), axis=-1, keepdims=True)
                                          for h in range(GDN_HEADS)])
    b = per_chunk(pick(b_ref[rows, :], 0))
    gc = per_chunk(pick(gc_ref[rows, :], GDN_HEADS))
    gr = gr_ref[:, c] if n == 1 else gr_ref[:, c:c + n].reshape(GDN_HEADS * n, 1, CHUNK)
    dm = jnp.where(incl, jnp.exp(jnp.where(incl, gc - gr, 0.0)), 0.0)
    kb = k * b
    vb = v * b
    e = jnp.exp(gc)
    kk_qk = _bm_nt(jnp.concatenate([kb, q], axis=1), k)
    a = jnp.where(strict, kk_qk[:, :CHUNK] * dm, 0.0)
    p = jnp.where(incl, kk_qk[:, CHUNK:] * dm, 0.0)
    gl = gc[:, CHUNK - 1:CHUNK, :]
    eg = jnp.exp(gl - gc)
    return rows, q, k, v, b, gc, dm, kb, vb, e, a, p, gl, eg


def _unit_lower_inverse(a, eye):
    x = -a
    tm = eye + x
    xp = _bm(x, x)
    for _ in range(4):
        both = _bm(jnp.concatenate([xp, tm], axis=1), xp)
        tm = tm + both[:, CHUNK:]
        xp = both[:, :CHUNK]
    return tm + _bm(tm, xp)


def _gdn_specs(t_len, n_chunks, reverse):
    cps = GDN_CHUNKS_PER_STEP
    steps = n_chunks // cps
    at = (lambda g: steps - 1 - g) if reverse else (lambda g: g)
    rows_blk = lambda width, part=0: pl.BlockSpec((cps * CHUNK, width), lambda g: (at(g), part))
    gate_r = pl.BlockSpec((GDN_HEADS, cps, 1, CHUNK), lambda g: (0, at(g), 0, 0))
    per_chunk = lambda r, c: pl.BlockSpec((GDN_HEADS, cps, r, c), lambda g: (0, at(g), 0, 0))
    return cps, steps, rows_blk, gate_r, per_chunk


def _gdn_fwd_call(gact, beta_c, gam_c, gam_r, t_len):
    n_chunks = t_len // CHUNK
    dk = GDN_HEAD_DIM
    width = GDN_HEADS * dk
    cps, steps, rows_blk, gate_r, per_chunk = _gdn_specs(t_len, n_chunks, False)

    def body(q_ref, k_ref, v_ref, b_ref, gc_ref, gr_ref, o_ref, s_ref, t_ref, state_ref):
        row = lax.broadcasted_iota(jnp.int32, (CHUNK, CHUNK), 0)
        col = lax.broadcasted_iota(jnp.int32, (CHUNK, CHUNK), 1)
        incl, strict = row >= col, row > col
        eye = (row == col).astype(F32)

        @pl.when(pl.program_id(0) == 0)
        def _():
            state_ref[...] = jnp.zeros_like(state_ref)

        _, q, k, v, b, gc, dm, kb, vb, e, a, p, gl, eg = _chunk_terms(
            q_ref, k_ref, v_ref, b_ref, gc_ref, gr_ref, 0, incl, strict, cps)
        tm = _unit_lower_inverse(a, eye)
        uw = _bm(tm, jnp.concatenate([vb, kb * e], axis=2))
        w_qe = jnp.concatenate([uw[:, :, dk:], q * e], axis=1)
        u, kd, decay = uw[:, :, :dk], k * eg, jnp.exp(gl)
        t_ref[...] = tm.reshape(GDN_HEADS, cps, CHUNK, CHUNK)

        of_chunk = lambda x, c: jnp.stack([x[h * cps + c] for h in range(GDN_HEADS)])
        s = state_ref[...]
        for c in range(cps):
            ws_qs = _bm(of_chunk(w_qe, c), s)
            vn = of_chunk(u, c) - ws_qs[:, :CHUNK]
            o = ws_qs[:, CHUNK:] + _bm(of_chunk(p, c), vn)
            for h in range(GDN_HEADS):
                o_ref[c * CHUNK:(c + 1) * CHUNK, h * dk:(h + 1) * dk] = o[h]
            s_ref[:, c] = s
            s = s * of_chunk(decay, c) + _bm_tn(of_chunk(kd, c), vn)
        state_ref[...] = s

    return pl.pallas_call(
        body, name="gdn_fwd",
        grid=(steps,),
        in_specs=[rows_blk(width, 0), rows_blk(width, 1), rows_blk(width, 2), rows_blk(LANES), rows_blk(LANES), gate_r],
        out_specs=[rows_blk(width), per_chunk(dk, dk), per_chunk(CHUNK, CHUNK)],
        out_shape=[jax.ShapeDtypeStruct((t_len, width), F32),
                   jax.ShapeDtypeStruct((GDN_HEADS, n_chunks, dk, dk), F32),
                   jax.ShapeDtypeStruct((GDN_HEADS, n_chunks, CHUNK, CHUNK), F32)],
        scratch_shapes=[pltpu.VMEM((GDN_HEADS, dk, dk), F32)],
        compiler_params=_params("arbitrary"),
    )(gact, gact, gact, beta_c, gam_c, gam_r)


def _gdn_bwd_call(gact, beta_c, gam_c, gam_r, s_all, t_all, d_o, t_len):
    n_chunks = t_len // CHUNK
    dk = GDN_HEAD_DIM
    width = GDN_HEADS * dk
    cps, steps, rows_blk, gate_r, per_chunk = _gdn_specs(t_len, n_chunks, True)

    def body(q_ref, k_ref, v_ref, b_ref, gc_ref, gr_ref, s_ref, t_ref, do_ref, d_ref, dgate_ref, dstate_ref):
        row = lax.broadcasted_iota(jnp.int32, (CHUNK, CHUNK), 0)
        col = lax.broadcasted_iota(jnp.int32, (CHUNK, CHUNK), 1)
        incl, strict = row >= col, row > col
        ng = GDN_BWD_GROUP
        nb = GDN_HEADS * ng
        upper = jnp.broadcast_to((row <= col).astype(F32), (nb, CHUNK, CHUNK))
        ones = jnp.ones((nb, CHUNK, LANES), F32)
        last_row = lax.broadcasted_iota(jnp.int32, (CHUNK, 1), 0) == CHUNK - 1
        lane_ids = lax.broadcasted_iota(jnp.int32, (1, LANES), 1)
        rsum = lambda m: jnp.sum(m, axis=-1, keepdims=True)
        total = lambda m: jnp.sum(rsum(m), axis=1, keepdims=True)
        of_chunk = lambda x, c: jnp.stack([x[h * ng + c] for h in range(GDN_HEADS)])

        @pl.when(pl.program_id(0) == 0)
        def _():
            dstate_ref[...] = jnp.zeros_like(dstate_ref)

        for c0 in range(cps - ng, -1, -ng):
            group(c0, q_ref, k_ref, v_ref, b_ref, gc_ref, gr_ref, s_ref, t_ref, do_ref, d_ref, dgate_ref, dstate_ref,
                  incl, strict, upper, ones, last_row, lane_ids, rsum, total, of_chunk)

    def group(c0, q_ref, k_ref, v_ref, b_ref, gc_ref, gr_ref, s_ref, t_ref, do_ref, d_ref, dgate_ref, dstate_ref,
              incl, strict, upper, ones, last_row, lane_ids, rsum, total, of_chunk):
        ng = GDN_BWD_GROUP
        nb = GDN_HEADS * ng
        rows = pl.ds(c0 * CHUNK, ng * CHUNK)
        _, q, k, v, b, gc, dm, kb, vb, e, a, p, gl, eg = _chunk_terms(
            q_ref, k_ref, v_ref, b_ref, gc_ref, gr_ref, c0, incl, strict, ng)
        s = s_ref[:, c0:c0 + ng].reshape(nb, dk, dk)
        tm = t_ref[:, c0:c0 + ng].reshape(nb, CHUNK, CHUNK)
        d_out = _heads_of(do_ref, rows).reshape(nb, CHUNK, dk)
        el = jnp.exp(gl)
        kbe = kb * e
        qe = q * e
        kd = k * eg
        uw = _bm(tm, jnp.concatenate([vb, kbe], axis=2))
        u, w = uw[:, :, :dk], uw[:, :, dk:]
        vn = u - _bm(w, s)
        pt_do = _bm_tn(p, d_out)
        qet_do = _bm_tn(qe, d_out)

        ds = dstate_ref[...]
        d_vn_c, ds_c = [None] * ng, [None] * ng
        for c in range(ng - 1, -1, -1):
            ds_c[c] = ds
            d_vn_c[c] = of_chunk(pt_do, c) + _bm(of_chunk(kd, c), ds)
            ds = of_chunk(el, c) * ds + of_chunk(qet_do, c) - _bm_tn(of_chunk(w, c), d_vn_c[c])
        dstate_ref[...] = ds
        by_chunk = lambda xs: jnp.stack([xs[c][h] for h in range(GDN_HEADS) for c in range(ng)])
        d_vn, ds = by_chunk(d_vn_c), by_chunk(ds_c)

        on_s = _bm_nt(jnp.concatenate([d_out, d_vn], axis=1), s)
        d_qe, d_w = on_s[:, :CHUNK], -on_s[:, CHUNK:]
        d_p = jnp.where(incl, _bm_nt(d_out, vn), 0.0)
        d_kd = _bm_nt(vn, ds)
        d_both = _bm_tn(tm, jnp.concatenate([d_vn, d_w], axis=2))
        d_vb, d_kbe = d_both[:, :, :dk], d_both[:, :, dk:]
        d_a = -jnp.where(strict, _bm_nt(d_both, uw), 0.0)
        m = d_a * dm
        n = d_p * dm
        on_k = _bm(jnp.concatenate([m, n], axis=1), k)
        d_kb = on_k[:, :CHUNK] + d_kbe * e
        d_q = on_k[:, CHUNK:] + d_qe * e
        d_k = (_bm_tn(jnp.concatenate([m, n], axis=1), jnp.concatenate([kb, q], axis=1))
               + d_kd * eg + b * d_kb)
        d_v = b * d_vb
        r = d_a * a + d_p * p
        kd_term = rsum(d_kd * kd)
        d_gl = total(ds * s) * el + jnp.sum(kd_term, axis=1, keepdims=True)
        d_gam = (rsum(r) - _bm_tn(r, ones)[:, :, 0:1] + rsum(d_qe * qe) + rsum(d_kbe * kbe) - kd_term
                 + jnp.where(last_row, d_gl, 0.0))
        d_beta = rsum(d_kb * k) + rsum(d_vb * v)
        d_g = _bm(upper, d_gam * ones)[:, :, 0:1]
        per_head = lambda x: x.reshape(GDN_HEADS, ng * CHUNK, x.shape[-1])
        d_q, d_k, d_v, d_beta, d_g = (per_head(x) for x in (d_q, d_k, d_v, d_beta, d_g))
        gates = jnp.zeros((ng * CHUNK, LANES), F32)
        for h in range(GDN_HEADS):
            lanes = slice(h * dk, (h + 1) * dk)
            d_ref[0, rows, lanes] = d_q[h]
            d_ref[1, rows, lanes] = d_k[h]
            d_ref[2, rows, lanes] = d_v[h]
            gates = gates + (jnp.where(lane_ids == h, d_beta[h], 0.0)
                             + jnp.where(lane_ids == GDN_HEADS + h, d_g[h], 0.0))
        dgate_ref[rows, :] = gates

    d_spec = pl.BlockSpec((3, cps * CHUNK, width), lambda g: (0, steps - 1 - g, 0))
    return pl.pallas_call(
        body, name="gdn_bwd",
        grid=(steps,),
        in_specs=[rows_blk(width, 0), rows_blk(width, 1), rows_blk(width, 2), rows_blk(LANES), rows_blk(LANES), gate_r,
                  per_chunk(dk, dk), per_chunk(CHUNK, CHUNK), rows_blk(width)],
        out_specs=[d_spec, rows_blk(LANES)],
        out_shape=[jax.ShapeDtypeStruct((3, t_len, width), F32),
                   jax.ShapeDtypeStruct((t_len, LANES), F32)],
        scratch_shapes=[pltpu.VMEM((GDN_HEADS, dk, dk), F32)],
        compiler_params=_params("arbitrary"),
    )(gact, gact, gact, beta_c, gam_c, gam_r, s_all, t_all, d_o)


def _group_matrix(width, group):
    r = lax.broadcasted_iota(jnp.int32, (width, width), 0)
    c = lax.broadcasted_iota(jnp.int32, (width, width), 1)
    return ((r // group) == (c // group)).astype(F32)


def _post_call(o_sb, o_gd, proj, x, target, w_out, sbw, gdw, fw, tm=256):
    t_len, d = x.shape
    half = 512
    zsb_blk = 1536 // half
    zgd_blk = 3584 // half

    def body(osb_ref, ogd_ref, zsb_ref, zgd_ref, x_ref, tg_ref, wo_ref, sbw_ref, gdw_ref, fw_ref,
             dx2_ref, dosb_ref, dogd_ref, dz_ref, loss_ref, gfw_ref, gsb_ref, ggd_ref, gwo_ref):
        step = pl.program_id(0)

        @pl.when(step == 0)
        def _():
            loss_ref[...] = jnp.zeros_like(loss_ref)
            gfw_ref[...] = jnp.zeros_like(gfw_ref)
            gsb_ref[...] = jnp.zeros_like(gsb_ref)
            ggd_ref[...] = jnp.zeros_like(ggd_ref)
            gwo_ref[...] = jnp.zeros_like(gwo_ref)

        def head_forward(o, z, w, gmat, inv):
            r = lax.rsqrt(_running_sum_mm(o * o, gmat) * inv + EPS)
            nrm = o * r * w
            sg = _sigmoid(z)
            return r, nrm, sg, nrm * (z * sg)

        def head_backward(d_m, o, z, w, gmat, inv, r, nrm, sg):
            d_n = d_m * (z * sg)
            d_z = d_m * nrm * (sg * (1.0 + z * (1.0 - sg)))
            dnw = d_n * w
            d_o = r * dnw - o * (r * r * r) * (_running_sum_mm(dnw * o, gmat) * inv)
            return d_o, d_z, jnp.sum(d_n * o * r, axis=0, keepdims=True)

        g_sb = _group_matrix(half, SB_HEAD_DIM).astype(MXU_DTYPE)
        g_gd = _group_matrix(half, GDN_HEAD_DIM).astype(MXU_DTYPE)
        osb, ogd, zsb, zgd = osb_ref[...], ogd_ref[...], zsb_ref[...], zgd_ref[...]
        sbw_v, gdw_v = sbw_ref[...], gdw_ref[...]
        r_sb, n_sb, sg_sb, m_sb = head_forward(osb, zsb, sbw_v, g_sb, 1.0 / SB_HEAD_DIM)
        r_gd, n_gd, sg_gd, m_gd = head_forward(ogd, zgd, gdw_v, g_gd, 1.0 / GDN_HEAD_DIM)
        mixed = jnp.concatenate([m_sb, m_gd], axis=1).astype(MXU_DTYPE)
        wo = wo_ref[...]
        x2 = x_ref[...] + jnp.dot(mixed, wo, preferred_element_type=F32)
        r2 = lax.rsqrt(jnp.mean(x2 * x2, axis=-1, keepdims=True) + EPS)
        fw_v = fw_ref[...]
        err = x2 * r2 * fw_v - tg_ref[...]
        loss_ref[...] += 0.5 * jnp.sum(jnp.sum(err * err, axis=-1, keepdims=True) * (1.0 / d))
        dy = err * (1.0 / d)
        gg = dy * fw_v
        dx2 = r2 * gg - x2 * ((r2 * r2 * r2) * jnp.mean(gg * x2, axis=-1, keepdims=True))
        gfw_ref[...] += jnp.sum(dy * x2 * r2, axis=0, keepdims=True)
        dx2_ref[...] = dx2
        dx2b = dx2.astype(MXU_DTYPE)
        d_mixed = lax.dot_general(dx2b, wo, _NT, preferred_element_type=F32)
        gwo_ref[...] += lax.dot_general(mixed, dx2b, _TN, preferred_element_type=F32)
        d_osb, d_zsb, gsb = head_backward(d_mixed[:, :half], osb, zsb, sbw_v, g_sb, 1.0 / SB_HEAD_DIM, r_sb, n_sb, sg_sb)
        d_ogd, d_zgd, ggd = head_backward(d_mixed[:, half:], ogd, zgd, gdw_v, g_gd, 1.0 / GDN_HEAD_DIM, r_gd, n_gd, sg_gd)
        dosb_ref[...] = d_osb
        dogd_ref[...] = d_ogd
        dz_ref[0] = d_zsb
        dz_ref[1] = d_zgd
        gsb_ref[...] += gsb
        ggd_ref[...] += ggd

    row_blk = lambda w: pl.BlockSpec((tm, w), lambda i: (i, 0))
    fixed = lambda r, w: pl.BlockSpec((r, w), lambda i: (0, 0))
    return pl.pallas_call(
        body, name="post",
        grid=(t_len // tm,),
        in_specs=[row_blk(half), row_blk(half),
                  pl.BlockSpec((tm, half), lambda i: (i, zsb_blk)),
                  pl.BlockSpec((tm, half), lambda i: (i, zgd_blk)),
                  row_blk(d), row_blk(d), fixed(d, d), fixed(1, half), fixed(1, half), fixed(1, d)],
        out_specs=[row_blk(d), row_blk(half), row_blk(half),
                   pl.BlockSpec((2, tm, half), lambda i: (DPROJ_GATE_SLOT // 2, i, 0)),
                   fixed(1, LANES), fixed(1, d), fixed(1, half), fixed(1, half), fixed(d, d)],
        out_shape=[jax.ShapeDtypeStruct((t_len, d), F32)] + [jax.ShapeDtypeStruct((t_len, half), F32)] * 2
                  + [jax.ShapeDtypeStruct((len(DPROJ_PIECE_OF_SLOT), t_len, half), F32),
                     jax.ShapeDtypeStruct((1, LANES), F32), jax.ShapeDtypeStruct((1, d), F32),
                     jax.ShapeDtypeStruct((1, half), F32), jax.ShapeDtypeStruct((1, half), F32),
                     jax.ShapeDtypeStruct((d, d), F32)],
        compiler_params=_params("arbitrary"),
    )(o_sb, o_gd, proj, proj, x, target, w_out, sbw, gdw, fw)


def _piece_of_slot(s):
    return jnp.where(s < DPROJ_GDN_SLOT, s, jnp.where(s < DPROJ_GATE_SLOT, s + 1,
                                                     jnp.where(s == DPROJ_GATE_SLOT, 3, 7)))


def _gw_in_call(h_t, dproj8):
    d, t_len = h_t.shape
    n_piece, _, pw = dproj8.shape

    def body(ht_ref, dp_ref, gw_ref):
        gw_ref[...] = jnp.dot(ht_ref[...], dp_ref[0].astype(MXU_DTYPE), preferred_element_type=F32)

    return pl.pallas_call(
        body, name="gw_in",
        grid=(n_piece,),
        in_specs=[pl.BlockSpec((d, t_len), lambda s: (0, 0)),
                  pl.BlockSpec((1, t_len, pw), lambda s: (s, 0, 0))],
        out_specs=pl.BlockSpec((d, pw), lambda s: (0, _piece_of_slot(s))),
        out_shape=jax.ShapeDtypeStruct((d, n_piece * pw), F32),
        compiler_params=_params("arbitrary"),
    )(h_t, dproj8)


def _gw_small_call(h_t, dsmall, tm=512):
    d, t_len = h_t.shape
    ns = dsmall.shape[1]

    def body(ht_ref, dp_ref, gw_ref):
        @pl.when(pl.program_id(0) == 0)
        def _():
            gw_ref[...] = jnp.zeros_like(gw_ref)

        gw_ref[...] += jnp.dot(ht_ref[...], dp_ref[...].astype(MXU_DTYPE), preferred_element_type=F32)

    return pl.pallas_call(
        body, name="gw_small",
        grid=(t_len // tm,),
        in_specs=[pl.BlockSpec((d, tm), lambda t: (0, t)),
                  pl.BlockSpec((tm, ns), lambda t: (t, 0))],
        out_specs=pl.BlockSpec((d, ns), lambda t: (0, 0)),
        out_shape=jax.ShapeDtypeStruct((d, ns), F32),
        compiler_params=_params("arbitrary"),
    )(h_t, dsmall)


def _dx_call(dproj8, dsmall, w_main, w_small, x, r, dx2, norm_w, tm=256):
    t_len, d = x.shape
    n_piece, _, pw = dproj8.shape
    ns = dsmall.shape[1]

    def body(dp_ref, ds_ref, wm_ref, ws_ref, x_ref, r_ref, dx2_ref, nw_ref, gx_ref, gnw_ref):
        @pl.when(pl.program_id(0) == 0)
        def _():
            gnw_ref[...] = jnp.zeros_like(gnw_ref)

        dh = lax.dot_general(ds_ref[...].astype(MXU_DTYPE), ws_ref[...], _NT, preferred_element_type=F32)
        for s, p in enumerate(DPROJ_PIECE_OF_SLOT):
            dh = dh + lax.dot_general(dp_ref[s].astype(MXU_DTYPE), wm_ref[:, p * pw:(p + 1) * pw], _NT,
                                      preferred_element_type=F32)
        xv, rv = x_ref[...], r_ref[...]
        dn = dh * nw_ref[...]
        gx_ref[...] = dx2_ref[...] + rv * dn - xv * ((rv * rv * rv) * jnp.mean(dn * xv, axis=-1, keepdims=True))
        gnw_ref[...] += jnp.sum(dh * xv * rv, axis=0, keepdims=True)

    return pl.pallas_call(
        body, name="dx",
        grid=(t_len // tm,),
        in_specs=[pl.BlockSpec((n_piece, tm, pw), lambda i: (0, i, 0)),
                  pl.BlockSpec((tm, ns), lambda i: (i, 0)),
                  pl.BlockSpec((d, n_piece * pw), lambda i: (0, 0)),
                  pl.BlockSpec((d, ns), lambda i: (0, 0)),
                  pl.BlockSpec((tm, d), lambda i: (i, 0)),
                  pl.BlockSpec((tm, 1), lambda i: (i, 0)),
                  pl.BlockSpec((tm, d), lambda i: (i, 0)),
                  pl.BlockSpec((1, d), lambda i: (0, 0))],
        out_specs=[pl.BlockSpec((tm, d), lambda i: (i, 0)),
                   pl.BlockSpec((1, d), lambda i: (0, 0))],
        out_shape=[jax.ShapeDtypeStruct((t_len, d), F32), jax.ShapeDtypeStruct((1, d), F32)],
        compiler_params=_params("arbitrary"),
    )(dproj8, dsmall, w_main, w_small, x, r, dx2, norm_w)


def _exchange_call(name, srcs, per_peer):
    n = len(srcs)
    out_shapes = [jax.ShapeDtypeStruct(s.shape if pp else (N_DEV,) + s.shape, s.dtype) for s, pp in zip(srcs, per_peer)]

    def body(*refs):
        src_refs, out_refs = refs[:n], refs[n:2 * n]
        send_sems, recv_sems, local_sems = refs[2 * n:]
        x, y, c = lax.axis_index("x"), lax.axis_index("y"), lax.axis_index("c")
        me = 4 * x + 2 * y + c
        copies = []
        for a in range(n):
            mine = src_refs[a].at[me] if per_peer[a] else src_refs[a]
            local = pltpu.make_async_copy(mine, out_refs[a].at[me], local_sems.at[a])
            local.start()
            copies.append(local)
        remote = []
        for k in range(1, N_DEV):
            kx, ky, kc = (k >> 2) & 1, (k >> 1) & 1, k & 1
            px = 1 - x if kx else x
            py = 1 - y if ky else y
            pc = 1 - c if kc else c
            peer = 4 * px + 2 * py + pc
            for a in range(n):
                sem = a * (N_DEV - 1) + (k - 1)
                src = src_refs[a].at[peer] if per_peer[a] else src_refs[a]
                cp = pltpu.make_async_remote_copy(
                    src_ref=src, dst_ref=out_refs[a].at[me],
                    send_sem=send_sems.at[sem], recv_sem=recv_sems.at[sem],
                    device_id=(px, py, pc), device_id_type=pl.DeviceIdType.MESH)
                cp.start()
                remote.append(cp)
        for cp in remote:
            cp.wait_send()
        for cp in remote:
            cp.wait_recv()
        for cp in copies:
            cp.wait()

    hbm = pl.BlockSpec(memory_space=pl.ANY)
    return pl.pallas_call(
        body, name=name,
        in_specs=[hbm] * n, out_specs=[hbm] * n, out_shape=out_shapes,
        scratch_shapes=[pltpu.SemaphoreType.DMA((n * (N_DEV - 1),)),
                        pltpu.SemaphoreType.DMA((n * (N_DEV - 1),)),
                        pltpu.SemaphoreType.DMA((n,))],
    )(*srcs)


N_CHIPS = 4
_HBM = pl.BlockSpec(memory_space=pl.ANY)
_MESH = pl.DeviceIdType.MESH


def _gather_call(name, srcs):
    n = len(srcs)
    per = N_DEV - 1

    def body(*refs):
        src_refs, out_refs = refs[:n], refs[n:2 * n]
        send_sems, recv_sems, local_sems = refs[2 * n:]
        x, y, c = lax.axis_index("x"), lax.axis_index("y"), lax.axis_index("c")
        me, sibling = (x, y, c), (x, y, 1 - c)
        chips = [(1 - x, y), (x, 1 - y), (1 - x, 1 - y)]
        slot = lambda px, py, pc: 4 * px + 2 * py + pc

        def copy(a, k, block, to, from_src=False):
            rows = out_refs[a].at[slot(*block)]
            return pltpu.make_async_remote_copy(
                src_ref=src_refs[a] if from_src else rows, dst_ref=rows,
                send_sem=send_sems.at[a * per + k], recv_sem=recv_sems.at[a * per + k],
                device_id=to, device_id_type=_MESH)

        local = [pltpu.make_async_copy(src_refs[a], out_refs[a].at[slot(*me)], local_sems.at[a]) for a in range(n)]
        for cp in local:
            cp.start()
        first = []
        for a in range(n):
            first.append(copy(a, 0, me, sibling, True))
            first += [copy(a, 1 + j, me, (*chip, c), True) for j, chip in enumerate(chips)]
        for cp in first:
            cp.start()
        passed = []
        for j, chip in enumerate(chips):
            for a in range(n):
                copy(a, 1 + j, (*chip, c), me).wait_recv()
                fwd = copy(a, 4 + j, (*chip, c), sibling)
                fwd.start()
                passed.append(fwd)
        for a in range(n):
            copy(a, 0, sibling, me).wait_recv()
            for j, chip in enumerate(chips):
                copy(a, 4 + j, (*chip, 1 - c), me).wait_recv()
        for cp in first + passed:
            cp.wait_send()
        for cp in local:
            cp.wait()

    return pl.pallas_call(
        body, name=name,
        in_specs=[_HBM] * n, out_specs=[_HBM] * n,
        out_shape=[jax.ShapeDtypeStruct((N_DEV,) + s.shape, s.dtype) for s in srcs],
        scratch_shapes=[pltpu.SemaphoreType.DMA((n * per,)), pltpu.SemaphoreType.DMA((n * per,)),
                        pltpu.SemaphoreType.DMA((n,))],
    )(*srcs)


def _sibling_send_call(name, srcs):
    n = len(srcs)

    def body(*refs):
        src_refs, out_refs = refs[:n], refs[n:2 * n]
        send_sems, recv_sems = refs[2 * n:]
        x, y, c = lax.axis_index("x"), lax.axis_index("y"), lax.axis_index("c")
        copies = []
        for a in range(n):
            for ch in range(N_CHIPS):
                copies.append(pltpu.make_async_remote_copy(
                    src_ref=src_refs[a].at[2 * ch + (1 - c)], dst_ref=out_refs[a].at[ch],
                    send_sem=send_sems.at[a * N_CHIPS + ch], recv_sem=recv_sems.at[a * N_CHIPS + ch],
                    device_id=(x, y, 1 - c), device_id_type=_MESH))
        for cp in copies:
            cp.start()
        for cp in copies:
            cp.wait_send()
        for cp in copies:
            cp.wait_recv()

    return pl.pallas_call(
        body, name=name,
        in_specs=[_HBM] * n, out_specs=[_HBM] * n,
        out_shape=[jax.ShapeDtypeStruct((N_CHIPS,) + s.shape[1:], s.dtype) for s in srcs],
        scratch_shapes=[pltpu.SemaphoreType.DMA((n * N_CHIPS,)), pltpu.SemaphoreType.DMA((n * N_CHIPS,))],
    )(*srcs)


def _pair_sum_call(name, parts, from_sibling, tr):
    _, rows, cols = parts.shape

    def body(p_ref, s_ref, o_ref):
        o_ref[...] = (p_ref[...].astype(F32) + s_ref[...].astype(F32)).astype(o_ref.dtype)

    return pl.pallas_call(
        body, name=name,
        grid=(N_CHIPS, rows // tr),
        in_specs=[pl.BlockSpec((1, tr, cols), lambda ch, i: (2 * ch + lax.axis_index("c"), i, 0)),
                  pl.BlockSpec((1, tr, cols), lambda ch, i: (ch, i, 0))],
        out_specs=pl.BlockSpec((1, tr, cols), lambda ch, i: (ch, i, 0)),
        out_shape=jax.ShapeDtypeStruct((N_CHIPS, rows, cols), WIRE_DTYPE),
        compiler_params=_params("arbitrary", "arbitrary"),
    )(parts, from_sibling)


def _chip_exchange_call(name, srcs):
    n = len(srcs)
    per = N_CHIPS - 1

    def body(*refs):
        src_refs, out_refs = refs[:n], refs[n:2 * n]
        send_sems, recv_sems, local_sems = refs[2 * n:]
        x, y, c = lax.axis_index("x"), lax.axis_index("y"), lax.axis_index("c")
        mine = 2 * x + y
        chips = [(1 - x, y), (x, 1 - y), (1 - x, 1 - y)]
        local = [pltpu.make_async_copy(src_refs[a].at[mine], out_refs[a].at[mine], local_sems.at[a]) for a in range(n)]
        for cp in local:
            cp.start()
        remote = []
        for a in range(n):
            for j, (px, py) in enumerate(chips):
                remote.append(pltpu.make_async_remote_copy(
                    src_ref=src_refs[a].at[2 * px + py], dst_ref=out_refs[a].at[mine],
                    send_sem=send_sems.at[a * per + j], recv_sem=recv_sems.at[a * per + j],
                    device_id=(px, py, c), device_id_type=_MESH))
        for cp in remote:
            cp.start()
        for cp in remote:
            cp.wait_send()
        for cp in remote:
            cp.wait_recv()
        for cp in local:
            cp.wait()

    return pl.pallas_call(
        body, name=name,
        in_specs=[_HBM] * n, out_specs=[_HBM] * n,
        out_shape=[jax.ShapeDtypeStruct(s.shape, s.dtype) for s in srcs],
        scratch_shapes=[pltpu.SemaphoreType.DMA((n * per,)), pltpu.SemaphoreType.DMA((n * per,)),
                        pltpu.SemaphoreType.DMA((n,))],
    )(*srcs)


def _adam_call(name, parts, w, m, v, tr):
    rows, cols = w.shape
    n_slots = parts.shape[0]

    def body(p_ref, w_ref, m_ref, v_ref, g_ref, d_ref, nm_ref, nv_ref):
        g = p_ref[0].astype(F32)
        for s in range(1, n_slots):
            g = g + p_ref[s].astype(F32)
        m_new = ADAM_B1 * m_ref[...] + (1.0 - ADAM_B1) * g
        v_new = ADAM_B2 * v_ref[...] + (1.0 - ADAM_B2) * (g * g)
        m_hat = m_new / (1.0 - ADAM_B1 ** ADAM_STEP)
        v_hat = v_new / (1.0 - ADAM_B2 ** ADAM_STEP)
        g_ref[...] = g
        d_ref[...] = -ADAM_LR * (m_hat / (jnp.sqrt(v_hat) + ADAM_EPS) + ADAM_WD * w_ref[...])
        nm_ref[...] = m_new
        nv_ref[...] = v_new

    blk = pl.BlockSpec((tr, cols), lambda i: (i, 0))
    return pl.pallas_call(
        body, name=name,
        grid=(rows // tr,),
        in_specs=[pl.BlockSpec((n_slots, tr, cols), lambda i: (0, i, 0)), blk, blk, blk],
        out_specs=[blk] * 4,
        out_shape=[jax.ShapeDtypeStruct((rows, cols), F32)] * 4,
        compiler_params=_params("arbitrary"),
    )(parts, w, m, v)


N_PIECES = 8
PIECE = 512
SHARD_COLS = 513
SHARD_PAD = 640
RELAYOUT_ROWS = 256


def _from_shards_call(shards):
    _, d, _ = shards.shape
    tr = RELAYOUT_ROWS

    def body(p_ref, m_ref, s_ref):
        lane = lax.broadcasted_iota(jnp.int32, (tr, SHARD_PAD), 1)
        pad = jnp.zeros((tr, SHARD_PAD - SHARD_COLS), F32)
        sh = [jnp.concatenate([p_ref[s].astype(F32), pad], axis=1) for s in range(N_DEV)]
        for p in range(N_PIECES):
            y = sh[p] if p == 0 else pltpu.roll(sh[p], p, axis=1)
            if p > 0:
                y = jnp.where(lane < p, pltpu.roll(sh[p - 1], SHARD_PAD - (SHARD_COLS - p), axis=1), y)
            m_ref[:, p * PIECE:(p + 1) * PIECE] = y[:, :PIECE].astype(m_ref.dtype)
        first_gate = N_PIECES * PIECE - (N_DEV - 1) * SHARD_COLS
        s_ref[...] = pltpu.roll(sh[N_DEV - 1], SHARD_PAD - first_gate, axis=1)[:, :LANES].astype(s_ref.dtype)

    return pl.pallas_call(
        body, name="w_in_from_shards",
        grid=(d // tr,),
        in_specs=[pl.BlockSpec((N_DEV, tr, SHARD_COLS), lambda i: (0, i, 0))],
        out_specs=[pl.BlockSpec((tr, N_PIECES * PIECE), lambda i: (i, 0)), pl.BlockSpec((tr, LANES), lambda i: (i, 0))],
        out_shape=[jax.ShapeDtypeStruct((d, N_PIECES * PIECE), shards.dtype),
                   jax.ShapeDtypeStruct((d, LANES), shards.dtype)],
        compiler_params=_params("arbitrary"),
    )(shards)


def _to_shards_call(main, gates, out_dtype):
    d = main.shape[0]
    tr = RELAYOUT_ROWS

    def body(m_ref, s_ref, o_ref):
        for s in range(N_DEV):
            if s < N_DEV - 1:
                x = m_ref[:, s * PIECE:s * PIECE + SHARD_PAD]
            else:
                x = jnp.concatenate([m_ref[:, s * PIECE:(s + 1) * PIECE], s_ref[...]], axis=1)
            y = x if s == 0 else pltpu.roll(x, SHARD_PAD - s, axis=1)
            o_ref[s] = y[:, :SHARD_COLS].astype(out_dtype)

    return pl.pallas_call(
        body, name="w_in_to_shards",
        grid=(d // tr,),
        in_specs=[pl.BlockSpec((tr, N_PIECES * PIECE), lambda i: (i, 0)), pl.BlockSpec((tr, LANES), lambda i: (i, 0))],
        out_specs=pl.BlockSpec((N_DEV, tr, SHARD_COLS), lambda i: (0, i, 0)),
        out_shape=jax.ShapeDtypeStruct((N_DEV, d, SHARD_COLS), out_dtype),
        compiler_params=_params("arbitrary"),
    )(main, gates)


def _adamw(g, w, m, v):
    m_new = ADAM_B1 * m + (1.0 - ADAM_B1) * g
    v_new = ADAM_B2 * v + (1.0 - ADAM_B2) * (g * g)
    m_hat = m_new / (1.0 - ADAM_B1 ** ADAM_STEP)
    v_hat = v_new / (1.0 - ADAM_B2 ** ADAM_STEP)
    return -ADAM_LR * (m_hat / (jnp.sqrt(v_hat) + ADAM_EPS) + ADAM_WD * w), m_new, v_new


def _adam_small_call(parts, ws, ms, vs):
    n = len(ws)
    n_slots = parts.shape[0]

    def body(*refs):
        p_ref = refs[0]
        w_refs, m_refs, v_refs = refs[1:1 + n], refs[1 + n:1 + 2 * n], refs[1 + 2 * n:1 + 3 * n]
        loss_ref = refs[1 + 3 * n]
        outs = refs[2 + 3 * n:]
        g_all = p_ref[0]
        for s in range(1, n_slots):
            g_all = g_all + p_ref[s]
        loss_ref[...] = g_all[n:n + 1, 0:1]
        for r in range(n):
            size = w_refs[r].shape[1]
            g = g_all[r:r + 1, :size]
            delta, m_new, v_new = _adamw(g, w_refs[r][...], m_refs[r][...], v_refs[r][...])
            for kind, val in enumerate((g, delta, m_new, v_new)):
                outs[kind * n + r][...] = val

    vm = pl.BlockSpec(memory_space=pltpu.VMEM)
    shapes = [jax.ShapeDtypeStruct(w.shape, F32) for w in ws]
    return pl.pallas_call(
        body, name="adam_small",
        in_specs=[vm] * (1 + 3 * n), out_specs=[vm] * (1 + 4 * n),
        out_shape=[jax.ShapeDtypeStruct((1, 1), F32)] + shapes * 4,
    )(parts, *ws, *ms, *vs)


_SMALL_ROWS = ("norm1_w", "final_norm_w", "sb_norm_w", "gdn_norm_w", "gdn_A_log", "gdn_dt_bias", "loss")


def _pack_small(vals, width):
    rows = [jnp.pad(a.reshape(1, -1).astype(F32), ((0, 0), (0, width - a.size))) for a in vals]
    rows += [jnp.zeros((1, width), F32)] * (8 - len(rows))
    return jnp.concatenate(rows, axis=0)


def _device_step(x2d, tgt, w_main, w_small, w_out_full, conv_full, norm1_w, sb_norm_w, gdn_A_log, gdn_dt_bias,
                 gdn_norm_w, final_norm_w):
    t_len, d = x2d.shape
    n_chunks = t_len // CHUNK
    w_main, w_small, w_out_full = (a.astype(MXU_DTYPE) for a in (w_main, w_small, w_out_full))
    w_small_t = w_small[:, :2 * GDN_HEADS].T

    pad_lanes = lambda a, lo: jnp.pad(a.reshape(1, -1), ((0, 0), (lo, LANES - lo - a.size)))
    alog_l, dtb_l = pad_lanes(gdn_A_log, GDN_HEADS), pad_lanes(gdn_dt_bias, GDN_HEADS)
    alog_c, dtb_c = alog_l[:, :8].T, dtb_l[:, :8].T
    sbw = jnp.tile(sb_norm_w, (1, 512 // SB_HEAD_DIM))
    gdw = jnp.tile(gdn_norm_w, (1, 512 // GDN_HEAD_DIM))
    fw = final_norm_w.reshape(1, d)

    proj, ps, pst, h_t, r1 = _inproj_call(x2d, norm1_w, w_main, w_small, w_small_t)
    o_sb, sp_total, sb_blocks_run = _sb_fwd_call(proj, t_len)
    gact = _gdn_prep_call(proj, conv_full, t_len)
    beta_l, gcol_l, grow = _gdn_gates_call(ps, pst, alog_l, dtb_l, alog_c, dtb_c, t_len)
    gam_r = grow[GDN_HEADS:2 * GDN_HEADS].reshape(GDN_HEADS, n_chunks, 1, CHUNK)
    o_gd, s_all, t_all = _gdn_fwd_call(gact, beta_l, gcol_l, gam_r, t_len)

    (dx2, d_osb, d_ogd, dproj8, loss_p, g_fw, g_sbw, g_gdw, g_wout) = _post_call(
        o_sb, o_gd, proj, x2d, tgt, w_out_full, sbw, gdw, fw)

    dproj8 = _sb_bwd_call(proj, sp_total, sb_blocks_run, d_osb, dproj8, t_len)
    d_gact3, d_gates = _gdn_bwd_call(gact, beta_l, gcol_l, gam_r, s_all, t_all, d_ogd, t_len)
    dproj8, g_conv = _gdn_prep_bwd_call(proj, conv_full, d_gact3, dproj8, t_len)
    dsmall, g_alog, g_dtb = _gdn_gates_bwd_call(ps, alog_l, dtb_l, d_gates, t_len)

    g_w_main = _gw_in_call(h_t, dproj8)
    g_w_small = _gw_small_call(h_t, dsmall)
    grad_x, g_n1 = _dx_call(dproj8, dsmall, w_main, w_small, x2d, r1, dx2, norm1_w)
    return (loss_p, grad_x, g_n1, (g_w_main, g_w_small), g_sbw, g_conv, g_alog, g_dtb, g_gdw, g_wout, g_fw)


def kernel(x, norm1_w, w_in, sb_norm_w, gdn_conv_w, gdn_A_log, gdn_dt_bias, gdn_norm_w, w_out, final_norm_w, loss_target, m_norm1_w, m_w_in, m_sb_norm_w, m_gdn_conv_w, m_gdn_A_log, m_gdn_dt_bias, m_gdn_norm_w, m_w_out, m_final_norm_w, v_norm1_w, v_w_in, v_sb_norm_w, v_gdn_conv_w, v_gdn_A_log, v_gdn_dt_bias, v_gdn_norm_w, v_w_out, v_final_norm_w):
    d = x.shape[2]
    shard_cols = w_in.shape[2]
    conv_cols = gdn_conv_w.shape[2]

    w_in_g, w_out_g, conv_g = _gather_call(
        "gather_weights", [w_in[0].astype(WIRE_DTYPE), w_out[0].astype(WIRE_DTYPE), gdn_conv_w[0]])
    w_main, w_small = _from_shards_call(w_in_g)
    conv_full = conv_g.transpose(1, 0, 2).reshape(CONV_WIDTH, N_DEV * conv_cols)

    (loss_p, grad_x, g_n1, (g_w_main, g_w_small), g_sbw, g_conv, g_alog, g_dtb, g_gdw, g_wout, g_fw) = _device_step(
        x[0], loss_target[0], w_main, w_small, w_out_g.reshape(d, d), conv_full, norm1_w, sb_norm_w, gdn_A_log,
        gdn_dt_bias, gdn_norm_w, final_norm_w)

    g_w_in_parts = _to_shards_call(g_w_main, g_w_small, WIRE_DTYPE)
    g_wout_parts = g_wout.reshape(N_DEV, d // N_DEV, d)
    g_conv_parts = g_conv.reshape(CONV_WIDTH, N_DEV, conv_cols).transpose(1, 0, 2)
    fold = lambda a, group: a.reshape(-1, group).sum(axis=0)
    small_g = _pack_small([g_n1, g_fw, fold(g_sbw, SB_HEAD_DIM), fold(g_gdw, GDN_HEAD_DIM),
                           g_alog[0, GDN_HEADS:2 * GDN_HEADS], g_dtb[0, GDN_HEADS:2 * GDN_HEADS],
                           loss_p[0, :1]], d)
    sib_w_in, sib_wout, sib_conv = _sibling_send_call("grads_to_sibling", [g_w_in_parts, g_wout_parts, g_conv_parts])
    c_w_in = _pair_sum_call("pair_sum_w_in", g_w_in_parts, sib_w_in, 256)
    c_wout = _pair_sum_call("pair_sum_w_out", g_wout_parts, sib_wout, d // N_DEV)
    c_conv = _pair_sum_call("pair_sum_conv", g_conv_parts, sib_conv, CONV_WIDTH)
    p_w_in, p_wout, p_conv = _chip_exchange_call("grads_to_chips", [c_w_in, c_wout, c_conv])
    (p_small,) = _exchange_call("exchange_small", [small_g], [False])

    r_w_in = _adam_call("adam_w_in", p_w_in, w_in[0], m_w_in[0], v_w_in[0], 256)
    r_wout = _adam_call("adam_w_out", p_wout, w_out[0], m_w_out[0], v_w_out[0], d // N_DEV)
    r_conv = _adam_call("adam_conv", p_conv, gdn_conv_w[0], m_gdn_conv_w[0], v_gdn_conv_w[0], CONV_WIDTH)

    row = lambda a: a.reshape(1, -1)
    n_small = len(_SMALL_ROWS) - 1
    r_small = _adam_small_call(
        p_small,
        [norm1_w, row(final_norm_w), sb_norm_w, gdn_norm_w, gdn_A_log, gdn_dt_bias],
        [m_norm1_w, row(m_final_norm_w), m_sb_norm_w, m_gdn_norm_w, m_gdn_A_log, m_gdn_dt_bias],
        [v_norm1_w, row(v_final_norm_w), v_sb_norm_w, v_gdn_norm_w, v_gdn_A_log, v_gdn_dt_bias])

    def small_out(kind, name):
        out = r_small[1 + kind * n_small + _SMALL_ROWS.index(name)]
        return out.reshape(final_norm_w.shape) if name == "final_norm_w" else out

    def outputs(kind):
        return (small_out(kind, "norm1_w"), r_w_in[kind][None], small_out(kind, "sb_norm_w"), r_conv[kind][None],
                small_out(kind, "gdn_A_log"), small_out(kind, "gdn_dt_bias"), small_out(kind, "gdn_norm_w"),
                r_wout[kind][None], small_out(kind, "final_norm_w"))

    return (r_small[0][0, 0], grad_x[None], *outputs(0), *outputs(1), *outputs(2), *outputs(3))
```

```python
import functools

import jax
import jax.numpy as jnp
from jax import lax
from jax.experimental import pallas as pl
from jax.experimental.pallas import tpu as pltpu

F32 = jnp.float32
MXU_DTYPE = jnp.bfloat16
WIRE_DTYPE = jnp.bfloat16
EXACT = lax.Precision.HIGHEST
EPS = 1e-6
N_DEV = 8
SB_HEAD_DIM = 64
GDN_HEAD_DIM = 128
GDN_HEADS = 4
GDN_CHUNKS_PER_STEP = 4
GDN_BWD_GROUP = 1
CHUNK = 64
CONV_WIDTH = 4
LANES = 128
SB_BLOCK = 128
SB_BQ = 256
VMEM_LIMIT_BYTES = 56 * 1024 * 1024

DPROJ_PIECE_OF_SLOT = (0, 1, 2, 4, 5, 6, 3, 7)
DPROJ_SB_SLOT, DPROJ_GDN_SLOT, DPROJ_GATE_SLOT = 0, 3, 6

ADAM_LR = 0.001
ADAM_B1 = 0.9
ADAM_B2 = 0.999
ADAM_EPS = 1e-08
ADAM_WD = 0.01
ADAM_STEP = 10

_NN = (((1,), (0,)), ((), ()))
_NT = (((1,), (1,)), ((), ()))
_TN = (((0,), (0,)), ((), ()))
_BNN = (((2,), (1,)), ((0,), (0,)))
_BNT = (((2,), (2,)), ((0,), (0,)))
_BTN = (((1,), (1,)), ((0,), (0,)))


def _mm(a, b):
    return jnp.dot(a.astype(MXU_DTYPE), b.astype(MXU_DTYPE), preferred_element_type=F32)


def _mm_nt(a, b):
    return lax.dot_general(a.astype(MXU_DTYPE), b.astype(MXU_DTYPE), _NT, preferred_element_type=F32)


def _mm_tn(a, b):
    return lax.dot_general(a.astype(MXU_DTYPE), b.astype(MXU_DTYPE), _TN, preferred_element_type=F32)


def _mx(a, b):
    return jnp.dot(a, b, precision=EXACT, preferred_element_type=F32)


def _mx_nt(a, b):
    return lax.dot_general(a, b, _NT, precision=EXACT, preferred_element_type=F32)


def _mx_tn(a, b):
    return lax.dot_general(a, b, _TN, precision=EXACT, preferred_element_type=F32)


def _split(x):
    hi = x.astype(MXU_DTYPE)
    return hi, (x - hi.astype(F32)).astype(MXU_DTYPE)


def _m3_general(a, b, dims):
    ah, al = _split(a)
    bh, bl = _split(b)
    dot = lambda x, y: lax.dot_general(x, y, dims, preferred_element_type=F32)
    (contract, _), (batch, _) = dims
    free = [ax for ax in range(a.ndim) if ax not in contract and ax not in batch][0]
    m = a.shape[free]
    both = dot(jnp.concatenate([ah, al], axis=free), bh)
    out_axis = len(batch)
    hi_part = lax.slice_in_dim(both, 0, m, axis=out_axis)
    lo_part = lax.slice_in_dim(both, m, 2 * m, axis=out_axis)
    return hi_part + (dot(ah, bl) + lo_part)


def _m3(a, b):
    return _m3_general(a, b, _NN)


def _m3_nt(a, b):
    return _m3_general(a, b, _NT)


def _m3_tn(a, b):
    return _m3_general(a, b, _TN)


def _sigmoid(z):
    return 1.0 / (1.0 + jnp.exp(-z))


def _softplus(z):
    return jnp.maximum(z, 0.0) + jnp.log(1.0 + jnp.exp(-jnp.abs(z)))


def _params(*semantics):
    return pltpu.CompilerParams(dimension_semantics=semantics, vmem_limit_bytes=VMEM_LIMIT_BYTES)


def _inproj_call(x, norm_w, w_main, w_small, w_small_t, gather=(), tm=256):
    t_len, d = x.shape
    n = w_main.shape[1]
    ns = w_small.shape[1]
    nst = w_small_t.shape[0]
    ng = len(gather)
    steps = t_len // tm

    def body(*refs):
        x_ref, nw_ref, wm_ref, ws_ref, wst_ref = refs[:5]
        pm_ref, ps_ref, pst_ref, ht_ref, r_ref = refs[5 + ng:10 + ng]
        copies = lambda: _direct_copies(refs[5:5 + ng], refs[10 + ng:10 + 2 * ng], *refs[10 + 2 * ng:], (False,) * ng)
        if ng:
            pl.when(pl.program_id(0) == 0)(lambda: _start_all(copies()))
        xv = x_ref[...]
        r = lax.rsqrt(jnp.mean(xv * xv, axis=-1, keepdims=True) + EPS)
        h = xv * r * nw_ref[...]
        hb = h.astype(MXU_DTYPE)
        for n0 in range(0, n, 512):
            pm_ref[:, n0:n0 + 512] = jnp.dot(hb, wm_ref[:, n0:n0 + 512], preferred_element_type=F32)
        ps_ref[...] = jnp.dot(hb, ws_ref[...], preferred_element_type=F32)
        pst_ref[...] = lax.dot_general(wst_ref[...], hb, _NT, preferred_element_type=F32)
        ht_ref[...] = h.T.astype(MXU_DTYPE)
        r_ref[...] = r
        if ng:
            pl.when(pl.program_id(0) == steps - 1)(lambda: _wait_all(copies()))

    return pl.pallas_call(
        body, name="inproj",
        grid=(steps,),
        in_specs=[pl.BlockSpec((tm, d), lambda i: (i, 0)),
                  pl.BlockSpec((1, d), lambda i: (0, 0)),
                  pl.BlockSpec((d, n), lambda i: (0, 0)),
                  pl.BlockSpec((d, ns), lambda i: (0, 0)),
                  pl.BlockSpec((nst, d), lambda i: (0, 0))] + [_HBM] * ng,
        out_specs=[pl.BlockSpec((tm, n), lambda i: (i, 0)),
                   pl.BlockSpec((tm, ns), lambda i: (i, 0)),
                   pl.BlockSpec((nst, tm), lambda i: (0, i)),
                   pl.BlockSpec((d, tm), lambda i: (0, i)),
                   pl.BlockSpec((tm, 1), lambda i: (i, 0))] + [_HBM] * ng,
        out_shape=[jax.ShapeDtypeStruct((t_len, n), F32),
                   jax.ShapeDtypeStruct((t_len, ns), F32),
                   jax.ShapeDtypeStruct((nst, t_len), F32),
                   jax.ShapeDtypeStruct((d, t_len), MXU_DTYPE),
                   jax.ShapeDtypeStruct((t_len, 1), F32)] + _direct_out_shapes(gather, (False,) * ng),
        scratch_shapes=_direct_semaphores(ng) if ng else [],
        compiler_params=_params("arbitrary"),
    )(x, norm_w, w_main, w_small, w_small_t, *gather)


def _running_sum_mm(x, tri):
    hi = x.astype(MXU_DTYPE)
    lo = (x - hi.astype(F32)).astype(MXU_DTYPE)
    return jnp.dot(hi, tri, preferred_element_type=F32) + jnp.dot(lo, tri, preferred_element_type=F32)


def _sb_iotas():
    row_i = lax.broadcasted_iota(jnp.int32, (SB_BQ, SB_BLOCK), 0)
    col_i = lax.broadcasted_iota(jnp.int32, (SB_BQ, SB_BLOCK), 1)
    sq_r = lax.broadcasted_iota(jnp.int32, (SB_BLOCK, SB_BLOCK), 0)
    sq_c = lax.broadcasted_iota(jnp.int32, (SB_BLOCK, SB_BLOCK), 1)
    return row_i, col_i, sq_r, sq_c


SB_DIAG_BLOCKS = SB_BQ // SB_BLOCK
SB_EXP_FLOOR = -110.0


def _sb_keys_descending(qi, tile, carry, z_bounds, n_heads):
    n_free = SB_DIAG_BLOCKS * qi
    for j in range(SB_DIAG_BLOCKS - 1, -1, -1):
        carry = tile(n_free + j, True, carry, j * SB_BLOCK)

    def largest_exponent(c):
        worst = jnp.max(z_bounds[0] - c[1])
        for h in range(1, n_heads):
            worst = jnp.maximum(worst, jnp.max(z_bounds[h] - c[1 + h]))
        return worst

    def cond(state):
        return (state[0] < n_free) & (state[1] > SB_EXP_FLOOR)

    def body(state):
        c = tile(n_free - 1 - state[0], False, state[2:])
        return (state[0] + 1, largest_exponent(c), *c)

    out = lax.while_loop(cond, body, (jnp.int32(0), largest_exponent(carry), *carry))
    return out[2:], out[0]


def _sb_keys_ascending(qi, n_run, tile, carry):
    n_free = SB_DIAG_BLOCKS * qi
    carry = lax.fori_loop(0, n_run, lambda s, c: tile(n_free - n_run + s, False, c), carry)
    for j in range(SB_DIAG_BLOCKS):
        carry = tile(n_free + j, True, carry, j * SB_BLOCK)
    return carry


def _sb_fwd_call(proj, t_len):
    nq = t_len // SB_BQ
    scale = float(SB_HEAD_DIM) ** -0.5
    n_pairs = 512 // LANES
    per_pair = LANES // SB_HEAD_DIM

    def body(q_ref, k_ref, v_ref, o_ref, st_ref, nrun_ref):
        lane = lax.broadcasted_iota(jnp.int32, (1, LANES), 1)
        row_i, col_i, sq_r, sq_c = _sb_iotas()
        ge = (sq_r >= sq_c).astype(MXU_DTYPE)
        hms = [((lane // SB_HEAD_DIM) == hh).astype(F32) for hh in range(per_pair)]
        k_sq = k_ref[...] * k_ref[...]
        k_norms = [jnp.sqrt(jnp.max(jnp.sum(k_sq * hm, axis=-1, keepdims=True))) * (1.02 * scale) for hm in hms]

        def q_loop(qi, carry):
            r0 = pl.multiple_of(qi * SB_BQ, SB_BQ)
            rows = pl.ds(r0, SB_BQ)
            q_all = q_ref[rows, :]
            qms = [(q_all * (hm * scale)).astype(MXU_DTYPE) for hm in hms]
            z_bounds = [jnp.sqrt(jnp.sum(q_all * q_all * hm, axis=-1, keepdims=True)) * kn
                        for hm, kn in zip(hms, k_norms)]

            def tile(kj, masked, kc, lo=0):
                heads = range(per_pair)
                live = lambda a: a[lo:]
                merge = lambda old, new: new if lo == 0 else jnp.concatenate([old[:lo], new], axis=0)
                acc, cs = live(kc[0]), [live(c) for c in kc[1:]]
                s0 = pl.multiple_of(kj * SB_BLOCK, SB_BLOCK)
                cols = pl.ds(s0, SB_BLOCK)
                kb = k_ref[cols, :].astype(MXU_DTYPE)
                v_all = v_ref[cols, :]
                vms = [(v_all * hms[h]).astype(MXU_DTYPE) for h in heads]
                zs = [lax.dot_general(live(qms[h]), kb, _NT, preferred_element_type=F32) for h in heads]
                sps = [_softplus(z) for z in zs]
                if masked:
                    mask = (live(col_i) + s0) < (live(row_i) + r0)
                    sps = [jnp.where(mask, sp, 0.0) for sp in sps]
                sums = [_running_sum_mm(sp, ge) for sp in sps]
                ws = [jnp.exp(zs[h] - (sums[h] + cs[h])) for h in heads]
                if masked:
                    ws = [jnp.where(mask, w, 0.0) for w in ws]
                for h in heads:
                    acc = acc + jnp.dot(ws[h].astype(MXU_DTYPE), vms[h], preferred_element_type=F32)
                cs = [cs[h] + jnp.sum(sps[h], axis=-1, keepdims=True) for h in heads]
                return tuple(merge(old, new) for old, new in zip(kc, (acc, *cs)))

            zero_col = jnp.zeros((SB_BQ, 1), F32)
            out, n_run = _sb_keys_descending(
                qi, tile, (jnp.zeros((SB_BQ, LANES), F32),) + (zero_col,) * per_pair, z_bounds, per_pair)
            o_ref[rows, :] = out[0]
            for hh in range(per_pair):
                st_ref[hh, rows, :] = out[1 + hh]
            nrun_ref[pl.program_id(0), qi] = n_run
            return carry

        lax.fori_loop(0, nq, q_loop, 0)

    return pl.pallas_call(
        body, name="sb_fwd",
        grid=(n_pairs,),
        in_specs=[pl.BlockSpec((t_len, LANES), lambda p: (0, p)),
                  pl.BlockSpec((t_len, LANES), lambda p: (0, n_pairs + p)),
                  pl.BlockSpec((t_len, LANES), lambda p: (0, 2 * n_pairs + p))],
        out_specs=[pl.BlockSpec((t_len, LANES), lambda p: (0, p)),
                   pl.BlockSpec((per_pair, t_len, 1), lambda p: (p, 0, 0)),
                   pl.BlockSpec(memory_space=pltpu.SMEM)],
        out_shape=[jax.ShapeDtypeStruct((t_len, 512), F32),
                   jax.ShapeDtypeStruct((n_pairs * per_pair, t_len, 1), F32),
                   jax.ShapeDtypeStruct((n_pairs, nq), jnp.int32)],
        compiler_params=_params("arbitrary"),
    )(proj, proj, proj)


def _sb_bwd_call(proj, sp_total, n_run_all, d_o, dproj, t_len):
    nq = t_len // SB_BQ
    scale = float(SB_HEAD_DIM) ** -0.5
    n_pairs = 512 // LANES
    per_pair = LANES // SB_HEAD_DIM

    def body(q_ref, k_ref, v_ref, st_ref, nrun_ref, do_ref, dproj_in_ref, d_ref):
        lane = lax.broadcasted_iota(jnp.int32, (1, LANES), 1)
        row_i, col_i, sq_r, sq_c = _sb_iotas()
        lt = (sq_r < sq_c).astype(MXU_DTYPE)
        le = (sq_r <= sq_c).astype(MXU_DTYPE)
        hms = [((lane // SB_HEAD_DIM) == hh).astype(F32) for hh in range(per_pair)]
        d_ref[1] = jnp.zeros((t_len, LANES), F32)
        d_ref[2] = jnp.zeros((t_len, LANES), F32)

        def q_loop(qi, carry):
            r0 = pl.multiple_of(qi * SB_BQ, SB_BQ)
            rows = pl.ds(r0, SB_BQ)
            q_all, do_all = q_ref[rows, :], do_ref[rows, :]
            qms = [(q_all * (hm * scale)).astype(MXU_DTYPE) for hm in hms]
            doms = [(do_all * hm).astype(MXU_DTYPE) for hm in hms]
            totals = [st_ref[hh, rows, :] for hh in range(per_pair)]

            def tile(kj, masked, kc, lo=0):
                heads = range(per_pair)
                live = lambda a: a[lo:]
                merge = lambda old, new: new if lo == 0 else jnp.concatenate([old[:lo], new], axis=0)
                dq = live(kc[0])
                cls, gls = [live(c) for c in kc[1:1 + per_pair]], [live(c) for c in kc[1 + per_pair:]]
                q_live, do_live = [live(a) for a in qms], [live(a) for a in doms]
                s0 = pl.multiple_of(kj * SB_BLOCK, SB_BLOCK)
                cols = pl.ds(s0, SB_BLOCK)
                k_all, v_all = k_ref[cols, :], v_ref[cols, :]
                kb = k_all.astype(MXU_DTYPE)
                vms = [(v_all * hms[h]).astype(MXU_DTYPE) for h in heads]
                kms = [(k_all * (hms[h] * scale)).astype(MXU_DTYPE) for h in heads]
                zs = [lax.dot_general(q_live[h], kb, _NT, preferred_element_type=F32) for h in heads]
                das = [lax.dot_general(do_live[h], vms[h], _NT, preferred_element_type=F32) for h in heads]
                sp_alls = [_softplus(z) for z in zs]
                sps = sp_alls
                if masked:
                    mask = (live(col_i) + s0) < (live(row_i) + r0)
                    sps = [jnp.where(mask, sp, 0.0) for sp in sp_alls]
                lefts = [_running_sum_mm(sp, lt) for sp in sps]
                ws = [jnp.exp(zs[h] - (live(totals[h]) - cls[h] - lefts[h])) for h in heads]
                if masked:
                    ws = [jnp.where(mask, w, 0.0) for w in ws]
                gs = [das[h] * ws[h] for h in heads]
                g_sums = [_running_sum_mm(g, le) for g in gs]
                dzs = [gs[h] - jnp.exp(zs[h] - sp_alls[h]) * (gls[h] + g_sums[h]) for h in heads]
                if masked:
                    dzs = [jnp.where(mask, dz, 0.0) for dz in dzs]
                dzs = [dz.astype(MXU_DTYPE) for dz in dzs]
                dk_t = jnp.zeros((SB_BLOCK, LANES), F32)
                dv_t = jnp.zeros((SB_BLOCK, LANES), F32)
                for h in heads:
                    dq = dq + jnp.dot(dzs[h], kms[h], preferred_element_type=F32)
                    dk_t = dk_t + lax.dot_general(dzs[h], q_live[h], _TN, preferred_element_type=F32)
                    dv_t = dv_t + lax.dot_general(ws[h].astype(MXU_DTYPE), do_live[h], _TN,
                                                  preferred_element_type=F32)
                d_ref[1, cols, :] += dk_t
                d_ref[2, cols, :] += dv_t
                cls = [cls[h] + jnp.sum(sps[h], axis=-1, keepdims=True) for h in heads]
                gls = [gls[h] + jnp.sum(gs[h], axis=-1, keepdims=True) for h in heads]
                return tuple(merge(old, new) for old, new in zip(kc, (dq, *cls, *gls)))

            zero_col = jnp.zeros((SB_BQ, 1), F32)
            out = _sb_keys_ascending(qi, nrun_ref[pl.program_id(0), qi], tile,
                                     (jnp.zeros((SB_BQ, LANES), F32),) + (zero_col,) * (2 * per_pair))
            d_ref[0, rows, :] = out[0]
            return carry

        lax.fori_loop(0, nq, q_loop, 0)

    col = lambda off: pl.BlockSpec((t_len, LANES), lambda p: (0, off + p))
    return pl.pallas_call(
        body, name="sb_bwd",
        grid=(n_pairs,),
        in_specs=[col(0), col(n_pairs), col(2 * n_pairs),
                  pl.BlockSpec((per_pair, t_len, 1), lambda p: (p, 0, 0)),
                  pl.BlockSpec(memory_space=pltpu.SMEM), col(0), _HBM],
        out_specs=pl.BlockSpec((3, t_len, LANES), lambda p: (DPROJ_SB_SLOT // 3, 0, p)),
        out_shape=jax.ShapeDtypeStruct(dproj.shape, dproj.dtype),
        input_output_aliases={6: 0},
        compiler_params=_params("arbitrary"),
    )(proj, proj, proj, sp_total, n_run_all, d_o, dproj)


def _conv_taps(xin, rows, t_len):
    taps = []
    for i in range(CONV_WIDTH):
        shift = CONV_WIDTH - 1 - i
        if shift == 0:
            taps.append(xin)
        else:
            taps.append(jnp.where(rows >= shift, pltpu.roll(xin, shift, axis=0), 0.0))
    return taps


def _gdn_prep_body_common(x_ref, w_ref, t_len):
    j = pl.program_id(0)
    xin = x_ref[...]
    rows = lax.broadcasted_iota(jnp.int32, (t_len, LANES), 0)
    taps = _conv_taps(xin, rows, t_len)
    pre = taps[0] * w_ref[0:1, :]
    for i in range(1, CONV_WIDTH):
        pre = pre + taps[i] * w_ref[i:i + 1, :]
    sg = _sigmoid(pre)
    act = pre * sg
    is_qk = j < 2 * GDN_HEADS
    nrm = jnp.where(is_qk, lax.rsqrt(jnp.sum(act * act, axis=-1, keepdims=True) + EPS), 1.0)
    sc = jnp.where(j < GDN_HEADS, float(GDN_HEAD_DIM) ** -0.5, 1.0)
    return j, rows, taps, pre, sg, act, is_qk, nrm, sc


def _gdn_prep_call(proj, conv_w, t_len):
    first = 2048 // LANES

    def body(x_ref, w_ref, out_ref):
        _, _, _, _, _, act, _, nrm, sc = _gdn_prep_body_common(x_ref, w_ref, t_len)
        out_ref[...] = act * nrm * sc

    return pl.pallas_call(
        body, name="gdn_prep",
        grid=(3 * GDN_HEADS,),
        in_specs=[pl.BlockSpec((t_len, LANES), lambda j: (0, first + j)),
                  pl.BlockSpec((CONV_WIDTH, LANES), lambda j: (0, j))],
        out_specs=pl.BlockSpec((t_len, LANES), lambda j: (0, j)),
        out_shape=jax.ShapeDtypeStruct((t_len, 3 * 512), F32),
        compiler_params=_params("arbitrary"),
    )(proj, conv_w)


def _gdn_prep_bwd_call(proj, conv_w, d_act3, dproj, t_len):
    first = 2048 // LANES

    def body(x_ref, w_ref, d_ref, dproj_in_ref, dx_ref, dw_ref):
        _, rows, taps, pre, sg, act, is_qk, nrm, sc = _gdn_prep_body_common(x_ref, w_ref, t_len)
        d_out = d_ref[0]
        dn = d_out * sc
        d_norm = nrm * dn - act * (nrm * nrm * nrm) * jnp.sum(dn * act, axis=-1, keepdims=True)
        d_act = jnp.where(is_qk, d_norm, d_out)
        d_pre = d_act * sg * (1.0 + pre * (1.0 - sg))
        dx = d_pre * w_ref[CONV_WIDTH - 1:CONV_WIDTH, :]
        dw_ref[CONV_WIDTH - 1:CONV_WIDTH, :] = jnp.sum(d_pre * taps[CONV_WIDTH - 1], axis=0, keepdims=True)
        for i in range(CONV_WIDTH - 1):
            shift = CONV_WIDTH - 1 - i
            up = jnp.where(rows < t_len - shift, pltpu.roll(d_pre, t_len - shift, axis=0), 0.0)
            dx = dx + up * w_ref[i:i + 1, :]
            dw_ref[i:i + 1, :] = jnp.sum(d_pre * taps[i], axis=0, keepdims=True)
        dx_ref[0] = dx

    return pl.pallas_call(
        body, name="gdn_prep_bwd",
        grid=(3 * GDN_HEADS,),
        in_specs=[pl.BlockSpec((t_len, LANES), lambda j: (0, first + j)),
                  pl.BlockSpec((CONV_WIDTH, LANES), lambda j: (0, j)),
                  pl.BlockSpec((1, t_len, LANES), lambda j: (j // GDN_HEADS, 0, j % GDN_HEADS)), _HBM],
        out_specs=[pl.BlockSpec((1, t_len, LANES), lambda j: (DPROJ_GDN_SLOT + j // GDN_HEADS, 0, j % GDN_HEADS)),
                   pl.BlockSpec((CONV_WIDTH, LANES), lambda j: (0, j))],
        out_shape=[jax.ShapeDtypeStruct(dproj.shape, dproj.dtype),
                   jax.ShapeDtypeStruct((CONV_WIDTH, 3 * 512), F32)],
        input_output_aliases={3: 0},
        compiler_params=_params("arbitrary"),
    )(proj, conv_w, d_act3, dproj)


def _chunk_cumsum_matrix():
    r = lax.broadcasted_iota(jnp.int32, (LANES, LANES), 0)
    c = lax.broadcasted_iota(jnp.int32, (LANES, LANES), 1)
    return ((r <= c) & ((r // CHUNK) == (c // CHUNK))).astype(F32)


def _gdn_gates_call(ps, pst, alog_l, dtb_l, alog_c, dtb_c, t_len):
    def body(ps_ref, pst_ref, al_ref, dl_ref, ac_ref, dc_ref, beta_ref, gcol_ref, grow_ref):
        upper = _chunk_cumsum_matrix()
        lower = upper.T
        psv = ps_ref[...]
        beta_ref[...] = _sigmoid(psv)
        g_l = -jnp.exp(al_ref[...]) * _softplus(psv + dl_ref[...])
        g_r = -jnp.exp(ac_ref[...]) * _softplus(pst_ref[...] + dc_ref[...])
        for w in range(t_len // LANES):
            sl = slice(w * LANES, (w + 1) * LANES)
            gcol_ref[sl, :] = _mx(lower, g_l[sl, :])
            grow_ref[:, sl] = _mx(g_r[:, sl], upper)

    vm = pl.BlockSpec(memory_space=pltpu.VMEM)
    return pl.pallas_call(
        body, name="gdn_gates",
        in_specs=[vm] * 6, out_specs=[vm] * 3,
        out_shape=[jax.ShapeDtypeStruct((t_len, LANES), F32),
                   jax.ShapeDtypeStruct((t_len, LANES), F32),
                   jax.ShapeDtypeStruct((8, t_len), F32)],
        compiler_params=pltpu.CompilerParams(vmem_limit_bytes=VMEM_LIMIT_BYTES),
    )(ps, pst, alog_l, dtb_l, alog_c, dtb_c)


def _gdn_gates_bwd_call(ps, alog_l, dtb_l, d_l, t_len):
    def body(ps_ref, al_ref, dl_ref, d_ref, dps_ref, gal_ref, gdt_ref):
        lane = lax.broadcasted_iota(jnp.int32, (1, LANES), 1)
        psv = ps_ref[...]
        dv = d_ref[...]
        beta = _sigmoid(psv)
        ea = jnp.exp(al_ref[...])
        arg = psv + dl_ref[...]
        g = -ea * _softplus(arg)
        d_a = dv * (-ea) * _sigmoid(arg)
        is_a = (lane >= GDN_HEADS) & (lane < 2 * GDN_HEADS)
        dps_ref[...] = jnp.where(lane < GDN_HEADS, dv * beta * (1.0 - beta), jnp.where(is_a, d_a, 0.0))
        gdt_ref[...] = jnp.where(is_a, jnp.sum(d_a, axis=0, keepdims=True), 0.0)
        gal_ref[...] = jnp.where(is_a, jnp.sum(dv * g, axis=0, keepdims=True), 0.0)

    vm = pl.BlockSpec(memory_space=pltpu.VMEM)
    return pl.pallas_call(
        body, name="gdn_gates_bwd",
        in_specs=[vm] * 4, out_specs=[vm] * 3,
        out_shape=[jax.ShapeDtypeStruct((t_len, LANES), F32),
                   jax.ShapeDtypeStruct((1, LANES), F32),
                   jax.ShapeDtypeStruct((1, LANES), F32)],
        compiler_params=pltpu.CompilerParams(vmem_limit_bytes=VMEM_LIMIT_BYTES),
    )(ps, alog_l, dtb_l, d_l)


def _bm(a, b):
    return _m3_general(a, b, _BNN)


def _bm_nt(a, b):
    return _m3_general(a, b, _BNT)


def _bm_tn(a, b):
    return _m3_general(a, b, _BTN)


def _heads_of(ref, rows):
    return jnp.stack([ref[rows, h * GDN_HEAD_DIM:(h + 1) * GDN_HEAD_DIM] for h in range(GDN_HEADS)])


def _chunk_terms(q_ref, k_ref, v_ref, b_ref, gc_ref, gr_ref, c, incl, strict, n=1):
    r0 = c * CHUNK if isinstance(c, int) else pl.multiple_of(c * CHUNK, CHUNK)
    rows = pl.ds(r0, n * CHUNK)
    per_chunk = lambda x: x.reshape(GDN_HEADS * n, CHUNK, x.shape[-1])
    q, k, v = (per_chunk(_heads_of(ref, rows)) for ref in (q_ref, k_ref, v_ref))
    lane_ids = lax.broadcasted_iota(jnp.int32, (1, LANES), 1)
    pick = lambda slab, first: jnp.stack([jnp.sum(jnp.where(lane_ids == first + h, slab, 0.0), axis=-1, keepdims=True)
                                          for h in range(GDN_HEADS)])
    b = per_chunk(pick(b_ref[rows, :], 0))
    gc = per_chunk(pick(gc_ref[rows, :], GDN_HEADS))
    gr = gr_ref[:, c] if n == 1 else gr_ref[:, c:c + n].reshape(GDN_HEADS * n, 1, CHUNK)
    dm = jnp.where(incl, jnp.exp(jnp.where(incl, gc - gr, 0.0)), 0.0)
    kb = k * b
    vb = v * b
    e = jnp.exp(gc)
    kk_qk = _bm_nt(jnp.concatenate([kb, q], axis=1), k)
    a = jnp.where(strict, kk_qk[:, :CHUNK] * dm, 0.0)
    p = jnp.where(incl, kk_qk[:, CHUNK:] * dm, 0.0)
    gl = gc[:, CHUNK - 1:CHUNK, :]
    eg = jnp.exp(gl - gc)
    return rows, q, k, v, b, gc, dm, kb, vb, e, a, p, gl, eg


def _unit_lower_inverse(a, eye):
    x = -a
    tm = eye + x
    xp = _bm(x, x)
    for _ in range(4):
        both = _bm(jnp.concatenate([xp, tm], axis=1), xp)
        tm = tm + both[:, CHUNK:]
        xp = both[:, :CHUNK]
    return tm + _bm(tm, xp)


def _gdn_specs(t_len, n_chunks, reverse):
    cps = GDN_CHUNKS_PER_STEP
    steps = n_chunks // cps
    at = (lambda g: steps - 1 - g) if reverse else (lambda g: g)
    rows_blk = lambda width, part=0: pl.BlockSpec((cps * CHUNK, width), lambda g: (at(g), part))
    gate_r = pl.BlockSpec((GDN_HEADS, cps, 1, CHUNK), lambda g: (0, at(g), 0, 0))
    per_chunk = lambda r, c: pl.BlockSpec((GDN_HEADS, cps, r, c), lambda g: (0, at(g), 0, 0))
    return cps, steps, rows_blk, gate_r, per_chunk


def _gdn_fwd_call(gact, beta_c, gam_c, gam_r, t_len):
    n_chunks = t_len // CHUNK
    dk = GDN_HEAD_DIM
    width = GDN_HEADS * dk
    cps, steps, rows_blk, gate_r, per_chunk = _gdn_specs(t_len, n_chunks, False)

    def body(q_ref, k_ref, v_ref, b_ref, gc_ref, gr_ref, o_ref, s_ref, t_ref, state_ref):
        row = lax.broadcasted_iota(jnp.int32, (CHUNK, CHUNK), 0)
        col = lax.broadcasted_iota(jnp.int32, (CHUNK, CHUNK), 1)
        incl, strict = row >= col, row > col
        eye = (row == col).astype(F32)

        @pl.when(pl.program_id(0) == 0)
        def _():
            state_ref[...] = jnp.zeros_like(state_ref)

        _, q, k, v, b, gc, dm, kb, vb, e, a, p, gl, eg = _chunk_terms(
            q_ref, k_ref, v_ref, b_ref, gc_ref, gr_ref, 0, incl, strict, cps)
        tm = _unit_lower_inverse(a, eye)
        uw = _bm(tm, jnp.concatenate([vb, kb * e], axis=2))
        w_qe = jnp.concatenate([uw[:, :, dk:], q * e], axis=1)
        u, kd, decay = uw[:, :, :dk], k * eg, jnp.exp(gl)
        t_ref[...] = tm.reshape(GDN_HEADS, cps, CHUNK, CHUNK)

        of_chunk = lambda x, c: jnp.stack([x[h * cps + c] for h in range(GDN_HEADS)])
        s = state_ref[...]
        for c in range(cps):
            ws_qs = _bm(of_chunk(w_qe, c), s)
            vn = of_chunk(u, c) - ws_qs[:, :CHUNK]
            o = ws_qs[:, CHUNK:] + _bm(of_chunk(p, c), vn)
            for h in range(GDN_HEADS):
                o_ref[c * CHUNK:(c + 1) * CHUNK, h * dk:(h + 1) * dk] = o[h]
            s_ref[:, c] = s
            s = s * of_chunk(decay, c) + _bm_tn(of_chunk(kd, c), vn)
        state_ref[...] = s

    return pl.pallas_call(
        body, name="gdn_fwd",
        grid=(steps,),
        in_specs=[rows_blk(width, 0), rows_blk(width, 1), rows_blk(width, 2), rows_blk(LANES), rows_blk(LANES), gate_r],
        out_specs=[rows_blk(width), per_chunk(dk, dk), per_chunk(CHUNK, CHUNK)],
        out_shape=[jax.ShapeDtypeStruct((t_len, width), F32),
                   jax.ShapeDtypeStruct((GDN_HEADS, n_chunks, dk, dk), F32),
                   jax.ShapeDtypeStruct((GDN_HEADS, n_chunks, CHUNK, CHUNK), F32)],
        scratch_shapes=[pltpu.VMEM((GDN_HEADS, dk, dk), F32)],
        compiler_params=_params("arbitrary"),
    )(gact, gact, gact, beta_c, gam_c, gam_r)


def _gdn_bwd_call(gact, beta_c, gam_c, gam_r, s_all, t_all, d_o, t_len, scatter=()):
    n_chunks = t_len // CHUNK
    dk = GDN_HEAD_DIM
    width = GDN_HEADS * dk
    cps, steps, rows_blk, gate_r, per_chunk = _gdn_specs(t_len, n_chunks, True)
    nx = len(scatter)

    def body(*refs):
        q_ref, k_ref, v_ref, b_ref, gc_ref, gr_ref, s_ref, t_ref, do_ref = refs[:9]
        d_ref, dgate_ref = refs[9 + nx:11 + nx]
        dstate_ref = refs[11 + 2 * nx]
        copies = lambda: _direct_copies(refs[9:9 + nx], refs[11 + nx:11 + 2 * nx], *refs[12 + 2 * nx:], (True,) * nx)
        if nx:
            pl.when(pl.program_id(0) == 0)(lambda: _start_all(copies()))
        row = lax.broadcasted_iota(jnp.int32, (CHUNK, CHUNK), 0)
        col = lax.broadcasted_iota(jnp.int32, (CHUNK, CHUNK), 1)
        incl, strict = row >= col, row > col
        ng = GDN_BWD_GROUP
        nb = GDN_HEADS * ng
        upper = jnp.broadcast_to((row <= col).astype(F32), (nb, CHUNK, CHUNK))
        ones = jnp.ones((nb, CHUNK, LANES), F32)
        last_row = lax.broadcasted_iota(jnp.int32, (CHUNK, 1), 0) == CHUNK - 1
        lane_ids = lax.broadcasted_iota(jnp.int32, (1, LANES), 1)
        rsum = lambda m: jnp.sum(m, axis=-1, keepdims=True)
        total = lambda m: jnp.sum(rsum(m), axis=1, keepdims=True)
        of_chunk = lambda x, c: jnp.stack([x[h * ng + c] for h in range(GDN_HEADS)])

        @pl.when(pl.program_id(0) == 0)
        def _():
            dstate_ref[...] = jnp.zeros_like(dstate_ref)

        for c0 in range(cps - ng, -1, -ng):
            group(c0, q_ref, k_ref, v_ref, b_ref, gc_ref, gr_ref, s_ref, t_ref, do_ref, d_ref, dgate_ref, dstate_ref,
                  incl, strict, upper, ones, last_row, lane_ids, rsum, total, of_chunk)
        if nx:
            pl.when(pl.program_id(0) == steps - 1)(lambda: _wait_all(copies()))

    def group(c0, q_ref, k_ref, v_ref, b_ref, gc_ref, gr_ref, s_ref, t_ref, do_ref, d_ref, dgate_ref, dstate_ref,
              incl, strict, upper, ones, last_row, lane_ids, rsum, total, of_chunk):
        ng = GDN_BWD_GROUP
        nb = GDN_HEADS * ng
        rows = pl.ds(c0 * CHUNK, ng * CHUNK)
        _, q, k, v, b, gc, dm, kb, vb, e, a, p, gl, eg = _chunk_terms(
            q_ref, k_ref, v_ref, b_ref, gc_ref, gr_ref, c0, incl, strict, ng)
        s = s_ref[:, c0:c0 + ng].reshape(nb, dk, dk)
        tm = t_ref[:, c0:c0 + ng].reshape(nb, CHUNK, CHUNK)
        d_out = _heads_of(do_ref, rows).reshape(nb, CHUNK, dk)
        el = jnp.exp(gl)
        kbe = kb * e
        qe = q * e
        kd = k * eg
        uw = _bm(tm, jnp.concatenate([vb, kbe], axis=2))
        u, w = uw[:, :, :dk], uw[:, :, dk:]
        vn = u - _bm(w, s)
        pt_do = _bm_tn(p, d_out)
        qet_do = _bm_tn(qe, d_out)

        ds = dstate_ref[...]
        d_vn_c, ds_c = [None] * ng, [None] * ng
        for c in range(ng - 1, -1, -1):
            ds_c[c] = ds
            d_vn_c[c] = of_chunk(pt_do, c) + _bm(of_chunk(kd, c), ds)
            ds = of_chunk(el, c) * ds + of_chunk(qet_do, c) - _bm_tn(of_chunk(w, c), d_vn_c[c])
        dstate_ref[...] = ds
        by_chunk = lambda xs: jnp.stack([xs[c][h] for h in range(GDN_HEADS) for c in range(ng)])
        d_vn, ds = by_chunk(d_vn_c), by_chunk(ds_c)

        on_s = _bm_nt(jnp.concatenate([d_out, d_vn], axis=1), s)
        d_qe, d_w = on_s[:, :CHUNK], -on_s[:, CHUNK:]
        d_p = jnp.where(incl, _bm_nt(d_out, vn), 0.0)
        d_kd = _bm_nt(vn, ds)
        d_both = _bm_tn(tm, jnp.concatenate([d_vn, d_w], axis=2))
        d_vb, d_kbe = d_both[:, :, :dk], d_both[:, :, dk:]
        d_a = -jnp.where(strict, _bm_nt(d_both, uw), 0.0)
        m = d_a * dm
        n = d_p * dm
        on_k = _bm(jnp.concatenate([m, n], axis=1), k)
        d_kb = on_k[:, :CHUNK] + d_kbe * e
        d_q = on_k[:, CHUNK:] + d_qe * e
        d_k = (_bm_tn(jnp.concatenate([m, n], axis=1), jnp.concatenate([kb, q], axis=1))
               + d_kd * eg + b * d_kb)
        d_v = b * d_vb
        r = d_a * a + d_p * p
        kd_term = rsum(d_kd * kd)
        d_gl = total(ds * s) * el + jnp.sum(kd_term, axis=1, keepdims=True)
        d_gam = (rsum(r) - _bm_tn(r, ones)[:, :, 0:1] + rsum(d_qe * qe) + rsum(d_kbe * kbe) - kd_term
                 + jnp.where(last_row, d_gl, 0.0))
        d_beta = rsum(d_kb * k) + rsum(d_vb * v)
        d_g = _bm(upper, d_gam * ones)[:, :, 0:1]
        per_head = lambda x: x.reshape(GDN_HEADS, ng * CHUNK, x.shape[-1])
        d_q, d_k, d_v, d_beta, d_g = (per_head(x) for x in (d_q, d_k, d_v, d_beta, d_g))
        gates = jnp.zeros((ng * CHUNK, LANES), F32)
        for h in range(GDN_HEADS):
            lanes = slice(h * dk, (h + 1) * dk)
            d_ref[0, rows, lanes] = d_q[h]
            d_ref[1, rows, lanes] = d_k[h]
            d_ref[2, rows, lanes] = d_v[h]
            gates = gates + (jnp.where(lane_ids == h, d_beta[h], 0.0)
                             + jnp.where(lane_ids == GDN_HEADS + h, d_g[h], 0.0))
        dgate_ref[rows, :] = gates

    d_spec = pl.BlockSpec((3, cps * CHUNK, width), lambda g: (0, steps - 1 - g, 0))
    return pl.pallas_call(
        body, name="gdn_bwd",
        grid=(steps,),
        in_specs=[rows_blk(width, 0), rows_blk(width, 1), rows_blk(width, 2), rows_blk(LANES), rows_blk(LANES), gate_r,
                  per_chunk(dk, dk), per_chunk(CHUNK, CHUNK), rows_blk(width)] + [_HBM] * nx,
        out_specs=[d_spec, rows_blk(LANES)] + [_HBM] * nx,
        out_shape=[jax.ShapeDtypeStruct((3, t_len, width), F32),
                   jax.ShapeDtypeStruct((t_len, LANES), F32)] + _direct_out_shapes(scatter, (True,) * nx),
        scratch_shapes=[pltpu.VMEM((GDN_HEADS, dk, dk), F32)] + (_direct_semaphores(nx) if nx else []),
        compiler_params=_params("arbitrary"),
    )(gact, gact, gact, beta_c, gam_c, gam_r, s_all, t_all, d_o, *scatter)


def _group_matrix(width, group):
    r = lax.broadcasted_iota(jnp.int32, (width, width), 0)
    c = lax.broadcasted_iota(jnp.int32, (width, width), 1)
    return ((r // group) == (c // group)).astype(F32)


def _post_call(o_sb, o_gd, proj, x, target, w_out, sbw, gdw, fw, tm=256):
    t_len, d = x.shape
    half = 512
    zsb_blk = 1536 // half
    zgd_blk = 3584 // half

    def body(osb_ref, ogd_ref, zsb_ref, zgd_ref, x_ref, tg_ref, wo_ref, sbw_ref, gdw_ref, fw_ref,
             dx2_ref, dosb_ref, dogd_ref, dz_ref, loss_ref, gfw_ref, gsb_ref, ggd_ref, gwo_ref):
        step = pl.program_id(0)

        @pl.when(step == 0)
        def _():
            loss_ref[...] = jnp.zeros_like(loss_ref)
            gfw_ref[...] = jnp.zeros_like(gfw_ref)
            gsb_ref[...] = jnp.zeros_like(gsb_ref)
            ggd_ref[...] = jnp.zeros_like(ggd_ref)
            gwo_ref[...] = jnp.zeros_like(gwo_ref)

        def head_forward(o, z, w, gmat, inv):
            r = lax.rsqrt(_running_sum_mm(o * o, gmat) * inv + EPS)
            nrm = o * r * w
            sg = _sigmoid(z)
            return r, nrm, sg, nrm * (z * sg)

        def head_backward(d_m, o, z, w, gmat, inv, r, nrm, sg):
            d_n = d_m * (z * sg)
            d_z = d_m * nrm * (sg * (1.0 + z * (1.0 - sg)))
            dnw = d_n * w
            d_o = r * dnw - o * (r * r * r) * (_running_sum_mm(dnw * o, gmat) * inv)
            return d_o, d_z, jnp.sum(d_n * o * r, axis=0, keepdims=True)

        g_sb = _group_matrix(half, SB_HEAD_DIM).astype(MXU_DTYPE)
        g_gd = _group_matrix(half, GDN_HEAD_DIM).astype(MXU_DTYPE)
        osb, ogd, zsb, zgd = osb_ref[...], ogd_ref[...], zsb_ref[...], zgd_ref[...]
        sbw_v, gdw_v = sbw_ref[...], gdw_ref[...]
        r_sb, n_sb, sg_sb, m_sb = head_forward(osb, zsb, sbw_v, g_sb, 1.0 / SB_HEAD_DIM)
        r_gd, n_gd, sg_gd, m_gd = head_forward(ogd, zgd, gdw_v, g_gd, 1.0 / GDN_HEAD_DIM)
        mixed = jnp.concatenate([m_sb, m_gd], axis=1).astype(MXU_DTYPE)
        wo = wo_ref[...]
        x2 = x_ref[...] + jnp.dot(mixed, wo, preferred_element_type=F32)
        r2 = lax.rsqrt(jnp.mean(x2 * x2, axis=-1, keepdims=True) + EPS)
        fw_v = fw_ref[...]
        err = x2 * r2 * fw_v - tg_ref[...]
        loss_ref[...] += 0.5 * jnp.sum(jnp.sum(err * err, axis=-1, keepdims=True) * (1.0 / d))
        dy = err * (1.0 / d)
        gg = dy * fw_v
        dx2 = r2 * gg - x2 * ((r2 * r2 * r2) * jnp.mean(gg * x2, axis=-1, keepdims=True))
        gfw_ref[...] += jnp.sum(dy * x2 * r2, axis=0, keepdims=True)
        dx2_ref[...] = dx2
        dx2b = dx2.astype(MXU_DTYPE)
        d_mixed = lax.dot_general(dx2b, wo, _NT, preferred_element_type=F32)
        gwo_ref[...] += lax.dot_general(mixed, dx2b, _TN, preferred_element_type=F32)
        d_osb, d_zsb, gsb = head_backward(d_mixed[:, :half], osb, zsb, sbw_v, g_sb, 1.0 / SB_HEAD_DIM, r_sb, n_sb, sg_sb)
        d_ogd, d_zgd, ggd = head_backward(d_mixed[:, half:], ogd, zgd, gdw_v, g_gd, 1.0 / GDN_HEAD_DIM, r_gd, n_gd, sg_gd)
        dosb_ref[...] = d_osb
        dogd_ref[...] = d_ogd
        dz_ref[0] = d_zsb
        dz_ref[1] = d_zgd
        gsb_ref[...] += gsb
        ggd_ref[...] += ggd

    row_blk = lambda w: pl.BlockSpec((tm, w), lambda i: (i, 0))
    fixed = lambda r, w: pl.BlockSpec((r, w), lambda i: (0, 0))
    return pl.pallas_call(
        body, name="post",
        grid=(t_len // tm,),
        in_specs=[row_blk(half), row_blk(half),
                  pl.BlockSpec((tm, half), lambda i: (i, zsb_blk)),
                  pl.BlockSpec((tm, half), lambda i: (i, zgd_blk)),
                  row_blk(d), row_blk(d), fixed(d, d), fixed(1, half), fixed(1, half), fixed(1, d)],
        out_specs=[row_blk(d), row_blk(half), row_blk(half),
                   pl.BlockSpec((2, tm, half), lambda i: (DPROJ_GATE_SLOT // 2, i, 0)),
                   fixed(1, LANES), fixed(1, d), fixed(1, half), fixed(1, half), fixed(d, d)],
        out_shape=[jax.ShapeDtypeStruct((t_len, d), F32)] + [jax.ShapeDtypeStruct((t_len, half), F32)] * 2
                  + [jax.ShapeDtypeStruct((len(DPROJ_PIECE_OF_SLOT), t_len, half), F32),
                     jax.ShapeDtypeStruct((1, LANES), F32), jax.ShapeDtypeStruct((1, d), F32),
                     jax.ShapeDtypeStruct((1, half), F32), jax.ShapeDtypeStruct((1, half), F32),
                     jax.ShapeDtypeStruct((d, d), F32)],
        compiler_params=_params("arbitrary"),
    )(o_sb, o_gd, proj, proj, x, target, w_out, sbw, gdw, fw)


def _piece_of_slot(s):
    return jnp.where(s < DPROJ_GDN_SLOT, s, jnp.where(s < DPROJ_GATE_SLOT, s + 1,
                                                     jnp.where(s == DPROJ_GATE_SLOT, 3, 7)))


def _gw_in_call(h_t, dproj8):
    d, t_len = h_t.shape
    n_piece, _, pw = dproj8.shape

    def body(ht_ref, dp_ref, gw_ref):
        gw_ref[...] = jnp.dot(ht_ref[...], dp_ref[0].astype(MXU_DTYPE), preferred_element_type=F32)

    return pl.pallas_call(
        body, name="gw_in",
        grid=(n_piece,),
        in_specs=[pl.BlockSpec((d, t_len), lambda s: (0, 0)),
                  pl.BlockSpec((1, t_len, pw), lambda s: (s, 0, 0))],
        out_specs=pl.BlockSpec((d, pw), lambda s: (0, _piece_of_slot(s))),
        out_shape=jax.ShapeDtypeStruct((d, n_piece * pw), F32),
        compiler_params=_params("arbitrary"),
    )(h_t, dproj8)


def _gw_small_call(h_t, dsmall, tm=512):
    d, t_len = h_t.shape
    ns = dsmall.shape[1]

    def body(ht_ref, dp_ref, gw_ref):
        @pl.when(pl.program_id(0) == 0)
        def _():
            gw_ref[...] = jnp.zeros_like(gw_ref)

        gw_ref[...] += jnp.dot(ht_ref[...], dp_ref[...].astype(MXU_DTYPE), preferred_element_type=F32)

    return pl.pallas_call(
        body, name="gw_small",
        grid=(t_len // tm,),
        in_specs=[pl.BlockSpec((d, tm), lambda t: (0, t)),
                  pl.BlockSpec((tm, ns), lambda t: (t, 0))],
        out_specs=pl.BlockSpec((d, ns), lambda t: (0, 0)),
        out_shape=jax.ShapeDtypeStruct((d, ns), F32),
        compiler_params=_params("arbitrary"),
    )(h_t, dsmall)


def _dx_call(dproj8, dsmall, w_main, w_small, x, r, dx2, norm_w, tm=256):
    t_len, d = x.shape
    n_piece, _, pw = dproj8.shape
    ns = dsmall.shape[1]

    def body(dp_ref, ds_ref, wm_ref, ws_ref, x_ref, r_ref, dx2_ref, nw_ref, gx_ref, gnw_ref):
        @pl.when(pl.program_id(0) == 0)
        def _():
            gnw_ref[...] = jnp.zeros_like(gnw_ref)

        dh = lax.dot_general(ds_ref[...].astype(MXU_DTYPE), ws_ref[...], _NT, preferred_element_type=F32)
        for s, p in enumerate(DPROJ_PIECE_OF_SLOT):
            dh = dh + lax.dot_general(dp_ref[s].astype(MXU_DTYPE), wm_ref[:, p * pw:(p + 1) * pw], _NT,
                                      preferred_element_type=F32)
        xv, rv = x_ref[...], r_ref[...]
        dn = dh * nw_ref[...]
        gx_ref[...] = dx2_ref[...] + rv * dn - xv * ((rv * rv * rv) * jnp.mean(dn * xv, axis=-1, keepdims=True))
        gnw_ref[...] += jnp.sum(dh * xv * rv, axis=0, keepdims=True)

    return pl.pallas_call(
        body, name="dx",
        grid=(t_len // tm,),
        in_specs=[pl.BlockSpec((n_piece, tm, pw), lambda i: (0, i, 0)),
                  pl.BlockSpec((tm, ns), lambda i: (i, 0)),
                  pl.BlockSpec((d, n_piece * pw), lambda i: (0, 0)),
                  pl.BlockSpec((d, ns), lambda i: (0, 0)),
                  pl.BlockSpec((tm, d), lambda i: (i, 0)),
                  pl.BlockSpec((tm, 1), lambda i: (i, 0)),
                  pl.BlockSpec((tm, d), lambda i: (i, 0)),
                  pl.BlockSpec((1, d), lambda i: (0, 0))],
        out_specs=[pl.BlockSpec((tm, d), lambda i: (i, 0)),
                   pl.BlockSpec((1, d), lambda i: (0, 0))],
        out_shape=[jax.ShapeDtypeStruct((t_len, d), F32), jax.ShapeDtypeStruct((1, d), F32)],
        compiler_params=_params("arbitrary"),
    )(dproj8, dsmall, w_main, w_small, x, r, dx2, norm_w)


def _exchange_call(name, srcs, per_peer):
    n = len(srcs)

    def body(*refs):
        src_refs, out_refs = refs[:n], refs[n:2 * n]
        copies = _direct_copies(src_refs, out_refs, *refs[2 * n:], per_peer)
        _start_all(copies)
        _wait_all(copies)

    hbm = pl.BlockSpec(memory_space=pl.ANY)
    return pl.pallas_call(
        body, name=name,
        in_specs=[hbm] * n, out_specs=[hbm] * n, out_shape=_direct_out_shapes(srcs, per_peer),
        scratch_shapes=_direct_semaphores(n),
    )(*srcs)


def _direct_out_shapes(srcs, per_peer):
    return [jax.ShapeDtypeStruct(s.shape if pp else (N_DEV,) + s.shape, s.dtype) for s, pp in zip(srcs, per_peer)]


def _direct_semaphores(n):
    return [pltpu.SemaphoreType.DMA((n * (N_DEV - 1),)), pltpu.SemaphoreType.DMA((n * (N_DEV - 1),)),
            pltpu.SemaphoreType.DMA((n,))]


def _direct_copies(src_refs, out_refs, send_sems, recv_sems, local_sems, per_peer):
    x, y, c = lax.axis_index("x"), lax.axis_index("y"), lax.axis_index("c")
    me = 4 * x + 2 * y + c
    local, remote = [], []
    for a in range(len(src_refs)):
        mine = src_refs[a].at[me] if per_peer[a] else src_refs[a]
        local.append(pltpu.make_async_copy(mine, out_refs[a].at[me], local_sems.at[a]))
    for k in range(1, N_DEV):
        kx, ky, kc = (k >> 2) & 1, (k >> 1) & 1, k & 1
        px = 1 - x if kx else x
        py = 1 - y if ky else y
        pc = 1 - c if kc else c
        peer = 4 * px + 2 * py + pc
        for a in range(len(src_refs)):
            sem = a * (N_DEV - 1) + (k - 1)
            remote.append(pltpu.make_async_remote_copy(
                src_ref=src_refs[a].at[peer] if per_peer[a] else src_refs[a], dst_ref=out_refs[a].at[me],
                send_sem=send_sems.at[sem], recv_sem=recv_sems.at[sem],
                device_id=(px, py, pc), device_id_type=pl.DeviceIdType.MESH))
    return local, remote


def _start_all(copies):
    local, remote = copies
    for cp in local + remote:
        cp.start()


def _wait_all(copies):
    local, remote = copies
    for cp in remote:
        cp.wait_send()
    for cp in remote:
        cp.wait_recv()
    for cp in local:
        cp.wait()


N_CHIPS = 4
_HBM = pl.BlockSpec(memory_space=pl.ANY)
_MESH = pl.DeviceIdType.MESH


def _gather_call(name, srcs):
    n = len(srcs)
    per = N_DEV - 1

    def body(*refs):
        src_refs, out_refs = refs[:n], refs[n:2 * n]
        send_sems, recv_sems, local_sems = refs[2 * n:]
        x, y, c = lax.axis_index("x"), lax.axis_index("y"), lax.axis_index("c")
        me, sibling = (x, y, c), (x, y, 1 - c)
        chips = [(1 - x, y), (x, 1 - y), (1 - x, 1 - y)]
        slot = lambda px, py, pc: 4 * px + 2 * py + pc

        def copy(a, k, block, to, from_src=False):
            rows = out_refs[a].at[slot(*block)]
            return pltpu.make_async_remote_copy(
                src_ref=src_refs[a] if from_src else rows, dst_ref=rows,
                send_sem=send_sems.at[a * per + k], recv_sem=recv_sems.at[a * per + k],
                device_id=to, device_id_type=_MESH)

        local = [pltpu.make_async_copy(src_refs[a], out_refs[a].at[slot(*me)], local_sems.at[a]) for a in range(n)]
        for cp in local:
            cp.start()
        first = []
        for a in range(n):
            first.append(copy(a, 0, me, sibling, True))
            first += [copy(a, 1 + j, me, (*chip, c), True) for j, chip in enumerate(chips)]
        for cp in first:
            cp.start()
        passed = []
        for j, chip in enumerate(chips):
            for a in range(n):
                copy(a, 1 + j, (*chip, c), me).wait_recv()
                fwd = copy(a, 4 + j, (*chip, c), sibling)
                fwd.start()
                passed.append(fwd)
        for a in range(n):
            copy(a, 0, sibling, me).wait_recv()
            for j, chip in enumerate(chips):
                copy(a, 4 + j, (*chip, 1 - c), me).wait_recv()
        for cp in first + passed:
            cp.wait_send()
        for cp in local:
            cp.wait()

    return pl.pallas_call(
        body, name=name,
        in_specs=[_HBM] * n, out_specs=[_HBM] * n,
        out_shape=[jax.ShapeDtypeStruct((N_DEV,) + s.shape, s.dtype) for s in srcs],
        scratch_shapes=[pltpu.SemaphoreType.DMA((n * per,)), pltpu.SemaphoreType.DMA((n * per,)),
                        pltpu.SemaphoreType.DMA((n,))],
    )(*srcs)


def _sibling_send_call(name, srcs):
    n = len(srcs)

    def body(*refs):
        src_refs, out_refs = refs[:n], refs[n:2 * n]
        send_sems, recv_sems = refs[2 * n:]
        x, y, c = lax.axis_index("x"), lax.axis_index("y"), lax.axis_index("c")
        copies = []
        for a in range(n):
            for ch in range(N_CHIPS):
                copies.append(pltpu.make_async_remote_copy(
                    src_ref=src_refs[a].at[2 * ch + (1 - c)], dst_ref=out_refs[a].at[ch],
                    send_sem=send_sems.at[a * N_CHIPS + ch], recv_sem=recv_sems.at[a * N_CHIPS + ch],
                    device_id=(x, y, 1 - c), device_id_type=_MESH))
        for cp in copies:
            cp.start()
        for cp in copies:
            cp.wait_send()
        for cp in copies:
            cp.wait_recv()

    return pl.pallas_call(
        body, name=name,
        in_specs=[_HBM] * n, out_specs=[_HBM] * n,
        out_shape=[jax.ShapeDtypeStruct((N_CHIPS,) + s.shape[1:], s.dtype) for s in srcs],
        scratch_shapes=[pltpu.SemaphoreType.DMA((n * N_CHIPS,)), pltpu.SemaphoreType.DMA((n * N_CHIPS,))],
    )(*srcs)


def _pair_sum_call(name, parts, from_sibling, tr):
    _, rows, cols = parts.shape

    def body(p_ref, s_ref, o_ref):
        o_ref[...] = (p_ref[...].astype(F32) + s_ref[...].astype(F32)).astype(o_ref.dtype)

    return pl.pallas_call(
        body, name=name,
        grid=(N_CHIPS, rows // tr),
        in_specs=[pl.BlockSpec((1, tr, cols), lambda ch, i: (2 * ch + lax.axis_index("c"), i, 0)),
                  pl.BlockSpec((1, tr, cols), lambda ch, i: (ch, i, 0))],
        out_specs=pl.BlockSpec((1, tr, cols), lambda ch, i: (ch, i, 0)),
        out_shape=jax.ShapeDtypeStruct((N_CHIPS, rows, cols), WIRE_DTYPE),
        compiler_params=_params("arbitrary", "arbitrary"),
    )(parts, from_sibling)


def _chip_exchange_call(name, srcs):
    n = len(srcs)
    per = N_CHIPS - 1

    def body(*refs):
        src_refs, out_refs = refs[:n], refs[n:2 * n]
        send_sems, recv_sems, local_sems = refs[2 * n:]
        x, y, c = lax.axis_index("x"), lax.axis_index("y"), lax.axis_index("c")
        mine = 2 * x + y
        chips = [(1 - x, y), (x, 1 - y), (1 - x, 1 - y)]
        local = [pltpu.make_async_copy(src_refs[a].at[mine], out_refs[a].at[mine], local_sems.at[a]) for a in range(n)]
        for cp in local:
            cp.start()
        remote = []
        for a in range(n):
            for j, (px, py) in enumerate(chips):
                remote.append(pltpu.make_async_remote_copy(
                    src_ref=src_refs[a].at[2 * px + py], dst_ref=out_refs[a].at[mine],
                    send_sem=send_sems.at[a * per + j], recv_sem=recv_sems.at[a * per + j],
                    device_id=(px, py, c), device_id_type=_MESH))
        for cp in remote:
            cp.start()
        for cp in remote:
            cp.wait_send()
        for cp in remote:
            cp.wait_recv()
        for cp in local:
            cp.wait()

    return pl.pallas_call(
        body, name=name,
        in_specs=[_HBM] * n, out_specs=[_HBM] * n,
        out_shape=[jax.ShapeDtypeStruct(s.shape, s.dtype) for s in srcs],
        scratch_shapes=[pltpu.SemaphoreType.DMA((n * per,)), pltpu.SemaphoreType.DMA((n * per,)),
                        pltpu.SemaphoreType.DMA((n,))],
    )(*srcs)


def _adam_call(name, parts, w, m, v, tr):
    rows, cols = w.shape
    n_slots = parts.shape[0]

    def body(p_ref, w_ref, m_ref, v_ref, g_ref, d_ref, nm_ref, nv_ref):
        g = p_ref[0].astype(F32)
        for s in range(1, n_slots):
            g = g + p_ref[s].astype(F32)
        m_new = ADAM_B1 * m_ref[...] + (1.0 - ADAM_B1) * g
        v_new = ADAM_B2 * v_ref[...] + (1.0 - ADAM_B2) * (g * g)
        m_hat = m_new / (1.0 - ADAM_B1 ** ADAM_STEP)
        v_hat = v_new / (1.0 - ADAM_B2 ** ADAM_STEP)
        g_ref[...] = g
        d_ref[...] = -ADAM_LR * (m_hat / (jnp.sqrt(v_hat) + ADAM_EPS) + ADAM_WD * w_ref[...])
        nm_ref[...] = m_new
        nv_ref[...] = v_new

    blk = pl.BlockSpec((tr, cols), lambda i: (i, 0))
    return pl.pallas_call(
        body, name=name,
        grid=(rows // tr,),
        in_specs=[pl.BlockSpec((n_slots, tr, cols), lambda i: (0, i, 0)), blk, blk, blk],
        out_specs=[blk] * 4,
        out_shape=[jax.ShapeDtypeStruct((rows, cols), F32)] * 4,
        compiler_params=_params("arbitrary"),
    )(parts, w, m, v)


N_PIECES = 8
PIECE = 512
SHARD_COLS = 513
SHARD_PAD = 640
RELAYOUT_ROWS = 256


def _from_shards_call(shards):
    _, d, _ = shards.shape
    tr = RELAYOUT_ROWS

    def body(p_ref, m_ref, s_ref):
        lane = lax.broadcasted_iota(jnp.int32, (tr, SHARD_PAD), 1)
        pad = jnp.zeros((tr, SHARD_PAD - SHARD_COLS), F32)
        sh = [jnp.concatenate([p_ref[s].astype(F32), pad], axis=1) for s in range(N_DEV)]
        for p in range(N_PIECES):
            y = sh[p] if p == 0 else pltpu.roll(sh[p], p, axis=1)
            if p > 0:
                y = jnp.where(lane < p, pltpu.roll(sh[p - 1], SHARD_PAD - (SHARD_COLS - p), axis=1), y)
            m_ref[:, p * PIECE:(p + 1) * PIECE] = y[:, :PIECE].astype(m_ref.dtype)
        first_gate = N_PIECES * PIECE - (N_DEV - 1) * SHARD_COLS
        s_ref[...] = pltpu.roll(sh[N_DEV - 1], SHARD_PAD - first_gate, axis=1)[:, :LANES].astype(s_ref.dtype)

    return pl.pallas_call(
        body, name="w_in_from_shards",
        grid=(d // tr,),
        in_specs=[pl.BlockSpec((N_DEV, tr, SHARD_COLS), lambda i: (0, i, 0))],
        out_specs=[pl.BlockSpec((tr, N_PIECES * PIECE), lambda i: (i, 0)), pl.BlockSpec((tr, LANES), lambda i: (i, 0))],
        out_shape=[jax.ShapeDtypeStruct((d, N_PIECES * PIECE), shards.dtype),
                   jax.ShapeDtypeStruct((d, LANES), shards.dtype)],
        compiler_params=_params("arbitrary"),
    )(shards)


def _to_shards_call(main, gates, out_dtype):
    d = main.shape[0]
    tr = RELAYOUT_ROWS

    def body(m_ref, s_ref, o_ref):
        for s in range(N_DEV):
            if s < N_DEV - 1:
                x = m_ref[:, s * PIECE:s * PIECE + SHARD_PAD]
            else:
                x = jnp.concatenate([m_ref[:, s * PIECE:(s + 1) * PIECE], s_ref[...]], axis=1)
            y = x if s == 0 else pltpu.roll(x, SHARD_PAD - s, axis=1)
            o_ref[s] = y[:, :SHARD_COLS].astype(out_dtype)

    return pl.pallas_call(
        body, name="w_in_to_shards",
        grid=(d // tr,),
        in_specs=[pl.BlockSpec((tr, N_PIECES * PIECE), lambda i: (i, 0)), pl.BlockSpec((tr, LANES), lambda i: (i, 0))],
        out_specs=pl.BlockSpec((N_DEV, tr, SHARD_COLS), lambda i: (0, i, 0)),
        out_shape=jax.ShapeDtypeStruct((N_DEV, d, SHARD_COLS), out_dtype),
        compiler_params=_params("arbitrary"),
    )(main, gates)


def _adamw(g, w, m, v):
    m_new = ADAM_B1 * m + (1.0 - ADAM_B1) * g
    v_new = ADAM_B2 * v + (1.0 - ADAM_B2) * (g * g)
    m_hat = m_new / (1.0 - ADAM_B1 ** ADAM_STEP)
    v_hat = v_new / (1.0 - ADAM_B2 ** ADAM_STEP)
    return -ADAM_LR * (m_hat / (jnp.sqrt(v_hat) + ADAM_EPS) + ADAM_WD * w), m_new, v_new


def _adam_small_call(parts, ws, ms, vs):
    n = len(ws)
    n_slots = parts.shape[0]

    def body(*refs):
        p_ref = refs[0]
        w_refs, m_refs, v_refs = refs[1:1 + n], refs[1 + n:1 + 2 * n], refs[1 + 2 * n:1 + 3 * n]
        loss_ref = refs[1 + 3 * n]
        outs = refs[2 + 3 * n:]
        g_all = p_ref[0]
        for s in range(1, n_slots):
            g_all = g_all + p_ref[s]
        loss_ref[...] = g_all[n:n + 1, 0:1]
        for r in range(n):
            size = w_refs[r].shape[1]
            g = g_all[r:r + 1, :size]
            delta, m_new, v_new = _adamw(g, w_refs[r][...], m_refs[r][...], v_refs[r][...])
            for kind, val in enumerate((g, delta, m_new, v_new)):
                outs[kind * n + r][...] = val

    vm = pl.BlockSpec(memory_space=pltpu.VMEM)
    shapes = [jax.ShapeDtypeStruct(w.shape, F32) for w in ws]
    return pl.pallas_call(
        body, name="adam_small",
        in_specs=[vm] * (1 + 3 * n), out_specs=[vm] * (1 + 4 * n),
        out_shape=[jax.ShapeDtypeStruct((1, 1), F32)] + shapes * 4,
    )(parts, *ws, *ms, *vs)


_SMALL_ROWS = ("norm1_w", "final_norm_w", "sb_norm_w", "gdn_norm_w", "gdn_A_log", "gdn_dt_bias", "loss")


def _pack_small(vals, width):
    rows = [jnp.pad(a.reshape(1, -1).astype(F32), ((0, 0), (0, width - a.size))) for a in vals]
    rows += [jnp.zeros((1, width), F32)] * (8 - len(rows))
    return jnp.concatenate(rows, axis=0)


def _device_step(x2d, tgt, w_main, w_small, w_out_full, conv_full, norm1_w, sb_norm_w, gdn_A_log, gdn_dt_bias,
                 gdn_norm_w, final_norm_w, distributed=False):
    t_len, d = x2d.shape
    n_chunks = t_len // CHUNK
    w_main, w_small, w_out_full = (a.astype(MXU_DTYPE) for a in (w_main, w_small, w_out_full))
    w_small_t = w_small[:, :2 * GDN_HEADS].T

    pad_lanes = lambda a, lo: jnp.pad(a.reshape(1, -1), ((0, 0), (lo, LANES - lo - a.size)))
    alog_l, dtb_l = pad_lanes(gdn_A_log, GDN_HEADS), pad_lanes(gdn_dt_bias, GDN_HEADS)
    alog_c, dtb_c = alog_l[:, :8].T, dtb_l[:, :8].T
    sbw = jnp.tile(sb_norm_w, (1, 512 // SB_HEAD_DIM))
    gdw = jnp.tile(gdn_norm_w, (1, 512 // GDN_HEAD_DIM))
    fw = final_norm_w.reshape(1, d)

    if distributed:
        proj, ps, pst, h_t, r1, w_out_g, conv_g = _inproj_call(
            x2d, norm1_w, w_main, w_small, w_small_t, gather=(w_out_full, conv_full))
        w_out_full = w_out_g.reshape(d, d)
        conv_full = conv_g.transpose(1, 0, 2).reshape(CONV_WIDTH, N_DEV * conv_g.shape[2])
    else:
        proj, ps, pst, h_t, r1 = _inproj_call(x2d, norm1_w, w_main, w_small, w_small_t)
    o_sb, sp_total, sb_blocks_run = _sb_fwd_call(proj, t_len)
    gact = _gdn_prep_call(proj, conv_full, t_len)
    beta_l, gcol_l, grow = _gdn_gates_call(ps, pst, alog_l, dtb_l, alog_c, dtb_c, t_len)
    gam_r = grow[GDN_HEADS:2 * GDN_HEADS].reshape(GDN_HEADS, n_chunks, 1, CHUNK)
    o_gd, s_all, t_all = _gdn_fwd_call(gact, beta_l, gcol_l, gam_r, t_len)

    (dx2, d_osb, d_ogd, dproj8, loss_p, g_fw, g_sbw, g_gdw, g_wout) = _post_call(
        o_sb, o_gd, proj, x2d, tgt, w_out_full, sbw, gdw, fw)

    dproj8 = _sb_bwd_call(proj, sp_total, sb_blocks_run, d_osb, dproj8, t_len)
    if distributed:
        d_gact3, d_gates, g_wout = _gdn_bwd_call(gact, beta_l, gcol_l, gam_r, s_all, t_all, d_ogd, t_len,
                                                 scatter=(g_wout.reshape(N_DEV, d // N_DEV, d),))
    else:
        d_gact3, d_gates = _gdn_bwd_call(gact, beta_l, gcol_l, gam_r, s_all, t_all, d_ogd, t_len)
    dproj8, g_conv = _gdn_prep_bwd_call(proj, conv_full, d_gact3, dproj8, t_len)
    dsmall, g_alog, g_dtb = _gdn_gates_bwd_call(ps, alog_l, dtb_l, d_gates, t_len)

    g_w_main = _gw_in_call(h_t, dproj8)
    g_w_small = _gw_small_call(h_t, dsmall)
    grad_x, g_n1 = _dx_call(dproj8, dsmall, w_main, w_small, x2d, r1, dx2, norm1_w)
    return (loss_p, grad_x, g_n1, (g_w_main, g_w_small), g_sbw, g_conv, g_alog, g_dtb, g_gdw, g_wout, g_fw)


def kernel(x, norm1_w, w_in, sb_norm_w, gdn_conv_w, gdn_A_log, gdn_dt_bias, gdn_norm_w, w_out, final_norm_w, loss_target, m_norm1_w, m_w_in, m_sb_norm_w, m_gdn_conv_w, m_gdn_A_log, m_gdn_dt_bias, m_gdn_norm_w, m_w_out, m_final_norm_w, v_norm1_w, v_w_in, v_sb_norm_w, v_gdn_conv_w, v_gdn_A_log, v_gdn_dt_bias, v_gdn_norm_w, v_w_out, v_final_norm_w):
    d = x.shape[2]
    shard_cols = w_in.shape[2]
    conv_cols = gdn_conv_w.shape[2]

    (w_in_g,) = _gather_call("gather_weights", [w_in[0].astype(WIRE_DTYPE)])
    w_main, w_small = _from_shards_call(w_in_g)

    (loss_p, grad_x, g_n1, (g_w_main, g_w_small), g_sbw, g_conv, g_alog, g_dtb, g_gdw, p_wout, g_fw) = _device_step(
        x[0], loss_target[0], w_main, w_small, w_out[0].astype(WIRE_DTYPE), gdn_conv_w[0], norm1_w, sb_norm_w,
        gdn_A_log, gdn_dt_bias, gdn_norm_w, final_norm_w, distributed=True)

    g_w_in_parts = _to_shards_call(g_w_main, g_w_small, WIRE_DTYPE)
    g_conv_parts = g_conv.reshape(CONV_WIDTH, N_DEV, conv_cols).transpose(1, 0, 2)
    fold = lambda a, group: a.reshape(-1, group).sum(axis=0)
    small_g = _pack_small([g_n1, g_fw, fold(g_sbw, SB_HEAD_DIM), fold(g_gdw, GDN_HEAD_DIM),
                           g_alog[0, GDN_HEADS:2 * GDN_HEADS], g_dtb[0, GDN_HEADS:2 * GDN_HEADS],
                           loss_p[0, :1]], d)
    (sib_w_in,) = _sibling_send_call("grads_to_sibling", [g_w_in_parts])
    c_w_in = _pair_sum_call("pair_sum_w_in", g_w_in_parts, sib_w_in, 256)
    (p_w_in,) = _chip_exchange_call("grads_to_chips", [c_w_in])
    p_small, p_conv = _exchange_call("exchange_small", [small_g, g_conv_parts], [False, True])

    r_w_in = _adam_call("adam_w_in", p_w_in, w_in[0], m_w_in[0], v_w_in[0], 256)
    r_wout = _adam_call("adam_w_out", p_wout, w_out[0], m_w_out[0], v_w_out[0], d // N_DEV)
    r_conv = _adam_call("adam_conv", p_conv, gdn_conv_w[0], m_gdn_conv_w[0], v_gdn_conv_w[0], CONV_WIDTH)

    row = lambda a: a.reshape(1, -1)
    n_small = len(_SMALL_ROWS) - 1
    r_small = _adam_small_call(
        p_small,
        [norm1_w, row(final_norm_w), sb_norm_w, gdn_norm_w, gdn_A_log, gdn_dt_bias],
        [m_norm1_w, row(m_final_norm_w), m_sb_norm_w, m_gdn_norm_w, m_gdn_A_log, m_gdn_dt_bias],
        [v_norm1_w, row(v_final_norm_w), v_sb_norm_w, v_gdn_norm_w, v_gdn_A_log, v_gdn_dt_bias])

    def small_out(kind, name):
        out = r_small[1 + kind * n_small + _SMALL_ROWS.index(name)]
        return out.reshape(final_norm_w.shape) if name == "final_norm_w" else out

    def outputs(kind):
        return (small_out(kind, "norm1_w"), r_w_in[kind][None], small_out(kind, "sb_norm_w"), r_conv[kind][None],
                small_out(kind, "gdn_A_log"), small_out(kind, "gdn_dt_bias"), small_out(kind, "gdn_norm_w"),
                r_wout[kind][None], small_out(kind, "final_norm_w"))

    return (r_small[0][0, 0], grad_x[None], *outputs(0), *outputs(1), *outputs(2), *outputs(3))
```

```python
import functools

import jax
import jax.numpy as jnp
from jax import lax
from jax.experimental import pallas as pl
from jax.experimental.pallas import tpu as pltpu

F32 = jnp.float32
MXU_DTYPE = jnp.bfloat16
WIRE_DTYPE = jnp.bfloat16
EXACT = lax.Precision.HIGHEST
EPS = 1e-6
N_DEV = 8
SB_HEAD_DIM = 64
GDN_HEAD_DIM = 128
GDN_HEADS = 4
GDN_CHUNKS_PER_STEP = 4
GDN_BWD_GROUP = 1
CHUNK = 64
CONV_WIDTH = 4
LANES = 128
SB_BLOCK = 128
SB_BQ = 256
VMEM_LIMIT_BYTES = 56 * 1024 * 1024

DPROJ_PIECE_OF_SLOT = (0, 1, 2, 4, 5, 6, 3, 7)
DPROJ_SB_SLOT, DPROJ_GDN_SLOT, DPROJ_GATE_SLOT = 0, 3, 6

ADAM_LR = 0.001
ADAM_B1 = 0.9
ADAM_B2 = 0.999
ADAM_EPS = 1e-08
ADAM_WD = 0.01
ADAM_STEP = 10

_NN = (((1,), (0,)), ((), ()))
_NT = (((1,), (1,)), ((), ()))
_TN = (((0,), (0,)), ((), ()))
_BNN = (((2,), (1,)), ((0,), (0,)))
_BNT = (((2,), (2,)), ((0,), (0,)))
_BTN = (((1,), (1,)), ((0,), (0,)))


def _mm(a, b):
    return jnp.dot(a.astype(MXU_DTYPE), b.astype(MXU_DTYPE), preferred_element_type=F32)


def _mm_nt(a, b):
    return lax.dot_general(a.astype(MXU_DTYPE), b.astype(MXU_DTYPE), _NT, preferred_element_type=F32)


def _mm_tn(a, b):
    return lax.dot_general(a.astype(MXU_DTYPE), b.astype(MXU_DTYPE), _TN, preferred_element_type=F32)


def _mx(a, b):
    return jnp.dot(a, b, precision=EXACT, preferred_element_type=F32)


def _mx_nt(a, b):
    return lax.dot_general(a, b, _NT, precision=EXACT, preferred_element_type=F32)


def _mx_tn(a, b):
    return lax.dot_general(a, b, _TN, precision=EXACT, preferred_element_type=F32)


def _split(x):
    hi = x.astype(MXU_DTYPE)
    return hi, (x - hi.astype(F32)).astype(MXU_DTYPE)


def _m3_general(a, b, dims):
    ah, al = _split(a)
    bh, bl = _split(b)
    dot = lambda x, y: lax.dot_general(x, y, dims, preferred_element_type=F32)
    (contract, _), (batch, _) = dims
    free = [ax for ax in range(a.ndim) if ax not in contract and ax not in batch][0]
    m = a.shape[free]
    both = dot(jnp.concatenate([ah, al], axis=free), bh)
    out_axis = len(batch)
    hi_part = lax.slice_in_dim(both, 0, m, axis=out_axis)
    lo_part = lax.slice_in_dim(both, m, 2 * m, axis=out_axis)
    return hi_part + (dot(ah, bl) + lo_part)


def _m3(a, b):
    return _m3_general(a, b, _NN)


def _m3_nt(a, b):
    return _m3_general(a, b, _NT)


def _m3_tn(a, b):
    return _m3_general(a, b, _TN)


def _sigmoid(z):
    return 1.0 / (1.0 + jnp.exp(-z))


def _softplus(z):
    return jnp.maximum(z, 0.0) + jnp.log(1.0 + jnp.exp(-jnp.abs(z)))


def _params(*semantics):
    return pltpu.CompilerParams(dimension_semantics=semantics, vmem_limit_bytes=VMEM_LIMIT_BYTES)


def _inproj_call(x, norm_w, w_main, w_small, w_small_t, gather=(), tm=256):
    t_len, d = x.shape
    n = w_main.shape[1]
    ns = w_small.shape[1]
    nst = w_small_t.shape[0]
    ng = len(gather)
    steps = t_len // tm

    def body(*refs):
        x_ref, nw_ref, wm_ref, ws_ref, wst_ref = refs[:5]
        pm_ref, ps_ref, pst_ref, ht_ref, r_ref = refs[5 + ng:10 + ng]
        copies = lambda: _direct_copies(refs[5:5 + ng], refs[10 + ng:10 + 2 * ng], *refs[10 + 2 * ng:], (False,) * ng)
        if ng:
            pl.when(pl.program_id(0) == 0)(lambda: _start_all(copies()))
        xv = x_ref[...]
        r = lax.rsqrt(jnp.mean(xv * xv, axis=-1, keepdims=True) + EPS)
        h = xv * r * nw_ref[...]
        hb = h.astype(MXU_DTYPE)
        for n0 in range(0, n, 512):
            pm_ref[:, n0:n0 + 512] = jnp.dot(hb, wm_ref[:, n0:n0 + 512], preferred_element_type=F32)
        ps_ref[...] = jnp.dot(hb, ws_ref[...], preferred_element_type=F32)
        pst_ref[...] = lax.dot_general(wst_ref[...], hb, _NT, preferred_element_type=F32)
        ht_ref[...] = h.T.astype(MXU_DTYPE)
        r_ref[...] = r
        if ng:
            pl.when(pl.program_id(0) == steps - 1)(lambda: _wait_all(copies()))

    return pl.pallas_call(
        body, name="inproj",
        grid=(steps,),
        in_specs=[pl.BlockSpec((tm, d), lambda i: (i, 0)),
                  pl.BlockSpec((1, d), lambda i: (0, 0)),
                  pl.BlockSpec((d, n), lambda i: (0, 0)),
                  pl.BlockSpec((d, ns), lambda i: (0, 0)),
                  pl.BlockSpec((nst, d), lambda i: (0, 0))] + [_HBM] * ng,
        out_specs=[pl.BlockSpec((tm, n), lambda i: (i, 0)),
                   pl.BlockSpec((tm, ns), lambda i: (i, 0)),
                   pl.BlockSpec((nst, tm), lambda i: (0, i)),
                   pl.BlockSpec((d, tm), lambda i: (0, i)),
                   pl.BlockSpec((tm, 1), lambda i: (i, 0))] + [_HBM] * ng,
        out_shape=[jax.ShapeDtypeStruct((t_len, n), F32),
                   jax.ShapeDtypeStruct((t_len, ns), F32),
                   jax.ShapeDtypeStruct((nst, t_len), F32),
                   jax.ShapeDtypeStruct((d, t_len), MXU_DTYPE),
                   jax.ShapeDtypeStruct((t_len, 1), F32)] + _direct_out_shapes(gather, (False,) * ng),
        scratch_shapes=_direct_semaphores(ng) if ng else [],
        compiler_params=_params("arbitrary"),
    )(x, norm_w, w_main, w_small, w_small_t, *gather)


def _running_sum_mm(x, tri):
    hi = x.astype(MXU_DTYPE)
    lo = (x - hi.astype(F32)).astype(MXU_DTYPE)
    return jnp.dot(hi, tri, preferred_element_type=F32) + jnp.dot(lo, tri, preferred_element_type=F32)


def _sb_iotas():
    row_i = lax.broadcasted_iota(jnp.int32, (SB_BQ, SB_BLOCK), 0)
    col_i = lax.broadcasted_iota(jnp.int32, (SB_BQ, SB_BLOCK), 1)
    sq_r = lax.broadcasted_iota(jnp.int32, (SB_BLOCK, SB_BLOCK), 0)
    sq_c = lax.broadcasted_iota(jnp.int32, (SB_BLOCK, SB_BLOCK), 1)
    return row_i, col_i, sq_r, sq_c


SB_DIAG_BLOCKS = SB_BQ // SB_BLOCK
SB_EXP_FLOOR = -110.0


def _sb_keys_descending(qi, tile, carry, z_bounds, n_heads):
    n_free = SB_DIAG_BLOCKS * qi
    diag = list(range(SB_DIAG_BLOCKS - 1, -1, -1))
    carry = tile([n_free + j for j in diag], True, carry, [j * SB_BLOCK for j in diag])

    def largest_exponent(c):
        worst = jnp.max(z_bounds[0] - c[1])
        for h in range(1, n_heads):
            worst = jnp.maximum(worst, jnp.max(z_bounds[h] - c[1 + h]))
        return worst

    def cond(state):
        return (state[0] < n_free) & (state[1] > SB_EXP_FLOOR)

    def body(state):
        first = n_free - 1 - state[0]
        c = tile([first - j for j in range(SB_DIAG_BLOCKS)], False, state[2:])
        return (state[0] + SB_DIAG_BLOCKS, largest_exponent(c), *c)

    out = lax.while_loop(cond, body, (jnp.int32(0), largest_exponent(carry), *carry))
    return out[2:], out[0]


def _sb_keys_ascending(qi, n_run, tile, carry):
    n_free = SB_DIAG_BLOCKS * qi
    group = lambda s: [n_free - n_run + SB_DIAG_BLOCKS * s + j for j in range(SB_DIAG_BLOCKS)]
    carry = lax.fori_loop(0, n_run // SB_DIAG_BLOCKS, lambda s, c: tile(group(s), False, c), carry)
    diag = list(range(SB_DIAG_BLOCKS))
    return tile([n_free + j for j in diag], True, carry, [j * SB_BLOCK for j in diag])


def _sb_fwd_call(proj, t_len):
    nq = t_len // SB_BQ
    scale = float(SB_HEAD_DIM) ** -0.5
    n_pairs = 512 // LANES
    per_pair = LANES // SB_HEAD_DIM

    def body(q_ref, k_ref, v_ref, o_ref, st_ref, nrun_ref):
        lane = lax.broadcasted_iota(jnp.int32, (1, LANES), 1)
        row_i, col_i, sq_r, sq_c = _sb_iotas()
        ge = (sq_r >= sq_c).astype(MXU_DTYPE)
        hms = [((lane // SB_HEAD_DIM) == hh).astype(F32) for hh in range(per_pair)]
        k_sq = k_ref[...] * k_ref[...]
        k_norms = [jnp.sqrt(jnp.max(jnp.sum(k_sq * hm, axis=-1, keepdims=True))) * (1.02 * scale) for hm in hms]

        def q_loop(qi, carry):
            r0 = pl.multiple_of(qi * SB_BQ, SB_BQ)
            rows = pl.ds(r0, SB_BQ)
            q_all = q_ref[rows, :]
            qms = [(q_all * (hm * scale)).astype(MXU_DTYPE) for hm in hms]
            z_bounds = [jnp.sqrt(jnp.sum(q_all * q_all * hm, axis=-1, keepdims=True)) * kn
                        for hm, kn in zip(hms, k_norms)]

            def tile(kjs, masked, kc, los=None):
                heads = range(per_pair)
                los = los or [0] * len(kjs)
                pairs = [(t, h) for t in range(len(kjs)) for h in heads]
                add_rows = lambda full, lo, part: full + part if lo == 0 else jnp.concatenate(
                    [full[:lo], full[lo:] + part], axis=0)
                acc, cs = kc[0], list(kc[1:])
                s0s = [pl.multiple_of(kj * SB_BLOCK, SB_BLOCK) for kj in kjs]
                kbs = [k_ref[pl.ds(s0, SB_BLOCK), :].astype(MXU_DTYPE) for s0 in s0s]
                v_alls = [v_ref[pl.ds(s0, SB_BLOCK), :] for s0 in s0s]
                vms = {(t, h): (v_alls[t] * hms[h]).astype(MXU_DTYPE) for t, h in pairs}
                zs = {(t, h): lax.dot_general(qms[h][los[t]:], kbs[t], _NT, preferred_element_type=F32)
                      for t, h in pairs}
                sps = {p: _softplus(zs[p]) for p in pairs}
                if masked:
                    masks = [(col_i[lo:] + s0) < (row_i[lo:] + r0) for lo, s0 in zip(los, s0s)]
                    sps = {(t, h): jnp.where(masks[t], sps[t, h], 0.0) for t, h in pairs}
                sums = {p: _running_sum_mm(sps[p], ge) for p in pairs}
                mass = {}
                for t, h in pairs:
                    mass[t, h] = cs[h] if t == 0 else add_rows(
                        mass[t - 1, h], los[t - 1], jnp.sum(sps[t - 1, h], axis=-1, keepdims=True))
                ws = {(t, h): jnp.exp(zs[t, h] - (sums[t, h] + mass[t, h][los[t]:])) for t, h in pairs}
                if masked:
                    ws = {(t, h): jnp.where(masks[t], ws[t, h], 0.0) for t, h in pairs}
                for t, h in pairs:
                    acc = add_rows(acc, los[t], jnp.dot(ws[t, h].astype(MXU_DTYPE), vms[t, h],
                                                        preferred_element_type=F32))
                last = len(kjs) - 1
                cs = [add_rows(mass[last, h], los[last], jnp.sum(sps[last, h], axis=-1, keepdims=True)) for h in heads]
                return (acc, *cs)

            zero_col = jnp.zeros((SB_BQ, 1), F32)
            out, n_run = _sb_keys_descending(
                qi, tile, (jnp.zeros((SB_BQ, LANES), F32),) + (zero_col,) * per_pair, z_bounds, per_pair)
            o_ref[rows, :] = out[0]
            for hh in range(per_pair):
                st_ref[hh, rows, :] = out[1 + hh]
            nrun_ref[pl.program_id(0), qi] = n_run
            return carry

        lax.fori_loop(0, nq, q_loop, 0)

    return pl.pallas_call(
        body, name="sb_fwd",
        grid=(n_pairs,),
        in_specs=[pl.BlockSpec((t_len, LANES), lambda p: (0, p)),
                  pl.BlockSpec((t_len, LANES), lambda p: (0, n_pairs + p)),
                  pl.BlockSpec((t_len, LANES), lambda p: (0, 2 * n_pairs + p))],
        out_specs=[pl.BlockSpec((t_len, LANES), lambda p: (0, p)),
                   pl.BlockSpec((per_pair, t_len, 1), lambda p: (p, 0, 0)),
                   pl.BlockSpec(memory_space=pltpu.SMEM)],
        out_shape=[jax.ShapeDtypeStruct((t_len, 512), F32),
                   jax.ShapeDtypeStruct((n_pairs * per_pair, t_len, 1), F32),
                   jax.ShapeDtypeStruct((n_pairs, nq), jnp.int32)],
        compiler_params=_params("arbitrary"),
    )(proj, proj, proj)


def _sb_bwd_call(proj, sp_total, n_run_all, d_o, dproj, t_len):
    nq = t_len // SB_BQ
    scale = float(SB_HEAD_DIM) ** -0.5
    n_pairs = 512 // LANES
    per_pair = LANES // SB_HEAD_DIM

    def body(q_ref, k_ref, v_ref, st_ref, nrun_ref, do_ref, dproj_in_ref, d_ref):
        lane = lax.broadcasted_iota(jnp.int32, (1, LANES), 1)
        row_i, col_i, sq_r, sq_c = _sb_iotas()
        lt = (sq_r < sq_c).astype(MXU_DTYPE)
        le = (sq_r <= sq_c).astype(MXU_DTYPE)
        hms = [((lane // SB_HEAD_DIM) == hh).astype(F32) for hh in range(per_pair)]
        d_ref[1] = jnp.zeros((t_len, LANES), F32)
        d_ref[2] = jnp.zeros((t_len, LANES), F32)

        def q_loop(qi, carry):
            r0 = pl.multiple_of(qi * SB_BQ, SB_BQ)
            rows = pl.ds(r0, SB_BQ)
            q_all, do_all = q_ref[rows, :], do_ref[rows, :]
            qms = [(q_all * (hm * scale)).astype(MXU_DTYPE) for hm in hms]
            doms = [(do_all * hm).astype(MXU_DTYPE) for hm in hms]
            totals = [st_ref[hh, rows, :] for hh in range(per_pair)]

            def tile(kjs, masked, kc, los=None):
                heads = range(per_pair)
                los = los or [0] * len(kjs)
                pairs = [(t, h) for t in range(len(kjs)) for h in heads]
                add_rows = lambda full, lo, part: full + part if lo == 0 else jnp.concatenate(
                    [full[:lo], full[lo:] + part], axis=0)
                rsum = lambda a: jnp.sum(a, axis=-1, keepdims=True)
                dq, cls, gls = kc[0], list(kc[1:1 + per_pair]), list(kc[1 + per_pair:])
                s0s = [pl.multiple_of(kj * SB_BLOCK, SB_BLOCK) for kj in kjs]
                k_alls = [k_ref[pl.ds(s0, SB_BLOCK), :] for s0 in s0s]
                v_alls = [v_ref[pl.ds(s0, SB_BLOCK), :] for s0 in s0s]
                kbs = [k_all.astype(MXU_DTYPE) for k_all in k_alls]
                vms = {(t, h): (v_alls[t] * hms[h]).astype(MXU_DTYPE) for t, h in pairs}
                kms = {(t, h): (k_alls[t] * (hms[h] * scale)).astype(MXU_DTYPE) for t, h in pairs}
                q_live = {(t, h): qms[h][los[t]:] for t, h in pairs}
                do_live = {(t, h): doms[h][los[t]:] for t, h in pairs}
                zs = {p: lax.dot_general(q_live[p], kbs[p[0]], _NT, preferred_element_type=F32) for p in pairs}
                das = {p: lax.dot_general(do_live[p], vms[p], _NT, preferred_element_type=F32) for p in pairs}
                sp_alls = {p: _softplus(zs[p]) for p in pairs}
                sps = sp_alls
                if masked:
                    masks = [(col_i[lo:] + s0) < (row_i[lo:] + r0) for lo, s0 in zip(los, s0s)]
                    sps = {(t, h): jnp.where(masks[t], sp_alls[t, h], 0.0) for t, h in pairs}
                lefts = {p: _running_sum_mm(sps[p], lt) for p in pairs}
                cl = {}
                for t, h in pairs:
                    cl[t, h] = cls[h] if t == 0 else add_rows(cl[t - 1, h], los[t - 1], rsum(sps[t - 1, h]))
                ws = {(t, h): jnp.exp(zs[t, h] - ((totals[h] - cl[t, h])[los[t]:] - lefts[t, h])) for t, h in pairs}
                if masked:
                    ws = {(t, h): jnp.where(masks[t], ws[t, h], 0.0) for t, h in pairs}
                gs = {p: das[p] * ws[p] for p in pairs}
                g_sums = {p: _running_sum_mm(gs[p], le) for p in pairs}
                gl = {}
                for t, h in pairs:
                    gl[t, h] = gls[h] if t == 0 else add_rows(gl[t - 1, h], los[t - 1], rsum(gs[t - 1, h]))
                dzs = {(t, h): gs[t, h] - jnp.exp(zs[t, h] - sp_alls[t, h]) * (gl[t, h][los[t]:] + g_sums[t, h])
                       for t, h in pairs}
                if masked:
                    dzs = {(t, h): jnp.where(masks[t], dzs[t, h], 0.0) for t, h in pairs}
                dzs = {p: dzs[p].astype(MXU_DTYPE) for p in pairs}
                for t in range(len(kjs)):
                    dk_t = jnp.zeros((SB_BLOCK, LANES), F32)
                    dv_t = jnp.zeros((SB_BLOCK, LANES), F32)
                    for h in heads:
                        dq = add_rows(dq, los[t], jnp.dot(dzs[t, h], kms[t, h], preferred_element_type=F32))
                        dk_t = dk_t + lax.dot_general(dzs[t, h], q_live[t, h], _TN, preferred_element_type=F32)
                        dv_t = dv_t + lax.dot_general(ws[t, h].astype(MXU_DTYPE), do_live[t, h], _TN,
                                                      preferred_element_type=F32)
                    d_ref[1, pl.ds(s0s[t], SB_BLOCK), :] += dk_t
                    d_ref[2, pl.ds(s0s[t], SB_BLOCK), :] += dv_t
                last = len(kjs) - 1
                cls = [add_rows(cl[last, h], los[last], rsum(sps[last, h])) for h in heads]
                gls = [add_rows(gl[last, h], los[last], rsum(gs[last, h])) for h in heads]
                return (dq, *cls, *gls)

            zero_col = jnp.zeros((SB_BQ, 1), F32)
            out = _sb_keys_ascending(qi, nrun_ref[pl.program_id(0), qi], tile,
                                     (jnp.zeros((SB_BQ, LANES), F32),) + (zero_col,) * (2 * per_pair))
            d_ref[0, rows, :] = out[0]
            return carry

        lax.fori_loop(0, nq, q_loop, 0)

    col = lambda off: pl.BlockSpec((t_len, LANES), lambda p: (0, off + p))
    return pl.pallas_call(
        body, name="sb_bwd",
        grid=(n_pairs,),
        in_specs=[col(0), col(n_pairs), col(2 * n_pairs),
                  pl.BlockSpec((per_pair, t_len, 1), lambda p: (p, 0, 0)),
                  pl.BlockSpec(memory_space=pltpu.SMEM), col(0), _HBM],
        out_specs=pl.BlockSpec((3, t_len, LANES), lambda p: (DPROJ_SB_SLOT // 3, 0, p)),
        out_shape=jax.ShapeDtypeStruct(dproj.shape, dproj.dtype),
        input_output_aliases={6: 0},
        compiler_params=_params("arbitrary"),
    )(proj, proj, proj, sp_total, n_run_all, d_o, dproj)


def _conv_taps(xin, rows, t_len):
    taps = []
    for i in range(CONV_WIDTH):
        shift = CONV_WIDTH - 1 - i
        if shift == 0:
            taps.append(xin)
        else:
            taps.append(jnp.where(rows >= shift, pltpu.roll(xin, shift, axis=0), 0.0))
    return taps


def _gdn_prep_body_common(x_ref, w_ref, t_len):
    j = pl.program_id(0)
    xin = x_ref[...]
    rows = lax.broadcasted_iota(jnp.int32, (t_len, LANES), 0)
    taps = _conv_taps(xin, rows, t_len)
    pre = taps[0] * w_ref[0:1, :]
    for i in range(1, CONV_WIDTH):
        pre = pre + taps[i] * w_ref[i:i + 1, :]
    sg = _sigmoid(pre)
    act = pre * sg
    is_qk = j < 2 * GDN_HEADS
    nrm = jnp.where(is_qk, lax.rsqrt(jnp.sum(act * act, axis=-1, keepdims=True) + EPS), 1.0)
    sc = jnp.where(j < GDN_HEADS, float(GDN_HEAD_DIM) ** -0.5, 1.0)
    return j, rows, taps, pre, sg, act, is_qk, nrm, sc


def _gdn_prep_call(proj, conv_w, t_len):
    first = 2048 // LANES

    def body(x_ref, w_ref, out_ref):
        _, _, _, _, _, act, _, nrm, sc = _gdn_prep_body_common(x_ref, w_ref, t_len)
        out_ref[...] = act * nrm * sc

    return pl.pallas_call(
        body, name="gdn_prep",
        grid=(3 * GDN_HEADS,),
        in_specs=[pl.BlockSpec((t_len, LANES), lambda j: (0, first + j)),
                  pl.BlockSpec((CONV_WIDTH, LANES), lambda j: (0, j))],
        out_specs=pl.BlockSpec((t_len, LANES), lambda j: (0, j)),
        out_shape=jax.ShapeDtypeStruct((t_len, 3 * 512), F32),
        compiler_params=_params("arbitrary"),
    )(proj, conv_w)


def _gdn_prep_bwd_call(proj, conv_w, d_act3, dproj, t_len):
    first = 2048 // LANES

    def body(x_ref, w_ref, d_ref, dproj_in_ref, dx_ref, dw_ref):
        _, rows, taps, pre, sg, act, is_qk, nrm, sc = _gdn_prep_body_common(x_ref, w_ref, t_len)
        d_out = d_ref[0]
        dn = d_out * sc
        d_norm = nrm * dn - act * (nrm * nrm * nrm) * jnp.sum(dn * act, axis=-1, keepdims=True)
        d_act = jnp.where(is_qk, d_norm, d_out)
        d_pre = d_act * sg * (1.0 + pre * (1.0 - sg))
        dx = d_pre * w_ref[CONV_WIDTH - 1:CONV_WIDTH, :]
        dw_ref[CONV_WIDTH - 1:CONV_WIDTH, :] = jnp.sum(d_pre * taps[CONV_WIDTH - 1], axis=0, keepdims=True)
        for i in range(CONV_WIDTH - 1):
            shift = CONV_WIDTH - 1 - i
            up = jnp.where(rows < t_len - shift, pltpu.roll(d_pre, t_len - shift, axis=0), 0.0)
            dx = dx + up * w_ref[i:i + 1, :]
            dw_ref[i:i + 1, :] = jnp.sum(d_pre * taps[i], axis=0, keepdims=True)
        dx_ref[0] = dx

    return pl.pallas_call(
        body, name="gdn_prep_bwd",
        grid=(3 * GDN_HEADS,),
        in_specs=[pl.BlockSpec((t_len, LANES), lambda j: (0, first + j)),
                  pl.BlockSpec((CONV_WIDTH, LANES), lambda j: (0, j)),
                  pl.BlockSpec((1, t_len, LANES), lambda j: (j // GDN_HEADS, 0, j % GDN_HEADS)), _HBM],
        out_specs=[pl.BlockSpec((1, t_len, LANES), lambda j: (DPROJ_GDN_SLOT + j // GDN_HEADS, 0, j % GDN_HEADS)),
                   pl.BlockSpec((CONV_WIDTH, LANES), lambda j: (0, j))],
        out_shape=[jax.ShapeDtypeStruct(dproj.shape, dproj.dtype),
                   jax.ShapeDtypeStruct((CONV_WIDTH, 3 * 512), F32)],
        input_output_aliases={3: 0},
        compiler_params=_params("arbitrary"),
    )(proj, conv_w, d_act3, dproj)


def _chunk_cumsum_matrix():
    r = lax.broadcasted_iota(jnp.int32, (LANES, LANES), 0)
    c = lax.broadcasted_iota(jnp.int32, (LANES, LANES), 1)
    return ((r <= c) & ((r // CHUNK) == (c // CHUNK))).astype(F32)


def _gdn_gates_call(ps, pst, alog_l, dtb_l, alog_c, dtb_c, t_len):
    def body(ps_ref, pst_ref, al_ref, dl_ref, ac_ref, dc_ref, beta_ref, gcol_ref, grow_ref):
        upper = _chunk_cumsum_matrix()
        lower = upper.T
        psv = ps_ref[...]
        beta_ref[...] = _sigmoid(psv)
        g_l = -jnp.exp(al_ref[...]) * _softplus(psv + dl_ref[...])
        g_r = -jnp.exp(ac_ref[...]) * _softplus(pst_ref[...] + dc_ref[...])
        for w in range(t_len // LANES):
            sl = slice(w * LANES, (w + 1) * LANES)
            gcol_ref[sl, :] = _mx(lower, g_l[sl, :])
            grow_ref[:, sl] = _mx(g_r[:, sl], upper)

    vm = pl.BlockSpec(memory_space=pltpu.VMEM)
    return pl.pallas_call(
        body, name="gdn_gates",
        in_specs=[vm] * 6, out_specs=[vm] * 3,
        out_shape=[jax.ShapeDtypeStruct((t_len, LANES), F32),
                   jax.ShapeDtypeStruct((t_len, LANES), F32),
                   jax.ShapeDtypeStruct((8, t_len), F32)],
        compiler_params=pltpu.CompilerParams(vmem_limit_bytes=VMEM_LIMIT_BYTES),
    )(ps, pst, alog_l, dtb_l, alog_c, dtb_c)


def _gdn_gates_bwd_call(ps, alog_l, dtb_l, d_l, t_len):
    def body(ps_ref, al_ref, dl_ref, d_ref, dps_ref, gal_ref, gdt_ref):
        lane = lax.broadcasted_iota(jnp.int32, (1, LANES), 1)
        psv = ps_ref[...]
        dv = d_ref[...]
        beta = _sigmoid(psv)
        ea = jnp.exp(al_ref[...])
        arg = psv + dl_ref[...]
        g = -ea * _softplus(arg)
        d_a = dv * (-ea) * _sigmoid(arg)
        is_a = (lane >= GDN_HEADS) & (lane < 2 * GDN_HEADS)
        dps_ref[...] = jnp.where(lane < GDN_HEADS, dv * beta * (1.0 - beta), jnp.where(is_a, d_a, 0.0))
        gdt_ref[...] = jnp.where(is_a, jnp.sum(d_a, axis=0, keepdims=True), 0.0)
        gal_ref[...] = jnp.where(is_a, jnp.sum(dv * g, axis=0, keepdims=True), 0.0)

    vm = pl.BlockSpec(memory_space=pltpu.VMEM)
    return pl.pallas_call(
        body, name="gdn_gates_bwd",
        in_specs=[vm] * 4, out_specs=[vm] * 3,
        out_shape=[jax.ShapeDtypeStruct((t_len, LANES), F32),
                   jax.ShapeDtypeStruct((1, LANES), F32),
                   jax.ShapeDtypeStruct((1, LANES), F32)],
        compiler_params=pltpu.CompilerParams(vmem_limit_bytes=VMEM_LIMIT_BYTES),
    )(ps, alog_l, dtb_l, d_l)


def _bm(a, b):
    return _m3_general(a, b, _BNN)


def _bm_nt(a, b):
    return _m3_general(a, b, _BNT)


def _bm_tn(a, b):
    return _m3_general(a, b, _BTN)


def _heads_of(ref, rows):
    return jnp.stack([ref[rows, h * GDN_HEAD_DIM:(h + 1) * GDN_HEAD_DIM] for h in range(GDN_HEADS)])


def _chunk_terms(q_ref, k_ref, v_ref, b_ref, gc_ref, gr_ref, c, incl, strict, n=1):
    r0 = c * CHUNK if isinstance(c, int) else pl.multiple_of(c * CHUNK, CHUNK)
    rows = pl.ds(r0, n * CHUNK)
    per_chunk = lambda x: x.reshape(GDN_HEADS * n, CHUNK, x.shape[-1])
    q, k, v = (per_chunk(_heads_of(ref, rows)) for ref in (q_ref, k_ref, v_ref))
    lane_ids = lax.broadcasted_iota(jnp.int32, (1, LANES), 1)
    pick = lambda slab, first: jnp.stack([jnp.sum(jnp.where(lane_ids == first + h, slab, 0.0), axis=-1, keepdims=True)
                                          for h in range(GDN_HEADS)])
    b = per_chunk(pick(b_ref[rows, :], 0))
    gc = per_chunk(pick(gc_ref[rows, :], GDN_HEADS))
    gr = gr_ref[:, c] if n == 1 else gr_ref[:, c:c + n].reshape(GDN_HEADS * n, 1, CHUNK)
    dm = jnp.where(incl, jnp.exp(jnp.where(incl, gc - gr, 0.0)), 0.0)
    kb = k * b
    vb = v * b
    e = jnp.exp(gc)
    kk_qk = _bm_nt(jnp.concatenate([kb, q], axis=1), k)
    a = jnp.where(strict, kk_qk[:, :CHUNK] * dm, 0.0)
    p = jnp.where(incl, kk_qk[:, CHUNK:] * dm, 0.0)
    gl = gc[:, CHUNK - 1:CHUNK, :]
    eg = jnp.exp(gl - gc)
    return rows, q, k, v, b, gc, dm, kb, vb, e, a, p, gl, eg


def _unit_lower_inverse(a, eye):
    x = -a
    tm = eye + x
    xp = _bm(x, x)
    for _ in range(4):
        both = _bm(jnp.concatenate([xp, tm], axis=1), xp)
        tm = tm + both[:, CHUNK:]
        xp = both[:, :CHUNK]
    return tm + _bm(tm, xp)


def _gdn_specs(t_len, n_chunks, reverse):
    cps = GDN_CHUNKS_PER_STEP
    steps = n_chunks // cps
    at = (lambda g: steps - 1 - g) if reverse else (lambda g: g)
    rows_blk = lambda width, part=0: pl.BlockSpec((cps * CHUNK, width), lambda g: (at(g), part))
    gate_r = pl.BlockSpec((GDN_HEADS, cps, 1, CHUNK), lambda g: (0, at(g), 0, 0))
    per_chunk = lambda r, c: pl.BlockSpec((GDN_HEADS, cps, r, c), lambda g: (0, at(g), 0, 0))
    return cps, steps, rows_blk, gate_r, per_chunk


def _gdn_fwd_call(gact, beta_c, gam_c, gam_r, t_len):
    n_chunks = t_len // CHUNK
    dk = GDN_HEAD_DIM
    width = GDN_HEADS * dk
    cps, steps, rows_blk, gate_r, per_chunk = _gdn_specs(t_len, n_chunks, False)

    def body(q_ref, k_ref, v_ref, b_ref, gc_ref, gr_ref, o_ref, s_ref, t_ref, state_ref):
        row = lax.broadcasted_iota(jnp.int32, (CHUNK, CHUNK), 0)
        col = lax.broadcasted_iota(jnp.int32, (CHUNK, CHUNK), 1)
        incl, strict = row >= col, row > col
        eye = (row == col).astype(F32)

        @pl.when(pl.program_id(0) == 0)
        def _():
            state_ref[...] = jnp.zeros_like(state_ref)

        _, q, k, v, b, gc, dm, kb, vb, e, a, p, gl, eg = _chunk_terms(
            q_ref, k_ref, v_ref, b_ref, gc_ref, gr_ref, 0, incl, strict, cps)
        tm = _unit_lower_inverse(a, eye)
        uw = _bm(tm, jnp.concatenate([vb, kb * e], axis=2))
        w_qe = jnp.concatenate([uw[:, :, dk:], q * e], axis=1)
        u, kd, decay = uw[:, :, :dk], k * eg, jnp.exp(gl)
        t_ref[...] = tm.reshape(GDN_HEADS, cps, CHUNK, CHUNK)

        of_chunk = lambda x, c: jnp.stack([x[h * cps + c] for h in range(GDN_HEADS)])
        s = state_ref[...]
        for c in range(cps):
            ws_qs = _bm(of_chunk(w_qe, c), s)
            vn = of_chunk(u, c) - ws_qs[:, :CHUNK]
            o = ws_qs[:, CHUNK:] + _bm(of_chunk(p, c), vn)
            for h in range(GDN_HEADS):
                o_ref[c * CHUNK:(c + 1) * CHUNK, h * dk:(h + 1) * dk] = o[h]
            s_ref[:, c] = s
            s = s * of_chunk(decay, c) + _bm_tn(of_chunk(kd, c), vn)
        state_ref[...] = s

    return pl.pallas_call(
        body, name="gdn_fwd",
        grid=(steps,),
        in_specs=[rows_blk(width, 0), rows_blk(width, 1), rows_blk(width, 2), rows_blk(LANES), rows_blk(LANES), gate_r],
        out_specs=[rows_blk(width), per_chunk(dk, dk), per_chunk(CHUNK, CHUNK)],
        out_shape=[jax.ShapeDtypeStruct((t_len, width), F32),
                   jax.ShapeDtypeStruct((GDN_HEADS, n_chunks, dk, dk), F32),
                   jax.ShapeDtypeStruct((GDN_HEADS, n_chunks, CHUNK, CHUNK), F32)],
        scratch_shapes=[pltpu.VMEM((GDN_HEADS, dk, dk), F32)],
        compiler_params=_params("arbitrary"),
    )(gact, gact, gact, beta_c, gam_c, gam_r)


def _gdn_bwd_call(gact, beta_c, gam_c, gam_r, s_all, t_all, d_o, t_len, scatter=()):
    n_chunks = t_len // CHUNK
    dk = GDN_HEAD_DIM
    width = GDN_HEADS * dk
    cps, steps, rows_blk, gate_r, per_chunk = _gdn_specs(t_len, n_chunks, True)
    nx = len(scatter)

    def body(*refs):
        q_ref, k_ref, v_ref, b_ref, gc_ref, gr_ref, s_ref, t_ref, do_ref = refs[:9]
        d_ref, dgate_ref = refs[9 + nx:11 + nx]
        dstate_ref = refs[11 + 2 * nx]
        copies = lambda: _direct_copies(refs[9:9 + nx], refs[11 + nx:11 + 2 * nx], *refs[12 + 2 * nx:], (True,) * nx)
        if nx:
            pl.when(pl.program_id(0) == 0)(lambda: _start_all(copies()))
        row = lax.broadcasted_iota(jnp.int32, (CHUNK, CHUNK), 0)
        col = lax.broadcasted_iota(jnp.int32, (CHUNK, CHUNK), 1)
        incl, strict = row >= col, row > col
        ng = GDN_BWD_GROUP
        nb = GDN_HEADS * ng
        upper = jnp.broadcast_to((row <= col).astype(F32), (nb, CHUNK, CHUNK))
        ones = jnp.ones((nb, CHUNK, LANES), F32)
        last_row = lax.broadcasted_iota(jnp.int32, (CHUNK, 1), 0) == CHUNK - 1
        lane_ids = lax.broadcasted_iota(jnp.int32, (1, LANES), 1)
        rsum = lambda m: jnp.sum(m, axis=-1, keepdims=True)
        total = lambda m: jnp.sum(rsum(m), axis=1, keepdims=True)
        of_chunk = lambda x, c: jnp.stack([x[h * ng + c] for h in range(GDN_HEADS)])

        @pl.when(pl.program_id(0) == 0)
        def _():
            dstate_ref[...] = jnp.zeros_like(dstate_ref)

        for c0 in range(cps - ng, -1, -ng):
            group(c0, q_ref, k_ref, v_ref, b_ref, gc_ref, gr_ref, s_ref, t_ref, do_ref, d_ref, dgate_ref, dstate_ref,
                  incl, strict, upper, ones, last_row, lane_ids, rsum, total, of_chunk)
        if nx:
            pl.when(pl.program_id(0) == steps - 1)(lambda: _wait_all(copies()))

    def group(c0, q_ref, k_ref, v_ref, b_ref, gc_ref, gr_ref, s_ref, t_ref, do_ref, d_ref, dgate_ref, dstate_ref,
              incl, strict, upper, ones, last_row, lane_ids, rsum, total, of_chunk):
        ng = GDN_BWD_GROUP
        nb = GDN_HEADS * ng
        rows = pl.ds(c0 * CHUNK, ng * CHUNK)
        _, q, k, v, b, gc, dm, kb, vb, e, a, p, gl, eg = _chunk_terms(
            q_ref, k_ref, v_ref, b_ref, gc_ref, gr_ref, c0, incl, strict, ng)
        s = s_ref[:, c0:c0 + ng].reshape(nb, dk, dk)
        tm = t_ref[:, c0:c0 + ng].reshape(nb, CHUNK, CHUNK)
        d_out = _heads_of(do_ref, rows).reshape(nb, CHUNK, dk)
        el = jnp.exp(gl)
        kbe = kb * e
        qe = q * e
        kd = k * eg
        uw = _bm(tm, jnp.concatenate([vb, kbe], axis=2))
        u, w = uw[:, :, :dk], uw[:, :, dk:]
        vn = u - _bm(w, s)
        pt_do = _bm_tn(p, d_out)
        qet_do = _bm_tn(qe, d_out)

        ds = dstate_ref[...]
        d_vn_c, ds_c = [None] * ng, [None] * ng
        for c in range(ng - 1, -1, -1):
            ds_c[c] = ds
            d_vn_c[c] = of_chunk(pt_do, c) + _bm(of_chunk(kd, c), ds)
            ds = of_chunk(el, c) * ds + of_chunk(qet_do, c) - _bm_tn(of_chunk(w, c), d_vn_c[c])
        dstate_ref[...] = ds
        by_chunk = lambda xs: jnp.stack([xs[c][h] for h in range(GDN_HEADS) for c in range(ng)])
        d_vn, ds = by_chunk(d_vn_c), by_chunk(ds_c)

        on_s = _bm_nt(jnp.concatenate([d_out, d_vn], axis=1), s)
        d_qe, d_w = on_s[:, :CHUNK], -on_s[:, CHUNK:]
        d_p = jnp.where(incl, _bm_nt(d_out, vn), 0.0)
        d_kd = _bm_nt(vn, ds)
        d_both = _bm_tn(tm, jnp.concatenate([d_vn, d_w], axis=2))
        d_vb, d_kbe = d_both[:, :, :dk], d_both[:, :, dk:]
        d_a = -jnp.where(strict, _bm_nt(d_both, uw), 0.0)
        m = d_a * dm
        n = d_p * dm
        on_k = _bm(jnp.concatenate([m, n], axis=1), k)
        d_kb = on_k[:, :CHUNK] + d_kbe * e
        d_q = on_k[:, CHUNK:] + d_qe * e
        d_k = (_bm_tn(jnp.concatenate([m, n], axis=1), jnp.concatenate([kb, q], axis=1))
               + d_kd * eg + b * d_kb)
        d_v = b * d_vb
        r = d_a * a + d_p * p
        kd_term = rsum(d_kd * kd)
        d_gl = total(ds * s) * el + jnp.sum(kd_term, axis=1, keepdims=True)
        d_gam = (rsum(r) - _bm_tn(r, ones)[:, :, 0:1] + rsum(d_qe * qe) + rsum(d_kbe * kbe) - kd_term
                 + jnp.where(last_row, d_gl, 0.0))
        d_beta = rsum(d_kb * k) + rsum(d_vb * v)
        d_g = _bm(upper, d_gam * ones)[:, :, 0:1]
        per_head = lambda x: x.reshape(GDN_HEADS, ng * CHUNK, x.shape[-1])
        d_q, d_k, d_v, d_beta, d_g = (per_head(x) for x in (d_q, d_k, d_v, d_beta, d_g))
        gates = jnp.zeros((ng * CHUNK, LANES), F32)
        for h in range(GDN_HEADS):
            lanes = slice(h * dk, (h + 1) * dk)
            d_ref[0, rows, lanes] = d_q[h]
            d_ref[1, rows, lanes] = d_k[h]
            d_ref[2, rows, lanes] = d_v[h]
            gates = gates + (jnp.where(lane_ids == h, d_beta[h], 0.0)
                             + jnp.where(lane_ids == GDN_HEADS + h, d_g[h], 0.0))
        dgate_ref[rows, :] = gates

    d_spec = pl.BlockSpec((3, cps * CHUNK, width), lambda g: (0, steps - 1 - g, 0))
    return pl.pallas_call(
        body, name="gdn_bwd",
        grid=(steps,),
        in_specs=[rows_blk(width, 0), rows_blk(width, 1), rows_blk(width, 2), rows_blk(LANES), rows_blk(LANES), gate_r,
                  per_chunk(dk, dk), per_chunk(CHUNK, CHUNK), rows_blk(width)] + [_HBM] * nx,
        out_specs=[d_spec, rows_blk(LANES)] + [_HBM] * nx,
        out_shape=[jax.ShapeDtypeStruct((3, t_len, width), F32),
                   jax.ShapeDtypeStruct((t_len, LANES), F32)] + _direct_out_shapes(scatter, (True,) * nx),
        scratch_shapes=[pltpu.VMEM((GDN_HEADS, dk, dk), F32)] + (_direct_semaphores(nx) if nx else []),
        compiler_params=_params("arbitrary"),
    )(gact, gact, gact, beta_c, gam_c, gam_r, s_all, t_all, d_o, *scatter)


def _group_matrix(width, group):
    r = lax.broadcasted_iota(jnp.int32, (width, width), 0)
    c = lax.broadcasted_iota(jnp.int32, (width, width), 1)
    return ((r // group) == (c // group)).astype(F32)


def _post_call(o_sb, o_gd, proj, x, target, w_out, sbw, gdw, fw, tm=256):
    t_len, d = x.shape
    half = 512
    zsb_blk = 1536 // half
    zgd_blk = 3584 // half

    def body(osb_ref, ogd_ref, zsb_ref, zgd_ref, x_ref, tg_ref, wo_ref, sbw_ref, gdw_ref, fw_ref,
             dx2_ref, dosb_ref, dogd_ref, dz_ref, loss_ref, gfw_ref, gsb_ref, ggd_ref, gwo_ref):
        step = pl.program_id(0)

        @pl.when(step == 0)
        def _():
            loss_ref[...] = jnp.zeros_like(loss_ref)
            gfw_ref[...] = jnp.zeros_like(gfw_ref)
            gsb_ref[...] = jnp.zeros_like(gsb_ref)
            ggd_ref[...] = jnp.zeros_like(ggd_ref)
            gwo_ref[...] = jnp.zeros_like(gwo_ref)

        def head_forward(o, z, w, gmat, inv):
            r = lax.rsqrt(_running_sum_mm(o * o, gmat) * inv + EPS)
            nrm = o * r * w
            sg = _sigmoid(z)
            return r, nrm, sg, nrm * (z * sg)

        def head_backward(d_m, o, z, w, gmat, inv, r, nrm, sg):
            d_n = d_m * (z * sg)
            d_z = d_m * nrm * (sg * (1.0 + z * (1.0 - sg)))
            dnw = d_n * w
            d_o = r * dnw - o * (r * r * r) * (_running_sum_mm(dnw * o, gmat) * inv)
            return d_o, d_z, jnp.sum(d_n * o * r, axis=0, keepdims=True)

        g_sb = _group_matrix(half, SB_HEAD_DIM).astype(MXU_DTYPE)
        g_gd = _group_matrix(half, GDN_HEAD_DIM).astype(MXU_DTYPE)
        osb, ogd, zsb, zgd = osb_ref[...], ogd_ref[...], zsb_ref[...], zgd_ref[...]
        sbw_v, gdw_v = sbw_ref[...], gdw_ref[...]
        r_sb, n_sb, sg_sb, m_sb = head_forward(osb, zsb, sbw_v, g_sb, 1.0 / SB_HEAD_DIM)
        r_gd, n_gd, sg_gd, m_gd = head_forward(ogd, zgd, gdw_v, g_gd, 1.0 / GDN_HEAD_DIM)
        mixed = jnp.concatenate([m_sb, m_gd], axis=1).astype(MXU_DTYPE)
        wo = wo_ref[...]
        x2 = x_ref[...] + jnp.dot(mixed, wo, preferred_element_type=F32)
        r2 = lax.rsqrt(jnp.mean(x2 * x2, axis=-1, keepdims=True) + EPS)
        fw_v = fw_ref[...]
        err = x2 * r2 * fw_v - tg_ref[...]
        loss_ref[...] += 0.5 * jnp.sum(jnp.sum(err * err, axis=-1, keepdims=True) * (1.0 / d))
        dy = err * (1.0 / d)
        gg = dy * fw_v
        dx2 = r2 * gg - x2 * ((r2 * r2 * r2) * jnp.mean(gg * x2, axis=-1, keepdims=True))
        gfw_ref[...] += jnp.sum(dy * x2 * r2, axis=0, keepdims=True)
        dx2_ref[...] = dx2
        dx2b = dx2.astype(MXU_DTYPE)
        d_mixed = lax.dot_general(dx2b, wo, _NT, preferred_element_type=F32)
        gwo_ref[...] += lax.dot_general(mixed, dx2b, _TN, preferred_element_type=F32)
        d_osb, d_zsb, gsb = head_backward(d_mixed[:, :half], osb, zsb, sbw_v, g_sb, 1.0 / SB_HEAD_DIM, r_sb, n_sb, sg_sb)
        d_ogd, d_zgd, ggd = head_backward(d_mixed[:, half:], ogd, zgd, gdw_v, g_gd, 1.0 / GDN_HEAD_DIM, r_gd, n_gd, sg_gd)
        dosb_ref[...] = d_osb
        dogd_ref[...] = d_ogd
        dz_ref[0] = d_zsb
        dz_ref[1] = d_zgd
        gsb_ref[...] += gsb
        ggd_ref[...] += ggd

    row_blk = lambda w: pl.BlockSpec((tm, w), lambda i: (i, 0))
    fixed = lambda r, w: pl.BlockSpec((r, w), lambda i: (0, 0))
    return pl.pallas_call(
        body, name="post",
        grid=(t_len // tm,),
        in_specs=[row_blk(half), row_blk(half),
                  pl.BlockSpec((tm, half), lambda i: (i, zsb_blk)),
                  pl.BlockSpec((tm, half), lambda i: (i, zgd_blk)),
                  row_blk(d), row_blk(d), fixed(d, d), fixed(1, half), fixed(1, half), fixed(1, d)],
        out_specs=[row_blk(d), row_blk(half), row_blk(half),
                   pl.BlockSpec((2, tm, half), lambda i: (DPROJ_GATE_SLOT // 2, i, 0)),
                   fixed(1, LANES), fixed(1, d), fixed(1, half), fixed(1, half), fixed(d, d)],
        out_shape=[jax.ShapeDtypeStruct((t_len, d), F32)] + [jax.ShapeDtypeStruct((t_len, half), F32)] * 2
                  + [jax.ShapeDtypeStruct((len(DPROJ_PIECE_OF_SLOT), t_len, half), F32),
                     jax.ShapeDtypeStruct((1, LANES), F32), jax.ShapeDtypeStruct((1, d), F32),
                     jax.ShapeDtypeStruct((1, half), F32), jax.ShapeDtypeStruct((1, half), F32),
                     jax.ShapeDtypeStruct((d, d), F32)],
        compiler_params=_params("arbitrary"),
    )(o_sb, o_gd, proj, proj, x, target, w_out, sbw, gdw, fw)


def _piece_of_slot(s):
    return jnp.where(s < DPROJ_GDN_SLOT, s, jnp.where(s < DPROJ_GATE_SLOT, s + 1,
                                                     jnp.where(s == DPROJ_GATE_SLOT, 3, 7)))


def _gw_in_call(h_t, dproj8):
    d, t_len = h_t.shape
    n_piece, _, pw = dproj8.shape

    def body(ht_ref, dp_ref, gw_ref):
        gw_ref[...] = jnp.dot(ht_ref[...], dp_ref[0].astype(MXU_DTYPE), preferred_element_type=F32)

    return pl.pallas_call(
        body, name="gw_in",
        grid=(n_piece,),
        in_specs=[pl.BlockSpec((d, t_len), lambda s: (0, 0)),
                  pl.BlockSpec((1, t_len, pw), lambda s: (s, 0, 0))],
        out_specs=pl.BlockSpec((d, pw), lambda s: (0, _piece_of_slot(s))),
        out_shape=jax.ShapeDtypeStruct((d, n_piece * pw), F32),
        compiler_params=_params("arbitrary"),
    )(h_t, dproj8)


def _gw_small_call(h_t, dsmall, tm=512):
    d, t_len = h_t.shape
    ns = dsmall.shape[1]

    def body(ht_ref, dp_ref, gw_ref):
        @pl.when(pl.program_id(0) == 0)
        def _():
            gw_ref[...] = jnp.zeros_like(gw_ref)

        gw_ref[...] += jnp.dot(ht_ref[...], dp_ref[...].astype(MXU_DTYPE), preferred_element_type=F32)

    return pl.pallas_call(
        body, name="gw_small",
        grid=(t_len // tm,),
        in_specs=[pl.BlockSpec((d, tm), lambda t: (0, t)),
                  pl.BlockSpec((tm, ns), lambda t: (t, 0))],
        out_specs=pl.BlockSpec((d, ns), lambda t: (0, 0)),
        out_shape=jax.ShapeDtypeStruct((d, ns), F32),
        compiler_params=_params("arbitrary"),
    )(h_t, dsmall)


def _dx_call(dproj8, dsmall, w_main, w_small, x, r, dx2, norm_w, tm=256):
    t_len, d = x.shape
    n_piece, _, pw = dproj8.shape
    ns = dsmall.shape[1]

    def body(dp_ref, ds_ref, wm_ref, ws_ref, x_ref, r_ref, dx2_ref, nw_ref, gx_ref, gnw_ref):
        @pl.when(pl.program_id(0) == 0)
        def _():
            gnw_ref[...] = jnp.zeros_like(gnw_ref)

        dh = lax.dot_general(ds_ref[...].astype(MXU_DTYPE), ws_ref[...], _NT, preferred_element_type=F32)
        for s, p in enumerate(DPROJ_PIECE_OF_SLOT):
            dh = dh + lax.dot_general(dp_ref[s].astype(MXU_DTYPE), wm_ref[:, p * pw:(p + 1) * pw], _NT,
                                      preferred_element_type=F32)
        xv, rv = x_ref[...], r_ref[...]
        dn = dh * nw_ref[...]
        gx_ref[...] = dx2_ref[...] + rv * dn - xv * ((rv * rv * rv) * jnp.mean(dn * xv, axis=-1, keepdims=True))
        gnw_ref[...] += jnp.sum(dh * xv * rv, axis=0, keepdims=True)

    return pl.pallas_call(
        body, name="dx",
        grid=(t_len // tm,),
        in_specs=[pl.BlockSpec((n_piece, tm, pw), lambda i: (0, i, 0)),
                  pl.BlockSpec((tm, ns), lambda i: (i, 0)),
                  pl.BlockSpec((d, n_piece * pw), lambda i: (0, 0)),
                  pl.BlockSpec((d, ns), lambda i: (0, 0)),
                  pl.BlockSpec((tm, d), lambda i: (i, 0)),
                  pl.BlockSpec((tm, 1), lambda i: (i, 0)),
                  pl.BlockSpec((tm, d), lambda i: (i, 0)),
                  pl.BlockSpec((1, d), lambda i: (0, 0))],
        out_specs=[pl.BlockSpec((tm, d), lambda i: (i, 0)),
                   pl.BlockSpec((1, d), lambda i: (0, 0))],
        out_shape=[jax.ShapeDtypeStruct((t_len, d), F32), jax.ShapeDtypeStruct((1, d), F32)],
        compiler_params=_params("arbitrary"),
    )(dproj8, dsmall, w_main, w_small, x, r, dx2, norm_w)


def _exchange_call(name, srcs, per_peer):
    n = len(srcs)

    def body(*refs):
        src_refs, out_refs = refs[:n], refs[n:2 * n]
        copies = _direct_copies(src_refs, out_refs, *refs[2 * n:], per_peer)
        _start_all(copies)
        _wait_all(copies)

    hbm = pl.BlockSpec(memory_space=pl.ANY)
    return pl.pallas_call(
        body, name=name,
        in_specs=[hbm] * n, out_specs=[hbm] * n, out_shape=_direct_out_shapes(srcs, per_peer),
        scratch_shapes=_direct_semaphores(n),
    )(*srcs)


def _direct_out_shapes(srcs, per_peer):
    return [jax.ShapeDtypeStruct(s.shape if pp else (N_DEV,) + s.shape, s.dtype) for s, pp in zip(srcs, per_peer)]


def _direct_semaphores(n):
    return [pltpu.SemaphoreType.DMA((n * (N_DEV - 1),)), pltpu.SemaphoreType.DMA((n * (N_DEV - 1),)),
            pltpu.SemaphoreType.DMA((n,))]


def _direct_copies(src_refs, out_refs, send_sems, recv_sems, local_sems, per_peer):
    x, y, c = lax.axis_index("x"), lax.axis_index("y"), lax.axis_index("c")
    me = 4 * x + 2 * y + c
    local, remote = [], []
    for a in range(len(src_refs)):
        mine = src_refs[a].at[me] if per_peer[a] else src_refs[a]
        local.append(pltpu.make_async_copy(mine, out_refs[a].at[me], local_sems.at[a]))
    for k in range(1, N_DEV):
        kx, ky, kc = (k >> 2) & 1, (k >> 1) & 1, k & 1
        px = 1 - x if kx else x
        py = 1 - y if ky else y
        pc = 1 - c if kc else c
        peer = 4 * px + 2 * py + pc
        for a in range(len(src_refs)):
            sem = a * (N_DEV - 1) + (k - 1)
            remote.append(pltpu.make_async_remote_copy(
                src_ref=src_refs[a].at[peer] if per_peer[a] else src_refs[a], dst_ref=out_refs[a].at[me],
                send_sem=send_sems.at[sem], recv_sem=recv_sems.at[sem],
                device_id=(px, py, pc), device_id_type=pl.DeviceIdType.MESH))
    return local, remote


def _start_all(copies):
    local, remote = copies
    for cp in local + remote:
        cp.start()


def _wait_all(copies):
    local, remote = copies
    for cp in remote:
        cp.wait_send()
    for cp in remote:
        cp.wait_recv()
    for cp in local:
        cp.wait()


N_CHIPS = 4
_HBM = pl.BlockSpec(memory_space=pl.ANY)
_MESH = pl.DeviceIdType.MESH


def _gather_call(name, srcs):
    n = len(srcs)
    per = N_DEV - 1

    def body(*refs):
        src_refs, out_refs = refs[:n], refs[n:2 * n]
        send_sems, recv_sems, local_sems = refs[2 * n:]
        x, y, c = lax.axis_index("x"), lax.axis_index("y"), lax.axis_index("c")
        me, sibling = (x, y, c), (x, y, 1 - c)
        chips = [(1 - x, y), (x, 1 - y), (1 - x, 1 - y)]
        slot = lambda px, py, pc: 4 * px + 2 * py + pc

        def copy(a, k, block, to, from_src=False):
            rows = out_refs[a].at[slot(*block)]
            return pltpu.make_async_remote_copy(
                src_ref=src_refs[a] if from_src else rows, dst_ref=rows,
                send_sem=send_sems.at[a * per + k], recv_sem=recv_sems.at[a * per + k],
                device_id=to, device_id_type=_MESH)

        local = [pltpu.make_async_copy(src_refs[a], out_refs[a].at[slot(*me)], local_sems.at[a]) for a in range(n)]
        for cp in local:
            cp.start()
        first = []
        for a in range(n):
            first.append(copy(a, 0, me, sibling, True))
            first += [copy(a, 1 + j, me, (*chip, c), True) for j, chip in enumerate(chips)]
        for cp in first:
            cp.start()
        passed = []
        for j, chip in enumerate(chips):
            for a in range(n):
                copy(a, 1 + j, (*chip, c), me).wait_recv()
                fwd = copy(a, 4 + j, (*chip, c), sibling)
                fwd.start()
                passed.append(fwd)
        for a in range(n):
            copy(a, 0, sibling, me).wait_recv()
            for j, chip in enumerate(chips):
                copy(a, 4 + j, (*chip, 1 - c), me).wait_recv()
        for cp in first + passed:
            cp.wait_send()
        for cp in local:
            cp.wait()

    return pl.pallas_call(
        body, name=name,
        in_specs=[_HBM] * n, out_specs=[_HBM] * n,
        out_shape=[jax.ShapeDtypeStruct((N_DEV,) + s.shape, s.dtype) for s in srcs],
        scratch_shapes=[pltpu.SemaphoreType.DMA((n * per,)), pltpu.SemaphoreType.DMA((n * per,)),
                        pltpu.SemaphoreType.DMA((n,))],
    )(*srcs)


def _sibling_send_call(name, srcs):
    n = len(srcs)

    def body(*refs):
        src_refs, out_refs = refs[:n], refs[n:2 * n]
        send_sems, recv_sems = refs[2 * n:]
        x, y, c = lax.axis_index("x"), lax.axis_index("y"), lax.axis_index("c")
        copies = []
        for a in range(n):
            for ch in range(N_CHIPS):
                copies.append(pltpu.make_async_remote_copy(
                    src_ref=src_refs[a].at[2 * ch + (1 - c)], dst_ref=out_refs[a].at[ch],
                    send_sem=send_sems.at[a * N_CHIPS + ch], recv_sem=recv_sems.at[a * N_CHIPS + ch],
                    device_id=(x, y, 1 - c), device_id_type=_MESH))
        for cp in copies:
            cp.start()
        for cp in copies:
            cp.wait_send()
        for cp in copies:
            cp.wait_recv()

    return pl.pallas_call(
        body, name=name,
        in_specs=[_HBM] * n, out_specs=[_HBM] * n,
        out_shape=[jax.ShapeDtypeStruct((N_CHIPS,) + s.shape[1:], s.dtype) for s in srcs],
        scratch_shapes=[pltpu.SemaphoreType.DMA((n * N_CHIPS,)), pltpu.SemaphoreType.DMA((n * N_CHIPS,))],
    )(*srcs)


def _pair_sum_call(name, parts, from_sibling, tr):
    _, rows, cols = parts.shape

    def body(p_ref, s_ref, o_ref):
        o_ref[...] = (p_ref[...].astype(F32) + s_ref[...].astype(F32)).astype(o_ref.dtype)

    return pl.pallas_call(
        body, name=name,
        grid=(N_CHIPS, rows // tr),
        in_specs=[pl.BlockSpec((1, tr, cols), lambda ch, i: (2 * ch + lax.axis_index("c"), i, 0)),
                  pl.BlockSpec((1, tr, cols), lambda ch, i: (ch, i, 0))],
        out_specs=pl.BlockSpec((1, tr, cols), lambda ch, i: (ch, i, 0)),
        out_shape=jax.ShapeDtypeStruct((N_CHIPS, rows, cols), WIRE_DTYPE),
        compiler_params=_params("arbitrary", "arbitrary"),
    )(parts, from_sibling)


def _chip_exchange_call(name, srcs):
    n = len(srcs)
    per = N_CHIPS - 1

    def body(*refs):
        src_refs, out_refs = refs[:n], refs[n:2 * n]
        send_sems, recv_sems, local_sems = refs[2 * n:]
        x, y, c = lax.axis_index("x"), lax.axis_index("y"), lax.axis_index("c")
        mine = 2 * x + y
        chips = [(1 - x, y), (x, 1 - y), (1 - x, 1 - y)]
        local = [pltpu.make_async_copy(src_refs[a].at[mine], out_refs[a].at[mine], local_sems.at[a]) for a in range(n)]
        for cp in local:
            cp.start()
        remote = []
        for a in range(n):
            for j, (px, py) in enumerate(chips):
                remote.append(pltpu.make_async_remote_copy(
                    src_ref=src_refs[a].at[2 * px + py], dst_ref=out_refs[a].at[mine],
                    send_sem=send_sems.at[a * per + j], recv_sem=recv_sems.at[a * per + j],
                    device_id=(px, py, c), device_id_type=_MESH))
        for cp in remote:
            cp.start()
        for cp in remote:
            cp.wait_send()
        for cp in remote:
            cp.wait_recv()
        for cp in local:
            cp.wait()

    return pl.pallas_call(
        body, name=name,
        in_specs=[_HBM] * n, out_specs=[_HBM] * n,
        out_shape=[jax.ShapeDtypeStruct(s.shape, s.dtype) for s in srcs],
        scratch_shapes=[pltpu.SemaphoreType.DMA((n * per,)), pltpu.SemaphoreType.DMA((n * per,)),
                        pltpu.SemaphoreType.DMA((n,))],
    )(*srcs)


def _adam_call(name, parts, w, m, v, tr):
    rows, cols = w.shape
    n_slots = parts.shape[0]

    def body(p_ref, w_ref, m_ref, v_ref, g_ref, d_ref, nm_ref, nv_ref):
        g = p_ref[0].astype(F32)
        for s in range(1, n_slots):
            g = g + p_ref[s].astype(F32)
        m_new = ADAM_B1 * m_ref[...] + (1.0 - ADAM_B1) * g
        v_new = ADAM_B2 * v_ref[...] + (1.0 - ADAM_B2) * (g * g)
        m_hat = m_new / (1.0 - ADAM_B1 ** ADAM_STEP)
        v_hat = v_new / (1.0 - ADAM_B2 ** ADAM_STEP)
        g_ref[...] = g
        d_ref[...] = -ADAM_LR * (m_hat / (jnp.sqrt(v_hat) + ADAM_EPS) + ADAM_WD * w_ref[...])
        nm_ref[...] = m_new
        nv_ref[...] = v_new

    blk = pl.BlockSpec((tr, cols), lambda i: (i, 0))
    return pl.pallas_call(
        body, name=name,
        grid=(rows // tr,),
        in_specs=[pl.BlockSpec((n_slots, tr, cols), lambda i: (0, i, 0)), blk, blk, blk],
        out_specs=[blk] * 4,
        out_shape=[jax.ShapeDtypeStruct((rows, cols), F32)] * 4,
        compiler_params=_params("arbitrary"),
    )(parts, w, m, v)


N_PIECES = 8
PIECE = 512
SHARD_COLS = 513
SHARD_PAD = 640
RELAYOUT_ROWS = 256


def _from_shards_call(shards):
    _, d, _ = shards.shape
    tr = RELAYOUT_ROWS

    def body(p_ref, m_ref, s_ref):
        lane = lax.broadcasted_iota(jnp.int32, (tr, SHARD_PAD), 1)
        pad = jnp.zeros((tr, SHARD_PAD - SHARD_COLS), F32)
        sh = [jnp.concatenate([p_ref[s].astype(F32), pad], axis=1) for s in range(N_DEV)]
        for p in range(N_PIECES):
            y = sh[p] if p == 0 else pltpu.roll(sh[p], p, axis=1)
            if p > 0:
                y = jnp.where(lane < p, pltpu.roll(sh[p - 1], SHARD_PAD - (SHARD_COLS - p), axis=1), y)
            m_ref[:, p * PIECE:(p + 1) * PIECE] = y[:, :PIECE].astype(m_ref.dtype)
        first_gate = N_PIECES * PIECE - (N_DEV - 1) * SHARD_COLS
        s_ref[...] = pltpu.roll(sh[N_DEV - 1], SHARD_PAD - first_gate, axis=1)[:, :LANES].astype(s_ref.dtype)

    return pl.pallas_call(
        body, name="w_in_from_shards",
        grid=(d // tr,),
        in_specs=[pl.BlockSpec((N_DEV, tr, SHARD_COLS), lambda i: (0, i, 0))],
        out_specs=[pl.BlockSpec((tr, N_PIECES * PIECE), lambda i: (i, 0)), pl.BlockSpec((tr, LANES), lambda i: (i, 0))],
        out_shape=[jax.ShapeDtypeStruct((d, N_PIECES * PIECE), shards.dtype),
                   jax.ShapeDtypeStruct((d, LANES), shards.dtype)],
        compiler_params=_params("arbitrary"),
    )(shards)


def _to_shards_call(main, gates, out_dtype):
    d = main.shape[0]
    tr = RELAYOUT_ROWS

    def body(m_ref, s_ref, o_ref):
        for s in range(N_DEV):
            if s < N_DEV - 1:
                x = m_ref[:, s * PIECE:s * PIECE + SHARD_PAD]
            else:
                x = jnp.concatenate([m_ref[:, s * PIECE:(s + 1) * PIECE], s_ref[...]], axis=1)
            y = x if s == 0 else pltpu.roll(x, SHARD_PAD - s, axis=1)
            o_ref[s] = y[:, :SHARD_COLS].astype(out_dtype)

    return pl.pallas_call(
        body, name="w_in_to_shards",
        grid=(d // tr,),
        in_specs=[pl.BlockSpec((tr, N_PIECES * PIECE), lambda i: (i, 0)), pl.BlockSpec((tr, LANES), lambda i: (i, 0))],
        out_specs=pl.BlockSpec((N_DEV, tr, SHARD_COLS), lambda i: (0, i, 0)),
        out_shape=jax.ShapeDtypeStruct((N_DEV, d, SHARD_COLS), out_dtype),
        compiler_params=_params("arbitrary"),
    )(main, gates)


def _adamw(g, w, m, v):
    m_new = ADAM_B1 * m + (1.0 - ADAM_B1) * g
    v_new = ADAM_B2 * v + (1.0 - ADAM_B2) * (g * g)
    m_hat = m_new / (1.0 - ADAM_B1 ** ADAM_STEP)
    v_hat = v_new / (1.0 - ADAM_B2 ** ADAM_STEP)
    return -ADAM_LR * (m_hat / (jnp.sqrt(v_hat) + ADAM_EPS) + ADAM_WD * w), m_new, v_new


def _adam_small_call(parts, ws, ms, vs):
    n = len(ws)
    n_slots = parts.shape[0]

    def body(*refs):
        p_ref = refs[0]
        w_refs, m_refs, v_refs = refs[1:1 + n], refs[1 + n:1 + 2 * n], refs[1 + 2 * n:1 + 3 * n]
        loss_ref = refs[1 + 3 * n]
        outs = refs[2 + 3 * n:]
        g_all = p_ref[0]
        for s in range(1, n_slots):
            g_all = g_all + p_ref[s]
        loss_ref[...] = g_all[n:n + 1, 0:1]
        for r in range(n):
            size = w_refs[r].shape[1]
            g = g_all[r:r + 1, :size]
            delta, m_new, v_new = _adamw(g, w_refs[r][...], m_refs[r][...], v_refs[r][...])
            for kind, val in enumerate((g, delta, m_new, v_new)):
                outs[kind * n + r][...] = val

    vm = pl.BlockSpec(memory_space=pltpu.VMEM)
    shapes = [jax.ShapeDtypeStruct(w.shape, F32) for w in ws]
    return pl.pallas_call(
        body, name="adam_small",
        in_specs=[vm] * (1 + 3 * n), out_specs=[vm] * (1 + 4 * n),
        out_shape=[jax.ShapeDtypeStruct((1, 1), F32)] + shapes * 4,
    )(parts, *ws, *ms, *vs)


_SMALL_ROWS = ("norm1_w", "final_norm_w", "sb_norm_w", "gdn_norm_w", "gdn_A_log", "gdn_dt_bias", "loss")


def _pack_small(vals, width):
    rows = [jnp.pad(a.reshape(1, -1).astype(F32), ((0, 0), (0, width - a.size))) for a in vals]
    rows += [jnp.zeros((1, width), F32)] * (8 - len(rows))
    return jnp.concatenate(rows, axis=0)


def _device_step(x2d, tgt, w_main, w_small, w_out_full, conv_full, norm1_w, sb_norm_w, gdn_A_log, gdn_dt_bias,
                 gdn_norm_w, final_norm_w, distributed=False):
    t_len, d = x2d.shape
    n_chunks = t_len // CHUNK
    w_main, w_small, w_out_full = (a.astype(MXU_DTYPE) for a in (w_main, w_small, w_out_full))
    w_small_t = w_small[:, :2 * GDN_HEADS].T

    pad_lanes = lambda a, lo: jnp.pad(a.reshape(1, -1), ((0, 0), (lo, LANES - lo - a.size)))
    alog_l, dtb_l = pad_lanes(gdn_A_log, GDN_HEADS), pad_lanes(gdn_dt_bias, GDN_HEADS)
    alog_c, dtb_c = alog_l[:, :8].T, dtb_l[:, :8].T
    sbw = jnp.tile(sb_norm_w, (1, 512 // SB_HEAD_DIM))
    gdw = jnp.tile(gdn_norm_w, (1, 512 // GDN_HEAD_DIM))
    fw = final_norm_w.reshape(1, d)

    if distributed:
        proj, ps, pst, h_t, r1, w_out_g, conv_g = _inproj_call(
            x2d, norm1_w, w_main, w_small, w_small_t, gather=(w_out_full, conv_full))
        w_out_full = w_out_g.reshape(d, d)
        conv_full = conv_g.transpose(1, 0, 2).reshape(CONV_WIDTH, N_DEV * conv_g.shape[2])
    else:
        proj, ps, pst, h_t, r1 = _inproj_call(x2d, norm1_w, w_main, w_small, w_small_t)
    o_sb, sp_total, sb_blocks_run = _sb_fwd_call(proj, t_len)
    gact = _gdn_prep_call(proj, conv_full, t_len)
    beta_l, gcol_l, grow = _gdn_gates_call(ps, pst, alog_l, dtb_l, alog_c, dtb_c, t_len)
    gam_r = grow[GDN_HEADS:2 * GDN_HEADS].reshape(GDN_HEADS, n_chunks, 1, CHUNK)
    o_gd, s_all, t_all = _gdn_fwd_call(gact, beta_l, gcol_l, gam_r, t_len)

    (dx2, d_osb, d_ogd, dproj8, loss_p, g_fw, g_sbw, g_gdw, g_wout) = _post_call(
        o_sb, o_gd, proj, x2d, tgt, w_out_full, sbw, gdw, fw)

    dproj8 = _sb_bwd_call(proj, sp_total, sb_blocks_run, d_osb, dproj8, t_len)
    if distributed:
        d_gact3, d_gates, g_wout = _gdn_bwd_call(gact, beta_l, gcol_l, gam_r, s_all, t_all, d_ogd, t_len,
                                                 scatter=(g_wout.reshape(N_DEV, d // N_DEV, d),))
    else:
        d_gact3, d_gates = _gdn_bwd_call(gact, beta_l, gcol_l, gam_r, s_all, t_all, d_ogd, t_len)
    dproj8, g_conv = _gdn_prep_bwd_call(proj, conv_full, d_gact3, dproj8, t_len)
    dsmall, g_alog, g_dtb = _gdn_gates_bwd_call(ps, alog_l, dtb_l, d_gates, t_len)

    g_w_main = _gw_in_call(h_t, dproj8)
    g_w_small = _gw_small_call(h_t, dsmall)
    grad_x, g_n1 = _dx_call(dproj8, dsmall, w_main, w_small, x2d, r1, dx2, norm1_w)
    return (loss_p, grad_x, g_n1, (g_w_main, g_w_small), g_sbw, g_conv, g_alog, g_dtb, g_gdw, g_wout, g_fw)


def kernel(x, norm1_w, w_in, sb_norm_w, gdn_conv_w, gdn_A_log, gdn_dt_bias, gdn_norm_w, w_out, final_norm_w, loss_target, m_norm1_w, m_w_in, m_sb_norm_w, m_gdn_conv_w, m_gdn_A_log, m_gdn_dt_bias, m_gdn_norm_w, m_w_out, m_final_norm_w, v_norm1_w, v_w_in, v_sb_norm_w, v_gdn_conv_w, v_gdn_A_log, v_gdn_dt_bias, v_gdn_norm_w, v_w_out, v_final_norm_w):
    d = x.shape[2]
    shard_cols = w_in.shape[2]
    conv_cols = gdn_conv_w.shape[2]

    (w_in_g,) = _gather_call("gather_weights", [w_in[0].astype(WIRE_DTYPE)])
    w_main, w_small = _from_shards_call(w_in_g)

    (loss_p, grad_x, g_n1, (g_w_main, g_w_small), g_sbw, g_conv, g_alog, g_dtb, g_gdw, p_wout, g_fw) = _device_step(
        x[0], loss_target[0], w_main, w_small, w_out[0].astype(WIRE_DTYPE), gdn_conv_w[0], norm1_w, sb_norm_w,
        gdn_A_log, gdn_dt_bias, gdn_norm_w, final_norm_w, distributed=True)

    g_w_in_parts = _to_shards_call(g_w_main, g_w_small, WIRE_DTYPE)
    g_conv_parts = g_conv.reshape(CONV_WIDTH, N_DEV, conv_cols).transpose(1, 0, 2)
    fold = lambda a, group: a.reshape(-1, group).sum(axis=0)
    small_g = _pack_small([g_n1, g_fw, fold(g_sbw, SB_HEAD_DIM), fold(g_gdw, GDN_HEAD_DIM),
                           g_alog[0, GDN_HEADS:2 * GDN_HEADS], g_dtb[0, GDN_HEADS:2 * GDN_HEADS],
                           loss_p[0, :1]], d)
    (sib_w_in,) = _sibling_send_call("grads_to_sibling", [g_w_in_parts])
    c_w_in = _pair_sum_call("pair_sum_w_in", g_w_in_parts, sib_w_in, 256)
    (p_w_in,) = _chip_exchange_call("grads_to_chips", [c_w_in])
    p_small, p_conv = _exchange_call("exchange_small", [small_g, g_conv_parts], [False, True])

    r_w_in = _adam_call("adam_w_in", p_w_in, w_in[0], m_w_in[0], v_w_in[0], 256)
    r_wout = _adam_call("adam_w_out", p_wout, w_out[0], m_w_out[0], v_w_out[0], d // N_DEV)
    r_conv = _adam_call("adam_conv", p_conv, gdn_conv_w[0], m_gdn_conv_w[0], v_gdn_conv_w[0], CONV_WIDTH)

    row = lambda a: a.reshape(1, -1)
    n_small = len(_SMALL_ROWS) - 1
    r_small = _adam_small_call(
        p_small,
        [norm1_w, row(final_norm_w), sb_norm_w, gdn_norm_w, gdn_A_log, gdn_dt_bias],
        [m_norm1_w, row(m_final_norm_w), m_sb_norm_w, m_gdn_norm_w, m_gdn_A_log, m_gdn_dt_bias],
        [v_norm1_w, row(v_final_norm_w), v_sb_norm_w, v_gdn_norm_w, v_gdn_A_log, v_gdn_dt_bias])

    def small_out(kind, name):
        out = r_small[1 + kind * n_small + _SMALL_ROWS.index(name)]
        return out.reshape(final_norm_w.shape) if name == "final_norm_w" else out

    def outputs(kind):
        return (small_out(kind, "norm1_w"), r_w_in[kind][None], small_out(kind, "sb_norm_w"), r_conv[kind][None],
                small_out(kind, "gdn_A_log"), small_out(kind, "gdn_dt_bias"), small_out(kind, "gdn_norm_w"),
                r_wout[kind][None], small_out(kind, "final_norm_w"))

    return (r_small[0][0, 0], grad_x[None], *outputs(0), *outputs(1), *outputs(2), *outputs(3))
```

```python
import functools

import jax
import jax.numpy as jnp
from jax import lax
from jax.experimental import pallas as pl
from jax.experimental.pallas import tpu as pltpu

F32 = jnp.float32
MXU_DTYPE = jnp.bfloat16
WIRE_DTYPE = jnp.bfloat16
EXACT = lax.Precision.HIGHEST
EPS = 1e-6
N_DEV = 8
SB_HEAD_DIM = 64
GDN_HEAD_DIM = 128
GDN_HEADS = 4
GDN_CHUNKS_PER_STEP = 4
GDN_BWD_GROUP = 1
CHUNK = 64
CONV_WIDTH = 4
LANES = 128
SB_BLOCK = 128
SB_BQ = 256
VMEM_LIMIT_BYTES = 56 * 1024 * 1024

DPROJ_PIECE_OF_SLOT = (0, 1, 2, 4, 5, 6, 3, 7)
DPROJ_SB_SLOT, DPROJ_GDN_SLOT, DPROJ_GATE_SLOT = 0, 3, 6

ADAM_LR = 0.001
ADAM_B1 = 0.9
ADAM_B2 = 0.999
ADAM_EPS = 1e-08
ADAM_WD = 0.01
ADAM_STEP = 10

_NN = (((1,), (0,)), ((), ()))
_NT = (((1,), (1,)), ((), ()))
_TN = (((0,), (0,)), ((), ()))
_BNN = (((2,), (1,)), ((0,), (0,)))
_BNT = (((2,), (2,)), ((0,), (0,)))
_BTN = (((1,), (1,)), ((0,), (0,)))


def _mm(a, b):
    return jnp.dot(a.astype(MXU_DTYPE), b.astype(MXU_DTYPE), preferred_element_type=F32)


def _mm_nt(a, b):
    return lax.dot_general(a.astype(MXU_DTYPE), b.astype(MXU_DTYPE), _NT, preferred_element_type=F32)


def _mm_tn(a, b):
    return lax.dot_general(a.astype(MXU_DTYPE), b.astype(MXU_DTYPE), _TN, preferred_element_type=F32)


def _mx(a, b):
    return jnp.dot(a, b, precision=EXACT, preferred_element_type=F32)


def _mx_nt(a, b):
    return lax.dot_general(a, b, _NT, precision=EXACT, preferred_element_type=F32)


def _mx_tn(a, b):
    return lax.dot_general(a, b, _TN, precision=EXACT, preferred_element_type=F32)


def _split(x):
    hi = x.astype(MXU_DTYPE)
    return hi, (x - hi.astype(F32)).astype(MXU_DTYPE)


def _m3_general(a, b, dims):
    ah, al = _split(a)
    bh, bl = _split(b)
    dot = lambda x, y: lax.dot_general(x, y, dims, preferred_element_type=F32)
    (contract, _), (batch, _) = dims
    free = [ax for ax in range(a.ndim) if ax not in contract and ax not in batch][0]
    m = a.shape[free]
    both = dot(jnp.concatenate([ah, al], axis=free), bh)
    out_axis = len(batch)
    hi_part = lax.slice_in_dim(both, 0, m, axis=out_axis)
    lo_part = lax.slice_in_dim(both, m, 2 * m, axis=out_axis)
    return hi_part + (dot(ah, bl) + lo_part)


def _m3(a, b):
    return _m3_general(a, b, _NN)


def _m3_nt(a, b):
    return _m3_general(a, b, _NT)


def _m3_tn(a, b):
    return _m3_general(a, b, _TN)


def _sigmoid(z):
    return 1.0 / (1.0 + jnp.exp(-z))


def _softplus(z):
    return jnp.maximum(z, 0.0) + jnp.log(1.0 + jnp.exp(-jnp.abs(z)))


def _params(*semantics):
    return pltpu.CompilerParams(dimension_semantics=semantics, vmem_limit_bytes=VMEM_LIMIT_BYTES)


def _inproj_call(x, norm_w, w_main, w_small, w_small_t, gather=(), tm=256):
    t_len, d = x.shape
    n = w_main.shape[1]
    ns = w_small.shape[1]
    nst = w_small_t.shape[0]
    ng = len(gather)
    steps = t_len // tm

    def body(*refs):
        x_ref, nw_ref, wm_ref, ws_ref, wst_ref = refs[:5]
        pm_ref, ps_ref, pst_ref, ht_ref, r_ref = refs[5 + ng:10 + ng]
        copies = lambda: _direct_copies(refs[5:5 + ng], refs[10 + ng:10 + 2 * ng], *refs[10 + 2 * ng:], (False,) * ng)
        if ng:
            pl.when(pl.program_id(0) == 0)(lambda: _start_all(copies()))
        xv = x_ref[...]
        r = lax.rsqrt(jnp.mean(xv * xv, axis=-1, keepdims=True) + EPS)
        h = xv * r * nw_ref[...]
        hb = h.astype(MXU_DTYPE)
        for n0 in range(0, n, 512):
            pm_ref[:, n0:n0 + 512] = jnp.dot(hb, wm_ref[:, n0:n0 + 512], preferred_element_type=F32)
        ps_ref[...] = jnp.dot(hb, ws_ref[...], preferred_element_type=F32)
        pst_ref[...] = lax.dot_general(wst_ref[...], hb, _NT, preferred_element_type=F32)
        ht_ref[...] = h.T.astype(MXU_DTYPE)
        r_ref[...] = r
        if ng:
            pl.when(pl.program_id(0) == steps - 1)(lambda: _wait_all(copies()))

    return pl.pallas_call(
        body, name="inproj",
        grid=(steps,),
        in_specs=[pl.BlockSpec((tm, d), lambda i: (i, 0)),
                  pl.BlockSpec((1, d), lambda i: (0, 0)),
                  pl.BlockSpec((d, n), lambda i: (0, 0)),
                  pl.BlockSpec((d, ns), lambda i: (0, 0)),
                  pl.BlockSpec((nst, d), lambda i: (0, 0))] + [_HBM] * ng,
        out_specs=[pl.BlockSpec((tm, n), lambda i: (i, 0)),
                   pl.BlockSpec((tm, ns), lambda i: (i, 0)),
                   pl.BlockSpec((nst, tm), lambda i: (0, i)),
                   pl.BlockSpec((d, tm), lambda i: (0, i)),
                   pl.BlockSpec((tm, 1), lambda i: (i, 0))] + [_HBM] * ng,
        out_shape=[jax.ShapeDtypeStruct((t_len, n), F32),
                   jax.ShapeDtypeStruct((t_len, ns), F32),
                   jax.ShapeDtypeStruct((nst, t_len), F32),
                   jax.ShapeDtypeStruct((d, t_len), MXU_DTYPE),
                   jax.ShapeDtypeStruct((t_len, 1), F32)] + _direct_out_shapes(gather, (False,) * ng),
        scratch_shapes=_direct_semaphores(ng) if ng else [],
        compiler_params=_params("arbitrary"),
    )(x, norm_w, w_main, w_small, w_small_t, *gather)


def _running_sum_mm(x, tri):
    hi = x.astype(MXU_DTYPE)
    lo = (x - hi.astype(F32)).astype(MXU_DTYPE)
    return jnp.dot(hi, tri, preferred_element_type=F32) + jnp.dot(lo, tri, preferred_element_type=F32)


def _sb_iotas():
    row_i = lax.broadcasted_iota(jnp.int32, (SB_BQ, SB_BLOCK), 0)
    col_i = lax.broadcasted_iota(jnp.int32, (SB_BQ, SB_BLOCK), 1)
    sq_r = lax.broadcasted_iota(jnp.int32, (SB_BLOCK, SB_BLOCK), 0)
    sq_c = lax.broadcasted_iota(jnp.int32, (SB_BLOCK, SB_BLOCK), 1)
    return row_i, col_i, sq_r, sq_c


SB_DIAG_BLOCKS = SB_BQ // SB_BLOCK
SB_EXP_FLOOR = -110.0


def _sb_keys_descending(qi, tile, carry, z_bounds, n_heads, has_free):
    group = SB_DIAG_BLOCKS
    n_free = group * qi
    diag = list(range(group - 1, -1, -1))
    carry = tile([n_free + j for j in diag], [True] * group, carry, [j * SB_BLOCK for j in diag])

    def largest_exponent(c):
        worst = jnp.max(z_bounds[0] - c[1])
        for h in range(1, n_heads):
            worst = jnp.maximum(worst, jnp.max(z_bounds[h] - c[1 + h]))
        return worst

    always = group if has_free else 0

    def cond(state):
        return (state[0] < n_free) & ((state[1] > SB_EXP_FLOOR) | (state[0] < always))

    def body(state):
        first = n_free - 1 - state[0]
        c = tile([first - j for j in range(group)], [False] * group, state[2:])
        return (state[0] + group, largest_exponent(c), *c)

    out = lax.while_loop(cond, body, (jnp.int32(0), largest_exponent(carry), *carry))
    return out[2:], out[0]


def _sb_keys_ascending(qi, n_run, tile, carry, has_free):
    group = SB_DIAG_BLOCKS
    n_free = group * qi
    diag = list(range(group))
    kjs, los, masked = [n_free + j for j in diag], [j * SB_BLOCK for j in diag], [True] * group
    if has_free:
        early = lambda s: [n_free - n_run + group * s + j for j in range(group)]
        carry = lax.fori_loop(0, n_run // group - 1, lambda s, c: tile(early(s), [False] * group, c), carry)
        kjs, los, masked = [n_free - group + j for j in range(group)] + kjs, [0] * group + los, [False] * group + masked
    return tile(kjs, masked, carry, los)


def _sb_fwd_call(proj, t_len):
    nq = t_len // SB_BQ
    scale = float(SB_HEAD_DIM) ** -0.5
    n_pairs = 512 // LANES
    per_pair = LANES // SB_HEAD_DIM

    def body(q_ref, k_ref, v_ref, o_ref, st_ref, nrun_ref):
        lane = lax.broadcasted_iota(jnp.int32, (1, LANES), 1)
        row_i, col_i, sq_r, sq_c = _sb_iotas()
        ge = (sq_r >= sq_c).astype(MXU_DTYPE)
        hms = [((lane // SB_HEAD_DIM) == hh).astype(F32) for hh in range(per_pair)]
        k_sq = k_ref[...] * k_ref[...]
        k_norms = [jnp.sqrt(jnp.max(jnp.sum(k_sq * hm, axis=-1, keepdims=True))) * (1.02 * scale) for hm in hms]

        def q_block(qi, has_free):
            r0 = qi * SB_BQ if isinstance(qi, int) else pl.multiple_of(qi * SB_BQ, SB_BQ)
            rows = pl.ds(r0, SB_BQ)
            q_all = q_ref[rows, :]
            qms = [(q_all * (hm * scale)).astype(MXU_DTYPE) for hm in hms]
            z_bounds = [jnp.sqrt(jnp.sum(q_all * q_all * hm, axis=-1, keepdims=True)) * kn
                        for hm, kn in zip(hms, k_norms)]

            def tile(kjs, masked, kc, los=None):
                heads = range(per_pair)
                los = los or [0] * len(kjs)
                pairs = [(t, h) for t in range(len(kjs)) for h in heads]
                add_rows = lambda full, lo, part: full + part if lo == 0 else jnp.concatenate(
                    [full[:lo], full[lo:] + part], axis=0)
                acc, cs = kc[0], list(kc[1:])
                s0s = [kj * SB_BLOCK if isinstance(kj, int) else pl.multiple_of(kj * SB_BLOCK, SB_BLOCK) for kj in kjs]
                kbs = [k_ref[pl.ds(s0, SB_BLOCK), :].astype(MXU_DTYPE) for s0 in s0s]
                v_alls = [v_ref[pl.ds(s0, SB_BLOCK), :] for s0 in s0s]
                vms = {(t, h): (v_alls[t] * hms[h]).astype(MXU_DTYPE) for t, h in pairs}
                zs = {(t, h): lax.dot_general(qms[h][los[t]:], kbs[t], _NT, preferred_element_type=F32)
                      for t, h in pairs}
                masks = [(col_i[lo:] + s0) < (row_i[lo:] + r0) if m else None for m, lo, s0 in zip(masked, los, s0s)]
                keep = lambda t, a: a if masks[t] is None else jnp.where(masks[t], a, 0.0)
                sps = {(t, h): keep(t, _softplus(zs[t, h])) for t, h in pairs}
                sums = {p: _running_sum_mm(sps[p], ge) for p in pairs}
                mass = {}
                for t, h in pairs:
                    mass[t, h] = cs[h] if t == 0 else add_rows(
                        mass[t - 1, h], los[t - 1], jnp.sum(sps[t - 1, h], axis=-1, keepdims=True))
                ws = {(t, h): keep(t, jnp.exp(zs[t, h] - (sums[t, h] + mass[t, h][los[t]:]))) for t, h in pairs}
                for t, h in pairs:
                    acc = add_rows(acc, los[t], jnp.dot(ws[t, h].astype(MXU_DTYPE), vms[t, h],
                                                        preferred_element_type=F32))
                last = len(kjs) - 1
                cs = [add_rows(mass[last, h], los[last], jnp.sum(sps[last, h], axis=-1, keepdims=True)) for h in heads]
                return (acc, *cs)

            zero_col = jnp.zeros((SB_BQ, 1), F32)
            out, n_run = _sb_keys_descending(
                qi, tile, (jnp.zeros((SB_BQ, LANES), F32),) + (zero_col,) * per_pair, z_bounds, per_pair, has_free)
            o_ref[rows, :] = out[0]
            for hh in range(per_pair):
                st_ref[hh, rows, :] = out[1 + hh]
            nrun_ref[pl.program_id(0), qi] = n_run

        q_block(0, False)
        lax.fori_loop(1, nq, lambda qi, carry: (q_block(qi, True), carry)[1], 0)

    return pl.pallas_call(
        body, name="sb_fwd",
        grid=(n_pairs,),
        in_specs=[pl.BlockSpec((t_len, LANES), lambda p: (0, p)),
                  pl.BlockSpec((t_len, LANES), lambda p: (0, n_pairs + p)),
                  pl.BlockSpec((t_len, LANES), lambda p: (0, 2 * n_pairs + p))],
        out_specs=[pl.BlockSpec((t_len, LANES), lambda p: (0, p)),
                   pl.BlockSpec((per_pair, t_len, 1), lambda p: (p, 0, 0)),
                   pl.BlockSpec(memory_space=pltpu.SMEM)],
        out_shape=[jax.ShapeDtypeStruct((t_len, 512), F32),
                   jax.ShapeDtypeStruct((n_pairs * per_pair, t_len, 1), F32),
                   jax.ShapeDtypeStruct((n_pairs, nq), jnp.int32)],
        compiler_params=_params("arbitrary"),
    )(proj, proj, proj)


def _sb_bwd_call(proj, sp_total, n_run_all, d_o, dproj, t_len):
    nq = t_len // SB_BQ
    scale = float(SB_HEAD_DIM) ** -0.5
    n_pairs = 512 // LANES
    per_pair = LANES // SB_HEAD_DIM

    def body(q_ref, k_ref, v_ref, st_ref, nrun_ref, do_ref, dproj_in_ref, d_ref):
        lane = lax.broadcasted_iota(jnp.int32, (1, LANES), 1)
        row_i, col_i, sq_r, sq_c = _sb_iotas()
        lt = (sq_r < sq_c).astype(MXU_DTYPE)
        le = (sq_r <= sq_c).astype(MXU_DTYPE)
        hms = [((lane // SB_HEAD_DIM) == hh).astype(F32) for hh in range(per_pair)]
        d_ref[1] = jnp.zeros((t_len, LANES), F32)
        d_ref[2] = jnp.zeros((t_len, LANES), F32)

        def q_block(qi, has_free):
            r0 = qi * SB_BQ if isinstance(qi, int) else pl.multiple_of(qi * SB_BQ, SB_BQ)
            rows = pl.ds(r0, SB_BQ)
            q_all, do_all = q_ref[rows, :], do_ref[rows, :]
            qms = [(q_all * (hm * scale)).astype(MXU_DTYPE) for hm in hms]
            doms = [(do_all * hm).astype(MXU_DTYPE) for hm in hms]
            totals = [st_ref[hh, rows, :] for hh in range(per_pair)]

            def tile(kjs, masked, kc, los=None):
                heads = range(per_pair)
                los = los or [0] * len(kjs)
                pairs = [(t, h) for t in range(len(kjs)) for h in heads]
                add_rows = lambda full, lo, part: full + part if lo == 0 else jnp.concatenate(
                    [full[:lo], full[lo:] + part], axis=0)
                rsum = lambda a: jnp.sum(a, axis=-1, keepdims=True)
                dq, cls, gls = kc[0], list(kc[1:1 + per_pair]), list(kc[1 + per_pair:])
                s0s = [kj * SB_BLOCK if isinstance(kj, int) else pl.multiple_of(kj * SB_BLOCK, SB_BLOCK) for kj in kjs]
                k_alls = [k_ref[pl.ds(s0, SB_BLOCK), :] for s0 in s0s]
                v_alls = [v_ref[pl.ds(s0, SB_BLOCK), :] for s0 in s0s]
                kbs = [k_all.astype(MXU_DTYPE) for k_all in k_alls]
                vms = {(t, h): (v_alls[t] * hms[h]).astype(MXU_DTYPE) for t, h in pairs}
                kms = {(t, h): (k_alls[t] * (hms[h] * scale)).astype(MXU_DTYPE) for t, h in pairs}
                q_live = {(t, h): qms[h][los[t]:] for t, h in pairs}
                do_live = {(t, h): doms[h][los[t]:] for t, h in pairs}
                zs = {p: lax.dot_general(q_live[p], kbs[p[0]], _NT, preferred_element_type=F32) for p in pairs}
                das = {p: lax.dot_general(do_live[p], vms[p], _NT, preferred_element_type=F32) for p in pairs}
                masks = [(col_i[lo:] + s0) < (row_i[lo:] + r0) if m else None for m, lo, s0 in zip(masked, los, s0s)]
                keep = lambda t, a: a if masks[t] is None else jnp.where(masks[t], a, 0.0)
                sp_alls = {p: _softplus(zs[p]) for p in pairs}
                sps = {(t, h): keep(t, sp_alls[t, h]) for t, h in pairs}
                lefts = {p: _running_sum_mm(sps[p], lt) for p in pairs}
                cl = {}
                for t, h in pairs:
                    cl[t, h] = cls[h] if t == 0 else add_rows(cl[t - 1, h], los[t - 1], rsum(sps[t - 1, h]))
                ws = {(t, h): keep(t, jnp.exp(zs[t, h] - ((totals[h] - cl[t, h])[los[t]:] - lefts[t, h])))
                      for t, h in pairs}
                gs = {p: das[p] * ws[p] for p in pairs}
                g_sums = {p: _running_sum_mm(gs[p], le) for p in pairs}
                gl = {}
                for t, h in pairs:
                    gl[t, h] = gls[h] if t == 0 else add_rows(gl[t - 1, h], los[t - 1], rsum(gs[t - 1, h]))
                dzs = {(t, h): keep(t, gs[t, h] - jnp.exp(zs[t, h] - sp_alls[t, h]) * (gl[t, h][los[t]:] + g_sums[t, h])
                               ).astype(MXU_DTYPE) for t, h in pairs}
                for t in range(len(kjs)):
                    dk_t = jnp.zeros((SB_BLOCK, LANES), F32)
                    dv_t = jnp.zeros((SB_BLOCK, LANES), F32)
                    for h in heads:
                        dq = add_rows(dq, los[t], jnp.dot(dzs[t, h], kms[t, h], preferred_element_type=F32))
                        dk_t = dk_t + lax.dot_general(dzs[t, h], q_live[t, h], _TN, preferred_element_type=F32)
                        dv_t = dv_t + lax.dot_general(ws[t, h].astype(MXU_DTYPE), do_live[t, h], _TN,
                                                      preferred_element_type=F32)
                    d_ref[1, pl.ds(s0s[t], SB_BLOCK), :] += dk_t
                    d_ref[2, pl.ds(s0s[t], SB_BLOCK), :] += dv_t
                last = len(kjs) - 1
                cls = [add_rows(cl[last, h], los[last], rsum(sps[last, h])) for h in heads]
                gls = [add_rows(gl[last, h], los[last], rsum(gs[last, h])) for h in heads]
                return (dq, *cls, *gls)

            zero_col = jnp.zeros((SB_BQ, 1), F32)
            out = _sb_keys_ascending(qi, nrun_ref[pl.program_id(0), qi], tile,
                                     (jnp.zeros((SB_BQ, LANES), F32),) + (zero_col,) * (2 * per_pair), has_free)
            d_ref[0, rows, :] = out[0]

        q_block(0, False)
        lax.fori_loop(1, nq, lambda qi, carry: (q_block(qi, True), carry)[1], 0)

    col = lambda off: pl.BlockSpec((t_len, LANES), lambda p: (0, off + p))
    return pl.pallas_call(
        body, name="sb_bwd",
        grid=(n_pairs,),
        in_specs=[col(0), col(n_pairs), col(2 * n_pairs),
                  pl.BlockSpec((per_pair, t_len, 1), lambda p: (p, 0, 0)),
                  pl.BlockSpec(memory_space=pltpu.SMEM), col(0), _HBM],
        out_specs=pl.BlockSpec((3, t_len, LANES), lambda p: (DPROJ_SB_SLOT // 3, 0, p)),
        out_shape=jax.ShapeDtypeStruct(dproj.shape, dproj.dtype),
        input_output_aliases={6: 0},
        compiler_params=_params("arbitrary"),
    )(proj, proj, proj, sp_total, n_run_all, d_o, dproj)


def _conv_taps(xin, rows, t_len):
    taps = []
    for i in range(CONV_WIDTH):
        shift = CONV_WIDTH - 1 - i
        if shift == 0:
            taps.append(xin)
        else:
            taps.append(jnp.where(rows >= shift, pltpu.roll(xin, shift, axis=0), 0.0))
    return taps


def _gdn_prep_body_common(x_ref, w_ref, t_len):
    j = pl.program_id(0)
    xin = x_ref[...]
    rows = lax.broadcasted_iota(jnp.int32, (t_len, LANES), 0)
    taps = _conv_taps(xin, rows, t_len)
    pre = taps[0] * w_ref[0:1, :]
    for i in range(1, CONV_WIDTH):
        pre = pre + taps[i] * w_ref[i:i + 1, :]
    sg = _sigmoid(pre)
    act = pre * sg
    is_qk = j < 2 * GDN_HEADS
    nrm = jnp.where(is_qk, lax.rsqrt(jnp.sum(act * act, axis=-1, keepdims=True) + EPS), 1.0)
    sc = jnp.where(j < GDN_HEADS, float(GDN_HEAD_DIM) ** -0.5, 1.0)
    return j, rows, taps, pre, sg, act, is_qk, nrm, sc


def _gdn_prep_call(proj, conv_w, t_len):
    first = 2048 // LANES

    def body(x_ref, w_ref, out_ref):
        _, _, _, _, _, act, _, nrm, sc = _gdn_prep_body_common(x_ref, w_ref, t_len)
        out_ref[...] = act * nrm * sc

    return pl.pallas_call(
        body, name="gdn_prep",
        grid=(3 * GDN_HEADS,),
        in_specs=[pl.BlockSpec((t_len, LANES), lambda j: (0, first + j)),
                  pl.BlockSpec((CONV_WIDTH, LANES), lambda j: (0, j))],
        out_specs=pl.BlockSpec((t_len, LANES), lambda j: (0, j)),
        out_shape=jax.ShapeDtypeStruct((t_len, 3 * 512), F32),
        compiler_params=_params("arbitrary"),
    )(proj, conv_w)


def _gdn_prep_bwd_call(proj, conv_w, d_act3, dproj, t_len):
    first = 2048 // LANES

    def body(x_ref, w_ref, d_ref, dproj_in_ref, dx_ref, dw_ref):
        _, rows, taps, pre, sg, act, is_qk, nrm, sc = _gdn_prep_body_common(x_ref, w_ref, t_len)
        d_out = d_ref[0]
        dn = d_out * sc
        d_norm = nrm * dn - act * (nrm * nrm * nrm) * jnp.sum(dn * act, axis=-1, keepdims=True)
        d_act = jnp.where(is_qk, d_norm, d_out)
        d_pre = d_act * sg * (1.0 + pre * (1.0 - sg))
        dx = d_pre * w_ref[CONV_WIDTH - 1:CONV_WIDTH, :]
        dw_ref[CONV_WIDTH - 1:CONV_WIDTH, :] = jnp.sum(d_pre * taps[CONV_WIDTH - 1], axis=0, keepdims=True)
        for i in range(CONV_WIDTH - 1):
            shift = CONV_WIDTH - 1 - i
            up = jnp.where(rows < t_len - shift, pltpu.roll(d_pre, t_len - shift, axis=0), 0.0)
            dx = dx + up * w_ref[i:i + 1, :]
            dw_ref[i:i + 1, :] = jnp.sum(d_pre * taps[i], axis=0, keepdims=True)
        dx_ref[0] = dx

    return pl.pallas_call(
        body, name="gdn_prep_bwd",
        grid=(3 * GDN_HEADS,),
        in_specs=[pl.BlockSpec((t_len, LANES), lambda j: (0, first + j)),
                  pl.BlockSpec((CONV_WIDTH, LANES), lambda j: (0, j)),
                  pl.BlockSpec((1, t_len, LANES), lambda j: (j // GDN_HEADS, 0, j % GDN_HEADS)), _HBM],
        out_specs=[pl.BlockSpec((1, t_len, LANES), lambda j: (DPROJ_GDN_SLOT + j // GDN_HEADS, 0, j % GDN_HEADS)),
                   pl.BlockSpec((CONV_WIDTH, LANES), lambda j: (0, j))],
        out_shape=[jax.ShapeDtypeStruct(dproj.shape, dproj.dtype),
                   jax.ShapeDtypeStruct((CONV_WIDTH, 3 * 512), F32)],
        input_output_aliases={3: 0},
        compiler_params=_params("arbitrary"),
    )(proj, conv_w, d_act3, dproj)


def _chunk_cumsum_matrix():
    r = lax.broadcasted_iota(jnp.int32, (LANES, LANES), 0)
    c = lax.broadcasted_iota(jnp.int32, (LANES, LANES), 1)
    return ((r <= c) & ((r // CHUNK) == (c // CHUNK))).astype(F32)


def _gdn_gates_call(ps, pst, alog_l, dtb_l, alog_c, dtb_c, t_len):
    def body(ps_ref, pst_ref, al_ref, dl_ref, ac_ref, dc_ref, beta_ref, gcol_ref, grow_ref):
        upper = _chunk_cumsum_matrix()
        lower = upper.T
        psv = ps_ref[...]
        beta_ref[...] = _sigmoid(psv)
        g_l = -jnp.exp(al_ref[...]) * _softplus(psv + dl_ref[...])
        g_r = -jnp.exp(ac_ref[...]) * _softplus(pst_ref[...] + dc_ref[...])
        for w in range(t_len // LANES):
            sl = slice(w * LANES, (w + 1) * LANES)
            gcol_ref[sl, :] = _mx(lower, g_l[sl, :])
            grow_ref[:, sl] = _mx(g_r[:, sl], upper)

    vm = pl.BlockSpec(memory_space=pltpu.VMEM)
    return pl.pallas_call(
        body, name="gdn_gates",
        in_specs=[vm] * 6, out_specs=[vm] * 3,
        out_shape=[jax.ShapeDtypeStruct((t_len, LANES), F32),
                   jax.ShapeDtypeStruct((t_len, LANES), F32),
                   jax.ShapeDtypeStruct((8, t_len), F32)],
        compiler_params=pltpu.CompilerParams(vmem_limit_bytes=VMEM_LIMIT_BYTES),
    )(ps, pst, alog_l, dtb_l, alog_c, dtb_c)


def _gdn_gates_bwd_call(ps, alog_l, dtb_l, d_l, t_len):
    def body(ps_ref, al_ref, dl_ref, d_ref, dps_ref, gal_ref, gdt_ref):
        lane = lax.broadcasted_iota(jnp.int32, (1, LANES), 1)
        psv = ps_ref[...]
        dv = d_ref[...]
        beta = _sigmoid(psv)
        ea = jnp.exp(al_ref[...])
        arg = psv + dl_ref[...]
        g = -ea * _softplus(arg)
        d_a = dv * (-ea) * _sigmoid(arg)
        is_a = (lane >= GDN_HEADS) & (lane < 2 * GDN_HEADS)
        dps_ref[...] = jnp.where(lane < GDN_HEADS, dv * beta * (1.0 - beta), jnp.where(is_a, d_a, 0.0))
        gdt_ref[...] = jnp.where(is_a, jnp.sum(d_a, axis=0, keepdims=True), 0.0)
        gal_ref[...] = jnp.where(is_a, jnp.sum(dv * g, axis=0, keepdims=True), 0.0)

    vm = pl.BlockSpec(memory_space=pltpu.VMEM)
    return pl.pallas_call(
        body, name="gdn_gates_bwd",
        in_specs=[vm] * 4, out_specs=[vm] * 3,
        out_shape=[jax.ShapeDtypeStruct((t_len, LANES), F32),
                   jax.ShapeDtypeStruct((1, LANES), F32),
                   jax.ShapeDtypeStruct((1, LANES), F32)],
        compiler_params=pltpu.CompilerParams(vmem_limit_bytes=VMEM_LIMIT_BYTES),
    )(ps, alog_l, dtb_l, d_l)


def _bm(a, b):
    return _m3_general(a, b, _BNN)


def _bm_nt(a, b):
    return _m3_general(a, b, _BNT)


def _bm_tn(a, b):
    return _m3_general(a, b, _BTN)


def _heads_of(ref, rows):
    return jnp.stack([ref[rows, h * GDN_HEAD_DIM:(h + 1) * GDN_HEAD_DIM] for h in range(GDN_HEADS)])


def _chunk_terms(q_ref, k_ref, v_ref, b_ref, gc_ref, gr_ref, c, incl, strict, n=1):
    r0 = c * CHUNK if isinstance(c, int) else pl.multiple_of(c * CHUNK, CHUNK)
    rows = pl.ds(r0, n * CHUNK)
    per_chunk = lambda x: x.reshape(GDN_HEADS * n, CHUNK, x.shape[-1])
    q, k, v = (per_chunk(_heads_of(ref, rows)) for ref in (q_ref, k_ref, v_ref))
    lane_ids = lax.broadcasted_iota(jnp.int32, (1, LANES), 1)
    pick = lambda slab, first: jnp.stack([jnp.sum(jnp.where(lane_ids == first + h, slab, 0.0), axis=-1, keepdims=True)
                                          for h in range(GDN_HEADS)])
    b = per_chunk(pick(b_ref[rows, :], 0))
    gc = per_chunk(pick(gc_ref[rows, :], GDN_HEADS))
    gr = gr_ref[:, c] if n == 1 else gr_ref[:, c:c + n].reshape(GDN_HEADS * n, 1, CHUNK)
    dm = jnp.where(incl, jnp.exp(jnp.where(incl, gc - gr, 0.0)), 0.0)
    kb = k * b
    vb = v * b
    e = jnp.exp(gc)
    kk_qk = _bm_nt(jnp.concatenate([kb, q], axis=1), k)
    a = jnp.where(strict, kk_qk[:, :CHUNK] * dm, 0.0)
    p = jnp.where(incl, kk_qk[:, CHUNK:] * dm, 0.0)
    gl = gc[:, CHUNK - 1:CHUNK, :]
    eg = jnp.exp(gl - gc)
    return rows, q, k, v, b, gc, dm, kb, vb, e, a, p, gl, eg


def _unit_lower_inverse(a, eye):
    x = -a
    tm = eye + x
    xp = _bm(x, x)
    for _ in range(4):
        both = _bm(jnp.concatenate([xp, tm], axis=1), xp)
        tm = tm + both[:, CHUNK:]
        xp = both[:, :CHUNK]
    return tm + _bm(tm, xp)


def _gdn_specs(t_len, n_chunks, reverse):
    cps = GDN_CHUNKS_PER_STEP
    steps = n_chunks // cps
    at = (lambda g: steps - 1 - g) if reverse else (lambda g: g)
    rows_blk = lambda width, part=0: pl.BlockSpec((cps * CHUNK, width), lambda g: (at(g), part))
    gate_r = pl.BlockSpec((GDN_HEADS, cps, 1, CHUNK), lambda g: (0, at(g), 0, 0))
    per_chunk = lambda r, c: pl.BlockSpec((GDN_HEADS, cps, r, c), lambda g: (0, at(g), 0, 0))
    return cps, steps, rows_blk, gate_r, per_chunk


def _gdn_fwd_call(gact, beta_c, gam_c, gam_r, t_len):
    n_chunks = t_len // CHUNK
    dk = GDN_HEAD_DIM
    width = GDN_HEADS * dk
    cps, steps, rows_blk, gate_r, per_chunk = _gdn_specs(t_len, n_chunks, False)

    def body(q_ref, k_ref, v_ref, b_ref, gc_ref, gr_ref, o_ref, s_ref, t_ref, state_ref):
        row = lax.broadcasted_iota(jnp.int32, (CHUNK, CHUNK), 0)
        col = lax.broadcasted_iota(jnp.int32, (CHUNK, CHUNK), 1)
        incl, strict = row >= col, row > col
        eye = (row == col).astype(F32)

        @pl.when(pl.program_id(0) == 0)
        def _():
            state_ref[...] = jnp.zeros_like(state_ref)

        _, q, k, v, b, gc, dm, kb, vb, e, a, p, gl, eg = _chunk_terms(
            q_ref, k_ref, v_ref, b_ref, gc_ref, gr_ref, 0, incl, strict, cps)
        tm = _unit_lower_inverse(a, eye)
        uw = _bm(tm, jnp.concatenate([vb, kb * e], axis=2))
        w_qe = jnp.concatenate([uw[:, :, dk:], q * e], axis=1)
        u, kd, decay = uw[:, :, :dk], k * eg, jnp.exp(gl)
        t_ref[...] = tm.reshape(GDN_HEADS, cps, CHUNK, CHUNK)

        of_chunk = lambda x, c: jnp.stack([x[h * cps + c] for h in range(GDN_HEADS)])
        s = state_ref[...]
        for c in range(cps):
            ws_qs = _bm(of_chunk(w_qe, c), s)
            vn = of_chunk(u, c) - ws_qs[:, :CHUNK]
            o = ws_qs[:, CHUNK:] + _bm(of_chunk(p, c), vn)
            for h in range(GDN_HEADS):
                o_ref[c * CHUNK:(c + 1) * CHUNK, h * dk:(h + 1) * dk] = o[h]
            s_ref[:, c] = s
            s = s * of_chunk(decay, c) + _bm_tn(of_chunk(kd, c), vn)
        state_ref[...] = s

    return pl.pallas_call(
        body, name="gdn_fwd",
        grid=(steps,),
        in_specs=[rows_blk(width, 0), rows_blk(width, 1), rows_blk(width, 2), rows_blk(LANES), rows_blk(LANES), gate_r],
        out_specs=[rows_blk(width), per_chunk(dk, dk), per_chunk(CHUNK, CHUNK)],
        out_shape=[jax.ShapeDtypeStruct((t_len, width), F32),
                   jax.ShapeDtypeStruct((GDN_HEADS, n_chunks, dk, dk), F32),
                   jax.ShapeDtypeStruct((GDN_HEADS, n_chunks, CHUNK, CHUNK), F32)],
        scratch_shapes=[pltpu.VMEM((GDN_HEADS, dk, dk), F32)],
        compiler_params=_params("arbitrary"),
    )(gact, gact, gact, beta_c, gam_c, gam_r)


def _gdn_bwd_call(gact, beta_c, gam_c, gam_r, s_all, t_all, d_o, t_len, scatter=()):
    n_chunks = t_len // CHUNK
    dk = GDN_HEAD_DIM
    width = GDN_HEADS * dk
    cps, steps, rows_blk, gate_r, per_chunk = _gdn_specs(t_len, n_chunks, True)
    nx = len(scatter)

    def body(*refs):
        q_ref, k_ref, v_ref, b_ref, gc_ref, gr_ref, s_ref, t_ref, do_ref = refs[:9]
        d_ref, dgate_ref = refs[9 + nx:11 + nx]
        dstate_ref = refs[11 + 2 * nx]
        copies = lambda: _direct_copies(refs[9:9 + nx], refs[11 + nx:11 + 2 * nx], *refs[12 + 2 * nx:], (True,) * nx)
        if nx:
            pl.when(pl.program_id(0) == 0)(lambda: _start_all(copies()))
        row = lax.broadcasted_iota(jnp.int32, (CHUNK, CHUNK), 0)
        col = lax.broadcasted_iota(jnp.int32, (CHUNK, CHUNK), 1)
        incl, strict = row >= col, row > col
        ng = GDN_BWD_GROUP
        nb = GDN_HEADS * ng
        upper = jnp.broadcast_to((row <= col).astype(F32), (nb, CHUNK, CHUNK))
        ones = jnp.ones((nb, CHUNK, LANES), F32)
        last_row = lax.broadcasted_iota(jnp.int32, (CHUNK, 1), 0) == CHUNK - 1
        lane_ids = lax.broadcasted_iota(jnp.int32, (1, LANES), 1)
        rsum = lambda m: jnp.sum(m, axis=-1, keepdims=True)
        total = lambda m: jnp.sum(rsum(m), axis=1, keepdims=True)
        of_chunk = lambda x, c: jnp.stack([x[h * ng + c] for h in range(GDN_HEADS)])

        @pl.when(pl.program_id(0) == 0)
        def _():
            dstate_ref[...] = jnp.zeros_like(dstate_ref)

        for c0 in range(cps - ng, -1, -ng):
            group(c0, q_ref, k_ref, v_ref, b_ref, gc_ref, gr_ref, s_ref, t_ref, do_ref, d_ref, dgate_ref, dstate_ref,
                  incl, strict, upper, ones, last_row, lane_ids, rsum, total, of_chunk)
        if nx:
            pl.when(pl.program_id(0) == steps - 1)(lambda: _wait_all(copies()))

    def group(c0, q_ref, k_ref, v_ref, b_ref, gc_ref, gr_ref, s_ref, t_ref, do_ref, d_ref, dgate_ref, dstate_ref,
              incl, strict, upper, ones, last_row, lane_ids, rsum, total, of_chunk):
        ng = GDN_BWD_GROUP
        nb = GDN_HEADS * ng
        rows = pl.ds(c0 * CHUNK, ng * CHUNK)
        _, q, k, v, b, gc, dm, kb, vb, e, a, p, gl, eg = _chunk_terms(
            q_ref, k_ref, v_ref, b_ref, gc_ref, gr_ref, c0, incl, strict, ng)
        s = s_ref[:, c0:c0 + ng].reshape(nb, dk, dk)
        tm = t_ref[:, c0:c0 + ng].reshape(nb, CHUNK, CHUNK)
        d_out = _heads_of(do_ref, rows).reshape(nb, CHUNK, dk)
        el = jnp.exp(gl)
        kbe = kb * e
        qe = q * e
        kd = k * eg
        uw = _bm(tm, jnp.concatenate([vb, kbe], axis=2))
        u, w = uw[:, :, :dk], uw[:, :, dk:]
        vn = u - _bm(w, s)
        pt_do = _bm_tn(p, d_out)
        qet_do = _bm_tn(qe, d_out)

        ds = dstate_ref[...]
        d_vn_c, ds_c = [None] * ng, [None] * ng
        for c in range(ng - 1, -1, -1):
            ds_c[c] = ds
            d_vn_c[c] = of_chunk(pt_do, c) + _bm(of_chunk(kd, c), ds)
            ds = of_chunk(el, c) * ds + of_chunk(qet_do, c) - _bm_tn(of_chunk(w, c), d_vn_c[c])
        dstate_ref[...] = ds
        by_chunk = lambda xs: jnp.stack([xs[c][h] for h in range(GDN_HEADS) for c in range(ng)])
        d_vn, ds = by_chunk(d_vn_c), by_chunk(ds_c)

        on_s = _bm_nt(jnp.concatenate([d_out, d_vn], axis=1), s)
        d_qe, d_w = on_s[:, :CHUNK], -on_s[:, CHUNK:]
        d_p = jnp.where(incl, _bm_nt(d_out, vn), 0.0)
        d_kd = _bm_nt(vn, ds)
        d_both = _bm_tn(tm, jnp.concatenate([d_vn, d_w], axis=2))
        d_vb, d_kbe = d_both[:, :, :dk], d_both[:, :, dk:]
        d_a = -jnp.where(strict, _bm_nt(d_both, uw), 0.0)
        m = d_a * dm
        n = d_p * dm
        on_k = _bm(jnp.concatenate([m, n], axis=1), k)
        d_kb = on_k[:, :CHUNK] + d_kbe * e
        d_q = on_k[:, CHUNK:] + d_qe * e
        d_k = (_bm_tn(jnp.concatenate([m, n], axis=1), jnp.concatenate([kb, q], axis=1))
               + d_kd * eg + b * d_kb)
        d_v = b * d_vb
        r = d_a * a + d_p * p
        kd_term = rsum(d_kd * kd)
        d_gl = total(ds * s) * el + jnp.sum(kd_term, axis=1, keepdims=True)
        d_gam = (rsum(r) - _bm_tn(r, ones)[:, :, 0:1] + rsum(d_qe * qe) + rsum(d_kbe * kbe) - kd_term
                 + jnp.where(last_row, d_gl, 0.0))
        d_beta = rsum(d_kb * k) + rsum(d_vb * v)
        d_g = _bm(upper, d_gam * ones)[:, :, 0:1]
        per_head = lambda x: x.reshape(GDN_HEADS, ng * CHUNK, x.shape[-1])
        d_q, d_k, d_v, d_beta, d_g = (per_head(x) for x in (d_q, d_k, d_v, d_beta, d_g))
        gates = jnp.zeros((ng * CHUNK, LANES), F32)
        for h in range(GDN_HEADS):
            lanes = slice(h * dk, (h + 1) * dk)
            d_ref[0, rows, lanes] = d_q[h]
            d_ref[1, rows, lanes] = d_k[h]
            d_ref[2, rows, lanes] = d_v[h]
            gates = gates + (jnp.where(lane_ids == h, d_beta[h], 0.0)
                             + jnp.where(lane_ids == GDN_HEADS + h, d_g[h], 0.0))
        dgate_ref[rows, :] = gates

    d_spec = pl.BlockSpec((3, cps * CHUNK, width), lambda g: (0, steps - 1 - g, 0))
    return pl.pallas_call(
        body, name="gdn_bwd",
        grid=(steps,),
        in_specs=[rows_blk(width, 0), rows_blk(width, 1), rows_blk(width, 2), rows_blk(LANES), rows_blk(LANES), gate_r,
                  per_chunk(dk, dk), per_chunk(CHUNK, CHUNK), rows_blk(width)] + [_HBM] * nx,
        out_specs=[d_spec, rows_blk(LANES)] + [_HBM] * nx,
        out_shape=[jax.ShapeDtypeStruct((3, t_len, width), F32),
                   jax.ShapeDtypeStruct((t_len, LANES), F32)] + _direct_out_shapes(scatter, (True,) * nx),
        scratch_shapes=[pltpu.VMEM((GDN_HEADS, dk, dk), F32)] + (_direct_semaphores(nx) if nx else []),
        compiler_params=_params("arbitrary"),
    )(gact, gact, gact, beta_c, gam_c, gam_r, s_all, t_all, d_o, *scatter)


def _group_matrix(width, group):
    r = lax.broadcasted_iota(jnp.int32, (width, width), 0)
    c = lax.broadcasted_iota(jnp.int32, (width, width), 1)
    return ((r // group) == (c // group)).astype(F32)


def _post_call(o_sb, o_gd, proj, x, target, w_out, sbw, gdw, fw, tm=256):
    t_len, d = x.shape
    half = 512
    zsb_blk = 1536 // half
    zgd_blk = 3584 // half

    def body(osb_ref, ogd_ref, zsb_ref, zgd_ref, x_ref, tg_ref, wo_ref, sbw_ref, gdw_ref, fw_ref,
             dx2_ref, dosb_ref, dogd_ref, dz_ref, loss_ref, gfw_ref, gsb_ref, ggd_ref, gwo_ref):
        step = pl.program_id(0)

        @pl.when(step == 0)
        def _():
            loss_ref[...] = jnp.zeros_like(loss_ref)
            gfw_ref[...] = jnp.zeros_like(gfw_ref)
            gsb_ref[...] = jnp.zeros_like(gsb_ref)
            ggd_ref[...] = jnp.zeros_like(ggd_ref)
            gwo_ref[...] = jnp.zeros_like(gwo_ref)

        def head_forward(o, z, w, gmat, inv):
            r = lax.rsqrt(_running_sum_mm(o * o, gmat) * inv + EPS)
            nrm = o * r * w
            sg = _sigmoid(z)
            return r, nrm, sg, nrm * (z * sg)

        def head_backward(d_m, o, z, w, gmat, inv, r, nrm, sg):
            d_n = d_m * (z * sg)
            d_z = d_m * nrm * (sg * (1.0 + z * (1.0 - sg)))
            dnw = d_n * w
            d_o = r * dnw - o * (r * r * r) * (_running_sum_mm(dnw * o, gmat) * inv)
            return d_o, d_z, jnp.sum(d_n * o * r, axis=0, keepdims=True)

        g_sb = _group_matrix(half, SB_HEAD_DIM).astype(MXU_DTYPE)
        g_gd = _group_matrix(half, GDN_HEAD_DIM).astype(MXU_DTYPE)
        osb, ogd, zsb, zgd = osb_ref[...], ogd_ref[...], zsb_ref[...], zgd_ref[...]
        sbw_v, gdw_v = sbw_ref[...], gdw_ref[...]
        r_sb, n_sb, sg_sb, m_sb = head_forward(osb, zsb, sbw_v, g_sb, 1.0 / SB_HEAD_DIM)
        r_gd, n_gd, sg_gd, m_gd = head_forward(ogd, zgd, gdw_v, g_gd, 1.0 / GDN_HEAD_DIM)
        mixed = jnp.concatenate([m_sb, m_gd], axis=1).astype(MXU_DTYPE)
        wo = wo_ref[...]
        x2 = x_ref[...] + jnp.dot(mixed, wo, preferred_element_type=F32)
        r2 = lax.rsqrt(jnp.mean(x2 * x2, axis=-1, keepdims=True) + EPS)
        fw_v = fw_ref[...]
        err = x2 * r2 * fw_v - tg_ref[...]
        loss_ref[...] += 0.5 * jnp.sum(jnp.sum(err * err, axis=-1, keepdims=True) * (1.0 / d))
        dy = err * (1.0 / d)
        gg = dy * fw_v
        dx2 = r2 * gg - x2 * ((r2 * r2 * r2) * jnp.mean(gg * x2, axis=-1, keepdims=True))
        gfw_ref[...] += jnp.sum(dy * x2 * r2, axis=0, keepdims=True)
        dx2_ref[...] = dx2
        dx2b = dx2.astype(MXU_DTYPE)
        d_mixed = lax.dot_general(dx2b, wo, _NT, preferred_element_type=F32)
        gwo_ref[...] += lax.dot_general(mixed, dx2b, _TN, preferred_element_type=F32)
        d_osb, d_zsb, gsb = head_backward(d_mixed[:, :half], osb, zsb, sbw_v, g_sb, 1.0 / SB_HEAD_DIM, r_sb, n_sb, sg_sb)
        d_ogd, d_zgd, ggd = head_backward(d_mixed[:, half:], ogd, zgd, gdw_v, g_gd, 1.0 / GDN_HEAD_DIM, r_gd, n_gd, sg_gd)
        dosb_ref[...] = d_osb
        dogd_ref[...] = d_ogd
        dz_ref[0] = d_zsb
        dz_ref[1] = d_zgd
        gsb_ref[...] += gsb
        ggd_ref[...] += ggd

    row_blk = lambda w: pl.BlockSpec((tm, w), lambda i: (i, 0))
    fixed = lambda r, w: pl.BlockSpec((r, w), lambda i: (0, 0))
    return pl.pallas_call(
        body, name="post",
        grid=(t_len // tm,),
        in_specs=[row_blk(half), row_blk(half),
                  pl.BlockSpec((tm, half), lambda i: (i, zsb_blk)),
                  pl.BlockSpec((tm, half), lambda i: (i, zgd_blk)),
                  row_blk(d), row_blk(d), fixed(d, d), fixed(1, half), fixed(1, half), fixed(1, d)],
        out_specs=[row_blk(d), row_blk(half), row_blk(half),
                   pl.BlockSpec((2, tm, half), lambda i: (DPROJ_GATE_SLOT // 2, i, 0)),
                   fixed(1, LANES), fixed(1, d), fixed(1, half), fixed(1, half), fixed(d, d)],
        out_shape=[jax.ShapeDtypeStruct((t_len, d), F32)] + [jax.ShapeDtypeStruct((t_len, half), F32)] * 2
                  + [jax.ShapeDtypeStruct((len(DPROJ_PIECE_OF_SLOT), t_len, half), F32),
                     jax.ShapeDtypeStruct((1, LANES), F32), jax.ShapeDtypeStruct((1, d), F32),
                     jax.ShapeDtypeStruct((1, half), F32), jax.ShapeDtypeStruct((1, half), F32),
                     jax.ShapeDtypeStruct((d, d), F32)],
        compiler_params=_params("arbitrary"),
    )(o_sb, o_gd, proj, proj, x, target, w_out, sbw, gdw, fw)


def _piece_of_slot(s):
    return jnp.where(s < DPROJ_GDN_SLOT, s, jnp.where(s < DPROJ_GATE_SLOT, s + 1,
                                                     jnp.where(s == DPROJ_GATE_SLOT, 3, 7)))


def _gw_in_call(h_t, dproj8):
    d, t_len = h_t.shape
    n_piece, _, pw = dproj8.shape

    def body(ht_ref, dp_ref, gw_ref):
        gw_ref[...] = jnp.dot(ht_ref[...], dp_ref[0].astype(MXU_DTYPE), preferred_element_type=F32)

    return pl.pallas_call(
        body, name="gw_in",
        grid=(n_piece,),
        in_specs=[pl.BlockSpec((d, t_len), lambda s: (0, 0)),
                  pl.BlockSpec((1, t_len, pw), lambda s: (s, 0, 0))],
        out_specs=pl.BlockSpec((d, pw), lambda s: (0, _piece_of_slot(s))),
        out_shape=jax.ShapeDtypeStruct((d, n_piece * pw), F32),
        compiler_params=_params("arbitrary"),
    )(h_t, dproj8)


def _gw_small_call(h_t, dsmall, tm=512):
    d, t_len = h_t.shape
    ns = dsmall.shape[1]

    def body(ht_ref, dp_ref, gw_ref):
        @pl.when(pl.program_id(0) == 0)
        def _():
            gw_ref[...] = jnp.zeros_like(gw_ref)

        gw_ref[...] += jnp.dot(ht_ref[...], dp_ref[...].astype(MXU_DTYPE), preferred_element_type=F32)

    return pl.pallas_call(
        body, name="gw_small",
        grid=(t_len // tm,),
        in_specs=[pl.BlockSpec((d, tm), lambda t: (0, t)),
                  pl.BlockSpec((tm, ns), lambda t: (t, 0))],
        out_specs=pl.BlockSpec((d, ns), lambda t: (0, 0)),
        out_shape=jax.ShapeDtypeStruct((d, ns), F32),
        compiler_params=_params("arbitrary"),
    )(h_t, dsmall)


def _dx_call(dproj8, dsmall, w_main, w_small, x, r, dx2, norm_w, tm=256):
    t_len, d = x.shape
    n_piece, _, pw = dproj8.shape
    ns = dsmall.shape[1]

    def body(dp_ref, ds_ref, wm_ref, ws_ref, x_ref, r_ref, dx2_ref, nw_ref, gx_ref, gnw_ref):
        @pl.when(pl.program_id(0) == 0)
        def _():
            gnw_ref[...] = jnp.zeros_like(gnw_ref)

        dh = lax.dot_general(ds_ref[...].astype(MXU_DTYPE), ws_ref[...], _NT, preferred_element_type=F32)
        for s, p in enumerate(DPROJ_PIECE_OF_SLOT):
            dh = dh + lax.dot_general(dp_ref[s].astype(MXU_DTYPE), wm_ref[:, p * pw:(p + 1) * pw], _NT,
                                      preferred_element_type=F32)
        xv, rv = x_ref[...], r_ref[...]
        dn = dh * nw_ref[...]
        gx_ref[...] = dx2_ref[...] + rv * dn - xv * ((rv * rv * rv) * jnp.mean(dn * xv, axis=-1, keepdims=True))
        gnw_ref[...] += jnp.sum(dh * xv * rv, axis=0, keepdims=True)

    return pl.pallas_call(
        body, name="dx",
        grid=(t_len // tm,),
        in_specs=[pl.BlockSpec((n_piece, tm, pw), lambda i: (0, i, 0)),
                  pl.BlockSpec((tm, ns), lambda i: (i, 0)),
                  pl.BlockSpec((d, n_piece * pw), lambda i: (0, 0)),
                  pl.BlockSpec((d, ns), lambda i: (0, 0)),
                  pl.BlockSpec((tm, d), lambda i: (i, 0)),
                  pl.BlockSpec((tm, 1), lambda i: (i, 0)),
                  pl.BlockSpec((tm, d), lambda i: (i, 0)),
                  pl.BlockSpec((1, d), lambda i: (0, 0))],
        out_specs=[pl.BlockSpec((tm, d), lambda i: (i, 0)),
                   pl.BlockSpec((1, d), lambda i: (0, 0))],
        out_shape=[jax.ShapeDtypeStruct((t_len, d), F32), jax.ShapeDtypeStruct((1, d), F32)],
        compiler_params=_params("arbitrary"),
    )(dproj8, dsmall, w_main, w_small, x, r, dx2, norm_w)


def _exchange_call(name, srcs, per_peer):
    n = len(srcs)

    def body(*refs):
        src_refs, out_refs = refs[:n], refs[n:2 * n]
        copies = _direct_copies(src_refs, out_refs, *refs[2 * n:], per_peer)
        _start_all(copies)
        _wait_all(copies)

    hbm = pl.BlockSpec(memory_space=pl.ANY)
    return pl.pallas_call(
        body, name=name,
        in_specs=[hbm] * n, out_specs=[hbm] * n, out_shape=_direct_out_shapes(srcs, per_peer),
        scratch_shapes=_direct_semaphores(n),
    )(*srcs)


def _direct_out_shapes(srcs, per_peer):
    return [jax.ShapeDtypeStruct(s.shape if pp else (N_DEV,) + s.shape, s.dtype) for s, pp in zip(srcs, per_peer)]


def _direct_semaphores(n):
    return [pltpu.SemaphoreType.DMA((n * (N_DEV - 1),)), pltpu.SemaphoreType.DMA((n * (N_DEV - 1),)),
            pltpu.SemaphoreType.DMA((n,))]


def _direct_copies(src_refs, out_refs, send_sems, recv_sems, local_sems, per_peer):
    x, y, c = lax.axis_index("x"), lax.axis_index("y"), lax.axis_index("c")
    me = 4 * x + 2 * y + c
    local, remote = [], []
    for a in range(len(src_refs)):
        mine = src_refs[a].at[me] if per_peer[a] else src_refs[a]
        local.append(pltpu.make_async_copy(mine, out_refs[a].at[me], local_sems.at[a]))
    for k in range(1, N_DEV):
        kx, ky, kc = (k >> 2) & 1, (k >> 1) & 1, k & 1
        px = 1 - x if kx else x
        py = 1 - y if ky else y
        pc = 1 - c if kc else c
        peer = 4 * px + 2 * py + pc
        for a in range(len(src_refs)):
            sem = a * (N_DEV - 1) + (k - 1)
            remote.append(pltpu.make_async_remote_copy(
                src_ref=src_refs[a].at[peer] if per_peer[a] else src_refs[a], dst_ref=out_refs[a].at[me],
                send_sem=send_sems.at[sem], recv_sem=recv_sems.at[sem],
                device_id=(px, py, pc), device_id_type=pl.DeviceIdType.MESH))
    return local, remote


def _start_all(copies):
    local, remote = copies
    for cp in local + remote:
        cp.start()


def _wait_all(copies):
    local, remote = copies
    for cp in remote:
        cp.wait_send()
    for cp in remote:
        cp.wait_recv()
    for cp in local:
        cp.wait()


N_CHIPS = 4
_HBM = pl.BlockSpec(memory_space=pl.ANY)
_MESH = pl.DeviceIdType.MESH


def _gather_call(name, srcs):
    n = len(srcs)
    per = N_DEV - 1

    def body(*refs):
        src_refs, out_refs = refs[:n], refs[n:2 * n]
        send_sems, recv_sems, local_sems = refs[2 * n:]
        x, y, c = lax.axis_index("x"), lax.axis_index("y"), lax.axis_index("c")
        me, sibling = (x, y, c), (x, y, 1 - c)
        chips = [(1 - x, y), (x, 1 - y), (1 - x, 1 - y)]
        slot = lambda px, py, pc: 4 * px + 2 * py + pc

        def copy(a, k, block, to, from_src=False):
            rows = out_refs[a].at[slot(*block)]
            return pltpu.make_async_remote_copy(
                src_ref=src_refs[a] if from_src else rows, dst_ref=rows,
                send_sem=send_sems.at[a * per + k], recv_sem=recv_sems.at[a * per + k],
                device_id=to, device_id_type=_MESH)

        local = [pltpu.make_async_copy(src_refs[a], out_refs[a].at[slot(*me)], local_sems.at[a]) for a in range(n)]
        for cp in local:
            cp.start()
        first = []
        for a in range(n):
            first.append(copy(a, 0, me, sibling, True))
            first += [copy(a, 1 + j, me, (*chip, c), True) for j, chip in enumerate(chips)]
        for cp in first:
            cp.start()
        passed = []
        for j, chip in enumerate(chips):
            for a in range(n):
                copy(a, 1 + j, (*chip, c), me).wait_recv()
                fwd = copy(a, 4 + j, (*chip, c), sibling)
                fwd.start()
                passed.append(fwd)
        for a in range(n):
            copy(a, 0, sibling, me).wait_recv()
            for j, chip in enumerate(chips):
                copy(a, 4 + j, (*chip, 1 - c), me).wait_recv()
        for cp in first + passed:
            cp.wait_send()
        for cp in local:
            cp.wait()

    return pl.pallas_call(
        body, name=name,
        in_specs=[_HBM] * n, out_specs=[_HBM] * n,
        out_shape=[jax.ShapeDtypeStruct((N_DEV,) + s.shape, s.dtype) for s in srcs],
        scratch_shapes=[pltpu.SemaphoreType.DMA((n * per,)), pltpu.SemaphoreType.DMA((n * per,)),
                        pltpu.SemaphoreType.DMA((n,))],
    )(*srcs)


def _sibling_send_call(name, srcs):
    n = len(srcs)

    def body(*refs):
        src_refs, out_refs = refs[:n], refs[n:2 * n]
        send_sems, recv_sems = refs[2 * n:]
        x, y, c = lax.axis_index("x"), lax.axis_index("y"), lax.axis_index("c")
        copies = []
        for a in range(n):
            for ch in range(N_CHIPS):
                copies.append(pltpu.make_async_remote_copy(
                    src_ref=src_refs[a].at[2 * ch + (1 - c)], dst_ref=out_refs[a].at[ch],
                    send_sem=send_sems.at[a * N_CHIPS + ch], recv_sem=recv_sems.at[a * N_CHIPS + ch],
                    device_id=(x, y, 1 - c), device_id_type=_MESH))
        for cp in copies:
            cp.start()
        for cp in copies:
            cp.wait_send()
        for cp in copies:
            cp.wait_recv()

    return pl.pallas_call(
        body, name=name,
        in_specs=[_HBM] * n, out_specs=[_HBM] * n,
        out_shape=[jax.ShapeDtypeStruct((N_CHIPS,) + s.shape[1:], s.dtype) for s in srcs],
        scratch_shapes=[pltpu.SemaphoreType.DMA((n * N_CHIPS,)), pltpu.SemaphoreType.DMA((n * N_CHIPS,))],
    )(*srcs)


def _pair_sum_call(name, parts, from_sibling, tr):
    _, rows, cols = parts.shape

    def body(p_ref, s_ref, o_ref):
        o_ref[...] = (p_ref[...].astype(F32) + s_ref[...].astype(F32)).astype(o_ref.dtype)

    return pl.pallas_call(
        body, name=name,
        grid=(N_CHIPS, rows // tr),
        in_specs=[pl.BlockSpec((1, tr, cols), lambda ch, i: (2 * ch + lax.axis_index("c"), i, 0)),
                  pl.BlockSpec((1, tr, cols), lambda ch, i: (ch, i, 0))],
        out_specs=pl.BlockSpec((1, tr, cols), lambda ch, i: (ch, i, 0)),
        out_shape=jax.ShapeDtypeStruct((N_CHIPS, rows, cols), WIRE_DTYPE),
        compiler_params=_params("arbitrary", "arbitrary"),
    )(parts, from_sibling)


def _chip_exchange_call(name, srcs):
    n = len(srcs)
    per = N_CHIPS - 1

    def body(*refs):
        src_refs, out_refs = refs[:n], refs[n:2 * n]
        send_sems, recv_sems, local_sems = refs[2 * n:]
        x, y, c = lax.axis_index("x"), lax.axis_index("y"), lax.axis_index("c")
        mine = 2 * x + y
        chips = [(1 - x, y), (x, 1 - y), (1 - x, 1 - y)]
        local = [pltpu.make_async_copy(src_refs[a].at[mine], out_refs[a].at[mine], local_sems.at[a]) for a in range(n)]
        for cp in local:
            cp.start()
        remote = []
        for a in range(n):
            for j, (px, py) in enumerate(chips):
                remote.append(pltpu.make_async_remote_copy(
                    src_ref=src_refs[a].at[2 * px + py], dst_ref=out_refs[a].at[mine],
                    send_sem=send_sems.at[a * per + j], recv_sem=recv_sems.at[a * per + j],
                    device_id=(px, py, c), device_id_type=_MESH))
        for cp in remote:
            cp.start()
        for cp in remote:
            cp.wait_send()
        for cp in remote:
            cp.wait_recv()
        for cp in local:
            cp.wait()

    return pl.pallas_call(
        body, name=name,
        in_specs=[_HBM] * n, out_specs=[_HBM] * n,
        out_shape=[jax.ShapeDtypeStruct(s.shape, s.dtype) for s in srcs],
        scratch_shapes=[pltpu.SemaphoreType.DMA((n * per,)), pltpu.SemaphoreType.DMA((n * per,)),
                        pltpu.SemaphoreType.DMA((n,))],
    )(*srcs)


def _adam_call(name, parts, w, m, v, tr):
    rows, cols = w.shape
    n_slots = parts.shape[0]

    def body(p_ref, w_ref, m_ref, v_ref, g_ref, d_ref, nm_ref, nv_ref):
        g = p_ref[0].astype(F32)
        for s in range(1, n_slots):
            g = g + p_ref[s].astype(F32)
        m_new = ADAM_B1 * m_ref[...] + (1.0 - ADAM_B1) * g
        v_new = ADAM_B2 * v_ref[...] + (1.0 - ADAM_B2) * (g * g)
        m_hat = m_new / (1.0 - ADAM_B1 ** ADAM_STEP)
        v_hat = v_new / (1.0 - ADAM_B2 ** ADAM_STEP)
        g_ref[...] = g
        d_ref[...] = -ADAM_LR * (m_hat / (jnp.sqrt(v_hat) + ADAM_EPS) + ADAM_WD * w_ref[...])
        nm_ref[...] = m_new
        nv_ref[...] = v_new

    blk = pl.BlockSpec((tr, cols), lambda i: (i, 0))
    return pl.pallas_call(
        body, name=name,
        grid=(rows // tr,),
        in_specs=[pl.BlockSpec((n_slots, tr, cols), lambda i: (0, i, 0)), blk, blk, blk],
        out_specs=[blk] * 4,
        out_shape=[jax.ShapeDtypeStruct((rows, cols), F32)] * 4,
        compiler_params=_params("arbitrary"),
    )(parts, w, m, v)


N_PIECES = 8
PIECE = 512
SHARD_COLS = 513
SHARD_PAD = 640
RELAYOUT_ROWS = 256


def _from_shards_call(shards):
    _, d, _ = shards.shape
    tr = RELAYOUT_ROWS

    def body(p_ref, m_ref, s_ref):
        lane = lax.broadcasted_iota(jnp.int32, (tr, SHARD_PAD), 1)
        pad = jnp.zeros((tr, SHARD_PAD - SHARD_COLS), F32)
        sh = [jnp.concatenate([p_ref[s].astype(F32), pad], axis=1) for s in range(N_DEV)]
        for p in range(N_PIECES):
            y = sh[p] if p == 0 else pltpu.roll(sh[p], p, axis=1)
            if p > 0:
                y = jnp.where(lane < p, pltpu.roll(sh[p - 1], SHARD_PAD - (SHARD_COLS - p), axis=1), y)
            m_ref[:, p * PIECE:(p + 1) * PIECE] = y[:, :PIECE].astype(m_ref.dtype)
        first_gate = N_PIECES * PIECE - (N_DEV - 1) * SHARD_COLS
        s_ref[...] = pltpu.roll(sh[N_DEV - 1], SHARD_PAD - first_gate, axis=1)[:, :LANES].astype(s_ref.dtype)

    return pl.pallas_call(
        body, name="w_in_from_shards",
        grid=(d // tr,),
        in_specs=[pl.BlockSpec((N_DEV, tr, SHARD_COLS), lambda i: (0, i, 0))],
        out_specs=[pl.BlockSpec((tr, N_PIECES * PIECE), lambda i: (i, 0)), pl.BlockSpec((tr, LANES), lambda i: (i, 0))],
        out_shape=[jax.ShapeDtypeStruct((d, N_PIECES * PIECE), shards.dtype),
                   jax.ShapeDtypeStruct((d, LANES), shards.dtype)],
        compiler_params=_params("arbitrary"),
    )(shards)


def _to_shards_call(main, gates, out_dtype):
    d = main.shape[0]
    tr = RELAYOUT_ROWS

    def body(m_ref, s_ref, o_ref):
        for s in range(N_DEV):
            if s < N_DEV - 1:
                x = m_ref[:, s * PIECE:s * PIECE + SHARD_PAD]
            else:
                x = jnp.concatenate([m_ref[:, s * PIECE:(s + 1) * PIECE], s_ref[...]], axis=1)
            y = x if s == 0 else pltpu.roll(x, SHARD_PAD - s, axis=1)
            o_ref[s] = y[:, :SHARD_COLS].astype(out_dtype)

    return pl.pallas_call(
        body, name="w_in_to_shards",
        grid=(d // tr,),
        in_specs=[pl.BlockSpec((tr, N_PIECES * PIECE), lambda i: (i, 0)), pl.BlockSpec((tr, LANES), lambda i: (i, 0))],
        out_specs=pl.BlockSpec((N_DEV, tr, SHARD_COLS), lambda i: (0, i, 0)),
        out_shape=jax.ShapeDtypeStruct((N_DEV, d, SHARD_COLS), out_dtype),
        compiler_params=_params("arbitrary"),
    )(main, gates)


def _adamw(g, w, m, v):
    m_new = ADAM_B1 * m + (1.0 - ADAM_B1) * g
    v_new = ADAM_B2 * v + (1.0 - ADAM_B2) * (g * g)
    m_hat = m_new / (1.0 - ADAM_B1 ** ADAM_STEP)
    v_hat = v_new / (1.0 - ADAM_B2 ** ADAM_STEP)
    return -ADAM_LR * (m_hat / (jnp.sqrt(v_hat) + ADAM_EPS) + ADAM_WD * w), m_new, v_new


def _adam_small_call(parts, ws, ms, vs):
    n = len(ws)
    n_slots = parts.shape[0]

    def body(*refs):
        p_ref = refs[0]
        w_refs, m_refs, v_refs = refs[1:1 + n], refs[1 + n:1 + 2 * n], refs[1 + 2 * n:1 + 3 * n]
        loss_ref = refs[1 + 3 * n]
        outs = refs[2 + 3 * n:]
        g_all = p_ref[0]
        for s in range(1, n_slots):
            g_all = g_all + p_ref[s]
        loss_ref[...] = g_all[n:n + 1, 0:1]
        for r in range(n):
            size = w_refs[r].shape[1]
            g = g_all[r:r + 1, :size]
            delta, m_new, v_new = _adamw(g, w_refs[r][...], m_refs[r][...], v_refs[r][...])
            for kind, val in enumerate((g, delta, m_new, v_new)):
                outs[kind * n + r][...] = val

    vm = pl.BlockSpec(memory_space=pltpu.VMEM)
    shapes = [jax.ShapeDtypeStruct(w.shape, F32) for w in ws]
    return pl.pallas_call(
        body, name="adam_small",
        in_specs=[vm] * (1 + 3 * n), out_specs=[vm] * (1 + 4 * n),
        out_shape=[jax.ShapeDtypeStruct((1, 1), F32)] + shapes * 4,
    )(parts, *ws, *ms, *vs)


_SMALL_ROWS = ("norm1_w", "final_norm_w", "sb_norm_w", "gdn_norm_w", "gdn_A_log", "gdn_dt_bias", "loss")


def _pack_small(vals, width):
    rows = [jnp.pad(a.reshape(1, -1).astype(F32), ((0, 0), (0, width - a.size))) for a in vals]
    rows += [jnp.zeros((1, width), F32)] * (8 - len(rows))
    return jnp.concatenate(rows, axis=0)


def _device_step(x2d, tgt, w_main, w_small, w_out_full, conv_full, norm1_w, sb_norm_w, gdn_A_log, gdn_dt_bias,
                 gdn_norm_w, final_norm_w, distributed=False):
    t_len, d = x2d.shape
    n_chunks = t_len // CHUNK
    w_main, w_small, w_out_full = (a.astype(MXU_DTYPE) for a in (w_main, w_small, w_out_full))
    w_small_t = w_small[:, :2 * GDN_HEADS].T

    pad_lanes = lambda a, lo: jnp.pad(a.reshape(1, -1), ((0, 0), (lo, LANES - lo - a.size)))
    alog_l, dtb_l = pad_lanes(gdn_A_log, GDN_HEADS), pad_lanes(gdn_dt_bias, GDN_HEADS)
    alog_c, dtb_c = alog_l[:, :8].T, dtb_l[:, :8].T
    sbw = jnp.tile(sb_norm_w, (1, 512 // SB_HEAD_DIM))
    gdw = jnp.tile(gdn_norm_w, (1, 512 // GDN_HEAD_DIM))
    fw = final_norm_w.reshape(1, d)

    if distributed:
        proj, ps, pst, h_t, r1, w_out_g, conv_g = _inproj_call(
            x2d, norm1_w, w_main, w_small, w_small_t, gather=(w_out_full, conv_full))
        w_out_full = w_out_g.reshape(d, d)
        conv_full = conv_g.transpose(1, 0, 2).reshape(CONV_WIDTH, N_DEV * conv_g.shape[2])
    else:
        proj, ps, pst, h_t, r1 = _inproj_call(x2d, norm1_w, w_main, w_small, w_small_t)
    o_sb, sp_total, sb_blocks_run = _sb_fwd_call(proj, t_len)
    gact = _gdn_prep_call(proj, conv_full, t_len)
    beta_l, gcol_l, grow = _gdn_gates_call(ps, pst, alog_l, dtb_l, alog_c, dtb_c, t_len)
    gam_r = grow[GDN_HEADS:2 * GDN_HEADS].reshape(GDN_HEADS, n_chunks, 1, CHUNK)
    o_gd, s_all, t_all = _gdn_fwd_call(gact, beta_l, gcol_l, gam_r, t_len)

    (dx2, d_osb, d_ogd, dproj8, loss_p, g_fw, g_sbw, g_gdw, g_wout) = _post_call(
        o_sb, o_gd, proj, x2d, tgt, w_out_full, sbw, gdw, fw)

    dproj8 = _sb_bwd_call(proj, sp_total, sb_blocks_run, d_osb, dproj8, t_len)
    if distributed:
        d_gact3, d_gates, g_wout = _gdn_bwd_call(gact, beta_l, gcol_l, gam_r, s_all, t_all, d_ogd, t_len,
                                                 scatter=(g_wout.reshape(N_DEV, d // N_DEV, d),))
    else:
        d_gact3, d_gates = _gdn_bwd_call(gact, beta_l, gcol_l, gam_r, s_all, t_all, d_ogd, t_len)
    dproj8, g_conv = _gdn_prep_bwd_call(proj, conv_full, d_gact3, dproj8, t_len)
    dsmall, g_alog, g_dtb = _gdn_gates_bwd_call(ps, alog_l, dtb_l, d_gates, t_len)

    g_w_main = _gw_in_call(h_t, dproj8)
    g_w_small = _gw_small_call(h_t, dsmall)
    grad_x, g_n1 = _dx_call(dproj8, dsmall, w_main, w_small, x2d, r1, dx2, norm1_w)
    return (loss_p, grad_x, g_n1, (g_w_main, g_w_small), g_sbw, g_conv, g_alog, g_dtb, g_gdw, g_wout, g_fw)


def kernel(x, norm1_w, w_in, sb_norm_w, gdn_conv_w, gdn_A_log, gdn_dt_bias, gdn_norm_w, w_out, final_norm_w, loss_target, m_norm1_w, m_w_in, m_sb_norm_w, m_gdn_conv_w, m_gdn_A_log, m_gdn_dt_bias, m_gdn_norm_w, m_w_out, m_final_norm_w, v_norm1_w, v_w_in, v_sb_norm_w, v_gdn_conv_w, v_gdn_A_log, v_gdn_dt_bias, v_gdn_norm_w, v_w_out, v_final_norm_w):
    d = x.shape[2]
    shard_cols = w_in.shape[2]
    conv_cols = gdn_conv_w.shape[2]

    (w_in_g,) = _gather_call("gather_weights", [w_in[0].astype(WIRE_DTYPE)])
    w_main, w_small = _from_shards_call(w_in_g)

    (loss_p, grad_x, g_n1, (g_w_main, g_w_small), g_sbw, g_conv, g_alog, g_dtb, g_gdw, p_wout, g_fw) = _device_step(
        x[0], loss_target[0], w_main, w_small, w_out[0].astype(WIRE_DTYPE), gdn_conv_w[0], norm1_w, sb_norm_w,
        gdn_A_log, gdn_dt_bias, gdn_norm_w, final_norm_w, distributed=True)

    g_w_in_parts = _to_shards_call(g_w_main, g_w_small, WIRE_DTYPE)
    g_conv_parts = g_conv.reshape(CONV_WIDTH, N_DEV, conv_cols).transpose(1, 0, 2)
    fold = lambda a, group: a.reshape(-1, group).sum(axis=0)
    small_g = _pack_small([g_n1, g_fw, fold(g_sbw, SB_HEAD_DIM), fold(g_gdw, GDN_HEAD_DIM),
                           g_alog[0, GDN_HEADS:2 * GDN_HEADS], g_dtb[0, GDN_HEADS:2 * GDN_HEADS],
                           loss_p[0, :1]], d)
    (sib_w_in,) = _sibling_send_call("grads_to_sibling", [g_w_in_parts])
    c_w_in = _pair_sum_call("pair_sum_w_in", g_w_in_parts, sib_w_in, 256)
    (p_w_in,) = _chip_exchange_call("grads_to_chips", [c_w_in])
    p_small, p_conv = _exchange_call("exchange_small", [small_g, g_conv_parts], [False, True])

    r_w_in = _adam_call("adam_w_in", p_w_in, w_in[0], m_w_in[0], v_w_in[0], 256)
    r_wout = _adam_call("adam_w_out", p_wout, w_out[0], m_w_out[0], v_w_out[0], d // N_DEV)
    r_conv = _adam_call("adam_conv", p_conv, gdn_conv_w[0], m_gdn_conv_w[0], v_gdn_conv_w[0], CONV_WIDTH)

    row = lambda a: a.reshape(1, -1)
    n_small = len(_SMALL_ROWS) - 1
    r_small = _adam_small_call(
        p_small,
        [norm1_w, row(final_norm_w), sb_norm_w, gdn_norm_w, gdn_A_log, gdn_dt_bias],
        [m_norm1_w, row(m_final_norm_w), m_sb_norm_w, m_gdn_norm_w, m_gdn_A_log, m_gdn_dt_bias],
        [v_norm1_w, row(v_final_norm_w), v_sb_norm_w, v_gdn_norm_w, v_gdn_A_log, v_gdn_dt_bias])

    def small_out(kind, name):
        out = r_small[1 + kind * n_small + _SMALL_ROWS.index(name)]
        return out.reshape(final_norm_w.shape) if name == "final_norm_w" else out

    def outputs(kind):
        return (small_out(kind, "norm1_w"), r_w_in[kind][None], small_out(kind, "sb_norm_w"), r_conv[kind][None],
                small_out(kind, "gdn_A_log"), small_out(kind, "gdn_dt_bias"), small_out(kind, "gdn_norm_w"),
                r_wout[kind][None], small_out(kind, "final_norm_w"))

    return (r_small[0][0, 0], grad_x[None], *outputs(0), *outputs(1), *outputs(2), *outputs(3))
```

```python
import functools

import jax
import jax.numpy as jnp
from jax import lax
from jax.experimental import pallas as pl
from jax.experimental.pallas import tpu as pltpu

F32 = jnp.float32
MXU_DTYPE = jnp.bfloat16
WIRE_DTYPE = jnp.bfloat16
EXACT = lax.Precision.HIGHEST
EPS = 1e-6
N_DEV = 8
SB_HEAD_DIM = 64
GDN_HEAD_DIM = 128
GDN_HEADS = 4
GDN_CHUNKS_PER_STEP = 4
GDN_BWD_GROUP = 1
CHUNK = 64
CONV_WIDTH = 4
LANES = 128
SB_BLOCK = 128
SB_BQ = 256
VMEM_LIMIT_BYTES = 56 * 1024 * 1024

DPROJ_PIECE_OF_SLOT = (0, 1, 2, 4, 5, 6, 3, 7)
DPROJ_SB_SLOT, DPROJ_GDN_SLOT, DPROJ_GATE_SLOT = 0, 3, 6

ADAM_LR = 0.001
ADAM_B1 = 0.9
ADAM_B2 = 0.999
ADAM_EPS = 1e-08
ADAM_WD = 0.01
ADAM_STEP = 10

_NN = (((1,), (0,)), ((), ()))
_NT = (((1,), (1,)), ((), ()))
_TN = (((0,), (0,)), ((), ()))
_BNN = (((2,), (1,)), ((0,), (0,)))
_BNT = (((2,), (2,)), ((0,), (0,)))
_BTN = (((1,), (1,)), ((0,), (0,)))


def _mm(a, b):
    return jnp.dot(a.astype(MXU_DTYPE), b.astype(MXU_DTYPE), preferred_element_type=F32)


def _mm_nt(a, b):
    return lax.dot_general(a.astype(MXU_DTYPE), b.astype(MXU_DTYPE), _NT, preferred_element_type=F32)


def _mm_tn(a, b):
    return lax.dot_general(a.astype(MXU_DTYPE), b.astype(MXU_DTYPE), _TN, preferred_element_type=F32)


def _mx(a, b):
    return jnp.dot(a, b, precision=EXACT, preferred_element_type=F32)


def _mx_nt(a, b):
    return lax.dot_general(a, b, _NT, precision=EXACT, preferred_element_type=F32)


def _mx_tn(a, b):
    return lax.dot_general(a, b, _TN, precision=EXACT, preferred_element_type=F32)


def _split(x):
    hi = x.astype(MXU_DTYPE)
    return hi, (x - hi.astype(F32)).astype(MXU_DTYPE)


def _m3_general(a, b, dims):
    ah, al = _split(a)
    bh, bl = _split(b)
    dot = lambda x, y: lax.dot_general(x, y, dims, preferred_element_type=F32)
    (contract, _), (batch, _) = dims
    free = [ax for ax in range(a.ndim) if ax not in contract and ax not in batch][0]
    m = a.shape[free]
    both = dot(jnp.concatenate([ah, al], axis=free), bh)
    out_axis = len(batch)
    hi_part = lax.slice_in_dim(both, 0, m, axis=out_axis)
    lo_part = lax.slice_in_dim(both, m, 2 * m, axis=out_axis)
    return hi_part + (dot(ah, bl) + lo_part)


def _m3(a, b):
    return _m3_general(a, b, _NN)


def _m3_nt(a, b):
    return _m3_general(a, b, _NT)


def _m3_tn(a, b):
    return _m3_general(a, b, _TN)


def _sigmoid(z):
    return 1.0 / (1.0 + jnp.exp(-z))


def _softplus(z):
    return jnp.maximum(z, 0.0) + jnp.log(1.0 + jnp.exp(-jnp.abs(z)))


def _params(*semantics):
    return pltpu.CompilerParams(dimension_semantics=semantics, vmem_limit_bytes=VMEM_LIMIT_BYTES)


def _inproj_call(x, norm_w, w_main, w_small, w_small_t, gather=(), tm=256):
    t_len, d = x.shape
    n = w_main.shape[1]
    ns = w_small.shape[1]
    nst = w_small_t.shape[0]
    ng = len(gather)
    steps = t_len // tm

    def body(*refs):
        x_ref, nw_ref, wm_ref, ws_ref, wst_ref = refs[:5]
        pm_ref, ps_ref, pst_ref, ht_ref, r_ref = refs[5 + ng:10 + ng]
        copies = lambda: _direct_copies(refs[5:5 + ng], refs[10 + ng:10 + 2 * ng], *refs[10 + 2 * ng:], (False,) * ng)
        if ng:
            pl.when(pl.program_id(0) == 0)(lambda: _start_all(copies()))
        xv = x_ref[...]
        r = lax.rsqrt(jnp.mean(xv * xv, axis=-1, keepdims=True) + EPS)
        h = xv * r * nw_ref[...]
        hb = h.astype(MXU_DTYPE)
        for n0 in range(0, n, 512):
            pm_ref[:, n0:n0 + 512] = jnp.dot(hb, wm_ref[:, n0:n0 + 512], preferred_element_type=F32)
        ps_ref[...] = jnp.dot(hb, ws_ref[...], preferred_element_type=F32)
        pst_ref[...] = lax.dot_general(wst_ref[...], hb, _NT, preferred_element_type=F32)
        ht_ref[...] = h.T.astype(MXU_DTYPE)
        r_ref[...] = r
        if ng:
            pl.when(pl.program_id(0) == steps - 1)(lambda: _wait_all(copies()))

    return pl.pallas_call(
        body, name="inproj",
        grid=(steps,),
        in_specs=[pl.BlockSpec((tm, d), lambda i: (i, 0)),
                  pl.BlockSpec((1, d), lambda i: (0, 0)),
                  pl.BlockSpec((d, n), lambda i: (0, 0)),
                  pl.BlockSpec((d, ns), lambda i: (0, 0)),
                  pl.BlockSpec((nst, d), lambda i: (0, 0))] + [_HBM] * ng,
        out_specs=[pl.BlockSpec((tm, n), lambda i: (i, 0)),
                   pl.BlockSpec((tm, ns), lambda i: (i, 0)),
                   pl.BlockSpec((nst, tm), lambda i: (0, i)),
                   pl.BlockSpec((d, tm), lambda i: (0, i)),
                   pl.BlockSpec((tm, 1), lambda i: (i, 0))] + [_HBM] * ng,
        out_shape=[jax.ShapeDtypeStruct((t_len, n), F32),
                   jax.ShapeDtypeStruct((t_len, ns), F32),
                   jax.ShapeDtypeStruct((nst, t_len), F32),
                   jax.ShapeDtypeStruct((d, t_len), MXU_DTYPE),
                   jax.ShapeDtypeStruct((t_len, 1), F32)] + _direct_out_shapes(gather, (False,) * ng),
        scratch_shapes=_direct_semaphores(ng) if ng else [],
        compiler_params=_params("arbitrary"),
    )(x, norm_w, w_main, w_small, w_small_t, *gather)


def _running_sum_mm(x, tri):
    hi = x.astype(MXU_DTYPE)
    lo = (x - hi.astype(F32)).astype(MXU_DTYPE)
    return jnp.dot(hi, tri, preferred_element_type=F32) + jnp.dot(lo, tri, preferred_element_type=F32)


def _sb_iotas():
    row_i = lax.broadcasted_iota(jnp.int32, (SB_BQ, SB_BLOCK), 0)
    col_i = lax.broadcasted_iota(jnp.int32, (SB_BQ, SB_BLOCK), 1)
    sq_r = lax.broadcasted_iota(jnp.int32, (SB_BLOCK, SB_BLOCK), 0)
    sq_c = lax.broadcasted_iota(jnp.int32, (SB_BLOCK, SB_BLOCK), 1)
    return row_i, col_i, sq_r, sq_c


SB_DIAG_BLOCKS = SB_BQ // SB_BLOCK
SB_EXP_FLOOR = -110.0


def _sb_keys_descending(qi, tile, carry, z_bounds, n_heads, has_free):
    group = SB_DIAG_BLOCKS
    n_free = group * qi
    diag = list(range(group - 1, -1, -1))
    carry = tile([n_free + j for j in diag], [True] * group, carry, [j * SB_BLOCK for j in diag])

    def largest_exponent(c):
        worst = jnp.max(z_bounds[0] - c[1])
        for h in range(1, n_heads):
            worst = jnp.maximum(worst, jnp.max(z_bounds[h] - c[1 + h]))
        return worst

    always = group if has_free else 0

    def cond(state):
        return (state[0] < n_free) & ((state[1] > SB_EXP_FLOOR) | (state[0] < always))

    def body(state):
        first = n_free - 1 - state[0]
        c = tile([first - j for j in range(group)], [False] * group, state[2:])
        return (state[0] + group, largest_exponent(c), *c)

    out = lax.while_loop(cond, body, (jnp.int32(0), largest_exponent(carry), *carry))
    return out[2:], out[0]


def _sb_keys_ascending(qi, n_run, tile, carry, has_free):
    group = SB_DIAG_BLOCKS
    n_free = group * qi
    diag = list(range(group))
    kjs, los, masked = [n_free + j for j in diag], [j * SB_BLOCK for j in diag], [True] * group
    if has_free:
        early = lambda s: [n_free - n_run + group * s + j for j in range(group)]
        carry = lax.fori_loop(0, n_run // group - 1, lambda s, c: tile(early(s), [False] * group, c), carry)
        kjs, los, masked = [n_free - group + j for j in range(group)] + kjs, [0] * group + los, [False] * group + masked
    return tile(kjs, masked, carry, los)


def _sb_fwd_call(proj, t_len):
    nq = t_len // SB_BQ
    scale = float(SB_HEAD_DIM) ** -0.5
    n_pairs = 512 // LANES
    per_pair = LANES // SB_HEAD_DIM

    def body(q_ref, k_ref, v_ref, o_ref, st_ref, nrun_ref):
        lane = lax.broadcasted_iota(jnp.int32, (1, LANES), 1)
        row_i, col_i, sq_r, sq_c = _sb_iotas()
        ge = (sq_r >= sq_c).astype(MXU_DTYPE)
        hms = [((lane // SB_HEAD_DIM) == hh).astype(F32) for hh in range(per_pair)]
        k_sq = k_ref[...] * k_ref[...]
        k_norms = [jnp.sqrt(jnp.max(jnp.sum(k_sq * hm, axis=-1, keepdims=True))) * (1.02 * scale) for hm in hms]

        def q_block(qi, has_free):
            r0 = qi * SB_BQ if isinstance(qi, int) else pl.multiple_of(qi * SB_BQ, SB_BQ)
            rows = pl.ds(r0, SB_BQ)
            q_all = q_ref[rows, :]
            qms = [(q_all * (hm * scale)).astype(MXU_DTYPE) for hm in hms]
            z_bounds = [jnp.sqrt(jnp.sum(q_all * q_all * hm, axis=-1, keepdims=True)) * kn
                        for hm, kn in zip(hms, k_norms)]

            def tile(kjs, masked, kc, los=None):
                heads = range(per_pair)
                los = los or [0] * len(kjs)
                pairs = [(t, h) for t in range(len(kjs)) for h in heads]
                add_rows = lambda full, lo, part: full + part if lo == 0 else jnp.concatenate(
                    [full[:lo], full[lo:] + part], axis=0)
                acc, cs = kc[0], list(kc[1:])
                s0s = [kj * SB_BLOCK if isinstance(kj, int) else pl.multiple_of(kj * SB_BLOCK, SB_BLOCK) for kj in kjs]
                kbs = [k_ref[pl.ds(s0, SB_BLOCK), :].astype(MXU_DTYPE) for s0 in s0s]
                v_alls = [v_ref[pl.ds(s0, SB_BLOCK), :] for s0 in s0s]
                vms = {(t, h): (v_alls[t] * hms[h]).astype(MXU_DTYPE) for t, h in pairs}
                zs = {(t, h): lax.dot_general(qms[h][los[t]:], kbs[t], _NT, preferred_element_type=F32)
                      for t, h in pairs}
                masks = [(col_i[lo:] + s0) < (row_i[lo:] + r0) if m else None for m, lo, s0 in zip(masked, los, s0s)]
                keep = lambda t, a: a if masks[t] is None else jnp.where(masks[t], a, 0.0)
                sps = {(t, h): keep(t, _softplus(zs[t, h])) for t, h in pairs}
                sums = {p: _running_sum_mm(sps[p], ge) for p in pairs}
                mass = {}
                for t, h in pairs:
                    mass[t, h] = cs[h] if t == 0 else add_rows(
                        mass[t - 1, h], los[t - 1], jnp.sum(sps[t - 1, h], axis=-1, keepdims=True))
                ws = {(t, h): keep(t, jnp.exp(zs[t, h] - (sums[t, h] + mass[t, h][los[t]:]))) for t, h in pairs}
                for t, h in pairs:
                    acc = add_rows(acc, los[t], jnp.dot(ws[t, h].astype(MXU_DTYPE), vms[t, h],
                                                        preferred_element_type=F32))
                last = len(kjs) - 1
                cs = [add_rows(mass[last, h], los[last], jnp.sum(sps[last, h], axis=-1, keepdims=True)) for h in heads]
                return (acc, *cs)

            zero_col = jnp.zeros((SB_BQ, 1), F32)
            out, n_run = _sb_keys_descending(
                qi, tile, (jnp.zeros((SB_BQ, LANES), F32),) + (zero_col,) * per_pair, z_bounds, per_pair, has_free)
            o_ref[rows, :] = out[0]
            for hh in range(per_pair):
                st_ref[hh, rows, :] = out[1 + hh]
            nrun_ref[pl.program_id(0), qi] = n_run

        q_block(0, False)
        lax.fori_loop(1, nq, lambda qi, carry: (q_block(qi, True), carry)[1], 0)

    return pl.pallas_call(
        body, name="sb_fwd",
        grid=(n_pairs,),
        in_specs=[pl.BlockSpec((t_len, LANES), lambda p: (0, p)),
                  pl.BlockSpec((t_len, LANES), lambda p: (0, n_pairs + p)),
                  pl.BlockSpec((t_len, LANES), lambda p: (0, 2 * n_pairs + p))],
        out_specs=[pl.BlockSpec((t_len, LANES), lambda p: (0, p)),
                   pl.BlockSpec((per_pair, t_len, 1), lambda p: (p, 0, 0)),
                   pl.BlockSpec(memory_space=pltpu.SMEM)],
        out_shape=[jax.ShapeDtypeStruct((t_len, 512), F32),
                   jax.ShapeDtypeStruct((n_pairs * per_pair, t_len, 1), F32),
                   jax.ShapeDtypeStruct((n_pairs, nq), jnp.int32)],
        compiler_params=_params("arbitrary"),
    )(proj, proj, proj)


def _sb_bwd_call(proj, sp_total, n_run_all, d_o, dproj, t_len):
    nq = t_len // SB_BQ
    scale = float(SB_HEAD_DIM) ** -0.5
    n_pairs = 512 // LANES
    per_pair = LANES // SB_HEAD_DIM

    def body(q_ref, k_ref, v_ref, st_ref, nrun_ref, do_ref, dproj_in_ref, d_ref):
        lane = lax.broadcasted_iota(jnp.int32, (1, LANES), 1)
        row_i, col_i, sq_r, sq_c = _sb_iotas()
        lt = (sq_r < sq_c).astype(MXU_DTYPE)
        le = (sq_r <= sq_c).astype(MXU_DTYPE)
        hms = [((lane // SB_HEAD_DIM) == hh).astype(F32) for hh in range(per_pair)]
        d_ref[1] = jnp.zeros((t_len, LANES), F32)
        d_ref[2] = jnp.zeros((t_len, LANES), F32)

        def q_block(qi, has_free):
            r0 = qi * SB_BQ if isinstance(qi, int) else pl.multiple_of(qi * SB_BQ, SB_BQ)
            rows = pl.ds(r0, SB_BQ)
            q_all, do_all = q_ref[rows, :], do_ref[rows, :]
            qms = [(q_all * (hm * scale)).astype(MXU_DTYPE) for hm in hms]
            doms = [(do_all * hm).astype(MXU_DTYPE) for hm in hms]
            totals = [st_ref[hh, rows, :] for hh in range(per_pair)]

            def tile(kjs, masked, kc, los=None):
                heads = range(per_pair)
                los = los or [0] * len(kjs)
                pairs = [(t, h) for t in range(len(kjs)) for h in heads]
                add_rows = lambda full, lo, part: full + part if lo == 0 else jnp.concatenate(
                    [full[:lo], full[lo:] + part], axis=0)
                rsum = lambda a: jnp.sum(a, axis=-1, keepdims=True)
                dq, cls, gls = kc[0], list(kc[1:1 + per_pair]), list(kc[1 + per_pair:])
                s0s = [kj * SB_BLOCK if isinstance(kj, int) else pl.multiple_of(kj * SB_BLOCK, SB_BLOCK) for kj in kjs]
                k_alls = [k_ref[pl.ds(s0, SB_BLOCK), :] for s0 in s0s]
                v_alls = [v_ref[pl.ds(s0, SB_BLOCK), :] for s0 in s0s]
                kbs = [k_all.astype(MXU_DTYPE) for k_all in k_alls]
                vms = {(t, h): (v_alls[t] * hms[h]).astype(MXU_DTYPE) for t, h in pairs}
                kms = {(t, h): (k_alls[t] * (hms[h] * scale)).astype(MXU_DTYPE) for t, h in pairs}
                q_live = {(t, h): qms[h][los[t]:] for t, h in pairs}
                do_live = {(t, h): doms[h][los[t]:] for t, h in pairs}
                zs = {p: lax.dot_general(q_live[p], kbs[p[0]], _NT, preferred_element_type=F32) for p in pairs}
                das = {p: lax.dot_general(do_live[p], vms[p], _NT, preferred_element_type=F32) for p in pairs}
                masks = [(col_i[lo:] + s0) < (row_i[lo:] + r0) if m else None for m, lo, s0 in zip(masked, los, s0s)]
                keep = lambda t, a: a if masks[t] is None else jnp.where(masks[t], a, 0.0)
                sp_alls = {p: _softplus(zs[p]) for p in pairs}
                sps = {(t, h): keep(t, sp_alls[t, h]) for t, h in pairs}
                lefts = {p: _running_sum_mm(sps[p], lt) for p in pairs}
                cl = {}
                for t, h in pairs:
                    cl[t, h] = cls[h] if t == 0 else add_rows(cl[t - 1, h], los[t - 1], rsum(sps[t - 1, h]))
                ws = {(t, h): keep(t, jnp.exp(zs[t, h] - ((totals[h] - cl[t, h])[los[t]:] - lefts[t, h])))
                      for t, h in pairs}
                gs = {p: das[p] * ws[p] for p in pairs}
                g_sums = {p: _running_sum_mm(gs[p], le) for p in pairs}
                gl = {}
                for t, h in pairs:
                    gl[t, h] = gls[h] if t == 0 else add_rows(gl[t - 1, h], los[t - 1], rsum(gs[t - 1, h]))
                dzs = {(t, h): keep(t, gs[t, h] - jnp.exp(zs[t, h] - sp_alls[t, h]) * (gl[t, h][los[t]:] + g_sums[t, h])
                               ).astype(MXU_DTYPE) for t, h in pairs}
                for t in range(len(kjs)):
                    dk_t = jnp.zeros((SB_BLOCK, LANES), F32)
                    dv_t = jnp.zeros((SB_BLOCK, LANES), F32)
                    for h in heads:
                        dq = add_rows(dq, los[t], jnp.dot(dzs[t, h], kms[t, h], preferred_element_type=F32))
                        dk_t = dk_t + lax.dot_general(dzs[t, h], q_live[t, h], _TN, preferred_element_type=F32)
                        dv_t = dv_t + lax.dot_general(ws[t, h].astype(MXU_DTYPE), do_live[t, h], _TN,
                                                      preferred_element_type=F32)
                    d_ref[1, pl.ds(s0s[t], SB_BLOCK), :] += dk_t
                    d_ref[2, pl.ds(s0s[t], SB_BLOCK), :] += dv_t
                last = len(kjs) - 1
                cls = [add_rows(cl[last, h], los[last], rsum(sps[last, h])) for h in heads]
                gls = [add_rows(gl[last, h], los[last], rsum(gs[last, h])) for h in heads]
                return (dq, *cls, *gls)

            zero_col = jnp.zeros((SB_BQ, 1), F32)
            out = _sb_keys_ascending(qi, nrun_ref[pl.program_id(0), qi], tile,
                                     (jnp.zeros((SB_BQ, LANES), F32),) + (zero_col,) * (2 * per_pair), has_free)
            d_ref[0, rows, :] = out[0]

        q_block(0, False)
        lax.fori_loop(1, nq, lambda qi, carry: (q_block(qi, True), carry)[1], 0)

    col = lambda off: pl.BlockSpec((t_len, LANES), lambda p: (0, off + p))
    return pl.pallas_call(
        body, name="sb_bwd",
        grid=(n_pairs,),
        in_specs=[col(0), col(n_pairs), col(2 * n_pairs),
                  pl.BlockSpec((per_pair, t_len, 1), lambda p: (p, 0, 0)),
                  pl.BlockSpec(memory_space=pltpu.SMEM), col(0), _HBM],
        out_specs=pl.BlockSpec((3, t_len, LANES), lambda p: (DPROJ_SB_SLOT // 3, 0, p)),
        out_shape=jax.ShapeDtypeStruct(dproj.shape, dproj.dtype),
        input_output_aliases={6: 0},
        compiler_params=_params("arbitrary"),
    )(proj, proj, proj, sp_total, n_run_all, d_o, dproj)


def _conv_taps(xin, rows, t_len):
    taps = []
    for i in range(CONV_WIDTH):
        shift = CONV_WIDTH - 1 - i
        if shift == 0:
            taps.append(xin)
        else:
            taps.append(jnp.where(rows >= shift, pltpu.roll(xin, shift, axis=0), 0.0))
    return taps


def _gdn_prep_body_common(x_ref, w_ref, t_len):
    j = pl.program_id(0)
    xin = x_ref[...]
    rows = lax.broadcasted_iota(jnp.int32, (t_len, LANES), 0)
    taps = _conv_taps(xin, rows, t_len)
    pre = taps[0] * w_ref[0:1, :]
    for i in range(1, CONV_WIDTH):
        pre = pre + taps[i] * w_ref[i:i + 1, :]
    sg = _sigmoid(pre)
    act = pre * sg
    is_qk = j < 2 * GDN_HEADS
    nrm = jnp.where(is_qk, lax.rsqrt(jnp.sum(act * act, axis=-1, keepdims=True) + EPS), 1.0)
    sc = jnp.where(j < GDN_HEADS, float(GDN_HEAD_DIM) ** -0.5, 1.0)
    return j, rows, taps, pre, sg, act, is_qk, nrm, sc


def _gdn_prep_call(proj, conv_w, t_len):
    first = 2048 // LANES

    def body(x_ref, w_ref, out_ref):
        _, _, _, _, _, act, _, nrm, sc = _gdn_prep_body_common(x_ref, w_ref, t_len)
        out_ref[...] = act * nrm * sc

    return pl.pallas_call(
        body, name="gdn_prep",
        grid=(3 * GDN_HEADS,),
        in_specs=[pl.BlockSpec((t_len, LANES), lambda j: (0, first + j)),
                  pl.BlockSpec((CONV_WIDTH, LANES), lambda j: (0, j))],
        out_specs=pl.BlockSpec((t_len, LANES), lambda j: (0, j)),
        out_shape=jax.ShapeDtypeStruct((t_len, 3 * 512), F32),
        compiler_params=_params("arbitrary"),
    )(proj, conv_w)


def _gdn_prep_bwd_call(proj, conv_w, d_act3, dproj, t_len):
    first = 2048 // LANES

    def body(x_ref, w_ref, d_ref, dproj_in_ref, dx_ref, dw_ref):
        _, rows, taps, pre, sg, act, is_qk, nrm, sc = _gdn_prep_body_common(x_ref, w_ref, t_len)
        d_out = d_ref[0]
        dn = d_out * sc
        d_norm = nrm * dn - act * (nrm * nrm * nrm) * jnp.sum(dn * act, axis=-1, keepdims=True)
        d_act = jnp.where(is_qk, d_norm, d_out)
        d_pre = d_act * sg * (1.0 + pre * (1.0 - sg))
        dx = d_pre * w_ref[CONV_WIDTH - 1:CONV_WIDTH, :]
        dw_ref[CONV_WIDTH - 1:CONV_WIDTH, :] = jnp.sum(d_pre * taps[CONV_WIDTH - 1], axis=0, keepdims=True)
        for i in range(CONV_WIDTH - 1):
            shift = CONV_WIDTH - 1 - i
            up = jnp.where(rows < t_len - shift, pltpu.roll(d_pre, t_len - shift, axis=0), 0.0)
            dx = dx + up * w_ref[i:i + 1, :]
            dw_ref[i:i + 1, :] = jnp.sum(d_pre * taps[i], axis=0, keepdims=True)
        dx_ref[0] = dx

    return pl.pallas_call(
        body, name="gdn_prep_bwd",
        grid=(3 * GDN_HEADS,),
        in_specs=[pl.BlockSpec((t_len, LANES), lambda j: (0, first + j)),
                  pl.BlockSpec((CONV_WIDTH, LANES), lambda j: (0, j)),
                  pl.BlockSpec((1, t_len, LANES), lambda j: (j // GDN_HEADS, 0, j % GDN_HEADS)), _HBM],
        out_specs=[pl.BlockSpec((1, t_len, LANES), lambda j: (DPROJ_GDN_SLOT + j // GDN_HEADS, 0, j % GDN_HEADS)),
                   pl.BlockSpec((CONV_WIDTH, LANES), lambda j: (0, j))],
        out_shape=[jax.ShapeDtypeStruct(dproj.shape, dproj.dtype),
                   jax.ShapeDtypeStruct((CONV_WIDTH, 3 * 512), F32)],
        input_output_aliases={3: 0},
        compiler_params=_params("arbitrary"),
    )(proj, conv_w, d_act3, dproj)


def _chunk_cumsum_matrix():
    r = lax.broadcasted_iota(jnp.int32, (LANES, LANES), 0)
    c = lax.broadcasted_iota(jnp.int32, (LANES, LANES), 1)
    return ((r <= c) & ((r // CHUNK) == (c // CHUNK))).astype(F32)


def _gdn_gates_call(ps, pst, alog_l, dtb_l, alog_c, dtb_c, t_len):
    def body(ps_ref, pst_ref, al_ref, dl_ref, ac_ref, dc_ref, beta_ref, gcol_ref, grow_ref):
        upper = _chunk_cumsum_matrix()
        lower = upper.T
        psv = ps_ref[...]
        beta_ref[...] = _sigmoid(psv)
        g_l = -jnp.exp(al_ref[...]) * _softplus(psv + dl_ref[...])
        g_r = -jnp.exp(ac_ref[...]) * _softplus(pst_ref[...] + dc_ref[...])
        for w in range(t_len // LANES):
            sl = slice(w * LANES, (w + 1) * LANES)
            gcol_ref[sl, :] = _mx(lower, g_l[sl, :])
            grow_ref[:, sl] = _mx(g_r[:, sl], upper)

    vm = pl.BlockSpec(memory_space=pltpu.VMEM)
    return pl.pallas_call(
        body, name="gdn_gates",
        in_specs=[vm] * 6, out_specs=[vm] * 3,
        out_shape=[jax.ShapeDtypeStruct((t_len, LANES), F32),
                   jax.ShapeDtypeStruct((t_len, LANES), F32),
                   jax.ShapeDtypeStruct((8, t_len), F32)],
        compiler_params=pltpu.CompilerParams(vmem_limit_bytes=VMEM_LIMIT_BYTES),
    )(ps, pst, alog_l, dtb_l, alog_c, dtb_c)


def _gdn_gates_bwd_call(ps, alog_l, dtb_l, d_l, t_len):
    def body(ps_ref, al_ref, dl_ref, d_ref, dps_ref, gal_ref, gdt_ref):
        lane = lax.broadcasted_iota(jnp.int32, (1, LANES), 1)
        psv = ps_ref[...]
        dv = d_ref[...]
        beta = _sigmoid(psv)
        ea = jnp.exp(al_ref[...])
        arg = psv + dl_ref[...]
        g = -ea * _softplus(arg)
        d_a = dv * (-ea) * _sigmoid(arg)
        is_a = (lane >= GDN_HEADS) & (lane < 2 * GDN_HEADS)
        dps_ref[...] = jnp.where(lane < GDN_HEADS, dv * beta * (1.0 - beta), jnp.where(is_a, d_a, 0.0))
        gdt_ref[...] = jnp.where(is_a, jnp.sum(d_a, axis=0, keepdims=True), 0.0)
        gal_ref[...] = jnp.where(is_a, jnp.sum(dv * g, axis=0, keepdims=True), 0.0)

    vm = pl.BlockSpec(memory_space=pltpu.VMEM)
    return pl.pallas_call(
        body, name="gdn_gates_bwd",
        in_specs=[vm] * 4, out_specs=[vm] * 3,
        out_shape=[jax.ShapeDtypeStruct((t_len, LANES), F32),
                   jax.ShapeDtypeStruct((1, LANES), F32),
                   jax.ShapeDtypeStruct((1, LANES), F32)],
        compiler_params=pltpu.CompilerParams(vmem_limit_bytes=VMEM_LIMIT_BYTES),
    )(ps, alog_l, dtb_l, d_l)


def _bm(a, b):
    return _m3_general(a, b, _BNN)


def _bm_nt(a, b):
    return _m3_general(a, b, _BNT)


def _bm_tn(a, b):
    return _m3_general(a, b, _BTN)


def _heads_of(ref, rows):
    return jnp.stack([ref[rows, h * GDN_HEAD_DIM:(h + 1) * GDN_HEAD_DIM] for h in range(GDN_HEADS)])


def _chunk_terms(q_ref, k_ref, v_ref, b_ref, gc_ref, gr_ref, c, incl, strict, n=1):
    r0 = c * CHUNK if isinstance(c, int) else pl.multiple_of(c * CHUNK, CHUNK)
    rows = pl.ds(r0, n * CHUNK)
    per_chunk = lambda x: x.reshape(GDN_HEADS * n, CHUNK, x.shape[-1])
    q, k, v = (per_chunk(_heads_of(ref, rows)) for ref in (q_ref, k_ref, v_ref))
    lane_ids = lax.broadcasted_iota(jnp.int32, (1, LANES), 1)
    pick = lambda slab, first: jnp.stack([jnp.sum(jnp.where(lane_ids == first + h, slab, 0.0), axis=-1, keepdims=True)
                                          for h in range(GDN_HEADS)])
    b = per_chunk(pick(b_ref[rows, :], 0))
    gc = per_chunk(pick(gc_ref[rows, :], GDN_HEADS))
    gr = gr_ref[:, c] if n == 1 else gr_ref[:, c:c + n].reshape(GDN_HEADS * n, 1, CHUNK)
    dm = jnp.where(incl, jnp.exp(jnp.where(incl, gc - gr, 0.0)), 0.0)
    kb = k * b
    vb = v * b
    e = jnp.exp(gc)
    kk_qk = _bm_nt(jnp.concatenate([kb, q], axis=1), k)
    a = jnp.where(strict, kk_qk[:, :CHUNK] * dm, 0.0)
    p = jnp.where(incl, kk_qk[:, CHUNK:] * dm, 0.0)
    gl = gc[:, CHUNK - 1:CHUNK, :]
    eg = jnp.exp(gl - gc)
    return rows, q, k, v, b, gc, dm, kb, vb, e, a, p, gl, eg


def _unit_lower_inverse(a, eye):
    x = -a
    tm = eye + x
    xp = _bm(x, x)
    for _ in range(4):
        both = _bm(jnp.concatenate([xp, tm], axis=1), xp)
        tm = tm + both[:, CHUNK:]
        xp = both[:, :CHUNK]
    return tm + _bm(tm, xp)


def _gdn_specs(t_len, n_chunks, reverse):
    cps = GDN_CHUNKS_PER_STEP
    steps = n_chunks // cps
    at = (lambda g: steps - 1 - g) if reverse else (lambda g: g)
    rows_blk = lambda width, part=0: pl.BlockSpec((cps * CHUNK, width), lambda g: (at(g), part))
    gate_r = pl.BlockSpec((GDN_HEADS, cps, 1, CHUNK), lambda g: (0, at(g), 0, 0))
    per_chunk = lambda r, c: pl.BlockSpec((GDN_HEADS, cps, r, c), lambda g: (0, at(g), 0, 0))
    return cps, steps, rows_blk, gate_r, per_chunk


def _gdn_fwd_call(gact, beta_c, gam_c, gam_r, t_len):
    n_chunks = t_len // CHUNK
    dk = GDN_HEAD_DIM
    width = GDN_HEADS * dk
    cps, steps, rows_blk, gate_r, per_chunk = _gdn_specs(t_len, n_chunks, False)

    def body(q_ref, k_ref, v_ref, b_ref, gc_ref, gr_ref, o_ref, s_ref, t_ref, state_ref):
        row = lax.broadcasted_iota(jnp.int32, (CHUNK, CHUNK), 0)
        col = lax.broadcasted_iota(jnp.int32, (CHUNK, CHUNK), 1)
        incl, strict = row >= col, row > col
        eye = (row == col).astype(F32)

        @pl.when(pl.program_id(0) == 0)
        def _():
            state_ref[...] = jnp.zeros_like(state_ref)

        _, q, k, v, b, gc, dm, kb, vb, e, a, p, gl, eg = _chunk_terms(
            q_ref, k_ref, v_ref, b_ref, gc_ref, gr_ref, 0, incl, strict, cps)
        tm = _unit_lower_inverse(a, eye)
        uw = _bm(tm, jnp.concatenate([vb, kb * e], axis=2))
        w_qe = jnp.concatenate([uw[:, :, dk:], q * e], axis=1)
        u, kd, decay = uw[:, :, :dk], k * eg, jnp.exp(gl)
        t_ref[...] = tm.reshape(GDN_HEADS, cps, CHUNK, CHUNK)

        of_chunk = lambda x, c: jnp.stack([x[h * cps + c] for h in range(GDN_HEADS)])
        s = state_ref[...]
        for c in range(cps):
            ws_qs = _bm(of_chunk(w_qe, c), s)
            vn = of_chunk(u, c) - ws_qs[:, :CHUNK]
            o = ws_qs[:, CHUNK:] + _bm(of_chunk(p, c), vn)
            for h in range(GDN_HEADS):
                o_ref[c * CHUNK:(c + 1) * CHUNK, h * dk:(h + 1) * dk] = o[h]
            s_ref[:, c] = s
            s = s * of_chunk(decay, c) + _bm_tn(of_chunk(kd, c), vn)
        state_ref[...] = s

    return pl.pallas_call(
        body, name="gdn_fwd",
        grid=(steps,),
        in_specs=[rows_blk(width, 0), rows_blk(width, 1), rows_blk(width, 2), rows_blk(LANES), rows_blk(LANES), gate_r],
        out_specs=[rows_blk(width), per_chunk(dk, dk), per_chunk(CHUNK, CHUNK)],
        out_shape=[jax.ShapeDtypeStruct((t_len, width), F32),
                   jax.ShapeDtypeStruct((GDN_HEADS, n_chunks, dk, dk), F32),
                   jax.ShapeDtypeStruct((GDN_HEADS, n_chunks, CHUNK, CHUNK), F32)],
        scratch_shapes=[pltpu.VMEM((GDN_HEADS, dk, dk), F32)],
        compiler_params=_params("arbitrary"),
    )(gact, gact, gact, beta_c, gam_c, gam_r)


def _gdn_bwd_call(gact, beta_c, gam_c, gam_r, s_all, t_all, d_o, t_len, scatter=()):
    n_chunks = t_len // CHUNK
    dk = GDN_HEAD_DIM
    width = GDN_HEADS * dk
    cps, steps, rows_blk, gate_r, per_chunk = _gdn_specs(t_len, n_chunks, True)
    nx = len(scatter)

    def body(*refs):
        q_ref, k_ref, v_ref, b_ref, gc_ref, gr_ref, s_ref, t_ref, do_ref = refs[:9]
        d_ref, dgate_ref = refs[9 + nx:11 + nx]
        dstate_ref = refs[11 + 2 * nx]
        copies = lambda: _direct_copies(refs[9:9 + nx], refs[11 + nx:11 + 2 * nx], *refs[12 + 2 * nx:], (True,) * nx)
        if nx:
            pl.when(pl.program_id(0) == 0)(lambda: _start_all(copies()))
        row = lax.broadcasted_iota(jnp.int32, (CHUNK, CHUNK), 0)
        col = lax.broadcasted_iota(jnp.int32, (CHUNK, CHUNK), 1)
        incl, strict = row >= col, row > col
        ng = GDN_BWD_GROUP
        nb = GDN_HEADS * ng
        upper = jnp.broadcast_to((row <= col).astype(F32), (nb, CHUNK, CHUNK))
        ones = jnp.ones((nb, CHUNK, LANES), F32)
        last_row = lax.broadcasted_iota(jnp.int32, (CHUNK, 1), 0) == CHUNK - 1
        lane_ids = lax.broadcasted_iota(jnp.int32, (1, LANES), 1)
        rsum = lambda m: jnp.sum(m, axis=-1, keepdims=True)
        total = lambda m: jnp.sum(rsum(m), axis=1, keepdims=True)
        of_chunk = lambda x, c: jnp.stack([x[h * ng + c] for h in range(GDN_HEADS)])

        @pl.when(pl.program_id(0) == 0)
        def _():
            dstate_ref[...] = jnp.zeros_like(dstate_ref)

        for c0 in range(cps - ng, -1, -ng):
            group(c0, q_ref, k_ref, v_ref, b_ref, gc_ref, gr_ref, s_ref, t_ref, do_ref, d_ref, dgate_ref, dstate_ref,
                  incl, strict, upper, ones, last_row, lane_ids, rsum, total, of_chunk)
        if nx:
            pl.when(pl.program_id(0) == steps - 1)(lambda: _wait_all(copies()))

    def group(c0, q_ref, k_ref, v_ref, b_ref, gc_ref, gr_ref, s_ref, t_ref, do_ref, d_ref, dgate_ref, dstate_ref,
              incl, strict, upper, ones, last_row, lane_ids, rsum, total, of_chunk):
        ng = GDN_BWD_GROUP
        nb = GDN_HEADS * ng
        rows = pl.ds(c0 * CHUNK, ng * CHUNK)
        _, q, k, v, b, gc, dm, kb, vb, e, a, p, gl, eg = _chunk_terms(
            q_ref, k_ref, v_ref, b_ref, gc_ref, gr_ref, c0, incl, strict, ng)
        s = s_ref[:, c0:c0 + ng].reshape(nb, dk, dk)
        tm = t_ref[:, c0:c0 + ng].reshape(nb, CHUNK, CHUNK)
        d_out = _heads_of(do_ref, rows).reshape(nb, CHUNK, dk)
        el = jnp.exp(gl)
        kbe = kb * e
        qe = q * e
        kd = k * eg
        uw = _bm(tm, jnp.concatenate([vb, kbe], axis=2))
        u, w = uw[:, :, :dk], uw[:, :, dk:]
        vn = u - _bm(w, s)
        pt_do = _bm_tn(p, d_out)
        qet_do = _bm_tn(qe, d_out)

        ds = dstate_ref[...]
        d_vn_c, ds_c = [None] * ng, [None] * ng
        for c in range(ng - 1, -1, -1):
            ds_c[c] = ds
            d_vn_c[c] = of_chunk(pt_do, c) + _bm(of_chunk(kd, c), ds)
            ds = of_chunk(el, c) * ds + of_chunk(qet_do, c) - _bm_tn(of_chunk(w, c), d_vn_c[c])
        dstate_ref[...] = ds
        by_chunk = lambda xs: jnp.stack([xs[c][h] for h in range(GDN_HEADS) for c in range(ng)])
        d_vn, ds = by_chunk(d_vn_c), by_chunk(ds_c)

        on_s = _bm_nt(jnp.concatenate([d_out, d_vn], axis=1), s)
        d_qe, d_w = on_s[:, :CHUNK], -on_s[:, CHUNK:]
        d_p = jnp.where(incl, _bm_nt(d_out, vn), 0.0)
        d_kd = _bm_nt(vn, ds)
        d_both = _bm_tn(tm, jnp.concatenate([d_vn, d_w], axis=2))
        d_vb, d_kbe = d_both[:, :, :dk], d_both[:, :, dk:]
        d_a = -jnp.where(strict, _bm_nt(d_both, uw), 0.0)
        m = d_a * dm
        n = d_p * dm
        on_k = _bm(jnp.concatenate([m, n], axis=1), k)
        d_kb = on_k[:, :CHUNK] + d_kbe * e
        d_q = on_k[:, CHUNK:] + d_qe * e
        d_k = (_bm_tn(jnp.concatenate([m, n], axis=1), jnp.concatenate([kb, q], axis=1))
               + d_kd * eg + b * d_kb)
        d_v = b * d_vb
        r = d_a * a + d_p * p
        kd_term = rsum(d_kd * kd)
        d_gl = total(ds * s) * el + jnp.sum(kd_term, axis=1, keepdims=True)
        d_gam = (rsum(r) - _bm_tn(r, ones)[:, :, 0:1] + rsum(d_qe * qe) + rsum(d_kbe * kbe) - kd_term
                 + jnp.where(last_row, d_gl, 0.0))
        d_beta = rsum(d_kb * k) + rsum(d_vb * v)
        d_g = _bm(upper, d_gam * ones)[:, :, 0:1]
        per_head = lambda x: x.reshape(GDN_HEADS, ng * CHUNK, x.shape[-1])
        d_q, d_k, d_v, d_beta, d_g = (per_head(x) for x in (d_q, d_k, d_v, d_beta, d_g))
        gates = jnp.zeros((ng * CHUNK, LANES), F32)
        for h in range(GDN_HEADS):
            lanes = slice(h * dk, (h + 1) * dk)
            d_ref[0, rows, lanes] = d_q[h]
            d_ref[1, rows, lanes] = d_k[h]
            d_ref[2, rows, lanes] = d_v[h]
            gates = gates + (jnp.where(lane_ids == h, d_beta[h], 0.0)
                             + jnp.where(lane_ids == GDN_HEADS + h, d_g[h], 0.0))
        dgate_ref[rows, :] = gates

    d_spec = pl.BlockSpec((3, cps * CHUNK, width), lambda g: (0, steps - 1 - g, 0))
    return pl.pallas_call(
        body, name="gdn_bwd",
        grid=(steps,),
        in_specs=[rows_blk(width, 0), rows_blk(width, 1), rows_blk(width, 2), rows_blk(LANES), rows_blk(LANES), gate_r,
                  per_chunk(dk, dk), per_chunk(CHUNK, CHUNK), rows_blk(width)] + [_HBM] * nx,
        out_specs=[d_spec, rows_blk(LANES)] + [_HBM] * nx,
        out_shape=[jax.ShapeDtypeStruct((3, t_len, width), F32),
                   jax.ShapeDtypeStruct((t_len, LANES), F32)] + _direct_out_shapes(scatter, (True,) * nx),
        scratch_shapes=[pltpu.VMEM((GDN_HEADS, dk, dk), F32)] + (_direct_semaphores(nx) if nx else []),
        compiler_params=_params("arbitrary"),
    )(gact, gact, gact, beta_c, gam_c, gam_r, s_all, t_all, d_o, *scatter)


def _group_matrix(width, group):
    r = lax.broadcasted_iota(jnp.int32, (width, width), 0)
    c = lax.broadcasted_iota(jnp.int32, (width, width), 1)
    return ((r // group) == (c // group)).astype(F32)


def _post_call(o_sb, o_gd, proj, x, target, w_out, sbw, gdw, fw, tm=256):
    t_len, d = x.shape
    half = 512
    zsb_blk = 1536 // half
    zgd_blk = 3584 // half

    def body(osb_ref, ogd_ref, zsb_ref, zgd_ref, x_ref, tg_ref, wo_ref, sbw_ref, gdw_ref, fw_ref,
             dx2_ref, dosb_ref, dogd_ref, dz_ref, loss_ref, gfw_ref, gsb_ref, ggd_ref, gwo_ref):
        step = pl.program_id(0)

        @pl.when(step == 0)
        def _():
            loss_ref[...] = jnp.zeros_like(loss_ref)
            gfw_ref[...] = jnp.zeros_like(gfw_ref)
            gsb_ref[...] = jnp.zeros_like(gsb_ref)
            ggd_ref[...] = jnp.zeros_like(ggd_ref)
            gwo_ref[...] = jnp.zeros_like(gwo_ref)

        def head_forward(o, z, w, gmat, inv):
            r = lax.rsqrt(_running_sum_mm(o * o, gmat) * inv + EPS)
            nrm = o * r * w
            sg = _sigmoid(z)
            return r, nrm, sg, nrm * (z * sg)

        def head_backward(d_m, o, z, w, gmat, inv, r, nrm, sg):
            d_n = d_m * (z * sg)
            d_z = d_m * nrm * (sg * (1.0 + z * (1.0 - sg)))
            dnw = d_n * w
            d_o = r * dnw - o * (r * r * r) * (_running_sum_mm(dnw * o, gmat) * inv)
            return d_o, d_z, jnp.sum(d_n * o * r, axis=0, keepdims=True)

        g_sb = _group_matrix(half, SB_HEAD_DIM).astype(MXU_DTYPE)
        g_gd = _group_matrix(half, GDN_HEAD_DIM).astype(MXU_DTYPE)
        osb, ogd, zsb, zgd = osb_ref[...], ogd_ref[...], zsb_ref[...], zgd_ref[...]
        sbw_v, gdw_v = sbw_ref[...], gdw_ref[...]
        r_sb, n_sb, sg_sb, m_sb = head_forward(osb, zsb, sbw_v, g_sb, 1.0 / SB_HEAD_DIM)
        r_gd, n_gd, sg_gd, m_gd = head_forward(ogd, zgd, gdw_v, g_gd, 1.0 / GDN_HEAD_DIM)
        mixed = jnp.concatenate([m_sb, m_gd], axis=1).astype(MXU_DTYPE)
        wo = wo_ref[...]
        x2 = x_ref[...] + jnp.dot(mixed, wo, preferred_element_type=F32)
        r2 = lax.rsqrt(jnp.mean(x2 * x2, axis=-1, keepdims=True) + EPS)
        fw_v = fw_ref[...]
        err = x2 * r2 * fw_v - tg_ref[...]
        loss_ref[...] += 0.5 * jnp.sum(jnp.sum(err * err, axis=-1, keepdims=True) * (1.0 / d))
        dy = err * (1.0 / d)
        gg = dy * fw_v
        dx2 = r2 * gg - x2 * ((r2 * r2 * r2) * jnp.mean(gg * x2, axis=-1, keepdims=True))
        gfw_ref[...] += jnp.sum(dy * x2 * r2, axis=0, keepdims=True)
        dx2_ref[...] = dx2
        dx2b = dx2.astype(MXU_DTYPE)
        d_mixed = lax.dot_general(dx2b, wo, _NT, preferred_element_type=F32)
        gwo_ref[...] += lax.dot_general(mixed, dx2b, _TN, preferred_element_type=F32)
        d_osb, d_zsb, gsb = head_backward(d_mixed[:, :half], osb, zsb, sbw_v, g_sb, 1.0 / SB_HEAD_DIM, r_sb, n_sb, sg_sb)
        d_ogd, d_zgd, ggd = head_backward(d_mixed[:, half:], ogd, zgd, gdw_v, g_gd, 1.0 / GDN_HEAD_DIM, r_gd, n_gd, sg_gd)
        dosb_ref[...] = d_osb
        dogd_ref[...] = d_ogd
        dz_ref[0] = d_zsb
        dz_ref[1] = d_zgd
        gsb_ref[...] += gsb
        ggd_ref[...] += ggd

    row_blk = lambda w: pl.BlockSpec((tm, w), lambda i: (i, 0))
    fixed = lambda r, w: pl.BlockSpec((r, w), lambda i: (0, 0))
    return pl.pallas_call(
        body, name="post",
        grid=(t_len // tm,),
        in_specs=[row_blk(half), row_blk(half),
                  pl.BlockSpec((tm, half), lambda i: (i, zsb_blk)),
                  pl.BlockSpec((tm, half), lambda i: (i, zgd_blk)),
                  row_blk(d), row_blk(d), fixed(d, d), fixed(1, half), fixed(1, half), fixed(1, d)],
        out_specs=[row_blk(d), row_blk(half), row_blk(half),
                   pl.BlockSpec((2, tm, half), lambda i: (DPROJ_GATE_SLOT // 2, i, 0)),
                   fixed(1, LANES), fixed(1, d), fixed(1, half), fixed(1, half), fixed(d, d)],
        out_shape=[jax.ShapeDtypeStruct((t_len, d), F32)] + [jax.ShapeDtypeStruct((t_len, half), F32)] * 2
                  + [jax.ShapeDtypeStruct((len(DPROJ_PIECE_OF_SLOT), t_len, half), F32),
                     jax.ShapeDtypeStruct((1, LANES), F32), jax.ShapeDtypeStruct((1, d), F32),
                     jax.ShapeDtypeStruct((1, half), F32), jax.ShapeDtypeStruct((1, half), F32),
                     jax.ShapeDtypeStruct((d, d), F32)],
        compiler_params=_params("arbitrary"),
    )(o_sb, o_gd, proj, proj, x, target, w_out, sbw, gdw, fw)


def _piece_of_slot(s):
    return jnp.where(s < DPROJ_GDN_SLOT, s, jnp.where(s < DPROJ_GATE_SLOT, s + 1,
                                                     jnp.where(s == DPROJ_GATE_SLOT, 3, 7)))


def _gw_in_call(h_t, dproj8):
    d, t_len = h_t.shape
    n_piece, _, pw = dproj8.shape

    def body(ht_ref, dp_ref, gw_ref):
        gw_ref[...] = jnp.dot(ht_ref[...], dp_ref[0].astype(MXU_DTYPE), preferred_element_type=F32)

    return pl.pallas_call(
        body, name="gw_in",
        grid=(n_piece,),
        in_specs=[pl.BlockSpec((d, t_len), lambda s: (0, 0)),
                  pl.BlockSpec((1, t_len, pw), lambda s: (s, 0, 0))],
        out_specs=pl.BlockSpec((d, pw), lambda s: (0, _piece_of_slot(s))),
        out_shape=jax.ShapeDtypeStruct((d, n_piece * pw), F32),
        compiler_params=_params("arbitrary"),
    )(h_t, dproj8)


def _gw_small_call(h_t, dsmall, tm=512):
    d, t_len = h_t.shape
    ns = dsmall.shape[1]

    def body(ht_ref, dp_ref, gw_ref):
        @pl.when(pl.program_id(0) == 0)
        def _():
            gw_ref[...] = jnp.zeros_like(gw_ref)

        gw_ref[...] += jnp.dot(ht_ref[...], dp_ref[...].astype(MXU_DTYPE), preferred_element_type=F32)

    return pl.pallas_call(
        body, name="gw_small",
        grid=(t_len // tm,),
        in_specs=[pl.BlockSpec((d, tm), lambda t: (0, t)),
                  pl.BlockSpec((tm, ns), lambda t: (t, 0))],
        out_specs=pl.BlockSpec((d, ns), lambda t: (0, 0)),
        out_shape=jax.ShapeDtypeStruct((d, ns), F32),
        compiler_params=_params("arbitrary"),
    )(h_t, dsmall)


def _dx_call(dproj8, dsmall, w_main, w_small, x, r, dx2, norm_w, chip_scatter=(), tm=256):
    t_len, d = x.shape
    n_piece, _, pw = dproj8.shape
    ns = dsmall.shape[1]
    nx = len(chip_scatter)
    steps = t_len // tm

    def body(*refs):
        dp_ref, ds_ref, wm_ref, ws_ref, x_ref, r_ref, dx2_ref, nw_ref = refs[:8]
        gx_ref, gnw_ref = refs[8 + nx:10 + nx]
        copies = lambda: _chip_copies(refs[8:8 + nx], refs[10 + nx:10 + 2 * nx], *refs[10 + 2 * nx:])
        if nx:
            pl.when(pl.program_id(0) == 0)(lambda: _start_all(copies()))

        @pl.when(pl.program_id(0) == 0)
        def _():
            gnw_ref[...] = jnp.zeros_like(gnw_ref)

        dh = lax.dot_general(ds_ref[...].astype(MXU_DTYPE), ws_ref[...], _NT, preferred_element_type=F32)
        for s, p in enumerate(DPROJ_PIECE_OF_SLOT):
            dh = dh + lax.dot_general(dp_ref[s].astype(MXU_DTYPE), wm_ref[:, p * pw:(p + 1) * pw], _NT,
                                      preferred_element_type=F32)
        xv, rv = x_ref[...], r_ref[...]
        dn = dh * nw_ref[...]
        gx_ref[...] = dx2_ref[...] + rv * dn - xv * ((rv * rv * rv) * jnp.mean(dn * xv, axis=-1, keepdims=True))
        gnw_ref[...] += jnp.sum(dh * xv * rv, axis=0, keepdims=True)
        if nx:
            pl.when(pl.program_id(0) == steps - 1)(lambda: _wait_all(copies()))

    return pl.pallas_call(
        body, name="dx",
        grid=(steps,),
        in_specs=[pl.BlockSpec((n_piece, tm, pw), lambda i: (0, i, 0)),
                  pl.BlockSpec((tm, ns), lambda i: (i, 0)),
                  pl.BlockSpec((d, n_piece * pw), lambda i: (0, 0)),
                  pl.BlockSpec((d, ns), lambda i: (0, 0)),
                  pl.BlockSpec((tm, d), lambda i: (i, 0)),
                  pl.BlockSpec((tm, 1), lambda i: (i, 0)),
                  pl.BlockSpec((tm, d), lambda i: (i, 0)),
                  pl.BlockSpec((1, d), lambda i: (0, 0))] + [_HBM] * nx,
        out_specs=[pl.BlockSpec((tm, d), lambda i: (i, 0)),
                   pl.BlockSpec((1, d), lambda i: (0, 0))] + [_HBM] * nx,
        out_shape=[jax.ShapeDtypeStruct((t_len, d), F32), jax.ShapeDtypeStruct((1, d), F32)]
                  + [jax.ShapeDtypeStruct(a.shape, a.dtype) for a in chip_scatter],
        scratch_shapes=_chip_semaphores(nx) if nx else [],
        compiler_params=_params("arbitrary"),
    )(dproj8, dsmall, w_main, w_small, x, r, dx2, norm_w, *chip_scatter)


def _exchange_call(name, srcs, per_peer):
    n = len(srcs)

    def body(*refs):
        src_refs, out_refs = refs[:n], refs[n:2 * n]
        copies = _direct_copies(src_refs, out_refs, *refs[2 * n:], per_peer)
        _start_all(copies)
        _wait_all(copies)

    hbm = pl.BlockSpec(memory_space=pl.ANY)
    return pl.pallas_call(
        body, name=name,
        in_specs=[hbm] * n, out_specs=[hbm] * n, out_shape=_direct_out_shapes(srcs, per_peer),
        scratch_shapes=_direct_semaphores(n),
    )(*srcs)


def _direct_out_shapes(srcs, per_peer):
    return [jax.ShapeDtypeStruct(s.shape if pp else (N_DEV,) + s.shape, s.dtype) for s, pp in zip(srcs, per_peer)]


def _direct_semaphores(n):
    return [pltpu.SemaphoreType.DMA((n * (N_DEV - 1),)), pltpu.SemaphoreType.DMA((n * (N_DEV - 1),)),
            pltpu.SemaphoreType.DMA((n,))]


def _direct_copies(src_refs, out_refs, send_sems, recv_sems, local_sems, per_peer):
    x, y, c = lax.axis_index("x"), lax.axis_index("y"), lax.axis_index("c")
    me = 4 * x + 2 * y + c
    local, remote = [], []
    for a in range(len(src_refs)):
        mine = src_refs[a].at[me] if per_peer[a] else src_refs[a]
        local.append(pltpu.make_async_copy(mine, out_refs[a].at[me], local_sems.at[a]))
    for k in range(1, N_DEV):
        kx, ky, kc = (k >> 2) & 1, (k >> 1) & 1, k & 1
        px = 1 - x if kx else x
        py = 1 - y if ky else y
        pc = 1 - c if kc else c
        peer = 4 * px + 2 * py + pc
        for a in range(len(src_refs)):
            sem = a * (N_DEV - 1) + (k - 1)
            remote.append(pltpu.make_async_remote_copy(
                src_ref=src_refs[a].at[peer] if per_peer[a] else src_refs[a], dst_ref=out_refs[a].at[me],
                send_sem=send_sems.at[sem], recv_sem=recv_sems.at[sem],
                device_id=(px, py, pc), device_id_type=pl.DeviceIdType.MESH))
    return local, remote


def _start_all(copies):
    local, remote = copies
    for cp in local + remote:
        cp.start()


def _wait_all(copies):
    local, remote = copies
    for cp in remote:
        cp.wait_send()
    for cp in remote:
        cp.wait_recv()
    for cp in local:
        cp.wait()


N_CHIPS = 4
_HBM = pl.BlockSpec(memory_space=pl.ANY)
_MESH = pl.DeviceIdType.MESH


def _gather_call(name, srcs):
    n = len(srcs)
    per = N_DEV - 1

    def body(*refs):
        src_refs, out_refs = refs[:n], refs[n:2 * n]
        send_sems, recv_sems, local_sems = refs[2 * n:]
        x, y, c = lax.axis_index("x"), lax.axis_index("y"), lax.axis_index("c")
        me, sibling = (x, y, c), (x, y, 1 - c)
        chips = [(1 - x, y), (x, 1 - y), (1 - x, 1 - y)]
        slot = lambda px, py, pc: 4 * px + 2 * py + pc

        def copy(a, k, block, to, from_src=False):
            rows = out_refs[a].at[slot(*block)]
            return pltpu.make_async_remote_copy(
                src_ref=src_refs[a] if from_src else rows, dst_ref=rows,
                send_sem=send_sems.at[a * per + k], recv_sem=recv_sems.at[a * per + k],
                device_id=to, device_id_type=_MESH)

        local = [pltpu.make_async_copy(src_refs[a], out_refs[a].at[slot(*me)], local_sems.at[a]) for a in range(n)]
        for cp in local:
            cp.start()
        first = []
        for a in range(n):
            first.append(copy(a, 0, me, sibling, True))
            first += [copy(a, 1 + j, me, (*chip, c), True) for j, chip in enumerate(chips)]
        for cp in first:
            cp.start()
        passed = []
        for j, chip in enumerate(chips):
            for a in range(n):
                copy(a, 1 + j, (*chip, c), me).wait_recv()
                fwd = copy(a, 4 + j, (*chip, c), sibling)
                fwd.start()
                passed.append(fwd)
        for a in range(n):
            copy(a, 0, sibling, me).wait_recv()
            for j, chip in enumerate(chips):
                copy(a, 4 + j, (*chip, 1 - c), me).wait_recv()
        for cp in first + passed:
            cp.wait_send()
        for cp in local:
            cp.wait()

    return pl.pallas_call(
        body, name=name,
        in_specs=[_HBM] * n, out_specs=[_HBM] * n,
        out_shape=[jax.ShapeDtypeStruct((N_DEV,) + s.shape, s.dtype) for s in srcs],
        scratch_shapes=[pltpu.SemaphoreType.DMA((n * per,)), pltpu.SemaphoreType.DMA((n * per,)),
                        pltpu.SemaphoreType.DMA((n,))],
    )(*srcs)


def _sibling_send_call(name, srcs):
    n = len(srcs)

    def body(*refs):
        src_refs, out_refs = refs[:n], refs[n:2 * n]
        send_sems, recv_sems = refs[2 * n:]
        x, y, c = lax.axis_index("x"), lax.axis_index("y"), lax.axis_index("c")
        copies = []
        for a in range(n):
            for ch in range(N_CHIPS):
                copies.append(pltpu.make_async_remote_copy(
                    src_ref=src_refs[a].at[2 * ch + (1 - c)], dst_ref=out_refs[a].at[ch],
                    send_sem=send_sems.at[a * N_CHIPS + ch], recv_sem=recv_sems.at[a * N_CHIPS + ch],
                    device_id=(x, y, 1 - c), device_id_type=_MESH))
        for cp in copies:
            cp.start()
        for cp in copies:
            cp.wait_send()
        for cp in copies:
            cp.wait_recv()

    return pl.pallas_call(
        body, name=name,
        in_specs=[_HBM] * n, out_specs=[_HBM] * n,
        out_shape=[jax.ShapeDtypeStruct((N_CHIPS,) + s.shape[1:], s.dtype) for s in srcs],
        scratch_shapes=[pltpu.SemaphoreType.DMA((n * N_CHIPS,)), pltpu.SemaphoreType.DMA((n * N_CHIPS,))],
    )(*srcs)


def _pair_sum_call(name, parts, from_sibling, tr):
    _, rows, cols = parts.shape

    def body(p_ref, s_ref, o_ref):
        o_ref[...] = (p_ref[...].astype(F32) + s_ref[...].astype(F32)).astype(o_ref.dtype)

    return pl.pallas_call(
        body, name=name,
        grid=(N_CHIPS, rows // tr),
        in_specs=[pl.BlockSpec((1, tr, cols), lambda ch, i: (2 * ch + lax.axis_index("c"), i, 0)),
                  pl.BlockSpec((1, tr, cols), lambda ch, i: (ch, i, 0))],
        out_specs=pl.BlockSpec((1, tr, cols), lambda ch, i: (ch, i, 0)),
        out_shape=jax.ShapeDtypeStruct((N_CHIPS, rows, cols), WIRE_DTYPE),
        compiler_params=_params("arbitrary", "arbitrary"),
    )(parts, from_sibling)


def _chip_exchange_call(name, srcs):
    n = len(srcs)

    def body(*refs):
        copies = _chip_copies(refs[:n], refs[n:2 * n], *refs[2 * n:])
        _start_all(copies)
        _wait_all(copies)

    return pl.pallas_call(
        body, name=name,
        in_specs=[_HBM] * n, out_specs=[_HBM] * n,
        out_shape=[jax.ShapeDtypeStruct(s.shape, s.dtype) for s in srcs],
        scratch_shapes=_chip_semaphores(n),
    )(*srcs)


def _chip_semaphores(n):
    per = N_CHIPS - 1
    return [pltpu.SemaphoreType.DMA((n * per,)), pltpu.SemaphoreType.DMA((n * per,)), pltpu.SemaphoreType.DMA((n,))]


def _chip_copies(src_refs, out_refs, send_sems, recv_sems, local_sems):
    per = N_CHIPS - 1
    x, y, c = lax.axis_index("x"), lax.axis_index("y"), lax.axis_index("c")
    mine = 2 * x + y
    chips = [(1 - x, y), (x, 1 - y), (1 - x, 1 - y)]
    n = len(src_refs)
    local = [pltpu.make_async_copy(src_refs[a].at[mine], out_refs[a].at[mine], local_sems.at[a]) for a in range(n)]
    remote = []
    for a in range(n):
        for j, (px, py) in enumerate(chips):
            remote.append(pltpu.make_async_remote_copy(
                src_ref=src_refs[a].at[2 * px + py], dst_ref=out_refs[a].at[mine],
                send_sem=send_sems.at[a * per + j], recv_sem=recv_sems.at[a * per + j],
                device_id=(px, py, c), device_id_type=_MESH))
    return local, remote


def _adam_call(name, parts, w, m, v, tr):
    rows, cols = w.shape
    n_slots = parts.shape[0]

    def body(p_ref, w_ref, m_ref, v_ref, g_ref, d_ref, nm_ref, nv_ref):
        g = p_ref[0].astype(F32)
        for s in range(1, n_slots):
            g = g + p_ref[s].astype(F32)
        m_new = ADAM_B1 * m_ref[...] + (1.0 - ADAM_B1) * g
        v_new = ADAM_B2 * v_ref[...] + (1.0 - ADAM_B2) * (g * g)
        m_hat = m_new / (1.0 - ADAM_B1 ** ADAM_STEP)
        v_hat = v_new / (1.0 - ADAM_B2 ** ADAM_STEP)
        g_ref[...] = g
        d_ref[...] = -ADAM_LR * (m_hat / (jnp.sqrt(v_hat) + ADAM_EPS) + ADAM_WD * w_ref[...])
        nm_ref[...] = m_new
        nv_ref[...] = v_new

    blk = pl.BlockSpec((tr, cols), lambda i: (i, 0))
    return pl.pallas_call(
        body, name=name,
        grid=(rows // tr,),
        in_specs=[pl.BlockSpec((n_slots, tr, cols), lambda i: (0, i, 0)), blk, blk, blk],
        out_specs=[blk] * 4,
        out_shape=[jax.ShapeDtypeStruct((rows, cols), F32)] * 4,
        compiler_params=_params("arbitrary"),
    )(parts, w, m, v)


N_PIECES = 8
PIECE = 512
SHARD_COLS = 513
SHARD_PAD = 640
RELAYOUT_ROWS = 256


def _from_shards_call(shards):
    _, d, _ = shards.shape
    tr = RELAYOUT_ROWS

    def body(p_ref, m_ref, s_ref):
        lane = lax.broadcasted_iota(jnp.int32, (tr, SHARD_PAD), 1)
        pad = jnp.zeros((tr, SHARD_PAD - SHARD_COLS), F32)
        sh = [jnp.concatenate([p_ref[s].astype(F32), pad], axis=1) for s in range(N_DEV)]
        for p in range(N_PIECES):
            y = sh[p] if p == 0 else pltpu.roll(sh[p], p, axis=1)
            if p > 0:
                y = jnp.where(lane < p, pltpu.roll(sh[p - 1], SHARD_PAD - (SHARD_COLS - p), axis=1), y)
            m_ref[:, p * PIECE:(p + 1) * PIECE] = y[:, :PIECE].astype(m_ref.dtype)
        first_gate = N_PIECES * PIECE - (N_DEV - 1) * SHARD_COLS
        s_ref[...] = pltpu.roll(sh[N_DEV - 1], SHARD_PAD - first_gate, axis=1)[:, :LANES].astype(s_ref.dtype)

    return pl.pallas_call(
        body, name="w_in_from_shards",
        grid=(d // tr,),
        in_specs=[pl.BlockSpec((N_DEV, tr, SHARD_COLS), lambda i: (0, i, 0))],
        out_specs=[pl.BlockSpec((tr, N_PIECES * PIECE), lambda i: (i, 0)), pl.BlockSpec((tr, LANES), lambda i: (i, 0))],
        out_shape=[jax.ShapeDtypeStruct((d, N_PIECES * PIECE), shards.dtype),
                   jax.ShapeDtypeStruct((d, LANES), shards.dtype)],
        compiler_params=_params("arbitrary"),
    )(shards)


def _to_shards_call(main, gates, out_dtype):
    d = main.shape[0]
    tr = RELAYOUT_ROWS

    def body(m_ref, s_ref, o_ref):
        for s in range(N_DEV):
            if s < N_DEV - 1:
                x = m_ref[:, s * PIECE:s * PIECE + SHARD_PAD]
            else:
                x = jnp.concatenate([m_ref[:, s * PIECE:(s + 1) * PIECE], s_ref[...]], axis=1)
            y = x if s == 0 else pltpu.roll(x, SHARD_PAD - s, axis=1)
            o_ref[s] = y[:, :SHARD_COLS].astype(out_dtype)

    return pl.pallas_call(
        body, name="w_in_to_shards",
        grid=(d // tr,),
        in_specs=[pl.BlockSpec((tr, N_PIECES * PIECE), lambda i: (i, 0)), pl.BlockSpec((tr, LANES), lambda i: (i, 0))],
        out_specs=pl.BlockSpec((N_DEV, tr, SHARD_COLS), lambda i: (0, i, 0)),
        out_shape=jax.ShapeDtypeStruct((N_DEV, d, SHARD_COLS), out_dtype),
        compiler_params=_params("arbitrary"),
    )(main, gates)


def _adamw(g, w, m, v):
    m_new = ADAM_B1 * m + (1.0 - ADAM_B1) * g
    v_new = ADAM_B2 * v + (1.0 - ADAM_B2) * (g * g)
    m_hat = m_new / (1.0 - ADAM_B1 ** ADAM_STEP)
    v_hat = v_new / (1.0 - ADAM_B2 ** ADAM_STEP)
    return -ADAM_LR * (m_hat / (jnp.sqrt(v_hat) + ADAM_EPS) + ADAM_WD * w), m_new, v_new


def _adam_small_call(parts, ws, ms, vs):
    n = len(ws)
    n_slots = parts.shape[0]

    def body(*refs):
        p_ref = refs[0]
        w_refs, m_refs, v_refs = refs[1:1 + n], refs[1 + n:1 + 2 * n], refs[1 + 2 * n:1 + 3 * n]
        loss_ref = refs[1 + 3 * n]
        outs = refs[2 + 3 * n:]
        g_all = p_ref[0]
        for s in range(1, n_slots):
            g_all = g_all + p_ref[s]
        loss_ref[...] = g_all[n:n + 1, 0:1]
        for r in range(n):
            size = w_refs[r].shape[1]
            g = g_all[r:r + 1, :size]
            delta, m_new, v_new = _adamw(g, w_refs[r][...], m_refs[r][...], v_refs[r][...])
            for kind, val in enumerate((g, delta, m_new, v_new)):
                outs[kind * n + r][...] = val

    vm = pl.BlockSpec(memory_space=pltpu.VMEM)
    shapes = [jax.ShapeDtypeStruct(w.shape, F32) for w in ws]
    return pl.pallas_call(
        body, name="adam_small",
        in_specs=[vm] * (1 + 3 * n), out_specs=[vm] * (1 + 4 * n),
        out_shape=[jax.ShapeDtypeStruct((1, 1), F32)] + shapes * 4,
    )(parts, *ws, *ms, *vs)


_SMALL_ROWS = ("norm1_w", "final_norm_w", "sb_norm_w", "gdn_norm_w", "gdn_A_log", "gdn_dt_bias", "loss")


def _pack_small(vals, width):
    rows = [jnp.pad(a.reshape(1, -1).astype(F32), ((0, 0), (0, width - a.size))) for a in vals]
    rows += [jnp.zeros((1, width), F32)] * (8 - len(rows))
    return jnp.concatenate(rows, axis=0)


def _device_step(x2d, tgt, w_main, w_small, w_out_full, conv_full, norm1_w, sb_norm_w, gdn_A_log, gdn_dt_bias,
                 gdn_norm_w, final_norm_w, distributed=False, w_in_chip_partials=None):
    t_len, d = x2d.shape
    n_chunks = t_len // CHUNK
    w_main, w_small, w_out_full = (a.astype(MXU_DTYPE) for a in (w_main, w_small, w_out_full))
    w_small_t = w_small[:, :2 * GDN_HEADS].T

    pad_lanes = lambda a, lo: jnp.pad(a.reshape(1, -1), ((0, 0), (lo, LANES - lo - a.size)))
    alog_l, dtb_l = pad_lanes(gdn_A_log, GDN_HEADS), pad_lanes(gdn_dt_bias, GDN_HEADS)
    alog_c, dtb_c = alog_l[:, :8].T, dtb_l[:, :8].T
    sbw = jnp.tile(sb_norm_w, (1, 512 // SB_HEAD_DIM))
    gdw = jnp.tile(gdn_norm_w, (1, 512 // GDN_HEAD_DIM))
    fw = final_norm_w.reshape(1, d)

    if distributed:
        proj, ps, pst, h_t, r1, w_out_g, conv_g = _inproj_call(
            x2d, norm1_w, w_main, w_small, w_small_t, gather=(w_out_full, conv_full))
        w_out_full = w_out_g.reshape(d, d)
        conv_full = conv_g.transpose(1, 0, 2).reshape(CONV_WIDTH, N_DEV * conv_g.shape[2])
    else:
        proj, ps, pst, h_t, r1 = _inproj_call(x2d, norm1_w, w_main, w_small, w_small_t)
    o_sb, sp_total, sb_blocks_run = _sb_fwd_call(proj, t_len)
    gact = _gdn_prep_call(proj, conv_full, t_len)
    beta_l, gcol_l, grow = _gdn_gates_call(ps, pst, alog_l, dtb_l, alog_c, dtb_c, t_len)
    gam_r = grow[GDN_HEADS:2 * GDN_HEADS].reshape(GDN_HEADS, n_chunks, 1, CHUNK)
    o_gd, s_all, t_all = _gdn_fwd_call(gact, beta_l, gcol_l, gam_r, t_len)

    (dx2, d_osb, d_ogd, dproj8, loss_p, g_fw, g_sbw, g_gdw, g_wout) = _post_call(
        o_sb, o_gd, proj, x2d, tgt, w_out_full, sbw, gdw, fw)

    dproj8 = _sb_bwd_call(proj, sp_total, sb_blocks_run, d_osb, dproj8, t_len)
    if distributed:
        d_gact3, d_gates, g_wout = _gdn_bwd_call(gact, beta_l, gcol_l, gam_r, s_all, t_all, d_ogd, t_len,
                                                 scatter=(g_wout.reshape(N_DEV, d // N_DEV, d),))
    else:
        d_gact3, d_gates = _gdn_bwd_call(gact, beta_l, gcol_l, gam_r, s_all, t_all, d_ogd, t_len)
    dproj8, g_conv = _gdn_prep_bwd_call(proj, conv_full, d_gact3, dproj8, t_len)
    dsmall, g_alog, g_dtb = _gdn_gates_bwd_call(ps, alog_l, dtb_l, d_gates, t_len)

    g_w_main = _gw_in_call(h_t, dproj8)
    g_w_small = _gw_small_call(h_t, dsmall)
    if distributed:
        grad_x, g_n1, g_w_in = _dx_call(dproj8, dsmall, w_main, w_small, x2d, r1, dx2, norm1_w,
                                        chip_scatter=(w_in_chip_partials(g_w_main, g_w_small),))
    else:
        grad_x, g_n1 = _dx_call(dproj8, dsmall, w_main, w_small, x2d, r1, dx2, norm1_w)
        g_w_in = (g_w_main, g_w_small)
    return (loss_p, grad_x, g_n1, g_w_in, g_sbw, g_conv, g_alog, g_dtb, g_gdw, g_wout, g_fw)


def kernel(x, norm1_w, w_in, sb_norm_w, gdn_conv_w, gdn_A_log, gdn_dt_bias, gdn_norm_w, w_out, final_norm_w, loss_target, m_norm1_w, m_w_in, m_sb_norm_w, m_gdn_conv_w, m_gdn_A_log, m_gdn_dt_bias, m_gdn_norm_w, m_w_out, m_final_norm_w, v_norm1_w, v_w_in, v_sb_norm_w, v_gdn_conv_w, v_gdn_A_log, v_gdn_dt_bias, v_gdn_norm_w, v_w_out, v_final_norm_w):
    d = x.shape[2]
    shard_cols = w_in.shape[2]
    conv_cols = gdn_conv_w.shape[2]

    (w_in_g,) = _gather_call("gather_weights", [w_in[0].astype(WIRE_DTYPE)])
    w_main, w_small = _from_shards_call(w_in_g)

    def w_in_chip_partials(g_w_main, g_w_small):
        parts = _to_shards_call(g_w_main, g_w_small, WIRE_DTYPE)
        (from_sibling,) = _sibling_send_call("grads_to_sibling", [parts])
        return _pair_sum_call("pair_sum_w_in", parts, from_sibling, 256)

    (loss_p, grad_x, g_n1, p_w_in, g_sbw, g_conv, g_alog, g_dtb, g_gdw, p_wout, g_fw) = _device_step(
        x[0], loss_target[0], w_main, w_small, w_out[0].astype(WIRE_DTYPE), gdn_conv_w[0], norm1_w, sb_norm_w,
        gdn_A_log, gdn_dt_bias, gdn_norm_w, final_norm_w, distributed=True, w_in_chip_partials=w_in_chip_partials)

    g_conv_parts = g_conv.reshape(CONV_WIDTH, N_DEV, conv_cols).transpose(1, 0, 2)
    fold = lambda a, group: a.reshape(-1, group).sum(axis=0)
    small_g = _pack_small([g_n1, g_fw, fold(g_sbw, SB_HEAD_DIM), fold(g_gdw, GDN_HEAD_DIM),
                           g_alog[0, GDN_HEADS:2 * GDN_HEADS], g_dtb[0, GDN_HEADS:2 * GDN_HEADS],
                           loss_p[0, :1]], d)
    p_small, p_conv = _exchange_call("exchange_small", [small_g, g_conv_parts], [False, True])

    r_w_in = _adam_call("adam_w_in", p_w_in, w_in[0], m_w_in[0], v_w_in[0], 256)
    r_wout = _adam_call("adam_w_out", p_wout, w_out[0], m_w_out[0], v_w_out[0], d // N_DEV)
    r_conv = _adam_call("adam_conv", p_conv, gdn_conv_w[0], m_gdn_conv_w[0], v_gdn_conv_w[0], CONV_WIDTH)

    row = lambda a: a.reshape(1, -1)
    n_small = len(_SMALL_ROWS) - 1
    r_small = _adam_small_call(
        p_small,
        [norm1_w, row(final_norm_w), sb_norm_w, gdn_norm_w, gdn_A_log, gdn_dt_bias],
        [m_norm1_w, row(m_final_norm_w), m_sb_norm_w, m_gdn_norm_w, m_gdn_A_log, m_gdn_dt_bias],
        [v_norm1_w, row(v_final_norm_w), v_sb_norm_w, v_gdn_norm_w, v_gdn_A_log, v_gdn_dt_bias])

    def small_out(kind, name):
        out = r_small[1 + kind * n_small + _SMALL_ROWS.index(name)]
        return out.reshape(final_norm_w.shape) if name == "final_norm_w" else out

    def outputs(kind):
        return (small_out(kind, "norm1_w"), r_w_in[kind][None], small_out(kind, "sb_norm_w"), r_conv[kind][None],
                small_out(kind, "gdn_A_log"), small_out(kind, "gdn_dt_bias"), small_out(kind, "gdn_norm_w"),
                r_wout[kind][None], small_out(kind, "final_norm_w"))

    return (r_small[0][0, 0], grad_x[None], *outputs(0), *outputs(1), *outputs(2), *outputs(3))
```

```python
import functools

import jax
import jax.numpy as jnp
from jax import lax
from jax.experimental import pallas as pl
from jax.experimental.pallas import tpu as pltpu

F32 = jnp.float32
MXU_DTYPE = jnp.bfloat16
WIRE_DTYPE = jnp.bfloat16
EXACT = lax.Precision.HIGHEST
EPS = 1e-6
N_DEV = 8
SB_HEAD_DIM = 64
GDN_HEAD_DIM = 128
GDN_HEADS = 4
GDN_CHUNKS_PER_STEP = 4
GDN_BWD_GROUP = 1
CHUNK = 64
CONV_WIDTH = 4
LANES = 128
SB_BLOCK = 128
SB_BQ = 256
VMEM_LIMIT_BYTES = 56 * 1024 * 1024

DPROJ_PIECE_OF_SLOT = (0, 1, 2, 4, 5, 6, 3, 7)
DPROJ_SB_SLOT, DPROJ_GDN_SLOT, DPROJ_GATE_SLOT = 0, 3, 6

ADAM_LR = 0.001
ADAM_B1 = 0.9
ADAM_B2 = 0.999
ADAM_EPS = 1e-08
ADAM_WD = 0.01
ADAM_STEP = 10

_NN = (((1,), (0,)), ((), ()))
_NT = (((1,), (1,)), ((), ()))
_TN = (((0,), (0,)), ((), ()))
_BNN = (((2,), (1,)), ((0,), (0,)))
_BNT = (((2,), (2,)), ((0,), (0,)))
_BTN = (((1,), (1,)), ((0,), (0,)))


def _mm(a, b):
    return jnp.dot(a.astype(MXU_DTYPE), b.astype(MXU_DTYPE), preferred_element_type=F32)


def _mm_nt(a, b):
    return lax.dot_general(a.astype(MXU_DTYPE), b.astype(MXU_DTYPE), _NT, preferred_element_type=F32)


def _mm_tn(a, b):
    return lax.dot_general(a.astype(MXU_DTYPE), b.astype(MXU_DTYPE), _TN, preferred_element_type=F32)


def _mx(a, b):
    return jnp.dot(a, b, precision=EXACT, preferred_element_type=F32)


def _mx_nt(a, b):
    return lax.dot_general(a, b, _NT, precision=EXACT, preferred_element_type=F32)


def _mx_tn(a, b):
    return lax.dot_general(a, b, _TN, precision=EXACT, preferred_element_type=F32)


def _split(x):
    hi = x.astype(MXU_DTYPE)
    return hi, (x - hi.astype(F32)).astype(MXU_DTYPE)


def _m3_general(a, b, dims):
    ah, al = _split(a)
    bh, bl = _split(b)
    dot = lambda x, y: lax.dot_general(x, y, dims, preferred_element_type=F32)
    (contract, _), (batch, _) = dims
    free = [ax for ax in range(a.ndim) if ax not in contract and ax not in batch][0]
    m = a.shape[free]
    both = dot(jnp.concatenate([ah, al], axis=free), bh)
    out_axis = len(batch)
    hi_part = lax.slice_in_dim(both, 0, m, axis=out_axis)
    lo_part = lax.slice_in_dim(both, m, 2 * m, axis=out_axis)
    return hi_part + (dot(ah, bl) + lo_part)


def _m3(a, b):
    return _m3_general(a, b, _NN)


def _m3_nt(a, b):
    return _m3_general(a, b, _NT)


def _m3_tn(a, b):
    return _m3_general(a, b, _TN)


def _sigmoid(z):
    return 1.0 / (1.0 + jnp.exp(-z))


def _softplus(z):
    return jnp.maximum(z, 0.0) + jnp.log(1.0 + jnp.exp(-jnp.abs(z)))


def _params(*semantics):
    return pltpu.CompilerParams(dimension_semantics=semantics, vmem_limit_bytes=VMEM_LIMIT_BYTES)


def _inproj_call(x, norm_w, w_main, w_small, w_small_t, gather=(), tm=256):
    t_len, d = x.shape
    n = w_main.shape[1]
    ns = w_small.shape[1]
    nst = w_small_t.shape[0]
    ng = len(gather)
    steps = t_len // tm

    def body(*refs):
        x_ref, nw_ref, wm_ref, ws_ref, wst_ref = refs[:5]
        pm_ref, ps_ref, pst_ref, ht_ref, r_ref = refs[5 + ng:10 + ng]
        copies = lambda: _direct_copies(refs[5:5 + ng], refs[10 + ng:10 + 2 * ng], *refs[10 + 2 * ng:], (False,) * ng)
        if ng:
            pl.when(pl.program_id(0) == 0)(lambda: _start_all(copies()))
        xv = x_ref[...]
        r = lax.rsqrt(jnp.mean(xv * xv, axis=-1, keepdims=True) + EPS)
        h = xv * r * nw_ref[...]
        hb = h.astype(MXU_DTYPE)
        for n0 in range(0, n, 512):
            pm_ref[:, n0:n0 + 512] = jnp.dot(hb, wm_ref[:, n0:n0 + 512], preferred_element_type=F32)
        ps_ref[...] = jnp.dot(hb, ws_ref[...], preferred_element_type=F32)
        pst_ref[...] = lax.dot_general(wst_ref[...], hb, _NT, preferred_element_type=F32)
        ht_ref[...] = h.T.astype(MXU_DTYPE)
        r_ref[...] = r
        if ng:
            pl.when(pl.program_id(0) == steps - 1)(lambda: _wait_all(copies()))

    return pl.pallas_call(
        body, name="inproj",
        grid=(steps,),
        in_specs=[pl.BlockSpec((tm, d), lambda i: (i, 0)),
                  pl.BlockSpec((1, d), lambda i: (0, 0)),
                  pl.BlockSpec((d, n), lambda i: (0, 0)),
                  pl.BlockSpec((d, ns), lambda i: (0, 0)),
                  pl.BlockSpec((nst, d), lambda i: (0, 0))] + [_HBM] * ng,
        out_specs=[pl.BlockSpec((tm, n), lambda i: (i, 0)),
                   pl.BlockSpec((tm, ns), lambda i: (i, 0)),
                   pl.BlockSpec((nst, tm), lambda i: (0, i)),
                   pl.BlockSpec((d, tm), lambda i: (0, i)),
                   pl.BlockSpec((tm, 1), lambda i: (i, 0))] + [_HBM] * ng,
        out_shape=[jax.ShapeDtypeStruct((t_len, n), F32),
                   jax.ShapeDtypeStruct((t_len, ns), F32),
                   jax.ShapeDtypeStruct((nst, t_len), F32),
                   jax.ShapeDtypeStruct((d, t_len), MXU_DTYPE),
                   jax.ShapeDtypeStruct((t_len, 1), F32)] + _direct_out_shapes(gather, (False,) * ng),
        scratch_shapes=_direct_semaphores(ng) if ng else [],
        compiler_params=_params("arbitrary"),
    )(x, norm_w, w_main, w_small, w_small_t, *gather)


def _running_sum_mm(x, tri):
    hi = x.astype(MXU_DTYPE)
    lo = (x - hi.astype(F32)).astype(MXU_DTYPE)
    return jnp.dot(hi, tri, preferred_element_type=F32) + jnp.dot(lo, tri, preferred_element_type=F32)


def _sb_iotas():
    row_i = lax.broadcasted_iota(jnp.int32, (SB_BQ, SB_BLOCK), 0)
    col_i = lax.broadcasted_iota(jnp.int32, (SB_BQ, SB_BLOCK), 1)
    sq_r = lax.broadcasted_iota(jnp.int32, (SB_BLOCK, SB_BLOCK), 0)
    sq_c = lax.broadcasted_iota(jnp.int32, (SB_BLOCK, SB_BLOCK), 1)
    return row_i, col_i, sq_r, sq_c


SB_DIAG_BLOCKS = SB_BQ // SB_BLOCK
SB_EXP_FLOOR = -110.0


def _sb_keys_descending(qi, tile, carry, z_bounds, n_heads, has_free):
    group = SB_DIAG_BLOCKS
    n_free = group * qi
    diag = list(range(group - 1, -1, -1))
    carry = tile([n_free + j for j in diag], [True] * group, carry, [j * SB_BLOCK for j in diag])

    def largest_exponent(c):
        worst = jnp.max(z_bounds[0] - c[1])
        for h in range(1, n_heads):
            worst = jnp.maximum(worst, jnp.max(z_bounds[h] - c[1 + h]))
        return worst

    always = group if has_free else 0

    def cond(state):
        return (state[0] < n_free) & ((state[1] > SB_EXP_FLOOR) | (state[0] < always))

    def body(state):
        first = n_free - 1 - state[0]
        c = tile([first - j for j in range(group)], [False] * group, state[2:])
        return (state[0] + group, largest_exponent(c), *c)

    out = lax.while_loop(cond, body, (jnp.int32(0), largest_exponent(carry), *carry))
    return out[2:], out[0]


def _sb_keys_ascending(qi, n_run, tile, carry, has_free):
    group = SB_DIAG_BLOCKS
    n_free = group * qi
    diag = list(range(group))
    kjs, los, masked = [n_free + j for j in diag], [j * SB_BLOCK for j in diag], [True] * group
    if has_free:
        early = lambda s: [n_free - n_run + group * s + j for j in range(group)]
        carry = lax.fori_loop(0, n_run // group - 1, lambda s, c: tile(early(s), [False] * group, c), carry)
        kjs, los, masked = [n_free - group + j for j in range(group)] + kjs, [0] * group + los, [False] * group + masked
    return tile(kjs, masked, carry, los)


def _sb_fwd_call(proj, t_len):
    nq = t_len // SB_BQ
    scale = float(SB_HEAD_DIM) ** -0.5
    n_pairs = 512 // LANES
    per_pair = LANES // SB_HEAD_DIM

    def body(q_ref, k_ref, v_ref, o_ref, st_ref, nrun_ref):
        lane = lax.broadcasted_iota(jnp.int32, (1, LANES), 1)
        row_i, col_i, sq_r, sq_c = _sb_iotas()
        ge = (sq_r >= sq_c).astype(MXU_DTYPE)
        hms = [((lane // SB_HEAD_DIM) == hh).astype(F32) for hh in range(per_pair)]
        k_sq = k_ref[...] * k_ref[...]
        k_norms = [jnp.sqrt(jnp.max(jnp.sum(k_sq * hm, axis=-1, keepdims=True))) * (1.02 * scale) for hm in hms]

        def q_block(qi, has_free):
            r0 = qi * SB_BQ if isinstance(qi, int) else pl.multiple_of(qi * SB_BQ, SB_BQ)
            rows = pl.ds(r0, SB_BQ)
            q_all = q_ref[rows, :]
            qms = [(q_all * (hm * scale)).astype(MXU_DTYPE) for hm in hms]
            z_bounds = [jnp.sqrt(jnp.sum(q_all * q_all * hm, axis=-1, keepdims=True)) * kn
                        for hm, kn in zip(hms, k_norms)]

            def tile(kjs, masked, kc, los=None):
                heads = range(per_pair)
                los = los or [0] * len(kjs)
                pairs = [(t, h) for t in range(len(kjs)) for h in heads]
                add_rows = lambda full, lo, part: full + part if lo == 0 else jnp.concatenate(
                    [full[:lo], full[lo:] + part], axis=0)
                acc, cs = kc[0], list(kc[1:])
                s0s = [kj * SB_BLOCK if isinstance(kj, int) else pl.multiple_of(kj * SB_BLOCK, SB_BLOCK) for kj in kjs]
                kbs = [k_ref[pl.ds(s0, SB_BLOCK), :].astype(MXU_DTYPE) for s0 in s0s]
                v_alls = [v_ref[pl.ds(s0, SB_BLOCK), :] for s0 in s0s]
                vms = {(t, h): (v_alls[t] * hms[h]).astype(MXU_DTYPE) for t, h in pairs}
                zs = {(t, h): lax.dot_general(qms[h][los[t]:], kbs[t], _NT, preferred_element_type=F32)
                      for t, h in pairs}
                masks = [(col_i[lo:] + s0) < (row_i[lo:] + r0) if m else None for m, lo, s0 in zip(masked, los, s0s)]
                keep = lambda t, a: a if masks[t] is None else jnp.where(masks[t], a, 0.0)
                sps = {(t, h): keep(t, _softplus(zs[t, h])) for t, h in pairs}
                sums = {p: _running_sum_mm(sps[p], ge) for p in pairs}
                mass = {}
                for t, h in pairs:
                    mass[t, h] = cs[h] if t == 0 else add_rows(
                        mass[t - 1, h], los[t - 1], jnp.sum(sps[t - 1, h], axis=-1, keepdims=True))
                ws = {(t, h): keep(t, jnp.exp(zs[t, h] - (sums[t, h] + mass[t, h][los[t]:]))) for t, h in pairs}
                for t, h in pairs:
                    acc = add_rows(acc, los[t], jnp.dot(ws[t, h].astype(MXU_DTYPE), vms[t, h],
                                                        preferred_element_type=F32))
                last = len(kjs) - 1
                cs = [add_rows(mass[last, h], los[last], jnp.sum(sps[last, h], axis=-1, keepdims=True)) for h in heads]
                return (acc, *cs)

            zero_col = jnp.zeros((SB_BQ, 1), F32)
            out, n_run = _sb_keys_descending(
                qi, tile, (jnp.zeros((SB_BQ, LANES), F32),) + (zero_col,) * per_pair, z_bounds, per_pair, has_free)
            o_ref[rows, :] = out[0]
            for hh in range(per_pair):
                st_ref[hh, rows, :] = out[1 + hh]
            nrun_ref[pl.program_id(0), qi] = n_run

        q_block(0, False)
        lax.fori_loop(1, nq, lambda qi, carry: (q_block(qi, True), carry)[1], 0)

    return pl.pallas_call(
        body, name="sb_fwd",
        grid=(n_pairs,),
        in_specs=[pl.BlockSpec((t_len, LANES), lambda p: (0, p)),
                  pl.BlockSpec((t_len, LANES), lambda p: (0, n_pairs + p)),
                  pl.BlockSpec((t_len, LANES), lambda p: (0, 2 * n_pairs + p))],
        out_specs=[pl.BlockSpec((t_len, LANES), lambda p: (0, p)),
                   pl.BlockSpec((per_pair, t_len, 1), lambda p: (p, 0, 0)),
                   pl.BlockSpec(memory_space=pltpu.SMEM)],
        out_shape=[jax.ShapeDtypeStruct((t_len, 512), F32),
                   jax.ShapeDtypeStruct((n_pairs * per_pair, t_len, 1), F32),
                   jax.ShapeDtypeStruct((n_pairs, nq), jnp.int32)],
        compiler_params=_params("arbitrary"),
    )(proj, proj, proj)


def _sb_bwd_call(proj, sp_total, n_run_all, d_o, dproj, t_len):
    nq = t_len // SB_BQ
    scale = float(SB_HEAD_DIM) ** -0.5
    n_pairs = 512 // LANES
    per_pair = LANES // SB_HEAD_DIM

    def body(q_ref, k_ref, v_ref, st_ref, nrun_ref, do_ref, dproj_in_ref, d_ref):
        lane = lax.broadcasted_iota(jnp.int32, (1, LANES), 1)
        row_i, col_i, sq_r, sq_c = _sb_iotas()
        lt = (sq_r < sq_c).astype(MXU_DTYPE)
        le = (sq_r <= sq_c).astype(MXU_DTYPE)
        hms = [((lane // SB_HEAD_DIM) == hh).astype(F32) for hh in range(per_pair)]
        d_ref[1] = jnp.zeros((t_len, LANES), F32)
        d_ref[2] = jnp.zeros((t_len, LANES), F32)

        def q_block(qi, has_free):
            r0 = qi * SB_BQ if isinstance(qi, int) else pl.multiple_of(qi * SB_BQ, SB_BQ)
            rows = pl.ds(r0, SB_BQ)
            q_all, do_all = q_ref[rows, :], do_ref[rows, :]
            qms = [(q_all * (hm * scale)).astype(MXU_DTYPE) for hm in hms]
            doms = [(do_all * hm).astype(MXU_DTYPE) for hm in hms]
            totals = [st_ref[hh, rows, :] for hh in range(per_pair)]

            def tile(kjs, masked, kc, los=None):
                heads = range(per_pair)
                los = los or [0] * len(kjs)
                pairs = [(t, h) for t in range(len(kjs)) for h in heads]
                add_rows = lambda full, lo, part: full + part if lo == 0 else jnp.concatenate(
                    [full[:lo], full[lo:] + part], axis=0)
                rsum = lambda a: jnp.sum(a, axis=-1, keepdims=True)
                dq, cls, gls = kc[0], list(kc[1:1 + per_pair]), list(kc[1 + per_pair:])
                s0s = [kj * SB_BLOCK if isinstance(kj, int) else pl.multiple_of(kj * SB_BLOCK, SB_BLOCK) for kj in kjs]
                k_alls = [k_ref[pl.ds(s0, SB_BLOCK), :] for s0 in s0s]
                v_alls = [v_ref[pl.ds(s0, SB_BLOCK), :] for s0 in s0s]
                kbs = [k_all.astype(MXU_DTYPE) for k_all in k_alls]
                vms = {(t, h): (v_alls[t] * hms[h]).astype(MXU_DTYPE) for t, h in pairs}
                kms = {(t, h): (k_alls[t] * (hms[h] * scale)).astype(MXU_DTYPE) for t, h in pairs}
                q_live = {(t, h): qms[h][los[t]:] for t, h in pairs}
                do_live = {(t, h): doms[h][los[t]:] for t, h in pairs}
                zs = {p: lax.dot_general(q_live[p], kbs[p[0]], _NT, preferred_element_type=F32) for p in pairs}
                das = {p: lax.dot_general(do_live[p], vms[p], _NT, preferred_element_type=F32) for p in pairs}
                masks = [(col_i[lo:] + s0) < (row_i[lo:] + r0) if m else None for m, lo, s0 in zip(masked, los, s0s)]
                keep = lambda t, a: a if masks[t] is None else jnp.where(masks[t], a, 0.0)
                sp_alls = {p: _softplus(zs[p]) for p in pairs}
                sps = {(t, h): keep(t, sp_alls[t, h]) for t, h in pairs}
                lefts = {p: _running_sum_mm(sps[p], lt) for p in pairs}
                cl = {}
                for t, h in pairs:
                    cl[t, h] = cls[h] if t == 0 else add_rows(cl[t - 1, h], los[t - 1], rsum(sps[t - 1, h]))
                ws = {(t, h): keep(t, jnp.exp(zs[t, h] - ((totals[h] - cl[t, h])[los[t]:] - lefts[t, h])))
                      for t, h in pairs}
                gs = {p: das[p] * ws[p] for p in pairs}
                g_sums = {p: _running_sum_mm(gs[p], le) for p in pairs}
                gl = {}
                for t, h in pairs:
                    gl[t, h] = gls[h] if t == 0 else add_rows(gl[t - 1, h], los[t - 1], rsum(gs[t - 1, h]))
                dzs = {(t, h): keep(t, gs[t, h] - jnp.exp(zs[t, h] - sp_alls[t, h]) * (gl[t, h][los[t]:] + g_sums[t, h])
                               ).astype(MXU_DTYPE) for t, h in pairs}
                for t in range(len(kjs)):
                    dk_t = jnp.zeros((SB_BLOCK, LANES), F32)
                    dv_t = jnp.zeros((SB_BLOCK, LANES), F32)
                    for h in heads:
                        dq = add_rows(dq, los[t], jnp.dot(dzs[t, h], kms[t, h], preferred_element_type=F32))
                        dk_t = dk_t + lax.dot_general(dzs[t, h], q_live[t, h], _TN, preferred_element_type=F32)
                        dv_t = dv_t + lax.dot_general(ws[t, h].astype(MXU_DTYPE), do_live[t, h], _TN,
                                                      preferred_element_type=F32)
                    d_ref[1, pl.ds(s0s[t], SB_BLOCK), :] += dk_t
                    d_ref[2, pl.ds(s0s[t], SB_BLOCK), :] += dv_t
                last = len(kjs) - 1
                cls = [add_rows(cl[last, h], los[last], rsum(sps[last, h])) for h in heads]
                gls = [add_rows(gl[last, h], los[last], rsum(gs[last, h])) for h in heads]
                return (dq, *cls, *gls)

            zero_col = jnp.zeros((SB_BQ, 1), F32)
            out = _sb_keys_ascending(qi, nrun_ref[pl.program_id(0), qi], tile,
                                     (jnp.zeros((SB_BQ, LANES), F32),) + (zero_col,) * (2 * per_pair), has_free)
            d_ref[0, rows, :] = out[0]

        q_block(0, False)
        lax.fori_loop(1, nq, lambda qi, carry: (q_block(qi, True), carry)[1], 0)

    col = lambda off: pl.BlockSpec((t_len, LANES), lambda p: (0, off + p))
    return pl.pallas_call(
        body, name="sb_bwd",
        grid=(n_pairs,),
        in_specs=[col(0), col(n_pairs), col(2 * n_pairs),
                  pl.BlockSpec((per_pair, t_len, 1), lambda p: (p, 0, 0)),
                  pl.BlockSpec(memory_space=pltpu.SMEM), col(0), _HBM],
        out_specs=pl.BlockSpec((3, t_len, LANES), lambda p: (DPROJ_SB_SLOT // 3, 0, p)),
        out_shape=jax.ShapeDtypeStruct(dproj.shape, dproj.dtype),
        input_output_aliases={6: 0},
        compiler_params=_params("arbitrary"),
    )(proj, proj, proj, sp_total, n_run_all, d_o, dproj)


def _conv_taps(xin, rows, t_len):
    taps = []
    for i in range(CONV_WIDTH):
        shift = CONV_WIDTH - 1 - i
        if shift == 0:
            taps.append(xin)
        else:
            taps.append(jnp.where(rows >= shift, pltpu.roll(xin, shift, axis=0), 0.0))
    return taps


def _gdn_prep_body_common(x_ref, w_ref, t_len):
    j = pl.program_id(0)
    xin = x_ref[...]
    rows = lax.broadcasted_iota(jnp.int32, (t_len, LANES), 0)
    taps = _conv_taps(xin, rows, t_len)
    pre = taps[0] * w_ref[0:1, :]
    for i in range(1, CONV_WIDTH):
        pre = pre + taps[i] * w_ref[i:i + 1, :]
    sg = _sigmoid(pre)
    act = pre * sg
    is_qk = j < 2 * GDN_HEADS
    nrm = jnp.where(is_qk, lax.rsqrt(jnp.sum(act * act, axis=-1, keepdims=True) + EPS), 1.0)
    sc = jnp.where(j < GDN_HEADS, float(GDN_HEAD_DIM) ** -0.5, 1.0)
    return j, rows, taps, pre, sg, act, is_qk, nrm, sc


def _gdn_prep_call(proj, conv_w, t_len):
    first = 2048 // LANES

    def body(x_ref, w_ref, out_ref):
        _, _, _, _, _, act, _, nrm, sc = _gdn_prep_body_common(x_ref, w_ref, t_len)
        out_ref[...] = act * nrm * sc

    return pl.pallas_call(
        body, name="gdn_prep",
        grid=(3 * GDN_HEADS,),
        in_specs=[pl.BlockSpec((t_len, LANES), lambda j: (0, first + j)),
                  pl.BlockSpec((CONV_WIDTH, LANES), lambda j: (0, j))],
        out_specs=pl.BlockSpec((t_len, LANES), lambda j: (0, j)),
        out_shape=jax.ShapeDtypeStruct((t_len, 3 * 512), F32),
        compiler_params=_params("arbitrary"),
    )(proj, conv_w)


def _gdn_prep_bwd_call(proj, conv_w, d_act3, dproj, t_len):
    first = 2048 // LANES

    def body(x_ref, w_ref, d_ref, dproj_in_ref, dx_ref, dw_ref):
        _, rows, taps, pre, sg, act, is_qk, nrm, sc = _gdn_prep_body_common(x_ref, w_ref, t_len)
        d_out = d_ref[0]
        dn = d_out * sc
        d_norm = nrm * dn - act * (nrm * nrm * nrm) * jnp.sum(dn * act, axis=-1, keepdims=True)
        d_act = jnp.where(is_qk, d_norm, d_out)
        d_pre = d_act * sg * (1.0 + pre * (1.0 - sg))
        dx = d_pre * w_ref[CONV_WIDTH - 1:CONV_WIDTH, :]
        dw_ref[CONV_WIDTH - 1:CONV_WIDTH, :] = jnp.sum(d_pre * taps[CONV_WIDTH - 1], axis=0, keepdims=True)
        for i in range(CONV_WIDTH - 1):
            shift = CONV_WIDTH - 1 - i
            up = jnp.where(rows < t_len - shift, pltpu.roll(d_pre, t_len - shift, axis=0), 0.0)
            dx = dx + up * w_ref[i:i + 1, :]
            dw_ref[i:i + 1, :] = jnp.sum(d_pre * taps[i], axis=0, keepdims=True)
        dx_ref[0] = dx

    return pl.pallas_call(
        body, name="gdn_prep_bwd",
        grid=(3 * GDN_HEADS,),
        in_specs=[pl.BlockSpec((t_len, LANES), lambda j: (0, first + j)),
                  pl.BlockSpec((CONV_WIDTH, LANES), lambda j: (0, j)),
                  pl.BlockSpec((1, t_len, LANES), lambda j: (j // GDN_HEADS, 0, j % GDN_HEADS)), _HBM],
        out_specs=[pl.BlockSpec((1, t_len, LANES), lambda j: (DPROJ_GDN_SLOT + j // GDN_HEADS, 0, j % GDN_HEADS)),
                   pl.BlockSpec((CONV_WIDTH, LANES), lambda j: (0, j))],
        out_shape=[jax.ShapeDtypeStruct(dproj.shape, dproj.dtype),
                   jax.ShapeDtypeStruct((CONV_WIDTH, 3 * 512), F32)],
        input_output_aliases={3: 0},
        compiler_params=_params("arbitrary"),
    )(proj, conv_w, d_act3, dproj)


def _chunk_cumsum_matrix():
    r = lax.broadcasted_iota(jnp.int32, (LANES, LANES), 0)
    c = lax.broadcasted_iota(jnp.int32, (LANES, LANES), 1)
    return ((r <= c) & ((r // CHUNK) == (c // CHUNK))).astype(F32)


def _gdn_gates_call(ps, pst, alog_l, dtb_l, alog_c, dtb_c, t_len):
    def body(ps_ref, pst_ref, al_ref, dl_ref, ac_ref, dc_ref, beta_ref, gcol_ref, grow_ref):
        upper = _chunk_cumsum_matrix()
        lower = upper.T
        psv = ps_ref[...]
        beta_ref[...] = _sigmoid(psv)
        g_l = -jnp.exp(al_ref[...]) * _softplus(psv + dl_ref[...])
        g_r = -jnp.exp(ac_ref[...]) * _softplus(pst_ref[...] + dc_ref[...])
        for w in range(t_len // LANES):
            sl = slice(w * LANES, (w + 1) * LANES)
            gcol_ref[sl, :] = _mx(lower, g_l[sl, :])
            grow_ref[:, sl] = _mx(g_r[:, sl], upper)

    vm = pl.BlockSpec(memory_space=pltpu.VMEM)
    return pl.pallas_call(
        body, name="gdn_gates",
        in_specs=[vm] * 6, out_specs=[vm] * 3,
        out_shape=[jax.ShapeDtypeStruct((t_len, LANES), F32),
                   jax.ShapeDtypeStruct((t_len, LANES), F32),
                   jax.ShapeDtypeStruct((8, t_len), F32)],
        compiler_params=pltpu.CompilerParams(vmem_limit_bytes=VMEM_LIMIT_BYTES),
    )(ps, pst, alog_l, dtb_l, alog_c, dtb_c)


def _gdn_gates_bwd_call(ps, alog_l, dtb_l, d_l, t_len):
    def body(ps_ref, al_ref, dl_ref, d_ref, dps_ref, gal_ref, gdt_ref):
        lane = lax.broadcasted_iota(jnp.int32, (1, LANES), 1)
        psv = ps_ref[...]
        dv = d_ref[...]
        beta = _sigmoid(psv)
        ea = jnp.exp(al_ref[...])
        arg = psv + dl_ref[...]
        g = -ea * _softplus(arg)
        d_a = dv * (-ea) * _sigmoid(arg)
        is_a = (lane >= GDN_HEADS) & (lane < 2 * GDN_HEADS)
        dps_ref[...] = jnp.where(lane < GDN_HEADS, dv * beta * (1.0 - beta), jnp.where(is_a, d_a, 0.0))
        gdt_ref[...] = jnp.where(is_a, jnp.sum(d_a, axis=0, keepdims=True), 0.0)
        gal_ref[...] = jnp.where(is_a, jnp.sum(dv * g, axis=0, keepdims=True), 0.0)

    vm = pl.BlockSpec(memory_space=pltpu.VMEM)
    return pl.pallas_call(
        body, name="gdn_gates_bwd",
        in_specs=[vm] * 4, out_specs=[vm] * 3,
        out_shape=[jax.ShapeDtypeStruct((t_len, LANES), F32),
                   jax.ShapeDtypeStruct((1, LANES), F32),
                   jax.ShapeDtypeStruct((1, LANES), F32)],
        compiler_params=pltpu.CompilerParams(vmem_limit_bytes=VMEM_LIMIT_BYTES),
    )(ps, alog_l, dtb_l, d_l)


def _bm(a, b):
    return _m3_general(a, b, _BNN)


def _bm_nt(a, b):
    return _m3_general(a, b, _BNT)


def _bm_tn(a, b):
    return _m3_general(a, b, _BTN)


def _heads_of(ref, rows):
    return jnp.stack([ref[rows, h * GDN_HEAD_DIM:(h + 1) * GDN_HEAD_DIM] for h in range(GDN_HEADS)])


def _chunk_terms(q_ref, k_ref, v_ref, b_ref, gc_ref, gr_ref, c, incl, strict, n=1):
    r0 = c * CHUNK if isinstance(c, int) else pl.multiple_of(c * CHUNK, CHUNK)
    rows = pl.ds(r0, n * CHUNK)
    per_chunk = lambda x: x.reshape(GDN_HEADS * n, CHUNK, x.shape[-1])
    q, k, v = (per_chunk(_heads_of(ref, rows)) for ref in (q_ref, k_ref, v_ref))
    lane_ids = lax.broadcasted_iota(jnp.int32, (1, LANES), 1)
    pick = lambda slab, first: jnp.stack([jnp.sum(jnp.where(lane_ids == first + h, slab, 0.0), axis=-1, keepdims=True)
                                          for h in range(GDN_HEADS)])
    b = per_chunk(pick(b_ref[rows, :], 0))
    gc = per_chunk(pick(gc_ref[rows, :], GDN_HEADS))
    gr = gr_ref[:, c] if n == 1 else gr_ref[:, c:c + n].reshape(GDN_HEADS * n, 1, CHUNK)
    dm = jnp.where(incl, jnp.exp(jnp.where(incl, gc - gr, 0.0)), 0.0)
    kb = k * b
    vb = v * b
    e = jnp.exp(gc)
    kk_qk = _bm_nt(jnp.concatenate([kb, q], axis=1), k)
    a = jnp.where(strict, kk_qk[:, :CHUNK] * dm, 0.0)
    p = jnp.where(incl, kk_qk[:, CHUNK:] * dm, 0.0)
    gl = gc[:, CHUNK - 1:CHUNK, :]
    eg = jnp.exp(gl - gc)
    return rows, q, k, v, b, gc, dm, kb, vb, e, a, p, gl, eg


def _unit_lower_inverse(a, eye):
    x = -a
    tm = eye + x
    xp = _bm(x, x)
    for _ in range(4):
        both = _bm(jnp.concatenate([xp, tm], axis=1), xp)
        tm = tm + both[:, CHUNK:]
        xp = both[:, :CHUNK]
    return tm + _bm(tm, xp)


def _gdn_specs(t_len, n_chunks, reverse):
    cps = GDN_CHUNKS_PER_STEP
    steps = n_chunks // cps
    at = (lambda g: steps - 1 - g) if reverse else (lambda g: g)
    rows_blk = lambda width, part=0: pl.BlockSpec((cps * CHUNK, width), lambda g: (at(g), part))
    gate_r = pl.BlockSpec((GDN_HEADS, cps, 1, CHUNK), lambda g: (0, at(g), 0, 0))
    per_chunk = lambda r, c: pl.BlockSpec((GDN_HEADS, cps, r, c), lambda g: (0, at(g), 0, 0))
    return cps, steps, rows_blk, gate_r, per_chunk


def _gdn_fwd_call(gact, beta_c, gam_c, gam_r, t_len):
    n_chunks = t_len // CHUNK
    dk = GDN_HEAD_DIM
    width = GDN_HEADS * dk
    cps, steps, rows_blk, gate_r, per_chunk = _gdn_specs(t_len, n_chunks, False)

    def body(q_ref, k_ref, v_ref, b_ref, gc_ref, gr_ref, o_ref, s_ref, t_ref, state_ref):
        row = lax.broadcasted_iota(jnp.int32, (CHUNK, CHUNK), 0)
        col = lax.broadcasted_iota(jnp.int32, (CHUNK, CHUNK), 1)
        incl, strict = row >= col, row > col
        eye = (row == col).astype(F32)

        @pl.when(pl.program_id(0) == 0)
        def _():
            state_ref[...] = jnp.zeros_like(state_ref)

        _, q, k, v, b, gc, dm, kb, vb, e, a, p, gl, eg = _chunk_terms(
            q_ref, k_ref, v_ref, b_ref, gc_ref, gr_ref, 0, incl, strict, cps)
        tm = _unit_lower_inverse(a, eye)
        uw = _bm(tm, jnp.concatenate([vb, kb * e], axis=2))
        w_qe = jnp.concatenate([uw[:, :, dk:], q * e], axis=1)
        u, kd, decay = uw[:, :, :dk], k * eg, jnp.exp(gl)
        t_ref[...] = tm.reshape(GDN_HEADS, cps, CHUNK, CHUNK)

        of_chunk = lambda x, c: jnp.stack([x[h * cps + c] for h in range(GDN_HEADS)])
        s = state_ref[...]
        for c in range(cps):
            ws_qs = _bm(of_chunk(w_qe, c), s)
            vn = of_chunk(u, c) - ws_qs[:, :CHUNK]
            o = ws_qs[:, CHUNK:] + _bm(of_chunk(p, c), vn)
            for h in range(GDN_HEADS):
                o_ref[c * CHUNK:(c + 1) * CHUNK, h * dk:(h + 1) * dk] = o[h]
            s_ref[:, c] = s
            s = s * of_chunk(decay, c) + _bm_tn(of_chunk(kd, c), vn)
        state_ref[...] = s

    return pl.pallas_call(
        body, name="gdn_fwd",
        grid=(steps,),
        in_specs=[rows_blk(width, 0), rows_blk(width, 1), rows_blk(width, 2), rows_blk(LANES), rows_blk(LANES), gate_r],
        out_specs=[rows_blk(width), per_chunk(dk, dk), per_chunk(CHUNK, CHUNK)],
        out_shape=[jax.ShapeDtypeStruct((t_len, width), F32),
                   jax.ShapeDtypeStruct((GDN_HEADS, n_chunks, dk, dk), F32),
                   jax.ShapeDtypeStruct((GDN_HEADS, n_chunks, CHUNK, CHUNK), F32)],
        scratch_shapes=[pltpu.VMEM((GDN_HEADS, dk, dk), F32)],
        compiler_params=_params("arbitrary"),
    )(gact, gact, gact, beta_c, gam_c, gam_r)


def _gdn_bwd_call(gact, beta_c, gam_c, gam_r, s_all, t_all, d_o, t_len, scatter=()):
    n_chunks = t_len // CHUNK
    dk = GDN_HEAD_DIM
    width = GDN_HEADS * dk
    cps, steps, rows_blk, gate_r, per_chunk = _gdn_specs(t_len, n_chunks, True)
    nx = len(scatter)

    def body(*refs):
        q_ref, k_ref, v_ref, b_ref, gc_ref, gr_ref, s_ref, t_ref, do_ref = refs[:9]
        d_ref, dgate_ref = refs[9 + nx:11 + nx]
        dstate_ref = refs[11 + 2 * nx]
        copies = lambda: _direct_copies(refs[9:9 + nx], refs[11 + nx:11 + 2 * nx], *refs[12 + 2 * nx:], (True,) * nx)
        if nx:
            pl.when(pl.program_id(0) == 0)(lambda: _start_all(copies()))
        row = lax.broadcasted_iota(jnp.int32, (CHUNK, CHUNK), 0)
        col = lax.broadcasted_iota(jnp.int32, (CHUNK, CHUNK), 1)
        incl, strict = row >= col, row > col
        ng = GDN_BWD_GROUP
        nb = GDN_HEADS * ng
        upper = jnp.broadcast_to((row <= col).astype(F32), (nb, CHUNK, CHUNK))
        ones = jnp.ones((nb, CHUNK, LANES), F32)
        last_row = lax.broadcasted_iota(jnp.int32, (CHUNK, 1), 0) == CHUNK - 1
        lane_ids = lax.broadcasted_iota(jnp.int32, (1, LANES), 1)
        rsum = lambda m: jnp.sum(m, axis=-1, keepdims=True)
        total = lambda m: jnp.sum(rsum(m), axis=1, keepdims=True)
        of_chunk = lambda x, c: jnp.stack([x[h * ng + c] for h in range(GDN_HEADS)])

        @pl.when(pl.program_id(0) == 0)
        def _():
            dstate_ref[...] = jnp.zeros_like(dstate_ref)

        for c0 in range(cps - ng, -1, -ng):
            group(c0, q_ref, k_ref, v_ref, b_ref, gc_ref, gr_ref, s_ref, t_ref, do_ref, d_ref, dgate_ref, dstate_ref,
                  incl, strict, upper, ones, last_row, lane_ids, rsum, total, of_chunk)
        if nx:
            pl.when(pl.program_id(0) == steps - 1)(lambda: _wait_all(copies()))

    def group(c0, q_ref, k_ref, v_ref, b_ref, gc_ref, gr_ref, s_ref, t_ref, do_ref, d_ref, dgate_ref, dstate_ref,
              incl, strict, upper, ones, last_row, lane_ids, rsum, total, of_chunk):
        ng = GDN_BWD_GROUP
        nb = GDN_HEADS * ng
        rows = pl.ds(c0 * CHUNK, ng * CHUNK)
        _, q, k, v, b, gc, dm, kb, vb, e, a, p, gl, eg = _chunk_terms(
            q_ref, k_ref, v_ref, b_ref, gc_ref, gr_ref, c0, incl, strict, ng)
        s = s_ref[:, c0:c0 + ng].reshape(nb, dk, dk)
        tm = t_ref[:, c0:c0 + ng].reshape(nb, CHUNK, CHUNK)
        d_out = _heads_of(do_ref, rows).reshape(nb, CHUNK, dk)
        el = jnp.exp(gl)
        kbe = kb * e
        qe = q * e
        kd = k * eg
        uw = _bm(tm, jnp.concatenate([vb, kbe], axis=2))
        u, w = uw[:, :, :dk], uw[:, :, dk:]
        vn = u - _bm(w, s)
        pt_do = _bm_tn(p, d_out)
        qet_do = _bm_tn(qe, d_out)

        ds = dstate_ref[...]
        d_vn_c, ds_c = [None] * ng, [None] * ng
        for c in range(ng - 1, -1, -1):
            ds_c[c] = ds
            d_vn_c[c] = of_chunk(pt_do, c) + _bm(of_chunk(kd, c), ds)
            ds = of_chunk(el, c) * ds + of_chunk(qet_do, c) - _bm_tn(of_chunk(w, c), d_vn_c[c])
        dstate_ref[...] = ds
        by_chunk = lambda xs: jnp.stack([xs[c][h] for h in range(GDN_HEADS) for c in range(ng)])
        d_vn, ds = by_chunk(d_vn_c), by_chunk(ds_c)

        on_s = _bm_nt(jnp.concatenate([d_out, d_vn], axis=1), s)
        d_qe, d_w = on_s[:, :CHUNK], -on_s[:, CHUNK:]
        d_p = jnp.where(incl, _bm_nt(d_out, vn), 0.0)
        d_kd = _bm_nt(vn, ds)
        d_both = _bm_tn(tm, jnp.concatenate([d_vn, d_w], axis=2))
        d_vb, d_kbe = d_both[:, :, :dk], d_both[:, :, dk:]
        d_a = -jnp.where(strict, _bm_nt(d_both, uw), 0.0)
        m = d_a * dm
        n = d_p * dm
        on_k = _bm(jnp.concatenate([m, n], axis=1), k)
        d_kb = on_k[:, :CHUNK] + d_kbe * e
        d_q = on_k[:, CHUNK:] + d_qe * e
        d_k = (_bm_tn(jnp.concatenate([m, n], axis=1), jnp.concatenate([kb, q], axis=1))
               + d_kd * eg + b * d_kb)
        d_v = b * d_vb
        r = d_a * a + d_p * p
        kd_term = rsum(d_kd * kd)
        d_gl = total(ds * s) * el + jnp.sum(kd_term, axis=1, keepdims=True)
        d_gam = (rsum(r) - _bm_tn(r, ones)[:, :, 0:1] + rsum(d_qe * qe) + rsum(d_kbe * kbe) - kd_term
                 + jnp.where(last_row, d_gl, 0.0))
        d_beta = rsum(d_kb * k) + rsum(d_vb * v)
        d_g = _bm(upper, d_gam * ones)[:, :, 0:1]
        per_head = lambda x: x.reshape(GDN_HEADS, ng * CHUNK, x.shape[-1])
        d_q, d_k, d_v, d_beta, d_g = (per_head(x) for x in (d_q, d_k, d_v, d_beta, d_g))
        gates = jnp.zeros((ng * CHUNK, LANES), F32)
        for h in range(GDN_HEADS):
            lanes = slice(h * dk, (h + 1) * dk)
            d_ref[0, rows, lanes] = d_q[h]
            d_ref[1, rows, lanes] = d_k[h]
            d_ref[2, rows, lanes] = d_v[h]
            gates = gates + (jnp.where(lane_ids == h, d_beta[h], 0.0)
                             + jnp.where(lane_ids == GDN_HEADS + h, d_g[h], 0.0))
        dgate_ref[rows, :] = gates

    d_spec = pl.BlockSpec((3, cps * CHUNK, width), lambda g: (0, steps - 1 - g, 0))
    return pl.pallas_call(
        body, name="gdn_bwd",
        grid=(steps,),
        in_specs=[rows_blk(width, 0), rows_blk(width, 1), rows_blk(width, 2), rows_blk(LANES), rows_blk(LANES), gate_r,
                  per_chunk(dk, dk), per_chunk(CHUNK, CHUNK), rows_blk(width)] + [_HBM] * nx,
        out_specs=[d_spec, rows_blk(LANES)] + [_HBM] * nx,
        out_shape=[jax.ShapeDtypeStruct((3, t_len, width), F32),
                   jax.ShapeDtypeStruct((t_len, LANES), F32)] + _direct_out_shapes(scatter, (True,) * nx),
        scratch_shapes=[pltpu.VMEM((GDN_HEADS, dk, dk), F32)] + (_direct_semaphores(nx) if nx else []),
        compiler_params=_params("arbitrary"),
    )(gact, gact, gact, beta_c, gam_c, gam_r, s_all, t_all, d_o, *scatter)


def _group_matrix(width, group):
    r = lax.broadcasted_iota(jnp.int32, (width, width), 0)
    c = lax.broadcasted_iota(jnp.int32, (width, width), 1)
    return ((r // group) == (c // group)).astype(F32)


def _post_call(o_sb, o_gd, proj, x, target, w_out, sbw, gdw, fw, tm=256):
    t_len, d = x.shape
    half = 512
    zsb_blk = 1536 // half
    zgd_blk = 3584 // half

    def body(osb_ref, ogd_ref, zsb_ref, zgd_ref, x_ref, tg_ref, wo_ref, sbw_ref, gdw_ref, fw_ref,
             dx2_ref, dosb_ref, dogd_ref, dz_ref, loss_ref, gfw_ref, gsb_ref, ggd_ref, gwo_ref):
        step = pl.program_id(0)

        @pl.when(step == 0)
        def _():
            loss_ref[...] = jnp.zeros_like(loss_ref)
            gfw_ref[...] = jnp.zeros_like(gfw_ref)
            gsb_ref[...] = jnp.zeros_like(gsb_ref)
            ggd_ref[...] = jnp.zeros_like(ggd_ref)
            gwo_ref[...] = jnp.zeros_like(gwo_ref)

        def head_forward(o, z, w, gmat, inv):
            r = lax.rsqrt(_running_sum_mm(o * o, gmat) * inv + EPS)
            nrm = o * r * w
            sg = _sigmoid(z)
            return r, nrm, sg, nrm * (z * sg)

        def head_backward(d_m, o, z, w, gmat, inv, r, nrm, sg):
            d_n = d_m * (z * sg)
            d_z = d_m * nrm * (sg * (1.0 + z * (1.0 - sg)))
            dnw = d_n * w
            d_o = r * dnw - o * (r * r * r) * (_running_sum_mm(dnw * o, gmat) * inv)
            return d_o, d_z, jnp.sum(d_n * o * r, axis=0, keepdims=True)

        g_sb = _group_matrix(half, SB_HEAD_DIM).astype(MXU_DTYPE)
        g_gd = _group_matrix(half, GDN_HEAD_DIM).astype(MXU_DTYPE)
        osb, ogd, zsb, zgd = osb_ref[...], ogd_ref[...], zsb_ref[...], zgd_ref[...]
        sbw_v, gdw_v = sbw_ref[...], gdw_ref[...]
        r_sb, n_sb, sg_sb, m_sb = head_forward(osb, zsb, sbw_v, g_sb, 1.0 / SB_HEAD_DIM)
        r_gd, n_gd, sg_gd, m_gd = head_forward(ogd, zgd, gdw_v, g_gd, 1.0 / GDN_HEAD_DIM)
        mixed = jnp.concatenate([m_sb, m_gd], axis=1).astype(MXU_DTYPE)
        wo = wo_ref[...]
        x2 = x_ref[...] + jnp.dot(mixed, wo, preferred_element_type=F32)
        r2 = lax.rsqrt(jnp.mean(x2 * x2, axis=-1, keepdims=True) + EPS)
        fw_v = fw_ref[...]
        err = x2 * r2 * fw_v - tg_ref[...]
        loss_ref[...] += 0.5 * jnp.sum(jnp.sum(err * err, axis=-1, keepdims=True) * (1.0 / d))
        dy = err * (1.0 / d)
        gg = dy * fw_v
        dx2 = r2 * gg - x2 * ((r2 * r2 * r2) * jnp.mean(gg * x2, axis=-1, keepdims=True))
        gfw_ref[...] += jnp.sum(dy * x2 * r2, axis=0, keepdims=True)
        dx2_ref[...] = dx2
        dx2b = dx2.astype(MXU_DTYPE)
        d_mixed = lax.dot_general(dx2b, wo, _NT, preferred_element_type=F32)
        gwo_ref[...] += lax.dot_general(mixed, dx2b, _TN, preferred_element_type=F32)
        d_osb, d_zsb, gsb = head_backward(d_mixed[:, :half], osb, zsb, sbw_v, g_sb, 1.0 / SB_HEAD_DIM, r_sb, n_sb, sg_sb)
        d_ogd, d_zgd, ggd = head_backward(d_mixed[:, half:], ogd, zgd, gdw_v, g_gd, 1.0 / GDN_HEAD_DIM, r_gd, n_gd, sg_gd)
        dosb_ref[...] = d_osb
        dogd_ref[...] = d_ogd
        dz_ref[0] = d_zsb
        dz_ref[1] = d_zgd
        gsb_ref[...] += gsb
        ggd_ref[...] += ggd

    row_blk = lambda w: pl.BlockSpec((tm, w), lambda i: (i, 0))
    fixed = lambda r, w: pl.BlockSpec((r, w), lambda i: (0, 0))
    return pl.pallas_call(
        body, name="post",
        grid=(t_len // tm,),
        in_specs=[row_blk(half), row_blk(half),
                  pl.BlockSpec((tm, half), lambda i: (i, zsb_blk)),
                  pl.BlockSpec((tm, half), lambda i: (i, zgd_blk)),
                  row_blk(d), row_blk(d), fixed(d, d), fixed(1, half), fixed(1, half), fixed(1, d)],
        out_specs=[row_blk(d), row_blk(half), row_blk(half),
                   pl.BlockSpec((2, tm, half), lambda i: (DPROJ_GATE_SLOT // 2, i, 0)),
                   fixed(1, LANES), fixed(1, d), fixed(1, half), fixed(1, half), fixed(d, d)],
        out_shape=[jax.ShapeDtypeStruct((t_len, d), F32)] + [jax.ShapeDtypeStruct((t_len, half), F32)] * 2
                  + [jax.ShapeDtypeStruct((len(DPROJ_PIECE_OF_SLOT), t_len, half), F32),
                     jax.ShapeDtypeStruct((1, LANES), F32), jax.ShapeDtypeStruct((1, d), F32),
                     jax.ShapeDtypeStruct((1, half), F32), jax.ShapeDtypeStruct((1, half), F32),
                     jax.ShapeDtypeStruct((d, d), F32)],
        compiler_params=_params("arbitrary"),
    )(o_sb, o_gd, proj, proj, x, target, w_out, sbw, gdw, fw)


def _piece_of_slot(s):
    return jnp.where(s < DPROJ_GDN_SLOT, s, jnp.where(s < DPROJ_GATE_SLOT, s + 1,
                                                     jnp.where(s == DPROJ_GATE_SLOT, 3, 7)))


def _gw_in_call(h_t, dproj8):
    d, t_len = h_t.shape
    n_piece, _, pw = dproj8.shape

    def body(ht_ref, dp_ref, gw_ref):
        gw_ref[...] = jnp.dot(ht_ref[...], dp_ref[0].astype(MXU_DTYPE), preferred_element_type=F32)

    return pl.pallas_call(
        body, name="gw_in",
        grid=(n_piece,),
        in_specs=[pl.BlockSpec((d, t_len), lambda s: (0, 0)),
                  pl.BlockSpec((1, t_len, pw), lambda s: (s, 0, 0))],
        out_specs=pl.BlockSpec((d, pw), lambda s: (0, _piece_of_slot(s))),
        out_shape=jax.ShapeDtypeStruct((d, n_piece * pw), F32),
        compiler_params=_params("arbitrary"),
    )(h_t, dproj8)


def _gw_small_call(h_t, dsmall, tm=512):
    d, t_len = h_t.shape
    ns = dsmall.shape[1]

    def body(ht_ref, dp_ref, gw_ref):
        @pl.when(pl.program_id(0) == 0)
        def _():
            gw_ref[...] = jnp.zeros_like(gw_ref)

        gw_ref[...] += jnp.dot(ht_ref[...], dp_ref[...].astype(MXU_DTYPE), preferred_element_type=F32)

    return pl.pallas_call(
        body, name="gw_small",
        grid=(t_len // tm,),
        in_specs=[pl.BlockSpec((d, tm), lambda t: (0, t)),
                  pl.BlockSpec((tm, ns), lambda t: (t, 0))],
        out_specs=pl.BlockSpec((d, ns), lambda t: (0, 0)),
        out_shape=jax.ShapeDtypeStruct((d, ns), F32),
        compiler_params=_params("arbitrary"),
    )(h_t, dsmall)


def _dx_call(dproj8, dsmall, w_main, w_small, x, r, dx2, norm_w, chip_scatter=(), tm=256):
    t_len, d = x.shape
    n_piece, _, pw = dproj8.shape
    ns = dsmall.shape[1]
    nx = len(chip_scatter)
    steps = t_len // tm

    def body(*refs):
        dp_ref, ds_ref, wm_ref, ws_ref, x_ref, r_ref, dx2_ref, nw_ref = refs[:8]
        gx_ref, gnw_ref = refs[8 + nx:10 + nx]
        copies = lambda: _chip_copies(refs[8:8 + nx], refs[10 + nx:10 + 2 * nx], *refs[10 + 2 * nx:])
        if nx:
            pl.when(pl.program_id(0) == 0)(lambda: _start_all(copies()))

        @pl.when(pl.program_id(0) == 0)
        def _():
            gnw_ref[...] = jnp.zeros_like(gnw_ref)

        dh = lax.dot_general(ds_ref[...].astype(MXU_DTYPE), ws_ref[...], _NT, preferred_element_type=F32)
        for s, p in enumerate(DPROJ_PIECE_OF_SLOT):
            dh = dh + lax.dot_general(dp_ref[s].astype(MXU_DTYPE), wm_ref[:, p * pw:(p + 1) * pw], _NT,
                                      preferred_element_type=F32)
        xv, rv = x_ref[...], r_ref[...]
        dn = dh * nw_ref[...]
        gx_ref[...] = dx2_ref[...] + rv * dn - xv * ((rv * rv * rv) * jnp.mean(dn * xv, axis=-1, keepdims=True))
        gnw_ref[...] += jnp.sum(dh * xv * rv, axis=0, keepdims=True)
        if nx:
            pl.when(pl.program_id(0) == steps - 1)(lambda: _wait_all(copies()))

    return pl.pallas_call(
        body, name="dx",
        grid=(steps,),
        in_specs=[pl.BlockSpec((n_piece, tm, pw), lambda i: (0, i, 0)),
                  pl.BlockSpec((tm, ns), lambda i: (i, 0)),
                  pl.BlockSpec((d, n_piece * pw), lambda i: (0, 0)),
                  pl.BlockSpec((d, ns), lambda i: (0, 0)),
                  pl.BlockSpec((tm, d), lambda i: (i, 0)),
                  pl.BlockSpec((tm, 1), lambda i: (i, 0)),
                  pl.BlockSpec((tm, d), lambda i: (i, 0)),
                  pl.BlockSpec((1, d), lambda i: (0, 0))] + [_HBM] * nx,
        out_specs=[pl.BlockSpec((tm, d), lambda i: (i, 0)),
                   pl.BlockSpec((1, d), lambda i: (0, 0))] + [_HBM] * nx,
        out_shape=[jax.ShapeDtypeStruct((t_len, d), F32), jax.ShapeDtypeStruct((1, d), F32)]
                  + [jax.ShapeDtypeStruct(a.shape, a.dtype) for a in chip_scatter],
        scratch_shapes=_chip_semaphores(nx) if nx else [],
        compiler_params=_params("arbitrary"),
    )(dproj8, dsmall, w_main, w_small, x, r, dx2, norm_w, *chip_scatter)


def _exchange_call(name, srcs, per_peer):
    n = len(srcs)

    def body(*refs):
        src_refs, out_refs = refs[:n], refs[n:2 * n]
        copies = _direct_copies(src_refs, out_refs, *refs[2 * n:], per_peer)
        _start_all(copies)
        _wait_all(copies)

    hbm = pl.BlockSpec(memory_space=pl.ANY)
    return pl.pallas_call(
        body, name=name,
        in_specs=[hbm] * n, out_specs=[hbm] * n, out_shape=_direct_out_shapes(srcs, per_peer),
        scratch_shapes=_direct_semaphores(n),
    )(*srcs)


def _direct_out_shapes(srcs, per_peer):
    return [jax.ShapeDtypeStruct(s.shape if pp else (N_DEV,) + s.shape, s.dtype) for s, pp in zip(srcs, per_peer)]


def _direct_semaphores(n):
    return [pltpu.SemaphoreType.DMA((n * (N_DEV - 1),)), pltpu.SemaphoreType.DMA((n * (N_DEV - 1),)),
            pltpu.SemaphoreType.DMA((n,))]


def _direct_copies(src_refs, out_refs, send_sems, recv_sems, local_sems, per_peer):
    x, y, c = lax.axis_index("x"), lax.axis_index("y"), lax.axis_index("c")
    me = 4 * x + 2 * y + c
    local, remote = [], []
    for a in range(len(src_refs)):
        mine = src_refs[a].at[me] if per_peer[a] else src_refs[a]
        local.append(pltpu.make_async_copy(mine, out_refs[a].at[me], local_sems.at[a]))
    for k in range(1, N_DEV):
        kx, ky, kc = (k >> 2) & 1, (k >> 1) & 1, k & 1
        px = 1 - x if kx else x
        py = 1 - y if ky else y
        pc = 1 - c if kc else c
        peer = 4 * px + 2 * py + pc
        for a in range(len(src_refs)):
            sem = a * (N_DEV - 1) + (k - 1)
            remote.append(pltpu.make_async_remote_copy(
                src_ref=src_refs[a].at[peer] if per_peer[a] else src_refs[a], dst_ref=out_refs[a].at[me],
                send_sem=send_sems.at[sem], recv_sem=recv_sems.at[sem],
                device_id=(px, py, pc), device_id_type=pl.DeviceIdType.MESH))
    return local, remote


def _start_all(copies):
    local, remote = copies
    for cp in local + remote:
        cp.start()


def _wait_all(copies):
    local, remote = copies
    for cp in remote:
        cp.wait_send()
    for cp in remote:
        cp.wait_recv()
    for cp in local:
        cp.wait()


N_CHIPS = 4
_HBM = pl.BlockSpec(memory_space=pl.ANY)
_MESH = pl.DeviceIdType.MESH


def _gather_call(name, srcs):
    n = len(srcs)
    per = N_DEV - 1

    def body(*refs):
        src_refs, out_refs = refs[:n], refs[n:2 * n]
        send_sems, recv_sems, local_sems = refs[2 * n:]
        x, y, c = lax.axis_index("x"), lax.axis_index("y"), lax.axis_index("c")
        me, sibling = (x, y, c), (x, y, 1 - c)
        x_nbr, y_nbr, diagonal = (1 - x, y), (x, 1 - y), (1 - x, 1 - y)
        held = ((1 - x) * c + x * (1 - c), y * c + (1 - y) * (1 - c))
        onward = (x * c + (1 - x) * (1 - c), (1 - y) * c + y * (1 - c))
        slot = lambda px, py, pc: 4 * px + 2 * py + pc

        def copy(a, k, block, to, from_src=False):
            rows = out_refs[a].at[slot(*block)]
            return pltpu.make_async_remote_copy(
                src_ref=src_refs[a] if from_src else rows, dst_ref=rows,
                send_sem=send_sems.at[a * per + k], recv_sem=recv_sems.at[a * per + k],
                device_id=to, device_id_type=_MESH)

        local = [pltpu.make_async_copy(src_refs[a], out_refs[a].at[slot(*me)], local_sems.at[a]) for a in range(n)]
        started = []

        def start(cp):
            cp.start()
            started.append(cp)

        for cp in local:
            cp.start()
        for a in range(n):
            start(copy(a, 0, me, sibling, True))
            start(copy(a, 1, me, (*x_nbr, c), True))
            start(copy(a, 2, me, (*y_nbr, c), True))
        for a in range(n):
            copy(a, 1, (*x_nbr, c), me).wait_recv()
            copy(a, 2, (*y_nbr, c), me).wait_recv()
            start(copy(a, 3, (*held, c), (*onward, c)))
            start(copy(a, 4, (*x_nbr, c), sibling))
            start(copy(a, 5, (*y_nbr, c), sibling))
        for a in range(n):
            copy(a, 3, (*diagonal, c), me).wait_recv()
            start(copy(a, 6, (*diagonal, c), sibling))
        for a in range(n):
            copy(a, 0, sibling, me).wait_recv()
            for k, chip in ((4, x_nbr), (5, y_nbr), (6, diagonal)):
                copy(a, k, (*chip, 1 - c), me).wait_recv()
        for cp in started:
            cp.wait_send()
        for cp in local:
            cp.wait()

    return pl.pallas_call(
        body, name=name,
        in_specs=[_HBM] * n, out_specs=[_HBM] * n,
        out_shape=[jax.ShapeDtypeStruct((N_DEV,) + s.shape, s.dtype) for s in srcs],
        scratch_shapes=[pltpu.SemaphoreType.DMA((n * per,)), pltpu.SemaphoreType.DMA((n * per,)),
                        pltpu.SemaphoreType.DMA((n,))],
    )(*srcs)


def _sibling_send_call(name, srcs):
    n = len(srcs)

    def body(*refs):
        src_refs, out_refs = refs[:n], refs[n:2 * n]
        send_sems, recv_sems = refs[2 * n:]
        x, y, c = lax.axis_index("x"), lax.axis_index("y"), lax.axis_index("c")
        copies = []
        for a in range(n):
            for ch in range(N_CHIPS):
                copies.append(pltpu.make_async_remote_copy(
                    src_ref=src_refs[a].at[2 * ch + (1 - c)], dst_ref=out_refs[a].at[ch],
                    send_sem=send_sems.at[a * N_CHIPS + ch], recv_sem=recv_sems.at[a * N_CHIPS + ch],
                    device_id=(x, y, 1 - c), device_id_type=_MESH))
        for cp in copies:
            cp.start()
        for cp in copies:
            cp.wait_send()
        for cp in copies:
            cp.wait_recv()

    return pl.pallas_call(
        body, name=name,
        in_specs=[_HBM] * n, out_specs=[_HBM] * n,
        out_shape=[jax.ShapeDtypeStruct((N_CHIPS,) + s.shape[1:], s.dtype) for s in srcs],
        scratch_shapes=[pltpu.SemaphoreType.DMA((n * N_CHIPS,)), pltpu.SemaphoreType.DMA((n * N_CHIPS,))],
    )(*srcs)


def _pair_sum_call(name, parts, from_sibling, tr):
    _, rows, cols = parts.shape

    def body(p_ref, s_ref, o_ref):
        o_ref[...] = (p_ref[...].astype(F32) + s_ref[...].astype(F32)).astype(o_ref.dtype)

    return pl.pallas_call(
        body, name=name,
        grid=(N_CHIPS, rows // tr),
        in_specs=[pl.BlockSpec((1, tr, cols), lambda ch, i: (2 * ch + lax.axis_index("c"), i, 0)),
                  pl.BlockSpec((1, tr, cols), lambda ch, i: (ch, i, 0))],
        out_specs=pl.BlockSpec((1, tr, cols), lambda ch, i: (ch, i, 0)),
        out_shape=jax.ShapeDtypeStruct((N_CHIPS, rows, cols), WIRE_DTYPE),
        compiler_params=_params("arbitrary", "arbitrary"),
    )(parts, from_sibling)


def _chip_exchange_call(name, srcs):
    n = len(srcs)

    def body(*refs):
        copies = _chip_copies(refs[:n], refs[n:2 * n], *refs[2 * n:])
        _start_all(copies)
        _wait_all(copies)

    return pl.pallas_call(
        body, name=name,
        in_specs=[_HBM] * n, out_specs=[_HBM] * n,
        out_shape=[jax.ShapeDtypeStruct(s.shape, s.dtype) for s in srcs],
        scratch_shapes=_chip_semaphores(n),
    )(*srcs)


def _chip_semaphores(n):
    per = N_CHIPS - 1
    return [pltpu.SemaphoreType.DMA((n * per,)), pltpu.SemaphoreType.DMA((n * per,)), pltpu.SemaphoreType.DMA((n,))]


def _chip_copies(src_refs, out_refs, send_sems, recv_sems, local_sems):
    per = N_CHIPS - 1
    x, y, c = lax.axis_index("x"), lax.axis_index("y"), lax.axis_index("c")
    mine = 2 * x + y
    chips = [(1 - x, y), (x, 1 - y), (1 - x, 1 - y)]
    n = len(src_refs)
    local = [pltpu.make_async_copy(src_refs[a].at[mine], out_refs[a].at[mine], local_sems.at[a]) for a in range(n)]
    remote = []
    for a in range(n):
        for j, (px, py) in enumerate(chips):
            remote.append(pltpu.make_async_remote_copy(
                src_ref=src_refs[a].at[2 * px + py], dst_ref=out_refs[a].at[mine],
                send_sem=send_sems.at[a * per + j], recv_sem=recv_sems.at[a * per + j],
                device_id=(px, py, c), device_id_type=_MESH))
    return local, remote


def _adam_call(name, parts, w, m, v, tr):
    rows, cols = w.shape
    n_slots = parts.shape[0]

    def body(p_ref, w_ref, m_ref, v_ref, g_ref, d_ref, nm_ref, nv_ref):
        g = p_ref[0].astype(F32)
        for s in range(1, n_slots):
            g = g + p_ref[s].astype(F32)
        m_new = ADAM_B1 * m_ref[...] + (1.0 - ADAM_B1) * g
        v_new = ADAM_B2 * v_ref[...] + (1.0 - ADAM_B2) * (g * g)
        m_hat = m_new / (1.0 - ADAM_B1 ** ADAM_STEP)
        v_hat = v_new / (1.0 - ADAM_B2 ** ADAM_STEP)
        g_ref[...] = g
        d_ref[...] = -ADAM_LR * (m_hat / (jnp.sqrt(v_hat) + ADAM_EPS) + ADAM_WD * w_ref[...])
        nm_ref[...] = m_new
        nv_ref[...] = v_new

    blk = pl.BlockSpec((tr, cols), lambda i: (i, 0))
    return pl.pallas_call(
        body, name=name,
        grid=(rows // tr,),
        in_specs=[pl.BlockSpec((n_slots, tr, cols), lambda i: (0, i, 0)), blk, blk, blk],
        out_specs=[blk] * 4,
        out_shape=[jax.ShapeDtypeStruct((rows, cols), F32)] * 4,
        compiler_params=_params("arbitrary"),
    )(parts, w, m, v)


N_PIECES = 8
PIECE = 512
SHARD_COLS = 513
SHARD_PAD = 640
RELAYOUT_ROWS = 256


def _from_shards_call(shards):
    _, d, _ = shards.shape
    tr = RELAYOUT_ROWS

    def body(p_ref, m_ref, s_ref):
        lane = lax.broadcasted_iota(jnp.int32, (tr, SHARD_PAD), 1)
        pad = jnp.zeros((tr, SHARD_PAD - SHARD_COLS), F32)
        sh = [jnp.concatenate([p_ref[s].astype(F32), pad], axis=1) for s in range(N_DEV)]
        for p in range(N_PIECES):
            y = sh[p] if p == 0 else pltpu.roll(sh[p], p, axis=1)
            if p > 0:
                y = jnp.where(lane < p, pltpu.roll(sh[p - 1], SHARD_PAD - (SHARD_COLS - p), axis=1), y)
            m_ref[:, p * PIECE:(p + 1) * PIECE] = y[:, :PIECE].astype(m_ref.dtype)
        first_gate = N_PIECES * PIECE - (N_DEV - 1) * SHARD_COLS
        s_ref[...] = pltpu.roll(sh[N_DEV - 1], SHARD_PAD - first_gate, axis=1)[:, :LANES].astype(s_ref.dtype)

    return pl.pallas_call(
        body, name="w_in_from_shards",
        grid=(d // tr,),
        in_specs=[pl.BlockSpec((N_DEV, tr, SHARD_COLS), lambda i: (0, i, 0))],
        out_specs=[pl.BlockSpec((tr, N_PIECES * PIECE), lambda i: (i, 0)), pl.BlockSpec((tr, LANES), lambda i: (i, 0))],
        out_shape=[jax.ShapeDtypeStruct((d, N_PIECES * PIECE), shards.dtype),
                   jax.ShapeDtypeStruct((d, LANES), shards.dtype)],
        compiler_params=_params("arbitrary"),
    )(shards)


def _to_shards_call(main, gates, out_dtype):
    d = main.shape[0]
    tr = RELAYOUT_ROWS

    def body(m_ref, s_ref, o_ref):
        for s in range(N_DEV):
            if s < N_DEV - 1:
                x = m_ref[:, s * PIECE:s * PIECE + SHARD_PAD]
            else:
                x = jnp.concatenate([m_ref[:, s * PIECE:(s + 1) * PIECE], s_ref[...]], axis=1)
            y = x if s == 0 else pltpu.roll(x, SHARD_PAD - s, axis=1)
            o_ref[s] = y[:, :SHARD_COLS].astype(out_dtype)

    return pl.pallas_call(
        body, name="w_in_to_shards",
        grid=(d // tr,),
        in_specs=[pl.BlockSpec((tr, N_PIECES * PIECE), lambda i: (i, 0)), pl.BlockSpec((tr, LANES), lambda i: (i, 0))],
        out_specs=pl.BlockSpec((N_DEV, tr, SHARD_COLS), lambda i: (0, i, 0)),
        out_shape=jax.ShapeDtypeStruct((N_DEV, d, SHARD_COLS), out_dtype),
        compiler_params=_params("arbitrary"),
    )(main, gates)


def _adamw(g, w, m, v):
    m_new = ADAM_B1 * m + (1.0 - ADAM_B1) * g
    v_new = ADAM_B2 * v + (1.0 - ADAM_B2) * (g * g)
    m_hat = m_new / (1.0 - ADAM_B1 ** ADAM_STEP)
    v_hat = v_new / (1.0 - ADAM_B2 ** ADAM_STEP)
    return -ADAM_LR * (m_hat / (jnp.sqrt(v_hat) + ADAM_EPS) + ADAM_WD * w), m_new, v_new


def _adam_small_call(parts, ws, ms, vs):
    n = len(ws)
    n_slots = parts.shape[0]

    def body(*refs):
        p_ref = refs[0]
        w_refs, m_refs, v_refs = refs[1:1 + n], refs[1 + n:1 + 2 * n], refs[1 + 2 * n:1 + 3 * n]
        loss_ref = refs[1 + 3 * n]
        outs = refs[2 + 3 * n:]
        g_all = p_ref[0]
        for s in range(1, n_slots):
            g_all = g_all + p_ref[s]
        loss_ref[...] = g_all[n:n + 1, 0:1]
        for r in range(n):
            size = w_refs[r].shape[1]
            g = g_all[r:r + 1, :size]
            delta, m_new, v_new = _adamw(g, w_refs[r][...], m_refs[r][...], v_refs[r][...])
            for kind, val in enumerate((g, delta, m_new, v_new)):
                outs[kind * n + r][...] = val

    vm = pl.BlockSpec(memory_space=pltpu.VMEM)
    shapes = [jax.ShapeDtypeStruct(w.shape, F32) for w in ws]
    return pl.pallas_call(
        body, name="adam_small",
        in_specs=[vm] * (1 + 3 * n), out_specs=[vm] * (1 + 4 * n),
        out_shape=[jax.ShapeDtypeStruct((1, 1), F32)] + shapes * 4,
    )(parts, *ws, *ms, *vs)


_SMALL_ROWS = ("norm1_w", "final_norm_w", "sb_norm_w", "gdn_norm_w", "gdn_A_log", "gdn_dt_bias", "loss")


def _pack_small(vals, width):
    rows = [jnp.pad(a.reshape(1, -1).astype(F32), ((0, 0), (0, width - a.size))) for a in vals]
    rows += [jnp.zeros((1, width), F32)] * (8 - len(rows))
    return jnp.concatenate(rows, axis=0)


def _device_step(x2d, tgt, w_main, w_small, w_out_full, conv_full, norm1_w, sb_norm_w, gdn_A_log, gdn_dt_bias,
                 gdn_norm_w, final_norm_w, distributed=False, w_in_chip_partials=None):
    t_len, d = x2d.shape
    n_chunks = t_len // CHUNK
    w_main, w_small, w_out_full = (a.astype(MXU_DTYPE) for a in (w_main, w_small, w_out_full))
    w_small_t = w_small[:, :2 * GDN_HEADS].T

    pad_lanes = lambda a, lo: jnp.pad(a.reshape(1, -1), ((0, 0), (lo, LANES - lo - a.size)))
    alog_l, dtb_l = pad_lanes(gdn_A_log, GDN_HEADS), pad_lanes(gdn_dt_bias, GDN_HEADS)
    alog_c, dtb_c = alog_l[:, :8].T, dtb_l[:, :8].T
    sbw = jnp.tile(sb_norm_w, (1, 512 // SB_HEAD_DIM))
    gdw = jnp.tile(gdn_norm_w, (1, 512 // GDN_HEAD_DIM))
    fw = final_norm_w.reshape(1, d)

    if distributed:
        proj, ps, pst, h_t, r1, w_out_g, conv_g = _inproj_call(
            x2d, norm1_w, w_main, w_small, w_small_t, gather=(w_out_full, conv_full))
        w_out_full = w_out_g.reshape(d, d)
        conv_full = conv_g.transpose(1, 0, 2).reshape(CONV_WIDTH, N_DEV * conv_g.shape[2])
    else:
        proj, ps, pst, h_t, r1 = _inproj_call(x2d, norm1_w, w_main, w_small, w_small_t)
    o_sb, sp_total, sb_blocks_run = _sb_fwd_call(proj, t_len)
    gact = _gdn_prep_call(proj, conv_full, t_len)
    beta_l, gcol_l, grow = _gdn_gates_call(ps, pst, alog_l, dtb_l, alog_c, dtb_c, t_len)
    gam_r = grow[GDN_HEADS:2 * GDN_HEADS].reshape(GDN_HEADS, n_chunks, 1, CHUNK)
    o_gd, s_all, t_all = _gdn_fwd_call(gact, beta_l, gcol_l, gam_r, t_len)

    (dx2, d_osb, d_ogd, dproj8, loss_p, g_fw, g_sbw, g_gdw, g_wout) = _post_call(
        o_sb, o_gd, proj, x2d, tgt, w_out_full, sbw, gdw, fw)

    dproj8 = _sb_bwd_call(proj, sp_total, sb_blocks_run, d_osb, dproj8, t_len)
    if distributed:
        d_gact3, d_gates, g_wout = _gdn_bwd_call(gact, beta_l, gcol_l, gam_r, s_all, t_all, d_ogd, t_len,
                                                 scatter=(g_wout.reshape(N_DEV, d // N_DEV, d),))
    else:
        d_gact3, d_gates = _gdn_bwd_call(gact, beta_l, gcol_l, gam_r, s_all, t_all, d_ogd, t_len)
    dproj8, g_conv = _gdn_prep_bwd_call(proj, conv_full, d_gact3, dproj8, t_len)
    dsmall, g_alog, g_dtb = _gdn_gates_bwd_call(ps, alog_l, dtb_l, d_gates, t_len)

    g_w_main = _gw_in_call(h_t, dproj8)
    g_w_small = _gw_small_call(h_t, dsmall)
    if distributed:
        grad_x, g_n1, g_w_in = _dx_call(dproj8, dsmall, w_main, w_small, x2d, r1, dx2, norm1_w,
                                        chip_scatter=(w_in_chip_partials(g_w_main, g_w_small),))
    else:
        grad_x, g_n1 = _dx_call(dproj8, dsmall, w_main, w_small, x2d, r1, dx2, norm1_w)
        g_w_in = (g_w_main, g_w_small)
    return (loss_p, grad_x, g_n1, g_w_in, g_sbw, g_conv, g_alog, g_dtb, g_gdw, g_wout, g_fw)


def kernel(x, norm1_w, w_in, sb_norm_w, gdn_conv_w, gdn_A_log, gdn_dt_bias, gdn_norm_w, w_out, final_norm_w, loss_target, m_norm1_w, m_w_in, m_sb_norm_w, m_gdn_conv_w, m_gdn_A_log, m_gdn_dt_bias, m_gdn_norm_w, m_w_out, m_final_norm_w, v_norm1_w, v_w_in, v_sb_norm_w, v_gdn_conv_w, v_gdn_A_log, v_gdn_dt_bias, v_gdn_norm_w, v_w_out, v_final_norm_w):
    d = x.shape[2]
    shard_cols = w_in.shape[2]
    conv_cols = gdn_conv_w.shape[2]

    (w_in_g,) = _gather_call("gather_weights", [w_in[0].astype(WIRE_DTYPE)])
    w_main, w_small = _from_shards_call(w_in_g)

    def w_in_chip_partials(g_w_main, g_w_small):
        parts = _to_shards_call(g_w_main, g_w_small, WIRE_DTYPE)
        (from_sibling,) = _sibling_send_call("grads_to_sibling", [parts])
        return _pair_sum_call("pair_sum_w_in", parts, from_sibling, 256)

    (loss_p, grad_x, g_n1, p_w_in, g_sbw, g_conv, g_alog, g_dtb, g_gdw, p_wout, g_fw) = _device_step(
        x[0], loss_target[0], w_main, w_small, w_out[0].astype(WIRE_DTYPE), gdn_conv_w[0], norm1_w, sb_norm_w,
        gdn_A_log, gdn_dt_bias, gdn_norm_w, final_norm_w, distributed=True, w_in_chip_partials=w_in_chip_partials)

    g_conv_parts = g_conv.reshape(CONV_WIDTH, N_DEV, conv_cols).transpose(1, 0, 2)
    fold = lambda a, group: a.reshape(-1, group).sum(axis=0)
    small_g = _pack_small([g_n1, g_fw, fold(g_sbw, SB_HEAD_DIM), fold(g_gdw, GDN_HEAD_DIM),
                           g_alog[0, GDN_HEADS:2 * GDN_HEADS], g_dtb[0, GDN_HEADS:2 * GDN_HEADS],
                           loss_p[0, :1]], d)
    p_small, p_conv = _exchange_call("exchange_small", [small_g, g_conv_parts], [False, True])

    r_w_in = _adam_call("adam_w_in", p_w_in, w_in[0], m_w_in[0], v_w_in[0], 256)
    r_wout = _adam_call("adam_w_out", p_wout, w_out[0], m_w_out[0], v_w_out[0], d // N_DEV)
    r_conv = _adam_call("adam_conv", p_conv, gdn_conv_w[0], m_gdn_conv_w[0], v_gdn_conv_w[0], CONV_WIDTH)

    row = lambda a: a.reshape(1, -1)
    n_small = len(_SMALL_ROWS) - 1
    r_small = _adam_small_call(
        p_small,
        [norm1_w, row(final_norm_w), sb_norm_w, gdn_norm_w, gdn_A_log, gdn_dt_bias],
        [m_norm1_w, row(m_final_norm_w), m_sb_norm_w, m_gdn_norm_w, m_gdn_A_log, m_gdn_dt_bias],
        [v_norm1_w, row(v_final_norm_w), v_sb_norm_w, v_gdn_norm_w, v_gdn_A_log, v_gdn_dt_bias])

    def small_out(kind, name):
        out = r_small[1 + kind * n_small + _SMALL_ROWS.index(name)]
        return out.reshape(final_norm_w.shape) if name == "final_norm_w" else out

    def outputs(kind):
        return (small_out(kind, "norm1_w"), r_w_in[kind][None], small_out(kind, "sb_norm_w"), r_conv[kind][None],
                small_out(kind, "gdn_A_log"), small_out(kind, "gdn_dt_bias"), small_out(kind, "gdn_norm_w"),
                r_wout[kind][None], small_out(kind, "final_norm_w"))

    return (r_small[0][0, 0], grad_x[None], *outputs(0), *outputs(1), *outputs(2), *outputs(3))
```

```python
import functools

import jax
import jax.numpy as jnp
from jax import lax
from jax.experimental import pallas as pl
from jax.experimental.pallas import tpu as pltpu

F32 = jnp.float32
MXU_DTYPE = jnp.bfloat16
WIRE_DTYPE = jnp.bfloat16
EXACT = lax.Precision.HIGHEST
EPS = 1e-6
N_DEV = 8
SB_HEAD_DIM = 64
GDN_HEAD_DIM = 128
GDN_HEADS = 4
GDN_CHUNKS_PER_STEP = 4
GDN_BWD_GROUP = 1
CHUNK = 64
CONV_WIDTH = 4
LANES = 128
SB_BLOCK = 128
SB_BQ = 256
VMEM_LIMIT_BYTES = 56 * 1024 * 1024

DPROJ_PIECE_OF_SLOT = (0, 1, 2, 4, 5, 6, 3, 7)
DPROJ_SB_SLOT, DPROJ_GDN_SLOT, DPROJ_GATE_SLOT = 0, 3, 6

ADAM_LR = 0.001
ADAM_B1 = 0.9
ADAM_B2 = 0.999
ADAM_EPS = 1e-08
ADAM_WD = 0.01
ADAM_STEP = 10

_NN = (((1,), (0,)), ((), ()))
_NT = (((1,), (1,)), ((), ()))
_TN = (((0,), (0,)), ((), ()))
_BNN = (((2,), (1,)), ((0,), (0,)))
_BNT = (((2,), (2,)), ((0,), (0,)))
_BTN = (((1,), (1,)), ((0,), (0,)))


def _mm(a, b):
    return jnp.dot(a.astype(MXU_DTYPE), b.astype(MXU_DTYPE), preferred_element_type=F32)


def _mm_nt(a, b):
    return lax.dot_general(a.astype(MXU_DTYPE), b.astype(MXU_DTYPE), _NT, preferred_element_type=F32)


def _mm_tn(a, b):
    return lax.dot_general(a.astype(MXU_DTYPE), b.astype(MXU_DTYPE), _TN, preferred_element_type=F32)


def _mx(a, b):
    return jnp.dot(a, b, precision=EXACT, preferred_element_type=F32)


def _mx_nt(a, b):
    return lax.dot_general(a, b, _NT, precision=EXACT, preferred_element_type=F32)


def _mx_tn(a, b):
    return lax.dot_general(a, b, _TN, precision=EXACT, preferred_element_type=F32)


def _split(x):
    hi = x.astype(MXU_DTYPE)
    return hi, (x - hi.astype(F32)).astype(MXU_DTYPE)


def _m3_general(a, b, dims):
    ah, al = _split(a)
    bh, bl = _split(b)
    dot = lambda x, y: lax.dot_general(x, y, dims, preferred_element_type=F32)
    (contract, _), (batch, _) = dims
    free = [ax for ax in range(a.ndim) if ax not in contract and ax not in batch][0]
    m = a.shape[free]
    both = dot(jnp.concatenate([ah, al], axis=free), bh)
    out_axis = len(batch)
    hi_part = lax.slice_in_dim(both, 0, m, axis=out_axis)
    lo_part = lax.slice_in_dim(both, m, 2 * m, axis=out_axis)
    return hi_part + (dot(ah, bl) + lo_part)


def _m3(a, b):
    return _m3_general(a, b, _NN)


def _m3_nt(a, b):
    return _m3_general(a, b, _NT)


def _m3_tn(a, b):
    return _m3_general(a, b, _TN)


def _sigmoid(z):
    return 1.0 / (1.0 + jnp.exp(-z))


def _softplus(z):
    return jnp.maximum(z, 0.0) + jnp.log(1.0 + jnp.exp(-jnp.abs(z)))


def _params(*semantics):
    return pltpu.CompilerParams(dimension_semantics=semantics, vmem_limit_bytes=VMEM_LIMIT_BYTES)


def _inproj_call(x, norm_w, w_main, w_small, w_small_t, gather=(), tm=256):
    t_len, d = x.shape
    n = w_main.shape[1]
    ns = w_small.shape[1]
    nst = w_small_t.shape[0]
    ng = len(gather)
    steps = t_len // tm

    def body(*refs):
        x_ref, nw_ref, wm_ref, ws_ref, wst_ref = refs[:5]
        pm_ref, ps_ref, pst_ref, ht_ref, r_ref = refs[5 + ng:10 + ng]
        copies = lambda: _direct_copies(refs[5:5 + ng], refs[10 + ng:10 + 2 * ng], *refs[10 + 2 * ng:], (False,) * ng)
        if ng:
            pl.when(pl.program_id(0) == 0)(lambda: _start_all(copies()))
        xv = x_ref[...]
        r = lax.rsqrt(jnp.mean(xv * xv, axis=-1, keepdims=True) + EPS)
        h = xv * r * nw_ref[...]
        hb = h.astype(MXU_DTYPE)
        for n0 in range(0, n, 512):
            pm_ref[:, n0:n0 + 512] = jnp.dot(hb, wm_ref[:, n0:n0 + 512], preferred_element_type=F32)
        ps_ref[...] = jnp.dot(hb, ws_ref[...], preferred_element_type=F32)
        pst_ref[...] = lax.dot_general(wst_ref[...], hb, _NT, preferred_element_type=F32)
        ht_ref[...] = h.T.astype(MXU_DTYPE)
        r_ref[...] = r
        if ng:
            pl.when(pl.program_id(0) == steps - 1)(lambda: _wait_all(copies()))

    return pl.pallas_call(
        body, name="inproj",
        grid=(steps,),
        in_specs=[pl.BlockSpec((tm, d), lambda i: (i, 0)),
                  pl.BlockSpec((1, d), lambda i: (0, 0)),
                  pl.BlockSpec((d, n), lambda i: (0, 0)),
                  pl.BlockSpec((d, ns), lambda i: (0, 0)),
                  pl.BlockSpec((nst, d), lambda i: (0, 0))] + [_HBM] * ng,
        out_specs=[pl.BlockSpec((tm, n), lambda i: (i, 0)),
                   pl.BlockSpec((tm, ns), lambda i: (i, 0)),
                   pl.BlockSpec((nst, tm), lambda i: (0, i)),
                   pl.BlockSpec((d, tm), lambda i: (0, i)),
                   pl.BlockSpec((tm, 1), lambda i: (i, 0))] + [_HBM] * ng,
        out_shape=[jax.ShapeDtypeStruct((t_len, n), F32),
                   jax.ShapeDtypeStruct((t_len, ns), F32),
                   jax.ShapeDtypeStruct((nst, t_len), F32),
                   jax.ShapeDtypeStruct((d, t_len), MXU_DTYPE),
                   jax.ShapeDtypeStruct((t_len, 1), F32)] + _direct_out_shapes(gather, (False,) * ng),
        scratch_shapes=_direct_semaphores(ng) if ng else [],
        compiler_params=_params("arbitrary"),
    )(x, norm_w, w_main, w_small, w_small_t, *gather)


def _running_sum_mm(x, tri):
    hi = x.astype(MXU_DTYPE)
    lo = (x - hi.astype(F32)).astype(MXU_DTYPE)
    return jnp.dot(hi, tri, preferred_element_type=F32) + jnp.dot(lo, tri, preferred_element_type=F32)


def _sb_iotas():
    row_i = lax.broadcasted_iota(jnp.int32, (SB_BQ, SB_BLOCK), 0)
    col_i = lax.broadcasted_iota(jnp.int32, (SB_BQ, SB_BLOCK), 1)
    sq_r = lax.broadcasted_iota(jnp.int32, (SB_BLOCK, SB_BLOCK), 0)
    sq_c = lax.broadcasted_iota(jnp.int32, (SB_BLOCK, SB_BLOCK), 1)
    return row_i, col_i, sq_r, sq_c


SB_DIAG_BLOCKS = SB_BQ // SB_BLOCK
SB_EXP_FLOOR = -110.0


def _sb_keys_descending(qi, tile, carry, z_bounds, n_heads, has_free):
    group = SB_DIAG_BLOCKS
    n_free = group * qi
    diag = list(range(group - 1, -1, -1))
    carry = tile([n_free + j for j in diag], [True] * group, carry, [j * SB_BLOCK for j in diag])

    def largest_exponent(c):
        worst = jnp.max(z_bounds[0] - c[1])
        for h in range(1, n_heads):
            worst = jnp.maximum(worst, jnp.max(z_bounds[h] - c[1 + h]))
        return worst

    always = group if has_free else 0

    def cond(state):
        return (state[0] < n_free) & ((state[1] > SB_EXP_FLOOR) | (state[0] < always))

    def body(state):
        first = n_free - 1 - state[0]
        c = tile([first - j for j in range(group)], [False] * group, state[2:])
        return (state[0] + group, largest_exponent(c), *c)

    out = lax.while_loop(cond, body, (jnp.int32(0), largest_exponent(carry), *carry))
    return out[2:], out[0]


def _sb_keys_ascending(qi, n_run, tile, carry, has_free):
    group = SB_DIAG_BLOCKS
    n_free = group * qi
    diag = list(range(group))
    kjs, los, masked = [n_free + j for j in diag], [j * SB_BLOCK for j in diag], [True] * group
    if has_free:
        early = lambda s: [n_free - n_run + group * s + j for j in range(group)]
        carry = lax.fori_loop(0, n_run // group - 1, lambda s, c: tile(early(s), [False] * group, c), carry)
        kjs, los, masked = [n_free - group + j for j in range(group)] + kjs, [0] * group + los, [False] * group + masked
    return tile(kjs, masked, carry, los)


def _sb_fwd_call(proj, t_len):
    nq = t_len // SB_BQ
    scale = float(SB_HEAD_DIM) ** -0.5
    n_pairs = 512 // LANES
    per_pair = LANES // SB_HEAD_DIM

    def body(q_ref, k_ref, v_ref, o_ref, st_ref, nrun_ref):
        lane = lax.broadcasted_iota(jnp.int32, (1, LANES), 1)
        row_i, col_i, sq_r, sq_c = _sb_iotas()
        ge = (sq_r >= sq_c).astype(MXU_DTYPE)
        hms = [((lane // SB_HEAD_DIM) == hh).astype(F32) for hh in range(per_pair)]
        k_sq = k_ref[...] * k_ref[...]
        k_norms = [jnp.sqrt(jnp.max(jnp.sum(k_sq * hm, axis=-1, keepdims=True))) * (1.02 * scale) for hm in hms]

        def q_block(qi, has_free):
            r0 = qi * SB_BQ if isinstance(qi, int) else pl.multiple_of(qi * SB_BQ, SB_BQ)
            rows = pl.ds(r0, SB_BQ)
            q_all = q_ref[rows, :]
            qms = [(q_all * (hm * scale)).astype(MXU_DTYPE) for hm in hms]
            z_bounds = [jnp.sqrt(jnp.sum(q_all * q_all * hm, axis=-1, keepdims=True)) * kn
                        for hm, kn in zip(hms, k_norms)]

            def tile(kjs, masked, kc, los=None):
                heads = range(per_pair)
                los = los or [0] * len(kjs)
                pairs = [(t, h) for t in range(len(kjs)) for h in heads]
                add_rows = lambda full, lo, part: full + part if lo == 0 else jnp.concatenate(
                    [full[:lo], full[lo:] + part], axis=0)
                acc, cs = kc[0], list(kc[1:])
                s0s = [kj * SB_BLOCK if isinstance(kj, int) else pl.multiple_of(kj * SB_BLOCK, SB_BLOCK) for kj in kjs]
                kbs = [k_ref[pl.ds(s0, SB_BLOCK), :].astype(MXU_DTYPE) for s0 in s0s]
                v_alls = [v_ref[pl.ds(s0, SB_BLOCK), :] for s0 in s0s]
                vms = {(t, h): (v_alls[t] * hms[h]).astype(MXU_DTYPE) for t, h in pairs}
                zs = {(t, h): lax.dot_general(qms[h][los[t]:], kbs[t], _NT, preferred_element_type=F32)
                      for t, h in pairs}
                masks = [(col_i[lo:] + s0) < (row_i[lo:] + r0) if m else None for m, lo, s0 in zip(masked, los, s0s)]
                keep = lambda t, a: a if masks[t] is None else jnp.where(masks[t], a, 0.0)
                sps = {(t, h): keep(t, _softplus(zs[t, h])) for t, h in pairs}
                sums = {p: _running_sum_mm(sps[p], ge) for p in pairs}
                mass = {}
                for t, h in pairs:
                    mass[t, h] = cs[h] if t == 0 else add_rows(
                        mass[t - 1, h], los[t - 1], jnp.sum(sps[t - 1, h], axis=-1, keepdims=True))
                ws = {(t, h): keep(t, jnp.exp(zs[t, h] - (sums[t, h] + mass[t, h][los[t]:]))) for t, h in pairs}
                for t, h in pairs:
                    acc = add_rows(acc, los[t], jnp.dot(ws[t, h].astype(MXU_DTYPE), vms[t, h],
                                                        preferred_element_type=F32))
                last = len(kjs) - 1
                cs = [add_rows(mass[last, h], los[last], jnp.sum(sps[last, h], axis=-1, keepdims=True)) for h in heads]
                return (acc, *cs)

            zero_col = jnp.zeros((SB_BQ, 1), F32)
            out, n_run = _sb_keys_descending(
                qi, tile, (jnp.zeros((SB_BQ, LANES), F32),) + (zero_col,) * per_pair, z_bounds, per_pair, has_free)
            o_ref[rows, :] = out[0]
            for hh in range(per_pair):
                st_ref[hh, rows, :] = out[1 + hh]
            nrun_ref[pl.program_id(0), qi] = n_run

        q_block(0, False)
        lax.fori_loop(1, nq, lambda qi, carry: (q_block(qi, True), carry)[1], 0)

    return pl.pallas_call(
        body, name="sb_fwd",
        grid=(n_pairs,),
        in_specs=[pl.BlockSpec((t_len, LANES), lambda p: (0, p)),
                  pl.BlockSpec((t_len, LANES), lambda p: (0, n_pairs + p)),
                  pl.BlockSpec((t_len, LANES), lambda p: (0, 2 * n_pairs + p))],
        out_specs=[pl.BlockSpec((t_len, LANES), lambda p: (0, p)),
                   pl.BlockSpec((per_pair, t_len, 1), lambda p: (p, 0, 0)),
                   pl.BlockSpec(memory_space=pltpu.SMEM)],
        out_shape=[jax.ShapeDtypeStruct((t_len, 512), F32),
                   jax.ShapeDtypeStruct((n_pairs * per_pair, t_len, 1), F32),
                   jax.ShapeDtypeStruct((n_pairs, nq), jnp.int32)],
        compiler_params=_params("arbitrary"),
    )(proj, proj, proj)


def _sb_bwd_call(proj, sp_total, n_run_all, d_o, dproj, t_len):
    nq = t_len // SB_BQ
    scale = float(SB_HEAD_DIM) ** -0.5
    n_pairs = 512 // LANES
    per_pair = LANES // SB_HEAD_DIM

    def body(q_ref, k_ref, v_ref, st_ref, nrun_ref, do_ref, dproj_in_ref, d_ref):
        lane = lax.broadcasted_iota(jnp.int32, (1, LANES), 1)
        row_i, col_i, sq_r, sq_c = _sb_iotas()
        lt = (sq_r < sq_c).astype(MXU_DTYPE)
        le = (sq_r <= sq_c).astype(MXU_DTYPE)
        hms = [((lane // SB_HEAD_DIM) == hh).astype(F32) for hh in range(per_pair)]
        d_ref[1] = jnp.zeros((t_len, LANES), F32)
        d_ref[2] = jnp.zeros((t_len, LANES), F32)

        def q_block(qi, has_free):
            r0 = qi * SB_BQ if isinstance(qi, int) else pl.multiple_of(qi * SB_BQ, SB_BQ)
            rows = pl.ds(r0, SB_BQ)
            q_all, do_all = q_ref[rows, :], do_ref[rows, :]
            qms = [(q_all * (hm * scale)).astype(MXU_DTYPE) for hm in hms]
            doms = [(do_all * hm).astype(MXU_DTYPE) for hm in hms]
            totals = [st_ref[hh, rows, :] for hh in range(per_pair)]

            def tile(kjs, masked, kc, los=None):
                heads = range(per_pair)
                los = los or [0] * len(kjs)
                pairs = [(t, h) for t in range(len(kjs)) for h in heads]
                add_rows = lambda full, lo, part: full + part if lo == 0 else jnp.concatenate(
                    [full[:lo], full[lo:] + part], axis=0)
                rsum = lambda a: jnp.sum(a, axis=-1, keepdims=True)
                dq, cls, gls = kc[0], list(kc[1:1 + per_pair]), list(kc[1 + per_pair:])
                s0s = [kj * SB_BLOCK if isinstance(kj, int) else pl.multiple_of(kj * SB_BLOCK, SB_BLOCK) for kj in kjs]
                k_alls = [k_ref[pl.ds(s0, SB_BLOCK), :] for s0 in s0s]
                v_alls = [v_ref[pl.ds(s0, SB_BLOCK), :] for s0 in s0s]
                kbs = [k_all.astype(MXU_DTYPE) for k_all in k_alls]
                vms = {(t, h): (v_alls[t] * hms[h]).astype(MXU_DTYPE) for t, h in pairs}
                kms = {(t, h): (k_alls[t] * (hms[h] * scale)).astype(MXU_DTYPE) for t, h in pairs}
                q_live = {(t, h): qms[h][los[t]:] for t, h in pairs}
                do_live = {(t, h): doms[h][los[t]:] for t, h in pairs}
                zs = {p: lax.dot_general(q_live[p], kbs[p[0]], _NT, preferred_element_type=F32) for p in pairs}
                das = {p: lax.dot_general(do_live[p], vms[p], _NT, preferred_element_type=F32) for p in pairs}
                masks = [(col_i[lo:] + s0) < (row_i[lo:] + r0) if m else None for m, lo, s0 in zip(masked, los, s0s)]
                keep = lambda t, a: a if masks[t] is None else jnp.where(masks[t], a, 0.0)
                sp_alls = {p: _softplus(zs[p]) for p in pairs}
                sps = {(t, h): keep(t, sp_alls[t, h]) for t, h in pairs}
                lefts = {p: _running_sum_mm(sps[p], lt) for p in pairs}
                cl = {}
                for t, h in pairs:
                    cl[t, h] = cls[h] if t == 0 else add_rows(cl[t - 1, h], los[t - 1], rsum(sps[t - 1, h]))
                ws = {(t, h): keep(t, jnp.exp(zs[t, h] - ((totals[h] - cl[t, h])[los[t]:] - lefts[t, h])))
                      for t, h in pairs}
                gs = {p: das[p] * ws[p] for p in pairs}
                g_sums = {p: _running_sum_mm(gs[p], le) for p in pairs}
                gl = {}
                for t, h in pairs:
                    gl[t, h] = gls[h] if t == 0 else add_rows(gl[t - 1, h], los[t - 1], rsum(gs[t - 1, h]))
                dzs = {(t, h): keep(t, gs[t, h] - jnp.exp(zs[t, h] - sp_alls[t, h]) * (gl[t, h][los[t]:] + g_sums[t, h])
                               ).astype(MXU_DTYPE) for t, h in pairs}
                for t in range(len(kjs)):
                    dk_t = jnp.zeros((SB_BLOCK, LANES), F32)
                    dv_t = jnp.zeros((SB_BLOCK, LANES), F32)
                    for h in heads:
                        dq = add_rows(dq, los[t], jnp.dot(dzs[t, h], kms[t, h], preferred_element_type=F32))
                        dk_t = dk_t + lax.dot_general(dzs[t, h], q_live[t, h], _TN, preferred_element_type=F32)
                        dv_t = dv_t + lax.dot_general(ws[t, h].astype(MXU_DTYPE), do_live[t, h], _TN,
                                                      preferred_element_type=F32)
                    d_ref[1, pl.ds(s0s[t], SB_BLOCK), :] += dk_t
                    d_ref[2, pl.ds(s0s[t], SB_BLOCK), :] += dv_t
                last = len(kjs) - 1
                cls = [add_rows(cl[last, h], los[last], rsum(sps[last, h])) for h in heads]
                gls = [add_rows(gl[last, h], los[last], rsum(gs[last, h])) for h in heads]
                return (dq, *cls, *gls)

            zero_col = jnp.zeros((SB_BQ, 1), F32)
            out = _sb_keys_ascending(qi, nrun_ref[pl.program_id(0), qi], tile,
                                     (jnp.zeros((SB_BQ, LANES), F32),) + (zero_col,) * (2 * per_pair), has_free)
            d_ref[0, rows, :] = out[0]

        q_block(0, False)
        lax.fori_loop(1, nq, lambda qi, carry: (q_block(qi, True), carry)[1], 0)

    col = lambda off: pl.BlockSpec((t_len, LANES), lambda p: (0, off + p))
    return pl.pallas_call(
        body, name="sb_bwd",
        grid=(n_pairs,),
        in_specs=[col(0), col(n_pairs), col(2 * n_pairs),
                  pl.BlockSpec((per_pair, t_len, 1), lambda p: (p, 0, 0)),
                  pl.BlockSpec(memory_space=pltpu.SMEM), col(0), _HBM],
        out_specs=pl.BlockSpec((3, t_len, LANES), lambda p: (DPROJ_SB_SLOT // 3, 0, p)),
        out_shape=jax.ShapeDtypeStruct(dproj.shape, dproj.dtype),
        input_output_aliases={6: 0},
        compiler_params=_params("arbitrary"),
    )(proj, proj, proj, sp_total, n_run_all, d_o, dproj)


def _conv_taps(xin, rows, t_len):
    taps = []
    for i in range(CONV_WIDTH):
        shift = CONV_WIDTH - 1 - i
        if shift == 0:
            taps.append(xin)
        else:
            taps.append(jnp.where(rows >= shift, pltpu.roll(xin, shift, axis=0), 0.0))
    return taps


def _gdn_prep_body_common(x_ref, w_ref, t_len):
    j = pl.program_id(0)
    xin = x_ref[...]
    rows = lax.broadcasted_iota(jnp.int32, (t_len, LANES), 0)
    taps = _conv_taps(xin, rows, t_len)
    pre = taps[0] * w_ref[0:1, :]
    for i in range(1, CONV_WIDTH):
        pre = pre + taps[i] * w_ref[i:i + 1, :]
    sg = _sigmoid(pre)
    act = pre * sg
    is_qk = j < 2 * GDN_HEADS
    nrm = jnp.where(is_qk, lax.rsqrt(jnp.sum(act * act, axis=-1, keepdims=True) + EPS), 1.0)
    sc = jnp.where(j < GDN_HEADS, float(GDN_HEAD_DIM) ** -0.5, 1.0)
    return j, rows, taps, pre, sg, act, is_qk, nrm, sc


def _gdn_prep_call(proj, conv_w, t_len):
    first = 2048 // LANES

    def body(x_ref, w_ref, out_ref):
        _, _, _, _, _, act, _, nrm, sc = _gdn_prep_body_common(x_ref, w_ref, t_len)
        out_ref[...] = act * nrm * sc

    return pl.pallas_call(
        body, name="gdn_prep",
        grid=(3 * GDN_HEADS,),
        in_specs=[pl.BlockSpec((t_len, LANES), lambda j: (0, first + j)),
                  pl.BlockSpec((CONV_WIDTH, LANES), lambda j: (0, j))],
        out_specs=pl.BlockSpec((t_len, LANES), lambda j: (0, j)),
        out_shape=jax.ShapeDtypeStruct((t_len, 3 * 512), F32),
        compiler_params=_params("arbitrary"),
    )(proj, conv_w)


def _gdn_prep_bwd_call(proj, conv_w, d_act3, dproj, t_len):
    first = 2048 // LANES

    def body(x_ref, w_ref, d_ref, dproj_in_ref, dx_ref, dw_ref):
        _, rows, taps, pre, sg, act, is_qk, nrm, sc = _gdn_prep_body_common(x_ref, w_ref, t_len)
        d_out = d_ref[0]
        dn = d_out * sc
        d_norm = nrm * dn - act * (nrm * nrm * nrm) * jnp.sum(dn * act, axis=-1, keepdims=True)
        d_act = jnp.where(is_qk, d_norm, d_out)
        d_pre = d_act * sg * (1.0 + pre * (1.0 - sg))
        dx = d_pre * w_ref[CONV_WIDTH - 1:CONV_WIDTH, :]
        dw_ref[CONV_WIDTH - 1:CONV_WIDTH, :] = jnp.sum(d_pre * taps[CONV_WIDTH - 1], axis=0, keepdims=True)
        for i in range(CONV_WIDTH - 1):
            shift = CONV_WIDTH - 1 - i
            up = jnp.where(rows < t_len - shift, pltpu.roll(d_pre, t_len - shift, axis=0), 0.0)
            dx = dx + up * w_ref[i:i + 1, :]
            dw_ref[i:i + 1, :] = jnp.sum(d_pre * taps[i], axis=0, keepdims=True)
        dx_ref[0] = dx

    return pl.pallas_call(
        body, name="gdn_prep_bwd",
        grid=(3 * GDN_HEADS,),
        in_specs=[pl.BlockSpec((t_len, LANES), lambda j: (0, first + j)),
                  pl.BlockSpec((CONV_WIDTH, LANES), lambda j: (0, j)),
                  pl.BlockSpec((1, t_len, LANES), lambda j: (j // GDN_HEADS, 0, j % GDN_HEADS)), _HBM],
        out_specs=[pl.BlockSpec((1, t_len, LANES), lambda j: (DPROJ_GDN_SLOT + j // GDN_HEADS, 0, j % GDN_HEADS)),
                   pl.BlockSpec((CONV_WIDTH, LANES), lambda j: (0, j))],
        out_shape=[jax.ShapeDtypeStruct(dproj.shape, dproj.dtype),
                   jax.ShapeDtypeStruct((CONV_WIDTH, 3 * 512), F32)],
        input_output_aliases={3: 0},
        compiler_params=_params("arbitrary"),
    )(proj, conv_w, d_act3, dproj)


def _chunk_cumsum_matrix():
    r = lax.broadcasted_iota(jnp.int32, (LANES, LANES), 0)
    c = lax.broadcasted_iota(jnp.int32, (LANES, LANES), 1)
    return ((r <= c) & ((r // CHUNK) == (c // CHUNK))).astype(F32)


def _gdn_gates_call(ps, pst, alog_l, dtb_l, alog_c, dtb_c, t_len):
    def body(ps_ref, pst_ref, al_ref, dl_ref, ac_ref, dc_ref, beta_ref, gcol_ref, grow_ref):
        upper = _chunk_cumsum_matrix()
        lower = upper.T
        psv = ps_ref[...]
        beta_ref[...] = _sigmoid(psv)
        g_l = -jnp.exp(al_ref[...]) * _softplus(psv + dl_ref[...])
        g_r = -jnp.exp(ac_ref[...]) * _softplus(pst_ref[...] + dc_ref[...])
        for w in range(t_len // LANES):
            sl = slice(w * LANES, (w + 1) * LANES)
            gcol_ref[sl, :] = _mx(lower, g_l[sl, :])
            grow_ref[:, sl] = _mx(g_r[:, sl], upper)

    vm = pl.BlockSpec(memory_space=pltpu.VMEM)
    return pl.pallas_call(
        body, name="gdn_gates",
        in_specs=[vm] * 6, out_specs=[vm] * 3,
        out_shape=[jax.ShapeDtypeStruct((t_len, LANES), F32),
                   jax.ShapeDtypeStruct((t_len, LANES), F32),
                   jax.ShapeDtypeStruct((8, t_len), F32)],
        compiler_params=pltpu.CompilerParams(vmem_limit_bytes=VMEM_LIMIT_BYTES),
    )(ps, pst, alog_l, dtb_l, alog_c, dtb_c)


def _gdn_gates_bwd_call(ps, alog_l, dtb_l, d_l, t_len):
    def body(ps_ref, al_ref, dl_ref, d_ref, dps_ref, gal_ref, gdt_ref):
        lane = lax.broadcasted_iota(jnp.int32, (1, LANES), 1)
        psv = ps_ref[...]
        dv = d_ref[...]
        beta = _sigmoid(psv)
        ea = jnp.exp(al_ref[...])
        arg = psv + dl_ref[...]
        g = -ea * _softplus(arg)
        d_a = dv * (-ea) * _sigmoid(arg)
        is_a = (lane >= GDN_HEADS) & (lane < 2 * GDN_HEADS)
        dps_ref[...] = jnp.where(lane < GDN_HEADS, dv * beta * (1.0 - beta), jnp.where(is_a, d_a, 0.0))
        gdt_ref[...] = jnp.where(is_a, jnp.sum(d_a, axis=0, keepdims=True), 0.0)
        gal_ref[...] = jnp.where(is_a, jnp.sum(dv * g, axis=0, keepdims=True), 0.0)

    vm = pl.BlockSpec(memory_space=pltpu.VMEM)
    return pl.pallas_call(
        body, name="gdn_gates_bwd",
        in_specs=[vm] * 4, out_specs=[vm] * 3,
        out_shape=[jax.ShapeDtypeStruct((t_len, LANES), F32),
                   jax.ShapeDtypeStruct((1, LANES), F32),
                   jax.ShapeDtypeStruct((1, LANES), F32)],
        compiler_params=pltpu.CompilerParams(vmem_limit_bytes=VMEM_LIMIT_BYTES),
    )(ps, alog_l, dtb_l, d_l)


def _bm(a, b):
    return _m3_general(a, b, _BNN)


def _bm_nt(a, b):
    return _m3_general(a, b, _BNT)


def _bm_tn(a, b):
    return _m3_general(a, b, _BTN)


def _heads_of(ref, rows):
    return jnp.stack([ref[rows, h * GDN_HEAD_DIM:(h + 1) * GDN_HEAD_DIM] for h in range(GDN_HEADS)])


def _chunk_terms(q_ref, k_ref, v_ref, b_ref, gc_ref, gr_ref, c, incl, strict, n=1, scores=True):
    r0 = c * CHUNK if isinstance(c, int) else pl.multiple_of(c * CHUNK, CHUNK)
    rows = pl.ds(r0, n * CHUNK)
    per_chunk = lambda x: x.reshape(GDN_HEADS * n, CHUNK, x.shape[-1])
    q, k, v = (per_chunk(_heads_of(ref, rows)) for ref in (q_ref, k_ref, v_ref))
    lane_ids = lax.broadcasted_iota(jnp.int32, (1, LANES), 1)
    pick = lambda slab, first: jnp.stack([jnp.sum(jnp.where(lane_ids == first + h, slab, 0.0), axis=-1, keepdims=True)
                                          for h in range(GDN_HEADS)])
    b = per_chunk(pick(b_ref[rows, :], 0))
    gc = per_chunk(pick(gc_ref[rows, :], GDN_HEADS))
    gr = gr_ref[:, c] if n == 1 else gr_ref[:, c:c + n].reshape(GDN_HEADS * n, 1, CHUNK)
    dm = jnp.where(incl, jnp.exp(jnp.where(incl, gc - gr, 0.0)), 0.0)
    kb = k * b
    vb = v * b
    e = jnp.exp(gc)
    a = p = None
    if scores:
        kk_qk = _bm_nt(jnp.concatenate([kb, q], axis=1), k)
        a = jnp.where(strict, kk_qk[:, :CHUNK] * dm, 0.0)
        p = jnp.where(incl, kk_qk[:, CHUNK:] * dm, 0.0)
    gl = gc[:, CHUNK - 1:CHUNK, :]
    eg = jnp.exp(gl - gc)
    return rows, q, k, v, b, gc, dm, kb, vb, e, a, p, gl, eg


def _unit_lower_inverse(a, eye):
    x = -a
    tm = eye + x
    xp = _bm(x, x)
    for _ in range(4):
        both = _bm(jnp.concatenate([xp, tm], axis=1), xp)
        tm = tm + both[:, CHUNK:]
        xp = both[:, :CHUNK]
    return tm + _bm(tm, xp)


def _gdn_specs(t_len, n_chunks, reverse):
    cps = GDN_CHUNKS_PER_STEP
    steps = n_chunks // cps
    at = (lambda g: steps - 1 - g) if reverse else (lambda g: g)
    rows_blk = lambda width, part=0: pl.BlockSpec((cps * CHUNK, width), lambda g: (at(g), part))
    gate_r = pl.BlockSpec((GDN_HEADS, cps, 1, CHUNK), lambda g: (0, at(g), 0, 0))
    per_chunk = lambda r, c: pl.BlockSpec((GDN_HEADS, cps, r, c), lambda g: (0, at(g), 0, 0))
    return cps, steps, rows_blk, gate_r, per_chunk


def _gdn_fwd_call(gact, beta_c, gam_c, gam_r, t_len):
    n_chunks = t_len // CHUNK
    dk = GDN_HEAD_DIM
    width = GDN_HEADS * dk
    cps, steps, rows_blk, gate_r, per_chunk = _gdn_specs(t_len, n_chunks, False)

    def body(q_ref, k_ref, v_ref, b_ref, gc_ref, gr_ref, o_ref, s_ref, t_ref, a_ref, p_ref, uw_ref, vn_ref, state_ref):
        row = lax.broadcasted_iota(jnp.int32, (CHUNK, CHUNK), 0)
        col = lax.broadcasted_iota(jnp.int32, (CHUNK, CHUNK), 1)
        incl, strict = row >= col, row > col
        eye = (row == col).astype(F32)

        @pl.when(pl.program_id(0) == 0)
        def _():
            state_ref[...] = jnp.zeros_like(state_ref)

        _, q, k, v, b, gc, dm, kb, vb, e, a, p, gl, eg = _chunk_terms(
            q_ref, k_ref, v_ref, b_ref, gc_ref, gr_ref, 0, incl, strict, cps)
        tm = _unit_lower_inverse(a, eye)
        uw = _bm(tm, jnp.concatenate([vb, kb * e], axis=2))
        w_qe = jnp.concatenate([uw[:, :, dk:], q * e], axis=1)
        u, kd, decay = uw[:, :, :dk], k * eg, jnp.exp(gl)
        per_chunk_block = lambda x: x.reshape(GDN_HEADS, cps, CHUNK, CHUNK)
        t_ref[...], a_ref[...], p_ref[...] = per_chunk_block(tm), per_chunk_block(a), per_chunk_block(p)
        uw_heads = uw.reshape(GDN_HEADS, cps * CHUNK, 2 * dk)
        for h in range(GDN_HEADS):
            uw_ref[:, h * 2 * dk:(h + 1) * 2 * dk] = uw_heads[h]

        of_chunk = lambda x, c: jnp.stack([x[h * cps + c] for h in range(GDN_HEADS)])
        s = state_ref[...]
        for c in range(cps):
            ws_qs = _bm(of_chunk(w_qe, c), s)
            vn = of_chunk(u, c) - ws_qs[:, :CHUNK]
            o = ws_qs[:, CHUNK:] + _bm(of_chunk(p, c), vn)
            for h in range(GDN_HEADS):
                o_ref[c * CHUNK:(c + 1) * CHUNK, h * dk:(h + 1) * dk] = o[h]
                vn_ref[c * CHUNK:(c + 1) * CHUNK, h * dk:(h + 1) * dk] = vn[h]
            s_ref[:, c] = s
            s = s * of_chunk(decay, c) + _bm_tn(of_chunk(kd, c), vn)
        state_ref[...] = s

    scores = jax.ShapeDtypeStruct((GDN_HEADS, n_chunks, CHUNK, CHUNK), F32)
    return pl.pallas_call(
        body, name="gdn_fwd",
        grid=(steps,),
        in_specs=[rows_blk(width, 0), rows_blk(width, 1), rows_blk(width, 2), rows_blk(LANES), rows_blk(LANES), gate_r],
        out_specs=[rows_blk(width), per_chunk(dk, dk), per_chunk(CHUNK, CHUNK), per_chunk(CHUNK, CHUNK),
                   per_chunk(CHUNK, CHUNK), rows_blk(2 * width), rows_blk(width)],
        out_shape=[jax.ShapeDtypeStruct((t_len, width), F32),
                   jax.ShapeDtypeStruct((GDN_HEADS, n_chunks, dk, dk), F32), scores, scores, scores,
                   jax.ShapeDtypeStruct((t_len, 2 * width), F32), jax.ShapeDtypeStruct((t_len, width), F32)],
        scratch_shapes=[pltpu.VMEM((GDN_HEADS, dk, dk), F32)],
        compiler_params=_params("arbitrary"),
    )(gact, gact, gact, beta_c, gam_c, gam_r)


def _gdn_bwd_call(gact, beta_c, gam_c, gam_r, saved, d_o, t_len, scatter=()):
    n_chunks = t_len // CHUNK
    dk = GDN_HEAD_DIM
    width = GDN_HEADS * dk
    cps, steps, rows_blk, gate_r, per_chunk = _gdn_specs(t_len, n_chunks, True)
    nx = len(scatter)
    n_in = 13

    def body(*refs):
        q_ref, k_ref, v_ref, b_ref, gc_ref, gr_ref = refs[:6]
        saved_refs, do_ref = refs[6:12], refs[12]
        d_ref, dgate_ref = refs[n_in + nx:n_in + 2 + nx]
        dstate_ref = refs[n_in + 2 + 2 * nx]
        copies = lambda: _direct_copies(refs[n_in:n_in + nx], refs[n_in + 2 + nx:n_in + 2 + 2 * nx],
                                        *refs[n_in + 3 + 2 * nx:], (True,) * nx)
        if nx:
            pl.when(pl.program_id(0) == 0)(lambda: _start_all(copies()))
        row = lax.broadcasted_iota(jnp.int32, (CHUNK, CHUNK), 0)
        col = lax.broadcasted_iota(jnp.int32, (CHUNK, CHUNK), 1)
        incl, strict = row >= col, row > col
        ng = GDN_BWD_GROUP
        nb = GDN_HEADS * ng
        upper = jnp.broadcast_to((row <= col).astype(F32), (nb, CHUNK, CHUNK))
        ones = jnp.ones((nb, CHUNK, LANES), F32)
        last_row = lax.broadcasted_iota(jnp.int32, (CHUNK, 1), 0) == CHUNK - 1
        lane_ids = lax.broadcasted_iota(jnp.int32, (1, LANES), 1)
        rsum = lambda m: jnp.sum(m, axis=-1, keepdims=True)
        total = lambda m: jnp.sum(rsum(m), axis=1, keepdims=True)
        of_chunk = lambda x, c: jnp.stack([x[h * ng + c] for h in range(GDN_HEADS)])

        @pl.when(pl.program_id(0) == 0)
        def _():
            dstate_ref[...] = jnp.zeros_like(dstate_ref)

        for c0 in range(cps - ng, -1, -ng):
            group(c0, q_ref, k_ref, v_ref, b_ref, gc_ref, gr_ref, saved_refs, do_ref, d_ref, dgate_ref, dstate_ref,
                  incl, strict, upper, ones, last_row, lane_ids, rsum, total, of_chunk)
        if nx:
            pl.when(pl.program_id(0) == steps - 1)(lambda: _wait_all(copies()))

    def group(c0, q_ref, k_ref, v_ref, b_ref, gc_ref, gr_ref, saved_refs, do_ref, d_ref, dgate_ref, dstate_ref,
              incl, strict, upper, ones, last_row, lane_ids, rsum, total, of_chunk):
        ng = GDN_BWD_GROUP
        nb = GDN_HEADS * ng
        rows = pl.ds(c0 * CHUNK, ng * CHUNK)
        s_ref, t_ref, a_ref, p_ref, uw_ref, vn_ref = saved_refs
        _, q, k, v, b, gc, dm, kb, vb, e, _, _, gl, eg = _chunk_terms(
            q_ref, k_ref, v_ref, b_ref, gc_ref, gr_ref, c0, incl, strict, ng, scores=False)
        s = s_ref[:, c0:c0 + ng].reshape(nb, dk, dk)
        tm = t_ref[:, c0:c0 + ng].reshape(nb, CHUNK, CHUNK)
        a = a_ref[:, c0:c0 + ng].reshape(nb, CHUNK, CHUNK)
        p = p_ref[:, c0:c0 + ng].reshape(nb, CHUNK, CHUNK)
        d_out = _heads_of(do_ref, rows).reshape(nb, CHUNK, dk)
        vn = _heads_of(vn_ref, rows).reshape(nb, CHUNK, dk)
        uw = jnp.stack([uw_ref[rows, h * 2 * dk:(h + 1) * 2 * dk] for h in range(GDN_HEADS)]).reshape(nb, CHUNK, 2 * dk)
        u, w = uw[:, :, :dk], uw[:, :, dk:]
        el = jnp.exp(gl)
        kbe = kb * e
        qe = q * e
        kd = k * eg
        pt_do = _bm_tn(p, d_out)
        qet_do = _bm_tn(qe, d_out)

        ds = dstate_ref[...]
        d_vn_c, ds_c = [None] * ng, [None] * ng
        for c in range(ng - 1, -1, -1):
            ds_c[c] = ds
            d_vn_c[c] = of_chunk(pt_do, c) + _bm(of_chunk(kd, c), ds)
            ds = of_chunk(el, c) * ds + of_chunk(qet_do, c) - _bm_tn(of_chunk(w, c), d_vn_c[c])
        dstate_ref[...] = ds
        by_chunk = lambda xs: jnp.stack([xs[c][h] for h in range(GDN_HEADS) for c in range(ng)])
        d_vn, ds = by_chunk(d_vn_c), by_chunk(ds_c)

        on_s = _bm_nt(jnp.concatenate([d_out, d_vn], axis=1), s)
        d_qe, d_w = on_s[:, :CHUNK], -on_s[:, CHUNK:]
        d_p = jnp.where(incl, _bm_nt(d_out, vn), 0.0)
        d_kd = _bm_nt(vn, ds)
        d_both = _bm_tn(tm, jnp.concatenate([d_vn, d_w], axis=2))
        d_vb, d_kbe = d_both[:, :, :dk], d_both[:, :, dk:]
        d_a = -jnp.where(strict, _bm_nt(d_both, uw), 0.0)
        m = d_a * dm
        n = d_p * dm
        on_k = _bm(jnp.concatenate([m, n], axis=1), k)
        d_kb = on_k[:, :CHUNK] + d_kbe * e
        d_q = on_k[:, CHUNK:] + d_qe * e
        d_k = (_bm_tn(jnp.concatenate([m, n], axis=1), jnp.concatenate([kb, q], axis=1))
               + d_kd * eg + b * d_kb)
        d_v = b * d_vb
        r = d_a * a + d_p * p
        kd_term = rsum(d_kd * kd)
        d_gl = total(ds * s) * el + jnp.sum(kd_term, axis=1, keepdims=True)
        d_gam = (rsum(r) - _bm_tn(r, ones)[:, :, 0:1] + rsum(d_qe * qe) + rsum(d_kbe * kbe) - kd_term
                 + jnp.where(last_row, d_gl, 0.0))
        d_beta = rsum(d_kb * k) + rsum(d_vb * v)
        d_g = _bm(upper, d_gam * ones)[:, :, 0:1]
        per_head = lambda x: x.reshape(GDN_HEADS, ng * CHUNK, x.shape[-1])
        d_q, d_k, d_v, d_beta, d_g = (per_head(x) for x in (d_q, d_k, d_v, d_beta, d_g))
        gates = jnp.zeros((ng * CHUNK, LANES), F32)
        for h in range(GDN_HEADS):
            lanes = slice(h * dk, (h + 1) * dk)
            d_ref[0, rows, lanes] = d_q[h]
            d_ref[1, rows, lanes] = d_k[h]
            d_ref[2, rows, lanes] = d_v[h]
            gates = gates + (jnp.where(lane_ids == h, d_beta[h], 0.0)
                             + jnp.where(lane_ids == GDN_HEADS + h, d_g[h], 0.0))
        dgate_ref[rows, :] = gates

    d_spec = pl.BlockSpec((3, cps * CHUNK, width), lambda g: (0, steps - 1 - g, 0))
    return pl.pallas_call(
        body, name="gdn_bwd",
        grid=(steps,),
        in_specs=[rows_blk(width, 0), rows_blk(width, 1), rows_blk(width, 2), rows_blk(LANES), rows_blk(LANES), gate_r,
                  per_chunk(dk, dk), per_chunk(CHUNK, CHUNK), per_chunk(CHUNK, CHUNK), per_chunk(CHUNK, CHUNK),
                  rows_blk(2 * width), rows_blk(width), rows_blk(width)] + [_HBM] * nx,
        out_specs=[d_spec, rows_blk(LANES)] + [_HBM] * nx,
        out_shape=[jax.ShapeDtypeStruct((3, t_len, width), F32),
                   jax.ShapeDtypeStruct((t_len, LANES), F32)] + _direct_out_shapes(scatter, (True,) * nx),
        scratch_shapes=[pltpu.VMEM((GDN_HEADS, dk, dk), F32)] + (_direct_semaphores(nx) if nx else []),
        compiler_params=_params("arbitrary"),
    )(gact, gact, gact, beta_c, gam_c, gam_r, *saved, d_o, *scatter)


def _group_matrix(width, group):
    r = lax.broadcasted_iota(jnp.int32, (width, width), 0)
    c = lax.broadcasted_iota(jnp.int32, (width, width), 1)
    return ((r // group) == (c // group)).astype(F32)


def _post_call(o_sb, o_gd, proj, x, target, w_out, sbw, gdw, fw, tm=256):
    t_len, d = x.shape
    half = 512
    zsb_blk = 1536 // half
    zgd_blk = 3584 // half

    def body(osb_ref, ogd_ref, zsb_ref, zgd_ref, x_ref, tg_ref, wo_ref, sbw_ref, gdw_ref, fw_ref,
             dx2_ref, dosb_ref, dogd_ref, dz_ref, loss_ref, gfw_ref, gsb_ref, ggd_ref, gwo_ref):
        step = pl.program_id(0)

        @pl.when(step == 0)
        def _():
            loss_ref[...] = jnp.zeros_like(loss_ref)
            gfw_ref[...] = jnp.zeros_like(gfw_ref)
            gsb_ref[...] = jnp.zeros_like(gsb_ref)
            ggd_ref[...] = jnp.zeros_like(ggd_ref)
            gwo_ref[...] = jnp.zeros_like(gwo_ref)

        def head_forward(o, z, w, gmat, inv):
            r = lax.rsqrt(_running_sum_mm(o * o, gmat) * inv + EPS)
            nrm = o * r * w
            sg = _sigmoid(z)
            return r, nrm, sg, nrm * (z * sg)

        def head_backward(d_m, o, z, w, gmat, inv, r, nrm, sg):
            d_n = d_m * (z * sg)
            d_z = d_m * nrm * (sg * (1.0 + z * (1.0 - sg)))
            dnw = d_n * w
            d_o = r * dnw - o * (r * r * r) * (_running_sum_mm(dnw * o, gmat) * inv)
            return d_o, d_z, jnp.sum(d_n * o * r, axis=0, keepdims=True)

        g_sb = _group_matrix(half, SB_HEAD_DIM).astype(MXU_DTYPE)
        g_gd = _group_matrix(half, GDN_HEAD_DIM).astype(MXU_DTYPE)
        osb, ogd, zsb, zgd = osb_ref[...], ogd_ref[...], zsb_ref[...], zgd_ref[...]
        sbw_v, gdw_v = sbw_ref[...], gdw_ref[...]
        r_sb, n_sb, sg_sb, m_sb = head_forward(osb, zsb, sbw_v, g_sb, 1.0 / SB_HEAD_DIM)
        r_gd, n_gd, sg_gd, m_gd = head_forward(ogd, zgd, gdw_v, g_gd, 1.0 / GDN_HEAD_DIM)
        mixed = jnp.concatenate([m_sb, m_gd], axis=1).astype(MXU_DTYPE)
        wo = wo_ref[...]
        x2 = x_ref[...] + jnp.dot(mixed, wo, preferred_element_type=F32)
        r2 = lax.rsqrt(jnp.mean(x2 * x2, axis=-1, keepdims=True) + EPS)
        fw_v = fw_ref[...]
        err = x2 * r2 * fw_v - tg_ref[...]
        loss_ref[...] += 0.5 * jnp.sum(jnp.sum(err * err, axis=-1, keepdims=True) * (1.0 / d))
        dy = err * (1.0 / d)
        gg = dy * fw_v
        dx2 = r2 * gg - x2 * ((r2 * r2 * r2) * jnp.mean(gg * x2, axis=-1, keepdims=True))
        gfw_ref[...] += jnp.sum(dy * x2 * r2, axis=0, keepdims=True)
        dx2_ref[...] = dx2
        dx2b = dx2.astype(MXU_DTYPE)
        d_mixed = lax.dot_general(dx2b, wo, _NT, preferred_element_type=F32)
        gwo_ref[...] += lax.dot_general(mixed, dx2b, _TN, preferred_element_type=F32)
        d_osb, d_zsb, gsb = head_backward(d_mixed[:, :half], osb, zsb, sbw_v, g_sb, 1.0 / SB_HEAD_DIM, r_sb, n_sb, sg_sb)
        d_ogd, d_zgd, ggd = head_backward(d_mixed[:, half:], ogd, zgd, gdw_v, g_gd, 1.0 / GDN_HEAD_DIM, r_gd, n_gd, sg_gd)
        dosb_ref[...] = d_osb
        dogd_ref[...] = d_ogd
        dz_ref[0] = d_zsb
        dz_ref[1] = d_zgd
        gsb_ref[...] += gsb
        ggd_ref[...] += ggd

    row_blk = lambda w: pl.BlockSpec((tm, w), lambda i: (i, 0))
    fixed = lambda r, w: pl.BlockSpec((r, w), lambda i: (0, 0))
    return pl.pallas_call(
        body, name="post",
        grid=(t_len // tm,),
        in_specs=[row_blk(half), row_blk(half),
                  pl.BlockSpec((tm, half), lambda i: (i, zsb_blk)),
                  pl.BlockSpec((tm, half), lambda i: (i, zgd_blk)),
                  row_blk(d), row_blk(d), fixed(d, d), fixed(1, half), fixed(1, half), fixed(1, d)],
        out_specs=[row_blk(d), row_blk(half), row_blk(half),
                   pl.BlockSpec((2, tm, half), lambda i: (DPROJ_GATE_SLOT // 2, i, 0)),
                   fixed(1, LANES), fixed(1, d), fixed(1, half), fixed(1, half), fixed(d, d)],
        out_shape=[jax.ShapeDtypeStruct((t_len, d), F32)] + [jax.ShapeDtypeStruct((t_len, half), F32)] * 2
                  + [jax.ShapeDtypeStruct((len(DPROJ_PIECE_OF_SLOT), t_len, half), F32),
                     jax.ShapeDtypeStruct((1, LANES), F32), jax.ShapeDtypeStruct((1, d), F32),
                     jax.ShapeDtypeStruct((1, half), F32), jax.ShapeDtypeStruct((1, half), F32),
                     jax.ShapeDtypeStruct((d, d), F32)],
        compiler_params=_params("arbitrary"),
    )(o_sb, o_gd, proj, proj, x, target, w_out, sbw, gdw, fw)


def _piece_of_slot(s):
    return jnp.where(s < DPROJ_GDN_SLOT, s, jnp.where(s < DPROJ_GATE_SLOT, s + 1,
                                                     jnp.where(s == DPROJ_GATE_SLOT, 3, 7)))


def _gw_in_call(h_t, dproj8):
    d, t_len = h_t.shape
    n_piece, _, pw = dproj8.shape

    def body(ht_ref, dp_ref, gw_ref):
        gw_ref[...] = jnp.dot(ht_ref[...], dp_ref[0].astype(MXU_DTYPE), preferred_element_type=F32)

    return pl.pallas_call(
        body, name="gw_in",
        grid=(n_piece,),
        in_specs=[pl.BlockSpec((d, t_len), lambda s: (0, 0)),
                  pl.BlockSpec((1, t_len, pw), lambda s: (s, 0, 0))],
        out_specs=pl.BlockSpec((d, pw), lambda s: (0, _piece_of_slot(s))),
        out_shape=jax.ShapeDtypeStruct((d, n_piece * pw), F32),
        compiler_params=_params("arbitrary"),
    )(h_t, dproj8)


def _gw_small_call(h_t, dsmall, tm=512):
    d, t_len = h_t.shape
    ns = dsmall.shape[1]

    def body(ht_ref, dp_ref, gw_ref):
        @pl.when(pl.program_id(0) == 0)
        def _():
            gw_ref[...] = jnp.zeros_like(gw_ref)

        gw_ref[...] += jnp.dot(ht_ref[...], dp_ref[...].astype(MXU_DTYPE), preferred_element_type=F32)

    return pl.pallas_call(
        body, name="gw_small",
        grid=(t_len // tm,),
        in_specs=[pl.BlockSpec((d, tm), lambda t: (0, t)),
                  pl.BlockSpec((tm, ns), lambda t: (t, 0))],
        out_specs=pl.BlockSpec((d, ns), lambda t: (0, 0)),
        out_shape=jax.ShapeDtypeStruct((d, ns), F32),
        compiler_params=_params("arbitrary"),
    )(h_t, dsmall)


def _dx_call(dproj8, dsmall, w_main, w_small, x, r, dx2, norm_w, chip_scatter=(), tm=256):
    t_len, d = x.shape
    n_piece, _, pw = dproj8.shape
    ns = dsmall.shape[1]
    nx = len(chip_scatter)
    steps = t_len // tm

    def body(*refs):
        dp_ref, ds_ref, wm_ref, ws_ref, x_ref, r_ref, dx2_ref, nw_ref = refs[:8]
        gx_ref, gnw_ref = refs[8 + nx:10 + nx]
        copies = lambda: _chip_copies(refs[8:8 + nx], refs[10 + nx:10 + 2 * nx], *refs[10 + 2 * nx:])
        if nx:
            pl.when(pl.program_id(0) == 0)(lambda: _start_all(copies()))

        @pl.when(pl.program_id(0) == 0)
        def _():
            gnw_ref[...] = jnp.zeros_like(gnw_ref)

        dh = lax.dot_general(ds_ref[...].astype(MXU_DTYPE), ws_ref[...], _NT, preferred_element_type=F32)
        for s, p in enumerate(DPROJ_PIECE_OF_SLOT):
            dh = dh + lax.dot_general(dp_ref[s].astype(MXU_DTYPE), wm_ref[:, p * pw:(p + 1) * pw], _NT,
                                      preferred_element_type=F32)
        xv, rv = x_ref[...], r_ref[...]
        dn = dh * nw_ref[...]
        gx_ref[...] = dx2_ref[...] + rv * dn - xv * ((rv * rv * rv) * jnp.mean(dn * xv, axis=-1, keepdims=True))
        gnw_ref[...] += jnp.sum(dh * xv * rv, axis=0, keepdims=True)
        if nx:
            pl.when(pl.program_id(0) == steps - 1)(lambda: _wait_all(copies()))

    return pl.pallas_call(
        body, name="dx",
        grid=(steps,),
        in_specs=[pl.BlockSpec((n_piece, tm, pw), lambda i: (0, i, 0)),
                  pl.BlockSpec((tm, ns), lambda i: (i, 0)),
                  pl.BlockSpec((d, n_piece * pw), lambda i: (0, 0)),
                  pl.BlockSpec((d, ns), lambda i: (0, 0)),
                  pl.BlockSpec((tm, d), lambda i: (i, 0)),
                  pl.BlockSpec((tm, 1), lambda i: (i, 0)),
                  pl.BlockSpec((tm, d), lambda i: (i, 0)),
                  pl.BlockSpec((1, d), lambda i: (0, 0))] + [_HBM] * nx,
        out_specs=[pl.BlockSpec((tm, d), lambda i: (i, 0)),
                   pl.BlockSpec((1, d), lambda i: (0, 0))] + [_HBM] * nx,
        out_shape=[jax.ShapeDtypeStruct((t_len, d), F32), jax.ShapeDtypeStruct((1, d), F32)]
                  + [jax.ShapeDtypeStruct(a.shape, a.dtype) for a in chip_scatter],
        scratch_shapes=_chip_semaphores(nx) if nx else [],
        compiler_params=_params("arbitrary"),
    )(dproj8, dsmall, w_main, w_small, x, r, dx2, norm_w, *chip_scatter)


def _exchange_call(name, srcs, per_peer):
    n = len(srcs)

    def body(*refs):
        src_refs, out_refs = refs[:n], refs[n:2 * n]
        copies = _direct_copies(src_refs, out_refs, *refs[2 * n:], per_peer)
        _start_all(copies)
        _wait_all(copies)

    hbm = pl.BlockSpec(memory_space=pl.ANY)
    return pl.pallas_call(
        body, name=name,
        in_specs=[hbm] * n, out_specs=[hbm] * n, out_shape=_direct_out_shapes(srcs, per_peer),
        scratch_shapes=_direct_semaphores(n),
    )(*srcs)


def _direct_out_shapes(srcs, per_peer):
    return [jax.ShapeDtypeStruct(s.shape if pp else (N_DEV,) + s.shape, s.dtype) for s, pp in zip(srcs, per_peer)]


def _direct_semaphores(n):
    return [pltpu.SemaphoreType.DMA((n * (N_DEV - 1),)), pltpu.SemaphoreType.DMA((n * (N_DEV - 1),)),
            pltpu.SemaphoreType.DMA((n,))]


def _direct_copies(src_refs, out_refs, send_sems, recv_sems, local_sems, per_peer):
    x, y, c = lax.axis_index("x"), lax.axis_index("y"), lax.axis_index("c")
    me = 4 * x + 2 * y + c
    local, remote = [], []
    for a in range(len(src_refs)):
        mine = src_refs[a].at[me] if per_peer[a] else src_refs[a]
        local.append(pltpu.make_async_copy(mine, out_refs[a].at[me], local_sems.at[a]))
    for k in range(1, N_DEV):
        kx, ky, kc = (k >> 2) & 1, (k >> 1) & 1, k & 1
        px = 1 - x if kx else x
        py = 1 - y if ky else y
        pc = 1 - c if kc else c
        peer = 4 * px + 2 * py + pc
        for a in range(len(src_refs)):
            sem = a * (N_DEV - 1) + (k - 1)
            remote.append(pltpu.make_async_remote_copy(
                src_ref=src_refs[a].at[peer] if per_peer[a] else src_refs[a], dst_ref=out_refs[a].at[me],
                send_sem=send_sems.at[sem], recv_sem=recv_sems.at[sem],
                device_id=(px, py, pc), device_id_type=pl.DeviceIdType.MESH))
    return local, remote


def _start_all(copies):
    local, remote = copies
    for cp in local + remote:
        cp.start()


def _wait_all(copies):
    local, remote = copies
    for cp in remote:
        cp.wait_send()
    for cp in remote:
        cp.wait_recv()
    for cp in local:
        cp.wait()


N_CHIPS = 4
_HBM = pl.BlockSpec(memory_space=pl.ANY)
_MESH = pl.DeviceIdType.MESH


def _gather_call(name, srcs):
    n = len(srcs)
    per = N_DEV - 1

    def body(*refs):
        src_refs, out_refs = refs[:n], refs[n:2 * n]
        send_sems, recv_sems, local_sems = refs[2 * n:]
        x, y, c = lax.axis_index("x"), lax.axis_index("y"), lax.axis_index("c")
        me, sibling = (x, y, c), (x, y, 1 - c)
        x_nbr, y_nbr, diagonal = (1 - x, y), (x, 1 - y), (1 - x, 1 - y)
        held = ((1 - x) * c + x * (1 - c), y * c + (1 - y) * (1 - c))
        onward = (x * c + (1 - x) * (1 - c), (1 - y) * c + y * (1 - c))
        slot = lambda px, py, pc: 4 * px + 2 * py + pc

        def copy(a, k, block, to, from_src=False):
            rows = out_refs[a].at[slot(*block)]
            return pltpu.make_async_remote_copy(
                src_ref=src_refs[a] if from_src else rows, dst_ref=rows,
                send_sem=send_sems.at[a * per + k], recv_sem=recv_sems.at[a * per + k],
                device_id=to, device_id_type=_MESH)

        local = [pltpu.make_async_copy(src_refs[a], out_refs[a].at[slot(*me)], local_sems.at[a]) for a in range(n)]
        started = []

        def start(cp):
            cp.start()
            started.append(cp)

        for cp in local:
            cp.start()
        for a in range(n):
            start(copy(a, 0, me, sibling, True))
            start(copy(a, 1, me, (*x_nbr, c), True))
            start(copy(a, 2, me, (*y_nbr, c), True))
        for a in range(n):
            copy(a, 1, (*x_nbr, c), me).wait_recv()
            copy(a, 2, (*y_nbr, c), me).wait_recv()
            start(copy(a, 3, (*held, c), (*onward, c)))
            start(copy(a, 4, (*x_nbr, c), sibling))
            start(copy(a, 5, (*y_nbr, c), sibling))
        for a in range(n):
            copy(a, 3, (*diagonal, c), me).wait_recv()
            start(copy(a, 6, (*diagonal, c), sibling))
        for a in range(n):
            copy(a, 0, sibling, me).wait_recv()
            for k, chip in ((4, x_nbr), (5, y_nbr), (6, diagonal)):
                copy(a, k, (*chip, 1 - c), me).wait_recv()
        for cp in started:
            cp.wait_send()
        for cp in local:
            cp.wait()

    return pl.pallas_call(
        body, name=name,
        in_specs=[_HBM] * n, out_specs=[_HBM] * n,
        out_shape=[jax.ShapeDtypeStruct((N_DEV,) + s.shape, s.dtype) for s in srcs],
        scratch_shapes=[pltpu.SemaphoreType.DMA((n * per,)), pltpu.SemaphoreType.DMA((n * per,)),
                        pltpu.SemaphoreType.DMA((n,))],
    )(*srcs)


def _sibling_send_call(name, srcs):
    n = len(srcs)

    def body(*refs):
        src_refs, out_refs = refs[:n], refs[n:2 * n]
        send_sems, recv_sems = refs[2 * n:]
        x, y, c = lax.axis_index("x"), lax.axis_index("y"), lax.axis_index("c")
        copies = []
        for a in range(n):
            for ch in range(N_CHIPS):
                copies.append(pltpu.make_async_remote_copy(
                    src_ref=src_refs[a].at[2 * ch + (1 - c)], dst_ref=out_refs[a].at[ch],
                    send_sem=send_sems.at[a * N_CHIPS + ch], recv_sem=recv_sems.at[a * N_CHIPS + ch],
                    device_id=(x, y, 1 - c), device_id_type=_MESH))
        for cp in copies:
            cp.start()
        for cp in copies:
            cp.wait_send()
        for cp in copies:
            cp.wait_recv()

    return pl.pallas_call(
        body, name=name,
        in_specs=[_HBM] * n, out_specs=[_HBM] * n,
        out_shape=[jax.ShapeDtypeStruct((N_CHIPS,) + s.shape[1:], s.dtype) for s in srcs],
        scratch_shapes=[pltpu.SemaphoreType.DMA((n * N_CHIPS,)), pltpu.SemaphoreType.DMA((n * N_CHIPS,))],
    )(*srcs)


def _pair_sum_call(name, parts, from_sibling, tr):
    _, rows, cols = parts.shape

    def body(p_ref, s_ref, o_ref):
        o_ref[...] = (p_ref[...].astype(F32) + s_ref[...].astype(F32)).astype(o_ref.dtype)

    return pl.pallas_call(
        body, name=name,
        grid=(N_CHIPS, rows // tr),
        in_specs=[pl.BlockSpec((1, tr, cols), lambda ch, i: (2 * ch + lax.axis_index("c"), i, 0)),
                  pl.BlockSpec((1, tr, cols), lambda ch, i: (ch, i, 0))],
        out_specs=pl.BlockSpec((1, tr, cols), lambda ch, i: (ch, i, 0)),
        out_shape=jax.ShapeDtypeStruct((N_CHIPS, rows, cols), WIRE_DTYPE),
        compiler_params=_params("arbitrary", "arbitrary"),
    )(parts, from_sibling)


def _chip_exchange_call(name, srcs):
    n = len(srcs)

    def body(*refs):
        copies = _chip_copies(refs[:n], refs[n:2 * n], *refs[2 * n:])
        _start_all(copies)
        _wait_all(copies)

    return pl.pallas_call(
        body, name=name,
        in_specs=[_HBM] * n, out_specs=[_HBM] * n,
        out_shape=[jax.ShapeDtypeStruct(s.shape, s.dtype) for s in srcs],
        scratch_shapes=_chip_semaphores(n),
    )(*srcs)


def _chip_semaphores(n):
    per = N_CHIPS - 1
    return [pltpu.SemaphoreType.DMA((n * per,)), pltpu.SemaphoreType.DMA((n * per,)), pltpu.SemaphoreType.DMA((n,))]


def _chip_copies(src_refs, out_refs, send_sems, recv_sems, local_sems):
    per = N_CHIPS - 1
    x, y, c = lax.axis_index("x"), lax.axis_index("y"), lax.axis_index("c")
    mine = 2 * x + y
    chips = [(1 - x, y), (x, 1 - y), (1 - x, 1 - y)]
    n = len(src_refs)
    local = [pltpu.make_async_copy(src_refs[a].at[mine], out_refs[a].at[mine], local_sems.at[a]) for a in range(n)]
    remote = []
    for a in range(n):
        for j, (px, py) in enumerate(chips):
            remote.append(pltpu.make_async_remote_copy(
                src_ref=src_refs[a].at[2 * px + py], dst_ref=out_refs[a].at[mine],
                send_sem=send_sems.at[a * per + j], recv_sem=recv_sems.at[a * per + j],
                device_id=(px, py, c), device_id_type=_MESH))
    return local, remote


def _adam_call(name, parts, w, m, v, tr):
    rows, cols = w.shape
    n_slots = parts.shape[0]

    def body(p_ref, w_ref, m_ref, v_ref, g_ref, d_ref, nm_ref, nv_ref):
        g = p_ref[0].astype(F32)
        for s in range(1, n_slots):
            g = g + p_ref[s].astype(F32)
        m_new = ADAM_B1 * m_ref[...] + (1.0 - ADAM_B1) * g
        v_new = ADAM_B2 * v_ref[...] + (1.0 - ADAM_B2) * (g * g)
        m_hat = m_new / (1.0 - ADAM_B1 ** ADAM_STEP)
        v_hat = v_new / (1.0 - ADAM_B2 ** ADAM_STEP)
        g_ref[...] = g
        d_ref[...] = -ADAM_LR * (m_hat / (jnp.sqrt(v_hat) + ADAM_EPS) + ADAM_WD * w_ref[...])
        nm_ref[...] = m_new
        nv_ref[...] = v_new

    blk = pl.BlockSpec((tr, cols), lambda i: (i, 0))
    return pl.pallas_call(
        body, name=name,
        grid=(rows // tr,),
        in_specs=[pl.BlockSpec((n_slots, tr, cols), lambda i: (0, i, 0)), blk, blk, blk],
        out_specs=[blk] * 4,
        out_shape=[jax.ShapeDtypeStruct((rows, cols), F32)] * 4,
        compiler_params=_params("arbitrary"),
    )(parts, w, m, v)


N_PIECES = 8
PIECE = 512
SHARD_COLS = 513
SHARD_PAD = 640
RELAYOUT_ROWS = 256


def _from_shards_call(shards):
    _, d, _ = shards.shape
    tr = RELAYOUT_ROWS

    def body(p_ref, m_ref, s_ref):
        lane = lax.broadcasted_iota(jnp.int32, (tr, SHARD_PAD), 1)
        pad = jnp.zeros((tr, SHARD_PAD - SHARD_COLS), F32)
        sh = [jnp.concatenate([p_ref[s].astype(F32), pad], axis=1) for s in range(N_DEV)]
        for p in range(N_PIECES):
            y = sh[p] if p == 0 else pltpu.roll(sh[p], p, axis=1)
            if p > 0:
                y = jnp.where(lane < p, pltpu.roll(sh[p - 1], SHARD_PAD - (SHARD_COLS - p), axis=1), y)
            m_ref[:, p * PIECE:(p + 1) * PIECE] = y[:, :PIECE].astype(m_ref.dtype)
        first_gate = N_PIECES * PIECE - (N_DEV - 1) * SHARD_COLS
        s_ref[...] = pltpu.roll(sh[N_DEV - 1], SHARD_PAD - first_gate, axis=1)[:, :LANES].astype(s_ref.dtype)

    return pl.pallas_call(
        body, name="w_in_from_shards",
        grid=(d // tr,),
        in_specs=[pl.BlockSpec((N_DEV, tr, SHARD_COLS), lambda i: (0, i, 0))],
        out_specs=[pl.BlockSpec((tr, N_PIECES * PIECE), lambda i: (i, 0)), pl.BlockSpec((tr, LANES), lambda i: (i, 0))],
        out_shape=[jax.ShapeDtypeStruct((d, N_PIECES * PIECE), shards.dtype),
                   jax.ShapeDtypeStruct((d, LANES), shards.dtype)],
        compiler_params=_params("arbitrary"),
    )(shards)


def _to_shards_call(main, gates, out_dtype):
    d = main.shape[0]
    tr = RELAYOUT_ROWS

    def body(m_ref, s_ref, o_ref):
        for s in range(N_DEV):
            if s < N_DEV - 1:
                x = m_ref[:, s * PIECE:s * PIECE + SHARD_PAD]
            else:
                x = jnp.concatenate([m_ref[:, s * PIECE:(s + 1) * PIECE], s_ref[...]], axis=1)
            y = x if s == 0 else pltpu.roll(x, SHARD_PAD - s, axis=1)
            o_ref[s] = y[:, :SHARD_COLS].astype(out_dtype)

    return pl.pallas_call(
        body, name="w_in_to_shards",
        grid=(d // tr,),
        in_specs=[pl.BlockSpec((tr, N_PIECES * PIECE), lambda i: (i, 0)), pl.BlockSpec((tr, LANES), lambda i: (i, 0))],
        out_specs=pl.BlockSpec((N_DEV, tr, SHARD_COLS), lambda i: (0, i, 0)),
        out_shape=jax.ShapeDtypeStruct((N_DEV, d, SHARD_COLS), out_dtype),
        compiler_params=_params("arbitrary"),
    )(main, gates)


def _adamw(g, w, m, v):
    m_new = ADAM_B1 * m + (1.0 - ADAM_B1) * g
    v_new = ADAM_B2 * v + (1.0 - ADAM_B2) * (g * g)
    m_hat = m_new / (1.0 - ADAM_B1 ** ADAM_STEP)
    v_hat = v_new / (1.0 - ADAM_B2 ** ADAM_STEP)
    return -ADAM_LR * (m_hat / (jnp.sqrt(v_hat) + ADAM_EPS) + ADAM_WD * w), m_new, v_new


def _adam_small_call(parts, ws, ms, vs):
    n = len(ws)
    n_slots = parts.shape[0]

    def body(*refs):
        p_ref = refs[0]
        w_refs, m_refs, v_refs = refs[1:1 + n], refs[1 + n:1 + 2 * n], refs[1 + 2 * n:1 + 3 * n]
        loss_ref = refs[1 + 3 * n]
        outs = refs[2 + 3 * n:]
        g_all = p_ref[0]
        for s in range(1, n_slots):
            g_all = g_all + p_ref[s]
        loss_ref[...] = g_all[n:n + 1, 0:1]
        for r in range(n):
            size = w_refs[r].shape[1]
            g = g_all[r:r + 1, :size]
            delta, m_new, v_new = _adamw(g, w_refs[r][...], m_refs[r][...], v_refs[r][...])
            for kind, val in enumerate((g, delta, m_new, v_new)):
                outs[kind * n + r][...] = val

    vm = pl.BlockSpec(memory_space=pltpu.VMEM)
    shapes = [jax.ShapeDtypeStruct(w.shape, F32) for w in ws]
    return pl.pallas_call(
        body, name="adam_small",
        in_specs=[vm] * (1 + 3 * n), out_specs=[vm] * (1 + 4 * n),
        out_shape=[jax.ShapeDtypeStruct((1, 1), F32)] + shapes * 4,
    )(parts, *ws, *ms, *vs)


_SMALL_ROWS = ("norm1_w", "final_norm_w", "sb_norm_w", "gdn_norm_w", "gdn_A_log", "gdn_dt_bias", "loss")


def _pack_small(vals, width):
    rows = [jnp.pad(a.reshape(1, -1).astype(F32), ((0, 0), (0, width - a.size))) for a in vals]
    rows += [jnp.zeros((1, width), F32)] * (8 - len(rows))
    return jnp.concatenate(rows, axis=0)


def _device_step(x2d, tgt, w_main, w_small, w_out_full, conv_full, norm1_w, sb_norm_w, gdn_A_log, gdn_dt_bias,
                 gdn_norm_w, final_norm_w, distributed=False, w_in_chip_partials=None):
    t_len, d = x2d.shape
    n_chunks = t_len // CHUNK
    w_main, w_small, w_out_full = (a.astype(MXU_DTYPE) for a in (w_main, w_small, w_out_full))
    w_small_t = w_small[:, :2 * GDN_HEADS].T

    pad_lanes = lambda a, lo: jnp.pad(a.reshape(1, -1), ((0, 0), (lo, LANES - lo - a.size)))
    alog_l, dtb_l = pad_lanes(gdn_A_log, GDN_HEADS), pad_lanes(gdn_dt_bias, GDN_HEADS)
    alog_c, dtb_c = alog_l[:, :8].T, dtb_l[:, :8].T
    sbw = jnp.tile(sb_norm_w, (1, 512 // SB_HEAD_DIM))
    gdw = jnp.tile(gdn_norm_w, (1, 512 // GDN_HEAD_DIM))
    fw = final_norm_w.reshape(1, d)

    if distributed:
        proj, ps, pst, h_t, r1, w_out_g, conv_g = _inproj_call(
            x2d, norm1_w, w_main, w_small, w_small_t, gather=(w_out_full, conv_full))
        w_out_full = w_out_g.reshape(d, d)
        conv_full = conv_g.transpose(1, 0, 2).reshape(CONV_WIDTH, N_DEV * conv_g.shape[2])
    else:
        proj, ps, pst, h_t, r1 = _inproj_call(x2d, norm1_w, w_main, w_small, w_small_t)
    o_sb, sp_total, sb_blocks_run = _sb_fwd_call(proj, t_len)
    gact = _gdn_prep_call(proj, conv_full, t_len)
    beta_l, gcol_l, grow = _gdn_gates_call(ps, pst, alog_l, dtb_l, alog_c, dtb_c, t_len)
    gam_r = grow[GDN_HEADS:2 * GDN_HEADS].reshape(GDN_HEADS, n_chunks, 1, CHUNK)
    o_gd, *gdn_saved = _gdn_fwd_call(gact, beta_l, gcol_l, gam_r, t_len)

    (dx2, d_osb, d_ogd, dproj8, loss_p, g_fw, g_sbw, g_gdw, g_wout) = _post_call(
        o_sb, o_gd, proj, x2d, tgt, w_out_full, sbw, gdw, fw)

    dproj8 = _sb_bwd_call(proj, sp_total, sb_blocks_run, d_osb, dproj8, t_len)
    if distributed:
        d_gact3, d_gates, g_wout = _gdn_bwd_call(gact, beta_l, gcol_l, gam_r, gdn_saved, d_ogd, t_len,
                                                 scatter=(g_wout.reshape(N_DEV, d // N_DEV, d),))
    else:
        d_gact3, d_gates = _gdn_bwd_call(gact, beta_l, gcol_l, gam_r, gdn_saved, d_ogd, t_len)
    dproj8, g_conv = _gdn_prep_bwd_call(proj, conv_full, d_gact3, dproj8, t_len)
    dsmall, g_alog, g_dtb = _gdn_gates_bwd_call(ps, alog_l, dtb_l, d_gates, t_len)

    g_w_main = _gw_in_call(h_t, dproj8)
    g_w_small = _gw_small_call(h_t, dsmall)
    if distributed:
        grad_x, g_n1, g_w_in = _dx_call(dproj8, dsmall, w_main, w_small, x2d, r1, dx2, norm1_w,
                                        chip_scatter=(w_in_chip_partials(g_w_main, g_w_small),))
    else:
        grad_x, g_n1 = _dx_call(dproj8, dsmall, w_main, w_small, x2d, r1, dx2, norm1_w)
        g_w_in = (g_w_main, g_w_small)
    return (loss_p, grad_x, g_n1, g_w_in, g_sbw, g_conv, g_alog, g_dtb, g_gdw, g_wout, g_fw)


def kernel(x, norm1_w, w_in, sb_norm_w, gdn_conv_w, gdn_A_log, gdn_dt_bias, gdn_norm_w, w_out, final_norm_w, loss_target, m_norm1_w, m_w_in, m_sb_norm_w, m_gdn_conv_w, m_gdn_A_log, m_gdn_dt_bias, m_gdn_norm_w, m_w_out, m_final_norm_w, v_norm1_w, v_w_in, v_sb_norm_w, v_gdn_conv_w, v_gdn_A_log, v_gdn_dt_bias, v_gdn_norm_w, v_w_out, v_final_norm_w):
    d = x.shape[2]
    shard_cols = w_in.shape[2]
    conv_cols = gdn_conv_w.shape[2]

    (w_in_g,) = _gather_call("gather_weights", [w_in[0].astype(WIRE_DTYPE)])
    w_main, w_small = _from_shards_call(w_in_g)

    def w_in_chip_partials(g_w_main, g_w_small):
        parts = _to_shards_call(g_w_main, g_w_small, WIRE_DTYPE)
        (from_sibling,) = _sibling_send_call("grads_to_sibling", [parts])
        return _pair_sum_call("pair_sum_w_in", parts, from_sibling, 256)

    (loss_p, grad_x, g_n1, p_w_in, g_sbw, g_conv, g_alog, g_dtb, g_gdw, p_wout, g_fw) = _device_step(
        x[0], loss_target[0], w_main, w_small, w_out[0].astype(WIRE_DTYPE), gdn_conv_w[0], norm1_w, sb_norm_w,
        gdn_A_log, gdn_dt_bias, gdn_norm_w, final_norm_w, distributed=True, w_in_chip_partials=w_in_chip_partials)

    g_conv_parts = g_conv.reshape(CONV_WIDTH, N_DEV, conv_cols).transpose(1, 0, 2)
    fold = lambda a, group: a.reshape(-1, group).sum(axis=0)
    small_g = _pack_small([g_n1, g_fw, fold(g_sbw, SB_HEAD_DIM), fold(g_gdw, GDN_HEAD_DIM),
                           g_alog[0, GDN_HEADS:2 * GDN_HEADS], g_dtb[0, GDN_HEADS:2 * GDN_HEADS],
                           loss_p[0, :1]], d)
    p_small, p_conv = _exchange_call("exchange_small", [small_g, g_conv_parts], [False, True])

    r_w_in = _adam_call("adam_w_in", p_w_in, w_in[0], m_w_in[0], v_w_in[0], 256)
    r_wout = _adam_call("adam_w_out", p_wout, w_out[0], m_w_out[0], v_w_out[0], d // N_DEV)
    r_conv = _adam_call("adam_conv", p_conv, gdn_conv_w[0], m_gdn_conv_w[0], v_gdn_conv_w[0], CONV_WIDTH)

    row = lambda a: a.reshape(1, -1)
    n_small = len(_SMALL_ROWS) - 1
    r_small = _adam_small_call(
        p_small,
        [norm1_w, row(final_norm_w), sb_norm_w, gdn_norm_w, gdn_A_log, gdn_dt_bias],
        [m_norm1_w, row(m_final_norm_w), m_sb_norm_w, m_gdn_norm_w, m_gdn_A_log, m_gdn_dt_bias],
        [v_norm1_w, row(v_final_norm_w), v_sb_norm_w, v_gdn_norm_w, v_gdn_A_log, v_gdn_dt_bias])

    def small_out(kind, name):
        out = r_small[1 + kind * n_small + _SMALL_ROWS.index(name)]
        return out.reshape(final_norm_w.shape) if name == "final_norm_w" else out

    def outputs(kind):
        return (small_out(kind, "norm1_w"), r_w_in[kind][None], small_out(kind, "sb_norm_w"), r_conv[kind][None],
                small_out(kind, "gdn_A_log"), small_out(kind, "gdn_dt_bias"), small_out(kind, "gdn_norm_w"),
                r_wout[kind][None], small_out(kind, "final_norm_w"))

    return (r_small[0][0, 0], grad_x[None], *outputs(0), *outputs(1), *outputs(2), *outputs(3))
```

```python
import functools

import jax
import jax.numpy as jnp
from jax import lax
from jax.experimental import pallas as pl
from jax.experimental.pallas import tpu as pltpu

F32 = jnp.float32
MXU_DTYPE = jnp.bfloat16
WIRE_DTYPE = jnp.bfloat16
EXACT = lax.Precision.HIGHEST
EPS = 1e-6
N_DEV = 8
SB_HEAD_DIM = 64
GDN_HEAD_DIM = 128
GDN_HEADS = 4
GDN_CHUNKS_PER_STEP = 4
GDN_BWD_GROUP = 1
CHUNK = 64
CONV_WIDTH = 4
LANES = 128
SB_BLOCK = 128
SB_BQ = 256
VMEM_LIMIT_BYTES = 56 * 1024 * 1024

DPROJ_PIECE_OF_SLOT = (0, 1, 2, 4, 5, 6, 3, 7)
DPROJ_SB_SLOT, DPROJ_GDN_SLOT, DPROJ_GATE_SLOT = 0, 3, 6

ADAM_LR = 0.001
ADAM_B1 = 0.9
ADAM_B2 = 0.999
ADAM_EPS = 1e-08
ADAM_WD = 0.01
ADAM_STEP = 10

_NN = (((1,), (0,)), ((), ()))
_NT = (((1,), (1,)), ((), ()))
_TN = (((0,), (0,)), ((), ()))
_BNN = (((2,), (1,)), ((0,), (0,)))
_BNT = (((2,), (2,)), ((0,), (0,)))
_BTN = (((1,), (1,)), ((0,), (0,)))


def _mm(a, b):
    return jnp.dot(a.astype(MXU_DTYPE), b.astype(MXU_DTYPE), preferred_element_type=F32)


def _mm_nt(a, b):
    return lax.dot_general(a.astype(MXU_DTYPE), b.astype(MXU_DTYPE), _NT, preferred_element_type=F32)


def _mm_tn(a, b):
    return lax.dot_general(a.astype(MXU_DTYPE), b.astype(MXU_DTYPE), _TN, preferred_element_type=F32)


def _mx(a, b):
    return jnp.dot(a, b, precision=EXACT, preferred_element_type=F32)


def _mx_nt(a, b):
    return lax.dot_general(a, b, _NT, precision=EXACT, preferred_element_type=F32)


def _mx_tn(a, b):
    return lax.dot_general(a, b, _TN, precision=EXACT, preferred_element_type=F32)


def _split(x):
    hi = x.astype(MXU_DTYPE)
    return hi, (x - hi.astype(F32)).astype(MXU_DTYPE)


def _m3_general(a, b, dims):
    ah, al = _split(a)
    bh, bl = _split(b)
    dot = lambda x, y: lax.dot_general(x, y, dims, preferred_element_type=F32)
    (contract, _), (batch, _) = dims
    free = [ax for ax in range(a.ndim) if ax not in contract and ax not in batch][0]
    m = a.shape[free]
    both = dot(jnp.concatenate([ah, al], axis=free), bh)
    out_axis = len(batch)
    hi_part = lax.slice_in_dim(both, 0, m, axis=out_axis)
    lo_part = lax.slice_in_dim(both, m, 2 * m, axis=out_axis)
    return hi_part + (dot(ah, bl) + lo_part)


def _m3(a, b):
    return _m3_general(a, b, _NN)


def _m3_nt(a, b):
    return _m3_general(a, b, _NT)


def _m3_tn(a, b):
    return _m3_general(a, b, _TN)


def _sigmoid(z):
    return 1.0 / (1.0 + jnp.exp(-z))


def _softplus(z):
    return jnp.maximum(z, 0.0) + jnp.log(1.0 + jnp.exp(-jnp.abs(z)))


def _params(*semantics):
    return pltpu.CompilerParams(dimension_semantics=semantics, vmem_limit_bytes=VMEM_LIMIT_BYTES)


def _inproj_call(x, norm_w, w_main, w_small, w_small_t, gather=(), tm=256):
    t_len, d = x.shape
    n = w_main.shape[1]
    ns = w_small.shape[1]
    nst = w_small_t.shape[0]
    ng = len(gather)
    steps = t_len // tm

    def body(*refs):
        x_ref, nw_ref, wm_ref, ws_ref, wst_ref = refs[:5]
        pm_ref, ps_ref, pst_ref, ht_ref, r_ref = refs[5 + ng:10 + ng]
        copies = lambda: _direct_copies(refs[5:5 + ng], refs[10 + ng:10 + 2 * ng], *refs[10 + 2 * ng:], (False,) * ng)
        if ng:
            pl.when(pl.program_id(0) == 0)(lambda: _start_all(copies()))
        xv = x_ref[...]
        r = lax.rsqrt(jnp.mean(xv * xv, axis=-1, keepdims=True) + EPS)
        h = xv * r * nw_ref[...]
        hb = h.astype(MXU_DTYPE)
        for n0 in range(0, n, 512):
            pm_ref[:, n0:n0 + 512] = jnp.dot(hb, wm_ref[:, n0:n0 + 512], preferred_element_type=F32)
        ps_ref[...] = jnp.dot(hb, ws_ref[...], preferred_element_type=F32)
        pst_ref[...] = lax.dot_general(wst_ref[...], hb, _NT, preferred_element_type=F32)
        ht_ref[...] = h.T.astype(MXU_DTYPE)
        r_ref[...] = r
        if ng:
            pl.when(pl.program_id(0) == steps - 1)(lambda: _wait_all(copies()))

    return pl.pallas_call(
        body, name="inproj",
        grid=(steps,),
        in_specs=[pl.BlockSpec((tm, d), lambda i: (i, 0)),
                  pl.BlockSpec((1, d), lambda i: (0, 0)),
                  pl.BlockSpec((d, n), lambda i: (0, 0)),
                  pl.BlockSpec((d, ns), lambda i: (0, 0)),
                  pl.BlockSpec((nst, d), lambda i: (0, 0))] + [_HBM] * ng,
        out_specs=[pl.BlockSpec((tm, n), lambda i: (i, 0)),
                   pl.BlockSpec((tm, ns), lambda i: (i, 0)),
                   pl.BlockSpec((nst, tm), lambda i: (0, i)),
                   pl.BlockSpec((d, tm), lambda i: (0, i)),
                   pl.BlockSpec((tm, 1), lambda i: (i, 0))] + [_HBM] * ng,
        out_shape=[jax.ShapeDtypeStruct((t_len, n), F32),
                   jax.ShapeDtypeStruct((t_len, ns), F32),
                   jax.ShapeDtypeStruct((nst, t_len), F32),
                   jax.ShapeDtypeStruct((d, t_len), MXU_DTYPE),
                   jax.ShapeDtypeStruct((t_len, 1), F32)] + _direct_out_shapes(gather, (False,) * ng),
        scratch_shapes=_direct_semaphores(ng) if ng else [],
        compiler_params=_params("arbitrary"),
    )(x, norm_w, w_main, w_small, w_small_t, *gather)


def _running_sum_mm(x, tri):
    hi = x.astype(MXU_DTYPE)
    lo = (x - hi.astype(F32)).astype(MXU_DTYPE)
    return jnp.dot(hi, tri, preferred_element_type=F32) + jnp.dot(lo, tri, preferred_element_type=F32)


def _sb_iotas():
    row_i = lax.broadcasted_iota(jnp.int32, (SB_BQ, SB_BLOCK), 0)
    col_i = lax.broadcasted_iota(jnp.int32, (SB_BQ, SB_BLOCK), 1)
    sq_r = lax.broadcasted_iota(jnp.int32, (SB_BLOCK, SB_BLOCK), 0)
    sq_c = lax.broadcasted_iota(jnp.int32, (SB_BLOCK, SB_BLOCK), 1)
    return row_i, col_i, sq_r, sq_c


SB_DIAG_BLOCKS = SB_BQ // SB_BLOCK
SB_EXP_FLOOR = -110.0


def _sb_keys_descending(qi, tile, carry, z_bounds, n_heads, has_free):
    group = SB_DIAG_BLOCKS
    n_free = group * qi
    diag = list(range(group - 1, -1, -1))
    carry = tile([n_free + j for j in diag], [True] * group, carry, [j * SB_BLOCK for j in diag])

    def largest_exponent(c):
        worst = jnp.max(z_bounds[0] - c[1])
        for h in range(1, n_heads):
            worst = jnp.maximum(worst, jnp.max(z_bounds[h] - c[1 + h]))
        return worst

    always = group if has_free else 0

    def cond(state):
        return (state[0] < n_free) & ((state[1] > SB_EXP_FLOOR) | (state[0] < always))

    def body(state):
        first = n_free - 1 - state[0]
        c = tile([first - j for j in range(group)], [False] * group, state[2:])
        return (state[0] + group, largest_exponent(c), *c)

    out = lax.while_loop(cond, body, (jnp.int32(0), largest_exponent(carry), *carry))
    return out[2:], out[0]


def _sb_keys_ascending(qi, n_run, tile, carry, has_free):
    group = SB_DIAG_BLOCKS
    n_free = group * qi
    diag = list(range(group))
    kjs, los, masked = [n_free + j for j in diag], [j * SB_BLOCK for j in diag], [True] * group
    if has_free:
        early = lambda s: [n_free - n_run + group * s + j for j in range(group)]
        carry = lax.fori_loop(0, n_run // group - 1, lambda s, c: tile(early(s), [False] * group, c), carry)
        kjs, los, masked = [n_free - group + j for j in range(group)] + kjs, [0] * group + los, [False] * group + masked
    return tile(kjs, masked, carry, los)


def _sb_fwd_call(proj, t_len):
    nq = t_len // SB_BQ
    scale = float(SB_HEAD_DIM) ** -0.5
    n_pairs = 512 // LANES
    per_pair = LANES // SB_HEAD_DIM

    def body(q_ref, k_ref, v_ref, o_ref, st_ref, nrun_ref):
        lane = lax.broadcasted_iota(jnp.int32, (1, LANES), 1)
        row_i, col_i, sq_r, sq_c = _sb_iotas()
        ge = (sq_r >= sq_c).astype(MXU_DTYPE)
        hms = [((lane // SB_HEAD_DIM) == hh).astype(F32) for hh in range(per_pair)]
        k_sq = k_ref[...] * k_ref[...]
        k_norms = [jnp.sqrt(jnp.max(jnp.sum(k_sq * hm, axis=-1, keepdims=True))) * (1.02 * scale) for hm in hms]

        def q_block(qi, has_free):
            r0 = qi * SB_BQ if isinstance(qi, int) else pl.multiple_of(qi * SB_BQ, SB_BQ)
            rows = pl.ds(r0, SB_BQ)
            q_all = q_ref[rows, :]
            qms = [(q_all * (hm * scale)).astype(MXU_DTYPE) for hm in hms]
            z_bounds = [jnp.sqrt(jnp.sum(q_all * q_all * hm, axis=-1, keepdims=True)) * kn
                        for hm, kn in zip(hms, k_norms)]

            def tile(kjs, masked, kc, los=None):
                heads = range(per_pair)
                los = los or [0] * len(kjs)
                pairs = [(t, h) for t in range(len(kjs)) for h in heads]
                add_rows = lambda full, lo, part: full + part if lo == 0 else jnp.concatenate(
                    [full[:lo], full[lo:] + part], axis=0)
                acc, cs = kc[0], list(kc[1:])
                s0s = [kj * SB_BLOCK if isinstance(kj, int) else pl.multiple_of(kj * SB_BLOCK, SB_BLOCK) for kj in kjs]
                kbs = [k_ref[pl.ds(s0, SB_BLOCK), :].astype(MXU_DTYPE) for s0 in s0s]
                v_alls = [v_ref[pl.ds(s0, SB_BLOCK), :] for s0 in s0s]
                vms = {(t, h): (v_alls[t] * hms[h]).astype(MXU_DTYPE) for t, h in pairs}
                zs = {(t, h): lax.dot_general(qms[h][los[t]:], kbs[t], _NT, preferred_element_type=F32)
                      for t, h in pairs}
                masks = [(col_i[lo:] + s0) < (row_i[lo:] + r0) if m else None for m, lo, s0 in zip(masked, los, s0s)]
                keep = lambda t, a: a if masks[t] is None else jnp.where(masks[t], a, 0.0)
                sps = {(t, h): keep(t, _softplus(zs[t, h])) for t, h in pairs}
                sums = {p: _running_sum_mm(sps[p], ge) for p in pairs}
                mass = {}
                for t, h in pairs:
                    mass[t, h] = cs[h] if t == 0 else add_rows(
                        mass[t - 1, h], los[t - 1], jnp.sum(sps[t - 1, h], axis=-1, keepdims=True))
                ws = {(t, h): keep(t, jnp.exp(zs[t, h] - (sums[t, h] + mass[t, h][los[t]:]))) for t, h in pairs}
                for t, h in pairs:
                    acc = add_rows(acc, los[t], jnp.dot(ws[t, h].astype(MXU_DTYPE), vms[t, h],
                                                        preferred_element_type=F32))
                last = len(kjs) - 1
                cs = [add_rows(mass[last, h], los[last], jnp.sum(sps[last, h], axis=-1, keepdims=True)) for h in heads]
                return (acc, *cs)

            zero_col = jnp.zeros((SB_BQ, 1), F32)
            out, n_run = _sb_keys_descending(
                qi, tile, (jnp.zeros((SB_BQ, LANES), F32),) + (zero_col,) * per_pair, z_bounds, per_pair, has_free)
            o_ref[rows, :] = out[0]
            for hh in range(per_pair):
                st_ref[hh, rows, :] = out[1 + hh]
            nrun_ref[pl.program_id(0), qi] = n_run

        q_block(0, False)
        lax.fori_loop(1, nq, lambda qi, carry: (q_block(qi, True), carry)[1], 0)

    return pl.pallas_call(
        body, name="sb_fwd",
        grid=(n_pairs,),
        in_specs=[pl.BlockSpec((t_len, LANES), lambda p: (0, p)),
                  pl.BlockSpec((t_len, LANES), lambda p: (0, n_pairs + p)),
                  pl.BlockSpec((t_len, LANES), lambda p: (0, 2 * n_pairs + p))],
        out_specs=[pl.BlockSpec((t_len, LANES), lambda p: (0, p)),
                   pl.BlockSpec((per_pair, t_len, 1), lambda p: (p, 0, 0)),
                   pl.BlockSpec(memory_space=pltpu.SMEM)],
        out_shape=[jax.ShapeDtypeStruct((t_len, 512), F32),
                   jax.ShapeDtypeStruct((n_pairs * per_pair, t_len, 1), F32),
                   jax.ShapeDtypeStruct((n_pairs, nq), jnp.int32)],
        compiler_params=_params("arbitrary"),
    )(proj, proj, proj)


def _sb_bwd_call(proj, sp_total, n_run_all, d_o, dproj, t_len):
    nq = t_len // SB_BQ
    scale = float(SB_HEAD_DIM) ** -0.5
    n_pairs = 512 // LANES
    per_pair = LANES // SB_HEAD_DIM

    def body(q_ref, k_ref, v_ref, st_ref, nrun_ref, do_ref, dproj_in_ref, d_ref):
        lane = lax.broadcasted_iota(jnp.int32, (1, LANES), 1)
        row_i, col_i, sq_r, sq_c = _sb_iotas()
        lt = (sq_r < sq_c).astype(MXU_DTYPE)
        le = (sq_r <= sq_c).astype(MXU_DTYPE)
        hms = [((lane // SB_HEAD_DIM) == hh).astype(F32) for hh in range(per_pair)]
        d_ref[1] = jnp.zeros((t_len, LANES), F32)
        d_ref[2] = jnp.zeros((t_len, LANES), F32)

        def q_block(qi, has_free):
            r0 = qi * SB_BQ if isinstance(qi, int) else pl.multiple_of(qi * SB_BQ, SB_BQ)
            rows = pl.ds(r0, SB_BQ)
            q_all, do_all = q_ref[rows, :], do_ref[rows, :]
            qms = [(q_all * (hm * scale)).astype(MXU_DTYPE) for hm in hms]
            doms = [(do_all * hm).astype(MXU_DTYPE) for hm in hms]
            totals = [st_ref[hh, rows, :] for hh in range(per_pair)]

            def tile(kjs, masked, kc, los=None):
                heads = range(per_pair)
                los = los or [0] * len(kjs)
                pairs = [(t, h) for t in range(len(kjs)) for h in heads]
                add_rows = lambda full, lo, part: full + part if lo == 0 else jnp.concatenate(
                    [full[:lo], full[lo:] + part], axis=0)
                rsum = lambda a: jnp.sum(a, axis=-1, keepdims=True)
                dq, cls, gls = kc[0], list(kc[1:1 + per_pair]), list(kc[1 + per_pair:])
                s0s = [kj * SB_BLOCK if isinstance(kj, int) else pl.multiple_of(kj * SB_BLOCK, SB_BLOCK) for kj in kjs]
                k_alls = [k_ref[pl.ds(s0, SB_BLOCK), :] for s0 in s0s]
                v_alls = [v_ref[pl.ds(s0, SB_BLOCK), :] for s0 in s0s]
                kbs = [k_all.astype(MXU_DTYPE) for k_all in k_alls]
                vms = {(t, h): (v_alls[t] * hms[h]).astype(MXU_DTYPE) for t, h in pairs}
                kms = {(t, h): (k_alls[t] * (hms[h] * scale)).astype(MXU_DTYPE) for t, h in pairs}
                q_live = {(t, h): qms[h][los[t]:] for t, h in pairs}
                do_live = {(t, h): doms[h][los[t]:] for t, h in pairs}
                zs = {p: lax.dot_general(q_live[p], kbs[p[0]], _NT, preferred_element_type=F32) for p in pairs}
                das = {p: lax.dot_general(do_live[p], vms[p], _NT, preferred_element_type=F32) for p in pairs}
                masks = [(col_i[lo:] + s0) < (row_i[lo:] + r0) if m else None for m, lo, s0 in zip(masked, los, s0s)]
                keep = lambda t, a: a if masks[t] is None else jnp.where(masks[t], a, 0.0)
                sp_alls = {p: _softplus(zs[p]) for p in pairs}
                sps = {(t, h): keep(t, sp_alls[t, h]) for t, h in pairs}
                lefts = {p: _running_sum_mm(sps[p], lt) for p in pairs}
                cl = {}
                for t, h in pairs:
                    cl[t, h] = cls[h] if t == 0 else add_rows(cl[t - 1, h], los[t - 1], rsum(sps[t - 1, h]))
                ws = {(t, h): keep(t, jnp.exp(zs[t, h] - ((totals[h] - cl[t, h])[los[t]:] - lefts[t, h])))
                      for t, h in pairs}
                gs = {p: das[p] * ws[p] for p in pairs}
                g_sums = {p: _running_sum_mm(gs[p], le) for p in pairs}
                gl = {}
                for t, h in pairs:
                    gl[t, h] = gls[h] if t == 0 else add_rows(gl[t - 1, h], los[t - 1], rsum(gs[t - 1, h]))
                dzs = {(t, h): keep(t, gs[t, h] - jnp.exp(zs[t, h] - sp_alls[t, h]) * (gl[t, h][los[t]:] + g_sums[t, h])
                               ).astype(MXU_DTYPE) for t, h in pairs}
                for t in range(len(kjs)):
                    dk_t = jnp.zeros((SB_BLOCK, LANES), F32)
                    dv_t = jnp.zeros((SB_BLOCK, LANES), F32)
                    for h in heads:
                        dq = add_rows(dq, los[t], jnp.dot(dzs[t, h], kms[t, h], preferred_element_type=F32))
                        dk_t = dk_t + lax.dot_general(dzs[t, h], q_live[t, h], _TN, preferred_element_type=F32)
                        dv_t = dv_t + lax.dot_general(ws[t, h].astype(MXU_DTYPE), do_live[t, h], _TN,
                                                      preferred_element_type=F32)
                    d_ref[1, pl.ds(s0s[t], SB_BLOCK), :] += dk_t
                    d_ref[2, pl.ds(s0s[t], SB_BLOCK), :] += dv_t
                last = len(kjs) - 1
                cls = [add_rows(cl[last, h], los[last], rsum(sps[last, h])) for h in heads]
                gls = [add_rows(gl[last, h], los[last], rsum(gs[last, h])) for h in heads]
                return (dq, *cls, *gls)

            zero_col = jnp.zeros((SB_BQ, 1), F32)
            out = _sb_keys_ascending(qi, nrun_ref[pl.program_id(0), qi], tile,
                                     (jnp.zeros((SB_BQ, LANES), F32),) + (zero_col,) * (2 * per_pair), has_free)
            d_ref[0, rows, :] = out[0]

        q_block(0, False)
        lax.fori_loop(1, nq, lambda qi, carry: (q_block(qi, True), carry)[1], 0)

    col = lambda off: pl.BlockSpec((t_len, LANES), lambda p: (0, off + p))
    return pl.pallas_call(
        body, name="sb_bwd",
        grid=(n_pairs,),
        in_specs=[col(0), col(n_pairs), col(2 * n_pairs),
                  pl.BlockSpec((per_pair, t_len, 1), lambda p: (p, 0, 0)),
                  pl.BlockSpec(memory_space=pltpu.SMEM), col(0), _HBM],
        out_specs=pl.BlockSpec((3, t_len, LANES), lambda p: (DPROJ_SB_SLOT // 3, 0, p)),
        out_shape=jax.ShapeDtypeStruct(dproj.shape, dproj.dtype),
        input_output_aliases={6: 0},
        compiler_params=_params("arbitrary"),
    )(proj, proj, proj, sp_total, n_run_all, d_o, dproj)


def _conv_taps(xin, rows, t_len):
    taps = []
    for i in range(CONV_WIDTH):
        shift = CONV_WIDTH - 1 - i
        if shift == 0:
            taps.append(xin)
        else:
            taps.append(jnp.where(rows >= shift, pltpu.roll(xin, shift, axis=0), 0.0))
    return taps


def _gdn_prep_body_common(x_ref, w_ref, t_len):
    j = pl.program_id(0)
    xin = x_ref[...]
    rows = lax.broadcasted_iota(jnp.int32, (t_len, LANES), 0)
    taps = _conv_taps(xin, rows, t_len)
    pre = taps[0] * w_ref[0:1, :]
    for i in range(1, CONV_WIDTH):
        pre = pre + taps[i] * w_ref[i:i + 1, :]
    sg = _sigmoid(pre)
    act = pre * sg
    is_qk = j < 2 * GDN_HEADS
    nrm = jnp.where(is_qk, lax.rsqrt(jnp.sum(act * act, axis=-1, keepdims=True) + EPS), 1.0)
    sc = jnp.where(j < GDN_HEADS, float(GDN_HEAD_DIM) ** -0.5, 1.0)
    return j, rows, taps, pre, sg, act, is_qk, nrm, sc


def _gdn_prep_call(proj, conv_w, t_len):
    first = 2048 // LANES

    def body(x_ref, w_ref, out_ref):
        _, _, _, _, _, act, _, nrm, sc = _gdn_prep_body_common(x_ref, w_ref, t_len)
        out_ref[...] = act * nrm * sc

    return pl.pallas_call(
        body, name="gdn_prep",
        grid=(3 * GDN_HEADS,),
        in_specs=[pl.BlockSpec((t_len, LANES), lambda j: (0, first + j)),
                  pl.BlockSpec((CONV_WIDTH, LANES), lambda j: (0, j))],
        out_specs=pl.BlockSpec((t_len, LANES), lambda j: (0, j)),
        out_shape=jax.ShapeDtypeStruct((t_len, 3 * 512), F32),
        compiler_params=_params("arbitrary"),
    )(proj, conv_w)


def _gdn_prep_bwd_call(proj, conv_w, d_act3, dproj, t_len):
    first = 2048 // LANES

    def body(x_ref, w_ref, d_ref, dproj_in_ref, dx_ref, dw_ref):
        _, rows, taps, pre, sg, act, is_qk, nrm, sc = _gdn_prep_body_common(x_ref, w_ref, t_len)
        d_out = d_ref[0]
        dn = d_out * sc
        d_norm = nrm * dn - act * (nrm * nrm * nrm) * jnp.sum(dn * act, axis=-1, keepdims=True)
        d_act = jnp.where(is_qk, d_norm, d_out)
        d_pre = d_act * sg * (1.0 + pre * (1.0 - sg))
        dx = d_pre * w_ref[CONV_WIDTH - 1:CONV_WIDTH, :]
        dw_ref[CONV_WIDTH - 1:CONV_WIDTH, :] = jnp.sum(d_pre * taps[CONV_WIDTH - 1], axis=0, keepdims=True)
        for i in range(CONV_WIDTH - 1):
            shift = CONV_WIDTH - 1 - i
            up = jnp.where(rows < t_len - shift, pltpu.roll(d_pre, t_len - shift, axis=0), 0.0)
            dx = dx + up * w_ref[i:i + 1, :]
            dw_ref[i:i + 1, :] = jnp.sum(d_pre * taps[i], axis=0, keepdims=True)
        dx_ref[0] = dx

    return pl.pallas_call(
        body, name="gdn_prep_bwd",
        grid=(3 * GDN_HEADS,),
        in_specs=[pl.BlockSpec((t_len, LANES), lambda j: (0, first + j)),
                  pl.BlockSpec((CONV_WIDTH, LANES), lambda j: (0, j)),
                  pl.BlockSpec((1, t_len, LANES), lambda j: (j // GDN_HEADS, 0, j % GDN_HEADS)), _HBM],
        out_specs=[pl.BlockSpec((1, t_len, LANES), lambda j: (DPROJ_GDN_SLOT + j // GDN_HEADS, 0, j % GDN_HEADS)),
                   pl.BlockSpec((CONV_WIDTH, LANES), lambda j: (0, j))],
        out_shape=[jax.ShapeDtypeStruct(dproj.shape, dproj.dtype),
                   jax.ShapeDtypeStruct((CONV_WIDTH, 3 * 512), F32)],
        input_output_aliases={3: 0},
        compiler_params=_params("arbitrary"),
    )(proj, conv_w, d_act3, dproj)


def _chunk_cumsum_matrix():
    r = lax.broadcasted_iota(jnp.int32, (LANES, LANES), 0)
    c = lax.broadcasted_iota(jnp.int32, (LANES, LANES), 1)
    return ((r <= c) & ((r // CHUNK) == (c // CHUNK))).astype(F32)


def _gdn_gates_call(ps, pst, alog_l, dtb_l, alog_c, dtb_c, t_len):
    def body(ps_ref, pst_ref, al_ref, dl_ref, ac_ref, dc_ref, beta_ref, gcol_ref, grow_ref):
        upper = _chunk_cumsum_matrix()
        lower = upper.T
        psv = ps_ref[...]
        beta_ref[...] = _sigmoid(psv)
        g_l = -jnp.exp(al_ref[...]) * _softplus(psv + dl_ref[...])
        g_r = -jnp.exp(ac_ref[...]) * _softplus(pst_ref[...] + dc_ref[...])
        for w in range(t_len // LANES):
            sl = slice(w * LANES, (w + 1) * LANES)
            gcol_ref[sl, :] = _mx(lower, g_l[sl, :])
            grow_ref[:, sl] = _mx(g_r[:, sl], upper)

    vm = pl.BlockSpec(memory_space=pltpu.VMEM)
    return pl.pallas_call(
        body, name="gdn_gates",
        in_specs=[vm] * 6, out_specs=[vm] * 3,
        out_shape=[jax.ShapeDtypeStruct((t_len, LANES), F32),
                   jax.ShapeDtypeStruct((t_len, LANES), F32),
                   jax.ShapeDtypeStruct((8, t_len), F32)],
        compiler_params=pltpu.CompilerParams(vmem_limit_bytes=VMEM_LIMIT_BYTES),
    )(ps, pst, alog_l, dtb_l, alog_c, dtb_c)


def _gdn_gates_bwd_call(ps, alog_l, dtb_l, d_l, t_len):
    def body(ps_ref, al_ref, dl_ref, d_ref, dps_ref, gal_ref, gdt_ref):
        lane = lax.broadcasted_iota(jnp.int32, (1, LANES), 1)
        psv = ps_ref[...]
        dv = d_ref[...]
        beta = _sigmoid(psv)
        ea = jnp.exp(al_ref[...])
        arg = psv + dl_ref[...]
        g = -ea * _softplus(arg)
        d_a = dv * (-ea) * _sigmoid(arg)
        is_a = (lane >= GDN_HEADS) & (lane < 2 * GDN_HEADS)
        dps_ref[...] = jnp.where(lane < GDN_HEADS, dv * beta * (1.0 - beta), jnp.where(is_a, d_a, 0.0))
        gdt_ref[...] = jnp.where(is_a, jnp.sum(d_a, axis=0, keepdims=True), 0.0)
        gal_ref[...] = jnp.where(is_a, jnp.sum(dv * g, axis=0, keepdims=True), 0.0)

    vm = pl.BlockSpec(memory_space=pltpu.VMEM)
    return pl.pallas_call(
        body, name="gdn_gates_bwd",
        in_specs=[vm] * 4, out_specs=[vm] * 3,
        out_shape=[jax.ShapeDtypeStruct((t_len, LANES), F32),
                   jax.ShapeDtypeStruct((1, LANES), F32),
                   jax.ShapeDtypeStruct((1, LANES), F32)],
        compiler_params=pltpu.CompilerParams(vmem_limit_bytes=VMEM_LIMIT_BYTES),
    )(ps, alog_l, dtb_l, d_l)


def _bm(a, b):
    return _m3_general(a, b, _BNN)


def _bm_nt(a, b):
    return _m3_general(a, b, _BNT)


def _bm_tn(a, b):
    return _m3_general(a, b, _BTN)


def _heads_of(ref, rows):
    return jnp.stack([ref[rows, h * GDN_HEAD_DIM:(h + 1) * GDN_HEAD_DIM] for h in range(GDN_HEADS)])


def _chunk_terms(q_ref, k_ref, v_ref, b_ref, gc_ref, gr_ref, c, incl, strict, n=1, scores=True):
    r0 = c * CHUNK if isinstance(c, int) else pl.multiple_of(c * CHUNK, CHUNK)
    rows = pl.ds(r0, n * CHUNK)
    per_chunk = lambda x: x.reshape(GDN_HEADS * n, CHUNK, x.shape[-1])
    q, k, v = (per_chunk(_heads_of(ref, rows)) for ref in (q_ref, k_ref, v_ref))
    lane_ids = lax.broadcasted_iota(jnp.int32, (1, LANES), 1)
    pick = lambda slab, first: jnp.stack([jnp.sum(jnp.where(lane_ids == first + h, slab, 0.0), axis=-1, keepdims=True)
                                          for h in range(GDN_HEADS)])
    b = per_chunk(pick(b_ref[rows, :], 0))
    gc = per_chunk(pick(gc_ref[rows, :], GDN_HEADS))
    gr = gr_ref[:, c] if n == 1 else gr_ref[:, c:c + n].reshape(GDN_HEADS * n, 1, CHUNK)
    dm = jnp.where(incl, jnp.exp(jnp.where(incl, gc - gr, 0.0)), 0.0)
    kb = k * b
    vb = v * b
    e = jnp.exp(gc)
    a = p = None
    if scores:
        kk_qk = _bm_nt(jnp.concatenate([kb, q], axis=1), k)
        a = jnp.where(strict, kk_qk[:, :CHUNK] * dm, 0.0)
        p = jnp.where(incl, kk_qk[:, CHUNK:] * dm, 0.0)
    gl = gc[:, CHUNK - 1:CHUNK, :]
    eg = jnp.exp(gl - gc)
    return rows, q, k, v, b, gc, dm, kb, vb, e, a, p, gl, eg


def _unit_lower_inverse(a, eye):
    x = -a
    tm = eye + x
    xp = _bm(x, x)
    for _ in range(4):
        both = _bm(jnp.concatenate([xp, tm], axis=1), xp)
        tm = tm + both[:, CHUNK:]
        xp = both[:, :CHUNK]
    return tm + _bm(tm, xp)


def _gdn_specs(t_len, n_chunks, reverse):
    cps = GDN_CHUNKS_PER_STEP
    steps = n_chunks // cps
    at = (lambda g: steps - 1 - g) if reverse else (lambda g: g)
    rows_blk = lambda width, part=0: pl.BlockSpec((cps * CHUNK, width), lambda g: (at(g), part))
    gate_r = pl.BlockSpec((GDN_HEADS, cps, 1, CHUNK), lambda g: (0, at(g), 0, 0))
    per_chunk = lambda r, c: pl.BlockSpec((GDN_HEADS, cps, r, c), lambda g: (0, at(g), 0, 0))
    return cps, steps, rows_blk, gate_r, per_chunk


def _gdn_fwd_call(gact, beta_c, gam_c, gam_r, t_len):
    n_chunks = t_len // CHUNK
    dk = GDN_HEAD_DIM
    width = GDN_HEADS * dk
    cps, steps, rows_blk, gate_r, per_chunk = _gdn_specs(t_len, n_chunks, False)

    def body(q_ref, k_ref, v_ref, b_ref, gc_ref, gr_ref, o_ref, s_ref, t_ref, a_ref, p_ref, uw_ref, vn_ref, state_ref):
        row = lax.broadcasted_iota(jnp.int32, (CHUNK, CHUNK), 0)
        col = lax.broadcasted_iota(jnp.int32, (CHUNK, CHUNK), 1)
        incl, strict = row >= col, row > col
        eye = (row == col).astype(F32)

        @pl.when(pl.program_id(0) == 0)
        def _():
            state_ref[...] = jnp.zeros_like(state_ref)

        _, q, k, v, b, gc, dm, kb, vb, e, a, p, gl, eg = _chunk_terms(
            q_ref, k_ref, v_ref, b_ref, gc_ref, gr_ref, 0, incl, strict, cps)
        tm = _unit_lower_inverse(a, eye)
        uw = _bm(tm, jnp.concatenate([vb, kb * e], axis=2))
        w_qe = jnp.concatenate([uw[:, :, dk:], q * e], axis=1)
        u, kd, decay = uw[:, :, :dk], k * eg, jnp.exp(gl)
        per_chunk_block = lambda x: x.reshape(GDN_HEADS, cps, CHUNK, CHUNK)
        t_ref[...], a_ref[...], p_ref[...] = per_chunk_block(tm), per_chunk_block(a), per_chunk_block(p)
        uw_heads = uw.reshape(GDN_HEADS, cps * CHUNK, 2 * dk)
        for h in range(GDN_HEADS):
            uw_ref[:, h * 2 * dk:(h + 1) * 2 * dk] = uw_heads[h]

        of_chunk = lambda x, c: jnp.stack([x[h * cps + c] for h in range(GDN_HEADS)])
        s = state_ref[...]
        for c in range(cps):
            ws_qs = _bm(of_chunk(w_qe, c), s)
            vn = of_chunk(u, c) - ws_qs[:, :CHUNK]
            o = ws_qs[:, CHUNK:] + _bm(of_chunk(p, c), vn)
            for h in range(GDN_HEADS):
                o_ref[c * CHUNK:(c + 1) * CHUNK, h * dk:(h + 1) * dk] = o[h]
                vn_ref[c * CHUNK:(c + 1) * CHUNK, h * dk:(h + 1) * dk] = vn[h]
            s_ref[:, c] = s
            s = s * of_chunk(decay, c) + _bm_tn(of_chunk(kd, c), vn)
        state_ref[...] = s

    scores = jax.ShapeDtypeStruct((GDN_HEADS, n_chunks, CHUNK, CHUNK), F32)
    return pl.pallas_call(
        body, name="gdn_fwd",
        grid=(steps,),
        in_specs=[rows_blk(width, 0), rows_blk(width, 1), rows_blk(width, 2), rows_blk(LANES), rows_blk(LANES), gate_r],
        out_specs=[rows_blk(width), per_chunk(dk, dk), per_chunk(CHUNK, CHUNK), per_chunk(CHUNK, CHUNK),
                   per_chunk(CHUNK, CHUNK), rows_blk(2 * width), rows_blk(width)],
        out_shape=[jax.ShapeDtypeStruct((t_len, width), F32),
                   jax.ShapeDtypeStruct((GDN_HEADS, n_chunks, dk, dk), F32), scores, scores, scores,
                   jax.ShapeDtypeStruct((t_len, 2 * width), F32), jax.ShapeDtypeStruct((t_len, width), F32)],
        scratch_shapes=[pltpu.VMEM((GDN_HEADS, dk, dk), F32)],
        compiler_params=_params("arbitrary"),
    )(gact, gact, gact, beta_c, gam_c, gam_r)


def _gdn_bwd_call(gact, beta_c, gam_c, gam_r, saved, d_o, t_len, scatter=()):
    n_chunks = t_len // CHUNK
    dk = GDN_HEAD_DIM
    width = GDN_HEADS * dk
    cps, steps, rows_blk, gate_r, per_chunk = _gdn_specs(t_len, n_chunks, True)
    nx = len(scatter)
    n_in = 13

    def body(*refs):
        q_ref, k_ref, v_ref, b_ref, gc_ref, gr_ref = refs[:6]
        saved_refs, do_ref = refs[6:12], refs[12]
        d_ref, dgate_ref = refs[n_in + nx:n_in + 2 + nx]
        dstate_ref = refs[n_in + 2 + 2 * nx]
        copies = lambda: _direct_copies(refs[n_in:n_in + nx], refs[n_in + 2 + nx:n_in + 2 + 2 * nx],
                                        *refs[n_in + 3 + 2 * nx:], (True,) * nx)
        if nx:
            pl.when(pl.program_id(0) == 0)(lambda: _start_all(copies()))
        row = lax.broadcasted_iota(jnp.int32, (CHUNK, CHUNK), 0)
        col = lax.broadcasted_iota(jnp.int32, (CHUNK, CHUNK), 1)
        incl, strict = row >= col, row > col
        ng = GDN_BWD_GROUP
        nb = GDN_HEADS * ng
        upper = jnp.broadcast_to((row <= col).astype(F32), (nb, CHUNK, CHUNK))
        ones = jnp.ones((nb, CHUNK, LANES), F32)
        last_row = lax.broadcasted_iota(jnp.int32, (CHUNK, 1), 0) == CHUNK - 1
        lane_ids = lax.broadcasted_iota(jnp.int32, (1, LANES), 1)
        rsum = lambda m: jnp.sum(m, axis=-1, keepdims=True)
        total = lambda m: jnp.sum(rsum(m), axis=1, keepdims=True)
        of_chunk = lambda x, c: jnp.stack([x[h * ng + c] for h in range(GDN_HEADS)])

        @pl.when(pl.program_id(0) == 0)
        def _():
            dstate_ref[...] = jnp.zeros_like(dstate_ref)

        for c0 in range(cps - ng, -1, -ng):
            group(c0, q_ref, k_ref, v_ref, b_ref, gc_ref, gr_ref, saved_refs, do_ref, d_ref, dgate_ref, dstate_ref,
                  incl, strict, upper, ones, last_row, lane_ids, rsum, total, of_chunk)
        if nx:
            pl.when(pl.program_id(0) == steps - 1)(lambda: _wait_all(copies()))

    def group(c0, q_ref, k_ref, v_ref, b_ref, gc_ref, gr_ref, saved_refs, do_ref, d_ref, dgate_ref, dstate_ref,
              incl, strict, upper, ones, last_row, lane_ids, rsum, total, of_chunk):
        ng = GDN_BWD_GROUP
        nb = GDN_HEADS * ng
        rows = pl.ds(c0 * CHUNK, ng * CHUNK)
        s_ref, t_ref, a_ref, p_ref, uw_ref, vn_ref = saved_refs
        _, q, k, v, b, gc, dm, kb, vb, e, _, _, gl, eg = _chunk_terms(
            q_ref, k_ref, v_ref, b_ref, gc_ref, gr_ref, c0, incl, strict, ng, scores=False)
        s = s_ref[:, c0:c0 + ng].reshape(nb, dk, dk)
        tm = t_ref[:, c0:c0 + ng].reshape(nb, CHUNK, CHUNK)
        a = a_ref[:, c0:c0 + ng].reshape(nb, CHUNK, CHUNK)
        p = p_ref[:, c0:c0 + ng].reshape(nb, CHUNK, CHUNK)
        d_out = _heads_of(do_ref, rows).reshape(nb, CHUNK, dk)
        vn = _heads_of(vn_ref, rows).reshape(nb, CHUNK, dk)
        uw = jnp.stack([uw_ref[rows, h * 2 * dk:(h + 1) * 2 * dk] for h in range(GDN_HEADS)]).reshape(nb, CHUNK, 2 * dk)
        u, w = uw[:, :, :dk], uw[:, :, dk:]
        el = jnp.exp(gl)
        kbe = kb * e
        qe = q * e
        kd = k * eg
        pt_do = _bm_tn(p, d_out)
        qet_do = _bm_tn(qe, d_out)

        ds = dstate_ref[...]
        d_vn_c, ds_c = [None] * ng, [None] * ng
        for c in range(ng - 1, -1, -1):
            ds_c[c] = ds
            d_vn_c[c] = of_chunk(pt_do, c) + _bm(of_chunk(kd, c), ds)
            ds = of_chunk(el, c) * ds + of_chunk(qet_do, c) - _bm_tn(of_chunk(w, c), d_vn_c[c])
        dstate_ref[...] = ds
        by_chunk = lambda xs: jnp.stack([xs[c][h] for h in range(GDN_HEADS) for c in range(ng)])
        d_vn, ds = by_chunk(d_vn_c), by_chunk(ds_c)

        on_s = _bm_nt(jnp.concatenate([d_out, d_vn], axis=1), s)
        d_qe, d_w = on_s[:, :CHUNK], -on_s[:, CHUNK:]
        d_p = jnp.where(incl, _bm_nt(d_out, vn), 0.0)
        d_kd = _bm_nt(vn, ds)
        d_both = _bm_tn(tm, jnp.concatenate([d_vn, d_w], axis=2))
        d_vb, d_kbe = d_both[:, :, :dk], d_both[:, :, dk:]
        d_a = -jnp.where(strict, _bm_nt(d_both, uw), 0.0)
        m = d_a * dm
        n = d_p * dm
        on_k = _bm(jnp.concatenate([m, n], axis=1), k)
        d_kb = on_k[:, :CHUNK] + d_kbe * e
        d_q = on_k[:, CHUNK:] + d_qe * e
        d_k = (_bm_tn(jnp.concatenate([m, n], axis=1), jnp.concatenate([kb, q], axis=1))
               + d_kd * eg + b * d_kb)
        d_v = b * d_vb
        r = d_a * a + d_p * p
        kd_term = rsum(d_kd * kd)
        d_gl = total(ds * s) * el + jnp.sum(kd_term, axis=1, keepdims=True)
        d_gam = (rsum(r) - _bm_tn(r, ones)[:, :, 0:1] + rsum(d_qe * qe) + rsum(d_kbe * kbe) - kd_term
                 + jnp.where(last_row, d_gl, 0.0))
        d_beta = rsum(d_kb * k) + rsum(d_vb * v)
        d_g = _bm(upper, d_gam * ones)[:, :, 0:1]
        per_head = lambda x: x.reshape(GDN_HEADS, ng * CHUNK, x.shape[-1])
        d_q, d_k, d_v, d_beta, d_g = (per_head(x) for x in (d_q, d_k, d_v, d_beta, d_g))
        gates = jnp.zeros((ng * CHUNK, LANES), F32)
        for h in range(GDN_HEADS):
            lanes = slice(h * dk, (h + 1) * dk)
            d_ref[0, rows, lanes] = d_q[h]
            d_ref[1, rows, lanes] = d_k[h]
            d_ref[2, rows, lanes] = d_v[h]
            gates = gates + (jnp.where(lane_ids == h, d_beta[h], 0.0)
                             + jnp.where(lane_ids == GDN_HEADS + h, d_g[h], 0.0))
        dgate_ref[rows, :] = gates

    d_spec = pl.BlockSpec((3, cps * CHUNK, width), lambda g: (0, steps - 1 - g, 0))
    return pl.pallas_call(
        body, name="gdn_bwd",
        grid=(steps,),
        in_specs=[rows_blk(width, 0), rows_blk(width, 1), rows_blk(width, 2), rows_blk(LANES), rows_blk(LANES), gate_r,
                  per_chunk(dk, dk), per_chunk(CHUNK, CHUNK), per_chunk(CHUNK, CHUNK), per_chunk(CHUNK, CHUNK),
                  rows_blk(2 * width), rows_blk(width), rows_blk(width)] + [_HBM] * nx,
        out_specs=[d_spec, rows_blk(LANES)] + [_HBM] * nx,
        out_shape=[jax.ShapeDtypeStruct((3, t_len, width), F32),
                   jax.ShapeDtypeStruct((t_len, LANES), F32)] + _direct_out_shapes(scatter, (True,) * nx),
        scratch_shapes=[pltpu.VMEM((GDN_HEADS, dk, dk), F32)] + (_direct_semaphores(nx) if nx else []),
        compiler_params=_params("arbitrary"),
    )(gact, gact, gact, beta_c, gam_c, gam_r, *saved, d_o, *scatter)


def _group_matrix(width, group):
    r = lax.broadcasted_iota(jnp.int32, (width, width), 0)
    c = lax.broadcasted_iota(jnp.int32, (width, width), 1)
    return ((r // group) == (c // group)).astype(F32)


def _post_call(o_sb, o_gd, proj, x, target, w_out, sbw, gdw, fw, tm=256):
    t_len, d = x.shape
    half = 512
    zsb_blk = 1536 // half
    zgd_blk = 3584 // half

    def body(osb_ref, ogd_ref, zsb_ref, zgd_ref, x_ref, tg_ref, wo_ref, sbw_ref, gdw_ref, fw_ref,
             dx2_ref, dosb_ref, dogd_ref, dz_ref, loss_ref, gfw_ref, gsb_ref, ggd_ref, gwo_ref):
        step = pl.program_id(0)

        @pl.when(step == 0)
        def _():
            loss_ref[...] = jnp.zeros_like(loss_ref)
            gfw_ref[...] = jnp.zeros_like(gfw_ref)
            gsb_ref[...] = jnp.zeros_like(gsb_ref)
            ggd_ref[...] = jnp.zeros_like(ggd_ref)
            gwo_ref[...] = jnp.zeros_like(gwo_ref)

        def head_forward(o, z, w, gmat, inv):
            r = lax.rsqrt(_running_sum_mm(o * o, gmat) * inv + EPS)
            nrm = o * r * w
            sg = _sigmoid(z)
            return r, nrm, sg, nrm * (z * sg)

        def head_backward(d_m, o, z, w, gmat, inv, r, nrm, sg):
            d_n = d_m * (z * sg)
            d_z = d_m * nrm * (sg * (1.0 + z * (1.0 - sg)))
            dnw = d_n * w
            d_o = r * dnw - o * (r * r * r) * (_running_sum_mm(dnw * o, gmat) * inv)
            return d_o, d_z, jnp.sum(d_n * o * r, axis=0, keepdims=True)

        g_sb = _group_matrix(half, SB_HEAD_DIM).astype(MXU_DTYPE)
        g_gd = _group_matrix(half, GDN_HEAD_DIM).astype(MXU_DTYPE)
        osb, ogd, zsb, zgd = osb_ref[...], ogd_ref[...], zsb_ref[...], zgd_ref[...]
        sbw_v, gdw_v = sbw_ref[...], gdw_ref[...]
        r_sb, n_sb, sg_sb, m_sb = head_forward(osb, zsb, sbw_v, g_sb, 1.0 / SB_HEAD_DIM)
        r_gd, n_gd, sg_gd, m_gd = head_forward(ogd, zgd, gdw_v, g_gd, 1.0 / GDN_HEAD_DIM)
        mixed = jnp.concatenate([m_sb, m_gd], axis=1).astype(MXU_DTYPE)
        wo = wo_ref[...]
        x2 = x_ref[...] + jnp.dot(mixed, wo, preferred_element_type=F32)
        r2 = lax.rsqrt(jnp.mean(x2 * x2, axis=-1, keepdims=True) + EPS)
        fw_v = fw_ref[...]
        err = x2 * r2 * fw_v - tg_ref[...]
        loss_ref[...] += 0.5 * jnp.sum(jnp.sum(err * err, axis=-1, keepdims=True) * (1.0 / d))
        dy = err * (1.0 / d)
        gg = dy * fw_v
        dx2 = r2 * gg - x2 * ((r2 * r2 * r2) * jnp.mean(gg * x2, axis=-1, keepdims=True))
        gfw_ref[...] += jnp.sum(dy * x2 * r2, axis=0, keepdims=True)
        dx2_ref[...] = dx2
        dx2b = dx2.astype(MXU_DTYPE)
        d_mixed = lax.dot_general(dx2b, wo, _NT, preferred_element_type=F32)
        gwo_ref[...] += lax.dot_general(mixed, dx2b, _TN, preferred_element_type=F32)
        d_osb, d_zsb, gsb = head_backward(d_mixed[:, :half], osb, zsb, sbw_v, g_sb, 1.0 / SB_HEAD_DIM, r_sb, n_sb, sg_sb)
        d_ogd, d_zgd, ggd = head_backward(d_mixed[:, half:], ogd, zgd, gdw_v, g_gd, 1.0 / GDN_HEAD_DIM, r_gd, n_gd, sg_gd)
        dosb_ref[...] = d_osb
        dogd_ref[...] = d_ogd
        dz_ref[0] = d_zsb
        dz_ref[1] = d_zgd
        gsb_ref[...] += gsb
        ggd_ref[...] += ggd

    row_blk = lambda w: pl.BlockSpec((tm, w), lambda i: (i, 0))
    fixed = lambda r, w: pl.BlockSpec((r, w), lambda i: (0, 0))
    return pl.pallas_call(
        body, name="post",
        grid=(t_len // tm,),
        in_specs=[row_blk(half), row_blk(half),
                  pl.BlockSpec((tm, half), lambda i: (i, zsb_blk)),
                  pl.BlockSpec((tm, half), lambda i: (i, zgd_blk)),
                  row_blk(d), row_blk(d), fixed(d, d), fixed(1, half), fixed(1, half), fixed(1, d)],
        out_specs=[row_blk(d), row_blk(half), row_blk(half),
                   pl.BlockSpec((2, tm, half), lambda i: (DPROJ_GATE_SLOT // 2, i, 0)),
                   fixed(1, LANES), fixed(1, d), fixed(1, half), fixed(1, half), fixed(d, d)],
        out_shape=[jax.ShapeDtypeStruct((t_len, d), F32)] + [jax.ShapeDtypeStruct((t_len, half), F32)] * 2
                  + [jax.ShapeDtypeStruct((len(DPROJ_PIECE_OF_SLOT), t_len, half), F32),
                     jax.ShapeDtypeStruct((1, LANES), F32), jax.ShapeDtypeStruct((1, d), F32),
                     jax.ShapeDtypeStruct((1, half), F32), jax.ShapeDtypeStruct((1, half), F32),
                     jax.ShapeDtypeStruct((d, d), F32)],
        compiler_params=_params("arbitrary"),
    )(o_sb, o_gd, proj, proj, x, target, w_out, sbw, gdw, fw)


def _piece_of_slot(s):
    return jnp.where(s < DPROJ_GDN_SLOT, s, jnp.where(s < DPROJ_GATE_SLOT, s + 1,
                                                     jnp.where(s == DPROJ_GATE_SLOT, 3, 7)))


def _gw_in_call(h_t, dproj8):
    d, t_len = h_t.shape
    n_piece, _, pw = dproj8.shape

    def body(ht_ref, dp_ref, gw_ref):
        gw_ref[...] = jnp.dot(ht_ref[...], dp_ref[0].astype(MXU_DTYPE), preferred_element_type=F32)

    return pl.pallas_call(
        body, name="gw_in",
        grid=(n_piece,),
        in_specs=[pl.BlockSpec((d, t_len), lambda s: (0, 0)),
                  pl.BlockSpec((1, t_len, pw), lambda s: (s, 0, 0))],
        out_specs=pl.BlockSpec((d, pw), lambda s: (0, _piece_of_slot(s))),
        out_shape=jax.ShapeDtypeStruct((d, n_piece * pw), F32),
        compiler_params=_params("arbitrary"),
    )(h_t, dproj8)


def _slot_of_piece(p):
    return jnp.where(p < DPROJ_GDN_SLOT, p, jnp.where(p == 3, DPROJ_GATE_SLOT, jnp.where(p < 7, p - 1, 7)))


def _gw_in_shards_call(h_t, dproj8, dsmall, out_dtype):
    d, t_len = h_t.shape
    n_piece, _, pw = dproj8.shape
    ns = dsmall.shape[1]

    def body(ht_ref, dp_ref, ds_ref, o_ref, prev_ref, gates_ref):
        p = pl.program_id(0)

        @pl.when(p == 0)
        def _():
            gates_ref[...] = jnp.dot(ht_ref[...], ds_ref[...].astype(MXU_DTYPE), preferred_element_type=F32)

        def emit(s, tail):
            x = jnp.concatenate([prev_ref[...], tail], axis=1)
            y = x if s == 0 else pltpu.roll(x, SHARD_PAD - s, axis=1)
            o_ref[0] = y[:, :SHARD_COLS].astype(out_dtype)

        @pl.when(p < n_piece)
        def _():
            cur = jnp.dot(ht_ref[...], dp_ref[0].astype(MXU_DTYPE), preferred_element_type=F32)
            for s in range(n_piece - 1):
                pl.when(p == s + 1)(functools.partial(emit, s, cur[:, :SHARD_PAD - pw]))
            prev_ref[...] = cur

        @pl.when(p == n_piece)
        def _():
            emit(n_piece - 1, gates_ref[...])

    return pl.pallas_call(
        body, name="gw_in",
        grid=(n_piece + 1,),
        in_specs=[pl.BlockSpec((d, t_len), lambda p: (0, 0)),
                  pl.BlockSpec((1, t_len, pw), lambda p: (_slot_of_piece(jnp.minimum(p, n_piece - 1)), 0, 0)),
                  pl.BlockSpec((t_len, ns), lambda p: (0, 0))],
        out_specs=pl.BlockSpec((1, d, SHARD_COLS), lambda p: (jnp.maximum(p - 1, 0), 0, 0)),
        out_shape=jax.ShapeDtypeStruct((N_DEV, d, SHARD_COLS), out_dtype),
        scratch_shapes=[pltpu.VMEM((d, pw), F32), pltpu.VMEM((d, ns), F32)],
        compiler_params=_params("arbitrary"),
    )(h_t, dproj8, dsmall)


def _gw_small_call(h_t, dsmall, tm=512):
    d, t_len = h_t.shape
    ns = dsmall.shape[1]

    def body(ht_ref, dp_ref, gw_ref):
        @pl.when(pl.program_id(0) == 0)
        def _():
            gw_ref[...] = jnp.zeros_like(gw_ref)

        gw_ref[...] += jnp.dot(ht_ref[...], dp_ref[...].astype(MXU_DTYPE), preferred_element_type=F32)

    return pl.pallas_call(
        body, name="gw_small",
        grid=(t_len // tm,),
        in_specs=[pl.BlockSpec((d, tm), lambda t: (0, t)),
                  pl.BlockSpec((tm, ns), lambda t: (t, 0))],
        out_specs=pl.BlockSpec((d, ns), lambda t: (0, 0)),
        out_shape=jax.ShapeDtypeStruct((d, ns), F32),
        compiler_params=_params("arbitrary"),
    )(h_t, dsmall)


def _dx_call(dproj8, dsmall, w_main, w_small, x, r, dx2, norm_w, chip_scatter=(), tm=256):
    t_len, d = x.shape
    n_piece, _, pw = dproj8.shape
    ns = dsmall.shape[1]
    nx = len(chip_scatter)
    steps = t_len // tm

    def body(*refs):
        dp_ref, ds_ref, wm_ref, ws_ref, x_ref, r_ref, dx2_ref, nw_ref = refs[:8]
        gx_ref, gnw_ref = refs[8 + nx:10 + nx]
        copies = lambda: _chip_copies(refs[8:8 + nx], refs[10 + nx:10 + 2 * nx], *refs[10 + 2 * nx:])
        if nx:
            pl.when(pl.program_id(0) == 0)(lambda: _start_all(copies()))

        @pl.when(pl.program_id(0) == 0)
        def _():
            gnw_ref[...] = jnp.zeros_like(gnw_ref)

        dh = lax.dot_general(ds_ref[...].astype(MXU_DTYPE), ws_ref[...], _NT, preferred_element_type=F32)
        for s, p in enumerate(DPROJ_PIECE_OF_SLOT):
            dh = dh + lax.dot_general(dp_ref[s].astype(MXU_DTYPE), wm_ref[:, p * pw:(p + 1) * pw], _NT,
                                      preferred_element_type=F32)
        xv, rv = x_ref[...], r_ref[...]
        dn = dh * nw_ref[...]
        gx_ref[...] = dx2_ref[...] + rv * dn - xv * ((rv * rv * rv) * jnp.mean(dn * xv, axis=-1, keepdims=True))
        gnw_ref[...] += jnp.sum(dh * xv * rv, axis=0, keepdims=True)
        if nx:
            pl.when(pl.program_id(0) == steps - 1)(lambda: _wait_all(copies()))

    return pl.pallas_call(
        body, name="dx",
        grid=(steps,),
        in_specs=[pl.BlockSpec((n_piece, tm, pw), lambda i: (0, i, 0)),
                  pl.BlockSpec((tm, ns), lambda i: (i, 0)),
                  pl.BlockSpec((d, n_piece * pw), lambda i: (0, 0)),
                  pl.BlockSpec((d, ns), lambda i: (0, 0)),
                  pl.BlockSpec((tm, d), lambda i: (i, 0)),
                  pl.BlockSpec((tm, 1), lambda i: (i, 0)),
                  pl.BlockSpec((tm, d), lambda i: (i, 0)),
                  pl.BlockSpec((1, d), lambda i: (0, 0))] + [_HBM] * nx,
        out_specs=[pl.BlockSpec((tm, d), lambda i: (i, 0)),
                   pl.BlockSpec((1, d), lambda i: (0, 0))] + [_HBM] * nx,
        out_shape=[jax.ShapeDtypeStruct((t_len, d), F32), jax.ShapeDtypeStruct((1, d), F32)]
                  + [jax.ShapeDtypeStruct(a.shape, a.dtype) for a in chip_scatter],
        scratch_shapes=_chip_semaphores(nx) if nx else [],
        compiler_params=_params("arbitrary"),
    )(dproj8, dsmall, w_main, w_small, x, r, dx2, norm_w, *chip_scatter)


def _exchange_call(name, srcs, per_peer):
    n = len(srcs)

    def body(*refs):
        src_refs, out_refs = refs[:n], refs[n:2 * n]
        copies = _direct_copies(src_refs, out_refs, *refs[2 * n:], per_peer)
        _start_all(copies)
        _wait_all(copies)

    hbm = pl.BlockSpec(memory_space=pl.ANY)
    return pl.pallas_call(
        body, name=name,
        in_specs=[hbm] * n, out_specs=[hbm] * n, out_shape=_direct_out_shapes(srcs, per_peer),
        scratch_shapes=_direct_semaphores(n),
    )(*srcs)


def _direct_out_shapes(srcs, per_peer):
    return [jax.ShapeDtypeStruct(s.shape if pp else (N_DEV,) + s.shape, s.dtype) for s, pp in zip(srcs, per_peer)]


def _direct_semaphores(n):
    return [pltpu.SemaphoreType.DMA((n * (N_DEV - 1),)), pltpu.SemaphoreType.DMA((n * (N_DEV - 1),)),
            pltpu.SemaphoreType.DMA((n,))]


def _direct_copies(src_refs, out_refs, send_sems, recv_sems, local_sems, per_peer):
    x, y, c = lax.axis_index("x"), lax.axis_index("y"), lax.axis_index("c")
    me = 4 * x + 2 * y + c
    local, remote = [], []
    for a in range(len(src_refs)):
        mine = src_refs[a].at[me] if per_peer[a] else src_refs[a]
        local.append(pltpu.make_async_copy(mine, out_refs[a].at[me], local_sems.at[a]))
    for k in range(1, N_DEV):
        kx, ky, kc = (k >> 2) & 1, (k >> 1) & 1, k & 1
        px = 1 - x if kx else x
        py = 1 - y if ky else y
        pc = 1 - c if kc else c
        peer = 4 * px + 2 * py + pc
        for a in range(len(src_refs)):
            sem = a * (N_DEV - 1) + (k - 1)
            remote.append(pltpu.make_async_remote_copy(
                src_ref=src_refs[a].at[peer] if per_peer[a] else src_refs[a], dst_ref=out_refs[a].at[me],
                send_sem=send_sems.at[sem], recv_sem=recv_sems.at[sem],
                device_id=(px, py, pc), device_id_type=pl.DeviceIdType.MESH))
    return local, remote


def _start_all(copies):
    local, remote = copies
    for cp in local + remote:
        cp.start()


def _wait_all(copies):
    local, remote = copies
    for cp in remote:
        cp.wait_send()
    for cp in remote:
        cp.wait_recv()
    for cp in local:
        cp.wait()


N_CHIPS = 4
_HBM = pl.BlockSpec(memory_space=pl.ANY)
_MESH = pl.DeviceIdType.MESH


def _gather_call(name, srcs):
    n = len(srcs)
    per = N_DEV - 1

    def body(*refs):
        src_refs, out_refs = refs[:n], refs[n:2 * n]
        send_sems, recv_sems, local_sems = refs[2 * n:]
        x, y, c = lax.axis_index("x"), lax.axis_index("y"), lax.axis_index("c")
        me, sibling = (x, y, c), (x, y, 1 - c)
        x_nbr, y_nbr, diagonal = (1 - x, y), (x, 1 - y), (1 - x, 1 - y)
        held = ((1 - x) * c + x * (1 - c), y * c + (1 - y) * (1 - c))
        onward = (x * c + (1 - x) * (1 - c), (1 - y) * c + y * (1 - c))
        slot = lambda px, py, pc: 4 * px + 2 * py + pc

        def copy(a, k, block, to, from_src=False):
            rows = out_refs[a].at[slot(*block)]
            return pltpu.make_async_remote_copy(
                src_ref=src_refs[a] if from_src else rows, dst_ref=rows,
                send_sem=send_sems.at[a * per + k], recv_sem=recv_sems.at[a * per + k],
                device_id=to, device_id_type=_MESH)

        local = [pltpu.make_async_copy(src_refs[a], out_refs[a].at[slot(*me)], local_sems.at[a]) for a in range(n)]
        started = []

        def start(cp):
            cp.start()
            started.append(cp)

        for cp in local:
            cp.start()
        for a in range(n):
            start(copy(a, 0, me, sibling, True))
            start(copy(a, 1, me, (*x_nbr, c), True))
            start(copy(a, 2, me, (*y_nbr, c), True))
        for a in range(n):
            copy(a, 1, (*x_nbr, c), me).wait_recv()
            copy(a, 2, (*y_nbr, c), me).wait_recv()
            start(copy(a, 3, (*held, c), (*onward, c)))
            start(copy(a, 4, (*x_nbr, c), sibling))
            start(copy(a, 5, (*y_nbr, c), sibling))
        for a in range(n):
            copy(a, 3, (*diagonal, c), me).wait_recv()
            start(copy(a, 6, (*diagonal, c), sibling))
        for a in range(n):
            copy(a, 0, sibling, me).wait_recv()
            for k, chip in ((4, x_nbr), (5, y_nbr), (6, diagonal)):
                copy(a, k, (*chip, 1 - c), me).wait_recv()
        for cp in started:
            cp.wait_send()
        for cp in local:
            cp.wait()

    return pl.pallas_call(
        body, name=name,
        in_specs=[_HBM] * n, out_specs=[_HBM] * n,
        out_shape=[jax.ShapeDtypeStruct((N_DEV,) + s.shape, s.dtype) for s in srcs],
        scratch_shapes=[pltpu.SemaphoreType.DMA((n * per,)), pltpu.SemaphoreType.DMA((n * per,)),
                        pltpu.SemaphoreType.DMA((n,))],
    )(*srcs)


def _sibling_send_call(name, srcs):
    n = len(srcs)

    def body(*refs):
        src_refs, out_refs = refs[:n], refs[n:2 * n]
        send_sems, recv_sems = refs[2 * n:]
        x, y, c = lax.axis_index("x"), lax.axis_index("y"), lax.axis_index("c")
        copies = []
        for a in range(n):
            for ch in range(N_CHIPS):
                copies.append(pltpu.make_async_remote_copy(
                    src_ref=src_refs[a].at[2 * ch + (1 - c)], dst_ref=out_refs[a].at[ch],
                    send_sem=send_sems.at[a * N_CHIPS + ch], recv_sem=recv_sems.at[a * N_CHIPS + ch],
                    device_id=(x, y, 1 - c), device_id_type=_MESH))
        for cp in copies:
            cp.start()
        for cp in copies:
            cp.wait_send()
        for cp in copies:
            cp.wait_recv()

    return pl.pallas_call(
        body, name=name,
        in_specs=[_HBM] * n, out_specs=[_HBM] * n,
        out_shape=[jax.ShapeDtypeStruct((N_CHIPS,) + s.shape[1:], s.dtype) for s in srcs],
        scratch_shapes=[pltpu.SemaphoreType.DMA((n * N_CHIPS,)), pltpu.SemaphoreType.DMA((n * N_CHIPS,))],
    )(*srcs)


def _pair_sum_call(name, parts, from_sibling, tr):
    _, rows, cols = parts.shape

    def body(p_ref, s_ref, o_ref):
        o_ref[...] = (p_ref[...].astype(F32) + s_ref[...].astype(F32)).astype(o_ref.dtype)

    return pl.pallas_call(
        body, name=name,
        grid=(N_CHIPS, rows // tr),
        in_specs=[pl.BlockSpec((1, tr, cols), lambda ch, i: (2 * ch + lax.axis_index("c"), i, 0)),
                  pl.BlockSpec((1, tr, cols), lambda ch, i: (ch, i, 0))],
        out_specs=pl.BlockSpec((1, tr, cols), lambda ch, i: (ch, i, 0)),
        out_shape=jax.ShapeDtypeStruct((N_CHIPS, rows, cols), WIRE_DTYPE),
        compiler_params=_params("arbitrary", "arbitrary"),
    )(parts, from_sibling)


def _chip_exchange_call(name, srcs):
    n = len(srcs)

    def body(*refs):
        copies = _chip_copies(refs[:n], refs[n:2 * n], *refs[2 * n:])
        _start_all(copies)
        _wait_all(copies)

    return pl.pallas_call(
        body, name=name,
        in_specs=[_HBM] * n, out_specs=[_HBM] * n,
        out_shape=[jax.ShapeDtypeStruct(s.shape, s.dtype) for s in srcs],
        scratch_shapes=_chip_semaphores(n),
    )(*srcs)


def _chip_semaphores(n):
    per = N_CHIPS - 1
    return [pltpu.SemaphoreType.DMA((n * per,)), pltpu.SemaphoreType.DMA((n * per,)), pltpu.SemaphoreType.DMA((n,))]


def _chip_copies(src_refs, out_refs, send_sems, recv_sems, local_sems):
    per = N_CHIPS - 1
    x, y, c = lax.axis_index("x"), lax.axis_index("y"), lax.axis_index("c")
    mine = 2 * x + y
    chips = [(1 - x, y), (x, 1 - y), (1 - x, 1 - y)]
    n = len(src_refs)
    local = [pltpu.make_async_copy(src_refs[a].at[mine], out_refs[a].at[mine], local_sems.at[a]) for a in range(n)]
    remote = []
    for a in range(n):
        for j, (px, py) in enumerate(chips):
            remote.append(pltpu.make_async_remote_copy(
                src_ref=src_refs[a].at[2 * px + py], dst_ref=out_refs[a].at[mine],
                send_sem=send_sems.at[a * per + j], recv_sem=recv_sems.at[a * per + j],
                device_id=(px, py, c), device_id_type=_MESH))
    return local, remote


def _adam_call(name, parts, w, m, v, tr):
    rows, cols = w.shape
    n_slots = parts.shape[0]

    def body(p_ref, w_ref, m_ref, v_ref, g_ref, d_ref, nm_ref, nv_ref):
        g = p_ref[0].astype(F32)
        for s in range(1, n_slots):
            g = g + p_ref[s].astype(F32)
        m_new = ADAM_B1 * m_ref[...] + (1.0 - ADAM_B1) * g
        v_new = ADAM_B2 * v_ref[...] + (1.0 - ADAM_B2) * (g * g)
        m_hat = m_new / (1.0 - ADAM_B1 ** ADAM_STEP)
        v_hat = v_new / (1.0 - ADAM_B2 ** ADAM_STEP)
        g_ref[...] = g
        d_ref[...] = -ADAM_LR * (m_hat / (jnp.sqrt(v_hat) + ADAM_EPS) + ADAM_WD * w_ref[...])
        nm_ref[...] = m_new
        nv_ref[...] = v_new

    blk = pl.BlockSpec((tr, cols), lambda i: (i, 0))
    return pl.pallas_call(
        body, name=name,
        grid=(rows // tr,),
        in_specs=[pl.BlockSpec((n_slots, tr, cols), lambda i: (0, i, 0)), blk, blk, blk],
        out_specs=[blk] * 4,
        out_shape=[jax.ShapeDtypeStruct((rows, cols), F32)] * 4,
        compiler_params=_params("arbitrary"),
    )(parts, w, m, v)


N_PIECES = 8
PIECE = 512
SHARD_COLS = 513
SHARD_PAD = 640
RELAYOUT_ROWS = 256


def _from_shards_call(shards):
    _, d, _ = shards.shape
    tr = RELAYOUT_ROWS

    def body(p_ref, m_ref, s_ref):
        lane = lax.broadcasted_iota(jnp.int32, (tr, SHARD_PAD), 1)
        pad = jnp.zeros((tr, SHARD_PAD - SHARD_COLS), F32)
        sh = [jnp.concatenate([p_ref[s].astype(F32), pad], axis=1) for s in range(N_DEV)]
        for p in range(N_PIECES):
            y = sh[p] if p == 0 else pltpu.roll(sh[p], p, axis=1)
            if p > 0:
                y = jnp.where(lane < p, pltpu.roll(sh[p - 1], SHARD_PAD - (SHARD_COLS - p), axis=1), y)
            m_ref[:, p * PIECE:(p + 1) * PIECE] = y[:, :PIECE].astype(m_ref.dtype)
        first_gate = N_PIECES * PIECE - (N_DEV - 1) * SHARD_COLS
        s_ref[...] = pltpu.roll(sh[N_DEV - 1], SHARD_PAD - first_gate, axis=1)[:, :LANES].astype(s_ref.dtype)

    return pl.pallas_call(
        body, name="w_in_from_shards",
        grid=(d // tr,),
        in_specs=[pl.BlockSpec((N_DEV, tr, SHARD_COLS), lambda i: (0, i, 0))],
        out_specs=[pl.BlockSpec((tr, N_PIECES * PIECE), lambda i: (i, 0)), pl.BlockSpec((tr, LANES), lambda i: (i, 0))],
        out_shape=[jax.ShapeDtypeStruct((d, N_PIECES * PIECE), shards.dtype),
                   jax.ShapeDtypeStruct((d, LANES), shards.dtype)],
        compiler_params=_params("arbitrary"),
    )(shards)


def _to_shards_call(main, gates, out_dtype):
    d = main.shape[0]
    tr = RELAYOUT_ROWS

    def body(m_ref, s_ref, o_ref):
        for s in range(N_DEV):
            if s < N_DEV - 1:
                x = m_ref[:, s * PIECE:s * PIECE + SHARD_PAD]
            else:
                x = jnp.concatenate([m_ref[:, s * PIECE:(s + 1) * PIECE], s_ref[...]], axis=1)
            y = x if s == 0 else pltpu.roll(x, SHARD_PAD - s, axis=1)
            o_ref[s] = y[:, :SHARD_COLS].astype(out_dtype)

    return pl.pallas_call(
        body, name="w_in_to_shards",
        grid=(d // tr,),
        in_specs=[pl.BlockSpec((tr, N_PIECES * PIECE), lambda i: (i, 0)), pl.BlockSpec((tr, LANES), lambda i: (i, 0))],
        out_specs=pl.BlockSpec((N_DEV, tr, SHARD_COLS), lambda i: (0, i, 0)),
        out_shape=jax.ShapeDtypeStruct((N_DEV, d, SHARD_COLS), out_dtype),
        compiler_params=_params("arbitrary"),
    )(main, gates)


def _adamw(g, w, m, v):
    m_new = ADAM_B1 * m + (1.0 - ADAM_B1) * g
    v_new = ADAM_B2 * v + (1.0 - ADAM_B2) * (g * g)
    m_hat = m_new / (1.0 - ADAM_B1 ** ADAM_STEP)
    v_hat = v_new / (1.0 - ADAM_B2 ** ADAM_STEP)
    return -ADAM_LR * (m_hat / (jnp.sqrt(v_hat) + ADAM_EPS) + ADAM_WD * w), m_new, v_new


def _adam_small_call(parts, ws, ms, vs):
    n = len(ws)
    n_slots = parts.shape[0]

    def body(*refs):
        p_ref = refs[0]
        w_refs, m_refs, v_refs = refs[1:1 + n], refs[1 + n:1 + 2 * n], refs[1 + 2 * n:1 + 3 * n]
        loss_ref = refs[1 + 3 * n]
        outs = refs[2 + 3 * n:]
        g_all = p_ref[0]
        for s in range(1, n_slots):
            g_all = g_all + p_ref[s]
        loss_ref[...] = g_all[n:n + 1, 0:1]
        for r in range(n):
            size = w_refs[r].shape[1]
            g = g_all[r:r + 1, :size]
            delta, m_new, v_new = _adamw(g, w_refs[r][...], m_refs[r][...], v_refs[r][...])
            for kind, val in enumerate((g, delta, m_new, v_new)):
                outs[kind * n + r][...] = val

    vm = pl.BlockSpec(memory_space=pltpu.VMEM)
    shapes = [jax.ShapeDtypeStruct(w.shape, F32) for w in ws]
    return pl.pallas_call(
        body, name="adam_small",
        in_specs=[vm] * (1 + 3 * n), out_specs=[vm] * (1 + 4 * n),
        out_shape=[jax.ShapeDtypeStruct((1, 1), F32)] + shapes * 4,
    )(parts, *ws, *ms, *vs)


_SMALL_ROWS = ("norm1_w", "final_norm_w", "sb_norm_w", "gdn_norm_w", "gdn_A_log", "gdn_dt_bias", "loss")


def _pack_small(vals, width):
    rows = [jnp.pad(a.reshape(1, -1).astype(F32), ((0, 0), (0, width - a.size))) for a in vals]
    rows += [jnp.zeros((1, width), F32)] * (8 - len(rows))
    return jnp.concatenate(rows, axis=0)


def _device_step(x2d, tgt, w_main, w_small, w_out_full, conv_full, norm1_w, sb_norm_w, gdn_A_log, gdn_dt_bias,
                 gdn_norm_w, final_norm_w, distributed=False, w_in_chip_partials=None):
    t_len, d = x2d.shape
    n_chunks = t_len // CHUNK
    w_main, w_small, w_out_full = (a.astype(MXU_DTYPE) for a in (w_main, w_small, w_out_full))
    w_small_t = w_small[:, :2 * GDN_HEADS].T

    pad_lanes = lambda a, lo: jnp.pad(a.reshape(1, -1), ((0, 0), (lo, LANES - lo - a.size)))
    alog_l, dtb_l = pad_lanes(gdn_A_log, GDN_HEADS), pad_lanes(gdn_dt_bias, GDN_HEADS)
    alog_c, dtb_c = alog_l[:, :8].T, dtb_l[:, :8].T
    sbw = jnp.tile(sb_norm_w, (1, 512 // SB_HEAD_DIM))
    gdw = jnp.tile(gdn_norm_w, (1, 512 // GDN_HEAD_DIM))
    fw = final_norm_w.reshape(1, d)

    if distributed:
        proj, ps, pst, h_t, r1, w_out_g, conv_g = _inproj_call(
            x2d, norm1_w, w_main, w_small, w_small_t, gather=(w_out_full, conv_full))
        w_out_full = w_out_g.reshape(d, d)
        conv_full = conv_g.transpose(1, 0, 2).reshape(CONV_WIDTH, N_DEV * conv_g.shape[2])
    else:
        proj, ps, pst, h_t, r1 = _inproj_call(x2d, norm1_w, w_main, w_small, w_small_t)
    o_sb, sp_total, sb_blocks_run = _sb_fwd_call(proj, t_len)
    gact = _gdn_prep_call(proj, conv_full, t_len)
    beta_l, gcol_l, grow = _gdn_gates_call(ps, pst, alog_l, dtb_l, alog_c, dtb_c, t_len)
    gam_r = grow[GDN_HEADS:2 * GDN_HEADS].reshape(GDN_HEADS, n_chunks, 1, CHUNK)
    o_gd, *gdn_saved = _gdn_fwd_call(gact, beta_l, gcol_l, gam_r, t_len)

    (dx2, d_osb, d_ogd, dproj8, loss_p, g_fw, g_sbw, g_gdw, g_wout) = _post_call(
        o_sb, o_gd, proj, x2d, tgt, w_out_full, sbw, gdw, fw)

    dproj8 = _sb_bwd_call(proj, sp_total, sb_blocks_run, d_osb, dproj8, t_len)
    if distributed:
        d_gact3, d_gates, g_wout = _gdn_bwd_call(gact, beta_l, gcol_l, gam_r, gdn_saved, d_ogd, t_len,
                                                 scatter=(g_wout.reshape(N_DEV, d // N_DEV, d),))
    else:
        d_gact3, d_gates = _gdn_bwd_call(gact, beta_l, gcol_l, gam_r, gdn_saved, d_ogd, t_len)
    dproj8, g_conv = _gdn_prep_bwd_call(proj, conv_full, d_gact3, dproj8, t_len)
    dsmall, g_alog, g_dtb = _gdn_gates_bwd_call(ps, alog_l, dtb_l, d_gates, t_len)

    if distributed:
        shards = _gw_in_shards_call(h_t, dproj8, dsmall, WIRE_DTYPE)
        grad_x, g_n1, g_w_in = _dx_call(dproj8, dsmall, w_main, w_small, x2d, r1, dx2, norm1_w,
                                        chip_scatter=(w_in_chip_partials(shards),))
    else:
        grad_x, g_n1 = _dx_call(dproj8, dsmall, w_main, w_small, x2d, r1, dx2, norm1_w)
        g_w_in = (_gw_in_call(h_t, dproj8), _gw_small_call(h_t, dsmall))
    return (loss_p, grad_x, g_n1, g_w_in, g_sbw, g_conv, g_alog, g_dtb, g_gdw, g_wout, g_fw)


def kernel(x, norm1_w, w_in, sb_norm_w, gdn_conv_w, gdn_A_log, gdn_dt_bias, gdn_norm_w, w_out, final_norm_w, loss_target, m_norm1_w, m_w_in, m_sb_norm_w, m_gdn_conv_w, m_gdn_A_log, m_gdn_dt_bias, m_gdn_norm_w, m_w_out, m_final_norm_w, v_norm1_w, v_w_in, v_sb_norm_w, v_gdn_conv_w, v_gdn_A_log, v_gdn_dt_bias, v_gdn_norm_w, v_w_out, v_final_norm_w):
    d = x.shape[2]
    shard_cols = w_in.shape[2]
    conv_cols = gdn_conv_w.shape[2]

    (w_in_g,) = _gather_call("gather_weights", [w_in[0].astype(WIRE_DTYPE)])
    w_main, w_small = _from_shards_call(w_in_g)

    def w_in_chip_partials(parts):
        (from_sibling,) = _sibling_send_call("grads_to_sibling", [parts])
        return _pair_sum_call("pair_sum_w_in", parts, from_sibling, 256)

    (loss_p, grad_x, g_n1, p_w_in, g_sbw, g_conv, g_alog, g_dtb, g_gdw, p_wout, g_fw) = _device_step(
        x[0], loss_target[0], w_main, w_small, w_out[0].astype(WIRE_DTYPE), gdn_conv_w[0], norm1_w, sb_norm_w,
        gdn_A_log, gdn_dt_bias, gdn_norm_w, final_norm_w, distributed=True, w_in_chip_partials=w_in_chip_partials)

    g_conv_parts = g_conv.reshape(CONV_WIDTH, N_DEV, conv_cols).transpose(1, 0, 2)
    fold = lambda a, group: a.reshape(-1, group).sum(axis=0)
    small_g = _pack_small([g_n1, g_fw, fold(g_sbw, SB_HEAD_DIM), fold(g_gdw, GDN_HEAD_DIM),
                           g_alog[0, GDN_HEADS:2 * GDN_HEADS], g_dtb[0, GDN_HEADS:2 * GDN_HEADS],
                           loss_p[0, :1]], d)
    p_small, p_conv = _exchange_call("exchange_small", [small_g, g_conv_parts], [False, True])

    r_w_in = _adam_call("adam_w_in", p_w_in, w_in[0], m_w_in[0], v_w_in[0], 256)
    r_wout = _adam_call("adam_w_out", p_wout, w_out[0], m_w_out[0], v_w_out[0], d // N_DEV)
    r_conv = _adam_call("adam_conv", p_conv, gdn_conv_w[0], m_gdn_conv_w[0], v_gdn_conv_w[0], CONV_WIDTH)

    row = lambda a: a.reshape(1, -1)
    n_small = len(_SMALL_ROWS) - 1
    r_small = _adam_small_call(
        p_small,
        [norm1_w, row(final_norm_w), sb_norm_w, gdn_norm_w, gdn_A_log, gdn_dt_bias],
        [m_norm1_w, row(m_final_norm_w), m_sb_norm_w, m_gdn_norm_w, m_gdn_A_log, m_gdn_dt_bias],
        [v_norm1_w, row(v_final_norm_w), v_sb_norm_w, v_gdn_norm_w, v_gdn_A_log, v_gdn_dt_bias])

    def small_out(kind, name):
        out = r_small[1 + kind * n_small + _SMALL_ROWS.index(name)]
        return out.reshape(final_norm_w.shape) if name == "final_norm_w" else out

    def outputs(kind):
        return (small_out(kind, "norm1_w"), r_w_in[kind][None], small_out(kind, "sb_norm_w"), r_conv[kind][None],
                small_out(kind, "gdn_A_log"), small_out(kind, "gdn_dt_bias"), small_out(kind, "gdn_norm_w"),
                r_wout[kind][None], small_out(kind, "final_norm_w"))

    return (r_small[0][0, 0], grad_x[None], *outputs(0), *outputs(1), *outputs(2), *outputs(3))
```

```python
import functools

import jax
import jax.numpy as jnp
from jax import lax
from jax.experimental import pallas as pl
from jax.experimental.pallas import tpu as pltpu

F32 = jnp.float32
MXU_DTYPE = jnp.bfloat16
WIRE_DTYPE = jnp.bfloat16
EXACT = lax.Precision.HIGHEST
EPS = 1e-6
N_DEV = 8
SB_HEAD_DIM = 64
GDN_HEAD_DIM = 128
GDN_HEADS = 4
GDN_CHUNKS_PER_STEP = 4
GDN_BWD_GROUP = 1
CHUNK = 64
CONV_WIDTH = 4
LANES = 128
SB_BLOCK = 128
SB_BQ = 256
VMEM_LIMIT_BYTES = 56 * 1024 * 1024

DPROJ_PIECE_OF_SLOT = (0, 1, 2, 4, 5, 6, 3, 7)
DPROJ_SB_SLOT, DPROJ_GDN_SLOT, DPROJ_GATE_SLOT = 0, 3, 6

ADAM_LR = 0.001
ADAM_B1 = 0.9
ADAM_B2 = 0.999
ADAM_EPS = 1e-08
ADAM_WD = 0.01
ADAM_STEP = 10

_NN = (((1,), (0,)), ((), ()))
_NT = (((1,), (1,)), ((), ()))
_TN = (((0,), (0,)), ((), ()))
_BNN = (((2,), (1,)), ((0,), (0,)))
_BNT = (((2,), (2,)), ((0,), (0,)))
_BTN = (((1,), (1,)), ((0,), (0,)))


def _mm(a, b):
    return jnp.dot(a.astype(MXU_DTYPE), b.astype(MXU_DTYPE), preferred_element_type=F32)


def _mm_nt(a, b):
    return lax.dot_general(a.astype(MXU_DTYPE), b.astype(MXU_DTYPE), _NT, preferred_element_type=F32)


def _mm_tn(a, b):
    return lax.dot_general(a.astype(MXU_DTYPE), b.astype(MXU_DTYPE), _TN, preferred_element_type=F32)


def _mx(a, b):
    return jnp.dot(a, b, precision=EXACT, preferred_element_type=F32)


def _mx_nt(a, b):
    return lax.dot_general(a, b, _NT, precision=EXACT, preferred_element_type=F32)


def _mx_tn(a, b):
    return lax.dot_general(a, b, _TN, precision=EXACT, preferred_element_type=F32)


def _split(x):
    hi = x.astype(MXU_DTYPE)
    return hi, (x - hi.astype(F32)).astype(MXU_DTYPE)


def _m3_general(a, b, dims):
    ah, al = _split(a)
    bh, bl = _split(b)
    dot = lambda x, y: lax.dot_general(x, y, dims, preferred_element_type=F32)
    (contract, _), (batch, _) = dims
    free = [ax for ax in range(a.ndim) if ax not in contract and ax not in batch][0]
    m = a.shape[free]
    both = dot(jnp.concatenate([ah, al], axis=free), bh)
    out_axis = len(batch)
    hi_part = lax.slice_in_dim(both, 0, m, axis=out_axis)
    lo_part = lax.slice_in_dim(both, m, 2 * m, axis=out_axis)
    return hi_part + (dot(ah, bl) + lo_part)


def _m3(a, b):
    return _m3_general(a, b, _NN)


def _m3_nt(a, b):
    return _m3_general(a, b, _NT)


def _m3_tn(a, b):
    return _m3_general(a, b, _TN)


def _sigmoid(z):
    return 1.0 / (1.0 + jnp.exp(-z))


def _softplus(z):
    return jnp.maximum(z, 0.0) + jnp.log(1.0 + jnp.exp(-jnp.abs(z)))


def _params(*semantics):
    return pltpu.CompilerParams(dimension_semantics=semantics, vmem_limit_bytes=VMEM_LIMIT_BYTES)


def _inproj_call(x, norm_w, w_main, w_small, w_small_t, gather=(), tm=256):
    t_len, d = x.shape
    n = w_main.shape[1]
    ns = w_small.shape[1]
    nst = w_small_t.shape[0]
    ng = len(gather)
    steps = t_len // tm

    def body(*refs):
        x_ref, nw_ref, wm_ref, ws_ref, wst_ref = refs[:5]
        pm_ref, ps_ref, pst_ref, ht_ref, r_ref = refs[5 + ng:10 + ng]
        copies = lambda: _direct_copies(refs[5:5 + ng], refs[10 + ng:10 + 2 * ng], *refs[10 + 2 * ng:], (False,) * ng)
        if ng:
            pl.when(pl.program_id(0) == 0)(lambda: _start_all(copies()))
        xv = x_ref[...]
        r = lax.rsqrt(jnp.mean(xv * xv, axis=-1, keepdims=True) + EPS)
        h = xv * r * nw_ref[...]
        hb = h.astype(MXU_DTYPE)
        for n0 in range(0, n, 512):
            pm_ref[:, n0:n0 + 512] = jnp.dot(hb, wm_ref[:, n0:n0 + 512], preferred_element_type=F32)
        ps_ref[...] = jnp.dot(hb, ws_ref[...], preferred_element_type=F32)
        pst_ref[...] = lax.dot_general(wst_ref[...], hb, _NT, preferred_element_type=F32)
        ht_ref[...] = h.T.astype(MXU_DTYPE)
        r_ref[...] = r
        if ng:
            pl.when(pl.program_id(0) == steps - 1)(lambda: _wait_all(copies()))

    return pl.pallas_call(
        body, name="inproj",
        grid=(steps,),
        in_specs=[pl.BlockSpec((tm, d), lambda i: (i, 0)),
                  pl.BlockSpec((1, d), lambda i: (0, 0)),
                  pl.BlockSpec((d, n), lambda i: (0, 0)),
                  pl.BlockSpec((d, ns), lambda i: (0, 0)),
                  pl.BlockSpec((nst, d), lambda i: (0, 0))] + [_HBM] * ng,
        out_specs=[pl.BlockSpec((tm, n), lambda i: (i, 0)),
                   pl.BlockSpec((tm, ns), lambda i: (i, 0)),
                   pl.BlockSpec((nst, tm), lambda i: (0, i)),
                   pl.BlockSpec((d, tm), lambda i: (0, i)),
                   pl.BlockSpec((tm, 1), lambda i: (i, 0))] + [_HBM] * ng,
        out_shape=[jax.ShapeDtypeStruct((t_len, n), F32),
                   jax.ShapeDtypeStruct((t_len, ns), F32),
                   jax.ShapeDtypeStruct((nst, t_len), F32),
                   jax.ShapeDtypeStruct((d, t_len), MXU_DTYPE),
                   jax.ShapeDtypeStruct((t_len, 1), F32)] + _direct_out_shapes(gather, (False,) * ng),
        scratch_shapes=_direct_semaphores(ng) if ng else [],
        compiler_params=_params("arbitrary"),
    )(x, norm_w, w_main, w_small, w_small_t, *gather)


def _running_sum_mm(x, tri):
    hi = x.astype(MXU_DTYPE)
    lo = (x - hi.astype(F32)).astype(MXU_DTYPE)
    return jnp.dot(hi, tri, preferred_element_type=F32) + jnp.dot(lo, tri, preferred_element_type=F32)


def _sb_iotas():
    row_i = lax.broadcasted_iota(jnp.int32, (SB_BQ, SB_BLOCK), 0)
    col_i = lax.broadcasted_iota(jnp.int32, (SB_BQ, SB_BLOCK), 1)
    sq_r = lax.broadcasted_iota(jnp.int32, (SB_BLOCK, SB_BLOCK), 0)
    sq_c = lax.broadcasted_iota(jnp.int32, (SB_BLOCK, SB_BLOCK), 1)
    return row_i, col_i, sq_r, sq_c


SB_DIAG_BLOCKS = SB_BQ // SB_BLOCK
SB_EXP_FLOOR = -110.0


def _sb_keys_descending(qi, tile, carry, z_bounds, n_heads, has_free):
    group = SB_DIAG_BLOCKS
    n_free = group * qi
    diag = list(range(group - 1, -1, -1))
    carry = tile([n_free + j for j in diag], [True] * group, carry, [j * SB_BLOCK for j in diag])

    def largest_exponent(c):
        worst = jnp.max(z_bounds[0] - c[1])
        for h in range(1, n_heads):
            worst = jnp.maximum(worst, jnp.max(z_bounds[h] - c[1 + h]))
        return worst

    always = group if has_free else 0

    def cond(state):
        return (state[0] < n_free) & ((state[1] > SB_EXP_FLOOR) | (state[0] < always))

    def body(state):
        first = n_free - 1 - state[0]
        c = tile([first - j for j in range(group)], [False] * group, state[2:])
        return (state[0] + group, largest_exponent(c), *c)

    out = lax.while_loop(cond, body, (jnp.int32(0), largest_exponent(carry), *carry))
    return out[2:], out[0]


def _sb_keys_ascending(qi, n_run, tile, carry, has_free):
    group = SB_DIAG_BLOCKS
    n_free = group * qi
    diag = list(range(group))
    kjs, los, masked = [n_free + j for j in diag], [j * SB_BLOCK for j in diag], [True] * group
    if has_free:
        early = lambda s: [n_free - n_run + group * s + j for j in range(group)]
        carry = lax.fori_loop(0, n_run // group - 1, lambda s, c: tile(early(s), [False] * group, c), carry)
        kjs, los, masked = [n_free - group + j for j in range(group)] + kjs, [0] * group + los, [False] * group + masked
    return tile(kjs, masked, carry, los)


def _sb_fwd_call(proj, t_len):
    nq = t_len // SB_BQ
    scale = float(SB_HEAD_DIM) ** -0.5
    n_pairs = 512 // LANES
    per_pair = LANES // SB_HEAD_DIM

    def body(q_ref, k_ref, v_ref, o_ref, st_ref, nrun_ref):
        lane = lax.broadcasted_iota(jnp.int32, (1, LANES), 1)
        row_i, col_i, sq_r, sq_c = _sb_iotas()
        ge = (sq_r >= sq_c).astype(MXU_DTYPE)
        hms = [((lane // SB_HEAD_DIM) == hh).astype(F32) for hh in range(per_pair)]
        k_sq = k_ref[...] * k_ref[...]
        k_norms = [jnp.sqrt(jnp.max(jnp.sum(k_sq * hm, axis=-1, keepdims=True))) * (1.02 * scale) for hm in hms]

        def q_block(qi, has_free):
            r0 = qi * SB_BQ if isinstance(qi, int) else pl.multiple_of(qi * SB_BQ, SB_BQ)
            rows = pl.ds(r0, SB_BQ)
            q_all = q_ref[rows, :]
            qms = [(q_all * (hm * scale)).astype(MXU_DTYPE) for hm in hms]
            z_bounds = [jnp.sqrt(jnp.sum(q_all * q_all * hm, axis=-1, keepdims=True)) * kn
                        for hm, kn in zip(hms, k_norms)]

            def tile(kjs, masked, kc, los=None):
                heads = range(per_pair)
                los = los or [0] * len(kjs)
                pairs = [(t, h) for t in range(len(kjs)) for h in heads]
                add_rows = lambda full, lo, part: full + part if lo == 0 else jnp.concatenate(
                    [full[:lo], full[lo:] + part], axis=0)
                acc, cs = kc[0], list(kc[1:])
                s0s = [kj * SB_BLOCK if isinstance(kj, int) else pl.multiple_of(kj * SB_BLOCK, SB_BLOCK) for kj in kjs]
                kbs = [k_ref[pl.ds(s0, SB_BLOCK), :].astype(MXU_DTYPE) for s0 in s0s]
                v_alls = [v_ref[pl.ds(s0, SB_BLOCK), :] for s0 in s0s]
                vms = {(t, h): (v_alls[t] * hms[h]).astype(MXU_DTYPE) for t, h in pairs}
                zs = {(t, h): lax.dot_general(qms[h][los[t]:], kbs[t], _NT, preferred_element_type=F32)
                      for t, h in pairs}
                masks = [(col_i[lo:] + s0) < (row_i[lo:] + r0) if m else None for m, lo, s0 in zip(masked, los, s0s)]
                keep = lambda t, a: a if masks[t] is None else jnp.where(masks[t], a, 0.0)
                sps = {(t, h): keep(t, _softplus(zs[t, h])) for t, h in pairs}
                sums = {p: _running_sum_mm(sps[p], ge) for p in pairs}
                mass = {}
                for t, h in pairs:
                    mass[t, h] = cs[h] if t == 0 else add_rows(
                        mass[t - 1, h], los[t - 1], jnp.sum(sps[t - 1, h], axis=-1, keepdims=True))
                ws = {(t, h): keep(t, jnp.exp(zs[t, h] - (sums[t, h] + mass[t, h][los[t]:]))) for t, h in pairs}
                for t, h in pairs:
                    acc = add_rows(acc, los[t], jnp.dot(ws[t, h].astype(MXU_DTYPE), vms[t, h],
                                                        preferred_element_type=F32))
                last = len(kjs) - 1
                cs = [add_rows(mass[last, h], los[last], jnp.sum(sps[last, h], axis=-1, keepdims=True)) for h in heads]
                return (acc, *cs)

            zero_col = jnp.zeros((SB_BQ, 1), F32)
            out, n_run = _sb_keys_descending(
                qi, tile, (jnp.zeros((SB_BQ, LANES), F32),) + (zero_col,) * per_pair, z_bounds, per_pair, has_free)
            o_ref[rows, :] = out[0]
            for hh in range(per_pair):
                st_ref[hh, rows, :] = out[1 + hh]
            nrun_ref[pl.program_id(0), qi] = n_run

        q_block(0, False)
        lax.fori_loop(1, nq, lambda qi, carry: (q_block(qi, True), carry)[1], 0)

    return pl.pallas_call(
        body, name="sb_fwd",
        grid=(n_pairs,),
        in_specs=[pl.BlockSpec((t_len, LANES), lambda p: (0, p)),
                  pl.BlockSpec((t_len, LANES), lambda p: (0, n_pairs + p)),
                  pl.BlockSpec((t_len, LANES), lambda p: (0, 2 * n_pairs + p))],
        out_specs=[pl.BlockSpec((t_len, LANES), lambda p: (0, p)),
                   pl.BlockSpec((per_pair, t_len, 1), lambda p: (p, 0, 0)),
                   pl.BlockSpec(memory_space=pltpu.SMEM)],
        out_shape=[jax.ShapeDtypeStruct((t_len, 512), F32),
                   jax.ShapeDtypeStruct((n_pairs * per_pair, t_len, 1), F32),
                   jax.ShapeDtypeStruct((n_pairs, nq), jnp.int32)],
        compiler_params=_params("arbitrary"),
    )(proj, proj, proj)


def _sb_bwd_call(proj, sp_total, n_run_all, d_o, dproj, t_len):
    nq = t_len // SB_BQ
    scale = float(SB_HEAD_DIM) ** -0.5
    n_pairs = 512 // LANES
    per_pair = LANES // SB_HEAD_DIM

    def body(q_ref, k_ref, v_ref, st_ref, nrun_ref, do_ref, dproj_in_ref, d_ref):
        lane = lax.broadcasted_iota(jnp.int32, (1, LANES), 1)
        row_i, col_i, sq_r, sq_c = _sb_iotas()
        lt = (sq_r < sq_c).astype(MXU_DTYPE)
        le = (sq_r <= sq_c).astype(MXU_DTYPE)
        hms = [((lane // SB_HEAD_DIM) == hh).astype(F32) for hh in range(per_pair)]
        d_ref[1] = jnp.zeros((t_len, LANES), F32)
        d_ref[2] = jnp.zeros((t_len, LANES), F32)

        def q_block(qi, has_free):
            r0 = qi * SB_BQ if isinstance(qi, int) else pl.multiple_of(qi * SB_BQ, SB_BQ)
            rows = pl.ds(r0, SB_BQ)
            q_all, do_all = q_ref[rows, :], do_ref[rows, :]
            qms = [(q_all * (hm * scale)).astype(MXU_DTYPE) for hm in hms]
            doms = [(do_all * hm).astype(MXU_DTYPE) for hm in hms]
            totals = [st_ref[hh, rows, :] for hh in range(per_pair)]

            def tile(kjs, masked, kc, los=None):
                heads = range(per_pair)
                los = los or [0] * len(kjs)
                pairs = [(t, h) for t in range(len(kjs)) for h in heads]
                add_rows = lambda full, lo, part: full + part if lo == 0 else jnp.concatenate(
                    [full[:lo], full[lo:] + part], axis=0)
                rsum = lambda a: jnp.sum(a, axis=-1, keepdims=True)
                dq, cls, gls = kc[0], list(kc[1:1 + per_pair]), list(kc[1 + per_pair:])
                s0s = [kj * SB_BLOCK if isinstance(kj, int) else pl.multiple_of(kj * SB_BLOCK, SB_BLOCK) for kj in kjs]
                k_alls = [k_ref[pl.ds(s0, SB_BLOCK), :] for s0 in s0s]
                v_alls = [v_ref[pl.ds(s0, SB_BLOCK), :] for s0 in s0s]
                kbs = [k_all.astype(MXU_DTYPE) for k_all in k_alls]
                vms = {(t, h): (v_alls[t] * hms[h]).astype(MXU_DTYPE) for t, h in pairs}
                kms = {(t, h): (k_alls[t] * (hms[h] * scale)).astype(MXU_DTYPE) for t, h in pairs}
                q_live = {(t, h): qms[h][los[t]:] for t, h in pairs}
                do_live = {(t, h): doms[h][los[t]:] for t, h in pairs}
                zs = {p: lax.dot_general(q_live[p], kbs[p[0]], _NT, preferred_element_type=F32) for p in pairs}
                das = {p: lax.dot_general(do_live[p], vms[p], _NT, preferred_element_type=F32) for p in pairs}
                masks = [(col_i[lo:] + s0) < (row_i[lo:] + r0) if m else None for m, lo, s0 in zip(masked, los, s0s)]
                keep = lambda t, a: a if masks[t] is None else jnp.where(masks[t], a, 0.0)
                sp_alls = {p: _softplus(zs[p]) for p in pairs}
                sps = {(t, h): keep(t, sp_alls[t, h]) for t, h in pairs}
                lefts = {p: _running_sum_mm(sps[p], lt) for p in pairs}
                cl = {}
                for t, h in pairs:
                    cl[t, h] = cls[h] if t == 0 else add_rows(cl[t - 1, h], los[t - 1], rsum(sps[t - 1, h]))
                ws = {(t, h): keep(t, jnp.exp(zs[t, h] - ((totals[h] - cl[t, h])[los[t]:] - lefts[t, h])))
                      for t, h in pairs}
                gs = {p: das[p] * ws[p] for p in pairs}
                g_sums = {p: _running_sum_mm(gs[p], le) for p in pairs}
                gl = {}
                for t, h in pairs:
                    gl[t, h] = gls[h] if t == 0 else add_rows(gl[t - 1, h], los[t - 1], rsum(gs[t - 1, h]))
                dzs = {(t, h): keep(t, gs[t, h] - jnp.exp(zs[t, h] - sp_alls[t, h]) * (gl[t, h][los[t]:] + g_sums[t, h])
                               ).astype(MXU_DTYPE) for t, h in pairs}
                for t in range(len(kjs)):
                    dk_t = jnp.zeros((SB_BLOCK, LANES), F32)
                    dv_t = jnp.zeros((SB_BLOCK, LANES), F32)
                    for h in heads:
                        dq = add_rows(dq, los[t], jnp.dot(dzs[t, h], kms[t, h], preferred_element_type=F32))
                        dk_t = dk_t + lax.dot_general(dzs[t, h], q_live[t, h], _TN, preferred_element_type=F32)
                        dv_t = dv_t + lax.dot_general(ws[t, h].astype(MXU_DTYPE), do_live[t, h], _TN,
                                                      preferred_element_type=F32)
                    d_ref[1, pl.ds(s0s[t], SB_BLOCK), :] += dk_t
                    d_ref[2, pl.ds(s0s[t], SB_BLOCK), :] += dv_t
                last = len(kjs) - 1
                cls = [add_rows(cl[last, h], los[last], rsum(sps[last, h])) for h in heads]
                gls = [add_rows(gl[last, h], los[last], rsum(gs[last, h])) for h in heads]
                return (dq, *cls, *gls)

            zero_col = jnp.zeros((SB_BQ, 1), F32)
            out = _sb_keys_ascending(qi, nrun_ref[pl.program_id(0), qi], tile,
                                     (jnp.zeros((SB_BQ, LANES), F32),) + (zero_col,) * (2 * per_pair), has_free)
            d_ref[0, rows, :] = out[0]

        q_block(0, False)
        lax.fori_loop(1, nq, lambda qi, carry: (q_block(qi, True), carry)[1], 0)

    col = lambda off: pl.BlockSpec((t_len, LANES), lambda p: (0, off + p))
    return pl.pallas_call(
        body, name="sb_bwd",
        grid=(n_pairs,),
        in_specs=[col(0), col(n_pairs), col(2 * n_pairs),
                  pl.BlockSpec((per_pair, t_len, 1), lambda p: (p, 0, 0)),
                  pl.BlockSpec(memory_space=pltpu.SMEM), col(0), _HBM],
        out_specs=pl.BlockSpec((3, t_len, LANES), lambda p: (DPROJ_SB_SLOT // 3, 0, p)),
        out_shape=jax.ShapeDtypeStruct(dproj.shape, dproj.dtype),
        input_output_aliases={6: 0},
        compiler_params=_params("arbitrary"),
    )(proj, proj, proj, sp_total, n_run_all, d_o, dproj)


def _conv_taps(xin, rows, t_len):
    taps = []
    for i in range(CONV_WIDTH):
        shift = CONV_WIDTH - 1 - i
        if shift == 0:
            taps.append(xin)
        else:
            taps.append(jnp.where(rows >= shift, pltpu.roll(xin, shift, axis=0), 0.0))
    return taps


def _gdn_prep_body_common(x_ref, w_ref, t_len):
    j = pl.program_id(0)
    xin = x_ref[...]
    rows = lax.broadcasted_iota(jnp.int32, (t_len, LANES), 0)
    taps = _conv_taps(xin, rows, t_len)
    pre = taps[0] * w_ref[0:1, :]
    for i in range(1, CONV_WIDTH):
        pre = pre + taps[i] * w_ref[i:i + 1, :]
    sg = _sigmoid(pre)
    act = pre * sg
    is_qk = j < 2 * GDN_HEADS
    nrm = jnp.where(is_qk, lax.rsqrt(jnp.sum(act * act, axis=-1, keepdims=True) + EPS), 1.0)
    sc = jnp.where(j < GDN_HEADS, float(GDN_HEAD_DIM) ** -0.5, 1.0)
    return j, rows, taps, pre, sg, act, is_qk, nrm, sc


def _gdn_prep_call(proj, conv_w, t_len):
    first = 2048 // LANES

    def body(x_ref, w_ref, out_ref):
        _, _, _, _, _, act, _, nrm, sc = _gdn_prep_body_common(x_ref, w_ref, t_len)
        out_ref[...] = act * nrm * sc

    return pl.pallas_call(
        body, name="gdn_prep",
        grid=(3 * GDN_HEADS,),
        in_specs=[pl.BlockSpec((t_len, LANES), lambda j: (0, first + j)),
                  pl.BlockSpec((CONV_WIDTH, LANES), lambda j: (0, j))],
        out_specs=pl.BlockSpec((t_len, LANES), lambda j: (0, j)),
        out_shape=jax.ShapeDtypeStruct((t_len, 3 * 512), F32),
        compiler_params=_params("arbitrary"),
    )(proj, conv_w)


def _gdn_prep_bwd_call(proj, conv_w, d_act3, dproj, t_len):
    first = 2048 // LANES

    def body(x_ref, w_ref, d_ref, dproj_in_ref, dx_ref, dw_ref):
        _, rows, taps, pre, sg, act, is_qk, nrm, sc = _gdn_prep_body_common(x_ref, w_ref, t_len)
        d_out = d_ref[0]
        dn = d_out * sc
        d_norm = nrm * dn - act * (nrm * nrm * nrm) * jnp.sum(dn * act, axis=-1, keepdims=True)
        d_act = jnp.where(is_qk, d_norm, d_out)
        d_pre = d_act * sg * (1.0 + pre * (1.0 - sg))
        dx = d_pre * w_ref[CONV_WIDTH - 1:CONV_WIDTH, :]
        dw_ref[CONV_WIDTH - 1:CONV_WIDTH, :] = jnp.sum(d_pre * taps[CONV_WIDTH - 1], axis=0, keepdims=True)
        for i in range(CONV_WIDTH - 1):
            shift = CONV_WIDTH - 1 - i
            up = jnp.where(rows < t_len - shift, pltpu.roll(d_pre, t_len - shift, axis=0), 0.0)
            dx = dx + up * w_ref[i:i + 1, :]
            dw_ref[i:i + 1, :] = jnp.sum(d_pre * taps[i], axis=0, keepdims=True)
        dx_ref[0] = dx

    return pl.pallas_call(
        body, name="gdn_prep_bwd",
        grid=(3 * GDN_HEADS,),
        in_specs=[pl.BlockSpec((t_len, LANES), lambda j: (0, first + j)),
                  pl.BlockSpec((CONV_WIDTH, LANES), lambda j: (0, j)),
                  pl.BlockSpec((1, t_len, LANES), lambda j: (j // GDN_HEADS, 0, j % GDN_HEADS)), _HBM],
        out_specs=[pl.BlockSpec((1, t_len, LANES), lambda j: (DPROJ_GDN_SLOT + j // GDN_HEADS, 0, j % GDN_HEADS)),
                   pl.BlockSpec((CONV_WIDTH, LANES), lambda j: (0, j))],
        out_shape=[jax.ShapeDtypeStruct(dproj.shape, dproj.dtype),
                   jax.ShapeDtypeStruct((CONV_WIDTH, 3 * 512), F32)],
        input_output_aliases={3: 0},
        compiler_params=_params("arbitrary"),
    )(proj, conv_w, d_act3, dproj)


def _chunk_cumsum_matrix():
    r = lax.broadcasted_iota(jnp.int32, (LANES, LANES), 0)
    c = lax.broadcasted_iota(jnp.int32, (LANES, LANES), 1)
    return ((r <= c) & ((r // CHUNK) == (c // CHUNK))).astype(F32)


def _gdn_gates_call(ps, pst, alog_l, dtb_l, alog_c, dtb_c, t_len):
    def body(ps_ref, pst_ref, al_ref, dl_ref, ac_ref, dc_ref, beta_ref, gcol_ref, grow_ref):
        upper = _chunk_cumsum_matrix()
        lower = upper.T
        psv = ps_ref[...]
        beta_ref[...] = _sigmoid(psv)
        g_l = -jnp.exp(al_ref[...]) * _softplus(psv + dl_ref[...])
        g_r = -jnp.exp(ac_ref[...]) * _softplus(pst_ref[...] + dc_ref[...])
        for w in range(t_len // LANES):
            sl = slice(w * LANES, (w + 1) * LANES)
            gcol_ref[sl, :] = _mx(lower, g_l[sl, :])
            grow_ref[:, sl] = _mx(g_r[:, sl], upper)

    vm = pl.BlockSpec(memory_space=pltpu.VMEM)
    return pl.pallas_call(
        body, name="gdn_gates",
        in_specs=[vm] * 6, out_specs=[vm] * 3,
        out_shape=[jax.ShapeDtypeStruct((t_len, LANES), F32),
                   jax.ShapeDtypeStruct((t_len, LANES), F32),
                   jax.ShapeDtypeStruct((8, t_len), F32)],
        compiler_params=pltpu.CompilerParams(vmem_limit_bytes=VMEM_LIMIT_BYTES),
    )(ps, pst, alog_l, dtb_l, alog_c, dtb_c)


def _gdn_gates_bwd_call(ps, alog_l, dtb_l, d_l, t_len):
    def body(ps_ref, al_ref, dl_ref, d_ref, dps_ref, gal_ref, gdt_ref):
        lane = lax.broadcasted_iota(jnp.int32, (1, LANES), 1)
        psv = ps_ref[...]
        dv = d_ref[...]
        beta = _sigmoid(psv)
        ea = jnp.exp(al_ref[...])
        arg = psv + dl_ref[...]
        g = -ea * _softplus(arg)
        d_a = dv * (-ea) * _sigmoid(arg)
        is_a = (lane >= GDN_HEADS) & (lane < 2 * GDN_HEADS)
        dps_ref[...] = jnp.where(lane < GDN_HEADS, dv * beta * (1.0 - beta), jnp.where(is_a, d_a, 0.0))
        gdt_ref[...] = jnp.where(is_a, jnp.sum(d_a, axis=0, keepdims=True), 0.0)
        gal_ref[...] = jnp.where(is_a, jnp.sum(dv * g, axis=0, keepdims=True), 0.0)

    vm = pl.BlockSpec(memory_space=pltpu.VMEM)
    return pl.pallas_call(
        body, name="gdn_gates_bwd",
        in_specs=[vm] * 4, out_specs=[vm] * 3,
        out_shape=[jax.ShapeDtypeStruct((t_len, LANES), F32),
                   jax.ShapeDtypeStruct((1, LANES), F32),
                   jax.ShapeDtypeStruct((1, LANES), F32)],
        compiler_params=pltpu.CompilerParams(vmem_limit_bytes=VMEM_LIMIT_BYTES),
    )(ps, alog_l, dtb_l, d_l)


def _bm(a, b):
    return _m3_general(a, b, _BNN)


def _bm_nt(a, b):
    return _m3_general(a, b, _BNT)


def _bm_tn(a, b):
    return _m3_general(a, b, _BTN)


def _heads_of(ref, rows):
    return jnp.stack([ref[rows, h * GDN_HEAD_DIM:(h + 1) * GDN_HEAD_DIM] for h in range(GDN_HEADS)])


def _chunk_terms(q_ref, k_ref, v_ref, b_ref, gc_ref, gr_ref, c, incl, strict, n=1, scores=True):
    r0 = c * CHUNK if isinstance(c, int) else pl.multiple_of(c * CHUNK, CHUNK)
    rows = pl.ds(r0, n * CHUNK)
    per_chunk = lambda x: x.reshape(GDN_HEADS * n, CHUNK, x.shape[-1])
    q, k, v = (per_chunk(_heads_of(ref, rows)) for ref in (q_ref, k_ref, v_ref))
    lane_ids = lax.broadcasted_iota(jnp.int32, (1, LANES), 1)
    pick = lambda slab, first: jnp.stack([jnp.sum(jnp.where(lane_ids == first + h, slab, 0.0), axis=-1, keepdims=True)
                                          for h in range(GDN_HEADS)])
    b = per_chunk(pick(b_ref[rows, :], 0))
    gc = per_chunk(pick(gc_ref[rows, :], GDN_HEADS))
    gr = gr_ref[:, c] if n == 1 else gr_ref[:, c:c + n].reshape(GDN_HEADS * n, 1, CHUNK)
    dm = jnp.where(incl, jnp.exp(jnp.where(incl, gc - gr, 0.0)), 0.0)
    kb = k * b
    vb = v * b
    e = jnp.exp(gc)
    a = p = None
    if scores:
        kk_qk = _bm_nt(jnp.concatenate([kb, q], axis=1), k)
        a = jnp.where(strict, kk_qk[:, :CHUNK] * dm, 0.0)
        p = jnp.where(incl, kk_qk[:, CHUNK:] * dm, 0.0)
    gl = gc[:, CHUNK - 1:CHUNK, :]
    eg = jnp.exp(gl - gc)
    return rows, q, k, v, b, gc, dm, kb, vb, e, a, p, gl, eg


def _unit_lower_inverse(a, eye):
    x = -a
    tm = eye + x
    xp = _bm(x, x)
    for _ in range(4):
        both = _bm(jnp.concatenate([xp, tm], axis=1), xp)
        tm = tm + both[:, CHUNK:]
        xp = both[:, :CHUNK]
    return tm + _bm(tm, xp)


def _gdn_specs(t_len, n_chunks, reverse):
    cps = GDN_CHUNKS_PER_STEP
    steps = n_chunks // cps
    at = (lambda g: steps - 1 - g) if reverse else (lambda g: g)
    rows_blk = lambda width, part=0: pl.BlockSpec((cps * CHUNK, width), lambda g: (at(g), part))
    gate_r = pl.BlockSpec((GDN_HEADS, cps, 1, CHUNK), lambda g: (0, at(g), 0, 0))
    per_chunk = lambda r, c: pl.BlockSpec((GDN_HEADS, cps, r, c), lambda g: (0, at(g), 0, 0))
    return cps, steps, rows_blk, gate_r, per_chunk


def _gdn_fwd_call(gact, beta_c, gam_c, gam_r, t_len):
    n_chunks = t_len // CHUNK
    dk = GDN_HEAD_DIM
    width = GDN_HEADS * dk
    cps, steps, rows_blk, gate_r, per_chunk = _gdn_specs(t_len, n_chunks, False)

    def body(q_ref, k_ref, v_ref, b_ref, gc_ref, gr_ref, o_ref, s_ref, t_ref, a_ref, p_ref, uw_ref, vn_ref, state_ref):
        row = lax.broadcasted_iota(jnp.int32, (CHUNK, CHUNK), 0)
        col = lax.broadcasted_iota(jnp.int32, (CHUNK, CHUNK), 1)
        incl, strict = row >= col, row > col
        eye = (row == col).astype(F32)

        @pl.when(pl.program_id(0) == 0)
        def _():
            state_ref[...] = jnp.zeros_like(state_ref)

        _, q, k, v, b, gc, dm, kb, vb, e, a, p, gl, eg = _chunk_terms(
            q_ref, k_ref, v_ref, b_ref, gc_ref, gr_ref, 0, incl, strict, cps)
        tm = _unit_lower_inverse(a, eye)
        uw = _bm(tm, jnp.concatenate([vb, kb * e], axis=2))
        w_qe = jnp.concatenate([uw[:, :, dk:], q * e], axis=1)
        u, kd, decay = uw[:, :, :dk], k * eg, jnp.exp(gl)
        per_chunk_block = lambda x: x.reshape(GDN_HEADS, cps, CHUNK, CHUNK)
        t_ref[...], a_ref[...], p_ref[...] = per_chunk_block(tm), per_chunk_block(a), per_chunk_block(p)
        uw_heads = uw.reshape(GDN_HEADS, cps * CHUNK, 2 * dk)
        for h in range(GDN_HEADS):
            uw_ref[:, h * 2 * dk:(h + 1) * 2 * dk] = uw_heads[h]

        of_chunk = lambda x, c: jnp.stack([x[h * cps + c] for h in range(GDN_HEADS)])
        s = state_ref[...]
        for c in range(cps):
            ws_qs = _bm(of_chunk(w_qe, c), s)
            vn = of_chunk(u, c) - ws_qs[:, :CHUNK]
            o = ws_qs[:, CHUNK:] + _bm(of_chunk(p, c), vn)
            for h in range(GDN_HEADS):
                o_ref[c * CHUNK:(c + 1) * CHUNK, h * dk:(h + 1) * dk] = o[h]
                vn_ref[c * CHUNK:(c + 1) * CHUNK, h * dk:(h + 1) * dk] = vn[h]
            s_ref[:, c] = s
            s = s * of_chunk(decay, c) + _bm_tn(of_chunk(kd, c), vn)
        state_ref[...] = s

    scores = jax.ShapeDtypeStruct((GDN_HEADS, n_chunks, CHUNK, CHUNK), F32)
    return pl.pallas_call(
        body, name="gdn_fwd",
        grid=(steps,),
        in_specs=[rows_blk(width, 0), rows_blk(width, 1), rows_blk(width, 2), rows_blk(LANES), rows_blk(LANES), gate_r],
        out_specs=[rows_blk(width), per_chunk(dk, dk), per_chunk(CHUNK, CHUNK), per_chunk(CHUNK, CHUNK),
                   per_chunk(CHUNK, CHUNK), rows_blk(2 * width), rows_blk(width)],
        out_shape=[jax.ShapeDtypeStruct((t_len, width), F32),
                   jax.ShapeDtypeStruct((GDN_HEADS, n_chunks, dk, dk), F32), scores, scores, scores,
                   jax.ShapeDtypeStruct((t_len, 2 * width), F32), jax.ShapeDtypeStruct((t_len, width), F32)],
        scratch_shapes=[pltpu.VMEM((GDN_HEADS, dk, dk), F32)],
        compiler_params=_params("arbitrary"),
    )(gact, gact, gact, beta_c, gam_c, gam_r)


def _gdn_bwd_call(gact, beta_c, gam_c, gam_r, saved, d_o, t_len, scatter=()):
    n_chunks = t_len // CHUNK
    dk = GDN_HEAD_DIM
    width = GDN_HEADS * dk
    cps, steps, rows_blk, gate_r, per_chunk = _gdn_specs(t_len, n_chunks, True)
    nx = len(scatter)
    n_in = 13

    def body(*refs):
        q_ref, k_ref, v_ref, b_ref, gc_ref, gr_ref = refs[:6]
        saved_refs, do_ref = refs[6:12], refs[12]
        d_ref, dgate_ref = refs[n_in + nx:n_in + 2 + nx]
        dstate_ref = refs[n_in + 2 + 2 * nx]
        copies = lambda: _direct_copies(refs[n_in:n_in + nx], refs[n_in + 2 + nx:n_in + 2 + 2 * nx],
                                        *refs[n_in + 3 + 2 * nx:], (True,) * nx)
        if nx:
            pl.when(pl.program_id(0) == 0)(lambda: _start_all(copies()))
        row = lax.broadcasted_iota(jnp.int32, (CHUNK, CHUNK), 0)
        col = lax.broadcasted_iota(jnp.int32, (CHUNK, CHUNK), 1)
        incl, strict = row >= col, row > col
        ng = GDN_BWD_GROUP
        nb = GDN_HEADS * ng
        upper = jnp.broadcast_to((row <= col).astype(F32), (nb, CHUNK, CHUNK))
        ones = jnp.ones((nb, CHUNK, LANES), F32)
        last_row = lax.broadcasted_iota(jnp.int32, (CHUNK, 1), 0) == CHUNK - 1
        lane_ids = lax.broadcasted_iota(jnp.int32, (1, LANES), 1)
        rsum = lambda m: jnp.sum(m, axis=-1, keepdims=True)
        total = lambda m: jnp.sum(rsum(m), axis=1, keepdims=True)
        of_chunk = lambda x, c: jnp.stack([x[h * ng + c] for h in range(GDN_HEADS)])

        @pl.when(pl.program_id(0) == 0)
        def _():
            dstate_ref[...] = jnp.zeros_like(dstate_ref)

        for c0 in range(cps - ng, -1, -ng):
            group(c0, q_ref, k_ref, v_ref, b_ref, gc_ref, gr_ref, saved_refs, do_ref, d_ref, dgate_ref, dstate_ref,
                  incl, strict, upper, ones, last_row, lane_ids, rsum, total, of_chunk)
        if nx:
            pl.when(pl.program_id(0) == steps - 1)(lambda: _wait_all(copies()))

    def group(c0, q_ref, k_ref, v_ref, b_ref, gc_ref, gr_ref, saved_refs, do_ref, d_ref, dgate_ref, dstate_ref,
              incl, strict, upper, ones, last_row, lane_ids, rsum, total, of_chunk):
        ng = GDN_BWD_GROUP
        nb = GDN_HEADS * ng
        rows = pl.ds(c0 * CHUNK, ng * CHUNK)
        s_ref, t_ref, a_ref, p_ref, uw_ref, vn_ref = saved_refs
        _, q, k, v, b, gc, dm, kb, vb, e, _, _, gl, eg = _chunk_terms(
            q_ref, k_ref, v_ref, b_ref, gc_ref, gr_ref, c0, incl, strict, ng, scores=False)
        s = s_ref[:, c0:c0 + ng].reshape(nb, dk, dk)
        tm = t_ref[:, c0:c0 + ng].reshape(nb, CHUNK, CHUNK)
        a = a_ref[:, c0:c0 + ng].reshape(nb, CHUNK, CHUNK)
        p = p_ref[:, c0:c0 + ng].reshape(nb, CHUNK, CHUNK)
        d_out = _heads_of(do_ref, rows).reshape(nb, CHUNK, dk)
        vn = _heads_of(vn_ref, rows).reshape(nb, CHUNK, dk)
        uw = jnp.stack([uw_ref[rows, h * 2 * dk:(h + 1) * 2 * dk] for h in range(GDN_HEADS)]).reshape(nb, CHUNK, 2 * dk)
        u, w = uw[:, :, :dk], uw[:, :, dk:]
        el = jnp.exp(gl)
        kbe = kb * e
        qe = q * e
        kd = k * eg
        pt_do = _bm_tn(p, d_out)
        qet_do = _bm_tn(qe, d_out)

        ds = dstate_ref[...]
        d_vn_c, ds_c = [None] * ng, [None] * ng
        for c in range(ng - 1, -1, -1):
            ds_c[c] = ds
            d_vn_c[c] = of_chunk(pt_do, c) + _bm(of_chunk(kd, c), ds)
            ds = of_chunk(el, c) * ds + of_chunk(qet_do, c) - _bm_tn(of_chunk(w, c), d_vn_c[c])
        dstate_ref[...] = ds
        by_chunk = lambda xs: jnp.stack([xs[c][h] for h in range(GDN_HEADS) for c in range(ng)])
        d_vn, ds = by_chunk(d_vn_c), by_chunk(ds_c)

        on_s = _bm_nt(jnp.concatenate([d_out, d_vn], axis=1), s)
        d_qe, d_w = on_s[:, :CHUNK], -on_s[:, CHUNK:]
        d_p = jnp.where(incl, _bm_nt(d_out, vn), 0.0)
        d_kd = _bm_nt(vn, ds)
        d_both = _bm_tn(tm, jnp.concatenate([d_vn, d_w], axis=2))
        d_vb, d_kbe = d_both[:, :, :dk], d_both[:, :, dk:]
        d_a = -jnp.where(strict, _bm_nt(d_both, uw), 0.0)
        m = d_a * dm
        n = d_p * dm
        on_k = _bm(jnp.concatenate([m, n], axis=1), k)
        d_kb = on_k[:, :CHUNK] + d_kbe * e
        d_q = on_k[:, CHUNK:] + d_qe * e
        d_k = (_bm_tn(jnp.concatenate([m, n], axis=1), jnp.concatenate([kb, q], axis=1))
               + d_kd * eg + b * d_kb)
        d_v = b * d_vb
        r = d_a * a + d_p * p
        kd_term = rsum(d_kd * kd)
        d_gl = total(ds * s) * el + jnp.sum(kd_term, axis=1, keepdims=True)
        d_gam = (rsum(r) - _bm_tn(r, ones)[:, :, 0:1] + rsum(d_qe * qe) + rsum(d_kbe * kbe) - kd_term
                 + jnp.where(last_row, d_gl, 0.0))
        d_beta = rsum(d_kb * k) + rsum(d_vb * v)
        d_g = _bm(upper, d_gam * ones)[:, :, 0:1]
        per_head = lambda x: x.reshape(GDN_HEADS, ng * CHUNK, x.shape[-1])
        d_q, d_k, d_v, d_beta, d_g = (per_head(x) for x in (d_q, d_k, d_v, d_beta, d_g))
        gates = jnp.zeros((ng * CHUNK, LANES), F32)
        for h in range(GDN_HEADS):
            lanes = slice(h * dk, (h + 1) * dk)
            d_ref[0, rows, lanes] = d_q[h]
            d_ref[1, rows, lanes] = d_k[h]
            d_ref[2, rows, lanes] = d_v[h]
            gates = gates + (jnp.where(lane_ids == h, d_beta[h], 0.0)
                             + jnp.where(lane_ids == GDN_HEADS + h, d_g[h], 0.0))
        dgate_ref[rows, :] = gates

    d_spec = pl.BlockSpec((3, cps * CHUNK, width), lambda g: (0, steps - 1 - g, 0))
    return pl.pallas_call(
        body, name="gdn_bwd",
        grid=(steps,),
        in_specs=[rows_blk(width, 0), rows_blk(width, 1), rows_blk(width, 2), rows_blk(LANES), rows_blk(LANES), gate_r,
                  per_chunk(dk, dk), per_chunk(CHUNK, CHUNK), per_chunk(CHUNK, CHUNK), per_chunk(CHUNK, CHUNK),
                  rows_blk(2 * width), rows_blk(width), rows_blk(width)] + [_HBM] * nx,
        out_specs=[d_spec, rows_blk(LANES)] + [_HBM] * nx,
        out_shape=[jax.ShapeDtypeStruct((3, t_len, width), F32),
                   jax.ShapeDtypeStruct((t_len, LANES), F32)] + _direct_out_shapes(scatter, (True,) * nx),
        scratch_shapes=[pltpu.VMEM((GDN_HEADS, dk, dk), F32)] + (_direct_semaphores(nx) if nx else []),
        compiler_params=_params("arbitrary"),
    )(gact, gact, gact, beta_c, gam_c, gam_r, *saved, d_o, *scatter)


def _group_matrix(width, group):
    r = lax.broadcasted_iota(jnp.int32, (width, width), 0)
    c = lax.broadcasted_iota(jnp.int32, (width, width), 1)
    return ((r // group) == (c // group)).astype(F32)


def _post_call(o_sb, o_gd, proj, x, target, w_out, sbw, gdw, fw, tm=256):
    t_len, d = x.shape
    half = 512
    zsb_blk = 1536 // half
    zgd_blk = 3584 // half

    def body(osb_ref, ogd_ref, zsb_ref, zgd_ref, x_ref, tg_ref, wo_ref, sbw_ref, gdw_ref, fw_ref,
             dx2_ref, dosb_ref, dogd_ref, dz_ref, loss_ref, gfw_ref, gsb_ref, ggd_ref, gwo_ref):
        step = pl.program_id(0)

        @pl.when(step == 0)
        def _():
            loss_ref[...] = jnp.zeros_like(loss_ref)
            gfw_ref[...] = jnp.zeros_like(gfw_ref)
            gsb_ref[...] = jnp.zeros_like(gsb_ref)
            ggd_ref[...] = jnp.zeros_like(ggd_ref)
            gwo_ref[...] = jnp.zeros_like(gwo_ref)

        def head_forward(o, z, w, gmat, inv):
            r = lax.rsqrt(_running_sum_mm(o * o, gmat) * inv + EPS)
            nrm = o * r * w
            sg = _sigmoid(z)
            return r, nrm, sg, nrm * (z * sg)

        def head_backward(d_m, o, z, w, gmat, inv, r, nrm, sg):
            d_n = d_m * (z * sg)
            d_z = d_m * nrm * (sg * (1.0 + z * (1.0 - sg)))
            dnw = d_n * w
            d_o = r * dnw - o * (r * r * r) * (_running_sum_mm(dnw * o, gmat) * inv)
            return d_o, d_z, jnp.sum(d_n * o * r, axis=0, keepdims=True)

        g_sb = _group_matrix(half, SB_HEAD_DIM).astype(MXU_DTYPE)
        g_gd = _group_matrix(half, GDN_HEAD_DIM).astype(MXU_DTYPE)
        osb, ogd, zsb, zgd = osb_ref[...], ogd_ref[...], zsb_ref[...], zgd_ref[...]
        sbw_v, gdw_v = sbw_ref[...], gdw_ref[...]
        r_sb, n_sb, sg_sb, m_sb = head_forward(osb, zsb, sbw_v, g_sb, 1.0 / SB_HEAD_DIM)
        r_gd, n_gd, sg_gd, m_gd = head_forward(ogd, zgd, gdw_v, g_gd, 1.0 / GDN_HEAD_DIM)
        mixed = jnp.concatenate([m_sb, m_gd], axis=1).astype(MXU_DTYPE)
        wo = wo_ref[...]
        x2 = x_ref[...] + jnp.dot(mixed, wo, preferred_element_type=F32)
        r2 = lax.rsqrt(jnp.mean(x2 * x2, axis=-1, keepdims=True) + EPS)
        fw_v = fw_ref[...]
        err = x2 * r2 * fw_v - tg_ref[...]
        loss_ref[...] += 0.5 * jnp.sum(jnp.sum(err * err, axis=-1, keepdims=True) * (1.0 / d))
        dy = err * (1.0 / d)
        gg = dy * fw_v
        dx2 = r2 * gg - x2 * ((r2 * r2 * r2) * jnp.mean(gg * x2, axis=-1, keepdims=True))
        gfw_ref[...] += jnp.sum(dy * x2 * r2, axis=0, keepdims=True)
        dx2_ref[...] = dx2
        dx2b = dx2.astype(MXU_DTYPE)
        d_mixed = lax.dot_general(dx2b, wo, _NT, preferred_element_type=F32)
        gwo_ref[...] += lax.dot_general(mixed, dx2b, _TN, preferred_element_type=F32)
        d_osb, d_zsb, gsb = head_backward(d_mixed[:, :half], osb, zsb, sbw_v, g_sb, 1.0 / SB_HEAD_DIM, r_sb, n_sb, sg_sb)
        d_ogd, d_zgd, ggd = head_backward(d_mixed[:, half:], ogd, zgd, gdw_v, g_gd, 1.0 / GDN_HEAD_DIM, r_gd, n_gd, sg_gd)
        dosb_ref[...] = d_osb
        dogd_ref[...] = d_ogd
        dz_ref[0] = d_zsb
        dz_ref[1] = d_zgd
        gsb_ref[...] += gsb
        ggd_ref[...] += ggd

    row_blk = lambda w: pl.BlockSpec((tm, w), lambda i: (i, 0))
    fixed = lambda r, w: pl.BlockSpec((r, w), lambda i: (0, 0))
    return pl.pallas_call(
        body, name="post",
        grid=(t_len // tm,),
        in_specs=[row_blk(half), row_blk(half),
                  pl.BlockSpec((tm, half), lambda i: (i, zsb_blk)),
                  pl.BlockSpec((tm, half), lambda i: (i, zgd_blk)),
                  row_blk(d), row_blk(d), fixed(d, d), fixed(1, half), fixed(1, half), fixed(1, d)],
        out_specs=[row_blk(d), row_blk(half), row_blk(half),
                   pl.BlockSpec((2, tm, half), lambda i: (DPROJ_GATE_SLOT // 2, i, 0)),
                   fixed(1, LANES), fixed(1, d), fixed(1, half), fixed(1, half), fixed(d, d)],
        out_shape=[jax.ShapeDtypeStruct((t_len, d), F32)] + [jax.ShapeDtypeStruct((t_len, half), F32)] * 2
                  + [jax.ShapeDtypeStruct((len(DPROJ_PIECE_OF_SLOT), t_len, half), F32),
                     jax.ShapeDtypeStruct((1, LANES), F32), jax.ShapeDtypeStruct((1, d), F32),
                     jax.ShapeDtypeStruct((1, half), F32), jax.ShapeDtypeStruct((1, half), F32),
                     jax.ShapeDtypeStruct((d, d), F32)],
        compiler_params=_params("arbitrary"),
    )(o_sb, o_gd, proj, proj, x, target, w_out, sbw, gdw, fw)


def _piece_of_slot(s):
    return jnp.where(s < DPROJ_GDN_SLOT, s, jnp.where(s < DPROJ_GATE_SLOT, s + 1,
                                                     jnp.where(s == DPROJ_GATE_SLOT, 3, 7)))


def _gw_in_call(h_t, dproj8):
    d, t_len = h_t.shape
    n_piece, _, pw = dproj8.shape

    def body(ht_ref, dp_ref, gw_ref):
        gw_ref[...] = jnp.dot(ht_ref[...], dp_ref[0].astype(MXU_DTYPE), preferred_element_type=F32)

    return pl.pallas_call(
        body, name="gw_in",
        grid=(n_piece,),
        in_specs=[pl.BlockSpec((d, t_len), lambda s: (0, 0)),
                  pl.BlockSpec((1, t_len, pw), lambda s: (s, 0, 0))],
        out_specs=pl.BlockSpec((d, pw), lambda s: (0, _piece_of_slot(s))),
        out_shape=jax.ShapeDtypeStruct((d, n_piece * pw), F32),
        compiler_params=_params("arbitrary"),
    )(h_t, dproj8)


def _slot_of_piece(p):
    return jnp.where(p < DPROJ_GDN_SLOT, p, jnp.where(p == 3, DPROJ_GATE_SLOT, jnp.where(p < 7, p - 1, 7)))


def _gw_in_shards_call(h_t, dproj8, dsmall, out_dtype):
    d, t_len = h_t.shape
    n_piece, _, pw = dproj8.shape
    ns = dsmall.shape[1]
    n_pairs = N_DEV // 2

    def body(ht_ref, dp_ref, ds_ref, o_ref, sib_ref, prev_ref, gates_ref, send_ref, send_sems, recv_sems):
        p = pl.program_id(0)
        x_pos, y_pos, c = lax.axis_index("x"), lax.axis_index("y"), lax.axis_index("c")
        to_sibling = lambda pair: pltpu.make_async_remote_copy(
            src_ref=send_ref.at[pair], dst_ref=sib_ref.at[pair], send_sem=send_sems.at[pair],
            recv_sem=recv_sems.at[pair], device_id=(x_pos, y_pos, 1 - c), device_id_type=_MESH)

        @pl.when(p == 0)
        def _():
            gates_ref[...] = jnp.dot(ht_ref[...], ds_ref[...].astype(MXU_DTYPE), preferred_element_type=F32)

        def emit(s, tail):
            x = jnp.concatenate([prev_ref[...], tail], axis=1)
            y = x if s == 0 else pltpu.roll(x, SHARD_PAD - s, axis=1)
            shard = y[:, :SHARD_COLS].astype(out_dtype)
            o_ref[0] = shard

            @pl.when(c != s % 2)
            def _():
                send_ref[s // 2] = shard
                to_sibling(s // 2).start()

        @pl.when(p < n_piece)
        def _():
            cur = jnp.dot(ht_ref[...], dp_ref[0].astype(MXU_DTYPE), preferred_element_type=F32)
            for s in range(n_piece - 1):
                pl.when(p == s + 1)(functools.partial(emit, s, cur[:, :SHARD_PAD - pw]))
            prev_ref[...] = cur

        @pl.when(p == n_piece)
        def _():
            emit(n_piece - 1, gates_ref[...])
            for pair in range(n_pairs):
                to_sibling(pair).wait_send()
            for pair in range(n_pairs):
                to_sibling(pair).wait_recv()

    return pl.pallas_call(
        body, name="gw_in",
        grid=(n_piece + 1,),
        in_specs=[pl.BlockSpec((d, t_len), lambda p: (0, 0)),
                  pl.BlockSpec((1, t_len, pw), lambda p: (_slot_of_piece(jnp.minimum(p, n_piece - 1)), 0, 0)),
                  pl.BlockSpec((t_len, ns), lambda p: (0, 0))],
        out_specs=[pl.BlockSpec((1, d, SHARD_COLS), lambda p: (jnp.maximum(p - 1, 0), 0, 0)), _HBM],
        out_shape=[jax.ShapeDtypeStruct((N_DEV, d, SHARD_COLS), out_dtype),
                   jax.ShapeDtypeStruct((n_pairs, d, SHARD_COLS), out_dtype)],
        scratch_shapes=[pltpu.VMEM((d, pw), F32), pltpu.VMEM((d, ns), F32),
                        pltpu.VMEM((n_pairs, d, SHARD_COLS), out_dtype),
                        pltpu.SemaphoreType.DMA((n_pairs,)), pltpu.SemaphoreType.DMA((n_pairs,))],
        compiler_params=_params("arbitrary"),
    )(h_t, dproj8, dsmall)


def _gw_small_call(h_t, dsmall, tm=512):
    d, t_len = h_t.shape
    ns = dsmall.shape[1]

    def body(ht_ref, dp_ref, gw_ref):
        @pl.when(pl.program_id(0) == 0)
        def _():
            gw_ref[...] = jnp.zeros_like(gw_ref)

        gw_ref[...] += jnp.dot(ht_ref[...], dp_ref[...].astype(MXU_DTYPE), preferred_element_type=F32)

    return pl.pallas_call(
        body, name="gw_small",
        grid=(t_len // tm,),
        in_specs=[pl.BlockSpec((d, tm), lambda t: (0, t)),
                  pl.BlockSpec((tm, ns), lambda t: (t, 0))],
        out_specs=pl.BlockSpec((d, ns), lambda t: (0, 0)),
        out_shape=jax.ShapeDtypeStruct((d, ns), F32),
        compiler_params=_params("arbitrary"),
    )(h_t, dsmall)


def _dx_call(dproj8, dsmall, w_main, w_small, x, r, dx2, norm_w, chip_scatter=(), tm=256):
    t_len, d = x.shape
    n_piece, _, pw = dproj8.shape
    ns = dsmall.shape[1]
    nx = len(chip_scatter)
    steps = t_len // tm

    def body(*refs):
        dp_ref, ds_ref, wm_ref, ws_ref, x_ref, r_ref, dx2_ref, nw_ref = refs[:8]
        gx_ref, gnw_ref = refs[8 + nx:10 + nx]
        copies = lambda: _chip_copies(refs[8:8 + nx], refs[10 + nx:10 + 2 * nx], *refs[10 + 2 * nx:])
        if nx:
            pl.when(pl.program_id(0) == 0)(lambda: _start_all(copies()))

        @pl.when(pl.program_id(0) == 0)
        def _():
            gnw_ref[...] = jnp.zeros_like(gnw_ref)

        dh = lax.dot_general(ds_ref[...].astype(MXU_DTYPE), ws_ref[...], _NT, preferred_element_type=F32)
        for s, p in enumerate(DPROJ_PIECE_OF_SLOT):
            dh = dh + lax.dot_general(dp_ref[s].astype(MXU_DTYPE), wm_ref[:, p * pw:(p + 1) * pw], _NT,
                                      preferred_element_type=F32)
        xv, rv = x_ref[...], r_ref[...]
        dn = dh * nw_ref[...]
        gx_ref[...] = dx2_ref[...] + rv * dn - xv * ((rv * rv * rv) * jnp.mean(dn * xv, axis=-1, keepdims=True))
        gnw_ref[...] += jnp.sum(dh * xv * rv, axis=0, keepdims=True)
        if nx:
            pl.when(pl.program_id(0) == steps - 1)(lambda: _wait_all(copies()))

    return pl.pallas_call(
        body, name="dx",
        grid=(steps,),
        in_specs=[pl.BlockSpec((n_piece, tm, pw), lambda i: (0, i, 0)),
                  pl.BlockSpec((tm, ns), lambda i: (i, 0)),
                  pl.BlockSpec((d, n_piece * pw), lambda i: (0, 0)),
                  pl.BlockSpec((d, ns), lambda i: (0, 0)),
                  pl.BlockSpec((tm, d), lambda i: (i, 0)),
                  pl.BlockSpec((tm, 1), lambda i: (i, 0)),
                  pl.BlockSpec((tm, d), lambda i: (i, 0)),
                  pl.BlockSpec((1, d), lambda i: (0, 0))] + [_HBM] * nx,
        out_specs=[pl.BlockSpec((tm, d), lambda i: (i, 0)),
                   pl.BlockSpec((1, d), lambda i: (0, 0))] + [_HBM] * nx,
        out_shape=[jax.ShapeDtypeStruct((t_len, d), F32), jax.ShapeDtypeStruct((1, d), F32)]
                  + [jax.ShapeDtypeStruct(a.shape, a.dtype) for a in chip_scatter],
        scratch_shapes=_chip_semaphores(nx) if nx else [],
        compiler_params=_params("arbitrary"),
    )(dproj8, dsmall, w_main, w_small, x, r, dx2, norm_w, *chip_scatter)


def _exchange_call(name, srcs, per_peer):
    n = len(srcs)

    def body(*refs):
        src_refs, out_refs = refs[:n], refs[n:2 * n]
        copies = _direct_copies(src_refs, out_refs, *refs[2 * n:], per_peer)
        _start_all(copies)
        _wait_all(copies)

    hbm = pl.BlockSpec(memory_space=pl.ANY)
    return pl.pallas_call(
        body, name=name,
        in_specs=[hbm] * n, out_specs=[hbm] * n, out_shape=_direct_out_shapes(srcs, per_peer),
        scratch_shapes=_direct_semaphores(n),
    )(*srcs)


def _direct_out_shapes(srcs, per_peer):
    return [jax.ShapeDtypeStruct(s.shape if pp else (N_DEV,) + s.shape, s.dtype) for s, pp in zip(srcs, per_peer)]


def _direct_semaphores(n):
    return [pltpu.SemaphoreType.DMA((n * (N_DEV - 1),)), pltpu.SemaphoreType.DMA((n * (N_DEV - 1),)),
            pltpu.SemaphoreType.DMA((n,))]


def _direct_copies(src_refs, out_refs, send_sems, recv_sems, local_sems, per_peer):
    x, y, c = lax.axis_index("x"), lax.axis_index("y"), lax.axis_index("c")
    me = 4 * x + 2 * y + c
    local, remote = [], []
    for a in range(len(src_refs)):
        mine = src_refs[a].at[me] if per_peer[a] else src_refs[a]
        local.append(pltpu.make_async_copy(mine, out_refs[a].at[me], local_sems.at[a]))
    for k in range(1, N_DEV):
        kx, ky, kc = (k >> 2) & 1, (k >> 1) & 1, k & 1
        px = 1 - x if kx else x
        py = 1 - y if ky else y
        pc = 1 - c if kc else c
        peer = 4 * px + 2 * py + pc
        for a in range(len(src_refs)):
            sem = a * (N_DEV - 1) + (k - 1)
            remote.append(pltpu.make_async_remote_copy(
                src_ref=src_refs[a].at[peer] if per_peer[a] else src_refs[a], dst_ref=out_refs[a].at[me],
                send_sem=send_sems.at[sem], recv_sem=recv_sems.at[sem],
                device_id=(px, py, pc), device_id_type=pl.DeviceIdType.MESH))
    return local, remote


def _start_all(copies):
    local, remote = copies
    for cp in local + remote:
        cp.start()


def _wait_all(copies):
    local, remote = copies
    for cp in remote:
        cp.wait_send()
    for cp in remote:
        cp.wait_recv()
    for cp in local:
        cp.wait()


N_CHIPS = 4
_HBM = pl.BlockSpec(memory_space=pl.ANY)
_MESH = pl.DeviceIdType.MESH


def _gather_call(name, srcs):
    n = len(srcs)
    per = N_DEV - 1

    def body(*refs):
        src_refs, out_refs = refs[:n], refs[n:2 * n]
        send_sems, recv_sems, local_sems = refs[2 * n:]
        x, y, c = lax.axis_index("x"), lax.axis_index("y"), lax.axis_index("c")
        me, sibling = (x, y, c), (x, y, 1 - c)
        x_nbr, y_nbr, diagonal = (1 - x, y), (x, 1 - y), (1 - x, 1 - y)
        held = ((1 - x) * c + x * (1 - c), y * c + (1 - y) * (1 - c))
        onward = (x * c + (1 - x) * (1 - c), (1 - y) * c + y * (1 - c))
        slot = lambda px, py, pc: 4 * px + 2 * py + pc

        def copy(a, k, block, to, from_src=False):
            rows = out_refs[a].at[slot(*block)]
            return pltpu.make_async_remote_copy(
                src_ref=src_refs[a] if from_src else rows, dst_ref=rows,
                send_sem=send_sems.at[a * per + k], recv_sem=recv_sems.at[a * per + k],
                device_id=to, device_id_type=_MESH)

        local = [pltpu.make_async_copy(src_refs[a], out_refs[a].at[slot(*me)], local_sems.at[a]) for a in range(n)]
        started = []

        def start(cp):
            cp.start()
            started.append(cp)

        for cp in local:
            cp.start()
        for a in range(n):
            start(copy(a, 0, me, sibling, True))
            start(copy(a, 1, me, (*x_nbr, c), True))
            start(copy(a, 2, me, (*y_nbr, c), True))
        for a in range(n):
            copy(a, 1, (*x_nbr, c), me).wait_recv()
            copy(a, 2, (*y_nbr, c), me).wait_recv()
            start(copy(a, 3, (*held, c), (*onward, c)))
            start(copy(a, 4, (*x_nbr, c), sibling))
            start(copy(a, 5, (*y_nbr, c), sibling))
        for a in range(n):
            copy(a, 3, (*diagonal, c), me).wait_recv()
            start(copy(a, 6, (*diagonal, c), sibling))
        for a in range(n):
            copy(a, 0, sibling, me).wait_recv()
            for k, chip in ((4, x_nbr), (5, y_nbr), (6, diagonal)):
                copy(a, k, (*chip, 1 - c), me).wait_recv()
        for cp in started:
            cp.wait_send()
        for cp in local:
            cp.wait()

    return pl.pallas_call(
        body, name=name,
        in_specs=[_HBM] * n, out_specs=[_HBM] * n,
        out_shape=[jax.ShapeDtypeStruct((N_DEV,) + s.shape, s.dtype) for s in srcs],
        scratch_shapes=[pltpu.SemaphoreType.DMA((n * per,)), pltpu.SemaphoreType.DMA((n * per,)),
                        pltpu.SemaphoreType.DMA((n,))],
    )(*srcs)


def _sibling_send_call(name, srcs):
    n = len(srcs)

    def body(*refs):
        src_refs, out_refs = refs[:n], refs[n:2 * n]
        send_sems, recv_sems = refs[2 * n:]
        x, y, c = lax.axis_index("x"), lax.axis_index("y"), lax.axis_index("c")
        copies = []
        for a in range(n):
            for ch in range(N_CHIPS):
                copies.append(pltpu.make_async_remote_copy(
                    src_ref=src_refs[a].at[2 * ch + (1 - c)], dst_ref=out_refs[a].at[ch],
                    send_sem=send_sems.at[a * N_CHIPS + ch], recv_sem=recv_sems.at[a * N_CHIPS + ch],
                    device_id=(x, y, 1 - c), device_id_type=_MESH))
        for cp in copies:
            cp.start()
        for cp in copies:
            cp.wait_send()
        for cp in copies:
            cp.wait_recv()

    return pl.pallas_call(
        body, name=name,
        in_specs=[_HBM] * n, out_specs=[_HBM] * n,
        out_shape=[jax.ShapeDtypeStruct((N_CHIPS,) + s.shape[1:], s.dtype) for s in srcs],
        scratch_shapes=[pltpu.SemaphoreType.DMA((n * N_CHIPS,)), pltpu.SemaphoreType.DMA((n * N_CHIPS,))],
    )(*srcs)


def _pair_sum_call(name, parts, from_sibling, tr):
    _, rows, cols = parts.shape

    def body(p_ref, s_ref, o_ref):
        o_ref[...] = (p_ref[...].astype(F32) + s_ref[...].astype(F32)).astype(o_ref.dtype)

    return pl.pallas_call(
        body, name=name,
        grid=(N_CHIPS, rows // tr),
        in_specs=[pl.BlockSpec((1, tr, cols), lambda ch, i: (2 * ch + lax.axis_index("c"), i, 0)),
                  pl.BlockSpec((1, tr, cols), lambda ch, i: (ch, i, 0))],
        out_specs=pl.BlockSpec((1, tr, cols), lambda ch, i: (ch, i, 0)),
        out_shape=jax.ShapeDtypeStruct((N_CHIPS, rows, cols), WIRE_DTYPE),
        compiler_params=_params("arbitrary", "arbitrary"),
    )(parts, from_sibling)


def _chip_exchange_call(name, srcs):
    n = len(srcs)

    def body(*refs):
        copies = _chip_copies(refs[:n], refs[n:2 * n], *refs[2 * n:])
        _start_all(copies)
        _wait_all(copies)

    return pl.pallas_call(
        body, name=name,
        in_specs=[_HBM] * n, out_specs=[_HBM] * n,
        out_shape=[jax.ShapeDtypeStruct(s.shape, s.dtype) for s in srcs],
        scratch_shapes=_chip_semaphores(n),
    )(*srcs)


def _chip_semaphores(n):
    per = N_CHIPS - 1
    return [pltpu.SemaphoreType.DMA((n * per,)), pltpu.SemaphoreType.DMA((n * per,)), pltpu.SemaphoreType.DMA((n,))]


def _chip_copies(src_refs, out_refs, send_sems, recv_sems, local_sems):
    per = N_CHIPS - 1
    x, y, c = lax.axis_index("x"), lax.axis_index("y"), lax.axis_index("c")
    mine = 2 * x + y
    chips = [(1 - x, y), (x, 1 - y), (1 - x, 1 - y)]
    n = len(src_refs)
    local = [pltpu.make_async_copy(src_refs[a].at[mine], out_refs[a].at[mine], local_sems.at[a]) for a in range(n)]
    remote = []
    for a in range(n):
        for j, (px, py) in enumerate(chips):
            remote.append(pltpu.make_async_remote_copy(
                src_ref=src_refs[a].at[2 * px + py], dst_ref=out_refs[a].at[mine],
                send_sem=send_sems.at[a * per + j], recv_sem=recv_sems.at[a * per + j],
                device_id=(px, py, c), device_id_type=_MESH))
    return local, remote


def _adam_call(name, parts, w, m, v, tr):
    rows, cols = w.shape
    n_slots = parts.shape[0]

    def body(p_ref, w_ref, m_ref, v_ref, g_ref, d_ref, nm_ref, nv_ref):
        g = p_ref[0].astype(F32)
        for s in range(1, n_slots):
            g = g + p_ref[s].astype(F32)
        m_new = ADAM_B1 * m_ref[...] + (1.0 - ADAM_B1) * g
        v_new = ADAM_B2 * v_ref[...] + (1.0 - ADAM_B2) * (g * g)
        m_hat = m_new / (1.0 - ADAM_B1 ** ADAM_STEP)
        v_hat = v_new / (1.0 - ADAM_B2 ** ADAM_STEP)
        g_ref[...] = g
        d_ref[...] = -ADAM_LR * (m_hat / (jnp.sqrt(v_hat) + ADAM_EPS) + ADAM_WD * w_ref[...])
        nm_ref[...] = m_new
        nv_ref[...] = v_new

    blk = pl.BlockSpec((tr, cols), lambda i: (i, 0))
    return pl.pallas_call(
        body, name=name,
        grid=(rows // tr,),
        in_specs=[pl.BlockSpec((n_slots, tr, cols), lambda i: (0, i, 0)), blk, blk, blk],
        out_specs=[blk] * 4,
        out_shape=[jax.ShapeDtypeStruct((rows, cols), F32)] * 4,
        compiler_params=_params("arbitrary"),
    )(parts, w, m, v)


N_PIECES = 8
PIECE = 512
SHARD_COLS = 513
SHARD_PAD = 640
RELAYOUT_ROWS = 256


def _from_shards_call(shards):
    _, d, _ = shards.shape
    tr = RELAYOUT_ROWS

    def body(p_ref, m_ref, s_ref):
        lane = lax.broadcasted_iota(jnp.int32, (tr, SHARD_PAD), 1)
        pad = jnp.zeros((tr, SHARD_PAD - SHARD_COLS), F32)
        sh = [jnp.concatenate([p_ref[s].astype(F32), pad], axis=1) for s in range(N_DEV)]
        for p in range(N_PIECES):
            y = sh[p] if p == 0 else pltpu.roll(sh[p], p, axis=1)
            if p > 0:
                y = jnp.where(lane < p, pltpu.roll(sh[p - 1], SHARD_PAD - (SHARD_COLS - p), axis=1), y)
            m_ref[:, p * PIECE:(p + 1) * PIECE] = y[:, :PIECE].astype(m_ref.dtype)
        first_gate = N_PIECES * PIECE - (N_DEV - 1) * SHARD_COLS
        s_ref[...] = pltpu.roll(sh[N_DEV - 1], SHARD_PAD - first_gate, axis=1)[:, :LANES].astype(s_ref.dtype)

    return pl.pallas_call(
        body, name="w_in_from_shards",
        grid=(d // tr,),
        in_specs=[pl.BlockSpec((N_DEV, tr, SHARD_COLS), lambda i: (0, i, 0))],
        out_specs=[pl.BlockSpec((tr, N_PIECES * PIECE), lambda i: (i, 0)), pl.BlockSpec((tr, LANES), lambda i: (i, 0))],
        out_shape=[jax.ShapeDtypeStruct((d, N_PIECES * PIECE), shards.dtype),
                   jax.ShapeDtypeStruct((d, LANES), shards.dtype)],
        compiler_params=_params("arbitrary"),
    )(shards)


def _to_shards_call(main, gates, out_dtype):
    d = main.shape[0]
    tr = RELAYOUT_ROWS

    def body(m_ref, s_ref, o_ref):
        for s in range(N_DEV):
            if s < N_DEV - 1:
                x = m_ref[:, s * PIECE:s * PIECE + SHARD_PAD]
            else:
                x = jnp.concatenate([m_ref[:, s * PIECE:(s + 1) * PIECE], s_ref[...]], axis=1)
            y = x if s == 0 else pltpu.roll(x, SHARD_PAD - s, axis=1)
            o_ref[s] = y[:, :SHARD_COLS].astype(out_dtype)

    return pl.pallas_call(
        body, name="w_in_to_shards",
        grid=(d // tr,),
        in_specs=[pl.BlockSpec((tr, N_PIECES * PIECE), lambda i: (i, 0)), pl.BlockSpec((tr, LANES), lambda i: (i, 0))],
        out_specs=pl.BlockSpec((N_DEV, tr, SHARD_COLS), lambda i: (0, i, 0)),
        out_shape=jax.ShapeDtypeStruct((N_DEV, d, SHARD_COLS), out_dtype),
        compiler_params=_params("arbitrary"),
    )(main, gates)


def _adamw(g, w, m, v):
    m_new = ADAM_B1 * m + (1.0 - ADAM_B1) * g
    v_new = ADAM_B2 * v + (1.0 - ADAM_B2) * (g * g)
    m_hat = m_new / (1.0 - ADAM_B1 ** ADAM_STEP)
    v_hat = v_new / (1.0 - ADAM_B2 ** ADAM_STEP)
    return -ADAM_LR * (m_hat / (jnp.sqrt(v_hat) + ADAM_EPS) + ADAM_WD * w), m_new, v_new


def _adam_small_call(parts, ws, ms, vs):
    n = len(ws)
    n_slots = parts.shape[0]

    def body(*refs):
        p_ref = refs[0]
        w_refs, m_refs, v_refs = refs[1:1 + n], refs[1 + n:1 + 2 * n], refs[1 + 2 * n:1 + 3 * n]
        loss_ref = refs[1 + 3 * n]
        outs = refs[2 + 3 * n:]
        g_all = p_ref[0]
        for s in range(1, n_slots):
            g_all = g_all + p_ref[s]
        loss_ref[...] = g_all[n:n + 1, 0:1]
        for r in range(n):
            size = w_refs[r].shape[1]
            g = g_all[r:r + 1, :size]
            delta, m_new, v_new = _adamw(g, w_refs[r][...], m_refs[r][...], v_refs[r][...])
            for kind, val in enumerate((g, delta, m_new, v_new)):
                outs[kind * n + r][...] = val

    vm = pl.BlockSpec(memory_space=pltpu.VMEM)
    shapes = [jax.ShapeDtypeStruct(w.shape, F32) for w in ws]
    return pl.pallas_call(
        body, name="adam_small",
        in_specs=[vm] * (1 + 3 * n), out_specs=[vm] * (1 + 4 * n),
        out_shape=[jax.ShapeDtypeStruct((1, 1), F32)] + shapes * 4,
    )(parts, *ws, *ms, *vs)


_SMALL_ROWS = ("norm1_w", "final_norm_w", "sb_norm_w", "gdn_norm_w", "gdn_A_log", "gdn_dt_bias", "loss")


def _pack_small(vals, width):
    rows = [jnp.pad(a.reshape(1, -1).astype(F32), ((0, 0), (0, width - a.size))) for a in vals]
    rows += [jnp.zeros((1, width), F32)] * (8 - len(rows))
    return jnp.concatenate(rows, axis=0)


def _device_step(x2d, tgt, w_main, w_small, w_out_full, conv_full, norm1_w, sb_norm_w, gdn_A_log, gdn_dt_bias,
                 gdn_norm_w, final_norm_w, distributed=False, w_in_chip_partials=None):
    t_len, d = x2d.shape
    n_chunks = t_len // CHUNK
    w_main, w_small, w_out_full = (a.astype(MXU_DTYPE) for a in (w_main, w_small, w_out_full))
    w_small_t = w_small[:, :2 * GDN_HEADS].T

    pad_lanes = lambda a, lo: jnp.pad(a.reshape(1, -1), ((0, 0), (lo, LANES - lo - a.size)))
    alog_l, dtb_l = pad_lanes(gdn_A_log, GDN_HEADS), pad_lanes(gdn_dt_bias, GDN_HEADS)
    alog_c, dtb_c = alog_l[:, :8].T, dtb_l[:, :8].T
    sbw = jnp.tile(sb_norm_w, (1, 512 // SB_HEAD_DIM))
    gdw = jnp.tile(gdn_norm_w, (1, 512 // GDN_HEAD_DIM))
    fw = final_norm_w.reshape(1, d)

    if distributed:
        proj, ps, pst, h_t, r1, w_out_g, conv_g = _inproj_call(
            x2d, norm1_w, w_main, w_small, w_small_t, gather=(w_out_full, conv_full))
        w_out_full = w_out_g.reshape(d, d)
        conv_full = conv_g.transpose(1, 0, 2).reshape(CONV_WIDTH, N_DEV * conv_g.shape[2])
    else:
        proj, ps, pst, h_t, r1 = _inproj_call(x2d, norm1_w, w_main, w_small, w_small_t)
    o_sb, sp_total, sb_blocks_run = _sb_fwd_call(proj, t_len)
    gact = _gdn_prep_call(proj, conv_full, t_len)
    beta_l, gcol_l, grow = _gdn_gates_call(ps, pst, alog_l, dtb_l, alog_c, dtb_c, t_len)
    gam_r = grow[GDN_HEADS:2 * GDN_HEADS].reshape(GDN_HEADS, n_chunks, 1, CHUNK)
    o_gd, *gdn_saved = _gdn_fwd_call(gact, beta_l, gcol_l, gam_r, t_len)

    (dx2, d_osb, d_ogd, dproj8, loss_p, g_fw, g_sbw, g_gdw, g_wout) = _post_call(
        o_sb, o_gd, proj, x2d, tgt, w_out_full, sbw, gdw, fw)

    dproj8 = _sb_bwd_call(proj, sp_total, sb_blocks_run, d_osb, dproj8, t_len)
    if distributed:
        d_gact3, d_gates, g_wout = _gdn_bwd_call(gact, beta_l, gcol_l, gam_r, gdn_saved, d_ogd, t_len,
                                                 scatter=(g_wout.reshape(N_DEV, d // N_DEV, d),))
    else:
        d_gact3, d_gates = _gdn_bwd_call(gact, beta_l, gcol_l, gam_r, gdn_saved, d_ogd, t_len)
    dproj8, g_conv = _gdn_prep_bwd_call(proj, conv_full, d_gact3, dproj8, t_len)
    dsmall, g_alog, g_dtb = _gdn_gates_bwd_call(ps, alog_l, dtb_l, d_gates, t_len)

    if distributed:
        shards, from_sibling = _gw_in_shards_call(h_t, dproj8, dsmall, WIRE_DTYPE)
        grad_x, g_n1, g_w_in = _dx_call(dproj8, dsmall, w_main, w_small, x2d, r1, dx2, norm1_w,
                                        chip_scatter=(w_in_chip_partials(shards, from_sibling),))
    else:
        grad_x, g_n1 = _dx_call(dproj8, dsmall, w_main, w_small, x2d, r1, dx2, norm1_w)
        g_w_in = (_gw_in_call(h_t, dproj8), _gw_small_call(h_t, dsmall))
    return (loss_p, grad_x, g_n1, g_w_in, g_sbw, g_conv, g_alog, g_dtb, g_gdw, g_wout, g_fw)


def kernel(x, norm1_w, w_in, sb_norm_w, gdn_conv_w, gdn_A_log, gdn_dt_bias, gdn_norm_w, w_out, final_norm_w, loss_target, m_norm1_w, m_w_in, m_sb_norm_w, m_gdn_conv_w, m_gdn_A_log, m_gdn_dt_bias, m_gdn_norm_w, m_w_out, m_final_norm_w, v_norm1_w, v_w_in, v_sb_norm_w, v_gdn_conv_w, v_gdn_A_log, v_gdn_dt_bias, v_gdn_norm_w, v_w_out, v_final_norm_w):
    d = x.shape[2]
    shard_cols = w_in.shape[2]
    conv_cols = gdn_conv_w.shape[2]

    (w_in_g,) = _gather_call("gather_weights", [w_in[0].astype(WIRE_DTYPE)])
    w_main, w_small = _from_shards_call(w_in_g)

    def w_in_chip_partials(parts, from_sibling):
        return _pair_sum_call("pair_sum_w_in", parts, from_sibling, 256)

    (loss_p, grad_x, g_n1, p_w_in, g_sbw, g_conv, g_alog, g_dtb, g_gdw, p_wout, g_fw) = _device_step(
        x[0], loss_target[0], w_main, w_small, w_out[0].astype(WIRE_DTYPE), gdn_conv_w[0], norm1_w, sb_norm_w,
        gdn_A_log, gdn_dt_bias, gdn_norm_w, final_norm_w, distributed=True, w_in_chip_partials=w_in_chip_partials)

    g_conv_parts = g_conv.reshape(CONV_WIDTH, N_DEV, conv_cols).transpose(1, 0, 2)
    fold = lambda a, group: a.reshape(-1, group).sum(axis=0)
    small_g = _pack_small([g_n1, g_fw, fold(g_sbw, SB_HEAD_DIM), fold(g_gdw, GDN_HEAD_DIM),
                           g_alog[0, GDN_HEADS:2 * GDN_HEADS], g_dtb[0, GDN_HEADS:2 * GDN_HEADS],
                           loss_p[0, :1]], d)
    p_small, p_conv = _exchange_call("exchange_small", [small_g, g_conv_parts], [False, True])

    r_w_in = _adam_call("adam_w_in", p_w_in, w_in[0], m_w_in[0], v_w_in[0], 256)
    r_wout = _adam_call("adam_w_out", p_wout, w_out[0], m_w_out[0], v_w_out[0], d // N_DEV)
    r_conv = _adam_call("adam_conv", p_conv, gdn_conv_w[0], m_gdn_conv_w[0], v_gdn_conv_w[0], CONV_WIDTH)

    row = lambda a: a.reshape(1, -1)
    n_small = len(_SMALL_ROWS) - 1
    r_small = _adam_small_call(
        p_small,
        [norm1_w, row(final_norm_w), sb_norm_w, gdn_norm_w, gdn_A_log, gdn_dt_bias],
        [m_norm1_w, row(m_final_norm_w), m_sb_norm_w, m_gdn_norm_w, m_gdn_A_log, m_gdn_dt_bias],
        [v_norm1_w, row(v_final_norm_w), v_sb_norm_w, v_gdn_norm_w, v_gdn_A_log, v_gdn_dt_bias])

    def small_out(kind, name):
        out = r_small[1 + kind * n_small + _SMALL_ROWS.index(name)]
        return out.reshape(final_norm_w.shape) if name == "final_norm_w" else out

    def outputs(kind):
        return (small_out(kind, "norm1_w"), r_w_in[kind][None], small_out(kind, "sb_norm_w"), r_conv[kind][None],
                small_out(kind, "gdn_A_log"), small_out(kind, "gdn_dt_bias"), small_out(kind, "gdn_norm_w"),
                r_wout[kind][None], small_out(kind, "final_norm_w"))

    return (r_small[0][0, 0], grad_x[None], *outputs(0), *outputs(1), *outputs(2), *outputs(3))
```

```python
import functools

import jax
import jax.numpy as jnp
from jax import lax
from jax.experimental import pallas as pl
from jax.experimental.pallas import tpu as pltpu

F32 = jnp.float32
MXU_DTYPE = jnp.bfloat16
WIRE_DTYPE = jnp.bfloat16
EXACT = lax.Precision.HIGHEST
EPS = 1e-6
N_DEV = 8
SB_HEAD_DIM = 64
GDN_HEAD_DIM = 128
GDN_HEADS = 4
GDN_CHUNKS_PER_STEP = 4
GDN_BWD_GROUP = 1
CHUNK = 64
CONV_WIDTH = 4
LANES = 128
SB_BLOCK = 128
SB_BQ = 256
VMEM_LIMIT_BYTES = 56 * 1024 * 1024

DPROJ_PIECE_OF_SLOT = (0, 1, 2, 4, 5, 6, 3, 7)
DPROJ_SB_SLOT, DPROJ_GDN_SLOT, DPROJ_GATE_SLOT = 0, 3, 6

ADAM_LR = 0.001
ADAM_B1 = 0.9
ADAM_B2 = 0.999
ADAM_EPS = 1e-08
ADAM_WD = 0.01
ADAM_STEP = 10

_NN = (((1,), (0,)), ((), ()))
_NT = (((1,), (1,)), ((), ()))
_TN = (((0,), (0,)), ((), ()))
_BNN = (((2,), (1,)), ((0,), (0,)))
_BNT = (((2,), (2,)), ((0,), (0,)))
_BTN = (((1,), (1,)), ((0,), (0,)))


def _mm(a, b):
    return jnp.dot(a.astype(MXU_DTYPE), b.astype(MXU_DTYPE), preferred_element_type=F32)


def _mm_nt(a, b):
    return lax.dot_general(a.astype(MXU_DTYPE), b.astype(MXU_DTYPE), _NT, preferred_element_type=F32)


def _mm_tn(a, b):
    return lax.dot_general(a.astype(MXU_DTYPE), b.astype(MXU_DTYPE), _TN, preferred_element_type=F32)


def _mx(a, b):
    return jnp.dot(a, b, precision=EXACT, preferred_element_type=F32)


def _mx_nt(a, b):
    return lax.dot_general(a, b, _NT, precision=EXACT, preferred_element_type=F32)


def _mx_tn(a, b):
    return lax.dot_general(a, b, _TN, precision=EXACT, preferred_element_type=F32)


def _split(x):
    hi = x.astype(MXU_DTYPE)
    return hi, (x - hi.astype(F32)).astype(MXU_DTYPE)


def _m3_general(a, b, dims):
    ah, al = _split(a)
    bh, bl = _split(b)
    dot = lambda x, y: lax.dot_general(x, y, dims, preferred_element_type=F32)
    (contract, _), (batch, _) = dims
    free = [ax for ax in range(a.ndim) if ax not in contract and ax not in batch][0]
    m = a.shape[free]
    both = dot(jnp.concatenate([ah, al], axis=free), bh)
    out_axis = len(batch)
    hi_part = lax.slice_in_dim(both, 0, m, axis=out_axis)
    lo_part = lax.slice_in_dim(both, m, 2 * m, axis=out_axis)
    return hi_part + (dot(ah, bl) + lo_part)


def _m3(a, b):
    return _m3_general(a, b, _NN)


def _m3_nt(a, b):
    return _m3_general(a, b, _NT)


def _m3_tn(a, b):
    return _m3_general(a, b, _TN)


def _sigmoid(z):
    return 1.0 / (1.0 + jnp.exp(-z))


def _softplus(z):
    return jnp.maximum(z, 0.0) + jnp.log(1.0 + jnp.exp(-jnp.abs(z)))


def _params(*semantics):
    return pltpu.CompilerParams(dimension_semantics=semantics, vmem_limit_bytes=VMEM_LIMIT_BYTES)


def _inproj_call(x, norm_w, w_main, w_small, w_small_t, gather=(), tm=256):
    t_len, d = x.shape
    n = w_main.shape[1]
    ns = w_small.shape[1]
    nst = w_small_t.shape[0]
    ng = len(gather)
    steps = t_len // tm

    def body(*refs):
        x_ref, nw_ref, wm_ref, ws_ref, wst_ref = refs[:5]
        pm_ref, ps_ref, pst_ref, ht_ref, r_ref = refs[5 + ng:10 + ng]
        copies = lambda: _direct_copies(refs[5:5 + ng], refs[10 + ng:10 + 2 * ng], *refs[10 + 2 * ng:], (False,) * ng)
        if ng:
            pl.when(pl.program_id(0) == 0)(lambda: _start_all(copies()))
        xv = x_ref[...]
        r = lax.rsqrt(jnp.mean(xv * xv, axis=-1, keepdims=True) + EPS)
        h = xv * r * nw_ref[...]
        hb = h.astype(MXU_DTYPE)
        for n0 in range(0, n, 512):
            pm_ref[:, n0:n0 + 512] = jnp.dot(hb, wm_ref[:, n0:n0 + 512], preferred_element_type=F32)
        ps_ref[...] = jnp.dot(hb, ws_ref[...], preferred_element_type=F32)
        pst_ref[...] = lax.dot_general(wst_ref[...], hb, _NT, preferred_element_type=F32)
        ht_ref[...] = h.T.astype(MXU_DTYPE)
        r_ref[...] = r
        if ng:
            pl.when(pl.program_id(0) == steps - 1)(lambda: _wait_all(copies()))

    return pl.pallas_call(
        body, name="inproj",
        grid=(steps,),
        in_specs=[pl.BlockSpec((tm, d), lambda i: (i, 0)),
                  pl.BlockSpec((1, d), lambda i: (0, 0)),
                  pl.BlockSpec((d, n), lambda i: (0, 0)),
                  pl.BlockSpec((d, ns), lambda i: (0, 0)),
                  pl.BlockSpec((nst, d), lambda i: (0, 0))] + [_HBM] * ng,
        out_specs=[pl.BlockSpec((tm, n), lambda i: (i, 0)),
                   pl.BlockSpec((tm, ns), lambda i: (i, 0)),
                   pl.BlockSpec((nst, tm), lambda i: (0, i)),
                   pl.BlockSpec((d, tm), lambda i: (0, i)),
                   pl.BlockSpec((tm, 1), lambda i: (i, 0))] + [_HBM] * ng,
        out_shape=[jax.ShapeDtypeStruct((t_len, n), F32),
                   jax.ShapeDtypeStruct((t_len, ns), F32),
                   jax.ShapeDtypeStruct((nst, t_len), F32),
                   jax.ShapeDtypeStruct((d, t_len), MXU_DTYPE),
                   jax.ShapeDtypeStruct((t_len, 1), F32)] + _direct_out_shapes(gather, (False,) * ng),
        scratch_shapes=_direct_semaphores(ng) if ng else [],
        compiler_params=_params("arbitrary"),
    )(x, norm_w, w_main, w_small, w_small_t, *gather)


def _running_sum_mm(x, tri):
    hi = x.astype(MXU_DTYPE)
    lo = (x - hi.astype(F32)).astype(MXU_DTYPE)
    return jnp.dot(hi, tri, preferred_element_type=F32) + jnp.dot(lo, tri, preferred_element_type=F32)


def _sb_iotas():
    row_i = lax.broadcasted_iota(jnp.int32, (SB_BQ, SB_BLOCK), 0)
    col_i = lax.broadcasted_iota(jnp.int32, (SB_BQ, SB_BLOCK), 1)
    sq_r = lax.broadcasted_iota(jnp.int32, (SB_BLOCK, SB_BLOCK), 0)
    sq_c = lax.broadcasted_iota(jnp.int32, (SB_BLOCK, SB_BLOCK), 1)
    return row_i, col_i, sq_r, sq_c


SB_DIAG_BLOCKS = SB_BQ // SB_BLOCK
SB_EXP_FLOOR = -110.0


def _sb_keys_descending(qi, tile, carry, z_bounds, n_heads, has_free):
    group = SB_DIAG_BLOCKS
    n_free = group * qi
    diag = list(range(group - 1, -1, -1))
    carry = tile([n_free + j for j in diag], [True] * group, carry, [j * SB_BLOCK for j in diag])

    def largest_exponent(c):
        worst = jnp.max(z_bounds[0] - c[1])
        for h in range(1, n_heads):
            worst = jnp.maximum(worst, jnp.max(z_bounds[h] - c[1 + h]))
        return worst

    always = group if has_free else 0

    def cond(state):
        return (state[0] < n_free) & ((state[1] > SB_EXP_FLOOR) | (state[0] < always))

    def body(state):
        first = n_free - 1 - state[0]
        c = tile([first - j for j in range(group)], [False] * group, state[2:])
        return (state[0] + group, largest_exponent(c), *c)

    out = lax.while_loop(cond, body, (jnp.int32(0), largest_exponent(carry), *carry))
    return out[2:], out[0]


def _sb_keys_ascending(qi, n_run, tile, carry, has_free):
    group = SB_DIAG_BLOCKS
    n_free = group * qi
    diag = list(range(group))
    kjs, los, masked = [n_free + j for j in diag], [j * SB_BLOCK for j in diag], [True] * group
    if has_free:
        early = lambda s: [n_free - n_run + group * s + j for j in range(group)]
        carry = lax.fori_loop(0, n_run // group - 1, lambda s, c: tile(early(s), [False] * group, c), carry)
        kjs, los, masked = [n_free - group + j for j in range(group)] + kjs, [0] * group + los, [False] * group + masked
    return tile(kjs, masked, carry, los)


def _sb_fwd_call(proj, t_len):
    nq = t_len // SB_BQ
    scale = float(SB_HEAD_DIM) ** -0.5
    n_pairs = 512 // LANES
    per_pair = LANES // SB_HEAD_DIM

    def body(q_ref, k_ref, v_ref, o_ref, st_ref, nrun_ref):
        lane = lax.broadcasted_iota(jnp.int32, (1, LANES), 1)
        row_i, col_i, sq_r, sq_c = _sb_iotas()
        ge = (sq_r >= sq_c).astype(MXU_DTYPE)
        hms = [((lane // SB_HEAD_DIM) == hh).astype(F32) for hh in range(per_pair)]
        k_sq = k_ref[...] * k_ref[...]
        k_norms = [jnp.sqrt(jnp.max(jnp.sum(k_sq * hm, axis=-1, keepdims=True))) * (1.02 * scale) for hm in hms]

        def q_block(qi, has_free):
            r0 = qi * SB_BQ if isinstance(qi, int) else pl.multiple_of(qi * SB_BQ, SB_BQ)
            rows = pl.ds(r0, SB_BQ)
            q_all = q_ref[rows, :]
            qms = [(q_all * (hm * scale)).astype(MXU_DTYPE) for hm in hms]
            z_bounds = [jnp.sqrt(jnp.sum(q_all * q_all * hm, axis=-1, keepdims=True)) * kn
                        for hm, kn in zip(hms, k_norms)]

            def tile(kjs, masked, kc, los=None):
                heads = range(per_pair)
                los = los or [0] * len(kjs)
                pairs = [(t, h) for t in range(len(kjs)) for h in heads]
                add_rows = lambda full, lo, part: full + part if lo == 0 else jnp.concatenate(
                    [full[:lo], full[lo:] + part], axis=0)
                acc, cs = kc[0], list(kc[1:])
                s0s = [kj * SB_BLOCK if isinstance(kj, int) else pl.multiple_of(kj * SB_BLOCK, SB_BLOCK) for kj in kjs]
                kbs = [k_ref[pl.ds(s0, SB_BLOCK), :].astype(MXU_DTYPE) for s0 in s0s]
                v_alls = [v_ref[pl.ds(s0, SB_BLOCK), :] for s0 in s0s]
                vms = {(t, h): (v_alls[t] * hms[h]).astype(MXU_DTYPE) for t, h in pairs}
                zs = {(t, h): lax.dot_general(qms[h][los[t]:], kbs[t], _NT, preferred_element_type=F32)
                      for t, h in pairs}
                masks = [(col_i[lo:] + s0) < (row_i[lo:] + r0) if m else None for m, lo, s0 in zip(masked, los, s0s)]
                keep = lambda t, a: a if masks[t] is None else jnp.where(masks[t], a, 0.0)
                sps = {(t, h): keep(t, _softplus(zs[t, h])) for t, h in pairs}
                sums = {p: _running_sum_mm(sps[p], ge) for p in pairs}
                mass = {}
                for t, h in pairs:
                    mass[t, h] = cs[h] if t == 0 else add_rows(
                        mass[t - 1, h], los[t - 1], jnp.sum(sps[t - 1, h], axis=-1, keepdims=True))
                ws = {(t, h): keep(t, jnp.exp(zs[t, h] - (sums[t, h] + mass[t, h][los[t]:]))) for t, h in pairs}
                for t, h in pairs:
                    acc = add_rows(acc, los[t], jnp.dot(ws[t, h].astype(MXU_DTYPE), vms[t, h],
                                                        preferred_element_type=F32))
                last = len(kjs) - 1
                cs = [add_rows(mass[last, h], los[last], jnp.sum(sps[last, h], axis=-1, keepdims=True)) for h in heads]
                return (acc, *cs)

            zero_col = jnp.zeros((SB_BQ, 1), F32)
            out, n_run = _sb_keys_descending(
                qi, tile, (jnp.zeros((SB_BQ, LANES), F32),) + (zero_col,) * per_pair, z_bounds, per_pair, has_free)
            o_ref[rows, :] = out[0]
            for hh in range(per_pair):
                st_ref[hh, rows, :] = out[1 + hh]
            nrun_ref[pl.program_id(0), qi] = n_run

        q_block(0, False)
        lax.fori_loop(1, nq, lambda qi, carry: (q_block(qi, True), carry)[1], 0)

    return pl.pallas_call(
        body, name="sb_fwd",
        grid=(n_pairs,),
        in_specs=[pl.BlockSpec((t_len, LANES), lambda p: (0, p)),
                  pl.BlockSpec((t_len, LANES), lambda p: (0, n_pairs + p)),
                  pl.BlockSpec((t_len, LANES), lambda p: (0, 2 * n_pairs + p))],
        out_specs=[pl.BlockSpec((t_len, LANES), lambda p: (0, p)),
                   pl.BlockSpec((per_pair, t_len, 1), lambda p: (p, 0, 0)),
                   pl.BlockSpec(memory_space=pltpu.SMEM)],
        out_shape=[jax.ShapeDtypeStruct((t_len, 512), F32),
                   jax.ShapeDtypeStruct((n_pairs * per_pair, t_len, 1), F32),
                   jax.ShapeDtypeStruct((n_pairs, nq), jnp.int32)],
        compiler_params=_params("arbitrary"),
    )(proj, proj, proj)


def _sb_bwd_call(proj, sp_total, n_run_all, d_o, dproj, t_len):
    nq = t_len // SB_BQ
    scale = float(SB_HEAD_DIM) ** -0.5
    n_pairs = 512 // LANES
    per_pair = LANES // SB_HEAD_DIM

    def body(q_ref, k_ref, v_ref, st_ref, nrun_ref, do_ref, dproj_in_ref, d_ref):
        lane = lax.broadcasted_iota(jnp.int32, (1, LANES), 1)
        row_i, col_i, sq_r, sq_c = _sb_iotas()
        lt = (sq_r < sq_c).astype(MXU_DTYPE)
        le = (sq_r <= sq_c).astype(MXU_DTYPE)
        hms = [((lane // SB_HEAD_DIM) == hh).astype(F32) for hh in range(per_pair)]
        d_ref[1] = jnp.zeros((t_len, LANES), F32)
        d_ref[2] = jnp.zeros((t_len, LANES), F32)

        def q_block(qi, has_free):
            r0 = qi * SB_BQ if isinstance(qi, int) else pl.multiple_of(qi * SB_BQ, SB_BQ)
            rows = pl.ds(r0, SB_BQ)
            q_all, do_all = q_ref[rows, :], do_ref[rows, :]
            qms = [(q_all * (hm * scale)).astype(MXU_DTYPE) for hm in hms]
            doms = [(do_all * hm).astype(MXU_DTYPE) for hm in hms]
            totals = [st_ref[hh, rows, :] for hh in range(per_pair)]

            def tile(kjs, masked, kc, los=None):
                heads = range(per_pair)
                los = los or [0] * len(kjs)
                pairs = [(t, h) for t in range(len(kjs)) for h in heads]
                add_rows = lambda full, lo, part: full + part if lo == 0 else jnp.concatenate(
                    [full[:lo], full[lo:] + part], axis=0)
                rsum = lambda a: jnp.sum(a, axis=-1, keepdims=True)
                dq, cls, gls = kc[0], list(kc[1:1 + per_pair]), list(kc[1 + per_pair:])
                s0s = [kj * SB_BLOCK if isinstance(kj, int) else pl.multiple_of(kj * SB_BLOCK, SB_BLOCK) for kj in kjs]
                k_alls = [k_ref[pl.ds(s0, SB_BLOCK), :] for s0 in s0s]
                v_alls = [v_ref[pl.ds(s0, SB_BLOCK), :] for s0 in s0s]
                kbs = [k_all.astype(MXU_DTYPE) for k_all in k_alls]
                vms = {(t, h): (v_alls[t] * hms[h]).astype(MXU_DTYPE) for t, h in pairs}
                kms = {(t, h): (k_alls[t] * (hms[h] * scale)).astype(MXU_DTYPE) for t, h in pairs}
                q_live = {(t, h): qms[h][los[t]:] for t, h in pairs}
                do_live = {(t, h): doms[h][los[t]:] for t, h in pairs}
                zs = {p: lax.dot_general(q_live[p], kbs[p[0]], _NT, preferred_element_type=F32) for p in pairs}
                das = {p: lax.dot_general(do_live[p], vms[p], _NT, preferred_element_type=F32) for p in pairs}
                masks = [(col_i[lo:] + s0) < (row_i[lo:] + r0) if m else None for m, lo, s0 in zip(masked, los, s0s)]
                keep = lambda t, a: a if masks[t] is None else jnp.where(masks[t], a, 0.0)
                sp_alls = {p: _softplus(zs[p]) for p in pairs}
                sps = {(t, h): keep(t, sp_alls[t, h]) for t, h in pairs}
                lefts = {p: _running_sum_mm(sps[p], lt) for p in pairs}
                cl = {}
                for t, h in pairs:
                    cl[t, h] = cls[h] if t == 0 else add_rows(cl[t - 1, h], los[t - 1], rsum(sps[t - 1, h]))
                ws = {(t, h): keep(t, jnp.exp(zs[t, h] - ((totals[h] - cl[t, h])[los[t]:] - lefts[t, h])))
                      for t, h in pairs}
                gs = {p: das[p] * ws[p] for p in pairs}
                g_sums = {p: _running_sum_mm(gs[p], le) for p in pairs}
                gl = {}
                for t, h in pairs:
                    gl[t, h] = gls[h] if t == 0 else add_rows(gl[t - 1, h], los[t - 1], rsum(gs[t - 1, h]))
                dzs = {(t, h): keep(t, gs[t, h] - jnp.exp(zs[t, h] - sp_alls[t, h]) * (gl[t, h][los[t]:] + g_sums[t, h])
                               ).astype(MXU_DTYPE) for t, h in pairs}
                for t in range(len(kjs)):
                    dk_t = jnp.zeros((SB_BLOCK, LANES), F32)
                    dv_t = jnp.zeros((SB_BLOCK, LANES), F32)
                    for h in heads:
                        dq = add_rows(dq, los[t], jnp.dot(dzs[t, h], kms[t, h], preferred_element_type=F32))
                        dk_t = dk_t + lax.dot_general(dzs[t, h], q_live[t, h], _TN, preferred_element_type=F32)
                        dv_t = dv_t + lax.dot_general(ws[t, h].astype(MXU_DTYPE), do_live[t, h], _TN,
                                                      preferred_element_type=F32)
                    d_ref[1, pl.ds(s0s[t], SB_BLOCK), :] += dk_t
                    d_ref[2, pl.ds(s0s[t], SB_BLOCK), :] += dv_t
                last = len(kjs) - 1
                cls = [add_rows(cl[last, h], los[last], rsum(sps[last, h])) for h in heads]
                gls = [add_rows(gl[last, h], los[last], rsum(gs[last, h])) for h in heads]
                return (dq, *cls, *gls)

            zero_col = jnp.zeros((SB_BQ, 1), F32)
            out = _sb_keys_ascending(qi, nrun_ref[pl.program_id(0), qi], tile,
                                     (jnp.zeros((SB_BQ, LANES), F32),) + (zero_col,) * (2 * per_pair), has_free)
            d_ref[0, rows, :] = out[0]

        q_block(0, False)
        lax.fori_loop(1, nq, lambda qi, carry: (q_block(qi, True), carry)[1], 0)

    col = lambda off: pl.BlockSpec((t_len, LANES), lambda p: (0, off + p))
    return pl.pallas_call(
        body, name="sb_bwd",
        grid=(n_pairs,),
        in_specs=[col(0), col(n_pairs), col(2 * n_pairs),
                  pl.BlockSpec((per_pair, t_len, 1), lambda p: (p, 0, 0)),
                  pl.BlockSpec(memory_space=pltpu.SMEM), col(0), _HBM],
        out_specs=pl.BlockSpec((3, t_len, LANES), lambda p: (DPROJ_SB_SLOT // 3, 0, p)),
        out_shape=jax.ShapeDtypeStruct(dproj.shape, dproj.dtype),
        input_output_aliases={6: 0},
        compiler_params=_params("arbitrary"),
    )(proj, proj, proj, sp_total, n_run_all, d_o, dproj)


def _conv_taps(xin, rows, t_len):
    taps = []
    for i in range(CONV_WIDTH):
        shift = CONV_WIDTH - 1 - i
        if shift == 0:
            taps.append(xin)
        else:
            taps.append(jnp.where(rows >= shift, pltpu.roll(xin, shift, axis=0), 0.0))
    return taps


def _gdn_prep_body_common(x_ref, w_ref, t_len):
    j = pl.program_id(0)
    xin = x_ref[...]
    rows = lax.broadcasted_iota(jnp.int32, (t_len, LANES), 0)
    taps = _conv_taps(xin, rows, t_len)
    pre = taps[0] * w_ref[0:1, :]
    for i in range(1, CONV_WIDTH):
        pre = pre + taps[i] * w_ref[i:i + 1, :]
    sg = _sigmoid(pre)
    act = pre * sg
    is_qk = j < 2 * GDN_HEADS
    nrm = jnp.where(is_qk, lax.rsqrt(jnp.sum(act * act, axis=-1, keepdims=True) + EPS), 1.0)
    sc = jnp.where(j < GDN_HEADS, float(GDN_HEAD_DIM) ** -0.5, 1.0)
    return j, rows, taps, pre, sg, act, is_qk, nrm, sc


def _gdn_prep_call(proj, conv_w, t_len):
    first = 2048 // LANES

    def body(x_ref, w_ref, out_ref):
        _, _, _, _, _, act, _, nrm, sc = _gdn_prep_body_common(x_ref, w_ref, t_len)
        out_ref[...] = act * nrm * sc

    return pl.pallas_call(
        body, name="gdn_prep",
        grid=(3 * GDN_HEADS,),
        in_specs=[pl.BlockSpec((t_len, LANES), lambda j: (0, first + j)),
                  pl.BlockSpec((CONV_WIDTH, LANES), lambda j: (0, j))],
        out_specs=pl.BlockSpec((t_len, LANES), lambda j: (0, j)),
        out_shape=jax.ShapeDtypeStruct((t_len, 3 * 512), F32),
        compiler_params=_params("arbitrary"),
    )(proj, conv_w)


def _gdn_prep_bwd_call(proj, conv_w, d_act3, dproj, t_len):
    first = 2048 // LANES

    def body(x_ref, w_ref, d_ref, dproj_in_ref, dx_ref, dw_ref):
        _, rows, taps, pre, sg, act, is_qk, nrm, sc = _gdn_prep_body_common(x_ref, w_ref, t_len)
        d_out = d_ref[0]
        dn = d_out * sc
        d_norm = nrm * dn - act * (nrm * nrm * nrm) * jnp.sum(dn * act, axis=-1, keepdims=True)
        d_act = jnp.where(is_qk, d_norm, d_out)
        d_pre = d_act * sg * (1.0 + pre * (1.0 - sg))
        dx = d_pre * w_ref[CONV_WIDTH - 1:CONV_WIDTH, :]
        dw_ref[CONV_WIDTH - 1:CONV_WIDTH, :] = jnp.sum(d_pre * taps[CONV_WIDTH - 1], axis=0, keepdims=True)
        for i in range(CONV_WIDTH - 1):
            shift = CONV_WIDTH - 1 - i
            up = jnp.where(rows < t_len - shift, pltpu.roll(d_pre, t_len - shift, axis=0), 0.0)
            dx = dx + up * w_ref[i:i + 1, :]
            dw_ref[i:i + 1, :] = jnp.sum(d_pre * taps[i], axis=0, keepdims=True)
        dx_ref[0] = dx

    return pl.pallas_call(
        body, name="gdn_prep_bwd",
        grid=(3 * GDN_HEADS,),
        in_specs=[pl.BlockSpec((t_len, LANES), lambda j: (0, first + j)),
                  pl.BlockSpec((CONV_WIDTH, LANES), lambda j: (0, j)),
                  pl.BlockSpec((1, t_len, LANES), lambda j: (j // GDN_HEADS, 0, j % GDN_HEADS)), _HBM],
        out_specs=[pl.BlockSpec((1, t_len, LANES), lambda j: (DPROJ_GDN_SLOT + j // GDN_HEADS, 0, j % GDN_HEADS)),
                   pl.BlockSpec((CONV_WIDTH, LANES), lambda j: (0, j))],
        out_shape=[jax.ShapeDtypeStruct(dproj.shape, dproj.dtype),
                   jax.ShapeDtypeStruct((CONV_WIDTH, 3 * 512), F32)],
        input_output_aliases={3: 0},
        compiler_params=_params("arbitrary"),
    )(proj, conv_w, d_act3, dproj)


def _chunk_cumsum_matrix():
    r = lax.broadcasted_iota(jnp.int32, (LANES, LANES), 0)
    c = lax.broadcasted_iota(jnp.int32, (LANES, LANES), 1)
    return ((r <= c) & ((r // CHUNK) == (c // CHUNK))).astype(F32)


def _gdn_gates_call(ps, pst, alog_l, dtb_l, alog_c, dtb_c, t_len):
    def body(ps_ref, pst_ref, al_ref, dl_ref, ac_ref, dc_ref, beta_ref, gcol_ref, grow_ref):
        upper = _chunk_cumsum_matrix()
        lower = upper.T
        psv = ps_ref[...]
        beta_ref[...] = _sigmoid(psv)
        g_l = -jnp.exp(al_ref[...]) * _softplus(psv + dl_ref[...])
        g_r = -jnp.exp(ac_ref[...]) * _softplus(pst_ref[...] + dc_ref[...])
        for w in range(t_len // LANES):
            sl = slice(w * LANES, (w + 1) * LANES)
            gcol_ref[sl, :] = _mx(lower, g_l[sl, :])
            grow_ref[:, sl] = _mx(g_r[:, sl], upper)

    vm = pl.BlockSpec(memory_space=pltpu.VMEM)
    return pl.pallas_call(
        body, name="gdn_gates",
        in_specs=[vm] * 6, out_specs=[vm] * 3,
        out_shape=[jax.ShapeDtypeStruct((t_len, LANES), F32),
                   jax.ShapeDtypeStruct((t_len, LANES), F32),
                   jax.ShapeDtypeStruct((8, t_len), F32)],
        compiler_params=pltpu.CompilerParams(vmem_limit_bytes=VMEM_LIMIT_BYTES),
    )(ps, pst, alog_l, dtb_l, alog_c, dtb_c)


def _gdn_gates_bwd_call(ps, alog_l, dtb_l, d_l, t_len):
    def body(ps_ref, al_ref, dl_ref, d_ref, dps_ref, gal_ref, gdt_ref):
        lane = lax.broadcasted_iota(jnp.int32, (1, LANES), 1)
        psv = ps_ref[...]
        dv = d_ref[...]
        beta = _sigmoid(psv)
        ea = jnp.exp(al_ref[...])
        arg = psv + dl_ref[...]
        g = -ea * _softplus(arg)
        d_a = dv * (-ea) * _sigmoid(arg)
        is_a = (lane >= GDN_HEADS) & (lane < 2 * GDN_HEADS)
        dps_ref[...] = jnp.where(lane < GDN_HEADS, dv * beta * (1.0 - beta), jnp.where(is_a, d_a, 0.0))
        gdt_ref[...] = jnp.where(is_a, jnp.sum(d_a, axis=0, keepdims=True), 0.0)
        gal_ref[...] = jnp.where(is_a, jnp.sum(dv * g, axis=0, keepdims=True), 0.0)

    vm = pl.BlockSpec(memory_space=pltpu.VMEM)
    return pl.pallas_call(
        body, name="gdn_gates_bwd",
        in_specs=[vm] * 4, out_specs=[vm] * 3,
        out_shape=[jax.ShapeDtypeStruct((t_len, LANES), F32),
                   jax.ShapeDtypeStruct((1, LANES), F32),
                   jax.ShapeDtypeStruct((1, LANES), F32)],
        compiler_params=pltpu.CompilerParams(vmem_limit_bytes=VMEM_LIMIT_BYTES),
    )(ps, alog_l, dtb_l, d_l)


def _bm(a, b):
    return _m3_general(a, b, _BNN)


def _bm_nt(a, b):
    return _m3_general(a, b, _BNT)


def _bm_tn(a, b):
    return _m3_general(a, b, _BTN)


def _heads_of(ref, rows):
    return jnp.stack([ref[rows, h * GDN_HEAD_DIM:(h + 1) * GDN_HEAD_DIM] for h in range(GDN_HEADS)])


def _chunk_terms(q_ref, k_ref, v_ref, b_ref, gc_ref, gr_ref, c, incl, strict, n=1, scores=True):
    r0 = c * CHUNK if isinstance(c, int) else pl.multiple_of(c * CHUNK, CHUNK)
    rows = pl.ds(r0, n * CHUNK)
    per_chunk = lambda x: x.reshape(GDN_HEADS * n, CHUNK, x.shape[-1])
    q, k, v = (per_chunk(_heads_of(ref, rows)) for ref in (q_ref, k_ref, v_ref))
    lane_ids = lax.broadcasted_iota(jnp.int32, (1, LANES), 1)
    pick = lambda slab, first: jnp.stack([jnp.sum(jnp.where(lane_ids == first + h, slab, 0.0), axis=-1, keepdims=True)
                                          for h in range(GDN_HEADS)])
    b = per_chunk(pick(b_ref[rows, :], 0))
    gc = per_chunk(pick(gc_ref[rows, :], GDN_HEADS))
    gr = gr_ref[:, c] if n == 1 else gr_ref[:, c:c + n].reshape(GDN_HEADS * n, 1, CHUNK)
    dm = jnp.where(incl, jnp.exp(jnp.where(incl, gc - gr, 0.0)), 0.0)
    kb = k * b
    vb = v * b
    e = jnp.exp(gc)
    a = p = None
    if scores:
        kk_qk = _bm_nt(jnp.concatenate([kb, q], axis=1), k)
        a = jnp.where(strict, kk_qk[:, :CHUNK] * dm, 0.0)
        p = jnp.where(incl, kk_qk[:, CHUNK:] * dm, 0.0)
    gl = gc[:, CHUNK - 1:CHUNK, :]
    eg = jnp.exp(gl - gc)
    return rows, q, k, v, b, gc, dm, kb, vb, e, a, p, gl, eg


def _unit_lower_inverse(a, eye):
    x = -a
    tm = eye + x
    xp = _bm(x, x)
    for _ in range(4):
        both = _bm(jnp.concatenate([xp, tm], axis=1), xp)
        tm = tm + both[:, CHUNK:]
        xp = both[:, :CHUNK]
    return tm + _bm(tm, xp)


def _gdn_specs(t_len, n_chunks, reverse):
    cps = GDN_CHUNKS_PER_STEP
    steps = n_chunks // cps
    at = (lambda g: steps - 1 - g) if reverse else (lambda g: g)
    rows_blk = lambda width, part=0: pl.BlockSpec((cps * CHUNK, width), lambda g: (at(g), part))
    gate_r = pl.BlockSpec((GDN_HEADS, cps, 1, CHUNK), lambda g: (0, at(g), 0, 0))
    per_chunk = lambda r, c: pl.BlockSpec((GDN_HEADS, cps, r, c), lambda g: (0, at(g), 0, 0))
    return cps, steps, rows_blk, gate_r, per_chunk


def _gdn_fwd_call(gact, beta_c, gam_c, gam_r, t_len):
    n_chunks = t_len // CHUNK
    dk = GDN_HEAD_DIM
    width = GDN_HEADS * dk
    cps, steps, rows_blk, gate_r, per_chunk = _gdn_specs(t_len, n_chunks, False)

    def body(q_ref, k_ref, v_ref, b_ref, gc_ref, gr_ref, o_ref, s_ref, t_ref, a_ref, p_ref, uw_ref, vn_ref, state_ref):
        row = lax.broadcasted_iota(jnp.int32, (CHUNK, CHUNK), 0)
        col = lax.broadcasted_iota(jnp.int32, (CHUNK, CHUNK), 1)
        incl, strict = row >= col, row > col
        eye = (row == col).astype(F32)

        @pl.when(pl.program_id(0) == 0)
        def _():
            state_ref[...] = jnp.zeros_like(state_ref)

        _, q, k, v, b, gc, dm, kb, vb, e, a, p, gl, eg = _chunk_terms(
            q_ref, k_ref, v_ref, b_ref, gc_ref, gr_ref, 0, incl, strict, cps)
        tm = _unit_lower_inverse(a, eye)
        uw = _bm(tm, jnp.concatenate([vb, kb * e], axis=2))
        w_qe = jnp.concatenate([uw[:, :, dk:], q * e], axis=1)
        u, kd, decay = uw[:, :, :dk], k * eg, jnp.exp(gl)
        per_chunk_block = lambda x: x.reshape(GDN_HEADS, cps, CHUNK, CHUNK)
        t_ref[...], a_ref[...], p_ref[...] = per_chunk_block(tm), per_chunk_block(a), per_chunk_block(p)
        uw_heads = uw.reshape(GDN_HEADS, cps * CHUNK, 2 * dk)
        for h in range(GDN_HEADS):
            uw_ref[:, h * 2 * dk:(h + 1) * 2 * dk] = uw_heads[h]

        of_chunk = lambda x, c: jnp.stack([x[h * cps + c] for h in range(GDN_HEADS)])
        s = state_ref[...]
        for c in range(cps):
            ws_qs = _bm(of_chunk(w_qe, c), s)
            vn = of_chunk(u, c) - ws_qs[:, :CHUNK]
            o = ws_qs[:, CHUNK:] + _bm(of_chunk(p, c), vn)
            for h in range(GDN_HEADS):
                o_ref[c * CHUNK:(c + 1) * CHUNK, h * dk:(h + 1) * dk] = o[h]
                vn_ref[c * CHUNK:(c + 1) * CHUNK, h * dk:(h + 1) * dk] = vn[h]
            s_ref[:, c] = s
            s = s * of_chunk(decay, c) + _bm_tn(of_chunk(kd, c), vn)
        state_ref[...] = s

    scores = jax.ShapeDtypeStruct((GDN_HEADS, n_chunks, CHUNK, CHUNK), F32)
    return pl.pallas_call(
        body, name="gdn_fwd",
        grid=(steps,),
        in_specs=[rows_blk(width, 0), rows_blk(width, 1), rows_blk(width, 2), rows_blk(LANES), rows_blk(LANES), gate_r],
        out_specs=[rows_blk(width), per_chunk(dk, dk), per_chunk(CHUNK, CHUNK), per_chunk(CHUNK, CHUNK),
                   per_chunk(CHUNK, CHUNK), rows_blk(2 * width), rows_blk(width)],
        out_shape=[jax.ShapeDtypeStruct((t_len, width), F32),
                   jax.ShapeDtypeStruct((GDN_HEADS, n_chunks, dk, dk), F32), scores, scores, scores,
                   jax.ShapeDtypeStruct((t_len, 2 * width), F32), jax.ShapeDtypeStruct((t_len, width), F32)],
        scratch_shapes=[pltpu.VMEM((GDN_HEADS, dk, dk), F32)],
        compiler_params=_params("arbitrary"),
    )(gact, gact, gact, beta_c, gam_c, gam_r)


def _gdn_bwd_call(gact, beta_c, gam_c, gam_r, saved, d_o, t_len, scatter=()):
    n_chunks = t_len // CHUNK
    dk = GDN_HEAD_DIM
    width = GDN_HEADS * dk
    cps, steps, rows_blk, gate_r, per_chunk = _gdn_specs(t_len, n_chunks, True)
    nx = len(scatter)
    n_in = 13

    def body(*refs):
        q_ref, k_ref, v_ref, b_ref, gc_ref, gr_ref = refs[:6]
        saved_refs, do_ref = refs[6:12], refs[12]
        d_ref, dgate_ref = refs[n_in + nx:n_in + 2 + nx]
        dstate_ref = refs[n_in + 2 + 2 * nx]
        copies = lambda: _direct_copies(refs[n_in:n_in + nx], refs[n_in + 2 + nx:n_in + 2 + 2 * nx],
                                        *refs[n_in + 3 + 2 * nx:], (True,) * nx)
        if nx:
            pl.when(pl.program_id(0) == 0)(lambda: _start_all(copies()))
        row = lax.broadcasted_iota(jnp.int32, (CHUNK, CHUNK), 0)
        col = lax.broadcasted_iota(jnp.int32, (CHUNK, CHUNK), 1)
        incl, strict = row >= col, row > col
        ng = GDN_BWD_GROUP
        nb = GDN_HEADS * ng
        upper = jnp.broadcast_to((row <= col).astype(F32), (nb, CHUNK, CHUNK))
        ones = jnp.ones((nb, CHUNK, LANES), F32)
        last_row = lax.broadcasted_iota(jnp.int32, (CHUNK, 1), 0) == CHUNK - 1
        lane_ids = lax.broadcasted_iota(jnp.int32, (1, LANES), 1)
        rsum = lambda m: jnp.sum(m, axis=-1, keepdims=True)
        total = lambda m: jnp.sum(rsum(m), axis=1, keepdims=True)
        of_chunk = lambda x, c: jnp.stack([x[h * ng + c] for h in range(GDN_HEADS)])

        @pl.when(pl.program_id(0) == 0)
        def _():
            dstate_ref[...] = jnp.zeros_like(dstate_ref)

        for c0 in range(cps - ng, -1, -ng):
            group(c0, q_ref, k_ref, v_ref, b_ref, gc_ref, gr_ref, saved_refs, do_ref, d_ref, dgate_ref, dstate_ref,
                  incl, strict, upper, ones, last_row, lane_ids, rsum, total, of_chunk)
        if nx:
            pl.when(pl.program_id(0) == steps - 1)(lambda: _wait_all(copies()))

    def group(c0, q_ref, k_ref, v_ref, b_ref, gc_ref, gr_ref, saved_refs, do_ref, d_ref, dgate_ref, dstate_ref,
              incl, strict, upper, ones, last_row, lane_ids, rsum, total, of_chunk):
        ng = GDN_BWD_GROUP
        nb = GDN_HEADS * ng
        rows = pl.ds(c0 * CHUNK, ng * CHUNK)
        s_ref, t_ref, a_ref, p_ref, uw_ref, vn_ref = saved_refs
        _, q, k, v, b, gc, dm, kb, vb, e, _, _, gl, eg = _chunk_terms(
            q_ref, k_ref, v_ref, b_ref, gc_ref, gr_ref, c0, incl, strict, ng, scores=False)
        s = s_ref[:, c0:c0 + ng].reshape(nb, dk, dk)
        tm = t_ref[:, c0:c0 + ng].reshape(nb, CHUNK, CHUNK)
        a = a_ref[:, c0:c0 + ng].reshape(nb, CHUNK, CHUNK)
        p = p_ref[:, c0:c0 + ng].reshape(nb, CHUNK, CHUNK)
        d_out = _heads_of(do_ref, rows).reshape(nb, CHUNK, dk)
        vn = _heads_of(vn_ref, rows).reshape(nb, CHUNK, dk)
        uw = jnp.stack([uw_ref[rows, h * 2 * dk:(h + 1) * 2 * dk] for h in range(GDN_HEADS)]).reshape(nb, CHUNK, 2 * dk)
        u, w = uw[:, :, :dk], uw[:, :, dk:]
        el = jnp.exp(gl)
        kbe = kb * e
        qe = q * e
        kd = k * eg
        pt_do = _bm_tn(p, d_out)
        qet_do = _bm_tn(qe, d_out)

        ds = dstate_ref[...]
        d_vn_c, ds_c = [None] * ng, [None] * ng
        for c in range(ng - 1, -1, -1):
            ds_c[c] = ds
            d_vn_c[c] = of_chunk(pt_do, c) + _bm(of_chunk(kd, c), ds)
            ds = of_chunk(el, c) * ds + of_chunk(qet_do, c) - _bm_tn(of_chunk(w, c), d_vn_c[c])
        dstate_ref[...] = ds
        by_chunk = lambda xs: jnp.stack([xs[c][h] for h in range(GDN_HEADS) for c in range(ng)])
        d_vn, ds = by_chunk(d_vn_c), by_chunk(ds_c)

        on_s = _bm_nt(jnp.concatenate([d_out, d_vn], axis=1), s)
        d_qe, d_w = on_s[:, :CHUNK], -on_s[:, CHUNK:]
        d_p = jnp.where(incl, _bm_nt(d_out, vn), 0.0)
        d_kd = _bm_nt(vn, ds)
        d_both = _bm_tn(tm, jnp.concatenate([d_vn, d_w], axis=2))
        d_vb, d_kbe = d_both[:, :, :dk], d_both[:, :, dk:]
        d_a = -jnp.where(strict, _bm_nt(d_both, uw), 0.0)
        m = d_a * dm
        n = d_p * dm
        on_k = _bm(jnp.concatenate([m, n], axis=1), k)
        d_kb = on_k[:, :CHUNK] + d_kbe * e
        d_q = on_k[:, CHUNK:] + d_qe * e
        d_k = (_bm_tn(jnp.concatenate([m, n], axis=1), jnp.concatenate([kb, q], axis=1))
               + d_kd * eg + b * d_kb)
        d_v = b * d_vb
        r = d_a * a + d_p * p
        kd_term = rsum(d_kd * kd)
        d_gl = total(ds * s) * el + jnp.sum(kd_term, axis=1, keepdims=True)
        d_gam = (rsum(r) - _bm_tn(r, ones)[:, :, 0:1] + rsum(d_qe * qe) + rsum(d_kbe * kbe) - kd_term
                 + jnp.where(last_row, d_gl, 0.0))
        d_beta = rsum(d_kb * k) + rsum(d_vb * v)
        d_g = _bm(upper, d_gam * ones)[:, :, 0:1]
        per_head = lambda x: x.reshape(GDN_HEADS, ng * CHUNK, x.shape[-1])
        d_q, d_k, d_v, d_beta, d_g = (per_head(x) for x in (d_q, d_k, d_v, d_beta, d_g))
        gates = jnp.zeros((ng * CHUNK, LANES), F32)
        for h in range(GDN_HEADS):
            lanes = slice(h * dk, (h + 1) * dk)
            d_ref[0, rows, lanes] = d_q[h]
            d_ref[1, rows, lanes] = d_k[h]
            d_ref[2, rows, lanes] = d_v[h]
            gates = gates + (jnp.where(lane_ids == h, d_beta[h], 0.0)
                             + jnp.where(lane_ids == GDN_HEADS + h, d_g[h], 0.0))
        dgate_ref[rows, :] = gates

    d_spec = pl.BlockSpec((3, cps * CHUNK, width), lambda g: (0, steps - 1 - g, 0))
    return pl.pallas_call(
        body, name="gdn_bwd",
        grid=(steps,),
        in_specs=[rows_blk(width, 0), rows_blk(width, 1), rows_blk(width, 2), rows_blk(LANES), rows_blk(LANES), gate_r,
                  per_chunk(dk, dk), per_chunk(CHUNK, CHUNK), per_chunk(CHUNK, CHUNK), per_chunk(CHUNK, CHUNK),
                  rows_blk(2 * width), rows_blk(width), rows_blk(width)] + [_HBM] * nx,
        out_specs=[d_spec, rows_blk(LANES)] + [_HBM] * nx,
        out_shape=[jax.ShapeDtypeStruct((3, t_len, width), F32),
                   jax.ShapeDtypeStruct((t_len, LANES), F32)] + _direct_out_shapes(scatter, (True,) * nx),
        scratch_shapes=[pltpu.VMEM((GDN_HEADS, dk, dk), F32)] + (_direct_semaphores(nx) if nx else []),
        compiler_params=_params("arbitrary"),
    )(gact, gact, gact, beta_c, gam_c, gam_r, *saved, d_o, *scatter)


def _group_matrix(width, group):
    r = lax.broadcasted_iota(jnp.int32, (width, width), 0)
    c = lax.broadcasted_iota(jnp.int32, (width, width), 1)
    return ((r // group) == (c // group)).astype(F32)


def _post_call(o_sb, o_gd, proj, x, target, w_out, sbw, gdw, fw, tm=256):
    t_len, d = x.shape
    half = 512
    zsb_blk = 1536 // half
    zgd_blk = 3584 // half

    def body(osb_ref, ogd_ref, zsb_ref, zgd_ref, x_ref, tg_ref, wo_ref, sbw_ref, gdw_ref, fw_ref,
             dx2_ref, dosb_ref, dogd_ref, dz_ref, loss_ref, gfw_ref, gsb_ref, ggd_ref, gwo_ref):
        step = pl.program_id(0)

        @pl.when(step == 0)
        def _():
            loss_ref[...] = jnp.zeros_like(loss_ref)
            gfw_ref[...] = jnp.zeros_like(gfw_ref)
            gsb_ref[...] = jnp.zeros_like(gsb_ref)
            ggd_ref[...] = jnp.zeros_like(ggd_ref)
            gwo_ref[...] = jnp.zeros_like(gwo_ref)

        def head_forward(o, z, w, gmat, inv):
            r = lax.rsqrt(_running_sum_mm(o * o, gmat) * inv + EPS)
            nrm = o * r * w
            sg = _sigmoid(z)
            return r, nrm, sg, nrm * (z * sg)

        def head_backward(d_m, o, z, w, gmat, inv, r, nrm, sg):
            d_n = d_m * (z * sg)
            d_z = d_m * nrm * (sg * (1.0 + z * (1.0 - sg)))
            dnw = d_n * w
            d_o = r * dnw - o * (r * r * r) * (_running_sum_mm(dnw * o, gmat) * inv)
            return d_o, d_z, jnp.sum(d_n * o * r, axis=0, keepdims=True)

        g_sb = _group_matrix(half, SB_HEAD_DIM).astype(MXU_DTYPE)
        g_gd = _group_matrix(half, GDN_HEAD_DIM).astype(MXU_DTYPE)
        osb, ogd, zsb, zgd = osb_ref[...], ogd_ref[...], zsb_ref[...], zgd_ref[...]
        sbw_v, gdw_v = sbw_ref[...], gdw_ref[...]
        r_sb, n_sb, sg_sb, m_sb = head_forward(osb, zsb, sbw_v, g_sb, 1.0 / SB_HEAD_DIM)
        r_gd, n_gd, sg_gd, m_gd = head_forward(ogd, zgd, gdw_v, g_gd, 1.0 / GDN_HEAD_DIM)
        mixed = jnp.concatenate([m_sb, m_gd], axis=1).astype(MXU_DTYPE)
        wo = wo_ref[...]
        x2 = x_ref[...] + jnp.dot(mixed, wo, preferred_element_type=F32)
        r2 = lax.rsqrt(jnp.mean(x2 * x2, axis=-1, keepdims=True) + EPS)
        fw_v = fw_ref[...]
        err = x2 * r2 * fw_v - tg_ref[...]
        loss_ref[...] += 0.5 * jnp.sum(jnp.sum(err * err, axis=-1, keepdims=True) * (1.0 / d))
        dy = err * (1.0 / d)
        gg = dy * fw_v
        dx2 = r2 * gg - x2 * ((r2 * r2 * r2) * jnp.mean(gg * x2, axis=-1, keepdims=True))
        gfw_ref[...] += jnp.sum(dy * x2 * r2, axis=0, keepdims=True)
        dx2_ref[...] = dx2
        dx2b = dx2.astype(MXU_DTYPE)
        d_mixed = lax.dot_general(dx2b, wo, _NT, preferred_element_type=F32)
        gwo_ref[...] += lax.dot_general(mixed, dx2b, _TN, preferred_element_type=F32)
        d_osb, d_zsb, gsb = head_backward(d_mixed[:, :half], osb, zsb, sbw_v, g_sb, 1.0 / SB_HEAD_DIM, r_sb, n_sb, sg_sb)
        d_ogd, d_zgd, ggd = head_backward(d_mixed[:, half:], ogd, zgd, gdw_v, g_gd, 1.0 / GDN_HEAD_DIM, r_gd, n_gd, sg_gd)
        dosb_ref[...] = d_osb
        dogd_ref[...] = d_ogd
        dz_ref[0] = d_zsb
        dz_ref[1] = d_zgd
        gsb_ref[...] += gsb
        ggd_ref[...] += ggd

    row_blk = lambda w: pl.BlockSpec((tm, w), lambda i: (i, 0))
    fixed = lambda r, w: pl.BlockSpec((r, w), lambda i: (0, 0))
    return pl.pallas_call(
        body, name="post",
        grid=(t_len // tm,),
        in_specs=[row_blk(half), row_blk(half),
                  pl.BlockSpec((tm, half), lambda i: (i, zsb_blk)),
                  pl.BlockSpec((tm, half), lambda i: (i, zgd_blk)),
                  row_blk(d), row_blk(d), fixed(d, d), fixed(1, half), fixed(1, half), fixed(1, d)],
        out_specs=[row_blk(d), row_blk(half), row_blk(half),
                   pl.BlockSpec((2, tm, half), lambda i: (DPROJ_GATE_SLOT // 2, i, 0)),
                   fixed(1, LANES), fixed(1, d), fixed(1, half), fixed(1, half), fixed(d, d)],
        out_shape=[jax.ShapeDtypeStruct((t_len, d), F32)] + [jax.ShapeDtypeStruct((t_len, half), F32)] * 2
                  + [jax.ShapeDtypeStruct((len(DPROJ_PIECE_OF_SLOT), t_len, half), F32),
                     jax.ShapeDtypeStruct((1, LANES), F32), jax.ShapeDtypeStruct((1, d), F32),
                     jax.ShapeDtypeStruct((1, half), F32), jax.ShapeDtypeStruct((1, half), F32),
                     jax.ShapeDtypeStruct((d, d), F32)],
        compiler_params=_params("arbitrary"),
    )(o_sb, o_gd, proj, proj, x, target, w_out, sbw, gdw, fw)


def _piece_of_slot(s):
    return jnp.where(s < DPROJ_GDN_SLOT, s, jnp.where(s < DPROJ_GATE_SLOT, s + 1,
                                                     jnp.where(s == DPROJ_GATE_SLOT, 3, 7)))


def _gw_in_call(h_t, dproj8):
    d, t_len = h_t.shape
    n_piece, _, pw = dproj8.shape

    def body(ht_ref, dp_ref, gw_ref):
        gw_ref[...] = jnp.dot(ht_ref[...], dp_ref[0].astype(MXU_DTYPE), preferred_element_type=F32)

    return pl.pallas_call(
        body, name="gw_in",
        grid=(n_piece,),
        in_specs=[pl.BlockSpec((d, t_len), lambda s: (0, 0)),
                  pl.BlockSpec((1, t_len, pw), lambda s: (s, 0, 0))],
        out_specs=pl.BlockSpec((d, pw), lambda s: (0, _piece_of_slot(s))),
        out_shape=jax.ShapeDtypeStruct((d, n_piece * pw), F32),
        compiler_params=_params("arbitrary"),
    )(h_t, dproj8)


def _slot_of_piece(p):
    return jnp.where(p < DPROJ_GDN_SLOT, p, jnp.where(p == 3, DPROJ_GATE_SLOT, jnp.where(p < 7, p - 1, 7)))


def _gw_in_shards_call(h_t, dproj8, dsmall, out_dtype):
    d, t_len = h_t.shape
    n_piece, _, pw = dproj8.shape
    ns = dsmall.shape[1]
    n_pairs = N_DEV // 2

    def body(ht_ref, dp_ref, ds_ref, chip_ref, prev_ref, gates_ref, send_ref, recv_ref, send_sems, recv_sems):
        p = pl.program_id(0)
        x_pos, y_pos, c = lax.axis_index("x"), lax.axis_index("y"), lax.axis_index("c")
        to_sibling = lambda pair: pltpu.make_async_remote_copy(
            src_ref=send_ref.at[pair], dst_ref=recv_ref.at[pair], send_sem=send_sems.at[pair],
            recv_sem=recv_sems.at[pair], device_id=(x_pos, y_pos, 1 - c), device_id_type=_MESH)

        @pl.when(p == 0)
        def _():
            gates_ref[...] = jnp.dot(ht_ref[...], ds_ref[...].astype(MXU_DTYPE), preferred_element_type=F32)

        def emit(s, tail):
            x = jnp.concatenate([prev_ref[...], tail], axis=1)
            y = x if s == 0 else pltpu.roll(x, SHARD_PAD - s, axis=1)
            shard = y[:, :SHARD_COLS].astype(out_dtype)

            @pl.when(c == s % 2)
            def _():
                chip_ref[s // 2] = shard

            @pl.when(c != s % 2)
            def _():
                send_ref[s // 2] = shard
                to_sibling(s // 2).start()

        @pl.when(p < n_piece)
        def _():
            cur = jnp.dot(ht_ref[...], dp_ref[0].astype(MXU_DTYPE), preferred_element_type=F32)
            for s in range(n_piece - 1):
                pl.when(p == s + 1)(functools.partial(emit, s, cur[:, :SHARD_PAD - pw]))
            prev_ref[...] = cur

        @pl.when(p == n_piece)
        def _():
            emit(n_piece - 1, gates_ref[...])
            for pair in range(n_pairs):
                to_sibling(pair).wait_send()
            for pair in range(n_pairs):
                to_sibling(pair).wait_recv()
                chip_ref[pair] = (chip_ref[pair].astype(F32) + recv_ref[pair].astype(F32)).astype(out_dtype)

    shards_of_side = lambda: pltpu.VMEM((n_pairs, d, SHARD_COLS), out_dtype)
    return pl.pallas_call(
        body, name="gw_in",
        grid=(n_piece + 1,),
        in_specs=[pl.BlockSpec((d, t_len), lambda p: (0, 0)),
                  pl.BlockSpec((1, t_len, pw), lambda p: (_slot_of_piece(jnp.minimum(p, n_piece - 1)), 0, 0)),
                  pl.BlockSpec((t_len, ns), lambda p: (0, 0))],
        out_specs=pl.BlockSpec((n_pairs, d, SHARD_COLS), lambda p: (0, 0, 0)),
        out_shape=jax.ShapeDtypeStruct((n_pairs, d, SHARD_COLS), out_dtype),
        scratch_shapes=[pltpu.VMEM((d, pw), F32), pltpu.VMEM((d, ns), F32), shards_of_side(), shards_of_side(),
                        pltpu.SemaphoreType.DMA((n_pairs,)), pltpu.SemaphoreType.DMA((n_pairs,))],
        compiler_params=_params("arbitrary"),
    )(h_t, dproj8, dsmall)


def _gw_small_call(h_t, dsmall, tm=512):
    d, t_len = h_t.shape
    ns = dsmall.shape[1]

    def body(ht_ref, dp_ref, gw_ref):
        @pl.when(pl.program_id(0) == 0)
        def _():
            gw_ref[...] = jnp.zeros_like(gw_ref)

        gw_ref[...] += jnp.dot(ht_ref[...], dp_ref[...].astype(MXU_DTYPE), preferred_element_type=F32)

    return pl.pallas_call(
        body, name="gw_small",
        grid=(t_len // tm,),
        in_specs=[pl.BlockSpec((d, tm), lambda t: (0, t)),
                  pl.BlockSpec((tm, ns), lambda t: (t, 0))],
        out_specs=pl.BlockSpec((d, ns), lambda t: (0, 0)),
        out_shape=jax.ShapeDtypeStruct((d, ns), F32),
        compiler_params=_params("arbitrary"),
    )(h_t, dsmall)


def _dx_call(dproj8, dsmall, w_main, w_small, x, r, dx2, norm_w, chip_scatter=(), tm=256):
    t_len, d = x.shape
    n_piece, _, pw = dproj8.shape
    ns = dsmall.shape[1]
    nx = len(chip_scatter)
    steps = t_len // tm

    def body(*refs):
        dp_ref, ds_ref, wm_ref, ws_ref, x_ref, r_ref, dx2_ref, nw_ref = refs[:8]
        gx_ref, gnw_ref = refs[8 + nx:10 + nx]
        copies = lambda: _chip_copies(refs[8:8 + nx], refs[10 + nx:10 + 2 * nx], *refs[10 + 2 * nx:])
        if nx:
            pl.when(pl.program_id(0) == 0)(lambda: _start_all(copies()))

        @pl.when(pl.program_id(0) == 0)
        def _():
            gnw_ref[...] = jnp.zeros_like(gnw_ref)

        dh = lax.dot_general(ds_ref[...].astype(MXU_DTYPE), ws_ref[...], _NT, preferred_element_type=F32)
        for s, p in enumerate(DPROJ_PIECE_OF_SLOT):
            dh = dh + lax.dot_general(dp_ref[s].astype(MXU_DTYPE), wm_ref[:, p * pw:(p + 1) * pw], _NT,
                                      preferred_element_type=F32)
        xv, rv = x_ref[...], r_ref[...]
        dn = dh * nw_ref[...]
        gx_ref[...] = dx2_ref[...] + rv * dn - xv * ((rv * rv * rv) * jnp.mean(dn * xv, axis=-1, keepdims=True))
        gnw_ref[...] += jnp.sum(dh * xv * rv, axis=0, keepdims=True)
        if nx:
            pl.when(pl.program_id(0) == steps - 1)(lambda: _wait_all(copies()))

    return pl.pallas_call(
        body, name="dx",
        grid=(steps,),
        in_specs=[pl.BlockSpec((n_piece, tm, pw), lambda i: (0, i, 0)),
                  pl.BlockSpec((tm, ns), lambda i: (i, 0)),
                  pl.BlockSpec((d, n_piece * pw), lambda i: (0, 0)),
                  pl.BlockSpec((d, ns), lambda i: (0, 0)),
                  pl.BlockSpec((tm, d), lambda i: (i, 0)),
                  pl.BlockSpec((tm, 1), lambda i: (i, 0)),
                  pl.BlockSpec((tm, d), lambda i: (i, 0)),
                  pl.BlockSpec((1, d), lambda i: (0, 0))] + [_HBM] * nx,
        out_specs=[pl.BlockSpec((tm, d), lambda i: (i, 0)),
                   pl.BlockSpec((1, d), lambda i: (0, 0))] + [_HBM] * nx,
        out_shape=[jax.ShapeDtypeStruct((t_len, d), F32), jax.ShapeDtypeStruct((1, d), F32)]
                  + [jax.ShapeDtypeStruct(a.shape, a.dtype) for a in chip_scatter],
        scratch_shapes=_chip_semaphores(nx) if nx else [],
        compiler_params=_params("arbitrary"),
    )(dproj8, dsmall, w_main, w_small, x, r, dx2, norm_w, *chip_scatter)


def _exchange_call(name, srcs, per_peer):
    n = len(srcs)

    def body(*refs):
        src_refs, out_refs = refs[:n], refs[n:2 * n]
        copies = _direct_copies(src_refs, out_refs, *refs[2 * n:], per_peer)
        _start_all(copies)
        _wait_all(copies)

    hbm = pl.BlockSpec(memory_space=pl.ANY)
    return pl.pallas_call(
        body, name=name,
        in_specs=[hbm] * n, out_specs=[hbm] * n, out_shape=_direct_out_shapes(srcs, per_peer),
        scratch_shapes=_direct_semaphores(n),
    )(*srcs)


def _direct_out_shapes(srcs, per_peer):
    return [jax.ShapeDtypeStruct(s.shape if pp else (N_DEV,) + s.shape, s.dtype) for s, pp in zip(srcs, per_peer)]


def _direct_semaphores(n):
    return [pltpu.SemaphoreType.DMA((n * (N_DEV - 1),)), pltpu.SemaphoreType.DMA((n * (N_DEV - 1),)),
            pltpu.SemaphoreType.DMA((n,))]


def _direct_copies(src_refs, out_refs, send_sems, recv_sems, local_sems, per_peer):
    x, y, c = lax.axis_index("x"), lax.axis_index("y"), lax.axis_index("c")
    me = 4 * x + 2 * y + c
    local, remote = [], []
    for a in range(len(src_refs)):
        mine = src_refs[a].at[me] if per_peer[a] else src_refs[a]
        local.append(pltpu.make_async_copy(mine, out_refs[a].at[me], local_sems.at[a]))
    for k in range(1, N_DEV):
        kx, ky, kc = (k >> 2) & 1, (k >> 1) & 1, k & 1
        px = 1 - x if kx else x
        py = 1 - y if ky else y
        pc = 1 - c if kc else c
        peer = 4 * px + 2 * py + pc
        for a in range(len(src_refs)):
            sem = a * (N_DEV - 1) + (k - 1)
            remote.append(pltpu.make_async_remote_copy(
                src_ref=src_refs[a].at[peer] if per_peer[a] else src_refs[a], dst_ref=out_refs[a].at[me],
                send_sem=send_sems.at[sem], recv_sem=recv_sems.at[sem],
                device_id=(px, py, pc), device_id_type=pl.DeviceIdType.MESH))
    return local, remote


def _start_all(copies):
    local, remote = copies
    for cp in local + remote:
        cp.start()


def _wait_all(copies):
    local, remote = copies
    for cp in remote:
        cp.wait_send()
    for cp in remote:
        cp.wait_recv()
    for cp in local:
        cp.wait()


N_CHIPS = 4
_HBM = pl.BlockSpec(memory_space=pl.ANY)
_MESH = pl.DeviceIdType.MESH


def _gather_call(name, srcs):
    n = len(srcs)
    per = N_DEV - 1

    def body(*refs):
        src_refs, out_refs = refs[:n], refs[n:2 * n]
        send_sems, recv_sems, local_sems = refs[2 * n:]
        x, y, c = lax.axis_index("x"), lax.axis_index("y"), lax.axis_index("c")
        me, sibling = (x, y, c), (x, y, 1 - c)
        x_nbr, y_nbr, diagonal = (1 - x, y), (x, 1 - y), (1 - x, 1 - y)
        held = ((1 - x) * c + x * (1 - c), y * c + (1 - y) * (1 - c))
        onward = (x * c + (1 - x) * (1 - c), (1 - y) * c + y * (1 - c))
        slot = lambda px, py, pc: 4 * px + 2 * py + pc

        def copy(a, k, block, to, from_src=False):
            rows = out_refs[a].at[slot(*block)]
            return pltpu.make_async_remote_copy(
                src_ref=src_refs[a] if from_src else rows, dst_ref=rows,
                send_sem=send_sems.at[a * per + k], recv_sem=recv_sems.at[a * per + k],
                device_id=to, device_id_type=_MESH)

        local = [pltpu.make_async_copy(src_refs[a], out_refs[a].at[slot(*me)], local_sems.at[a]) for a in range(n)]
        started = []

        def start(cp):
            cp.start()
            started.append(cp)

        for cp in local:
            cp.start()
        for a in range(n):
            start(copy(a, 0, me, sibling, True))
            start(copy(a, 1, me, (*x_nbr, c), True))
            start(copy(a, 2, me, (*y_nbr, c), True))
        for a in range(n):
            copy(a, 1, (*x_nbr, c), me).wait_recv()
            copy(a, 2, (*y_nbr, c), me).wait_recv()
            start(copy(a, 3, (*held, c), (*onward, c)))
            start(copy(a, 4, (*x_nbr, c), sibling))
            start(copy(a, 5, (*y_nbr, c), sibling))
        for a in range(n):
            copy(a, 3, (*diagonal, c), me).wait_recv()
            start(copy(a, 6, (*diagonal, c), sibling))
        for a in range(n):
            copy(a, 0, sibling, me).wait_recv()
            for k, chip in ((4, x_nbr), (5, y_nbr), (6, diagonal)):
                copy(a, k, (*chip, 1 - c), me).wait_recv()
        for cp in started:
            cp.wait_send()
        for cp in local:
            cp.wait()

    return pl.pallas_call(
        body, name=name,
        in_specs=[_HBM] * n, out_specs=[_HBM] * n,
        out_shape=[jax.ShapeDtypeStruct((N_DEV,) + s.shape, s.dtype) for s in srcs],
        scratch_shapes=[pltpu.SemaphoreType.DMA((n * per,)), pltpu.SemaphoreType.DMA((n * per,)),
                        pltpu.SemaphoreType.DMA((n,))],
    )(*srcs)


def _sibling_send_call(name, srcs):
    n = len(srcs)

    def body(*refs):
        src_refs, out_refs = refs[:n], refs[n:2 * n]
        send_sems, recv_sems = refs[2 * n:]
        x, y, c = lax.axis_index("x"), lax.axis_index("y"), lax.axis_index("c")
        copies = []
        for a in range(n):
            for ch in range(N_CHIPS):
                copies.append(pltpu.make_async_remote_copy(
                    src_ref=src_refs[a].at[2 * ch + (1 - c)], dst_ref=out_refs[a].at[ch],
                    send_sem=send_sems.at[a * N_CHIPS + ch], recv_sem=recv_sems.at[a * N_CHIPS + ch],
                    device_id=(x, y, 1 - c), device_id_type=_MESH))
        for cp in copies:
            cp.start()
        for cp in copies:
            cp.wait_send()
        for cp in copies:
            cp.wait_recv()

    return pl.pallas_call(
        body, name=name,
        in_specs=[_HBM] * n, out_specs=[_HBM] * n,
        out_shape=[jax.ShapeDtypeStruct((N_CHIPS,) + s.shape[1:], s.dtype) for s in srcs],
        scratch_shapes=[pltpu.SemaphoreType.DMA((n * N_CHIPS,)), pltpu.SemaphoreType.DMA((n * N_CHIPS,))],
    )(*srcs)


def _pair_sum_call(name, parts, from_sibling, tr):
    _, rows, cols = parts.shape

    def body(p_ref, s_ref, o_ref):
        o_ref[...] = (p_ref[...].astype(F32) + s_ref[...].astype(F32)).astype(o_ref.dtype)

    return pl.pallas_call(
        body, name=name,
        grid=(N_CHIPS, rows // tr),
        in_specs=[pl.BlockSpec((1, tr, cols), lambda ch, i: (2 * ch + lax.axis_index("c"), i, 0)),
                  pl.BlockSpec((1, tr, cols), lambda ch, i: (ch, i, 0))],
        out_specs=pl.BlockSpec((1, tr, cols), lambda ch, i: (ch, i, 0)),
        out_shape=jax.ShapeDtypeStruct((N_CHIPS, rows, cols), WIRE_DTYPE),
        compiler_params=_params("arbitrary", "arbitrary"),
    )(parts, from_sibling)


def _chip_exchange_call(name, srcs):
    n = len(srcs)

    def body(*refs):
        copies = _chip_copies(refs[:n], refs[n:2 * n], *refs[2 * n:])
        _start_all(copies)
        _wait_all(copies)

    return pl.pallas_call(
        body, name=name,
        in_specs=[_HBM] * n, out_specs=[_HBM] * n,
        out_shape=[jax.ShapeDtypeStruct(s.shape, s.dtype) for s in srcs],
        scratch_shapes=_chip_semaphores(n),
    )(*srcs)


def _chip_semaphores(n):
    per = N_CHIPS - 1
    return [pltpu.SemaphoreType.DMA((n * per,)), pltpu.SemaphoreType.DMA((n * per,)), pltpu.SemaphoreType.DMA((n,))]


def _chip_copies(src_refs, out_refs, send_sems, recv_sems, local_sems):
    per = N_CHIPS - 1
    x, y, c = lax.axis_index("x"), lax.axis_index("y"), lax.axis_index("c")
    mine = 2 * x + y
    chips = [(1 - x, y), (x, 1 - y), (1 - x, 1 - y)]
    n = len(src_refs)
    local = [pltpu.make_async_copy(src_refs[a].at[mine], out_refs[a].at[mine], local_sems.at[a]) for a in range(n)]
    remote = []
    for a in range(n):
        for j, (px, py) in enumerate(chips):
            remote.append(pltpu.make_async_remote_copy(
                src_ref=src_refs[a].at[2 * px + py], dst_ref=out_refs[a].at[mine],
                send_sem=send_sems.at[a * per + j], recv_sem=recv_sems.at[a * per + j],
                device_id=(px, py, c), device_id_type=_MESH))
    return local, remote


def _adam_call(name, parts, w, m, v, tr):
    rows, cols = w.shape
    n_slots = parts.shape[0]

    def body(p_ref, w_ref, m_ref, v_ref, g_ref, d_ref, nm_ref, nv_ref):
        g = p_ref[0].astype(F32)
        for s in range(1, n_slots):
            g = g + p_ref[s].astype(F32)
        m_new = ADAM_B1 * m_ref[...] + (1.0 - ADAM_B1) * g
        v_new = ADAM_B2 * v_ref[...] + (1.0 - ADAM_B2) * (g * g)
        m_hat = m_new / (1.0 - ADAM_B1 ** ADAM_STEP)
        v_hat = v_new / (1.0 - ADAM_B2 ** ADAM_STEP)
        g_ref[...] = g
        d_ref[...] = -ADAM_LR * (m_hat / (jnp.sqrt(v_hat) + ADAM_EPS) + ADAM_WD * w_ref[...])
        nm_ref[...] = m_new
        nv_ref[...] = v_new

    blk = pl.BlockSpec((tr, cols), lambda i: (i, 0))
    return pl.pallas_call(
        body, name=name,
        grid=(rows // tr,),
        in_specs=[pl.BlockSpec((n_slots, tr, cols), lambda i: (0, i, 0)), blk, blk, blk],
        out_specs=[blk] * 4,
        out_shape=[jax.ShapeDtypeStruct((rows, cols), F32)] * 4,
        compiler_params=_params("arbitrary"),
    )(parts, w, m, v)


N_PIECES = 8
PIECE = 512
SHARD_COLS = 513
SHARD_PAD = 640
RELAYOUT_ROWS = 256


def _from_shards_call(shards):
    _, d, _ = shards.shape
    tr = RELAYOUT_ROWS

    def body(p_ref, m_ref, s_ref):
        lane = lax.broadcasted_iota(jnp.int32, (tr, SHARD_PAD), 1)
        pad = jnp.zeros((tr, SHARD_PAD - SHARD_COLS), F32)
        sh = [jnp.concatenate([p_ref[s].astype(F32), pad], axis=1) for s in range(N_DEV)]
        for p in range(N_PIECES):
            y = sh[p] if p == 0 else pltpu.roll(sh[p], p, axis=1)
            if p > 0:
                y = jnp.where(lane < p, pltpu.roll(sh[p - 1], SHARD_PAD - (SHARD_COLS - p), axis=1), y)
            m_ref[:, p * PIECE:(p + 1) * PIECE] = y[:, :PIECE].astype(m_ref.dtype)
        first_gate = N_PIECES * PIECE - (N_DEV - 1) * SHARD_COLS
        s_ref[...] = pltpu.roll(sh[N_DEV - 1], SHARD_PAD - first_gate, axis=1)[:, :LANES].astype(s_ref.dtype)

    return pl.pallas_call(
        body, name="w_in_from_shards",
        grid=(d // tr,),
        in_specs=[pl.BlockSpec((N_DEV, tr, SHARD_COLS), lambda i: (0, i, 0))],
        out_specs=[pl.BlockSpec((tr, N_PIECES * PIECE), lambda i: (i, 0)), pl.BlockSpec((tr, LANES), lambda i: (i, 0))],
        out_shape=[jax.ShapeDtypeStruct((d, N_PIECES * PIECE), shards.dtype),
                   jax.ShapeDtypeStruct((d, LANES), shards.dtype)],
        compiler_params=_params("arbitrary"),
    )(shards)


def _to_shards_call(main, gates, out_dtype):
    d = main.shape[0]
    tr = RELAYOUT_ROWS

    def body(m_ref, s_ref, o_ref):
        for s in range(N_DEV):
            if s < N_DEV - 1:
                x = m_ref[:, s * PIECE:s * PIECE + SHARD_PAD]
            else:
                x = jnp.concatenate([m_ref[:, s * PIECE:(s + 1) * PIECE], s_ref[...]], axis=1)
            y = x if s == 0 else pltpu.roll(x, SHARD_PAD - s, axis=1)
            o_ref[s] = y[:, :SHARD_COLS].astype(out_dtype)

    return pl.pallas_call(
        body, name="w_in_to_shards",
        grid=(d // tr,),
        in_specs=[pl.BlockSpec((tr, N_PIECES * PIECE), lambda i: (i, 0)), pl.BlockSpec((tr, LANES), lambda i: (i, 0))],
        out_specs=pl.BlockSpec((N_DEV, tr, SHARD_COLS), lambda i: (0, i, 0)),
        out_shape=jax.ShapeDtypeStruct((N_DEV, d, SHARD_COLS), out_dtype),
        compiler_params=_params("arbitrary"),
    )(main, gates)


def _adamw(g, w, m, v):
    m_new = ADAM_B1 * m + (1.0 - ADAM_B1) * g
    v_new = ADAM_B2 * v + (1.0 - ADAM_B2) * (g * g)
    m_hat = m_new / (1.0 - ADAM_B1 ** ADAM_STEP)
    v_hat = v_new / (1.0 - ADAM_B2 ** ADAM_STEP)
    return -ADAM_LR * (m_hat / (jnp.sqrt(v_hat) + ADAM_EPS) + ADAM_WD * w), m_new, v_new


def _adam_small_call(parts, ws, ms, vs):
    n = len(ws)
    n_slots = parts.shape[0]

    def body(*refs):
        p_ref = refs[0]
        w_refs, m_refs, v_refs = refs[1:1 + n], refs[1 + n:1 + 2 * n], refs[1 + 2 * n:1 + 3 * n]
        loss_ref = refs[1 + 3 * n]
        outs = refs[2 + 3 * n:]
        g_all = p_ref[0]
        for s in range(1, n_slots):
            g_all = g_all + p_ref[s]
        loss_ref[...] = g_all[n:n + 1, 0:1]
        for r in range(n):
            size = w_refs[r].shape[1]
            g = g_all[r:r + 1, :size]
            delta, m_new, v_new = _adamw(g, w_refs[r][...], m_refs[r][...], v_refs[r][...])
            for kind, val in enumerate((g, delta, m_new, v_new)):
                outs[kind * n + r][...] = val

    vm = pl.BlockSpec(memory_space=pltpu.VMEM)
    shapes = [jax.ShapeDtypeStruct(w.shape, F32) for w in ws]
    return pl.pallas_call(
        body, name="adam_small",
        in_specs=[vm] * (1 + 3 * n), out_specs=[vm] * (1 + 4 * n),
        out_shape=[jax.ShapeDtypeStruct((1, 1), F32)] + shapes * 4,
    )(parts, *ws, *ms, *vs)


_SMALL_ROWS = ("norm1_w", "final_norm_w", "sb_norm_w", "gdn_norm_w", "gdn_A_log", "gdn_dt_bias", "loss")


def _pack_small(vals, width):
    rows = [jnp.pad(a.reshape(1, -1).astype(F32), ((0, 0), (0, width - a.size))) for a in vals]
    rows += [jnp.zeros((1, width), F32)] * (8 - len(rows))
    return jnp.concatenate(rows, axis=0)


def _device_step(x2d, tgt, w_main, w_small, w_out_full, conv_full, norm1_w, sb_norm_w, gdn_A_log, gdn_dt_bias,
                 gdn_norm_w, final_norm_w, distributed=False):
    t_len, d = x2d.shape
    n_chunks = t_len // CHUNK
    w_main, w_small, w_out_full = (a.astype(MXU_DTYPE) for a in (w_main, w_small, w_out_full))
    w_small_t = w_small[:, :2 * GDN_HEADS].T

    pad_lanes = lambda a, lo: jnp.pad(a.reshape(1, -1), ((0, 0), (lo, LANES - lo - a.size)))
    alog_l, dtb_l = pad_lanes(gdn_A_log, GDN_HEADS), pad_lanes(gdn_dt_bias, GDN_HEADS)
    alog_c, dtb_c = alog_l[:, :8].T, dtb_l[:, :8].T
    sbw = jnp.tile(sb_norm_w, (1, 512 // SB_HEAD_DIM))
    gdw = jnp.tile(gdn_norm_w, (1, 512 // GDN_HEAD_DIM))
    fw = final_norm_w.reshape(1, d)

    if distributed:
        proj, ps, pst, h_t, r1, w_out_g, conv_g = _inproj_call(
            x2d, norm1_w, w_main, w_small, w_small_t, gather=(w_out_full, conv_full))
        w_out_full = w_out_g.reshape(d, d)
        conv_full = conv_g.transpose(1, 0, 2).reshape(CONV_WIDTH, N_DEV * conv_g.shape[2])
    else:
        proj, ps, pst, h_t, r1 = _inproj_call(x2d, norm1_w, w_main, w_small, w_small_t)
    o_sb, sp_total, sb_blocks_run = _sb_fwd_call(proj, t_len)
    gact = _gdn_prep_call(proj, conv_full, t_len)
    beta_l, gcol_l, grow = _gdn_gates_call(ps, pst, alog_l, dtb_l, alog_c, dtb_c, t_len)
    gam_r = grow[GDN_HEADS:2 * GDN_HEADS].reshape(GDN_HEADS, n_chunks, 1, CHUNK)
    o_gd, *gdn_saved = _gdn_fwd_call(gact, beta_l, gcol_l, gam_r, t_len)

    (dx2, d_osb, d_ogd, dproj8, loss_p, g_fw, g_sbw, g_gdw, g_wout) = _post_call(
        o_sb, o_gd, proj, x2d, tgt, w_out_full, sbw, gdw, fw)

    dproj8 = _sb_bwd_call(proj, sp_total, sb_blocks_run, d_osb, dproj8, t_len)
    if distributed:
        d_gact3, d_gates, g_wout = _gdn_bwd_call(gact, beta_l, gcol_l, gam_r, gdn_saved, d_ogd, t_len,
                                                 scatter=(g_wout.reshape(N_DEV, d // N_DEV, d),))
    else:
        d_gact3, d_gates = _gdn_bwd_call(gact, beta_l, gcol_l, gam_r, gdn_saved, d_ogd, t_len)
    dproj8, g_conv = _gdn_prep_bwd_call(proj, conv_full, d_gact3, dproj8, t_len)
    dsmall, g_alog, g_dtb = _gdn_gates_bwd_call(ps, alog_l, dtb_l, d_gates, t_len)

    if distributed:
        chip_partials = _gw_in_shards_call(h_t, dproj8, dsmall, WIRE_DTYPE)
        grad_x, g_n1, g_w_in = _dx_call(dproj8, dsmall, w_main, w_small, x2d, r1, dx2, norm1_w,
                                        chip_scatter=(chip_partials,))
    else:
        grad_x, g_n1 = _dx_call(dproj8, dsmall, w_main, w_small, x2d, r1, dx2, norm1_w)
        g_w_in = (_gw_in_call(h_t, dproj8), _gw_small_call(h_t, dsmall))
    return (loss_p, grad_x, g_n1, g_w_in, g_sbw, g_conv, g_alog, g_dtb, g_gdw, g_wout, g_fw)


def kernel(x, norm1_w, w_in, sb_norm_w, gdn_conv_w, gdn_A_log, gdn_dt_bias, gdn_norm_w, w_out, final_norm_w, loss_target, m_norm1_w, m_w_in, m_sb_norm_w, m_gdn_conv_w, m_gdn_A_log, m_gdn_dt_bias, m_gdn_norm_w, m_w_out, m_final_norm_w, v_norm1_w, v_w_in, v_sb_norm_w, v_gdn_conv_w, v_gdn_A_log, v_gdn_dt_bias, v_gdn_norm_w, v_w_out, v_final_norm_w):
    d = x.shape[2]
    shard_cols = w_in.shape[2]
    conv_cols = gdn_conv_w.shape[2]

    (w_in_g,) = _gather_call("gather_weights", [w_in[0].astype(WIRE_DTYPE)])
    w_main, w_small = _from_shards_call(w_in_g)

    (loss_p, grad_x, g_n1, p_w_in, g_sbw, g_conv, g_alog, g_dtb, g_gdw, p_wout, g_fw) = _device_step(
        x[0], loss_target[0], w_main, w_small, w_out[0].astype(WIRE_DTYPE), gdn_conv_w[0], norm1_w, sb_norm_w,
        gdn_A_log, gdn_dt_bias, gdn_norm_w, final_norm_w, distributed=True)

    g_conv_parts = g_conv.reshape(CONV_WIDTH, N_DEV, conv_cols).transpose(1, 0, 2)
    fold = lambda a, group: a.reshape(-1, group).sum(axis=0)
    small_g = _pack_small([g_n1, g_fw, fold(g_sbw, SB_HEAD_DIM), fold(g_gdw, GDN_HEAD_DIM),
                           g_alog[0, GDN_HEADS:2 * GDN_HEADS], g_dtb[0, GDN_HEADS:2 * GDN_HEADS],
                           loss_p[0, :1]], d)
    p_small, p_conv = _exchange_call("exchange_small", [small_g, g_conv_parts], [False, True])

    r_w_in = _adam_call("adam_w_in", p_w_in, w_in[0], m_w_in[0], v_w_in[0], 256)
    r_wout = _adam_call("adam_w_out", p_wout, w_out[0], m_w_out[0], v_w_out[0], d // N_DEV)
    r_conv = _adam_call("adam_conv", p_conv, gdn_conv_w[0], m_gdn_conv_w[0], v_gdn_conv_w[0], CONV_WIDTH)

    row = lambda a: a.reshape(1, -1)
    n_small = len(_SMALL_ROWS) - 1
    r_small = _adam_small_call(
        p_small,
        [norm1_w, row(final_norm_w), sb_norm_w, gdn_norm_w, gdn_A_log, gdn_dt_bias],
        [m_norm1_w, row(m_final_norm_w), m_sb_norm_w, m_gdn_norm_w, m_gdn_A_log, m_gdn_dt_bias],
        [v_norm1_w, row(v_final_norm_w), v_sb_norm_w, v_gdn_norm_w, v_gdn_A_log, v_gdn_dt_bias])

    def small_out(kind, name):
        out = r_small[1 + kind * n_small + _SMALL_ROWS.index(name)]
        return out.reshape(final_norm_w.shape) if name == "final_norm_w" else out

    def outputs(kind):
        return (small_out(kind, "norm1_w"), r_w_in[kind][None], small_out(kind, "sb_norm_w"), r_conv[kind][None],
                small_out(kind, "gdn_A_log"), small_out(kind, "gdn_dt_bias"), small_out(kind, "gdn_norm_w"),
                r_wout[kind][None], small_out(kind, "final_norm_w"))

    return (r_small[0][0, 0], grad_x[None], *outputs(0), *outputs(1), *outputs(2), *outputs(3))
```

```python
import functools

import jax
import jax.numpy as jnp
from jax import lax
from jax.experimental import pallas as pl
from jax.experimental.pallas import tpu as pltpu

F32 = jnp.float32
MXU_DTYPE = jnp.bfloat16
WIRE_DTYPE = jnp.bfloat16
EXACT = lax.Precision.HIGHEST
EPS = 1e-6
N_DEV = 8
SB_HEAD_DIM = 64
GDN_HEAD_DIM = 128
GDN_HEADS = 4
GDN_CHUNKS_PER_STEP = 4
GDN_BWD_GROUP = 1
CHUNK = 64
CONV_WIDTH = 4
LANES = 128
SB_BLOCK = 128
SB_BQ = 256
VMEM_LIMIT_BYTES = 56 * 1024 * 1024

DPROJ_PIECE_OF_SLOT = (0, 1, 2, 4, 5, 6, 3, 7)
DPROJ_SB_SLOT, DPROJ_GDN_SLOT, DPROJ_GATE_SLOT = 0, 3, 6

ADAM_LR = 0.001
ADAM_B1 = 0.9
ADAM_B2 = 0.999
ADAM_EPS = 1e-08
ADAM_WD = 0.01
ADAM_STEP = 10

_NN = (((1,), (0,)), ((), ()))
_NT = (((1,), (1,)), ((), ()))
_TN = (((0,), (0,)), ((), ()))
_BNN = (((2,), (1,)), ((0,), (0,)))
_BNT = (((2,), (2,)), ((0,), (0,)))
_BTN = (((1,), (1,)), ((0,), (0,)))


def _mm(a, b):
    return jnp.dot(a.astype(MXU_DTYPE), b.astype(MXU_DTYPE), preferred_element_type=F32)


def _mm_nt(a, b):
    return lax.dot_general(a.astype(MXU_DTYPE), b.astype(MXU_DTYPE), _NT, preferred_element_type=F32)


def _mm_tn(a, b):
    return lax.dot_general(a.astype(MXU_DTYPE), b.astype(MXU_DTYPE), _TN, preferred_element_type=F32)


def _mx(a, b):
    return jnp.dot(a, b, precision=EXACT, preferred_element_type=F32)


def _mx_nt(a, b):
    return lax.dot_general(a, b, _NT, precision=EXACT, preferred_element_type=F32)


def _mx_tn(a, b):
    return lax.dot_general(a, b, _TN, precision=EXACT, preferred_element_type=F32)


def _split(x):
    hi = x.astype(MXU_DTYPE)
    return hi, (x - hi.astype(F32)).astype(MXU_DTYPE)


def _m3_general(a, b, dims):
    ah, al = _split(a)
    bh, bl = _split(b)
    dot = lambda x, y: lax.dot_general(x, y, dims, preferred_element_type=F32)
    (contract, _), (batch, _) = dims
    free = [ax for ax in range(a.ndim) if ax not in contract and ax not in batch][0]
    m = a.shape[free]
    both = dot(jnp.concatenate([ah, al], axis=free), bh)
    out_axis = len(batch)
    hi_part = lax.slice_in_dim(both, 0, m, axis=out_axis)
    lo_part = lax.slice_in_dim(both, m, 2 * m, axis=out_axis)
    return hi_part + (dot(ah, bl) + lo_part)


def _times_exact(a, b_exact, dims):
    ah, al = _split(a)
    (contract, _), (batch, _) = dims
    free = [ax for ax in range(a.ndim) if ax not in contract and ax not in batch][0]
    m = a.shape[free]
    both = lax.dot_general(jnp.concatenate([ah, al], axis=free), b_exact.astype(MXU_DTYPE), dims,
                           preferred_element_type=F32)
    out_axis = len(batch)
    return lax.slice_in_dim(both, 0, m, axis=out_axis) + lax.slice_in_dim(both, m, 2 * m, axis=out_axis)


def _exact_times(a_exact, b, dims):
    bh, bl = _split(b)
    n = b.shape[-1]
    both = lax.dot_general(a_exact.astype(MXU_DTYPE), jnp.concatenate([bh, bl], axis=-1), dims,
                           preferred_element_type=F32)
    return both[..., :n] + both[..., n:]


def _m3(a, b):
    return _m3_general(a, b, _NN)


def _m3_nt(a, b):
    return _m3_general(a, b, _NT)


def _m3_tn(a, b):
    return _m3_general(a, b, _TN)


def _sigmoid(z):
    return 1.0 / (1.0 + jnp.exp(-z))


def _softplus(z):
    return jnp.maximum(z, 0.0) + jnp.log(1.0 + jnp.exp(-jnp.abs(z)))


def _params(*semantics):
    return pltpu.CompilerParams(dimension_semantics=semantics, vmem_limit_bytes=VMEM_LIMIT_BYTES)


def _inproj_call(x, norm_w, w_main, w_small, w_small_t, gather=(), tm=256):
    t_len, d = x.shape
    n = w_main.shape[1]
    ns = w_small.shape[1]
    nst = w_small_t.shape[0]
    ng = len(gather)
    steps = t_len // tm

    def body(*refs):
        x_ref, nw_ref, wm_ref, ws_ref, wst_ref = refs[:5]
        pm_ref, ps_ref, pst_ref, ht_ref, r_ref = refs[5 + ng:10 + ng]
        copies = lambda: _direct_copies(refs[5:5 + ng], refs[10 + ng:10 + 2 * ng], *refs[10 + 2 * ng:], (False,) * ng)
        if ng:
            pl.when(pl.program_id(0) == 0)(lambda: _start_all(copies()))
        xv = x_ref[...]
        r = lax.rsqrt(jnp.mean(xv * xv, axis=-1, keepdims=True) + EPS)
        h = xv * r * nw_ref[...]
        hb = h.astype(MXU_DTYPE)
        for n0 in range(0, n, 512):
            pm_ref[:, n0:n0 + 512] = jnp.dot(hb, wm_ref[:, n0:n0 + 512], preferred_element_type=F32)
        ps_ref[...] = jnp.dot(hb, ws_ref[...], preferred_element_type=F32)
        pst_ref[...] = lax.dot_general(wst_ref[...], hb, _NT, preferred_element_type=F32)
        ht_ref[...] = h.T.astype(MXU_DTYPE)
        r_ref[...] = r
        if ng:
            pl.when(pl.program_id(0) == steps - 1)(lambda: _wait_all(copies()))

    return pl.pallas_call(
        body, name="inproj",
        grid=(steps,),
        in_specs=[pl.BlockSpec((tm, d), lambda i: (i, 0)),
                  pl.BlockSpec((1, d), lambda i: (0, 0)),
                  pl.BlockSpec((d, n), lambda i: (0, 0)),
                  pl.BlockSpec((d, ns), lambda i: (0, 0)),
                  pl.BlockSpec((nst, d), lambda i: (0, 0))] + [_HBM] * ng,
        out_specs=[pl.BlockSpec((tm, n), lambda i: (i, 0)),
                   pl.BlockSpec((tm, ns), lambda i: (i, 0)),
                   pl.BlockSpec((nst, tm), lambda i: (0, i)),
                   pl.BlockSpec((d, tm), lambda i: (0, i)),
                   pl.BlockSpec((tm, 1), lambda i: (i, 0))] + [_HBM] * ng,
        out_shape=[jax.ShapeDtypeStruct((t_len, n), F32),
                   jax.ShapeDtypeStruct((t_len, ns), F32),
                   jax.ShapeDtypeStruct((nst, t_len), F32),
                   jax.ShapeDtypeStruct((d, t_len), MXU_DTYPE),
                   jax.ShapeDtypeStruct((t_len, 1), F32)] + _direct_out_shapes(gather, (False,) * ng),
        scratch_shapes=_direct_semaphores(ng) if ng else [],
        compiler_params=_params("arbitrary"),
    )(x, norm_w, w_main, w_small, w_small_t, *gather)


def _running_sum_mm(x, tri):
    hi = x.astype(MXU_DTYPE)
    lo = (x - hi.astype(F32)).astype(MXU_DTYPE)
    return jnp.dot(hi, tri, preferred_element_type=F32) + jnp.dot(lo, tri, preferred_element_type=F32)


def _sb_iotas():
    row_i = lax.broadcasted_iota(jnp.int32, (SB_BQ, SB_BLOCK), 0)
    col_i = lax.broadcasted_iota(jnp.int32, (SB_BQ, SB_BLOCK), 1)
    sq_r = lax.broadcasted_iota(jnp.int32, (SB_BLOCK, SB_BLOCK), 0)
    sq_c = lax.broadcasted_iota(jnp.int32, (SB_BLOCK, SB_BLOCK), 1)
    return row_i, col_i, sq_r, sq_c


SB_DIAG_BLOCKS = SB_BQ // SB_BLOCK
SB_EXP_FLOOR = -110.0


def _sb_keys_descending(qi, tile, carry, z_bounds, n_heads, has_free):
    group = SB_DIAG_BLOCKS
    n_free = group * qi
    diag = list(range(group - 1, -1, -1))
    carry = tile([n_free + j for j in diag], [True] * group, carry, [j * SB_BLOCK for j in diag])

    def largest_exponent(c):
        worst = jnp.max(z_bounds[0] - c[1])
        for h in range(1, n_heads):
            worst = jnp.maximum(worst, jnp.max(z_bounds[h] - c[1 + h]))
        return worst

    always = group if has_free else 0

    def cond(state):
        return (state[0] < n_free) & ((state[1] > SB_EXP_FLOOR) | (state[0] < always))

    def body(state):
        first = n_free - 1 - state[0]
        c = tile([first - j for j in range(group)], [False] * group, state[2:])
        return (state[0] + group, largest_exponent(c), *c)

    out = lax.while_loop(cond, body, (jnp.int32(0), largest_exponent(carry), *carry))
    return out[2:], out[0]


def _sb_keys_ascending(qi, n_run, tile, carry, has_free):
    group = SB_DIAG_BLOCKS
    n_free = group * qi
    diag = list(range(group))
    kjs, los, masked = [n_free + j for j in diag], [j * SB_BLOCK for j in diag], [True] * group
    if has_free:
        early = lambda s: [n_free - n_run + group * s + j for j in range(group)]
        carry = lax.fori_loop(0, n_run // group - 1, lambda s, c: tile(early(s), [False] * group, c), carry)
        kjs, los, masked = [n_free - group + j for j in range(group)] + kjs, [0] * group + los, [False] * group + masked
    return tile(kjs, masked, carry, los)


def _sb_fwd_call(proj, t_len):
    nq = t_len // SB_BQ
    scale = float(SB_HEAD_DIM) ** -0.5
    n_pairs = 512 // LANES
    per_pair = LANES // SB_HEAD_DIM

    def body(q_ref, k_ref, v_ref, o_ref, st_ref, nrun_ref):
        lane = lax.broadcasted_iota(jnp.int32, (1, LANES), 1)
        row_i, col_i, sq_r, sq_c = _sb_iotas()
        ge = (sq_r >= sq_c).astype(MXU_DTYPE)
        hms = [((lane // SB_HEAD_DIM) == hh).astype(F32) for hh in range(per_pair)]
        k_sq = k_ref[...] * k_ref[...]
        k_norms = [jnp.sqrt(jnp.max(jnp.sum(k_sq * hm, axis=-1, keepdims=True))) * (1.02 * scale) for hm in hms]

        def q_block(qi, has_free):
            r0 = qi * SB_BQ if isinstance(qi, int) else pl.multiple_of(qi * SB_BQ, SB_BQ)
            rows = pl.ds(r0, SB_BQ)
            q_all = q_ref[rows, :]
            qms = [(q_all * (hm * scale)).astype(MXU_DTYPE) for hm in hms]
            z_bounds = [jnp.sqrt(jnp.sum(q_all * q_all * hm, axis=-1, keepdims=True)) * kn
                        for hm, kn in zip(hms, k_norms)]

            def tile(kjs, masked, kc, los=None):
                heads = range(per_pair)
                los = los or [0] * len(kjs)
                pairs = [(t, h) for t in range(len(kjs)) for h in heads]
                add_rows = lambda full, lo, part: full + part if lo == 0 else jnp.concatenate(
                    [full[:lo], full[lo:] + part], axis=0)
                acc, cs = kc[0], list(kc[1:])
                s0s = [kj * SB_BLOCK if isinstance(kj, int) else pl.multiple_of(kj * SB_BLOCK, SB_BLOCK) for kj in kjs]
                kbs = [k_ref[pl.ds(s0, SB_BLOCK), :].astype(MXU_DTYPE) for s0 in s0s]
                v_alls = [v_ref[pl.ds(s0, SB_BLOCK), :] for s0 in s0s]
                vms = {(t, h): (v_alls[t] * hms[h]).astype(MXU_DTYPE) for t, h in pairs}
                zs = {(t, h): lax.dot_general(qms[h][los[t]:], kbs[t], _NT, preferred_element_type=F32)
                      for t, h in pairs}
                masks = [(col_i[lo:] + s0) < (row_i[lo:] + r0) if m else None for m, lo, s0 in zip(masked, los, s0s)]
                keep = lambda t, a: a if masks[t] is None else jnp.where(masks[t], a, 0.0)
                sps = {(t, h): keep(t, _softplus(zs[t, h])) for t, h in pairs}
                sums = {p: _running_sum_mm(sps[p], ge) for p in pairs}
                mass = {}
                for t, h in pairs:
                    mass[t, h] = cs[h] if t == 0 else add_rows(
                        mass[t - 1, h], los[t - 1], jnp.sum(sps[t - 1, h], axis=-1, keepdims=True))
                ws = {(t, h): keep(t, jnp.exp(zs[t, h] - (sums[t, h] + mass[t, h][los[t]:]))) for t, h in pairs}
                for t, h in pairs:
                    acc = add_rows(acc, los[t], jnp.dot(ws[t, h].astype(MXU_DTYPE), vms[t, h],
                                                        preferred_element_type=F32))
                last = len(kjs) - 1
                cs = [add_rows(mass[last, h], los[last], jnp.sum(sps[last, h], axis=-1, keepdims=True)) for h in heads]
                return (acc, *cs)

            zero_col = jnp.zeros((SB_BQ, 1), F32)
            out, n_run = _sb_keys_descending(
                qi, tile, (jnp.zeros((SB_BQ, LANES), F32),) + (zero_col,) * per_pair, z_bounds, per_pair, has_free)
            o_ref[rows, :] = out[0]
            for hh in range(per_pair):
                st_ref[hh, rows, :] = out[1 + hh]
            nrun_ref[pl.program_id(0), qi] = n_run

        q_block(0, False)
        lax.fori_loop(1, nq, lambda qi, carry: (q_block(qi, True), carry)[1], 0)

    return pl.pallas_call(
        body, name="sb_fwd",
        grid=(n_pairs,),
        in_specs=[pl.BlockSpec((t_len, LANES), lambda p: (0, p)),
                  pl.BlockSpec((t_len, LANES), lambda p: (0, n_pairs + p)),
                  pl.BlockSpec((t_len, LANES), lambda p: (0, 2 * n_pairs + p))],
        out_specs=[pl.BlockSpec((t_len, LANES), lambda p: (0, p)),
                   pl.BlockSpec((per_pair, t_len, 1), lambda p: (p, 0, 0)),
                   pl.BlockSpec(memory_space=pltpu.SMEM)],
        out_shape=[jax.ShapeDtypeStruct((t_len, 512), F32),
                   jax.ShapeDtypeStruct((n_pairs * per_pair, t_len, 1), F32),
                   jax.ShapeDtypeStruct((n_pairs, nq), jnp.int32)],
        compiler_params=_params("arbitrary"),
    )(proj, proj, proj)


def _sb_bwd_call(proj, sp_total, n_run_all, d_o, dproj, t_len):
    nq = t_len // SB_BQ
    scale = float(SB_HEAD_DIM) ** -0.5
    n_pairs = 512 // LANES
    per_pair = LANES // SB_HEAD_DIM

    def body(q_ref, k_ref, v_ref, st_ref, nrun_ref, do_ref, dproj_in_ref, d_ref):
        lane = lax.broadcasted_iota(jnp.int32, (1, LANES), 1)
        row_i, col_i, sq_r, sq_c = _sb_iotas()
        lt = (sq_r < sq_c).astype(MXU_DTYPE)
        le = (sq_r <= sq_c).astype(MXU_DTYPE)
        hms = [((lane // SB_HEAD_DIM) == hh).astype(F32) for hh in range(per_pair)]
        d_ref[1] = jnp.zeros((t_len, LANES), F32)
        d_ref[2] = jnp.zeros((t_len, LANES), F32)

        def q_block(qi, has_free):
            r0 = qi * SB_BQ if isinstance(qi, int) else pl.multiple_of(qi * SB_BQ, SB_BQ)
            rows = pl.ds(r0, SB_BQ)
            q_all, do_all = q_ref[rows, :], do_ref[rows, :]
            qms = [(q_all * (hm * scale)).astype(MXU_DTYPE) for hm in hms]
            doms = [(do_all * hm).astype(MXU_DTYPE) for hm in hms]
            totals = [st_ref[hh, rows, :] for hh in range(per_pair)]

            def tile(kjs, masked, kc, los=None):
                heads = range(per_pair)
                los = los or [0] * len(kjs)
                pairs = [(t, h) for t in range(len(kjs)) for h in heads]
                add_rows = lambda full, lo, part: full + part if lo == 0 else jnp.concatenate(
                    [full[:lo], full[lo:] + part], axis=0)
                rsum = lambda a: jnp.sum(a, axis=-1, keepdims=True)
                dq, cls, gls = kc[0], list(kc[1:1 + per_pair]), list(kc[1 + per_pair:])
                s0s = [kj * SB_BLOCK if isinstance(kj, int) else pl.multiple_of(kj * SB_BLOCK, SB_BLOCK) for kj in kjs]
                k_alls = [k_ref[pl.ds(s0, SB_BLOCK), :] for s0 in s0s]
                v_alls = [v_ref[pl.ds(s0, SB_BLOCK), :] for s0 in s0s]
                kbs = [k_all.astype(MXU_DTYPE) for k_all in k_alls]
                vms = {(t, h): (v_alls[t] * hms[h]).astype(MXU_DTYPE) for t, h in pairs}
                kms = {(t, h): (k_alls[t] * (hms[h] * scale)).astype(MXU_DTYPE) for t, h in pairs}
                q_live = {(t, h): qms[h][los[t]:] for t, h in pairs}
                do_live = {(t, h): doms[h][los[t]:] for t, h in pairs}
                zs = {p: lax.dot_general(q_live[p], kbs[p[0]], _NT, preferred_element_type=F32) for p in pairs}
                das = {p: lax.dot_general(do_live[p], vms[p], _NT, preferred_element_type=F32) for p in pairs}
                masks = [(col_i[lo:] + s0) < (row_i[lo:] + r0) if m else None for m, lo, s0 in zip(masked, los, s0s)]
                keep = lambda t, a: a if masks[t] is None else jnp.where(masks[t], a, 0.0)
                sp_alls = {p: _softplus(zs[p]) for p in pairs}
                sps = {(t, h): keep(t, sp_alls[t, h]) for t, h in pairs}
                lefts = {p: _running_sum_mm(sps[p], lt) for p in pairs}
                cl = {}
                for t, h in pairs:
                    cl[t, h] = cls[h] if t == 0 else add_rows(cl[t - 1, h], los[t - 1], rsum(sps[t - 1, h]))
                ws = {(t, h): keep(t, jnp.exp(zs[t, h] - ((totals[h] - cl[t, h])[los[t]:] - lefts[t, h])))
                      for t, h in pairs}
                gs = {p: das[p] * ws[p] for p in pairs}
                g_sums = {p: _running_sum_mm(gs[p], le) for p in pairs}
                gl = {}
                for t, h in pairs:
                    gl[t, h] = gls[h] if t == 0 else add_rows(gl[t - 1, h], los[t - 1], rsum(gs[t - 1, h]))
                dzs = {(t, h): keep(t, gs[t, h] - jnp.exp(zs[t, h] - sp_alls[t, h]) * (gl[t, h][los[t]:] + g_sums[t, h])
                               ).astype(MXU_DTYPE) for t, h in pairs}
                for t in range(len(kjs)):
                    dk_t = jnp.zeros((SB_BLOCK, LANES), F32)
                    dv_t = jnp.zeros((SB_BLOCK, LANES), F32)
                    for h in heads:
                        dq = add_rows(dq, los[t], jnp.dot(dzs[t, h], kms[t, h], preferred_element_type=F32))
                        dk_t = dk_t + lax.dot_general(dzs[t, h], q_live[t, h], _TN, preferred_element_type=F32)
                        dv_t = dv_t + lax.dot_general(ws[t, h].astype(MXU_DTYPE), do_live[t, h], _TN,
                                                      preferred_element_type=F32)
                    d_ref[1, pl.ds(s0s[t], SB_BLOCK), :] += dk_t
                    d_ref[2, pl.ds(s0s[t], SB_BLOCK), :] += dv_t
                last = len(kjs) - 1
                cls = [add_rows(cl[last, h], los[last], rsum(sps[last, h])) for h in heads]
                gls = [add_rows(gl[last, h], los[last], rsum(gs[last, h])) for h in heads]
                return (dq, *cls, *gls)

            zero_col = jnp.zeros((SB_BQ, 1), F32)
            out = _sb_keys_ascending(qi, nrun_ref[pl.program_id(0), qi], tile,
                                     (jnp.zeros((SB_BQ, LANES), F32),) + (zero_col,) * (2 * per_pair), has_free)
            d_ref[0, rows, :] = out[0]

        q_block(0, False)
        lax.fori_loop(1, nq, lambda qi, carry: (q_block(qi, True), carry)[1], 0)

    col = lambda off: pl.BlockSpec((t_len, LANES), lambda p: (0, off + p))
    return pl.pallas_call(
        body, name="sb_bwd",
        grid=(n_pairs,),
        in_specs=[col(0), col(n_pairs), col(2 * n_pairs),
                  pl.BlockSpec((per_pair, t_len, 1), lambda p: (p, 0, 0)),
                  pl.BlockSpec(memory_space=pltpu.SMEM), col(0), _HBM],
        out_specs=pl.BlockSpec((3, t_len, LANES), lambda p: (DPROJ_SB_SLOT // 3, 0, p)),
        out_shape=jax.ShapeDtypeStruct(dproj.shape, dproj.dtype),
        input_output_aliases={6: 0},
        compiler_params=_params("arbitrary"),
    )(proj, proj, proj, sp_total, n_run_all, d_o, dproj)


def _conv_taps(xin, rows, t_len):
    taps = []
    for i in range(CONV_WIDTH):
        shift = CONV_WIDTH - 1 - i
        if shift == 0:
            taps.append(xin)
        else:
            taps.append(jnp.where(rows >= shift, pltpu.roll(xin, shift, axis=0), 0.0))
    return taps


def _gdn_prep_body_common(x_ref, w_ref, t_len):
    j = pl.program_id(0)
    xin = x_ref[...]
    rows = lax.broadcasted_iota(jnp.int32, (t_len, LANES), 0)
    taps = _conv_taps(xin, rows, t_len)
    pre = taps[0] * w_ref[0:1, :]
    for i in range(1, CONV_WIDTH):
        pre = pre + taps[i] * w_ref[i:i + 1, :]
    sg = _sigmoid(pre)
    act = pre * sg
    is_qk = j < 2 * GDN_HEADS
    nrm = jnp.where(is_qk, lax.rsqrt(jnp.sum(act * act, axis=-1, keepdims=True) + EPS), 1.0)
    sc = jnp.where(j < GDN_HEADS, float(GDN_HEAD_DIM) ** -0.5, 1.0)
    return j, rows, taps, pre, sg, act, is_qk, nrm, sc


def _gdn_prep_call(proj, conv_w, t_len):
    first = 2048 // LANES

    def body(x_ref, w_ref, out_ref):
        _, _, _, _, _, act, _, nrm, sc = _gdn_prep_body_common(x_ref, w_ref, t_len)
        out_ref[...] = act * nrm * sc

    return pl.pallas_call(
        body, name="gdn_prep",
        grid=(3 * GDN_HEADS,),
        in_specs=[pl.BlockSpec((t_len, LANES), lambda j: (0, first + j)),
                  pl.BlockSpec((CONV_WIDTH, LANES), lambda j: (0, j))],
        out_specs=pl.BlockSpec((t_len, LANES), lambda j: (0, j)),
        out_shape=jax.ShapeDtypeStruct((t_len, 3 * 512), F32),
        compiler_params=_params("arbitrary"),
    )(proj, conv_w)


def _gdn_prep_bwd_call(proj, conv_w, d_act3, dproj, t_len):
    first = 2048 // LANES

    def body(x_ref, w_ref, d_ref, dproj_in_ref, dx_ref, dw_ref):
        _, rows, taps, pre, sg, act, is_qk, nrm, sc = _gdn_prep_body_common(x_ref, w_ref, t_len)
        d_out = d_ref[0]
        dn = d_out * sc
        d_norm = nrm * dn - act * (nrm * nrm * nrm) * jnp.sum(dn * act, axis=-1, keepdims=True)
        d_act = jnp.where(is_qk, d_norm, d_out)
        d_pre = d_act * sg * (1.0 + pre * (1.0 - sg))
        dx = d_pre * w_ref[CONV_WIDTH - 1:CONV_WIDTH, :]
        dw_ref[CONV_WIDTH - 1:CONV_WIDTH, :] = jnp.sum(d_pre * taps[CONV_WIDTH - 1], axis=0, keepdims=True)
        for i in range(CONV_WIDTH - 1):
            shift = CONV_WIDTH - 1 - i
            up = jnp.where(rows < t_len - shift, pltpu.roll(d_pre, t_len - shift, axis=0), 0.0)
            dx = dx + up * w_ref[i:i + 1, :]
            dw_ref[i:i + 1, :] = jnp.sum(d_pre * taps[i], axis=0, keepdims=True)
        dx_ref[0] = dx

    return pl.pallas_call(
        body, name="gdn_prep_bwd",
        grid=(3 * GDN_HEADS,),
        in_specs=[pl.BlockSpec((t_len, LANES), lambda j: (0, first + j)),
                  pl.BlockSpec((CONV_WIDTH, LANES), lambda j: (0, j)),
                  pl.BlockSpec((1, t_len, LANES), lambda j: (j // GDN_HEADS, 0, j % GDN_HEADS)), _HBM],
        out_specs=[pl.BlockSpec((1, t_len, LANES), lambda j: (DPROJ_GDN_SLOT + j // GDN_HEADS, 0, j % GDN_HEADS)),
                   pl.BlockSpec((CONV_WIDTH, LANES), lambda j: (0, j))],
        out_shape=[jax.ShapeDtypeStruct(dproj.shape, dproj.dtype),
                   jax.ShapeDtypeStruct((CONV_WIDTH, 3 * 512), F32)],
        input_output_aliases={3: 0},
        compiler_params=_params("arbitrary"),
    )(proj, conv_w, d_act3, dproj)


def _chunk_cumsum_matrix():
    r = lax.broadcasted_iota(jnp.int32, (LANES, LANES), 0)
    c = lax.broadcasted_iota(jnp.int32, (LANES, LANES), 1)
    return ((r <= c) & ((r // CHUNK) == (c // CHUNK))).astype(F32)


def _gdn_gates_call(ps, pst, alog_l, dtb_l, alog_c, dtb_c, t_len):
    def body(ps_ref, pst_ref, al_ref, dl_ref, ac_ref, dc_ref, beta_ref, gcol_ref, grow_ref):
        upper = _chunk_cumsum_matrix()
        lower = upper.T
        psv = ps_ref[...]
        beta_ref[...] = _sigmoid(psv)
        g_l = -jnp.exp(al_ref[...]) * _softplus(psv + dl_ref[...])
        g_r = -jnp.exp(ac_ref[...]) * _softplus(pst_ref[...] + dc_ref[...])
        for w in range(t_len // LANES):
            sl = slice(w * LANES, (w + 1) * LANES)
            gcol_ref[sl, :] = _mx(lower, g_l[sl, :])
            grow_ref[:, sl] = _mx(g_r[:, sl], upper)

    vm = pl.BlockSpec(memory_space=pltpu.VMEM)
    return pl.pallas_call(
        body, name="gdn_gates",
        in_specs=[vm] * 6, out_specs=[vm] * 3,
        out_shape=[jax.ShapeDtypeStruct((t_len, LANES), F32),
                   jax.ShapeDtypeStruct((t_len, LANES), F32),
                   jax.ShapeDtypeStruct((8, t_len), F32)],
        compiler_params=pltpu.CompilerParams(vmem_limit_bytes=VMEM_LIMIT_BYTES),
    )(ps, pst, alog_l, dtb_l, alog_c, dtb_c)


def _gdn_gates_bwd_call(ps, alog_l, dtb_l, d_l, t_len):
    def body(ps_ref, al_ref, dl_ref, d_ref, dps_ref, gal_ref, gdt_ref):
        lane = lax.broadcasted_iota(jnp.int32, (1, LANES), 1)
        psv = ps_ref[...]
        dv = d_ref[...]
        beta = _sigmoid(psv)
        ea = jnp.exp(al_ref[...])
        arg = psv + dl_ref[...]
        g = -ea * _softplus(arg)
        d_a = dv * (-ea) * _sigmoid(arg)
        is_a = (lane >= GDN_HEADS) & (lane < 2 * GDN_HEADS)
        dps_ref[...] = jnp.where(lane < GDN_HEADS, dv * beta * (1.0 - beta), jnp.where(is_a, d_a, 0.0))
        gdt_ref[...] = jnp.where(is_a, jnp.sum(d_a, axis=0, keepdims=True), 0.0)
        gal_ref[...] = jnp.where(is_a, jnp.sum(dv * g, axis=0, keepdims=True), 0.0)

    vm = pl.BlockSpec(memory_space=pltpu.VMEM)
    return pl.pallas_call(
        body, name="gdn_gates_bwd",
        in_specs=[vm] * 4, out_specs=[vm] * 3,
        out_shape=[jax.ShapeDtypeStruct((t_len, LANES), F32),
                   jax.ShapeDtypeStruct((1, LANES), F32),
                   jax.ShapeDtypeStruct((1, LANES), F32)],
        compiler_params=pltpu.CompilerParams(vmem_limit_bytes=VMEM_LIMIT_BYTES),
    )(ps, alog_l, dtb_l, d_l)


def _bm(a, b):
    return _m3_general(a, b, _BNN)


def _bm_nt(a, b):
    return _m3_general(a, b, _BNT)


def _bm_tn(a, b):
    return _m3_general(a, b, _BTN)


def _heads_of(ref, rows):
    return jnp.stack([ref[rows, h * GDN_HEAD_DIM:(h + 1) * GDN_HEAD_DIM] for h in range(GDN_HEADS)])


def _chunk_terms(q_ref, k_ref, v_ref, b_ref, gc_ref, gr_ref, c, incl, strict, n=1, scores=True):
    r0 = c * CHUNK if isinstance(c, int) else pl.multiple_of(c * CHUNK, CHUNK)
    rows = pl.ds(r0, n * CHUNK)
    per_chunk = lambda x: x.reshape(GDN_HEADS * n, CHUNK, x.shape[-1])
    q, k, v = (per_chunk(_heads_of(ref, rows)) for ref in (q_ref, k_ref, v_ref))
    lane_ids = lax.broadcasted_iota(jnp.int32, (1, LANES), 1)
    pick = lambda slab, first: jnp.stack([jnp.sum(jnp.where(lane_ids == first + h, slab, 0.0), axis=-1, keepdims=True)
                                          for h in range(GDN_HEADS)])
    b = per_chunk(pick(b_ref[rows, :], 0))
    gc = per_chunk(pick(gc_ref[rows, :], GDN_HEADS))
    gr = gr_ref[:, c] if n == 1 else gr_ref[:, c:c + n].reshape(GDN_HEADS * n, 1, CHUNK)
    dm = jnp.where(incl, jnp.exp(jnp.where(incl, gc - gr, 0.0)), 0.0)
    kb = k * b
    vb = v * b
    e = jnp.exp(gc)
    a = p = None
    if scores:
        kk_qk = _bm_nt(jnp.concatenate([kb, q], axis=1), k)
        a = jnp.where(strict, kk_qk[:, :CHUNK] * dm, 0.0)
        p = jnp.where(incl, kk_qk[:, CHUNK:] * dm, 0.0)
    gl = gc[:, CHUNK - 1:CHUNK, :]
    eg = jnp.exp(gl - gc)
    return rows, q, k, v, b, gc, dm, kb, vb, e, a, p, gl, eg


def _unit_lower_inverse(a, eye):
    x = -a
    tm = eye + x
    xp = _bm(x, x)
    for _ in range(4):
        both = _bm(jnp.concatenate([xp, tm], axis=1), xp)
        tm = tm + both[:, CHUNK:]
        xp = both[:, :CHUNK]
    return tm + _bm(tm, xp)


def _gdn_specs(t_len, n_chunks, reverse):
    cps = GDN_CHUNKS_PER_STEP
    steps = n_chunks // cps
    at = (lambda g: steps - 1 - g) if reverse else (lambda g: g)
    rows_blk = lambda width, part=0: pl.BlockSpec((cps * CHUNK, width), lambda g: (at(g), part))
    gate_r = pl.BlockSpec((GDN_HEADS, cps, 1, CHUNK), lambda g: (0, at(g), 0, 0))
    per_chunk = lambda r, c: pl.BlockSpec((GDN_HEADS, cps, r, c), lambda g: (0, at(g), 0, 0))
    return cps, steps, rows_blk, gate_r, per_chunk


def _gdn_fwd_call(gact, beta_c, gam_c, gam_r, t_len):
    n_chunks = t_len // CHUNK
    dk = GDN_HEAD_DIM
    width = GDN_HEADS * dk
    cps, steps, rows_blk, gate_r, per_chunk = _gdn_specs(t_len, n_chunks, False)

    def body(q_ref, k_ref, v_ref, b_ref, gc_ref, gr_ref, o_ref, s_ref, t_ref, a_ref, p_ref, uw_ref, vn_ref, state_ref):
        row = lax.broadcasted_iota(jnp.int32, (CHUNK, CHUNK), 0)
        col = lax.broadcasted_iota(jnp.int32, (CHUNK, CHUNK), 1)
        incl, strict = row >= col, row > col
        eye = (row == col).astype(F32)

        @pl.when(pl.program_id(0) == 0)
        def _():
            state_ref[...] = jnp.zeros_like(state_ref)

        _, q, k, v, b, gc, dm, kb, vb, e, a, p, gl, eg = _chunk_terms(
            q_ref, k_ref, v_ref, b_ref, gc_ref, gr_ref, 0, incl, strict, cps)
        tm = _unit_lower_inverse(a, eye)
        uw = _bm(tm, jnp.concatenate([vb, kb * e], axis=2))
        w_qe = jnp.concatenate([uw[:, :, dk:], q * e], axis=1)
        u, kd, decay = uw[:, :, :dk], k * eg, jnp.exp(gl)
        per_chunk_block = lambda x: x.reshape(GDN_HEADS, cps, CHUNK, CHUNK)
        t_ref[...], a_ref[...], p_ref[...] = per_chunk_block(tm), per_chunk_block(a), per_chunk_block(p)
        uw_heads = uw.reshape(GDN_HEADS, cps * CHUNK, 2 * dk)
        for h in range(GDN_HEADS):
            uw_ref[:, h * 2 * dk:(h + 1) * 2 * dk] = uw_heads[h]

        of_chunk = lambda x, c: jnp.stack([x[h * cps + c] for h in range(GDN_HEADS)])
        s = state_ref[...]
        for c in range(cps):
            ws_qs = _bm(of_chunk(w_qe, c), s)
            vn = of_chunk(u, c) - ws_qs[:, :CHUNK]
            o = ws_qs[:, CHUNK:] + _bm(of_chunk(p, c), vn)
            for h in range(GDN_HEADS):
                o_ref[c * CHUNK:(c + 1) * CHUNK, h * dk:(h + 1) * dk] = o[h]
                vn_ref[c * CHUNK:(c + 1) * CHUNK, h * dk:(h + 1) * dk] = vn[h]
            s_ref[:, c] = s
            s = s * of_chunk(decay, c) + _bm_tn(of_chunk(kd, c), vn)
        state_ref[...] = s

    scores = jax.ShapeDtypeStruct((GDN_HEADS, n_chunks, CHUNK, CHUNK), F32)
    return pl.pallas_call(
        body, name="gdn_fwd",
        grid=(steps,),
        in_specs=[rows_blk(width, 0), rows_blk(width, 1), rows_blk(width, 2), rows_blk(LANES), rows_blk(LANES), gate_r],
        out_specs=[rows_blk(width), per_chunk(dk, dk), per_chunk(CHUNK, CHUNK), per_chunk(CHUNK, CHUNK),
                   per_chunk(CHUNK, CHUNK), rows_blk(2 * width), rows_blk(width)],
        out_shape=[jax.ShapeDtypeStruct((t_len, width), F32),
                   jax.ShapeDtypeStruct((GDN_HEADS, n_chunks, dk, dk), F32), scores, scores, scores,
                   jax.ShapeDtypeStruct((t_len, 2 * width), F32), jax.ShapeDtypeStruct((t_len, width), F32)],
        scratch_shapes=[pltpu.VMEM((GDN_HEADS, dk, dk), F32)],
        compiler_params=_params("arbitrary"),
    )(gact, gact, gact, beta_c, gam_c, gam_r)


def _gdn_bwd_call(gact, beta_c, gam_c, gam_r, saved, d_o, t_len, scatter=()):
    n_chunks = t_len // CHUNK
    dk = GDN_HEAD_DIM
    width = GDN_HEADS * dk
    cps, steps, rows_blk, gate_r, per_chunk = _gdn_specs(t_len, n_chunks, True)
    nx = len(scatter)
    n_in = 13

    def body(*refs):
        q_ref, k_ref, v_ref, b_ref, gc_ref, gr_ref = refs[:6]
        saved_refs, do_ref = refs[6:12], refs[12]
        d_ref, dgate_ref = refs[n_in + nx:n_in + 2 + nx]
        dstate_ref = refs[n_in + 2 + 2 * nx]
        copies = lambda: _direct_copies(refs[n_in:n_in + nx], refs[n_in + 2 + nx:n_in + 2 + 2 * nx],
                                        *refs[n_in + 3 + 2 * nx:], (True,) * nx)
        if nx:
            pl.when(pl.program_id(0) == 0)(lambda: _start_all(copies()))
        row = lax.broadcasted_iota(jnp.int32, (CHUNK, CHUNK), 0)
        col = lax.broadcasted_iota(jnp.int32, (CHUNK, CHUNK), 1)
        incl, strict = row >= col, row > col
        ng = GDN_BWD_GROUP
        nb = GDN_HEADS * ng
        upper = jnp.broadcast_to((row <= col).astype(F32), (nb, CHUNK, CHUNK))
        ones = jnp.ones((nb, CHUNK, LANES), F32)
        last_row = lax.broadcasted_iota(jnp.int32, (CHUNK, 1), 0) == CHUNK - 1
        lane_ids = lax.broadcasted_iota(jnp.int32, (1, LANES), 1)
        rsum = lambda m: jnp.sum(m, axis=-1, keepdims=True)
        total = lambda m: jnp.sum(rsum(m), axis=1, keepdims=True)
        of_chunk = lambda x, c: jnp.stack([x[h * ng + c] for h in range(GDN_HEADS)])

        @pl.when(pl.program_id(0) == 0)
        def _():
            dstate_ref[...] = jnp.zeros_like(dstate_ref)

        for c0 in range(cps - ng, -1, -ng):
            group(c0, q_ref, k_ref, v_ref, b_ref, gc_ref, gr_ref, saved_refs, do_ref, d_ref, dgate_ref, dstate_ref,
                  incl, strict, upper, ones, last_row, lane_ids, rsum, total, of_chunk)
        if nx:
            pl.when(pl.program_id(0) == steps - 1)(lambda: _wait_all(copies()))

    def group(c0, q_ref, k_ref, v_ref, b_ref, gc_ref, gr_ref, saved_refs, do_ref, d_ref, dgate_ref, dstate_ref,
              incl, strict, upper, ones, last_row, lane_ids, rsum, total, of_chunk):
        ng = GDN_BWD_GROUP
        nb = GDN_HEADS * ng
        rows = pl.ds(c0 * CHUNK, ng * CHUNK)
        s_ref, t_ref, a_ref, p_ref, uw_ref, vn_ref = saved_refs
        _, q, k, v, b, gc, dm, kb, vb, e, _, _, gl, eg = _chunk_terms(
            q_ref, k_ref, v_ref, b_ref, gc_ref, gr_ref, c0, incl, strict, ng, scores=False)
        s = s_ref[:, c0:c0 + ng].reshape(nb, dk, dk)
        tm = t_ref[:, c0:c0 + ng].reshape(nb, CHUNK, CHUNK)
        a = a_ref[:, c0:c0 + ng].reshape(nb, CHUNK, CHUNK)
        p = p_ref[:, c0:c0 + ng].reshape(nb, CHUNK, CHUNK)
        d_out = _heads_of(do_ref, rows).reshape(nb, CHUNK, dk)
        vn = _heads_of(vn_ref, rows).reshape(nb, CHUNK, dk)
        uw = jnp.stack([uw_ref[rows, h * 2 * dk:(h + 1) * 2 * dk] for h in range(GDN_HEADS)]).reshape(nb, CHUNK, 2 * dk)
        u, w = uw[:, :, :dk], uw[:, :, dk:]
        el = jnp.exp(gl)
        kbe = kb * e
        qe = q * e
        kd = k * eg
        pt_do = _bm_tn(p, d_out)
        qet_do = _bm_tn(qe, d_out)

        ds = dstate_ref[...]
        d_vn_c, ds_c = [None] * ng, [None] * ng
        for c in range(ng - 1, -1, -1):
            ds_c[c] = ds
            d_vn_c[c] = of_chunk(pt_do, c) + _bm(of_chunk(kd, c), ds)
            ds = of_chunk(el, c) * ds + of_chunk(qet_do, c) - _bm_tn(of_chunk(w, c), d_vn_c[c])
        dstate_ref[...] = ds
        by_chunk = lambda xs: jnp.stack([xs[c][h] for h in range(GDN_HEADS) for c in range(ng)])
        d_vn, ds = by_chunk(d_vn_c), by_chunk(ds_c)

        on_s = _bm_nt(jnp.concatenate([d_out, d_vn], axis=1), s)
        d_qe, d_w = on_s[:, :CHUNK], -on_s[:, CHUNK:]
        d_p = jnp.where(incl, _bm_nt(d_out, vn), 0.0)
        d_kd = _bm_nt(vn, ds)
        d_both = _bm_tn(tm, jnp.concatenate([d_vn, d_w], axis=2))
        d_vb, d_kbe = d_both[:, :, :dk], d_both[:, :, dk:]
        d_a = -jnp.where(strict, _bm_nt(d_both, uw), 0.0)
        m = d_a * dm
        n = d_p * dm
        on_k = _bm(jnp.concatenate([m, n], axis=1), k)
        d_kb = on_k[:, :CHUNK] + d_kbe * e
        d_q = on_k[:, CHUNK:] + d_qe * e
        d_k = (_bm_tn(jnp.concatenate([m, n], axis=1), jnp.concatenate([kb, q], axis=1))
               + d_kd * eg + b * d_kb)
        d_v = b * d_vb
        r = d_a * a + d_p * p
        kd_term = rsum(d_kd * kd)
        d_gl = total(ds * s) * el + jnp.sum(kd_term, axis=1, keepdims=True)
        d_gam = (rsum(r) - _times_exact(r, ones, _BTN)[:, :, 0:1] + rsum(d_qe * qe) + rsum(d_kbe * kbe) - kd_term
                 + jnp.where(last_row, d_gl, 0.0))
        d_beta = rsum(d_kb * k) + rsum(d_vb * v)
        d_g = _exact_times(upper, d_gam * ones, _BNN)[:, :, 0:1]
        per_head = lambda x: x.reshape(GDN_HEADS, ng * CHUNK, x.shape[-1])
        d_q, d_k, d_v, d_beta, d_g = (per_head(x) for x in (d_q, d_k, d_v, d_beta, d_g))
        gates = jnp.zeros((ng * CHUNK, LANES), F32)
        for h in range(GDN_HEADS):
            lanes = slice(h * dk, (h + 1) * dk)
            d_ref[0, rows, lanes] = d_q[h]
            d_ref[1, rows, lanes] = d_k[h]
            d_ref[2, rows, lanes] = d_v[h]
            gates = gates + (jnp.where(lane_ids == h, d_beta[h], 0.0)
                             + jnp.where(lane_ids == GDN_HEADS + h, d_g[h], 0.0))
        dgate_ref[rows, :] = gates

    d_spec = pl.BlockSpec((3, cps * CHUNK, width), lambda g: (0, steps - 1 - g, 0))
    return pl.pallas_call(
        body, name="gdn_bwd",
        grid=(steps,),
        in_specs=[rows_blk(width, 0), rows_blk(width, 1), rows_blk(width, 2), rows_blk(LANES), rows_blk(LANES), gate_r,
                  per_chunk(dk, dk), per_chunk(CHUNK, CHUNK), per_chunk(CHUNK, CHUNK), per_chunk(CHUNK, CHUNK),
                  rows_blk(2 * width), rows_blk(width), rows_blk(width)] + [_HBM] * nx,
        out_specs=[d_spec, rows_blk(LANES)] + [_HBM] * nx,
        out_shape=[jax.ShapeDtypeStruct((3, t_len, width), F32),
                   jax.ShapeDtypeStruct((t_len, LANES), F32)] + _direct_out_shapes(scatter, (True,) * nx),
        scratch_shapes=[pltpu.VMEM((GDN_HEADS, dk, dk), F32)] + (_direct_semaphores(nx) if nx else []),
        compiler_params=_params("arbitrary"),
    )(gact, gact, gact, beta_c, gam_c, gam_r, *saved, d_o, *scatter)


def _group_sums(x, group):
    rows, width = x.shape
    lane = lax.broadcasted_iota(jnp.int32, (1, LANES), 1)
    out = []
    for t in range(width // LANES):
        seg = x[:, t * LANES:(t + 1) * LANES]
        if group == LANES:
            out.append(jnp.broadcast_to(jnp.sum(seg, axis=-1, keepdims=True), (rows, LANES)))
        else:
            low = jnp.sum(jnp.where(lane < group, seg, 0.0), axis=-1, keepdims=True)
            high = jnp.sum(jnp.where(lane < group, 0.0, seg), axis=-1, keepdims=True)
            out.append(jnp.where(lane < group, low, high))
    return jnp.concatenate(out, axis=1)


def _post_call(o_sb, o_gd, proj, x, target, w_out, sbw, gdw, fw, tm=256):
    t_len, d = x.shape
    half = 512
    zsb_blk = 1536 // half
    zgd_blk = 3584 // half

    def body(osb_ref, ogd_ref, zsb_ref, zgd_ref, x_ref, tg_ref, wo_ref, sbw_ref, gdw_ref, fw_ref,
             dx2_ref, dosb_ref, dogd_ref, dz_ref, loss_ref, gfw_ref, gsb_ref, ggd_ref, gwo_ref):
        step = pl.program_id(0)

        @pl.when(step == 0)
        def _():
            loss_ref[...] = jnp.zeros_like(loss_ref)
            gfw_ref[...] = jnp.zeros_like(gfw_ref)
            gsb_ref[...] = jnp.zeros_like(gsb_ref)
            ggd_ref[...] = jnp.zeros_like(ggd_ref)
            gwo_ref[...] = jnp.zeros_like(gwo_ref)

        def head_forward(o, z, w, head_dim):
            r = lax.rsqrt(_group_sums(o * o, head_dim) * (1.0 / head_dim) + EPS)
            nrm = o * r * w
            sg = _sigmoid(z)
            return r, nrm, sg, nrm * (z * sg)

        def head_backward(d_m, o, z, w, head_dim, r, nrm, sg):
            d_n = d_m * (z * sg)
            d_z = d_m * nrm * (sg * (1.0 + z * (1.0 - sg)))
            dnw = d_n * w
            d_o = r * dnw - o * (r * r * r) * (_group_sums(dnw * o, head_dim) * (1.0 / head_dim))
            return d_o, d_z, jnp.sum(d_n * o * r, axis=0, keepdims=True)

        osb, ogd, zsb, zgd = osb_ref[...], ogd_ref[...], zsb_ref[...], zgd_ref[...]
        sbw_v, gdw_v = sbw_ref[...], gdw_ref[...]
        r_sb, n_sb, sg_sb, m_sb = head_forward(osb, zsb, sbw_v, SB_HEAD_DIM)
        r_gd, n_gd, sg_gd, m_gd = head_forward(ogd, zgd, gdw_v, GDN_HEAD_DIM)
        mixed = jnp.concatenate([m_sb, m_gd], axis=1).astype(MXU_DTYPE)
        wo = wo_ref[...]
        x2 = x_ref[...] + jnp.dot(mixed, wo, preferred_element_type=F32)
        r2 = lax.rsqrt(jnp.mean(x2 * x2, axis=-1, keepdims=True) + EPS)
        fw_v = fw_ref[...]
        err = x2 * r2 * fw_v - tg_ref[...]
        loss_ref[...] += 0.5 * jnp.sum(jnp.sum(err * err, axis=-1, keepdims=True) * (1.0 / d))
        dy = err * (1.0 / d)
        gg = dy * fw_v
        dx2 = r2 * gg - x2 * ((r2 * r2 * r2) * jnp.mean(gg * x2, axis=-1, keepdims=True))
        gfw_ref[...] += jnp.sum(dy * x2 * r2, axis=0, keepdims=True)
        dx2_ref[...] = dx2
        dx2b = dx2.astype(MXU_DTYPE)
        d_mixed = lax.dot_general(dx2b, wo, _NT, preferred_element_type=F32)
        gwo_ref[...] += lax.dot_general(mixed, dx2b, _TN, preferred_element_type=F32)
        d_osb, d_zsb, gsb = head_backward(d_mixed[:, :half], osb, zsb, sbw_v, SB_HEAD_DIM, r_sb, n_sb, sg_sb)
        d_ogd, d_zgd, ggd = head_backward(d_mixed[:, half:], ogd, zgd, gdw_v, GDN_HEAD_DIM, r_gd, n_gd, sg_gd)
        dosb_ref[...] = d_osb
        dogd_ref[...] = d_ogd
        dz_ref[0] = d_zsb
        dz_ref[1] = d_zgd
        gsb_ref[...] += gsb
        ggd_ref[...] += ggd

    row_blk = lambda w: pl.BlockSpec((tm, w), lambda i: (i, 0))
    fixed = lambda r, w: pl.BlockSpec((r, w), lambda i: (0, 0))
    return pl.pallas_call(
        body, name="post",
        grid=(t_len // tm,),
        in_specs=[row_blk(half), row_blk(half),
                  pl.BlockSpec((tm, half), lambda i: (i, zsb_blk)),
                  pl.BlockSpec((tm, half), lambda i: (i, zgd_blk)),
                  row_blk(d), row_blk(d), fixed(d, d), fixed(1, half), fixed(1, half), fixed(1, d)],
        out_specs=[row_blk(d), row_blk(half), row_blk(half),
                   pl.BlockSpec((2, tm, half), lambda i: (DPROJ_GATE_SLOT // 2, i, 0)),
                   fixed(1, LANES), fixed(1, d), fixed(1, half), fixed(1, half), fixed(d, d)],
        out_shape=[jax.ShapeDtypeStruct((t_len, d), F32)] + [jax.ShapeDtypeStruct((t_len, half), F32)] * 2
                  + [jax.ShapeDtypeStruct((len(DPROJ_PIECE_OF_SLOT), t_len, half), F32),
                     jax.ShapeDtypeStruct((1, LANES), F32), jax.ShapeDtypeStruct((1, d), F32),
                     jax.ShapeDtypeStruct((1, half), F32), jax.ShapeDtypeStruct((1, half), F32),
                     jax.ShapeDtypeStruct((d, d), F32)],
        compiler_params=_params("arbitrary"),
    )(o_sb, o_gd, proj, proj, x, target, w_out, sbw, gdw, fw)


def _piece_of_slot(s):
    return jnp.where(s < DPROJ_GDN_SLOT, s, jnp.where(s < DPROJ_GATE_SLOT, s + 1,
                                                     jnp.where(s == DPROJ_GATE_SLOT, 3, 7)))


def _gw_in_call(h_t, dproj8):
    d, t_len = h_t.shape
    n_piece, _, pw = dproj8.shape

    def body(ht_ref, dp_ref, gw_ref):
        gw_ref[...] = jnp.dot(ht_ref[...], dp_ref[0].astype(MXU_DTYPE), preferred_element_type=F32)

    return pl.pallas_call(
        body, name="gw_in",
        grid=(n_piece,),
        in_specs=[pl.BlockSpec((d, t_len), lambda s: (0, 0)),
                  pl.BlockSpec((1, t_len, pw), lambda s: (s, 0, 0))],
        out_specs=pl.BlockSpec((d, pw), lambda s: (0, _piece_of_slot(s))),
        out_shape=jax.ShapeDtypeStruct((d, n_piece * pw), F32),
        compiler_params=_params("arbitrary"),
    )(h_t, dproj8)


def _slot_of_piece(p):
    return jnp.where(p < DPROJ_GDN_SLOT, p, jnp.where(p == 3, DPROJ_GATE_SLOT, jnp.where(p < 7, p - 1, 7)))


def _gw_in_shards_call(h_t, dproj8, dsmall, out_dtype):
    d, t_len = h_t.shape
    n_piece, _, pw = dproj8.shape
    ns = dsmall.shape[1]
    n_pairs = N_DEV // 2

    def body(ht_ref, dp_ref, ds_ref, chip_ref, prev_ref, gates_ref, send_ref, recv_ref, send_sems, recv_sems):
        p = pl.program_id(0)
        x_pos, y_pos, c = lax.axis_index("x"), lax.axis_index("y"), lax.axis_index("c")
        to_sibling = lambda pair: pltpu.make_async_remote_copy(
            src_ref=send_ref.at[pair], dst_ref=recv_ref.at[pair], send_sem=send_sems.at[pair],
            recv_sem=recv_sems.at[pair], device_id=(x_pos, y_pos, 1 - c), device_id_type=_MESH)

        @pl.when(p == 0)
        def _():
            gates_ref[...] = jnp.dot(ht_ref[...], ds_ref[...].astype(MXU_DTYPE), preferred_element_type=F32)

        def emit(s, tail):
            x = jnp.concatenate([prev_ref[...], tail], axis=1)
            y = x if s == 0 else pltpu.roll(x, SHARD_PAD - s, axis=1)
            shard = y[:, :SHARD_COLS].astype(out_dtype)

            @pl.when(c == s % 2)
            def _():
                chip_ref[s // 2] = shard

            @pl.when(c != s % 2)
            def _():
                send_ref[s // 2] = shard
                to_sibling(s // 2).start()

        @pl.when(p < n_piece)
        def _():
            cur = jnp.dot(ht_ref[...], dp_ref[0].astype(MXU_DTYPE), preferred_element_type=F32)
            for s in range(n_piece - 1):
                pl.when(p == s + 1)(functools.partial(emit, s, cur[:, :SHARD_PAD - pw]))
            prev_ref[...] = cur

        @pl.when(p == n_piece)
        def _():
            emit(n_piece - 1, gates_ref[...])
            for pair in range(n_pairs):
                to_sibling(pair).wait_send()
            for pair in range(n_pairs):
                to_sibling(pair).wait_recv()
                chip_ref[pair] = (chip_ref[pair].astype(F32) + recv_ref[pair].astype(F32)).astype(out_dtype)

    shards_of_side = lambda: pltpu.VMEM((n_pairs, d, SHARD_COLS), out_dtype)
    return pl.pallas_call(
        body, name="gw_in",
        grid=(n_piece + 1,),
        in_specs=[pl.BlockSpec((d, t_len), lambda p: (0, 0)),
                  pl.BlockSpec((1, t_len, pw), lambda p: (_slot_of_piece(jnp.minimum(p, n_piece - 1)), 0, 0)),
                  pl.BlockSpec((t_len, ns), lambda p: (0, 0))],
        out_specs=pl.BlockSpec((n_pairs, d, SHARD_COLS), lambda p: (0, 0, 0)),
        out_shape=jax.ShapeDtypeStruct((n_pairs, d, SHARD_COLS), out_dtype),
        scratch_shapes=[pltpu.VMEM((d, pw), F32), pltpu.VMEM((d, ns), F32), shards_of_side(), shards_of_side(),
                        pltpu.SemaphoreType.DMA((n_pairs,)), pltpu.SemaphoreType.DMA((n_pairs,))],
        compiler_params=_params("arbitrary"),
    )(h_t, dproj8, dsmall)


def _gw_small_call(h_t, dsmall, tm=512):
    d, t_len = h_t.shape
    ns = dsmall.shape[1]

    def body(ht_ref, dp_ref, gw_ref):
        @pl.when(pl.program_id(0) == 0)
        def _():
            gw_ref[...] = jnp.zeros_like(gw_ref)

        gw_ref[...] += jnp.dot(ht_ref[...], dp_ref[...].astype(MXU_DTYPE), preferred_element_type=F32)

    return pl.pallas_call(
        body, name="gw_small",
        grid=(t_len // tm,),
        in_specs=[pl.BlockSpec((d, tm), lambda t: (0, t)),
                  pl.BlockSpec((tm, ns), lambda t: (t, 0))],
        out_specs=pl.BlockSpec((d, ns), lambda t: (0, 0)),
        out_shape=jax.ShapeDtypeStruct((d, ns), F32),
        compiler_params=_params("arbitrary"),
    )(h_t, dsmall)


def _dx_call(dproj8, dsmall, w_main, w_small, x, r, dx2, norm_w, chip_scatter=(), tm=256):
    t_len, d = x.shape
    n_piece, _, pw = dproj8.shape
    ns = dsmall.shape[1]
    nx = len(chip_scatter)
    steps = t_len // tm

    def body(*refs):
        dp_ref, ds_ref, wm_ref, ws_ref, x_ref, r_ref, dx2_ref, nw_ref = refs[:8]
        gx_ref, gnw_ref = refs[8 + nx:10 + nx]
        copies = lambda: _chip_copies(refs[8:8 + nx], refs[10 + nx:10 + 2 * nx], *refs[10 + 2 * nx:])
        if nx:
            pl.when(pl.program_id(0) == 0)(lambda: _start_all(copies()))

        @pl.when(pl.program_id(0) == 0)
        def _():
            gnw_ref[...] = jnp.zeros_like(gnw_ref)

        dh = lax.dot_general(ds_ref[...].astype(MXU_DTYPE), ws_ref[...], _NT, preferred_element_type=F32)
        for s, p in enumerate(DPROJ_PIECE_OF_SLOT):
            dh = dh + lax.dot_general(dp_ref[s].astype(MXU_DTYPE), wm_ref[:, p * pw:(p + 1) * pw], _NT,
                                      preferred_element_type=F32)
        xv, rv = x_ref[...], r_ref[...]
        dn = dh * nw_ref[...]
        gx_ref[...] = dx2_ref[...] + rv * dn - xv * ((rv * rv * rv) * jnp.mean(dn * xv, axis=-1, keepdims=True))
        gnw_ref[...] += jnp.sum(dh * xv * rv, axis=0, keepdims=True)
        if nx:
            pl.when(pl.program_id(0) == steps - 1)(lambda: _wait_all(copies()))

    return pl.pallas_call(
        body, name="dx",
        grid=(steps,),
        in_specs=[pl.BlockSpec((n_piece, tm, pw), lambda i: (0, i, 0)),
                  pl.BlockSpec((tm, ns), lambda i: (i, 0)),
                  pl.BlockSpec((d, n_piece * pw), lambda i: (0, 0)),
                  pl.BlockSpec((d, ns), lambda i: (0, 0)),
                  pl.BlockSpec((tm, d), lambda i: (i, 0)),
                  pl.BlockSpec((tm, 1), lambda i: (i, 0)),
                  pl.BlockSpec((tm, d), lambda i: (i, 0)),
                  pl.BlockSpec((1, d), lambda i: (0, 0))] + [_HBM] * nx,
        out_specs=[pl.BlockSpec((tm, d), lambda i: (i, 0)),
                   pl.BlockSpec((1, d), lambda i: (0, 0))] + [_HBM] * nx,
        out_shape=[jax.ShapeDtypeStruct((t_len, d), F32), jax.ShapeDtypeStruct((1, d), F32)]
                  + [jax.ShapeDtypeStruct(a.shape, a.dtype) for a in chip_scatter],
        scratch_shapes=_chip_semaphores(nx) if nx else [],
        compiler_params=_params("arbitrary"),
    )(dproj8, dsmall, w_main, w_small, x, r, dx2, norm_w, *chip_scatter)


def _exchange_call(name, srcs, per_peer):
    n = len(srcs)

    def body(*refs):
        src_refs, out_refs = refs[:n], refs[n:2 * n]
        copies = _direct_copies(src_refs, out_refs, *refs[2 * n:], per_peer)
        _start_all(copies)
        _wait_all(copies)

    hbm = pl.BlockSpec(memory_space=pl.ANY)
    return pl.pallas_call(
        body, name=name,
        in_specs=[hbm] * n, out_specs=[hbm] * n, out_shape=_direct_out_shapes(srcs, per_peer),
        scratch_shapes=_direct_semaphores(n),
    )(*srcs)


def _direct_out_shapes(srcs, per_peer):
    return [jax.ShapeDtypeStruct(s.shape if pp else (N_DEV,) + s.shape, s.dtype) for s, pp in zip(srcs, per_peer)]


def _direct_semaphores(n):
    return [pltpu.SemaphoreType.DMA((n * (N_DEV - 1),)), pltpu.SemaphoreType.DMA((n * (N_DEV - 1),)),
            pltpu.SemaphoreType.DMA((n,))]


def _direct_copies(src_refs, out_refs, send_sems, recv_sems, local_sems, per_peer):
    x, y, c = lax.axis_index("x"), lax.axis_index("y"), lax.axis_index("c")
    me = 4 * x + 2 * y + c
    local, remote = [], []
    for a in range(len(src_refs)):
        mine = src_refs[a].at[me] if per_peer[a] else src_refs[a]
        local.append(pltpu.make_async_copy(mine, out_refs[a].at[me], local_sems.at[a]))
    for k in range(1, N_DEV):
        kx, ky, kc = (k >> 2) & 1, (k >> 1) & 1, k & 1
        px = 1 - x if kx else x
        py = 1 - y if ky else y
        pc = 1 - c if kc else c
        peer = 4 * px + 2 * py + pc
        for a in range(len(src_refs)):
            sem = a * (N_DEV - 1) + (k - 1)
            remote.append(pltpu.make_async_remote_copy(
                src_ref=src_refs[a].at[peer] if per_peer[a] else src_refs[a], dst_ref=out_refs[a].at[me],
                send_sem=send_sems.at[sem], recv_sem=recv_sems.at[sem],
                device_id=(px, py, pc), device_id_type=pl.DeviceIdType.MESH))
    return local, remote


def _start_all(copies):
    local, remote = copies
    for cp in local + remote:
        cp.start()


def _wait_all(copies):
    local, remote = copies
    for cp in remote:
        cp.wait_send()
    for cp in remote:
        cp.wait_recv()
    for cp in local:
        cp.wait()


N_CHIPS = 4
_HBM = pl.BlockSpec(memory_space=pl.ANY)
_MESH = pl.DeviceIdType.MESH


def _gather_call(name, srcs):
    n = len(srcs)
    per = N_DEV - 1

    def body(*refs):
        src_refs, out_refs = refs[:n], refs[n:2 * n]
        send_sems, recv_sems, local_sems = refs[2 * n:]
        x, y, c = lax.axis_index("x"), lax.axis_index("y"), lax.axis_index("c")
        me, sibling = (x, y, c), (x, y, 1 - c)
        x_nbr, y_nbr, diagonal = (1 - x, y), (x, 1 - y), (1 - x, 1 - y)
        held = ((1 - x) * c + x * (1 - c), y * c + (1 - y) * (1 - c))
        onward = (x * c + (1 - x) * (1 - c), (1 - y) * c + y * (1 - c))
        slot = lambda px, py, pc: 4 * px + 2 * py + pc

        def copy(a, k, block, to, from_src=False):
            rows = out_refs[a].at[slot(*block)]
            return pltpu.make_async_remote_copy(
                src_ref=src_refs[a] if from_src else rows, dst_ref=rows,
                send_sem=send_sems.at[a * per + k], recv_sem=recv_sems.at[a * per + k],
                device_id=to, device_id_type=_MESH)

        local = [pltpu.make_async_copy(src_refs[a], out_refs[a].at[slot(*me)], local_sems.at[a]) for a in range(n)]
        started = []

        def start(cp):
            cp.start()
            started.append(cp)

        for cp in local:
            cp.start()
        for a in range(n):
            start(copy(a, 0, me, sibling, True))
            start(copy(a, 1, me, (*x_nbr, c), True))
            start(copy(a, 2, me, (*y_nbr, c), True))
        for a in range(n):
            copy(a, 1, (*x_nbr, c), me).wait_recv()
            copy(a, 2, (*y_nbr, c), me).wait_recv()
            start(copy(a, 3, (*held, c), (*onward, c)))
            start(copy(a, 4, (*x_nbr, c), sibling))
            start(copy(a, 5, (*y_nbr, c), sibling))
        for a in range(n):
            copy(a, 3, (*diagonal, c), me).wait_recv()
            start(copy(a, 6, (*diagonal, c), sibling))
        for a in range(n):
            copy(a, 0, sibling, me).wait_recv()
            for k, chip in ((4, x_nbr), (5, y_nbr), (6, diagonal)):
                copy(a, k, (*chip, 1 - c), me).wait_recv()
        for cp in started:
            cp.wait_send()
        for cp in local:
            cp.wait()

    return pl.pallas_call(
        body, name=name,
        in_specs=[_HBM] * n, out_specs=[_HBM] * n,
        out_shape=[jax.ShapeDtypeStruct((N_DEV,) + s.shape, s.dtype) for s in srcs],
        scratch_shapes=[pltpu.SemaphoreType.DMA((n * per,)), pltpu.SemaphoreType.DMA((n * per,)),
                        pltpu.SemaphoreType.DMA((n,))],
    )(*srcs)


def _sibling_send_call(name, srcs):
    n = len(srcs)

    def body(*refs):
        src_refs, out_refs = refs[:n], refs[n:2 * n]
        send_sems, recv_sems = refs[2 * n:]
        x, y, c = lax.axis_index("x"), lax.axis_index("y"), lax.axis_index("c")
        copies = []
        for a in range(n):
            for ch in range(N_CHIPS):
                copies.append(pltpu.make_async_remote_copy(
                    src_ref=src_refs[a].at[2 * ch + (1 - c)], dst_ref=out_refs[a].at[ch],
                    send_sem=send_sems.at[a * N_CHIPS + ch], recv_sem=recv_sems.at[a * N_CHIPS + ch],
                    device_id=(x, y, 1 - c), device_id_type=_MESH))
        for cp in copies:
            cp.start()
        for cp in copies:
            cp.wait_send()
        for cp in copies:
            cp.wait_recv()

    return pl.pallas_call(
        body, name=name,
        in_specs=[_HBM] * n, out_specs=[_HBM] * n,
        out_shape=[jax.ShapeDtypeStruct((N_CHIPS,) + s.shape[1:], s.dtype) for s in srcs],
        scratch_shapes=[pltpu.SemaphoreType.DMA((n * N_CHIPS,)), pltpu.SemaphoreType.DMA((n * N_CHIPS,))],
    )(*srcs)


def _pair_sum_call(name, parts, from_sibling, tr):
    _, rows, cols = parts.shape

    def body(p_ref, s_ref, o_ref):
        o_ref[...] = (p_ref[...].astype(F32) + s_ref[...].astype(F32)).astype(o_ref.dtype)

    return pl.pallas_call(
        body, name=name,
        grid=(N_CHIPS, rows // tr),
        in_specs=[pl.BlockSpec((1, tr, cols), lambda ch, i: (2 * ch + lax.axis_index("c"), i, 0)),
                  pl.BlockSpec((1, tr, cols), lambda ch, i: (ch, i, 0))],
        out_specs=pl.BlockSpec((1, tr, cols), lambda ch, i: (ch, i, 0)),
        out_shape=jax.ShapeDtypeStruct((N_CHIPS, rows, cols), WIRE_DTYPE),
        compiler_params=_params("arbitrary", "arbitrary"),
    )(parts, from_sibling)


def _chip_exchange_call(name, srcs):
    n = len(srcs)

    def body(*refs):
        copies = _chip_copies(refs[:n], refs[n:2 * n], *refs[2 * n:])
        _start_all(copies)
        _wait_all(copies)

    return pl.pallas_call(
        body, name=name,
        in_specs=[_HBM] * n, out_specs=[_HBM] * n,
        out_shape=[jax.ShapeDtypeStruct(s.shape, s.dtype) for s in srcs],
        scratch_shapes=_chip_semaphores(n),
    )(*srcs)


def _chip_semaphores(n):
    per = N_CHIPS - 1
    return [pltpu.SemaphoreType.DMA((n * per,)), pltpu.SemaphoreType.DMA((n * per,)), pltpu.SemaphoreType.DMA((n,))]


def _chip_copies(src_refs, out_refs, send_sems, recv_sems, local_sems):
    per = N_CHIPS - 1
    x, y, c = lax.axis_index("x"), lax.axis_index("y"), lax.axis_index("c")
    mine = 2 * x + y
    chips = [(1 - x, y), (x, 1 - y), (1 - x, 1 - y)]
    n = len(src_refs)
    local = [pltpu.make_async_copy(src_refs[a].at[mine], out_refs[a].at[mine], local_sems.at[a]) for a in range(n)]
    remote = []
    for a in range(n):
        for j, (px, py) in enumerate(chips):
            remote.append(pltpu.make_async_remote_copy(
                src_ref=src_refs[a].at[2 * px + py], dst_ref=out_refs[a].at[mine],
                send_sem=send_sems.at[a * per + j], recv_sem=recv_sems.at[a * per + j],
                device_id=(px, py, c), device_id_type=_MESH))
    return local, remote


def _adam_call(name, parts, w, m, v, tr):
    rows, cols = w.shape
    n_slots = parts.shape[0]

    def body(p_ref, w_ref, m_ref, v_ref, g_ref, d_ref, nm_ref, nv_ref):
        g = p_ref[0].astype(F32)
        for s in range(1, n_slots):
            g = g + p_ref[s].astype(F32)
        m_new = ADAM_B1 * m_ref[...] + (1.0 - ADAM_B1) * g
        v_new = ADAM_B2 * v_ref[...] + (1.0 - ADAM_B2) * (g * g)
        m_hat = m_new / (1.0 - ADAM_B1 ** ADAM_STEP)
        v_hat = v_new / (1.0 - ADAM_B2 ** ADAM_STEP)
        g_ref[...] = g
        d_ref[...] = -ADAM_LR * (m_hat / (jnp.sqrt(v_hat) + ADAM_EPS) + ADAM_WD * w_ref[...])
        nm_ref[...] = m_new
        nv_ref[...] = v_new

    blk = pl.BlockSpec((tr, cols), lambda i: (i, 0))
    return pl.pallas_call(
        body, name=name,
        grid=(rows // tr,),
        in_specs=[pl.BlockSpec((n_slots, tr, cols), lambda i: (0, i, 0)), blk, blk, blk],
        out_specs=[blk] * 4,
        out_shape=[jax.ShapeDtypeStruct((rows, cols), F32)] * 4,
        compiler_params=_params("arbitrary"),
    )(parts, w, m, v)


N_PIECES = 8
PIECE = 512
SHARD_COLS = 513
SHARD_PAD = 640
RELAYOUT_ROWS = 256


def _from_shards_call(shards):
    _, d, _ = shards.shape
    tr = RELAYOUT_ROWS

    def body(p_ref, m_ref, s_ref):
        lane = lax.broadcasted_iota(jnp.int32, (tr, SHARD_PAD), 1)
        pad = jnp.zeros((tr, SHARD_PAD - SHARD_COLS), F32)
        sh = [jnp.concatenate([p_ref[s].astype(F32), pad], axis=1) for s in range(N_DEV)]
        for p in range(N_PIECES):
            y = sh[p] if p == 0 else pltpu.roll(sh[p], p, axis=1)
            if p > 0:
                y = jnp.where(lane < p, pltpu.roll(sh[p - 1], SHARD_PAD - (SHARD_COLS - p), axis=1), y)
            m_ref[:, p * PIECE:(p + 1) * PIECE] = y[:, :PIECE].astype(m_ref.dtype)
        first_gate = N_PIECES * PIECE - (N_DEV - 1) * SHARD_COLS
        s_ref[...] = pltpu.roll(sh[N_DEV - 1], SHARD_PAD - first_gate, axis=1)[:, :LANES].astype(s_ref.dtype)

    return pl.pallas_call(
        body, name="w_in_from_shards",
        grid=(d // tr,),
        in_specs=[pl.BlockSpec((N_DEV, tr, SHARD_COLS), lambda i: (0, i, 0))],
        out_specs=[pl.BlockSpec((tr, N_PIECES * PIECE), lambda i: (i, 0)), pl.BlockSpec((tr, LANES), lambda i: (i, 0))],
        out_shape=[jax.ShapeDtypeStruct((d, N_PIECES * PIECE), shards.dtype),
                   jax.ShapeDtypeStruct((d, LANES), shards.dtype)],
        compiler_params=_params("arbitrary"),
    )(shards)


def _to_shards_call(main, gates, out_dtype):
    d = main.shape[0]
    tr = RELAYOUT_ROWS

    def body(m_ref, s_ref, o_ref):
        for s in range(N_DEV):
            if s < N_DEV - 1:
                x = m_ref[:, s * PIECE:s * PIECE + SHARD_PAD]
            else:
                x = jnp.concatenate([m_ref[:, s * PIECE:(s + 1) * PIECE], s_ref[...]], axis=1)
            y = x if s == 0 else pltpu.roll(x, SHARD_PAD - s, axis=1)
            o_ref[s] = y[:, :SHARD_COLS].astype(out_dtype)

    return pl.pallas_call(
        body, name="w_in_to_shards",
        grid=(d // tr,),
        in_specs=[pl.BlockSpec((tr, N_PIECES * PIECE), lambda i: (i, 0)), pl.BlockSpec((tr, LANES), lambda i: (i, 0))],
        out_specs=pl.BlockSpec((N_DEV, tr, SHARD_COLS), lambda i: (0, i, 0)),
        out_shape=jax.ShapeDtypeStruct((N_DEV, d, SHARD_COLS), out_dtype),
        compiler_params=_params("arbitrary"),
    )(main, gates)


def _adamw(g, w, m, v):
    m_new = ADAM_B1 * m + (1.0 - ADAM_B1) * g
    v_new = ADAM_B2 * v + (1.0 - ADAM_B2) * (g * g)
    m_hat = m_new / (1.0 - ADAM_B1 ** ADAM_STEP)
    v_hat = v_new / (1.0 - ADAM_B2 ** ADAM_STEP)
    return -ADAM_LR * (m_hat / (jnp.sqrt(v_hat) + ADAM_EPS) + ADAM_WD * w), m_new, v_new


def _adam_small_call(parts, ws, ms, vs):
    n = len(ws)
    n_slots = parts.shape[0]

    def body(*refs):
        p_ref = refs[0]
        w_refs, m_refs, v_refs = refs[1:1 + n], refs[1 + n:1 + 2 * n], refs[1 + 2 * n:1 + 3 * n]
        loss_ref = refs[1 + 3 * n]
        outs = refs[2 + 3 * n:]
        g_all = p_ref[0]
        for s in range(1, n_slots):
            g_all = g_all + p_ref[s]
        loss_ref[...] = g_all[n:n + 1, 0:1]
        for r in range(n):
            size = w_refs[r].shape[1]
            g = g_all[r:r + 1, :size]
            delta, m_new, v_new = _adamw(g, w_refs[r][...], m_refs[r][...], v_refs[r][...])
            for kind, val in enumerate((g, delta, m_new, v_new)):
                outs[kind * n + r][...] = val

    vm = pl.BlockSpec(memory_space=pltpu.VMEM)
    shapes = [jax.ShapeDtypeStruct(w.shape, F32) for w in ws]
    return pl.pallas_call(
        body, name="adam_small",
        in_specs=[vm] * (1 + 3 * n), out_specs=[vm] * (1 + 4 * n),
        out_shape=[jax.ShapeDtypeStruct((1, 1), F32)] + shapes * 4,
    )(parts, *ws, *ms, *vs)


_SMALL_ROWS = ("norm1_w", "final_norm_w", "sb_norm_w", "gdn_norm_w", "gdn_A_log", "gdn_dt_bias", "loss")


def _pack_small(vals, width):
    rows = [jnp.pad(a.reshape(1, -1).astype(F32), ((0, 0), (0, width - a.size))) for a in vals]
    rows += [jnp.zeros((1, width), F32)] * (8 - len(rows))
    return jnp.concatenate(rows, axis=0)


def _device_step(x2d, tgt, w_main, w_small, w_out_full, conv_full, norm1_w, sb_norm_w, gdn_A_log, gdn_dt_bias,
                 gdn_norm_w, final_norm_w, distributed=False):
    t_len, d = x2d.shape
    n_chunks = t_len // CHUNK
    w_main, w_small, w_out_full = (a.astype(MXU_DTYPE) for a in (w_main, w_small, w_out_full))
    w_small_t = w_small[:, :2 * GDN_HEADS].T

    pad_lanes = lambda a, lo: jnp.pad(a.reshape(1, -1), ((0, 0), (lo, LANES - lo - a.size)))
    alog_l, dtb_l = pad_lanes(gdn_A_log, GDN_HEADS), pad_lanes(gdn_dt_bias, GDN_HEADS)
    alog_c, dtb_c = alog_l[:, :8].T, dtb_l[:, :8].T
    sbw = jnp.tile(sb_norm_w, (1, 512 // SB_HEAD_DIM))
    gdw = jnp.tile(gdn_norm_w, (1, 512 // GDN_HEAD_DIM))
    fw = final_norm_w.reshape(1, d)

    if distributed:
        proj, ps, pst, h_t, r1, w_out_g, conv_g = _inproj_call(
            x2d, norm1_w, w_main, w_small, w_small_t, gather=(w_out_full, conv_full))
        w_out_full = w_out_g.reshape(d, d)
        conv_full = conv_g.transpose(1, 0, 2).reshape(CONV_WIDTH, N_DEV * conv_g.shape[2])
    else:
        proj, ps, pst, h_t, r1 = _inproj_call(x2d, norm1_w, w_main, w_small, w_small_t)
    o_sb, sp_total, sb_blocks_run = _sb_fwd_call(proj, t_len)
    gact = _gdn_prep_call(proj, conv_full, t_len)
    beta_l, gcol_l, grow = _gdn_gates_call(ps, pst, alog_l, dtb_l, alog_c, dtb_c, t_len)
    gam_r = grow[GDN_HEADS:2 * GDN_HEADS].reshape(GDN_HEADS, n_chunks, 1, CHUNK)
    o_gd, *gdn_saved = _gdn_fwd_call(gact, beta_l, gcol_l, gam_r, t_len)

    (dx2, d_osb, d_ogd, dproj8, loss_p, g_fw, g_sbw, g_gdw, g_wout) = _post_call(
        o_sb, o_gd, proj, x2d, tgt, w_out_full, sbw, gdw, fw)

    dproj8 = _sb_bwd_call(proj, sp_total, sb_blocks_run, d_osb, dproj8, t_len)
    if distributed:
        d_gact3, d_gates, g_wout = _gdn_bwd_call(gact, beta_l, gcol_l, gam_r, gdn_saved, d_ogd, t_len,
                                                 scatter=(g_wout.reshape(N_DEV, d // N_DEV, d),))
    else:
        d_gact3, d_gates = _gdn_bwd_call(gact, beta_l, gcol_l, gam_r, gdn_saved, d_ogd, t_len)
    dproj8, g_conv = _gdn_prep_bwd_call(proj, conv_full, d_gact3, dproj8, t_len)
    dsmall, g_alog, g_dtb = _gdn_gates_bwd_call(ps, alog_l, dtb_l, d_gates, t_len)

    if distributed:
        chip_partials = _gw_in_shards_call(h_t, dproj8, dsmall, WIRE_DTYPE)
        grad_x, g_n1, g_w_in = _dx_call(dproj8, dsmall, w_main, w_small, x2d, r1, dx2, norm1_w,
                                        chip_scatter=(chip_partials,))
    else:
        grad_x, g_n1 = _dx_call(dproj8, dsmall, w_main, w_small, x2d, r1, dx2, norm1_w)
        g_w_in = (_gw_in_call(h_t, dproj8), _gw_small_call(h_t, dsmall))
    return (loss_p, grad_x, g_n1, g_w_in, g_sbw, g_conv, g_alog, g_dtb, g_gdw, g_wout, g_fw)


def kernel(x, norm1_w, w_in, sb_norm_w, gdn_conv_w, gdn_A_log, gdn_dt_bias, gdn_norm_w, w_out, final_norm_w, loss_target, m_norm1_w, m_w_in, m_sb_norm_w, m_gdn_conv_w, m_gdn_A_log, m_gdn_dt_bias, m_gdn_norm_w, m_w_out, m_final_norm_w, v_norm1_w, v_w_in, v_sb_norm_w, v_gdn_conv_w, v_gdn_A_log, v_gdn_dt_bias, v_gdn_norm_w, v_w_out, v_final_norm_w):
    d = x.shape[2]
    shard_cols = w_in.shape[2]
    conv_cols = gdn_conv_w.shape[2]

    (w_in_g,) = _gather_call("gather_weights", [w_in[0].astype(WIRE_DTYPE)])
    w_main, w_small = _from_shards_call(w_in_g)

    (loss_p, grad_x, g_n1, p_w_in, g_sbw, g_conv, g_alog, g_dtb, g_gdw, p_wout, g_fw) = _device_step(
        x[0], loss_target[0], w_main, w_small, w_out[0].astype(WIRE_DTYPE), gdn_conv_w[0], norm1_w, sb_norm_w,
        gdn_A_log, gdn_dt_bias, gdn_norm_w, final_norm_w, distributed=True)

    g_conv_parts = g_conv.reshape(CONV_WIDTH, N_DEV, conv_cols).transpose(1, 0, 2)
    fold = lambda a, group: a.reshape(-1, group).sum(axis=0)
    small_g = _pack_small([g_n1, g_fw, fold(g_sbw, SB_HEAD_DIM), fold(g_gdw, GDN_HEAD_DIM),
                           g_alog[0, GDN_HEADS:2 * GDN_HEADS], g_dtb[0, GDN_HEADS:2 * GDN_HEADS],
                           loss_p[0, :1]], d)
    p_small, p_conv = _exchange_call("exchange_small", [small_g, g_conv_parts], [False, True])

    r_w_in = _adam_call("adam_w_in", p_w_in, w_in[0], m_w_in[0], v_w_in[0], 256)
    r_wout = _adam_call("adam_w_out", p_wout, w_out[0], m_w_out[0], v_w_out[0], d // N_DEV)
    r_conv = _adam_call("adam_conv", p_conv, gdn_conv_w[0], m_gdn_conv_w[0], v_gdn_conv_w[0], CONV_WIDTH)

    row = lambda a: a.reshape(1, -1)
    n_small = len(_SMALL_ROWS) - 1
    r_small = _adam_small_call(
        p_small,
        [norm1_w, row(final_norm_w), sb_norm_w, gdn_norm_w, gdn_A_log, gdn_dt_bias],
        [m_norm1_w, row(m_final_norm_w), m_sb_norm_w, m_gdn_norm_w, m_gdn_A_log, m_gdn_dt_bias],
        [v_norm1_w, row(v_final_norm_w), v_sb_norm_w, v_gdn_norm_w, v_gdn_A_log, v_gdn_dt_bias])

    def small_out(kind, name):
        out = r_small[1 + kind * n_small + _SMALL_ROWS.index(name)]
        return out.reshape(final_norm_w.shape) if name == "final_norm_w" else out

    def outputs(kind):
        return (small_out(kind, "norm1_w"), r_w_in[kind][None], small_out(kind, "sb_norm_w"), r_conv[kind][None],
                small_out(kind, "gdn_A_log"), small_out(kind, "gdn_dt_bias"), small_out(kind, "gdn_norm_w"),
                r_wout[kind][None], small_out(kind, "final_norm_w"))

    return (r_small[0][0, 0], grad_x[None], *outputs(0), *outputs(1), *outputs(2), *outputs(3))
```

```python
import functools

import jax
import jax.numpy as jnp
from jax import lax
from jax.experimental import pallas as pl
from jax.experimental.pallas import tpu as pltpu

F32 = jnp.float32
MXU_DTYPE = jnp.bfloat16
WIRE_DTYPE = jnp.bfloat16
EXACT = lax.Precision.HIGHEST
EPS = 1e-6
N_DEV = 8
SB_HEAD_DIM = 64
GDN_HEAD_DIM = 128
GDN_HEADS = 4
GDN_CHUNKS_PER_STEP = 4
GDN_BWD_GROUP = 1
CHUNK = 64
CONV_WIDTH = 4
LANES = 128
SB_BLOCK = 128
SB_BQ = 256
VMEM_LIMIT_BYTES = 56 * 1024 * 1024

DPROJ_PIECE_OF_SLOT = (0, 1, 2, 4, 5, 6, 3, 7)
DPROJ_SB_SLOT, DPROJ_GDN_SLOT, DPROJ_GATE_SLOT = 0, 3, 6

ADAM_LR = 0.001
ADAM_B1 = 0.9
ADAM_B2 = 0.999
ADAM_EPS = 1e-08
ADAM_WD = 0.01
ADAM_STEP = 10

_NN = (((1,), (0,)), ((), ()))
_NT = (((1,), (1,)), ((), ()))
_TN = (((0,), (0,)), ((), ()))
_BNN = (((2,), (1,)), ((0,), (0,)))
_BNT = (((2,), (2,)), ((0,), (0,)))
_BTN = (((1,), (1,)), ((0,), (0,)))


def _mx(a, b):
    return jnp.dot(a, b, precision=EXACT, preferred_element_type=F32)


def _split(x):
    hi = x.astype(MXU_DTYPE)
    return hi, (x - hi.astype(F32)).astype(MXU_DTYPE)


def _m3_general(a, b, dims):
    ah, al = _split(a)
    bh, bl = _split(b)
    dot = lambda x, y: lax.dot_general(x, y, dims, preferred_element_type=F32)
    (contract, _), (batch, _) = dims
    free = [ax for ax in range(a.ndim) if ax not in contract and ax not in batch][0]
    m = a.shape[free]
    both = dot(jnp.concatenate([ah, al], axis=free), bh)
    out_axis = len(batch)
    hi_part = lax.slice_in_dim(both, 0, m, axis=out_axis)
    lo_part = lax.slice_in_dim(both, m, 2 * m, axis=out_axis)
    return hi_part + (dot(ah, bl) + lo_part)


def _times_exact(a, b_exact, dims):
    ah, al = _split(a)
    (contract, _), (batch, _) = dims
    free = [ax for ax in range(a.ndim) if ax not in contract and ax not in batch][0]
    m = a.shape[free]
    both = lax.dot_general(jnp.concatenate([ah, al], axis=free), b_exact.astype(MXU_DTYPE), dims,
                           preferred_element_type=F32)
    out_axis = len(batch)
    return lax.slice_in_dim(both, 0, m, axis=out_axis) + lax.slice_in_dim(both, m, 2 * m, axis=out_axis)


def _exact_times(a_exact, b, dims):
    bh, bl = _split(b)
    n = b.shape[-1]
    both = lax.dot_general(a_exact.astype(MXU_DTYPE), jnp.concatenate([bh, bl], axis=-1), dims,
                           preferred_element_type=F32)
    return both[..., :n] + both[..., n:]


def _sigmoid(z):
    return 1.0 / (1.0 + jnp.exp(-z))


def _softplus(z):
    return jnp.maximum(z, 0.0) + jnp.log(1.0 + jnp.exp(-jnp.abs(z)))


def _params(*semantics):
    return pltpu.CompilerParams(dimension_semantics=semantics, vmem_limit_bytes=VMEM_LIMIT_BYTES)


def _inproj_call(x, norm_w, w_main, w_small, w_small_t, gather=(), tm=256):
    t_len, d = x.shape
    n = w_main.shape[1]
    ns = w_small.shape[1]
    nst = w_small_t.shape[0]
    ng = len(gather)
    steps = t_len // tm

    def body(*refs):
        x_ref, nw_ref, wm_ref, ws_ref, wst_ref = refs[:5]
        pm_ref, ps_ref, pst_ref, ht_ref, r_ref = refs[5 + ng:10 + ng]
        copies = lambda: _direct_copies(refs[5:5 + ng], refs[10 + ng:10 + 2 * ng], *refs[10 + 2 * ng:], (False,) * ng)
        if ng:
            pl.when(pl.program_id(0) == 0)(lambda: _start_all(copies()))
        xv = x_ref[...]
        r = lax.rsqrt(jnp.mean(xv * xv, axis=-1, keepdims=True) + EPS)
        h = xv * r * nw_ref[...]
        hb = h.astype(MXU_DTYPE)
        for n0 in range(0, n, 512):
            pm_ref[:, n0:n0 + 512] = jnp.dot(hb, wm_ref[:, n0:n0 + 512], preferred_element_type=F32)
        ps_ref[...] = jnp.dot(hb, ws_ref[...], preferred_element_type=F32)
        pst_ref[...] = lax.dot_general(wst_ref[...], hb, _NT, preferred_element_type=F32)
        ht_ref[...] = h.T.astype(MXU_DTYPE)
        r_ref[...] = r
        if ng:
            pl.when(pl.program_id(0) == steps - 1)(lambda: _wait_all(copies()))

    return pl.pallas_call(
        body, name="inproj",
        grid=(steps,),
        in_specs=[pl.BlockSpec((tm, d), lambda i: (i, 0)),
                  pl.BlockSpec((1, d), lambda i: (0, 0)),
                  pl.BlockSpec((d, n), lambda i: (0, 0)),
                  pl.BlockSpec((d, ns), lambda i: (0, 0)),
                  pl.BlockSpec((nst, d), lambda i: (0, 0))] + [_HBM] * ng,
        out_specs=[pl.BlockSpec((tm, n), lambda i: (i, 0)),
                   pl.BlockSpec((tm, ns), lambda i: (i, 0)),
                   pl.BlockSpec((nst, tm), lambda i: (0, i)),
                   pl.BlockSpec((d, tm), lambda i: (0, i)),
                   pl.BlockSpec((tm, 1), lambda i: (i, 0))] + [_HBM] * ng,
        out_shape=[jax.ShapeDtypeStruct((t_len, n), F32),
                   jax.ShapeDtypeStruct((t_len, ns), F32),
                   jax.ShapeDtypeStruct((nst, t_len), F32),
                   jax.ShapeDtypeStruct((d, t_len), MXU_DTYPE),
                   jax.ShapeDtypeStruct((t_len, 1), F32)] + _direct_out_shapes(gather, (False,) * ng),
        scratch_shapes=_direct_semaphores(ng) if ng else [],
        compiler_params=_params("arbitrary"),
    )(x, norm_w, w_main, w_small, w_small_t, *gather)


def _running_sum_mm(x, tri):
    hi = x.astype(MXU_DTYPE)
    lo = (x - hi.astype(F32)).astype(MXU_DTYPE)
    return jnp.dot(hi, tri, preferred_element_type=F32) + jnp.dot(lo, tri, preferred_element_type=F32)


def _sb_iotas():
    row_i = lax.broadcasted_iota(jnp.int32, (SB_BQ, SB_BLOCK), 0)
    col_i = lax.broadcasted_iota(jnp.int32, (SB_BQ, SB_BLOCK), 1)
    sq_r = lax.broadcasted_iota(jnp.int32, (SB_BLOCK, SB_BLOCK), 0)
    sq_c = lax.broadcasted_iota(jnp.int32, (SB_BLOCK, SB_BLOCK), 1)
    return row_i, col_i, sq_r, sq_c


SB_DIAG_BLOCKS = SB_BQ // SB_BLOCK
SB_EXP_FLOOR = -110.0


def _sb_keys_descending(qi, tile, carry, z_bounds, n_heads, has_free):
    group = SB_DIAG_BLOCKS
    n_free = group * qi
    diag = list(range(group - 1, -1, -1))
    carry = tile([n_free + j for j in diag], [True] * group, carry, [j * SB_BLOCK for j in diag])

    def largest_exponent(c):
        worst = jnp.max(z_bounds[0] - c[1])
        for h in range(1, n_heads):
            worst = jnp.maximum(worst, jnp.max(z_bounds[h] - c[1 + h]))
        return worst

    always = group if has_free else 0

    def cond(state):
        return (state[0] < n_free) & ((state[1] > SB_EXP_FLOOR) | (state[0] < always))

    def body(state):
        first = n_free - 1 - state[0]
        c = tile([first - j for j in range(group)], [False] * group, state[2:])
        return (state[0] + group, largest_exponent(c), *c)

    out = lax.while_loop(cond, body, (jnp.int32(0), largest_exponent(carry), *carry))
    return out[2:], out[0]


def _sb_keys_ascending(qi, n_run, tile, carry, has_free):
    group = SB_DIAG_BLOCKS
    n_free = group * qi
    diag = list(range(group))
    kjs, los, masked = [n_free + j for j in diag], [j * SB_BLOCK for j in diag], [True] * group
    if has_free:
        early = lambda s: [n_free - n_run + group * s + j for j in range(group)]
        carry = lax.fori_loop(0, n_run // group - 1, lambda s, c: tile(early(s), [False] * group, c), carry)
        kjs, los, masked = [n_free - group + j for j in range(group)] + kjs, [0] * group + los, [False] * group + masked
    return tile(kjs, masked, carry, los)


def _sb_fwd_call(proj, t_len):
    nq = t_len // SB_BQ
    scale = float(SB_HEAD_DIM) ** -0.5
    n_pairs = 512 // LANES
    per_pair = LANES // SB_HEAD_DIM

    def body(q_ref, k_ref, v_ref, o_ref, st_ref, nrun_ref):
        lane = lax.broadcasted_iota(jnp.int32, (1, LANES), 1)
        row_i, col_i, sq_r, sq_c = _sb_iotas()
        ge = (sq_r >= sq_c).astype(MXU_DTYPE)
        hms = [((lane // SB_HEAD_DIM) == hh).astype(F32) for hh in range(per_pair)]
        k_sq = k_ref[...] * k_ref[...]
        k_norms = [jnp.sqrt(jnp.max(jnp.sum(k_sq * hm, axis=-1, keepdims=True))) * (1.02 * scale) for hm in hms]

        def q_block(qi, has_free):
            r0 = qi * SB_BQ if isinstance(qi, int) else pl.multiple_of(qi * SB_BQ, SB_BQ)
            rows = pl.ds(r0, SB_BQ)
            q_all = q_ref[rows, :]
            qms = [(q_all * (hm * scale)).astype(MXU_DTYPE) for hm in hms]
            z_bounds = [jnp.sqrt(jnp.sum(q_all * q_all * hm, axis=-1, keepdims=True)) * kn
                        for hm, kn in zip(hms, k_norms)]

            def tile(kjs, masked, kc, los=None):
                heads = range(per_pair)
                los = los or [0] * len(kjs)
                pairs = [(t, h) for t in range(len(kjs)) for h in heads]
                add_rows = lambda full, lo, part: full + part if lo == 0 else jnp.concatenate(
                    [full[:lo], full[lo:] + part], axis=0)
                acc, cs = kc[0], list(kc[1:])
                s0s = [kj * SB_BLOCK if isinstance(kj, int) else pl.multiple_of(kj * SB_BLOCK, SB_BLOCK) for kj in kjs]
                kbs = [k_ref[pl.ds(s0, SB_BLOCK), :].astype(MXU_DTYPE) for s0 in s0s]
                v_alls = [v_ref[pl.ds(s0, SB_BLOCK), :] for s0 in s0s]
                vms = {(t, h): (v_alls[t] * hms[h]).astype(MXU_DTYPE) for t, h in pairs}
                zs = {(t, h): lax.dot_general(qms[h][los[t]:], kbs[t], _NT, preferred_element_type=F32)
                      for t, h in pairs}
                masks = [(col_i[lo:] + s0) < (row_i[lo:] + r0) if m else None for m, lo, s0 in zip(masked, los, s0s)]
                keep = lambda t, a: a if masks[t] is None else jnp.where(masks[t], a, 0.0)
                sps = {(t, h): keep(t, _softplus(zs[t, h])) for t, h in pairs}
                sums = {p: _running_sum_mm(sps[p], ge) for p in pairs}
                mass = {}
                for t, h in pairs:
                    mass[t, h] = cs[h] if t == 0 else add_rows(
                        mass[t - 1, h], los[t - 1], jnp.sum(sps[t - 1, h], axis=-1, keepdims=True))
                ws = {(t, h): keep(t, jnp.exp(zs[t, h] - (sums[t, h] + mass[t, h][los[t]:]))) for t, h in pairs}
                for t, h in pairs:
                    acc = add_rows(acc, los[t], jnp.dot(ws[t, h].astype(MXU_DTYPE), vms[t, h],
                                                        preferred_element_type=F32))
                last = len(kjs) - 1
                cs = [add_rows(mass[last, h], los[last], jnp.sum(sps[last, h], axis=-1, keepdims=True)) for h in heads]
                return (acc, *cs)

            zero_col = jnp.zeros((SB_BQ, 1), F32)
            out, n_run = _sb_keys_descending(
                qi, tile, (jnp.zeros((SB_BQ, LANES), F32),) + (zero_col,) * per_pair, z_bounds, per_pair, has_free)
            o_ref[rows, :] = out[0]
            masses = jnp.zeros((SB_BQ, LANES), F32)
            for hh in range(per_pair):
                masses = jnp.where(lane == hh, out[1 + hh], masses)
            st_ref[rows, :] = masses
            nrun_ref[pl.program_id(0), qi] = n_run

        q_block(0, False)
        lax.fori_loop(1, nq, lambda qi, carry: (q_block(qi, True), carry)[1], 0)

    return pl.pallas_call(
        body, name="sb_fwd",
        grid=(n_pairs,),
        in_specs=[pl.BlockSpec((t_len, LANES), lambda p: (0, p)),
                  pl.BlockSpec((t_len, LANES), lambda p: (0, n_pairs + p)),
                  pl.BlockSpec((t_len, LANES), lambda p: (0, 2 * n_pairs + p))],
        out_specs=[pl.BlockSpec((t_len, LANES), lambda p: (0, p)),
                   pl.BlockSpec((t_len, LANES), lambda p: (0, p)),
                   pl.BlockSpec(memory_space=pltpu.SMEM)],
        out_shape=[jax.ShapeDtypeStruct((t_len, 512), F32),
                   jax.ShapeDtypeStruct((t_len, n_pairs * LANES), F32),
                   jax.ShapeDtypeStruct((n_pairs, nq), jnp.int32)],
        compiler_params=_params("arbitrary"),
    )(proj, proj, proj)


def _sb_bwd_call(proj, sp_total, n_run_all, d_o, dproj, t_len):
    nq = t_len // SB_BQ
    scale = float(SB_HEAD_DIM) ** -0.5
    n_pairs = 512 // LANES
    per_pair = LANES // SB_HEAD_DIM

    def body(q_ref, k_ref, v_ref, st_ref, nrun_ref, do_ref, dproj_in_ref, d_ref):
        lane = lax.broadcasted_iota(jnp.int32, (1, LANES), 1)
        row_i, col_i, sq_r, sq_c = _sb_iotas()
        lt = (sq_r < sq_c).astype(MXU_DTYPE)
        le = (sq_r <= sq_c).astype(MXU_DTYPE)
        hms = [((lane // SB_HEAD_DIM) == hh).astype(F32) for hh in range(per_pair)]
        d_ref[1] = jnp.zeros((t_len, LANES), F32)
        d_ref[2] = jnp.zeros((t_len, LANES), F32)

        def q_block(qi, has_free):
            r0 = qi * SB_BQ if isinstance(qi, int) else pl.multiple_of(qi * SB_BQ, SB_BQ)
            rows = pl.ds(r0, SB_BQ)
            q_all, do_all = q_ref[rows, :], do_ref[rows, :]
            qms = [(q_all * (hm * scale)).astype(MXU_DTYPE) for hm in hms]
            doms = [(do_all * hm).astype(MXU_DTYPE) for hm in hms]
            masses = st_ref[rows, :]
            totals = [jnp.sum(jnp.where(lane == hh, masses, 0.0), axis=-1, keepdims=True) for hh in range(per_pair)]

            def tile(kjs, masked, kc, los=None):
                heads = range(per_pair)
                los = los or [0] * len(kjs)
                pairs = [(t, h) for t in range(len(kjs)) for h in heads]
                add_rows = lambda full, lo, part: full + part if lo == 0 else jnp.concatenate(
                    [full[:lo], full[lo:] + part], axis=0)
                rsum = lambda a: jnp.sum(a, axis=-1, keepdims=True)
                dq, cls, gls = kc[0], list(kc[1:1 + per_pair]), list(kc[1 + per_pair:])
                s0s = [kj * SB_BLOCK if isinstance(kj, int) else pl.multiple_of(kj * SB_BLOCK, SB_BLOCK) for kj in kjs]
                k_alls = [k_ref[pl.ds(s0, SB_BLOCK), :] for s0 in s0s]
                v_alls = [v_ref[pl.ds(s0, SB_BLOCK), :] for s0 in s0s]
                kbs = [k_all.astype(MXU_DTYPE) for k_all in k_alls]
                vms = {(t, h): (v_alls[t] * hms[h]).astype(MXU_DTYPE) for t, h in pairs}
                kms = {(t, h): (k_alls[t] * (hms[h] * scale)).astype(MXU_DTYPE) for t, h in pairs}
                q_live = {(t, h): qms[h][los[t]:] for t, h in pairs}
                do_live = {(t, h): doms[h][los[t]:] for t, h in pairs}
                zs = {p: lax.dot_general(q_live[p], kbs[p[0]], _NT, preferred_element_type=F32) for p in pairs}
                das = {p: lax.dot_general(do_live[p], vms[p], _NT, preferred_element_type=F32) for p in pairs}
                masks = [(col_i[lo:] + s0) < (row_i[lo:] + r0) if m else None for m, lo, s0 in zip(masked, los, s0s)]
                keep = lambda t, a: a if masks[t] is None else jnp.where(masks[t], a, 0.0)
                sp_alls = {p: _softplus(zs[p]) for p in pairs}
                sps = {(t, h): keep(t, sp_alls[t, h]) for t, h in pairs}
                lefts = {p: _running_sum_mm(sps[p], lt) for p in pairs}
                cl = {}
                for t, h in pairs:
                    cl[t, h] = cls[h] if t == 0 else add_rows(cl[t - 1, h], los[t - 1], rsum(sps[t - 1, h]))
                ws = {(t, h): keep(t, jnp.exp(zs[t, h] - ((totals[h] - cl[t, h])[los[t]:] - lefts[t, h])))
                      for t, h in pairs}
                gs = {p: das[p] * ws[p] for p in pairs}
                g_sums = {p: _running_sum_mm(gs[p], le) for p in pairs}
                gl = {}
                for t, h in pairs:
                    gl[t, h] = gls[h] if t == 0 else add_rows(gl[t - 1, h], los[t - 1], rsum(gs[t - 1, h]))
                dzs = {(t, h): keep(t, gs[t, h] - jnp.exp(zs[t, h] - sp_alls[t, h]) * (gl[t, h][los[t]:] + g_sums[t, h])
                               ).astype(MXU_DTYPE) for t, h in pairs}
                for t in range(len(kjs)):
                    dk_t = jnp.zeros((SB_BLOCK, LANES), F32)
                    dv_t = jnp.zeros((SB_BLOCK, LANES), F32)
                    for h in heads:
                        dq = add_rows(dq, los[t], jnp.dot(dzs[t, h], kms[t, h], preferred_element_type=F32))
                        dk_t = dk_t + lax.dot_general(dzs[t, h], q_live[t, h], _TN, preferred_element_type=F32)
                        dv_t = dv_t + lax.dot_general(ws[t, h].astype(MXU_DTYPE), do_live[t, h], _TN,
                                                      preferred_element_type=F32)
                    d_ref[1, pl.ds(s0s[t], SB_BLOCK), :] += dk_t
                    d_ref[2, pl.ds(s0s[t], SB_BLOCK), :] += dv_t
                last = len(kjs) - 1
                cls = [add_rows(cl[last, h], los[last], rsum(sps[last, h])) for h in heads]
                gls = [add_rows(gl[last, h], los[last], rsum(gs[last, h])) for h in heads]
                return (dq, *cls, *gls)

            zero_col = jnp.zeros((SB_BQ, 1), F32)
            out = _sb_keys_ascending(qi, nrun_ref[pl.program_id(0), qi], tile,
                                     (jnp.zeros((SB_BQ, LANES), F32),) + (zero_col,) * (2 * per_pair), has_free)
            d_ref[0, rows, :] = out[0]

        q_block(0, False)
        lax.fori_loop(1, nq, lambda qi, carry: (q_block(qi, True), carry)[1], 0)

    col = lambda off: pl.BlockSpec((t_len, LANES), lambda p: (0, off + p))
    return pl.pallas_call(
        body, name="sb_bwd",
        grid=(n_pairs,),
        in_specs=[col(0), col(n_pairs), col(2 * n_pairs),
                  pl.BlockSpec((t_len, LANES), lambda p: (0, p)),
                  pl.BlockSpec(memory_space=pltpu.SMEM), col(0), _HBM],
        out_specs=pl.BlockSpec((3, t_len, LANES), lambda p: (DPROJ_SB_SLOT // 3, 0, p)),
        out_shape=jax.ShapeDtypeStruct(dproj.shape, dproj.dtype),
        input_output_aliases={6: 0},
        compiler_params=_params("arbitrary"),
    )(proj, proj, proj, sp_total, n_run_all, d_o, dproj)


def _conv_taps(xin, rows, t_len):
    taps = []
    for i in range(CONV_WIDTH):
        shift = CONV_WIDTH - 1 - i
        if shift == 0:
            taps.append(xin)
        else:
            taps.append(jnp.where(rows >= shift, pltpu.roll(xin, shift, axis=0), 0.0))
    return taps


def _gdn_prep_body_common(x_ref, w_ref, t_len):
    j = pl.program_id(0)
    xin = x_ref[...]
    rows = lax.broadcasted_iota(jnp.int32, (t_len, LANES), 0)
    taps = _conv_taps(xin, rows, t_len)
    pre = taps[0] * w_ref[0:1, :]
    for i in range(1, CONV_WIDTH):
        pre = pre + taps[i] * w_ref[i:i + 1, :]
    sg = _sigmoid(pre)
    act = pre * sg
    is_qk = j < 2 * GDN_HEADS
    nrm = jnp.where(is_qk, lax.rsqrt(jnp.sum(act * act, axis=-1, keepdims=True) + EPS), 1.0)
    sc = jnp.where(j < GDN_HEADS, float(GDN_HEAD_DIM) ** -0.5, 1.0)
    return j, rows, taps, pre, sg, act, is_qk, nrm, sc


def _gdn_prep_call(proj, conv_w, t_len):
    first = 2048 // LANES

    def body(x_ref, w_ref, out_ref):
        _, _, _, _, _, act, _, nrm, sc = _gdn_prep_body_common(x_ref, w_ref, t_len)
        out_ref[...] = act * nrm * sc

    return pl.pallas_call(
        body, name="gdn_prep",
        grid=(3 * GDN_HEADS,),
        in_specs=[pl.BlockSpec((t_len, LANES), lambda j: (0, first + j)),
                  pl.BlockSpec((CONV_WIDTH, LANES), lambda j: (0, j))],
        out_specs=pl.BlockSpec((t_len, LANES), lambda j: (0, j)),
        out_shape=jax.ShapeDtypeStruct((t_len, 3 * 512), F32),
        compiler_params=_params("arbitrary"),
    )(proj, conv_w)


def _gdn_prep_bwd_call(proj, conv_w, d_act3, dproj, t_len):
    first = 2048 // LANES

    def body(x_ref, w_ref, d_ref, dproj_in_ref, dx_ref, dw_ref):
        _, rows, taps, pre, sg, act, is_qk, nrm, sc = _gdn_prep_body_common(x_ref, w_ref, t_len)
        d_out = d_ref[0]
        dn = d_out * sc
        d_norm = nrm * dn - act * (nrm * nrm * nrm) * jnp.sum(dn * act, axis=-1, keepdims=True)
        d_act = jnp.where(is_qk, d_norm, d_out)
        d_pre = d_act * sg * (1.0 + pre * (1.0 - sg))
        dx = d_pre * w_ref[CONV_WIDTH - 1:CONV_WIDTH, :]
        dw_ref[CONV_WIDTH - 1:CONV_WIDTH, :] = jnp.sum(d_pre * taps[CONV_WIDTH - 1], axis=0, keepdims=True)
        for i in range(CONV_WIDTH - 1):
            shift = CONV_WIDTH - 1 - i
            up = jnp.where(rows < t_len - shift, pltpu.roll(d_pre, t_len - shift, axis=0), 0.0)
            dx = dx + up * w_ref[i:i + 1, :]
            dw_ref[i:i + 1, :] = jnp.sum(d_pre * taps[i], axis=0, keepdims=True)
        dx_ref[0] = dx

    return pl.pallas_call(
        body, name="gdn_prep_bwd",
        grid=(3 * GDN_HEADS,),
        in_specs=[pl.BlockSpec((t_len, LANES), lambda j: (0, first + j)),
                  pl.BlockSpec((CONV_WIDTH, LANES), lambda j: (0, j)),
                  pl.BlockSpec((1, t_len, LANES), lambda j: (j // GDN_HEADS, 0, j % GDN_HEADS)), _HBM],
        out_specs=[pl.BlockSpec((1, t_len, LANES), lambda j: (DPROJ_GDN_SLOT + j // GDN_HEADS, 0, j % GDN_HEADS)),
                   pl.BlockSpec((CONV_WIDTH, LANES), lambda j: (0, j))],
        out_shape=[jax.ShapeDtypeStruct(dproj.shape, dproj.dtype),
                   jax.ShapeDtypeStruct((CONV_WIDTH, 3 * 512), F32)],
        input_output_aliases={3: 0},
        compiler_params=_params("arbitrary"),
    )(proj, conv_w, d_act3, dproj)


def _chunk_cumsum_matrix():
    r = lax.broadcasted_iota(jnp.int32, (LANES, LANES), 0)
    c = lax.broadcasted_iota(jnp.int32, (LANES, LANES), 1)
    return ((r <= c) & ((r // CHUNK) == (c // CHUNK))).astype(F32)


def _gdn_gates_call(ps, pst, alog_l, dtb_l, alog_c, dtb_c, t_len):
    def body(ps_ref, pst_ref, al_ref, dl_ref, ac_ref, dc_ref, beta_ref, gcol_ref, grow_ref):
        upper = _chunk_cumsum_matrix()
        lower = upper.T
        psv = ps_ref[...]
        beta_ref[...] = _sigmoid(psv)
        g_l = -jnp.exp(al_ref[...]) * _softplus(psv + dl_ref[...])
        g_r = -jnp.exp(ac_ref[...]) * _softplus(pst_ref[...] + dc_ref[...])
        for w in range(t_len // LANES):
            sl = slice(w * LANES, (w + 1) * LANES)
            gcol_ref[sl, :] = _mx(lower, g_l[sl, :])
            grow_ref[:, sl] = _mx(g_r[:, sl], upper)

    vm = pl.BlockSpec(memory_space=pltpu.VMEM)
    return pl.pallas_call(
        body, name="gdn_gates",
        in_specs=[vm] * 6, out_specs=[vm] * 3,
        out_shape=[jax.ShapeDtypeStruct((t_len, LANES), F32),
                   jax.ShapeDtypeStruct((t_len, LANES), F32),
                   jax.ShapeDtypeStruct((8, t_len), F32)],
        compiler_params=pltpu.CompilerParams(vmem_limit_bytes=VMEM_LIMIT_BYTES),
    )(ps, pst, alog_l, dtb_l, alog_c, dtb_c)


def _gdn_gates_bwd_call(ps, alog_l, dtb_l, d_l, t_len):
    def body(ps_ref, al_ref, dl_ref, d_ref, dps_ref, gal_ref, gdt_ref):
        lane = lax.broadcasted_iota(jnp.int32, (1, LANES), 1)
        psv = ps_ref[...]
        dv = d_ref[...]
        beta = _sigmoid(psv)
        ea = jnp.exp(al_ref[...])
        arg = psv + dl_ref[...]
        g = -ea * _softplus(arg)
        d_a = dv * (-ea) * _sigmoid(arg)
        is_a = (lane >= GDN_HEADS) & (lane < 2 * GDN_HEADS)
        dps_ref[...] = jnp.where(lane < GDN_HEADS, dv * beta * (1.0 - beta), jnp.where(is_a, d_a, 0.0))
        gdt_ref[...] = jnp.where(is_a, jnp.sum(d_a, axis=0, keepdims=True), 0.0)
        gal_ref[...] = jnp.where(is_a, jnp.sum(dv * g, axis=0, keepdims=True), 0.0)

    vm = pl.BlockSpec(memory_space=pltpu.VMEM)
    return pl.pallas_call(
        body, name="gdn_gates_bwd",
        in_specs=[vm] * 4, out_specs=[vm] * 3,
        out_shape=[jax.ShapeDtypeStruct((t_len, LANES), F32),
                   jax.ShapeDtypeStruct((1, LANES), F32),
                   jax.ShapeDtypeStruct((1, LANES), F32)],
        compiler_params=pltpu.CompilerParams(vmem_limit_bytes=VMEM_LIMIT_BYTES),
    )(ps, alog_l, dtb_l, d_l)


def _bm(a, b):
    return _m3_general(a, b, _BNN)


def _bm_nt(a, b):
    return _m3_general(a, b, _BNT)


def _bm_tn(a, b):
    return _m3_general(a, b, _BTN)


def _heads_of(ref, rows):
    return jnp.stack([ref[rows, h * GDN_HEAD_DIM:(h + 1) * GDN_HEAD_DIM] for h in range(GDN_HEADS)])


def _chunk_terms(q_ref, k_ref, v_ref, b_ref, gc_ref, gr_ref, c, incl, strict, n=1, scores=True):
    r0 = c * CHUNK if isinstance(c, int) else pl.multiple_of(c * CHUNK, CHUNK)
    rows = pl.ds(r0, n * CHUNK)
    per_chunk = lambda x: x.reshape(GDN_HEADS * n, CHUNK, x.shape[-1])
    q, k, v = (per_chunk(_heads_of(ref, rows)) for ref in (q_ref, k_ref, v_ref))
    lane_ids = lax.broadcasted_iota(jnp.int32, (1, LANES), 1)
    pick = lambda slab, first: jnp.stack([jnp.sum(jnp.where(lane_ids == first + h, slab, 0.0), axis=-1, keepdims=True)
                                          for h in range(GDN_HEADS)])
    b = per_chunk(pick(b_ref[rows, :], 0))
    gc = per_chunk(pick(gc_ref[rows, :], GDN_HEADS))
    gr = gr_ref[:, c] if n == 1 else gr_ref[:, c:c + n].reshape(GDN_HEADS * n, 1, CHUNK)
    dm = jnp.where(incl, jnp.exp(jnp.where(incl, gc - gr, 0.0)), 0.0)
    kb = k * b
    vb = v * b
    e = jnp.exp(gc)
    a = p = None
    if scores:
        kk_qk = _bm_nt(jnp.concatenate([kb, q], axis=1), k)
        a = jnp.where(strict, kk_qk[:, :CHUNK] * dm, 0.0)
        p = jnp.where(incl, kk_qk[:, CHUNK:] * dm, 0.0)
    gl = gc[:, CHUNK - 1:CHUNK, :]
    eg = jnp.exp(gl - gc)
    return rows, q, k, v, b, gc, dm, kb, vb, e, a, p, gl, eg


def _unit_lower_inverse(a, eye):
    x = -a
    tm = eye + x
    xp = _bm(x, x)
    for _ in range(4):
        both = _bm(jnp.concatenate([xp, tm], axis=1), xp)
        tm = tm + both[:, CHUNK:]
        xp = both[:, :CHUNK]
    return tm + _bm(tm, xp)


def _gdn_specs(t_len, n_chunks, reverse):
    cps = GDN_CHUNKS_PER_STEP
    steps = n_chunks // cps
    at = (lambda g: steps - 1 - g) if reverse else (lambda g: g)
    rows_blk = lambda width, part=0: pl.BlockSpec((cps * CHUNK, width), lambda g: (at(g), part))
    gate_r = pl.BlockSpec((GDN_HEADS, cps, 1, CHUNK), lambda g: (0, at(g), 0, 0))
    per_chunk = lambda r, c: pl.BlockSpec((GDN_HEADS, cps, r, c), lambda g: (0, at(g), 0, 0))
    return cps, steps, rows_blk, gate_r, per_chunk


def _gdn_fwd_call(gact, beta_c, gam_c, gam_r, t_len):
    n_chunks = t_len // CHUNK
    dk = GDN_HEAD_DIM
    width = GDN_HEADS * dk
    cps, steps, rows_blk, gate_r, per_chunk = _gdn_specs(t_len, n_chunks, False)

    def body(q_ref, k_ref, v_ref, b_ref, gc_ref, gr_ref, o_ref, s_ref, t_ref, a_ref, p_ref, uw_ref, vn_ref, state_ref):
        row = lax.broadcasted_iota(jnp.int32, (CHUNK, CHUNK), 0)
        col = lax.broadcasted_iota(jnp.int32, (CHUNK, CHUNK), 1)
        incl, strict = row >= col, row > col
        eye = (row == col).astype(F32)

        @pl.when(pl.program_id(0) == 0)
        def _():
            state_ref[...] = jnp.zeros_like(state_ref)

        _, q, k, v, b, gc, dm, kb, vb, e, a, p, gl, eg = _chunk_terms(
            q_ref, k_ref, v_ref, b_ref, gc_ref, gr_ref, 0, incl, strict, cps)
        tm = _unit_lower_inverse(a, eye)
        uw = _bm(tm, jnp.concatenate([vb, kb * e], axis=2))
        w_qe = jnp.concatenate([uw[:, :, dk:], q * e], axis=1)
        u, kd, decay = uw[:, :, :dk], k * eg, jnp.exp(gl)
        per_chunk_block = lambda x: x.reshape(GDN_HEADS, cps, CHUNK, CHUNK)
        t_ref[...], a_ref[...], p_ref[...] = per_chunk_block(tm), per_chunk_block(a), per_chunk_block(p)
        uw_heads = uw.reshape(GDN_HEADS, cps * CHUNK, 2 * dk)
        for h in range(GDN_HEADS):
            uw_ref[:, h * 2 * dk:(h + 1) * 2 * dk] = uw_heads[h]

        of_chunk = lambda x, c: jnp.stack([x[h * cps + c] for h in range(GDN_HEADS)])
        s = state_ref[...]
        for c in range(cps):
            ws_qs = _bm(of_chunk(w_qe, c), s)
            vn = of_chunk(u, c) - ws_qs[:, :CHUNK]
            o = ws_qs[:, CHUNK:] + _bm(of_chunk(p, c), vn)
            for h in range(GDN_HEADS):
                o_ref[c * CHUNK:(c + 1) * CHUNK, h * dk:(h + 1) * dk] = o[h]
                vn_ref[c * CHUNK:(c + 1) * CHUNK, h * dk:(h + 1) * dk] = vn[h]
            s_ref[:, c] = s
            s = s * of_chunk(decay, c) + _bm_tn(of_chunk(kd, c), vn)
        state_ref[...] = s

    scores = jax.ShapeDtypeStruct((GDN_HEADS, n_chunks, CHUNK, CHUNK), F32)
    return pl.pallas_call(
        body, name="gdn_fwd",
        grid=(steps,),
        in_specs=[rows_blk(width, 0), rows_blk(width, 1), rows_blk(width, 2), rows_blk(LANES), rows_blk(LANES), gate_r],
        out_specs=[rows_blk(width), per_chunk(dk, dk), per_chunk(CHUNK, CHUNK), per_chunk(CHUNK, CHUNK),
                   per_chunk(CHUNK, CHUNK), rows_blk(2 * width), rows_blk(width)],
        out_shape=[jax.ShapeDtypeStruct((t_len, width), F32),
                   jax.ShapeDtypeStruct((GDN_HEADS, n_chunks, dk, dk), F32), scores, scores, scores,
                   jax.ShapeDtypeStruct((t_len, 2 * width), F32), jax.ShapeDtypeStruct((t_len, width), F32)],
        scratch_shapes=[pltpu.VMEM((GDN_HEADS, dk, dk), F32)],
        compiler_params=_params("arbitrary"),
    )(gact, gact, gact, beta_c, gam_c, gam_r)


def _gdn_bwd_call(gact, beta_c, gam_c, gam_r, saved, d_o, t_len, scatter=()):
    n_chunks = t_len // CHUNK
    dk = GDN_HEAD_DIM
    width = GDN_HEADS * dk
    cps, steps, rows_blk, gate_r, per_chunk = _gdn_specs(t_len, n_chunks, True)
    nx = len(scatter)
    n_in = 13

    def body(*refs):
        q_ref, k_ref, v_ref, b_ref, gc_ref, gr_ref = refs[:6]
        saved_refs, do_ref = refs[6:12], refs[12]
        d_ref, dgate_ref = refs[n_in + nx:n_in + 2 + nx]
        dstate_ref = refs[n_in + 2 + 2 * nx]
        copies = lambda: _direct_copies(refs[n_in:n_in + nx], refs[n_in + 2 + nx:n_in + 2 + 2 * nx],
                                        *refs[n_in + 3 + 2 * nx:], (True,) * nx)
        if nx:
            pl.when(pl.program_id(0) == 0)(lambda: _start_all(copies()))
        row = lax.broadcasted_iota(jnp.int32, (CHUNK, CHUNK), 0)
        col = lax.broadcasted_iota(jnp.int32, (CHUNK, CHUNK), 1)
        incl, strict = row >= col, row > col
        ng = GDN_BWD_GROUP
        nb = GDN_HEADS * ng
        upper = jnp.broadcast_to((row <= col).astype(F32), (nb, CHUNK, CHUNK))
        ones = jnp.ones((nb, CHUNK, LANES), F32)
        last_row = lax.broadcasted_iota(jnp.int32, (CHUNK, 1), 0) == CHUNK - 1
        lane_ids = lax.broadcasted_iota(jnp.int32, (1, LANES), 1)
        rsum = lambda m: jnp.sum(m, axis=-1, keepdims=True)
        total = lambda m: jnp.sum(rsum(m), axis=1, keepdims=True)
        of_chunk = lambda x, c: jnp.stack([x[h * ng + c] for h in range(GDN_HEADS)])

        @pl.when(pl.program_id(0) == 0)
        def _():
            dstate_ref[...] = jnp.zeros_like(dstate_ref)

        for c0 in range(cps - ng, -1, -ng):
            group(c0, q_ref, k_ref, v_ref, b_ref, gc_ref, gr_ref, saved_refs, do_ref, d_ref, dgate_ref, dstate_ref,
                  incl, strict, upper, ones, last_row, lane_ids, rsum, total, of_chunk)
        if nx:
            pl.when(pl.program_id(0) == steps - 1)(lambda: _wait_all(copies()))

    def group(c0, q_ref, k_ref, v_ref, b_ref, gc_ref, gr_ref, saved_refs, do_ref, d_ref, dgate_ref, dstate_ref,
              incl, strict, upper, ones, last_row, lane_ids, rsum, total, of_chunk):
        ng = GDN_BWD_GROUP
        nb = GDN_HEADS * ng
        rows = pl.ds(c0 * CHUNK, ng * CHUNK)
        s_ref, t_ref, a_ref, p_ref, uw_ref, vn_ref = saved_refs
        _, q, k, v, b, gc, dm, kb, vb, e, _, _, gl, eg = _chunk_terms(
            q_ref, k_ref, v_ref, b_ref, gc_ref, gr_ref, c0, incl, strict, ng, scores=False)
        s = s_ref[:, c0:c0 + ng].reshape(nb, dk, dk)
        tm = t_ref[:, c0:c0 + ng].reshape(nb, CHUNK, CHUNK)
        a = a_ref[:, c0:c0 + ng].reshape(nb, CHUNK, CHUNK)
        p = p_ref[:, c0:c0 + ng].reshape(nb, CHUNK, CHUNK)
        d_out = _heads_of(do_ref, rows).reshape(nb, CHUNK, dk)
        vn = _heads_of(vn_ref, rows).reshape(nb, CHUNK, dk)
        uw = jnp.stack([uw_ref[rows, h * 2 * dk:(h + 1) * 2 * dk] for h in range(GDN_HEADS)]).reshape(nb, CHUNK, 2 * dk)
        u, w = uw[:, :, :dk], uw[:, :, dk:]
        el = jnp.exp(gl)
        kbe = kb * e
        qe = q * e
        kd = k * eg
        pt_do = _bm_tn(p, d_out)
        qet_do = _bm_tn(qe, d_out)

        ds = dstate_ref[...]
        d_vn_c, ds_c = [None] * ng, [None] * ng
        for c in range(ng - 1, -1, -1):
            ds_c[c] = ds
            d_vn_c[c] = of_chunk(pt_do, c) + _bm(of_chunk(kd, c), ds)
            ds = of_chunk(el, c) * ds + of_chunk(qet_do, c) - _bm_tn(of_chunk(w, c), d_vn_c[c])
        dstate_ref[...] = ds
        by_chunk = lambda xs: jnp.stack([xs[c][h] for h in range(GDN_HEADS) for c in range(ng)])
        d_vn, ds = by_chunk(d_vn_c), by_chunk(ds_c)

        on_s = _bm_nt(jnp.concatenate([d_out, d_vn], axis=1), s)
        d_qe, d_w = on_s[:, :CHUNK], -on_s[:, CHUNK:]
        d_p = jnp.where(incl, _bm_nt(d_out, vn), 0.0)
        d_kd = _bm_nt(vn, ds)
        d_both = _bm_tn(tm, jnp.concatenate([d_vn, d_w], axis=2))
        d_vb, d_kbe = d_both[:, :, :dk], d_both[:, :, dk:]
        d_a = -jnp.where(strict, _bm_nt(d_both, uw), 0.0)
        m = d_a * dm
        n = d_p * dm
        on_k = _bm(jnp.concatenate([m, n], axis=1), k)
        d_kb = on_k[:, :CHUNK] + d_kbe * e
        d_q = on_k[:, CHUNK:] + d_qe * e
        d_k = (_bm_tn(jnp.concatenate([m, n], axis=1), jnp.concatenate([kb, q], axis=1))
               + d_kd * eg + b * d_kb)
        d_v = b * d_vb
        r = d_a * a + d_p * p
        kd_term = rsum(d_kd * kd)
        d_gl = total(ds * s) * el + jnp.sum(kd_term, axis=1, keepdims=True)
        d_gam = (rsum(r) - _times_exact(r, ones, _BTN)[:, :, 0:1] + rsum(d_qe * qe) + rsum(d_kbe * kbe) - kd_term
                 + jnp.where(last_row, d_gl, 0.0))
        d_beta = rsum(d_kb * k) + rsum(d_vb * v)
        d_g = _exact_times(upper, d_gam * ones, _BNN)[:, :, 0:1]
        per_head = lambda x: x.reshape(GDN_HEADS, ng * CHUNK, x.shape[-1])
        d_q, d_k, d_v, d_beta, d_g = (per_head(x) for x in (d_q, d_k, d_v, d_beta, d_g))
        gates = jnp.zeros((ng * CHUNK, LANES), F32)
        for h in range(GDN_HEADS):
            lanes = slice(h * dk, (h + 1) * dk)
            d_ref[0, rows, lanes] = d_q[h]
            d_ref[1, rows, lanes] = d_k[h]
            d_ref[2, rows, lanes] = d_v[h]
            gates = gates + (jnp.where(lane_ids == h, d_beta[h], 0.0)
                             + jnp.where(lane_ids == GDN_HEADS + h, d_g[h], 0.0))
        dgate_ref[rows, :] = gates

    d_spec = pl.BlockSpec((3, cps * CHUNK, width), lambda g: (0, steps - 1 - g, 0))
    return pl.pallas_call(
        body, name="gdn_bwd",
        grid=(steps,),
        in_specs=[rows_blk(width, 0), rows_blk(width, 1), rows_blk(width, 2), rows_blk(LANES), rows_blk(LANES), gate_r,
                  per_chunk(dk, dk), per_chunk(CHUNK, CHUNK), per_chunk(CHUNK, CHUNK), per_chunk(CHUNK, CHUNK),
                  rows_blk(2 * width), rows_blk(width), rows_blk(width)] + [_HBM] * nx,
        out_specs=[d_spec, rows_blk(LANES)] + [_HBM] * nx,
        out_shape=[jax.ShapeDtypeStruct((3, t_len, width), F32),
                   jax.ShapeDtypeStruct((t_len, LANES), F32)] + _direct_out_shapes(scatter, (True,) * nx),
        scratch_shapes=[pltpu.VMEM((GDN_HEADS, dk, dk), F32)] + (_direct_semaphores(nx) if nx else []),
        compiler_params=_params("arbitrary"),
    )(gact, gact, gact, beta_c, gam_c, gam_r, *saved, d_o, *scatter)


def _group_sums(x, group):
    rows, width = x.shape
    lane = lax.broadcasted_iota(jnp.int32, (1, LANES), 1)
    out = []
    for t in range(width // LANES):
        seg = x[:, t * LANES:(t + 1) * LANES]
        if group == LANES:
            out.append(jnp.broadcast_to(jnp.sum(seg, axis=-1, keepdims=True), (rows, LANES)))
        else:
            low = jnp.sum(jnp.where(lane < group, seg, 0.0), axis=-1, keepdims=True)
            high = jnp.sum(jnp.where(lane < group, 0.0, seg), axis=-1, keepdims=True)
            out.append(jnp.where(lane < group, low, high))
    return jnp.concatenate(out, axis=1)


def _post_call(o_sb, o_gd, proj, x, target, w_out, sbw, gdw, fw, tm=256):
    t_len, d = x.shape
    half = 512
    zsb_blk = 1536 // half
    zgd_blk = 3584 // half

    def body(osb_ref, ogd_ref, zsb_ref, zgd_ref, x_ref, tg_ref, wo_ref, sbw_ref, gdw_ref, fw_ref,
             dx2_ref, dosb_ref, dogd_ref, dz_ref, loss_ref, gfw_ref, gsb_ref, ggd_ref, gwo_ref):
        step = pl.program_id(0)

        @pl.when(step == 0)
        def _():
            loss_ref[...] = jnp.zeros_like(loss_ref)
            gfw_ref[...] = jnp.zeros_like(gfw_ref)
            gsb_ref[...] = jnp.zeros_like(gsb_ref)
            ggd_ref[...] = jnp.zeros_like(ggd_ref)
            gwo_ref[...] = jnp.zeros_like(gwo_ref)

        def head_forward(o, z, w, head_dim):
            r = lax.rsqrt(_group_sums(o * o, head_dim) * (1.0 / head_dim) + EPS)
            nrm = o * r * w
            sg = _sigmoid(z)
            return r, nrm, sg, nrm * (z * sg)

        def head_backward(d_m, o, z, w, head_dim, r, nrm, sg):
            d_n = d_m * (z * sg)
            d_z = d_m * nrm * (sg * (1.0 + z * (1.0 - sg)))
            dnw = d_n * w
            d_o = r * dnw - o * (r * r * r) * (_group_sums(dnw * o, head_dim) * (1.0 / head_dim))
            return d_o, d_z, jnp.sum(d_n * o * r, axis=0, keepdims=True)

        osb, ogd, zsb, zgd = osb_ref[...], ogd_ref[...], zsb_ref[...], zgd_ref[...]
        sbw_v, gdw_v = sbw_ref[...], gdw_ref[...]
        r_sb, n_sb, sg_sb, m_sb = head_forward(osb, zsb, sbw_v, SB_HEAD_DIM)
        r_gd, n_gd, sg_gd, m_gd = head_forward(ogd, zgd, gdw_v, GDN_HEAD_DIM)
        mixed = jnp.concatenate([m_sb, m_gd], axis=1).astype(MXU_DTYPE)
        wo = wo_ref[...]
        x2 = x_ref[...] + jnp.dot(mixed, wo, preferred_element_type=F32)
        r2 = lax.rsqrt(jnp.mean(x2 * x2, axis=-1, keepdims=True) + EPS)
        fw_v = fw_ref[...]
        err = x2 * r2 * fw_v - tg_ref[...]
        loss_ref[...] += 0.5 * jnp.sum(jnp.sum(err * err, axis=-1, keepdims=True) * (1.0 / d))
        dy = err * (1.0 / d)
        gg = dy * fw_v
        dx2 = r2 * gg - x2 * ((r2 * r2 * r2) * jnp.mean(gg * x2, axis=-1, keepdims=True))
        gfw_ref[...] += jnp.sum(dy * x2 * r2, axis=0, keepdims=True)
        dx2_ref[...] = dx2
        dx2b = dx2.astype(MXU_DTYPE)
        d_mixed = lax.dot_general(dx2b, wo, _NT, preferred_element_type=F32)
        gwo_ref[...] += lax.dot_general(mixed, dx2b, _TN, preferred_element_type=F32)
        d_osb, d_zsb, gsb = head_backward(d_mixed[:, :half], osb, zsb, sbw_v, SB_HEAD_DIM, r_sb, n_sb, sg_sb)
        d_ogd, d_zgd, ggd = head_backward(d_mixed[:, half:], ogd, zgd, gdw_v, GDN_HEAD_DIM, r_gd, n_gd, sg_gd)
        dosb_ref[...] = d_osb
        dogd_ref[...] = d_ogd
        dz_ref[0] = d_zsb
        dz_ref[1] = d_zgd
        gsb_ref[...] += gsb
        ggd_ref[...] += ggd

    row_blk = lambda w: pl.BlockSpec((tm, w), lambda i: (i, 0))
    fixed = lambda r, w: pl.BlockSpec((r, w), lambda i: (0, 0))
    return pl.pallas_call(
        body, name="post",
        grid=(t_len // tm,),
        in_specs=[row_blk(half), row_blk(half),
                  pl.BlockSpec((tm, half), lambda i: (i, zsb_blk)),
                  pl.BlockSpec((tm, half), lambda i: (i, zgd_blk)),
                  row_blk(d), row_blk(d), fixed(d, d), fixed(1, half), fixed(1, half), fixed(1, d)],
        out_specs=[row_blk(d), row_blk(half), row_blk(half),
                   pl.BlockSpec((2, tm, half), lambda i: (DPROJ_GATE_SLOT // 2, i, 0)),
                   fixed(1, LANES), fixed(1, d), fixed(1, half), fixed(1, half), fixed(d, d)],
        out_shape=[jax.ShapeDtypeStruct((t_len, d), F32)] + [jax.ShapeDtypeStruct((t_len, half), F32)] * 2
                  + [jax.ShapeDtypeStruct((len(DPROJ_PIECE_OF_SLOT), t_len, half), F32),
                     jax.ShapeDtypeStruct((1, LANES), F32), jax.ShapeDtypeStruct((1, d), F32),
                     jax.ShapeDtypeStruct((1, half), F32), jax.ShapeDtypeStruct((1, half), F32),
                     jax.ShapeDtypeStruct((d, d), F32)],
        compiler_params=_params("arbitrary"),
    )(o_sb, o_gd, proj, proj, x, target, w_out, sbw, gdw, fw)


def _piece_of_slot(s):
    return jnp.where(s < DPROJ_GDN_SLOT, s, jnp.where(s < DPROJ_GATE_SLOT, s + 1,
                                                     jnp.where(s == DPROJ_GATE_SLOT, 3, 7)))


def _gw_in_call(h_t, dproj8):
    d, t_len = h_t.shape
    n_piece, _, pw = dproj8.shape

    def body(ht_ref, dp_ref, gw_ref):
        gw_ref[...] = jnp.dot(ht_ref[...], dp_ref[0].astype(MXU_DTYPE), preferred_element_type=F32)

    return pl.pallas_call(
        body, name="gw_in",
        grid=(n_piece,),
        in_specs=[pl.BlockSpec((d, t_len), lambda s: (0, 0)),
                  pl.BlockSpec((1, t_len, pw), lambda s: (s, 0, 0))],
        out_specs=pl.BlockSpec((d, pw), lambda s: (0, _piece_of_slot(s))),
        out_shape=jax.ShapeDtypeStruct((d, n_piece * pw), F32),
        compiler_params=_params("arbitrary"),
    )(h_t, dproj8)


def _slot_of_piece(p):
    return jnp.where(p < DPROJ_GDN_SLOT, p, jnp.where(p == 3, DPROJ_GATE_SLOT, jnp.where(p < 7, p - 1, 7)))


def _gw_in_shards_call(h_t, dproj8, dsmall, out_dtype):
    d, t_len = h_t.shape
    n_piece, _, pw = dproj8.shape
    ns = dsmall.shape[1]
    n_pairs = N_DEV // 2

    def body(ht_ref, dp_ref, ds_ref, chip_ref, prev_ref, gates_ref, send_ref, recv_ref, send_sems, recv_sems):
        p = pl.program_id(0)
        x_pos, y_pos, c = lax.axis_index("x"), lax.axis_index("y"), lax.axis_index("c")
        to_sibling = lambda pair: pltpu.make_async_remote_copy(
            src_ref=send_ref.at[pair], dst_ref=recv_ref.at[pair], send_sem=send_sems.at[pair],
            recv_sem=recv_sems.at[pair], device_id=(x_pos, y_pos, 1 - c), device_id_type=_MESH)

        @pl.when(p == 0)
        def _():
            gates_ref[...] = jnp.dot(ht_ref[...], ds_ref[...].astype(MXU_DTYPE), preferred_element_type=F32)

        def emit(s, tail):
            x = jnp.concatenate([prev_ref[...], tail], axis=1)
            y = x if s == 0 else pltpu.roll(x, SHARD_PAD - s, axis=1)
            shard = y[:, :SHARD_COLS].astype(out_dtype)

            @pl.when(c == s % 2)
            def _():
                chip_ref[s // 2] = shard

            @pl.when(c != s % 2)
            def _():
                send_ref[s // 2] = shard
                to_sibling(s // 2).start()

        @pl.when(p < n_piece)
        def _():
            cur = jnp.dot(ht_ref[...], dp_ref[0].astype(MXU_DTYPE), preferred_element_type=F32)
            for s in range(n_piece - 1):
                pl.when(p == s + 1)(functools.partial(emit, s, cur[:, :SHARD_PAD - pw]))
            prev_ref[...] = cur

        @pl.when(p == n_piece)
        def _():
            emit(n_piece - 1, gates_ref[...])
            for pair in range(n_pairs):
                to_sibling(pair).wait_send()
            for pair in range(n_pairs):
                to_sibling(pair).wait_recv()
                chip_ref[pair] = (chip_ref[pair].astype(F32) + recv_ref[pair].astype(F32)).astype(out_dtype)

    shards_of_side = lambda: pltpu.VMEM((n_pairs, d, SHARD_COLS), out_dtype)
    return pl.pallas_call(
        body, name="gw_in",
        grid=(n_piece + 1,),
        in_specs=[pl.BlockSpec((d, t_len), lambda p: (0, 0)),
                  pl.BlockSpec((1, t_len, pw), lambda p: (_slot_of_piece(jnp.minimum(p, n_piece - 1)), 0, 0)),
                  pl.BlockSpec((t_len, ns), lambda p: (0, 0))],
        out_specs=pl.BlockSpec((n_pairs, d, SHARD_COLS), lambda p: (0, 0, 0)),
        out_shape=jax.ShapeDtypeStruct((n_pairs, d, SHARD_COLS), out_dtype),
        scratch_shapes=[pltpu.VMEM((d, pw), F32), pltpu.VMEM((d, ns), F32), shards_of_side(), shards_of_side(),
                        pltpu.SemaphoreType.DMA((n_pairs,)), pltpu.SemaphoreType.DMA((n_pairs,))],
        compiler_params=_params("arbitrary"),
    )(h_t, dproj8, dsmall)


def _gw_small_call(h_t, dsmall, tm=512):
    d, t_len = h_t.shape
    ns = dsmall.shape[1]

    def body(ht_ref, dp_ref, gw_ref):
        @pl.when(pl.program_id(0) == 0)
        def _():
            gw_ref[...] = jnp.zeros_like(gw_ref)

        gw_ref[...] += jnp.dot(ht_ref[...], dp_ref[...].astype(MXU_DTYPE), preferred_element_type=F32)

    return pl.pallas_call(
        body, name="gw_small",
        grid=(t_len // tm,),
        in_specs=[pl.BlockSpec((d, tm), lambda t: (0, t)),
                  pl.BlockSpec((tm, ns), lambda t: (t, 0))],
        out_specs=pl.BlockSpec((d, ns), lambda t: (0, 0)),
        out_shape=jax.ShapeDtypeStruct((d, ns), F32),
        compiler_params=_params("arbitrary"),
    )(h_t, dsmall)


def _dx_call(dproj8, dsmall, w_main, w_small, x, r, dx2, norm_w, chip_scatter=(), tm=256):
    t_len, d = x.shape
    n_piece, _, pw = dproj8.shape
    ns = dsmall.shape[1]
    nx = len(chip_scatter)
    steps = t_len // tm

    def body(*refs):
        dp_ref, ds_ref, wm_ref, ws_ref, x_ref, r_ref, dx2_ref, nw_ref = refs[:8]
        gx_ref, gnw_ref = refs[8 + nx:10 + nx]
        copies = lambda: _chip_copies(refs[8:8 + nx], refs[10 + nx:10 + 2 * nx], *refs[10 + 2 * nx:])
        if nx:
            pl.when(pl.program_id(0) == 0)(lambda: _start_all(copies()))

        @pl.when(pl.program_id(0) == 0)
        def _():
            gnw_ref[...] = jnp.zeros_like(gnw_ref)

        dh = lax.dot_general(ds_ref[...].astype(MXU_DTYPE), ws_ref[...], _NT, preferred_element_type=F32)
        for s, p in enumerate(DPROJ_PIECE_OF_SLOT):
            dh = dh + lax.dot_general(dp_ref[s].astype(MXU_DTYPE), wm_ref[:, p * pw:(p + 1) * pw], _NT,
                                      preferred_element_type=F32)
        xv, rv = x_ref[...], r_ref[...]
        dn = dh * nw_ref[...]
        gx_ref[...] = dx2_ref[...] + rv * dn - xv * ((rv * rv * rv) * jnp.mean(dn * xv, axis=-1, keepdims=True))
        gnw_ref[...] += jnp.sum(dh * xv * rv, axis=0, keepdims=True)
        if nx:
            pl.when(pl.program_id(0) == steps - 1)(lambda: _wait_all(copies()))

    return pl.pallas_call(
        body, name="dx",
        grid=(steps,),
        in_specs=[pl.BlockSpec((n_piece, tm, pw), lambda i: (0, i, 0)),
                  pl.BlockSpec((tm, ns), lambda i: (i, 0)),
                  pl.BlockSpec((d, n_piece * pw), lambda i: (0, 0)),
                  pl.BlockSpec((d, ns), lambda i: (0, 0)),
                  pl.BlockSpec((tm, d), lambda i: (i, 0)),
                  pl.BlockSpec((tm, 1), lambda i: (i, 0)),
                  pl.BlockSpec((tm, d), lambda i: (i, 0)),
                  pl.BlockSpec((1, d), lambda i: (0, 0))] + [_HBM] * nx,
        out_specs=[pl.BlockSpec((tm, d), lambda i: (i, 0)),
                   pl.BlockSpec((1, d), lambda i: (0, 0))] + [_HBM] * nx,
        out_shape=[jax.ShapeDtypeStruct((t_len, d), F32), jax.ShapeDtypeStruct((1, d), F32)]
                  + [jax.ShapeDtypeStruct(a.shape, a.dtype) for a in chip_scatter],
        scratch_shapes=_chip_semaphores(nx) if nx else [],
        compiler_params=_params("arbitrary"),
    )(dproj8, dsmall, w_main, w_small, x, r, dx2, norm_w, *chip_scatter)


def _exchange_call(name, srcs, per_peer):
    n = len(srcs)

    def body(*refs):
        src_refs, out_refs = refs[:n], refs[n:2 * n]
        copies = _direct_copies(src_refs, out_refs, *refs[2 * n:], per_peer)
        _start_all(copies)
        _wait_all(copies)

    hbm = pl.BlockSpec(memory_space=pl.ANY)
    return pl.pallas_call(
        body, name=name,
        in_specs=[hbm] * n, out_specs=[hbm] * n, out_shape=_direct_out_shapes(srcs, per_peer),
        scratch_shapes=_direct_semaphores(n),
    )(*srcs)


def _direct_out_shapes(srcs, per_peer):
    return [jax.ShapeDtypeStruct(s.shape if pp else (N_DEV,) + s.shape, s.dtype) for s, pp in zip(srcs, per_peer)]


def _direct_semaphores(n):
    return [pltpu.SemaphoreType.DMA((n * (N_DEV - 1),)), pltpu.SemaphoreType.DMA((n * (N_DEV - 1),)),
            pltpu.SemaphoreType.DMA((n,))]


def _direct_copies(src_refs, out_refs, send_sems, recv_sems, local_sems, per_peer):
    x, y, c = lax.axis_index("x"), lax.axis_index("y"), lax.axis_index("c")
    me = 4 * x + 2 * y + c
    local, remote = [], []
    for a in range(len(src_refs)):
        mine = src_refs[a].at[me] if per_peer[a] else src_refs[a]
        local.append(pltpu.make_async_copy(mine, out_refs[a].at[me], local_sems.at[a]))
    for k in range(1, N_DEV):
        kx, ky, kc = (k >> 2) & 1, (k >> 1) & 1, k & 1
        px = 1 - x if kx else x
        py = 1 - y if ky else y
        pc = 1 - c if kc else c
        peer = 4 * px + 2 * py + pc
        for a in range(len(src_refs)):
            sem = a * (N_DEV - 1) + (k - 1)
            remote.append(pltpu.make_async_remote_copy(
                src_ref=src_refs[a].at[peer] if per_peer[a] else src_refs[a], dst_ref=out_refs[a].at[me],
                send_sem=send_sems.at[sem], recv_sem=recv_sems.at[sem],
                device_id=(px, py, pc), device_id_type=pl.DeviceIdType.MESH))
    return local, remote


def _start_all(copies):
    local, remote = copies
    for cp in local + remote:
        cp.start()


def _wait_all(copies):
    local, remote = copies
    for cp in remote:
        cp.wait_send()
    for cp in remote:
        cp.wait_recv()
    for cp in local:
        cp.wait()


N_CHIPS = 4
_HBM = pl.BlockSpec(memory_space=pl.ANY)
_MESH = pl.DeviceIdType.MESH


def _gather_call(name, srcs):
    n = len(srcs)
    per = N_DEV - 1

    def body(*refs):
        src_refs, out_refs = refs[:n], refs[n:2 * n]
        send_sems, recv_sems, local_sems = refs[2 * n:]
        x, y, c = lax.axis_index("x"), lax.axis_index("y"), lax.axis_index("c")
        me, sibling = (x, y, c), (x, y, 1 - c)
        x_nbr, y_nbr, diagonal = (1 - x, y), (x, 1 - y), (1 - x, 1 - y)
        held = ((1 - x) * c + x * (1 - c), y * c + (1 - y) * (1 - c))
        onward = (x * c + (1 - x) * (1 - c), (1 - y) * c + y * (1 - c))
        slot = lambda px, py, pc: 4 * px + 2 * py + pc

        def copy(a, k, block, to, from_src=False):
            rows = out_refs[a].at[slot(*block)]
            return pltpu.make_async_remote_copy(
                src_ref=src_refs[a] if from_src else rows, dst_ref=rows,
                send_sem=send_sems.at[a * per + k], recv_sem=recv_sems.at[a * per + k],
                device_id=to, device_id_type=_MESH)

        local = [pltpu.make_async_copy(src_refs[a], out_refs[a].at[slot(*me)], local_sems.at[a]) for a in range(n)]
        started = []

        def start(cp):
            cp.start()
            started.append(cp)

        for cp in local:
            cp.start()
        for a in range(n):
            start(copy(a, 0, me, sibling, True))
            start(copy(a, 1, me, (*x_nbr, c), True))
            start(copy(a, 2, me, (*y_nbr, c), True))
        for a in range(n):
            copy(a, 1, (*x_nbr, c), me).wait_recv()
            copy(a, 2, (*y_nbr, c), me).wait_recv()
            start(copy(a, 3, (*held, c), (*onward, c)))
            start(copy(a, 4, (*x_nbr, c), sibling))
            start(copy(a, 5, (*y_nbr, c), sibling))
        for a in range(n):
            copy(a, 3, (*diagonal, c), me).wait_recv()
            start(copy(a, 6, (*diagonal, c), sibling))
        for a in range(n):
            copy(a, 0, sibling, me).wait_recv()
            for k, chip in ((4, x_nbr), (5, y_nbr), (6, diagonal)):
                copy(a, k, (*chip, 1 - c), me).wait_recv()
        for cp in started:
            cp.wait_send()
        for cp in local:
            cp.wait()

    return pl.pallas_call(
        body, name=name,
        in_specs=[_HBM] * n, out_specs=[_HBM] * n,
        out_shape=[jax.ShapeDtypeStruct((N_DEV,) + s.shape, s.dtype) for s in srcs],
        scratch_shapes=[pltpu.SemaphoreType.DMA((n * per,)), pltpu.SemaphoreType.DMA((n * per,)),
                        pltpu.SemaphoreType.DMA((n,))],
    )(*srcs)


def _chip_semaphores(n):
    per = N_CHIPS - 1
    return [pltpu.SemaphoreType.DMA((n * per,)), pltpu.SemaphoreType.DMA((n * per,)), pltpu.SemaphoreType.DMA((n,))]


def _chip_copies(src_refs, out_refs, send_sems, recv_sems, local_sems):
    per = N_CHIPS - 1
    x, y, c = lax.axis_index("x"), lax.axis_index("y"), lax.axis_index("c")
    mine = 2 * x + y
    chips = [(1 - x, y), (x, 1 - y), (1 - x, 1 - y)]
    n = len(src_refs)
    local = [pltpu.make_async_copy(src_refs[a].at[mine], out_refs[a].at[mine], local_sems.at[a]) for a in range(n)]
    remote = []
    for a in range(n):
        for j, (px, py) in enumerate(chips):
            remote.append(pltpu.make_async_remote_copy(
                src_ref=src_refs[a].at[2 * px + py], dst_ref=out_refs[a].at[mine],
                send_sem=send_sems.at[a * per + j], recv_sem=recv_sems.at[a * per + j],
                device_id=(px, py, c), device_id_type=_MESH))
    return local, remote


def _adam_call(name, parts, w, m, v, tr):
    rows, cols = w.shape
    n_slots = parts.shape[0]

    def body(p_ref, w_ref, m_ref, v_ref, g_ref, d_ref, nm_ref, nv_ref):
        g = p_ref[0].astype(F32)
        for s in range(1, n_slots):
            g = g + p_ref[s].astype(F32)
        m_new = ADAM_B1 * m_ref[...] + (1.0 - ADAM_B1) * g
        v_new = ADAM_B2 * v_ref[...] + (1.0 - ADAM_B2) * (g * g)
        m_hat = m_new / (1.0 - ADAM_B1 ** ADAM_STEP)
        v_hat = v_new / (1.0 - ADAM_B2 ** ADAM_STEP)
        g_ref[...] = g
        d_ref[...] = -ADAM_LR * (m_hat / (jnp.sqrt(v_hat) + ADAM_EPS) + ADAM_WD * w_ref[...])
        nm_ref[...] = m_new
        nv_ref[...] = v_new

    blk = pl.BlockSpec((tr, cols), lambda i: (i, 0))
    return pl.pallas_call(
        body, name=name,
        grid=(rows // tr,),
        in_specs=[pl.BlockSpec((n_slots, tr, cols), lambda i: (0, i, 0)), blk, blk, blk],
        out_specs=[blk] * 4,
        out_shape=[jax.ShapeDtypeStruct((rows, cols), F32)] * 4,
        compiler_params=_params("arbitrary"),
    )(parts, w, m, v)


N_PIECES = 8
PIECE = 512
SHARD_COLS = 513
SHARD_PAD = 640
RELAYOUT_ROWS = 256


def _from_shards_call(shards):
    _, d, _ = shards.shape
    tr = RELAYOUT_ROWS

    def body(p_ref, m_ref, s_ref):
        lane = lax.broadcasted_iota(jnp.int32, (tr, SHARD_PAD), 1)
        pad = jnp.zeros((tr, SHARD_PAD - SHARD_COLS), F32)
        sh = [jnp.concatenate([p_ref[s].astype(F32), pad], axis=1) for s in range(N_DEV)]
        for p in range(N_PIECES):
            y = sh[p] if p == 0 else pltpu.roll(sh[p], p, axis=1)
            if p > 0:
                y = jnp.where(lane < p, pltpu.roll(sh[p - 1], SHARD_PAD - (SHARD_COLS - p), axis=1), y)
            m_ref[:, p * PIECE:(p + 1) * PIECE] = y[:, :PIECE].astype(m_ref.dtype)
        first_gate = N_PIECES * PIECE - (N_DEV - 1) * SHARD_COLS
        s_ref[...] = pltpu.roll(sh[N_DEV - 1], SHARD_PAD - first_gate, axis=1)[:, :LANES].astype(s_ref.dtype)

    return pl.pallas_call(
        body, name="w_in_from_shards",
        grid=(d // tr,),
        in_specs=[pl.BlockSpec((N_DEV, tr, SHARD_COLS), lambda i: (0, i, 0))],
        out_specs=[pl.BlockSpec((tr, N_PIECES * PIECE), lambda i: (i, 0)), pl.BlockSpec((tr, LANES), lambda i: (i, 0))],
        out_shape=[jax.ShapeDtypeStruct((d, N_PIECES * PIECE), shards.dtype),
                   jax.ShapeDtypeStruct((d, LANES), shards.dtype)],
        compiler_params=_params("arbitrary"),
    )(shards)


def _adamw(g, w, m, v):
    m_new = ADAM_B1 * m + (1.0 - ADAM_B1) * g
    v_new = ADAM_B2 * v + (1.0 - ADAM_B2) * (g * g)
    m_hat = m_new / (1.0 - ADAM_B1 ** ADAM_STEP)
    v_hat = v_new / (1.0 - ADAM_B2 ** ADAM_STEP)
    return -ADAM_LR * (m_hat / (jnp.sqrt(v_hat) + ADAM_EPS) + ADAM_WD * w), m_new, v_new


def _adam_small_call(parts, ws, ms, vs):
    n = len(ws)
    n_slots = parts.shape[0]

    def body(*refs):
        p_ref = refs[0]
        w_refs, m_refs, v_refs = refs[1:1 + n], refs[1 + n:1 + 2 * n], refs[1 + 2 * n:1 + 3 * n]
        loss_ref = refs[1 + 3 * n]
        outs = refs[2 + 3 * n:]
        g_all = p_ref[0]
        for s in range(1, n_slots):
            g_all = g_all + p_ref[s]
        loss_ref[...] = g_all[n:n + 1, 0:1]
        for r in range(n):
            size = w_refs[r].shape[1]
            g = g_all[r:r + 1, :size]
            delta, m_new, v_new = _adamw(g, w_refs[r][...], m_refs[r][...], v_refs[r][...])
            for kind, val in enumerate((g, delta, m_new, v_new)):
                outs[kind * n + r][...] = val

    vm = pl.BlockSpec(memory_space=pltpu.VMEM)
    shapes = [jax.ShapeDtypeStruct(w.shape, F32) for w in ws]
    return pl.pallas_call(
        body, name="adam_small",
        in_specs=[vm] * (1 + 3 * n), out_specs=[vm] * (1 + 4 * n),
        out_shape=[jax.ShapeDtypeStruct((1, 1), F32)] + shapes * 4,
    )(parts, *ws, *ms, *vs)


_SMALL_ROWS = ("norm1_w", "final_norm_w", "sb_norm_w", "gdn_norm_w", "gdn_A_log", "gdn_dt_bias", "loss")


def _pack_small(vals, width):
    rows = [jnp.pad(a.reshape(1, -1).astype(F32), ((0, 0), (0, width - a.size))) for a in vals]
    rows += [jnp.zeros((1, width), F32)] * (8 - len(rows))
    return jnp.concatenate(rows, axis=0)


def _device_step(x2d, tgt, w_main, w_small, w_out_full, conv_full, norm1_w, sb_norm_w, gdn_A_log, gdn_dt_bias,
                 gdn_norm_w, final_norm_w, distributed=False):
    t_len, d = x2d.shape
    n_chunks = t_len // CHUNK
    w_main, w_small, w_out_full = (a.astype(MXU_DTYPE) for a in (w_main, w_small, w_out_full))
    w_small_t = w_small[:, :2 * GDN_HEADS].T

    pad_lanes = lambda a, lo: jnp.pad(a.reshape(1, -1), ((0, 0), (lo, LANES - lo - a.size)))
    alog_l, dtb_l = pad_lanes(gdn_A_log, GDN_HEADS), pad_lanes(gdn_dt_bias, GDN_HEADS)
    alog_c, dtb_c = alog_l[:, :8].T, dtb_l[:, :8].T
    sbw = jnp.tile(sb_norm_w, (1, 512 // SB_HEAD_DIM))
    gdw = jnp.tile(gdn_norm_w, (1, 512 // GDN_HEAD_DIM))
    fw = final_norm_w.reshape(1, d)

    if distributed:
        proj, ps, pst, h_t, r1, w_out_g, conv_g = _inproj_call(
            x2d, norm1_w, w_main, w_small, w_small_t, gather=(w_out_full, conv_full))
        w_out_full = w_out_g.reshape(d, d)
        conv_full = conv_g.transpose(1, 0, 2).reshape(CONV_WIDTH, N_DEV * conv_g.shape[2])
    else:
        proj, ps, pst, h_t, r1 = _inproj_call(x2d, norm1_w, w_main, w_small, w_small_t)
    o_sb, sp_total, sb_blocks_run = _sb_fwd_call(proj, t_len)
    gact = _gdn_prep_call(proj, conv_full, t_len)
    beta_l, gcol_l, grow = _gdn_gates_call(ps, pst, alog_l, dtb_l, alog_c, dtb_c, t_len)
    gam_r = grow[GDN_HEADS:2 * GDN_HEADS].reshape(GDN_HEADS, n_chunks, 1, CHUNK)
    o_gd, *gdn_saved = _gdn_fwd_call(gact, beta_l, gcol_l, gam_r, t_len)

    (dx2, d_osb, d_ogd, dproj8, loss_p, g_fw, g_sbw, g_gdw, g_wout) = _post_call(
        o_sb, o_gd, proj, x2d, tgt, w_out_full, sbw, gdw, fw)

    dproj8 = _sb_bwd_call(proj, sp_total, sb_blocks_run, d_osb, dproj8, t_len)
    if distributed:
        d_gact3, d_gates, g_wout = _gdn_bwd_call(gact, beta_l, gcol_l, gam_r, gdn_saved, d_ogd, t_len,
                                                 scatter=(g_wout.reshape(N_DEV, d // N_DEV, d),))
    else:
        d_gact3, d_gates = _gdn_bwd_call(gact, beta_l, gcol_l, gam_r, gdn_saved, d_ogd, t_len)
    dproj8, g_conv = _gdn_prep_bwd_call(proj, conv_full, d_gact3, dproj8, t_len)
    dsmall, g_alog, g_dtb = _gdn_gates_bwd_call(ps, alog_l, dtb_l, d_gates, t_len)

    if distributed:
        chip_partials = _gw_in_shards_call(h_t, dproj8, dsmall, WIRE_DTYPE)
        grad_x, g_n1, g_w_in = _dx_call(dproj8, dsmall, w_main, w_small, x2d, r1, dx2, norm1_w,
                                        chip_scatter=(chip_partials,))
    else:
        grad_x, g_n1 = _dx_call(dproj8, dsmall, w_main, w_small, x2d, r1, dx2, norm1_w)
        g_w_in = (_gw_in_call(h_t, dproj8), _gw_small_call(h_t, dsmall))
    return (loss_p, grad_x, g_n1, g_w_in, g_sbw, g_conv, g_alog, g_dtb, g_gdw, g_wout, g_fw)


def kernel(x, norm1_w, w_in, sb_norm_w, gdn_conv_w, gdn_A_log, gdn_dt_bias, gdn_norm_w, w_out, final_norm_w, loss_target, m_norm1_w, m_w_in, m_sb_norm_w, m_gdn_conv_w, m_gdn_A_log, m_gdn_dt_bias, m_gdn_norm_w, m_w_out, m_final_norm_w, v_norm1_w, v_w_in, v_sb_norm_w, v_gdn_conv_w, v_gdn_A_log, v_gdn_dt_bias, v_gdn_norm_w, v_w_out, v_final_norm_w):
    d = x.shape[2]
    shard_cols = w_in.shape[2]
    conv_cols = gdn_conv_w.shape[2]

    (w_in_g,) = _gather_call("gather_weights", [w_in[0].astype(WIRE_DTYPE)])
    w_main, w_small = _from_shards_call(w_in_g)

    (loss_p, grad_x, g_n1, p_w_in, g_sbw, g_conv, g_alog, g_dtb, g_gdw, p_wout, g_fw) = _device_step(
        x[0], loss_target[0], w_main, w_small, w_out[0].astype(WIRE_DTYPE), gdn_conv_w[0], norm1_w, sb_norm_w,
        gdn_A_log, gdn_dt_bias, gdn_norm_w, final_norm_w, distributed=True)

    g_conv_parts = g_conv.reshape(CONV_WIDTH, N_DEV, conv_cols).transpose(1, 0, 2)
    fold = lambda a, group: a.reshape(-1, group).sum(axis=0)
    small_g = _pack_small([g_n1, g_fw, fold(g_sbw, SB_HEAD_DIM), fold(g_gdw, GDN_HEAD_DIM),
                           g_alog[0, GDN_HEADS:2 * GDN_HEADS], g_dtb[0, GDN_HEADS:2 * GDN_HEADS],
                           loss_p[0, :1]], d)
    p_small, p_conv = _exchange_call("exchange_small", [small_g, g_conv_parts], [False, True])

    r_w_in = _adam_call("adam_w_in", p_w_in, w_in[0], m_w_in[0], v_w_in[0], 256)
    r_wout = _adam_call("adam_w_out", p_wout, w_out[0], m_w_out[0], v_w_out[0], d // N_DEV)
    r_conv = _adam_call("adam_conv", p_conv, gdn_conv_w[0], m_gdn_conv_w[0], v_gdn_conv_w[0], CONV_WIDTH)

    row = lambda a: a.reshape(1, -1)
    n_small = len(_SMALL_ROWS) - 1
    r_small = _adam_small_call(
        p_small,
        [norm1_w, row(final_norm_w), sb_norm_w, gdn_norm_w, gdn_A_log, gdn_dt_bias],
        [m_norm1_w, row(m_final_norm_w), m_sb_norm_w, m_gdn_norm_w, m_gdn_A_log, m_gdn_dt_bias],
        [v_norm1_w, row(v_final_norm_w), v_sb_norm_w, v_gdn_norm_w, v_gdn_A_log, v_gdn_dt_bias])

    def small_out(kind, name):
        out = r_small[1 + kind * n_small + _SMALL_ROWS.index(name)]
        return out.reshape(final_norm_w.shape) if name == "final_norm_w" else out

    def outputs(kind):
        return (small_out(kind, "norm1_w"), r_w_in[kind][None], small_out(kind, "sb_norm_w"), r_conv[kind][None],
                small_out(kind, "gdn_A_log"), small_out(kind, "gdn_dt_bias"), small_out(kind, "gdn_norm_w"),
                r_wout[kind][None], small_out(kind, "final_norm_w"))

    return (r_small[0][0, 0], grad_x[None], *outputs(0), *outputs(1), *outputs(2), *outputs(3))
```

```python
import functools

import jax
import jax.numpy as jnp
from jax import lax
from jax.experimental import pallas as pl
from jax.experimental.pallas import tpu as pltpu

F32 = jnp.float32
MXU_DTYPE = jnp.bfloat16
WIRE_DTYPE = jnp.bfloat16
EXACT = lax.Precision.HIGHEST
EPS = 1e-6
N_DEV = 8
SB_HEAD_DIM = 64
GDN_HEAD_DIM = 128
GDN_HEADS = 4
GDN_CHUNKS_PER_STEP = 4
GDN_BWD_GROUP = 1
CHUNK = 64
CONV_WIDTH = 4
LANES = 128
SB_BLOCK = 128
SB_BQ = 256
VMEM_LIMIT_BYTES = 56 * 1024 * 1024

PIECE_COLS = 512
PROJ_PIECE_KINDS = ("heads", "heads", "heads", "gate", "heads", "heads", "heads", "gate")
SB_FIRST_BLOCK, GDN_FIRST_BLOCK = 0, 12

DPROJ_PIECE_OF_SLOT = (0, 1, 2, 4, 5, 6, 3, 7)
DPROJ_SB_SLOT, DPROJ_GDN_SLOT, DPROJ_GATE_SLOT = 0, 3, 6

ADAM_LR = 0.001
ADAM_B1 = 0.9
ADAM_B2 = 0.999
ADAM_EPS = 1e-08
ADAM_WD = 0.01
ADAM_STEP = 10

_NN = (((1,), (0,)), ((), ()))
_NT = (((1,), (1,)), ((), ()))
_TN = (((0,), (0,)), ((), ()))
_BNN = (((2,), (1,)), ((0,), (0,)))
_BNT = (((2,), (2,)), ((0,), (0,)))
_BTN = (((1,), (1,)), ((0,), (0,)))


def _mx(a, b):
    return jnp.dot(a, b, precision=EXACT, preferred_element_type=F32)


def _split(x):
    hi = x.astype(MXU_DTYPE)
    return hi, (x - hi.astype(F32)).astype(MXU_DTYPE)


def _m3_general(a, b, dims):
    ah, al = _split(a)
    bh, bl = _split(b)
    dot = lambda x, y: lax.dot_general(x, y, dims, preferred_element_type=F32)
    (contract, _), (batch, _) = dims
    free = [ax for ax in range(a.ndim) if ax not in contract and ax not in batch][0]
    m = a.shape[free]
    both = dot(jnp.concatenate([ah, al], axis=free), bh)
    out_axis = len(batch)
    hi_part = lax.slice_in_dim(both, 0, m, axis=out_axis)
    lo_part = lax.slice_in_dim(both, m, 2 * m, axis=out_axis)
    return hi_part + (dot(ah, bl) + lo_part)


def _times_exact(a, b_exact, dims):
    ah, al = _split(a)
    (contract, _), (batch, _) = dims
    free = [ax for ax in range(a.ndim) if ax not in contract and ax not in batch][0]
    m = a.shape[free]
    both = lax.dot_general(jnp.concatenate([ah, al], axis=free), b_exact.astype(MXU_DTYPE), dims,
                           preferred_element_type=F32)
    out_axis = len(batch)
    return lax.slice_in_dim(both, 0, m, axis=out_axis) + lax.slice_in_dim(both, m, 2 * m, axis=out_axis)


def _exact_times(a_exact, b, dims):
    bh, bl = _split(b)
    n = b.shape[-1]
    both = lax.dot_general(a_exact.astype(MXU_DTYPE), jnp.concatenate([bh, bl], axis=-1), dims,
                           preferred_element_type=F32)
    return both[..., :n] + both[..., n:]


def _sigmoid(z):
    return 1.0 / (1.0 + jnp.exp(-z))


def _softplus(z):
    return jnp.maximum(z, 0.0) + jnp.log(1.0 + jnp.exp(-jnp.abs(z)))


def _params(*semantics):
    return pltpu.CompilerParams(dimension_semantics=semantics, vmem_limit_bytes=VMEM_LIMIT_BYTES)


def _inproj_call(x, norm_w, w_main, w_small, w_small_t, gather=(), tm=256):
    t_len, d = x.shape
    n = w_main.shape[1]
    ns = w_small.shape[1]
    nst = w_small_t.shape[0]
    ng = len(gather)
    steps = t_len // tm

    blocks_per_piece = PIECE_COLS // LANES
    n_gate_cols = PIECE_COLS * PROJ_PIECE_KINDS.count("gate")
    n_col_blocks = blocks_per_piece * PROJ_PIECE_KINDS.count("heads")

    def body(*refs):
        x_ref, nw_ref, wm_ref, ws_ref, wst_ref = refs[:5]
        cols_ref, pz_ref, ps_ref, pst_ref, ht_ref, r_ref = refs[5 + ng:11 + ng]
        copies = lambda: _direct_copies(refs[5:5 + ng], refs[11 + ng:11 + 2 * ng], *refs[11 + 2 * ng:], (False,) * ng)
        if ng:
            pl.when(pl.program_id(0) == 0)(lambda: _start_all(copies()))
        xv = x_ref[...]
        r = lax.rsqrt(jnp.mean(xv * xv, axis=-1, keepdims=True) + EPS)
        h = xv * r * nw_ref[...]
        hb = h.astype(MXU_DTYPE)
        n_block = n_gate = 0
        for piece, kind in enumerate(PROJ_PIECE_KINDS):
            out = jnp.dot(hb, wm_ref[:, piece * PIECE_COLS:(piece + 1) * PIECE_COLS], preferred_element_type=F32)
            if kind == "gate":
                pz_ref[:, n_gate * PIECE_COLS:(n_gate + 1) * PIECE_COLS] = out
                n_gate += 1
            else:
                for j in range(blocks_per_piece):
                    cols_ref[n_block + j] = out[:, j * LANES:(j + 1) * LANES]
                n_block += blocks_per_piece
        ps_ref[...] = jnp.dot(hb, ws_ref[...], preferred_element_type=F32)
        pst_ref[...] = lax.dot_general(wst_ref[...], hb, _NT, preferred_element_type=F32)
        ht_ref[...] = h.T.astype(MXU_DTYPE)
        r_ref[...] = r
        if ng:
            pl.when(pl.program_id(0) == steps - 1)(lambda: _wait_all(copies()))

    return pl.pallas_call(
        body, name="inproj",
        grid=(steps,),
        in_specs=[pl.BlockSpec((tm, d), lambda i: (i, 0)),
                  pl.BlockSpec((1, d), lambda i: (0, 0)),
                  pl.BlockSpec((d, n), lambda i: (0, 0)),
                  pl.BlockSpec((d, ns), lambda i: (0, 0)),
                  pl.BlockSpec((nst, d), lambda i: (0, 0))] + [_HBM] * ng,
        out_specs=[pl.BlockSpec((n_col_blocks, tm, LANES), lambda i: (0, i, 0)),
                   pl.BlockSpec((tm, n_gate_cols), lambda i: (i, 0)),
                   pl.BlockSpec((tm, ns), lambda i: (i, 0)),
                   pl.BlockSpec((nst, tm), lambda i: (0, i)),
                   pl.BlockSpec((d, tm), lambda i: (0, i)),
                   pl.BlockSpec((tm, 1), lambda i: (i, 0))] + [_HBM] * ng,
        out_shape=[jax.ShapeDtypeStruct((n_col_blocks, t_len, LANES), F32),
                   jax.ShapeDtypeStruct((t_len, n_gate_cols), F32),
                   jax.ShapeDtypeStruct((t_len, ns), F32),
                   jax.ShapeDtypeStruct((nst, t_len), F32),
                   jax.ShapeDtypeStruct((d, t_len), MXU_DTYPE),
                   jax.ShapeDtypeStruct((t_len, 1), F32)] + _direct_out_shapes(gather, (False,) * ng),
        scratch_shapes=_direct_semaphores(ng) if ng else [],
        compiler_params=_params("arbitrary"),
    )(x, norm_w, w_main, w_small, w_small_t, *gather)


def _running_sum_mm(x, tri):
    hi = x.astype(MXU_DTYPE)
    lo = (x - hi.astype(F32)).astype(MXU_DTYPE)
    return jnp.dot(hi, tri, preferred_element_type=F32) + jnp.dot(lo, tri, preferred_element_type=F32)


def _col_block(t_len, first):
    return pl.BlockSpec((1, t_len, LANES), lambda p: (first + p, 0, 0))


def _sb_iotas():
    row_i = lax.broadcasted_iota(jnp.int32, (SB_BQ, SB_BLOCK), 0)
    col_i = lax.broadcasted_iota(jnp.int32, (SB_BQ, SB_BLOCK), 1)
    sq_r = lax.broadcasted_iota(jnp.int32, (SB_BLOCK, SB_BLOCK), 0)
    sq_c = lax.broadcasted_iota(jnp.int32, (SB_BLOCK, SB_BLOCK), 1)
    return row_i, col_i, sq_r, sq_c


SB_DIAG_BLOCKS = SB_BQ // SB_BLOCK
SB_EXP_FLOOR = -110.0


def _sb_keys_descending(qi, tile, carry, z_bounds, n_heads, has_free):
    group = SB_DIAG_BLOCKS
    n_free = group * qi
    diag = list(range(group - 1, -1, -1))
    carry = tile([n_free + j for j in diag], [True] * group, carry, [j * SB_BLOCK for j in diag])

    def largest_exponent(c):
        worst = jnp.max(z_bounds[0] - c[1])
        for h in range(1, n_heads):
            worst = jnp.maximum(worst, jnp.max(z_bounds[h] - c[1 + h]))
        return worst

    always = group if has_free else 0

    def cond(state):
        return (state[0] < n_free) & ((state[1] > SB_EXP_FLOOR) | (state[0] < always))

    def body(state):
        first = n_free - 1 - state[0]
        c = tile([first - j for j in range(group)], [False] * group, state[2:])
        return (state[0] + group, largest_exponent(c), *c)

    out = lax.while_loop(cond, body, (jnp.int32(0), largest_exponent(carry), *carry))
    return out[2:], out[0]


def _sb_keys_ascending(qi, n_run, tile, carry, has_free):
    group = SB_DIAG_BLOCKS
    n_free = group * qi
    diag = list(range(group))
    kjs, los, masked = [n_free + j for j in diag], [j * SB_BLOCK for j in diag], [True] * group
    if has_free:
        early = lambda s: [n_free - n_run + group * s + j for j in range(group)]
        carry = lax.fori_loop(0, n_run // group - 1, lambda s, c: tile(early(s), [False] * group, c), carry)
        kjs, los, masked = [n_free - group + j for j in range(group)] + kjs, [0] * group + los, [False] * group + masked
    return tile(kjs, masked, carry, los)


def _sb_fwd_call(cols, t_len):
    nq = t_len // SB_BQ
    scale = float(SB_HEAD_DIM) ** -0.5
    n_pairs = 512 // LANES
    per_pair = LANES // SB_HEAD_DIM

    def body(q_blk, k_blk, v_blk, o_blk, st_ref, nrun_ref):
        q_ref, k_ref, v_ref, o_ref = q_blk.at[0], k_blk.at[0], v_blk.at[0], o_blk.at[0]
        lane = lax.broadcasted_iota(jnp.int32, (1, LANES), 1)
        row_i, col_i, sq_r, sq_c = _sb_iotas()
        ge = (sq_r >= sq_c).astype(MXU_DTYPE)
        hms = [((lane // SB_HEAD_DIM) == hh).astype(F32) for hh in range(per_pair)]
        k_sq = k_ref[...] * k_ref[...]
        k_norms = [jnp.sqrt(jnp.max(jnp.sum(k_sq * hm, axis=-1, keepdims=True))) * (1.02 * scale) for hm in hms]

        def q_block(qi, has_free):
            r0 = qi * SB_BQ if isinstance(qi, int) else pl.multiple_of(qi * SB_BQ, SB_BQ)
            rows = pl.ds(r0, SB_BQ)
            q_all = q_ref[rows, :]
            qms = [(q_all * (hm * scale)).astype(MXU_DTYPE) for hm in hms]
            z_bounds = [jnp.sqrt(jnp.sum(q_all * q_all * hm, axis=-1, keepdims=True)) * kn
                        for hm, kn in zip(hms, k_norms)]

            def tile(kjs, masked, kc, los=None):
                heads = range(per_pair)
                los = los or [0] * len(kjs)
                pairs = [(t, h) for t in range(len(kjs)) for h in heads]
                add_rows = lambda full, lo, part: full + part if lo == 0 else jnp.concatenate(
                    [full[:lo], full[lo:] + part], axis=0)
                acc, cs = kc[0], list(kc[1:])
                s0s = [kj * SB_BLOCK if isinstance(kj, int) else pl.multiple_of(kj * SB_BLOCK, SB_BLOCK) for kj in kjs]
                kbs = [k_ref[pl.ds(s0, SB_BLOCK), :].astype(MXU_DTYPE) for s0 in s0s]
                v_alls = [v_ref[pl.ds(s0, SB_BLOCK), :] for s0 in s0s]
                vms = {(t, h): (v_alls[t] * hms[h]).astype(MXU_DTYPE) for t, h in pairs}
                zs = {(t, h): lax.dot_general(qms[h][los[t]:], kbs[t], _NT, preferred_element_type=F32)
                      for t, h in pairs}
                masks = [(col_i[lo:] + s0) < (row_i[lo:] + r0) if m else None for m, lo, s0 in zip(masked, los, s0s)]
                keep = lambda t, a: a if masks[t] is None else jnp.where(masks[t], a, 0.0)
                sps = {(t, h): keep(t, _softplus(zs[t, h])) for t, h in pairs}
                sums = {p: _running_sum_mm(sps[p], ge) for p in pairs}
                mass = {}
                for t, h in pairs:
                    mass[t, h] = cs[h] if t == 0 else add_rows(
                        mass[t - 1, h], los[t - 1], jnp.sum(sps[t - 1, h], axis=-1, keepdims=True))
                ws = {(t, h): keep(t, jnp.exp(zs[t, h] - (sums[t, h] + mass[t, h][los[t]:]))) for t, h in pairs}
                for t, h in pairs:
                    acc = add_rows(acc, los[t], jnp.dot(ws[t, h].astype(MXU_DTYPE), vms[t, h],
                                                        preferred_element_type=F32))
                last = len(kjs) - 1
                cs = [add_rows(mass[last, h], los[last], jnp.sum(sps[last, h], axis=-1, keepdims=True)) for h in heads]
                return (acc, *cs)

            zero_col = jnp.zeros((SB_BQ, 1), F32)
            out, n_run = _sb_keys_descending(
                qi, tile, (jnp.zeros((SB_BQ, LANES), F32),) + (zero_col,) * per_pair, z_bounds, per_pair, has_free)
            o_ref[rows, :] = out[0]
            masses = jnp.zeros((SB_BQ, LANES), F32)
            for hh in range(per_pair):
                masses = jnp.where(lane == hh, out[1 + hh], masses)
            st_ref[rows, :] = masses
            nrun_ref[pl.program_id(0), qi] = n_run

        q_block(0, False)
        lax.fori_loop(1, nq, lambda qi, carry: (q_block(qi, True), carry)[1], 0)

    return pl.pallas_call(
        body, name="sb_fwd",
        grid=(n_pairs,),
        in_specs=[_col_block(t_len, SB_FIRST_BLOCK), _col_block(t_len, SB_FIRST_BLOCK + n_pairs),
                  _col_block(t_len, SB_FIRST_BLOCK + 2 * n_pairs)],
        out_specs=[_col_block(t_len, 0),
                   pl.BlockSpec((t_len, LANES), lambda p: (0, p)),
                   pl.BlockSpec(memory_space=pltpu.SMEM)],
        out_shape=[jax.ShapeDtypeStruct((n_pairs, t_len, LANES), F32),
                   jax.ShapeDtypeStruct((t_len, n_pairs * LANES), F32),
                   jax.ShapeDtypeStruct((n_pairs, nq), jnp.int32)],
        compiler_params=_params("arbitrary"),
    )(cols, cols, cols)


def _sb_bwd_call(cols, sp_total, n_run_all, d_o, dproj, t_len):
    nq = t_len // SB_BQ
    scale = float(SB_HEAD_DIM) ** -0.5
    n_pairs = 512 // LANES
    per_pair = LANES // SB_HEAD_DIM

    def body(q_blk, k_blk, v_blk, st_ref, nrun_ref, do_blk, dproj_in_ref, d_ref):
        q_ref, k_ref, v_ref, do_ref = q_blk.at[0], k_blk.at[0], v_blk.at[0], do_blk.at[0]
        lane = lax.broadcasted_iota(jnp.int32, (1, LANES), 1)
        row_i, col_i, sq_r, sq_c = _sb_iotas()
        lt = (sq_r < sq_c).astype(MXU_DTYPE)
        le = (sq_r <= sq_c).astype(MXU_DTYPE)
        hms = [((lane // SB_HEAD_DIM) == hh).astype(F32) for hh in range(per_pair)]
        d_ref[1] = jnp.zeros((t_len, LANES), F32)
        d_ref[2] = jnp.zeros((t_len, LANES), F32)

        def q_block(qi, has_free):
            r0 = qi * SB_BQ if isinstance(qi, int) else pl.multiple_of(qi * SB_BQ, SB_BQ)
            rows = pl.ds(r0, SB_BQ)
            q_all, do_all = q_ref[rows, :], do_ref[rows, :]
            qms = [(q_all * (hm * scale)).astype(MXU_DTYPE) for hm in hms]
            doms = [(do_all * hm).astype(MXU_DTYPE) for hm in hms]
            masses = st_ref[rows, :]
            totals = [jnp.sum(jnp.where(lane == hh, masses, 0.0), axis=-1, keepdims=True) for hh in range(per_pair)]

            def tile(kjs, masked, kc, los=None):
                heads = range(per_pair)
                los = los or [0] * len(kjs)
                pairs = [(t, h) for t in range(len(kjs)) for h in heads]
                add_rows = lambda full, lo, part: full + part if lo == 0 else jnp.concatenate(
                    [full[:lo], full[lo:] + part], axis=0)
                rsum = lambda a: jnp.sum(a, axis=-1, keepdims=True)
                dq, cls, gls = kc[0], list(kc[1:1 + per_pair]), list(kc[1 + per_pair:])
                s0s = [kj * SB_BLOCK if isinstance(kj, int) else pl.multiple_of(kj * SB_BLOCK, SB_BLOCK) for kj in kjs]
                k_alls = [k_ref[pl.ds(s0, SB_BLOCK), :] for s0 in s0s]
                v_alls = [v_ref[pl.ds(s0, SB_BLOCK), :] for s0 in s0s]
                kbs = [k_all.astype(MXU_DTYPE) for k_all in k_alls]
                vms = {(t, h): (v_alls[t] * hms[h]).astype(MXU_DTYPE) for t, h in pairs}
                kms = {(t, h): (k_alls[t] * (hms[h] * scale)).astype(MXU_DTYPE) for t, h in pairs}
                q_live = {(t, h): qms[h][los[t]:] for t, h in pairs}
                do_live = {(t, h): doms[h][los[t]:] for t, h in pairs}
                zs = {p: lax.dot_general(q_live[p], kbs[p[0]], _NT, preferred_element_type=F32) for p in pairs}
                das = {p: lax.dot_general(do_live[p], vms[p], _NT, preferred_element_type=F32) for p in pairs}
                masks = [(col_i[lo:] + s0) < (row_i[lo:] + r0) if m else None for m, lo, s0 in zip(masked, los, s0s)]
                keep = lambda t, a: a if masks[t] is None else jnp.where(masks[t], a, 0.0)
                sp_alls = {p: _softplus(zs[p]) for p in pairs}
                sps = {(t, h): keep(t, sp_alls[t, h]) for t, h in pairs}
                lefts = {p: _running_sum_mm(sps[p], lt) for p in pairs}
                cl = {}
                for t, h in pairs:
                    cl[t, h] = cls[h] if t == 0 else add_rows(cl[t - 1, h], los[t - 1], rsum(sps[t - 1, h]))
                ws = {(t, h): keep(t, jnp.exp(zs[t, h] - ((totals[h] - cl[t, h])[los[t]:] - lefts[t, h])))
                      for t, h in pairs}
                gs = {p: das[p] * ws[p] for p in pairs}
                g_sums = {p: _running_sum_mm(gs[p], le) for p in pairs}
                gl = {}
                for t, h in pairs:
                    gl[t, h] = gls[h] if t == 0 else add_rows(gl[t - 1, h], los[t - 1], rsum(gs[t - 1, h]))
                dzs = {(t, h): keep(t, gs[t, h] - jnp.exp(zs[t, h] - sp_alls[t, h]) * (gl[t, h][los[t]:] + g_sums[t, h])
                               ).astype(MXU_DTYPE) for t, h in pairs}
                for t in range(len(kjs)):
                    dk_t = jnp.zeros((SB_BLOCK, LANES), F32)
                    dv_t = jnp.zeros((SB_BLOCK, LANES), F32)
                    for h in heads:
                        dq = add_rows(dq, los[t], jnp.dot(dzs[t, h], kms[t, h], preferred_element_type=F32))
                        dk_t = dk_t + lax.dot_general(dzs[t, h], q_live[t, h], _TN, preferred_element_type=F32)
                        dv_t = dv_t + lax.dot_general(ws[t, h].astype(MXU_DTYPE), do_live[t, h], _TN,
                                                      preferred_element_type=F32)
                    d_ref[1, pl.ds(s0s[t], SB_BLOCK), :] += dk_t
                    d_ref[2, pl.ds(s0s[t], SB_BLOCK), :] += dv_t
                last = len(kjs) - 1
                cls = [add_rows(cl[last, h], los[last], rsum(sps[last, h])) for h in heads]
                gls = [add_rows(gl[last, h], los[last], rsum(gs[last, h])) for h in heads]
                return (dq, *cls, *gls)

            zero_col = jnp.zeros((SB_BQ, 1), F32)
            out = _sb_keys_ascending(qi, nrun_ref[pl.program_id(0), qi], tile,
                                     (jnp.zeros((SB_BQ, LANES), F32),) + (zero_col,) * (2 * per_pair), has_free)
            d_ref[0, rows, :] = out[0]

        q_block(0, False)
        lax.fori_loop(1, nq, lambda qi, carry: (q_block(qi, True), carry)[1], 0)

    return pl.pallas_call(
        body, name="sb_bwd",
        grid=(n_pairs,),
        in_specs=[_col_block(t_len, SB_FIRST_BLOCK), _col_block(t_len, SB_FIRST_BLOCK + n_pairs),
                  _col_block(t_len, SB_FIRST_BLOCK + 2 * n_pairs),
                  pl.BlockSpec((t_len, LANES), lambda p: (0, p)),
                  pl.BlockSpec(memory_space=pltpu.SMEM), _col_block(t_len, 0), _HBM],
        out_specs=pl.BlockSpec((3, t_len, LANES), lambda p: (DPROJ_SB_SLOT // 3, 0, p)),
        out_shape=jax.ShapeDtypeStruct(dproj.shape, dproj.dtype),
        input_output_aliases={6: 0},
        compiler_params=_params("arbitrary"),
    )(cols, cols, cols, sp_total, n_run_all, d_o, dproj)


def _conv_taps(xin, rows, t_len):
    taps = []
    for i in range(CONV_WIDTH):
        shift = CONV_WIDTH - 1 - i
        if shift == 0:
            taps.append(xin)
        else:
            taps.append(jnp.where(rows >= shift, pltpu.roll(xin, shift, axis=0), 0.0))
    return taps


def _gdn_prep_body_common(x_ref, w_ref, t_len):
    j = pl.program_id(0)
    xin = x_ref[...]
    rows = lax.broadcasted_iota(jnp.int32, (t_len, LANES), 0)
    taps = _conv_taps(xin, rows, t_len)
    pre = taps[0] * w_ref[0:1, :]
    for i in range(1, CONV_WIDTH):
        pre = pre + taps[i] * w_ref[i:i + 1, :]
    sg = _sigmoid(pre)
    act = pre * sg
    is_qk = j < 2 * GDN_HEADS
    nrm = jnp.where(is_qk, lax.rsqrt(jnp.sum(act * act, axis=-1, keepdims=True) + EPS), 1.0)
    sc = jnp.where(j < GDN_HEADS, float(GDN_HEAD_DIM) ** -0.5, 1.0)
    return j, rows, taps, pre, sg, act, is_qk, nrm, sc


def _gdn_prep_call(cols, conv_w, t_len):
    def body(x_blk, w_ref, out_ref):
        _, _, _, _, _, act, _, nrm, sc = _gdn_prep_body_common(x_blk.at[0], w_ref, t_len)
        out_ref[...] = act * nrm * sc

    return pl.pallas_call(
        body, name="gdn_prep",
        grid=(3 * GDN_HEADS,),
        in_specs=[_col_block(t_len, GDN_FIRST_BLOCK),
                  pl.BlockSpec((CONV_WIDTH, LANES), lambda j: (0, j))],
        out_specs=pl.BlockSpec((t_len, LANES), lambda j: (0, j)),
        out_shape=jax.ShapeDtypeStruct((t_len, 3 * 512), F32),
        compiler_params=_params("arbitrary"),
    )(cols, conv_w)


def _gdn_prep_bwd_call(cols, conv_w, d_act3, dproj, t_len):
    def body(x_blk, w_ref, d_ref, dproj_in_ref, dx_ref, dw_ref):
        _, rows, taps, pre, sg, act, is_qk, nrm, sc = _gdn_prep_body_common(x_blk.at[0], w_ref, t_len)
        d_out = d_ref[0]
        dn = d_out * sc
        d_norm = nrm * dn - act * (nrm * nrm * nrm) * jnp.sum(dn * act, axis=-1, keepdims=True)
        d_act = jnp.where(is_qk, d_norm, d_out)
        d_pre = d_act * sg * (1.0 + pre * (1.0 - sg))
        dx = d_pre * w_ref[CONV_WIDTH - 1:CONV_WIDTH, :]
        dw_ref[CONV_WIDTH - 1:CONV_WIDTH, :] = jnp.sum(d_pre * taps[CONV_WIDTH - 1], axis=0, keepdims=True)
        for i in range(CONV_WIDTH - 1):
            shift = CONV_WIDTH - 1 - i
            up = jnp.where(rows < t_len - shift, pltpu.roll(d_pre, t_len - shift, axis=0), 0.0)
            dx = dx + up * w_ref[i:i + 1, :]
            dw_ref[i:i + 1, :] = jnp.sum(d_pre * taps[i], axis=0, keepdims=True)
        dx_ref[0] = dx

    return pl.pallas_call(
        body, name="gdn_prep_bwd",
        grid=(3 * GDN_HEADS,),
        in_specs=[_col_block(t_len, GDN_FIRST_BLOCK),
                  pl.BlockSpec((CONV_WIDTH, LANES), lambda j: (0, j)),
                  pl.BlockSpec((1, t_len, LANES), lambda j: (j // GDN_HEADS, 0, j % GDN_HEADS)), _HBM],
        out_specs=[pl.BlockSpec((1, t_len, LANES), lambda j: (DPROJ_GDN_SLOT + j // GDN_HEADS, 0, j % GDN_HEADS)),
                   pl.BlockSpec((CONV_WIDTH, LANES), lambda j: (0, j))],
        out_shape=[jax.ShapeDtypeStruct(dproj.shape, dproj.dtype),
                   jax.ShapeDtypeStruct((CONV_WIDTH, 3 * 512), F32)],
        input_output_aliases={3: 0},
        compiler_params=_params("arbitrary"),
    )(cols, conv_w, d_act3, dproj)


def _chunk_cumsum_matrix():
    r = lax.broadcasted_iota(jnp.int32, (LANES, LANES), 0)
    c = lax.broadcasted_iota(jnp.int32, (LANES, LANES), 1)
    return ((r <= c) & ((r // CHUNK) == (c // CHUNK))).astype(F32)


def _gdn_gates_call(ps, pst, alog_l, dtb_l, alog_c, dtb_c, t_len):
    def body(ps_ref, pst_ref, al_ref, dl_ref, ac_ref, dc_ref, beta_ref, gcol_ref, grow_ref):
        upper = _chunk_cumsum_matrix()
        lower = upper.T
        psv = ps_ref[...]
        beta_ref[...] = _sigmoid(psv)
        g_l = -jnp.exp(al_ref[...]) * _softplus(psv + dl_ref[...])
        g_r = -jnp.exp(ac_ref[...]) * _softplus(pst_ref[...] + dc_ref[...])
        for w in range(t_len // LANES):
            sl = slice(w * LANES, (w + 1) * LANES)
            gcol_ref[sl, :] = _mx(lower, g_l[sl, :])
            grow_ref[:, sl] = _mx(g_r[:, sl], upper)

    vm = pl.BlockSpec(memory_space=pltpu.VMEM)
    return pl.pallas_call(
        body, name="gdn_gates",
        in_specs=[vm] * 6, out_specs=[vm] * 3,
        out_shape=[jax.ShapeDtypeStruct((t_len, LANES), F32),
                   jax.ShapeDtypeStruct((t_len, LANES), F32),
                   jax.ShapeDtypeStruct((8, t_len), F32)],
        compiler_params=pltpu.CompilerParams(vmem_limit_bytes=VMEM_LIMIT_BYTES),
    )(ps, pst, alog_l, dtb_l, alog_c, dtb_c)


def _gdn_gates_bwd_call(ps, alog_l, dtb_l, d_l, t_len):
    def body(ps_ref, al_ref, dl_ref, d_ref, dps_ref, gal_ref, gdt_ref):
        lane = lax.broadcasted_iota(jnp.int32, (1, LANES), 1)
        psv = ps_ref[...]
        dv = d_ref[...]
        beta = _sigmoid(psv)
        ea = jnp.exp(al_ref[...])
        arg = psv + dl_ref[...]
        g = -ea * _softplus(arg)
        d_a = dv * (-ea) * _sigmoid(arg)
        is_a = (lane >= GDN_HEADS) & (lane < 2 * GDN_HEADS)
        dps_ref[...] = jnp.where(lane < GDN_HEADS, dv * beta * (1.0 - beta), jnp.where(is_a, d_a, 0.0))
        gdt_ref[...] = jnp.where(is_a, jnp.sum(d_a, axis=0, keepdims=True), 0.0)
        gal_ref[...] = jnp.where(is_a, jnp.sum(dv * g, axis=0, keepdims=True), 0.0)

    vm = pl.BlockSpec(memory_space=pltpu.VMEM)
    return pl.pallas_call(
        body, name="gdn_gates_bwd",
        in_specs=[vm] * 4, out_specs=[vm] * 3,
        out_shape=[jax.ShapeDtypeStruct((t_len, LANES), F32),
                   jax.ShapeDtypeStruct((1, LANES), F32),
                   jax.ShapeDtypeStruct((1, LANES), F32)],
        compiler_params=pltpu.CompilerParams(vmem_limit_bytes=VMEM_LIMIT_BYTES),
    )(ps, alog_l, dtb_l, d_l)


def _bm(a, b):
    return _m3_general(a, b, _BNN)


def _bm_nt(a, b):
    return _m3_general(a, b, _BNT)


def _bm_tn(a, b):
    return _m3_general(a, b, _BTN)


def _heads_of(ref, rows):
    return jnp.stack([ref[rows, h * GDN_HEAD_DIM:(h + 1) * GDN_HEAD_DIM] for h in range(GDN_HEADS)])


def _chunk_terms(q_ref, k_ref, v_ref, b_ref, gc_ref, gr_ref, c, incl, strict, n=1, scores=True):
    r0 = c * CHUNK if isinstance(c, int) else pl.multiple_of(c * CHUNK, CHUNK)
    rows = pl.ds(r0, n * CHUNK)
    per_chunk = lambda x: x.reshape(GDN_HEADS * n, CHUNK, x.shape[-1])
    q, k, v = (per_chunk(_heads_of(ref, rows)) for ref in (q_ref, k_ref, v_ref))
    lane_ids = lax.broadcasted_iota(jnp.int32, (1, LANES), 1)
    pick = lambda slab, first: jnp.stack([jnp.sum(jnp.where(lane_ids == first + h, slab, 0.0), axis=-1, keepdims=True)
                                          for h in range(GDN_HEADS)])
    b = per_chunk(pick(b_ref[rows, :], 0))
    gc = per_chunk(pick(gc_ref[rows, :], GDN_HEADS))
    gr = gr_ref[:, c] if n == 1 else gr_ref[:, c:c + n].reshape(GDN_HEADS * n, 1, CHUNK)
    dm = jnp.where(incl, jnp.exp(jnp.where(incl, gc - gr, 0.0)), 0.0)
    kb = k * b
    vb = v * b
    e = jnp.exp(gc)
    a = p = None
    if scores:
        kk_qk = _bm_nt(jnp.concatenate([kb, q], axis=1), k)
        a = jnp.where(strict, kk_qk[:, :CHUNK] * dm, 0.0)
        p = jnp.where(incl, kk_qk[:, CHUNK:] * dm, 0.0)
    gl = gc[:, CHUNK - 1:CHUNK, :]
    eg = jnp.exp(gl - gc)
    return rows, q, k, v, b, gc, dm, kb, vb, e, a, p, gl, eg


def _unit_lower_inverse(a, eye):
    x = -a
    tm = eye + x
    xp = _bm(x, x)
    for _ in range(4):
        both = _bm(jnp.concatenate([xp, tm], axis=1), xp)
        tm = tm + both[:, CHUNK:]
        xp = both[:, :CHUNK]
    return tm + _bm(tm, xp)


def _gdn_specs(t_len, n_chunks, reverse):
    cps = GDN_CHUNKS_PER_STEP
    steps = n_chunks // cps
    at = (lambda g: steps - 1 - g) if reverse else (lambda g: g)
    rows_blk = lambda width, part=0: pl.BlockSpec((cps * CHUNK, width), lambda g: (at(g), part))
    gate_r = pl.BlockSpec((GDN_HEADS, cps, 1, CHUNK), lambda g: (0, at(g), 0, 0))
    per_chunk = lambda r, c: pl.BlockSpec((GDN_HEADS, cps, r, c), lambda g: (0, at(g), 0, 0))
    return cps, steps, rows_blk, gate_r, per_chunk


def _gdn_fwd_call(gact, beta_c, gam_c, gam_r, t_len):
    n_chunks = t_len // CHUNK
    dk = GDN_HEAD_DIM
    width = GDN_HEADS * dk
    cps, steps, rows_blk, gate_r, per_chunk = _gdn_specs(t_len, n_chunks, False)

    def body(q_ref, k_ref, v_ref, b_ref, gc_ref, gr_ref, o_ref, s_ref, t_ref, a_ref, p_ref, uw_ref, vn_ref, state_ref):
        row = lax.broadcasted_iota(jnp.int32, (CHUNK, CHUNK), 0)
        col = lax.broadcasted_iota(jnp.int32, (CHUNK, CHUNK), 1)
        incl, strict = row >= col, row > col
        eye = (row == col).astype(F32)

        @pl.when(pl.program_id(0) == 0)
        def _():
            state_ref[...] = jnp.zeros_like(state_ref)

        _, q, k, v, b, gc, dm, kb, vb, e, a, p, gl, eg = _chunk_terms(
            q_ref, k_ref, v_ref, b_ref, gc_ref, gr_ref, 0, incl, strict, cps)
        tm = _unit_lower_inverse(a, eye)
        uw = _bm(tm, jnp.concatenate([vb, kb * e], axis=2))
        w_qe = jnp.concatenate([uw[:, :, dk:], q * e], axis=1)
        u, kd, decay = uw[:, :, :dk], k * eg, jnp.exp(gl)
        per_chunk_block = lambda x: x.reshape(GDN_HEADS, cps, CHUNK, CHUNK)
        t_ref[...], a_ref[...], p_ref[...] = per_chunk_block(tm), per_chunk_block(a), per_chunk_block(p)
        uw_heads = uw.reshape(GDN_HEADS, cps * CHUNK, 2 * dk)
        for h in range(GDN_HEADS):
            uw_ref[:, h * 2 * dk:(h + 1) * 2 * dk] = uw_heads[h]

        of_chunk = lambda x, c: jnp.stack([x[h * cps + c] for h in range(GDN_HEADS)])
        s = state_ref[...]
        for c in range(cps):
            ws_qs = _bm(of_chunk(w_qe, c), s)
            vn = of_chunk(u, c) - ws_qs[:, :CHUNK]
            o = ws_qs[:, CHUNK:] + _bm(of_chunk(p, c), vn)
            for h in range(GDN_HEADS):
                o_ref[c * CHUNK:(c + 1) * CHUNK, h * dk:(h + 1) * dk] = o[h]
                vn_ref[c * CHUNK:(c + 1) * CHUNK, h * dk:(h + 1) * dk] = vn[h]
            s_ref[:, c] = s
            s = s * of_chunk(decay, c) + _bm_tn(of_chunk(kd, c), vn)
        state_ref[...] = s

    scores = jax.ShapeDtypeStruct((GDN_HEADS, n_chunks, CHUNK, CHUNK), F32)
    return pl.pallas_call(
        body, name="gdn_fwd",
        grid=(steps,),
        in_specs=[rows_blk(width, 0), rows_blk(width, 1), rows_blk(width, 2), rows_blk(LANES), rows_blk(LANES), gate_r],
        out_specs=[rows_blk(width), per_chunk(dk, dk), per_chunk(CHUNK, CHUNK), per_chunk(CHUNK, CHUNK),
                   per_chunk(CHUNK, CHUNK), rows_blk(2 * width), rows_blk(width)],
        out_shape=[jax.ShapeDtypeStruct((t_len, width), F32),
                   jax.ShapeDtypeStruct((GDN_HEADS, n_chunks, dk, dk), F32), scores, scores, scores,
                   jax.ShapeDtypeStruct((t_len, 2 * width), F32), jax.ShapeDtypeStruct((t_len, width), F32)],
        scratch_shapes=[pltpu.VMEM((GDN_HEADS, dk, dk), F32)],
        compiler_params=_params("arbitrary"),
    )(gact, gact, gact, beta_c, gam_c, gam_r)


def _gdn_bwd_call(gact, beta_c, gam_c, gam_r, saved, d_o, t_len, scatter=()):
    n_chunks = t_len // CHUNK
    dk = GDN_HEAD_DIM
    width = GDN_HEADS * dk
    cps, steps, rows_blk, gate_r, per_chunk = _gdn_specs(t_len, n_chunks, True)
    nx = len(scatter)
    n_in = 13

    def body(*refs):
        q_ref, k_ref, v_ref, b_ref, gc_ref, gr_ref = refs[:6]
        saved_refs, do_ref = refs[6:12], refs[12]
        d_ref, dgate_ref = refs[n_in + nx:n_in + 2 + nx]
        dstate_ref = refs[n_in + 2 + 2 * nx]
        copies = lambda: _direct_copies(refs[n_in:n_in + nx], refs[n_in + 2 + nx:n_in + 2 + 2 * nx],
                                        *refs[n_in + 3 + 2 * nx:], (True,) * nx)
        if nx:
            pl.when(pl.program_id(0) == 0)(lambda: _start_all(copies()))
        row = lax.broadcasted_iota(jnp.int32, (CHUNK, CHUNK), 0)
        col = lax.broadcasted_iota(jnp.int32, (CHUNK, CHUNK), 1)
        incl, strict = row >= col, row > col
        ng = GDN_BWD_GROUP
        nb = GDN_HEADS * ng
        upper = jnp.broadcast_to((row <= col).astype(F32), (nb, CHUNK, CHUNK))
        ones = jnp.ones((nb, CHUNK, LANES), F32)
        last_row = lax.broadcasted_iota(jnp.int32, (CHUNK, 1), 0) == CHUNK - 1
        lane_ids = lax.broadcasted_iota(jnp.int32, (1, LANES), 1)
        rsum = lambda m: jnp.sum(m, axis=-1, keepdims=True)
        total = lambda m: jnp.sum(rsum(m), axis=1, keepdims=True)
        of_chunk = lambda x, c: jnp.stack([x[h * ng + c] for h in range(GDN_HEADS)])

        @pl.when(pl.program_id(0) == 0)
        def _():
            dstate_ref[...] = jnp.zeros_like(dstate_ref)

        for c0 in range(cps - ng, -1, -ng):
            group(c0, q_ref, k_ref, v_ref, b_ref, gc_ref, gr_ref, saved_refs, do_ref, d_ref, dgate_ref, dstate_ref,
                  incl, strict, upper, ones, last_row, lane_ids, rsum, total, of_chunk)
        if nx:
            pl.when(pl.program_id(0) == steps - 1)(lambda: _wait_all(copies()))

    def group(c0, q_ref, k_ref, v_ref, b_ref, gc_ref, gr_ref, saved_refs, do_ref, d_ref, dgate_ref, dstate_ref,
              incl, strict, upper, ones, last_row, lane_ids, rsum, total, of_chunk):
        ng = GDN_BWD_GROUP
        nb = GDN_HEADS * ng
        rows = pl.ds(c0 * CHUNK, ng * CHUNK)
        s_ref, t_ref, a_ref, p_ref, uw_ref, vn_ref = saved_refs
        _, q, k, v, b, gc, dm, kb, vb, e, _, _, gl, eg = _chunk_terms(
            q_ref, k_ref, v_ref, b_ref, gc_ref, gr_ref, c0, incl, strict, ng, scores=False)
        s = s_ref[:, c0:c0 + ng].reshape(nb, dk, dk)
        tm = t_ref[:, c0:c0 + ng].reshape(nb, CHUNK, CHUNK)
        a = a_ref[:, c0:c0 + ng].reshape(nb, CHUNK, CHUNK)
        p = p_ref[:, c0:c0 + ng].reshape(nb, CHUNK, CHUNK)
        d_out = _heads_of(do_ref, rows).reshape(nb, CHUNK, dk)
        vn = _heads_of(vn_ref, rows).reshape(nb, CHUNK, dk)
        uw = jnp.stack([uw_ref[rows, h * 2 * dk:(h + 1) * 2 * dk] for h in range(GDN_HEADS)]).reshape(nb, CHUNK, 2 * dk)
        u, w = uw[:, :, :dk], uw[:, :, dk:]
        el = jnp.exp(gl)
        kbe = kb * e
        qe = q * e
        kd = k * eg
        pt_do = _bm_tn(p, d_out)
        qet_do = _bm_tn(qe, d_out)

        ds = dstate_ref[...]
        d_vn_c, ds_c = [None] * ng, [None] * ng
        for c in range(ng - 1, -1, -1):
            ds_c[c] = ds
            d_vn_c[c] = of_chunk(pt_do, c) + _bm(of_chunk(kd, c), ds)
            ds = of_chunk(el, c) * ds + of_chunk(qet_do, c) - _bm_tn(of_chunk(w, c), d_vn_c[c])
        dstate_ref[...] = ds
        by_chunk = lambda xs: jnp.stack([xs[c][h] for h in range(GDN_HEADS) for c in range(ng)])
        d_vn, ds = by_chunk(d_vn_c), by_chunk(ds_c)

        on_s = _bm_nt(jnp.concatenate([d_out, d_vn], axis=1), s)
        d_qe, d_w = on_s[:, :CHUNK], -on_s[:, CHUNK:]
        d_p = jnp.where(incl, _bm_nt(d_out, vn), 0.0)
        d_kd = _bm_nt(vn, ds)
        d_both = _bm_tn(tm, jnp.concatenate([d_vn, d_w], axis=2))
        d_vb, d_kbe = d_both[:, :, :dk], d_both[:, :, dk:]
        d_a = -jnp.where(strict, _bm_nt(d_both, uw), 0.0)
        m = d_a * dm
        n = d_p * dm
        on_k = _bm(jnp.concatenate([m, n], axis=1), k)
        d_kb = on_k[:, :CHUNK] + d_kbe * e
        d_q = on_k[:, CHUNK:] + d_qe * e
        d_k = (_bm_tn(jnp.concatenate([m, n], axis=1), jnp.concatenate([kb, q], axis=1))
               + d_kd * eg + b * d_kb)
        d_v = b * d_vb
        r = d_a * a + d_p * p
        kd_term = rsum(d_kd * kd)
        d_gl = total(ds * s) * el + jnp.sum(kd_term, axis=1, keepdims=True)
        d_gam = (rsum(r) - _times_exact(r, ones, _BTN)[:, :, 0:1] + rsum(d_qe * qe) + rsum(d_kbe * kbe) - kd_term
                 + jnp.where(last_row, d_gl, 0.0))
        d_beta = rsum(d_kb * k) + rsum(d_vb * v)
        d_g = _exact_times(upper, d_gam * ones, _BNN)[:, :, 0:1]
        per_head = lambda x: x.reshape(GDN_HEADS, ng * CHUNK, x.shape[-1])
        d_q, d_k, d_v, d_beta, d_g = (per_head(x) for x in (d_q, d_k, d_v, d_beta, d_g))
        gates = jnp.zeros((ng * CHUNK, LANES), F32)
        for h in range(GDN_HEADS):
            lanes = slice(h * dk, (h + 1) * dk)
            d_ref[0, rows, lanes] = d_q[h]
            d_ref[1, rows, lanes] = d_k[h]
            d_ref[2, rows, lanes] = d_v[h]
            gates = gates + (jnp.where(lane_ids == h, d_beta[h], 0.0)
                             + jnp.where(lane_ids == GDN_HEADS + h, d_g[h], 0.0))
        dgate_ref[rows, :] = gates

    d_spec = pl.BlockSpec((3, cps * CHUNK, width), lambda g: (0, steps - 1 - g, 0))
    return pl.pallas_call(
        body, name="gdn_bwd",
        grid=(steps,),
        in_specs=[rows_blk(width, 0), rows_blk(width, 1), rows_blk(width, 2), rows_blk(LANES), rows_blk(LANES), gate_r,
                  per_chunk(dk, dk), per_chunk(CHUNK, CHUNK), per_chunk(CHUNK, CHUNK), per_chunk(CHUNK, CHUNK),
                  rows_blk(2 * width), rows_blk(width), rows_blk(width)] + [_HBM] * nx,
        out_specs=[d_spec, rows_blk(LANES)] + [_HBM] * nx,
        out_shape=[jax.ShapeDtypeStruct((3, t_len, width), F32),
                   jax.ShapeDtypeStruct((t_len, LANES), F32)] + _direct_out_shapes(scatter, (True,) * nx),
        scratch_shapes=[pltpu.VMEM((GDN_HEADS, dk, dk), F32)] + (_direct_semaphores(nx) if nx else []),
        compiler_params=_params("arbitrary"),
    )(gact, gact, gact, beta_c, gam_c, gam_r, *saved, d_o, *scatter)


def _group_sums(x, group):
    rows, width = x.shape
    lane = lax.broadcasted_iota(jnp.int32, (1, LANES), 1)
    out = []
    for t in range(width // LANES):
        seg = x[:, t * LANES:(t + 1) * LANES]
        if group == LANES:
            out.append(jnp.broadcast_to(jnp.sum(seg, axis=-1, keepdims=True), (rows, LANES)))
        else:
            low = jnp.sum(jnp.where(lane < group, seg, 0.0), axis=-1, keepdims=True)
            high = jnp.sum(jnp.where(lane < group, 0.0, seg), axis=-1, keepdims=True)
            out.append(jnp.where(lane < group, low, high))
    return jnp.concatenate(out, axis=1)


def _post_call(o_sb, o_gd, proj_gates, x, target, w_out, sbw, gdw, fw, tm=256):
    t_len, d = x.shape
    half = 512
    sb_blocks = half // LANES

    def body(osb_ref, ogd_ref, zsb_ref, zgd_ref, x_ref, tg_ref, wo_ref, sbw_ref, gdw_ref, fw_ref,
             dx2_ref, dosb_ref, dogd_ref, dz_ref, loss_ref, gfw_ref, gsb_ref, ggd_ref, gwo_ref):
        step = pl.program_id(0)

        @pl.when(step == 0)
        def _():
            loss_ref[...] = jnp.zeros_like(loss_ref)
            gfw_ref[...] = jnp.zeros_like(gfw_ref)
            gsb_ref[...] = jnp.zeros_like(gsb_ref)
            ggd_ref[...] = jnp.zeros_like(ggd_ref)
            gwo_ref[...] = jnp.zeros_like(gwo_ref)

        def head_forward(o, z, w, head_dim):
            r = lax.rsqrt(_group_sums(o * o, head_dim) * (1.0 / head_dim) + EPS)
            nrm = o * r * w
            sg = _sigmoid(z)
            return r, nrm, sg, nrm * (z * sg)

        def head_backward(d_m, o, z, w, head_dim, r, nrm, sg):
            d_n = d_m * (z * sg)
            d_z = d_m * nrm * (sg * (1.0 + z * (1.0 - sg)))
            dnw = d_n * w
            d_o = r * dnw - o * (r * r * r) * (_group_sums(dnw * o, head_dim) * (1.0 / head_dim))
            return d_o, d_z, jnp.sum(d_n * o * r, axis=0, keepdims=True)

        osb = jnp.concatenate([osb_ref[j] for j in range(sb_blocks)], axis=1)
        ogd, zsb, zgd = ogd_ref[...], zsb_ref[...], zgd_ref[...]
        sbw_v, gdw_v = sbw_ref[...], gdw_ref[...]
        r_sb, n_sb, sg_sb, m_sb = head_forward(osb, zsb, sbw_v, SB_HEAD_DIM)
        r_gd, n_gd, sg_gd, m_gd = head_forward(ogd, zgd, gdw_v, GDN_HEAD_DIM)
        mixed = jnp.concatenate([m_sb, m_gd], axis=1).astype(MXU_DTYPE)
        wo = wo_ref[...]
        x2 = x_ref[...] + jnp.dot(mixed, wo, preferred_element_type=F32)
        r2 = lax.rsqrt(jnp.mean(x2 * x2, axis=-1, keepdims=True) + EPS)
        fw_v = fw_ref[...]
        err = x2 * r2 * fw_v - tg_ref[...]
        loss_ref[...] += 0.5 * jnp.sum(jnp.sum(err * err, axis=-1, keepdims=True) * (1.0 / d))
        dy = err * (1.0 / d)
        gg = dy * fw_v
        dx2 = r2 * gg - x2 * ((r2 * r2 * r2) * jnp.mean(gg * x2, axis=-1, keepdims=True))
        gfw_ref[...] += jnp.sum(dy * x2 * r2, axis=0, keepdims=True)
        dx2_ref[...] = dx2
        dx2b = dx2.astype(MXU_DTYPE)
        d_mixed = lax.dot_general(dx2b, wo, _NT, preferred_element_type=F32)
        gwo_ref[...] += lax.dot_general(mixed, dx2b, _TN, preferred_element_type=F32)
        d_osb, d_zsb, gsb = head_backward(d_mixed[:, :half], osb, zsb, sbw_v, SB_HEAD_DIM, r_sb, n_sb, sg_sb)
        d_ogd, d_zgd, ggd = head_backward(d_mixed[:, half:], ogd, zgd, gdw_v, GDN_HEAD_DIM, r_gd, n_gd, sg_gd)
        for j in range(sb_blocks):
            dosb_ref[j] = d_osb[:, j * LANES:(j + 1) * LANES]
        dogd_ref[...] = d_ogd
        dz_ref[0] = d_zsb
        dz_ref[1] = d_zgd
        gsb_ref[...] += gsb
        ggd_ref[...] += ggd

    row_blk = lambda w: pl.BlockSpec((tm, w), lambda i: (i, 0))
    blocks_blk = pl.BlockSpec((sb_blocks, tm, LANES), lambda i: (0, i, 0))
    fixed = lambda r, w: pl.BlockSpec((r, w), lambda i: (0, 0))
    return pl.pallas_call(
        body, name="post",
        grid=(t_len // tm,),
        in_specs=[blocks_blk, row_blk(half),
                  pl.BlockSpec((tm, half), lambda i: (i, 0)),
                  pl.BlockSpec((tm, half), lambda i: (i, 1)),
                  row_blk(d), row_blk(d), fixed(d, d), fixed(1, half), fixed(1, half), fixed(1, d)],
        out_specs=[row_blk(d), blocks_blk, row_blk(half),
                   pl.BlockSpec((2, tm, half), lambda i: (DPROJ_GATE_SLOT // 2, i, 0)),
                   fixed(1, LANES), fixed(1, d), fixed(1, half), fixed(1, half), fixed(d, d)],
        out_shape=[jax.ShapeDtypeStruct((t_len, d), F32), jax.ShapeDtypeStruct((sb_blocks, t_len, LANES), F32),
                   jax.ShapeDtypeStruct((t_len, half), F32),
                   jax.ShapeDtypeStruct((len(DPROJ_PIECE_OF_SLOT), t_len, half), F32),
                     jax.ShapeDtypeStruct((1, LANES), F32), jax.ShapeDtypeStruct((1, d), F32),
                     jax.ShapeDtypeStruct((1, half), F32), jax.ShapeDtypeStruct((1, half), F32),
                     jax.ShapeDtypeStruct((d, d), F32)],
        compiler_params=_params("arbitrary"),
    )(o_sb, o_gd, proj_gates, proj_gates, x, target, w_out, sbw, gdw, fw)


def _piece_of_slot(s):
    return jnp.where(s < DPROJ_GDN_SLOT, s, jnp.where(s < DPROJ_GATE_SLOT, s + 1,
                                                     jnp.where(s == DPROJ_GATE_SLOT, 3, 7)))


def _gw_in_call(h_t, dproj8):
    d, t_len = h_t.shape
    n_piece, _, pw = dproj8.shape

    def body(ht_ref, dp_ref, gw_ref):
        gw_ref[...] = jnp.dot(ht_ref[...], dp_ref[0].astype(MXU_DTYPE), preferred_element_type=F32)

    return pl.pallas_call(
        body, name="gw_in",
        grid=(n_piece,),
        in_specs=[pl.BlockSpec((d, t_len), lambda s: (0, 0)),
                  pl.BlockSpec((1, t_len, pw), lambda s: (s, 0, 0))],
        out_specs=pl.BlockSpec((d, pw), lambda s: (0, _piece_of_slot(s))),
        out_shape=jax.ShapeDtypeStruct((d, n_piece * pw), F32),
        compiler_params=_params("arbitrary"),
    )(h_t, dproj8)


def _slot_of_piece(p):
    return jnp.where(p < DPROJ_GDN_SLOT, p, jnp.where(p == 3, DPROJ_GATE_SLOT, jnp.where(p < 7, p - 1, 7)))


def _gw_in_shards_call(h_t, dproj8, dsmall, out_dtype):
    d, t_len = h_t.shape
    n_piece, _, pw = dproj8.shape
    ns = dsmall.shape[1]
    n_pairs = N_DEV // 2

    def body(ht_ref, dp_ref, ds_ref, chip_ref, prev_ref, gates_ref, send_ref, recv_ref, send_sems, recv_sems):
        p = pl.program_id(0)
        x_pos, y_pos, c = lax.axis_index("x"), lax.axis_index("y"), lax.axis_index("c")
        to_sibling = lambda pair: pltpu.make_async_remote_copy(
            src_ref=send_ref.at[pair], dst_ref=recv_ref.at[pair], send_sem=send_sems.at[pair],
            recv_sem=recv_sems.at[pair], device_id=(x_pos, y_pos, 1 - c), device_id_type=_MESH)

        @pl.when(p == 0)
        def _():
            gates_ref[...] = jnp.dot(ht_ref[...], ds_ref[...].astype(MXU_DTYPE), preferred_element_type=F32)

        def emit(s, tail):
            x = jnp.concatenate([prev_ref[...], tail], axis=1)
            y = x if s == 0 else pltpu.roll(x, SHARD_PAD - s, axis=1)
            shard = y[:, :SHARD_COLS].astype(out_dtype)

            @pl.when(c == s % 2)
            def _():
                chip_ref[s // 2] = shard

            @pl.when(c != s % 2)
            def _():
                send_ref[s // 2] = shard
                to_sibling(s // 2).start()

        @pl.when(p < n_piece)
        def _():
            cur = jnp.dot(ht_ref[...], dp_ref[0].astype(MXU_DTYPE), preferred_element_type=F32)
            for s in range(n_piece - 1):
                pl.when(p == s + 1)(functools.partial(emit, s, cur[:, :SHARD_PAD - pw]))
            prev_ref[...] = cur

        @pl.when(p == n_piece)
        def _():
            emit(n_piece - 1, gates_ref[...])
            for pair in range(n_pairs):
                to_sibling(pair).wait_send()
            for pair in range(n_pairs):
                to_sibling(pair).wait_recv()
                chip_ref[pair] = (chip_ref[pair].astype(F32) + recv_ref[pair].astype(F32)).astype(out_dtype)

    shards_of_side = lambda: pltpu.VMEM((n_pairs, d, SHARD_COLS), out_dtype)
    return pl.pallas_call(
        body, name="gw_in",
        grid=(n_piece + 1,),
        in_specs=[pl.BlockSpec((d, t_len), lambda p: (0, 0)),
                  pl.BlockSpec((1, t_len, pw), lambda p: (_slot_of_piece(jnp.minimum(p, n_piece - 1)), 0, 0)),
                  pl.BlockSpec((t_len, ns), lambda p: (0, 0))],
        out_specs=pl.BlockSpec((n_pairs, d, SHARD_COLS), lambda p: (0, 0, 0)),
        out_shape=jax.ShapeDtypeStruct((n_pairs, d, SHARD_COLS), out_dtype),
        scratch_shapes=[pltpu.VMEM((d, pw), F32), pltpu.VMEM((d, ns), F32), shards_of_side(), shards_of_side(),
                        pltpu.SemaphoreType.DMA((n_pairs,)), pltpu.SemaphoreType.DMA((n_pairs,))],
        compiler_params=_params("arbitrary"),
    )(h_t, dproj8, dsmall)


def _gw_small_call(h_t, dsmall, tm=512):
    d, t_len = h_t.shape
    ns = dsmall.shape[1]

    def body(ht_ref, dp_ref, gw_ref):
        @pl.when(pl.program_id(0) == 0)
        def _():
            gw_ref[...] = jnp.zeros_like(gw_ref)

        gw_ref[...] += jnp.dot(ht_ref[...], dp_ref[...].astype(MXU_DTYPE), preferred_element_type=F32)

    return pl.pallas_call(
        body, name="gw_small",
        grid=(t_len // tm,),
        in_specs=[pl.BlockSpec((d, tm), lambda t: (0, t)),
                  pl.BlockSpec((tm, ns), lambda t: (t, 0))],
        out_specs=pl.BlockSpec((d, ns), lambda t: (0, 0)),
        out_shape=jax.ShapeDtypeStruct((d, ns), F32),
        compiler_params=_params("arbitrary"),
    )(h_t, dsmall)


def _dx_call(dproj8, dsmall, w_main, w_small, x, r, dx2, norm_w, chip_scatter=(), tm=256):
    t_len, d = x.shape
    n_piece, _, pw = dproj8.shape
    ns = dsmall.shape[1]
    nx = len(chip_scatter)
    steps = t_len // tm

    def body(*refs):
        dp_ref, ds_ref, wm_ref, ws_ref, x_ref, r_ref, dx2_ref, nw_ref = refs[:8]
        gx_ref, gnw_ref = refs[8 + nx:10 + nx]
        copies = lambda: _chip_copies(refs[8:8 + nx], refs[10 + nx:10 + 2 * nx], *refs[10 + 2 * nx:])
        if nx:
            pl.when(pl.program_id(0) == 0)(lambda: _start_all(copies()))

        @pl.when(pl.program_id(0) == 0)
        def _():
            gnw_ref[...] = jnp.zeros_like(gnw_ref)

        dh = lax.dot_general(ds_ref[...].astype(MXU_DTYPE), ws_ref[...], _NT, preferred_element_type=F32)
        for s, p in enumerate(DPROJ_PIECE_OF_SLOT):
            dh = dh + lax.dot_general(dp_ref[s].astype(MXU_DTYPE), wm_ref[:, p * pw:(p + 1) * pw], _NT,
                                      preferred_element_type=F32)
        xv, rv = x_ref[...], r_ref[...]
        dn = dh * nw_ref[...]
        gx_ref[...] = dx2_ref[...] + rv * dn - xv * ((rv * rv * rv) * jnp.mean(dn * xv, axis=-1, keepdims=True))
        gnw_ref[...] += jnp.sum(dh * xv * rv, axis=0, keepdims=True)
        if nx:
            pl.when(pl.program_id(0) == steps - 1)(lambda: _wait_all(copies()))

    return pl.pallas_call(
        body, name="dx",
        grid=(steps,),
        in_specs=[pl.BlockSpec((n_piece, tm, pw), lambda i: (0, i, 0)),
                  pl.BlockSpec((tm, ns), lambda i: (i, 0)),
                  pl.BlockSpec((d, n_piece * pw), lambda i: (0, 0)),
                  pl.BlockSpec((d, ns), lambda i: (0, 0)),
                  pl.BlockSpec((tm, d), lambda i: (i, 0)),
                  pl.BlockSpec((tm, 1), lambda i: (i, 0)),
                  pl.BlockSpec((tm, d), lambda i: (i, 0)),
                  pl.BlockSpec((1, d), lambda i: (0, 0))] + [_HBM] * nx,
        out_specs=[pl.BlockSpec((tm, d), lambda i: (i, 0)),
                   pl.BlockSpec((1, d), lambda i: (0, 0))] + [_HBM] * nx,
        out_shape=[jax.ShapeDtypeStruct((t_len, d), F32), jax.ShapeDtypeStruct((1, d), F32)]
                  + [jax.ShapeDtypeStruct(a.shape, a.dtype) for a in chip_scatter],
        scratch_shapes=_chip_semaphores(nx) if nx else [],
        compiler_params=_params("arbitrary"),
    )(dproj8, dsmall, w_main, w_small, x, r, dx2, norm_w, *chip_scatter)


def _exchange_call(name, srcs, per_peer):
    n = len(srcs)

    def body(*refs):
        src_refs, out_refs = refs[:n], refs[n:2 * n]
        copies = _direct_copies(src_refs, out_refs, *refs[2 * n:], per_peer)
        _start_all(copies)
        _wait_all(copies)

    hbm = pl.BlockSpec(memory_space=pl.ANY)
    return pl.pallas_call(
        body, name=name,
        in_specs=[hbm] * n, out_specs=[hbm] * n, out_shape=_direct_out_shapes(srcs, per_peer),
        scratch_shapes=_direct_semaphores(n),
    )(*srcs)


def _direct_out_shapes(srcs, per_peer):
    return [jax.ShapeDtypeStruct(s.shape if pp else (N_DEV,) + s.shape, s.dtype) for s, pp in zip(srcs, per_peer)]


def _direct_semaphores(n):
    return [pltpu.SemaphoreType.DMA((n * (N_DEV - 1),)), pltpu.SemaphoreType.DMA((n * (N_DEV - 1),)),
            pltpu.SemaphoreType.DMA((n,))]


def _direct_copies(src_refs, out_refs, send_sems, recv_sems, local_sems, per_peer):
    x, y, c = lax.axis_index("x"), lax.axis_index("y"), lax.axis_index("c")
    me = 4 * x + 2 * y + c
    local, remote = [], []
    for a in range(len(src_refs)):
        mine = src_refs[a].at[me] if per_peer[a] else src_refs[a]
        local.append(pltpu.make_async_copy(mine, out_refs[a].at[me], local_sems.at[a]))
    for k in range(1, N_DEV):
        kx, ky, kc = (k >> 2) & 1, (k >> 1) & 1, k & 1
        px = 1 - x if kx else x
        py = 1 - y if ky else y
        pc = 1 - c if kc else c
        peer = 4 * px + 2 * py + pc
        for a in range(len(src_refs)):
            sem = a * (N_DEV - 1) + (k - 1)
            remote.append(pltpu.make_async_remote_copy(
                src_ref=src_refs[a].at[peer] if per_peer[a] else src_refs[a], dst_ref=out_refs[a].at[me],
                send_sem=send_sems.at[sem], recv_sem=recv_sems.at[sem],
                device_id=(px, py, pc), device_id_type=pl.DeviceIdType.MESH))
    return local, remote


def _start_all(copies):
    local, remote = copies
    for cp in local + remote:
        cp.start()


def _wait_all(copies):
    local, remote = copies
    for cp in remote:
        cp.wait_send()
    for cp in remote:
        cp.wait_recv()
    for cp in local:
        cp.wait()


N_CHIPS = 4
_HBM = pl.BlockSpec(memory_space=pl.ANY)
_MESH = pl.DeviceIdType.MESH


def _gather_call(name, srcs):
    n = len(srcs)
    per = N_DEV - 1

    def body(*refs):
        src_refs, out_refs = refs[:n], refs[n:2 * n]
        send_sems, recv_sems, local_sems = refs[2 * n:]
        x, y, c = lax.axis_index("x"), lax.axis_index("y"), lax.axis_index("c")
        me, sibling = (x, y, c), (x, y, 1 - c)
        x_nbr, y_nbr, diagonal = (1 - x, y), (x, 1 - y), (1 - x, 1 - y)
        held = ((1 - x) * c + x * (1 - c), y * c + (1 - y) * (1 - c))
        onward = (x * c + (1 - x) * (1 - c), (1 - y) * c + y * (1 - c))
        slot = lambda px, py, pc: 4 * px + 2 * py + pc

        def copy(a, k, block, to, from_src=False):
            rows = out_refs[a].at[slot(*block)]
            return pltpu.make_async_remote_copy(
                src_ref=src_refs[a] if from_src else rows, dst_ref=rows,
                send_sem=send_sems.at[a * per + k], recv_sem=recv_sems.at[a * per + k],
                device_id=to, device_id_type=_MESH)

        local = [pltpu.make_async_copy(src_refs[a], out_refs[a].at[slot(*me)], local_sems.at[a]) for a in range(n)]
        started = []

        def start(cp):
            cp.start()
            started.append(cp)

        for cp in local:
            cp.start()
        for a in range(n):
            start(copy(a, 0, me, sibling, True))
            start(copy(a, 1, me, (*x_nbr, c), True))
            start(copy(a, 2, me, (*y_nbr, c), True))
        for a in range(n):
            copy(a, 1, (*x_nbr, c), me).wait_recv()
            copy(a, 2, (*y_nbr, c), me).wait_recv()
            start(copy(a, 3, (*held, c), (*onward, c)))
            start(copy(a, 4, (*x_nbr, c), sibling))
            start(copy(a, 5, (*y_nbr, c), sibling))
        for a in range(n):
            copy(a, 3, (*diagonal, c), me).wait_recv()
            start(copy(a, 6, (*diagonal, c), sibling))
        for a in range(n):
            copy(a, 0, sibling, me).wait_recv()
            for k, chip in ((4, x_nbr), (5, y_nbr), (6, diagonal)):
                copy(a, k, (*chip, 1 - c), me).wait_recv()
        for cp in started:
            cp.wait_send()
        for cp in local:
            cp.wait()

    return pl.pallas_call(
        body, name=name,
        in_specs=[_HBM] * n, out_specs=[_HBM] * n,
        out_shape=[jax.ShapeDtypeStruct((N_DEV,) + s.shape, s.dtype) for s in srcs],
        scratch_shapes=[pltpu.SemaphoreType.DMA((n * per,)), pltpu.SemaphoreType.DMA((n * per,)),
                        pltpu.SemaphoreType.DMA((n,))],
    )(*srcs)


def _chip_semaphores(n):
    per = N_CHIPS - 1
    return [pltpu.SemaphoreType.DMA((n * per,)), pltpu.SemaphoreType.DMA((n * per,)), pltpu.SemaphoreType.DMA((n,))]


def _chip_copies(src_refs, out_refs, send_sems, recv_sems, local_sems):
    per = N_CHIPS - 1
    x, y, c = lax.axis_index("x"), lax.axis_index("y"), lax.axis_index("c")
    mine = 2 * x + y
    chips = [(1 - x, y), (x, 1 - y), (1 - x, 1 - y)]
    n = len(src_refs)
    local = [pltpu.make_async_copy(src_refs[a].at[mine], out_refs[a].at[mine], local_sems.at[a]) for a in range(n)]
    remote = []
    for a in range(n):
        for j, (px, py) in enumerate(chips):
            remote.append(pltpu.make_async_remote_copy(
                src_ref=src_refs[a].at[2 * px + py], dst_ref=out_refs[a].at[mine],
                send_sem=send_sems.at[a * per + j], recv_sem=recv_sems.at[a * per + j],
                device_id=(px, py, c), device_id_type=_MESH))
    return local, remote


def _adam_call(name, parts, w, m, v, tr):
    rows, cols = w.shape
    n_slots = parts.shape[0]

    def body(p_ref, w_ref, m_ref, v_ref, g_ref, d_ref, nm_ref, nv_ref):
        g = p_ref[0].astype(F32)
        for s in range(1, n_slots):
            g = g + p_ref[s].astype(F32)
        m_new = ADAM_B1 * m_ref[...] + (1.0 - ADAM_B1) * g
        v_new = ADAM_B2 * v_ref[...] + (1.0 - ADAM_B2) * (g * g)
        m_hat = m_new / (1.0 - ADAM_B1 ** ADAM_STEP)
        v_hat = v_new / (1.0 - ADAM_B2 ** ADAM_STEP)
        g_ref[...] = g
        d_ref[...] = -ADAM_LR * (m_hat / (jnp.sqrt(v_hat) + ADAM_EPS) + ADAM_WD * w_ref[...])
        nm_ref[...] = m_new
        nv_ref[...] = v_new

    blk = pl.BlockSpec((tr, cols), lambda i: (i, 0))
    return pl.pallas_call(
        body, name=name,
        grid=(rows // tr,),
        in_specs=[pl.BlockSpec((n_slots, tr, cols), lambda i: (0, i, 0)), blk, blk, blk],
        out_specs=[blk] * 4,
        out_shape=[jax.ShapeDtypeStruct((rows, cols), F32)] * 4,
        compiler_params=_params("arbitrary"),
    )(parts, w, m, v)


N_PIECES = 8
PIECE = 512
SHARD_COLS = 513
SHARD_PAD = 640
RELAYOUT_ROWS = 256


def _from_shards_call(shards):
    _, d, _ = shards.shape
    tr = RELAYOUT_ROWS

    def body(p_ref, m_ref, s_ref):
        lane = lax.broadcasted_iota(jnp.int32, (tr, SHARD_PAD), 1)
        pad = jnp.zeros((tr, SHARD_PAD - SHARD_COLS), F32)
        sh = [jnp.concatenate([p_ref[s].astype(F32), pad], axis=1) for s in range(N_DEV)]
        for p in range(N_PIECES):
            y = sh[p] if p == 0 else pltpu.roll(sh[p], p, axis=1)
            if p > 0:
                y = jnp.where(lane < p, pltpu.roll(sh[p - 1], SHARD_PAD - (SHARD_COLS - p), axis=1), y)
            m_ref[:, p * PIECE:(p + 1) * PIECE] = y[:, :PIECE].astype(m_ref.dtype)
        first_gate = N_PIECES * PIECE - (N_DEV - 1) * SHARD_COLS
        s_ref[...] = pltpu.roll(sh[N_DEV - 1], SHARD_PAD - first_gate, axis=1)[:, :LANES].astype(s_ref.dtype)

    return pl.pallas_call(
        body, name="w_in_from_shards",
        grid=(d // tr,),
        in_specs=[pl.BlockSpec((N_DEV, tr, SHARD_COLS), lambda i: (0, i, 0))],
        out_specs=[pl.BlockSpec((tr, N_PIECES * PIECE), lambda i: (i, 0)), pl.BlockSpec((tr, LANES), lambda i: (i, 0))],
        out_shape=[jax.ShapeDtypeStruct((d, N_PIECES * PIECE), shards.dtype),
                   jax.ShapeDtypeStruct((d, LANES), shards.dtype)],
        compiler_params=_params("arbitrary"),
    )(shards)


def _adamw(g, w, m, v):
    m_new = ADAM_B1 * m + (1.0 - ADAM_B1) * g
    v_new = ADAM_B2 * v + (1.0 - ADAM_B2) * (g * g)
    m_hat = m_new / (1.0 - ADAM_B1 ** ADAM_STEP)
    v_hat = v_new / (1.0 - ADAM_B2 ** ADAM_STEP)
    return -ADAM_LR * (m_hat / (jnp.sqrt(v_hat) + ADAM_EPS) + ADAM_WD * w), m_new, v_new


def _adam_small_call(parts, ws, ms, vs):
    n = len(ws)
    n_slots = parts.shape[0]

    def body(*refs):
        p_ref = refs[0]
        w_refs, m_refs, v_refs = refs[1:1 + n], refs[1 + n:1 + 2 * n], refs[1 + 2 * n:1 + 3 * n]
        loss_ref = refs[1 + 3 * n]
        outs = refs[2 + 3 * n:]
        g_all = p_ref[0]
        for s in range(1, n_slots):
            g_all = g_all + p_ref[s]
        loss_ref[...] = g_all[n:n + 1, 0:1]
        for r in range(n):
            size = w_refs[r].shape[1]
            g = g_all[r:r + 1, :size]
            delta, m_new, v_new = _adamw(g, w_refs[r][...], m_refs[r][...], v_refs[r][...])
            for kind, val in enumerate((g, delta, m_new, v_new)):
                outs[kind * n + r][...] = val

    vm = pl.BlockSpec(memory_space=pltpu.VMEM)
    shapes = [jax.ShapeDtypeStruct(w.shape, F32) for w in ws]
    return pl.pallas_call(
        body, name="adam_small",
        in_specs=[vm] * (1 + 3 * n), out_specs=[vm] * (1 + 4 * n),
        out_shape=[jax.ShapeDtypeStruct((1, 1), F32)] + shapes * 4,
    )(parts, *ws, *ms, *vs)


_SMALL_ROWS = ("norm1_w", "final_norm_w", "sb_norm_w", "gdn_norm_w", "gdn_A_log", "gdn_dt_bias", "loss")


def _pack_small(vals, width):
    rows = [jnp.pad(a.reshape(1, -1).astype(F32), ((0, 0), (0, width - a.size))) for a in vals]
    rows += [jnp.zeros((1, width), F32)] * (8 - len(rows))
    return jnp.concatenate(rows, axis=0)


def _device_step(x2d, tgt, w_main, w_small, w_out_full, conv_full, norm1_w, sb_norm_w, gdn_A_log, gdn_dt_bias,
                 gdn_norm_w, final_norm_w, distributed=False):
    t_len, d = x2d.shape
    n_chunks = t_len // CHUNK
    w_main, w_small, w_out_full = (a.astype(MXU_DTYPE) for a in (w_main, w_small, w_out_full))
    w_small_t = w_small[:, :2 * GDN_HEADS].T

    pad_lanes = lambda a, lo: jnp.pad(a.reshape(1, -1), ((0, 0), (lo, LANES - lo - a.size)))
    alog_l, dtb_l = pad_lanes(gdn_A_log, GDN_HEADS), pad_lanes(gdn_dt_bias, GDN_HEADS)
    alog_c, dtb_c = alog_l[:, :8].T, dtb_l[:, :8].T
    sbw = jnp.tile(sb_norm_w, (1, 512 // SB_HEAD_DIM))
    gdw = jnp.tile(gdn_norm_w, (1, 512 // GDN_HEAD_DIM))
    fw = final_norm_w.reshape(1, d)

    if distributed:
        proj_cols, proj_gates, ps, pst, h_t, r1, w_out_g, conv_g = _inproj_call(
            x2d, norm1_w, w_main, w_small, w_small_t, gather=(w_out_full, conv_full))
        w_out_full = w_out_g.reshape(d, d)
        conv_full = conv_g.transpose(1, 0, 2).reshape(CONV_WIDTH, N_DEV * conv_g.shape[2])
    else:
        proj_cols, proj_gates, ps, pst, h_t, r1 = _inproj_call(x2d, norm1_w, w_main, w_small, w_small_t)
    o_sb, sp_total, sb_blocks_run = _sb_fwd_call(proj_cols, t_len)
    gact = _gdn_prep_call(proj_cols, conv_full, t_len)
    beta_l, gcol_l, grow = _gdn_gates_call(ps, pst, alog_l, dtb_l, alog_c, dtb_c, t_len)
    gam_r = grow[GDN_HEADS:2 * GDN_HEADS].reshape(GDN_HEADS, n_chunks, 1, CHUNK)
    o_gd, *gdn_saved = _gdn_fwd_call(gact, beta_l, gcol_l, gam_r, t_len)

    (dx2, d_osb, d_ogd, dproj8, loss_p, g_fw, g_sbw, g_gdw, g_wout) = _post_call(
        o_sb, o_gd, proj_gates, x2d, tgt, w_out_full, sbw, gdw, fw)

    dproj8 = _sb_bwd_call(proj_cols, sp_total, sb_blocks_run, d_osb, dproj8, t_len)
    if distributed:
        d_gact3, d_gates, g_wout = _gdn_bwd_call(gact, beta_l, gcol_l, gam_r, gdn_saved, d_ogd, t_len,
                                                 scatter=(g_wout.reshape(N_DEV, d // N_DEV, d),))
    else:
        d_gact3, d_gates = _gdn_bwd_call(gact, beta_l, gcol_l, gam_r, gdn_saved, d_ogd, t_len)
    dproj8, g_conv = _gdn_prep_bwd_call(proj_cols, conv_full, d_gact3, dproj8, t_len)
    dsmall, g_alog, g_dtb = _gdn_gates_bwd_call(ps, alog_l, dtb_l, d_gates, t_len)

    if distributed:
        chip_partials = _gw_in_shards_call(h_t, dproj8, dsmall, WIRE_DTYPE)
        grad_x, g_n1, g_w_in = _dx_call(dproj8, dsmall, w_main, w_small, x2d, r1, dx2, norm1_w,
                                        chip_scatter=(chip_partials,))
    else:
        grad_x, g_n1 = _dx_call(dproj8, dsmall, w_main, w_small, x2d, r1, dx2, norm1_w)
        g_w_in = (_gw_in_call(h_t, dproj8), _gw_small_call(h_t, dsmall))
    return (loss_p, grad_x, g_n1, g_w_in, g_sbw, g_conv, g_alog, g_dtb, g_gdw, g_wout, g_fw)


def kernel(x, norm1_w, w_in, sb_norm_w, gdn_conv_w, gdn_A_log, gdn_dt_bias, gdn_norm_w, w_out, final_norm_w, loss_target, m_norm1_w, m_w_in, m_sb_norm_w, m_gdn_conv_w, m_gdn_A_log, m_gdn_dt_bias, m_gdn_norm_w, m_w_out, m_final_norm_w, v_norm1_w, v_w_in, v_sb_norm_w, v_gdn_conv_w, v_gdn_A_log, v_gdn_dt_bias, v_gdn_norm_w, v_w_out, v_final_norm_w):
    d = x.shape[2]
    shard_cols = w_in.shape[2]
    conv_cols = gdn_conv_w.shape[2]

    (w_in_g,) = _gather_call("gather_weights", [w_in[0].astype(WIRE_DTYPE)])
    w_main, w_small = _from_shards_call(w_in_g)

    (loss_p, grad_x, g_n1, p_w_in, g_sbw, g_conv, g_alog, g_dtb, g_gdw, p_wout, g_fw) = _device_step(
        x[0], loss_target[0], w_main, w_small, w_out[0].astype(WIRE_DTYPE), gdn_conv_w[0], norm1_w, sb_norm_w,
        gdn_A_log, gdn_dt_bias, gdn_norm_w, final_norm_w, distributed=True)

    g_conv_parts = g_conv.reshape(CONV_WIDTH, N_DEV, conv_cols).transpose(1, 0, 2)
    fold = lambda a, group: a.reshape(-1, group).sum(axis=0)
    small_g = _pack_small([g_n1, g_fw, fold(g_sbw, SB_HEAD_DIM), fold(g_gdw, GDN_HEAD_DIM),
                           g_alog[0, GDN_HEADS:2 * GDN_HEADS], g_dtb[0, GDN_HEADS:2 * GDN_HEADS],
                           loss_p[0, :1]], d)
    p_small, p_conv = _exchange_call("exchange_small", [small_g, g_conv_parts], [False, True])

    r_w_in = _adam_call("adam_w_in", p_w_in, w_in[0], m_w_in[0], v_w_in[0], 256)
    r_wout = _adam_call("adam_w_out", p_wout, w_out[0], m_w_out[0], v_w_out[0], d // N_DEV)
    r_conv = _adam_call("adam_conv", p_conv, gdn_conv_w[0], m_gdn_conv_w[0], v_gdn_conv_w[0], CONV_WIDTH)

    row = lambda a: a.reshape(1, -1)
    n_small = len(_SMALL_ROWS) - 1
    r_small = _adam_small_call(
        p_small,
        [norm1_w, row(final_norm_w), sb_norm_w, gdn_norm_w, gdn_A_log, gdn_dt_bias],
        [m_norm1_w, row(m_final_norm_w), m_sb_norm_w, m_gdn_norm_w, m_gdn_A_log, m_gdn_dt_bias],
        [v_norm1_w, row(v_final_norm_w), v_sb_norm_w, v_gdn_norm_w, v_gdn_A_log, v_gdn_dt_bias])

    def small_out(kind, name):
        out = r_small[1 + kind * n_small + _SMALL_ROWS.index(name)]
        return out.reshape(final_norm_w.shape) if name == "final_norm_w" else out

    def outputs(kind):
        return (small_out(kind, "norm1_w"), r_w_in[kind][None], small_out(kind, "sb_norm_w"), r_conv[kind][None],
                small_out(kind, "gdn_A_log"), small_out(kind, "gdn_dt_bias"), small_out(kind, "gdn_norm_w"),
                r_wout[kind][None], small_out(kind, "final_norm_w"))

    return (r_small[0][0, 0], grad_x[None], *outputs(0), *outputs(1), *outputs(2), *outputs(3))
```

```python
import functools

import jax
import jax.numpy as jnp
from jax import lax
from jax.experimental import pallas as pl
from jax.experimental.pallas import tpu as pltpu

F32 = jnp.float32
MXU_DTYPE = jnp.bfloat16
WIRE_DTYPE = jnp.bfloat16
EXACT = lax.Precision.HIGHEST
EPS = 1e-6
N_DEV = 8
SB_HEAD_DIM = 64
GDN_HEAD_DIM = 128
GDN_HEADS = 4
GDN_CHUNKS_PER_STEP = 4
GDN_BWD_GROUP = 1
CHUNK = 64
CONV_WIDTH = 4
LANES = 128
SB_BLOCK = 128
SB_BQ = 256
VMEM_LIMIT_BYTES = 56 * 1024 * 1024

PIECE_COLS = 512
PROJ_PIECE_KINDS = ("heads", "heads", "heads", "gate", "heads", "heads", "heads", "gate")
SB_FIRST_BLOCK, GDN_FIRST_BLOCK = 0, 12

DPROJ_PIECE_OF_SLOT = (0, 1, 2, 4, 5, 6, 3, 7)
DPROJ_SB_SLOT, DPROJ_GDN_SLOT, DPROJ_GATE_SLOT = 0, 3, 6

ADAM_LR = 0.001
ADAM_B1 = 0.9
ADAM_B2 = 0.999
ADAM_EPS = 1e-08
ADAM_WD = 0.01
ADAM_STEP = 10

_NN = (((1,), (0,)), ((), ()))
_NT = (((1,), (1,)), ((), ()))
_TN = (((0,), (0,)), ((), ()))
_BNN = (((2,), (1,)), ((0,), (0,)))
_BNT = (((2,), (2,)), ((0,), (0,)))
_BTN = (((1,), (1,)), ((0,), (0,)))


def _mx(a, b):
    return jnp.dot(a, b, precision=EXACT, preferred_element_type=F32)


def _split(x):
    hi = x.astype(MXU_DTYPE)
    return hi, (x - hi.astype(F32)).astype(MXU_DTYPE)


def _m3_general(a, b, dims):
    ah, al = _split(a)
    bh, bl = _split(b)
    dot = lambda x, y: lax.dot_general(x, y, dims, preferred_element_type=F32)
    (contract, _), (batch, _) = dims
    free = [ax for ax in range(a.ndim) if ax not in contract and ax not in batch][0]
    m = a.shape[free]
    both = dot(jnp.concatenate([ah, al], axis=free), bh)
    out_axis = len(batch)
    hi_part = lax.slice_in_dim(both, 0, m, axis=out_axis)
    lo_part = lax.slice_in_dim(both, m, 2 * m, axis=out_axis)
    return hi_part + (dot(ah, bl) + lo_part)


def _times_exact(a, b_exact, dims):
    ah, al = _split(a)
    (contract, _), (batch, _) = dims
    free = [ax for ax in range(a.ndim) if ax not in contract and ax not in batch][0]
    m = a.shape[free]
    both = lax.dot_general(jnp.concatenate([ah, al], axis=free), b_exact.astype(MXU_DTYPE), dims,
                           preferred_element_type=F32)
    out_axis = len(batch)
    return lax.slice_in_dim(both, 0, m, axis=out_axis) + lax.slice_in_dim(both, m, 2 * m, axis=out_axis)


def _exact_times(a_exact, b, dims):
    bh, bl = _split(b)
    n = b.shape[-1]
    both = lax.dot_general(a_exact.astype(MXU_DTYPE), jnp.concatenate([bh, bl], axis=-1), dims,
                           preferred_element_type=F32)
    return both[..., :n] + both[..., n:]


def _sigmoid(z):
    return 1.0 / (1.0 + jnp.exp(-z))


def _softplus(z):
    return jnp.maximum(z, 0.0) + jnp.log(1.0 + jnp.exp(-jnp.abs(z)))


def _params(*semantics):
    return pltpu.CompilerParams(dimension_semantics=semantics, vmem_limit_bytes=VMEM_LIMIT_BYTES)


def _inproj_call(x, norm_w, w_main, w_small, w_small_t, gather=(), tm=256):
    t_len, d = x.shape
    n = w_main.shape[1]
    ns = w_small.shape[1]
    nst = w_small_t.shape[0]
    ng = len(gather)
    steps = t_len // tm

    blocks_per_piece = PIECE_COLS // LANES
    n_gate_cols = PIECE_COLS * PROJ_PIECE_KINDS.count("gate")
    n_col_blocks = blocks_per_piece * PROJ_PIECE_KINDS.count("heads")

    def body(*refs):
        x_ref, nw_ref, wm_ref, ws_ref, wst_ref = refs[:5]
        cols_ref, pz_ref, ps_ref, pst_ref, ht_ref, r_ref = refs[5 + ng:11 + ng]
        copies = lambda: _direct_copies(refs[5:5 + ng], refs[11 + ng:11 + 2 * ng], *refs[11 + 2 * ng:], (False,) * ng)
        if ng:
            pl.when(pl.program_id(0) == 0)(lambda: _start_all(copies()))
        xv = x_ref[...]
        r = lax.rsqrt(jnp.mean(xv * xv, axis=-1, keepdims=True) + EPS)
        h = xv * r * nw_ref[...]
        hb = h.astype(MXU_DTYPE)
        n_block = n_gate = 0
        for piece, kind in enumerate(PROJ_PIECE_KINDS):
            out = jnp.dot(hb, wm_ref[:, piece * PIECE_COLS:(piece + 1) * PIECE_COLS], preferred_element_type=F32)
            if kind == "gate":
                pz_ref[:, n_gate * PIECE_COLS:(n_gate + 1) * PIECE_COLS] = out
                n_gate += 1
            else:
                for j in range(blocks_per_piece):
                    cols_ref[n_block + j] = out[:, j * LANES:(j + 1) * LANES]
                n_block += blocks_per_piece
        ps_ref[...] = jnp.dot(hb, ws_ref[...], preferred_element_type=F32)
        pst_ref[...] = lax.dot_general(wst_ref[...], hb, _NT, preferred_element_type=F32)
        ht_ref[...] = h.T.astype(MXU_DTYPE)
        r_ref[...] = r
        if ng:
            pl.when(pl.program_id(0) == steps - 1)(lambda: _wait_all(copies()))

    return pl.pallas_call(
        body, name="inproj",
        grid=(steps,),
        in_specs=[pl.BlockSpec((tm, d), lambda i: (i, 0)),
                  pl.BlockSpec((1, d), lambda i: (0, 0)),
                  pl.BlockSpec((d, n), lambda i: (0, 0)),
                  pl.BlockSpec((d, ns), lambda i: (0, 0)),
                  pl.BlockSpec((nst, d), lambda i: (0, 0))] + [_HBM] * ng,
        out_specs=[pl.BlockSpec((n_col_blocks, tm, LANES), lambda i: (0, i, 0)),
                   pl.BlockSpec((tm, n_gate_cols), lambda i: (i, 0)),
                   pl.BlockSpec((tm, ns), lambda i: (i, 0)),
                   pl.BlockSpec((nst, tm), lambda i: (0, i)),
                   pl.BlockSpec((d, tm), lambda i: (0, i)),
                   pl.BlockSpec((tm, 1), lambda i: (i, 0))] + [_HBM] * ng,
        out_shape=[jax.ShapeDtypeStruct((n_col_blocks, t_len, LANES), F32),
                   jax.ShapeDtypeStruct((t_len, n_gate_cols), F32),
                   jax.ShapeDtypeStruct((t_len, ns), F32),
                   jax.ShapeDtypeStruct((nst, t_len), F32),
                   jax.ShapeDtypeStruct((d, t_len), MXU_DTYPE),
                   jax.ShapeDtypeStruct((t_len, 1), F32)] + _direct_out_shapes(gather, (False,) * ng),
        scratch_shapes=_direct_semaphores(ng) if ng else [],
        compiler_params=_params("arbitrary"),
    )(x, norm_w, w_main, w_small, w_small_t, *gather)


def _running_sum_mm(x, tri):
    hi = x.astype(MXU_DTYPE)
    lo = (x - hi.astype(F32)).astype(MXU_DTYPE)
    return jnp.dot(hi, tri, preferred_element_type=F32) + jnp.dot(lo, tri, preferred_element_type=F32)


def _col_block(t_len, first):
    return pl.BlockSpec((1, t_len, LANES), lambda p: (first + p, 0, 0))


def _sb_iotas():
    row_i = lax.broadcasted_iota(jnp.int32, (SB_BQ, SB_BLOCK), 0)
    col_i = lax.broadcasted_iota(jnp.int32, (SB_BQ, SB_BLOCK), 1)
    sq_r = lax.broadcasted_iota(jnp.int32, (SB_BLOCK, SB_BLOCK), 0)
    sq_c = lax.broadcasted_iota(jnp.int32, (SB_BLOCK, SB_BLOCK), 1)
    return row_i, col_i, sq_r, sq_c


SB_DIAG_BLOCKS = SB_BQ // SB_BLOCK
SB_EXP_FLOOR = -110.0


def _sb_keys_descending(qi, tile, carry, z_bounds, n_heads, has_free):
    group = SB_DIAG_BLOCKS
    n_free = group * qi
    diag = list(range(group - 1, -1, -1))
    carry = tile([n_free + j for j in diag], [True] * group, carry, [j * SB_BLOCK for j in diag])

    def largest_exponent(c):
        worst = jnp.max(z_bounds[0] - c[1])
        for h in range(1, n_heads):
            worst = jnp.maximum(worst, jnp.max(z_bounds[h] - c[1 + h]))
        return worst

    always = group if has_free else 0

    def cond(state):
        return (state[0] < n_free) & ((state[1] > SB_EXP_FLOOR) | (state[0] < always))

    def body(state):
        first = n_free - 1 - state[0]
        c = tile([first - j for j in range(group)], [False] * group, state[2:])
        return (state[0] + group, largest_exponent(c), *c)

    out = lax.while_loop(cond, body, (jnp.int32(0), largest_exponent(carry), *carry))
    return out[2:], out[0]


def _sb_keys_ascending(qi, n_run, tile, carry, has_free):
    group = SB_DIAG_BLOCKS
    n_free = group * qi
    diag = list(range(group))
    kjs, los, masked = [n_free + j for j in diag], [j * SB_BLOCK for j in diag], [True] * group
    if has_free:
        early = lambda s: [n_free - n_run + group * s + j for j in range(group)]
        carry = lax.fori_loop(0, n_run // group - 1, lambda s, c: tile(early(s), [False] * group, c), carry)
        kjs, los, masked = [n_free - group + j for j in range(group)] + kjs, [0] * group + los, [False] * group + masked
    return tile(kjs, masked, carry, los)


def _sb_fwd_call(cols, t_len):
    nq = t_len // SB_BQ
    scale = float(SB_HEAD_DIM) ** -0.5
    n_pairs = 512 // LANES
    per_pair = LANES // SB_HEAD_DIM

    def body(q_blk, k_blk, v_blk, o_blk, st_ref, nrun_ref):
        q_ref, k_ref, v_ref, o_ref = q_blk.at[0], k_blk.at[0], v_blk.at[0], o_blk.at[0]
        lane = lax.broadcasted_iota(jnp.int32, (1, LANES), 1)
        row_i, col_i, sq_r, sq_c = _sb_iotas()
        ge = (sq_r >= sq_c).astype(MXU_DTYPE)
        hms = [((lane // SB_HEAD_DIM) == hh).astype(F32) for hh in range(per_pair)]
        k_sq = k_ref[...] * k_ref[...]
        k_norms = [jnp.sqrt(jnp.max(jnp.sum(k_sq * hm, axis=-1, keepdims=True))) * (1.02 * scale) for hm in hms]

        def q_block(qi, has_free):
            r0 = qi * SB_BQ if isinstance(qi, int) else pl.multiple_of(qi * SB_BQ, SB_BQ)
            rows = pl.ds(r0, SB_BQ)
            q_all = q_ref[rows, :]
            qms = [(q_all * (hm * scale)).astype(MXU_DTYPE) for hm in hms]
            z_bounds = [jnp.sqrt(jnp.sum(q_all * q_all * hm, axis=-1, keepdims=True)) * kn
                        for hm, kn in zip(hms, k_norms)]

            def tile(kjs, masked, kc, los=None):
                heads = range(per_pair)
                los = los or [0] * len(kjs)
                pairs = [(t, h) for t in range(len(kjs)) for h in heads]
                add_rows = lambda full, lo, part: full + part if lo == 0 else jnp.concatenate(
                    [full[:lo], full[lo:] + part], axis=0)
                acc, cs = kc[0], list(kc[1:])
                s0s = [kj * SB_BLOCK if isinstance(kj, int) else pl.multiple_of(kj * SB_BLOCK, SB_BLOCK) for kj in kjs]
                kbs = [k_ref[pl.ds(s0, SB_BLOCK), :].astype(MXU_DTYPE) for s0 in s0s]
                v_alls = [v_ref[pl.ds(s0, SB_BLOCK), :] for s0 in s0s]
                vms = {(t, h): (v_alls[t] * hms[h]).astype(MXU_DTYPE) for t, h in pairs}
                zs = {(t, h): lax.dot_general(qms[h][los[t]:], kbs[t], _NT, preferred_element_type=F32)
                      for t, h in pairs}
                masks = [(col_i[lo:] + s0) < (row_i[lo:] + r0) if m else None for m, lo, s0 in zip(masked, los, s0s)]
                keep = lambda t, a: a if masks[t] is None else jnp.where(masks[t], a, 0.0)
                sps = {(t, h): keep(t, _softplus(zs[t, h])) for t, h in pairs}
                sums = {p: _running_sum_mm(sps[p], ge) for p in pairs}
                mass = {}
                for t, h in pairs:
                    mass[t, h] = cs[h] if t == 0 else add_rows(
                        mass[t - 1, h], los[t - 1], jnp.sum(sps[t - 1, h], axis=-1, keepdims=True))
                ws = {(t, h): keep(t, jnp.exp(zs[t, h] - (sums[t, h] + mass[t, h][los[t]:]))) for t, h in pairs}
                for t, h in pairs:
                    acc = add_rows(acc, los[t], jnp.dot(ws[t, h].astype(MXU_DTYPE), vms[t, h],
                                                        preferred_element_type=F32))
                last = len(kjs) - 1
                cs = [add_rows(mass[last, h], los[last], jnp.sum(sps[last, h], axis=-1, keepdims=True)) for h in heads]
                return (acc, *cs)

            zero_col = jnp.zeros((SB_BQ, 1), F32)
            out, n_run = _sb_keys_descending(
                qi, tile, (jnp.zeros((SB_BQ, LANES), F32),) + (zero_col,) * per_pair, z_bounds, per_pair, has_free)
            o_ref[rows, :] = out[0]
            masses = jnp.zeros((SB_BQ, LANES), F32)
            for hh in range(per_pair):
                masses = jnp.where(lane == hh, out[1 + hh], masses)
            st_ref[rows, :] = masses
            nrun_ref[pl.program_id(0), qi] = n_run

        q_block(0, False)
        lax.fori_loop(1, nq, lambda qi, carry: (q_block(qi, True), carry)[1], 0)

    return pl.pallas_call(
        body, name="sb_fwd",
        grid=(n_pairs,),
        in_specs=[_col_block(t_len, SB_FIRST_BLOCK), _col_block(t_len, SB_FIRST_BLOCK + n_pairs),
                  _col_block(t_len, SB_FIRST_BLOCK + 2 * n_pairs)],
        out_specs=[_col_block(t_len, 0),
                   pl.BlockSpec((t_len, LANES), lambda p: (0, p)),
                   pl.BlockSpec(memory_space=pltpu.SMEM)],
        out_shape=[jax.ShapeDtypeStruct((n_pairs, t_len, LANES), F32),
                   jax.ShapeDtypeStruct((t_len, n_pairs * LANES), F32),
                   jax.ShapeDtypeStruct((n_pairs, nq), jnp.int32)],
        compiler_params=_params("arbitrary"),
    )(cols, cols, cols)


def _sb_bwd_call(cols, sp_total, n_run_all, d_o, dproj, t_len):
    nq = t_len // SB_BQ
    scale = float(SB_HEAD_DIM) ** -0.5
    n_pairs = 512 // LANES
    per_pair = LANES // SB_HEAD_DIM

    def body(q_blk, k_blk, v_blk, st_ref, nrun_ref, do_blk, dproj_in_ref, d_ref):
        q_ref, k_ref, v_ref, do_ref = q_blk.at[0], k_blk.at[0], v_blk.at[0], do_blk.at[0]
        lane = lax.broadcasted_iota(jnp.int32, (1, LANES), 1)
        row_i, col_i, sq_r, sq_c = _sb_iotas()
        lt = (sq_r < sq_c).astype(MXU_DTYPE)
        le = (sq_r <= sq_c).astype(MXU_DTYPE)
        hms = [((lane // SB_HEAD_DIM) == hh).astype(F32) for hh in range(per_pair)]
        d_ref[1] = jnp.zeros((t_len, LANES), F32)
        d_ref[2] = jnp.zeros((t_len, LANES), F32)

        def q_block(qi, has_free):
            r0 = qi * SB_BQ if isinstance(qi, int) else pl.multiple_of(qi * SB_BQ, SB_BQ)
            rows = pl.ds(r0, SB_BQ)
            q_all, do_all = q_ref[rows, :], do_ref[rows, :]
            qms = [(q_all * (hm * scale)).astype(MXU_DTYPE) for hm in hms]
            doms = [(do_all * hm).astype(MXU_DTYPE) for hm in hms]
            masses = st_ref[rows, :]
            totals = [jnp.sum(jnp.where(lane == hh, masses, 0.0), axis=-1, keepdims=True) for hh in range(per_pair)]

            def tile(kjs, masked, kc, los=None):
                heads = range(per_pair)
                los = los or [0] * len(kjs)
                pairs = [(t, h) for t in range(len(kjs)) for h in heads]
                add_rows = lambda full, lo, part: full + part if lo == 0 else jnp.concatenate(
                    [full[:lo], full[lo:] + part], axis=0)
                rsum = lambda a: jnp.sum(a, axis=-1, keepdims=True)
                dq, cls, gls = kc[0], list(kc[1:1 + per_pair]), list(kc[1 + per_pair:])
                s0s = [kj * SB_BLOCK if isinstance(kj, int) else pl.multiple_of(kj * SB_BLOCK, SB_BLOCK) for kj in kjs]
                k_alls = [k_ref[pl.ds(s0, SB_BLOCK), :] for s0 in s0s]
                v_alls = [v_ref[pl.ds(s0, SB_BLOCK), :] for s0 in s0s]
                kbs = [k_all.astype(MXU_DTYPE) for k_all in k_alls]
                vms = {(t, h): (v_alls[t] * hms[h]).astype(MXU_DTYPE) for t, h in pairs}
                kms = {(t, h): (k_alls[t] * (hms[h] * scale)).astype(MXU_DTYPE) for t, h in pairs}
                q_live = {(t, h): qms[h][los[t]:] for t, h in pairs}
                do_live = {(t, h): doms[h][los[t]:] for t, h in pairs}
                zs = {p: lax.dot_general(q_live[p], kbs[p[0]], _NT, preferred_element_type=F32) for p in pairs}
                das = {p: lax.dot_general(do_live[p], vms[p], _NT, preferred_element_type=F32) for p in pairs}
                masks = [(col_i[lo:] + s0) < (row_i[lo:] + r0) if m else None for m, lo, s0 in zip(masked, los, s0s)]
                keep = lambda t, a: a if masks[t] is None else jnp.where(masks[t], a, 0.0)
                sp_alls = {p: _softplus(zs[p]) for p in pairs}
                sps = {(t, h): keep(t, sp_alls[t, h]) for t, h in pairs}
                lefts = {p: _running_sum_mm(sps[p], lt) for p in pairs}
                cl = {}
                for t, h in pairs:
                    cl[t, h] = cls[h] if t == 0 else add_rows(cl[t - 1, h], los[t - 1], rsum(sps[t - 1, h]))
                ws = {(t, h): keep(t, jnp.exp(zs[t, h] - ((totals[h] - cl[t, h])[los[t]:] - lefts[t, h])))
                      for t, h in pairs}
                gs = {p: das[p] * ws[p] for p in pairs}
                g_sums = {p: _running_sum_mm(gs[p], le) for p in pairs}
                gl = {}
                for t, h in pairs:
                    gl[t, h] = gls[h] if t == 0 else add_rows(gl[t - 1, h], los[t - 1], rsum(gs[t - 1, h]))
                dzs = {(t, h): keep(t, gs[t, h] - jnp.exp(zs[t, h] - sp_alls[t, h]) * (gl[t, h][los[t]:] + g_sums[t, h])
                               ).astype(MXU_DTYPE) for t, h in pairs}
                for t in range(len(kjs)):
                    dk_t = jnp.zeros((SB_BLOCK, LANES), F32)
                    dv_t = jnp.zeros((SB_BLOCK, LANES), F32)
                    for h in heads:
                        dq = add_rows(dq, los[t], jnp.dot(dzs[t, h], kms[t, h], preferred_element_type=F32))
                        dk_t = dk_t + lax.dot_general(dzs[t, h], q_live[t, h], _TN, preferred_element_type=F32)
                        dv_t = dv_t + lax.dot_general(ws[t, h].astype(MXU_DTYPE), do_live[t, h], _TN,
                                                      preferred_element_type=F32)
                    d_ref[1, pl.ds(s0s[t], SB_BLOCK), :] += dk_t
                    d_ref[2, pl.ds(s0s[t], SB_BLOCK), :] += dv_t
                last = len(kjs) - 1
                cls = [add_rows(cl[last, h], los[last], rsum(sps[last, h])) for h in heads]
                gls = [add_rows(gl[last, h], los[last], rsum(gs[last, h])) for h in heads]
                return (dq, *cls, *gls)

            zero_col = jnp.zeros((SB_BQ, 1), F32)
            out = _sb_keys_ascending(qi, nrun_ref[pl.program_id(0), qi], tile,
                                     (jnp.zeros((SB_BQ, LANES), F32),) + (zero_col,) * (2 * per_pair), has_free)
            d_ref[0, rows, :] = out[0]

        q_block(0, False)
        lax.fori_loop(1, nq, lambda qi, carry: (q_block(qi, True), carry)[1], 0)

    return pl.pallas_call(
        body, name="sb_bwd",
        grid=(n_pairs,),
        in_specs=[_col_block(t_len, SB_FIRST_BLOCK), _col_block(t_len, SB_FIRST_BLOCK + n_pairs),
                  _col_block(t_len, SB_FIRST_BLOCK + 2 * n_pairs),
                  pl.BlockSpec((t_len, LANES), lambda p: (0, p)),
                  pl.BlockSpec(memory_space=pltpu.SMEM), _col_block(t_len, 0), _HBM],
        out_specs=pl.BlockSpec((3, t_len, LANES), lambda p: (DPROJ_SB_SLOT // 3, 0, p)),
        out_shape=jax.ShapeDtypeStruct(dproj.shape, dproj.dtype),
        input_output_aliases={6: 0},
        compiler_params=_params("arbitrary"),
    )(cols, cols, cols, sp_total, n_run_all, d_o, dproj)


def _conv_taps(xin, rows, t_len):
    taps = []
    for i in range(CONV_WIDTH):
        shift = CONV_WIDTH - 1 - i
        if shift == 0:
            taps.append(xin)
        else:
            taps.append(jnp.where(rows >= shift, pltpu.roll(xin, shift, axis=0), 0.0))
    return taps


def _gdn_prep_body_common(x_ref, w_ref, t_len):
    j = pl.program_id(0)
    xin = x_ref[...]
    rows = lax.broadcasted_iota(jnp.int32, (t_len, LANES), 0)
    taps = _conv_taps(xin, rows, t_len)
    pre = taps[0] * w_ref[0:1, :]
    for i in range(1, CONV_WIDTH):
        pre = pre + taps[i] * w_ref[i:i + 1, :]
    sg = _sigmoid(pre)
    act = pre * sg
    is_qk = j < 2 * GDN_HEADS
    nrm = jnp.where(is_qk, lax.rsqrt(jnp.sum(act * act, axis=-1, keepdims=True) + EPS), 1.0)
    sc = jnp.where(j < GDN_HEADS, float(GDN_HEAD_DIM) ** -0.5, 1.0)
    return j, rows, taps, pre, sg, act, is_qk, nrm, sc


def _gdn_prep_call(cols, conv_w, t_len):
    def body(x_blk, w_ref, out_ref):
        _, _, _, _, _, act, _, nrm, sc = _gdn_prep_body_common(x_blk.at[0], w_ref, t_len)
        out_ref[...] = act * nrm * sc

    return pl.pallas_call(
        body, name="gdn_prep",
        grid=(3 * GDN_HEADS,),
        in_specs=[_col_block(t_len, GDN_FIRST_BLOCK),
                  pl.BlockSpec((CONV_WIDTH, LANES), lambda j: (0, j))],
        out_specs=pl.BlockSpec((t_len, LANES), lambda j: (0, j)),
        out_shape=jax.ShapeDtypeStruct((t_len, 3 * 512), F32),
        compiler_params=_params("arbitrary"),
    )(cols, conv_w)


def _gdn_prep_bwd_call(cols, conv_w, d_act3, dproj, t_len):
    def body(x_blk, w_ref, d_ref, dproj_in_ref, dx_ref, dw_ref):
        _, rows, taps, pre, sg, act, is_qk, nrm, sc = _gdn_prep_body_common(x_blk.at[0], w_ref, t_len)
        d_out = d_ref[0]
        dn = d_out * sc
        d_norm = nrm * dn - act * (nrm * nrm * nrm) * jnp.sum(dn * act, axis=-1, keepdims=True)
        d_act = jnp.where(is_qk, d_norm, d_out)
        d_pre = d_act * sg * (1.0 + pre * (1.0 - sg))
        dx = d_pre * w_ref[CONV_WIDTH - 1:CONV_WIDTH, :]
        dw_ref[CONV_WIDTH - 1:CONV_WIDTH, :] = jnp.sum(d_pre * taps[CONV_WIDTH - 1], axis=0, keepdims=True)
        for i in range(CONV_WIDTH - 1):
            shift = CONV_WIDTH - 1 - i
            up = jnp.where(rows < t_len - shift, pltpu.roll(d_pre, t_len - shift, axis=0), 0.0)
            dx = dx + up * w_ref[i:i + 1, :]
            dw_ref[i:i + 1, :] = jnp.sum(d_pre * taps[i], axis=0, keepdims=True)
        dx_ref[0] = dx

    return pl.pallas_call(
        body, name="gdn_prep_bwd",
        grid=(3 * GDN_HEADS,),
        in_specs=[_col_block(t_len, GDN_FIRST_BLOCK),
                  pl.BlockSpec((CONV_WIDTH, LANES), lambda j: (0, j)),
                  pl.BlockSpec((1, t_len, LANES), lambda j: (j // GDN_HEADS, 0, j % GDN_HEADS)), _HBM],
        out_specs=[pl.BlockSpec((1, t_len, LANES), lambda j: (DPROJ_GDN_SLOT + j // GDN_HEADS, 0, j % GDN_HEADS)),
                   pl.BlockSpec((CONV_WIDTH, LANES), lambda j: (0, j))],
        out_shape=[jax.ShapeDtypeStruct(dproj.shape, dproj.dtype),
                   jax.ShapeDtypeStruct((CONV_WIDTH, 3 * 512), F32)],
        input_output_aliases={3: 0},
        compiler_params=_params("arbitrary"),
    )(cols, conv_w, d_act3, dproj)


def _chunk_cumsum_matrix():
    r = lax.broadcasted_iota(jnp.int32, (LANES, LANES), 0)
    c = lax.broadcasted_iota(jnp.int32, (LANES, LANES), 1)
    return ((r <= c) & ((r // CHUNK) == (c // CHUNK))).astype(F32)


def _gdn_gates_call(ps, pst, alog_l, dtb_l, alog_c, dtb_c, t_len):
    def body(ps_ref, pst_ref, al_ref, dl_ref, ac_ref, dc_ref, beta_ref, gcol_ref, grow_ref):
        upper = _chunk_cumsum_matrix()
        lower = upper.T
        psv = ps_ref[...]
        beta_ref[...] = _sigmoid(psv)
        g_l = -jnp.exp(al_ref[...]) * _softplus(psv + dl_ref[...])
        g_r = -jnp.exp(ac_ref[...]) * _softplus(pst_ref[...] + dc_ref[...])
        for w in range(t_len // LANES):
            sl = slice(w * LANES, (w + 1) * LANES)
            gcol_ref[sl, :] = _mx(lower, g_l[sl, :])
            grow_ref[:, sl] = _mx(g_r[:, sl], upper)

    vm = pl.BlockSpec(memory_space=pltpu.VMEM)
    return pl.pallas_call(
        body, name="gdn_gates",
        in_specs=[vm] * 6, out_specs=[vm] * 3,
        out_shape=[jax.ShapeDtypeStruct((t_len, LANES), F32),
                   jax.ShapeDtypeStruct((t_len, LANES), F32),
                   jax.ShapeDtypeStruct((8, t_len), F32)],
        compiler_params=pltpu.CompilerParams(vmem_limit_bytes=VMEM_LIMIT_BYTES),
    )(ps, pst, alog_l, dtb_l, alog_c, dtb_c)


def _gdn_gates_bwd_call(ps, alog_l, dtb_l, d_l, t_len):
    def body(ps_ref, al_ref, dl_ref, d_ref, dps_ref, gal_ref, gdt_ref):
        lane = lax.broadcasted_iota(jnp.int32, (1, LANES), 1)
        psv = ps_ref[...]
        dv = d_ref[...]
        beta = _sigmoid(psv)
        ea = jnp.exp(al_ref[...])
        arg = psv + dl_ref[...]
        g = -ea * _softplus(arg)
        d_a = dv * (-ea) * _sigmoid(arg)
        is_a = (lane >= GDN_HEADS) & (lane < 2 * GDN_HEADS)
        dps_ref[...] = jnp.where(lane < GDN_HEADS, dv * beta * (1.0 - beta), jnp.where(is_a, d_a, 0.0))
        gdt_ref[...] = jnp.where(is_a, jnp.sum(d_a, axis=0, keepdims=True), 0.0)
        gal_ref[...] = jnp.where(is_a, jnp.sum(dv * g, axis=0, keepdims=True), 0.0)

    vm = pl.BlockSpec(memory_space=pltpu.VMEM)
    return pl.pallas_call(
        body, name="gdn_gates_bwd",
        in_specs=[vm] * 4, out_specs=[vm] * 3,
        out_shape=[jax.ShapeDtypeStruct((t_len, LANES), F32),
                   jax.ShapeDtypeStruct((1, LANES), F32),
                   jax.ShapeDtypeStruct((1, LANES), F32)],
        compiler_params=pltpu.CompilerParams(vmem_limit_bytes=VMEM_LIMIT_BYTES),
    )(ps, alog_l, dtb_l, d_l)


def _bm(a, b):
    return _m3_general(a, b, _BNN)


def _bm_nt(a, b):
    return _m3_general(a, b, _BNT)


def _bm_tn(a, b):
    return _m3_general(a, b, _BTN)


def _heads_of(ref, rows):
    return jnp.stack([ref[rows, h * GDN_HEAD_DIM:(h + 1) * GDN_HEAD_DIM] for h in range(GDN_HEADS)])


def _chunk_terms(q_ref, k_ref, v_ref, b_ref, gc_ref, gr_ref, c, incl, strict, n=1, scores=True):
    r0 = c * CHUNK if isinstance(c, int) else pl.multiple_of(c * CHUNK, CHUNK)
    rows = pl.ds(r0, n * CHUNK)
    per_chunk = lambda x: x.reshape(GDN_HEADS * n, CHUNK, x.shape[-1])
    q, k, v = (per_chunk(_heads_of(ref, rows)) for ref in (q_ref, k_ref, v_ref))
    lane_ids = lax.broadcasted_iota(jnp.int32, (1, LANES), 1)
    pick = lambda slab, first: jnp.stack([jnp.sum(jnp.where(lane_ids == first + h, slab, 0.0), axis=-1, keepdims=True)
                                          for h in range(GDN_HEADS)])
    b = per_chunk(pick(b_ref[rows, :], 0))
    gc = per_chunk(pick(gc_ref[rows, :], GDN_HEADS))
    gr = gr_ref[:, c] if n == 1 else gr_ref[:, c:c + n].reshape(GDN_HEADS * n, 1, CHUNK)
    dm = jnp.where(incl, jnp.exp(jnp.where(incl, gc - gr, 0.0)), 0.0)
    kb = k * b
    vb = v * b
    e = jnp.exp(gc)
    a = p = None
    if scores:
        kk_qk = _bm_nt(jnp.concatenate([kb, q], axis=1), k)
        a = jnp.where(strict, kk_qk[:, :CHUNK] * dm, 0.0)
        p = jnp.where(incl, kk_qk[:, CHUNK:] * dm, 0.0)
    gl = gc[:, CHUNK - 1:CHUNK, :]
    eg = jnp.exp(gl - gc)
    return rows, q, k, v, b, gc, dm, kb, vb, e, a, p, gl, eg


def _unit_lower_inverse(a, eye):
    x = -a
    tm = eye + x
    xp = _bm(x, x)
    for _ in range(4):
        both = _bm(jnp.concatenate([xp, tm], axis=1), xp)
        tm = tm + both[:, CHUNK:]
        xp = both[:, :CHUNK]
    return tm + _bm(tm, xp)


def _gdn_specs(t_len, n_chunks, reverse):
    cps = GDN_CHUNKS_PER_STEP
    steps = n_chunks // cps
    at = (lambda g: steps - 1 - g) if reverse else (lambda g: g)
    rows_blk = lambda width, part=0: pl.BlockSpec((cps * CHUNK, width), lambda g: (at(g), part))
    gate_r = pl.BlockSpec((GDN_HEADS, cps, 1, CHUNK), lambda g: (0, at(g), 0, 0))
    per_chunk = lambda r, c: pl.BlockSpec((GDN_HEADS, cps, r, c), lambda g: (0, at(g), 0, 0))
    return cps, steps, rows_blk, gate_r, per_chunk


def _gdn_fwd_call(gact, beta_c, gam_c, gam_r, t_len):
    n_chunks = t_len // CHUNK
    dk = GDN_HEAD_DIM
    width = GDN_HEADS * dk
    cps, steps, rows_blk, gate_r, per_chunk = _gdn_specs(t_len, n_chunks, False)

    def body(q_ref, k_ref, v_ref, b_ref, gc_ref, gr_ref, o_ref, s_ref, t_ref, a_ref, p_ref, uw_ref, vn_ref, state_ref):
        row = lax.broadcasted_iota(jnp.int32, (CHUNK, CHUNK), 0)
        col = lax.broadcasted_iota(jnp.int32, (CHUNK, CHUNK), 1)
        incl, strict = row >= col, row > col
        eye = (row == col).astype(F32)

        @pl.when(pl.program_id(0) == 0)
        def _():
            state_ref[...] = jnp.zeros_like(state_ref)

        _, q, k, v, b, gc, dm, kb, vb, e, a, p, gl, eg = _chunk_terms(
            q_ref, k_ref, v_ref, b_ref, gc_ref, gr_ref, 0, incl, strict, cps)
        tm = _unit_lower_inverse(a, eye)
        uw = _bm(tm, jnp.concatenate([vb, kb * e], axis=2))
        w_qe = jnp.concatenate([uw[:, :, dk:], q * e], axis=1)
        u, kd, decay = uw[:, :, :dk], k * eg, jnp.exp(gl)
        per_chunk_block = lambda x: x.reshape(GDN_HEADS, cps, CHUNK, CHUNK)
        t_ref[...], a_ref[...], p_ref[...] = per_chunk_block(tm), per_chunk_block(a), per_chunk_block(p)
        uw_heads = uw.reshape(GDN_HEADS, cps * CHUNK, 2 * dk)
        for h in range(GDN_HEADS):
            uw_ref[:, h * 2 * dk:(h + 1) * 2 * dk] = uw_heads[h]

        of_chunk = lambda x, c: jnp.stack([x[h * cps + c] for h in range(GDN_HEADS)])
        s = state_ref[...]
        for c in range(cps):
            ws_qs = _bm(of_chunk(w_qe, c), s)
            vn = of_chunk(u, c) - ws_qs[:, :CHUNK]
            o = ws_qs[:, CHUNK:] + _bm(of_chunk(p, c), vn)
            for h in range(GDN_HEADS):
                o_ref[c * CHUNK:(c + 1) * CHUNK, h * dk:(h + 1) * dk] = o[h]
                vn_ref[c * CHUNK:(c + 1) * CHUNK, h * dk:(h + 1) * dk] = vn[h]
            s_ref[:, c] = s
            s = s * of_chunk(decay, c) + _bm_tn(of_chunk(kd, c), vn)
        state_ref[...] = s

    scores = jax.ShapeDtypeStruct((GDN_HEADS, n_chunks, CHUNK, CHUNK), F32)
    return pl.pallas_call(
        body, name="gdn_fwd",
        grid=(steps,),
        in_specs=[rows_blk(width, 0), rows_blk(width, 1), rows_blk(width, 2), rows_blk(LANES), rows_blk(LANES), gate_r],
        out_specs=[rows_blk(width), per_chunk(dk, dk), per_chunk(CHUNK, CHUNK), per_chunk(CHUNK, CHUNK),
                   per_chunk(CHUNK, CHUNK), rows_blk(2 * width), rows_blk(width)],
        out_shape=[jax.ShapeDtypeStruct((t_len, width), F32),
                   jax.ShapeDtypeStruct((GDN_HEADS, n_chunks, dk, dk), F32), scores, scores, scores,
                   jax.ShapeDtypeStruct((t_len, 2 * width), F32), jax.ShapeDtypeStruct((t_len, width), F32)],
        scratch_shapes=[pltpu.VMEM((GDN_HEADS, dk, dk), F32)],
        compiler_params=_params("arbitrary"),
    )(gact, gact, gact, beta_c, gam_c, gam_r)


def _gdn_bwd_call(gact, beta_c, gam_c, gam_r, saved, d_o, t_len, scatter=()):
    n_chunks = t_len // CHUNK
    dk = GDN_HEAD_DIM
    width = GDN_HEADS * dk
    cps, steps, rows_blk, gate_r, per_chunk = _gdn_specs(t_len, n_chunks, True)
    nx = len(scatter)
    n_in = 13

    def body(*refs):
        q_ref, k_ref, v_ref, b_ref, gc_ref, gr_ref = refs[:6]
        saved_refs, do_ref = refs[6:12], refs[12]
        d_ref, dgate_ref = refs[n_in + nx:n_in + 2 + nx]
        dstate_ref = refs[n_in + 2 + 2 * nx]
        copies = lambda: _direct_copies(refs[n_in:n_in + nx], refs[n_in + 2 + nx:n_in + 2 + 2 * nx],
                                        *refs[n_in + 3 + 2 * nx:], (True,) * nx)
        if nx:
            pl.when(pl.program_id(0) == 0)(lambda: _start_all(copies()))
        row = lax.broadcasted_iota(jnp.int32, (CHUNK, CHUNK), 0)
        col = lax.broadcasted_iota(jnp.int32, (CHUNK, CHUNK), 1)
        incl, strict = row >= col, row > col
        ng = GDN_BWD_GROUP
        nb = GDN_HEADS * ng
        upper = jnp.broadcast_to((row <= col).astype(F32), (nb, CHUNK, CHUNK))
        ones = jnp.ones((nb, CHUNK, LANES), F32)
        last_row = lax.broadcasted_iota(jnp.int32, (CHUNK, 1), 0) == CHUNK - 1
        lane_ids = lax.broadcasted_iota(jnp.int32, (1, LANES), 1)
        rsum = lambda m: jnp.sum(m, axis=-1, keepdims=True)
        total = lambda m: jnp.sum(rsum(m), axis=1, keepdims=True)
        of_chunk = lambda x, c: jnp.stack([x[h * ng + c] for h in range(GDN_HEADS)])

        @pl.when(pl.program_id(0) == 0)
        def _():
            dstate_ref[...] = jnp.zeros_like(dstate_ref)

        for c0 in range(cps - ng, -1, -ng):
            group(c0, q_ref, k_ref, v_ref, b_ref, gc_ref, gr_ref, saved_refs, do_ref, d_ref, dgate_ref, dstate_ref,
                  incl, strict, upper, ones, last_row, lane_ids, rsum, total, of_chunk)
        if nx:
            pl.when(pl.program_id(0) == steps - 1)(lambda: _wait_all(copies()))

    def group(c0, q_ref, k_ref, v_ref, b_ref, gc_ref, gr_ref, saved_refs, do_ref, d_ref, dgate_ref, dstate_ref,
              incl, strict, upper, ones, last_row, lane_ids, rsum, total, of_chunk):
        ng = GDN_BWD_GROUP
        nb = GDN_HEADS * ng
        rows = pl.ds(c0 * CHUNK, ng * CHUNK)
        s_ref, t_ref, a_ref, p_ref, uw_ref, vn_ref = saved_refs
        _, q, k, v, b, gc, dm, kb, vb, e, _, _, gl, eg = _chunk_terms(
            q_ref, k_ref, v_ref, b_ref, gc_ref, gr_ref, c0, incl, strict, ng, scores=False)
        s = s_ref[:, c0:c0 + ng].reshape(nb, dk, dk)
        tm = t_ref[:, c0:c0 + ng].reshape(nb, CHUNK, CHUNK)
        a = a_ref[:, c0:c0 + ng].reshape(nb, CHUNK, CHUNK)
        p = p_ref[:, c0:c0 + ng].reshape(nb, CHUNK, CHUNK)
        d_out = _heads_of(do_ref, rows).reshape(nb, CHUNK, dk)
        vn = _heads_of(vn_ref, rows).reshape(nb, CHUNK, dk)
        uw = jnp.stack([uw_ref[rows, h * 2 * dk:(h + 1) * 2 * dk] for h in range(GDN_HEADS)]).reshape(nb, CHUNK, 2 * dk)
        u, w = uw[:, :, :dk], uw[:, :, dk:]
        el = jnp.exp(gl)
        kbe = kb * e
        qe = q * e
        kd = k * eg
        pt_do = _bm_tn(p, d_out)
        qet_do = _bm_tn(qe, d_out)

        ds = dstate_ref[...]
        d_vn_c, ds_c = [None] * ng, [None] * ng
        for c in range(ng - 1, -1, -1):
            ds_c[c] = ds
            d_vn_c[c] = of_chunk(pt_do, c) + _bm(of_chunk(kd, c), ds)
            ds = of_chunk(el, c) * ds + of_chunk(qet_do, c) - _bm_tn(of_chunk(w, c), d_vn_c[c])
        dstate_ref[...] = ds
        by_chunk = lambda xs: jnp.stack([xs[c][h] for h in range(GDN_HEADS) for c in range(ng)])
        d_vn, ds = by_chunk(d_vn_c), by_chunk(ds_c)

        on_s = _bm_nt(jnp.concatenate([d_out, d_vn], axis=1), s)
        d_qe, d_w = on_s[:, :CHUNK], -on_s[:, CHUNK:]
        d_p = jnp.where(incl, _bm_nt(d_out, vn), 0.0)
        d_kd = _bm_nt(vn, ds)
        d_both = _bm_tn(tm, jnp.concatenate([d_vn, d_w], axis=2))
        d_vb, d_kbe = d_both[:, :, :dk], d_both[:, :, dk:]
        d_a = -jnp.where(strict, _bm_nt(d_both, uw), 0.0)
        m = d_a * dm
        n = d_p * dm
        on_k = _bm(jnp.concatenate([m, n], axis=1), k)
        d_kb = on_k[:, :CHUNK] + d_kbe * e
        d_q = on_k[:, CHUNK:] + d_qe * e
        d_k = (_bm_tn(jnp.concatenate([m, n], axis=1), jnp.concatenate([kb, q], axis=1))
               + d_kd * eg + b * d_kb)
        d_v = b * d_vb
        r = d_a * a + d_p * p
        kd_term = rsum(d_kd * kd)
        d_gl = total(ds * s) * el + jnp.sum(kd_term, axis=1, keepdims=True)
        d_gam = (rsum(r) - _times_exact(r, ones, _BTN)[:, :, 0:1] + rsum(d_qe * qe) + rsum(d_kbe * kbe) - kd_term
                 + jnp.where(last_row, d_gl, 0.0))
        d_beta = rsum(d_kb * k) + rsum(d_vb * v)
        d_g = _exact_times(upper, d_gam * ones, _BNN)[:, :, 0:1]
        per_head = lambda x: x.reshape(GDN_HEADS, ng * CHUNK, x.shape[-1])
        d_q, d_k, d_v, d_beta, d_g = (per_head(x) for x in (d_q, d_k, d_v, d_beta, d_g))
        gates = jnp.zeros((ng * CHUNK, LANES), F32)
        for h in range(GDN_HEADS):
            lanes = slice(h * dk, (h + 1) * dk)
            d_ref[0, rows, lanes] = d_q[h]
            d_ref[1, rows, lanes] = d_k[h]
            d_ref[2, rows, lanes] = d_v[h]
            gates = gates + (jnp.where(lane_ids == h, d_beta[h], 0.0)
                             + jnp.where(lane_ids == GDN_HEADS + h, d_g[h], 0.0))
        dgate_ref[rows, :] = gates

    d_spec = pl.BlockSpec((3, cps * CHUNK, width), lambda g: (0, steps - 1 - g, 0))
    return pl.pallas_call(
        body, name="gdn_bwd",
        grid=(steps,),
        in_specs=[rows_blk(width, 0), rows_blk(width, 1), rows_blk(width, 2), rows_blk(LANES), rows_blk(LANES), gate_r,
                  per_chunk(dk, dk), per_chunk(CHUNK, CHUNK), per_chunk(CHUNK, CHUNK), per_chunk(CHUNK, CHUNK),
                  rows_blk(2 * width), rows_blk(width), rows_blk(width)] + [_HBM] * nx,
        out_specs=[d_spec, rows_blk(LANES)] + [_HBM] * nx,
        out_shape=[jax.ShapeDtypeStruct((3, t_len, width), F32),
                   jax.ShapeDtypeStruct((t_len, LANES), F32)] + _direct_out_shapes(scatter, (True,) * nx),
        scratch_shapes=[pltpu.VMEM((GDN_HEADS, dk, dk), F32)] + (_direct_semaphores(nx) if nx else []),
        compiler_params=_params("arbitrary"),
    )(gact, gact, gact, beta_c, gam_c, gam_r, *saved, d_o, *scatter)


def _group_sums(x, group):
    rows, width = x.shape
    lane = lax.broadcasted_iota(jnp.int32, (1, LANES), 1)
    out = []
    for t in range(width // LANES):
        seg = x[:, t * LANES:(t + 1) * LANES]
        if group == LANES:
            out.append(jnp.broadcast_to(jnp.sum(seg, axis=-1, keepdims=True), (rows, LANES)))
        else:
            low = jnp.sum(jnp.where(lane < group, seg, 0.0), axis=-1, keepdims=True)
            high = jnp.sum(jnp.where(lane < group, 0.0, seg), axis=-1, keepdims=True)
            out.append(jnp.where(lane < group, low, high))
    return jnp.concatenate(out, axis=1)


def _post_call(o_sb, o_gd, proj_gates, x, target, w_out, sbw, gdw, fw, tm=256):
    t_len, d = x.shape
    half = 512
    sb_blocks = half // LANES

    def body(osb_ref, ogd_ref, zsb_ref, zgd_ref, x_ref, tg_ref, wo_ref, sbw_ref, gdw_ref, fw_ref,
             dx2_ref, dosb_ref, dogd_ref, dz_ref, loss_ref, gfw_ref, gsb_ref, ggd_ref, gwo_ref):
        step = pl.program_id(0)

        @pl.when(step == 0)
        def _():
            loss_ref[...] = jnp.zeros_like(loss_ref)
            gfw_ref[...] = jnp.zeros_like(gfw_ref)
            gsb_ref[...] = jnp.zeros_like(gsb_ref)
            ggd_ref[...] = jnp.zeros_like(ggd_ref)
            gwo_ref[...] = jnp.zeros_like(gwo_ref)

        def head_forward(o, z, w, head_dim):
            r = lax.rsqrt(_group_sums(o * o, head_dim) * (1.0 / head_dim) + EPS)
            nrm = o * r * w
            sg = _sigmoid(z)
            return r, nrm, sg, nrm * (z * sg)

        def head_backward(d_m, o, z, w, head_dim, r, nrm, sg):
            d_n = d_m * (z * sg)
            d_z = d_m * nrm * (sg * (1.0 + z * (1.0 - sg)))
            dnw = d_n * w
            d_o = r * dnw - o * (r * r * r) * (_group_sums(dnw * o, head_dim) * (1.0 / head_dim))
            return d_o, d_z, jnp.sum(d_n * o * r, axis=0, keepdims=True)

        osb = jnp.concatenate([osb_ref[j] for j in range(sb_blocks)], axis=1)
        ogd, zsb, zgd = ogd_ref[...], zsb_ref[...], zgd_ref[...]
        sbw_v, gdw_v = sbw_ref[...], gdw_ref[...]
        r_sb, n_sb, sg_sb, m_sb = head_forward(osb, zsb, sbw_v, SB_HEAD_DIM)
        r_gd, n_gd, sg_gd, m_gd = head_forward(ogd, zgd, gdw_v, GDN_HEAD_DIM)
        mixed = jnp.concatenate([m_sb, m_gd], axis=1).astype(MXU_DTYPE)
        wo = wo_ref[...]
        x2 = x_ref[...] + jnp.dot(mixed, wo, preferred_element_type=F32)
        r2 = lax.rsqrt(jnp.mean(x2 * x2, axis=-1, keepdims=True) + EPS)
        fw_v = fw_ref[...]
        err = x2 * r2 * fw_v - tg_ref[...]
        loss_ref[...] += 0.5 * jnp.sum(jnp.sum(err * err, axis=-1, keepdims=True) * (1.0 / d))
        dy = err * (1.0 / d)
        gg = dy * fw_v
        dx2 = r2 * gg - x2 * ((r2 * r2 * r2) * jnp.mean(gg * x2, axis=-1, keepdims=True))
        gfw_ref[...] += jnp.sum(dy * x2 * r2, axis=0, keepdims=True)
        dx2_ref[...] = dx2
        dx2b = dx2.astype(MXU_DTYPE)
        d_mixed = lax.dot_general(dx2b, wo, _NT, preferred_element_type=F32)
        gwo_ref[...] += lax.dot_general(mixed, dx2b, _TN, preferred_element_type=F32)
        d_osb, d_zsb, gsb = head_backward(d_mixed[:, :half], osb, zsb, sbw_v, SB_HEAD_DIM, r_sb, n_sb, sg_sb)
        d_ogd, d_zgd, ggd = head_backward(d_mixed[:, half:], ogd, zgd, gdw_v, GDN_HEAD_DIM, r_gd, n_gd, sg_gd)
        for j in range(sb_blocks):
            dosb_ref[j] = d_osb[:, j * LANES:(j + 1) * LANES]
        dogd_ref[...] = d_ogd
        dz_ref[0] = d_zsb
        dz_ref[1] = d_zgd
        gsb_ref[...] += gsb
        ggd_ref[...] += ggd

    row_blk = lambda w: pl.BlockSpec((tm, w), lambda i: (i, 0))
    blocks_blk = pl.BlockSpec((sb_blocks, tm, LANES), lambda i: (0, i, 0))
    fixed = lambda r, w: pl.BlockSpec((r, w), lambda i: (0, 0))
    return pl.pallas_call(
        body, name="post",
        grid=(t_len // tm,),
        in_specs=[blocks_blk, row_blk(half),
                  pl.BlockSpec((tm, half), lambda i: (i, 0)),
                  pl.BlockSpec((tm, half), lambda i: (i, 1)),
                  row_blk(d), row_blk(d), fixed(d, d), fixed(1, half), fixed(1, half), fixed(1, d)],
        out_specs=[row_blk(d), blocks_blk, row_blk(half),
                   pl.BlockSpec((2, tm, half), lambda i: (DPROJ_GATE_SLOT // 2, i, 0)),
                   fixed(1, LANES), fixed(1, d), fixed(1, half), fixed(1, half), fixed(d, d)],
        out_shape=[jax.ShapeDtypeStruct((t_len, d), F32), jax.ShapeDtypeStruct((sb_blocks, t_len, LANES), F32),
                   jax.ShapeDtypeStruct((t_len, half), F32),
                   jax.ShapeDtypeStruct((len(DPROJ_PIECE_OF_SLOT), t_len, half), F32),
                     jax.ShapeDtypeStruct((1, LANES), F32), jax.ShapeDtypeStruct((1, d), F32),
                     jax.ShapeDtypeStruct((1, half), F32), jax.ShapeDtypeStruct((1, half), F32),
                     jax.ShapeDtypeStruct((d, d), F32)],
        compiler_params=_params("arbitrary"),
    )(o_sb, o_gd, proj_gates, proj_gates, x, target, w_out, sbw, gdw, fw)


def _piece_of_slot(s):
    return jnp.where(s < DPROJ_GDN_SLOT, s, jnp.where(s < DPROJ_GATE_SLOT, s + 1,
                                                     jnp.where(s == DPROJ_GATE_SLOT, 3, 7)))


def _gw_in_call(h_t, dproj8):
    d, t_len = h_t.shape
    n_piece, _, pw = dproj8.shape

    def body(ht_ref, dp_ref, gw_ref):
        gw_ref[...] = jnp.dot(ht_ref[...], dp_ref[0].astype(MXU_DTYPE), preferred_element_type=F32)

    return pl.pallas_call(
        body, name="gw_in",
        grid=(n_piece,),
        in_specs=[pl.BlockSpec((d, t_len), lambda s: (0, 0)),
                  pl.BlockSpec((1, t_len, pw), lambda s: (s, 0, 0))],
        out_specs=pl.BlockSpec((d, pw), lambda s: (0, _piece_of_slot(s))),
        out_shape=jax.ShapeDtypeStruct((d, n_piece * pw), F32),
        compiler_params=_params("arbitrary"),
    )(h_t, dproj8)


def _slot_of_piece(p):
    return jnp.where(p < DPROJ_GDN_SLOT, p, jnp.where(p == 3, DPROJ_GATE_SLOT, jnp.where(p < 7, p - 1, 7)))


def _gw_in_shards_call(h_t, dproj8, dsmall, out_dtype):
    d, t_len = h_t.shape
    n_piece, _, pw = dproj8.shape
    ns = dsmall.shape[1]
    n_pairs = N_DEV // 2

    def body(ht_ref, dp_ref, ds_ref, chip_ref, prev_ref, gates_ref, send_ref, recv_ref, send_sems, recv_sems):
        p = pl.program_id(0)
        x_pos, y_pos, c = lax.axis_index("x"), lax.axis_index("y"), lax.axis_index("c")
        to_sibling = lambda pair: pltpu.make_async_remote_copy(
            src_ref=send_ref.at[pair], dst_ref=recv_ref.at[pair], send_sem=send_sems.at[pair],
            recv_sem=recv_sems.at[pair], device_id=(x_pos, y_pos, 1 - c), device_id_type=_MESH)

        @pl.when(p == 0)
        def _():
            gates_ref[...] = jnp.dot(ht_ref[...], ds_ref[...].astype(MXU_DTYPE), preferred_element_type=F32)

        def emit(s, tail):
            x = jnp.concatenate([prev_ref[...], tail], axis=1)
            y = x if s == 0 else pltpu.roll(x, SHARD_PAD - s, axis=1)
            shard = y[:, :SHARD_COLS].astype(out_dtype)

            @pl.when(c == s % 2)
            def _():
                chip_ref[s // 2] = shard

            @pl.when(c != s % 2)
            def _():
                send_ref[s // 2] = shard
                to_sibling(s // 2).start()

        @pl.when(p < n_piece)
        def _():
            cur = jnp.dot(ht_ref[...], dp_ref[0].astype(MXU_DTYPE), preferred_element_type=F32)
            for s in range(n_piece - 1):
                pl.when(p == s + 1)(functools.partial(emit, s, cur[:, :SHARD_PAD - pw]))
            prev_ref[...] = cur

        @pl.when(p == n_piece)
        def _():
            emit(n_piece - 1, gates_ref[...])
            for pair in range(n_pairs):
                to_sibling(pair).wait_send()
            for pair in range(n_pairs):
                to_sibling(pair).wait_recv()
                chip_ref[pair] = (chip_ref[pair].astype(F32) + recv_ref[pair].astype(F32)).astype(out_dtype)

    shards_of_side = lambda: pltpu.VMEM((n_pairs, d, SHARD_COLS), out_dtype)
    return pl.pallas_call(
        body, name="gw_in",
        grid=(n_piece + 1,),
        in_specs=[pl.BlockSpec((d, t_len), lambda p: (0, 0)),
                  pl.BlockSpec((1, t_len, pw), lambda p: (_slot_of_piece(jnp.minimum(p, n_piece - 1)), 0, 0)),
                  pl.BlockSpec((t_len, ns), lambda p: (0, 0))],
        out_specs=pl.BlockSpec((n_pairs, d, SHARD_COLS), lambda p: (0, 0, 0)),
        out_shape=jax.ShapeDtypeStruct((n_pairs, d, SHARD_COLS), out_dtype),
        scratch_shapes=[pltpu.VMEM((d, pw), F32), pltpu.VMEM((d, ns), F32), shards_of_side(), shards_of_side(),
                        pltpu.SemaphoreType.DMA((n_pairs,)), pltpu.SemaphoreType.DMA((n_pairs,))],
        compiler_params=_params("arbitrary"),
    )(h_t, dproj8, dsmall)


def _gw_small_call(h_t, dsmall, tm=512):
    d, t_len = h_t.shape
    ns = dsmall.shape[1]

    def body(ht_ref, dp_ref, gw_ref):
        @pl.when(pl.program_id(0) == 0)
        def _():
            gw_ref[...] = jnp.zeros_like(gw_ref)

        gw_ref[...] += jnp.dot(ht_ref[...], dp_ref[...].astype(MXU_DTYPE), preferred_element_type=F32)

    return pl.pallas_call(
        body, name="gw_small",
        grid=(t_len // tm,),
        in_specs=[pl.BlockSpec((d, tm), lambda t: (0, t)),
                  pl.BlockSpec((tm, ns), lambda t: (t, 0))],
        out_specs=pl.BlockSpec((d, ns), lambda t: (0, 0)),
        out_shape=jax.ShapeDtypeStruct((d, ns), F32),
        compiler_params=_params("arbitrary"),
    )(h_t, dsmall)


def _dx_call(dproj8, dsmall, w_main, w_small, x, r, dx2, norm_w, chip_scatter=(), tm=256):
    t_len, d = x.shape
    n_piece, _, pw = dproj8.shape
    ns = dsmall.shape[1]
    nx = len(chip_scatter)
    steps = t_len // tm

    def body(*refs):
        dp_ref, ds_ref, wm_ref, ws_ref, x_ref, r_ref, dx2_ref, nw_ref = refs[:8]
        gx_ref, gnw_ref = refs[8 + nx:10 + nx]
        copies = lambda: _chip_copies(refs[8:8 + nx], refs[10 + nx:10 + 2 * nx], *refs[10 + 2 * nx:])
        if nx:
            pl.when(pl.program_id(0) == 0)(lambda: _start_all(copies()))

        @pl.when(pl.program_id(0) == 0)
        def _():
            gnw_ref[...] = jnp.zeros_like(gnw_ref)

        dh = lax.dot_general(ds_ref[...].astype(MXU_DTYPE), ws_ref[...], _NT, preferred_element_type=F32)
        for s, p in enumerate(DPROJ_PIECE_OF_SLOT):
            dh = dh + lax.dot_general(dp_ref[s].astype(MXU_DTYPE), wm_ref[:, p * pw:(p + 1) * pw], _NT,
                                      preferred_element_type=F32)
        xv, rv = x_ref[...], r_ref[...]
        dn = dh * nw_ref[...]
        gx_ref[...] = dx2_ref[...] + rv * dn - xv * ((rv * rv * rv) * jnp.mean(dn * xv, axis=-1, keepdims=True))
        gnw_ref[...] += jnp.sum(dh * xv * rv, axis=0, keepdims=True)
        if nx:
            pl.when(pl.program_id(0) == steps - 1)(lambda: _wait_all(copies()))

    return pl.pallas_call(
        body, name="dx",
        grid=(steps,),
        in_specs=[pl.BlockSpec((n_piece, tm, pw), lambda i: (0, i, 0)),
                  pl.BlockSpec((tm, ns), lambda i: (i, 0)),
                  pl.BlockSpec((d, n_piece * pw), lambda i: (0, 0)),
                  pl.BlockSpec((d, ns), lambda i: (0, 0)),
                  pl.BlockSpec((tm, d), lambda i: (i, 0)),
                  pl.BlockSpec((tm, 1), lambda i: (i, 0)),
                  pl.BlockSpec((tm, d), lambda i: (i, 0)),
                  pl.BlockSpec((1, d), lambda i: (0, 0))] + [_HBM] * nx,
        out_specs=[pl.BlockSpec((tm, d), lambda i: (i, 0)),
                   pl.BlockSpec((1, d), lambda i: (0, 0))] + [_HBM] * nx,
        out_shape=[jax.ShapeDtypeStruct((t_len, d), F32), jax.ShapeDtypeStruct((1, d), F32)]
                  + [jax.ShapeDtypeStruct(a.shape, a.dtype) for a in chip_scatter],
        scratch_shapes=_chip_semaphores(nx) if nx else [],
        compiler_params=_params("arbitrary"),
    )(dproj8, dsmall, w_main, w_small, x, r, dx2, norm_w, *chip_scatter)


def _exchange_call(name, srcs, per_peer):
    n = len(srcs)

    def body(*refs):
        src_refs, out_refs = refs[:n], refs[n:2 * n]
        copies = _direct_copies(src_refs, out_refs, *refs[2 * n:], per_peer)
        _start_all(copies)
        _wait_all(copies)

    hbm = pl.BlockSpec(memory_space=pl.ANY)
    return pl.pallas_call(
        body, name=name,
        in_specs=[hbm] * n, out_specs=[hbm] * n, out_shape=_direct_out_shapes(srcs, per_peer),
        scratch_shapes=_direct_semaphores(n),
    )(*srcs)


def _direct_out_shapes(srcs, per_peer):
    return [jax.ShapeDtypeStruct(s.shape if pp else (N_DEV,) + s.shape, s.dtype) for s, pp in zip(srcs, per_peer)]


def _direct_semaphores(n):
    return [pltpu.SemaphoreType.DMA((n * (N_DEV - 1),)), pltpu.SemaphoreType.DMA((n * (N_DEV - 1),)),
            pltpu.SemaphoreType.DMA((n,))]


def _direct_copies(src_refs, out_refs, send_sems, recv_sems, local_sems, per_peer):
    x, y, c = lax.axis_index("x"), lax.axis_index("y"), lax.axis_index("c")
    me = 4 * x + 2 * y + c
    local, remote = [], []
    for a in range(len(src_refs)):
        mine = src_refs[a].at[me] if per_peer[a] else src_refs[a]
        local.append(pltpu.make_async_copy(mine, out_refs[a].at[me], local_sems.at[a]))
    for k in range(1, N_DEV):
        kx, ky, kc = (k >> 2) & 1, (k >> 1) & 1, k & 1
        px = 1 - x if kx else x
        py = 1 - y if ky else y
        pc = 1 - c if kc else c
        peer = 4 * px + 2 * py + pc
        for a in range(len(src_refs)):
            sem = a * (N_DEV - 1) + (k - 1)
            remote.append(pltpu.make_async_remote_copy(
                src_ref=src_refs[a].at[peer] if per_peer[a] else src_refs[a], dst_ref=out_refs[a].at[me],
                send_sem=send_sems.at[sem], recv_sem=recv_sems.at[sem],
                device_id=(px, py, pc), device_id_type=pl.DeviceIdType.MESH))
    return local, remote


def _start_all(copies):
    local, remote = copies
    for cp in local + remote:
        cp.start()


def _wait_all(copies):
    local, remote = copies
    for cp in remote:
        cp.wait_send()
    for cp in remote:
        cp.wait_recv()
    for cp in local:
        cp.wait()


N_CHIPS = 4
_HBM = pl.BlockSpec(memory_space=pl.ANY)
_MESH = pl.DeviceIdType.MESH


def _gather_call(name, srcs):
    n = len(srcs)
    per = N_DEV - 1

    def body(*refs):
        src_refs, out_refs = refs[:n], refs[n:2 * n]
        send_sems, recv_sems, local_sems = refs[2 * n:]
        x, y, c = lax.axis_index("x"), lax.axis_index("y"), lax.axis_index("c")
        me, sibling = (x, y, c), (x, y, 1 - c)
        x_nbr, y_nbr, diagonal = (1 - x, y), (x, 1 - y), (1 - x, 1 - y)
        held = ((1 - x) * c + x * (1 - c), y * c + (1 - y) * (1 - c))
        onward = (x * c + (1 - x) * (1 - c), (1 - y) * c + y * (1 - c))
        slot = lambda px, py, pc: 4 * px + 2 * py + pc

        def copy(a, k, block, to, from_src=False):
            rows = out_refs[a].at[slot(*block)]
            return pltpu.make_async_remote_copy(
                src_ref=src_refs[a] if from_src else rows, dst_ref=rows,
                send_sem=send_sems.at[a * per + k], recv_sem=recv_sems.at[a * per + k],
                device_id=to, device_id_type=_MESH)

        local = [pltpu.make_async_copy(src_refs[a], out_refs[a].at[slot(*me)], local_sems.at[a]) for a in range(n)]
        started = []

        def start(cp):
            cp.start()
            started.append(cp)

        for cp in local:
            cp.start()
        for a in range(n):
            start(copy(a, 0, me, sibling, True))
            start(copy(a, 1, me, (*x_nbr, c), True))
            start(copy(a, 2, me, (*y_nbr, c), True))
        for a in range(n):
            copy(a, 1, (*x_nbr, c), me).wait_recv()
            copy(a, 2, (*y_nbr, c), me).wait_recv()
            start(copy(a, 3, (*held, c), (*onward, c)))
            start(copy(a, 4, (*x_nbr, c), sibling))
            start(copy(a, 5, (*y_nbr, c), sibling))
        for a in range(n):
            copy(a, 3, (*diagonal, c), me).wait_recv()
            start(copy(a, 6, (*diagonal, c), sibling))
        for a in range(n):
            copy(a, 0, sibling, me).wait_recv()
            for k, chip in ((4, x_nbr), (5, y_nbr), (6, diagonal)):
                copy(a, k, (*chip, 1 - c), me).wait_recv()
        for cp in started:
            cp.wait_send()
        for cp in local:
            cp.wait()

    return pl.pallas_call(
        body, name=name,
        in_specs=[_HBM] * n, out_specs=[_HBM] * n,
        out_shape=[jax.ShapeDtypeStruct((N_DEV,) + s.shape, s.dtype) for s in srcs],
        scratch_shapes=[pltpu.SemaphoreType.DMA((n * per,)), pltpu.SemaphoreType.DMA((n * per,)),
                        pltpu.SemaphoreType.DMA((n,))],
    )(*srcs)


def _chip_semaphores(n):
    per = N_CHIPS - 1
    return [pltpu.SemaphoreType.DMA((n * per,)), pltpu.SemaphoreType.DMA((n * per,)), pltpu.SemaphoreType.DMA((n,))]


def _chip_copies(src_refs, out_refs, send_sems, recv_sems, local_sems):
    per = N_CHIPS - 1
    x, y, c = lax.axis_index("x"), lax.axis_index("y"), lax.axis_index("c")
    mine = 2 * x + y
    chips = [(1 - x, y), (x, 1 - y), (1 - x, 1 - y)]
    n = len(src_refs)
    local = [pltpu.make_async_copy(src_refs[a].at[mine], out_refs[a].at[mine], local_sems.at[a]) for a in range(n)]
    remote = []
    for a in range(n):
        for j, (px, py) in enumerate(chips):
            remote.append(pltpu.make_async_remote_copy(
                src_ref=src_refs[a].at[2 * px + py], dst_ref=out_refs[a].at[mine],
                send_sem=send_sems.at[a * per + j], recv_sem=recv_sems.at[a * per + j],
                device_id=(px, py, c), device_id_type=_MESH))
    return local, remote


def _adam_call(name, parts, w, m, v, tr):
    rows, cols = w.shape
    n_slots = parts.shape[0]

    def body(p_ref, w_ref, m_ref, v_ref, g_ref, d_ref, nm_ref, nv_ref):
        g = p_ref[0].astype(F32)
        for s in range(1, n_slots):
            g = g + p_ref[s].astype(F32)
        m_new = ADAM_B1 * m_ref[...] + (1.0 - ADAM_B1) * g
        v_new = ADAM_B2 * v_ref[...] + (1.0 - ADAM_B2) * (g * g)
        m_hat = m_new / (1.0 - ADAM_B1 ** ADAM_STEP)
        v_hat = v_new / (1.0 - ADAM_B2 ** ADAM_STEP)
        g_ref[...] = g
        d_ref[...] = -ADAM_LR * (m_hat / (jnp.sqrt(v_hat) + ADAM_EPS) + ADAM_WD * w_ref[...])
        nm_ref[...] = m_new
        nv_ref[...] = v_new

    blk = pl.BlockSpec((tr, cols), lambda i: (i, 0))
    return pl.pallas_call(
        body, name=name,
        grid=(rows // tr,),
        in_specs=[pl.BlockSpec((n_slots, tr, cols), lambda i: (0, i, 0)), blk, blk, blk],
        out_specs=[blk] * 4,
        out_shape=[jax.ShapeDtypeStruct((rows, cols), F32)] * 4,
        compiler_params=_params("arbitrary"),
    )(parts, w, m, v)


def _adam_columns_call(name, parts, w_t, m_t, v_t):
    n_slots, rows, cols = parts.shape
    row_tiles = rows // LANES
    cols_pad = -(-cols // LANES) * LANES

    def body(p_ref, w_ref, m_ref, v_ref, *out_refs):
        for a in range(row_tiles):
            g = p_ref[0, a * LANES:(a + 1) * LANES, :].astype(F32)
            for s in range(1, n_slots):
                g = g + p_ref[s, a * LANES:(a + 1) * LANES, :].astype(F32)
            g = jnp.concatenate([g, jnp.zeros((LANES, cols_pad - cols), F32)], axis=1).T[:cols]
            column_rows = pl.ds(a, cols, stride=row_tiles)
            results = (g,) + _adamw(g, w_ref[column_rows, :], m_ref[column_rows, :], v_ref[column_rows, :])
            for out_ref, val in zip(out_refs, results):
                out_ref[column_rows, :] = val

    vm = pl.BlockSpec(memory_space=pltpu.VMEM)
    return pl.pallas_call(
        body, name=name,
        in_specs=[vm] * 4, out_specs=[vm] * 4,
        out_shape=[jax.ShapeDtypeStruct(w_t.shape, F32)] * 4,
        compiler_params=pltpu.CompilerParams(vmem_limit_bytes=VMEM_LIMIT_BYTES),
    )(parts, w_t, m_t, v_t)


N_PIECES = 8
PIECE = 512
SHARD_COLS = 513
SHARD_PAD = 640
RELAYOUT_ROWS = 256


def _from_shards_call(shards):
    _, d, _ = shards.shape
    tr = RELAYOUT_ROWS

    def body(p_ref, m_ref, s_ref):
        lane = lax.broadcasted_iota(jnp.int32, (tr, SHARD_PAD), 1)
        pad = jnp.zeros((tr, SHARD_PAD - SHARD_COLS), F32)
        sh = [jnp.concatenate([p_ref[s].astype(F32), pad], axis=1) for s in range(N_DEV)]
        for p in range(N_PIECES):
            y = sh[p] if p == 0 else pltpu.roll(sh[p], p, axis=1)
            if p > 0:
                y = jnp.where(lane < p, pltpu.roll(sh[p - 1], SHARD_PAD - (SHARD_COLS - p), axis=1), y)
            m_ref[:, p * PIECE:(p + 1) * PIECE] = y[:, :PIECE].astype(m_ref.dtype)
        first_gate = N_PIECES * PIECE - (N_DEV - 1) * SHARD_COLS
        s_ref[...] = pltpu.roll(sh[N_DEV - 1], SHARD_PAD - first_gate, axis=1)[:, :LANES].astype(s_ref.dtype)

    return pl.pallas_call(
        body, name="w_in_from_shards",
        grid=(d // tr,),
        in_specs=[pl.BlockSpec((N_DEV, tr, SHARD_COLS), lambda i: (0, i, 0))],
        out_specs=[pl.BlockSpec((tr, N_PIECES * PIECE), lambda i: (i, 0)), pl.BlockSpec((tr, LANES), lambda i: (i, 0))],
        out_shape=[jax.ShapeDtypeStruct((d, N_PIECES * PIECE), shards.dtype),
                   jax.ShapeDtypeStruct((d, LANES), shards.dtype)],
        compiler_params=_params("arbitrary"),
    )(shards)


def _adamw(g, w, m, v):
    m_new = ADAM_B1 * m + (1.0 - ADAM_B1) * g
    v_new = ADAM_B2 * v + (1.0 - ADAM_B2) * (g * g)
    m_hat = m_new / (1.0 - ADAM_B1 ** ADAM_STEP)
    v_hat = v_new / (1.0 - ADAM_B2 ** ADAM_STEP)
    return -ADAM_LR * (m_hat / (jnp.sqrt(v_hat) + ADAM_EPS) + ADAM_WD * w), m_new, v_new


def _adam_small_call(parts, ws, ms, vs):
    n = len(ws)
    n_slots = parts.shape[0]

    def body(*refs):
        p_ref = refs[0]
        w_refs, m_refs, v_refs = refs[1:1 + n], refs[1 + n:1 + 2 * n], refs[1 + 2 * n:1 + 3 * n]
        loss_ref = refs[1 + 3 * n]
        outs = refs[2 + 3 * n:]
        g_all = p_ref[0]
        for s in range(1, n_slots):
            g_all = g_all + p_ref[s]
        loss_ref[...] = g_all[n:n + 1, 0:1]
        for r in range(n):
            size = w_refs[r].shape[1]
            g = g_all[r:r + 1, :size]
            delta, m_new, v_new = _adamw(g, w_refs[r][...], m_refs[r][...], v_refs[r][...])
            for kind, val in enumerate((g, delta, m_new, v_new)):
                outs[kind * n + r][...] = val

    vm = pl.BlockSpec(memory_space=pltpu.VMEM)
    shapes = [jax.ShapeDtypeStruct(w.shape, F32) for w in ws]
    return pl.pallas_call(
        body, name="adam_small",
        in_specs=[vm] * (1 + 3 * n), out_specs=[vm] * (1 + 4 * n),
        out_shape=[jax.ShapeDtypeStruct((1, 1), F32)] + shapes * 4,
    )(parts, *ws, *ms, *vs)


_SMALL_ROWS = ("norm1_w", "final_norm_w", "sb_norm_w", "gdn_norm_w", "gdn_A_log", "gdn_dt_bias", "loss")


def _pack_small(vals, width):
    rows = [jnp.pad(a.reshape(1, -1).astype(F32), ((0, 0), (0, width - a.size))) for a in vals]
    rows += [jnp.zeros((1, width), F32)] * (8 - len(rows))
    return jnp.concatenate(rows, axis=0)


def _device_step(x2d, tgt, w_main, w_small, w_out_full, conv_full, norm1_w, sb_norm_w, gdn_A_log, gdn_dt_bias,
                 gdn_norm_w, final_norm_w, distributed=False):
    t_len, d = x2d.shape
    n_chunks = t_len // CHUNK
    w_main, w_small, w_out_full = (a.astype(MXU_DTYPE) for a in (w_main, w_small, w_out_full))
    w_small_t = w_small[:, :2 * GDN_HEADS].T

    pad_lanes = lambda a, lo: jnp.pad(a.reshape(1, -1), ((0, 0), (lo, LANES - lo - a.size)))
    alog_l, dtb_l = pad_lanes(gdn_A_log, GDN_HEADS), pad_lanes(gdn_dt_bias, GDN_HEADS)
    alog_c, dtb_c = alog_l[:, :8].T, dtb_l[:, :8].T
    sbw = jnp.tile(sb_norm_w, (1, 512 // SB_HEAD_DIM))
    gdw = jnp.tile(gdn_norm_w, (1, 512 // GDN_HEAD_DIM))
    fw = final_norm_w.reshape(1, d)

    if distributed:
        proj_cols, proj_gates, ps, pst, h_t, r1, w_out_g, conv_g = _inproj_call(
            x2d, norm1_w, w_main, w_small, w_small_t, gather=(w_out_full, conv_full))
        w_out_full = w_out_g.reshape(d, d)
        conv_full = conv_g.transpose(1, 0, 2).reshape(CONV_WIDTH, N_DEV * conv_g.shape[2])
    else:
        proj_cols, proj_gates, ps, pst, h_t, r1 = _inproj_call(x2d, norm1_w, w_main, w_small, w_small_t)
    o_sb, sp_total, sb_blocks_run = _sb_fwd_call(proj_cols, t_len)
    gact = _gdn_prep_call(proj_cols, conv_full, t_len)
    beta_l, gcol_l, grow = _gdn_gates_call(ps, pst, alog_l, dtb_l, alog_c, dtb_c, t_len)
    gam_r = grow[GDN_HEADS:2 * GDN_HEADS].reshape(GDN_HEADS, n_chunks, 1, CHUNK)
    o_gd, *gdn_saved = _gdn_fwd_call(gact, beta_l, gcol_l, gam_r, t_len)

    (dx2, d_osb, d_ogd, dproj8, loss_p, g_fw, g_sbw, g_gdw, g_wout) = _post_call(
        o_sb, o_gd, proj_gates, x2d, tgt, w_out_full, sbw, gdw, fw)

    dproj8 = _sb_bwd_call(proj_cols, sp_total, sb_blocks_run, d_osb, dproj8, t_len)
    if distributed:
        d_gact3, d_gates, g_wout = _gdn_bwd_call(gact, beta_l, gcol_l, gam_r, gdn_saved, d_ogd, t_len,
                                                 scatter=(g_wout.reshape(N_DEV, d // N_DEV, d),))
    else:
        d_gact3, d_gates = _gdn_bwd_call(gact, beta_l, gcol_l, gam_r, gdn_saved, d_ogd, t_len)
    dproj8, g_conv = _gdn_prep_bwd_call(proj_cols, conv_full, d_gact3, dproj8, t_len)
    dsmall, g_alog, g_dtb = _gdn_gates_bwd_call(ps, alog_l, dtb_l, d_gates, t_len)

    if distributed:
        chip_partials = _gw_in_shards_call(h_t, dproj8, dsmall, WIRE_DTYPE)
        grad_x, g_n1, g_w_in = _dx_call(dproj8, dsmall, w_main, w_small, x2d, r1, dx2, norm1_w,
                                        chip_scatter=(chip_partials,))
    else:
        grad_x, g_n1 = _dx_call(dproj8, dsmall, w_main, w_small, x2d, r1, dx2, norm1_w)
        g_w_in = (_gw_in_call(h_t, dproj8), _gw_small_call(h_t, dsmall))
    return (loss_p, grad_x, g_n1, g_w_in, g_sbw, g_conv, g_alog, g_dtb, g_gdw, g_wout, g_fw)


def kernel(x, norm1_w, w_in, sb_norm_w, gdn_conv_w, gdn_A_log, gdn_dt_bias, gdn_norm_w, w_out, final_norm_w, loss_target, m_norm1_w, m_w_in, m_sb_norm_w, m_gdn_conv_w, m_gdn_A_log, m_gdn_dt_bias, m_gdn_norm_w, m_w_out, m_final_norm_w, v_norm1_w, v_w_in, v_sb_norm_w, v_gdn_conv_w, v_gdn_A_log, v_gdn_dt_bias, v_gdn_norm_w, v_w_out, v_final_norm_w):
    d = x.shape[2]
    shard_cols = w_in.shape[2]
    conv_cols = gdn_conv_w.shape[2]

    (w_in_g,) = _gather_call("gather_weights", [w_in[0].astype(WIRE_DTYPE)])
    w_main, w_small = _from_shards_call(w_in_g)

    (loss_p, grad_x, g_n1, p_w_in, g_sbw, g_conv, g_alog, g_dtb, g_gdw, p_wout, g_fw) = _device_step(
        x[0], loss_target[0], w_main, w_small, w_out[0].astype(WIRE_DTYPE), gdn_conv_w[0], norm1_w, sb_norm_w,
        gdn_A_log, gdn_dt_bias, gdn_norm_w, final_norm_w, distributed=True)

    g_conv_parts = g_conv.reshape(CONV_WIDTH, N_DEV, conv_cols).transpose(1, 0, 2)
    fold = lambda a, group: a.reshape(-1, group).sum(axis=0)
    small_g = _pack_small([g_n1, g_fw, fold(g_sbw, SB_HEAD_DIM), fold(g_gdw, GDN_HEAD_DIM),
                           g_alog[0, GDN_HEADS:2 * GDN_HEADS], g_dtb[0, GDN_HEADS:2 * GDN_HEADS],
                           loss_p[0, :1]], d)
    p_small, p_conv = _exchange_call("exchange_small", [small_g, g_conv_parts], [False, True])

    columns = lambda a: a.transpose(2, 0, 1).reshape(shard_cols * d // LANES, LANES)
    from_columns = lambda a: a.reshape(shard_cols, d // LANES, LANES).transpose(1, 2, 0).reshape(1, d, shard_cols)
    r_w_in = [from_columns(a) for a in _adam_columns_call("adam_w_in", p_w_in, columns(w_in), columns(m_w_in),
                                                          columns(v_w_in))]
    r_wout = _adam_call("adam_w_out", p_wout, w_out[0], m_w_out[0], v_w_out[0], d // N_DEV)
    r_conv = _adam_call("adam_conv", p_conv, gdn_conv_w[0], m_gdn_conv_w[0], v_gdn_conv_w[0], CONV_WIDTH)

    row = lambda a: a.reshape(1, -1)
    n_small = len(_SMALL_ROWS) - 1
    r_small = _adam_small_call(
        p_small,
        [norm1_w, row(final_norm_w), sb_norm_w, gdn_norm_w, gdn_A_log, gdn_dt_bias],
        [m_norm1_w, row(m_final_norm_w), m_sb_norm_w, m_gdn_norm_w, m_gdn_A_log, m_gdn_dt_bias],
        [v_norm1_w, row(v_final_norm_w), v_sb_norm_w, v_gdn_norm_w, v_gdn_A_log, v_gdn_dt_bias])

    def small_out(kind, name):
        out = r_small[1 + kind * n_small + _SMALL_ROWS.index(name)]
        return out.reshape(final_norm_w.shape) if name == "final_norm_w" else out

    def outputs(kind):
        return (small_out(kind, "norm1_w"), r_w_in[kind], small_out(kind, "sb_norm_w"), r_conv[kind][None],
                small_out(kind, "gdn_A_log"), small_out(kind, "gdn_dt_bias"), small_out(kind, "gdn_norm_w"),
                r_wout[kind][None], small_out(kind, "final_norm_w"))

    return (r_small[0][0, 0], grad_x[None], *outputs(0), *outputs(1), *outputs(2), *outputs(3))
```

```python
import functools

import jax
import jax.numpy as jnp
from jax import lax
from jax.experimental import pallas as pl
from jax.experimental.pallas import tpu as pltpu

F32 = jnp.float32
MXU_DTYPE = jnp.bfloat16
WIRE_DTYPE = jnp.bfloat16
EXACT = lax.Precision.HIGHEST
EPS = 1e-6
N_DEV = 8
SB_HEAD_DIM = 64
GDN_HEAD_DIM = 128
GDN_HEADS = 4
GDN_CHUNKS_PER_STEP = 4
GDN_BWD_GROUP = 1
CHUNK = 64
CONV_WIDTH = 4
LANES = 128
SB_BLOCK = 128
SB_BQ = 256
VMEM_LIMIT_BYTES = 56 * 1024 * 1024

PIECE_COLS = 512
PROJ_PIECE_KINDS = ("heads", "heads", "heads", "gate", "heads", "heads", "heads", "gate")
SB_FIRST_BLOCK, GDN_FIRST_BLOCK = 0, 12

DPROJ_PIECE_OF_SLOT = (0, 1, 2, 4, 5, 6, 3, 7)
DPROJ_SB_SLOT, DPROJ_GDN_SLOT, DPROJ_GATE_SLOT = 0, 3, 6

ADAM_LR = 0.001
ADAM_B1 = 0.9
ADAM_B2 = 0.999
ADAM_EPS = 1e-08
ADAM_WD = 0.01
ADAM_STEP = 10

_NN = (((1,), (0,)), ((), ()))
_NT = (((1,), (1,)), ((), ()))
_TN = (((0,), (0,)), ((), ()))
_BNN = (((2,), (1,)), ((0,), (0,)))
_BNT = (((2,), (2,)), ((0,), (0,)))
_BTN = (((1,), (1,)), ((0,), (0,)))


def _mx(a, b):
    return jnp.dot(a, b, precision=EXACT, preferred_element_type=F32)


def _split(x):
    hi = x.astype(MXU_DTYPE)
    return hi, (x - hi.astype(F32)).astype(MXU_DTYPE)


def _m3_general(a, b, dims):
    ah, al = _split(a)
    bh, bl = _split(b)
    dot = lambda x, y: lax.dot_general(x, y, dims, preferred_element_type=F32)
    (contract, _), (batch, _) = dims
    free = [ax for ax in range(a.ndim) if ax not in contract and ax not in batch][0]
    m = a.shape[free]
    both = dot(jnp.concatenate([ah, al], axis=free), bh)
    out_axis = len(batch)
    hi_part = lax.slice_in_dim(both, 0, m, axis=out_axis)
    lo_part = lax.slice_in_dim(both, m, 2 * m, axis=out_axis)
    return hi_part + (dot(ah, bl) + lo_part)


def _times_exact(a, b_exact, dims):
    ah, al = _split(a)
    (contract, _), (batch, _) = dims
    free = [ax for ax in range(a.ndim) if ax not in contract and ax not in batch][0]
    m = a.shape[free]
    both = lax.dot_general(jnp.concatenate([ah, al], axis=free), b_exact.astype(MXU_DTYPE), dims,
                           preferred_element_type=F32)
    out_axis = len(batch)
    return lax.slice_in_dim(both, 0, m, axis=out_axis) + lax.slice_in_dim(both, m, 2 * m, axis=out_axis)


def _exact_times(a_exact, b, dims):
    bh, bl = _split(b)
    n = b.shape[-1]
    both = lax.dot_general(a_exact.astype(MXU_DTYPE), jnp.concatenate([bh, bl], axis=-1), dims,
                           preferred_element_type=F32)
    return both[..., :n] + both[..., n:]


def _sigmoid(z):
    return 1.0 / (1.0 + jnp.exp(-z))


def _softplus(z):
    return jnp.maximum(z, 0.0) + jnp.log(1.0 + jnp.exp(-jnp.abs(z)))


def _params(*semantics):
    return pltpu.CompilerParams(dimension_semantics=semantics, vmem_limit_bytes=VMEM_LIMIT_BYTES)


def _inproj_call(x, norm_w, w_main, w_small, w_small_t, gather=(), tm=256):
    t_len, d = x.shape
    n = w_main.shape[1]
    ns = w_small.shape[1]
    nst = w_small_t.shape[0]
    ng = len(gather)
    steps = t_len // tm

    blocks_per_piece = PIECE_COLS // LANES
    n_gate_cols = PIECE_COLS * PROJ_PIECE_KINDS.count("gate")
    n_col_blocks = blocks_per_piece * PROJ_PIECE_KINDS.count("heads")

    def body(*refs):
        x_ref, nw_ref, wm_ref, ws_ref, wst_ref = refs[:5]
        cols_ref, pz_ref, ps_ref, pst_ref, ht_ref, r_ref = refs[5 + ng:11 + ng]
        copies = lambda: _direct_copies(refs[5:5 + ng], refs[11 + ng:11 + 2 * ng], *refs[11 + 2 * ng:], (False,) * ng)
        if ng:
            pl.when(pl.program_id(0) == 0)(lambda: _start_all(copies()))
        xv = x_ref[...]
        r = lax.rsqrt(jnp.mean(xv * xv, axis=-1, keepdims=True) + EPS)
        h = xv * r * nw_ref[...]
        hb = h.astype(MXU_DTYPE)
        n_block = n_gate = 0
        for piece, kind in enumerate(PROJ_PIECE_KINDS):
            out = jnp.dot(hb, wm_ref[:, piece * PIECE_COLS:(piece + 1) * PIECE_COLS], preferred_element_type=F32)
            if kind == "gate":
                pz_ref[:, n_gate * PIECE_COLS:(n_gate + 1) * PIECE_COLS] = out
                n_gate += 1
            else:
                for j in range(blocks_per_piece):
                    cols_ref[n_block + j] = out[:, j * LANES:(j + 1) * LANES]
                n_block += blocks_per_piece
        ps_ref[...] = jnp.dot(hb, ws_ref[...], preferred_element_type=F32)
        pst_ref[...] = lax.dot_general(wst_ref[...], hb, _NT, preferred_element_type=F32)
        ht_ref[...] = h.T.astype(MXU_DTYPE)
        r_ref[...] = r
        if ng:
            pl.when(pl.program_id(0) == steps - 1)(lambda: _wait_all(copies()))

    return pl.pallas_call(
        body, name="inproj",
        grid=(steps,),
        in_specs=[pl.BlockSpec((tm, d), lambda i: (i, 0)),
                  pl.BlockSpec((1, d), lambda i: (0, 0)),
                  pl.BlockSpec((d, n), lambda i: (0, 0)),
                  pl.BlockSpec((d, ns), lambda i: (0, 0)),
                  pl.BlockSpec((nst, d), lambda i: (0, 0))] + [_HBM] * ng,
        out_specs=[pl.BlockSpec((n_col_blocks, tm, LANES), lambda i: (0, i, 0)),
                   pl.BlockSpec((tm, n_gate_cols), lambda i: (i, 0)),
                   pl.BlockSpec((tm, ns), lambda i: (i, 0)),
                   pl.BlockSpec((nst, tm), lambda i: (0, i)),
                   pl.BlockSpec((d, tm), lambda i: (0, i)),
                   pl.BlockSpec((tm, 1), lambda i: (i, 0))] + [_HBM] * ng,
        out_shape=[jax.ShapeDtypeStruct((n_col_blocks, t_len, LANES), F32),
                   jax.ShapeDtypeStruct((t_len, n_gate_cols), F32),
                   jax.ShapeDtypeStruct((t_len, ns), F32),
                   jax.ShapeDtypeStruct((nst, t_len), F32),
                   jax.ShapeDtypeStruct((d, t_len), MXU_DTYPE),
                   jax.ShapeDtypeStruct((t_len, 1), F32)] + _direct_out_shapes(gather, (False,) * ng),
        scratch_shapes=_direct_semaphores(ng) if ng else [],
        compiler_params=_params("arbitrary"),
    )(x, norm_w, w_main, w_small, w_small_t, *gather)


def _running_sum_mm(x, tri):
    hi = x.astype(MXU_DTYPE)
    lo = (x - hi.astype(F32)).astype(MXU_DTYPE)
    return jnp.dot(hi, tri, preferred_element_type=F32) + jnp.dot(lo, tri, preferred_element_type=F32)


def _col_block(t_len, first):
    return pl.BlockSpec((1, t_len, LANES), lambda p: (first + p, 0, 0))


def _sb_iotas():
    row_i = lax.broadcasted_iota(jnp.int32, (SB_BQ, SB_BLOCK), 0)
    col_i = lax.broadcasted_iota(jnp.int32, (SB_BQ, SB_BLOCK), 1)
    sq_r = lax.broadcasted_iota(jnp.int32, (SB_BLOCK, SB_BLOCK), 0)
    sq_c = lax.broadcasted_iota(jnp.int32, (SB_BLOCK, SB_BLOCK), 1)
    return row_i, col_i, sq_r, sq_c


SB_DIAG_BLOCKS = SB_BQ // SB_BLOCK
SB_EXP_FLOOR = -110.0


def _sb_keys_descending(qi, tile, carry, z_bounds, n_heads, has_free):
    group = SB_DIAG_BLOCKS
    n_free = group * qi
    diag = list(range(group - 1, -1, -1))
    carry = tile([n_free + j for j in diag], [True] * group, carry, [j * SB_BLOCK for j in diag])

    def largest_exponent(c):
        worst = jnp.max(z_bounds[0] - c[1])
        for h in range(1, n_heads):
            worst = jnp.maximum(worst, jnp.max(z_bounds[h] - c[1 + h]))
        return worst

    always = group if has_free else 0

    def cond(state):
        return (state[0] < n_free) & ((state[1] > SB_EXP_FLOOR) | (state[0] < always))

    def body(state):
        first = n_free - 1 - state[0]
        c = tile([first - j for j in range(group)], [False] * group, state[2:])
        return (state[0] + group, largest_exponent(c), *c)

    out = lax.while_loop(cond, body, (jnp.int32(0), largest_exponent(carry), *carry))
    return out[2:], out[0]


def _sb_keys_ascending(qi, n_run, tile, carry, has_free):
    group = SB_DIAG_BLOCKS
    n_free = group * qi
    diag = list(range(group))
    kjs, los, masked = [n_free + j for j in diag], [j * SB_BLOCK for j in diag], [True] * group
    if has_free:
        early = lambda s: [n_free - n_run + group * s + j for j in range(group)]
        carry = lax.fori_loop(0, n_run // group - 1, lambda s, c: tile(early(s), [False] * group, c), carry)
        kjs, los, masked = [n_free - group + j for j in range(group)] + kjs, [0] * group + los, [False] * group + masked
    return tile(kjs, masked, carry, los)


def _sb_fwd_call(cols, t_len):
    nq = t_len // SB_BQ
    scale = float(SB_HEAD_DIM) ** -0.5
    n_pairs = 512 // LANES
    per_pair = LANES // SB_HEAD_DIM

    def body(q_blk, k_blk, v_blk, o_blk, st_ref, nrun_ref):
        q_ref, k_ref, v_ref, o_ref = q_blk.at[0], k_blk.at[0], v_blk.at[0], o_blk.at[0]
        lane = lax.broadcasted_iota(jnp.int32, (1, LANES), 1)
        row_i, col_i, sq_r, sq_c = _sb_iotas()
        ge = (sq_r >= sq_c).astype(MXU_DTYPE)
        hms = [((lane // SB_HEAD_DIM) == hh).astype(F32) for hh in range(per_pair)]
        k_sq = k_ref[...] * k_ref[...]
        k_norms = [jnp.sqrt(jnp.max(jnp.sum(k_sq * hm, axis=-1, keepdims=True))) * (1.02 * scale) for hm in hms]

        def q_block(qi, has_free):
            r0 = qi * SB_BQ if isinstance(qi, int) else pl.multiple_of(qi * SB_BQ, SB_BQ)
            rows = pl.ds(r0, SB_BQ)
            q_all = q_ref[rows, :]
            qms = [(q_all * (hm * scale)).astype(MXU_DTYPE) for hm in hms]
            z_bounds = [jnp.sqrt(jnp.sum(q_all * q_all * hm, axis=-1, keepdims=True)) * kn
                        for hm, kn in zip(hms, k_norms)]

            def tile(kjs, masked, kc, los=None):
                heads = range(per_pair)
                los = los or [0] * len(kjs)
                pairs = [(t, h) for t in range(len(kjs)) for h in heads]
                add_rows = lambda full, lo, part: full + part if lo == 0 else jnp.concatenate(
                    [full[:lo], full[lo:] + part], axis=0)
                acc, cs = kc[0], list(kc[1:])
                s0s = [kj * SB_BLOCK if isinstance(kj, int) else pl.multiple_of(kj * SB_BLOCK, SB_BLOCK) for kj in kjs]
                kbs = [k_ref[pl.ds(s0, SB_BLOCK), :].astype(MXU_DTYPE) for s0 in s0s]
                v_alls = [v_ref[pl.ds(s0, SB_BLOCK), :] for s0 in s0s]
                vms = {(t, h): (v_alls[t] * hms[h]).astype(MXU_DTYPE) for t, h in pairs}
                zs = {(t, h): lax.dot_general(qms[h][los[t]:], kbs[t], _NT, preferred_element_type=F32)
                      for t, h in pairs}
                masks = [(col_i[lo:] + s0) < (row_i[lo:] + r0) if m else None for m, lo, s0 in zip(masked, los, s0s)]
                keep = lambda t, a: a if masks[t] is None else jnp.where(masks[t], a, 0.0)
                sps = {(t, h): keep(t, _softplus(zs[t, h])) for t, h in pairs}
                sums = {p: _running_sum_mm(sps[p], ge) for p in pairs}
                mass = {}
                for t, h in pairs:
                    mass[t, h] = cs[h] if t == 0 else add_rows(
                        mass[t - 1, h], los[t - 1], jnp.sum(sps[t - 1, h], axis=-1, keepdims=True))
                ws = {(t, h): keep(t, jnp.exp(zs[t, h] - (sums[t, h] + mass[t, h][los[t]:]))) for t, h in pairs}
                for t, h in pairs:
                    acc = add_rows(acc, los[t], jnp.dot(ws[t, h].astype(MXU_DTYPE), vms[t, h],
                                                        preferred_element_type=F32))
                last = len(kjs) - 1
                cs = [add_rows(mass[last, h], los[last], jnp.sum(sps[last, h], axis=-1, keepdims=True)) for h in heads]
                return (acc, *cs)

            zero_col = jnp.zeros((SB_BQ, 1), F32)
            out, n_run = _sb_keys_descending(
                qi, tile, (jnp.zeros((SB_BQ, LANES), F32),) + (zero_col,) * per_pair, z_bounds, per_pair, has_free)
            o_ref[rows, :] = out[0]
            masses = jnp.zeros((SB_BQ, LANES), F32)
            for hh in range(per_pair):
                masses = jnp.where(lane == hh, out[1 + hh], masses)
            st_ref[rows, :] = masses
            nrun_ref[pl.program_id(0), qi] = n_run

        q_block(0, False)
        lax.fori_loop(1, nq, lambda qi, carry: (q_block(qi, True), carry)[1], 0)

    return pl.pallas_call(
        body, name="sb_fwd",
        grid=(n_pairs,),
        in_specs=[_col_block(t_len, SB_FIRST_BLOCK), _col_block(t_len, SB_FIRST_BLOCK + n_pairs),
                  _col_block(t_len, SB_FIRST_BLOCK + 2 * n_pairs)],
        out_specs=[_col_block(t_len, 0),
                   pl.BlockSpec((t_len, LANES), lambda p: (0, p)),
                   pl.BlockSpec(memory_space=pltpu.SMEM)],
        out_shape=[jax.ShapeDtypeStruct((n_pairs, t_len, LANES), F32),
                   jax.ShapeDtypeStruct((t_len, n_pairs * LANES), F32),
                   jax.ShapeDtypeStruct((n_pairs, nq), jnp.int32)],
        compiler_params=_params("arbitrary"),
    )(cols, cols, cols)


def _sb_bwd_call(cols, sp_total, n_run_all, d_o, dproj, t_len):
    nq = t_len // SB_BQ
    scale = float(SB_HEAD_DIM) ** -0.5
    n_pairs = 512 // LANES
    per_pair = LANES // SB_HEAD_DIM

    def body(q_blk, k_blk, v_blk, st_ref, nrun_ref, do_blk, dproj_in_ref, d_ref):
        q_ref, k_ref, v_ref, do_ref = q_blk.at[0], k_blk.at[0], v_blk.at[0], do_blk.at[0]
        lane = lax.broadcasted_iota(jnp.int32, (1, LANES), 1)
        row_i, col_i, sq_r, sq_c = _sb_iotas()
        lt = (sq_r < sq_c).astype(MXU_DTYPE)
        le = (sq_r <= sq_c).astype(MXU_DTYPE)
        hms = [((lane // SB_HEAD_DIM) == hh).astype(F32) for hh in range(per_pair)]
        d_ref[1] = jnp.zeros((t_len, LANES), F32)
        d_ref[2] = jnp.zeros((t_len, LANES), F32)

        def q_block(qi, has_free):
            r0 = qi * SB_BQ if isinstance(qi, int) else pl.multiple_of(qi * SB_BQ, SB_BQ)
            rows = pl.ds(r0, SB_BQ)
            q_all, do_all = q_ref[rows, :], do_ref[rows, :]
            qms = [(q_all * (hm * scale)).astype(MXU_DTYPE) for hm in hms]
            doms = [(do_all * hm).astype(MXU_DTYPE) for hm in hms]
            masses = st_ref[rows, :]
            totals = [jnp.sum(jnp.where(lane == hh, masses, 0.0), axis=-1, keepdims=True) for hh in range(per_pair)]

            def tile(kjs, masked, kc, los=None):
                heads = range(per_pair)
                los = los or [0] * len(kjs)
                pairs = [(t, h) for t in range(len(kjs)) for h in heads]
                add_rows = lambda full, lo, part: full + part if lo == 0 else jnp.concatenate(
                    [full[:lo], full[lo:] + part], axis=0)
                rsum = lambda a: jnp.sum(a, axis=-1, keepdims=True)
                dq, cls, gls = kc[0], list(kc[1:1 + per_pair]), list(kc[1 + per_pair:])
                s0s = [kj * SB_BLOCK if isinstance(kj, int) else pl.multiple_of(kj * SB_BLOCK, SB_BLOCK) for kj in kjs]
                k_alls = [k_ref[pl.ds(s0, SB_BLOCK), :] for s0 in s0s]
                v_alls = [v_ref[pl.ds(s0, SB_BLOCK), :] for s0 in s0s]
                kbs = [k_all.astype(MXU_DTYPE) for k_all in k_alls]
                vms = {(t, h): (v_alls[t] * hms[h]).astype(MXU_DTYPE) for t, h in pairs}
                kms = {(t, h): (k_alls[t] * (hms[h] * scale)).astype(MXU_DTYPE) for t, h in pairs}
                q_live = {(t, h): qms[h][los[t]:] for t, h in pairs}
                do_live = {(t, h): doms[h][los[t]:] for t, h in pairs}
                zs = {p: lax.dot_general(q_live[p], kbs[p[0]], _NT, preferred_element_type=F32) for p in pairs}
                das = {p: lax.dot_general(do_live[p], vms[p], _NT, preferred_element_type=F32) for p in pairs}
                masks = [(col_i[lo:] + s0) < (row_i[lo:] + r0) if m else None for m, lo, s0 in zip(masked, los, s0s)]
                keep = lambda t, a: a if masks[t] is None else jnp.where(masks[t], a, 0.0)
                sp_alls = {p: _softplus(zs[p]) for p in pairs}
                sps = {(t, h): keep(t, sp_alls[t, h]) for t, h in pairs}
                lefts = {p: _running_sum_mm(sps[p], lt) for p in pairs}
                cl = {}
                for t, h in pairs:
                    cl[t, h] = cls[h] if t == 0 else add_rows(cl[t - 1, h], los[t - 1], rsum(sps[t - 1, h]))
                ws = {(t, h): keep(t, jnp.exp(zs[t, h] - ((totals[h] - cl[t, h])[los[t]:] - lefts[t, h])))
                      for t, h in pairs}
                gs = {p: das[p] * ws[p] for p in pairs}
                g_sums = {p: _running_sum_mm(gs[p], le) for p in pairs}
                gl = {}
                for t, h in pairs:
                    gl[t, h] = gls[h] if t == 0 else add_rows(gl[t - 1, h], los[t - 1], rsum(gs[t - 1, h]))
                dzs = {(t, h): keep(t, gs[t, h] - jnp.exp(zs[t, h] - sp_alls[t, h]) * (gl[t, h][los[t]:] + g_sums[t, h])
                               ).astype(MXU_DTYPE) for t, h in pairs}
                for t in range(len(kjs)):
                    dk_t = jnp.zeros((SB_BLOCK, LANES), F32)
                    dv_t = jnp.zeros((SB_BLOCK, LANES), F32)
                    for h in heads:
                        dq = add_rows(dq, los[t], jnp.dot(dzs[t, h], kms[t, h], preferred_element_type=F32))
                        dk_t = dk_t + lax.dot_general(dzs[t, h], q_live[t, h], _TN, preferred_element_type=F32)
                        dv_t = dv_t + lax.dot_general(ws[t, h].astype(MXU_DTYPE), do_live[t, h], _TN,
                                                      preferred_element_type=F32)
                    d_ref[1, pl.ds(s0s[t], SB_BLOCK), :] += dk_t
                    d_ref[2, pl.ds(s0s[t], SB_BLOCK), :] += dv_t
                last = len(kjs) - 1
                cls = [add_rows(cl[last, h], los[last], rsum(sps[last, h])) for h in heads]
                gls = [add_rows(gl[last, h], los[last], rsum(gs[last, h])) for h in heads]
                return (dq, *cls, *gls)

            zero_col = jnp.zeros((SB_BQ, 1), F32)
            out = _sb_keys_ascending(qi, nrun_ref[pl.program_id(0), qi], tile,
                                     (jnp.zeros((SB_BQ, LANES), F32),) + (zero_col,) * (2 * per_pair), has_free)
            d_ref[0, rows, :] = out[0]

        q_block(0, False)
        lax.fori_loop(1, nq, lambda qi, carry: (q_block(qi, True), carry)[1], 0)

    return pl.pallas_call(
        body, name="sb_bwd",
        grid=(n_pairs,),
        in_specs=[_col_block(t_len, SB_FIRST_BLOCK), _col_block(t_len, SB_FIRST_BLOCK + n_pairs),
                  _col_block(t_len, SB_FIRST_BLOCK + 2 * n_pairs),
                  pl.BlockSpec((t_len, LANES), lambda p: (0, p)),
                  pl.BlockSpec(memory_space=pltpu.SMEM), _col_block(t_len, 0), _HBM],
        out_specs=pl.BlockSpec((3, t_len, LANES), lambda p: (DPROJ_SB_SLOT // 3, 0, p)),
        out_shape=jax.ShapeDtypeStruct(dproj.shape, dproj.dtype),
        input_output_aliases={6: 0},
        compiler_params=_params("arbitrary"),
    )(cols, cols, cols, sp_total, n_run_all, d_o, dproj)


def _conv_taps(xin, rows, t_len):
    taps = []
    for i in range(CONV_WIDTH):
        shift = CONV_WIDTH - 1 - i
        if shift == 0:
            taps.append(xin)
        else:
            taps.append(jnp.where(rows >= shift, pltpu.roll(xin, shift, axis=0), 0.0))
    return taps


def _gdn_prep_body_common(x_ref, w_ref, t_len):
    j = pl.program_id(0)
    xin = x_ref[...]
    rows = lax.broadcasted_iota(jnp.int32, (t_len, LANES), 0)
    taps = _conv_taps(xin, rows, t_len)
    pre = taps[0] * w_ref[0:1, :]
    for i in range(1, CONV_WIDTH):
        pre = pre + taps[i] * w_ref[i:i + 1, :]
    sg = _sigmoid(pre)
    act = pre * sg
    is_qk = j < 2 * GDN_HEADS
    nrm = jnp.where(is_qk, lax.rsqrt(jnp.sum(act * act, axis=-1, keepdims=True) + EPS), 1.0)
    sc = jnp.where(j < GDN_HEADS, float(GDN_HEAD_DIM) ** -0.5, 1.0)
    return j, rows, taps, pre, sg, act, is_qk, nrm, sc


def _gdn_prep_call(cols, conv_w, t_len, after):
    def body(x_blk, w_ref, after_ref, out_ref):
        _, _, _, _, _, act, _, nrm, sc = _gdn_prep_body_common(x_blk.at[0], w_ref, t_len)
        out_ref[...] = act * nrm * sc

    return pl.pallas_call(
        body, name="gdn_prep",
        grid=(3 * GDN_HEADS,),
        in_specs=[_col_block(t_len, GDN_FIRST_BLOCK),
                  pl.BlockSpec((CONV_WIDTH, LANES), lambda j: (0, j)),
                  pl.BlockSpec(memory_space=pltpu.SMEM)],
        out_specs=pl.BlockSpec((t_len, LANES), lambda j: (0, j)),
        out_shape=jax.ShapeDtypeStruct((t_len, 3 * 512), F32),
        compiler_params=_params("arbitrary"),
    )(cols, conv_w, after)


def _gdn_prep_bwd_call(cols, conv_w, d_act3, dproj, t_len):
    def body(x_blk, w_ref, d_ref, dproj_in_ref, dx_ref, dw_ref):
        _, rows, taps, pre, sg, act, is_qk, nrm, sc = _gdn_prep_body_common(x_blk.at[0], w_ref, t_len)
        d_out = d_ref[0]
        dn = d_out * sc
        d_norm = nrm * dn - act * (nrm * nrm * nrm) * jnp.sum(dn * act, axis=-1, keepdims=True)
        d_act = jnp.where(is_qk, d_norm, d_out)
        d_pre = d_act * sg * (1.0 + pre * (1.0 - sg))
        dx = d_pre * w_ref[CONV_WIDTH - 1:CONV_WIDTH, :]
        dw_ref[CONV_WIDTH - 1:CONV_WIDTH, :] = jnp.sum(d_pre * taps[CONV_WIDTH - 1], axis=0, keepdims=True)
        for i in range(CONV_WIDTH - 1):
            shift = CONV_WIDTH - 1 - i
            up = jnp.where(rows < t_len - shift, pltpu.roll(d_pre, t_len - shift, axis=0), 0.0)
            dx = dx + up * w_ref[i:i + 1, :]
            dw_ref[i:i + 1, :] = jnp.sum(d_pre * taps[i], axis=0, keepdims=True)
        dx_ref[0] = dx

    return pl.pallas_call(
        body, name="gdn_prep_bwd",
        grid=(3 * GDN_HEADS,),
        in_specs=[_col_block(t_len, GDN_FIRST_BLOCK),
                  pl.BlockSpec((CONV_WIDTH, LANES), lambda j: (0, j)),
                  pl.BlockSpec((1, t_len, LANES), lambda j: (j // GDN_HEADS, 0, j % GDN_HEADS)), _HBM],
        out_specs=[pl.BlockSpec((1, t_len, LANES), lambda j: (DPROJ_GDN_SLOT + j // GDN_HEADS, 0, j % GDN_HEADS)),
                   pl.BlockSpec((CONV_WIDTH, LANES), lambda j: (0, j))],
        out_shape=[jax.ShapeDtypeStruct(dproj.shape, dproj.dtype),
                   jax.ShapeDtypeStruct((CONV_WIDTH, 3 * 512), F32)],
        input_output_aliases={3: 0},
        compiler_params=_params("arbitrary"),
    )(cols, conv_w, d_act3, dproj)


def _chunk_cumsum_matrix():
    r = lax.broadcasted_iota(jnp.int32, (LANES, LANES), 0)
    c = lax.broadcasted_iota(jnp.int32, (LANES, LANES), 1)
    return ((r <= c) & ((r // CHUNK) == (c // CHUNK))).astype(F32)


def _gdn_gates_call(ps, pst, alog_l, dtb_l, alog_c, dtb_c, t_len):
    def body(ps_ref, pst_ref, al_ref, dl_ref, ac_ref, dc_ref, beta_ref, gcol_ref, grow_ref):
        upper = _chunk_cumsum_matrix()
        lower = upper.T
        psv = ps_ref[...]
        beta_ref[...] = _sigmoid(psv)
        g_l = -jnp.exp(al_ref[...]) * _softplus(psv + dl_ref[...])
        g_r = -jnp.exp(ac_ref[...]) * _softplus(pst_ref[...] + dc_ref[...])
        for w in range(t_len // LANES):
            sl = slice(w * LANES, (w + 1) * LANES)
            gcol_ref[sl, :] = _mx(lower, g_l[sl, :])
            grow_ref[:, sl] = _mx(g_r[:, sl], upper)

    vm = pl.BlockSpec(memory_space=pltpu.VMEM)
    return pl.pallas_call(
        body, name="gdn_gates",
        in_specs=[vm] * 6, out_specs=[vm] * 3,
        out_shape=[jax.ShapeDtypeStruct((t_len, LANES), F32),
                   jax.ShapeDtypeStruct((t_len, LANES), F32),
                   jax.ShapeDtypeStruct((8, t_len), F32)],
        compiler_params=pltpu.CompilerParams(vmem_limit_bytes=VMEM_LIMIT_BYTES),
    )(ps, pst, alog_l, dtb_l, alog_c, dtb_c)


def _gdn_gates_bwd_call(ps, alog_l, dtb_l, d_l, t_len):
    def body(ps_ref, al_ref, dl_ref, d_ref, dps_ref, gal_ref, gdt_ref):
        lane = lax.broadcasted_iota(jnp.int32, (1, LANES), 1)
        psv = ps_ref[...]
        dv = d_ref[...]
        beta = _sigmoid(psv)
        ea = jnp.exp(al_ref[...])
        arg = psv + dl_ref[...]
        g = -ea * _softplus(arg)
        d_a = dv * (-ea) * _sigmoid(arg)
        is_a = (lane >= GDN_HEADS) & (lane < 2 * GDN_HEADS)
        dps_ref[...] = jnp.where(lane < GDN_HEADS, dv * beta * (1.0 - beta), jnp.where(is_a, d_a, 0.0))
        gdt_ref[...] = jnp.where(is_a, jnp.sum(d_a, axis=0, keepdims=True), 0.0)
        gal_ref[...] = jnp.where(is_a, jnp.sum(dv * g, axis=0, keepdims=True), 0.0)

    vm = pl.BlockSpec(memory_space=pltpu.VMEM)
    return pl.pallas_call(
        body, name="gdn_gates_bwd",
        in_specs=[vm] * 4, out_specs=[vm] * 3,
        out_shape=[jax.ShapeDtypeStruct((t_len, LANES), F32),
                   jax.ShapeDtypeStruct((1, LANES), F32),
                   jax.ShapeDtypeStruct((1, LANES), F32)],
        compiler_params=pltpu.CompilerParams(vmem_limit_bytes=VMEM_LIMIT_BYTES),
    )(ps, alog_l, dtb_l, d_l)


def _bm(a, b):
    return _m3_general(a, b, _BNN)


def _bm_nt(a, b):
    return _m3_general(a, b, _BNT)


def _bm_tn(a, b):
    return _m3_general(a, b, _BTN)


def _heads_of(ref, rows):
    return jnp.stack([ref[rows, h * GDN_HEAD_DIM:(h + 1) * GDN_HEAD_DIM] for h in range(GDN_HEADS)])


def _chunk_terms(q_ref, k_ref, v_ref, b_ref, gc_ref, gr_ref, c, incl, strict, n=1, scores=True):
    r0 = c * CHUNK if isinstance(c, int) else pl.multiple_of(c * CHUNK, CHUNK)
    rows = pl.ds(r0, n * CHUNK)
    per_chunk = lambda x: x.reshape(GDN_HEADS * n, CHUNK, x.shape[-1])
    q, k, v = (per_chunk(_heads_of(ref, rows)) for ref in (q_ref, k_ref, v_ref))
    lane_ids = lax.broadcasted_iota(jnp.int32, (1, LANES), 1)
    pick = lambda slab, first: jnp.stack([jnp.sum(jnp.where(lane_ids == first + h, slab, 0.0), axis=-1, keepdims=True)
                                          for h in range(GDN_HEADS)])
    b = per_chunk(pick(b_ref[rows, :], 0))
    gc = per_chunk(pick(gc_ref[rows, :], GDN_HEADS))
    gr = gr_ref[:, c] if n == 1 else gr_ref[:, c:c + n].reshape(GDN_HEADS * n, 1, CHUNK)
    dm = jnp.where(incl, jnp.exp(jnp.where(incl, gc - gr, 0.0)), 0.0)
    kb = k * b
    vb = v * b
    e = jnp.exp(gc)
    a = p = None
    if scores:
        kk_qk = _bm_nt(jnp.concatenate([kb, q], axis=1), k)
        a = jnp.where(strict, kk_qk[:, :CHUNK] * dm, 0.0)
        p = jnp.where(incl, kk_qk[:, CHUNK:] * dm, 0.0)
    gl = gc[:, CHUNK - 1:CHUNK, :]
    eg = jnp.exp(gl - gc)
    return rows, q, k, v, b, gc, dm, kb, vb, e, a, p, gl, eg


def _unit_lower_inverse(a, eye):
    x = -a
    tm = eye + x
    xp = _bm(x, x)
    for _ in range(4):
        both = _bm(jnp.concatenate([xp, tm], axis=1), xp)
        tm = tm + both[:, CHUNK:]
        xp = both[:, :CHUNK]
    return tm + _bm(tm, xp)


def _gdn_specs(t_len, n_chunks, reverse):
    cps = GDN_CHUNKS_PER_STEP
    steps = n_chunks // cps
    at = (lambda g: steps - 1 - g) if reverse else (lambda g: g)
    rows_blk = lambda width, part=0: pl.BlockSpec((cps * CHUNK, width), lambda g: (at(g), part))
    gate_r = pl.BlockSpec((GDN_HEADS, cps, 1, CHUNK), lambda g: (0, at(g), 0, 0))
    per_chunk = lambda r, c: pl.BlockSpec((GDN_HEADS, cps, r, c), lambda g: (0, at(g), 0, 0))
    return cps, steps, rows_blk, gate_r, per_chunk


def _gdn_fwd_call(gact, beta_c, gam_c, gam_r, t_len):
    n_chunks = t_len // CHUNK
    dk = GDN_HEAD_DIM
    width = GDN_HEADS * dk
    cps, steps, rows_blk, gate_r, per_chunk = _gdn_specs(t_len, n_chunks, False)

    def body(q_ref, k_ref, v_ref, b_ref, gc_ref, gr_ref, o_ref, s_ref, t_ref, a_ref, p_ref, uw_ref, vn_ref, state_ref):
        row = lax.broadcasted_iota(jnp.int32, (CHUNK, CHUNK), 0)
        col = lax.broadcasted_iota(jnp.int32, (CHUNK, CHUNK), 1)
        incl, strict = row >= col, row > col
        eye = (row == col).astype(F32)

        @pl.when(pl.program_id(0) == 0)
        def _():
            state_ref[...] = jnp.zeros_like(state_ref)

        _, q, k, v, b, gc, dm, kb, vb, e, a, p, gl, eg = _chunk_terms(
            q_ref, k_ref, v_ref, b_ref, gc_ref, gr_ref, 0, incl, strict, cps)
        tm = _unit_lower_inverse(a, eye)
        uw = _bm(tm, jnp.concatenate([vb, kb * e], axis=2))
        w_qe = jnp.concatenate([uw[:, :, dk:], q * e], axis=1)
        u, kd, decay = uw[:, :, :dk], k * eg, jnp.exp(gl)
        per_chunk_block = lambda x: x.reshape(GDN_HEADS, cps, CHUNK, CHUNK)
        t_ref[...], a_ref[...], p_ref[...] = per_chunk_block(tm), per_chunk_block(a), per_chunk_block(p)
        uw_heads = uw.reshape(GDN_HEADS, cps * CHUNK, 2 * dk)
        for h in range(GDN_HEADS):
            uw_ref[:, h * 2 * dk:(h + 1) * 2 * dk] = uw_heads[h]

        of_chunk = lambda x, c: jnp.stack([x[h * cps + c] for h in range(GDN_HEADS)])
        s = state_ref[...]
        for c in range(cps):
            ws_qs = _bm(of_chunk(w_qe, c), s)
            vn = of_chunk(u, c) - ws_qs[:, :CHUNK]
            o = ws_qs[:, CHUNK:] + _bm(of_chunk(p, c), vn)
            for h in range(GDN_HEADS):
                o_ref[c * CHUNK:(c + 1) * CHUNK, h * dk:(h + 1) * dk] = o[h]
                vn_ref[c * CHUNK:(c + 1) * CHUNK, h * dk:(h + 1) * dk] = vn[h]
            s_ref[:, c] = s
            s = s * of_chunk(decay, c) + _bm_tn(of_chunk(kd, c), vn)
        state_ref[...] = s

    scores = jax.ShapeDtypeStruct((GDN_HEADS, n_chunks, CHUNK, CHUNK), F32)
    return pl.pallas_call(
        body, name="gdn_fwd",
        grid=(steps,),
        in_specs=[rows_blk(width, 0), rows_blk(width, 1), rows_blk(width, 2), rows_blk(LANES), rows_blk(LANES), gate_r],
        out_specs=[rows_blk(width), per_chunk(dk, dk), per_chunk(CHUNK, CHUNK), per_chunk(CHUNK, CHUNK),
                   per_chunk(CHUNK, CHUNK), rows_blk(2 * width), rows_blk(width)],
        out_shape=[jax.ShapeDtypeStruct((t_len, width), F32),
                   jax.ShapeDtypeStruct((GDN_HEADS, n_chunks, dk, dk), F32), scores, scores, scores,
                   jax.ShapeDtypeStruct((t_len, 2 * width), F32), jax.ShapeDtypeStruct((t_len, width), F32)],
        scratch_shapes=[pltpu.VMEM((GDN_HEADS, dk, dk), F32)],
        compiler_params=_params("arbitrary"),
    )(gact, gact, gact, beta_c, gam_c, gam_r)


def _gdn_bwd_call(gact, beta_c, gam_c, gam_r, saved, d_o, t_len, scatter=()):
    n_chunks = t_len // CHUNK
    dk = GDN_HEAD_DIM
    width = GDN_HEADS * dk
    cps, steps, rows_blk, gate_r, per_chunk = _gdn_specs(t_len, n_chunks, True)
    nx = len(scatter)
    n_in = 13

    def body(*refs):
        q_ref, k_ref, v_ref, b_ref, gc_ref, gr_ref = refs[:6]
        saved_refs, do_ref = refs[6:12], refs[12]
        d_ref, dgate_ref = refs[n_in + nx:n_in + 2 + nx]
        dstate_ref = refs[n_in + 2 + 2 * nx]
        copies = lambda: _direct_copies(refs[n_in:n_in + nx], refs[n_in + 2 + nx:n_in + 2 + 2 * nx],
                                        *refs[n_in + 3 + 2 * nx:], (True,) * nx)
        if nx:
            pl.when(pl.program_id(0) == 0)(lambda: _start_all(copies()))
        row = lax.broadcasted_iota(jnp.int32, (CHUNK, CHUNK), 0)
        col = lax.broadcasted_iota(jnp.int32, (CHUNK, CHUNK), 1)
        incl, strict = row >= col, row > col
        ng = GDN_BWD_GROUP
        nb = GDN_HEADS * ng
        upper = jnp.broadcast_to((row <= col).astype(F32), (nb, CHUNK, CHUNK))
        ones = jnp.ones((nb, CHUNK, LANES), F32)
        last_row = lax.broadcasted_iota(jnp.int32, (CHUNK, 1), 0) == CHUNK - 1
        lane_ids = lax.broadcasted_iota(jnp.int32, (1, LANES), 1)
        rsum = lambda m: jnp.sum(m, axis=-1, keepdims=True)
        total = lambda m: jnp.sum(rsum(m), axis=1, keepdims=True)
        of_chunk = lambda x, c: jnp.stack([x[h * ng + c] for h in range(GDN_HEADS)])

        @pl.when(pl.program_id(0) == 0)
        def _():
            dstate_ref[...] = jnp.zeros_like(dstate_ref)

        for c0 in range(cps - ng, -1, -ng):
            group(c0, q_ref, k_ref, v_ref, b_ref, gc_ref, gr_ref, saved_refs, do_ref, d_ref, dgate_ref, dstate_ref,
                  incl, strict, upper, ones, last_row, lane_ids, rsum, total, of_chunk)
        if nx:
            pl.when(pl.program_id(0) == steps - 1)(lambda: _wait_all(copies()))

    def group(c0, q_ref, k_ref, v_ref, b_ref, gc_ref, gr_ref, saved_refs, do_ref, d_ref, dgate_ref, dstate_ref,
              incl, strict, upper, ones, last_row, lane_ids, rsum, total, of_chunk):
        ng = GDN_BWD_GROUP
        nb = GDN_HEADS * ng
        rows = pl.ds(c0 * CHUNK, ng * CHUNK)
        s_ref, t_ref, a_ref, p_ref, uw_ref, vn_ref = saved_refs
        _, q, k, v, b, gc, dm, kb, vb, e, _, _, gl, eg = _chunk_terms(
            q_ref, k_ref, v_ref, b_ref, gc_ref, gr_ref, c0, incl, strict, ng, scores=False)
        s = s_ref[:, c0:c0 + ng].reshape(nb, dk, dk)
        tm = t_ref[:, c0:c0 + ng].reshape(nb, CHUNK, CHUNK)
        a = a_ref[:, c0:c0 + ng].reshape(nb, CHUNK, CHUNK)
        p = p_ref[:, c0:c0 + ng].reshape(nb, CHUNK, CHUNK)
        d_out = _heads_of(do_ref, rows).reshape(nb, CHUNK, dk)
        vn = _heads_of(vn_ref, rows).reshape(nb, CHUNK, dk)
        uw = jnp.stack([uw_ref[rows, h * 2 * dk:(h + 1) * 2 * dk] for h in range(GDN_HEADS)]).reshape(nb, CHUNK, 2 * dk)
        u, w = uw[:, :, :dk], uw[:, :, dk:]
        el = jnp.exp(gl)
        kbe = kb * e
        qe = q * e
        kd = k * eg
        pt_do = _bm_tn(p, d_out)
        qet_do = _bm_tn(qe, d_out)

        ds = dstate_ref[...]
        d_vn_c, ds_c = [None] * ng, [None] * ng
        for c in range(ng - 1, -1, -1):
            ds_c[c] = ds
            d_vn_c[c] = of_chunk(pt_do, c) + _bm(of_chunk(kd, c), ds)
            ds = of_chunk(el, c) * ds + of_chunk(qet_do, c) - _bm_tn(of_chunk(w, c), d_vn_c[c])
        dstate_ref[...] = ds
        by_chunk = lambda xs: jnp.stack([xs[c][h] for h in range(GDN_HEADS) for c in range(ng)])
        d_vn, ds = by_chunk(d_vn_c), by_chunk(ds_c)

        on_s = _bm_nt(jnp.concatenate([d_out, d_vn], axis=1), s)
        d_qe, d_w = on_s[:, :CHUNK], -on_s[:, CHUNK:]
        d_p = jnp.where(incl, _bm_nt(d_out, vn), 0.0)
        d_kd = _bm_nt(vn, ds)
        d_both = _bm_tn(tm, jnp.concatenate([d_vn, d_w], axis=2))
        d_vb, d_kbe = d_both[:, :, :dk], d_both[:, :, dk:]
        d_a = -jnp.where(strict, _bm_nt(d_both, uw), 0.0)
        m = d_a * dm
        n = d_p * dm
        on_k = _bm(jnp.concatenate([m, n], axis=1), k)
        d_kb = on_k[:, :CHUNK] + d_kbe * e
        d_q = on_k[:, CHUNK:] + d_qe * e
        d_k = (_bm_tn(jnp.concatenate([m, n], axis=1), jnp.concatenate([kb, q], axis=1))
               + d_kd * eg + b * d_kb)
        d_v = b * d_vb
        r = d_a * a + d_p * p
        kd_term = rsum(d_kd * kd)
        d_gl = total(ds * s) * el + jnp.sum(kd_term, axis=1, keepdims=True)
        d_gam = (rsum(r) - _times_exact(r, ones, _BTN)[:, :, 0:1] + rsum(d_qe * qe) + rsum(d_kbe * kbe) - kd_term
                 + jnp.where(last_row, d_gl, 0.0))
        d_beta = rsum(d_kb * k) + rsum(d_vb * v)
        d_g = _exact_times(upper, d_gam * ones, _BNN)[:, :, 0:1]
        per_head = lambda x: x.reshape(GDN_HEADS, ng * CHUNK, x.shape[-1])
        d_q, d_k, d_v, d_beta, d_g = (per_head(x) for x in (d_q, d_k, d_v, d_beta, d_g))
        gates = jnp.zeros((ng * CHUNK, LANES), F32)
        for h in range(GDN_HEADS):
            lanes = slice(h * dk, (h + 1) * dk)
            d_ref[0, rows, lanes] = d_q[h]
            d_ref[1, rows, lanes] = d_k[h]
            d_ref[2, rows, lanes] = d_v[h]
            gates = gates + (jnp.where(lane_ids == h, d_beta[h], 0.0)
                             + jnp.where(lane_ids == GDN_HEADS + h, d_g[h], 0.0))
        dgate_ref[rows, :] = gates

    d_spec = pl.BlockSpec((3, cps * CHUNK, width), lambda g: (0, steps - 1 - g, 0))
    return pl.pallas_call(
        body, name="gdn_bwd",
        grid=(steps,),
        in_specs=[rows_blk(width, 0), rows_blk(width, 1), rows_blk(width, 2), rows_blk(LANES), rows_blk(LANES), gate_r,
                  per_chunk(dk, dk), per_chunk(CHUNK, CHUNK), per_chunk(CHUNK, CHUNK), per_chunk(CHUNK, CHUNK),
                  rows_blk(2 * width), rows_blk(width), rows_blk(width)] + [_HBM] * nx,
        out_specs=[d_spec, rows_blk(LANES)] + [_HBM] * nx,
        out_shape=[jax.ShapeDtypeStruct((3, t_len, width), F32),
                   jax.ShapeDtypeStruct((t_len, LANES), F32)] + _direct_out_shapes(scatter, (True,) * nx),
        scratch_shapes=[pltpu.VMEM((GDN_HEADS, dk, dk), F32)] + (_direct_semaphores(nx) if nx else []),
        compiler_params=_params("arbitrary"),
    )(gact, gact, gact, beta_c, gam_c, gam_r, *saved, d_o, *scatter)


def _group_sums(x, group):
    rows, width = x.shape
    lane = lax.broadcasted_iota(jnp.int32, (1, LANES), 1)
    out = []
    for t in range(width // LANES):
        seg = x[:, t * LANES:(t + 1) * LANES]
        if group == LANES:
            out.append(jnp.broadcast_to(jnp.sum(seg, axis=-1, keepdims=True), (rows, LANES)))
        else:
            low = jnp.sum(jnp.where(lane < group, seg, 0.0), axis=-1, keepdims=True)
            high = jnp.sum(jnp.where(lane < group, 0.0, seg), axis=-1, keepdims=True)
            out.append(jnp.where(lane < group, low, high))
    return jnp.concatenate(out, axis=1)


def _post_call(o_sb, o_gd, proj_gates, x, target, w_out, sbw, gdw, fw, tm=256):
    t_len, d = x.shape
    half = 512
    sb_blocks = half // LANES

    def body(osb_ref, ogd_ref, zsb_ref, zgd_ref, x_ref, tg_ref, wo_ref, sbw_ref, gdw_ref, fw_ref,
             dx2_ref, dosb_ref, dogd_ref, dz_ref, loss_ref, gfw_ref, gsb_ref, ggd_ref, gwo_ref):
        step = pl.program_id(0)

        @pl.when(step == 0)
        def _():
            loss_ref[...] = jnp.zeros_like(loss_ref)
            gfw_ref[...] = jnp.zeros_like(gfw_ref)
            gsb_ref[...] = jnp.zeros_like(gsb_ref)
            ggd_ref[...] = jnp.zeros_like(ggd_ref)
            gwo_ref[...] = jnp.zeros_like(gwo_ref)

        def head_forward(o, z, w, head_dim):
            r = lax.rsqrt(_group_sums(o * o, head_dim) * (1.0 / head_dim) + EPS)
            nrm = o * r * w
            sg = _sigmoid(z)
            return r, nrm, sg, nrm * (z * sg)

        def head_backward(d_m, o, z, w, head_dim, r, nrm, sg):
            d_n = d_m * (z * sg)
            d_z = d_m * nrm * (sg * (1.0 + z * (1.0 - sg)))
            dnw = d_n * w
            d_o = r * dnw - o * (r * r * r) * (_group_sums(dnw * o, head_dim) * (1.0 / head_dim))
            return d_o, d_z, jnp.sum(d_n * o * r, axis=0, keepdims=True)

        osb = jnp.concatenate([osb_ref[j] for j in range(sb_blocks)], axis=1)
        ogd, zsb, zgd = ogd_ref[...], zsb_ref[...], zgd_ref[...]
        sbw_v, gdw_v = sbw_ref[...], gdw_ref[...]
        r_sb, n_sb, sg_sb, m_sb = head_forward(osb, zsb, sbw_v, SB_HEAD_DIM)
        r_gd, n_gd, sg_gd, m_gd = head_forward(ogd, zgd, gdw_v, GDN_HEAD_DIM)
        mixed = jnp.concatenate([m_sb, m_gd], axis=1).astype(MXU_DTYPE)
        wo = wo_ref[...]
        x2 = x_ref[...] + jnp.dot(mixed, wo, preferred_element_type=F32)
        r2 = lax.rsqrt(jnp.mean(x2 * x2, axis=-1, keepdims=True) + EPS)
        fw_v = fw_ref[...]
        err = x2 * r2 * fw_v - tg_ref[...]
        loss_ref[...] += 0.5 * jnp.sum(jnp.sum(err * err, axis=-1, keepdims=True) * (1.0 / d))
        dy = err * (1.0 / d)
        gg = dy * fw_v
        dx2 = r2 * gg - x2 * ((r2 * r2 * r2) * jnp.mean(gg * x2, axis=-1, keepdims=True))
        gfw_ref[...] += jnp.sum(dy * x2 * r2, axis=0, keepdims=True)
        dx2_ref[...] = dx2
        dx2b = dx2.astype(MXU_DTYPE)
        d_mixed = lax.dot_general(dx2b, wo, _NT, preferred_element_type=F32)
        gwo_ref[...] += lax.dot_general(mixed, dx2b, _TN, preferred_element_type=F32)
        d_osb, d_zsb, gsb = head_backward(d_mixed[:, :half], osb, zsb, sbw_v, SB_HEAD_DIM, r_sb, n_sb, sg_sb)
        d_ogd, d_zgd, ggd = head_backward(d_mixed[:, half:], ogd, zgd, gdw_v, GDN_HEAD_DIM, r_gd, n_gd, sg_gd)
        for j in range(sb_blocks):
            dosb_ref[j] = d_osb[:, j * LANES:(j + 1) * LANES]
        dogd_ref[...] = d_ogd
        dz_ref[0] = d_zsb
        dz_ref[1] = d_zgd
        gsb_ref[...] += gsb
        ggd_ref[...] += ggd

    row_blk = lambda w: pl.BlockSpec((tm, w), lambda i: (i, 0))
    blocks_blk = pl.BlockSpec((sb_blocks, tm, LANES), lambda i: (0, i, 0))
    fixed = lambda r, w: pl.BlockSpec((r, w), lambda i: (0, 0))
    return pl.pallas_call(
        body, name="post",
        grid=(t_len // tm,),
        in_specs=[blocks_blk, row_blk(half),
                  pl.BlockSpec((tm, half), lambda i: (i, 0)),
                  pl.BlockSpec((tm, half), lambda i: (i, 1)),
                  row_blk(d), row_blk(d), fixed(d, d), fixed(1, half), fixed(1, half), fixed(1, d)],
        out_specs=[row_blk(d), blocks_blk, row_blk(half),
                   pl.BlockSpec((2, tm, half), lambda i: (DPROJ_GATE_SLOT // 2, i, 0)),
                   fixed(1, LANES), fixed(1, d), fixed(1, half), fixed(1, half), fixed(d, d)],
        out_shape=[jax.ShapeDtypeStruct((t_len, d), F32), jax.ShapeDtypeStruct((sb_blocks, t_len, LANES), F32),
                   jax.ShapeDtypeStruct((t_len, half), F32),
                   jax.ShapeDtypeStruct((len(DPROJ_PIECE_OF_SLOT), t_len, half), F32),
                     jax.ShapeDtypeStruct((1, LANES), F32), jax.ShapeDtypeStruct((1, d), F32),
                     jax.ShapeDtypeStruct((1, half), F32), jax.ShapeDtypeStruct((1, half), F32),
                     jax.ShapeDtypeStruct((d, d), F32)],
        compiler_params=_params("arbitrary"),
    )(o_sb, o_gd, proj_gates, proj_gates, x, target, w_out, sbw, gdw, fw)


def _piece_of_slot(s):
    return jnp.where(s < DPROJ_GDN_SLOT, s, jnp.where(s < DPROJ_GATE_SLOT, s + 1,
                                                     jnp.where(s == DPROJ_GATE_SLOT, 3, 7)))


def _gw_in_call(h_t, dproj8):
    d, t_len = h_t.shape
    n_piece, _, pw = dproj8.shape

    def body(ht_ref, dp_ref, gw_ref):
        gw_ref[...] = jnp.dot(ht_ref[...], dp_ref[0].astype(MXU_DTYPE), preferred_element_type=F32)

    return pl.pallas_call(
        body, name="gw_in",
        grid=(n_piece,),
        in_specs=[pl.BlockSpec((d, t_len), lambda s: (0, 0)),
                  pl.BlockSpec((1, t_len, pw), lambda s: (s, 0, 0))],
        out_specs=pl.BlockSpec((d, pw), lambda s: (0, _piece_of_slot(s))),
        out_shape=jax.ShapeDtypeStruct((d, n_piece * pw), F32),
        compiler_params=_params("arbitrary"),
    )(h_t, dproj8)


def _slot_of_piece(p):
    return jnp.where(p < DPROJ_GDN_SLOT, p, jnp.where(p == 3, DPROJ_GATE_SLOT, jnp.where(p < 7, p - 1, 7)))


def _gw_in_shards_call(h_t, dproj8, dsmall, out_dtype):
    d, t_len = h_t.shape
    n_piece, _, pw = dproj8.shape
    ns = dsmall.shape[1]
    n_pairs = N_DEV // 2

    def body(ht_ref, dp_ref, ds_ref, chip_ref, prev_ref, gates_ref, send_ref, recv_ref, send_sems, recv_sems):
        p = pl.program_id(0)
        x_pos, y_pos, c = lax.axis_index("x"), lax.axis_index("y"), lax.axis_index("c")
        to_sibling = lambda pair: pltpu.make_async_remote_copy(
            src_ref=send_ref.at[pair], dst_ref=recv_ref.at[pair], send_sem=send_sems.at[pair],
            recv_sem=recv_sems.at[pair], device_id=(x_pos, y_pos, 1 - c), device_id_type=_MESH)

        @pl.when(p == 0)
        def _():
            gates_ref[...] = jnp.dot(ht_ref[...], ds_ref[...].astype(MXU_DTYPE), preferred_element_type=F32)

        def emit(s, tail):
            x = jnp.concatenate([prev_ref[...], tail], axis=1)
            y = x if s == 0 else pltpu.roll(x, SHARD_PAD - s, axis=1)
            shard = y[:, :SHARD_COLS].astype(out_dtype)

            @pl.when(c == s % 2)
            def _():
                chip_ref[s // 2] = shard

            @pl.when(c != s % 2)
            def _():
                send_ref[s // 2] = shard
                to_sibling(s // 2).start()

        @pl.when(p < n_piece)
        def _():
            cur = jnp.dot(ht_ref[...], dp_ref[0].astype(MXU_DTYPE), preferred_element_type=F32)
            for s in range(n_piece - 1):
                pl.when(p == s + 1)(functools.partial(emit, s, cur[:, :SHARD_PAD - pw]))
            prev_ref[...] = cur

        @pl.when(p == n_piece)
        def _():
            emit(n_piece - 1, gates_ref[...])
            for pair in range(n_pairs):
                to_sibling(pair).wait_send()
            for pair in range(n_pairs):
                to_sibling(pair).wait_recv()
                chip_ref[pair] = (chip_ref[pair].astype(F32) + recv_ref[pair].astype(F32)).astype(out_dtype)

    shards_of_side = lambda: pltpu.VMEM((n_pairs, d, SHARD_COLS), out_dtype)
    return pl.pallas_call(
        body, name="gw_in",
        grid=(n_piece + 1,),
        in_specs=[pl.BlockSpec((d, t_len), lambda p: (0, 0)),
                  pl.BlockSpec((1, t_len, pw), lambda p: (_slot_of_piece(jnp.minimum(p, n_piece - 1)), 0, 0)),
                  pl.BlockSpec((t_len, ns), lambda p: (0, 0))],
        out_specs=pl.BlockSpec((n_pairs, d, SHARD_COLS), lambda p: (0, 0, 0)),
        out_shape=jax.ShapeDtypeStruct((n_pairs, d, SHARD_COLS), out_dtype),
        scratch_shapes=[pltpu.VMEM((d, pw), F32), pltpu.VMEM((d, ns), F32), shards_of_side(), shards_of_side(),
                        pltpu.SemaphoreType.DMA((n_pairs,)), pltpu.SemaphoreType.DMA((n_pairs,))],
        compiler_params=_params("arbitrary"),
    )(h_t, dproj8, dsmall)


def _gw_small_call(h_t, dsmall, tm=512):
    d, t_len = h_t.shape
    ns = dsmall.shape[1]

    def body(ht_ref, dp_ref, gw_ref):
        @pl.when(pl.program_id(0) == 0)
        def _():
            gw_ref[...] = jnp.zeros_like(gw_ref)

        gw_ref[...] += jnp.dot(ht_ref[...], dp_ref[...].astype(MXU_DTYPE), preferred_element_type=F32)

    return pl.pallas_call(
        body, name="gw_small",
        grid=(t_len // tm,),
        in_specs=[pl.BlockSpec((d, tm), lambda t: (0, t)),
                  pl.BlockSpec((tm, ns), lambda t: (t, 0))],
        out_specs=pl.BlockSpec((d, ns), lambda t: (0, 0)),
        out_shape=jax.ShapeDtypeStruct((d, ns), F32),
        compiler_params=_params("arbitrary"),
    )(h_t, dsmall)


def _dx_call(dproj8, dsmall, w_main, w_small, x, r, dx2, norm_w, chip_scatter=(), tm=256):
    t_len, d = x.shape
    n_piece, _, pw = dproj8.shape
    ns = dsmall.shape[1]
    nx = len(chip_scatter)
    steps = t_len // tm

    def body(*refs):
        dp_ref, ds_ref, wm_ref, ws_ref, x_ref, r_ref, dx2_ref, nw_ref = refs[:8]
        gx_ref, gnw_ref = refs[8 + nx:10 + nx]
        copies = lambda: _chip_copies(refs[8:8 + nx], refs[10 + nx:10 + 2 * nx], *refs[10 + 2 * nx:])
        if nx:
            pl.when(pl.program_id(0) == 0)(lambda: _start_all(copies()))

        @pl.when(pl.program_id(0) == 0)
        def _():
            gnw_ref[...] = jnp.zeros_like(gnw_ref)

        dh = lax.dot_general(ds_ref[...].astype(MXU_DTYPE), ws_ref[...], _NT, preferred_element_type=F32)
        for s, p in enumerate(DPROJ_PIECE_OF_SLOT):
            dh = dh + lax.dot_general(dp_ref[s].astype(MXU_DTYPE), wm_ref[:, p * pw:(p + 1) * pw], _NT,
                                      preferred_element_type=F32)
        xv, rv = x_ref[...], r_ref[...]
        dn = dh * nw_ref[...]
        gx_ref[...] = dx2_ref[...] + rv * dn - xv * ((rv * rv * rv) * jnp.mean(dn * xv, axis=-1, keepdims=True))
        gnw_ref[...] += jnp.sum(dh * xv * rv, axis=0, keepdims=True)
        if nx:
            pl.when(pl.program_id(0) == steps - 1)(lambda: _wait_all(copies()))

    return pl.pallas_call(
        body, name="dx",
        grid=(steps,),
        in_specs=[pl.BlockSpec((n_piece, tm, pw), lambda i: (0, i, 0)),
                  pl.BlockSpec((tm, ns), lambda i: (i, 0)),
                  pl.BlockSpec((d, n_piece * pw), lambda i: (0, 0)),
                  pl.BlockSpec((d, ns), lambda i: (0, 0)),
                  pl.BlockSpec((tm, d), lambda i: (i, 0)),
                  pl.BlockSpec((tm, 1), lambda i: (i, 0)),
                  pl.BlockSpec((tm, d), lambda i: (i, 0)),
                  pl.BlockSpec((1, d), lambda i: (0, 0))] + [_HBM] * nx,
        out_specs=[pl.BlockSpec((tm, d), lambda i: (i, 0)),
                   pl.BlockSpec((1, d), lambda i: (0, 0))] + [_HBM] * nx,
        out_shape=[jax.ShapeDtypeStruct((t_len, d), F32), jax.ShapeDtypeStruct((1, d), F32)]
                  + [jax.ShapeDtypeStruct(a.shape, a.dtype) for a in chip_scatter],
        scratch_shapes=_chip_semaphores(nx) if nx else [],
        compiler_params=_params("arbitrary"),
    )(dproj8, dsmall, w_main, w_small, x, r, dx2, norm_w, *chip_scatter)


def _exchange_call(name, srcs, per_peer):
    n = len(srcs)

    def body(*refs):
        src_refs, out_refs = refs[:n], refs[n:2 * n]
        copies = _direct_copies(src_refs, out_refs, *refs[2 * n:], per_peer)
        _start_all(copies)
        _wait_all(copies)

    hbm = pl.BlockSpec(memory_space=pl.ANY)
    return pl.pallas_call(
        body, name=name,
        in_specs=[hbm] * n, out_specs=[hbm] * n, out_shape=_direct_out_shapes(srcs, per_peer),
        scratch_shapes=_direct_semaphores(n),
    )(*srcs)


def _direct_out_shapes(srcs, per_peer):
    return [jax.ShapeDtypeStruct(s.shape if pp else (N_DEV,) + s.shape, s.dtype) for s, pp in zip(srcs, per_peer)]


def _direct_semaphores(n):
    return [pltpu.SemaphoreType.DMA((n * (N_DEV - 1),)), pltpu.SemaphoreType.DMA((n * (N_DEV - 1),)),
            pltpu.SemaphoreType.DMA((n,))]


def _direct_copies(src_refs, out_refs, send_sems, recv_sems, local_sems, per_peer):
    x, y, c = lax.axis_index("x"), lax.axis_index("y"), lax.axis_index("c")
    me = 4 * x + 2 * y + c
    local, remote = [], []
    for a in range(len(src_refs)):
        mine = src_refs[a].at[me] if per_peer[a] else src_refs[a]
        local.append(pltpu.make_async_copy(mine, out_refs[a].at[me], local_sems.at[a]))
    for k in range(1, N_DEV):
        kx, ky, kc = (k >> 2) & 1, (k >> 1) & 1, k & 1
        px = 1 - x if kx else x
        py = 1 - y if ky else y
        pc = 1 - c if kc else c
        peer = 4 * px + 2 * py + pc
        for a in range(len(src_refs)):
            sem = a * (N_DEV - 1) + (k - 1)
            remote.append(pltpu.make_async_remote_copy(
                src_ref=src_refs[a].at[peer] if per_peer[a] else src_refs[a], dst_ref=out_refs[a].at[me],
                send_sem=send_sems.at[sem], recv_sem=recv_sems.at[sem],
                device_id=(px, py, pc), device_id_type=pl.DeviceIdType.MESH))
    return local, remote


def _start_all(copies):
    local, remote = copies
    for cp in local + remote:
        cp.start()


def _wait_all(copies):
    local, remote = copies
    for cp in remote:
        cp.wait_send()
    for cp in remote:
        cp.wait_recv()
    for cp in local:
        cp.wait()


N_CHIPS = 4
_HBM = pl.BlockSpec(memory_space=pl.ANY)
_MESH = pl.DeviceIdType.MESH


def _gather_call(name, srcs):
    n = len(srcs)
    per = N_DEV - 1

    def body(*refs):
        src_refs, out_refs = refs[:n], refs[n:2 * n]
        send_sems, recv_sems, local_sems = refs[2 * n:]
        x, y, c = lax.axis_index("x"), lax.axis_index("y"), lax.axis_index("c")
        me, sibling = (x, y, c), (x, y, 1 - c)
        x_nbr, y_nbr, diagonal = (1 - x, y), (x, 1 - y), (1 - x, 1 - y)
        held = ((1 - x) * c + x * (1 - c), y * c + (1 - y) * (1 - c))
        onward = (x * c + (1 - x) * (1 - c), (1 - y) * c + y * (1 - c))
        slot = lambda px, py, pc: 4 * px + 2 * py + pc

        def copy(a, k, block, to, from_src=False):
            rows = out_refs[a].at[slot(*block)]
            return pltpu.make_async_remote_copy(
                src_ref=src_refs[a] if from_src else rows, dst_ref=rows,
                send_sem=send_sems.at[a * per + k], recv_sem=recv_sems.at[a * per + k],
                device_id=to, device_id_type=_MESH)

        local = [pltpu.make_async_copy(src_refs[a], out_refs[a].at[slot(*me)], local_sems.at[a]) for a in range(n)]
        started = []

        def start(cp):
            cp.start()
            started.append(cp)

        for cp in local:
            cp.start()
        for a in range(n):
            start(copy(a, 0, me, sibling, True))
            start(copy(a, 1, me, (*x_nbr, c), True))
            start(copy(a, 2, me, (*y_nbr, c), True))
        for a in range(n):
            copy(a, 1, (*x_nbr, c), me).wait_recv()
            copy(a, 2, (*y_nbr, c), me).wait_recv()
            start(copy(a, 3, (*held, c), (*onward, c)))
            start(copy(a, 4, (*x_nbr, c), sibling))
            start(copy(a, 5, (*y_nbr, c), sibling))
        for a in range(n):
            copy(a, 3, (*diagonal, c), me).wait_recv()
            start(copy(a, 6, (*diagonal, c), sibling))
        for a in range(n):
            copy(a, 0, sibling, me).wait_recv()
            for k, chip in ((4, x_nbr), (5, y_nbr), (6, diagonal)):
                copy(a, k, (*chip, 1 - c), me).wait_recv()
        for cp in started:
            cp.wait_send()
        for cp in local:
            cp.wait()

    return pl.pallas_call(
        body, name=name,
        in_specs=[_HBM] * n, out_specs=[_HBM] * n,
        out_shape=[jax.ShapeDtypeStruct((N_DEV,) + s.shape, s.dtype) for s in srcs],
        scratch_shapes=[pltpu.SemaphoreType.DMA((n * per,)), pltpu.SemaphoreType.DMA((n * per,)),
                        pltpu.SemaphoreType.DMA((n,))],
    )(*srcs)


def _chip_semaphores(n):
    per = N_CHIPS - 1
    return [pltpu.SemaphoreType.DMA((n * per,)), pltpu.SemaphoreType.DMA((n * per,)), pltpu.SemaphoreType.DMA((n,))]


def _chip_copies(src_refs, out_refs, send_sems, recv_sems, local_sems):
    per = N_CHIPS - 1
    x, y, c = lax.axis_index("x"), lax.axis_index("y"), lax.axis_index("c")
    mine = 2 * x + y
    chips = [(1 - x, y), (x, 1 - y), (1 - x, 1 - y)]
    n = len(src_refs)
    local = [pltpu.make_async_copy(src_refs[a].at[mine], out_refs[a].at[mine], local_sems.at[a]) for a in range(n)]
    remote = []
    for a in range(n):
        for j, (px, py) in enumerate(chips):
            remote.append(pltpu.make_async_remote_copy(
                src_ref=src_refs[a].at[2 * px + py], dst_ref=out_refs[a].at[mine],
                send_sem=send_sems.at[a * per + j], recv_sem=recv_sems.at[a * per + j],
                device_id=(px, py, c), device_id_type=_MESH))
    return local, remote


def _adam_call(name, parts, w, m, v, tr):
    rows, cols = w.shape
    n_slots = parts.shape[0]

    def body(p_ref, w_ref, m_ref, v_ref, g_ref, d_ref, nm_ref, nv_ref):
        g = p_ref[0].astype(F32)
        for s in range(1, n_slots):
            g = g + p_ref[s].astype(F32)
        m_new = ADAM_B1 * m_ref[...] + (1.0 - ADAM_B1) * g
        v_new = ADAM_B2 * v_ref[...] + (1.0 - ADAM_B2) * (g * g)
        m_hat = m_new / (1.0 - ADAM_B1 ** ADAM_STEP)
        v_hat = v_new / (1.0 - ADAM_B2 ** ADAM_STEP)
        g_ref[...] = g
        d_ref[...] = -ADAM_LR * (m_hat / (jnp.sqrt(v_hat) + ADAM_EPS) + ADAM_WD * w_ref[...])
        nm_ref[...] = m_new
        nv_ref[...] = v_new

    blk = pl.BlockSpec((tr, cols), lambda i: (i, 0))
    return pl.pallas_call(
        body, name=name,
        grid=(rows // tr,),
        in_specs=[pl.BlockSpec((n_slots, tr, cols), lambda i: (0, i, 0)), blk, blk, blk],
        out_specs=[blk] * 4,
        out_shape=[jax.ShapeDtypeStruct((rows, cols), F32)] * 4,
        compiler_params=_params("arbitrary"),
    )(parts, w, m, v)


def _adam_columns_call(name, parts, w_t, m_t, v_t):
    n_slots, rows, cols = parts.shape
    row_tiles = rows // LANES
    cols_pad = -(-cols // LANES) * LANES

    def body(p_ref, w_ref, m_ref, v_ref, *out_refs):
        for a in range(row_tiles):
            g = p_ref[0, a * LANES:(a + 1) * LANES, :].astype(F32)
            for s in range(1, n_slots):
                g = g + p_ref[s, a * LANES:(a + 1) * LANES, :].astype(F32)
            g = jnp.concatenate([g, jnp.zeros((LANES, cols_pad - cols), F32)], axis=1).T[:cols]
            column_rows = pl.ds(a, cols, stride=row_tiles)
            results = (g,) + _adamw(g, w_ref[column_rows, :], m_ref[column_rows, :], v_ref[column_rows, :])
            for out_ref, val in zip(out_refs, results):
                out_ref[column_rows, :] = val

    vm = pl.BlockSpec(memory_space=pltpu.VMEM)
    return pl.pallas_call(
        body, name=name,
        in_specs=[vm] * 4, out_specs=[vm] * 4,
        out_shape=[jax.ShapeDtypeStruct(w_t.shape, F32)] * 4,
        compiler_params=pltpu.CompilerParams(vmem_limit_bytes=VMEM_LIMIT_BYTES),
    )(parts, w_t, m_t, v_t)


N_PIECES = 8
PIECE = 512
SHARD_COLS = 513
SHARD_PAD = 640
RELAYOUT_ROWS = 256


def _from_shards_call(shards):
    _, d, _ = shards.shape
    tr = RELAYOUT_ROWS

    def body(p_ref, m_ref, s_ref):
        lane = lax.broadcasted_iota(jnp.int32, (tr, SHARD_PAD), 1)
        pad = jnp.zeros((tr, SHARD_PAD - SHARD_COLS), F32)
        sh = [jnp.concatenate([p_ref[s].astype(F32), pad], axis=1) for s in range(N_DEV)]
        for p in range(N_PIECES):
            y = sh[p] if p == 0 else pltpu.roll(sh[p], p, axis=1)
            if p > 0:
                y = jnp.where(lane < p, pltpu.roll(sh[p - 1], SHARD_PAD - (SHARD_COLS - p), axis=1), y)
            m_ref[:, p * PIECE:(p + 1) * PIECE] = y[:, :PIECE].astype(m_ref.dtype)
        first_gate = N_PIECES * PIECE - (N_DEV - 1) * SHARD_COLS
        s_ref[...] = pltpu.roll(sh[N_DEV - 1], SHARD_PAD - first_gate, axis=1)[:, :LANES].astype(s_ref.dtype)

    return pl.pallas_call(
        body, name="w_in_from_shards",
        grid=(d // tr,),
        in_specs=[pl.BlockSpec((N_DEV, tr, SHARD_COLS), lambda i: (0, i, 0))],
        out_specs=[pl.BlockSpec((tr, N_PIECES * PIECE), lambda i: (i, 0)), pl.BlockSpec((tr, LANES), lambda i: (i, 0))],
        out_shape=[jax.ShapeDtypeStruct((d, N_PIECES * PIECE), shards.dtype),
                   jax.ShapeDtypeStruct((d, LANES), shards.dtype)],
        compiler_params=_params("arbitrary"),
    )(shards)


def _adamw(g, w, m, v):
    m_new = ADAM_B1 * m + (1.0 - ADAM_B1) * g
    v_new = ADAM_B2 * v + (1.0 - ADAM_B2) * (g * g)
    m_hat = m_new / (1.0 - ADAM_B1 ** ADAM_STEP)
    v_hat = v_new / (1.0 - ADAM_B2 ** ADAM_STEP)
    return -ADAM_LR * (m_hat / (jnp.sqrt(v_hat) + ADAM_EPS) + ADAM_WD * w), m_new, v_new


def _adam_small_call(parts, ws, ms, vs):
    n = len(ws)
    n_slots = parts.shape[0]

    def body(*refs):
        p_ref = refs[0]
        w_refs, m_refs, v_refs = refs[1:1 + n], refs[1 + n:1 + 2 * n], refs[1 + 2 * n:1 + 3 * n]
        loss_ref = refs[1 + 3 * n]
        outs = refs[2 + 3 * n:]
        g_all = p_ref[0]
        for s in range(1, n_slots):
            g_all = g_all + p_ref[s]
        loss_ref[...] = g_all[n:n + 1, 0:1]
        for r in range(n):
            size = w_refs[r].shape[1]
            g = g_all[r:r + 1, :size]
            delta, m_new, v_new = _adamw(g, w_refs[r][...], m_refs[r][...], v_refs[r][...])
            for kind, val in enumerate((g, delta, m_new, v_new)):
                outs[kind * n + r][...] = val

    vm = pl.BlockSpec(memory_space=pltpu.VMEM)
    shapes = [jax.ShapeDtypeStruct(w.shape, F32) for w in ws]
    return pl.pallas_call(
        body, name="adam_small",
        in_specs=[vm] * (1 + 3 * n), out_specs=[vm] * (1 + 4 * n),
        out_shape=[jax.ShapeDtypeStruct((1, 1), F32)] + shapes * 4,
    )(parts, *ws, *ms, *vs)


_SMALL_ROWS = ("norm1_w", "final_norm_w", "sb_norm_w", "gdn_norm_w", "gdn_A_log", "gdn_dt_bias", "loss")


def _pack_small(vals, width):
    rows = [jnp.pad(a.reshape(1, -1).astype(F32), ((0, 0), (0, width - a.size))) for a in vals]
    rows += [jnp.zeros((1, width), F32)] * (8 - len(rows))
    return jnp.concatenate(rows, axis=0)


def _device_step(x2d, tgt, w_main, w_small, w_out_full, conv_full, norm1_w, sb_norm_w, gdn_A_log, gdn_dt_bias,
                 gdn_norm_w, final_norm_w, distributed=False):
    t_len, d = x2d.shape
    n_chunks = t_len // CHUNK
    w_main, w_small, w_out_full = (a.astype(MXU_DTYPE) for a in (w_main, w_small, w_out_full))
    w_small_t = w_small[:, :2 * GDN_HEADS].T

    pad_lanes = lambda a, lo: jnp.pad(a.reshape(1, -1), ((0, 0), (lo, LANES - lo - a.size)))
    alog_l, dtb_l = pad_lanes(gdn_A_log, GDN_HEADS), pad_lanes(gdn_dt_bias, GDN_HEADS)
    alog_c, dtb_c = alog_l[:, :8].T, dtb_l[:, :8].T
    sbw = jnp.tile(sb_norm_w, (1, 512 // SB_HEAD_DIM))
    gdw = jnp.tile(gdn_norm_w, (1, 512 // GDN_HEAD_DIM))
    fw = final_norm_w.reshape(1, d)

    if distributed:
        proj_cols, proj_gates, ps, pst, h_t, r1, w_out_g, conv_g = _inproj_call(
            x2d, norm1_w, w_main, w_small, w_small_t, gather=(w_out_full, conv_full))
        w_out_full = w_out_g.reshape(d, d)
        conv_full = conv_g.transpose(1, 0, 2).reshape(CONV_WIDTH, N_DEV * conv_g.shape[2])
    else:
        proj_cols, proj_gates, ps, pst, h_t, r1 = _inproj_call(x2d, norm1_w, w_main, w_small, w_small_t)
    o_sb, sp_total, sb_blocks_run = _sb_fwd_call(proj_cols, t_len)
    gact = _gdn_prep_call(proj_cols, conv_full, t_len, after=sb_blocks_run)
    beta_l, gcol_l, grow = _gdn_gates_call(ps, pst, alog_l, dtb_l, alog_c, dtb_c, t_len)
    gam_r = grow[GDN_HEADS:2 * GDN_HEADS].reshape(GDN_HEADS, n_chunks, 1, CHUNK)
    o_gd, *gdn_saved = _gdn_fwd_call(gact, beta_l, gcol_l, gam_r, t_len)

    (dx2, d_osb, d_ogd, dproj8, loss_p, g_fw, g_sbw, g_gdw, g_wout) = _post_call(
        o_sb, o_gd, proj_gates, x2d, tgt, w_out_full, sbw, gdw, fw)

    dproj8 = _sb_bwd_call(proj_cols, sp_total, sb_blocks_run, d_osb, dproj8, t_len)
    if distributed:
        d_gact3, d_gates, g_wout = _gdn_bwd_call(gact, beta_l, gcol_l, gam_r, gdn_saved, d_ogd, t_len,
                                                 scatter=(g_wout.reshape(N_DEV, d // N_DEV, d),))
    else:
        d_gact3, d_gates = _gdn_bwd_call(gact, beta_l, gcol_l, gam_r, gdn_saved, d_ogd, t_len)
    dproj8, g_conv = _gdn_prep_bwd_call(proj_cols, conv_full, d_gact3, dproj8, t_len)
    dsmall, g_alog, g_dtb = _gdn_gates_bwd_call(ps, alog_l, dtb_l, d_gates, t_len)

    if distributed:
        chip_partials = _gw_in_shards_call(h_t, dproj8, dsmall, WIRE_DTYPE)
        grad_x, g_n1, g_w_in = _dx_call(dproj8, dsmall, w_main, w_small, x2d, r1, dx2, norm1_w,
                                        chip_scatter=(chip_partials,))
    else:
        grad_x, g_n1 = _dx_call(dproj8, dsmall, w_main, w_small, x2d, r1, dx2, norm1_w)
        g_w_in = (_gw_in_call(h_t, dproj8), _gw_small_call(h_t, dsmall))
    return (loss_p, grad_x, g_n1, g_w_in, g_sbw, g_conv, g_alog, g_dtb, g_gdw, g_wout, g_fw)


def kernel(x, norm1_w, w_in, sb_norm_w, gdn_conv_w, gdn_A_log, gdn_dt_bias, gdn_norm_w, w_out, final_norm_w, loss_target, m_norm1_w, m_w_in, m_sb_norm_w, m_gdn_conv_w, m_gdn_A_log, m_gdn_dt_bias, m_gdn_norm_w, m_w_out, m_final_norm_w, v_norm1_w, v_w_in, v_sb_norm_w, v_gdn_conv_w, v_gdn_A_log, v_gdn_dt_bias, v_gdn_norm_w, v_w_out, v_final_norm_w):
    d = x.shape[2]
    shard_cols = w_in.shape[2]
    conv_cols = gdn_conv_w.shape[2]

    (w_in_g,) = _gather_call("gather_weights", [w_in[0].astype(WIRE_DTYPE)])
    w_main, w_small = _from_shards_call(w_in_g)

    (loss_p, grad_x, g_n1, p_w_in, g_sbw, g_conv, g_alog, g_dtb, g_gdw, p_wout, g_fw) = _device_step(
        x[0], loss_target[0], w_main, w_small, w_out[0].astype(WIRE_DTYPE), gdn_conv_w[0], norm1_w, sb_norm_w,
        gdn_A_log, gdn_dt_bias, gdn_norm_w, final_norm_w, distributed=True)

    g_conv_parts = g_conv.reshape(CONV_WIDTH, N_DEV, conv_cols).transpose(1, 0, 2)
    fold = lambda a, group: a.reshape(-1, group).sum(axis=0)
    small_g = _pack_small([g_n1, g_fw, fold(g_sbw, SB_HEAD_DIM), fold(g_gdw, GDN_HEAD_DIM),
                           g_alog[0, GDN_HEADS:2 * GDN_HEADS], g_dtb[0, GDN_HEADS:2 * GDN_HEADS],
                           loss_p[0, :1]], d)
    p_small, p_conv = _exchange_call("exchange_small", [small_g, g_conv_parts], [False, True])

    columns = lambda a: a.transpose(2, 0, 1).reshape(shard_cols * d // LANES, LANES)
    from_columns = lambda a: a.reshape(shard_cols, d // LANES, LANES).transpose(1, 2, 0).reshape(1, d, shard_cols)
    r_w_in = [from_columns(a) for a in _adam_columns_call("adam_w_in", p_w_in, columns(w_in), columns(m_w_in),
                                                          columns(v_w_in))]
    r_wout = _adam_call("adam_w_out", p_wout, w_out[0], m_w_out[0], v_w_out[0], d // N_DEV)
    r_conv = _adam_call("adam_conv", p_conv, gdn_conv_w[0], m_gdn_conv_w[0], v_gdn_conv_w[0], CONV_WIDTH)

    row = lambda a: a.reshape(1, -1)
    n_small = len(_SMALL_ROWS) - 1
    r_small = _adam_small_call(
        p_small,
        [norm1_w, row(final_norm_w), sb_norm_w, gdn_norm_w, gdn_A_log, gdn_dt_bias],
        [m_norm1_w, row(m_final_norm_w), m_sb_norm_w, m_gdn_norm_w, m_gdn_A_log, m_gdn_dt_bias],
        [v_norm1_w, row(v_final_norm_w), v_sb_norm_w, v_gdn_norm_w, v_gdn_A_log, v_gdn_dt_bias])

    def small_out(kind, name):
        out = r_small[1 + kind * n_small + _SMALL_ROWS.index(name)]
        return out.reshape(final_norm_w.shape) if name == "final_norm_w" else out

    def outputs(kind):
        return (small_out(kind, "norm1_w"), r_w_in[kind], small_out(kind, "sb_norm_w"), r_conv[kind][None],
                small_out(kind, "gdn_A_log"), small_out(kind, "gdn_dt_bias"), small_out(kind, "gdn_norm_w"),
                r_wout[kind][None], small_out(kind, "final_norm_w"))

    return (r_small[0][0, 0], grad_x[None], *outputs(0), *outputs(1), *outputs(2), *outputs(3))
```

```python
import functools

import jax
import jax.numpy as jnp
from jax import lax
from jax.experimental import pallas as pl
from jax.experimental.pallas import tpu as pltpu

F32 = jnp.float32
MXU_DTYPE = jnp.bfloat16
WIRE_DTYPE = jnp.bfloat16
EXACT = lax.Precision.HIGHEST
EPS = 1e-6
N_DEV = 8
SB_HEAD_DIM = 64
GDN_HEAD_DIM = 128
GDN_HEADS = 4
GDN_CHUNKS_PER_STEP = 4
GDN_BWD_GROUP = 1
CHUNK = 64
CONV_WIDTH = 4
LANES = 128
SB_BLOCK = 128
SB_BQ = 256
VMEM_LIMIT_BYTES = 56 * 1024 * 1024

PIECE_COLS = 512
PROJ_PIECE_KINDS = ("heads", "heads", "heads", "gate", "heads", "heads", "heads", "gate")
SB_FIRST_BLOCK, GDN_FIRST_BLOCK = 0, 12

DPROJ_PIECE_OF_SLOT = (0, 1, 2, 4, 5, 6, 3, 7)
DPROJ_SB_SLOT, DPROJ_GDN_SLOT, DPROJ_GATE_SLOT = 0, 3, 6

ADAM_LR = 0.001
ADAM_B1 = 0.9
ADAM_B2 = 0.999
ADAM_EPS = 1e-08
ADAM_WD = 0.01
ADAM_STEP = 10

_NN = (((1,), (0,)), ((), ()))
_NT = (((1,), (1,)), ((), ()))
_TN = (((0,), (0,)), ((), ()))
_BNN = (((2,), (1,)), ((0,), (0,)))
_BNT = (((2,), (2,)), ((0,), (0,)))
_BTN = (((1,), (1,)), ((0,), (0,)))


def _mx(a, b):
    return jnp.dot(a, b, precision=EXACT, preferred_element_type=F32)


def _split(x):
    hi = x.astype(MXU_DTYPE)
    return hi, (x - hi.astype(F32)).astype(MXU_DTYPE)


def _m3_general(a, b, dims):
    ah, al = _split(a)
    bh, bl = _split(b)
    dot = lambda x, y: lax.dot_general(x, y, dims, preferred_element_type=F32)
    (contract, _), (batch, _) = dims
    free = [ax for ax in range(a.ndim) if ax not in contract and ax not in batch][0]
    m = a.shape[free]
    both = dot(jnp.concatenate([ah, al], axis=free), bh)
    out_axis = len(batch)
    hi_part = lax.slice_in_dim(both, 0, m, axis=out_axis)
    lo_part = lax.slice_in_dim(both, m, 2 * m, axis=out_axis)
    return hi_part + (dot(ah, bl) + lo_part)


def _times_exact(a, b_exact, dims):
    ah, al = _split(a)
    (contract, _), (batch, _) = dims
    free = [ax for ax in range(a.ndim) if ax not in contract and ax not in batch][0]
    m = a.shape[free]
    both = lax.dot_general(jnp.concatenate([ah, al], axis=free), b_exact.astype(MXU_DTYPE), dims,
                           preferred_element_type=F32)
    out_axis = len(batch)
    return lax.slice_in_dim(both, 0, m, axis=out_axis) + lax.slice_in_dim(both, m, 2 * m, axis=out_axis)


def _exact_times(a_exact, b, dims):
    bh, bl = _split(b)
    n = b.shape[-1]
    both = lax.dot_general(a_exact.astype(MXU_DTYPE), jnp.concatenate([bh, bl], axis=-1), dims,
                           preferred_element_type=F32)
    return both[..., :n] + both[..., n:]


def _sigmoid(z):
    return 1.0 / (1.0 + jnp.exp(-z))


def _softplus(z):
    return jnp.maximum(z, 0.0) + jnp.log(1.0 + jnp.exp(-jnp.abs(z)))


def _params(*semantics):
    return pltpu.CompilerParams(dimension_semantics=semantics, vmem_limit_bytes=VMEM_LIMIT_BYTES)


def _inproj_call(x, norm_w, w_main, w_small, w_small_t, gather=(), tm=256):
    t_len, d = x.shape
    n = w_main.shape[1]
    ns = w_small.shape[1]
    nst = w_small_t.shape[0]
    ng = len(gather)
    steps = t_len // tm

    blocks_per_piece = PIECE_COLS // LANES
    n_gate_cols = PIECE_COLS * PROJ_PIECE_KINDS.count("gate")
    n_col_blocks = blocks_per_piece * PROJ_PIECE_KINDS.count("heads")

    def body(*refs):
        x_ref, nw_ref, wm_ref, ws_ref, wst_ref = refs[:5]
        cols_ref, pz_ref, ps_ref, pst_ref, ht_ref, r_ref = refs[5 + ng:11 + ng]
        copies = lambda: _direct_copies(refs[5:5 + ng], refs[11 + ng:11 + 2 * ng], *refs[11 + 2 * ng:], (False,) * ng)
        if ng:
            pl.when(pl.program_id(0) == 0)(lambda: _start_all(copies()))
        xv = x_ref[...]
        r = lax.rsqrt(jnp.mean(xv * xv, axis=-1, keepdims=True) + EPS)
        h = xv * r * nw_ref[...]
        hb = h.astype(MXU_DTYPE)
        n_block = n_gate = 0
        for piece, kind in enumerate(PROJ_PIECE_KINDS):
            out = jnp.dot(hb, wm_ref[:, piece * PIECE_COLS:(piece + 1) * PIECE_COLS], preferred_element_type=F32)
            if kind == "gate":
                pz_ref[:, n_gate * PIECE_COLS:(n_gate + 1) * PIECE_COLS] = out
                n_gate += 1
            else:
                for j in range(blocks_per_piece):
                    cols_ref[n_block + j] = out[:, j * LANES:(j + 1) * LANES]
                n_block += blocks_per_piece
        ps_ref[...] = jnp.dot(hb, ws_ref[...], preferred_element_type=F32)
        pst_ref[...] = lax.dot_general(wst_ref[...], hb, _NT, preferred_element_type=F32)
        ht_ref[...] = h.T.astype(MXU_DTYPE)
        r_ref[...] = r
        if ng:
            pl.when(pl.program_id(0) == steps - 1)(lambda: _wait_all(copies()))

    return pl.pallas_call(
        body, name="inproj",
        grid=(steps,),
        in_specs=[pl.BlockSpec((tm, d), lambda i: (i, 0)),
                  pl.BlockSpec((1, d), lambda i: (0, 0)),
                  pl.BlockSpec((d, n), lambda i: (0, 0)),
                  pl.BlockSpec((d, ns), lambda i: (0, 0)),
                  pl.BlockSpec((nst, d), lambda i: (0, 0))] + [_HBM] * ng,
        out_specs=[pl.BlockSpec((n_col_blocks, tm, LANES), lambda i: (0, i, 0)),
                   pl.BlockSpec((tm, n_gate_cols), lambda i: (i, 0)),
                   pl.BlockSpec((tm, ns), lambda i: (i, 0)),
                   pl.BlockSpec((nst, tm), lambda i: (0, i)),
                   pl.BlockSpec((d, tm), lambda i: (0, i)),
                   pl.BlockSpec((tm, 1), lambda i: (i, 0))] + [_HBM] * ng,
        out_shape=[jax.ShapeDtypeStruct((n_col_blocks, t_len, LANES), F32),
                   jax.ShapeDtypeStruct((t_len, n_gate_cols), F32),
                   jax.ShapeDtypeStruct((t_len, ns), F32),
                   jax.ShapeDtypeStruct((nst, t_len), F32),
                   jax.ShapeDtypeStruct((d, t_len), MXU_DTYPE),
                   jax.ShapeDtypeStruct((t_len, 1), F32)] + _direct_out_shapes(gather, (False,) * ng),
        scratch_shapes=_direct_semaphores(ng) if ng else [],
        compiler_params=_params("arbitrary"),
    )(x, norm_w, w_main, w_small, w_small_t, *gather)


def _running_sum_mm(x, tri):
    hi = x.astype(MXU_DTYPE)
    lo = (x - hi.astype(F32)).astype(MXU_DTYPE)
    return jnp.dot(hi, tri, preferred_element_type=F32) + jnp.dot(lo, tri, preferred_element_type=F32)


def _col_block(t_len, first):
    return pl.BlockSpec((1, t_len, LANES), lambda p: (first + p, 0, 0))


def _sb_iotas():
    row_i = lax.broadcasted_iota(jnp.int32, (SB_BQ, SB_BLOCK), 0)
    col_i = lax.broadcasted_iota(jnp.int32, (SB_BQ, SB_BLOCK), 1)
    sq_r = lax.broadcasted_iota(jnp.int32, (SB_BLOCK, SB_BLOCK), 0)
    sq_c = lax.broadcasted_iota(jnp.int32, (SB_BLOCK, SB_BLOCK), 1)
    return row_i, col_i, sq_r, sq_c


SB_DIAG_BLOCKS = SB_BQ // SB_BLOCK
SB_EXP_FLOOR = -110.0


def _sb_keys_descending(qi, tile, carry, z_bounds, n_heads, has_free):
    group = SB_DIAG_BLOCKS
    n_free = group * qi
    diag = list(range(group - 1, -1, -1))
    carry = tile([n_free + j for j in diag], [True] * group, carry, [j * SB_BLOCK for j in diag])

    def largest_exponent(c):
        worst = jnp.max(z_bounds[0] - c[1])
        for h in range(1, n_heads):
            worst = jnp.maximum(worst, jnp.max(z_bounds[h] - c[1 + h]))
        return worst

    always = group if has_free else 0

    def cond(state):
        return (state[0] < n_free) & ((state[1] > SB_EXP_FLOOR) | (state[0] < always))

    def body(state):
        first = n_free - 1 - state[0]
        c = tile([first - j for j in range(group)], [False] * group, state[2:])
        return (state[0] + group, largest_exponent(c), *c)

    out = lax.while_loop(cond, body, (jnp.int32(0), largest_exponent(carry), *carry))
    return out[2:], out[0]


def _sb_keys_ascending(qi, n_run, tile, carry, has_free):
    group = SB_DIAG_BLOCKS
    n_free = group * qi
    diag = list(range(group))
    kjs, los, masked = [n_free + j for j in diag], [j * SB_BLOCK for j in diag], [True] * group
    if has_free:
        early = lambda s: [n_free - n_run + group * s + j for j in range(group)]
        carry = lax.fori_loop(0, n_run // group - 1, lambda s, c: tile(early(s), [False] * group, c), carry)
        kjs, los, masked = [n_free - group + j for j in range(group)] + kjs, [0] * group + los, [False] * group + masked
    return tile(kjs, masked, carry, los)


def _sb_fwd_call(cols, t_len):
    nq = t_len // SB_BQ
    scale = float(SB_HEAD_DIM) ** -0.5
    n_pairs = 512 // LANES
    per_pair = LANES // SB_HEAD_DIM

    def body(q_blk, k_blk, v_blk, o_blk, st_ref, nrun_ref):
        q_ref, k_ref, v_ref, o_ref = q_blk.at[0], k_blk.at[0], v_blk.at[0], o_blk.at[0]
        lane = lax.broadcasted_iota(jnp.int32, (1, LANES), 1)
        row_i, col_i, sq_r, sq_c = _sb_iotas()
        ge = (sq_r >= sq_c).astype(MXU_DTYPE)
        hms = [((lane // SB_HEAD_DIM) == hh).astype(F32) for hh in range(per_pair)]
        k_sq = k_ref[...] * k_ref[...]
        k_norms = [jnp.sqrt(jnp.max(jnp.sum(k_sq * hm, axis=-1, keepdims=True))) * (1.02 * scale) for hm in hms]

        def q_block(qi, has_free):
            r0 = qi * SB_BQ if isinstance(qi, int) else pl.multiple_of(qi * SB_BQ, SB_BQ)
            rows = pl.ds(r0, SB_BQ)
            q_all = q_ref[rows, :]
            qms = [(q_all * (hm * scale)).astype(MXU_DTYPE) for hm in hms]
            z_bounds = [jnp.sqrt(jnp.sum(q_all * q_all * hm, axis=-1, keepdims=True)) * kn
                        for hm, kn in zip(hms, k_norms)]

            def tile(kjs, masked, kc, los=None):
                heads = range(per_pair)
                los = los or [0] * len(kjs)
                pairs = [(t, h) for t in range(len(kjs)) for h in heads]
                add_rows = lambda full, lo, part: full + part if lo == 0 else jnp.concatenate(
                    [full[:lo], full[lo:] + part], axis=0)
                acc, cs = kc[0], list(kc[1:])
                s0s = [kj * SB_BLOCK if isinstance(kj, int) else pl.multiple_of(kj * SB_BLOCK, SB_BLOCK) for kj in kjs]
                kbs = [k_ref[pl.ds(s0, SB_BLOCK), :].astype(MXU_DTYPE) for s0 in s0s]
                v_alls = [v_ref[pl.ds(s0, SB_BLOCK), :] for s0 in s0s]
                vms = {(t, h): (v_alls[t] * hms[h]).astype(MXU_DTYPE) for t, h in pairs}
                zs = {(t, h): lax.dot_general(qms[h][los[t]:], kbs[t], _NT, preferred_element_type=F32)
                      for t, h in pairs}
                masks = [(col_i[lo:] + s0) < (row_i[lo:] + r0) if m else None for m, lo, s0 in zip(masked, los, s0s)]
                keep = lambda t, a: a if masks[t] is None else jnp.where(masks[t], a, 0.0)
                sps = {(t, h): keep(t, _softplus(zs[t, h])) for t, h in pairs}
                sums = {p: _running_sum_mm(sps[p], ge) for p in pairs}
                mass = {}
                for t, h in pairs:
                    mass[t, h] = cs[h] if t == 0 else add_rows(
                        mass[t - 1, h], los[t - 1], jnp.sum(sps[t - 1, h], axis=-1, keepdims=True))
                ws = {(t, h): keep(t, jnp.exp(zs[t, h] - (sums[t, h] + mass[t, h][los[t]:]))) for t, h in pairs}
                for t, h in pairs:
                    acc = add_rows(acc, los[t], jnp.dot(ws[t, h].astype(MXU_DTYPE), vms[t, h],
                                                        preferred_element_type=F32))
                last = len(kjs) - 1
                cs = [add_rows(mass[last, h], los[last], jnp.sum(sps[last, h], axis=-1, keepdims=True)) for h in heads]
                return (acc, *cs)

            zero_col = jnp.zeros((SB_BQ, 1), F32)
            out, n_run = _sb_keys_descending(
                qi, tile, (jnp.zeros((SB_BQ, LANES), F32),) + (zero_col,) * per_pair, z_bounds, per_pair, has_free)
            o_ref[rows, :] = out[0]
            masses = jnp.zeros((SB_BQ, LANES), F32)
            for hh in range(per_pair):
                masses = jnp.where(lane == hh, out[1 + hh], masses)
            st_ref[rows, :] = masses
            nrun_ref[pl.program_id(0), qi] = n_run

        q_block(0, False)
        lax.fori_loop(1, nq, lambda qi, carry: (q_block(qi, True), carry)[1], 0)

    return pl.pallas_call(
        body, name="sb_fwd",
        grid=(n_pairs,),
        in_specs=[_col_block(t_len, SB_FIRST_BLOCK), _col_block(t_len, SB_FIRST_BLOCK + n_pairs),
                  _col_block(t_len, SB_FIRST_BLOCK + 2 * n_pairs)],
        out_specs=[_col_block(t_len, 0),
                   pl.BlockSpec((t_len, LANES), lambda p: (0, p)),
                   pl.BlockSpec(memory_space=pltpu.SMEM)],
        out_shape=[jax.ShapeDtypeStruct((n_pairs, t_len, LANES), F32),
                   jax.ShapeDtypeStruct((t_len, n_pairs * LANES), F32),
                   jax.ShapeDtypeStruct((n_pairs, nq), jnp.int32)],
        compiler_params=_params("arbitrary"),
    )(cols, cols, cols)


def _sb_bwd_call(cols, sp_total, n_run_all, d_o, dproj, t_len):
    nq = t_len // SB_BQ
    scale = float(SB_HEAD_DIM) ** -0.5
    n_pairs = 512 // LANES
    per_pair = LANES // SB_HEAD_DIM

    def body(q_blk, k_blk, v_blk, st_ref, nrun_ref, do_blk, dproj_in_ref, d_ref):
        q_ref, k_ref, v_ref, do_ref = q_blk.at[0], k_blk.at[0], v_blk.at[0], do_blk.at[0]
        lane = lax.broadcasted_iota(jnp.int32, (1, LANES), 1)
        row_i, col_i, sq_r, sq_c = _sb_iotas()
        lt = (sq_r < sq_c).astype(MXU_DTYPE)
        le = (sq_r <= sq_c).astype(MXU_DTYPE)
        hms = [((lane // SB_HEAD_DIM) == hh).astype(F32) for hh in range(per_pair)]
        d_ref[1] = jnp.zeros((t_len, LANES), F32)
        d_ref[2] = jnp.zeros((t_len, LANES), F32)

        def q_block(qi, has_free):
            r0 = qi * SB_BQ if isinstance(qi, int) else pl.multiple_of(qi * SB_BQ, SB_BQ)
            rows = pl.ds(r0, SB_BQ)
            q_all, do_all = q_ref[rows, :], do_ref[rows, :]
            qms = [(q_all * (hm * scale)).astype(MXU_DTYPE) for hm in hms]
            doms = [(do_all * hm).astype(MXU_DTYPE) for hm in hms]
            masses = st_ref[rows, :]
            totals = [jnp.sum(jnp.where(lane == hh, masses, 0.0), axis=-1, keepdims=True) for hh in range(per_pair)]

            def tile(kjs, masked, kc, los=None):
                heads = range(per_pair)
                los = los or [0] * len(kjs)
                pairs = [(t, h) for t in range(len(kjs)) for h in heads]
                add_rows = lambda full, lo, part: full + part if lo == 0 else jnp.concatenate(
                    [full[:lo], full[lo:] + part], axis=0)
                rsum = lambda a: jnp.sum(a, axis=-1, keepdims=True)
                dq, cls, gls = kc[0], list(kc[1:1 + per_pair]), list(kc[1 + per_pair:])
                s0s = [kj * SB_BLOCK if isinstance(kj, int) else pl.multiple_of(kj * SB_BLOCK, SB_BLOCK) for kj in kjs]
                k_alls = [k_ref[pl.ds(s0, SB_BLOCK), :] for s0 in s0s]
                v_alls = [v_ref[pl.ds(s0, SB_BLOCK), :] for s0 in s0s]
                kbs = [k_all.astype(MXU_DTYPE) for k_all in k_alls]
                vms = {(t, h): (v_alls[t] * hms[h]).astype(MXU_DTYPE) for t, h in pairs}
                kms = {(t, h): (k_alls[t] * (hms[h] * scale)).astype(MXU_DTYPE) for t, h in pairs}
                q_live = {(t, h): qms[h][los[t]:] for t, h in pairs}
                do_live = {(t, h): doms[h][los[t]:] for t, h in pairs}
                zs = {p: lax.dot_general(q_live[p], kbs[p[0]], _NT, preferred_element_type=F32) for p in pairs}
                das = {p: lax.dot_general(do_live[p], vms[p], _NT, preferred_element_type=F32) for p in pairs}
                masks = [(col_i[lo:] + s0) < (row_i[lo:] + r0) if m else None for m, lo, s0 in zip(masked, los, s0s)]
                keep = lambda t, a: a if masks[t] is None else jnp.where(masks[t], a, 0.0)
                sp_alls = {p: _softplus(zs[p]) for p in pairs}
                sps = {(t, h): keep(t, sp_alls[t, h]) for t, h in pairs}
                lefts = {p: _running_sum_mm(sps[p], lt) for p in pairs}
                cl = {}
                for t, h in pairs:
                    cl[t, h] = cls[h] if t == 0 else add_rows(cl[t - 1, h], los[t - 1], rsum(sps[t - 1, h]))
                ws = {(t, h): keep(t, jnp.exp(zs[t, h] - ((totals[h] - cl[t, h])[los[t]:] - lefts[t, h])))
                      for t, h in pairs}
                gs = {p: das[p] * ws[p] for p in pairs}
                g_sums = {p: _running_sum_mm(gs[p], le) for p in pairs}
                gl = {}
                for t, h in pairs:
                    gl[t, h] = gls[h] if t == 0 else add_rows(gl[t - 1, h], los[t - 1], rsum(gs[t - 1, h]))
                dzs = {(t, h): keep(t, gs[t, h] - jnp.exp(zs[t, h] - sp_alls[t, h]) * (gl[t, h][los[t]:] + g_sums[t, h])
                               ).astype(MXU_DTYPE) for t, h in pairs}
                for t in range(len(kjs)):
                    dk_t = jnp.zeros((SB_BLOCK, LANES), F32)
                    dv_t = jnp.zeros((SB_BLOCK, LANES), F32)
                    for h in heads:
                        dq = add_rows(dq, los[t], jnp.dot(dzs[t, h], kms[t, h], preferred_element_type=F32))
                        dk_t = dk_t + lax.dot_general(dzs[t, h], q_live[t, h], _TN, preferred_element_type=F32)
                        dv_t = dv_t + lax.dot_general(ws[t, h].astype(MXU_DTYPE), do_live[t, h], _TN,
                                                      preferred_element_type=F32)
                    d_ref[1, pl.ds(s0s[t], SB_BLOCK), :] += dk_t
                    d_ref[2, pl.ds(s0s[t], SB_BLOCK), :] += dv_t
                last = len(kjs) - 1
                cls = [add_rows(cl[last, h], los[last], rsum(sps[last, h])) for h in heads]
                gls = [add_rows(gl[last, h], los[last], rsum(gs[last, h])) for h in heads]
                return (dq, *cls, *gls)

            zero_col = jnp.zeros((SB_BQ, 1), F32)
            out = _sb_keys_ascending(qi, nrun_ref[pl.program_id(0), qi], tile,
                                     (jnp.zeros((SB_BQ, LANES), F32),) + (zero_col,) * (2 * per_pair), has_free)
            d_ref[0, rows, :] = out[0]

        q_block(0, False)
        lax.fori_loop(1, nq, lambda qi, carry: (q_block(qi, True), carry)[1], 0)

    return pl.pallas_call(
        body, name="sb_bwd",
        grid=(n_pairs,),
        in_specs=[_col_block(t_len, SB_FIRST_BLOCK), _col_block(t_len, SB_FIRST_BLOCK + n_pairs),
                  _col_block(t_len, SB_FIRST_BLOCK + 2 * n_pairs),
                  pl.BlockSpec((t_len, LANES), lambda p: (0, p)),
                  pl.BlockSpec(memory_space=pltpu.SMEM), _col_block(t_len, 0), _HBM],
        out_specs=pl.BlockSpec((3, t_len, LANES), lambda p: (DPROJ_SB_SLOT // 3, 0, p)),
        out_shape=jax.ShapeDtypeStruct(dproj.shape, dproj.dtype),
        input_output_aliases={6: 0},
        compiler_params=_params("arbitrary"),
    )(cols, cols, cols, sp_total, n_run_all, d_o, dproj)


def _conv_taps(xin, rows, t_len):
    taps = []
    for i in range(CONV_WIDTH):
        shift = CONV_WIDTH - 1 - i
        if shift == 0:
            taps.append(xin)
        else:
            taps.append(jnp.where(rows >= shift, pltpu.roll(xin, shift, axis=0), 0.0))
    return taps


def _gdn_prep_body_common(x_ref, w_ref, t_len):
    j = pl.program_id(0)
    xin = x_ref[...]
    rows = lax.broadcasted_iota(jnp.int32, (t_len, LANES), 0)
    taps = _conv_taps(xin, rows, t_len)
    pre = taps[0] * w_ref[0:1, :]
    for i in range(1, CONV_WIDTH):
        pre = pre + taps[i] * w_ref[i:i + 1, :]
    sg = _sigmoid(pre)
    act = pre * sg
    is_qk = j < 2 * GDN_HEADS
    nrm = jnp.where(is_qk, lax.rsqrt(jnp.sum(act * act, axis=-1, keepdims=True) + EPS), 1.0)
    sc = jnp.where(j < GDN_HEADS, float(GDN_HEAD_DIM) ** -0.5, 1.0)
    return j, rows, taps, pre, sg, act, is_qk, nrm, sc


def _gdn_prep_call(cols, conv_w, t_len, after):
    def body(x_blk, w_ref, after_ref, out_ref):
        _, _, _, _, _, act, _, nrm, sc = _gdn_prep_body_common(x_blk.at[0], w_ref, t_len)
        out_ref[...] = act * nrm * sc

    return pl.pallas_call(
        body, name="gdn_prep",
        grid=(3 * GDN_HEADS,),
        in_specs=[_col_block(t_len, GDN_FIRST_BLOCK),
                  pl.BlockSpec((CONV_WIDTH, LANES), lambda j: (0, j)),
                  pl.BlockSpec(memory_space=pltpu.SMEM)],
        out_specs=pl.BlockSpec((t_len, LANES), lambda j: (0, j)),
        out_shape=jax.ShapeDtypeStruct((t_len, 3 * 512), F32),
        compiler_params=_params("arbitrary"),
    )(cols, conv_w, after)


def _gdn_prep_bwd_call(cols, conv_w, d_act3, dproj, t_len):
    def body(x_blk, w_ref, d_ref, dproj_in_ref, dx_ref, dw_ref):
        _, rows, taps, pre, sg, act, is_qk, nrm, sc = _gdn_prep_body_common(x_blk.at[0], w_ref, t_len)
        d_out = d_ref[0]
        dn = d_out * sc
        d_norm = nrm * dn - act * (nrm * nrm * nrm) * jnp.sum(dn * act, axis=-1, keepdims=True)
        d_act = jnp.where(is_qk, d_norm, d_out)
        d_pre = d_act * sg * (1.0 + pre * (1.0 - sg))
        dx = d_pre * w_ref[CONV_WIDTH - 1:CONV_WIDTH, :]
        dw_ref[CONV_WIDTH - 1:CONV_WIDTH, :] = jnp.sum(d_pre * taps[CONV_WIDTH - 1], axis=0, keepdims=True)
        for i in range(CONV_WIDTH - 1):
            shift = CONV_WIDTH - 1 - i
            up = jnp.where(rows < t_len - shift, pltpu.roll(d_pre, t_len - shift, axis=0), 0.0)
            dx = dx + up * w_ref[i:i + 1, :]
            dw_ref[i:i + 1, :] = jnp.sum(d_pre * taps[i], axis=0, keepdims=True)
        dx_ref[0] = dx

    return pl.pallas_call(
        body, name="gdn_prep_bwd",
        grid=(3 * GDN_HEADS,),
        in_specs=[_col_block(t_len, GDN_FIRST_BLOCK),
                  pl.BlockSpec((CONV_WIDTH, LANES), lambda j: (0, j)),
                  pl.BlockSpec((1, t_len, LANES), lambda j: (j // GDN_HEADS, 0, j % GDN_HEADS)), _HBM],
        out_specs=[pl.BlockSpec((1, t_len, LANES), lambda j: (DPROJ_GDN_SLOT + j // GDN_HEADS, 0, j % GDN_HEADS)),
                   pl.BlockSpec((CONV_WIDTH, LANES), lambda j: (0, j))],
        out_shape=[jax.ShapeDtypeStruct(dproj.shape, dproj.dtype),
                   jax.ShapeDtypeStruct((CONV_WIDTH, 3 * 512), F32)],
        input_output_aliases={3: 0},
        compiler_params=_params("arbitrary"),
    )(cols, conv_w, d_act3, dproj)


def _chunk_cumsum_matrix():
    r = lax.broadcasted_iota(jnp.int32, (LANES, LANES), 0)
    c = lax.broadcasted_iota(jnp.int32, (LANES, LANES), 1)
    return ((r <= c) & ((r // CHUNK) == (c // CHUNK))).astype(F32)


def _gdn_gates_call(ps, pst, alog_l, dtb_l, alog_c, dtb_c, t_len):
    def body(ps_ref, pst_ref, al_ref, dl_ref, ac_ref, dc_ref, beta_ref, gcol_ref, grow_ref):
        upper = _chunk_cumsum_matrix()
        lower = upper.T
        psv = ps_ref[...]
        beta_ref[...] = _sigmoid(psv)
        g_l = -jnp.exp(al_ref[...]) * _softplus(psv + dl_ref[...])
        g_r = -jnp.exp(ac_ref[...]) * _softplus(pst_ref[...] + dc_ref[...])
        for w in range(t_len // LANES):
            sl = slice(w * LANES, (w + 1) * LANES)
            gcol_ref[sl, :] = _mx(lower, g_l[sl, :])
            grow_ref[:, sl] = _mx(g_r[:, sl], upper)

    vm = pl.BlockSpec(memory_space=pltpu.VMEM)
    return pl.pallas_call(
        body, name="gdn_gates",
        in_specs=[vm] * 6, out_specs=[vm] * 3,
        out_shape=[jax.ShapeDtypeStruct((t_len, LANES), F32),
                   jax.ShapeDtypeStruct((t_len, LANES), F32),
                   jax.ShapeDtypeStruct((8, t_len), F32)],
        compiler_params=pltpu.CompilerParams(vmem_limit_bytes=VMEM_LIMIT_BYTES),
    )(ps, pst, alog_l, dtb_l, alog_c, dtb_c)


def _gdn_gates_bwd_call(ps, alog_l, dtb_l, d_l, t_len):
    def body(ps_ref, al_ref, dl_ref, d_ref, dps_ref, gal_ref, gdt_ref):
        lane = lax.broadcasted_iota(jnp.int32, (1, LANES), 1)
        psv = ps_ref[...]
        dv = d_ref[...]
        beta = _sigmoid(psv)
        ea = jnp.exp(al_ref[...])
        arg = psv + dl_ref[...]
        g = -ea * _softplus(arg)
        d_a = dv * (-ea) * _sigmoid(arg)
        is_a = (lane >= GDN_HEADS) & (lane < 2 * GDN_HEADS)
        dps_ref[...] = jnp.where(lane < GDN_HEADS, dv * beta * (1.0 - beta), jnp.where(is_a, d_a, 0.0))
        gdt_ref[...] = jnp.where(is_a, jnp.sum(d_a, axis=0, keepdims=True), 0.0)
        gal_ref[...] = jnp.where(is_a, jnp.sum(dv * g, axis=0, keepdims=True), 0.0)

    vm = pl.BlockSpec(memory_space=pltpu.VMEM)
    return pl.pallas_call(
        body, name="gdn_gates_bwd",
        in_specs=[vm] * 4, out_specs=[vm] * 3,
        out_shape=[jax.ShapeDtypeStruct((t_len, LANES), F32),
                   jax.ShapeDtypeStruct((1, LANES), F32),
                   jax.ShapeDtypeStruct((1, LANES), F32)],
        compiler_params=pltpu.CompilerParams(vmem_limit_bytes=VMEM_LIMIT_BYTES),
    )(ps, alog_l, dtb_l, d_l)


def _bm(a, b):
    return _m3_general(a, b, _BNN)


def _bm_nt(a, b):
    return _m3_general(a, b, _BNT)


def _bm_tn(a, b):
    return _m3_general(a, b, _BTN)


def _heads_of(ref, rows):
    return jnp.stack([ref[rows, h * GDN_HEAD_DIM:(h + 1) * GDN_HEAD_DIM] for h in range(GDN_HEADS)])


def _chunk_terms(q_ref, k_ref, v_ref, b_ref, gc_ref, gr_ref, c, incl, strict, n=1, scores=True):
    r0 = c * CHUNK if isinstance(c, int) else pl.multiple_of(c * CHUNK, CHUNK)
    rows = pl.ds(r0, n * CHUNK)
    per_chunk = lambda x: x.reshape(GDN_HEADS * n, CHUNK, x.shape[-1])
    q, k, v = (per_chunk(_heads_of(ref, rows)) for ref in (q_ref, k_ref, v_ref))
    lane_ids = lax.broadcasted_iota(jnp.int32, (1, LANES), 1)
    pick = lambda slab, first: jnp.stack([jnp.sum(jnp.where(lane_ids == first + h, slab, 0.0), axis=-1, keepdims=True)
                                          for h in range(GDN_HEADS)])
    b = per_chunk(pick(b_ref[rows, :], 0))
    gc = per_chunk(pick(gc_ref[rows, :], GDN_HEADS))
    gr = gr_ref[:, c] if n == 1 else gr_ref[:, c:c + n].reshape(GDN_HEADS * n, 1, CHUNK)
    dm = jnp.where(incl, jnp.exp(jnp.where(incl, gc - gr, 0.0)), 0.0)
    kb = k * b
    vb = v * b
    e = jnp.exp(gc)
    a = p = None
    if scores:
        kk_qk = _bm_nt(jnp.concatenate([kb, q], axis=1), k)
        a = jnp.where(strict, kk_qk[:, :CHUNK] * dm, 0.0)
        p = jnp.where(incl, kk_qk[:, CHUNK:] * dm, 0.0)
    gl = gc[:, CHUNK - 1:CHUNK, :]
    eg = jnp.exp(gl - gc)
    return rows, q, k, v, b, gc, dm, kb, vb, e, a, p, gl, eg


def _unit_lower_inverse(a, eye):
    x = -a
    tm = eye + x
    xp = _bm(x, x)
    for _ in range(4):
        both = _bm(jnp.concatenate([xp, tm], axis=1), xp)
        tm = tm + both[:, CHUNK:]
        xp = both[:, :CHUNK]
    return tm + _bm(tm, xp)


def _gdn_specs(t_len, n_chunks, reverse):
    cps = GDN_CHUNKS_PER_STEP
    steps = n_chunks // cps
    at = (lambda g: steps - 1 - g) if reverse else (lambda g: g)
    rows_blk = lambda width, part=0: pl.BlockSpec((cps * CHUNK, width), lambda g: (at(g), part))
    gate_r = pl.BlockSpec((GDN_HEADS, cps, 1, CHUNK), lambda g: (0, at(g), 0, 0))
    per_chunk = lambda r, c: pl.BlockSpec((GDN_HEADS, cps, r, c), lambda g: (0, at(g), 0, 0))
    return cps, steps, rows_blk, gate_r, per_chunk


def _gdn_fwd_call(gact, beta_c, gam_c, gam_r, t_len):
    n_chunks = t_len // CHUNK
    dk = GDN_HEAD_DIM
    width = GDN_HEADS * dk
    cps, steps, rows_blk, gate_r, per_chunk = _gdn_specs(t_len, n_chunks, False)

    def body(q_ref, k_ref, v_ref, b_ref, gc_ref, gr_ref, o_ref, s_ref, t_ref, a_ref, p_ref, uw_ref, vn_ref, state_ref):
        row = lax.broadcasted_iota(jnp.int32, (CHUNK, CHUNK), 0)
        col = lax.broadcasted_iota(jnp.int32, (CHUNK, CHUNK), 1)
        incl, strict = row >= col, row > col
        eye = (row == col).astype(F32)

        @pl.when(pl.program_id(0) == 0)
        def _():
            state_ref[...] = jnp.zeros_like(state_ref)

        _, q, k, v, b, gc, dm, kb, vb, e, a, p, gl, eg = _chunk_terms(
            q_ref, k_ref, v_ref, b_ref, gc_ref, gr_ref, 0, incl, strict, cps)
        tm = _unit_lower_inverse(a, eye)
        uw = _bm(tm, jnp.concatenate([vb, kb * e], axis=2))
        w_qe = jnp.concatenate([uw[:, :, dk:], q * e], axis=1)
        u, kd, decay = uw[:, :, :dk], k * eg, jnp.exp(gl)
        per_chunk_block = lambda x: x.reshape(GDN_HEADS, cps, CHUNK, CHUNK)
        t_ref[...], a_ref[...], p_ref[...] = per_chunk_block(tm), per_chunk_block(a), per_chunk_block(p)
        uw_heads = uw.reshape(GDN_HEADS, cps * CHUNK, 2 * dk)
        for h in range(GDN_HEADS):
            uw_ref[:, h * 2 * dk:(h + 1) * 2 * dk] = uw_heads[h]

        of_chunk = lambda x, c: jnp.stack([x[h * cps + c] for h in range(GDN_HEADS)])
        s = state_ref[...]
        for c in range(cps):
            ws_qs = _bm(of_chunk(w_qe, c), s)
            vn = of_chunk(u, c) - ws_qs[:, :CHUNK]
            o = ws_qs[:, CHUNK:] + _bm(of_chunk(p, c), vn)
            for h in range(GDN_HEADS):
                o_ref[c * CHUNK:(c + 1) * CHUNK, h * dk:(h + 1) * dk] = o[h]
                vn_ref[c * CHUNK:(c + 1) * CHUNK, h * dk:(h + 1) * dk] = vn[h]
            s_ref[:, c] = s
            s = s * of_chunk(decay, c) + _bm_tn(of_chunk(kd, c), vn)
        state_ref[...] = s

    scores = jax.ShapeDtypeStruct((GDN_HEADS, n_chunks, CHUNK, CHUNK), F32)
    return pl.pallas_call(
        body, name="gdn_fwd",
        grid=(steps,),
        in_specs=[rows_blk(width, 0), rows_blk(width, 1), rows_blk(width, 2), rows_blk(LANES), rows_blk(LANES), gate_r],
        out_specs=[rows_blk(width), per_chunk(dk, dk), per_chunk(CHUNK, CHUNK), per_chunk(CHUNK, CHUNK),
                   per_chunk(CHUNK, CHUNK), rows_blk(2 * width), rows_blk(width)],
        out_shape=[jax.ShapeDtypeStruct((t_len, width), F32),
                   jax.ShapeDtypeStruct((GDN_HEADS, n_chunks, dk, dk), F32), scores, scores, scores,
                   jax.ShapeDtypeStruct((t_len, 2 * width), F32), jax.ShapeDtypeStruct((t_len, width), F32)],
        scratch_shapes=[pltpu.VMEM((GDN_HEADS, dk, dk), F32)],
        compiler_params=_params("arbitrary"),
    )(gact, gact, gact, beta_c, gam_c, gam_r)


def _gdn_bwd_call(gact, beta_c, gam_c, gam_r, saved, d_o, t_len, scatter=()):
    n_chunks = t_len // CHUNK
    dk = GDN_HEAD_DIM
    width = GDN_HEADS * dk
    cps, steps, rows_blk, gate_r, per_chunk = _gdn_specs(t_len, n_chunks, True)
    nx = len(scatter)
    n_in = 13

    def body(*refs):
        q_ref, k_ref, v_ref, b_ref, gc_ref, gr_ref = refs[:6]
        saved_refs, do_ref = refs[6:12], refs[12]
        d_ref, dgate_ref = refs[n_in + nx:n_in + 2 + nx]
        dstate_ref = refs[n_in + 2 + 2 * nx]
        copies = lambda: _direct_copies(refs[n_in:n_in + nx], refs[n_in + 2 + nx:n_in + 2 + 2 * nx],
                                        *refs[n_in + 3 + 2 * nx:], (True,) * nx)
        if nx:
            pl.when(pl.program_id(0) == 0)(lambda: _start_all(copies()))
        row = lax.broadcasted_iota(jnp.int32, (CHUNK, CHUNK), 0)
        col = lax.broadcasted_iota(jnp.int32, (CHUNK, CHUNK), 1)
        incl, strict = row >= col, row > col
        ng = GDN_BWD_GROUP
        nb = GDN_HEADS * ng
        upper = jnp.broadcast_to((row <= col).astype(F32), (nb, CHUNK, CHUNK))
        ones = jnp.ones((nb, CHUNK, LANES), F32)
        last_row = lax.broadcasted_iota(jnp.int32, (CHUNK, 1), 0) == CHUNK - 1
        lane_ids = lax.broadcasted_iota(jnp.int32, (1, LANES), 1)
        rsum = lambda m: jnp.sum(m, axis=-1, keepdims=True)
        total = lambda m: jnp.sum(rsum(m), axis=1, keepdims=True)
        of_chunk = lambda x, c: jnp.stack([x[h * ng + c] for h in range(GDN_HEADS)])

        @pl.when(pl.program_id(0) == 0)
        def _():
            dstate_ref[...] = jnp.zeros_like(dstate_ref)

        for c0 in range(cps - ng, -1, -ng):
            group(c0, q_ref, k_ref, v_ref, b_ref, gc_ref, gr_ref, saved_refs, do_ref, d_ref, dgate_ref, dstate_ref,
                  incl, strict, upper, ones, last_row, lane_ids, rsum, total, of_chunk)
        if nx:
            pl.when(pl.program_id(0) == steps - 1)(lambda: _wait_all(copies()))

    def group(c0, q_ref, k_ref, v_ref, b_ref, gc_ref, gr_ref, saved_refs, do_ref, d_ref, dgate_ref, dstate_ref,
              incl, strict, upper, ones, last_row, lane_ids, rsum, total, of_chunk):
        ng = GDN_BWD_GROUP
        nb = GDN_HEADS * ng
        rows = pl.ds(c0 * CHUNK, ng * CHUNK)
        s_ref, t_ref, a_ref, p_ref, uw_ref, vn_ref = saved_refs
        _, q, k, v, b, gc, dm, kb, vb, e, _, _, gl, eg = _chunk_terms(
            q_ref, k_ref, v_ref, b_ref, gc_ref, gr_ref, c0, incl, strict, ng, scores=False)
        s = s_ref[:, c0:c0 + ng].reshape(nb, dk, dk)
        tm = t_ref[:, c0:c0 + ng].reshape(nb, CHUNK, CHUNK)
        a = a_ref[:, c0:c0 + ng].reshape(nb, CHUNK, CHUNK)
        p = p_ref[:, c0:c0 + ng].reshape(nb, CHUNK, CHUNK)
        d_out = _heads_of(do_ref, rows).reshape(nb, CHUNK, dk)
        vn = _heads_of(vn_ref, rows).reshape(nb, CHUNK, dk)
        uw = jnp.stack([uw_ref[rows, h * 2 * dk:(h + 1) * 2 * dk] for h in range(GDN_HEADS)]).reshape(nb, CHUNK, 2 * dk)
        u, w = uw[:, :, :dk], uw[:, :, dk:]
        el = jnp.exp(gl)
        kbe = kb * e
        qe = q * e
        kd = k * eg
        pt_do = _bm_tn(p, d_out)
        qet_do = _bm_tn(qe, d_out)

        ds = dstate_ref[...]
        d_vn_c, ds_c = [None] * ng, [None] * ng
        for c in range(ng - 1, -1, -1):
            ds_c[c] = ds
            d_vn_c[c] = of_chunk(pt_do, c) + _bm(of_chunk(kd, c), ds)
            ds = of_chunk(el, c) * ds + of_chunk(qet_do, c) - _bm_tn(of_chunk(w, c), d_vn_c[c])
        dstate_ref[...] = ds
        by_chunk = lambda xs: jnp.stack([xs[c][h] for h in range(GDN_HEADS) for c in range(ng)])
        d_vn, ds = by_chunk(d_vn_c), by_chunk(ds_c)

        on_s = _bm_nt(jnp.concatenate([d_out, d_vn], axis=1), s)
        d_qe, d_w = on_s[:, :CHUNK], -on_s[:, CHUNK:]
        d_p = jnp.where(incl, _bm_nt(d_out, vn), 0.0)
        d_kd = _bm_nt(vn, ds)
        d_both = _bm_tn(tm, jnp.concatenate([d_vn, d_w], axis=2))
        d_vb, d_kbe = d_both[:, :, :dk], d_both[:, :, dk:]
        d_a = -jnp.where(strict, _bm_nt(d_both, uw), 0.0)
        m = d_a * dm
        n = d_p * dm
        on_k = _bm(jnp.concatenate([m, n], axis=1), k)
        d_kb = on_k[:, :CHUNK] + d_kbe * e
        d_q = on_k[:, CHUNK:] + d_qe * e
        d_k = (_bm_tn(jnp.concatenate([m, n], axis=1), jnp.concatenate([kb, q], axis=1))
               + d_kd * eg + b * d_kb)
        d_v = b * d_vb
        r = d_a * a + d_p * p
        kd_term = rsum(d_kd * kd)
        d_gl = total(ds * s) * el + jnp.sum(kd_term, axis=1, keepdims=True)
        d_gam = (rsum(r) - _times_exact(r, ones, _BTN)[:, :, 0:1] + rsum(d_qe * qe) + rsum(d_kbe * kbe) - kd_term
                 + jnp.where(last_row, d_gl, 0.0))
        d_beta = rsum(d_kb * k) + rsum(d_vb * v)
        d_g = _exact_times(upper, d_gam * ones, _BNN)[:, :, 0:1]
        per_head = lambda x: x.reshape(GDN_HEADS, ng * CHUNK, x.shape[-1])
        d_q, d_k, d_v, d_beta, d_g = (per_head(x) for x in (d_q, d_k, d_v, d_beta, d_g))
        gates = jnp.zeros((ng * CHUNK, LANES), F32)
        for h in range(GDN_HEADS):
            lanes = slice(h * dk, (h + 1) * dk)
            d_ref[0, rows, lanes] = d_q[h]
            d_ref[1, rows, lanes] = d_k[h]
            d_ref[2, rows, lanes] = d_v[h]
            gates = gates + (jnp.where(lane_ids == h, d_beta[h], 0.0)
                             + jnp.where(lane_ids == GDN_HEADS + h, d_g[h], 0.0))
        dgate_ref[rows, :] = gates

    d_spec = pl.BlockSpec((3, cps * CHUNK, width), lambda g: (0, steps - 1 - g, 0))
    return pl.pallas_call(
        body, name="gdn_bwd",
        grid=(steps,),
        in_specs=[rows_blk(width, 0), rows_blk(width, 1), rows_blk(width, 2), rows_blk(LANES), rows_blk(LANES), gate_r,
                  per_chunk(dk, dk), per_chunk(CHUNK, CHUNK), per_chunk(CHUNK, CHUNK), per_chunk(CHUNK, CHUNK),
                  rows_blk(2 * width), rows_blk(width), rows_blk(width)] + [_HBM] * nx,
        out_specs=[d_spec, rows_blk(LANES)] + [_HBM] * nx,
        out_shape=[jax.ShapeDtypeStruct((3, t_len, width), F32),
                   jax.ShapeDtypeStruct((t_len, LANES), F32)] + _direct_out_shapes(scatter, (True,) * nx),
        scratch_shapes=[pltpu.VMEM((GDN_HEADS, dk, dk), F32)] + (_direct_semaphores(nx) if nx else []),
        compiler_params=_params("arbitrary"),
    )(gact, gact, gact, beta_c, gam_c, gam_r, *saved, d_o, *scatter)


def _group_sums(x, group):
    rows, width = x.shape
    lane = lax.broadcasted_iota(jnp.int32, (1, LANES), 1)
    out = []
    for t in range(width // LANES):
        seg = x[:, t * LANES:(t + 1) * LANES]
        if group == LANES:
            out.append(jnp.broadcast_to(jnp.sum(seg, axis=-1, keepdims=True), (rows, LANES)))
        else:
            low = jnp.sum(jnp.where(lane < group, seg, 0.0), axis=-1, keepdims=True)
            high = jnp.sum(jnp.where(lane < group, 0.0, seg), axis=-1, keepdims=True)
            out.append(jnp.where(lane < group, low, high))
    return jnp.concatenate(out, axis=1)


def _post_call(o_sb, o_gd, proj_gates, x, target, w_out, sbw, gdw, fw, tm=256):
    t_len, d = x.shape
    half = 512
    sb_blocks = half // LANES

    def body(osb_ref, ogd_ref, zsb_ref, zgd_ref, x_ref, tg_ref, wo_ref, sbw_ref, gdw_ref, fw_ref,
             dx2_ref, dosb_ref, dogd_ref, dz_ref, loss_ref, gfw_ref, gsb_ref, ggd_ref, gwo_ref):
        step = pl.program_id(0)

        @pl.when(step == 0)
        def _():
            loss_ref[...] = jnp.zeros_like(loss_ref)
            gfw_ref[...] = jnp.zeros_like(gfw_ref)
            gsb_ref[...] = jnp.zeros_like(gsb_ref)
            ggd_ref[...] = jnp.zeros_like(ggd_ref)
            gwo_ref[...] = jnp.zeros_like(gwo_ref)

        def head_forward(o, z, w, head_dim):
            r = lax.rsqrt(_group_sums(o * o, head_dim) * (1.0 / head_dim) + EPS)
            nrm = o * r * w
            sg = _sigmoid(z)
            return r, nrm, sg, nrm * (z * sg)

        def head_backward(d_m, o, z, w, head_dim, r, nrm, sg):
            d_n = d_m * (z * sg)
            d_z = d_m * nrm * (sg * (1.0 + z * (1.0 - sg)))
            dnw = d_n * w
            d_o = r * dnw - o * (r * r * r) * (_group_sums(dnw * o, head_dim) * (1.0 / head_dim))
            return d_o, d_z, jnp.sum(d_n * o * r, axis=0, keepdims=True)

        osb = jnp.concatenate([osb_ref[j] for j in range(sb_blocks)], axis=1)
        ogd, zsb, zgd = ogd_ref[...], zsb_ref[...], zgd_ref[...]
        sbw_v, gdw_v = sbw_ref[...], gdw_ref[...]
        r_sb, n_sb, sg_sb, m_sb = head_forward(osb, zsb, sbw_v, SB_HEAD_DIM)
        r_gd, n_gd, sg_gd, m_gd = head_forward(ogd, zgd, gdw_v, GDN_HEAD_DIM)
        mixed = jnp.concatenate([m_sb, m_gd], axis=1).astype(MXU_DTYPE)
        wo = wo_ref[...]
        x2 = x_ref[...] + jnp.dot(mixed, wo, preferred_element_type=F32)
        r2 = lax.rsqrt(jnp.mean(x2 * x2, axis=-1, keepdims=True) + EPS)
        fw_v = fw_ref[...]
        err = x2 * r2 * fw_v - tg_ref[...]
        loss_ref[...] += 0.5 * jnp.sum(jnp.sum(err * err, axis=-1, keepdims=True) * (1.0 / d))
        dy = err * (1.0 / d)
        gg = dy * fw_v
        dx2 = r2 * gg - x2 * ((r2 * r2 * r2) * jnp.mean(gg * x2, axis=-1, keepdims=True))
        gfw_ref[...] += jnp.sum(dy * x2 * r2, axis=0, keepdims=True)
        dx2_ref[...] = dx2
        dx2b = dx2.astype(MXU_DTYPE)
        d_mixed = lax.dot_general(dx2b, wo, _NT, preferred_element_type=F32)
        gwo_ref[...] += lax.dot_general(mixed, dx2b, _TN, preferred_element_type=F32)
        d_osb, d_zsb, gsb = head_backward(d_mixed[:, :half], osb, zsb, sbw_v, SB_HEAD_DIM, r_sb, n_sb, sg_sb)
        d_ogd, d_zgd, ggd = head_backward(d_mixed[:, half:], ogd, zgd, gdw_v, GDN_HEAD_DIM, r_gd, n_gd, sg_gd)
        for j in range(sb_blocks):
            dosb_ref[j] = d_osb[:, j * LANES:(j + 1) * LANES]
        dogd_ref[...] = d_ogd
        dz_ref[0] = d_zsb
        dz_ref[1] = d_zgd
        gsb_ref[...] += gsb
        ggd_ref[...] += ggd

    row_blk = lambda w: pl.BlockSpec((tm, w), lambda i: (i, 0))
    blocks_blk = pl.BlockSpec((sb_blocks, tm, LANES), lambda i: (0, i, 0))
    fixed = lambda r, w: pl.BlockSpec((r, w), lambda i: (0, 0))
    return pl.pallas_call(
        body, name="post",
        grid=(t_len // tm,),
        in_specs=[blocks_blk, row_blk(half),
                  pl.BlockSpec((tm, half), lambda i: (i, 0)),
                  pl.BlockSpec((tm, half), lambda i: (i, 1)),
                  row_blk(d), row_blk(d), fixed(d, d), fixed(1, half), fixed(1, half), fixed(1, d)],
        out_specs=[row_blk(d), blocks_blk, row_blk(half),
                   pl.BlockSpec((2, tm, half), lambda i: (DPROJ_GATE_SLOT // 2, i, 0)),
                   fixed(1, LANES), fixed(1, d), fixed(1, half), fixed(1, half), fixed(d, d)],
        out_shape=[jax.ShapeDtypeStruct((t_len, d), F32), jax.ShapeDtypeStruct((sb_blocks, t_len, LANES), F32),
                   jax.ShapeDtypeStruct((t_len, half), F32),
                   jax.ShapeDtypeStruct((len(DPROJ_PIECE_OF_SLOT), t_len, half), F32),
                     jax.ShapeDtypeStruct((1, LANES), F32), jax.ShapeDtypeStruct((1, d), F32),
                     jax.ShapeDtypeStruct((1, half), F32), jax.ShapeDtypeStruct((1, half), F32),
                     jax.ShapeDtypeStruct((d, d), F32)],
        compiler_params=_params("arbitrary"),
    )(o_sb, o_gd, proj_gates, proj_gates, x, target, w_out, sbw, gdw, fw)


def _piece_of_slot(s):
    return jnp.where(s < DPROJ_GDN_SLOT, s, jnp.where(s < DPROJ_GATE_SLOT, s + 1,
                                                     jnp.where(s == DPROJ_GATE_SLOT, 3, 7)))


def _gw_in_call(h_t, dproj8):
    d, t_len = h_t.shape
    n_piece, _, pw = dproj8.shape

    def body(ht_ref, dp_ref, gw_ref):
        gw_ref[...] = jnp.dot(ht_ref[...], dp_ref[0].astype(MXU_DTYPE), preferred_element_type=F32)

    return pl.pallas_call(
        body, name="gw_in",
        grid=(n_piece,),
        in_specs=[pl.BlockSpec((d, t_len), lambda s: (0, 0)),
                  pl.BlockSpec((1, t_len, pw), lambda s: (s, 0, 0))],
        out_specs=pl.BlockSpec((d, pw), lambda s: (0, _piece_of_slot(s))),
        out_shape=jax.ShapeDtypeStruct((d, n_piece * pw), F32),
        compiler_params=_params("arbitrary"),
    )(h_t, dproj8)


def _slot_of_piece(p):
    return jnp.where(p < DPROJ_GDN_SLOT, p, jnp.where(p == 3, DPROJ_GATE_SLOT, jnp.where(p < 7, p - 1, 7)))


def _gw_in_shards_call(h_t, dproj8, dsmall, out_dtype):
    d, t_len = h_t.shape
    n_piece, _, pw = dproj8.shape
    ns = dsmall.shape[1]
    n_pairs = N_DEV // 2

    def body(ht_ref, dp_ref, ds_ref, chip_ref, prev_ref, gates_ref, send_ref, recv_ref, send_sems, recv_sems):
        p = pl.program_id(0)
        x_pos, y_pos, c = lax.axis_index("x"), lax.axis_index("y"), lax.axis_index("c")
        to_sibling = lambda pair: pltpu.make_async_remote_copy(
            src_ref=send_ref.at[pair], dst_ref=recv_ref.at[pair], send_sem=send_sems.at[pair],
            recv_sem=recv_sems.at[pair], device_id=(x_pos, y_pos, 1 - c), device_id_type=_MESH)

        @pl.when(p == 0)
        def _():
            gates_ref[...] = jnp.dot(ht_ref[...], ds_ref[...].astype(MXU_DTYPE), preferred_element_type=F32)

        def emit(s, tail):
            x = jnp.concatenate([prev_ref[...], tail], axis=1)
            y = x if s == 0 else pltpu.roll(x, SHARD_PAD - s, axis=1)
            shard = y[:, :SHARD_COLS].astype(out_dtype)

            @pl.when(c == s % 2)
            def _():
                chip_ref[s // 2] = shard

            @pl.when(c != s % 2)
            def _():
                send_ref[s // 2] = shard
                to_sibling(s // 2).start()

        @pl.when(p < n_piece)
        def _():
            cur = jnp.dot(ht_ref[...], dp_ref[0].astype(MXU_DTYPE), preferred_element_type=F32)
            for s in range(n_piece - 1):
                pl.when(p == s + 1)(functools.partial(emit, s, cur[:, :SHARD_PAD - pw]))
            prev_ref[...] = cur

        @pl.when(p == n_piece)
        def _():
            emit(n_piece - 1, gates_ref[...])
            for pair in range(n_pairs):
                to_sibling(pair).wait_send()
            for pair in range(n_pairs):
                to_sibling(pair).wait_recv()
                chip_ref[pair] = (chip_ref[pair].astype(F32) + recv_ref[pair].astype(F32)).astype(out_dtype)

    shards_of_side = lambda: pltpu.VMEM((n_pairs, d, SHARD_COLS), out_dtype)
    return pl.pallas_call(
        body, name="gw_in",
        grid=(n_piece + 1,),
        in_specs=[pl.BlockSpec((d, t_len), lambda p: (0, 0)),
                  pl.BlockSpec((1, t_len, pw), lambda p: (_slot_of_piece(jnp.minimum(p, n_piece - 1)), 0, 0)),
                  pl.BlockSpec((t_len, ns), lambda p: (0, 0))],
        out_specs=pl.BlockSpec((n_pairs, d, SHARD_COLS), lambda p: (0, 0, 0)),
        out_shape=jax.ShapeDtypeStruct((n_pairs, d, SHARD_COLS), out_dtype),
        scratch_shapes=[pltpu.VMEM((d, pw), F32), pltpu.VMEM((d, ns), F32), shards_of_side(), shards_of_side(),
                        pltpu.SemaphoreType.DMA((n_pairs,)), pltpu.SemaphoreType.DMA((n_pairs,))],
        compiler_params=_params("arbitrary"),
    )(h_t, dproj8, dsmall)


def _gw_small_call(h_t, dsmall, tm=512):
    d, t_len = h_t.shape
    ns = dsmall.shape[1]

    def body(ht_ref, dp_ref, gw_ref):
        @pl.when(pl.program_id(0) == 0)
        def _():
            gw_ref[...] = jnp.zeros_like(gw_ref)

        gw_ref[...] += jnp.dot(ht_ref[...], dp_ref[...].astype(MXU_DTYPE), preferred_element_type=F32)

    return pl.pallas_call(
        body, name="gw_small",
        grid=(t_len // tm,),
        in_specs=[pl.BlockSpec((d, tm), lambda t: (0, t)),
                  pl.BlockSpec((tm, ns), lambda t: (t, 0))],
        out_specs=pl.BlockSpec((d, ns), lambda t: (0, 0)),
        out_shape=jax.ShapeDtypeStruct((d, ns), F32),
        compiler_params=_params("arbitrary"),
    )(h_t, dsmall)


def _dx_call(dproj8, dsmall, w_main, w_small, x, r, dx2, norm_w, chip_scatter=(), peer_scatter=(), tm=256):
    t_len, d = x.shape
    n_piece, _, pw = dproj8.shape
    ns = dsmall.shape[1]
    nx = len(chip_scatter)
    n_peer = len(peer_scatter)
    steps = t_len // tm
    n_in = 8 + nx + n_peer
    n_out = 2 + nx + n_peer

    def body(*refs):
        dp_ref, ds_ref, wm_ref, ws_ref, x_ref, r_ref, dx2_ref, nw_ref = refs[:8]
        gx_ref, gnw_ref = refs[n_in:n_in + 2]
        scratch = refs[n_in + n_out:]
        copies = lambda: _chip_copies(refs[8:8 + nx], refs[n_in + 2:n_in + 2 + nx], *scratch[:3])
        if nx:
            pl.when(pl.program_id(0) == 0)(lambda: _start_all(copies()))

        @pl.when(pl.program_id(0) == 0)
        def _():
            gnw_ref[...] = jnp.zeros_like(gnw_ref)

        dh = lax.dot_general(ds_ref[...].astype(MXU_DTYPE), ws_ref[...], _NT, preferred_element_type=F32)
        for s, p in enumerate(DPROJ_PIECE_OF_SLOT):
            dh = dh + lax.dot_general(dp_ref[s].astype(MXU_DTYPE), wm_ref[:, p * pw:(p + 1) * pw], _NT,
                                      preferred_element_type=F32)
        xv, rv = x_ref[...], r_ref[...]
        dn = dh * nw_ref[...]
        gx_ref[...] = dx2_ref[...] + rv * dn - xv * ((rv * rv * rv) * jnp.mean(dn * xv, axis=-1, keepdims=True))
        gnw_ref[...] += jnp.sum(dh * xv * rv, axis=0, keepdims=True)

        @pl.when(pl.program_id(0) == steps - 1)
        def _():
            if n_peer:
                small_ref, parts_ref = refs[8 + nx:n_in]
                small_buf = scratch[6]
                small_buf[...] = small_ref[...]
                small_buf[0:1, :] = gnw_ref[...]
                peer_copies = _direct_copies([small_buf, parts_ref], refs[n_in + 2 + nx:n_in + n_out], *scratch[3:6],
                                             [False, True])
                _start_all(peer_copies)
            if nx:
                _wait_all(copies())
            if n_peer:
                _wait_all(peer_copies)

    assert n_peer in (0, 2) and (nx == 1 or not n_peer)
    peer_in_specs = [pl.BlockSpec(peer_scatter[0].shape, lambda i: (0, 0)), _HBM] if n_peer else []
    peer_scratch = _direct_semaphores(n_peer) + [pltpu.VMEM(peer_scatter[0].shape, F32)] if n_peer else []
    return pl.pallas_call(
        body, name="dx",
        grid=(steps,),
        in_specs=[pl.BlockSpec((n_piece, tm, pw), lambda i: (0, i, 0)),
                  pl.BlockSpec((tm, ns), lambda i: (i, 0)),
                  pl.BlockSpec((d, n_piece * pw), lambda i: (0, 0)),
                  pl.BlockSpec((d, ns), lambda i: (0, 0)),
                  pl.BlockSpec((tm, d), lambda i: (i, 0)),
                  pl.BlockSpec((tm, 1), lambda i: (i, 0)),
                  pl.BlockSpec((tm, d), lambda i: (i, 0)),
                  pl.BlockSpec((1, d), lambda i: (0, 0))] + [_HBM] * nx + peer_in_specs,
        out_specs=[pl.BlockSpec((tm, d), lambda i: (i, 0)),
                   pl.BlockSpec((1, d), lambda i: (0, 0))] + [_HBM] * (nx + n_peer),
        out_shape=[jax.ShapeDtypeStruct((t_len, d), F32), jax.ShapeDtypeStruct((1, d), F32)]
                  + [jax.ShapeDtypeStruct(a.shape, a.dtype) for a in chip_scatter]
                  + (_direct_out_shapes(peer_scatter, [False, True]) if n_peer else []),
        scratch_shapes=(_chip_semaphores(nx) if nx else []) + peer_scratch,
        compiler_params=_params("arbitrary"),
    )(dproj8, dsmall, w_main, w_small, x, r, dx2, norm_w, *chip_scatter, *peer_scatter)


def _direct_out_shapes(srcs, per_peer):
    return [jax.ShapeDtypeStruct(s.shape if pp else (N_DEV,) + s.shape, s.dtype) for s, pp in zip(srcs, per_peer)]


def _direct_semaphores(n):
    return [pltpu.SemaphoreType.DMA((n * (N_DEV - 1),)), pltpu.SemaphoreType.DMA((n * (N_DEV - 1),)),
            pltpu.SemaphoreType.DMA((n,))]


def _direct_copies(src_refs, out_refs, send_sems, recv_sems, local_sems, per_peer):
    x, y, c = lax.axis_index("x"), lax.axis_index("y"), lax.axis_index("c")
    me = 4 * x + 2 * y + c
    local, remote = [], []
    for a in range(len(src_refs)):
        mine = src_refs[a].at[me] if per_peer[a] else src_refs[a]
        local.append(pltpu.make_async_copy(mine, out_refs[a].at[me], local_sems.at[a]))
    for k in range(1, N_DEV):
        kx, ky, kc = (k >> 2) & 1, (k >> 1) & 1, k & 1
        px = 1 - x if kx else x
        py = 1 - y if ky else y
        pc = 1 - c if kc else c
        peer = 4 * px + 2 * py + pc
        for a in range(len(src_refs)):
            sem = a * (N_DEV - 1) + (k - 1)
            remote.append(pltpu.make_async_remote_copy(
                src_ref=src_refs[a].at[peer] if per_peer[a] else src_refs[a], dst_ref=out_refs[a].at[me],
                send_sem=send_sems.at[sem], recv_sem=recv_sems.at[sem],
                device_id=(px, py, pc), device_id_type=pl.DeviceIdType.MESH))
    return local, remote


def _start_all(copies):
    local, remote = copies
    for cp in local + remote:
        cp.start()


def _wait_all(copies):
    local, remote = copies
    for cp in remote:
        cp.wait_send()
    for cp in remote:
        cp.wait_recv()
    for cp in local:
        cp.wait()


N_CHIPS = 4
_HBM = pl.BlockSpec(memory_space=pl.ANY)
_MESH = pl.DeviceIdType.MESH


def _gather_call(name, srcs):
    n = len(srcs)
    per = N_DEV - 1

    def body(*refs):
        src_refs, out_refs = refs[:n], refs[n:2 * n]
        send_sems, recv_sems, local_sems = refs[2 * n:]
        x, y, c = lax.axis_index("x"), lax.axis_index("y"), lax.axis_index("c")
        me, sibling = (x, y, c), (x, y, 1 - c)
        x_nbr, y_nbr, diagonal = (1 - x, y), (x, 1 - y), (1 - x, 1 - y)
        held = ((1 - x) * c + x * (1 - c), y * c + (1 - y) * (1 - c))
        onward = (x * c + (1 - x) * (1 - c), (1 - y) * c + y * (1 - c))
        slot = lambda px, py, pc: 4 * px + 2 * py + pc

        def copy(a, k, block, to, from_src=False):
            rows = out_refs[a].at[slot(*block)]
            return pltpu.make_async_remote_copy(
                src_ref=src_refs[a] if from_src else rows, dst_ref=rows,
                send_sem=send_sems.at[a * per + k], recv_sem=recv_sems.at[a * per + k],
                device_id=to, device_id_type=_MESH)

        local = [pltpu.make_async_copy(src_refs[a], out_refs[a].at[slot(*me)], local_sems.at[a]) for a in range(n)]
        started = []

        def start(cp):
            cp.start()
            started.append(cp)

        for cp in local:
            cp.start()
        for a in range(n):
            start(copy(a, 0, me, sibling, True))
            start(copy(a, 1, me, (*x_nbr, c), True))
            start(copy(a, 2, me, (*y_nbr, c), True))
        for a in range(n):
            copy(a, 1, (*x_nbr, c), me).wait_recv()
            copy(a, 2, (*y_nbr, c), me).wait_recv()
            start(copy(a, 3, (*held, c), (*onward, c)))
            start(copy(a, 4, (*x_nbr, c), sibling))
            start(copy(a, 5, (*y_nbr, c), sibling))
        for a in range(n):
            copy(a, 3, (*diagonal, c), me).wait_recv()
            start(copy(a, 6, (*diagonal, c), sibling))
        for a in range(n):
            copy(a, 0, sibling, me).wait_recv()
            for k, chip in ((4, x_nbr), (5, y_nbr), (6, diagonal)):
                copy(a, k, (*chip, 1 - c), me).wait_recv()
        for cp in started:
            cp.wait_send()
        for cp in local:
            cp.wait()

    return pl.pallas_call(
        body, name=name,
        in_specs=[_HBM] * n, out_specs=[_HBM] * n,
        out_shape=[jax.ShapeDtypeStruct((N_DEV,) + s.shape, s.dtype) for s in srcs],
        scratch_shapes=[pltpu.SemaphoreType.DMA((n * per,)), pltpu.SemaphoreType.DMA((n * per,)),
                        pltpu.SemaphoreType.DMA((n,))],
    )(*srcs)


def _chip_semaphores(n):
    per = N_CHIPS - 1
    return [pltpu.SemaphoreType.DMA((n * per,)), pltpu.SemaphoreType.DMA((n * per,)), pltpu.SemaphoreType.DMA((n,))]


def _chip_copies(src_refs, out_refs, send_sems, recv_sems, local_sems):
    per = N_CHIPS - 1
    x, y, c = lax.axis_index("x"), lax.axis_index("y"), lax.axis_index("c")
    mine = 2 * x + y
    chips = [(1 - x, y), (x, 1 - y), (1 - x, 1 - y)]
    n = len(src_refs)
    local = [pltpu.make_async_copy(src_refs[a].at[mine], out_refs[a].at[mine], local_sems.at[a]) for a in range(n)]
    remote = []
    for a in range(n):
        for j, (px, py) in enumerate(chips):
            remote.append(pltpu.make_async_remote_copy(
                src_ref=src_refs[a].at[2 * px + py], dst_ref=out_refs[a].at[mine],
                send_sem=send_sems.at[a * per + j], recv_sem=recv_sems.at[a * per + j],
                device_id=(px, py, c), device_id_type=_MESH))
    return local, remote


def _adam_call(name, parts, w, m, v, tr):
    rows, cols = w.shape
    n_slots = parts.shape[0]

    def body(p_ref, w_ref, m_ref, v_ref, g_ref, d_ref, nm_ref, nv_ref):
        g = p_ref[0].astype(F32)
        for s in range(1, n_slots):
            g = g + p_ref[s].astype(F32)
        m_new = ADAM_B1 * m_ref[...] + (1.0 - ADAM_B1) * g
        v_new = ADAM_B2 * v_ref[...] + (1.0 - ADAM_B2) * (g * g)
        m_hat = m_new / (1.0 - ADAM_B1 ** ADAM_STEP)
        v_hat = v_new / (1.0 - ADAM_B2 ** ADAM_STEP)
        g_ref[...] = g
        d_ref[...] = -ADAM_LR * (m_hat / (jnp.sqrt(v_hat) + ADAM_EPS) + ADAM_WD * w_ref[...])
        nm_ref[...] = m_new
        nv_ref[...] = v_new

    blk = pl.BlockSpec((tr, cols), lambda i: (i, 0))
    return pl.pallas_call(
        body, name=name,
        grid=(rows // tr,),
        in_specs=[pl.BlockSpec((n_slots, tr, cols), lambda i: (0, i, 0)), blk, blk, blk],
        out_specs=[blk] * 4,
        out_shape=[jax.ShapeDtypeStruct((rows, cols), F32)] * 4,
        compiler_params=_params("arbitrary"),
    )(parts, w, m, v)


def _adam_columns_call(name, parts, w_t, m_t, v_t):
    n_slots, rows, cols = parts.shape
    row_tiles = rows // LANES
    cols_pad = -(-cols // LANES) * LANES

    def body(p_ref, w_ref, m_ref, v_ref, *out_refs):
        for a in range(row_tiles):
            g = p_ref[0, a * LANES:(a + 1) * LANES, :].astype(F32)
            for s in range(1, n_slots):
                g = g + p_ref[s, a * LANES:(a + 1) * LANES, :].astype(F32)
            g = jnp.concatenate([g, jnp.zeros((LANES, cols_pad - cols), F32)], axis=1).T[:cols]
            column_rows = pl.ds(a, cols, stride=row_tiles)
            results = (g,) + _adamw(g, w_ref[column_rows, :], m_ref[column_rows, :], v_ref[column_rows, :])
            for out_ref, val in zip(out_refs, results):
                out_ref[column_rows, :] = val

    vm = pl.BlockSpec(memory_space=pltpu.VMEM)
    return pl.pallas_call(
        body, name=name,
        in_specs=[vm] * 4, out_specs=[vm] * 4,
        out_shape=[jax.ShapeDtypeStruct(w_t.shape, F32)] * 4,
        compiler_params=pltpu.CompilerParams(vmem_limit_bytes=VMEM_LIMIT_BYTES),
    )(parts, w_t, m_t, v_t)


N_PIECES = 8
PIECE = 512
SHARD_COLS = 513
SHARD_PAD = 640
RELAYOUT_ROWS = 256


def _from_shards_call(shards):
    _, d, _ = shards.shape
    tr = RELAYOUT_ROWS

    def body(p_ref, m_ref, s_ref):
        lane = lax.broadcasted_iota(jnp.int32, (tr, SHARD_PAD), 1)
        pad = jnp.zeros((tr, SHARD_PAD - SHARD_COLS), F32)
        sh = [jnp.concatenate([p_ref[s].astype(F32), pad], axis=1) for s in range(N_DEV)]
        for p in range(N_PIECES):
            y = sh[p] if p == 0 else pltpu.roll(sh[p], p, axis=1)
            if p > 0:
                y = jnp.where(lane < p, pltpu.roll(sh[p - 1], SHARD_PAD - (SHARD_COLS - p), axis=1), y)
            m_ref[:, p * PIECE:(p + 1) * PIECE] = y[:, :PIECE].astype(m_ref.dtype)
        first_gate = N_PIECES * PIECE - (N_DEV - 1) * SHARD_COLS
        s_ref[...] = pltpu.roll(sh[N_DEV - 1], SHARD_PAD - first_gate, axis=1)[:, :LANES].astype(s_ref.dtype)

    return pl.pallas_call(
        body, name="w_in_from_shards",
        grid=(d // tr,),
        in_specs=[pl.BlockSpec((N_DEV, tr, SHARD_COLS), lambda i: (0, i, 0))],
        out_specs=[pl.BlockSpec((tr, N_PIECES * PIECE), lambda i: (i, 0)), pl.BlockSpec((tr, LANES), lambda i: (i, 0))],
        out_shape=[jax.ShapeDtypeStruct((d, N_PIECES * PIECE), shards.dtype),
                   jax.ShapeDtypeStruct((d, LANES), shards.dtype)],
        compiler_params=_params("arbitrary"),
    )(shards)


def _adamw(g, w, m, v):
    m_new = ADAM_B1 * m + (1.0 - ADAM_B1) * g
    v_new = ADAM_B2 * v + (1.0 - ADAM_B2) * (g * g)
    m_hat = m_new / (1.0 - ADAM_B1 ** ADAM_STEP)
    v_hat = v_new / (1.0 - ADAM_B2 ** ADAM_STEP)
    return -ADAM_LR * (m_hat / (jnp.sqrt(v_hat) + ADAM_EPS) + ADAM_WD * w), m_new, v_new


def _adam_small_call(parts, ws, ms, vs):
    n = len(ws)
    n_slots = parts.shape[0]

    def body(*refs):
        p_ref = refs[0]
        w_refs, m_refs, v_refs = refs[1:1 + n], refs[1 + n:1 + 2 * n], refs[1 + 2 * n:1 + 3 * n]
        loss_ref = refs[1 + 3 * n]
        outs = refs[2 + 3 * n:]
        g_all = p_ref[0]
        for s in range(1, n_slots):
            g_all = g_all + p_ref[s]
        loss_ref[...] = g_all[n:n + 1, 0:1]
        for r in range(n):
            size = w_refs[r].shape[1]
            g = g_all[r:r + 1, :size]
            delta, m_new, v_new = _adamw(g, w_refs[r][...], m_refs[r][...], v_refs[r][...])
            for kind, val in enumerate((g, delta, m_new, v_new)):
                outs[kind * n + r][...] = val

    vm = pl.BlockSpec(memory_space=pltpu.VMEM)
    shapes = [jax.ShapeDtypeStruct(w.shape, F32) for w in ws]
    return pl.pallas_call(
        body, name="adam_small",
        in_specs=[vm] * (1 + 3 * n), out_specs=[vm] * (1 + 4 * n),
        out_shape=[jax.ShapeDtypeStruct((1, 1), F32)] + shapes * 4,
    )(parts, *ws, *ms, *vs)


_SMALL_ROWS = ("norm1_w", "final_norm_w", "sb_norm_w", "gdn_norm_w", "gdn_A_log", "gdn_dt_bias", "loss")


def _pack_small(vals, width):
    rows = [jnp.pad(a.reshape(1, -1).astype(F32), ((0, 0), (0, width - a.size))) for a in vals]
    rows += [jnp.zeros((1, width), F32)] * (8 - len(rows))
    return jnp.concatenate(rows, axis=0)


def _device_step(x2d, tgt, w_main, w_small, w_out_full, conv_full, norm1_w, sb_norm_w, gdn_A_log, gdn_dt_bias,
                 gdn_norm_w, final_norm_w, distributed=False):
    t_len, d = x2d.shape
    n_chunks = t_len // CHUNK
    w_main, w_small, w_out_full = (a.astype(MXU_DTYPE) for a in (w_main, w_small, w_out_full))
    w_small_t = w_small[:, :2 * GDN_HEADS].T

    pad_lanes = lambda a, lo: jnp.pad(a.reshape(1, -1), ((0, 0), (lo, LANES - lo - a.size)))
    alog_l, dtb_l = pad_lanes(gdn_A_log, GDN_HEADS), pad_lanes(gdn_dt_bias, GDN_HEADS)
    alog_c, dtb_c = alog_l[:, :8].T, dtb_l[:, :8].T
    sbw = jnp.tile(sb_norm_w, (1, 512 // SB_HEAD_DIM))
    gdw = jnp.tile(gdn_norm_w, (1, 512 // GDN_HEAD_DIM))
    fw = final_norm_w.reshape(1, d)

    if distributed:
        proj_cols, proj_gates, ps, pst, h_t, r1, w_out_g, conv_g = _inproj_call(
            x2d, norm1_w, w_main, w_small, w_small_t, gather=(w_out_full, conv_full))
        w_out_full = w_out_g.reshape(d, d)
        conv_full = conv_g.transpose(1, 0, 2).reshape(CONV_WIDTH, N_DEV * conv_g.shape[2])
    else:
        proj_cols, proj_gates, ps, pst, h_t, r1 = _inproj_call(x2d, norm1_w, w_main, w_small, w_small_t)
    o_sb, sp_total, sb_blocks_run = _sb_fwd_call(proj_cols, t_len)
    gact = _gdn_prep_call(proj_cols, conv_full, t_len, after=sb_blocks_run)
    beta_l, gcol_l, grow = _gdn_gates_call(ps, pst, alog_l, dtb_l, alog_c, dtb_c, t_len)
    gam_r = grow[GDN_HEADS:2 * GDN_HEADS].reshape(GDN_HEADS, n_chunks, 1, CHUNK)
    o_gd, *gdn_saved = _gdn_fwd_call(gact, beta_l, gcol_l, gam_r, t_len)

    (dx2, d_osb, d_ogd, dproj8, loss_p, g_fw, g_sbw, g_gdw, g_wout) = _post_call(
        o_sb, o_gd, proj_gates, x2d, tgt, w_out_full, sbw, gdw, fw)

    dproj8 = _sb_bwd_call(proj_cols, sp_total, sb_blocks_run, d_osb, dproj8, t_len)
    if distributed:
        d_gact3, d_gates, g_wout = _gdn_bwd_call(gact, beta_l, gcol_l, gam_r, gdn_saved, d_ogd, t_len,
                                                 scatter=(g_wout.reshape(N_DEV, d // N_DEV, d),))
    else:
        d_gact3, d_gates = _gdn_bwd_call(gact, beta_l, gcol_l, gam_r, gdn_saved, d_ogd, t_len)
    dproj8, g_conv = _gdn_prep_bwd_call(proj_cols, conv_full, d_gact3, dproj8, t_len)
    dsmall, g_alog, g_dtb = _gdn_gates_bwd_call(ps, alog_l, dtb_l, d_gates, t_len)

    if distributed:
        chip_partials = _gw_in_shards_call(h_t, dproj8, dsmall, WIRE_DTYPE)
        fold = lambda a, group: a.reshape(-1, group).sum(axis=0)
        small_g = _pack_small([jnp.zeros((d,), F32), g_fw, fold(g_sbw, SB_HEAD_DIM), fold(g_gdw, GDN_HEAD_DIM),
                               g_alog[0, GDN_HEADS:2 * GDN_HEADS], g_dtb[0, GDN_HEADS:2 * GDN_HEADS],
                               loss_p[0, :1]], d)
        conv_cols = g_conv.shape[1] // N_DEV
        g_conv_parts = g_conv.reshape(CONV_WIDTH, N_DEV, conv_cols).transpose(1, 0, 2)
        grad_x, _, g_w_in, p_small, p_conv = _dx_call(dproj8, dsmall, w_main, w_small, x2d, r1, dx2, norm1_w,
                                                      chip_scatter=(chip_partials,),
                                                      peer_scatter=(small_g, g_conv_parts))
        return grad_x, g_w_in, g_wout, p_small, p_conv
    else:
        grad_x, g_n1 = _dx_call(dproj8, dsmall, w_main, w_small, x2d, r1, dx2, norm1_w)
        g_w_in = (_gw_in_call(h_t, dproj8), _gw_small_call(h_t, dsmall))
    return (loss_p, grad_x, g_n1, g_w_in, g_sbw, g_conv, g_alog, g_dtb, g_gdw, g_wout, g_fw)


def kernel(x, norm1_w, w_in, sb_norm_w, gdn_conv_w, gdn_A_log, gdn_dt_bias, gdn_norm_w, w_out, final_norm_w, loss_target, m_norm1_w, m_w_in, m_sb_norm_w, m_gdn_conv_w, m_gdn_A_log, m_gdn_dt_bias, m_gdn_norm_w, m_w_out, m_final_norm_w, v_norm1_w, v_w_in, v_sb_norm_w, v_gdn_conv_w, v_gdn_A_log, v_gdn_dt_bias, v_gdn_norm_w, v_w_out, v_final_norm_w):
    d = x.shape[2]
    shard_cols = w_in.shape[2]

    (w_in_g,) = _gather_call("gather_weights", [w_in[0].astype(WIRE_DTYPE)])
    w_main, w_small = _from_shards_call(w_in_g)

    grad_x, p_w_in, p_wout, p_small, p_conv = _device_step(
        x[0], loss_target[0], w_main, w_small, w_out[0].astype(WIRE_DTYPE), gdn_conv_w[0], norm1_w, sb_norm_w,
        gdn_A_log, gdn_dt_bias, gdn_norm_w, final_norm_w, distributed=True)

    columns = lambda a: a.transpose(2, 0, 1).reshape(shard_cols * d // LANES, LANES)
    from_columns = lambda a: a.reshape(shard_cols, d // LANES, LANES).transpose(1, 2, 0).reshape(1, d, shard_cols)
    r_w_in = [from_columns(a) for a in _adam_columns_call("adam_w_in", p_w_in, columns(w_in), columns(m_w_in),
                                                          columns(v_w_in))]
    r_wout = _adam_call("adam_w_out", p_wout, w_out[0], m_w_out[0], v_w_out[0], d // N_DEV)
    r_conv = _adam_call("adam_conv", p_conv, gdn_conv_w[0], m_gdn_conv_w[0], v_gdn_conv_w[0], CONV_WIDTH)

    row = lambda a: a.reshape(1, -1)
    n_small = len(_SMALL_ROWS) - 1
    r_small = _adam_small_call(
        p_small,
        [norm1_w, row(final_norm_w), sb_norm_w, gdn_norm_w, gdn_A_log, gdn_dt_bias],
        [m_norm1_w, row(m_final_norm_w), m_sb_norm_w, m_gdn_norm_w, m_gdn_A_log, m_gdn_dt_bias],
        [v_norm1_w, row(v_final_norm_w), v_sb_norm_w, v_gdn_norm_w, v_gdn_A_log, v_gdn_dt_bias])

    def small_out(kind, name):
        out = r_small[1 + kind * n_small + _SMALL_ROWS.index(name)]
        return out.reshape(final_norm_w.shape) if name == "final_norm_w" else out

    def outputs(kind):
        return (small_out(kind, "norm1_w"), r_w_in[kind], small_out(kind, "sb_norm_w"), r_conv[kind][None],
                small_out(kind, "gdn_A_log"), small_out(kind, "gdn_dt_bias"), small_out(kind, "gdn_norm_w"),
                r_wout[kind][None], small_out(kind, "final_norm_w"))

    return (r_small[0][0, 0], grad_x[None], *outputs(0), *outputs(1), *outputs(2), *outputs(3))
```

```python
import functools

import jax
import jax.numpy as jnp
from jax import lax
from jax.experimental import pallas as pl
from jax.experimental.pallas import tpu as pltpu

F32 = jnp.float32
MXU_DTYPE = jnp.bfloat16
WIRE_DTYPE = jnp.bfloat16
EXACT = lax.Precision.HIGHEST
EPS = 1e-6
N_DEV = 8
SB_HEAD_DIM = 64
GDN_HEAD_DIM = 128
GDN_HEADS = 4
GDN_CHUNKS_PER_STEP = 4
GDN_BWD_GROUP = 1
CHUNK = 64
CONV_WIDTH = 4
LANES = 128
SB_BLOCK = 128
SB_BQ = 256
VMEM_LIMIT_BYTES = 56 * 1024 * 1024

PIECE_COLS = 512
PROJ_PIECE_KINDS = ("heads", "heads", "heads", "gate", "heads", "heads", "heads", "gate")
SB_FIRST_BLOCK, GDN_FIRST_BLOCK = 0, 12

DPROJ_PIECE_OF_SLOT = (0, 1, 2, 4, 5, 6, 3, 7)
DPROJ_SB_SLOT, DPROJ_GDN_SLOT, DPROJ_GATE_SLOT = 0, 3, 6

ADAM_LR = 0.001
ADAM_B1 = 0.9
ADAM_B2 = 0.999
ADAM_EPS = 1e-08
ADAM_WD = 0.01
ADAM_STEP = 10

_NN = (((1,), (0,)), ((), ()))
_NT = (((1,), (1,)), ((), ()))
_TN = (((0,), (0,)), ((), ()))
_BNN = (((2,), (1,)), ((0,), (0,)))
_BNT = (((2,), (2,)), ((0,), (0,)))
_BTN = (((1,), (1,)), ((0,), (0,)))


def _mx(a, b):
    return jnp.dot(a, b, precision=EXACT, preferred_element_type=F32)


def _split(x):
    hi = x.astype(MXU_DTYPE)
    return hi, (x - hi.astype(F32)).astype(MXU_DTYPE)


def _m3_general(a, b, dims):
    ah, al = _split(a)
    bh, bl = _split(b)
    dot = lambda x, y: lax.dot_general(x, y, dims, preferred_element_type=F32)
    (contract, _), (batch, _) = dims
    free = [ax for ax in range(a.ndim) if ax not in contract and ax not in batch][0]
    m = a.shape[free]
    both = dot(jnp.concatenate([ah, al], axis=free), bh)
    out_axis = len(batch)
    hi_part = lax.slice_in_dim(both, 0, m, axis=out_axis)
    lo_part = lax.slice_in_dim(both, m, 2 * m, axis=out_axis)
    return hi_part + (dot(ah, bl) + lo_part)


def _times_exact(a, b_exact, dims):
    ah, al = _split(a)
    (contract, _), (batch, _) = dims
    free = [ax for ax in range(a.ndim) if ax not in contract and ax not in batch][0]
    m = a.shape[free]
    both = lax.dot_general(jnp.concatenate([ah, al], axis=free), b_exact.astype(MXU_DTYPE), dims,
                           preferred_element_type=F32)
    out_axis = len(batch)
    return lax.slice_in_dim(both, 0, m, axis=out_axis) + lax.slice_in_dim(both, m, 2 * m, axis=out_axis)


def _exact_times(a_exact, b, dims):
    bh, bl = _split(b)
    n = b.shape[-1]
    both = lax.dot_general(a_exact.astype(MXU_DTYPE), jnp.concatenate([bh, bl], axis=-1), dims,
                           preferred_element_type=F32)
    return both[..., :n] + both[..., n:]


def _sigmoid(z):
    return 1.0 / (1.0 + jnp.exp(-z))


def _softplus(z):
    return jnp.maximum(z, 0.0) + jnp.log(1.0 + jnp.exp(-jnp.abs(z)))


def _params(*semantics):
    return pltpu.CompilerParams(dimension_semantics=semantics, vmem_limit_bytes=VMEM_LIMIT_BYTES)


def _inproj_call(x, norm_w, w_main, w_small, w_small_t, gather=(), tm=256):
    t_len, d = x.shape
    n = w_main.shape[1]
    ns = w_small.shape[1]
    nst = w_small_t.shape[0]
    ng = len(gather)
    steps = t_len // tm

    blocks_per_piece = PIECE_COLS // LANES
    n_gate_cols = PIECE_COLS * PROJ_PIECE_KINDS.count("gate")
    n_col_blocks = blocks_per_piece * PROJ_PIECE_KINDS.count("heads")

    def body(*refs):
        x_ref, nw_ref, wm_ref, ws_ref, wst_ref = refs[:5]
        cols_ref, pz_ref, ps_ref, pst_ref, ht_ref, r_ref = refs[5 + ng:11 + ng]
        copies = lambda: _direct_copies(refs[5:5 + ng], refs[11 + ng:11 + 2 * ng], *refs[11 + 2 * ng:], (False,) * ng)
        if ng:
            pl.when(pl.program_id(0) == 0)(lambda: _start_all(copies()))
        xv = x_ref[...]
        r = lax.rsqrt(jnp.mean(xv * xv, axis=-1, keepdims=True) + EPS)
        h = xv * r * nw_ref[...]
        hb = h.astype(MXU_DTYPE)
        n_block = n_gate = 0
        for piece, kind in enumerate(PROJ_PIECE_KINDS):
            out = jnp.dot(hb, wm_ref[:, piece * PIECE_COLS:(piece + 1) * PIECE_COLS], preferred_element_type=F32)
            if kind == "gate":
                pz_ref[:, n_gate * PIECE_COLS:(n_gate + 1) * PIECE_COLS] = out
                n_gate += 1
            else:
                for j in range(blocks_per_piece):
                    cols_ref[n_block + j] = out[:, j * LANES:(j + 1) * LANES]
                n_block += blocks_per_piece
        ps_ref[...] = jnp.dot(hb, ws_ref[...], preferred_element_type=F32)
        pst_ref[...] = lax.dot_general(wst_ref[...], hb, _NT, preferred_element_type=F32)
        ht_ref[...] = h.T.astype(MXU_DTYPE)
        r_ref[...] = r
        if ng:
            pl.when(pl.program_id(0) == steps - 1)(lambda: _wait_all(copies()))

    return pl.pallas_call(
        body, name="inproj",
        grid=(steps,),
        in_specs=[pl.BlockSpec((tm, d), lambda i: (i, 0)),
                  pl.BlockSpec((1, d), lambda i: (0, 0)),
                  pl.BlockSpec((d, n), lambda i: (0, 0)),
                  pl.BlockSpec((d, ns), lambda i: (0, 0)),
                  pl.BlockSpec((nst, d), lambda i: (0, 0))] + [_HBM] * ng,
        out_specs=[pl.BlockSpec((n_col_blocks, tm, LANES), lambda i: (0, i, 0)),
                   pl.BlockSpec((tm, n_gate_cols), lambda i: (i, 0)),
                   pl.BlockSpec((tm, ns), lambda i: (i, 0)),
                   pl.BlockSpec((nst, tm), lambda i: (0, i)),
                   pl.BlockSpec((d, tm), lambda i: (0, i)),
                   pl.BlockSpec((tm, 1), lambda i: (i, 0))] + [_HBM] * ng,
        out_shape=[jax.ShapeDtypeStruct((n_col_blocks, t_len, LANES), F32),
                   jax.ShapeDtypeStruct((t_len, n_gate_cols), F32),
                   jax.ShapeDtypeStruct((t_len, ns), F32),
                   jax.ShapeDtypeStruct((nst, t_len), F32),
                   jax.ShapeDtypeStruct((d, t_len), MXU_DTYPE),
                   jax.ShapeDtypeStruct((t_len, 1), F32)] + _direct_out_shapes(gather, (False,) * ng),
        scratch_shapes=_direct_semaphores(ng) if ng else [],
        compiler_params=_params("arbitrary"),
    )(x, norm_w, w_main, w_small, w_small_t, *gather)


def _running_sum_mm(x, tri):
    hi = x.astype(MXU_DTYPE)
    lo = (x - hi.astype(F32)).astype(MXU_DTYPE)
    return jnp.dot(hi, tri, preferred_element_type=F32) + jnp.dot(lo, tri, preferred_element_type=F32)


def _col_block(t_len, first):
    return pl.BlockSpec((1, t_len, LANES), lambda p: (first + p, 0, 0))


def _sb_iotas():
    row_i = lax.broadcasted_iota(jnp.int32, (SB_BQ, SB_BLOCK), 0)
    col_i = lax.broadcasted_iota(jnp.int32, (SB_BQ, SB_BLOCK), 1)
    sq_r = lax.broadcasted_iota(jnp.int32, (SB_BLOCK, SB_BLOCK), 0)
    sq_c = lax.broadcasted_iota(jnp.int32, (SB_BLOCK, SB_BLOCK), 1)
    return row_i, col_i, sq_r, sq_c


SB_DIAG_BLOCKS = SB_BQ // SB_BLOCK
SB_EXP_FLOOR = -110.0


def _sb_keys_descending(qi, tile, carry, z_bounds, n_heads, has_free):
    group = SB_DIAG_BLOCKS
    n_free = group * qi
    diag = list(range(group - 1, -1, -1))
    carry = tile([n_free + j for j in diag], [True] * group, carry, [j * SB_BLOCK for j in diag])

    def largest_exponent(c):
        worst = jnp.max(z_bounds[0] - c[1])
        for h in range(1, n_heads):
            worst = jnp.maximum(worst, jnp.max(z_bounds[h] - c[1 + h]))
        return worst

    always = group if has_free else 0

    def cond(state):
        return (state[0] < n_free) & ((state[1] > SB_EXP_FLOOR) | (state[0] < always))

    def body(state):
        first = n_free - 1 - state[0]
        c = tile([first - j for j in range(group)], [False] * group, state[2:])
        return (state[0] + group, largest_exponent(c), *c)

    out = lax.while_loop(cond, body, (jnp.int32(0), largest_exponent(carry), *carry))
    return out[2:], out[0]


def _sb_keys_ascending(qi, n_run, tile, carry, has_free):
    group = SB_DIAG_BLOCKS
    n_free = group * qi
    diag = list(range(group))
    kjs, los, masked = [n_free + j for j in diag], [j * SB_BLOCK for j in diag], [True] * group
    if has_free:
        early = lambda s: [n_free - n_run + group * s + j for j in range(group)]
        carry = lax.fori_loop(0, n_run // group - 1, lambda s, c: tile(early(s), [False] * group, c), carry)
        kjs, los, masked = [n_free - group + j for j in range(group)] + kjs, [0] * group + los, [False] * group + masked
    return tile(kjs, masked, carry, los)


def _sb_fwd_call(cols, t_len):
    nq = t_len // SB_BQ
    scale = float(SB_HEAD_DIM) ** -0.5
    n_pairs = 512 // LANES
    per_pair = LANES // SB_HEAD_DIM

    def body(q_blk, k_blk, v_blk, o_blk, st_ref, nrun_ref):
        q_ref, k_ref, v_ref, o_ref = q_blk.at[0], k_blk.at[0], v_blk.at[0], o_blk.at[0]
        lane = lax.broadcasted_iota(jnp.int32, (1, LANES), 1)
        row_i, col_i, sq_r, sq_c = _sb_iotas()
        ge = (sq_r >= sq_c).astype(MXU_DTYPE)
        hms = [((lane // SB_HEAD_DIM) == hh).astype(F32) for hh in range(per_pair)]
        k_sq = k_ref[...] * k_ref[...]
        k_norms = [jnp.sqrt(jnp.max(jnp.sum(k_sq * hm, axis=-1, keepdims=True))) * (1.02 * scale) for hm in hms]

        def q_block(qi, has_free):
            r0 = qi * SB_BQ if isinstance(qi, int) else pl.multiple_of(qi * SB_BQ, SB_BQ)
            rows = pl.ds(r0, SB_BQ)
            q_all = q_ref[rows, :]
            qms = [(q_all * (hm * scale)).astype(MXU_DTYPE) for hm in hms]
            z_bounds = [jnp.sqrt(jnp.sum(q_all * q_all * hm, axis=-1, keepdims=True)) * kn
                        for hm, kn in zip(hms, k_norms)]

            def tile(kjs, masked, kc, los=None):
                heads = range(per_pair)
                los = los or [0] * len(kjs)
                pairs = [(t, h) for t in range(len(kjs)) for h in heads]
                add_rows = lambda full, lo, part: full + part if lo == 0 else jnp.concatenate(
                    [full[:lo], full[lo:] + part], axis=0)
                acc, cs = kc[0], list(kc[1:])
                s0s = [kj * SB_BLOCK if isinstance(kj, int) else pl.multiple_of(kj * SB_BLOCK, SB_BLOCK) for kj in kjs]
                kbs = [k_ref[pl.ds(s0, SB_BLOCK), :].astype(MXU_DTYPE) for s0 in s0s]
                v_alls = [v_ref[pl.ds(s0, SB_BLOCK), :] for s0 in s0s]
                vms = {(t, h): (v_alls[t] * hms[h]).astype(MXU_DTYPE) for t, h in pairs}
                zs = {(t, h): lax.dot_general(qms[h][los[t]:], kbs[t], _NT, preferred_element_type=F32)
                      for t, h in pairs}
                masks = [(col_i[lo:] + s0) < (row_i[lo:] + r0) if m else None for m, lo, s0 in zip(masked, los, s0s)]
                keep = lambda t, a: a if masks[t] is None else jnp.where(masks[t], a, 0.0)
                sps = {(t, h): keep(t, _softplus(zs[t, h])) for t, h in pairs}
                sums = {p: _running_sum_mm(sps[p], ge) for p in pairs}
                mass = {}
                for t, h in pairs:
                    mass[t, h] = cs[h] if t == 0 else add_rows(
                        mass[t - 1, h], los[t - 1], jnp.sum(sps[t - 1, h], axis=-1, keepdims=True))
                ws = {(t, h): keep(t, jnp.exp(zs[t, h] - (sums[t, h] + mass[t, h][los[t]:]))) for t, h in pairs}
                for t, h in pairs:
                    acc = add_rows(acc, los[t], jnp.dot(ws[t, h].astype(MXU_DTYPE), vms[t, h],
                                                        preferred_element_type=F32))
                last = len(kjs) - 1
                cs = [add_rows(mass[last, h], los[last], jnp.sum(sps[last, h], axis=-1, keepdims=True)) for h in heads]
                return (acc, *cs)

            zero_col = jnp.zeros((SB_BQ, 1), F32)
            out, n_run = _sb_keys_descending(
                qi, tile, (jnp.zeros((SB_BQ, LANES), F32),) + (zero_col,) * per_pair, z_bounds, per_pair, has_free)
            o_ref[rows, :] = out[0]
            masses = jnp.zeros((SB_BQ, LANES), F32)
            for hh in range(per_pair):
                masses = jnp.where(lane == hh, out[1 + hh], masses)
            st_ref[rows, :] = masses
            nrun_ref[pl.program_id(0), qi] = n_run

        q_block(0, False)
        lax.fori_loop(1, nq, lambda qi, carry: (q_block(qi, True), carry)[1], 0)

    return pl.pallas_call(
        body, name="sb_fwd",
        grid=(n_pairs,),
        in_specs=[_col_block(t_len, SB_FIRST_BLOCK), _col_block(t_len, SB_FIRST_BLOCK + n_pairs),
                  _col_block(t_len, SB_FIRST_BLOCK + 2 * n_pairs)],
        out_specs=[_col_block(t_len, 0),
                   pl.BlockSpec((t_len, LANES), lambda p: (0, p)),
                   pl.BlockSpec(memory_space=pltpu.SMEM)],
        out_shape=[jax.ShapeDtypeStruct((n_pairs, t_len, LANES), F32),
                   jax.ShapeDtypeStruct((t_len, n_pairs * LANES), F32),
                   jax.ShapeDtypeStruct((n_pairs, nq), jnp.int32)],
        compiler_params=_params("arbitrary"),
    )(cols, cols, cols)


def _sb_bwd_call(cols, sp_total, n_run_all, d_o, dproj, t_len):
    nq = t_len // SB_BQ
    scale = float(SB_HEAD_DIM) ** -0.5
    n_pairs = 512 // LANES
    per_pair = LANES // SB_HEAD_DIM

    def body(q_blk, k_blk, v_blk, st_ref, nrun_ref, do_blk, dproj_in_ref, d_ref):
        q_ref, k_ref, v_ref, do_ref = q_blk.at[0], k_blk.at[0], v_blk.at[0], do_blk.at[0]
        lane = lax.broadcasted_iota(jnp.int32, (1, LANES), 1)
        row_i, col_i, sq_r, sq_c = _sb_iotas()
        lt = (sq_r < sq_c).astype(MXU_DTYPE)
        le = (sq_r <= sq_c).astype(MXU_DTYPE)
        hms = [((lane // SB_HEAD_DIM) == hh).astype(F32) for hh in range(per_pair)]
        d_ref[1] = jnp.zeros((t_len, LANES), F32)
        d_ref[2] = jnp.zeros((t_len, LANES), F32)

        def q_block(qi, has_free):
            r0 = qi * SB_BQ if isinstance(qi, int) else pl.multiple_of(qi * SB_BQ, SB_BQ)
            rows = pl.ds(r0, SB_BQ)
            q_all, do_all = q_ref[rows, :], do_ref[rows, :]
            qms = [(q_all * (hm * scale)).astype(MXU_DTYPE) for hm in hms]
            doms = [(do_all * hm).astype(MXU_DTYPE) for hm in hms]
            masses = st_ref[rows, :]
            totals = [jnp.sum(jnp.where(lane == hh, masses, 0.0), axis=-1, keepdims=True) for hh in range(per_pair)]

            def tile(kjs, masked, kc, los=None):
                heads = range(per_pair)
                los = los or [0] * len(kjs)
                pairs = [(t, h) for t in range(len(kjs)) for h in heads]
                add_rows = lambda full, lo, part: full + part if lo == 0 else jnp.concatenate(
                    [full[:lo], full[lo:] + part], axis=0)
                rsum = lambda a: jnp.sum(a, axis=-1, keepdims=True)
                dq, cls, gls = kc[0], list(kc[1:1 + per_pair]), list(kc[1 + per_pair:])
                s0s = [kj * SB_BLOCK if isinstance(kj, int) else pl.multiple_of(kj * SB_BLOCK, SB_BLOCK) for kj in kjs]
                k_alls = [k_ref[pl.ds(s0, SB_BLOCK), :] for s0 in s0s]
                v_alls = [v_ref[pl.ds(s0, SB_BLOCK), :] for s0 in s0s]
                kbs = [k_all.astype(MXU_DTYPE) for k_all in k_alls]
                vms = {(t, h): (v_alls[t] * hms[h]).astype(MXU_DTYPE) for t, h in pairs}
                kms = {(t, h): (k_alls[t] * (hms[h] * scale)).astype(MXU_DTYPE) for t, h in pairs}
                q_live = {(t, h): qms[h][los[t]:] for t, h in pairs}
                do_live = {(t, h): doms[h][los[t]:] for t, h in pairs}
                zs = {p: lax.dot_general(q_live[p], kbs[p[0]], _NT, preferred_element_type=F32) for p in pairs}
                das = {p: lax.dot_general(do_live[p], vms[p], _NT, preferred_element_type=F32) for p in pairs}
                masks = [(col_i[lo:] + s0) < (row_i[lo:] + r0) if m else None for m, lo, s0 in zip(masked, los, s0s)]
                keep = lambda t, a: a if masks[t] is None else jnp.where(masks[t], a, 0.0)
                sp_alls = {p: _softplus(zs[p]) for p in pairs}
                sps = {(t, h): keep(t, sp_alls[t, h]) for t, h in pairs}
                lefts = {p: _running_sum_mm(sps[p], lt) for p in pairs}
                cl = {}
                for t, h in pairs:
                    cl[t, h] = cls[h] if t == 0 else add_rows(cl[t - 1, h], los[t - 1], rsum(sps[t - 1, h]))
                ws = {(t, h): keep(t, jnp.exp(zs[t, h] - ((totals[h] - cl[t, h])[los[t]:] - lefts[t, h])))
                      for t, h in pairs}
                gs = {p: das[p] * ws[p] for p in pairs}
                g_sums = {p: _running_sum_mm(gs[p], le) for p in pairs}
                gl = {}
                for t, h in pairs:
                    gl[t, h] = gls[h] if t == 0 else add_rows(gl[t - 1, h], los[t - 1], rsum(gs[t - 1, h]))
                dzs = {(t, h): keep(t, gs[t, h] - jnp.exp(zs[t, h] - sp_alls[t, h]) * (gl[t, h][los[t]:] + g_sums[t, h])
                               ).astype(MXU_DTYPE) for t, h in pairs}
                for t in range(len(kjs)):
                    dk_t = jnp.zeros((SB_BLOCK, LANES), F32)
                    dv_t = jnp.zeros((SB_BLOCK, LANES), F32)
                    for h in heads:
                        dq = add_rows(dq, los[t], jnp.dot(dzs[t, h], kms[t, h], preferred_element_type=F32))
                        dk_t = dk_t + lax.dot_general(dzs[t, h], q_live[t, h], _TN, preferred_element_type=F32)
                        dv_t = dv_t + lax.dot_general(ws[t, h].astype(MXU_DTYPE), do_live[t, h], _TN,
                                                      preferred_element_type=F32)
                    d_ref[1, pl.ds(s0s[t], SB_BLOCK), :] += dk_t
                    d_ref[2, pl.ds(s0s[t], SB_BLOCK), :] += dv_t
                last = len(kjs) - 1
                cls = [add_rows(cl[last, h], los[last], rsum(sps[last, h])) for h in heads]
                gls = [add_rows(gl[last, h], los[last], rsum(gs[last, h])) for h in heads]
                return (dq, *cls, *gls)

            zero_col = jnp.zeros((SB_BQ, 1), F32)
            out = _sb_keys_ascending(qi, nrun_ref[pl.program_id(0), qi], tile,
                                     (jnp.zeros((SB_BQ, LANES), F32),) + (zero_col,) * (2 * per_pair), has_free)
            d_ref[0, rows, :] = out[0]

        q_block(0, False)
        lax.fori_loop(1, nq, lambda qi, carry: (q_block(qi, True), carry)[1], 0)

    return pl.pallas_call(
        body, name="sb_bwd",
        grid=(n_pairs,),
        in_specs=[_col_block(t_len, SB_FIRST_BLOCK), _col_block(t_len, SB_FIRST_BLOCK + n_pairs),
                  _col_block(t_len, SB_FIRST_BLOCK + 2 * n_pairs),
                  pl.BlockSpec((t_len, LANES), lambda p: (0, p)),
                  pl.BlockSpec(memory_space=pltpu.SMEM), _col_block(t_len, 0), _HBM],
        out_specs=pl.BlockSpec((3, t_len, LANES), lambda p: (DPROJ_SB_SLOT // 3, 0, p)),
        out_shape=jax.ShapeDtypeStruct(dproj.shape, dproj.dtype),
        input_output_aliases={6: 0},
        compiler_params=_params("arbitrary"),
    )(cols, cols, cols, sp_total, n_run_all, d_o, dproj)


def _conv_taps(xin, rows, t_len):
    taps = []
    for i in range(CONV_WIDTH):
        shift = CONV_WIDTH - 1 - i
        if shift == 0:
            taps.append(xin)
        else:
            taps.append(jnp.where(rows >= shift, pltpu.roll(xin, shift, axis=0), 0.0))
    return taps


def _gdn_prep_body_common(x_ref, w_ref, t_len):
    j = pl.program_id(0)
    xin = x_ref[...]
    rows = lax.broadcasted_iota(jnp.int32, (t_len, LANES), 0)
    taps = _conv_taps(xin, rows, t_len)
    pre = taps[0] * w_ref[0:1, :]
    for i in range(1, CONV_WIDTH):
        pre = pre + taps[i] * w_ref[i:i + 1, :]
    sg = _sigmoid(pre)
    act = pre * sg
    is_qk = j < 2 * GDN_HEADS
    nrm = jnp.where(is_qk, lax.rsqrt(jnp.sum(act * act, axis=-1, keepdims=True) + EPS), 1.0)
    sc = jnp.where(j < GDN_HEADS, float(GDN_HEAD_DIM) ** -0.5, 1.0)
    return j, rows, taps, pre, sg, act, is_qk, nrm, sc


def _gdn_prep_call(cols, conv_w, t_len, after):
    def body(x_blk, w_ref, after_ref, out_ref):
        _, _, _, _, _, act, _, nrm, sc = _gdn_prep_body_common(x_blk.at[0], w_ref, t_len)
        out_ref[...] = act * nrm * sc

    return pl.pallas_call(
        body, name="gdn_prep",
        grid=(3 * GDN_HEADS,),
        in_specs=[_col_block(t_len, GDN_FIRST_BLOCK),
                  pl.BlockSpec((CONV_WIDTH, LANES), lambda j: (0, j)),
                  pl.BlockSpec(memory_space=pltpu.SMEM)],
        out_specs=pl.BlockSpec((t_len, LANES), lambda j: (0, j)),
        out_shape=jax.ShapeDtypeStruct((t_len, 3 * 512), F32),
        compiler_params=_params("arbitrary"),
    )(cols, conv_w, after)


def _gdn_prep_bwd_call(cols, conv_w, d_act3, dproj, t_len):
    def body(x_blk, w_ref, d_ref, dproj_in_ref, dx_ref, dw_ref):
        _, rows, taps, pre, sg, act, is_qk, nrm, sc = _gdn_prep_body_common(x_blk.at[0], w_ref, t_len)
        d_out = d_ref[0]
        dn = d_out * sc
        d_norm = nrm * dn - act * (nrm * nrm * nrm) * jnp.sum(dn * act, axis=-1, keepdims=True)
        d_act = jnp.where(is_qk, d_norm, d_out)
        d_pre = d_act * sg * (1.0 + pre * (1.0 - sg))
        dx = d_pre * w_ref[CONV_WIDTH - 1:CONV_WIDTH, :]
        dw_ref[CONV_WIDTH - 1:CONV_WIDTH, :] = jnp.sum(d_pre * taps[CONV_WIDTH - 1], axis=0, keepdims=True)
        for i in range(CONV_WIDTH - 1):
            shift = CONV_WIDTH - 1 - i
            up = jnp.where(rows < t_len - shift, pltpu.roll(d_pre, t_len - shift, axis=0), 0.0)
            dx = dx + up * w_ref[i:i + 1, :]
            dw_ref[i:i + 1, :] = jnp.sum(d_pre * taps[i], axis=0, keepdims=True)
        dx_ref[0] = dx

    return pl.pallas_call(
        body, name="gdn_prep_bwd",
        grid=(3 * GDN_HEADS,),
        in_specs=[_col_block(t_len, GDN_FIRST_BLOCK),
                  pl.BlockSpec((CONV_WIDTH, LANES), lambda j: (0, j)),
                  pl.BlockSpec((1, t_len, LANES), lambda j: (j // GDN_HEADS, 0, j % GDN_HEADS)), _HBM],
        out_specs=[pl.BlockSpec((1, t_len, LANES), lambda j: (DPROJ_GDN_SLOT + j // GDN_HEADS, 0, j % GDN_HEADS)),
                   pl.BlockSpec((CONV_WIDTH, LANES), lambda j: (0, j))],
        out_shape=[jax.ShapeDtypeStruct(dproj.shape, dproj.dtype),
                   jax.ShapeDtypeStruct((CONV_WIDTH, 3 * 512), F32)],
        input_output_aliases={3: 0},
        compiler_params=_params("arbitrary"),
    )(cols, conv_w, d_act3, dproj)


def _chunk_cumsum_matrix():
    r = lax.broadcasted_iota(jnp.int32, (LANES, LANES), 0)
    c = lax.broadcasted_iota(jnp.int32, (LANES, LANES), 1)
    return ((r <= c) & ((r // CHUNK) == (c // CHUNK))).astype(F32)


def _gdn_gates_call(ps, pst, alog_l, dtb_l, alog_c, dtb_c, t_len):
    def body(ps_ref, pst_ref, al_ref, dl_ref, ac_ref, dc_ref, beta_ref, gcol_ref, grow_ref):
        upper = _chunk_cumsum_matrix()
        lower = upper.T
        psv = ps_ref[...]
        beta_ref[...] = _sigmoid(psv)
        g_l = -jnp.exp(al_ref[...]) * _softplus(psv + dl_ref[...])
        g_r = -jnp.exp(ac_ref[...]) * _softplus(pst_ref[...] + dc_ref[...])
        for w in range(t_len // LANES):
            sl = slice(w * LANES, (w + 1) * LANES)
            gcol_ref[sl, :] = _mx(lower, g_l[sl, :])
            grow_ref[:, sl] = _mx(g_r[:, sl], upper)

    vm = pl.BlockSpec(memory_space=pltpu.VMEM)
    return pl.pallas_call(
        body, name="gdn_gates",
        in_specs=[vm] * 6, out_specs=[vm] * 3,
        out_shape=[jax.ShapeDtypeStruct((t_len, LANES), F32),
                   jax.ShapeDtypeStruct((t_len, LANES), F32),
                   jax.ShapeDtypeStruct((8, t_len), F32)],
        compiler_params=pltpu.CompilerParams(vmem_limit_bytes=VMEM_LIMIT_BYTES),
    )(ps, pst, alog_l, dtb_l, alog_c, dtb_c)


def _gdn_gates_bwd_call(ps, alog_l, dtb_l, d_l, t_len):
    def body(ps_ref, al_ref, dl_ref, d_ref, dps_ref, gal_ref, gdt_ref):
        lane = lax.broadcasted_iota(jnp.int32, (1, LANES), 1)
        psv = ps_ref[...]
        dv = d_ref[...]
        beta = _sigmoid(psv)
        ea = jnp.exp(al_ref[...])
        arg = psv + dl_ref[...]
        g = -ea * _softplus(arg)
        d_a = dv * (-ea) * _sigmoid(arg)
        is_a = (lane >= GDN_HEADS) & (lane < 2 * GDN_HEADS)
        dps_ref[...] = jnp.where(lane < GDN_HEADS, dv * beta * (1.0 - beta), jnp.where(is_a, d_a, 0.0))
        gdt_ref[...] = jnp.where(is_a, jnp.sum(d_a, axis=0, keepdims=True), 0.0)
        gal_ref[...] = jnp.where(is_a, jnp.sum(dv * g, axis=0, keepdims=True), 0.0)

    vm = pl.BlockSpec(memory_space=pltpu.VMEM)
    return pl.pallas_call(
        body, name="gdn_gates_bwd",
        in_specs=[vm] * 4, out_specs=[vm] * 3,
        out_shape=[jax.ShapeDtypeStruct((t_len, LANES), F32),
                   jax.ShapeDtypeStruct((1, LANES), F32),
                   jax.ShapeDtypeStruct((1, LANES), F32)],
        compiler_params=pltpu.CompilerParams(vmem_limit_bytes=VMEM_LIMIT_BYTES),
    )(ps, alog_l, dtb_l, d_l)


def _bm(a, b):
    return _m3_general(a, b, _BNN)


def _bm_nt(a, b):
    return _m3_general(a, b, _BNT)


def _bm_tn(a, b):
    return _m3_general(a, b, _BTN)


def _heads_of(ref, rows):
    return jnp.stack([ref[rows, h * GDN_HEAD_DIM:(h + 1) * GDN_HEAD_DIM] for h in range(GDN_HEADS)])


def _chunk_terms(q_ref, k_ref, v_ref, b_ref, gc_ref, gr_ref, c, incl, strict, n=1, scores=True):
    r0 = c * CHUNK if isinstance(c, int) else pl.multiple_of(c * CHUNK, CHUNK)
    rows = pl.ds(r0, n * CHUNK)
    per_chunk = lambda x: x.reshape(GDN_HEADS * n, CHUNK, x.shape[-1])
    q, k, v = (per_chunk(_heads_of(ref, rows)) for ref in (q_ref, k_ref, v_ref))
    lane_ids = lax.broadcasted_iota(jnp.int32, (1, LANES), 1)
    pick = lambda slab, first: jnp.stack([jnp.sum(jnp.where(lane_ids == first + h, slab, 0.0), axis=-1, keepdims=True)
                                          for h in range(GDN_HEADS)])
    b = per_chunk(pick(b_ref[rows, :], 0))
    gc = per_chunk(pick(gc_ref[rows, :], GDN_HEADS))
    gr = gr_ref[:, c] if n == 1 else gr_ref[:, c:c + n].reshape(GDN_HEADS * n, 1, CHUNK)
    dm = jnp.where(incl, jnp.exp(jnp.where(incl, gc - gr, 0.0)), 0.0)
    kb = k * b
    vb = v * b
    e = jnp.exp(gc)
    a = p = None
    if scores:
        kk_qk = _bm_nt(jnp.concatenate([kb, q], axis=1), k)
        a = jnp.where(strict, kk_qk[:, :CHUNK] * dm, 0.0)
        p = jnp.where(incl, kk_qk[:, CHUNK:] * dm, 0.0)
    gl = gc[:, CHUNK - 1:CHUNK, :]
    eg = jnp.exp(gl - gc)
    return rows, q, k, v, b, gc, dm, kb, vb, e, a, p, gl, eg


def _unit_lower_inverse(a, eye):
    x = -a
    tm = eye + x
    xp = _bm(x, x)
    for _ in range(4):
        both = _bm(jnp.concatenate([xp, tm], axis=1), xp)
        tm = tm + both[:, CHUNK:]
        xp = both[:, :CHUNK]
    return tm + _bm(tm, xp)


def _gdn_specs(t_len, n_chunks, reverse):
    cps = GDN_CHUNKS_PER_STEP
    steps = n_chunks // cps
    at = (lambda g: steps - 1 - g) if reverse else (lambda g: g)
    rows_blk = lambda width, part=0: pl.BlockSpec((cps * CHUNK, width), lambda g: (at(g), part))
    gate_r = pl.BlockSpec((GDN_HEADS, cps, 1, CHUNK), lambda g: (0, at(g), 0, 0))
    per_chunk = lambda r, c: pl.BlockSpec((GDN_HEADS, cps, r, c), lambda g: (0, at(g), 0, 0))
    return cps, steps, rows_blk, gate_r, per_chunk


def _gdn_fwd_call(gact, beta_c, gam_c, gam_r, t_len):
    n_chunks = t_len // CHUNK
    dk = GDN_HEAD_DIM
    width = GDN_HEADS * dk
    cps, steps, rows_blk, gate_r, per_chunk = _gdn_specs(t_len, n_chunks, False)

    def body(q_ref, k_ref, v_ref, b_ref, gc_ref, gr_ref, o_ref, s_ref, t_ref, a_ref, p_ref, uw_ref, vn_ref, state_ref):
        row = lax.broadcasted_iota(jnp.int32, (CHUNK, CHUNK), 0)
        col = lax.broadcasted_iota(jnp.int32, (CHUNK, CHUNK), 1)
        incl, strict = row >= col, row > col
        eye = (row == col).astype(F32)

        @pl.when(pl.program_id(0) == 0)
        def _():
            state_ref[...] = jnp.zeros_like(state_ref)

        _, q, k, v, b, gc, dm, kb, vb, e, a, p, gl, eg = _chunk_terms(
            q_ref, k_ref, v_ref, b_ref, gc_ref, gr_ref, 0, incl, strict, cps)
        tm = _unit_lower_inverse(a, eye)
        uw = _bm(tm, jnp.concatenate([vb, kb * e], axis=2))
        w_qe = jnp.concatenate([uw[:, :, dk:], q * e], axis=1)
        u, kd, decay = uw[:, :, :dk], k * eg, jnp.exp(gl)
        per_chunk_block = lambda x: x.reshape(GDN_HEADS, cps, CHUNK, CHUNK)
        t_ref[...], a_ref[...], p_ref[...] = per_chunk_block(tm), per_chunk_block(a), per_chunk_block(p)
        uw_heads = uw.reshape(GDN_HEADS, cps * CHUNK, 2 * dk)
        for h in range(GDN_HEADS):
            uw_ref[:, h * 2 * dk:(h + 1) * 2 * dk] = uw_heads[h]

        of_chunk = lambda x, c: jnp.stack([x[h * cps + c] for h in range(GDN_HEADS)])
        s = state_ref[...]
        for c in range(cps):
            ws_qs = _bm(of_chunk(w_qe, c), s)
            vn = of_chunk(u, c) - ws_qs[:, :CHUNK]
            o = ws_qs[:, CHUNK:] + _bm(of_chunk(p, c), vn)
            for h in range(GDN_HEADS):
                o_ref[c * CHUNK:(c + 1) * CHUNK, h * dk:(h + 1) * dk] = o[h]
                vn_ref[c * CHUNK:(c + 1) * CHUNK, h * dk:(h + 1) * dk] = vn[h]
            s_ref[:, c] = s
            s = s * of_chunk(decay, c) + _bm_tn(of_chunk(kd, c), vn)
        state_ref[...] = s

    scores = jax.ShapeDtypeStruct((GDN_HEADS, n_chunks, CHUNK, CHUNK), F32)
    return pl.pallas_call(
        body, name="gdn_fwd",
        grid=(steps,),
        in_specs=[rows_blk(width, 0), rows_blk(width, 1), rows_blk(width, 2), rows_blk(LANES), rows_blk(LANES), gate_r],
        out_specs=[rows_blk(width), per_chunk(dk, dk), per_chunk(CHUNK, CHUNK), per_chunk(CHUNK, CHUNK),
                   per_chunk(CHUNK, CHUNK), rows_blk(2 * width), rows_blk(width)],
        out_shape=[jax.ShapeDtypeStruct((t_len, width), F32),
                   jax.ShapeDtypeStruct((GDN_HEADS, n_chunks, dk, dk), F32), scores, scores, scores,
                   jax.ShapeDtypeStruct((t_len, 2 * width), F32), jax.ShapeDtypeStruct((t_len, width), F32)],
        scratch_shapes=[pltpu.VMEM((GDN_HEADS, dk, dk), F32)],
        compiler_params=_params("arbitrary"),
    )(gact, gact, gact, beta_c, gam_c, gam_r)


def _gdn_bwd_call(gact, beta_c, gam_c, gam_r, saved, d_o, t_len, scatter=()):
    n_chunks = t_len // CHUNK
    dk = GDN_HEAD_DIM
    width = GDN_HEADS * dk
    cps, steps, rows_blk, gate_r, per_chunk = _gdn_specs(t_len, n_chunks, True)
    nx = len(scatter)
    n_in = 13

    def body(*refs):
        q_ref, k_ref, v_ref, b_ref, gc_ref, gr_ref = refs[:6]
        saved_refs, do_ref = refs[6:12], refs[12]
        d_ref, dgate_ref = refs[n_in + nx:n_in + 2 + nx]
        dstate_ref = refs[n_in + 2 + 2 * nx]
        copies = lambda: _direct_copies(refs[n_in:n_in + nx], refs[n_in + 2 + nx:n_in + 2 + 2 * nx],
                                        *refs[n_in + 3 + 2 * nx:], (True,) * nx)
        if nx:
            pl.when(pl.program_id(0) == 0)(lambda: _start_all(copies()))
        row = lax.broadcasted_iota(jnp.int32, (CHUNK, CHUNK), 0)
        col = lax.broadcasted_iota(jnp.int32, (CHUNK, CHUNK), 1)
        incl, strict = row >= col, row > col
        ng = GDN_BWD_GROUP
        nb = GDN_HEADS * ng
        upper = jnp.broadcast_to((row <= col).astype(F32), (nb, CHUNK, CHUNK))
        ones = jnp.ones((nb, CHUNK, LANES), F32)
        last_row = lax.broadcasted_iota(jnp.int32, (CHUNK, 1), 0) == CHUNK - 1
        lane_ids = lax.broadcasted_iota(jnp.int32, (1, LANES), 1)
        rsum = lambda m: jnp.sum(m, axis=-1, keepdims=True)
        total = lambda m: jnp.sum(rsum(m), axis=1, keepdims=True)
        of_chunk = lambda x, c: jnp.stack([x[h * ng + c] for h in range(GDN_HEADS)])

        @pl.when(pl.program_id(0) == 0)
        def _():
            dstate_ref[...] = jnp.zeros_like(dstate_ref)

        for c0 in range(cps - ng, -1, -ng):
            group(c0, q_ref, k_ref, v_ref, b_ref, gc_ref, gr_ref, saved_refs, do_ref, d_ref, dgate_ref, dstate_ref,
                  incl, strict, upper, ones, last_row, lane_ids, rsum, total, of_chunk)
        if nx:
            pl.when(pl.program_id(0) == steps - 1)(lambda: _wait_all(copies()))

    def group(c0, q_ref, k_ref, v_ref, b_ref, gc_ref, gr_ref, saved_refs, do_ref, d_ref, dgate_ref, dstate_ref,
              incl, strict, upper, ones, last_row, lane_ids, rsum, total, of_chunk):
        ng = GDN_BWD_GROUP
        nb = GDN_HEADS * ng
        rows = pl.ds(c0 * CHUNK, ng * CHUNK)
        s_ref, t_ref, a_ref, p_ref, uw_ref, vn_ref = saved_refs
        _, q, k, v, b, gc, dm, kb, vb, e, _, _, gl, eg = _chunk_terms(
            q_ref, k_ref, v_ref, b_ref, gc_ref, gr_ref, c0, incl, strict, ng, scores=False)
        s = s_ref[:, c0:c0 + ng].reshape(nb, dk, dk)
        tm = t_ref[:, c0:c0 + ng].reshape(nb, CHUNK, CHUNK)
        a = a_ref[:, c0:c0 + ng].reshape(nb, CHUNK, CHUNK)
        p = p_ref[:, c0:c0 + ng].reshape(nb, CHUNK, CHUNK)
        d_out = _heads_of(do_ref, rows).reshape(nb, CHUNK, dk)
        vn = _heads_of(vn_ref, rows).reshape(nb, CHUNK, dk)
        uw = jnp.stack([uw_ref[rows, h * 2 * dk:(h + 1) * 2 * dk] for h in range(GDN_HEADS)]).reshape(nb, CHUNK, 2 * dk)
        u, w = uw[:, :, :dk], uw[:, :, dk:]
        el = jnp.exp(gl)
        kbe = kb * e
        qe = q * e
        kd = k * eg
        pt_do = _bm_tn(p, d_out)
        qet_do = _bm_tn(qe, d_out)

        ds = dstate_ref[...]
        d_vn_c, ds_c = [None] * ng, [None] * ng
        for c in range(ng - 1, -1, -1):
            ds_c[c] = ds
            d_vn_c[c] = of_chunk(pt_do, c) + _bm(of_chunk(kd, c), ds)
            ds = of_chunk(el, c) * ds + of_chunk(qet_do, c) - _bm_tn(of_chunk(w, c), d_vn_c[c])
        dstate_ref[...] = ds
        by_chunk = lambda xs: jnp.stack([xs[c][h] for h in range(GDN_HEADS) for c in range(ng)])
        d_vn, ds = by_chunk(d_vn_c), by_chunk(ds_c)

        on_s = _bm_nt(jnp.concatenate([d_out, d_vn], axis=1), s)
        d_qe, d_w = on_s[:, :CHUNK], -on_s[:, CHUNK:]
        d_p = jnp.where(incl, _bm_nt(d_out, vn), 0.0)
        d_kd = _bm_nt(vn, ds)
        d_both = _bm_tn(tm, jnp.concatenate([d_vn, d_w], axis=2))
        d_vb, d_kbe = d_both[:, :, :dk], d_both[:, :, dk:]
        d_a = -jnp.where(strict, _bm_nt(d_both, uw), 0.0)
        m = d_a * dm
        n = d_p * dm
        on_k = _bm(jnp.concatenate([m, n], axis=1), k)
        d_kb = on_k[:, :CHUNK] + d_kbe * e
        d_q = on_k[:, CHUNK:] + d_qe * e
        d_k = (_bm_tn(jnp.concatenate([m, n], axis=1), jnp.concatenate([kb, q], axis=1))
               + d_kd * eg + b * d_kb)
        d_v = b * d_vb
        r = d_a * a + d_p * p
        kd_term = rsum(d_kd * kd)
        d_gl = total(ds * s) * el + jnp.sum(kd_term, axis=1, keepdims=True)
        d_gam = (rsum(r) - _times_exact(r, ones, _BTN)[:, :, 0:1] + rsum(d_qe * qe) + rsum(d_kbe * kbe) - kd_term
                 + jnp.where(last_row, d_gl, 0.0))
        d_beta = rsum(d_kb * k) + rsum(d_vb * v)
        d_g = _exact_times(upper, d_gam * ones, _BNN)[:, :, 0:1]
        per_head = lambda x: x.reshape(GDN_HEADS, ng * CHUNK, x.shape[-1])
        d_q, d_k, d_v, d_beta, d_g = (per_head(x) for x in (d_q, d_k, d_v, d_beta, d_g))
        gates = jnp.zeros((ng * CHUNK, LANES), F32)
        for h in range(GDN_HEADS):
            lanes = slice(h * dk, (h + 1) * dk)
            d_ref[0, rows, lanes] = d_q[h]
            d_ref[1, rows, lanes] = d_k[h]
            d_ref[2, rows, lanes] = d_v[h]
            gates = gates + (jnp.where(lane_ids == h, d_beta[h], 0.0)
                             + jnp.where(lane_ids == GDN_HEADS + h, d_g[h], 0.0))
        dgate_ref[rows, :] = gates

    d_spec = pl.BlockSpec((3, cps * CHUNK, width), lambda g: (0, steps - 1 - g, 0))
    return pl.pallas_call(
        body, name="gdn_bwd",
        grid=(steps,),
        in_specs=[rows_blk(width, 0), rows_blk(width, 1), rows_blk(width, 2), rows_blk(LANES), rows_blk(LANES), gate_r,
                  per_chunk(dk, dk), per_chunk(CHUNK, CHUNK), per_chunk(CHUNK, CHUNK), per_chunk(CHUNK, CHUNK),
                  rows_blk(2 * width), rows_blk(width), rows_blk(width)] + [_HBM] * nx,
        out_specs=[d_spec, rows_blk(LANES)] + [_HBM] * nx,
        out_shape=[jax.ShapeDtypeStruct((3, t_len, width), F32),
                   jax.ShapeDtypeStruct((t_len, LANES), F32)] + _direct_out_shapes(scatter, (True,) * nx),
        scratch_shapes=[pltpu.VMEM((GDN_HEADS, dk, dk), F32)] + (_direct_semaphores(nx) if nx else []),
        compiler_params=_params("arbitrary"),
    )(gact, gact, gact, beta_c, gam_c, gam_r, *saved, d_o, *scatter)


def _group_sums(x, group):
    rows, width = x.shape
    lane = lax.broadcasted_iota(jnp.int32, (1, LANES), 1)
    out = []
    for t in range(width // LANES):
        seg = x[:, t * LANES:(t + 1) * LANES]
        if group == LANES:
            out.append(jnp.broadcast_to(jnp.sum(seg, axis=-1, keepdims=True), (rows, LANES)))
        else:
            low = jnp.sum(jnp.where(lane < group, seg, 0.0), axis=-1, keepdims=True)
            high = jnp.sum(jnp.where(lane < group, 0.0, seg), axis=-1, keepdims=True)
            out.append(jnp.where(lane < group, low, high))
    return jnp.concatenate(out, axis=1)


def _post_call(o_sb, o_gd, proj_gates, x, target, w_out, sbw, gdw, fw, tm=256):
    t_len, d = x.shape
    half = 512
    sb_blocks = half // LANES

    def body(osb_ref, ogd_ref, zsb_ref, zgd_ref, x_ref, tg_ref, wo_ref, sbw_ref, gdw_ref, fw_ref,
             dx2_ref, dosb_ref, dogd_ref, dz_ref, loss_ref, gfw_ref, gsb_ref, ggd_ref, gwo_ref):
        step = pl.program_id(0)

        @pl.when(step == 0)
        def _():
            loss_ref[...] = jnp.zeros_like(loss_ref)
            gfw_ref[...] = jnp.zeros_like(gfw_ref)
            gsb_ref[...] = jnp.zeros_like(gsb_ref)
            ggd_ref[...] = jnp.zeros_like(ggd_ref)
            gwo_ref[...] = jnp.zeros_like(gwo_ref)

        def head_forward(o, z, w, head_dim):
            r = lax.rsqrt(_group_sums(o * o, head_dim) * (1.0 / head_dim) + EPS)
            nrm = o * r * w
            sg = _sigmoid(z)
            return r, nrm, sg, nrm * (z * sg)

        def head_backward(d_m, o, z, w, head_dim, r, nrm, sg):
            d_n = d_m * (z * sg)
            d_z = d_m * nrm * (sg * (1.0 + z * (1.0 - sg)))
            dnw = d_n * w
            d_o = r * dnw - o * (r * r * r) * (_group_sums(dnw * o, head_dim) * (1.0 / head_dim))
            return d_o, d_z, jnp.sum(d_n * o * r, axis=0, keepdims=True)

        osb = jnp.concatenate([osb_ref[j] for j in range(sb_blocks)], axis=1)
        ogd, zsb, zgd = ogd_ref[...], zsb_ref[...], zgd_ref[...]
        sbw_v, gdw_v = sbw_ref[...], gdw_ref[...]
        r_sb, n_sb, sg_sb, m_sb = head_forward(osb, zsb, sbw_v, SB_HEAD_DIM)
        r_gd, n_gd, sg_gd, m_gd = head_forward(ogd, zgd, gdw_v, GDN_HEAD_DIM)
        mixed = jnp.concatenate([m_sb, m_gd], axis=1).astype(MXU_DTYPE)
        wo = wo_ref[...]
        x2 = x_ref[...] + jnp.dot(mixed, wo, preferred_element_type=F32)
        r2 = lax.rsqrt(jnp.mean(x2 * x2, axis=-1, keepdims=True) + EPS)
        fw_v = fw_ref[...]
        err = x2 * r2 * fw_v - tg_ref[...]
        loss_ref[...] += 0.5 * jnp.sum(jnp.sum(err * err, axis=-1, keepdims=True) * (1.0 / d))
        dy = err * (1.0 / d)
        gg = dy * fw_v
        dx2 = r2 * gg - x2 * ((r2 * r2 * r2) * jnp.mean(gg * x2, axis=-1, keepdims=True))
        gfw_ref[...] += jnp.sum(dy * x2 * r2, axis=0, keepdims=True)
        dx2_ref[...] = dx2
        dx2b = dx2.astype(MXU_DTYPE)
        d_mixed = lax.dot_general(dx2b, wo, _NT, preferred_element_type=F32)
        gwo_ref[...] += lax.dot_general(mixed, dx2b, _TN, preferred_element_type=F32)
        d_osb, d_zsb, gsb = head_backward(d_mixed[:, :half], osb, zsb, sbw_v, SB_HEAD_DIM, r_sb, n_sb, sg_sb)
        d_ogd, d_zgd, ggd = head_backward(d_mixed[:, half:], ogd, zgd, gdw_v, GDN_HEAD_DIM, r_gd, n_gd, sg_gd)
        for j in range(sb_blocks):
            dosb_ref[j] = d_osb[:, j * LANES:(j + 1) * LANES]
        dogd_ref[...] = d_ogd
        dz_ref[0] = d_zsb
        dz_ref[1] = d_zgd
        gsb_ref[...] += gsb
        ggd_ref[...] += ggd

    row_blk = lambda w: pl.BlockSpec((tm, w), lambda i: (i, 0))
    blocks_blk = pl.BlockSpec((sb_blocks, tm, LANES), lambda i: (0, i, 0))
    fixed = lambda r, w: pl.BlockSpec((r, w), lambda i: (0, 0))
    return pl.pallas_call(
        body, name="post",
        grid=(t_len // tm,),
        in_specs=[blocks_blk, row_blk(half),
                  pl.BlockSpec((tm, half), lambda i: (i, 0)),
                  pl.BlockSpec((tm, half), lambda i: (i, 1)),
                  row_blk(d), row_blk(d), fixed(d, d), fixed(1, half), fixed(1, half), fixed(1, d)],
        out_specs=[row_blk(d), blocks_blk, row_blk(half),
                   pl.BlockSpec((2, tm, half), lambda i: (DPROJ_GATE_SLOT // 2, i, 0)),
                   fixed(1, LANES), fixed(1, d), fixed(1, half), fixed(1, half), fixed(d, d)],
        out_shape=[jax.ShapeDtypeStruct((t_len, d), F32), jax.ShapeDtypeStruct((sb_blocks, t_len, LANES), F32),
                   jax.ShapeDtypeStruct((t_len, half), F32),
                   jax.ShapeDtypeStruct((len(DPROJ_PIECE_OF_SLOT), t_len, half), F32),
                     jax.ShapeDtypeStruct((1, LANES), F32), jax.ShapeDtypeStruct((1, d), F32),
                     jax.ShapeDtypeStruct((1, half), F32), jax.ShapeDtypeStruct((1, half), F32),
                     jax.ShapeDtypeStruct((d, d), F32)],
        compiler_params=_params("arbitrary"),
    )(o_sb, o_gd, proj_gates, proj_gates, x, target, w_out, sbw, gdw, fw)


def _piece_of_slot(s):
    return jnp.where(s < DPROJ_GDN_SLOT, s, jnp.where(s < DPROJ_GATE_SLOT, s + 1,
                                                     jnp.where(s == DPROJ_GATE_SLOT, 3, 7)))


def _gw_in_call(h_t, dproj8):
    d, t_len = h_t.shape
    n_piece, _, pw = dproj8.shape

    def body(ht_ref, dp_ref, gw_ref):
        gw_ref[...] = jnp.dot(ht_ref[...], dp_ref[0].astype(MXU_DTYPE), preferred_element_type=F32)

    return pl.pallas_call(
        body, name="gw_in",
        grid=(n_piece,),
        in_specs=[pl.BlockSpec((d, t_len), lambda s: (0, 0)),
                  pl.BlockSpec((1, t_len, pw), lambda s: (s, 0, 0))],
        out_specs=pl.BlockSpec((d, pw), lambda s: (0, _piece_of_slot(s))),
        out_shape=jax.ShapeDtypeStruct((d, n_piece * pw), F32),
        compiler_params=_params("arbitrary"),
    )(h_t, dproj8)


def _slot_of_piece(p):
    return jnp.where(p < DPROJ_GDN_SLOT, p, jnp.where(p == 3, DPROJ_GATE_SLOT, jnp.where(p < 7, p - 1, 7)))


def _gw_in_shards_call(h_t, dproj8, dsmall, out_dtype):
    d, t_len = h_t.shape
    n_piece, _, pw = dproj8.shape
    ns = dsmall.shape[1]
    n_pairs = N_DEV // 2

    def body(ht_ref, dp_ref, ds_ref, chip_ref, prev_ref, gates_ref, send_ref, recv_ref, send_sems, recv_sems):
        p = pl.program_id(0)
        x_pos, y_pos, c = lax.axis_index("x"), lax.axis_index("y"), lax.axis_index("c")
        to_sibling = lambda pair: pltpu.make_async_remote_copy(
            src_ref=send_ref.at[pair], dst_ref=recv_ref.at[pair], send_sem=send_sems.at[pair],
            recv_sem=recv_sems.at[pair], device_id=(x_pos, y_pos, 1 - c), device_id_type=_MESH)

        @pl.when(p == 0)
        def _():
            gates_ref[...] = jnp.dot(ht_ref[...], ds_ref[...].astype(MXU_DTYPE), preferred_element_type=F32)

        def emit(s, tail):
            x = jnp.concatenate([prev_ref[...], tail], axis=1)
            y = x if s == 0 else pltpu.roll(x, SHARD_PAD - s, axis=1)
            shard = y[:, :SHARD_COLS].astype(out_dtype)

            @pl.when(c == s % 2)
            def _():
                chip_ref[s // 2] = shard

            @pl.when(c != s % 2)
            def _():
                send_ref[s // 2] = shard
                to_sibling(s // 2).start()

        @pl.when(p < n_piece)
        def _():
            cur = jnp.dot(ht_ref[...], dp_ref[0].astype(MXU_DTYPE), preferred_element_type=F32)
            for s in range(n_piece - 1):
                pl.when(p == s + 1)(functools.partial(emit, s, cur[:, :SHARD_PAD - pw]))
            prev_ref[...] = cur

        @pl.when(p == n_piece)
        def _():
            emit(n_piece - 1, gates_ref[...])
            for pair in range(n_pairs):
                to_sibling(pair).wait_send()
            for pair in range(n_pairs):
                to_sibling(pair).wait_recv()
                chip_ref[pair] = (chip_ref[pair].astype(F32) + recv_ref[pair].astype(F32)).astype(out_dtype)

    shards_of_side = lambda: pltpu.VMEM((n_pairs, d, SHARD_COLS), out_dtype)
    return pl.pallas_call(
        body, name="gw_in",
        grid=(n_piece + 1,),
        in_specs=[pl.BlockSpec((d, t_len), lambda p: (0, 0)),
                  pl.BlockSpec((1, t_len, pw), lambda p: (_slot_of_piece(jnp.minimum(p, n_piece - 1)), 0, 0)),
                  pl.BlockSpec((t_len, ns), lambda p: (0, 0))],
        out_specs=pl.BlockSpec((n_pairs, d, SHARD_COLS), lambda p: (0, 0, 0)),
        out_shape=jax.ShapeDtypeStruct((n_pairs, d, SHARD_COLS), out_dtype),
        scratch_shapes=[pltpu.VMEM((d, pw), F32), pltpu.VMEM((d, ns), F32), shards_of_side(), shards_of_side(),
                        pltpu.SemaphoreType.DMA((n_pairs,)), pltpu.SemaphoreType.DMA((n_pairs,))],
        compiler_params=_params("arbitrary"),
    )(h_t, dproj8, dsmall)


def _gw_small_call(h_t, dsmall, tm=512):
    d, t_len = h_t.shape
    ns = dsmall.shape[1]

    def body(ht_ref, dp_ref, gw_ref):
        @pl.when(pl.program_id(0) == 0)
        def _():
            gw_ref[...] = jnp.zeros_like(gw_ref)

        gw_ref[...] += jnp.dot(ht_ref[...], dp_ref[...].astype(MXU_DTYPE), preferred_element_type=F32)

    return pl.pallas_call(
        body, name="gw_small",
        grid=(t_len // tm,),
        in_specs=[pl.BlockSpec((d, tm), lambda t: (0, t)),
                  pl.BlockSpec((tm, ns), lambda t: (t, 0))],
        out_specs=pl.BlockSpec((d, ns), lambda t: (0, 0)),
        out_shape=jax.ShapeDtypeStruct((d, ns), F32),
        compiler_params=_params("arbitrary"),
    )(h_t, dsmall)


def _dx_call(dproj8, dsmall, w_main, w_small, x, r, dx2, norm_w, chip_scatter=(), peer_scatter=(), tm=256):
    t_len, d = x.shape
    n_piece, _, pw = dproj8.shape
    ns = dsmall.shape[1]
    nx = len(chip_scatter)
    n_peer = len(peer_scatter)
    steps = t_len // tm
    n_in = 8 + nx + n_peer
    n_out = 2 + nx + n_peer

    def body(*refs):
        dp_ref, ds_ref, wm_ref, ws_ref, x_ref, r_ref, dx2_ref, nw_ref = refs[:8]
        gx_ref, gnw_ref = refs[n_in:n_in + 2]
        scratch = refs[n_in + n_out:]
        copies = lambda: _chip_copies(refs[8:8 + nx], refs[n_in + 2:n_in + 2 + nx], *scratch[:3])
        if nx:
            pl.when(pl.program_id(0) == 0)(lambda: _start_all(copies()))

        @pl.when(pl.program_id(0) == 0)
        def _():
            gnw_ref[...] = jnp.zeros_like(gnw_ref)

        dh = lax.dot_general(ds_ref[...].astype(MXU_DTYPE), ws_ref[...], _NT, preferred_element_type=F32)
        for s, p in enumerate(DPROJ_PIECE_OF_SLOT):
            dh = dh + lax.dot_general(dp_ref[s].astype(MXU_DTYPE), wm_ref[:, p * pw:(p + 1) * pw], _NT,
                                      preferred_element_type=F32)
        xv, rv = x_ref[...], r_ref[...]
        dn = dh * nw_ref[...]
        gx_ref[...] = dx2_ref[...] + rv * dn - xv * ((rv * rv * rv) * jnp.mean(dn * xv, axis=-1, keepdims=True))
        gnw_ref[...] += jnp.sum(dh * xv * rv, axis=0, keepdims=True)

        @pl.when(pl.program_id(0) == steps - 1)
        def _():
            if n_peer:
                small_ref, parts_ref = refs[8 + nx:n_in]
                small_buf = scratch[6]
                small_buf[...] = small_ref[...]
                small_buf[0:1, :] = gnw_ref[...]
                peer_copies = _direct_copies([small_buf, parts_ref], refs[n_in + 2 + nx:n_in + n_out], *scratch[3:6],
                                             [False, True])
                _start_all(peer_copies)
            if nx:
                _wait_all(copies())
            if n_peer:
                _wait_all(peer_copies)

    assert n_peer in (0, 2) and (nx == 1 or not n_peer)
    peer_in_specs = [pl.BlockSpec(peer_scatter[0].shape, lambda i: (0, 0)), _HBM] if n_peer else []
    peer_scratch = _direct_semaphores(n_peer) + [pltpu.VMEM(peer_scatter[0].shape, F32)] if n_peer else []
    return pl.pallas_call(
        body, name="dx",
        grid=(steps,),
        in_specs=[pl.BlockSpec((n_piece, tm, pw), lambda i: (0, i, 0)),
                  pl.BlockSpec((tm, ns), lambda i: (i, 0)),
                  pl.BlockSpec((d, n_piece * pw), lambda i: (0, 0)),
                  pl.BlockSpec((d, ns), lambda i: (0, 0)),
                  pl.BlockSpec((tm, d), lambda i: (i, 0)),
                  pl.BlockSpec((tm, 1), lambda i: (i, 0)),
                  pl.BlockSpec((tm, d), lambda i: (i, 0)),
                  pl.BlockSpec((1, d), lambda i: (0, 0))] + [_HBM] * nx + peer_in_specs,
        out_specs=[pl.BlockSpec((tm, d), lambda i: (i, 0)),
                   pl.BlockSpec((1, d), lambda i: (0, 0))] + [_HBM] * (nx + n_peer),
        out_shape=[jax.ShapeDtypeStruct((t_len, d), F32), jax.ShapeDtypeStruct((1, d), F32)]
                  + [jax.ShapeDtypeStruct(a.shape, a.dtype) for a in chip_scatter]
                  + (_direct_out_shapes(peer_scatter, [False, True]) if n_peer else []),
        scratch_shapes=(_chip_semaphores(nx) if nx else []) + peer_scratch,
        compiler_params=_params("arbitrary"),
    )(dproj8, dsmall, w_main, w_small, x, r, dx2, norm_w, *chip_scatter, *peer_scatter)


def _direct_out_shapes(srcs, per_peer):
    return [jax.ShapeDtypeStruct(s.shape if pp else (N_DEV,) + s.shape, s.dtype) for s, pp in zip(srcs, per_peer)]


def _direct_semaphores(n):
    return [pltpu.SemaphoreType.DMA((n * (N_DEV - 1),)), pltpu.SemaphoreType.DMA((n * (N_DEV - 1),)),
            pltpu.SemaphoreType.DMA((n,))]


def _direct_copies(src_refs, out_refs, send_sems, recv_sems, local_sems, per_peer):
    x, y, c = lax.axis_index("x"), lax.axis_index("y"), lax.axis_index("c")
    me = 4 * x + 2 * y + c
    local, remote = [], []
    for a in range(len(src_refs)):
        mine = src_refs[a].at[me] if per_peer[a] else src_refs[a]
        local.append(pltpu.make_async_copy(mine, out_refs[a].at[me], local_sems.at[a]))
    for k in range(1, N_DEV):
        kx, ky, kc = (k >> 2) & 1, (k >> 1) & 1, k & 1
        px = 1 - x if kx else x
        py = 1 - y if ky else y
        pc = 1 - c if kc else c
        peer = 4 * px + 2 * py + pc
        for a in range(len(src_refs)):
            sem = a * (N_DEV - 1) + (k - 1)
            remote.append(pltpu.make_async_remote_copy(
                src_ref=src_refs[a].at[peer] if per_peer[a] else src_refs[a], dst_ref=out_refs[a].at[me],
                send_sem=send_sems.at[sem], recv_sem=recv_sems.at[sem],
                device_id=(px, py, pc), device_id_type=pl.DeviceIdType.MESH))
    return local, remote


def _start_all(copies):
    local, remote = copies
    for cp in local + remote:
        cp.start()


def _wait_all(copies):
    local, remote = copies
    for cp in remote:
        cp.wait_send()
    for cp in remote:
        cp.wait_recv()
    for cp in local:
        cp.wait()


N_CHIPS = 4
_HBM = pl.BlockSpec(memory_space=pl.ANY)
_MESH = pl.DeviceIdType.MESH


def _gather_call(name, srcs):
    n = len(srcs)
    per = N_DEV - 1

    def body(*refs):
        src_refs, out_refs = refs[:n], refs[n:2 * n]
        send_sems, recv_sems, local_sems = refs[2 * n:]
        x, y, c = lax.axis_index("x"), lax.axis_index("y"), lax.axis_index("c")
        me, sibling = (x, y, c), (x, y, 1 - c)
        x_nbr, y_nbr, diagonal = (1 - x, y), (x, 1 - y), (1 - x, 1 - y)
        held = ((1 - x) * c + x * (1 - c), y * c + (1 - y) * (1 - c))
        onward = (x * c + (1 - x) * (1 - c), (1 - y) * c + y * (1 - c))
        slot = lambda px, py, pc: 4 * px + 2 * py + pc

        def copy(a, k, block, to, from_src=False):
            rows = out_refs[a].at[slot(*block)]
            return pltpu.make_async_remote_copy(
                src_ref=src_refs[a] if from_src else rows, dst_ref=rows,
                send_sem=send_sems.at[a * per + k], recv_sem=recv_sems.at[a * per + k],
                device_id=to, device_id_type=_MESH)

        local = [pltpu.make_async_copy(src_refs[a], out_refs[a].at[slot(*me)], local_sems.at[a]) for a in range(n)]
        started = []

        def start(cp):
            cp.start()
            started.append(cp)

        for cp in local:
            cp.start()
        for a in range(n):
            start(copy(a, 0, me, sibling, True))
            start(copy(a, 1, me, (*x_nbr, c), True))
            start(copy(a, 2, me, (*y_nbr, c), True))
        for a in range(n):
            copy(a, 1, (*x_nbr, c), me).wait_recv()
            copy(a, 2, (*y_nbr, c), me).wait_recv()
            start(copy(a, 3, (*held, c), (*onward, c)))
            start(copy(a, 4, (*x_nbr, c), sibling))
            start(copy(a, 5, (*y_nbr, c), sibling))
        for a in range(n):
            copy(a, 3, (*diagonal, c), me).wait_recv()
            start(copy(a, 6, (*diagonal, c), sibling))
        for a in range(n):
            copy(a, 0, sibling, me).wait_recv()
            for k, chip in ((4, x_nbr), (5, y_nbr), (6, diagonal)):
                copy(a, k, (*chip, 1 - c), me).wait_recv()
        for cp in started:
            cp.wait_send()
        for cp in local:
            cp.wait()

    return pl.pallas_call(
        body, name=name,
        in_specs=[_HBM] * n, out_specs=[_HBM] * n,
        out_shape=[jax.ShapeDtypeStruct((N_DEV,) + s.shape, s.dtype) for s in srcs],
        scratch_shapes=[pltpu.SemaphoreType.DMA((n * per,)), pltpu.SemaphoreType.DMA((n * per,)),
                        pltpu.SemaphoreType.DMA((n,))],
    )(*srcs)


def _chip_semaphores(n):
    per = N_CHIPS - 1
    return [pltpu.SemaphoreType.DMA((n * per,)), pltpu.SemaphoreType.DMA((n * per,)), pltpu.SemaphoreType.DMA((n,))]


def _chip_copies(src_refs, out_refs, send_sems, recv_sems, local_sems):
    per = N_CHIPS - 1
    x, y, c = lax.axis_index("x"), lax.axis_index("y"), lax.axis_index("c")
    mine = 2 * x + y
    chips = [(1 - x, y), (x, 1 - y), (1 - x, 1 - y)]
    n = len(src_refs)
    local = [pltpu.make_async_copy(src_refs[a].at[mine], out_refs[a].at[mine], local_sems.at[a]) for a in range(n)]
    remote = []
    for a in range(n):
        for j, (px, py) in enumerate(chips):
            remote.append(pltpu.make_async_remote_copy(
                src_ref=src_refs[a].at[2 * px + py], dst_ref=out_refs[a].at[mine],
                send_sem=send_sems.at[a * per + j], recv_sem=recv_sems.at[a * per + j],
                device_id=(px, py, c), device_id_type=_MESH))
    return local, remote


def _adam_call(name, parts, w, m, v, tr):
    rows, cols = w.shape
    n_slots = parts.shape[0]

    def body(p_ref, w_ref, m_ref, v_ref, g_ref, d_ref, nm_ref, nv_ref):
        g = p_ref[0].astype(F32)
        for s in range(1, n_slots):
            g = g + p_ref[s].astype(F32)
        m_new = ADAM_B1 * m_ref[...] + (1.0 - ADAM_B1) * g
        v_new = ADAM_B2 * v_ref[...] + (1.0 - ADAM_B2) * (g * g)
        m_hat = m_new / (1.0 - ADAM_B1 ** ADAM_STEP)
        v_hat = v_new / (1.0 - ADAM_B2 ** ADAM_STEP)
        g_ref[...] = g
        d_ref[...] = -ADAM_LR * (m_hat / (jnp.sqrt(v_hat) + ADAM_EPS) + ADAM_WD * w_ref[...])
        nm_ref[...] = m_new
        nv_ref[...] = v_new

    blk = pl.BlockSpec((tr, cols), lambda i: (i, 0))
    return pl.pallas_call(
        body, name=name,
        grid=(rows // tr,),
        in_specs=[pl.BlockSpec((n_slots, tr, cols), lambda i: (0, i, 0)), blk, blk, blk],
        out_specs=[blk] * 4,
        out_shape=[jax.ShapeDtypeStruct((rows, cols), F32)] * 4,
        compiler_params=_params("arbitrary"),
    )(parts, w, m, v)


def _columns_to_rows_call(name, w_t, rows, dtype):
    row_tiles = rows // LANES
    cols = w_t.shape[0] // row_tiles
    whole = cols // LANES * LANES

    def body(w_ref, out_ref):
        diagonal = (lax.broadcasted_iota(jnp.int32, (LANES, LANES), 0)
                    == lax.broadcasted_iota(jnp.int32, (LANES, LANES), 1))
        for a in range(row_tiles):
            out_ref[a * LANES:(a + 1) * LANES, :whole] = (
                w_ref[pl.ds(a, whole, stride=row_tiles), :].T.astype(dtype))
            for c in range(whole, cols):
                column = w_ref[pl.ds(c * row_tiles + a, 1), :]
                upright = jnp.sum(jnp.where(diagonal, column, 0.0), axis=1, keepdims=True)
                out_ref[a * LANES:(a + 1) * LANES, c:c + 1] = upright.astype(dtype)

    vm = pl.BlockSpec(memory_space=pltpu.VMEM)
    return pl.pallas_call(
        body, name=name,
        in_specs=[vm], out_specs=vm,
        out_shape=jax.ShapeDtypeStruct((rows, cols), dtype),
        compiler_params=pltpu.CompilerParams(vmem_limit_bytes=VMEM_LIMIT_BYTES),
    )(w_t)


def _adam_columns_call(name, parts, w_t, m_t, v_t):
    n_slots, rows, cols = parts.shape
    row_tiles = rows // LANES
    cols_pad = -(-cols // LANES) * LANES

    def body(p_ref, w_ref, m_ref, v_ref, *out_refs):
        for a in range(row_tiles):
            g = p_ref[0, a * LANES:(a + 1) * LANES, :].astype(F32)
            for s in range(1, n_slots):
                g = g + p_ref[s, a * LANES:(a + 1) * LANES, :].astype(F32)
            g = jnp.concatenate([g, jnp.zeros((LANES, cols_pad - cols), F32)], axis=1).T[:cols]
            column_rows = pl.ds(a, cols, stride=row_tiles)
            results = (g,) + _adamw(g, w_ref[column_rows, :], m_ref[column_rows, :], v_ref[column_rows, :])
            for out_ref, val in zip(out_refs, results):
                out_ref[column_rows, :] = val

    vm = pl.BlockSpec(memory_space=pltpu.VMEM)
    return pl.pallas_call(
        body, name=name,
        in_specs=[vm] * 4, out_specs=[vm] * 4,
        out_shape=[jax.ShapeDtypeStruct(w_t.shape, F32)] * 4,
        compiler_params=pltpu.CompilerParams(vmem_limit_bytes=VMEM_LIMIT_BYTES),
    )(parts, w_t, m_t, v_t)


N_PIECES = 8
PIECE = 512
SHARD_COLS = 513
SHARD_PAD = 640
RELAYOUT_ROWS = 256


def _from_shards_call(shards):
    _, d, _ = shards.shape
    tr = RELAYOUT_ROWS

    def body(p_ref, m_ref, s_ref):
        lane = lax.broadcasted_iota(jnp.int32, (tr, SHARD_PAD), 1)
        pad = jnp.zeros((tr, SHARD_PAD - SHARD_COLS), F32)
        sh = [jnp.concatenate([p_ref[s].astype(F32), pad], axis=1) for s in range(N_DEV)]
        for p in range(N_PIECES):
            y = sh[p] if p == 0 else pltpu.roll(sh[p], p, axis=1)
            if p > 0:
                y = jnp.where(lane < p, pltpu.roll(sh[p - 1], SHARD_PAD - (SHARD_COLS - p), axis=1), y)
            m_ref[:, p * PIECE:(p + 1) * PIECE] = y[:, :PIECE].astype(m_ref.dtype)
        first_gate = N_PIECES * PIECE - (N_DEV - 1) * SHARD_COLS
        s_ref[...] = pltpu.roll(sh[N_DEV - 1], SHARD_PAD - first_gate, axis=1)[:, :LANES].astype(s_ref.dtype)

    return pl.pallas_call(
        body, name="w_in_from_shards",
        grid=(d // tr,),
        in_specs=[pl.BlockSpec((N_DEV, tr, SHARD_COLS), lambda i: (0, i, 0))],
        out_specs=[pl.BlockSpec((tr, N_PIECES * PIECE), lambda i: (i, 0)), pl.BlockSpec((tr, LANES), lambda i: (i, 0))],
        out_shape=[jax.ShapeDtypeStruct((d, N_PIECES * PIECE), shards.dtype),
                   jax.ShapeDtypeStruct((d, LANES), shards.dtype)],
        compiler_params=_params("arbitrary"),
    )(shards)


def _adamw(g, w, m, v):
    m_new = ADAM_B1 * m + (1.0 - ADAM_B1) * g
    v_new = ADAM_B2 * v + (1.0 - ADAM_B2) * (g * g)
    m_hat = m_new / (1.0 - ADAM_B1 ** ADAM_STEP)
    v_hat = v_new / (1.0 - ADAM_B2 ** ADAM_STEP)
    return -ADAM_LR * (m_hat / (jnp.sqrt(v_hat) + ADAM_EPS) + ADAM_WD * w), m_new, v_new


def _adam_small_call(parts, ws, ms, vs):
    n = len(ws)
    n_slots = parts.shape[0]

    def body(*refs):
        p_ref = refs[0]
        w_refs, m_refs, v_refs = refs[1:1 + n], refs[1 + n:1 + 2 * n], refs[1 + 2 * n:1 + 3 * n]
        loss_ref = refs[1 + 3 * n]
        outs = refs[2 + 3 * n:]
        g_all = p_ref[0]
        for s in range(1, n_slots):
            g_all = g_all + p_ref[s]
        loss_ref[...] = g_all[n:n + 1, 0:1]
        for r in range(n):
            size = w_refs[r].shape[1]
            g = g_all[r:r + 1, :size]
            delta, m_new, v_new = _adamw(g, w_refs[r][...], m_refs[r][...], v_refs[r][...])
            for kind, val in enumerate((g, delta, m_new, v_new)):
                outs[kind * n + r][...] = val

    vm = pl.BlockSpec(memory_space=pltpu.VMEM)
    shapes = [jax.ShapeDtypeStruct(w.shape, F32) for w in ws]
    return pl.pallas_call(
        body, name="adam_small",
        in_specs=[vm] * (1 + 3 * n), out_specs=[vm] * (1 + 4 * n),
        out_shape=[jax.ShapeDtypeStruct((1, 1), F32)] + shapes * 4,
    )(parts, *ws, *ms, *vs)


_SMALL_ROWS = ("norm1_w", "final_norm_w", "sb_norm_w", "gdn_norm_w", "gdn_A_log", "gdn_dt_bias", "loss")


def _pack_small(vals, width):
    rows = [jnp.pad(a.reshape(1, -1).astype(F32), ((0, 0), (0, width - a.size))) for a in vals]
    rows += [jnp.zeros((1, width), F32)] * (8 - len(rows))
    return jnp.concatenate(rows, axis=0)


def _device_step(x2d, tgt, w_main, w_small, w_out_full, conv_full, norm1_w, sb_norm_w, gdn_A_log, gdn_dt_bias,
                 gdn_norm_w, final_norm_w, distributed=False):
    t_len, d = x2d.shape
    n_chunks = t_len // CHUNK
    w_main, w_small, w_out_full = (a.astype(MXU_DTYPE) for a in (w_main, w_small, w_out_full))
    w_small_t = w_small[:, :2 * GDN_HEADS].T

    pad_lanes = lambda a, lo: jnp.pad(a.reshape(1, -1), ((0, 0), (lo, LANES - lo - a.size)))
    alog_l, dtb_l = pad_lanes(gdn_A_log, GDN_HEADS), pad_lanes(gdn_dt_bias, GDN_HEADS)
    alog_c, dtb_c = alog_l[:, :8].T, dtb_l[:, :8].T
    sbw = jnp.tile(sb_norm_w, (1, 512 // SB_HEAD_DIM))
    gdw = jnp.tile(gdn_norm_w, (1, 512 // GDN_HEAD_DIM))
    fw = final_norm_w.reshape(1, d)

    if distributed:
        proj_cols, proj_gates, ps, pst, h_t, r1, w_out_g, conv_g = _inproj_call(
            x2d, norm1_w, w_main, w_small, w_small_t, gather=(w_out_full, conv_full))
        w_out_full = w_out_g.reshape(d, d)
        conv_full = conv_g.transpose(1, 0, 2).reshape(CONV_WIDTH, N_DEV * conv_g.shape[2])
    else:
        proj_cols, proj_gates, ps, pst, h_t, r1 = _inproj_call(x2d, norm1_w, w_main, w_small, w_small_t)
    o_sb, sp_total, sb_blocks_run = _sb_fwd_call(proj_cols, t_len)
    gact = _gdn_prep_call(proj_cols, conv_full, t_len, after=sb_blocks_run)
    beta_l, gcol_l, grow = _gdn_gates_call(ps, pst, alog_l, dtb_l, alog_c, dtb_c, t_len)
    gam_r = grow[GDN_HEADS:2 * GDN_HEADS].reshape(GDN_HEADS, n_chunks, 1, CHUNK)
    o_gd, *gdn_saved = _gdn_fwd_call(gact, beta_l, gcol_l, gam_r, t_len)

    (dx2, d_osb, d_ogd, dproj8, loss_p, g_fw, g_sbw, g_gdw, g_wout) = _post_call(
        o_sb, o_gd, proj_gates, x2d, tgt, w_out_full, sbw, gdw, fw)

    dproj8 = _sb_bwd_call(proj_cols, sp_total, sb_blocks_run, d_osb, dproj8, t_len)
    if distributed:
        d_gact3, d_gates, g_wout = _gdn_bwd_call(gact, beta_l, gcol_l, gam_r, gdn_saved, d_ogd, t_len,
                                                 scatter=(g_wout.reshape(N_DEV, d // N_DEV, d),))
    else:
        d_gact3, d_gates = _gdn_bwd_call(gact, beta_l, gcol_l, gam_r, gdn_saved, d_ogd, t_len)
    dproj8, g_conv = _gdn_prep_bwd_call(proj_cols, conv_full, d_gact3, dproj8, t_len)
    dsmall, g_alog, g_dtb = _gdn_gates_bwd_call(ps, alog_l, dtb_l, d_gates, t_len)

    if distributed:
        chip_partials = _gw_in_shards_call(h_t, dproj8, dsmall, WIRE_DTYPE)
        fold = lambda a, group: a.reshape(-1, group).sum(axis=0)
        small_g = _pack_small([jnp.zeros((d,), F32), g_fw, fold(g_sbw, SB_HEAD_DIM), fold(g_gdw, GDN_HEAD_DIM),
                               g_alog[0, GDN_HEADS:2 * GDN_HEADS], g_dtb[0, GDN_HEADS:2 * GDN_HEADS],
                               loss_p[0, :1]], d)
        conv_cols = g_conv.shape[1] // N_DEV
        g_conv_parts = g_conv.reshape(CONV_WIDTH, N_DEV, conv_cols).transpose(1, 0, 2)
        grad_x, _, g_w_in, p_small, p_conv = _dx_call(dproj8, dsmall, w_main, w_small, x2d, r1, dx2, norm1_w,
                                                      chip_scatter=(chip_partials,),
                                                      peer_scatter=(small_g, g_conv_parts))
        return grad_x, g_w_in, g_wout, p_small, p_conv
    else:
        grad_x, g_n1 = _dx_call(dproj8, dsmall, w_main, w_small, x2d, r1, dx2, norm1_w)
        g_w_in = (_gw_in_call(h_t, dproj8), _gw_small_call(h_t, dsmall))
    return (loss_p, grad_x, g_n1, g_w_in, g_sbw, g_conv, g_alog, g_dtb, g_gdw, g_wout, g_fw)


def kernel(x, norm1_w, w_in, sb_norm_w, gdn_conv_w, gdn_A_log, gdn_dt_bias, gdn_norm_w, w_out, final_norm_w, loss_target, m_norm1_w, m_w_in, m_sb_norm_w, m_gdn_conv_w, m_gdn_A_log, m_gdn_dt_bias, m_gdn_norm_w, m_w_out, m_final_norm_w, v_norm1_w, v_w_in, v_sb_norm_w, v_gdn_conv_w, v_gdn_A_log, v_gdn_dt_bias, v_gdn_norm_w, v_w_out, v_final_norm_w):
    d = x.shape[2]
    shard_cols = w_in.shape[2]

    columns = lambda a: a.transpose(2, 0, 1).reshape(shard_cols * d // LANES, LANES)
    from_columns = lambda a: a.reshape(shard_cols, d // LANES, LANES).transpose(1, 2, 0).reshape(1, d, shard_cols)
    (w_in_g,) = _gather_call("gather_weights", [_columns_to_rows_call("w_in_to_wire", columns(w_in), d, WIRE_DTYPE)])
    w_main, w_small = _from_shards_call(w_in_g)

    grad_x, p_w_in, p_wout, p_small, p_conv = _device_step(
        x[0], loss_target[0], w_main, w_small, w_out[0].astype(WIRE_DTYPE), gdn_conv_w[0], norm1_w, sb_norm_w,
        gdn_A_log, gdn_dt_bias, gdn_norm_w, final_norm_w, distributed=True)

    r_w_in = [from_columns(a) for a in _adam_columns_call("adam_w_in", p_w_in, columns(w_in), columns(m_w_in),
                                                          columns(v_w_in))]
    r_wout = _adam_call("adam_w_out", p_wout, w_out[0], m_w_out[0], v_w_out[0], d // N_DEV)
    r_conv = _adam_call("adam_conv", p_conv, gdn_conv_w[0], m_gdn_conv_w[0], v_gdn_conv_w[0], CONV_WIDTH)

    row = lambda a: a.reshape(1, -1)
    n_small = len(_SMALL_ROWS) - 1
    r_small = _adam_small_call(
        p_small,
        [norm1_w, row(final_norm_w), sb_norm_w, gdn_norm_w, gdn_A_log, gdn_dt_bias],
        [m_norm1_w, row(m_final_norm_w), m_sb_norm_w, m_gdn_norm_w, m_gdn_A_log, m_gdn_dt_bias],
        [v_norm1_w, row(v_final_norm_w), v_sb_norm_w, v_gdn_norm_w, v_gdn_A_log, v_gdn_dt_bias])

    def small_out(kind, name):
        out = r_small[1 + kind * n_small + _SMALL_ROWS.index(name)]
        return out.reshape(final_norm_w.shape) if name == "final_norm_w" else out

    def outputs(kind):
        return (small_out(kind, "norm1_w"), r_w_in[kind], small_out(kind, "sb_norm_w"), r_conv[kind][None],
                small_out(kind, "gdn_A_log"), small_out(kind, "gdn_dt_bias"), small_out(kind, "gdn_norm_w"),
                r_wout[kind][None], small_out(kind, "final_norm_w"))

    return (r_small[0][0, 0], grad_x[None], *outputs(0), *outputs(1), *outputs(2), *outputs(3))
```

```python
import functools

import jax
import jax.numpy as jnp
from jax import lax
from jax.experimental import pallas as pl
from jax.experimental.pallas import tpu as pltpu

F32 = jnp.float32
MXU_DTYPE = jnp.bfloat16
WIRE_DTYPE = jnp.bfloat16
EXACT = lax.Precision.HIGHEST
EPS = 1e-6
N_DEV = 8
SB_HEAD_DIM = 64
GDN_HEAD_DIM = 128
GDN_HEADS = 4
GDN_CHUNKS_PER_STEP = 4
GDN_BWD_GROUP = 1
CHUNK = 64
CONV_WIDTH = 4
LANES = 128
SB_BLOCK = 128
SB_BQ = 256
VMEM_LIMIT_BYTES = 56 * 1024 * 1024

PIECE_COLS = 512
PROJ_PIECE_KINDS = ("heads", "heads", "heads", "gate", "heads", "heads", "heads", "gate")
SB_FIRST_BLOCK, GDN_FIRST_BLOCK = 0, 12

DPROJ_PIECE_OF_SLOT = (0, 1, 2, 4, 5, 6, 3, 7)
DPROJ_SB_SLOT, DPROJ_GDN_SLOT, DPROJ_GATE_SLOT = 0, 3, 6

ADAM_LR = 0.001
ADAM_B1 = 0.9
ADAM_B2 = 0.999
ADAM_EPS = 1e-08
ADAM_WD = 0.01
ADAM_STEP = 10

_NN = (((1,), (0,)), ((), ()))
_NT = (((1,), (1,)), ((), ()))
_TN = (((0,), (0,)), ((), ()))
_BNN = (((2,), (1,)), ((0,), (0,)))
_BNT = (((2,), (2,)), ((0,), (0,)))
_BTN = (((1,), (1,)), ((0,), (0,)))


def _mx(a, b):
    return jnp.dot(a, b, precision=EXACT, preferred_element_type=F32)


def _split(x):
    hi = x.astype(MXU_DTYPE)
    return hi, (x - hi.astype(F32)).astype(MXU_DTYPE)


def _m3_general(a, b, dims):
    ah, al = _split(a)
    bh, bl = _split(b)
    dot = lambda x, y: lax.dot_general(x, y, dims, preferred_element_type=F32)
    (contract, _), (batch, _) = dims
    free = [ax for ax in range(a.ndim) if ax not in contract and ax not in batch][0]
    m = a.shape[free]
    both = dot(jnp.concatenate([ah, al], axis=free), bh)
    out_axis = len(batch)
    hi_part = lax.slice_in_dim(both, 0, m, axis=out_axis)
    lo_part = lax.slice_in_dim(both, m, 2 * m, axis=out_axis)
    return hi_part + (dot(ah, bl) + lo_part)


def _times_exact(a, b_exact, dims):
    ah, al = _split(a)
    (contract, _), (batch, _) = dims
    free = [ax for ax in range(a.ndim) if ax not in contract and ax not in batch][0]
    m = a.shape[free]
    both = lax.dot_general(jnp.concatenate([ah, al], axis=free), b_exact.astype(MXU_DTYPE), dims,
                           preferred_element_type=F32)
    out_axis = len(batch)
    return lax.slice_in_dim(both, 0, m, axis=out_axis) + lax.slice_in_dim(both, m, 2 * m, axis=out_axis)


def _exact_times(a_exact, b, dims):
    bh, bl = _split(b)
    n = b.shape[-1]
    both = lax.dot_general(a_exact.astype(MXU_DTYPE), jnp.concatenate([bh, bl], axis=-1), dims,
                           preferred_element_type=F32)
    return both[..., :n] + both[..., n:]


def _sigmoid(z):
    return 1.0 / (1.0 + jnp.exp(-z))


def _softplus(z):
    return jnp.maximum(z, 0.0) + jnp.log(1.0 + jnp.exp(-jnp.abs(z)))


def _params(*semantics):
    return pltpu.CompilerParams(dimension_semantics=semantics, vmem_limit_bytes=VMEM_LIMIT_BYTES)


def _inproj_call(x, norm_w, w_main, w_small, w_small_t, gather=(), tm=256):
    t_len, d = x.shape
    n = w_main.shape[1]
    ns = w_small.shape[1]
    nst = w_small_t.shape[0]
    ng = len(gather)
    steps = t_len // tm

    blocks_per_piece = PIECE_COLS // LANES
    n_gate_cols = PIECE_COLS * PROJ_PIECE_KINDS.count("gate")
    n_col_blocks = blocks_per_piece * PROJ_PIECE_KINDS.count("heads")

    def body(*refs):
        x_ref, nw_ref, wm_ref, ws_ref, wst_ref = refs[:5]
        cols_ref, pz_ref, ps_ref, pst_ref, ht_ref, r_ref = refs[5 + ng:11 + ng]
        copies = lambda: _direct_copies(refs[5:5 + ng], refs[11 + ng:11 + 2 * ng], *refs[11 + 2 * ng:], (False,) * ng)
        if ng:
            pl.when(pl.program_id(0) == 0)(lambda: _start_all(copies()))
        xv = x_ref[...]
        r = lax.rsqrt(jnp.mean(xv * xv, axis=-1, keepdims=True) + EPS)
        h = xv * r * nw_ref[...]
        hb = h.astype(MXU_DTYPE)
        n_block = n_gate = 0
        for piece, kind in enumerate(PROJ_PIECE_KINDS):
            out = jnp.dot(hb, wm_ref[:, piece * PIECE_COLS:(piece + 1) * PIECE_COLS], preferred_element_type=F32)
            if kind == "gate":
                pz_ref[:, n_gate * PIECE_COLS:(n_gate + 1) * PIECE_COLS] = out
                n_gate += 1
            else:
                for j in range(blocks_per_piece):
                    cols_ref[n_block + j] = out[:, j * LANES:(j + 1) * LANES]
                n_block += blocks_per_piece
        ps_ref[...] = jnp.dot(hb, ws_ref[...], preferred_element_type=F32)
        pst_ref[...] = lax.dot_general(wst_ref[...], hb, _NT, preferred_element_type=F32)
        ht_ref[...] = h.T.astype(MXU_DTYPE)
        r_ref[...] = r
        if ng:
            pl.when(pl.program_id(0) == steps - 1)(lambda: _wait_all(copies()))

    return pl.pallas_call(
        body, name="inproj",
        grid=(steps,),
        in_specs=[pl.BlockSpec((tm, d), lambda i: (i, 0)),
                  pl.BlockSpec((1, d), lambda i: (0, 0)),
                  pl.BlockSpec((d, n), lambda i: (0, 0)),
                  pl.BlockSpec((d, ns), lambda i: (0, 0)),
                  pl.BlockSpec((nst, d), lambda i: (0, 0))] + [_HBM] * ng,
        out_specs=[pl.BlockSpec((n_col_blocks, tm, LANES), lambda i: (0, i, 0)),
                   pl.BlockSpec((tm, n_gate_cols), lambda i: (i, 0)),
                   pl.BlockSpec((tm, ns), lambda i: (i, 0)),
                   pl.BlockSpec((nst, tm), lambda i: (0, i)),
                   pl.BlockSpec((d, tm), lambda i: (0, i)),
                   pl.BlockSpec((tm, 1), lambda i: (i, 0))] + [_HBM] * ng,
        out_shape=[jax.ShapeDtypeStruct((n_col_blocks, t_len, LANES), F32),
                   jax.ShapeDtypeStruct((t_len, n_gate_cols), F32),
                   jax.ShapeDtypeStruct((t_len, ns), F32),
                   jax.ShapeDtypeStruct((nst, t_len), F32),
                   jax.ShapeDtypeStruct((d, t_len), MXU_DTYPE),
                   jax.ShapeDtypeStruct((t_len, 1), F32)] + _direct_out_shapes(gather, (False,) * ng),
        scratch_shapes=_direct_semaphores(ng) if ng else [],
        compiler_params=_params("arbitrary"),
    )(x, norm_w, w_main, w_small, w_small_t, *gather)


def _running_sum_mm(x, tri):
    hi = x.astype(MXU_DTYPE)
    lo = (x - hi.astype(F32)).astype(MXU_DTYPE)
    return jnp.dot(hi, tri, preferred_element_type=F32) + jnp.dot(lo, tri, preferred_element_type=F32)


def _col_block(t_len, first):
    return pl.BlockSpec((1, t_len, LANES), lambda p: (first + p, 0, 0))


def _sb_iotas():
    row_i = lax.broadcasted_iota(jnp.int32, (SB_BQ, SB_BLOCK), 0)
    col_i = lax.broadcasted_iota(jnp.int32, (SB_BQ, SB_BLOCK), 1)
    sq_r = lax.broadcasted_iota(jnp.int32, (SB_BLOCK, SB_BLOCK), 0)
    sq_c = lax.broadcasted_iota(jnp.int32, (SB_BLOCK, SB_BLOCK), 1)
    return row_i, col_i, sq_r, sq_c


SB_DIAG_BLOCKS = SB_BQ // SB_BLOCK
SB_EXP_FLOOR = -110.0


def _sb_keys_descending(qi, tile, carry, z_bounds, n_heads, has_free):
    group = SB_DIAG_BLOCKS
    n_free = group * qi
    diag = list(range(group - 1, -1, -1))
    carry = tile([n_free + j for j in diag], [True] * group, carry, [j * SB_BLOCK for j in diag])

    def largest_exponent(c):
        worst = jnp.max(z_bounds[0] - c[1])
        for h in range(1, n_heads):
            worst = jnp.maximum(worst, jnp.max(z_bounds[h] - c[1 + h]))
        return worst

    always = group if has_free else 0

    def cond(state):
        return (state[0] < n_free) & ((state[1] > SB_EXP_FLOOR) | (state[0] < always))

    def body(state):
        first = n_free - 1 - state[0]
        c = tile([first - j for j in range(group)], [False] * group, state[2:])
        return (state[0] + group, largest_exponent(c), *c)

    out = lax.while_loop(cond, body, (jnp.int32(0), largest_exponent(carry), *carry))
    return out[2:], out[0]


def _sb_keys_ascending(qi, n_run, tile, carry, has_free):
    group = SB_DIAG_BLOCKS
    n_free = group * qi
    diag = list(range(group))
    kjs, los, masked = [n_free + j for j in diag], [j * SB_BLOCK for j in diag], [True] * group
    if has_free:
        early = lambda s: [n_free - n_run + group * s + j for j in range(group)]
        carry = lax.fori_loop(0, n_run // group - 1, lambda s, c: tile(early(s), [False] * group, c), carry)
        kjs, los, masked = [n_free - group + j for j in range(group)] + kjs, [0] * group + los, [False] * group + masked
    return tile(kjs, masked, carry, los)


def _sb_fwd_call(cols, t_len):
    nq = t_len // SB_BQ
    scale = float(SB_HEAD_DIM) ** -0.5
    n_pairs = 512 // LANES
    per_pair = LANES // SB_HEAD_DIM

    def body(q_blk, k_blk, v_blk, o_blk, st_ref, nrun_ref):
        q_ref, k_ref, v_ref, o_ref = q_blk.at[0], k_blk.at[0], v_blk.at[0], o_blk.at[0]
        lane = lax.broadcasted_iota(jnp.int32, (1, LANES), 1)
        row_i, col_i, sq_r, sq_c = _sb_iotas()
        ge = (sq_r >= sq_c).astype(MXU_DTYPE)
        hms = [((lane // SB_HEAD_DIM) == hh).astype(F32) for hh in range(per_pair)]
        k_sq = k_ref[...] * k_ref[...]
        k_norms = [jnp.sqrt(jnp.max(jnp.sum(k_sq * hm, axis=-1, keepdims=True))) * (1.02 * scale) for hm in hms]

        def q_block(qi, has_free):
            r0 = qi * SB_BQ if isinstance(qi, int) else pl.multiple_of(qi * SB_BQ, SB_BQ)
            rows = pl.ds(r0, SB_BQ)
            q_all = q_ref[rows, :]
            qms = [(q_all * (hm * scale)).astype(MXU_DTYPE) for hm in hms]
            z_bounds = [jnp.sqrt(jnp.sum(q_all * q_all * hm, axis=-1, keepdims=True)) * kn
                        for hm, kn in zip(hms, k_norms)]

            def tile(kjs, masked, kc, los=None):
                heads = range(per_pair)
                los = los or [0] * len(kjs)
                pairs = [(t, h) for t in range(len(kjs)) for h in heads]
                add_rows = lambda full, lo, part: full + part if lo == 0 else jnp.concatenate(
                    [full[:lo], full[lo:] + part], axis=0)
                acc, cs = kc[0], list(kc[1:])
                s0s = [kj * SB_BLOCK if isinstance(kj, int) else pl.multiple_of(kj * SB_BLOCK, SB_BLOCK) for kj in kjs]
                kbs = [k_ref[pl.ds(s0, SB_BLOCK), :].astype(MXU_DTYPE) for s0 in s0s]
                v_alls = [v_ref[pl.ds(s0, SB_BLOCK), :] for s0 in s0s]
                vms = {(t, h): (v_alls[t] * hms[h]).astype(MXU_DTYPE) for t, h in pairs}
                zs = {(t, h): lax.dot_general(qms[h][los[t]:], kbs[t], _NT, preferred_element_type=F32)
                      for t, h in pairs}
                masks = [(col_i[lo:] + s0) < (row_i[lo:] + r0) if m else None for m, lo, s0 in zip(masked, los, s0s)]
                keep = lambda t, a: a if masks[t] is None else jnp.where(masks[t], a, 0.0)
                sps = {(t, h): keep(t, _softplus(zs[t, h])) for t, h in pairs}
                sums = {p: _running_sum_mm(sps[p], ge) for p in pairs}
                mass = {}
                for t, h in pairs:
                    mass[t, h] = cs[h] if t == 0 else add_rows(
                        mass[t - 1, h], los[t - 1], jnp.sum(sps[t - 1, h], axis=-1, keepdims=True))
                ws = {(t, h): keep(t, jnp.exp(zs[t, h] - (sums[t, h] + mass[t, h][los[t]:]))) for t, h in pairs}
                for t, h in pairs:
                    acc = add_rows(acc, los[t], jnp.dot(ws[t, h].astype(MXU_DTYPE), vms[t, h],
                                                        preferred_element_type=F32))
                last = len(kjs) - 1
                cs = [add_rows(mass[last, h], los[last], jnp.sum(sps[last, h], axis=-1, keepdims=True)) for h in heads]
                return (acc, *cs)

            zero_col = jnp.zeros((SB_BQ, 1), F32)
            out, n_run = _sb_keys_descending(
                qi, tile, (jnp.zeros((SB_BQ, LANES), F32),) + (zero_col,) * per_pair, z_bounds, per_pair, has_free)
            o_ref[rows, :] = out[0]
            masses = jnp.zeros((SB_BQ, LANES), F32)
            for hh in range(per_pair):
                masses = jnp.where(lane == hh, out[1 + hh], masses)
            st_ref[rows, :] = masses
            nrun_ref[pl.program_id(0), qi] = n_run

        q_block(0, False)
        lax.fori_loop(1, nq, lambda qi, carry: (q_block(qi, True), carry)[1], 0)

    return pl.pallas_call(
        body, name="sb_fwd",
        grid=(n_pairs,),
        in_specs=[_col_block(t_len, SB_FIRST_BLOCK), _col_block(t_len, SB_FIRST_BLOCK + n_pairs),
                  _col_block(t_len, SB_FIRST_BLOCK + 2 * n_pairs)],
        out_specs=[_col_block(t_len, 0),
                   pl.BlockSpec((t_len, LANES), lambda p: (0, p)),
                   pl.BlockSpec(memory_space=pltpu.SMEM)],
        out_shape=[jax.ShapeDtypeStruct((n_pairs, t_len, LANES), F32),
                   jax.ShapeDtypeStruct((t_len, n_pairs * LANES), F32),
                   jax.ShapeDtypeStruct((n_pairs, nq), jnp.int32)],
        compiler_params=_params("arbitrary"),
    )(cols, cols, cols)


def _sb_bwd_call(cols, sp_total, n_run_all, d_o, dproj, t_len):
    nq = t_len // SB_BQ
    scale = float(SB_HEAD_DIM) ** -0.5
    n_pairs = 512 // LANES
    per_pair = LANES // SB_HEAD_DIM

    def body(q_blk, k_blk, v_blk, st_ref, nrun_ref, do_blk, dproj_in_ref, d_ref):
        q_ref, k_ref, v_ref, do_ref = q_blk.at[0], k_blk.at[0], v_blk.at[0], do_blk.at[0]
        lane = lax.broadcasted_iota(jnp.int32, (1, LANES), 1)
        row_i, col_i, sq_r, sq_c = _sb_iotas()
        lt = (sq_r < sq_c).astype(MXU_DTYPE)
        le = (sq_r <= sq_c).astype(MXU_DTYPE)
        hms = [((lane // SB_HEAD_DIM) == hh).astype(F32) for hh in range(per_pair)]
        d_ref[1] = jnp.zeros((t_len, LANES), F32)
        d_ref[2] = jnp.zeros((t_len, LANES), F32)

        def q_block(qi, has_free):
            r0 = qi * SB_BQ if isinstance(qi, int) else pl.multiple_of(qi * SB_BQ, SB_BQ)
            rows = pl.ds(r0, SB_BQ)
            q_all, do_all = q_ref[rows, :], do_ref[rows, :]
            qms = [(q_all * (hm * scale)).astype(MXU_DTYPE) for hm in hms]
            doms = [(do_all * hm).astype(MXU_DTYPE) for hm in hms]
            masses = st_ref[rows, :]
            totals = [jnp.sum(jnp.where(lane == hh, masses, 0.0), axis=-1, keepdims=True) for hh in range(per_pair)]

            def tile(kjs, masked, kc, los=None):
                heads = range(per_pair)
                los = los or [0] * len(kjs)
                pairs = [(t, h) for t in range(len(kjs)) for h in heads]
                add_rows = lambda full, lo, part: full + part if lo == 0 else jnp.concatenate(
                    [full[:lo], full[lo:] + part], axis=0)
                rsum = lambda a: jnp.sum(a, axis=-1, keepdims=True)
                dq, cls, gls = kc[0], list(kc[1:1 + per_pair]), list(kc[1 + per_pair:])
                s0s = [kj * SB_BLOCK if isinstance(kj, int) else pl.multiple_of(kj * SB_BLOCK, SB_BLOCK) for kj in kjs]
                k_alls = [k_ref[pl.ds(s0, SB_BLOCK), :] for s0 in s0s]
                v_alls = [v_ref[pl.ds(s0, SB_BLOCK), :] for s0 in s0s]
                kbs = [k_all.astype(MXU_DTYPE) for k_all in k_alls]
                vms = {(t, h): (v_alls[t] * hms[h]).astype(MXU_DTYPE) for t, h in pairs}
                kms = {(t, h): (k_alls[t] * (hms[h] * scale)).astype(MXU_DTYPE) for t, h in pairs}
                q_live = {(t, h): qms[h][los[t]:] for t, h in pairs}
                do_live = {(t, h): doms[h][los[t]:] for t, h in pairs}
                zs = {p: lax.dot_general(q_live[p], kbs[p[0]], _NT, preferred_element_type=F32) for p in pairs}
                das = {p: lax.dot_general(do_live[p], vms[p], _NT, preferred_element_type=F32) for p in pairs}
                masks = [(col_i[lo:] + s0) < (row_i[lo:] + r0) if m else None for m, lo, s0 in zip(masked, los, s0s)]
                keep = lambda t, a: a if masks[t] is None else jnp.where(masks[t], a, 0.0)
                sp_alls = {p: _softplus(zs[p]) for p in pairs}
                sps = {(t, h): keep(t, sp_alls[t, h]) for t, h in pairs}
                lefts = {p: _running_sum_mm(sps[p], lt) for p in pairs}
                cl = {}
                for t, h in pairs:
                    cl[t, h] = cls[h] if t == 0 else add_rows(cl[t - 1, h], los[t - 1], rsum(sps[t - 1, h]))
                ws = {(t, h): keep(t, jnp.exp(zs[t, h] - ((totals[h] - cl[t, h])[los[t]:] - lefts[t, h])))
                      for t, h in pairs}
                gs = {p: das[p] * ws[p] for p in pairs}
                g_sums = {p: _running_sum_mm(gs[p], le) for p in pairs}
                gl = {}
                for t, h in pairs:
                    gl[t, h] = gls[h] if t == 0 else add_rows(gl[t - 1, h], los[t - 1], rsum(gs[t - 1, h]))
                dzs = {(t, h): keep(t, gs[t, h] - jnp.exp(zs[t, h] - sp_alls[t, h]) * (gl[t, h][los[t]:] + g_sums[t, h])
                               ).astype(MXU_DTYPE) for t, h in pairs}
                for t in range(len(kjs)):
                    dk_t = jnp.zeros((SB_BLOCK, LANES), F32)
                    dv_t = jnp.zeros((SB_BLOCK, LANES), F32)
                    for h in heads:
                        dq = add_rows(dq, los[t], jnp.dot(dzs[t, h], kms[t, h], preferred_element_type=F32))
                        dk_t = dk_t + lax.dot_general(dzs[t, h], q_live[t, h], _TN, preferred_element_type=F32)
                        dv_t = dv_t + lax.dot_general(ws[t, h].astype(MXU_DTYPE), do_live[t, h], _TN,
                                                      preferred_element_type=F32)
                    d_ref[1, pl.ds(s0s[t], SB_BLOCK), :] += dk_t
                    d_ref[2, pl.ds(s0s[t], SB_BLOCK), :] += dv_t
                last = len(kjs) - 1
                cls = [add_rows(cl[last, h], los[last], rsum(sps[last, h])) for h in heads]
                gls = [add_rows(gl[last, h], los[last], rsum(gs[last, h])) for h in heads]
                return (dq, *cls, *gls)

            zero_col = jnp.zeros((SB_BQ, 1), F32)
            out = _sb_keys_ascending(qi, nrun_ref[pl.program_id(0), qi], tile,
                                     (jnp.zeros((SB_BQ, LANES), F32),) + (zero_col,) * (2 * per_pair), has_free)
            d_ref[0, rows, :] = out[0]

        q_block(0, False)
        lax.fori_loop(1, nq, lambda qi, carry: (q_block(qi, True), carry)[1], 0)

    return pl.pallas_call(
        body, name="sb_bwd",
        grid=(n_pairs,),
        in_specs=[_col_block(t_len, SB_FIRST_BLOCK), _col_block(t_len, SB_FIRST_BLOCK + n_pairs),
                  _col_block(t_len, SB_FIRST_BLOCK + 2 * n_pairs),
                  pl.BlockSpec((t_len, LANES), lambda p: (0, p)),
                  pl.BlockSpec(memory_space=pltpu.SMEM), _col_block(t_len, 0), _HBM],
        out_specs=pl.BlockSpec((3, t_len, LANES), lambda p: (DPROJ_SB_SLOT // 3, 0, p)),
        out_shape=jax.ShapeDtypeStruct(dproj.shape, dproj.dtype),
        input_output_aliases={6: 0},
        compiler_params=_params("arbitrary"),
    )(cols, cols, cols, sp_total, n_run_all, d_o, dproj)


def _conv_taps(xin, rows, t_len):
    taps = []
    for i in range(CONV_WIDTH):
        shift = CONV_WIDTH - 1 - i
        if shift == 0:
            taps.append(xin)
        else:
            taps.append(jnp.where(rows >= shift, pltpu.roll(xin, shift, axis=0), 0.0))
    return taps


def _gdn_prep_body_common(x_ref, w_ref, t_len):
    j = pl.program_id(0)
    xin = x_ref[...]
    rows = lax.broadcasted_iota(jnp.int32, (t_len, LANES), 0)
    taps = _conv_taps(xin, rows, t_len)
    pre = taps[0] * w_ref[0:1, :]
    for i in range(1, CONV_WIDTH):
        pre = pre + taps[i] * w_ref[i:i + 1, :]
    sg = _sigmoid(pre)
    act = pre * sg
    is_qk = j < 2 * GDN_HEADS
    nrm = jnp.where(is_qk, lax.rsqrt(jnp.sum(act * act, axis=-1, keepdims=True) + EPS), 1.0)
    sc = jnp.where(j < GDN_HEADS, float(GDN_HEAD_DIM) ** -0.5, 1.0)
    return j, rows, taps, pre, sg, act, is_qk, nrm, sc


def _gdn_prep_call(cols, conv_w, t_len, after):
    def body(x_blk, w_ref, after_ref, out_ref):
        _, _, _, _, _, act, _, nrm, sc = _gdn_prep_body_common(x_blk.at[0], w_ref, t_len)
        out_ref[...] = act * nrm * sc

    return pl.pallas_call(
        body, name="gdn_prep",
        grid=(3 * GDN_HEADS,),
        in_specs=[_col_block(t_len, GDN_FIRST_BLOCK),
                  pl.BlockSpec((CONV_WIDTH, LANES), lambda j: (0, j)),
                  pl.BlockSpec(memory_space=pl.ANY)],
        out_specs=pl.BlockSpec((t_len, LANES), lambda j: (0, j)),
        out_shape=jax.ShapeDtypeStruct((t_len, 3 * 512), F32),
        compiler_params=_params("arbitrary"),
    )(cols, conv_w, after)


def _gdn_prep_bwd_call(cols, conv_w, d_act3, dproj, t_len):
    def body(x_blk, w_ref, d_ref, dproj_in_ref, dx_ref, dw_ref):
        _, rows, taps, pre, sg, act, is_qk, nrm, sc = _gdn_prep_body_common(x_blk.at[0], w_ref, t_len)
        d_out = d_ref[0]
        dn = d_out * sc
        d_norm = nrm * dn - act * (nrm * nrm * nrm) * jnp.sum(dn * act, axis=-1, keepdims=True)
        d_act = jnp.where(is_qk, d_norm, d_out)
        d_pre = d_act * sg * (1.0 + pre * (1.0 - sg))
        dx = d_pre * w_ref[CONV_WIDTH - 1:CONV_WIDTH, :]
        dw_ref[CONV_WIDTH - 1:CONV_WIDTH, :] = jnp.sum(d_pre * taps[CONV_WIDTH - 1], axis=0, keepdims=True)
        for i in range(CONV_WIDTH - 1):
            shift = CONV_WIDTH - 1 - i
            up = jnp.where(rows < t_len - shift, pltpu.roll(d_pre, t_len - shift, axis=0), 0.0)
            dx = dx + up * w_ref[i:i + 1, :]
            dw_ref[i:i + 1, :] = jnp.sum(d_pre * taps[i], axis=0, keepdims=True)
        dx_ref[0] = dx

    return pl.pallas_call(
        body, name="gdn_prep_bwd",
        grid=(3 * GDN_HEADS,),
        in_specs=[_col_block(t_len, GDN_FIRST_BLOCK),
                  pl.BlockSpec((CONV_WIDTH, LANES), lambda j: (0, j)),
                  pl.BlockSpec((1, t_len, LANES), lambda j: (j // GDN_HEADS, 0, j % GDN_HEADS)), _HBM],
        out_specs=[pl.BlockSpec((1, t_len, LANES), lambda j: (DPROJ_GDN_SLOT + j // GDN_HEADS, 0, j % GDN_HEADS)),
                   pl.BlockSpec((CONV_WIDTH, LANES), lambda j: (0, j))],
        out_shape=[jax.ShapeDtypeStruct(dproj.shape, dproj.dtype),
                   jax.ShapeDtypeStruct((CONV_WIDTH, 3 * 512), F32)],
        input_output_aliases={3: 0},
        compiler_params=_params("arbitrary"),
    )(cols, conv_w, d_act3, dproj)


def _chunk_cumsum_matrix():
    r = lax.broadcasted_iota(jnp.int32, (LANES, LANES), 0)
    c = lax.broadcasted_iota(jnp.int32, (LANES, LANES), 1)
    return ((r <= c) & ((r // CHUNK) == (c // CHUNK))).astype(F32)


def _gdn_gates_call(ps, pst, alog_l, dtb_l, alog_c, dtb_c, t_len):
    def body(ps_ref, pst_ref, al_ref, dl_ref, ac_ref, dc_ref, beta_ref, gcol_ref, grow_ref):
        upper = _chunk_cumsum_matrix()
        lower = upper.T
        psv = ps_ref[...]
        beta_ref[...] = _sigmoid(psv)
        g_l = -jnp.exp(al_ref[...]) * _softplus(psv + dl_ref[...])
        g_r = -jnp.exp(ac_ref[...]) * _softplus(pst_ref[...] + dc_ref[...])
        for w in range(t_len // LANES):
            sl = slice(w * LANES, (w + 1) * LANES)
            gcol_ref[sl, :] = _mx(lower, g_l[sl, :])
            grow_ref[:, sl] = _mx(g_r[:, sl], upper)

    vm = pl.BlockSpec(memory_space=pltpu.VMEM)
    return pl.pallas_call(
        body, name="gdn_gates",
        in_specs=[vm] * 6, out_specs=[vm] * 3,
        out_shape=[jax.ShapeDtypeStruct((t_len, LANES), F32),
                   jax.ShapeDtypeStruct((t_len, LANES), F32),
                   jax.ShapeDtypeStruct((8, t_len), F32)],
        compiler_params=pltpu.CompilerParams(vmem_limit_bytes=VMEM_LIMIT_BYTES),
    )(ps, pst, alog_l, dtb_l, alog_c, dtb_c)


def _gdn_gates_bwd_call(ps, alog_l, dtb_l, d_l, t_len):
    def body(ps_ref, al_ref, dl_ref, d_ref, dps_ref, gal_ref, gdt_ref):
        lane = lax.broadcasted_iota(jnp.int32, (1, LANES), 1)
        psv = ps_ref[...]
        dv = d_ref[...]
        beta = _sigmoid(psv)
        ea = jnp.exp(al_ref[...])
        arg = psv + dl_ref[...]
        g = -ea * _softplus(arg)
        d_a = dv * (-ea) * _sigmoid(arg)
        is_a = (lane >= GDN_HEADS) & (lane < 2 * GDN_HEADS)
        dps_ref[...] = jnp.where(lane < GDN_HEADS, dv * beta * (1.0 - beta), jnp.where(is_a, d_a, 0.0))
        gdt_ref[...] = jnp.where(is_a, jnp.sum(d_a, axis=0, keepdims=True), 0.0)
        gal_ref[...] = jnp.where(is_a, jnp.sum(dv * g, axis=0, keepdims=True), 0.0)

    vm = pl.BlockSpec(memory_space=pltpu.VMEM)
    return pl.pallas_call(
        body, name="gdn_gates_bwd",
        in_specs=[vm] * 4, out_specs=[vm] * 3,
        out_shape=[jax.ShapeDtypeStruct((t_len, LANES), F32),
                   jax.ShapeDtypeStruct((1, LANES), F32),
                   jax.ShapeDtypeStruct((1, LANES), F32)],
        compiler_params=pltpu.CompilerParams(vmem_limit_bytes=VMEM_LIMIT_BYTES),
    )(ps, alog_l, dtb_l, d_l)


def _bm(a, b):
    return _m3_general(a, b, _BNN)


def _bm_nt(a, b):
    return _m3_general(a, b, _BNT)


def _bm_tn(a, b):
    return _m3_general(a, b, _BTN)


def _heads_of(ref, rows):
    return jnp.stack([ref[rows, h * GDN_HEAD_DIM:(h + 1) * GDN_HEAD_DIM] for h in range(GDN_HEADS)])


def _chunk_terms(q_ref, k_ref, v_ref, b_ref, gc_ref, gr_ref, c, incl, strict, n=1, scores=True):
    r0 = c * CHUNK if isinstance(c, int) else pl.multiple_of(c * CHUNK, CHUNK)
    rows = pl.ds(r0, n * CHUNK)
    per_chunk = lambda x: x.reshape(GDN_HEADS * n, CHUNK, x.shape[-1])
    q, k, v = (per_chunk(_heads_of(ref, rows)) for ref in (q_ref, k_ref, v_ref))
    lane_ids = lax.broadcasted_iota(jnp.int32, (1, LANES), 1)
    pick = lambda slab, first: jnp.stack([jnp.sum(jnp.where(lane_ids == first + h, slab, 0.0), axis=-1, keepdims=True)
                                          for h in range(GDN_HEADS)])
    b = per_chunk(pick(b_ref[rows, :], 0))
    gc = per_chunk(pick(gc_ref[rows, :], GDN_HEADS))
    gr = gr_ref[:, c] if n == 1 else gr_ref[:, c:c + n].reshape(GDN_HEADS * n, 1, CHUNK)
    dm = jnp.where(incl, jnp.exp(jnp.where(incl, gc - gr, 0.0)), 0.0)
    kb = k * b
    vb = v * b
    e = jnp.exp(gc)
    a = p = None
    if scores:
        kk_qk = _bm_nt(jnp.concatenate([kb, q], axis=1), k)
        a = jnp.where(strict, kk_qk[:, :CHUNK] * dm, 0.0)
        p = jnp.where(incl, kk_qk[:, CHUNK:] * dm, 0.0)
    gl = gc[:, CHUNK - 1:CHUNK, :]
    eg = jnp.exp(gl - gc)
    return rows, q, k, v, b, gc, dm, kb, vb, e, a, p, gl, eg


def _unit_lower_inverse(a, eye):
    x = -a
    tm = eye + x
    xp = _bm(x, x)
    for _ in range(4):
        both = _bm(jnp.concatenate([xp, tm], axis=1), xp)
        tm = tm + both[:, CHUNK:]
        xp = both[:, :CHUNK]
    return tm + _bm(tm, xp)


def _gdn_specs(t_len, n_chunks, reverse):
    cps = GDN_CHUNKS_PER_STEP
    steps = n_chunks // cps
    at = (lambda g: steps - 1 - g) if reverse else (lambda g: g)
    rows_blk = lambda width, part=0: pl.BlockSpec((cps * CHUNK, width), lambda g: (at(g), part))
    gate_r = pl.BlockSpec((GDN_HEADS, cps, 1, CHUNK), lambda g: (0, at(g), 0, 0))
    per_chunk = lambda r, c: pl.BlockSpec((GDN_HEADS, cps, r, c), lambda g: (0, at(g), 0, 0))
    return cps, steps, rows_blk, gate_r, per_chunk


def _gdn_fwd_call(gact, beta_c, gam_c, gam_r, t_len):
    n_chunks = t_len // CHUNK
    dk = GDN_HEAD_DIM
    width = GDN_HEADS * dk
    cps, steps, rows_blk, gate_r, per_chunk = _gdn_specs(t_len, n_chunks, False)

    def body(q_ref, k_ref, v_ref, b_ref, gc_ref, gr_ref, o_ref, s_ref, t_ref, a_ref, p_ref, uw_ref, vn_ref, state_ref):
        row = lax.broadcasted_iota(jnp.int32, (CHUNK, CHUNK), 0)
        col = lax.broadcasted_iota(jnp.int32, (CHUNK, CHUNK), 1)
        incl, strict = row >= col, row > col
        eye = (row == col).astype(F32)

        @pl.when(pl.program_id(0) == 0)
        def _():
            state_ref[...] = jnp.zeros_like(state_ref)

        _, q, k, v, b, gc, dm, kb, vb, e, a, p, gl, eg = _chunk_terms(
            q_ref, k_ref, v_ref, b_ref, gc_ref, gr_ref, 0, incl, strict, cps)
        tm = _unit_lower_inverse(a, eye)
        uw = _bm(tm, jnp.concatenate([vb, kb * e], axis=2))
        w_qe = jnp.concatenate([uw[:, :, dk:], q * e], axis=1)
        u, kd, decay = uw[:, :, :dk], k * eg, jnp.exp(gl)
        per_chunk_block = lambda x: x.reshape(GDN_HEADS, cps, CHUNK, CHUNK)
        t_ref[...], a_ref[...], p_ref[...] = per_chunk_block(tm), per_chunk_block(a), per_chunk_block(p)
        uw_heads = uw.reshape(GDN_HEADS, cps * CHUNK, 2 * dk)
        for h in range(GDN_HEADS):
            uw_ref[:, h * 2 * dk:(h + 1) * 2 * dk] = uw_heads[h]

        of_chunk = lambda x, c: jnp.stack([x[h * cps + c] for h in range(GDN_HEADS)])
        s = state_ref[...]
        for c in range(cps):
            ws_qs = _bm(of_chunk(w_qe, c), s)
            vn = of_chunk(u, c) - ws_qs[:, :CHUNK]
            o = ws_qs[:, CHUNK:] + _bm(of_chunk(p, c), vn)
            for h in range(GDN_HEADS):
                o_ref[c * CHUNK:(c + 1) * CHUNK, h * dk:(h + 1) * dk] = o[h]
                vn_ref[c * CHUNK:(c + 1) * CHUNK, h * dk:(h + 1) * dk] = vn[h]
            s_ref[:, c] = s
            s = s * of_chunk(decay, c) + _bm_tn(of_chunk(kd, c), vn)
        state_ref[...] = s

    scores = jax.ShapeDtypeStruct((GDN_HEADS, n_chunks, CHUNK, CHUNK), F32)
    return pl.pallas_call(
        body, name="gdn_fwd",
        grid=(steps,),
        in_specs=[rows_blk(width, 0), rows_blk(width, 1), rows_blk(width, 2), rows_blk(LANES), rows_blk(LANES), gate_r],
        out_specs=[rows_blk(width), per_chunk(dk, dk), per_chunk(CHUNK, CHUNK), per_chunk(CHUNK, CHUNK),
                   per_chunk(CHUNK, CHUNK), rows_blk(2 * width), rows_blk(width)],
        out_shape=[jax.ShapeDtypeStruct((t_len, width), F32),
                   jax.ShapeDtypeStruct((GDN_HEADS, n_chunks, dk, dk), F32), scores, scores, scores,
                   jax.ShapeDtypeStruct((t_len, 2 * width), F32), jax.ShapeDtypeStruct((t_len, width), F32)],
        scratch_shapes=[pltpu.VMEM((GDN_HEADS, dk, dk), F32)],
        compiler_params=_params("arbitrary"),
    )(gact, gact, gact, beta_c, gam_c, gam_r)


def _gdn_bwd_call(gact, beta_c, gam_c, gam_r, saved, d_o, t_len, scatter=()):
    n_chunks = t_len // CHUNK
    dk = GDN_HEAD_DIM
    width = GDN_HEADS * dk
    cps, steps, rows_blk, gate_r, per_chunk = _gdn_specs(t_len, n_chunks, True)
    nx = len(scatter)
    n_in = 13

    def body(*refs):
        q_ref, k_ref, v_ref, b_ref, gc_ref, gr_ref = refs[:6]
        saved_refs, do_ref = refs[6:12], refs[12]
        d_ref, dgate_ref = refs[n_in + nx:n_in + 2 + nx]
        dstate_ref = refs[n_in + 2 + 2 * nx]
        copies = lambda: _direct_copies(refs[n_in:n_in + nx], refs[n_in + 2 + nx:n_in + 2 + 2 * nx],
                                        *refs[n_in + 3 + 2 * nx:], (True,) * nx)
        if nx:
            pl.when(pl.program_id(0) == 0)(lambda: _start_all(copies()))
        row = lax.broadcasted_iota(jnp.int32, (CHUNK, CHUNK), 0)
        col = lax.broadcasted_iota(jnp.int32, (CHUNK, CHUNK), 1)
        incl, strict = row >= col, row > col
        ng = GDN_BWD_GROUP
        nb = GDN_HEADS * ng
        upper = jnp.broadcast_to((row <= col).astype(F32), (nb, CHUNK, CHUNK))
        ones = jnp.ones((nb, CHUNK, LANES), F32)
        last_row = lax.broadcasted_iota(jnp.int32, (CHUNK, 1), 0) == CHUNK - 1
        lane_ids = lax.broadcasted_iota(jnp.int32, (1, LANES), 1)
        rsum = lambda m: jnp.sum(m, axis=-1, keepdims=True)
        total = lambda m: jnp.sum(rsum(m), axis=1, keepdims=True)
        of_chunk = lambda x, c: jnp.stack([x[h * ng + c] for h in range(GDN_HEADS)])

        @pl.when(pl.program_id(0) == 0)
        def _():
            dstate_ref[...] = jnp.zeros_like(dstate_ref)

        for c0 in range(cps - ng, -1, -ng):
            group(c0, q_ref, k_ref, v_ref, b_ref, gc_ref, gr_ref, saved_refs, do_ref, d_ref, dgate_ref, dstate_ref,
                  incl, strict, upper, ones, last_row, lane_ids, rsum, total, of_chunk)
        if nx:
            pl.when(pl.program_id(0) == steps - 1)(lambda: _wait_all(copies()))

    def group(c0, q_ref, k_ref, v_ref, b_ref, gc_ref, gr_ref, saved_refs, do_ref, d_ref, dgate_ref, dstate_ref,
              incl, strict, upper, ones, last_row, lane_ids, rsum, total, of_chunk):
        ng = GDN_BWD_GROUP
        nb = GDN_HEADS * ng
        rows = pl.ds(c0 * CHUNK, ng * CHUNK)
        s_ref, t_ref, a_ref, p_ref, uw_ref, vn_ref = saved_refs
        _, q, k, v, b, gc, dm, kb, vb, e, _, _, gl, eg = _chunk_terms(
            q_ref, k_ref, v_ref, b_ref, gc_ref, gr_ref, c0, incl, strict, ng, scores=False)
        s = s_ref[:, c0:c0 + ng].reshape(nb, dk, dk)
        tm = t_ref[:, c0:c0 + ng].reshape(nb, CHUNK, CHUNK)
        a = a_ref[:, c0:c0 + ng].reshape(nb, CHUNK, CHUNK)
        p = p_ref[:, c0:c0 + ng].reshape(nb, CHUNK, CHUNK)
        d_out = _heads_of(do_ref, rows).reshape(nb, CHUNK, dk)
        vn = _heads_of(vn_ref, rows).reshape(nb, CHUNK, dk)
        uw = jnp.stack([uw_ref[rows, h * 2 * dk:(h + 1) * 2 * dk] for h in range(GDN_HEADS)]).reshape(nb, CHUNK, 2 * dk)
        u, w = uw[:, :, :dk], uw[:, :, dk:]
        el = jnp.exp(gl)
        kbe = kb * e
        qe = q * e
        kd = k * eg
        pt_do = _bm_tn(p, d_out)
        qet_do = _bm_tn(qe, d_out)

        ds = dstate_ref[...]
        d_vn_c, ds_c = [None] * ng, [None] * ng
        for c in range(ng - 1, -1, -1):
            ds_c[c] = ds
            d_vn_c[c] = of_chunk(pt_do, c) + _bm(of_chunk(kd, c), ds)
            ds = of_chunk(el, c) * ds + of_chunk(qet_do, c) - _bm_tn(of_chunk(w, c), d_vn_c[c])
        dstate_ref[...] = ds
        by_chunk = lambda xs: jnp.stack([xs[c][h] for h in range(GDN_HEADS) for c in range(ng)])
        d_vn, ds = by_chunk(d_vn_c), by_chunk(ds_c)

        on_s = _bm_nt(jnp.concatenate([d_out, d_vn], axis=1), s)
        d_qe, d_w = on_s[:, :CHUNK], -on_s[:, CHUNK:]
        d_p = jnp.where(incl, _bm_nt(d_out, vn), 0.0)
        d_kd = _bm_nt(vn, ds)
        d_both = _bm_tn(tm, jnp.concatenate([d_vn, d_w], axis=2))
        d_vb, d_kbe = d_both[:, :, :dk], d_both[:, :, dk:]
        d_a = -jnp.where(strict, _bm_nt(d_both, uw), 0.0)
        m = d_a * dm
        n = d_p * dm
        on_k = _bm(jnp.concatenate([m, n], axis=1), k)
        d_kb = on_k[:, :CHUNK] + d_kbe * e
        d_q = on_k[:, CHUNK:] + d_qe * e
        d_k = (_bm_tn(jnp.concatenate([m, n], axis=1), jnp.concatenate([kb, q], axis=1))
               + d_kd * eg + b * d_kb)
        d_v = b * d_vb
        r = d_a * a + d_p * p
        kd_term = rsum(d_kd * kd)
        d_gl = total(ds * s) * el + jnp.sum(kd_term, axis=1, keepdims=True)
        d_gam = (rsum(r) - _times_exact(r, ones, _BTN)[:, :, 0:1] + rsum(d_qe * qe) + rsum(d_kbe * kbe) - kd_term
                 + jnp.where(last_row, d_gl, 0.0))
        d_beta = rsum(d_kb * k) + rsum(d_vb * v)
        d_g = _exact_times(upper, d_gam * ones, _BNN)[:, :, 0:1]
        per_head = lambda x: x.reshape(GDN_HEADS, ng * CHUNK, x.shape[-1])
        d_q, d_k, d_v, d_beta, d_g = (per_head(x) for x in (d_q, d_k, d_v, d_beta, d_g))
        gates = jnp.zeros((ng * CHUNK, LANES), F32)
        for h in range(GDN_HEADS):
            lanes = slice(h * dk, (h + 1) * dk)
            d_ref[0, rows, lanes] = d_q[h]
            d_ref[1, rows, lanes] = d_k[h]
            d_ref[2, rows, lanes] = d_v[h]
            gates = gates + (jnp.where(lane_ids == h, d_beta[h], 0.0)
                             + jnp.where(lane_ids == GDN_HEADS + h, d_g[h], 0.0))
        dgate_ref[rows, :] = gates

    d_spec = pl.BlockSpec((3, cps * CHUNK, width), lambda g: (0, steps - 1 - g, 0))
    return pl.pallas_call(
        body, name="gdn_bwd",
        grid=(steps,),
        in_specs=[rows_blk(width, 0), rows_blk(width, 1), rows_blk(width, 2), rows_blk(LANES), rows_blk(LANES), gate_r,
                  per_chunk(dk, dk), per_chunk(CHUNK, CHUNK), per_chunk(CHUNK, CHUNK), per_chunk(CHUNK, CHUNK),
                  rows_blk(2 * width), rows_blk(width), rows_blk(width)] + [_HBM] * nx,
        out_specs=[d_spec, rows_blk(LANES)] + [_HBM] * nx,
        out_shape=[jax.ShapeDtypeStruct((3, t_len, width), F32),
                   jax.ShapeDtypeStruct((t_len, LANES), F32)] + _direct_out_shapes(scatter, (True,) * nx),
        scratch_shapes=[pltpu.VMEM((GDN_HEADS, dk, dk), F32)] + (_direct_semaphores(nx) if nx else []),
        compiler_params=_params("arbitrary"),
    )(gact, gact, gact, beta_c, gam_c, gam_r, *saved, d_o, *scatter)


def _group_sums(x, group):
    rows, width = x.shape
    lane = lax.broadcasted_iota(jnp.int32, (1, LANES), 1)
    out = []
    for t in range(width // LANES):
        seg = x[:, t * LANES:(t + 1) * LANES]
        if group == LANES:
            out.append(jnp.broadcast_to(jnp.sum(seg, axis=-1, keepdims=True), (rows, LANES)))
        else:
            low = jnp.sum(jnp.where(lane < group, seg, 0.0), axis=-1, keepdims=True)
            high = jnp.sum(jnp.where(lane < group, 0.0, seg), axis=-1, keepdims=True)
            out.append(jnp.where(lane < group, low, high))
    return jnp.concatenate(out, axis=1)


def _post_call(o_sb, o_gd, proj_gates, x, target, w_out, sbw, gdw, fw, tm=256):
    t_len, d = x.shape
    half = 512
    sb_blocks = half // LANES

    def body(osb_ref, ogd_ref, zsb_ref, zgd_ref, x_ref, tg_ref, wo_ref, sbw_ref, gdw_ref, fw_ref,
             dx2_ref, dosb_ref, dogd_ref, dz_ref, loss_ref, gfw_ref, gsb_ref, ggd_ref, gwo_ref):
        step = pl.program_id(0)

        @pl.when(step == 0)
        def _():
            loss_ref[...] = jnp.zeros_like(loss_ref)
            gfw_ref[...] = jnp.zeros_like(gfw_ref)
            gsb_ref[...] = jnp.zeros_like(gsb_ref)
            ggd_ref[...] = jnp.zeros_like(ggd_ref)
            gwo_ref[...] = jnp.zeros_like(gwo_ref)

        def head_forward(o, z, w, head_dim):
            r = lax.rsqrt(_group_sums(o * o, head_dim) * (1.0 / head_dim) + EPS)
            nrm = o * r * w
            sg = _sigmoid(z)
            return r, nrm, sg, nrm * (z * sg)

        def head_backward(d_m, o, z, w, head_dim, r, nrm, sg):
            d_n = d_m * (z * sg)
            d_z = d_m * nrm * (sg * (1.0 + z * (1.0 - sg)))
            dnw = d_n * w
            d_o = r * dnw - o * (r * r * r) * (_group_sums(dnw * o, head_dim) * (1.0 / head_dim))
            return d_o, d_z, jnp.sum(d_n * o * r, axis=0, keepdims=True)

        osb = jnp.concatenate([osb_ref[j] for j in range(sb_blocks)], axis=1)
        ogd, zsb, zgd = ogd_ref[...], zsb_ref[...], zgd_ref[...]
        sbw_v, gdw_v = sbw_ref[...], gdw_ref[...]
        r_sb, n_sb, sg_sb, m_sb = head_forward(osb, zsb, sbw_v, SB_HEAD_DIM)
        r_gd, n_gd, sg_gd, m_gd = head_forward(ogd, zgd, gdw_v, GDN_HEAD_DIM)
        mixed = jnp.concatenate([m_sb, m_gd], axis=1).astype(MXU_DTYPE)
        wo = wo_ref[...]
        x2 = x_ref[...] + jnp.dot(mixed, wo, preferred_element_type=F32)
        r2 = lax.rsqrt(jnp.mean(x2 * x2, axis=-1, keepdims=True) + EPS)
        fw_v = fw_ref[...]
        err = x2 * r2 * fw_v - tg_ref[...]
        loss_ref[...] += 0.5 * jnp.sum(jnp.sum(err * err, axis=-1, keepdims=True) * (1.0 / d))
        dy = err * (1.0 / d)
        gg = dy * fw_v
        dx2 = r2 * gg - x2 * ((r2 * r2 * r2) * jnp.mean(gg * x2, axis=-1, keepdims=True))
        gfw_ref[...] += jnp.sum(dy * x2 * r2, axis=0, keepdims=True)
        dx2_ref[...] = dx2
        dx2b = dx2.astype(MXU_DTYPE)
        d_mixed = lax.dot_general(dx2b, wo, _NT, preferred_element_type=F32)
        gwo_ref[...] += lax.dot_general(mixed, dx2b, _TN, preferred_element_type=F32)
        d_osb, d_zsb, gsb = head_backward(d_mixed[:, :half], osb, zsb, sbw_v, SB_HEAD_DIM, r_sb, n_sb, sg_sb)
        d_ogd, d_zgd, ggd = head_backward(d_mixed[:, half:], ogd, zgd, gdw_v, GDN_HEAD_DIM, r_gd, n_gd, sg_gd)
        for j in range(sb_blocks):
            dosb_ref[j] = d_osb[:, j * LANES:(j + 1) * LANES]
        dogd_ref[...] = d_ogd
        dz_ref[0] = d_zsb
        dz_ref[1] = d_zgd
        gsb_ref[...] += gsb
        ggd_ref[...] += ggd

    row_blk = lambda w: pl.BlockSpec((tm, w), lambda i: (i, 0))
    blocks_blk = pl.BlockSpec((sb_blocks, tm, LANES), lambda i: (0, i, 0))
    fixed = lambda r, w: pl.BlockSpec((r, w), lambda i: (0, 0))
    return pl.pallas_call(
        body, name="post",
        grid=(t_len // tm,),
        in_specs=[blocks_blk, row_blk(half),
                  pl.BlockSpec((tm, half), lambda i: (i, 0)),
                  pl.BlockSpec((tm, half), lambda i: (i, 1)),
                  row_blk(d), row_blk(d), fixed(d, d), fixed(1, half), fixed(1, half), fixed(1, d)],
        out_specs=[row_blk(d), blocks_blk, row_blk(half),
                   pl.BlockSpec((2, tm, half), lambda i: (DPROJ_GATE_SLOT // 2, i, 0)),
                   fixed(1, LANES), fixed(1, d), fixed(1, half), fixed(1, half), fixed(d, d)],
        out_shape=[jax.ShapeDtypeStruct((t_len, d), F32), jax.ShapeDtypeStruct((sb_blocks, t_len, LANES), F32),
                   jax.ShapeDtypeStruct((t_len, half), F32),
                   jax.ShapeDtypeStruct((len(DPROJ_PIECE_OF_SLOT), t_len, half), F32),
                     jax.ShapeDtypeStruct((1, LANES), F32), jax.ShapeDtypeStruct((1, d), F32),
                     jax.ShapeDtypeStruct((1, half), F32), jax.ShapeDtypeStruct((1, half), F32),
                     jax.ShapeDtypeStruct((d, d), F32)],
        compiler_params=_params("arbitrary"),
    )(o_sb, o_gd, proj_gates, proj_gates, x, target, w_out, sbw, gdw, fw)


def _piece_of_slot(s):
    return jnp.where(s < DPROJ_GDN_SLOT, s, jnp.where(s < DPROJ_GATE_SLOT, s + 1,
                                                     jnp.where(s == DPROJ_GATE_SLOT, 3, 7)))


def _gw_in_call(h_t, dproj8):
    d, t_len = h_t.shape
    n_piece, _, pw = dproj8.shape

    def body(ht_ref, dp_ref, gw_ref):
        gw_ref[...] = jnp.dot(ht_ref[...], dp_ref[0].astype(MXU_DTYPE), preferred_element_type=F32)

    return pl.pallas_call(
        body, name="gw_in",
        grid=(n_piece,),
        in_specs=[pl.BlockSpec((d, t_len), lambda s: (0, 0)),
                  pl.BlockSpec((1, t_len, pw), lambda s: (s, 0, 0))],
        out_specs=pl.BlockSpec((d, pw), lambda s: (0, _piece_of_slot(s))),
        out_shape=jax.ShapeDtypeStruct((d, n_piece * pw), F32),
        compiler_params=_params("arbitrary"),
    )(h_t, dproj8)


def _slot_of_piece(p):
    return jnp.where(p < DPROJ_GDN_SLOT, p, jnp.where(p == 3, DPROJ_GATE_SLOT, jnp.where(p < 7, p - 1, 7)))


def _gw_in_shards_call(h_t, dproj8, dsmall, out_dtype):
    d, t_len = h_t.shape
    n_piece, _, pw = dproj8.shape
    ns = dsmall.shape[1]
    n_pairs = N_DEV // 2

    def body(ht_ref, dp_ref, ds_ref, chip_ref, prev_ref, gates_ref, send_ref, recv_ref, send_sems, recv_sems):
        p = pl.program_id(0)
        x_pos, y_pos, c = lax.axis_index("x"), lax.axis_index("y"), lax.axis_index("c")
        to_sibling = lambda pair: pltpu.make_async_remote_copy(
            src_ref=send_ref.at[pair], dst_ref=recv_ref.at[pair], send_sem=send_sems.at[pair],
            recv_sem=recv_sems.at[pair], device_id=(x_pos, y_pos, 1 - c), device_id_type=_MESH)

        @pl.when(p == 0)
        def _():
            gates_ref[...] = jnp.dot(ht_ref[...], ds_ref[...].astype(MXU_DTYPE), preferred_element_type=F32)

        def emit(s, tail):
            x = jnp.concatenate([prev_ref[...], tail], axis=1)
            y = x if s == 0 else pltpu.roll(x, SHARD_PAD - s, axis=1)
            shard = y[:, :SHARD_COLS].astype(out_dtype)

            @pl.when(c == s % 2)
            def _():
                chip_ref[s // 2] = shard

            @pl.when(c != s % 2)
            def _():
                send_ref[s // 2] = shard
                to_sibling(s // 2).start()

        @pl.when(p < n_piece)
        def _():
            cur = jnp.dot(ht_ref[...], dp_ref[0].astype(MXU_DTYPE), preferred_element_type=F32)
            for s in range(n_piece - 1):
                pl.when(p == s + 1)(functools.partial(emit, s, cur[:, :SHARD_PAD - pw]))
            prev_ref[...] = cur

        @pl.when(p == n_piece)
        def _():
            emit(n_piece - 1, gates_ref[...])
            for pair in range(n_pairs):
                to_sibling(pair).wait_send()
            for pair in range(n_pairs):
                to_sibling(pair).wait_recv()
                chip_ref[pair] = (chip_ref[pair].astype(F32) + recv_ref[pair].astype(F32)).astype(out_dtype)

    shards_of_side = lambda: pltpu.VMEM((n_pairs, d, SHARD_COLS), out_dtype)
    return pl.pallas_call(
        body, name="gw_in",
        grid=(n_piece + 1,),
        in_specs=[pl.BlockSpec((d, t_len), lambda p: (0, 0)),
                  pl.BlockSpec((1, t_len, pw), lambda p: (_slot_of_piece(jnp.minimum(p, n_piece - 1)), 0, 0)),
                  pl.BlockSpec((t_len, ns), lambda p: (0, 0))],
        out_specs=pl.BlockSpec((n_pairs, d, SHARD_COLS), lambda p: (0, 0, 0)),
        out_shape=jax.ShapeDtypeStruct((n_pairs, d, SHARD_COLS), out_dtype),
        scratch_shapes=[pltpu.VMEM((d, pw), F32), pltpu.VMEM((d, ns), F32), shards_of_side(), shards_of_side(),
                        pltpu.SemaphoreType.DMA((n_pairs,)), pltpu.SemaphoreType.DMA((n_pairs,))],
        compiler_params=_params("arbitrary"),
    )(h_t, dproj8, dsmall)


def _gw_small_call(h_t, dsmall, tm=512):
    d, t_len = h_t.shape
    ns = dsmall.shape[1]

    def body(ht_ref, dp_ref, gw_ref):
        @pl.when(pl.program_id(0) == 0)
        def _():
            gw_ref[...] = jnp.zeros_like(gw_ref)

        gw_ref[...] += jnp.dot(ht_ref[...], dp_ref[...].astype(MXU_DTYPE), preferred_element_type=F32)

    return pl.pallas_call(
        body, name="gw_small",
        grid=(t_len // tm,),
        in_specs=[pl.BlockSpec((d, tm), lambda t: (0, t)),
                  pl.BlockSpec((tm, ns), lambda t: (t, 0))],
        out_specs=pl.BlockSpec((d, ns), lambda t: (0, 0)),
        out_shape=jax.ShapeDtypeStruct((d, ns), F32),
        compiler_params=_params("arbitrary"),
    )(h_t, dsmall)


def _dx_call(dproj8, dsmall, w_main, w_small, x, r, dx2, norm_w, chip_scatter=(), peer_scatter=(), tm=256):
    t_len, d = x.shape
    n_piece, _, pw = dproj8.shape
    ns = dsmall.shape[1]
    nx = len(chip_scatter)
    n_peer = len(peer_scatter)
    steps = t_len // tm
    n_in = 8 + nx + n_peer
    n_out = 2 + nx + n_peer

    def body(*refs):
        dp_ref, ds_ref, wm_ref, ws_ref, x_ref, r_ref, dx2_ref, nw_ref = refs[:8]
        gx_ref, gnw_ref = refs[n_in:n_in + 2]
        scratch = refs[n_in + n_out:]
        copies = lambda: _chip_copies(refs[8:8 + nx], refs[n_in + 2:n_in + 2 + nx], *scratch[:3])
        if nx:
            pl.when(pl.program_id(0) == 0)(lambda: _start_all(copies()))

        @pl.when(pl.program_id(0) == 0)
        def _():
            gnw_ref[...] = jnp.zeros_like(gnw_ref)

        dh = lax.dot_general(ds_ref[...].astype(MXU_DTYPE), ws_ref[...], _NT, preferred_element_type=F32)
        for s, p in enumerate(DPROJ_PIECE_OF_SLOT):
            dh = dh + lax.dot_general(dp_ref[s].astype(MXU_DTYPE), wm_ref[:, p * pw:(p + 1) * pw], _NT,
                                      preferred_element_type=F32)
        xv, rv = x_ref[...], r_ref[...]
        dn = dh * nw_ref[...]
        gx_ref[...] = dx2_ref[...] + rv * dn - xv * ((rv * rv * rv) * jnp.mean(dn * xv, axis=-1, keepdims=True))
        gnw_ref[...] += jnp.sum(dh * xv * rv, axis=0, keepdims=True)

        @pl.when(pl.program_id(0) == steps - 1)
        def _():
            if n_peer:
                small_ref, parts_ref = refs[8 + nx:n_in]
                small_buf = scratch[6]
                small_buf[...] = small_ref[...]
                small_buf[0:1, :] = gnw_ref[...]
                peer_copies = _direct_copies([small_buf, parts_ref], refs[n_in + 2 + nx:n_in + n_out], *scratch[3:6],
                                             [False, True])
                _start_all(peer_copies)
            if nx:
                _wait_all(copies())
            if n_peer:
                _wait_all(peer_copies)

    assert n_peer in (0, 2) and (nx == 1 or not n_peer)
    peer_in_specs = [pl.BlockSpec(peer_scatter[0].shape, lambda i: (0, 0)), _HBM] if n_peer else []
    peer_scratch = _direct_semaphores(n_peer) + [pltpu.VMEM(peer_scatter[0].shape, F32)] if n_peer else []
    return pl.pallas_call(
        body, name="dx",
        grid=(steps,),
        in_specs=[pl.BlockSpec((n_piece, tm, pw), lambda i: (0, i, 0)),
                  pl.BlockSpec((tm, ns), lambda i: (i, 0)),
                  pl.BlockSpec((d, n_piece * pw), lambda i: (0, 0)),
                  pl.BlockSpec((d, ns), lambda i: (0, 0)),
                  pl.BlockSpec((tm, d), lambda i: (i, 0)),
                  pl.BlockSpec((tm, 1), lambda i: (i, 0)),
                  pl.BlockSpec((tm, d), lambda i: (i, 0)),
                  pl.BlockSpec((1, d), lambda i: (0, 0))] + [_HBM] * nx + peer_in_specs,
        out_specs=[pl.BlockSpec((tm, d), lambda i: (i, 0)),
                   pl.BlockSpec((1, d), lambda i: (0, 0))] + [_HBM] * (nx + n_peer),
        out_shape=[jax.ShapeDtypeStruct((t_len, d), F32), jax.ShapeDtypeStruct((1, d), F32)]
                  + [jax.ShapeDtypeStruct(a.shape, a.dtype) for a in chip_scatter]
                  + (_direct_out_shapes(peer_scatter, [False, True]) if n_peer else []),
        scratch_shapes=(_chip_semaphores(nx) if nx else []) + peer_scratch,
        compiler_params=_params("arbitrary"),
    )(dproj8, dsmall, w_main, w_small, x, r, dx2, norm_w, *chip_scatter, *peer_scatter)


def _direct_out_shapes(srcs, per_peer):
    return [jax.ShapeDtypeStruct(s.shape if pp else (N_DEV,) + s.shape, s.dtype) for s, pp in zip(srcs, per_peer)]


def _direct_semaphores(n):
    return [pltpu.SemaphoreType.DMA((n * (N_DEV - 1),)), pltpu.SemaphoreType.DMA((n * (N_DEV - 1),)),
            pltpu.SemaphoreType.DMA((n,))]


def _direct_copies(src_refs, out_refs, send_sems, recv_sems, local_sems, per_peer):
    x, y, c = lax.axis_index("x"), lax.axis_index("y"), lax.axis_index("c")
    me = 4 * x + 2 * y + c
    local, remote = [], []
    for a in range(len(src_refs)):
        mine = src_refs[a].at[me] if per_peer[a] else src_refs[a]
        local.append(pltpu.make_async_copy(mine, out_refs[a].at[me], local_sems.at[a]))
    for k in range(1, N_DEV):
        kx, ky, kc = (k >> 2) & 1, (k >> 1) & 1, k & 1
        px = 1 - x if kx else x
        py = 1 - y if ky else y
        pc = 1 - c if kc else c
        peer = 4 * px + 2 * py + pc
        for a in range(len(src_refs)):
            sem = a * (N_DEV - 1) + (k - 1)
            remote.append(pltpu.make_async_remote_copy(
                src_ref=src_refs[a].at[peer] if per_peer[a] else src_refs[a], dst_ref=out_refs[a].at[me],
                send_sem=send_sems.at[sem], recv_sem=recv_sems.at[sem],
                device_id=(px, py, pc), device_id_type=pl.DeviceIdType.MESH))
    return local, remote


def _start_all(copies):
    local, remote = copies
    for cp in local + remote:
        cp.start()


def _wait_all(copies):
    local, remote = copies
    for cp in remote:
        cp.wait_send()
    for cp in remote:
        cp.wait_recv()
    for cp in local:
        cp.wait()


N_CHIPS = 4
_HBM = pl.BlockSpec(memory_space=pl.ANY)
_MESH = pl.DeviceIdType.MESH


def _gather_call(name, srcs):
    n = len(srcs)
    per = N_DEV - 1

    def body(*refs):
        src_refs, out_refs = refs[:n], refs[n:2 * n]
        send_sems, recv_sems, local_sems = refs[2 * n:]
        x, y, c = lax.axis_index("x"), lax.axis_index("y"), lax.axis_index("c")
        me, sibling = (x, y, c), (x, y, 1 - c)
        x_nbr, y_nbr, diagonal = (1 - x, y), (x, 1 - y), (1 - x, 1 - y)
        held = ((1 - x) * c + x * (1 - c), y * c + (1 - y) * (1 - c))
        onward = (x * c + (1 - x) * (1 - c), (1 - y) * c + y * (1 - c))
        slot = lambda px, py, pc: 4 * px + 2 * py + pc

        def copy(a, k, block, to, from_src=False):
            rows = out_refs[a].at[slot(*block)]
            return pltpu.make_async_remote_copy(
                src_ref=src_refs[a] if from_src else rows, dst_ref=rows,
                send_sem=send_sems.at[a * per + k], recv_sem=recv_sems.at[a * per + k],
                device_id=to, device_id_type=_MESH)

        local = [pltpu.make_async_copy(src_refs[a], out_refs[a].at[slot(*me)], local_sems.at[a]) for a in range(n)]
        started = []

        def start(cp):
            cp.start()
            started.append(cp)

        for cp in local:
            cp.start()
        for a in range(n):
            start(copy(a, 0, me, sibling, True))
            start(copy(a, 1, me, (*x_nbr, c), True))
            start(copy(a, 2, me, (*y_nbr, c), True))
        for a in range(n):
            copy(a, 1, (*x_nbr, c), me).wait_recv()
            copy(a, 2, (*y_nbr, c), me).wait_recv()
            start(copy(a, 3, (*held, c), (*onward, c)))
            start(copy(a, 4, (*x_nbr, c), sibling))
            start(copy(a, 5, (*y_nbr, c), sibling))
        for a in range(n):
            copy(a, 3, (*diagonal, c), me).wait_recv()
            start(copy(a, 6, (*diagonal, c), sibling))
        for a in range(n):
            copy(a, 0, sibling, me).wait_recv()
            for k, chip in ((4, x_nbr), (5, y_nbr), (6, diagonal)):
                copy(a, k, (*chip, 1 - c), me).wait_recv()
        for cp in started:
            cp.wait_send()
        for cp in local:
            cp.wait()

    return pl.pallas_call(
        body, name=name,
        in_specs=[_HBM] * n, out_specs=[_HBM] * n,
        out_shape=[jax.ShapeDtypeStruct((N_DEV,) + s.shape, s.dtype) for s in srcs],
        scratch_shapes=[pltpu.SemaphoreType.DMA((n * per,)), pltpu.SemaphoreType.DMA((n * per,)),
                        pltpu.SemaphoreType.DMA((n,))],
    )(*srcs)


def _chip_semaphores(n):
    per = N_CHIPS - 1
    return [pltpu.SemaphoreType.DMA((n * per,)), pltpu.SemaphoreType.DMA((n * per,)), pltpu.SemaphoreType.DMA((n,))]


def _chip_copies(src_refs, out_refs, send_sems, recv_sems, local_sems):
    per = N_CHIPS - 1
    x, y, c = lax.axis_index("x"), lax.axis_index("y"), lax.axis_index("c")
    mine = 2 * x + y
    chips = [(1 - x, y), (x, 1 - y), (1 - x, 1 - y)]
    n = len(src_refs)
    local = [pltpu.make_async_copy(src_refs[a].at[mine], out_refs[a].at[mine], local_sems.at[a]) for a in range(n)]
    remote = []
    for a in range(n):
        for j, (px, py) in enumerate(chips):
            remote.append(pltpu.make_async_remote_copy(
                src_ref=src_refs[a].at[2 * px + py], dst_ref=out_refs[a].at[mine],
                send_sem=send_sems.at[a * per + j], recv_sem=recv_sems.at[a * per + j],
                device_id=(px, py, c), device_id_type=_MESH))
    return local, remote


def _adam_call(name, parts, w, m, v, tr):
    rows, cols = w.shape
    n_slots = parts.shape[0]

    def body(p_ref, w_ref, m_ref, v_ref, g_ref, d_ref, nm_ref, nv_ref):
        g = p_ref[0].astype(F32)
        for s in range(1, n_slots):
            g = g + p_ref[s].astype(F32)
        m_new = ADAM_B1 * m_ref[...] + (1.0 - ADAM_B1) * g
        v_new = ADAM_B2 * v_ref[...] + (1.0 - ADAM_B2) * (g * g)
        m_hat = m_new / (1.0 - ADAM_B1 ** ADAM_STEP)
        v_hat = v_new / (1.0 - ADAM_B2 ** ADAM_STEP)
        g_ref[...] = g
        d_ref[...] = -ADAM_LR * (m_hat / (jnp.sqrt(v_hat) + ADAM_EPS) + ADAM_WD * w_ref[...])
        nm_ref[...] = m_new
        nv_ref[...] = v_new

    blk = pl.BlockSpec((tr, cols), lambda i: (i, 0))
    return pl.pallas_call(
        body, name=name,
        grid=(rows // tr,),
        in_specs=[pl.BlockSpec((n_slots, tr, cols), lambda i: (0, i, 0)), blk, blk, blk],
        out_specs=[blk] * 4,
        out_shape=[jax.ShapeDtypeStruct((rows, cols), F32)] * 4,
        compiler_params=_params("arbitrary"),
    )(parts, w, m, v)


def _columns_to_rows_call(name, w_t, rows, dtype):
    row_tiles = rows // LANES
    cols = w_t.shape[0] // row_tiles
    whole = cols // LANES * LANES

    def body(w_ref, out_ref):
        diagonal = (lax.broadcasted_iota(jnp.int32, (LANES, LANES), 0)
                    == lax.broadcasted_iota(jnp.int32, (LANES, LANES), 1))
        for a in range(row_tiles):
            out_ref[a * LANES:(a + 1) * LANES, :whole] = (
                w_ref[pl.ds(a, whole, stride=row_tiles), :].T.astype(dtype))
            for c in range(whole, cols):
                column = w_ref[pl.ds(c * row_tiles + a, 1), :]
                upright = jnp.sum(jnp.where(diagonal, column, 0.0), axis=1, keepdims=True)
                out_ref[a * LANES:(a + 1) * LANES, c:c + 1] = upright.astype(dtype)

    vm = pl.BlockSpec(memory_space=pltpu.VMEM)
    return pl.pallas_call(
        body, name=name,
        in_specs=[vm], out_specs=vm,
        out_shape=jax.ShapeDtypeStruct((rows, cols), dtype),
        compiler_params=pltpu.CompilerParams(vmem_limit_bytes=VMEM_LIMIT_BYTES),
    )(w_t)


def _adam_columns_call(name, parts, w_t, m_t, v_t):
    n_slots, rows, cols = parts.shape
    row_tiles = rows // LANES
    cols_pad = -(-cols // LANES) * LANES

    def body(p_ref, w_ref, m_ref, v_ref, *out_refs):
        for a in range(row_tiles):
            g = p_ref[0, a * LANES:(a + 1) * LANES, :].astype(F32)
            for s in range(1, n_slots):
                g = g + p_ref[s, a * LANES:(a + 1) * LANES, :].astype(F32)
            g = jnp.concatenate([g, jnp.zeros((LANES, cols_pad - cols), F32)], axis=1).T[:cols]
            column_rows = pl.ds(a, cols, stride=row_tiles)
            results = (g,) + _adamw(g, w_ref[column_rows, :], m_ref[column_rows, :], v_ref[column_rows, :])
            for out_ref, val in zip(out_refs, results):
                out_ref[column_rows, :] = val

    vm = pl.BlockSpec(memory_space=pltpu.VMEM)
    return pl.pallas_call(
        body, name=name,
        in_specs=[vm] * 4, out_specs=[vm] * 4,
        out_shape=[jax.ShapeDtypeStruct(w_t.shape, F32)] * 4,
        compiler_params=pltpu.CompilerParams(vmem_limit_bytes=VMEM_LIMIT_BYTES),
    )(parts, w_t, m_t, v_t)


N_PIECES = 8
PIECE = 512
SHARD_COLS = 513
SHARD_PAD = 640
RELAYOUT_ROWS = 256


def _from_shards_call(shards):
    _, d, _ = shards.shape
    tr = RELAYOUT_ROWS

    def body(p_ref, m_ref, s_ref):
        lane = lax.broadcasted_iota(jnp.int32, (tr, SHARD_PAD), 1)
        pad = jnp.zeros((tr, SHARD_PAD - SHARD_COLS), p_ref.dtype)
        sh = [jnp.concatenate([p_ref[s], pad], axis=1) for s in range(N_DEV)]
        for p in range(N_PIECES):
            y = sh[p] if p == 0 else pltpu.roll(sh[p], p, axis=1)
            if p > 0:
                y = jnp.where(lane < p, pltpu.roll(sh[p - 1], SHARD_PAD - (SHARD_COLS - p), axis=1), y)
            m_ref[:, p * PIECE:(p + 1) * PIECE] = y[:, :PIECE].astype(m_ref.dtype)
        first_gate = N_PIECES * PIECE - (N_DEV - 1) * SHARD_COLS
        s_ref[...] = pltpu.roll(sh[N_DEV - 1], SHARD_PAD - first_gate, axis=1)[:, :LANES].astype(s_ref.dtype)

    return pl.pallas_call(
        body, name="w_in_from_shards",
        grid=(d // tr,),
        in_specs=[pl.BlockSpec((N_DEV, tr, SHARD_COLS), lambda i: (0, i, 0))],
        out_specs=[pl.BlockSpec((tr, N_PIECES * PIECE), lambda i: (i, 0)), pl.BlockSpec((tr, LANES), lambda i: (i, 0))],
        out_shape=[jax.ShapeDtypeStruct((d, N_PIECES * PIECE), shards.dtype),
                   jax.ShapeDtypeStruct((d, LANES), shards.dtype)],
        compiler_params=_params("arbitrary"),
    )(shards)


def _adamw(g, w, m, v):
    m_new = ADAM_B1 * m + (1.0 - ADAM_B1) * g
    v_new = ADAM_B2 * v + (1.0 - ADAM_B2) * (g * g)
    m_hat = m_new / (1.0 - ADAM_B1 ** ADAM_STEP)
    v_hat = v_new / (1.0 - ADAM_B2 ** ADAM_STEP)
    return -ADAM_LR * (m_hat / (jnp.sqrt(v_hat) + ADAM_EPS) + ADAM_WD * w), m_new, v_new


def _adam_small_call(parts, ws, ms, vs):
    n = len(ws)
    n_slots = parts.shape[0]

    def body(*refs):
        p_ref = refs[0]
        w_refs, m_refs, v_refs = refs[1:1 + n], refs[1 + n:1 + 2 * n], refs[1 + 2 * n:1 + 3 * n]
        loss_ref = refs[1 + 3 * n]
        outs = refs[2 + 3 * n:]
        g_all = p_ref[0]
        for s in range(1, n_slots):
            g_all = g_all + p_ref[s]
        loss_ref[...] = g_all[n:n + 1, 0:1]
        for r in range(n):
            size = w_refs[r].shape[1]
            g = g_all[r:r + 1, :size]
            delta, m_new, v_new = _adamw(g, w_refs[r][...], m_refs[r][...], v_refs[r][...])
            for kind, val in enumerate((g, delta, m_new, v_new)):
                outs[kind * n + r][...] = val

    vm = pl.BlockSpec(memory_space=pltpu.VMEM)
    shapes = [jax.ShapeDtypeStruct(w.shape, F32) for w in ws]
    return pl.pallas_call(
        body, name="adam_small",
        in_specs=[vm] * (1 + 3 * n), out_specs=[vm] * (1 + 4 * n),
        out_shape=[jax.ShapeDtypeStruct((1, 1), F32)] + shapes * 4,
    )(parts, *ws, *ms, *vs)


_SMALL_ROWS = ("norm1_w", "final_norm_w", "sb_norm_w", "gdn_norm_w", "gdn_A_log", "gdn_dt_bias", "loss")


def _pack_small(vals, width):
    rows = [jnp.pad(a.reshape(1, -1).astype(F32), ((0, 0), (0, width - a.size))) for a in vals]
    rows += [jnp.zeros((1, width), F32)] * (8 - len(rows))
    return jnp.concatenate(rows, axis=0)


def _device_step(x2d, tgt, w_main, w_small, w_out_full, conv_full, norm1_w, sb_norm_w, gdn_A_log, gdn_dt_bias,
                 gdn_norm_w, final_norm_w, distributed=False):
    t_len, d = x2d.shape
    n_chunks = t_len // CHUNK
    w_main, w_small, w_out_full = (a.astype(MXU_DTYPE) for a in (w_main, w_small, w_out_full))
    w_small_t = w_small[:, :2 * GDN_HEADS].T

    pad_lanes = lambda a, lo: jnp.pad(a.reshape(1, -1), ((0, 0), (lo, LANES - lo - a.size)))
    alog_l, dtb_l = pad_lanes(gdn_A_log, GDN_HEADS), pad_lanes(gdn_dt_bias, GDN_HEADS)
    alog_c, dtb_c = alog_l[:, :8].T, dtb_l[:, :8].T
    sbw = jnp.tile(sb_norm_w, (1, 512 // SB_HEAD_DIM))
    gdw = jnp.tile(gdn_norm_w, (1, 512 // GDN_HEAD_DIM))
    fw = final_norm_w.reshape(1, d)

    if distributed:
        proj_cols, proj_gates, ps, pst, h_t, r1, w_out_g, conv_g = _inproj_call(
            x2d, norm1_w, w_main, w_small, w_small_t, gather=(w_out_full, conv_full))
        w_out_full = w_out_g.reshape(d, d)
        conv_full = conv_g.transpose(1, 0, 2).reshape(CONV_WIDTH, N_DEV * conv_g.shape[2])
    else:
        proj_cols, proj_gates, ps, pst, h_t, r1 = _inproj_call(x2d, norm1_w, w_main, w_small, w_small_t)
    o_sb, sp_total, sb_blocks_run = _sb_fwd_call(proj_cols, t_len)
    gact = _gdn_prep_call(proj_cols, conv_full, t_len, after=sp_total)
    beta_l, gcol_l, grow = _gdn_gates_call(ps, pst, alog_l, dtb_l, alog_c, dtb_c, t_len)
    gam_r = grow[GDN_HEADS:2 * GDN_HEADS].reshape(GDN_HEADS, n_chunks, 1, CHUNK)
    o_gd, *gdn_saved = _gdn_fwd_call(gact, beta_l, gcol_l, gam_r, t_len)

    (dx2, d_osb, d_ogd, dproj8, loss_p, g_fw, g_sbw, g_gdw, g_wout) = _post_call(
        o_sb, o_gd, proj_gates, x2d, tgt, w_out_full, sbw, gdw, fw)

    dproj8 = _sb_bwd_call(proj_cols, sp_total, sb_blocks_run, d_osb, dproj8, t_len)
    if distributed:
        d_gact3, d_gates, g_wout = _gdn_bwd_call(gact, beta_l, gcol_l, gam_r, gdn_saved, d_ogd, t_len,
                                                 scatter=(g_wout.reshape(N_DEV, d // N_DEV, d),))
    else:
        d_gact3, d_gates = _gdn_bwd_call(gact, beta_l, gcol_l, gam_r, gdn_saved, d_ogd, t_len)
    dproj8, g_conv = _gdn_prep_bwd_call(proj_cols, conv_full, d_gact3, dproj8, t_len)
    dsmall, g_alog, g_dtb = _gdn_gates_bwd_call(ps, alog_l, dtb_l, d_gates, t_len)

    if distributed:
        chip_partials = _gw_in_shards_call(h_t, dproj8, dsmall, WIRE_DTYPE)
        fold = lambda a, group: a.reshape(-1, group).sum(axis=0)
        small_g = _pack_small([jnp.zeros((d,), F32), g_fw, fold(g_sbw, SB_HEAD_DIM), fold(g_gdw, GDN_HEAD_DIM),
                               g_alog[0, GDN_HEADS:2 * GDN_HEADS], g_dtb[0, GDN_HEADS:2 * GDN_HEADS],
                               loss_p[0, :1]], d)
        conv_cols = g_conv.shape[1] // N_DEV
        g_conv_parts = g_conv.reshape(CONV_WIDTH, N_DEV, conv_cols).transpose(1, 0, 2)
        grad_x, _, g_w_in, p_small, p_conv = _dx_call(dproj8, dsmall, w_main, w_small, x2d, r1, dx2, norm1_w,
                                                      chip_scatter=(chip_partials,),
                                                      peer_scatter=(small_g, g_conv_parts))
        return grad_x, g_w_in, g_wout, p_small, p_conv
    else:
        grad_x, g_n1 = _dx_call(dproj8, dsmall, w_main, w_small, x2d, r1, dx2, norm1_w)
        g_w_in = (_gw_in_call(h_t, dproj8), _gw_small_call(h_t, dsmall))
    return (loss_p, grad_x, g_n1, g_w_in, g_sbw, g_conv, g_alog, g_dtb, g_gdw, g_wout, g_fw)


def kernel(x, norm1_w, w_in, sb_norm_w, gdn_conv_w, gdn_A_log, gdn_dt_bias, gdn_norm_w, w_out, final_norm_w, loss_target, m_norm1_w, m_w_in, m_sb_norm_w, m_gdn_conv_w, m_gdn_A_log, m_gdn_dt_bias, m_gdn_norm_w, m_w_out, m_final_norm_w, v_norm1_w, v_w_in, v_sb_norm_w, v_gdn_conv_w, v_gdn_A_log, v_gdn_dt_bias, v_gdn_norm_w, v_w_out, v_final_norm_w):
    d = x.shape[2]
    shard_cols = w_in.shape[2]

    columns = lambda a: a.transpose(2, 0, 1).reshape(shard_cols * d // LANES, LANES)
    from_columns = lambda a: a.reshape(shard_cols, d // LANES, LANES).transpose(1, 2, 0).reshape(1, d, shard_cols)
    (w_in_g,) = _gather_call("gather_weights", [_columns_to_rows_call("w_in_to_wire", columns(w_in), d, WIRE_DTYPE)])
    w_main, w_small = _from_shards_call(w_in_g)

    grad_x, p_w_in, p_wout, p_small, p_conv = _device_step(
        x[0], loss_target[0], w_main, w_small, w_out[0].astype(WIRE_DTYPE), gdn_conv_w[0], norm1_w, sb_norm_w,
        gdn_A_log, gdn_dt_bias, gdn_norm_w, final_norm_w, distributed=True)

    r_w_in = [from_columns(a) for a in _adam_columns_call("adam_w_in", p_w_in, columns(w_in), columns(m_w_in),
                                                          columns(v_w_in))]
    r_wout = _adam_call("adam_w_out", p_wout, w_out[0], m_w_out[0], v_w_out[0], d // N_DEV)
    r_conv = _adam_call("adam_conv", p_conv, gdn_conv_w[0], m_gdn_conv_w[0], v_gdn_conv_w[0], CONV_WIDTH)

    row = lambda a: a.reshape(1, -1)
    n_small = len(_SMALL_ROWS) - 1
    r_small = _adam_small_call(
        p_small,
        [norm1_w, row(final_norm_w), sb_norm_w, gdn_norm_w, gdn_A_log, gdn_dt_bias],
        [m_norm1_w, row(m_final_norm_w), m_sb_norm_w, m_gdn_norm_w, m_gdn_A_log, m_gdn_dt_bias],
        [v_norm1_w, row(v_final_norm_w), v_sb_norm_w, v_gdn_norm_w, v_gdn_A_log, v_gdn_dt_bias])

    def small_out(kind, name):
        out = r_small[1 + kind * n_small + _SMALL_ROWS.index(name)]
        return out.reshape(final_norm_w.shape) if name == "final_norm_w" else out

    def outputs(kind):
        return (small_out(kind, "norm1_w"), r_w_in[kind], small_out(kind, "sb_norm_w"), r_conv[kind][None],
                small_out(kind, "gdn_A_log"), small_out(kind, "gdn_dt_bias"), small_out(kind, "gdn_norm_w"),
                r_wout[kind][None], small_out(kind, "final_norm_w"))

    return (r_small[0][0, 0], grad_x[None], *outputs(0), *outputs(1), *outputs(2), *outputs(3))
```

```python
import functools

import jax
import jax.numpy as jnp
from jax import lax
from jax.experimental import pallas as pl
from jax.experimental.pallas import tpu as pltpu

F32 = jnp.float32
MXU_DTYPE = jnp.bfloat16
WIRE_DTYPE = jnp.bfloat16
EXACT = lax.Precision.HIGHEST
EPS = 1e-6
N_DEV = 8
SB_HEAD_DIM = 64
GDN_HEAD_DIM = 128
GDN_HEADS = 4
GDN_CHUNKS_PER_STEP = 4
GDN_BWD_GROUP = 1
CHUNK = 64
CONV_WIDTH = 4
LANES = 128
SB_BLOCK = 128
SB_BQ = 256
VMEM_LIMIT_BYTES = 56 * 1024 * 1024

PIECE_COLS = 512
PROJ_PIECE_KINDS = ("heads", "heads", "heads", "gate", "heads", "heads", "heads", "gate")
SB_FIRST_BLOCK, GDN_FIRST_BLOCK = 0, 12

DPROJ_PIECE_OF_SLOT = (0, 1, 2, 4, 5, 6, 3, 7)
DPROJ_SB_SLOT, DPROJ_GDN_SLOT, DPROJ_GATE_SLOT = 0, 3, 6

ADAM_LR = 0.001
ADAM_B1 = 0.9
ADAM_B2 = 0.999
ADAM_EPS = 1e-08
ADAM_WD = 0.01
ADAM_STEP = 10

_NN = (((1,), (0,)), ((), ()))
_NT = (((1,), (1,)), ((), ()))
_TN = (((0,), (0,)), ((), ()))
_BNN = (((2,), (1,)), ((0,), (0,)))
_BNT = (((2,), (2,)), ((0,), (0,)))
_BTN = (((1,), (1,)), ((0,), (0,)))


def _mx(a, b):
    return jnp.dot(a, b, precision=EXACT, preferred_element_type=F32)


def _split(x):
    hi = x.astype(MXU_DTYPE)
    return hi, (x - hi.astype(F32)).astype(MXU_DTYPE)


def _m3_general(a, b, dims):
    ah, al = _split(a)
    bh, bl = _split(b)
    dot = lambda x, y: lax.dot_general(x, y, dims, preferred_element_type=F32)
    (contract, _), (batch, _) = dims
    free = [ax for ax in range(a.ndim) if ax not in contract and ax not in batch][0]
    m = a.shape[free]
    both = dot(jnp.concatenate([ah, al], axis=free), bh)
    out_axis = len(batch)
    hi_part = lax.slice_in_dim(both, 0, m, axis=out_axis)
    lo_part = lax.slice_in_dim(both, m, 2 * m, axis=out_axis)
    return hi_part + lo_part


def _times_exact(a, b_exact, dims):
    ah, al = _split(a)
    (contract, _), (batch, _) = dims
    free = [ax for ax in range(a.ndim) if ax not in contract and ax not in batch][0]
    m = a.shape[free]
    both = lax.dot_general(jnp.concatenate([ah, al], axis=free), b_exact.astype(MXU_DTYPE), dims,
                           preferred_element_type=F32)
    out_axis = len(batch)
    return lax.slice_in_dim(both, 0, m, axis=out_axis) + lax.slice_in_dim(both, m, 2 * m, axis=out_axis)


def _exact_times(a_exact, b, dims):
    bh, bl = _split(b)
    n = b.shape[-1]
    both = lax.dot_general(a_exact.astype(MXU_DTYPE), jnp.concatenate([bh, bl], axis=-1), dims,
                           preferred_element_type=F32)
    return both[..., :n] + both[..., n:]


def _sigmoid(z):
    return 1.0 / (1.0 + jnp.exp(-z))


def _softplus(z):
    return jnp.maximum(z, 0.0) + jnp.log(1.0 + jnp.exp(-jnp.abs(z)))


def _params(*semantics):
    return pltpu.CompilerParams(dimension_semantics=semantics, vmem_limit_bytes=VMEM_LIMIT_BYTES)


def _inproj_call(x, norm_w, w_main, w_small, w_small_t, gather=(), tm=256):
    t_len, d = x.shape
    n = w_main.shape[1]
    ns = w_small.shape[1]
    nst = w_small_t.shape[0]
    ng = len(gather)
    steps = t_len // tm

    blocks_per_piece = PIECE_COLS // LANES
    n_gate_cols = PIECE_COLS * PROJ_PIECE_KINDS.count("gate")
    n_col_blocks = blocks_per_piece * PROJ_PIECE_KINDS.count("heads")

    def body(*refs):
        x_ref, nw_ref, wm_ref, ws_ref, wst_ref = refs[:5]
        cols_ref, pz_ref, ps_ref, pst_ref, ht_ref, r_ref = refs[5 + ng:11 + ng]
        copies = lambda: _direct_copies(refs[5:5 + ng], refs[11 + ng:11 + 2 * ng], *refs[11 + 2 * ng:], (False,) * ng)
        if ng:
            pl.when(pl.program_id(0) == 0)(lambda: _start_all(copies()))
        xv = x_ref[...]
        r = lax.rsqrt(jnp.mean(xv * xv, axis=-1, keepdims=True) + EPS)
        h = xv * r * nw_ref[...]
        hb = h.astype(MXU_DTYPE)
        n_block = n_gate = 0
        for piece, kind in enumerate(PROJ_PIECE_KINDS):
            out = jnp.dot(hb, wm_ref[:, piece * PIECE_COLS:(piece + 1) * PIECE_COLS], preferred_element_type=F32)
            if kind == "gate":
                pz_ref[:, n_gate * PIECE_COLS:(n_gate + 1) * PIECE_COLS] = out
                n_gate += 1
            else:
                for j in range(blocks_per_piece):
                    cols_ref[n_block + j] = out[:, j * LANES:(j + 1) * LANES]
                n_block += blocks_per_piece
        ps_ref[...] = jnp.dot(hb, ws_ref[...], preferred_element_type=F32)
        pst_ref[...] = lax.dot_general(wst_ref[...], hb, _NT, preferred_element_type=F32)
        ht_ref[...] = h.T.astype(MXU_DTYPE)
        r_ref[...] = r
        if ng:
            pl.when(pl.program_id(0) == steps - 1)(lambda: _wait_all(copies()))

    return pl.pallas_call(
        body, name="inproj",
        grid=(steps,),
        in_specs=[pl.BlockSpec((tm, d), lambda i: (i, 0)),
                  pl.BlockSpec((1, d), lambda i: (0, 0)),
                  pl.BlockSpec((d, n), lambda i: (0, 0)),
                  pl.BlockSpec((d, ns), lambda i: (0, 0)),
                  pl.BlockSpec((nst, d), lambda i: (0, 0))] + [_HBM] * ng,
        out_specs=[pl.BlockSpec((n_col_blocks, tm, LANES), lambda i: (0, i, 0)),
                   pl.BlockSpec((tm, n_gate_cols), lambda i: (i, 0)),
                   pl.BlockSpec((tm, ns), lambda i: (i, 0)),
                   pl.BlockSpec((nst, tm), lambda i: (0, i)),
                   pl.BlockSpec((d, tm), lambda i: (0, i)),
                   pl.BlockSpec((tm, 1), lambda i: (i, 0))] + [_HBM] * ng,
        out_shape=[jax.ShapeDtypeStruct((n_col_blocks, t_len, LANES), F32),
                   jax.ShapeDtypeStruct((t_len, n_gate_cols), F32),
                   jax.ShapeDtypeStruct((t_len, ns), F32),
                   jax.ShapeDtypeStruct((nst, t_len), F32),
                   jax.ShapeDtypeStruct((d, t_len), MXU_DTYPE),
                   jax.ShapeDtypeStruct((t_len, 1), F32)] + _direct_out_shapes(gather, (False,) * ng),
        scratch_shapes=_direct_semaphores(ng) if ng else [],
        compiler_params=_params("arbitrary"),
    )(x, norm_w, w_main, w_small, w_small_t, *gather)


def _running_sum_mm(x, tri):
    hi = x.astype(MXU_DTYPE)
    lo = (x - hi.astype(F32)).astype(MXU_DTYPE)
    return jnp.dot(hi, tri, preferred_element_type=F32) + jnp.dot(lo, tri, preferred_element_type=F32)


def _col_block(t_len, first):
    return pl.BlockSpec((1, t_len, LANES), lambda p: (first + p, 0, 0))


def _sb_iotas():
    row_i = lax.broadcasted_iota(jnp.int32, (SB_BQ, SB_BLOCK), 0)
    col_i = lax.broadcasted_iota(jnp.int32, (SB_BQ, SB_BLOCK), 1)
    sq_r = lax.broadcasted_iota(jnp.int32, (SB_BLOCK, SB_BLOCK), 0)
    sq_c = lax.broadcasted_iota(jnp.int32, (SB_BLOCK, SB_BLOCK), 1)
    return row_i, col_i, sq_r, sq_c


SB_DIAG_BLOCKS = SB_BQ // SB_BLOCK
SB_EXP_FLOOR = -110.0


def _sb_keys_descending(qi, tile, carry, z_bounds, n_heads, has_free):
    group = SB_DIAG_BLOCKS
    n_free = group * qi
    diag = list(range(group - 1, -1, -1))
    carry = tile([n_free + j for j in diag], [True] * group, carry, [j * SB_BLOCK for j in diag])

    def largest_exponent(c):
        worst = jnp.max(z_bounds[0] - c[1])
        for h in range(1, n_heads):
            worst = jnp.maximum(worst, jnp.max(z_bounds[h] - c[1 + h]))
        return worst

    always = group if has_free else 0

    def cond(state):
        return (state[0] < n_free) & ((state[1] > SB_EXP_FLOOR) | (state[0] < always))

    def body(state):
        first = n_free - 1 - state[0]
        c = tile([first - j for j in range(group)], [False] * group, state[2:])
        return (state[0] + group, largest_exponent(c), *c)

    out = lax.while_loop(cond, body, (jnp.int32(0), largest_exponent(carry), *carry))
    return out[2:], out[0]


def _sb_keys_ascending(qi, n_run, tile, carry, has_free):
    group = SB_DIAG_BLOCKS
    n_free = group * qi
    diag = list(range(group))
    kjs, los, masked = [n_free + j for j in diag], [j * SB_BLOCK for j in diag], [True] * group
    if has_free:
        early = lambda s: [n_free - n_run + group * s + j for j in range(group)]
        carry = lax.fori_loop(0, n_run // group - 1, lambda s, c: tile(early(s), [False] * group, c), carry)
        kjs, los, masked = [n_free - group + j for j in range(group)] + kjs, [0] * group + los, [False] * group + masked
    return tile(kjs, masked, carry, los)


def _sb_fwd_call(cols, t_len):
    nq = t_len // SB_BQ
    scale = float(SB_HEAD_DIM) ** -0.5
    n_pairs = 512 // LANES
    per_pair = LANES // SB_HEAD_DIM

    def body(q_blk, k_blk, v_blk, o_blk, st_ref, nrun_ref):
        q_ref, k_ref, v_ref, o_ref = q_blk.at[0], k_blk.at[0], v_blk.at[0], o_blk.at[0]
        lane = lax.broadcasted_iota(jnp.int32, (1, LANES), 1)
        row_i, col_i, sq_r, sq_c = _sb_iotas()
        ge = (sq_r >= sq_c).astype(MXU_DTYPE)
        hms = [((lane // SB_HEAD_DIM) == hh).astype(F32) for hh in range(per_pair)]
        k_sq = k_ref[...] * k_ref[...]
        k_norms = [jnp.sqrt(jnp.max(jnp.sum(k_sq * hm, axis=-1, keepdims=True))) * (1.02 * scale) for hm in hms]

        def q_block(qi, has_free):
            r0 = qi * SB_BQ if isinstance(qi, int) else pl.multiple_of(qi * SB_BQ, SB_BQ)
            rows = pl.ds(r0, SB_BQ)
            q_all = q_ref[rows, :]
            qms = [(q_all * (hm * scale)).astype(MXU_DTYPE) for hm in hms]
            z_bounds = [jnp.sqrt(jnp.sum(q_all * q_all * hm, axis=-1, keepdims=True)) * kn
                        for hm, kn in zip(hms, k_norms)]

            def tile(kjs, masked, kc, los=None):
                heads = range(per_pair)
                los = los or [0] * len(kjs)
                pairs = [(t, h) for t in range(len(kjs)) for h in heads]
                add_rows = lambda full, lo, part: full + part if lo == 0 else jnp.concatenate(
                    [full[:lo], full[lo:] + part], axis=0)
                acc, cs = kc[0], list(kc[1:])
                s0s = [kj * SB_BLOCK if isinstance(kj, int) else pl.multiple_of(kj * SB_BLOCK, SB_BLOCK) for kj in kjs]
                kbs = [k_ref[pl.ds(s0, SB_BLOCK), :].astype(MXU_DTYPE) for s0 in s0s]
                v_alls = [v_ref[pl.ds(s0, SB_BLOCK), :] for s0 in s0s]
                vms = {(t, h): (v_alls[t] * hms[h]).astype(MXU_DTYPE) for t, h in pairs}
                zs = {(t, h): lax.dot_general(qms[h][los[t]:], kbs[t], _NT, preferred_element_type=F32)
                      for t, h in pairs}
                masks = [(col_i[lo:] + s0) < (row_i[lo:] + r0) if m else None for m, lo, s0 in zip(masked, los, s0s)]
                keep = lambda t, a: a if masks[t] is None else jnp.where(masks[t], a, 0.0)
                sps = {(t, h): keep(t, _softplus(zs[t, h])) for t, h in pairs}
                sums = {p: _running_sum_mm(sps[p], ge) for p in pairs}
                mass = {}
                for t, h in pairs:
                    mass[t, h] = cs[h] if t == 0 else add_rows(
                        mass[t - 1, h], los[t - 1], jnp.sum(sps[t - 1, h], axis=-1, keepdims=True))
                ws = {(t, h): keep(t, jnp.exp(zs[t, h] - (sums[t, h] + mass[t, h][los[t]:]))) for t, h in pairs}
                for t, h in pairs:
                    acc = add_rows(acc, los[t], jnp.dot(ws[t, h].astype(MXU_DTYPE), vms[t, h],
                                                        preferred_element_type=F32))
                last = len(kjs) - 1
                cs = [add_rows(mass[last, h], los[last], jnp.sum(sps[last, h], axis=-1, keepdims=True)) for h in heads]
                return (acc, *cs)

            zero_col = jnp.zeros((SB_BQ, 1), F32)
            out, n_run = _sb_keys_descending(
                qi, tile, (jnp.zeros((SB_BQ, LANES), F32),) + (zero_col,) * per_pair, z_bounds, per_pair, has_free)
            o_ref[rows, :] = out[0]
            masses = jnp.zeros((SB_BQ, LANES), F32)
            for hh in range(per_pair):
                masses = jnp.where(lane == hh, out[1 + hh], masses)
            st_ref[rows, :] = masses
            nrun_ref[pl.program_id(0), qi] = n_run

        q_block(0, False)
        lax.fori_loop(1, nq, lambda qi, carry: (q_block(qi, True), carry)[1], 0)

    return pl.pallas_call(
        body, name="sb_fwd",
        grid=(n_pairs,),
        in_specs=[_col_block(t_len, SB_FIRST_BLOCK), _col_block(t_len, SB_FIRST_BLOCK + n_pairs),
                  _col_block(t_len, SB_FIRST_BLOCK + 2 * n_pairs)],
        out_specs=[_col_block(t_len, 0),
                   pl.BlockSpec((t_len, LANES), lambda p: (0, p)),
                   pl.BlockSpec(memory_space=pltpu.SMEM)],
        out_shape=[jax.ShapeDtypeStruct((n_pairs, t_len, LANES), F32),
                   jax.ShapeDtypeStruct((t_len, n_pairs * LANES), F32),
                   jax.ShapeDtypeStruct((n_pairs, nq), jnp.int32)],
        compiler_params=_params("arbitrary"),
    )(cols, cols, cols)


def _sb_bwd_call(cols, sp_total, n_run_all, d_o, dproj, t_len):
    nq = t_len // SB_BQ
    scale = float(SB_HEAD_DIM) ** -0.5
    n_pairs = 512 // LANES
    per_pair = LANES // SB_HEAD_DIM

    def body(q_blk, k_blk, v_blk, st_ref, nrun_ref, do_blk, dproj_in_ref, d_ref):
        q_ref, k_ref, v_ref, do_ref = q_blk.at[0], k_blk.at[0], v_blk.at[0], do_blk.at[0]
        lane = lax.broadcasted_iota(jnp.int32, (1, LANES), 1)
        row_i, col_i, sq_r, sq_c = _sb_iotas()
        lt = (sq_r < sq_c).astype(MXU_DTYPE)
        le = (sq_r <= sq_c).astype(MXU_DTYPE)
        hms = [((lane // SB_HEAD_DIM) == hh).astype(F32) for hh in range(per_pair)]
        d_ref[1] = jnp.zeros((t_len, LANES), F32)
        d_ref[2] = jnp.zeros((t_len, LANES), F32)

        def q_block(qi, has_free):
            r0 = qi * SB_BQ if isinstance(qi, int) else pl.multiple_of(qi * SB_BQ, SB_BQ)
            rows = pl.ds(r0, SB_BQ)
            q_all, do_all = q_ref[rows, :], do_ref[rows, :]
            qms = [(q_all * (hm * scale)).astype(MXU_DTYPE) for hm in hms]
            doms = [(do_all * hm).astype(MXU_DTYPE) for hm in hms]
            masses = st_ref[rows, :]
            totals = [jnp.sum(jnp.where(lane == hh, masses, 0.0), axis=-1, keepdims=True) for hh in range(per_pair)]

            def tile(kjs, masked, kc, los=None):
                heads = range(per_pair)
                los = los or [0] * len(kjs)
                pairs = [(t, h) for t in range(len(kjs)) for h in heads]
                add_rows = lambda full, lo, part: full + part if lo == 0 else jnp.concatenate(
                    [full[:lo], full[lo:] + part], axis=0)
                rsum = lambda a: jnp.sum(a, axis=-1, keepdims=True)
                dq, cls, gls = kc[0], list(kc[1:1 + per_pair]), list(kc[1 + per_pair:])
                s0s = [kj * SB_BLOCK if isinstance(kj, int) else pl.multiple_of(kj * SB_BLOCK, SB_BLOCK) for kj in kjs]
                k_alls = [k_ref[pl.ds(s0, SB_BLOCK), :] for s0 in s0s]
                v_alls = [v_ref[pl.ds(s0, SB_BLOCK), :] for s0 in s0s]
                kbs = [k_all.astype(MXU_DTYPE) for k_all in k_alls]
                vms = {(t, h): (v_alls[t] * hms[h]).astype(MXU_DTYPE) for t, h in pairs}
                kms = {(t, h): (k_alls[t] * (hms[h] * scale)).astype(MXU_DTYPE) for t, h in pairs}
                q_live = {(t, h): qms[h][los[t]:] for t, h in pairs}
                do_live = {(t, h): doms[h][los[t]:] for t, h in pairs}
                zs = {p: lax.dot_general(q_live[p], kbs[p[0]], _NT, preferred_element_type=F32) for p in pairs}
                das = {p: lax.dot_general(do_live[p], vms[p], _NT, preferred_element_type=F32) for p in pairs}
                masks = [(col_i[lo:] + s0) < (row_i[lo:] + r0) if m else None for m, lo, s0 in zip(masked, los, s0s)]
                keep = lambda t, a: a if masks[t] is None else jnp.where(masks[t], a, 0.0)
                sp_alls = {p: _softplus(zs[p]) for p in pairs}
                sps = {(t, h): keep(t, sp_alls[t, h]) for t, h in pairs}
                lefts = {p: _running_sum_mm(sps[p], lt) for p in pairs}
                cl = {}
                for t, h in pairs:
                    cl[t, h] = cls[h] if t == 0 else add_rows(cl[t - 1, h], los[t - 1], rsum(sps[t - 1, h]))
                ws = {(t, h): keep(t, jnp.exp(zs[t, h] - ((totals[h] - cl[t, h])[los[t]:] - lefts[t, h])))
                      for t, h in pairs}
                gs = {p: das[p] * ws[p] for p in pairs}
                g_sums = {p: _running_sum_mm(gs[p], le) for p in pairs}
                gl = {}
                for t, h in pairs:
                    gl[t, h] = gls[h] if t == 0 else add_rows(gl[t - 1, h], los[t - 1], rsum(gs[t - 1, h]))
                dzs = {(t, h): keep(t, gs[t, h] - jnp.exp(zs[t, h] - sp_alls[t, h]) * (gl[t, h][los[t]:] + g_sums[t, h])
                               ).astype(MXU_DTYPE) for t, h in pairs}
                for t in range(len(kjs)):
                    dk_t = jnp.zeros((SB_BLOCK, LANES), F32)
                    dv_t = jnp.zeros((SB_BLOCK, LANES), F32)
                    for h in heads:
                        dq = add_rows(dq, los[t], jnp.dot(dzs[t, h], kms[t, h], preferred_element_type=F32))
                        dk_t = dk_t + lax.dot_general(dzs[t, h], q_live[t, h], _TN, preferred_element_type=F32)
                        dv_t = dv_t + lax.dot_general(ws[t, h].astype(MXU_DTYPE), do_live[t, h], _TN,
                                                      preferred_element_type=F32)
                    d_ref[1, pl.ds(s0s[t], SB_BLOCK), :] += dk_t
                    d_ref[2, pl.ds(s0s[t], SB_BLOCK), :] += dv_t
                last = len(kjs) - 1
                cls = [add_rows(cl[last, h], los[last], rsum(sps[last, h])) for h in heads]
                gls = [add_rows(gl[last, h], los[last], rsum(gs[last, h])) for h in heads]
                return (dq, *cls, *gls)

            zero_col = jnp.zeros((SB_BQ, 1), F32)
            out = _sb_keys_ascending(qi, nrun_ref[pl.program_id(0), qi], tile,
                                     (jnp.zeros((SB_BQ, LANES), F32),) + (zero_col,) * (2 * per_pair), has_free)
            d_ref[0, rows, :] = out[0]

        q_block(0, False)
        lax.fori_loop(1, nq, lambda qi, carry: (q_block(qi, True), carry)[1], 0)

    return pl.pallas_call(
        body, name="sb_bwd",
        grid=(n_pairs,),
        in_specs=[_col_block(t_len, SB_FIRST_BLOCK), _col_block(t_len, SB_FIRST_BLOCK + n_pairs),
                  _col_block(t_len, SB_FIRST_BLOCK + 2 * n_pairs),
                  pl.BlockSpec((t_len, LANES), lambda p: (0, p)),
                  pl.BlockSpec(memory_space=pltpu.SMEM), _col_block(t_len, 0), _HBM],
        out_specs=pl.BlockSpec((3, t_len, LANES), lambda p: (DPROJ_SB_SLOT // 3, 0, p)),
        out_shape=jax.ShapeDtypeStruct(dproj.shape, dproj.dtype),
        input_output_aliases={6: 0},
        compiler_params=_params("arbitrary"),
    )(cols, cols, cols, sp_total, n_run_all, d_o, dproj)


def _conv_taps(xin, rows, t_len):
    taps = []
    for i in range(CONV_WIDTH):
        shift = CONV_WIDTH - 1 - i
        if shift == 0:
            taps.append(xin)
        else:
            taps.append(jnp.where(rows >= shift, pltpu.roll(xin, shift, axis=0), 0.0))
    return taps


def _gdn_prep_body_common(x_ref, w_ref, t_len):
    j = pl.program_id(0)
    xin = x_ref[...]
    rows = lax.broadcasted_iota(jnp.int32, (t_len, LANES), 0)
    taps = _conv_taps(xin, rows, t_len)
    pre = taps[0] * w_ref[0:1, :]
    for i in range(1, CONV_WIDTH):
        pre = pre + taps[i] * w_ref[i:i + 1, :]
    sg = _sigmoid(pre)
    act = pre * sg
    is_qk = j < 2 * GDN_HEADS
    nrm = jnp.where(is_qk, lax.rsqrt(jnp.sum(act * act, axis=-1, keepdims=True) + EPS), 1.0)
    sc = jnp.where(j < GDN_HEADS, float(GDN_HEAD_DIM) ** -0.5, 1.0)
    return j, rows, taps, pre, sg, act, is_qk, nrm, sc


def _gdn_prep_call(cols, conv_w, t_len, after):
    def body(x_blk, w_ref, after_ref, out_ref):
        _, _, _, _, _, act, _, nrm, sc = _gdn_prep_body_common(x_blk.at[0], w_ref, t_len)
        out_ref[...] = act * nrm * sc

    return pl.pallas_call(
        body, name="gdn_prep",
        grid=(3 * GDN_HEADS,),
        in_specs=[_col_block(t_len, GDN_FIRST_BLOCK),
                  pl.BlockSpec((CONV_WIDTH, LANES), lambda j: (0, j)),
                  pl.BlockSpec(memory_space=pl.ANY)],
        out_specs=pl.BlockSpec((t_len, LANES), lambda j: (0, j)),
        out_shape=jax.ShapeDtypeStruct((t_len, 3 * 512), F32),
        compiler_params=_params("arbitrary"),
    )(cols, conv_w, after)


def _gdn_prep_bwd_call(cols, conv_w, d_act3, dproj, t_len):
    def body(x_blk, w_ref, d_ref, dproj_in_ref, dx_ref, dw_ref):
        _, rows, taps, pre, sg, act, is_qk, nrm, sc = _gdn_prep_body_common(x_blk.at[0], w_ref, t_len)
        d_out = d_ref[0]
        dn = d_out * sc
        d_norm = nrm * dn - act * (nrm * nrm * nrm) * jnp.sum(dn * act, axis=-1, keepdims=True)
        d_act = jnp.where(is_qk, d_norm, d_out)
        d_pre = d_act * sg * (1.0 + pre * (1.0 - sg))
        dx = d_pre * w_ref[CONV_WIDTH - 1:CONV_WIDTH, :]
        dw_ref[CONV_WIDTH - 1:CONV_WIDTH, :] = jnp.sum(d_pre * taps[CONV_WIDTH - 1], axis=0, keepdims=True)
        for i in range(CONV_WIDTH - 1):
            shift = CONV_WIDTH - 1 - i
            up = jnp.where(rows < t_len - shift, pltpu.roll(d_pre, t_len - shift, axis=0), 0.0)
            dx = dx + up * w_ref[i:i + 1, :]
            dw_ref[i:i + 1, :] = jnp.sum(d_pre * taps[i], axis=0, keepdims=True)
        dx_ref[0] = dx

    return pl.pallas_call(
        body, name="gdn_prep_bwd",
        grid=(3 * GDN_HEADS,),
        in_specs=[_col_block(t_len, GDN_FIRST_BLOCK),
                  pl.BlockSpec((CONV_WIDTH, LANES), lambda j: (0, j)),
                  pl.BlockSpec((1, t_len, LANES), lambda j: (j // GDN_HEADS, 0, j % GDN_HEADS)), _HBM],
        out_specs=[pl.BlockSpec((1, t_len, LANES), lambda j: (DPROJ_GDN_SLOT + j // GDN_HEADS, 0, j % GDN_HEADS)),
                   pl.BlockSpec((CONV_WIDTH, LANES), lambda j: (0, j))],
        out_shape=[jax.ShapeDtypeStruct(dproj.shape, dproj.dtype),
                   jax.ShapeDtypeStruct((CONV_WIDTH, 3 * 512), F32)],
        input_output_aliases={3: 0},
        compiler_params=_params("arbitrary"),
    )(cols, conv_w, d_act3, dproj)


def _chunk_cumsum_matrix():
    r = lax.broadcasted_iota(jnp.int32, (LANES, LANES), 0)
    c = lax.broadcasted_iota(jnp.int32, (LANES, LANES), 1)
    return ((r <= c) & ((r // CHUNK) == (c // CHUNK))).astype(F32)


def _gdn_gates_call(ps, pst, alog_l, dtb_l, alog_c, dtb_c, t_len):
    def body(ps_ref, pst_ref, al_ref, dl_ref, ac_ref, dc_ref, beta_ref, gcol_ref, grow_ref):
        upper = _chunk_cumsum_matrix()
        lower = upper.T
        psv = ps_ref[...]
        beta_ref[...] = _sigmoid(psv)
        g_l = -jnp.exp(al_ref[...]) * _softplus(psv + dl_ref[...])
        g_r = -jnp.exp(ac_ref[...]) * _softplus(pst_ref[...] + dc_ref[...])
        for w in range(t_len // LANES):
            sl = slice(w * LANES, (w + 1) * LANES)
            gcol_ref[sl, :] = _mx(lower, g_l[sl, :])
            grow_ref[:, sl] = _mx(g_r[:, sl], upper)

    vm = pl.BlockSpec(memory_space=pltpu.VMEM)
    return pl.pallas_call(
        body, name="gdn_gates",
        in_specs=[vm] * 6, out_specs=[vm] * 3,
        out_shape=[jax.ShapeDtypeStruct((t_len, LANES), F32),
                   jax.ShapeDtypeStruct((t_len, LANES), F32),
                   jax.ShapeDtypeStruct((8, t_len), F32)],
        compiler_params=pltpu.CompilerParams(vmem_limit_bytes=VMEM_LIMIT_BYTES),
    )(ps, pst, alog_l, dtb_l, alog_c, dtb_c)


def _gdn_gates_bwd_call(ps, alog_l, dtb_l, d_l, t_len):
    def body(ps_ref, al_ref, dl_ref, d_ref, dps_ref, gal_ref, gdt_ref):
        lane = lax.broadcasted_iota(jnp.int32, (1, LANES), 1)
        psv = ps_ref[...]
        dv = d_ref[...]
        beta = _sigmoid(psv)
        ea = jnp.exp(al_ref[...])
        arg = psv + dl_ref[...]
        g = -ea * _softplus(arg)
        d_a = dv * (-ea) * _sigmoid(arg)
        is_a = (lane >= GDN_HEADS) & (lane < 2 * GDN_HEADS)
        dps_ref[...] = jnp.where(lane < GDN_HEADS, dv * beta * (1.0 - beta), jnp.where(is_a, d_a, 0.0))
        gdt_ref[...] = jnp.where(is_a, jnp.sum(d_a, axis=0, keepdims=True), 0.0)
        gal_ref[...] = jnp.where(is_a, jnp.sum(dv * g, axis=0, keepdims=True), 0.0)

    vm = pl.BlockSpec(memory_space=pltpu.VMEM)
    return pl.pallas_call(
        body, name="gdn_gates_bwd",
        in_specs=[vm] * 4, out_specs=[vm] * 3,
        out_shape=[jax.ShapeDtypeStruct((t_len, LANES), F32),
                   jax.ShapeDtypeStruct((1, LANES), F32),
                   jax.ShapeDtypeStruct((1, LANES), F32)],
        compiler_params=pltpu.CompilerParams(vmem_limit_bytes=VMEM_LIMIT_BYTES),
    )(ps, alog_l, dtb_l, d_l)


def _bm(a, b):
    return _m3_general(a, b, _BNN)


def _bm_nt(a, b):
    return _m3_general(a, b, _BNT)


def _bm_tn(a, b):
    return _m3_general(a, b, _BTN)


def _heads_of(ref, rows):
    return jnp.stack([ref[rows, h * GDN_HEAD_DIM:(h + 1) * GDN_HEAD_DIM] for h in range(GDN_HEADS)])


def _chunk_terms(q_ref, k_ref, v_ref, b_ref, gc_ref, gr_ref, c, incl, strict, n=1, scores=True):
    r0 = c * CHUNK if isinstance(c, int) else pl.multiple_of(c * CHUNK, CHUNK)
    rows = pl.ds(r0, n * CHUNK)
    per_chunk = lambda x: x.reshape(GDN_HEADS * n, CHUNK, x.shape[-1])
    q, k, v = (per_chunk(_heads_of(ref, rows)) for ref in (q_ref, k_ref, v_ref))
    lane_ids = lax.broadcasted_iota(jnp.int32, (1, LANES), 1)
    pick = lambda slab, first: jnp.stack([jnp.sum(jnp.where(lane_ids == first + h, slab, 0.0), axis=-1, keepdims=True)
                                          for h in range(GDN_HEADS)])
    b = per_chunk(pick(b_ref[rows, :], 0))
    gc = per_chunk(pick(gc_ref[rows, :], GDN_HEADS))
    gr = gr_ref[:, c] if n == 1 else gr_ref[:, c:c + n].reshape(GDN_HEADS * n, 1, CHUNK)
    dm = jnp.where(incl, jnp.exp(jnp.where(incl, gc - gr, 0.0)), 0.0)
    kb = k * b
    vb = v * b
    e = jnp.exp(gc)
    a = p = None
    if scores:
        kk_qk = _bm_nt(jnp.concatenate([kb, q], axis=1), k)
        a = jnp.where(strict, kk_qk[:, :CHUNK] * dm, 0.0)
        p = jnp.where(incl, kk_qk[:, CHUNK:] * dm, 0.0)
    gl = gc[:, CHUNK - 1:CHUNK, :]
    eg = jnp.exp(gl - gc)
    return rows, q, k, v, b, gc, dm, kb, vb, e, a, p, gl, eg


def _unit_lower_inverse(a, eye):
    x = -a
    tm = eye + x
    xp = _bm(x, x)
    for _ in range(4):
        both = _bm(jnp.concatenate([xp, tm], axis=1), xp)
        tm = tm + both[:, CHUNK:]
        xp = both[:, :CHUNK]
    return tm + _bm(tm, xp)


def _gdn_specs(t_len, n_chunks, reverse):
    cps = GDN_CHUNKS_PER_STEP
    steps = n_chunks // cps
    at = (lambda g: steps - 1 - g) if reverse else (lambda g: g)
    rows_blk = lambda width, part=0: pl.BlockSpec((cps * CHUNK, width), lambda g: (at(g), part))
    gate_r = pl.BlockSpec((GDN_HEADS, cps, 1, CHUNK), lambda g: (0, at(g), 0, 0))
    per_chunk = lambda r, c: pl.BlockSpec((GDN_HEADS, cps, r, c), lambda g: (0, at(g), 0, 0))
    return cps, steps, rows_blk, gate_r, per_chunk


def _gdn_fwd_call(gact, beta_c, gam_c, gam_r, t_len):
    n_chunks = t_len // CHUNK
    dk = GDN_HEAD_DIM
    width = GDN_HEADS * dk
    cps, steps, rows_blk, gate_r, per_chunk = _gdn_specs(t_len, n_chunks, False)

    def body(q_ref, k_ref, v_ref, b_ref, gc_ref, gr_ref, o_ref, s_ref, t_ref, a_ref, p_ref, uw_ref, vn_ref, state_ref):
        row = lax.broadcasted_iota(jnp.int32, (CHUNK, CHUNK), 0)
        col = lax.broadcasted_iota(jnp.int32, (CHUNK, CHUNK), 1)
        incl, strict = row >= col, row > col
        eye = (row == col).astype(F32)

        @pl.when(pl.program_id(0) == 0)
        def _():
            state_ref[...] = jnp.zeros_like(state_ref)

        _, q, k, v, b, gc, dm, kb, vb, e, a, p, gl, eg = _chunk_terms(
            q_ref, k_ref, v_ref, b_ref, gc_ref, gr_ref, 0, incl, strict, cps)
        tm = _unit_lower_inverse(a, eye)
        uw = _bm(tm, jnp.concatenate([vb, kb * e], axis=2))
        w_qe = jnp.concatenate([uw[:, :, dk:], q * e], axis=1)
        u, kd, decay = uw[:, :, :dk], k * eg, jnp.exp(gl)
        per_chunk_block = lambda x: x.reshape(GDN_HEADS, cps, CHUNK, CHUNK)
        t_ref[...], a_ref[...], p_ref[...] = per_chunk_block(tm), per_chunk_block(a), per_chunk_block(p)
        uw_heads = uw.reshape(GDN_HEADS, cps * CHUNK, 2 * dk)
        for h in range(GDN_HEADS):
            uw_ref[:, h * 2 * dk:(h + 1) * 2 * dk] = uw_heads[h]

        of_chunk = lambda x, c: jnp.stack([x[h * cps + c] for h in range(GDN_HEADS)])
        s = state_ref[...]
        for c in range(cps):
            ws_qs = _bm(of_chunk(w_qe, c), s)
            vn = of_chunk(u, c) - ws_qs[:, :CHUNK]
            o = ws_qs[:, CHUNK:] + _bm(of_chunk(p, c), vn)
            for h in range(GDN_HEADS):
                o_ref[c * CHUNK:(c + 1) * CHUNK, h * dk:(h + 1) * dk] = o[h]
                vn_ref[c * CHUNK:(c + 1) * CHUNK, h * dk:(h + 1) * dk] = vn[h]
            s_ref[:, c] = s
            s = s * of_chunk(decay, c) + _bm_tn(of_chunk(kd, c), vn)
        state_ref[...] = s

    scores = jax.ShapeDtypeStruct((GDN_HEADS, n_chunks, CHUNK, CHUNK), F32)
    return pl.pallas_call(
        body, name="gdn_fwd",
        grid=(steps,),
        in_specs=[rows_blk(width, 0), rows_blk(width, 1), rows_blk(width, 2), rows_blk(LANES), rows_blk(LANES), gate_r],
        out_specs=[rows_blk(width), per_chunk(dk, dk), per_chunk(CHUNK, CHUNK), per_chunk(CHUNK, CHUNK),
                   per_chunk(CHUNK, CHUNK), rows_blk(2 * width), rows_blk(width)],
        out_shape=[jax.ShapeDtypeStruct((t_len, width), F32),
                   jax.ShapeDtypeStruct((GDN_HEADS, n_chunks, dk, dk), F32), scores, scores, scores,
                   jax.ShapeDtypeStruct((t_len, 2 * width), F32), jax.ShapeDtypeStruct((t_len, width), F32)],
        scratch_shapes=[pltpu.VMEM((GDN_HEADS, dk, dk), F32)],
        compiler_params=_params("arbitrary"),
    )(gact, gact, gact, beta_c, gam_c, gam_r)


def _gdn_bwd_call(gact, beta_c, gam_c, gam_r, saved, d_o, t_len, scatter=()):
    n_chunks = t_len // CHUNK
    dk = GDN_HEAD_DIM
    width = GDN_HEADS * dk
    cps, steps, rows_blk, gate_r, per_chunk = _gdn_specs(t_len, n_chunks, True)
    nx = len(scatter)
    n_in = 13

    def body(*refs):
        q_ref, k_ref, v_ref, b_ref, gc_ref, gr_ref = refs[:6]
        saved_refs, do_ref = refs[6:12], refs[12]
        d_ref, dgate_ref = refs[n_in + nx:n_in + 2 + nx]
        dstate_ref = refs[n_in + 2 + 2 * nx]
        copies = lambda: _direct_copies(refs[n_in:n_in + nx], refs[n_in + 2 + nx:n_in + 2 + 2 * nx],
                                        *refs[n_in + 3 + 2 * nx:], (True,) * nx)
        if nx:
            pl.when(pl.program_id(0) == 0)(lambda: _start_all(copies()))
        row = lax.broadcasted_iota(jnp.int32, (CHUNK, CHUNK), 0)
        col = lax.broadcasted_iota(jnp.int32, (CHUNK, CHUNK), 1)
        incl, strict = row >= col, row > col
        ng = GDN_BWD_GROUP
        nb = GDN_HEADS * ng
        upper = jnp.broadcast_to((row <= col).astype(F32), (nb, CHUNK, CHUNK))
        ones = jnp.ones((nb, CHUNK, LANES), F32)
        last_row = lax.broadcasted_iota(jnp.int32, (CHUNK, 1), 0) == CHUNK - 1
        lane_ids = lax.broadcasted_iota(jnp.int32, (1, LANES), 1)
        rsum = lambda m: jnp.sum(m, axis=-1, keepdims=True)
        total = lambda m: jnp.sum(rsum(m), axis=1, keepdims=True)
        of_chunk = lambda x, c: jnp.stack([x[h * ng + c] for h in range(GDN_HEADS)])

        @pl.when(pl.program_id(0) == 0)
        def _():
            dstate_ref[...] = jnp.zeros_like(dstate_ref)

        for c0 in range(cps - ng, -1, -ng):
            group(c0, q_ref, k_ref, v_ref, b_ref, gc_ref, gr_ref, saved_refs, do_ref, d_ref, dgate_ref, dstate_ref,
                  incl, strict, upper, ones, last_row, lane_ids, rsum, total, of_chunk)
        if nx:
            pl.when(pl.program_id(0) == steps - 1)(lambda: _wait_all(copies()))

    def group(c0, q_ref, k_ref, v_ref, b_ref, gc_ref, gr_ref, saved_refs, do_ref, d_ref, dgate_ref, dstate_ref,
              incl, strict, upper, ones, last_row, lane_ids, rsum, total, of_chunk):
        ng = GDN_BWD_GROUP
        nb = GDN_HEADS * ng
        rows = pl.ds(c0 * CHUNK, ng * CHUNK)
        s_ref, t_ref, a_ref, p_ref, uw_ref, vn_ref = saved_refs
        _, q, k, v, b, gc, dm, kb, vb, e, _, _, gl, eg = _chunk_terms(
            q_ref, k_ref, v_ref, b_ref, gc_ref, gr_ref, c0, incl, strict, ng, scores=False)
        s = s_ref[:, c0:c0 + ng].reshape(nb, dk, dk)
        tm = t_ref[:, c0:c0 + ng].reshape(nb, CHUNK, CHUNK)
        a = a_ref[:, c0:c0 + ng].reshape(nb, CHUNK, CHUNK)
        p = p_ref[:, c0:c0 + ng].reshape(nb, CHUNK, CHUNK)
        d_out = _heads_of(do_ref, rows).reshape(nb, CHUNK, dk)
        vn = _heads_of(vn_ref, rows).reshape(nb, CHUNK, dk)
        uw = jnp.stack([uw_ref[rows, h * 2 * dk:(h + 1) * 2 * dk] for h in range(GDN_HEADS)]).reshape(nb, CHUNK, 2 * dk)
        u, w = uw[:, :, :dk], uw[:, :, dk:]
        el = jnp.exp(gl)
        kbe = kb * e
        qe = q * e
        kd = k * eg
        pt_do = _bm_tn(p, d_out)
        qet_do = _bm_tn(qe, d_out)

        ds = dstate_ref[...]
        d_vn_c, ds_c = [None] * ng, [None] * ng
        for c in range(ng - 1, -1, -1):
            ds_c[c] = ds
            d_vn_c[c] = of_chunk(pt_do, c) + _bm(of_chunk(kd, c), ds)
            ds = of_chunk(el, c) * ds + of_chunk(qet_do, c) - _bm_tn(of_chunk(w, c), d_vn_c[c])
        dstate_ref[...] = ds
        by_chunk = lambda xs: jnp.stack([xs[c][h] for h in range(GDN_HEADS) for c in range(ng)])
        d_vn, ds = by_chunk(d_vn_c), by_chunk(ds_c)

        on_s = _bm_nt(jnp.concatenate([d_out, d_vn], axis=1), s)
        d_qe, d_w = on_s[:, :CHUNK], -on_s[:, CHUNK:]
        d_p = jnp.where(incl, _bm_nt(d_out, vn), 0.0)
        d_kd = _bm_nt(vn, ds)
        d_both = _bm_tn(tm, jnp.concatenate([d_vn, d_w], axis=2))
        d_vb, d_kbe = d_both[:, :, :dk], d_both[:, :, dk:]
        d_a = -jnp.where(strict, _bm_nt(d_both, uw), 0.0)
        m = d_a * dm
        n = d_p * dm
        on_k = _bm(jnp.concatenate([m, n], axis=1), k)
        d_kb = on_k[:, :CHUNK] + d_kbe * e
        d_q = on_k[:, CHUNK:] + d_qe * e
        d_k = (_bm_tn(jnp.concatenate([m, n], axis=1), jnp.concatenate([kb, q], axis=1))
               + d_kd * eg + b * d_kb)
        d_v = b * d_vb
        r = d_a * a + d_p * p
        kd_term = rsum(d_kd * kd)
        d_gl = total(ds * s) * el + jnp.sum(kd_term, axis=1, keepdims=True)
        d_gam = (rsum(r) - _times_exact(r, ones, _BTN)[:, :, 0:1] + rsum(d_qe * qe) + rsum(d_kbe * kbe) - kd_term
                 + jnp.where(last_row, d_gl, 0.0))
        d_beta = rsum(d_kb * k) + rsum(d_vb * v)
        d_g = _exact_times(upper, d_gam * ones, _BNN)[:, :, 0:1]
        per_head = lambda x: x.reshape(GDN_HEADS, ng * CHUNK, x.shape[-1])
        d_q, d_k, d_v, d_beta, d_g = (per_head(x) for x in (d_q, d_k, d_v, d_beta, d_g))
        gates = jnp.zeros((ng * CHUNK, LANES), F32)
        for h in range(GDN_HEADS):
            lanes = slice(h * dk, (h + 1) * dk)
            d_ref[0, rows, lanes] = d_q[h]
            d_ref[1, rows, lanes] = d_k[h]
            d_ref[2, rows, lanes] = d_v[h]
            gates = gates + (jnp.where(lane_ids == h, d_beta[h], 0.0)
                             + jnp.where(lane_ids == GDN_HEADS + h, d_g[h], 0.0))
        dgate_ref[rows, :] = gates

    d_spec = pl.BlockSpec((3, cps * CHUNK, width), lambda g: (0, steps - 1 - g, 0))
    return pl.pallas_call(
        body, name="gdn_bwd",
        grid=(steps,),
        in_specs=[rows_blk(width, 0), rows_blk(width, 1), rows_blk(width, 2), rows_blk(LANES), rows_blk(LANES), gate_r,
                  per_chunk(dk, dk), per_chunk(CHUNK, CHUNK), per_chunk(CHUNK, CHUNK), per_chunk(CHUNK, CHUNK),
                  rows_blk(2 * width), rows_blk(width), rows_blk(width)] + [_HBM] * nx,
        out_specs=[d_spec, rows_blk(LANES)] + [_HBM] * nx,
        out_shape=[jax.ShapeDtypeStruct((3, t_len, width), F32),
                   jax.ShapeDtypeStruct((t_len, LANES), F32)] + _direct_out_shapes(scatter, (True,) * nx),
        scratch_shapes=[pltpu.VMEM((GDN_HEADS, dk, dk), F32)] + (_direct_semaphores(nx) if nx else []),
        compiler_params=_params("arbitrary"),
    )(gact, gact, gact, beta_c, gam_c, gam_r, *saved, d_o, *scatter)


def _group_sums(x, group):
    rows, width = x.shape
    lane = lax.broadcasted_iota(jnp.int32, (1, LANES), 1)
    out = []
    for t in range(width // LANES):
        seg = x[:, t * LANES:(t + 1) * LANES]
        if group == LANES:
            out.append(jnp.broadcast_to(jnp.sum(seg, axis=-1, keepdims=True), (rows, LANES)))
        else:
            low = jnp.sum(jnp.where(lane < group, seg, 0.0), axis=-1, keepdims=True)
            high = jnp.sum(jnp.where(lane < group, 0.0, seg), axis=-1, keepdims=True)
            out.append(jnp.where(lane < group, low, high))
    return jnp.concatenate(out, axis=1)


def _post_call(o_sb, o_gd, proj_gates, x, target, w_out, sbw, gdw, fw, tm=256):
    t_len, d = x.shape
    half = 512
    sb_blocks = half // LANES

    def body(osb_ref, ogd_ref, zsb_ref, zgd_ref, x_ref, tg_ref, wo_ref, sbw_ref, gdw_ref, fw_ref,
             dx2_ref, dosb_ref, dogd_ref, dz_ref, loss_ref, gfw_ref, gsb_ref, ggd_ref, gwo_ref):
        step = pl.program_id(0)

        @pl.when(step == 0)
        def _():
            loss_ref[...] = jnp.zeros_like(loss_ref)
            gfw_ref[...] = jnp.zeros_like(gfw_ref)
            gsb_ref[...] = jnp.zeros_like(gsb_ref)
            ggd_ref[...] = jnp.zeros_like(ggd_ref)
            gwo_ref[...] = jnp.zeros_like(gwo_ref)

        def head_forward(o, z, w, head_dim):
            r = lax.rsqrt(_group_sums(o * o, head_dim) * (1.0 / head_dim) + EPS)
            nrm = o * r * w
            sg = _sigmoid(z)
            return r, nrm, sg, nrm * (z * sg)

        def head_backward(d_m, o, z, w, head_dim, r, nrm, sg):
            d_n = d_m * (z * sg)
            d_z = d_m * nrm * (sg * (1.0 + z * (1.0 - sg)))
            dnw = d_n * w
            d_o = r * dnw - o * (r * r * r) * (_group_sums(dnw * o, head_dim) * (1.0 / head_dim))
            return d_o, d_z, jnp.sum(d_n * o * r, axis=0, keepdims=True)

        osb = jnp.concatenate([osb_ref[j] for j in range(sb_blocks)], axis=1)
        ogd, zsb, zgd = ogd_ref[...], zsb_ref[...], zgd_ref[...]
        sbw_v, gdw_v = sbw_ref[...], gdw_ref[...]
        r_sb, n_sb, sg_sb, m_sb = head_forward(osb, zsb, sbw_v, SB_HEAD_DIM)
        r_gd, n_gd, sg_gd, m_gd = head_forward(ogd, zgd, gdw_v, GDN_HEAD_DIM)
        mixed = jnp.concatenate([m_sb, m_gd], axis=1).astype(MXU_DTYPE)
        wo = wo_ref[...]
        x2 = x_ref[...] + jnp.dot(mixed, wo, preferred_element_type=F32)
        r2 = lax.rsqrt(jnp.mean(x2 * x2, axis=-1, keepdims=True) + EPS)
        fw_v = fw_ref[...]
        err = x2 * r2 * fw_v - tg_ref[...]
        loss_ref[...] += 0.5 * jnp.sum(jnp.sum(err * err, axis=-1, keepdims=True) * (1.0 / d))
        dy = err * (1.0 / d)
        gg = dy * fw_v
        dx2 = r2 * gg - x2 * ((r2 * r2 * r2) * jnp.mean(gg * x2, axis=-1, keepdims=True))
        gfw_ref[...] += jnp.sum(dy * x2 * r2, axis=0, keepdims=True)
        dx2_ref[...] = dx2
        dx2b = dx2.astype(MXU_DTYPE)
        d_mixed = lax.dot_general(dx2b, wo, _NT, preferred_element_type=F32)
        gwo_ref[...] += lax.dot_general(mixed, dx2b, _TN, preferred_element_type=F32)
        d_osb, d_zsb, gsb = head_backward(d_mixed[:, :half], osb, zsb, sbw_v, SB_HEAD_DIM, r_sb, n_sb, sg_sb)
        d_ogd, d_zgd, ggd = head_backward(d_mixed[:, half:], ogd, zgd, gdw_v, GDN_HEAD_DIM, r_gd, n_gd, sg_gd)
        for j in range(sb_blocks):
            dosb_ref[j] = d_osb[:, j * LANES:(j + 1) * LANES]
        dogd_ref[...] = d_ogd
        dz_ref[0] = d_zsb
        dz_ref[1] = d_zgd
        gsb_ref[...] += gsb
        ggd_ref[...] += ggd

    row_blk = lambda w: pl.BlockSpec((tm, w), lambda i: (i, 0))
    blocks_blk = pl.BlockSpec((sb_blocks, tm, LANES), lambda i: (0, i, 0))
    fixed = lambda r, w: pl.BlockSpec((r, w), lambda i: (0, 0))
    return pl.pallas_call(
        body, name="post",
        grid=(t_len // tm,),
        in_specs=[blocks_blk, row_blk(half),
                  pl.BlockSpec((tm, half), lambda i: (i, 0)),
                  pl.BlockSpec((tm, half), lambda i: (i, 1)),
                  row_blk(d), row_blk(d), fixed(d, d), fixed(1, half), fixed(1, half), fixed(1, d)],
        out_specs=[row_blk(d), blocks_blk, row_blk(half),
                   pl.BlockSpec((2, tm, half), lambda i: (DPROJ_GATE_SLOT // 2, i, 0)),
                   fixed(1, LANES), fixed(1, d), fixed(1, half), fixed(1, half), fixed(d, d)],
        out_shape=[jax.ShapeDtypeStruct((t_len, d), F32), jax.ShapeDtypeStruct((sb_blocks, t_len, LANES), F32),
                   jax.ShapeDtypeStruct((t_len, half), F32),
                   jax.ShapeDtypeStruct((len(DPROJ_PIECE_OF_SLOT), t_len, half), F32),
                     jax.ShapeDtypeStruct((1, LANES), F32), jax.ShapeDtypeStruct((1, d), F32),
                     jax.ShapeDtypeStruct((1, half), F32), jax.ShapeDtypeStruct((1, half), F32),
                     jax.ShapeDtypeStruct((d, d), F32)],
        compiler_params=_params("arbitrary"),
    )(o_sb, o_gd, proj_gates, proj_gates, x, target, w_out, sbw, gdw, fw)


def _piece_of_slot(s):
    return jnp.where(s < DPROJ_GDN_SLOT, s, jnp.where(s < DPROJ_GATE_SLOT, s + 1,
                                                     jnp.where(s == DPROJ_GATE_SLOT, 3, 7)))


def _gw_in_call(h_t, dproj8):
    d, t_len = h_t.shape
    n_piece, _, pw = dproj8.shape

    def body(ht_ref, dp_ref, gw_ref):
        gw_ref[...] = jnp.dot(ht_ref[...], dp_ref[0].astype(MXU_DTYPE), preferred_element_type=F32)

    return pl.pallas_call(
        body, name="gw_in",
        grid=(n_piece,),
        in_specs=[pl.BlockSpec((d, t_len), lambda s: (0, 0)),
                  pl.BlockSpec((1, t_len, pw), lambda s: (s, 0, 0))],
        out_specs=pl.BlockSpec((d, pw), lambda s: (0, _piece_of_slot(s))),
        out_shape=jax.ShapeDtypeStruct((d, n_piece * pw), F32),
        compiler_params=_params("arbitrary"),
    )(h_t, dproj8)


def _slot_of_piece(p):
    return jnp.where(p < DPROJ_GDN_SLOT, p, jnp.where(p == 3, DPROJ_GATE_SLOT, jnp.where(p < 7, p - 1, 7)))


def _gw_in_shards_call(h_t, dproj8, dsmall, out_dtype):
    d, t_len = h_t.shape
    n_piece, _, pw = dproj8.shape
    ns = dsmall.shape[1]
    n_pairs = N_DEV // 2

    def body(ht_ref, dp_ref, ds_ref, chip_ref, prev_ref, gates_ref, send_ref, recv_ref, send_sems, recv_sems):
        p = pl.program_id(0)
        x_pos, y_pos, c = lax.axis_index("x"), lax.axis_index("y"), lax.axis_index("c")
        to_sibling = lambda pair: pltpu.make_async_remote_copy(
            src_ref=send_ref.at[pair], dst_ref=recv_ref.at[pair], send_sem=send_sems.at[pair],
            recv_sem=recv_sems.at[pair], device_id=(x_pos, y_pos, 1 - c), device_id_type=_MESH)

        @pl.when(p == 0)
        def _():
            gates_ref[...] = jnp.dot(ht_ref[...], ds_ref[...].astype(MXU_DTYPE), preferred_element_type=F32)

        def emit(s, tail):
            x = jnp.concatenate([prev_ref[...], tail], axis=1)
            y = x if s == 0 else pltpu.roll(x, SHARD_PAD - s, axis=1)
            shard = y[:, :SHARD_COLS].astype(out_dtype)

            @pl.when(c == s % 2)
            def _():
                chip_ref[s // 2] = shard

            @pl.when(c != s % 2)
            def _():
                send_ref[s // 2] = shard
                to_sibling(s // 2).start()

        @pl.when(p < n_piece)
        def _():
            cur = jnp.dot(ht_ref[...], dp_ref[0].astype(MXU_DTYPE), preferred_element_type=F32)
            for s in range(n_piece - 1):
                pl.when(p == s + 1)(functools.partial(emit, s, cur[:, :SHARD_PAD - pw]))
            prev_ref[...] = cur

        @pl.when(p == n_piece)
        def _():
            emit(n_piece - 1, gates_ref[...])
            for pair in range(n_pairs):
                to_sibling(pair).wait_send()
            for pair in range(n_pairs):
                to_sibling(pair).wait_recv()
                chip_ref[pair] = (chip_ref[pair].astype(F32) + recv_ref[pair].astype(F32)).astype(out_dtype)

    shards_of_side = lambda: pltpu.VMEM((n_pairs, d, SHARD_COLS), out_dtype)
    return pl.pallas_call(
        body, name="gw_in",
        grid=(n_piece + 1,),
        in_specs=[pl.BlockSpec((d, t_len), lambda p: (0, 0)),
                  pl.BlockSpec((1, t_len, pw), lambda p: (_slot_of_piece(jnp.minimum(p, n_piece - 1)), 0, 0)),
                  pl.BlockSpec((t_len, ns), lambda p: (0, 0))],
        out_specs=pl.BlockSpec((n_pairs, d, SHARD_COLS), lambda p: (0, 0, 0)),
        out_shape=jax.ShapeDtypeStruct((n_pairs, d, SHARD_COLS), out_dtype),
        scratch_shapes=[pltpu.VMEM((d, pw), F32), pltpu.VMEM((d, ns), F32), shards_of_side(), shards_of_side(),
                        pltpu.SemaphoreType.DMA((n_pairs,)), pltpu.SemaphoreType.DMA((n_pairs,))],
        compiler_params=_params("arbitrary"),
    )(h_t, dproj8, dsmall)


def _gw_small_call(h_t, dsmall, tm=512):
    d, t_len = h_t.shape
    ns = dsmall.shape[1]

    def body(ht_ref, dp_ref, gw_ref):
        @pl.when(pl.program_id(0) == 0)
        def _():
            gw_ref[...] = jnp.zeros_like(gw_ref)

        gw_ref[...] += jnp.dot(ht_ref[...], dp_ref[...].astype(MXU_DTYPE), preferred_element_type=F32)

    return pl.pallas_call(
        body, name="gw_small",
        grid=(t_len // tm,),
        in_specs=[pl.BlockSpec((d, tm), lambda t: (0, t)),
                  pl.BlockSpec((tm, ns), lambda t: (t, 0))],
        out_specs=pl.BlockSpec((d, ns), lambda t: (0, 0)),
        out_shape=jax.ShapeDtypeStruct((d, ns), F32),
        compiler_params=_params("arbitrary"),
    )(h_t, dsmall)


def _dx_call(dproj8, dsmall, w_main, w_small, x, r, dx2, norm_w, chip_scatter=(), peer_scatter=(), tm=256):
    t_len, d = x.shape
    n_piece, _, pw = dproj8.shape
    ns = dsmall.shape[1]
    nx = len(chip_scatter)
    n_peer = len(peer_scatter)
    steps = t_len // tm
    n_in = 8 + nx + n_peer
    n_out = 2 + nx + n_peer

    def body(*refs):
        dp_ref, ds_ref, wm_ref, ws_ref, x_ref, r_ref, dx2_ref, nw_ref = refs[:8]
        gx_ref, gnw_ref = refs[n_in:n_in + 2]
        scratch = refs[n_in + n_out:]
        copies = lambda: _chip_copies(refs[8:8 + nx], refs[n_in + 2:n_in + 2 + nx], *scratch[:3])
        if nx:
            pl.when(pl.program_id(0) == 0)(lambda: _start_all(copies()))

        @pl.when(pl.program_id(0) == 0)
        def _():
            gnw_ref[...] = jnp.zeros_like(gnw_ref)

        dh = lax.dot_general(ds_ref[...].astype(MXU_DTYPE), ws_ref[...], _NT, preferred_element_type=F32)
        for s, p in enumerate(DPROJ_PIECE_OF_SLOT):
            dh = dh + lax.dot_general(dp_ref[s].astype(MXU_DTYPE), wm_ref[:, p * pw:(p + 1) * pw], _NT,
                                      preferred_element_type=F32)
        xv, rv = x_ref[...], r_ref[...]
        dn = dh * nw_ref[...]
        gx_ref[...] = dx2_ref[...] + rv * dn - xv * ((rv * rv * rv) * jnp.mean(dn * xv, axis=-1, keepdims=True))
        gnw_ref[...] += jnp.sum(dh * xv * rv, axis=0, keepdims=True)

        @pl.when(pl.program_id(0) == steps - 1)
        def _():
            if n_peer:
                small_ref, parts_ref = refs[8 + nx:n_in]
                small_buf = scratch[6]
                small_buf[...] = small_ref[...]
                small_buf[0:1, :] = gnw_ref[...]
                peer_copies = _direct_copies([small_buf, parts_ref], refs[n_in + 2 + nx:n_in + n_out], *scratch[3:6],
                                             [False, True])
                _start_all(peer_copies)
            if nx:
                _wait_all(copies())
            if n_peer:
                _wait_all(peer_copies)

    assert n_peer in (0, 2) and (nx == 1 or not n_peer)
    peer_in_specs = [pl.BlockSpec(peer_scatter[0].shape, lambda i: (0, 0)), _HBM] if n_peer else []
    peer_scratch = _direct_semaphores(n_peer) + [pltpu.VMEM(peer_scatter[0].shape, F32)] if n_peer else []
    return pl.pallas_call(
        body, name="dx",
        grid=(steps,),
        in_specs=[pl.BlockSpec((n_piece, tm, pw), lambda i: (0, i, 0)),
                  pl.BlockSpec((tm, ns), lambda i: (i, 0)),
                  pl.BlockSpec((d, n_piece * pw), lambda i: (0, 0)),
                  pl.BlockSpec((d, ns), lambda i: (0, 0)),
                  pl.BlockSpec((tm, d), lambda i: (i, 0)),
                  pl.BlockSpec((tm, 1), lambda i: (i, 0)),
                  pl.BlockSpec((tm, d), lambda i: (i, 0)),
                  pl.BlockSpec((1, d), lambda i: (0, 0))] + [_HBM] * nx + peer_in_specs,
        out_specs=[pl.BlockSpec((tm, d), lambda i: (i, 0)),
                   pl.BlockSpec((1, d), lambda i: (0, 0))] + [_HBM] * (nx + n_peer),
        out_shape=[jax.ShapeDtypeStruct((t_len, d), F32), jax.ShapeDtypeStruct((1, d), F32)]
                  + [jax.ShapeDtypeStruct(a.shape, a.dtype) for a in chip_scatter]
                  + (_direct_out_shapes(peer_scatter, [False, True]) if n_peer else []),
        scratch_shapes=(_chip_semaphores(nx) if nx else []) + peer_scratch,
        compiler_params=_params("arbitrary"),
    )(dproj8, dsmall, w_main, w_small, x, r, dx2, norm_w, *chip_scatter, *peer_scatter)


def _direct_out_shapes(srcs, per_peer):
    return [jax.ShapeDtypeStruct(s.shape if pp else (N_DEV,) + s.shape, s.dtype) for s, pp in zip(srcs, per_peer)]


def _direct_semaphores(n):
    return [pltpu.SemaphoreType.DMA((n * (N_DEV - 1),)), pltpu.SemaphoreType.DMA((n * (N_DEV - 1),)),
            pltpu.SemaphoreType.DMA((n,))]


def _direct_copies(src_refs, out_refs, send_sems, recv_sems, local_sems, per_peer):
    x, y, c = lax.axis_index("x"), lax.axis_index("y"), lax.axis_index("c")
    me = 4 * x + 2 * y + c
    local, remote = [], []
    for a in range(len(src_refs)):
        mine = src_refs[a].at[me] if per_peer[a] else src_refs[a]
        local.append(pltpu.make_async_copy(mine, out_refs[a].at[me], local_sems.at[a]))
    for k in range(1, N_DEV):
        kx, ky, kc = (k >> 2) & 1, (k >> 1) & 1, k & 1
        px = 1 - x if kx else x
        py = 1 - y if ky else y
        pc = 1 - c if kc else c
        peer = 4 * px + 2 * py + pc
        for a in range(len(src_refs)):
            sem = a * (N_DEV - 1) + (k - 1)
            remote.append(pltpu.make_async_remote_copy(
                src_ref=src_refs[a].at[peer] if per_peer[a] else src_refs[a], dst_ref=out_refs[a].at[me],
                send_sem=send_sems.at[sem], recv_sem=recv_sems.at[sem],
                device_id=(px, py, pc), device_id_type=pl.DeviceIdType.MESH))
    return local, remote


def _start_all(copies):
    local, remote = copies
    for cp in local + remote:
        cp.start()


def _wait_all(copies):
    local, remote = copies
    for cp in remote:
        cp.wait_send()
    for cp in remote:
        cp.wait_recv()
    for cp in local:
        cp.wait()


N_CHIPS = 4
_HBM = pl.BlockSpec(memory_space=pl.ANY)
_MESH = pl.DeviceIdType.MESH


def _gather_call(name, srcs):
    n = len(srcs)
    per = N_DEV - 1

    def body(*refs):
        src_refs, out_refs = refs[:n], refs[n:2 * n]
        send_sems, recv_sems, local_sems = refs[2 * n:]
        x, y, c = lax.axis_index("x"), lax.axis_index("y"), lax.axis_index("c")
        me, sibling = (x, y, c), (x, y, 1 - c)
        x_nbr, y_nbr, diagonal = (1 - x, y), (x, 1 - y), (1 - x, 1 - y)
        held = ((1 - x) * c + x * (1 - c), y * c + (1 - y) * (1 - c))
        onward = (x * c + (1 - x) * (1 - c), (1 - y) * c + y * (1 - c))
        slot = lambda px, py, pc: 4 * px + 2 * py + pc

        def copy(a, k, block, to, from_src=False):
            rows = out_refs[a].at[slot(*block)]
            return pltpu.make_async_remote_copy(
                src_ref=src_refs[a] if from_src else rows, dst_ref=rows,
                send_sem=send_sems.at[a * per + k], recv_sem=recv_sems.at[a * per + k],
                device_id=to, device_id_type=_MESH)

        local = [pltpu.make_async_copy(src_refs[a], out_refs[a].at[slot(*me)], local_sems.at[a]) for a in range(n)]
        started = []

        def start(cp):
            cp.start()
            started.append(cp)

        for cp in local:
            cp.start()
        for a in range(n):
            start(copy(a, 0, me, sibling, True))
            start(copy(a, 1, me, (*x_nbr, c), True))
            start(copy(a, 2, me, (*y_nbr, c), True))
        for a in range(n):
            copy(a, 1, (*x_nbr, c), me).wait_recv()
            copy(a, 2, (*y_nbr, c), me).wait_recv()
            start(copy(a, 3, (*held, c), (*onward, c)))
            start(copy(a, 4, (*x_nbr, c), sibling))
            start(copy(a, 5, (*y_nbr, c), sibling))
        for a in range(n):
            copy(a, 3, (*diagonal, c), me).wait_recv()
            start(copy(a, 6, (*diagonal, c), sibling))
        for a in range(n):
            copy(a, 0, sibling, me).wait_recv()
            for k, chip in ((4, x_nbr), (5, y_nbr), (6, diagonal)):
                copy(a, k, (*chip, 1 - c), me).wait_recv()
        for cp in started:
            cp.wait_send()
        for cp in local:
            cp.wait()

    return pl.pallas_call(
        body, name=name,
        in_specs=[_HBM] * n, out_specs=[_HBM] * n,
        out_shape=[jax.ShapeDtypeStruct((N_DEV,) + s.shape, s.dtype) for s in srcs],
        scratch_shapes=[pltpu.SemaphoreType.DMA((n * per,)), pltpu.SemaphoreType.DMA((n * per,)),
                        pltpu.SemaphoreType.DMA((n,))],
    )(*srcs)


def _chip_semaphores(n):
    per = N_CHIPS - 1
    return [pltpu.SemaphoreType.DMA((n * per,)), pltpu.SemaphoreType.DMA((n * per,)), pltpu.SemaphoreType.DMA((n,))]


def _chip_copies(src_refs, out_refs, send_sems, recv_sems, local_sems):
    per = N_CHIPS - 1
    x, y, c = lax.axis_index("x"), lax.axis_index("y"), lax.axis_index("c")
    mine = 2 * x + y
    chips = [(1 - x, y), (x, 1 - y), (1 - x, 1 - y)]
    n = len(src_refs)
    local = [pltpu.make_async_copy(src_refs[a].at[mine], out_refs[a].at[mine], local_sems.at[a]) for a in range(n)]
    remote = []
    for a in range(n):
        for j, (px, py) in enumerate(chips):
            remote.append(pltpu.make_async_remote_copy(
                src_ref=src_refs[a].at[2 * px + py], dst_ref=out_refs[a].at[mine],
                send_sem=send_sems.at[a * per + j], recv_sem=recv_sems.at[a * per + j],
                device_id=(px, py, c), device_id_type=_MESH))
    return local, remote


def _adam_call(name, parts, w, m, v, tr):
    rows, cols = w.shape
    n_slots = parts.shape[0]

    def body(p_ref, w_ref, m_ref, v_ref, g_ref, d_ref, nm_ref, nv_ref):
        g = p_ref[0].astype(F32)
        for s in range(1, n_slots):
            g = g + p_ref[s].astype(F32)
        m_new = ADAM_B1 * m_ref[...] + (1.0 - ADAM_B1) * g
        v_new = ADAM_B2 * v_ref[...] + (1.0 - ADAM_B2) * (g * g)
        m_hat = m_new / (1.0 - ADAM_B1 ** ADAM_STEP)
        v_hat = v_new / (1.0 - ADAM_B2 ** ADAM_STEP)
        g_ref[...] = g
        d_ref[...] = -ADAM_LR * (m_hat / (jnp.sqrt(v_hat) + ADAM_EPS) + ADAM_WD * w_ref[...])
        nm_ref[...] = m_new
        nv_ref[...] = v_new

    blk = pl.BlockSpec((tr, cols), lambda i: (i, 0))
    return pl.pallas_call(
        body, name=name,
        grid=(rows // tr,),
        in_specs=[pl.BlockSpec((n_slots, tr, cols), lambda i: (0, i, 0)), blk, blk, blk],
        out_specs=[blk] * 4,
        out_shape=[jax.ShapeDtypeStruct((rows, cols), F32)] * 4,
        compiler_params=_params("arbitrary"),
    )(parts, w, m, v)


def _columns_to_rows_call(name, w_t, rows, dtype):
    row_tiles = rows // LANES
    cols = w_t.shape[0] // row_tiles
    whole = cols // LANES * LANES

    def body(w_ref, out_ref):
        diagonal = (lax.broadcasted_iota(jnp.int32, (LANES, LANES), 0)
                    == lax.broadcasted_iota(jnp.int32, (LANES, LANES), 1))
        for a in range(row_tiles):
            out_ref[a * LANES:(a + 1) * LANES, :whole] = (
                w_ref[pl.ds(a, whole, stride=row_tiles), :].T.astype(dtype))
            for c in range(whole, cols):
                column = w_ref[pl.ds(c * row_tiles + a, 1), :]
                upright = jnp.sum(jnp.where(diagonal, column, 0.0), axis=1, keepdims=True)
                out_ref[a * LANES:(a + 1) * LANES, c:c + 1] = upright.astype(dtype)

    vm = pl.BlockSpec(memory_space=pltpu.VMEM)
    return pl.pallas_call(
        body, name=name,
        in_specs=[vm], out_specs=vm,
        out_shape=jax.ShapeDtypeStruct((rows, cols), dtype),
        compiler_params=pltpu.CompilerParams(vmem_limit_bytes=VMEM_LIMIT_BYTES),
    )(w_t)


def _adam_columns_call(name, parts, w_t, m_t, v_t):
    n_slots, rows, cols = parts.shape
    row_tiles = rows // LANES
    cols_pad = -(-cols // LANES) * LANES

    def body(p_ref, w_ref, m_ref, v_ref, *out_refs):
        for a in range(row_tiles):
            g = p_ref[0, a * LANES:(a + 1) * LANES, :].astype(F32)
            for s in range(1, n_slots):
                g = g + p_ref[s, a * LANES:(a + 1) * LANES, :].astype(F32)
            g = jnp.concatenate([g, jnp.zeros((LANES, cols_pad - cols), F32)], axis=1).T[:cols]
            column_rows = pl.ds(a, cols, stride=row_tiles)
            results = (g,) + _adamw(g, w_ref[column_rows, :], m_ref[column_rows, :], v_ref[column_rows, :])
            for out_ref, val in zip(out_refs, results):
                out_ref[column_rows, :] = val

    vm = pl.BlockSpec(memory_space=pltpu.VMEM)
    return pl.pallas_call(
        body, name=name,
        in_specs=[vm] * 4, out_specs=[vm] * 4,
        out_shape=[jax.ShapeDtypeStruct(w_t.shape, F32)] * 4,
        compiler_params=pltpu.CompilerParams(vmem_limit_bytes=VMEM_LIMIT_BYTES),
    )(parts, w_t, m_t, v_t)


N_PIECES = 8
PIECE = 512
SHARD_COLS = 513
SHARD_PAD = 640
RELAYOUT_ROWS = 256


def _from_shards_call(shards):
    _, d, _ = shards.shape
    tr = RELAYOUT_ROWS

    def body(p_ref, m_ref, s_ref):
        lane = lax.broadcasted_iota(jnp.int32, (tr, SHARD_PAD), 1)
        pad = jnp.zeros((tr, SHARD_PAD - SHARD_COLS), p_ref.dtype)
        sh = [jnp.concatenate([p_ref[s], pad], axis=1) for s in range(N_DEV)]
        for p in range(N_PIECES):
            y = sh[p] if p == 0 else pltpu.roll(sh[p], p, axis=1)
            if p > 0:
                y = jnp.where(lane < p, pltpu.roll(sh[p - 1], SHARD_PAD - (SHARD_COLS - p), axis=1), y)
            m_ref[:, p * PIECE:(p + 1) * PIECE] = y[:, :PIECE].astype(m_ref.dtype)
        first_gate = N_PIECES * PIECE - (N_DEV - 1) * SHARD_COLS
        s_ref[...] = pltpu.roll(sh[N_DEV - 1], SHARD_PAD - first_gate, axis=1)[:, :LANES].astype(s_ref.dtype)

    return pl.pallas_call(
        body, name="w_in_from_shards",
        grid=(d // tr,),
        in_specs=[pl.BlockSpec((N_DEV, tr, SHARD_COLS), lambda i: (0, i, 0))],
        out_specs=[pl.BlockSpec((tr, N_PIECES * PIECE), lambda i: (i, 0)), pl.BlockSpec((tr, LANES), lambda i: (i, 0))],
        out_shape=[jax.ShapeDtypeStruct((d, N_PIECES * PIECE), shards.dtype),
                   jax.ShapeDtypeStruct((d, LANES), shards.dtype)],
        compiler_params=_params("arbitrary"),
    )(shards)


def _adamw(g, w, m, v):
    m_new = ADAM_B1 * m + (1.0 - ADAM_B1) * g
    v_new = ADAM_B2 * v + (1.0 - ADAM_B2) * (g * g)
    m_hat = m_new / (1.0 - ADAM_B1 ** ADAM_STEP)
    v_hat = v_new / (1.0 - ADAM_B2 ** ADAM_STEP)
    return -ADAM_LR * (m_hat / (jnp.sqrt(v_hat) + ADAM_EPS) + ADAM_WD * w), m_new, v_new


def _adam_small_call(parts, ws, ms, vs):
    n = len(ws)
    n_slots = parts.shape[0]

    def body(*refs):
        p_ref = refs[0]
        w_refs, m_refs, v_refs = refs[1:1 + n], refs[1 + n:1 + 2 * n], refs[1 + 2 * n:1 + 3 * n]
        loss_ref = refs[1 + 3 * n]
        outs = refs[2 + 3 * n:]
        g_all = p_ref[0]
        for s in range(1, n_slots):
            g_all = g_all + p_ref[s]
        loss_ref[...] = g_all[n:n + 1, 0:1]
        for r in range(n):
            size = w_refs[r].shape[1]
            g = g_all[r:r + 1, :size]
            delta, m_new, v_new = _adamw(g, w_refs[r][...], m_refs[r][...], v_refs[r][...])
            for kind, val in enumerate((g, delta, m_new, v_new)):
                outs[kind * n + r][...] = val

    vm = pl.BlockSpec(memory_space=pltpu.VMEM)
    shapes = [jax.ShapeDtypeStruct(w.shape, F32) for w in ws]
    return pl.pallas_call(
        body, name="adam_small",
        in_specs=[vm] * (1 + 3 * n), out_specs=[vm] * (1 + 4 * n),
        out_shape=[jax.ShapeDtypeStruct((1, 1), F32)] + shapes * 4,
    )(parts, *ws, *ms, *vs)


_SMALL_ROWS = ("norm1_w", "final_norm_w", "sb_norm_w", "gdn_norm_w", "gdn_A_log", "gdn_dt_bias", "loss")


def _pack_small(vals, width):
    rows = [jnp.pad(a.reshape(1, -1).astype(F32), ((0, 0), (0, width - a.size))) for a in vals]
    rows += [jnp.zeros((1, width), F32)] * (8 - len(rows))
    return jnp.concatenate(rows, axis=0)


def _device_step(x2d, tgt, w_main, w_small, w_out_full, conv_full, norm1_w, sb_norm_w, gdn_A_log, gdn_dt_bias,
                 gdn_norm_w, final_norm_w, distributed=False):
    t_len, d = x2d.shape
    n_chunks = t_len // CHUNK
    w_main, w_small, w_out_full = (a.astype(MXU_DTYPE) for a in (w_main, w_small, w_out_full))
    w_small_t = w_small[:, :2 * GDN_HEADS].T

    pad_lanes = lambda a, lo: jnp.pad(a.reshape(1, -1), ((0, 0), (lo, LANES - lo - a.size)))
    alog_l, dtb_l = pad_lanes(gdn_A_log, GDN_HEADS), pad_lanes(gdn_dt_bias, GDN_HEADS)
    alog_c, dtb_c = alog_l[:, :8].T, dtb_l[:, :8].T
    sbw = jnp.tile(sb_norm_w, (1, 512 // SB_HEAD_DIM))
    gdw = jnp.tile(gdn_norm_w, (1, 512 // GDN_HEAD_DIM))
    fw = final_norm_w.reshape(1, d)

    if distributed:
        proj_cols, proj_gates, ps, pst, h_t, r1, w_out_g, conv_g = _inproj_call(
            x2d, norm1_w, w_main, w_small, w_small_t, gather=(w_out_full, conv_full))
        w_out_full = w_out_g.reshape(d, d)
        conv_full = conv_g.transpose(1, 0, 2).reshape(CONV_WIDTH, N_DEV * conv_g.shape[2])
    else:
        proj_cols, proj_gates, ps, pst, h_t, r1 = _inproj_call(x2d, norm1_w, w_main, w_small, w_small_t)
    o_sb, sp_total, sb_blocks_run = _sb_fwd_call(proj_cols, t_len)
    gact = _gdn_prep_call(proj_cols, conv_full, t_len, after=sp_total)
    beta_l, gcol_l, grow = _gdn_gates_call(ps, pst, alog_l, dtb_l, alog_c, dtb_c, t_len)
    gam_r = grow[GDN_HEADS:2 * GDN_HEADS].reshape(GDN_HEADS, n_chunks, 1, CHUNK)
    o_gd, *gdn_saved = _gdn_fwd_call(gact, beta_l, gcol_l, gam_r, t_len)

    (dx2, d_osb, d_ogd, dproj8, loss_p, g_fw, g_sbw, g_gdw, g_wout) = _post_call(
        o_sb, o_gd, proj_gates, x2d, tgt, w_out_full, sbw, gdw, fw)

    dproj8 = _sb_bwd_call(proj_cols, sp_total, sb_blocks_run, d_osb, dproj8, t_len)
    if distributed:
        d_gact3, d_gates, g_wout = _gdn_bwd_call(gact, beta_l, gcol_l, gam_r, gdn_saved, d_ogd, t_len,
                                                 scatter=(g_wout.reshape(N_DEV, d // N_DEV, d),))
    else:
        d_gact3, d_gates = _gdn_bwd_call(gact, beta_l, gcol_l, gam_r, gdn_saved, d_ogd, t_len)
    dproj8, g_conv = _gdn_prep_bwd_call(proj_cols, conv_full, d_gact3, dproj8, t_len)
    dsmall, g_alog, g_dtb = _gdn_gates_bwd_call(ps, alog_l, dtb_l, d_gates, t_len)

    if distributed:
        chip_partials = _gw_in_shards_call(h_t, dproj8, dsmall, WIRE_DTYPE)
        fold = lambda a, group: a.reshape(-1, group).sum(axis=0)
        small_g = _pack_small([jnp.zeros((d,), F32), g_fw, fold(g_sbw, SB_HEAD_DIM), fold(g_gdw, GDN_HEAD_DIM),
                               g_alog[0, GDN_HEADS:2 * GDN_HEADS], g_dtb[0, GDN_HEADS:2 * GDN_HEADS],
                               loss_p[0, :1]], d)
        conv_cols = g_conv.shape[1] // N_DEV
        g_conv_parts = g_conv.reshape(CONV_WIDTH, N_DEV, conv_cols).transpose(1, 0, 2)
        grad_x, _, g_w_in, p_small, p_conv = _dx_call(dproj8, dsmall, w_main, w_small, x2d, r1, dx2, norm1_w,
                                                      chip_scatter=(chip_partials,),
                                                      peer_scatter=(small_g, g_conv_parts))
        return grad_x, g_w_in, g_wout, p_small, p_conv
    else:
        grad_x, g_n1 = _dx_call(dproj8, dsmall, w_main, w_small, x2d, r1, dx2, norm1_w)
        g_w_in = (_gw_in_call(h_t, dproj8), _gw_small_call(h_t, dsmall))
    return (loss_p, grad_x, g_n1, g_w_in, g_sbw, g_conv, g_alog, g_dtb, g_gdw, g_wout, g_fw)


def kernel(x, norm1_w, w_in, sb_norm_w, gdn_conv_w, gdn_A_log, gdn_dt_bias, gdn_norm_w, w_out, final_norm_w, loss_target, m_norm1_w, m_w_in, m_sb_norm_w, m_gdn_conv_w, m_gdn_A_log, m_gdn_dt_bias, m_gdn_norm_w, m_w_out, m_final_norm_w, v_norm1_w, v_w_in, v_sb_norm_w, v_gdn_conv_w, v_gdn_A_log, v_gdn_dt_bias, v_gdn_norm_w, v_w_out, v_final_norm_w):
    d = x.shape[2]
    shard_cols = w_in.shape[2]

    columns = lambda a: a.transpose(2, 0, 1).reshape(shard_cols * d // LANES, LANES)
    from_columns = lambda a: a.reshape(shard_cols, d // LANES, LANES).transpose(1, 2, 0).reshape(1, d, shard_cols)
    (w_in_g,) = _gather_call("gather_weights", [_columns_to_rows_call("w_in_to_wire", columns(w_in), d, WIRE_DTYPE)])
    w_main, w_small = _from_shards_call(w_in_g)

    grad_x, p_w_in, p_wout, p_small, p_conv = _device_step(
        x[0], loss_target[0], w_main, w_small, w_out[0].astype(WIRE_DTYPE), gdn_conv_w[0], norm1_w, sb_norm_w,
        gdn_A_log, gdn_dt_bias, gdn_norm_w, final_norm_w, distributed=True)

    r_w_in = [from_columns(a) for a in _adam_columns_call("adam_w_in", p_w_in, columns(w_in), columns(m_w_in),
                                                          columns(v_w_in))]
    r_wout = _adam_call("adam_w_out", p_wout, w_out[0], m_w_out[0], v_w_out[0], d // N_DEV)
    r_conv = _adam_call("adam_conv", p_conv, gdn_conv_w[0], m_gdn_conv_w[0], v_gdn_conv_w[0], CONV_WIDTH)

    row = lambda a: a.reshape(1, -1)
    n_small = len(_SMALL_ROWS) - 1
    r_small = _adam_small_call(
        p_small,
        [norm1_w, row(final_norm_w), sb_norm_w, gdn_norm_w, gdn_A_log, gdn_dt_bias],
        [m_norm1_w, row(m_final_norm_w), m_sb_norm_w, m_gdn_norm_w, m_gdn_A_log, m_gdn_dt_bias],
        [v_norm1_w, row(v_final_norm_w), v_sb_norm_w, v_gdn_norm_w, v_gdn_A_log, v_gdn_dt_bias])

    def small_out(kind, name):
        out = r_small[1 + kind * n_small + _SMALL_ROWS.index(name)]
        return out.reshape(final_norm_w.shape) if name == "final_norm_w" else out

    def outputs(kind):
        return (small_out(kind, "norm1_w"), r_w_in[kind], small_out(kind, "sb_norm_w"), r_conv[kind][None],
                small_out(kind, "gdn_A_log"), small_out(kind, "gdn_dt_bias"), small_out(kind, "gdn_norm_w"),
                r_wout[kind][None], small_out(kind, "final_norm_w"))

    return (r_small[0][0, 0], grad_x[None], *outputs(0), *outputs(1), *outputs(2), *outputs(3))
```

```python
import functools

import jax
import jax.numpy as jnp
from jax import lax
from jax.experimental import pallas as pl
from jax.experimental.pallas import tpu as pltpu

F32 = jnp.float32
MXU_DTYPE = jnp.bfloat16
WIRE_DTYPE = jnp.bfloat16
EXACT = lax.Precision.HIGHEST
EPS = 1e-6
N_DEV = 8
SB_HEAD_DIM = 64
GDN_HEAD_DIM = 128
GDN_HEADS = 4
GDN_CHUNKS_PER_STEP = 4
GDN_BWD_GROUP = 1
CHUNK = 64
CONV_WIDTH = 4
LANES = 128
SB_BLOCK = 128
SB_BQ = 256
VMEM_LIMIT_BYTES = 56 * 1024 * 1024

PIECE_COLS = 512
PROJ_PIECE_KINDS = ("heads", "heads", "heads", "gate", "heads", "heads", "heads", "gate")
SB_FIRST_BLOCK, GDN_FIRST_BLOCK = 0, 12

DPROJ_PIECE_OF_SLOT = (0, 1, 2, 4, 5, 6, 3, 7)
DPROJ_SB_SLOT, DPROJ_GDN_SLOT, DPROJ_GATE_SLOT = 0, 3, 6

ADAM_LR = 0.001
ADAM_B1 = 0.9
ADAM_B2 = 0.999
ADAM_EPS = 1e-08
ADAM_WD = 0.01
ADAM_STEP = 10

_NN = (((1,), (0,)), ((), ()))
_NT = (((1,), (1,)), ((), ()))
_TN = (((0,), (0,)), ((), ()))
_BNN = (((2,), (1,)), ((0,), (0,)))
_BNT = (((2,), (2,)), ((0,), (0,)))
_BTN = (((1,), (1,)), ((0,), (0,)))


def _mx(a, b):
    return jnp.dot(a, b, precision=EXACT, preferred_element_type=F32)


def _split(x):
    hi = x.astype(MXU_DTYPE)
    return hi, (x - hi.astype(F32)).astype(MXU_DTYPE)


def _m3_general(a, b, dims, right_low=True):
    ah, al = _split(a)
    bh, bl = _split(b)
    dot = lambda x, y: lax.dot_general(x, y, dims, preferred_element_type=F32)
    (contract, _), (batch, _) = dims
    free = [ax for ax in range(a.ndim) if ax not in contract and ax not in batch][0]
    m = a.shape[free]
    both = dot(jnp.concatenate([ah, al], axis=free), bh)
    out_axis = len(batch)
    hi_part = lax.slice_in_dim(both, 0, m, axis=out_axis)
    lo_part = lax.slice_in_dim(both, m, 2 * m, axis=out_axis)
    return hi_part + (dot(ah, bl) + lo_part if right_low else lo_part)


def _times_exact(a, b_exact, dims):
    ah, al = _split(a)
    (contract, _), (batch, _) = dims
    free = [ax for ax in range(a.ndim) if ax not in contract and ax not in batch][0]
    m = a.shape[free]
    both = lax.dot_general(jnp.concatenate([ah, al], axis=free), b_exact.astype(MXU_DTYPE), dims,
                           preferred_element_type=F32)
    out_axis = len(batch)
    return lax.slice_in_dim(both, 0, m, axis=out_axis) + lax.slice_in_dim(both, m, 2 * m, axis=out_axis)


def _exact_times(a_exact, b, dims):
    bh, bl = _split(b)
    n = b.shape[-1]
    both = lax.dot_general(a_exact.astype(MXU_DTYPE), jnp.concatenate([bh, bl], axis=-1), dims,
                           preferred_element_type=F32)
    return both[..., :n] + both[..., n:]


def _sigmoid(z):
    return 1.0 / (1.0 + jnp.exp(-z))


def _softplus(z):
    return jnp.maximum(z, 0.0) + jnp.log(1.0 + jnp.exp(-jnp.abs(z)))


def _params(*semantics):
    return pltpu.CompilerParams(dimension_semantics=semantics, vmem_limit_bytes=VMEM_LIMIT_BYTES)


def _inproj_call(x, norm_w, w_main, w_small, w_small_t, gather=(), tm=256):
    t_len, d = x.shape
    n = w_main.shape[1]
    ns = w_small.shape[1]
    nst = w_small_t.shape[0]
    ng = len(gather)
    steps = t_len // tm

    blocks_per_piece = PIECE_COLS // LANES
    n_gate_cols = PIECE_COLS * PROJ_PIECE_KINDS.count("gate")
    n_col_blocks = blocks_per_piece * PROJ_PIECE_KINDS.count("heads")

    def body(*refs):
        x_ref, nw_ref, wm_ref, ws_ref, wst_ref = refs[:5]
        cols_ref, pz_ref, ps_ref, pst_ref, ht_ref, r_ref = refs[5 + ng:11 + ng]
        copies = lambda: _direct_copies(refs[5:5 + ng], refs[11 + ng:11 + 2 * ng], *refs[11 + 2 * ng:], (False,) * ng)
        if ng:
            pl.when(pl.program_id(0) == 0)(lambda: _start_all(copies()))
        xv = x_ref[...]
        r = lax.rsqrt(jnp.mean(xv * xv, axis=-1, keepdims=True) + EPS)
        h = xv * r * nw_ref[...]
        hb = h.astype(MXU_DTYPE)
        n_block = n_gate = 0
        for piece, kind in enumerate(PROJ_PIECE_KINDS):
            out = jnp.dot(hb, wm_ref[:, piece * PIECE_COLS:(piece + 1) * PIECE_COLS], preferred_element_type=F32)
            if kind == "gate":
                pz_ref[:, n_gate * PIECE_COLS:(n_gate + 1) * PIECE_COLS] = out
                n_gate += 1
            else:
                for j in range(blocks_per_piece):
                    cols_ref[n_block + j] = out[:, j * LANES:(j + 1) * LANES]
                n_block += blocks_per_piece
        ps_ref[...] = jnp.dot(hb, ws_ref[...], preferred_element_type=F32)
        pst_ref[...] = lax.dot_general(wst_ref[...], hb, _NT, preferred_element_type=F32)
        ht_ref[...] = h.T.astype(MXU_DTYPE)
        r_ref[...] = r
        if ng:
            pl.when(pl.program_id(0) == steps - 1)(lambda: _wait_all(copies()))

    return pl.pallas_call(
        body, name="inproj",
        grid=(steps,),
        in_specs=[pl.BlockSpec((tm, d), lambda i: (i, 0)),
                  pl.BlockSpec((1, d), lambda i: (0, 0)),
                  pl.BlockSpec((d, n), lambda i: (0, 0)),
                  pl.BlockSpec((d, ns), lambda i: (0, 0)),
                  pl.BlockSpec((nst, d), lambda i: (0, 0))] + [_HBM] * ng,
        out_specs=[pl.BlockSpec((n_col_blocks, tm, LANES), lambda i: (0, i, 0)),
                   pl.BlockSpec((tm, n_gate_cols), lambda i: (i, 0)),
                   pl.BlockSpec((tm, ns), lambda i: (i, 0)),
                   pl.BlockSpec((nst, tm), lambda i: (0, i)),
                   pl.BlockSpec((d, tm), lambda i: (0, i)),
                   pl.BlockSpec((tm, 1), lambda i: (i, 0))] + [_HBM] * ng,
        out_shape=[jax.ShapeDtypeStruct((n_col_blocks, t_len, LANES), F32),
                   jax.ShapeDtypeStruct((t_len, n_gate_cols), F32),
                   jax.ShapeDtypeStruct((t_len, ns), F32),
                   jax.ShapeDtypeStruct((nst, t_len), F32),
                   jax.ShapeDtypeStruct((d, t_len), MXU_DTYPE),
                   jax.ShapeDtypeStruct((t_len, 1), F32)] + _direct_out_shapes(gather, (False,) * ng),
        scratch_shapes=_direct_semaphores(ng) if ng else [],
        compiler_params=_params("arbitrary"),
    )(x, norm_w, w_main, w_small, w_small_t, *gather)


def _running_sum_mm(x, tri):
    hi = x.astype(MXU_DTYPE)
    lo = (x - hi.astype(F32)).astype(MXU_DTYPE)
    return jnp.dot(hi, tri, preferred_element_type=F32) + jnp.dot(lo, tri, preferred_element_type=F32)


def _col_block(t_len, first):
    return pl.BlockSpec((1, t_len, LANES), lambda p: (first + p, 0, 0))


def _sb_iotas():
    row_i = lax.broadcasted_iota(jnp.int32, (SB_BQ, SB_BLOCK), 0)
    col_i = lax.broadcasted_iota(jnp.int32, (SB_BQ, SB_BLOCK), 1)
    sq_r = lax.broadcasted_iota(jnp.int32, (SB_BLOCK, SB_BLOCK), 0)
    sq_c = lax.broadcasted_iota(jnp.int32, (SB_BLOCK, SB_BLOCK), 1)
    return row_i, col_i, sq_r, sq_c


SB_DIAG_BLOCKS = SB_BQ // SB_BLOCK
SB_EXP_FLOOR = -110.0


def _sb_keys_descending(qi, tile, carry, z_bounds, n_heads, has_free):
    group = SB_DIAG_BLOCKS
    n_free = group * qi
    diag = list(range(group - 1, -1, -1))
    carry = tile([n_free + j for j in diag], [True] * group, carry, [j * SB_BLOCK for j in diag])

    def largest_exponent(c):
        worst = jnp.max(z_bounds[0] - c[1])
        for h in range(1, n_heads):
            worst = jnp.maximum(worst, jnp.max(z_bounds[h] - c[1 + h]))
        return worst

    always = group if has_free else 0

    def cond(state):
        return (state[0] < n_free) & ((state[1] > SB_EXP_FLOOR) | (state[0] < always))

    def body(state):
        first = n_free - 1 - state[0]
        c = tile([first - j for j in range(group)], [False] * group, state[2:])
        return (state[0] + group, largest_exponent(c), *c)

    out = lax.while_loop(cond, body, (jnp.int32(0), largest_exponent(carry), *carry))
    return out[2:], out[0]


def _sb_keys_ascending(qi, n_run, tile, carry, has_free):
    group = SB_DIAG_BLOCKS
    n_free = group * qi
    diag = list(range(group))
    kjs, los, masked = [n_free + j for j in diag], [j * SB_BLOCK for j in diag], [True] * group
    if has_free:
        early = lambda s: [n_free - n_run + group * s + j for j in range(group)]
        carry = lax.fori_loop(0, n_run // group - 1, lambda s, c: tile(early(s), [False] * group, c), carry)
        kjs, los, masked = [n_free - group + j for j in range(group)] + kjs, [0] * group + los, [False] * group + masked
    return tile(kjs, masked, carry, los)


def _sb_fwd_call(cols, t_len):
    nq = t_len // SB_BQ
    scale = float(SB_HEAD_DIM) ** -0.5
    n_pairs = 512 // LANES
    per_pair = LANES // SB_HEAD_DIM

    def body(q_blk, k_blk, v_blk, o_blk, st_ref, nrun_ref):
        q_ref, k_ref, v_ref, o_ref = q_blk.at[0], k_blk.at[0], v_blk.at[0], o_blk.at[0]
        lane = lax.broadcasted_iota(jnp.int32, (1, LANES), 1)
        row_i, col_i, sq_r, sq_c = _sb_iotas()
        ge = (sq_r >= sq_c).astype(MXU_DTYPE)
        hms = [((lane // SB_HEAD_DIM) == hh).astype(F32) for hh in range(per_pair)]
        k_sq = k_ref[...] * k_ref[...]
        k_norms = [jnp.sqrt(jnp.max(jnp.sum(k_sq * hm, axis=-1, keepdims=True))) * (1.02 * scale) for hm in hms]

        def q_block(qi, has_free):
            r0 = qi * SB_BQ if isinstance(qi, int) else pl.multiple_of(qi * SB_BQ, SB_BQ)
            rows = pl.ds(r0, SB_BQ)
            q_all = q_ref[rows, :]
            qms = [(q_all * (hm * scale)).astype(MXU_DTYPE) for hm in hms]
            z_bounds = [jnp.sqrt(jnp.sum(q_all * q_all * hm, axis=-1, keepdims=True)) * kn
                        for hm, kn in zip(hms, k_norms)]

            def tile(kjs, masked, kc, los=None):
                heads = range(per_pair)
                los = los or [0] * len(kjs)
                pairs = [(t, h) for t in range(len(kjs)) for h in heads]
                add_rows = lambda full, lo, part: full + part if lo == 0 else jnp.concatenate(
                    [full[:lo], full[lo:] + part], axis=0)
                acc, cs = kc[0], list(kc[1:])
                s0s = [kj * SB_BLOCK if isinstance(kj, int) else pl.multiple_of(kj * SB_BLOCK, SB_BLOCK) for kj in kjs]
                kbs = [k_ref[pl.ds(s0, SB_BLOCK), :].astype(MXU_DTYPE) for s0 in s0s]
                v_alls = [v_ref[pl.ds(s0, SB_BLOCK), :] for s0 in s0s]
                vms = {(t, h): (v_alls[t] * hms[h]).astype(MXU_DTYPE) for t, h in pairs}
                zs = {(t, h): lax.dot_general(qms[h][los[t]:], kbs[t], _NT, preferred_element_type=F32)
                      for t, h in pairs}
                masks = [(col_i[lo:] + s0) < (row_i[lo:] + r0) if m else None for m, lo, s0 in zip(masked, los, s0s)]
                keep = lambda t, a: a if masks[t] is None else jnp.where(masks[t], a, 0.0)
                sps = {(t, h): keep(t, _softplus(zs[t, h])) for t, h in pairs}
                sums = {p: _running_sum_mm(sps[p], ge) for p in pairs}
                mass = {}
                for t, h in pairs:
                    mass[t, h] = cs[h] if t == 0 else add_rows(
                        mass[t - 1, h], los[t - 1], jnp.sum(sps[t - 1, h], axis=-1, keepdims=True))
                ws = {(t, h): keep(t, jnp.exp(zs[t, h] - (sums[t, h] + mass[t, h][los[t]:]))) for t, h in pairs}
                for t, h in pairs:
                    acc = add_rows(acc, los[t], jnp.dot(ws[t, h].astype(MXU_DTYPE), vms[t, h],
                                                        preferred_element_type=F32))
                last = len(kjs) - 1
                cs = [add_rows(mass[last, h], los[last], jnp.sum(sps[last, h], axis=-1, keepdims=True)) for h in heads]
                return (acc, *cs)

            zero_col = jnp.zeros((SB_BQ, 1), F32)
            out, n_run = _sb_keys_descending(
                qi, tile, (jnp.zeros((SB_BQ, LANES), F32),) + (zero_col,) * per_pair, z_bounds, per_pair, has_free)
            o_ref[rows, :] = out[0]
            masses = jnp.zeros((SB_BQ, LANES), F32)
            for hh in range(per_pair):
                masses = jnp.where(lane == hh, out[1 + hh], masses)
            st_ref[rows, :] = masses
            nrun_ref[pl.program_id(0), qi] = n_run

        q_block(0, False)
        lax.fori_loop(1, nq, lambda qi, carry: (q_block(qi, True), carry)[1], 0)

    return pl.pallas_call(
        body, name="sb_fwd",
        grid=(n_pairs,),
        in_specs=[_col_block(t_len, SB_FIRST_BLOCK), _col_block(t_len, SB_FIRST_BLOCK + n_pairs),
                  _col_block(t_len, SB_FIRST_BLOCK + 2 * n_pairs)],
        out_specs=[_col_block(t_len, 0),
                   pl.BlockSpec((t_len, LANES), lambda p: (0, p)),
                   pl.BlockSpec(memory_space=pltpu.SMEM)],
        out_shape=[jax.ShapeDtypeStruct((n_pairs, t_len, LANES), F32),
                   jax.ShapeDtypeStruct((t_len, n_pairs * LANES), F32),
                   jax.ShapeDtypeStruct((n_pairs, nq), jnp.int32)],
        compiler_params=_params("arbitrary"),
    )(cols, cols, cols)


def _sb_bwd_call(cols, sp_total, n_run_all, d_o, dproj, t_len):
    nq = t_len // SB_BQ
    scale = float(SB_HEAD_DIM) ** -0.5
    n_pairs = 512 // LANES
    per_pair = LANES // SB_HEAD_DIM

    def body(q_blk, k_blk, v_blk, st_ref, nrun_ref, do_blk, dproj_in_ref, d_ref):
        q_ref, k_ref, v_ref, do_ref = q_blk.at[0], k_blk.at[0], v_blk.at[0], do_blk.at[0]
        lane = lax.broadcasted_iota(jnp.int32, (1, LANES), 1)
        row_i, col_i, sq_r, sq_c = _sb_iotas()
        lt = (sq_r < sq_c).astype(MXU_DTYPE)
        le = (sq_r <= sq_c).astype(MXU_DTYPE)
        hms = [((lane // SB_HEAD_DIM) == hh).astype(F32) for hh in range(per_pair)]
        d_ref[1] = jnp.zeros((t_len, LANES), F32)
        d_ref[2] = jnp.zeros((t_len, LANES), F32)

        def q_block(qi, has_free):
            r0 = qi * SB_BQ if isinstance(qi, int) else pl.multiple_of(qi * SB_BQ, SB_BQ)
            rows = pl.ds(r0, SB_BQ)
            q_all, do_all = q_ref[rows, :], do_ref[rows, :]
            qms = [(q_all * (hm * scale)).astype(MXU_DTYPE) for hm in hms]
            doms = [(do_all * hm).astype(MXU_DTYPE) for hm in hms]
            masses = st_ref[rows, :]
            totals = [jnp.sum(jnp.where(lane == hh, masses, 0.0), axis=-1, keepdims=True) for hh in range(per_pair)]

            def tile(kjs, masked, kc, los=None):
                heads = range(per_pair)
                los = los or [0] * len(kjs)
                pairs = [(t, h) for t in range(len(kjs)) for h in heads]
                add_rows = lambda full, lo, part: full + part if lo == 0 else jnp.concatenate(
                    [full[:lo], full[lo:] + part], axis=0)
                rsum = lambda a: jnp.sum(a, axis=-1, keepdims=True)
                dq, cls, gls = kc[0], list(kc[1:1 + per_pair]), list(kc[1 + per_pair:])
                s0s = [kj * SB_BLOCK if isinstance(kj, int) else pl.multiple_of(kj * SB_BLOCK, SB_BLOCK) for kj in kjs]
                k_alls = [k_ref[pl.ds(s0, SB_BLOCK), :] for s0 in s0s]
                v_alls = [v_ref[pl.ds(s0, SB_BLOCK), :] for s0 in s0s]
                kbs = [k_all.astype(MXU_DTYPE) for k_all in k_alls]
                vms = {(t, h): (v_alls[t] * hms[h]).astype(MXU_DTYPE) for t, h in pairs}
                kms = {(t, h): (k_alls[t] * (hms[h] * scale)).astype(MXU_DTYPE) for t, h in pairs}
                q_live = {(t, h): qms[h][los[t]:] for t, h in pairs}
                do_live = {(t, h): doms[h][los[t]:] for t, h in pairs}
                zs = {p: lax.dot_general(q_live[p], kbs[p[0]], _NT, preferred_element_type=F32) for p in pairs}
                das = {p: lax.dot_general(do_live[p], vms[p], _NT, preferred_element_type=F32) for p in pairs}
                masks = [(col_i[lo:] + s0) < (row_i[lo:] + r0) if m else None for m, lo, s0 in zip(masked, los, s0s)]
                keep = lambda t, a: a if masks[t] is None else jnp.where(masks[t], a, 0.0)
                sp_alls = {p: _softplus(zs[p]) for p in pairs}
                sps = {(t, h): keep(t, sp_alls[t, h]) for t, h in pairs}
                lefts = {p: _running_sum_mm(sps[p], lt) for p in pairs}
                cl = {}
                for t, h in pairs:
                    cl[t, h] = cls[h] if t == 0 else add_rows(cl[t - 1, h], los[t - 1], rsum(sps[t - 1, h]))
                ws = {(t, h): keep(t, jnp.exp(zs[t, h] - ((totals[h] - cl[t, h])[los[t]:] - lefts[t, h])))
                      for t, h in pairs}
                gs = {p: das[p] * ws[p] for p in pairs}
                g_sums = {p: _running_sum_mm(gs[p], le) for p in pairs}
                gl = {}
                for t, h in pairs:
                    gl[t, h] = gls[h] if t == 0 else add_rows(gl[t - 1, h], los[t - 1], rsum(gs[t - 1, h]))
                dzs = {(t, h): keep(t, gs[t, h] - jnp.exp(zs[t, h] - sp_alls[t, h]) * (gl[t, h][los[t]:] + g_sums[t, h])
                               ).astype(MXU_DTYPE) for t, h in pairs}
                for t in range(len(kjs)):
                    dk_t = jnp.zeros((SB_BLOCK, LANES), F32)
                    dv_t = jnp.zeros((SB_BLOCK, LANES), F32)
                    for h in heads:
                        dq = add_rows(dq, los[t], jnp.dot(dzs[t, h], kms[t, h], preferred_element_type=F32))
                        dk_t = dk_t + lax.dot_general(dzs[t, h], q_live[t, h], _TN, preferred_element_type=F32)
                        dv_t = dv_t + lax.dot_general(ws[t, h].astype(MXU_DTYPE), do_live[t, h], _TN,
                                                      preferred_element_type=F32)
                    d_ref[1, pl.ds(s0s[t], SB_BLOCK), :] += dk_t
                    d_ref[2, pl.ds(s0s[t], SB_BLOCK), :] += dv_t
                last = len(kjs) - 1
                cls = [add_rows(cl[last, h], los[last], rsum(sps[last, h])) for h in heads]
                gls = [add_rows(gl[last, h], los[last], rsum(gs[last, h])) for h in heads]
                return (dq, *cls, *gls)

            zero_col = jnp.zeros((SB_BQ, 1), F32)
            out = _sb_keys_ascending(qi, nrun_ref[pl.program_id(0), qi], tile,
                                     (jnp.zeros((SB_BQ, LANES), F32),) + (zero_col,) * (2 * per_pair), has_free)
            d_ref[0, rows, :] = out[0]

        q_block(0, False)
        lax.fori_loop(1, nq, lambda qi, carry: (q_block(qi, True), carry)[1], 0)

    return pl.pallas_call(
        body, name="sb_bwd",
        grid=(n_pairs,),
        in_specs=[_col_block(t_len, SB_FIRST_BLOCK), _col_block(t_len, SB_FIRST_BLOCK + n_pairs),
                  _col_block(t_len, SB_FIRST_BLOCK + 2 * n_pairs),
                  pl.BlockSpec((t_len, LANES), lambda p: (0, p)),
                  pl.BlockSpec(memory_space=pltpu.SMEM), _col_block(t_len, 0), _HBM],
        out_specs=pl.BlockSpec((3, t_len, LANES), lambda p: (DPROJ_SB_SLOT // 3, 0, p)),
        out_shape=jax.ShapeDtypeStruct(dproj.shape, dproj.dtype),
        input_output_aliases={6: 0},
        compiler_params=_params("arbitrary"),
    )(cols, cols, cols, sp_total, n_run_all, d_o, dproj)


def _conv_taps(xin, rows, t_len):
    taps = []
    for i in range(CONV_WIDTH):
        shift = CONV_WIDTH - 1 - i
        if shift == 0:
            taps.append(xin)
        else:
            taps.append(jnp.where(rows >= shift, pltpu.roll(xin, shift, axis=0), 0.0))
    return taps


def _gdn_prep_body_common(x_ref, w_ref, t_len):
    j = pl.program_id(0)
    xin = x_ref[...]
    rows = lax.broadcasted_iota(jnp.int32, (t_len, LANES), 0)
    taps = _conv_taps(xin, rows, t_len)
    pre = taps[0] * w_ref[0:1, :]
    for i in range(1, CONV_WIDTH):
        pre = pre + taps[i] * w_ref[i:i + 1, :]
    sg = _sigmoid(pre)
    act = pre * sg
    is_qk = j < 2 * GDN_HEADS
    nrm = jnp.where(is_qk, lax.rsqrt(jnp.sum(act * act, axis=-1, keepdims=True) + EPS), 1.0)
    sc = jnp.where(j < GDN_HEADS, float(GDN_HEAD_DIM) ** -0.5, 1.0)
    return j, rows, taps, pre, sg, act, is_qk, nrm, sc


def _gdn_prep_call(cols, conv_w, t_len, after):
    def body(x_blk, w_ref, after_ref, out_ref):
        _, _, _, _, _, act, _, nrm, sc = _gdn_prep_body_common(x_blk.at[0], w_ref, t_len)
        out_ref[...] = act * nrm * sc

    return pl.pallas_call(
        body, name="gdn_prep",
        grid=(3 * GDN_HEADS,),
        in_specs=[_col_block(t_len, GDN_FIRST_BLOCK),
                  pl.BlockSpec((CONV_WIDTH, LANES), lambda j: (0, j)),
                  pl.BlockSpec(memory_space=pl.ANY)],
        out_specs=pl.BlockSpec((t_len, LANES), lambda j: (0, j)),
        out_shape=jax.ShapeDtypeStruct((t_len, 3 * 512), F32),
        compiler_params=_params("arbitrary"),
    )(cols, conv_w, after)


def _gdn_prep_bwd_call(cols, conv_w, d_act3, dproj, t_len):
    def body(x_blk, w_ref, d_ref, dproj_in_ref, dx_ref, dw_ref):
        _, rows, taps, pre, sg, act, is_qk, nrm, sc = _gdn_prep_body_common(x_blk.at[0], w_ref, t_len)
        d_out = d_ref[0]
        dn = d_out * sc
        d_norm = nrm * dn - act * (nrm * nrm * nrm) * jnp.sum(dn * act, axis=-1, keepdims=True)
        d_act = jnp.where(is_qk, d_norm, d_out)
        d_pre = d_act * sg * (1.0 + pre * (1.0 - sg))
        dx = d_pre * w_ref[CONV_WIDTH - 1:CONV_WIDTH, :]
        dw_ref[CONV_WIDTH - 1:CONV_WIDTH, :] = jnp.sum(d_pre * taps[CONV_WIDTH - 1], axis=0, keepdims=True)
        for i in range(CONV_WIDTH - 1):
            shift = CONV_WIDTH - 1 - i
            up = jnp.where(rows < t_len - shift, pltpu.roll(d_pre, t_len - shift, axis=0), 0.0)
            dx = dx + up * w_ref[i:i + 1, :]
            dw_ref[i:i + 1, :] = jnp.sum(d_pre * taps[i], axis=0, keepdims=True)
        dx_ref[0] = dx

    return pl.pallas_call(
        body, name="gdn_prep_bwd",
        grid=(3 * GDN_HEADS,),
        in_specs=[_col_block(t_len, GDN_FIRST_BLOCK),
                  pl.BlockSpec((CONV_WIDTH, LANES), lambda j: (0, j)),
                  pl.BlockSpec((1, t_len, LANES), lambda j: (j // GDN_HEADS, 0, j % GDN_HEADS)), _HBM],
        out_specs=[pl.BlockSpec((1, t_len, LANES), lambda j: (DPROJ_GDN_SLOT + j // GDN_HEADS, 0, j % GDN_HEADS)),
                   pl.BlockSpec((CONV_WIDTH, LANES), lambda j: (0, j))],
        out_shape=[jax.ShapeDtypeStruct(dproj.shape, dproj.dtype),
                   jax.ShapeDtypeStruct((CONV_WIDTH, 3 * 512), F32)],
        input_output_aliases={3: 0},
        compiler_params=_params("arbitrary"),
    )(cols, conv_w, d_act3, dproj)


def _chunk_cumsum_matrix():
    r = lax.broadcasted_iota(jnp.int32, (LANES, LANES), 0)
    c = lax.broadcasted_iota(jnp.int32, (LANES, LANES), 1)
    return ((r <= c) & ((r // CHUNK) == (c // CHUNK))).astype(F32)


def _gdn_gates_call(ps, pst, alog_l, dtb_l, alog_c, dtb_c, t_len):
    def body(ps_ref, pst_ref, al_ref, dl_ref, ac_ref, dc_ref, beta_ref, gcol_ref, grow_ref):
        upper = _chunk_cumsum_matrix()
        lower = upper.T
        psv = ps_ref[...]
        beta_ref[...] = _sigmoid(psv)
        g_l = -jnp.exp(al_ref[...]) * _softplus(psv + dl_ref[...])
        g_r = -jnp.exp(ac_ref[...]) * _softplus(pst_ref[...] + dc_ref[...])
        for w in range(t_len // LANES):
            sl = slice(w * LANES, (w + 1) * LANES)
            gcol_ref[sl, :] = _mx(lower, g_l[sl, :])
            grow_ref[:, sl] = _mx(g_r[:, sl], upper)

    vm = pl.BlockSpec(memory_space=pltpu.VMEM)
    return pl.pallas_call(
        body, name="gdn_gates",
        in_specs=[vm] * 6, out_specs=[vm] * 3,
        out_shape=[jax.ShapeDtypeStruct((t_len, LANES), F32),
                   jax.ShapeDtypeStruct((t_len, LANES), F32),
                   jax.ShapeDtypeStruct((8, t_len), F32)],
        compiler_params=pltpu.CompilerParams(vmem_limit_bytes=VMEM_LIMIT_BYTES),
    )(ps, pst, alog_l, dtb_l, alog_c, dtb_c)


def _gdn_gates_bwd_call(ps, alog_l, dtb_l, d_l, t_len):
    def body(ps_ref, al_ref, dl_ref, d_ref, dps_ref, gal_ref, gdt_ref):
        lane = lax.broadcasted_iota(jnp.int32, (1, LANES), 1)
        psv = ps_ref[...]
        dv = d_ref[...]
        beta = _sigmoid(psv)
        ea = jnp.exp(al_ref[...])
        arg = psv + dl_ref[...]
        g = -ea * _softplus(arg)
        d_a = dv * (-ea) * _sigmoid(arg)
        is_a = (lane >= GDN_HEADS) & (lane < 2 * GDN_HEADS)
        dps_ref[...] = jnp.where(lane < GDN_HEADS, dv * beta * (1.0 - beta), jnp.where(is_a, d_a, 0.0))
        gdt_ref[...] = jnp.where(is_a, jnp.sum(d_a, axis=0, keepdims=True), 0.0)
        gal_ref[...] = jnp.where(is_a, jnp.sum(dv * g, axis=0, keepdims=True), 0.0)

    vm = pl.BlockSpec(memory_space=pltpu.VMEM)
    return pl.pallas_call(
        body, name="gdn_gates_bwd",
        in_specs=[vm] * 4, out_specs=[vm] * 3,
        out_shape=[jax.ShapeDtypeStruct((t_len, LANES), F32),
                   jax.ShapeDtypeStruct((1, LANES), F32),
                   jax.ShapeDtypeStruct((1, LANES), F32)],
        compiler_params=pltpu.CompilerParams(vmem_limit_bytes=VMEM_LIMIT_BYTES),
    )(ps, alog_l, dtb_l, d_l)


def _bm(a, b, right_low=True):
    return _m3_general(a, b, _BNN, right_low)


def _bm_nt(a, b, right_low=True):
    return _m3_general(a, b, _BNT, right_low)


def _bm_tn(a, b, right_low=True):
    return _m3_general(a, b, _BTN, right_low)


def _heads_of(ref, rows):
    return jnp.stack([ref[rows, h * GDN_HEAD_DIM:(h + 1) * GDN_HEAD_DIM] for h in range(GDN_HEADS)])


def _chunk_terms(q_ref, k_ref, v_ref, b_ref, gc_ref, gr_ref, c, incl, strict, n=1, scores=True):
    r0 = c * CHUNK if isinstance(c, int) else pl.multiple_of(c * CHUNK, CHUNK)
    rows = pl.ds(r0, n * CHUNK)
    per_chunk = lambda x: x.reshape(GDN_HEADS * n, CHUNK, x.shape[-1])
    q, k, v = (per_chunk(_heads_of(ref, rows)) for ref in (q_ref, k_ref, v_ref))
    lane_ids = lax.broadcasted_iota(jnp.int32, (1, LANES), 1)
    pick = lambda slab, first: jnp.stack([jnp.sum(jnp.where(lane_ids == first + h, slab, 0.0), axis=-1, keepdims=True)
                                          for h in range(GDN_HEADS)])
    b = per_chunk(pick(b_ref[rows, :], 0))
    gc = per_chunk(pick(gc_ref[rows, :], GDN_HEADS))
    gr = gr_ref[:, c] if n == 1 else gr_ref[:, c:c + n].reshape(GDN_HEADS * n, 1, CHUNK)
    dm = jnp.where(incl, jnp.exp(jnp.where(incl, gc - gr, 0.0)), 0.0)
    kb = k * b
    vb = v * b
    e = jnp.exp(gc)
    a = p = None
    if scores:
        kk_qk = _bm_nt(jnp.concatenate([kb, q], axis=1), k)
        a = jnp.where(strict, kk_qk[:, :CHUNK] * dm, 0.0)
        p = jnp.where(incl, kk_qk[:, CHUNK:] * dm, 0.0)
    gl = gc[:, CHUNK - 1:CHUNK, :]
    eg = jnp.exp(gl - gc)
    return rows, q, k, v, b, gc, dm, kb, vb, e, a, p, gl, eg


def _unit_lower_inverse(a, eye):
    x = -a
    tm = eye + x
    xp = _bm(x, x)
    for _ in range(4):
        both = _bm(jnp.concatenate([xp, tm], axis=1), xp)
        tm = tm + both[:, CHUNK:]
        xp = both[:, :CHUNK]
    return tm + _bm(tm, xp)


def _gdn_specs(t_len, n_chunks, reverse):
    cps = GDN_CHUNKS_PER_STEP
    steps = n_chunks // cps
    at = (lambda g: steps - 1 - g) if reverse else (lambda g: g)
    rows_blk = lambda width, part=0: pl.BlockSpec((cps * CHUNK, width), lambda g: (at(g), part))
    gate_r = pl.BlockSpec((GDN_HEADS, cps, 1, CHUNK), lambda g: (0, at(g), 0, 0))
    per_chunk = lambda r, c: pl.BlockSpec((GDN_HEADS, cps, r, c), lambda g: (0, at(g), 0, 0))
    return cps, steps, rows_blk, gate_r, per_chunk


def _gdn_fwd_call(gact, beta_c, gam_c, gam_r, t_len):
    n_chunks = t_len // CHUNK
    dk = GDN_HEAD_DIM
    width = GDN_HEADS * dk
    cps, steps, rows_blk, gate_r, per_chunk = _gdn_specs(t_len, n_chunks, False)

    def body(q_ref, k_ref, v_ref, b_ref, gc_ref, gr_ref, o_ref, s_ref, t_ref, a_ref, p_ref, uw_ref, vn_ref, state_ref):
        row = lax.broadcasted_iota(jnp.int32, (CHUNK, CHUNK), 0)
        col = lax.broadcasted_iota(jnp.int32, (CHUNK, CHUNK), 1)
        incl, strict = row >= col, row > col
        eye = (row == col).astype(F32)

        @pl.when(pl.program_id(0) == 0)
        def _():
            state_ref[...] = jnp.zeros_like(state_ref)

        _, q, k, v, b, gc, dm, kb, vb, e, a, p, gl, eg = _chunk_terms(
            q_ref, k_ref, v_ref, b_ref, gc_ref, gr_ref, 0, incl, strict, cps)
        tm = _unit_lower_inverse(a, eye)
        uw = _bm(tm, jnp.concatenate([vb, kb * e], axis=2))
        w_qe = jnp.concatenate([uw[:, :, dk:], q * e], axis=1)
        u, kd, decay = uw[:, :, :dk], k * eg, jnp.exp(gl)
        per_chunk_block = lambda x: x.reshape(GDN_HEADS, cps, CHUNK, CHUNK)
        t_ref[...], a_ref[...], p_ref[...] = per_chunk_block(tm), per_chunk_block(a), per_chunk_block(p)
        uw_heads = uw.reshape(GDN_HEADS, cps * CHUNK, 2 * dk)
        for h in range(GDN_HEADS):
            uw_ref[:, h * 2 * dk:(h + 1) * 2 * dk] = uw_heads[h]

        of_chunk = lambda x, c: jnp.stack([x[h * cps + c] for h in range(GDN_HEADS)])
        s = state_ref[...]
        for c in range(cps):
            ws_qs = _bm(of_chunk(w_qe, c), s)
            vn = of_chunk(u, c) - ws_qs[:, :CHUNK]
            o = ws_qs[:, CHUNK:] + _bm(of_chunk(p, c), vn)
            for h in range(GDN_HEADS):
                o_ref[c * CHUNK:(c + 1) * CHUNK, h * dk:(h + 1) * dk] = o[h]
                vn_ref[c * CHUNK:(c + 1) * CHUNK, h * dk:(h + 1) * dk] = vn[h]
            s_ref[:, c] = s
            s = s * of_chunk(decay, c) + _bm_tn(of_chunk(kd, c), vn)
        state_ref[...] = s

    scores = jax.ShapeDtypeStruct((GDN_HEADS, n_chunks, CHUNK, CHUNK), F32)
    return pl.pallas_call(
        body, name="gdn_fwd",
        grid=(steps,),
        in_specs=[rows_blk(width, 0), rows_blk(width, 1), rows_blk(width, 2), rows_blk(LANES), rows_blk(LANES), gate_r],
        out_specs=[rows_blk(width), per_chunk(dk, dk), per_chunk(CHUNK, CHUNK), per_chunk(CHUNK, CHUNK),
                   per_chunk(CHUNK, CHUNK), rows_blk(2 * width), rows_blk(width)],
        out_shape=[jax.ShapeDtypeStruct((t_len, width), F32),
                   jax.ShapeDtypeStruct((GDN_HEADS, n_chunks, dk, dk), F32), scores, scores, scores,
                   jax.ShapeDtypeStruct((t_len, 2 * width), F32), jax.ShapeDtypeStruct((t_len, width), F32)],
        scratch_shapes=[pltpu.VMEM((GDN_HEADS, dk, dk), F32)],
        compiler_params=_params("arbitrary"),
    )(gact, gact, gact, beta_c, gam_c, gam_r)


def _gdn_bwd_call(gact, beta_c, gam_c, gam_r, saved, d_o, t_len, scatter=()):
    n_chunks = t_len // CHUNK
    dk = GDN_HEAD_DIM
    width = GDN_HEADS * dk
    cps, steps, rows_blk, gate_r, per_chunk = _gdn_specs(t_len, n_chunks, True)
    nx = len(scatter)
    n_in = 13

    def body(*refs):
        q_ref, k_ref, v_ref, b_ref, gc_ref, gr_ref = refs[:6]
        saved_refs, do_ref = refs[6:12], refs[12]
        d_ref, dgate_ref = refs[n_in + nx:n_in + 2 + nx]
        dstate_ref = refs[n_in + 2 + 2 * nx]
        copies = lambda: _direct_copies(refs[n_in:n_in + nx], refs[n_in + 2 + nx:n_in + 2 + 2 * nx],
                                        *refs[n_in + 3 + 2 * nx:], (True,) * nx)
        if nx:
            pl.when(pl.program_id(0) == 0)(lambda: _start_all(copies()))
        row = lax.broadcasted_iota(jnp.int32, (CHUNK, CHUNK), 0)
        col = lax.broadcasted_iota(jnp.int32, (CHUNK, CHUNK), 1)
        incl, strict = row >= col, row > col
        ng = GDN_BWD_GROUP
        nb = GDN_HEADS * ng
        upper = jnp.broadcast_to((row <= col).astype(F32), (nb, CHUNK, CHUNK))
        ones = jnp.ones((nb, CHUNK, LANES), F32)
        last_row = lax.broadcasted_iota(jnp.int32, (CHUNK, 1), 0) == CHUNK - 1
        lane_ids = lax.broadcasted_iota(jnp.int32, (1, LANES), 1)
        rsum = lambda m: jnp.sum(m, axis=-1, keepdims=True)
        total = lambda m: jnp.sum(rsum(m), axis=1, keepdims=True)
        of_chunk = lambda x, c: jnp.stack([x[h * ng + c] for h in range(GDN_HEADS)])

        @pl.when(pl.program_id(0) == 0)
        def _():
            dstate_ref[...] = jnp.zeros_like(dstate_ref)

        for c0 in range(cps - ng, -1, -ng):
            group(c0, q_ref, k_ref, v_ref, b_ref, gc_ref, gr_ref, saved_refs, do_ref, d_ref, dgate_ref, dstate_ref,
                  incl, strict, upper, ones, last_row, lane_ids, rsum, total, of_chunk)
        if nx:
            pl.when(pl.program_id(0) == steps - 1)(lambda: _wait_all(copies()))

    def group(c0, q_ref, k_ref, v_ref, b_ref, gc_ref, gr_ref, saved_refs, do_ref, d_ref, dgate_ref, dstate_ref,
              incl, strict, upper, ones, last_row, lane_ids, rsum, total, of_chunk):
        ng = GDN_BWD_GROUP
        nb = GDN_HEADS * ng
        rows = pl.ds(c0 * CHUNK, ng * CHUNK)
        s_ref, t_ref, a_ref, p_ref, uw_ref, vn_ref = saved_refs
        _, q, k, v, b, gc, dm, kb, vb, e, _, _, gl, eg = _chunk_terms(
            q_ref, k_ref, v_ref, b_ref, gc_ref, gr_ref, c0, incl, strict, ng, scores=False)
        s = s_ref[:, c0:c0 + ng].reshape(nb, dk, dk)
        tm = t_ref[:, c0:c0 + ng].reshape(nb, CHUNK, CHUNK)
        a = a_ref[:, c0:c0 + ng].reshape(nb, CHUNK, CHUNK)
        p = p_ref[:, c0:c0 + ng].reshape(nb, CHUNK, CHUNK)
        d_out = _heads_of(do_ref, rows).reshape(nb, CHUNK, dk)
        vn = _heads_of(vn_ref, rows).reshape(nb, CHUNK, dk)
        uw = jnp.stack([uw_ref[rows, h * 2 * dk:(h + 1) * 2 * dk] for h in range(GDN_HEADS)]).reshape(nb, CHUNK, 2 * dk)
        u, w = uw[:, :, :dk], uw[:, :, dk:]
        el = jnp.exp(gl)
        kbe = kb * e
        qe = q * e
        kd = k * eg
        bm, bm_nt, bm_tn = (functools.partial(f, right_low=False) for f in (_bm, _bm_nt, _bm_tn))
        pt_do = bm_tn(p, d_out)
        qet_do = bm_tn(qe, d_out)

        ds = dstate_ref[...]
        d_vn_c, ds_c = [None] * ng, [None] * ng
        for c in range(ng - 1, -1, -1):
            ds_c[c] = ds
            d_vn_c[c] = of_chunk(pt_do, c) + bm(of_chunk(kd, c), ds)
            ds = of_chunk(el, c) * ds + of_chunk(qet_do, c) - bm_tn(of_chunk(w, c), d_vn_c[c])
        dstate_ref[...] = ds
        by_chunk = lambda xs: jnp.stack([xs[c][h] for h in range(GDN_HEADS) for c in range(ng)])
        d_vn, ds = by_chunk(d_vn_c), by_chunk(ds_c)

        on_s = bm_nt(jnp.concatenate([d_out, d_vn], axis=1), s)
        d_qe, d_w = on_s[:, :CHUNK], -on_s[:, CHUNK:]
        d_p = jnp.where(incl, bm_nt(d_out, vn), 0.0)
        d_kd = bm_nt(vn, ds)
        d_both = bm_tn(tm, jnp.concatenate([d_vn, d_w], axis=2))
        d_vb, d_kbe = d_both[:, :, :dk], d_both[:, :, dk:]
        d_a = -jnp.where(strict, bm_nt(d_both, uw), 0.0)
        m = d_a * dm
        n = d_p * dm
        on_k = bm(jnp.concatenate([m, n], axis=1), k)
        d_kb = on_k[:, :CHUNK] + d_kbe * e
        d_q = on_k[:, CHUNK:] + d_qe * e
        d_k = (bm_tn(jnp.concatenate([m, n], axis=1), jnp.concatenate([kb, q], axis=1))
               + d_kd * eg + b * d_kb)
        d_v = b * d_vb
        r = d_a * a + d_p * p
        kd_term = rsum(d_kd * kd)
        d_gl = total(ds * s) * el + jnp.sum(kd_term, axis=1, keepdims=True)
        d_gam = (rsum(r) - _times_exact(r, ones, _BTN)[:, :, 0:1] + rsum(d_qe * qe) + rsum(d_kbe * kbe) - kd_term
                 + jnp.where(last_row, d_gl, 0.0))
        d_beta = rsum(d_kb * k) + rsum(d_vb * v)
        d_g = _exact_times(upper, d_gam * ones, _BNN)[:, :, 0:1]
        per_head = lambda x: x.reshape(GDN_HEADS, ng * CHUNK, x.shape[-1])
        d_q, d_k, d_v, d_beta, d_g = (per_head(x) for x in (d_q, d_k, d_v, d_beta, d_g))
        gates = jnp.zeros((ng * CHUNK, LANES), F32)
        for h in range(GDN_HEADS):
            lanes = slice(h * dk, (h + 1) * dk)
            d_ref[0, rows, lanes] = d_q[h]
            d_ref[1, rows, lanes] = d_k[h]
            d_ref[2, rows, lanes] = d_v[h]
            gates = gates + (jnp.where(lane_ids == h, d_beta[h], 0.0)
                             + jnp.where(lane_ids == GDN_HEADS + h, d_g[h], 0.0))
        dgate_ref[rows, :] = gates

    d_spec = pl.BlockSpec((3, cps * CHUNK, width), lambda g: (0, steps - 1 - g, 0))
    return pl.pallas_call(
        body, name="gdn_bwd",
        grid=(steps,),
        in_specs=[rows_blk(width, 0), rows_blk(width, 1), rows_blk(width, 2), rows_blk(LANES), rows_blk(LANES), gate_r,
                  per_chunk(dk, dk), per_chunk(CHUNK, CHUNK), per_chunk(CHUNK, CHUNK), per_chunk(CHUNK, CHUNK),
                  rows_blk(2 * width), rows_blk(width), rows_blk(width)] + [_HBM] * nx,
        out_specs=[d_spec, rows_blk(LANES)] + [_HBM] * nx,
        out_shape=[jax.ShapeDtypeStruct((3, t_len, width), F32),
                   jax.ShapeDtypeStruct((t_len, LANES), F32)] + _direct_out_shapes(scatter, (True,) * nx),
        scratch_shapes=[pltpu.VMEM((GDN_HEADS, dk, dk), F32)] + (_direct_semaphores(nx) if nx else []),
        compiler_params=_params("arbitrary"),
    )(gact, gact, gact, beta_c, gam_c, gam_r, *saved, d_o, *scatter)


def _group_sums(x, group):
    rows, width = x.shape
    lane = lax.broadcasted_iota(jnp.int32, (1, LANES), 1)
    out = []
    for t in range(width // LANES):
        seg = x[:, t * LANES:(t + 1) * LANES]
        if group == LANES:
            out.append(jnp.broadcast_to(jnp.sum(seg, axis=-1, keepdims=True), (rows, LANES)))
        else:
            low = jnp.sum(jnp.where(lane < group, seg, 0.0), axis=-1, keepdims=True)
            high = jnp.sum(jnp.where(lane < group, 0.0, seg), axis=-1, keepdims=True)
            out.append(jnp.where(lane < group, low, high))
    return jnp.concatenate(out, axis=1)


def _post_call(o_sb, o_gd, proj_gates, x, target, w_out, sbw, gdw, fw, tm=256):
    t_len, d = x.shape
    half = 512
    sb_blocks = half // LANES

    def body(osb_ref, ogd_ref, zsb_ref, zgd_ref, x_ref, tg_ref, wo_ref, sbw_ref, gdw_ref, fw_ref,
             dx2_ref, dosb_ref, dogd_ref, dz_ref, loss_ref, gfw_ref, gsb_ref, ggd_ref, gwo_ref):
        step = pl.program_id(0)

        @pl.when(step == 0)
        def _():
            loss_ref[...] = jnp.zeros_like(loss_ref)
            gfw_ref[...] = jnp.zeros_like(gfw_ref)
            gsb_ref[...] = jnp.zeros_like(gsb_ref)
            ggd_ref[...] = jnp.zeros_like(ggd_ref)
            gwo_ref[...] = jnp.zeros_like(gwo_ref)

        def head_forward(o, z, w, head_dim):
            r = lax.rsqrt(_group_sums(o * o, head_dim) * (1.0 / head_dim) + EPS)
            nrm = o * r * w
            sg = _sigmoid(z)
            return r, nrm, sg, nrm * (z * sg)

        def head_backward(d_m, o, z, w, head_dim, r, nrm, sg):
            d_n = d_m * (z * sg)
            d_z = d_m * nrm * (sg * (1.0 + z * (1.0 - sg)))
            dnw = d_n * w
            d_o = r * dnw - o * (r * r * r) * (_group_sums(dnw * o, head_dim) * (1.0 / head_dim))
            return d_o, d_z, jnp.sum(d_n * o * r, axis=0, keepdims=True)

        osb = jnp.concatenate([osb_ref[j] for j in range(sb_blocks)], axis=1)
        ogd, zsb, zgd = ogd_ref[...], zsb_ref[...], zgd_ref[...]
        sbw_v, gdw_v = sbw_ref[...], gdw_ref[...]
        r_sb, n_sb, sg_sb, m_sb = head_forward(osb, zsb, sbw_v, SB_HEAD_DIM)
        r_gd, n_gd, sg_gd, m_gd = head_forward(ogd, zgd, gdw_v, GDN_HEAD_DIM)
        mixed = jnp.concatenate([m_sb, m_gd], axis=1).astype(MXU_DTYPE)
        wo = wo_ref[...]
        x2 = x_ref[...] + jnp.dot(mixed, wo, preferred_element_type=F32)
        r2 = lax.rsqrt(jnp.mean(x2 * x2, axis=-1, keepdims=True) + EPS)
        fw_v = fw_ref[...]
        err = x2 * r2 * fw_v - tg_ref[...]
        loss_ref[...] += 0.5 * jnp.sum(jnp.sum(err * err, axis=-1, keepdims=True) * (1.0 / d))
        dy = err * (1.0 / d)
        gg = dy * fw_v
        dx2 = r2 * gg - x2 * ((r2 * r2 * r2) * jnp.mean(gg * x2, axis=-1, keepdims=True))
        gfw_ref[...] += jnp.sum(dy * x2 * r2, axis=0, keepdims=True)
        dx2_ref[...] = dx2
        dx2b = dx2.astype(MXU_DTYPE)
        d_mixed = lax.dot_general(dx2b, wo, _NT, preferred_element_type=F32)
        gwo_ref[...] += lax.dot_general(mixed, dx2b, _TN, preferred_element_type=F32)
        d_osb, d_zsb, gsb = head_backward(d_mixed[:, :half], osb, zsb, sbw_v, SB_HEAD_DIM, r_sb, n_sb, sg_sb)
        d_ogd, d_zgd, ggd = head_backward(d_mixed[:, half:], ogd, zgd, gdw_v, GDN_HEAD_DIM, r_gd, n_gd, sg_gd)
        for j in range(sb_blocks):
            dosb_ref[j] = d_osb[:, j * LANES:(j + 1) * LANES]
        dogd_ref[...] = d_ogd
        dz_ref[0] = d_zsb
        dz_ref[1] = d_zgd
        gsb_ref[...] += gsb
        ggd_ref[...] += ggd

    row_blk = lambda w: pl.BlockSpec((tm, w), lambda i: (i, 0))
    blocks_blk = pl.BlockSpec((sb_blocks, tm, LANES), lambda i: (0, i, 0))
    fixed = lambda r, w: pl.BlockSpec((r, w), lambda i: (0, 0))
    return pl.pallas_call(
        body, name="post",
        grid=(t_len // tm,),
        in_specs=[blocks_blk, row_blk(half),
                  pl.BlockSpec((tm, half), lambda i: (i, 0)),
                  pl.BlockSpec((tm, half), lambda i: (i, 1)),
                  row_blk(d), row_blk(d), fixed(d, d), fixed(1, half), fixed(1, half), fixed(1, d)],
        out_specs=[row_blk(d), blocks_blk, row_blk(half),
                   pl.BlockSpec((2, tm, half), lambda i: (DPROJ_GATE_SLOT // 2, i, 0)),
                   fixed(1, LANES), fixed(1, d), fixed(1, half), fixed(1, half), fixed(d, d)],
        out_shape=[jax.ShapeDtypeStruct((t_len, d), F32), jax.ShapeDtypeStruct((sb_blocks, t_len, LANES), F32),
                   jax.ShapeDtypeStruct((t_len, half), F32),
                   jax.ShapeDtypeStruct((len(DPROJ_PIECE_OF_SLOT), t_len, half), F32),
                     jax.ShapeDtypeStruct((1, LANES), F32), jax.ShapeDtypeStruct((1, d), F32),
                     jax.ShapeDtypeStruct((1, half), F32), jax.ShapeDtypeStruct((1, half), F32),
                     jax.ShapeDtypeStruct((d, d), F32)],
        compiler_params=_params("arbitrary"),
    )(o_sb, o_gd, proj_gates, proj_gates, x, target, w_out, sbw, gdw, fw)


def _piece_of_slot(s):
    return jnp.where(s < DPROJ_GDN_SLOT, s, jnp.where(s < DPROJ_GATE_SLOT, s + 1,
                                                     jnp.where(s == DPROJ_GATE_SLOT, 3, 7)))


def _gw_in_call(h_t, dproj8):
    d, t_len = h_t.shape
    n_piece, _, pw = dproj8.shape

    def body(ht_ref, dp_ref, gw_ref):
        gw_ref[...] = jnp.dot(ht_ref[...], dp_ref[0].astype(MXU_DTYPE), preferred_element_type=F32)

    return pl.pallas_call(
        body, name="gw_in",
        grid=(n_piece,),
        in_specs=[pl.BlockSpec((d, t_len), lambda s: (0, 0)),
                  pl.BlockSpec((1, t_len, pw), lambda s: (s, 0, 0))],
        out_specs=pl.BlockSpec((d, pw), lambda s: (0, _piece_of_slot(s))),
        out_shape=jax.ShapeDtypeStruct((d, n_piece * pw), F32),
        compiler_params=_params("arbitrary"),
    )(h_t, dproj8)


def _slot_of_piece(p):
    return jnp.where(p < DPROJ_GDN_SLOT, p, jnp.where(p == 3, DPROJ_GATE_SLOT, jnp.where(p < 7, p - 1, 7)))


def _gw_in_shards_call(h_t, dproj8, dsmall, out_dtype):
    d, t_len = h_t.shape
    n_piece, _, pw = dproj8.shape
    ns = dsmall.shape[1]
    n_pairs = N_DEV // 2

    def body(ht_ref, dp_ref, ds_ref, chip_ref, prev_ref, gates_ref, send_ref, recv_ref, send_sems, recv_sems):
        p = pl.program_id(0)
        x_pos, y_pos, c = lax.axis_index("x"), lax.axis_index("y"), lax.axis_index("c")
        to_sibling = lambda pair: pltpu.make_async_remote_copy(
            src_ref=send_ref.at[pair], dst_ref=recv_ref.at[pair], send_sem=send_sems.at[pair],
            recv_sem=recv_sems.at[pair], device_id=(x_pos, y_pos, 1 - c), device_id_type=_MESH)

        @pl.when(p == 0)
        def _():
            gates_ref[...] = jnp.dot(ht_ref[...], ds_ref[...].astype(MXU_DTYPE), preferred_element_type=F32)

        def emit(s, tail):
            x = jnp.concatenate([prev_ref[...], tail], axis=1)
            y = x if s == 0 else pltpu.roll(x, SHARD_PAD - s, axis=1)
            shard = y[:, :SHARD_COLS].astype(out_dtype)

            @pl.when(c == s % 2)
            def _():
                chip_ref[s // 2] = shard

            @pl.when(c != s % 2)
            def _():
                send_ref[s // 2] = shard
                to_sibling(s // 2).start()

        @pl.when(p < n_piece)
        def _():
            cur = jnp.dot(ht_ref[...], dp_ref[0].astype(MXU_DTYPE), preferred_element_type=F32)
            for s in range(n_piece - 1):
                pl.when(p == s + 1)(functools.partial(emit, s, cur[:, :SHARD_PAD - pw]))
            prev_ref[...] = cur

        @pl.when(p == n_piece)
        def _():
            emit(n_piece - 1, gates_ref[...])
            for pair in range(n_pairs):
                to_sibling(pair).wait_send()
            for pair in range(n_pairs):
                to_sibling(pair).wait_recv()
                chip_ref[pair] = (chip_ref[pair].astype(F32) + recv_ref[pair].astype(F32)).astype(out_dtype)

    shards_of_side = lambda: pltpu.VMEM((n_pairs, d, SHARD_COLS), out_dtype)
    return pl.pallas_call(
        body, name="gw_in",
        grid=(n_piece + 1,),
        in_specs=[pl.BlockSpec((d, t_len), lambda p: (0, 0)),
                  pl.BlockSpec((1, t_len, pw), lambda p: (_slot_of_piece(jnp.minimum(p, n_piece - 1)), 0, 0)),
                  pl.BlockSpec((t_len, ns), lambda p: (0, 0))],
        out_specs=pl.BlockSpec((n_pairs, d, SHARD_COLS), lambda p: (0, 0, 0)),
        out_shape=jax.ShapeDtypeStruct((n_pairs, d, SHARD_COLS), out_dtype),
        scratch_shapes=[pltpu.VMEM((d, pw), F32), pltpu.VMEM((d, ns), F32), shards_of_side(), shards_of_side(),
                        pltpu.SemaphoreType.DMA((n_pairs,)), pltpu.SemaphoreType.DMA((n_pairs,))],
        compiler_params=_params("arbitrary"),
    )(h_t, dproj8, dsmall)


def _gw_small_call(h_t, dsmall, tm=512):
    d, t_len = h_t.shape
    ns = dsmall.shape[1]

    def body(ht_ref, dp_ref, gw_ref):
        @pl.when(pl.program_id(0) == 0)
        def _():
            gw_ref[...] = jnp.zeros_like(gw_ref)

        gw_ref[...] += jnp.dot(ht_ref[...], dp_ref[...].astype(MXU_DTYPE), preferred_element_type=F32)

    return pl.pallas_call(
        body, name="gw_small",
        grid=(t_len // tm,),
        in_specs=[pl.BlockSpec((d, tm), lambda t: (0, t)),
                  pl.BlockSpec((tm, ns), lambda t: (t, 0))],
        out_specs=pl.BlockSpec((d, ns), lambda t: (0, 0)),
        out_shape=jax.ShapeDtypeStruct((d, ns), F32),
        compiler_params=_params("arbitrary"),
    )(h_t, dsmall)


def _dx_call(dproj8, dsmall, w_main, w_small, x, r, dx2, norm_w, chip_scatter=(), peer_scatter=(), tm=256):
    t_len, d = x.shape
    n_piece, _, pw = dproj8.shape
    ns = dsmall.shape[1]
    nx = len(chip_scatter)
    n_peer = len(peer_scatter)
    steps = t_len // tm
    n_in = 8 + nx + n_peer
    n_out = 2 + nx + n_peer

    def body(*refs):
        dp_ref, ds_ref, wm_ref, ws_ref, x_ref, r_ref, dx2_ref, nw_ref = refs[:8]
        gx_ref, gnw_ref = refs[n_in:n_in + 2]
        scratch = refs[n_in + n_out:]
        copies = lambda: _chip_copies(refs[8:8 + nx], refs[n_in + 2:n_in + 2 + nx], *scratch[:3])
        if nx:
            pl.when(pl.program_id(0) == 0)(lambda: _start_all(copies()))

        @pl.when(pl.program_id(0) == 0)
        def _():
            gnw_ref[...] = jnp.zeros_like(gnw_ref)

        dh = lax.dot_general(ds_ref[...].astype(MXU_DTYPE), ws_ref[...], _NT, preferred_element_type=F32)
        for s, p in enumerate(DPROJ_PIECE_OF_SLOT):
            dh = dh + lax.dot_general(dp_ref[s].astype(MXU_DTYPE), wm_ref[:, p * pw:(p + 1) * pw], _NT,
                                      preferred_element_type=F32)
        xv, rv = x_ref[...], r_ref[...]
        dn = dh * nw_ref[...]
        gx_ref[...] = dx2_ref[...] + rv * dn - xv * ((rv * rv * rv) * jnp.mean(dn * xv, axis=-1, keepdims=True))
        gnw_ref[...] += jnp.sum(dh * xv * rv, axis=0, keepdims=True)

        @pl.when(pl.program_id(0) == steps - 1)
        def _():
            if n_peer:
                small_ref, parts_ref = refs[8 + nx:n_in]
                small_buf = scratch[6]
                small_buf[...] = small_ref[...]
                small_buf[0:1, :] = gnw_ref[...]
                peer_copies = _direct_copies([small_buf, parts_ref], refs[n_in + 2 + nx:n_in + n_out], *scratch[3:6],
                                             [False, True])
                _start_all(peer_copies)
            if nx:
                _wait_all(copies())
            if n_peer:
                _wait_all(peer_copies)

    assert n_peer in (0, 2) and (nx == 1 or not n_peer)
    peer_in_specs = [pl.BlockSpec(peer_scatter[0].shape, lambda i: (0, 0)), _HBM] if n_peer else []
    peer_scratch = _direct_semaphores(n_peer) + [pltpu.VMEM(peer_scatter[0].shape, F32)] if n_peer else []
    return pl.pallas_call(
        body, name="dx",
        grid=(steps,),
        in_specs=[pl.BlockSpec((n_piece, tm, pw), lambda i: (0, i, 0)),
                  pl.BlockSpec((tm, ns), lambda i: (i, 0)),
                  pl.BlockSpec((d, n_piece * pw), lambda i: (0, 0)),
                  pl.BlockSpec((d, ns), lambda i: (0, 0)),
                  pl.BlockSpec((tm, d), lambda i: (i, 0)),
                  pl.BlockSpec((tm, 1), lambda i: (i, 0)),
                  pl.BlockSpec((tm, d), lambda i: (i, 0)),
                  pl.BlockSpec((1, d), lambda i: (0, 0))] + [_HBM] * nx + peer_in_specs,
        out_specs=[pl.BlockSpec((tm, d), lambda i: (i, 0)),
                   pl.BlockSpec((1, d), lambda i: (0, 0))] + [_HBM] * (nx + n_peer),
        out_shape=[jax.ShapeDtypeStruct((t_len, d), F32), jax.ShapeDtypeStruct((1, d), F32)]
                  + [jax.ShapeDtypeStruct(a.shape, a.dtype) for a in chip_scatter]
                  + (_direct_out_shapes(peer_scatter, [False, True]) if n_peer else []),
        scratch_shapes=(_chip_semaphores(nx) if nx else []) + peer_scratch,
        compiler_params=_params("arbitrary"),
    )(dproj8, dsmall, w_main, w_small, x, r, dx2, norm_w, *chip_scatter, *peer_scatter)


def _direct_out_shapes(srcs, per_peer):
    return [jax.ShapeDtypeStruct(s.shape if pp else (N_DEV,) + s.shape, s.dtype) for s, pp in zip(srcs, per_peer)]


def _direct_semaphores(n):
    return [pltpu.SemaphoreType.DMA((n * (N_DEV - 1),)), pltpu.SemaphoreType.DMA((n * (N_DEV - 1),)),
            pltpu.SemaphoreType.DMA((n,))]


def _direct_copies(src_refs, out_refs, send_sems, recv_sems, local_sems, per_peer):
    x, y, c = lax.axis_index("x"), lax.axis_index("y"), lax.axis_index("c")
    me = 4 * x + 2 * y + c
    local, remote = [], []
    for a in range(len(src_refs)):
        mine = src_refs[a].at[me] if per_peer[a] else src_refs[a]
        local.append(pltpu.make_async_copy(mine, out_refs[a].at[me], local_sems.at[a]))
    for k in range(1, N_DEV):
        kx, ky, kc = (k >> 2) & 1, (k >> 1) & 1, k & 1
        px = 1 - x if kx else x
        py = 1 - y if ky else y
        pc = 1 - c if kc else c
        peer = 4 * px + 2 * py + pc
        for a in range(len(src_refs)):
            sem = a * (N_DEV - 1) + (k - 1)
            remote.append(pltpu.make_async_remote_copy(
                src_ref=src_refs[a].at[peer] if per_peer[a] else src_refs[a], dst_ref=out_refs[a].at[me],
                send_sem=send_sems.at[sem], recv_sem=recv_sems.at[sem],
                device_id=(px, py, pc), device_id_type=pl.DeviceIdType.MESH))
    return local, remote


def _start_all(copies):
    local, remote = copies
    for cp in local + remote:
        cp.start()


def _wait_all(copies):
    local, remote = copies
    for cp in remote:
        cp.wait_send()
    for cp in remote:
        cp.wait_recv()
    for cp in local:
        cp.wait()


N_CHIPS = 4
_HBM = pl.BlockSpec(memory_space=pl.ANY)
_MESH = pl.DeviceIdType.MESH


def _gather_call(name, srcs):
    n = len(srcs)
    per = N_DEV - 1

    def body(*refs):
        src_refs, out_refs = refs[:n], refs[n:2 * n]
        send_sems, recv_sems, local_sems = refs[2 * n:]
        x, y, c = lax.axis_index("x"), lax.axis_index("y"), lax.axis_index("c")
        me, sibling = (x, y, c), (x, y, 1 - c)
        x_nbr, y_nbr, diagonal = (1 - x, y), (x, 1 - y), (1 - x, 1 - y)
        held = ((1 - x) * c + x * (1 - c), y * c + (1 - y) * (1 - c))
        onward = (x * c + (1 - x) * (1 - c), (1 - y) * c + y * (1 - c))
        slot = lambda px, py, pc: 4 * px + 2 * py + pc

        def copy(a, k, block, to, from_src=False):
            rows = out_refs[a].at[slot(*block)]
            return pltpu.make_async_remote_copy(
                src_ref=src_refs[a] if from_src else rows, dst_ref=rows,
                send_sem=send_sems.at[a * per + k], recv_sem=recv_sems.at[a * per + k],
                device_id=to, device_id_type=_MESH)

        local = [pltpu.make_async_copy(src_refs[a], out_refs[a].at[slot(*me)], local_sems.at[a]) for a in range(n)]
        started = []

        def start(cp):
            cp.start()
            started.append(cp)

        for cp in local:
            cp.start()
        for a in range(n):
            start(copy(a, 0, me, sibling, True))
            start(copy(a, 1, me, (*x_nbr, c), True))
            start(copy(a, 2, me, (*y_nbr, c), True))
        for a in range(n):
            copy(a, 1, (*x_nbr, c), me).wait_recv()
            copy(a, 2, (*y_nbr, c), me).wait_recv()
            start(copy(a, 3, (*held, c), (*onward, c)))
            start(copy(a, 4, (*x_nbr, c), sibling))
            start(copy(a, 5, (*y_nbr, c), sibling))
        for a in range(n):
            copy(a, 3, (*diagonal, c), me).wait_recv()
            start(copy(a, 6, (*diagonal, c), sibling))
        for a in range(n):
            copy(a, 0, sibling, me).wait_recv()
            for k, chip in ((4, x_nbr), (5, y_nbr), (6, diagonal)):
                copy(a, k, (*chip, 1 - c), me).wait_recv()
        for cp in started:
            cp.wait_send()
        for cp in local:
            cp.wait()

    return pl.pallas_call(
        body, name=name,
        in_specs=[_HBM] * n, out_specs=[_HBM] * n,
        out_shape=[jax.ShapeDtypeStruct((N_DEV,) + s.shape, s.dtype) for s in srcs],
        scratch_shapes=[pltpu.SemaphoreType.DMA((n * per,)), pltpu.SemaphoreType.DMA((n * per,)),
                        pltpu.SemaphoreType.DMA((n,))],
    )(*srcs)


def _chip_semaphores(n):
    per = N_CHIPS - 1
    return [pltpu.SemaphoreType.DMA((n * per,)), pltpu.SemaphoreType.DMA((n * per,)), pltpu.SemaphoreType.DMA((n,))]


def _chip_copies(src_refs, out_refs, send_sems, recv_sems, local_sems):
    per = N_CHIPS - 1
    x, y, c = lax.axis_index("x"), lax.axis_index("y"), lax.axis_index("c")
    mine = 2 * x + y
    chips = [(1 - x, y), (x, 1 - y), (1 - x, 1 - y)]
    n = len(src_refs)
    local = [pltpu.make_async_copy(src_refs[a].at[mine], out_refs[a].at[mine], local_sems.at[a]) for a in range(n)]
    remote = []
    for a in range(n):
        for j, (px, py) in enumerate(chips):
            remote.append(pltpu.make_async_remote_copy(
                src_ref=src_refs[a].at[2 * px + py], dst_ref=out_refs[a].at[mine],
                send_sem=send_sems.at[a * per + j], recv_sem=recv_sems.at[a * per + j],
                device_id=(px, py, c), device_id_type=_MESH))
    return local, remote


def _adam_call(name, parts, w, m, v, tr):
    rows, cols = w.shape
    n_slots = parts.shape[0]

    def body(p_ref, w_ref, m_ref, v_ref, g_ref, d_ref, nm_ref, nv_ref):
        g = p_ref[0].astype(F32)
        for s in range(1, n_slots):
            g = g + p_ref[s].astype(F32)
        m_new = ADAM_B1 * m_ref[...] + (1.0 - ADAM_B1) * g
        v_new = ADAM_B2 * v_ref[...] + (1.0 - ADAM_B2) * (g * g)
        m_hat = m_new / (1.0 - ADAM_B1 ** ADAM_STEP)
        v_hat = v_new / (1.0 - ADAM_B2 ** ADAM_STEP)
        g_ref[...] = g
        d_ref[...] = -ADAM_LR * (m_hat / (jnp.sqrt(v_hat) + ADAM_EPS) + ADAM_WD * w_ref[...])
        nm_ref[...] = m_new
        nv_ref[...] = v_new

    blk = pl.BlockSpec((tr, cols), lambda i: (i, 0))
    return pl.pallas_call(
        body, name=name,
        grid=(rows // tr,),
        in_specs=[pl.BlockSpec((n_slots, tr, cols), lambda i: (0, i, 0)), blk, blk, blk],
        out_specs=[blk] * 4,
        out_shape=[jax.ShapeDtypeStruct((rows, cols), F32)] * 4,
        compiler_params=_params("arbitrary"),
    )(parts, w, m, v)


def _columns_to_rows_call(name, w_t, rows, dtype):
    row_tiles = rows // LANES
    cols = w_t.shape[0] // row_tiles
    whole = cols // LANES * LANES

    def body(w_ref, out_ref):
        diagonal = (lax.broadcasted_iota(jnp.int32, (LANES, LANES), 0)
                    == lax.broadcasted_iota(jnp.int32, (LANES, LANES), 1))
        for a in range(row_tiles):
            out_ref[a * LANES:(a + 1) * LANES, :whole] = (
                w_ref[pl.ds(a, whole, stride=row_tiles), :].T.astype(dtype))
            for c in range(whole, cols):
                column = w_ref[pl.ds(c * row_tiles + a, 1), :]
                upright = jnp.sum(jnp.where(diagonal, column, 0.0), axis=1, keepdims=True)
                out_ref[a * LANES:(a + 1) * LANES, c:c + 1] = upright.astype(dtype)

    vm = pl.BlockSpec(memory_space=pltpu.VMEM)
    return pl.pallas_call(
        body, name=name,
        in_specs=[vm], out_specs=vm,
        out_shape=jax.ShapeDtypeStruct((rows, cols), dtype),
        compiler_params=pltpu.CompilerParams(vmem_limit_bytes=VMEM_LIMIT_BYTES),
    )(w_t)


def _adam_columns_call(name, parts, w_t, m_t, v_t):
    n_slots, rows, cols = parts.shape
    row_tiles = rows // LANES
    cols_pad = -(-cols // LANES) * LANES

    def body(p_ref, w_ref, m_ref, v_ref, *out_refs):
        for a in range(row_tiles):
            g = p_ref[0, a * LANES:(a + 1) * LANES, :].astype(F32)
            for s in range(1, n_slots):
                g = g + p_ref[s, a * LANES:(a + 1) * LANES, :].astype(F32)
            g = jnp.concatenate([g, jnp.zeros((LANES, cols_pad - cols), F32)], axis=1).T[:cols]
            column_rows = pl.ds(a, cols, stride=row_tiles)
            results = (g,) + _adamw(g, w_ref[column_rows, :], m_ref[column_rows, :], v_ref[column_rows, :])
            for out_ref, val in zip(out_refs, results):
                out_ref[column_rows, :] = val

    vm = pl.BlockSpec(memory_space=pltpu.VMEM)
    return pl.pallas_call(
        body, name=name,
        in_specs=[vm] * 4, out_specs=[vm] * 4,
        out_shape=[jax.ShapeDtypeStruct(w_t.shape, F32)] * 4,
        compiler_params=pltpu.CompilerParams(vmem_limit_bytes=VMEM_LIMIT_BYTES),
    )(parts, w_t, m_t, v_t)


N_PIECES = 8
PIECE = 512
SHARD_COLS = 513
SHARD_PAD = 640
RELAYOUT_ROWS = 256


def _from_shards_call(shards):
    _, d, _ = shards.shape
    tr = RELAYOUT_ROWS

    def body(p_ref, m_ref, s_ref):
        lane = lax.broadcasted_iota(jnp.int32, (tr, SHARD_PAD), 1)
        pad = jnp.zeros((tr, SHARD_PAD - SHARD_COLS), p_ref.dtype)
        sh = [jnp.concatenate([p_ref[s], pad], axis=1) for s in range(N_DEV)]
        for p in range(N_PIECES):
            y = sh[p] if p == 0 else pltpu.roll(sh[p], p, axis=1)
            if p > 0:
                y = jnp.where(lane < p, pltpu.roll(sh[p - 1], SHARD_PAD - (SHARD_COLS - p), axis=1), y)
            m_ref[:, p * PIECE:(p + 1) * PIECE] = y[:, :PIECE].astype(m_ref.dtype)
        first_gate = N_PIECES * PIECE - (N_DEV - 1) * SHARD_COLS
        s_ref[...] = pltpu.roll(sh[N_DEV - 1], SHARD_PAD - first_gate, axis=1)[:, :LANES].astype(s_ref.dtype)

    return pl.pallas_call(
        body, name="w_in_from_shards",
        grid=(d // tr,),
        in_specs=[pl.BlockSpec((N_DEV, tr, SHARD_COLS), lambda i: (0, i, 0))],
        out_specs=[pl.BlockSpec((tr, N_PIECES * PIECE), lambda i: (i, 0)), pl.BlockSpec((tr, LANES), lambda i: (i, 0))],
        out_shape=[jax.ShapeDtypeStruct((d, N_PIECES * PIECE), shards.dtype),
                   jax.ShapeDtypeStruct((d, LANES), shards.dtype)],
        compiler_params=_params("arbitrary"),
    )(shards)


def _adamw(g, w, m, v):
    m_new = ADAM_B1 * m + (1.0 - ADAM_B1) * g
    v_new = ADAM_B2 * v + (1.0 - ADAM_B2) * (g * g)
    m_hat = m_new / (1.0 - ADAM_B1 ** ADAM_STEP)
    v_hat = v_new / (1.0 - ADAM_B2 ** ADAM_STEP)
    return -ADAM_LR * (m_hat / (jnp.sqrt(v_hat) + ADAM_EPS) + ADAM_WD * w), m_new, v_new


def _adam_small_call(parts, ws, ms, vs):
    n = len(ws)
    n_slots = parts.shape[0]

    def body(*refs):
        p_ref = refs[0]
        w_refs, m_refs, v_refs = refs[1:1 + n], refs[1 + n:1 + 2 * n], refs[1 + 2 * n:1 + 3 * n]
        loss_ref = refs[1 + 3 * n]
        outs = refs[2 + 3 * n:]
        g_all = p_ref[0]
        for s in range(1, n_slots):
            g_all = g_all + p_ref[s]
        loss_ref[...] = g_all[n:n + 1, 0:1]
        for r in range(n):
            size = w_refs[r].shape[1]
            g = g_all[r:r + 1, :size]
            delta, m_new, v_new = _adamw(g, w_refs[r][...], m_refs[r][...], v_refs[r][...])
            for kind, val in enumerate((g, delta, m_new, v_new)):
                outs[kind * n + r][...] = val

    vm = pl.BlockSpec(memory_space=pltpu.VMEM)
    shapes = [jax.ShapeDtypeStruct(w.shape, F32) for w in ws]
    return pl.pallas_call(
        body, name="adam_small",
        in_specs=[vm] * (1 + 3 * n), out_specs=[vm] * (1 + 4 * n),
        out_shape=[jax.ShapeDtypeStruct((1, 1), F32)] + shapes * 4,
    )(parts, *ws, *ms, *vs)


_SMALL_ROWS = ("norm1_w", "final_norm_w", "sb_norm_w", "gdn_norm_w", "gdn_A_log", "gdn_dt_bias", "loss")


def _pack_small(vals, width):
    rows = [jnp.pad(a.reshape(1, -1).astype(F32), ((0, 0), (0, width - a.size))) for a in vals]
    rows += [jnp.zeros((1, width), F32)] * (8 - len(rows))
    return jnp.concatenate(rows, axis=0)


def _device_step(x2d, tgt, w_main, w_small, w_out_full, conv_full, norm1_w, sb_norm_w, gdn_A_log, gdn_dt_bias,
                 gdn_norm_w, final_norm_w, distributed=False):
    t_len, d = x2d.shape
    n_chunks = t_len // CHUNK
    w_main, w_small, w_out_full = (a.astype(MXU_DTYPE) for a in (w_main, w_small, w_out_full))
    w_small_t = w_small[:, :2 * GDN_HEADS].T

    pad_lanes = lambda a, lo: jnp.pad(a.reshape(1, -1), ((0, 0), (lo, LANES - lo - a.size)))
    alog_l, dtb_l = pad_lanes(gdn_A_log, GDN_HEADS), pad_lanes(gdn_dt_bias, GDN_HEADS)
    alog_c, dtb_c = alog_l[:, :8].T, dtb_l[:, :8].T
    sbw = jnp.tile(sb_norm_w, (1, 512 // SB_HEAD_DIM))
    gdw = jnp.tile(gdn_norm_w, (1, 512 // GDN_HEAD_DIM))
    fw = final_norm_w.reshape(1, d)

    if distributed:
        proj_cols, proj_gates, ps, pst, h_t, r1, w_out_g, conv_g = _inproj_call(
            x2d, norm1_w, w_main, w_small, w_small_t, gather=(w_out_full, conv_full))
        w_out_full = w_out_g.reshape(d, d)
        conv_full = conv_g.transpose(1, 0, 2).reshape(CONV_WIDTH, N_DEV * conv_g.shape[2])
    else:
        proj_cols, proj_gates, ps, pst, h_t, r1 = _inproj_call(x2d, norm1_w, w_main, w_small, w_small_t)
    o_sb, sp_total, sb_blocks_run = _sb_fwd_call(proj_cols, t_len)
    gact = _gdn_prep_call(proj_cols, conv_full, t_len, after=sp_total)
    beta_l, gcol_l, grow = _gdn_gates_call(ps, pst, alog_l, dtb_l, alog_c, dtb_c, t_len)
    gam_r = grow[GDN_HEADS:2 * GDN_HEADS].reshape(GDN_HEADS, n_chunks, 1, CHUNK)
    o_gd, *gdn_saved = _gdn_fwd_call(gact, beta_l, gcol_l, gam_r, t_len)

    (dx2, d_osb, d_ogd, dproj8, loss_p, g_fw, g_sbw, g_gdw, g_wout) = _post_call(
        o_sb, o_gd, proj_gates, x2d, tgt, w_out_full, sbw, gdw, fw)

    dproj8 = _sb_bwd_call(proj_cols, sp_total, sb_blocks_run, d_osb, dproj8, t_len)
    if distributed:
        d_gact3, d_gates, g_wout = _gdn_bwd_call(gact, beta_l, gcol_l, gam_r, gdn_saved, d_ogd, t_len,
                                                 scatter=(g_wout.reshape(N_DEV, d // N_DEV, d),))
    else:
        d_gact3, d_gates = _gdn_bwd_call(gact, beta_l, gcol_l, gam_r, gdn_saved, d_ogd, t_len)
    dproj8, g_conv = _gdn_prep_bwd_call(proj_cols, conv_full, d_gact3, dproj8, t_len)
    dsmall, g_alog, g_dtb = _gdn_gates_bwd_call(ps, alog_l, dtb_l, d_gates, t_len)

    if distributed:
        chip_partials = _gw_in_shards_call(h_t, dproj8, dsmall, WIRE_DTYPE)
        fold = lambda a, group: a.reshape(-1, group).sum(axis=0)
        small_g = _pack_small([jnp.zeros((d,), F32), g_fw, fold(g_sbw, SB_HEAD_DIM), fold(g_gdw, GDN_HEAD_DIM),
                               g_alog[0, GDN_HEADS:2 * GDN_HEADS], g_dtb[0, GDN_HEADS:2 * GDN_HEADS],
                               loss_p[0, :1]], d)
        conv_cols = g_conv.shape[1] // N_DEV
        g_conv_parts = g_conv.reshape(CONV_WIDTH, N_DEV, conv_cols).transpose(1, 0, 2)
        grad_x, _, g_w_in, p_small, p_conv = _dx_call(dproj8, dsmall, w_main, w_small, x2d, r1, dx2, norm1_w,
                                                      chip_scatter=(chip_partials,),
                                                      peer_scatter=(small_g, g_conv_parts))
        return grad_x, g_w_in, g_wout, p_small, p_conv
    else:
        grad_x, g_n1 = _dx_call(dproj8, dsmall, w_main, w_small, x2d, r1, dx2, norm1_w)
        g_w_in = (_gw_in_call(h_t, dproj8), _gw_small_call(h_t, dsmall))
    return (loss_p, grad_x, g_n1, g_w_in, g_sbw, g_conv, g_alog, g_dtb, g_gdw, g_wout, g_fw)


def kernel(x, norm1_w, w_in, sb_norm_w, gdn_conv_w, gdn_A_log, gdn_dt_bias, gdn_norm_w, w_out, final_norm_w, loss_target, m_norm1_w, m_w_in, m_sb_norm_w, m_gdn_conv_w, m_gdn_A_log, m_gdn_dt_bias, m_gdn_norm_w, m_w_out, m_final_norm_w, v_norm1_w, v_w_in, v_sb_norm_w, v_gdn_conv_w, v_gdn_A_log, v_gdn_dt_bias, v_gdn_norm_w, v_w_out, v_final_norm_w):
    d = x.shape[2]
    shard_cols = w_in.shape[2]

    columns = lambda a: a.transpose(2, 0, 1).reshape(shard_cols * d // LANES, LANES)
    from_columns = lambda a: a.reshape(shard_cols, d // LANES, LANES).transpose(1, 2, 0).reshape(1, d, shard_cols)
    (w_in_g,) = _gather_call("gather_weights", [_columns_to_rows_call("w_in_to_wire", columns(w_in), d, WIRE_DTYPE)])
    w_main, w_small = _from_shards_call(w_in_g)

    grad_x, p_w_in, p_wout, p_small, p_conv = _device_step(
        x[0], loss_target[0], w_main, w_small, w_out[0].astype(WIRE_DTYPE), gdn_conv_w[0], norm1_w, sb_norm_w,
        gdn_A_log, gdn_dt_bias, gdn_norm_w, final_norm_w, distributed=True)

    r_w_in = [from_columns(a) for a in _adam_columns_call("adam_w_in", p_w_in, columns(w_in), columns(m_w_in),
                                                          columns(v_w_in))]
    r_wout = _adam_call("adam_w_out", p_wout, w_out[0], m_w_out[0], v_w_out[0], d // N_DEV)
    r_conv = _adam_call("adam_conv", p_conv, gdn_conv_w[0], m_gdn_conv_w[0], v_gdn_conv_w[0], CONV_WIDTH)

    row = lambda a: a.reshape(1, -1)
    n_small = len(_SMALL_ROWS) - 1
    r_small = _adam_small_call(
        p_small,
        [norm1_w, row(final_norm_w), sb_norm_w, gdn_norm_w, gdn_A_log, gdn_dt_bias],
        [m_norm1_w, row(m_final_norm_w), m_sb_norm_w, m_gdn_norm_w, m_gdn_A_log, m_gdn_dt_bias],
        [v_norm1_w, row(v_final_norm_w), v_sb_norm_w, v_gdn_norm_w, v_gdn_A_log, v_gdn_dt_bias])

    def small_out(kind, name):
        out = r_small[1 + kind * n_small + _SMALL_ROWS.index(name)]
        return out.reshape(final_norm_w.shape) if name == "final_norm_w" else out

    def outputs(kind):
        return (small_out(kind, "norm1_w"), r_w_in[kind], small_out(kind, "sb_norm_w"), r_conv[kind][None],
                small_out(kind, "gdn_A_log"), small_out(kind, "gdn_dt_bias"), small_out(kind, "gdn_norm_w"),
                r_wout[kind][None], small_out(kind, "final_norm_w"))

    return (r_small[0][0, 0], grad_x[None], *outputs(0), *outputs(1), *outputs(2), *outputs(3))
```

```python
import functools

import jax
import jax.numpy as jnp
from jax import lax
from jax.experimental import pallas as pl
from jax.experimental.pallas import tpu as pltpu

F32 = jnp.float32
MXU_DTYPE = jnp.bfloat16
WIRE_DTYPE = jnp.bfloat16
EXACT = lax.Precision.HIGHEST
EPS = 1e-6
N_DEV = 8
SB_HEAD_DIM = 64
GDN_HEAD_DIM = 128
GDN_HEADS = 4
GDN_CHUNKS_PER_STEP = 4
GDN_BWD_GROUP = 1
CHUNK = 64
CONV_WIDTH = 4
LANES = 128
SB_BLOCK = 128
SB_BQ = 256
VMEM_LIMIT_BYTES = 56 * 1024 * 1024

PIECE_COLS = 512
PROJ_PIECE_KINDS = ("heads", "heads", "heads", "gate", "heads", "heads", "heads", "gate")
SB_FIRST_BLOCK, GDN_FIRST_BLOCK = 0, 12

DPROJ_PIECE_OF_SLOT = (0, 1, 2, 4, 5, 6, 3, 7)
DPROJ_SB_SLOT, DPROJ_GDN_SLOT, DPROJ_GATE_SLOT = 0, 3, 6

ADAM_LR = 0.001
ADAM_B1 = 0.9
ADAM_B2 = 0.999
ADAM_EPS = 1e-08
ADAM_WD = 0.01
ADAM_STEP = 10

_NN = (((1,), (0,)), ((), ()))
_NT = (((1,), (1,)), ((), ()))
_TN = (((0,), (0,)), ((), ()))
_BNN = (((2,), (1,)), ((0,), (0,)))
_BNT = (((2,), (2,)), ((0,), (0,)))
_BTN = (((1,), (1,)), ((0,), (0,)))


def _mx(a, b):
    return jnp.dot(a, b, precision=EXACT, preferred_element_type=F32)


def _split(x):
    hi = x.astype(MXU_DTYPE)
    return hi, (x - hi.astype(F32)).astype(MXU_DTYPE)


def _m3_general(a, b, dims, right_low=True):
    ah, al = _split(a)
    bh, bl = _split(b)
    dot = lambda x, y: lax.dot_general(x, y, dims, preferred_element_type=F32)
    (contract, _), (batch, _) = dims
    free = [ax for ax in range(a.ndim) if ax not in contract and ax not in batch][0]
    m = a.shape[free]
    both = dot(jnp.concatenate([ah, al], axis=free), bh)
    out_axis = len(batch)
    hi_part = lax.slice_in_dim(both, 0, m, axis=out_axis)
    lo_part = lax.slice_in_dim(both, m, 2 * m, axis=out_axis)
    return hi_part + (dot(ah, bl) + lo_part if right_low else lo_part)


def _times_exact(a, b_exact, dims):
    ah, al = _split(a)
    (contract, _), (batch, _) = dims
    free = [ax for ax in range(a.ndim) if ax not in contract and ax not in batch][0]
    m = a.shape[free]
    both = lax.dot_general(jnp.concatenate([ah, al], axis=free), b_exact.astype(MXU_DTYPE), dims,
                           preferred_element_type=F32)
    out_axis = len(batch)
    return lax.slice_in_dim(both, 0, m, axis=out_axis) + lax.slice_in_dim(both, m, 2 * m, axis=out_axis)


def _exact_times(a_exact, b, dims):
    bh, bl = _split(b)
    n = b.shape[-1]
    both = lax.dot_general(a_exact.astype(MXU_DTYPE), jnp.concatenate([bh, bl], axis=-1), dims,
                           preferred_element_type=F32)
    return both[..., :n] + both[..., n:]


def _sigmoid(z):
    return 1.0 / (1.0 + jnp.exp(-z))


def _softplus(z):
    return jnp.maximum(z, 0.0) + jnp.log(1.0 + jnp.exp(-jnp.abs(z)))


def _params(*semantics):
    return pltpu.CompilerParams(dimension_semantics=semantics, vmem_limit_bytes=VMEM_LIMIT_BYTES)


def _inproj_call(x, norm_w, w_main, w_small, w_small_t, gather=(), tm=256):
    t_len, d = x.shape
    n = w_main.shape[1]
    ns = w_small.shape[1]
    nst = w_small_t.shape[0]
    ng = len(gather)
    steps = t_len // tm

    blocks_per_piece = PIECE_COLS // LANES
    n_gate_cols = PIECE_COLS * PROJ_PIECE_KINDS.count("gate")
    n_col_blocks = blocks_per_piece * PROJ_PIECE_KINDS.count("heads")

    def body(*refs):
        x_ref, nw_ref, wm_ref, ws_ref, wst_ref = refs[:5]
        cols_ref, pz_ref, ps_ref, pst_ref, ht_ref, r_ref = refs[5 + ng:11 + ng]
        copies = lambda: _direct_copies(refs[5:5 + ng], refs[11 + ng:11 + 2 * ng], *refs[11 + 2 * ng:], (False,) * ng)
        if ng:
            pl.when(pl.program_id(0) == 0)(lambda: _start_all(copies()))
        xv = x_ref[...]
        r = lax.rsqrt(jnp.mean(xv * xv, axis=-1, keepdims=True) + EPS)
        h = xv * r * nw_ref[...]
        hb = h.astype(MXU_DTYPE)
        n_block = n_gate = 0
        for piece, kind in enumerate(PROJ_PIECE_KINDS):
            out = jnp.dot(hb, wm_ref[:, piece * PIECE_COLS:(piece + 1) * PIECE_COLS], preferred_element_type=F32)
            if kind == "gate":
                pz_ref[:, n_gate * PIECE_COLS:(n_gate + 1) * PIECE_COLS] = out
                n_gate += 1
            else:
                for j in range(blocks_per_piece):
                    cols_ref[n_block + j] = out[:, j * LANES:(j + 1) * LANES]
                n_block += blocks_per_piece
        ps_ref[...] = jnp.dot(hb, ws_ref[...], preferred_element_type=F32)
        pst_ref[...] = lax.dot_general(wst_ref[...], hb, _NT, preferred_element_type=F32)
        ht_ref[...] = h.T.astype(MXU_DTYPE)
        r_ref[...] = r
        if ng:
            pl.when(pl.program_id(0) == steps - 1)(lambda: _wait_all(copies()))

    return pl.pallas_call(
        body, name="inproj",
        grid=(steps,),
        in_specs=[pl.BlockSpec((tm, d), lambda i: (i, 0)),
                  pl.BlockSpec((1, d), lambda i: (0, 0)),
                  pl.BlockSpec((d, n), lambda i: (0, 0)),
                  pl.BlockSpec((d, ns), lambda i: (0, 0)),
                  pl.BlockSpec((nst, d), lambda i: (0, 0))] + [_HBM] * ng,
        out_specs=[pl.BlockSpec((n_col_blocks, tm, LANES), lambda i: (0, i, 0)),
                   pl.BlockSpec((tm, n_gate_cols), lambda i: (i, 0)),
                   pl.BlockSpec((tm, ns), lambda i: (i, 0)),
                   pl.BlockSpec((nst, tm), lambda i: (0, i)),
                   pl.BlockSpec((d, tm), lambda i: (0, i)),
                   pl.BlockSpec((tm, 1), lambda i: (i, 0))] + [_HBM] * ng,
        out_shape=[jax.ShapeDtypeStruct((n_col_blocks, t_len, LANES), F32),
                   jax.ShapeDtypeStruct((t_len, n_gate_cols), F32),
                   jax.ShapeDtypeStruct((t_len, ns), F32),
                   jax.ShapeDtypeStruct((nst, t_len), F32),
                   jax.ShapeDtypeStruct((d, t_len), MXU_DTYPE),
                   jax.ShapeDtypeStruct((t_len, 1), F32)] + _direct_out_shapes(gather, (False,) * ng),
        scratch_shapes=_direct_semaphores(ng) if ng else [],
        compiler_params=_params("arbitrary"),
    )(x, norm_w, w_main, w_small, w_small_t, *gather)


def _running_sum_mm(x, tri):
    hi = x.astype(MXU_DTYPE)
    lo = (x - hi.astype(F32)).astype(MXU_DTYPE)
    return jnp.dot(hi, tri, preferred_element_type=F32) + jnp.dot(lo, tri, preferred_element_type=F32)


def _col_block(t_len, first):
    return pl.BlockSpec((1, t_len, LANES), lambda p: (first + p, 0, 0))


def _sb_iotas():
    row_i = lax.broadcasted_iota(jnp.int32, (SB_BQ, SB_BLOCK), 0)
    col_i = lax.broadcasted_iota(jnp.int32, (SB_BQ, SB_BLOCK), 1)
    sq_r = lax.broadcasted_iota(jnp.int32, (SB_BLOCK, SB_BLOCK), 0)
    sq_c = lax.broadcasted_iota(jnp.int32, (SB_BLOCK, SB_BLOCK), 1)
    return row_i, col_i, sq_r, sq_c


SB_DIAG_BLOCKS = SB_BQ // SB_BLOCK
SB_EXP_FLOOR = -110.0


def _sb_keys_descending(qi, tile, carry, z_bounds, n_heads, has_free):
    group = SB_DIAG_BLOCKS
    n_free = group * qi
    diag = list(range(group - 1, -1, -1))
    carry = tile([n_free + j for j in diag], [True] * group, carry, [j * SB_BLOCK for j in diag])

    def largest_exponent(c):
        worst = jnp.max(z_bounds[0] - c[1])
        for h in range(1, n_heads):
            worst = jnp.maximum(worst, jnp.max(z_bounds[h] - c[1 + h]))
        return worst

    always = group if has_free else 0

    def cond(state):
        return (state[0] < n_free) & ((state[1] > SB_EXP_FLOOR) | (state[0] < always))

    def body(state):
        first = n_free - 1 - state[0]
        c = tile([first - j for j in range(group)], [False] * group, state[2:])
        return (state[0] + group, largest_exponent(c), *c)

    out = lax.while_loop(cond, body, (jnp.int32(0), largest_exponent(carry), *carry))
    return out[2:], out[0]


def _sb_keys_ascending(qi, n_run, tile, carry, has_free):
    group = SB_DIAG_BLOCKS
    n_free = group * qi
    diag = list(range(group))
    kjs, los, masked = [n_free + j for j in diag], [j * SB_BLOCK for j in diag], [True] * group
    if has_free:
        early = lambda s: [n_free - n_run + group * s + j for j in range(group)]
        carry = lax.fori_loop(0, n_run // group - 1, lambda s, c: tile(early(s), [False] * group, c), carry)
        kjs, los, masked = [n_free - group + j for j in range(group)] + kjs, [0] * group + los, [False] * group + masked
    return tile(kjs, masked, carry, los)


def _sb_fwd_call(cols, t_len):
    nq = t_len // SB_BQ
    scale = float(SB_HEAD_DIM) ** -0.5
    n_pairs = 512 // LANES
    per_pair = LANES // SB_HEAD_DIM

    def body(q_blk, k_blk, v_blk, o_blk, st_ref, nrun_ref):
        q_ref, k_ref, v_ref, o_ref = q_blk.at[0], k_blk.at[0], v_blk.at[0], o_blk.at[0]
        lane = lax.broadcasted_iota(jnp.int32, (1, LANES), 1)
        row_i, col_i, sq_r, sq_c = _sb_iotas()
        ge = (sq_r >= sq_c).astype(MXU_DTYPE)
        hms = [((lane // SB_HEAD_DIM) == hh).astype(F32) for hh in range(per_pair)]
        k_sq = k_ref[...] * k_ref[...]
        k_norms = [jnp.sqrt(jnp.max(jnp.sum(k_sq * hm, axis=-1, keepdims=True))) * (1.02 * scale) for hm in hms]

        def q_block(qi, has_free):
            r0 = qi * SB_BQ if isinstance(qi, int) else pl.multiple_of(qi * SB_BQ, SB_BQ)
            rows = pl.ds(r0, SB_BQ)
            q_all = q_ref[rows, :]
            qms = [(q_all * (hm * scale)).astype(MXU_DTYPE) for hm in hms]
            z_bounds = [jnp.sqrt(jnp.sum(q_all * q_all * hm, axis=-1, keepdims=True)) * kn
                        for hm, kn in zip(hms, k_norms)]

            def tile(kjs, masked, kc, los=None):
                heads = range(per_pair)
                los = los or [0] * len(kjs)
                pairs = [(t, h) for t in range(len(kjs)) for h in heads]
                add_rows = lambda full, lo, part: full + part if lo == 0 else jnp.concatenate(
                    [full[:lo], full[lo:] + part], axis=0)
                acc, cs = kc[0], list(kc[1:])
                s0s = [kj * SB_BLOCK if isinstance(kj, int) else pl.multiple_of(kj * SB_BLOCK, SB_BLOCK) for kj in kjs]
                kbs = [k_ref[pl.ds(s0, SB_BLOCK), :].astype(MXU_DTYPE) for s0 in s0s]
                v_alls = [v_ref[pl.ds(s0, SB_BLOCK), :] for s0 in s0s]
                vms = {(t, h): (v_alls[t] * hms[h]).astype(MXU_DTYPE) for t, h in pairs}
                zs = {(t, h): lax.dot_general(qms[h][los[t]:], kbs[t], _NT, preferred_element_type=F32)
                      for t, h in pairs}
                masks = [(col_i[lo:] + s0) < (row_i[lo:] + r0) if m else None for m, lo, s0 in zip(masked, los, s0s)]
                keep = lambda t, a: a if masks[t] is None else jnp.where(masks[t], a, 0.0)
                sps = {(t, h): keep(t, _softplus(zs[t, h])) for t, h in pairs}
                sums = {p: _running_sum_mm(sps[p], ge) for p in pairs}
                mass = {}
                for t, h in pairs:
                    mass[t, h] = cs[h] if t == 0 else add_rows(
                        mass[t - 1, h], los[t - 1], jnp.sum(sps[t - 1, h], axis=-1, keepdims=True))
                ws = {(t, h): keep(t, jnp.exp(zs[t, h] - (sums[t, h] + mass[t, h][los[t]:]))) for t, h in pairs}
                for t, h in pairs:
                    acc = add_rows(acc, los[t], jnp.dot(ws[t, h].astype(MXU_DTYPE), vms[t, h],
                                                        preferred_element_type=F32))
                last = len(kjs) - 1
                cs = [add_rows(mass[last, h], los[last], jnp.sum(sps[last, h], axis=-1, keepdims=True)) for h in heads]
                return (acc, *cs)

            zero_col = jnp.zeros((SB_BQ, 1), F32)
            out, n_run = _sb_keys_descending(
                qi, tile, (jnp.zeros((SB_BQ, LANES), F32),) + (zero_col,) * per_pair, z_bounds, per_pair, has_free)
            o_ref[rows, :] = out[0]
            masses = jnp.zeros((SB_BQ, LANES), F32)
            for hh in range(per_pair):
                masses = jnp.where(lane == hh, out[1 + hh], masses)
            st_ref[rows, :] = masses
            nrun_ref[pl.program_id(0), qi] = n_run

        q_block(0, False)
        lax.fori_loop(1, nq, lambda qi, carry: (q_block(qi, True), carry)[1], 0)

    return pl.pallas_call(
        body, name="sb_fwd",
        grid=(n_pairs,),
        in_specs=[_col_block(t_len, SB_FIRST_BLOCK), _col_block(t_len, SB_FIRST_BLOCK + n_pairs),
                  _col_block(t_len, SB_FIRST_BLOCK + 2 * n_pairs)],
        out_specs=[_col_block(t_len, 0),
                   pl.BlockSpec((t_len, LANES), lambda p: (0, p)),
                   pl.BlockSpec(memory_space=pltpu.SMEM)],
        out_shape=[jax.ShapeDtypeStruct((n_pairs, t_len, LANES), F32),
                   jax.ShapeDtypeStruct((t_len, n_pairs * LANES), F32),
                   jax.ShapeDtypeStruct((n_pairs, nq), jnp.int32)],
        compiler_params=_params("arbitrary"),
    )(cols, cols, cols)


def _sb_bwd_call(cols, sp_total, n_run_all, d_o, dproj, t_len):
    nq = t_len // SB_BQ
    scale = float(SB_HEAD_DIM) ** -0.5
    n_pairs = 512 // LANES
    per_pair = LANES // SB_HEAD_DIM

    def body(q_blk, k_blk, v_blk, st_ref, nrun_ref, do_blk, dproj_in_ref, d_ref):
        q_ref, k_ref, v_ref, do_ref = q_blk.at[0], k_blk.at[0], v_blk.at[0], do_blk.at[0]
        lane = lax.broadcasted_iota(jnp.int32, (1, LANES), 1)
        row_i, col_i, sq_r, sq_c = _sb_iotas()
        lt = (sq_r < sq_c).astype(MXU_DTYPE)
        le = (sq_r <= sq_c).astype(MXU_DTYPE)
        hms = [((lane // SB_HEAD_DIM) == hh).astype(F32) for hh in range(per_pair)]
        d_ref[1] = jnp.zeros((t_len, LANES), F32)
        d_ref[2] = jnp.zeros((t_len, LANES), F32)

        def q_block(qi, has_free):
            r0 = qi * SB_BQ if isinstance(qi, int) else pl.multiple_of(qi * SB_BQ, SB_BQ)
            rows = pl.ds(r0, SB_BQ)
            q_all, do_all = q_ref[rows, :], do_ref[rows, :]
            qms = [(q_all * (hm * scale)).astype(MXU_DTYPE) for hm in hms]
            doms = [(do_all * hm).astype(MXU_DTYPE) for hm in hms]
            masses = st_ref[rows, :]
            totals = [jnp.sum(jnp.where(lane == hh, masses, 0.0), axis=-1, keepdims=True) for hh in range(per_pair)]

            def tile(kjs, masked, kc, los=None):
                heads = range(per_pair)
                los = los or [0] * len(kjs)
                pairs = [(t, h) for t in range(len(kjs)) for h in heads]
                add_rows = lambda full, lo, part: full + part if lo == 0 else jnp.concatenate(
                    [full[:lo], full[lo:] + part], axis=0)
                rsum = lambda a: jnp.sum(a, axis=-1, keepdims=True)
                dq, cls, gls = kc[0], list(kc[1:1 + per_pair]), list(kc[1 + per_pair:])
                s0s = [kj * SB_BLOCK if isinstance(kj, int) else pl.multiple_of(kj * SB_BLOCK, SB_BLOCK) for kj in kjs]
                k_alls = [k_ref[pl.ds(s0, SB_BLOCK), :] for s0 in s0s]
                v_alls = [v_ref[pl.ds(s0, SB_BLOCK), :] for s0 in s0s]
                kbs = [k_all.astype(MXU_DTYPE) for k_all in k_alls]
                vms = {(t, h): (v_alls[t] * hms[h]).astype(MXU_DTYPE) for t, h in pairs}
                kms = {(t, h): (k_alls[t] * (hms[h] * scale)).astype(MXU_DTYPE) for t, h in pairs}
                q_live = {(t, h): qms[h][los[t]:] for t, h in pairs}
                do_live = {(t, h): doms[h][los[t]:] for t, h in pairs}
                zs = {p: lax.dot_general(q_live[p], kbs[p[0]], _NT, preferred_element_type=F32) for p in pairs}
                das = {p: lax.dot_general(do_live[p], vms[p], _NT, preferred_element_type=F32) for p in pairs}
                masks = [(col_i[lo:] + s0) < (row_i[lo:] + r0) if m else None for m, lo, s0 in zip(masked, los, s0s)]
                keep = lambda t, a: a if masks[t] is None else jnp.where(masks[t], a, 0.0)
                sp_alls = {p: _softplus(zs[p]) for p in pairs}
                sps = {(t, h): keep(t, sp_alls[t, h]) for t, h in pairs}
                lefts = {p: _running_sum_mm(sps[p], lt) for p in pairs}
                cl = {}
                for t, h in pairs:
                    cl[t, h] = cls[h] if t == 0 else add_rows(cl[t - 1, h], los[t - 1], rsum(sps[t - 1, h]))
                ws = {(t, h): keep(t, jnp.exp(zs[t, h] - ((totals[h] - cl[t, h])[los[t]:] - lefts[t, h])))
                      for t, h in pairs}
                gs = {p: das[p] * ws[p] for p in pairs}
                g_sums = {p: _running_sum_mm(gs[p], le) for p in pairs}
                gl = {}
                for t, h in pairs:
                    gl[t, h] = gls[h] if t == 0 else add_rows(gl[t - 1, h], los[t - 1], rsum(gs[t - 1, h]))
                dzs = {(t, h): keep(t, gs[t, h] - jnp.exp(zs[t, h] - sp_alls[t, h]) * (gl[t, h][los[t]:] + g_sums[t, h])
                               ).astype(MXU_DTYPE) for t, h in pairs}
                for t in range(len(kjs)):
                    dk_t = jnp.zeros((SB_BLOCK, LANES), F32)
                    dv_t = jnp.zeros((SB_BLOCK, LANES), F32)
                    for h in heads:
                        dq = add_rows(dq, los[t], jnp.dot(dzs[t, h], kms[t, h], preferred_element_type=F32))
                        dk_t = dk_t + lax.dot_general(dzs[t, h], q_live[t, h], _TN, preferred_element_type=F32)
                        dv_t = dv_t + lax.dot_general(ws[t, h].astype(MXU_DTYPE), do_live[t, h], _TN,
                                                      preferred_element_type=F32)
                    d_ref[1, pl.ds(s0s[t], SB_BLOCK), :] += dk_t
                    d_ref[2, pl.ds(s0s[t], SB_BLOCK), :] += dv_t
                last = len(kjs) - 1
                cls = [add_rows(cl[last, h], los[last], rsum(sps[last, h])) for h in heads]
                gls = [add_rows(gl[last, h], los[last], rsum(gs[last, h])) for h in heads]
                return (dq, *cls, *gls)

            zero_col = jnp.zeros((SB_BQ, 1), F32)
            out = _sb_keys_ascending(qi, nrun_ref[pl.program_id(0), qi], tile,
                                     (jnp.zeros((SB_BQ, LANES), F32),) + (zero_col,) * (2 * per_pair), has_free)
            d_ref[0, rows, :] = out[0]

        q_block(0, False)
        lax.fori_loop(1, nq, lambda qi, carry: (q_block(qi, True), carry)[1], 0)

    return pl.pallas_call(
        body, name="sb_bwd",
        grid=(n_pairs,),
        in_specs=[_col_block(t_len, SB_FIRST_BLOCK), _col_block(t_len, SB_FIRST_BLOCK + n_pairs),
                  _col_block(t_len, SB_FIRST_BLOCK + 2 * n_pairs),
                  pl.BlockSpec((t_len, LANES), lambda p: (0, p)),
                  pl.BlockSpec(memory_space=pltpu.SMEM), _col_block(t_len, 0), _HBM],
        out_specs=pl.BlockSpec((3, t_len, LANES), lambda p: (DPROJ_SB_SLOT // 3, 0, p)),
        out_shape=jax.ShapeDtypeStruct(dproj.shape, dproj.dtype),
        input_output_aliases={6: 0},
        compiler_params=_params("arbitrary"),
    )(cols, cols, cols, sp_total, n_run_all, d_o, dproj)


def _conv_taps(xin, rows, t_len):
    taps = []
    for i in range(CONV_WIDTH):
        shift = CONV_WIDTH - 1 - i
        if shift == 0:
            taps.append(xin)
        else:
            taps.append(jnp.where(rows >= shift, pltpu.roll(xin, shift, axis=0), 0.0))
    return taps


def _gdn_prep_body_common(x_ref, w_ref, t_len):
    j = pl.program_id(0)
    xin = x_ref[...]
    rows = lax.broadcasted_iota(jnp.int32, (t_len, LANES), 0)
    taps = _conv_taps(xin, rows, t_len)
    pre = taps[0] * w_ref[0:1, :]
    for i in range(1, CONV_WIDTH):
        pre = pre + taps[i] * w_ref[i:i + 1, :]
    sg = _sigmoid(pre)
    act = pre * sg
    is_qk = j < 2 * GDN_HEADS
    nrm = jnp.where(is_qk, lax.rsqrt(jnp.sum(act * act, axis=-1, keepdims=True) + EPS), 1.0)
    sc = jnp.where(j < GDN_HEADS, float(GDN_HEAD_DIM) ** -0.5, 1.0)
    return j, rows, taps, pre, sg, act, is_qk, nrm, sc


def _gdn_prep_call(cols, conv_w, t_len, after):
    def body(x_blk, w_ref, after_ref, out_ref):
        _, _, _, _, _, act, _, nrm, sc = _gdn_prep_body_common(x_blk.at[0], w_ref, t_len)
        out_ref[...] = act * nrm * sc

    return pl.pallas_call(
        body, name="gdn_prep",
        grid=(3 * GDN_HEADS,),
        in_specs=[_col_block(t_len, GDN_FIRST_BLOCK),
                  pl.BlockSpec((CONV_WIDTH, LANES), lambda j: (0, j)),
                  pl.BlockSpec(memory_space=pl.ANY)],
        out_specs=pl.BlockSpec((t_len, LANES), lambda j: (0, j)),
        out_shape=jax.ShapeDtypeStruct((t_len, 3 * 512), F32),
        compiler_params=_params("arbitrary"),
    )(cols, conv_w, after)


def _gdn_prep_bwd_call(cols, conv_w, d_act3, dproj, t_len):
    def body(x_blk, w_ref, d_ref, dproj_in_ref, dx_ref, dw_ref):
        _, rows, taps, pre, sg, act, is_qk, nrm, sc = _gdn_prep_body_common(x_blk.at[0], w_ref, t_len)
        d_out = d_ref[0]
        dn = d_out * sc
        d_norm = nrm * dn - act * (nrm * nrm * nrm) * jnp.sum(dn * act, axis=-1, keepdims=True)
        d_act = jnp.where(is_qk, d_norm, d_out)
        d_pre = d_act * sg * (1.0 + pre * (1.0 - sg))
        dx = d_pre * w_ref[CONV_WIDTH - 1:CONV_WIDTH, :]
        dw_ref[CONV_WIDTH - 1:CONV_WIDTH, :] = jnp.sum(d_pre * taps[CONV_WIDTH - 1], axis=0, keepdims=True)
        for i in range(CONV_WIDTH - 1):
            shift = CONV_WIDTH - 1 - i
            up = jnp.where(rows < t_len - shift, pltpu.roll(d_pre, t_len - shift, axis=0), 0.0)
            dx = dx + up * w_ref[i:i + 1, :]
            dw_ref[i:i + 1, :] = jnp.sum(d_pre * taps[i], axis=0, keepdims=True)
        dx_ref[0] = dx

    return pl.pallas_call(
        body, name="gdn_prep_bwd",
        grid=(3 * GDN_HEADS,),
        in_specs=[_col_block(t_len, GDN_FIRST_BLOCK),
                  pl.BlockSpec((CONV_WIDTH, LANES), lambda j: (0, j)),
                  pl.BlockSpec((1, t_len, LANES), lambda j: (j // GDN_HEADS, 0, j % GDN_HEADS)), _HBM],
        out_specs=[pl.BlockSpec((1, t_len, LANES), lambda j: (DPROJ_GDN_SLOT + j // GDN_HEADS, 0, j % GDN_HEADS)),
                   pl.BlockSpec((CONV_WIDTH, LANES), lambda j: (0, j))],
        out_shape=[jax.ShapeDtypeStruct(dproj.shape, dproj.dtype),
                   jax.ShapeDtypeStruct((CONV_WIDTH, 3 * 512), F32)],
        input_output_aliases={3: 0},
        compiler_params=_params("arbitrary"),
    )(cols, conv_w, d_act3, dproj)


def _chunk_cumsum_matrix():
    r = lax.broadcasted_iota(jnp.int32, (LANES, LANES), 0)
    c = lax.broadcasted_iota(jnp.int32, (LANES, LANES), 1)
    return ((r <= c) & ((r // CHUNK) == (c // CHUNK))).astype(F32)


def _gdn_gates_call(ps, pst, alog_l, dtb_l, alog_c, dtb_c, t_len):
    def body(ps_ref, pst_ref, al_ref, dl_ref, ac_ref, dc_ref, beta_ref, gcol_ref, grow_ref):
        upper = _chunk_cumsum_matrix()
        lower = upper.T
        psv = ps_ref[...]
        beta_ref[...] = _sigmoid(psv)
        g_l = -jnp.exp(al_ref[...]) * _softplus(psv + dl_ref[...])
        g_r = -jnp.exp(ac_ref[...]) * _softplus(pst_ref[...] + dc_ref[...])
        for w in range(t_len // LANES):
            sl = slice(w * LANES, (w + 1) * LANES)
            gcol_ref[sl, :] = _mx(lower, g_l[sl, :])
            grow_ref[:, sl] = _mx(g_r[:, sl], upper)

    vm = pl.BlockSpec(memory_space=pltpu.VMEM)
    return pl.pallas_call(
        body, name="gdn_gates",
        in_specs=[vm] * 6, out_specs=[vm] * 3,
        out_shape=[jax.ShapeDtypeStruct((t_len, LANES), F32),
                   jax.ShapeDtypeStruct((t_len, LANES), F32),
                   jax.ShapeDtypeStruct((8, t_len), F32)],
        compiler_params=pltpu.CompilerParams(vmem_limit_bytes=VMEM_LIMIT_BYTES),
    )(ps, pst, alog_l, dtb_l, alog_c, dtb_c)


def _gdn_gates_bwd_call(ps, alog_l, dtb_l, d_l, t_len):
    def body(ps_ref, al_ref, dl_ref, d_ref, dps_ref, gal_ref, gdt_ref):
        lane = lax.broadcasted_iota(jnp.int32, (1, LANES), 1)
        psv = ps_ref[...]
        dv = d_ref[...]
        beta = _sigmoid(psv)
        ea = jnp.exp(al_ref[...])
        arg = psv + dl_ref[...]
        g = -ea * _softplus(arg)
        d_a = dv * (-ea) * _sigmoid(arg)
        is_a = (lane >= GDN_HEADS) & (lane < 2 * GDN_HEADS)
        dps_ref[...] = jnp.where(lane < GDN_HEADS, dv * beta * (1.0 - beta), jnp.where(is_a, d_a, 0.0))
        gdt_ref[...] = jnp.where(is_a, jnp.sum(d_a, axis=0, keepdims=True), 0.0)
        gal_ref[...] = jnp.where(is_a, jnp.sum(dv * g, axis=0, keepdims=True), 0.0)

    vm = pl.BlockSpec(memory_space=pltpu.VMEM)
    return pl.pallas_call(
        body, name="gdn_gates_bwd",
        in_specs=[vm] * 4, out_specs=[vm] * 3,
        out_shape=[jax.ShapeDtypeStruct((t_len, LANES), F32),
                   jax.ShapeDtypeStruct((1, LANES), F32),
                   jax.ShapeDtypeStruct((1, LANES), F32)],
        compiler_params=pltpu.CompilerParams(vmem_limit_bytes=VMEM_LIMIT_BYTES),
    )(ps, alog_l, dtb_l, d_l)


def _bm(a, b, right_low=True):
    return _m3_general(a, b, _BNN, right_low)


def _bm_nt(a, b, right_low=True):
    return _m3_general(a, b, _BNT, right_low)


def _bm_tn(a, b, right_low=True):
    return _m3_general(a, b, _BTN, right_low)


def _heads_of(ref, rows):
    return jnp.stack([ref[rows, h * GDN_HEAD_DIM:(h + 1) * GDN_HEAD_DIM] for h in range(GDN_HEADS)])


def _chunk_terms(q_ref, k_ref, v_ref, b_ref, gc_ref, gr_ref, c, incl, strict, n=1, scores=True):
    r0 = c * CHUNK if isinstance(c, int) else pl.multiple_of(c * CHUNK, CHUNK)
    rows = pl.ds(r0, n * CHUNK)
    per_chunk = lambda x: x.reshape(GDN_HEADS * n, CHUNK, x.shape[-1])
    q, k, v = (per_chunk(_heads_of(ref, rows)) for ref in (q_ref, k_ref, v_ref))
    lane_ids = lax.broadcasted_iota(jnp.int32, (1, LANES), 1)
    pick = lambda slab, first: jnp.stack([jnp.sum(jnp.where(lane_ids == first + h, slab, 0.0), axis=-1, keepdims=True)
                                          for h in range(GDN_HEADS)])
    b = per_chunk(pick(b_ref[rows, :], 0))
    gc = per_chunk(pick(gc_ref[rows, :], GDN_HEADS))
    gr = gr_ref[:, c] if n == 1 else gr_ref[:, c:c + n].reshape(GDN_HEADS * n, 1, CHUNK)
    dm = jnp.where(incl, jnp.exp(jnp.where(incl, gc - gr, 0.0)), 0.0)
    kb = k * b
    vb = v * b
    e = jnp.exp(gc)
    a = p = None
    if scores:
        kk_qk = _bm_nt(jnp.concatenate([kb, q], axis=1), k)
        a = jnp.where(strict, kk_qk[:, :CHUNK] * dm, 0.0)
        p = jnp.where(incl, kk_qk[:, CHUNK:] * dm, 0.0)
    gl = gc[:, CHUNK - 1:CHUNK, :]
    eg = jnp.exp(gl - gc)
    return rows, q, k, v, b, gc, dm, kb, vb, e, a, p, gl, eg


def _unit_lower_inverse(a, eye):
    x = -a
    tm = eye + x
    xp = _bm(x, x)
    for _ in range(4):
        both = _bm(jnp.concatenate([xp, tm], axis=1), xp)
        tm = tm + both[:, CHUNK:]
        xp = both[:, :CHUNK]
    return tm + _bm(tm, xp)


def _gdn_specs(t_len, n_chunks, reverse):
    cps = GDN_CHUNKS_PER_STEP
    steps = n_chunks // cps
    at = (lambda g: steps - 1 - g) if reverse else (lambda g: g)
    rows_blk = lambda width, part=0: pl.BlockSpec((cps * CHUNK, width), lambda g: (at(g), part))
    gate_r = pl.BlockSpec((GDN_HEADS, cps, 1, CHUNK), lambda g: (0, at(g), 0, 0))
    per_chunk = lambda r, c: pl.BlockSpec((GDN_HEADS, cps, r, c), lambda g: (0, at(g), 0, 0))
    return cps, steps, rows_blk, gate_r, per_chunk


def _gdn_fwd_call(gact, beta_c, gam_c, gam_r, t_len):
    n_chunks = t_len // CHUNK
    dk = GDN_HEAD_DIM
    width = GDN_HEADS * dk
    cps, steps, rows_blk, gate_r, per_chunk = _gdn_specs(t_len, n_chunks, False)

    def body(q_ref, k_ref, v_ref, b_ref, gc_ref, gr_ref, o_ref, s_ref, t_ref, a_ref, p_ref, uw_ref, vn_ref, state_ref):
        row = lax.broadcasted_iota(jnp.int32, (CHUNK, CHUNK), 0)
        col = lax.broadcasted_iota(jnp.int32, (CHUNK, CHUNK), 1)
        incl, strict = row >= col, row > col
        eye = (row == col).astype(F32)

        @pl.when(pl.program_id(0) == 0)
        def _():
            state_ref[...] = jnp.zeros_like(state_ref)

        _, q, k, v, b, gc, dm, kb, vb, e, a, p, gl, eg = _chunk_terms(
            q_ref, k_ref, v_ref, b_ref, gc_ref, gr_ref, 0, incl, strict, cps)
        tm = _unit_lower_inverse(a, eye)
        uw = _bm(tm, jnp.concatenate([vb, kb * e], axis=2))
        w_qe = jnp.concatenate([uw[:, :, dk:], q * e], axis=1)
        u, kd, decay = uw[:, :, :dk], k * eg, jnp.exp(gl)
        per_chunk_block = lambda x: x.reshape(GDN_HEADS, cps, CHUNK, CHUNK)
        t_ref[...], a_ref[...], p_ref[...] = per_chunk_block(tm), per_chunk_block(a), per_chunk_block(p)
        uw_heads = uw.reshape(GDN_HEADS, cps * CHUNK, 2 * dk)
        for h in range(GDN_HEADS):
            uw_ref[:, h * 2 * dk:(h + 1) * 2 * dk] = uw_heads[h]

        of_chunk = lambda x, c: jnp.stack([x[h * cps + c] for h in range(GDN_HEADS)])
        s = state_ref[...]
        for c in range(cps):
            ws_qs = _bm(of_chunk(w_qe, c), s)
            vn = of_chunk(u, c) - ws_qs[:, :CHUNK]
            o = ws_qs[:, CHUNK:] + _bm(of_chunk(p, c), vn)
            for h in range(GDN_HEADS):
                o_ref[c * CHUNK:(c + 1) * CHUNK, h * dk:(h + 1) * dk] = o[h]
                vn_ref[c * CHUNK:(c + 1) * CHUNK, h * dk:(h + 1) * dk] = vn[h]
            s_ref[:, c] = s
            s = s * of_chunk(decay, c) + _bm_tn(of_chunk(kd, c), vn)
        state_ref[...] = s

    scores = jax.ShapeDtypeStruct((GDN_HEADS, n_chunks, CHUNK, CHUNK), F32)
    return pl.pallas_call(
        body, name="gdn_fwd",
        grid=(steps,),
        in_specs=[rows_blk(width, 0), rows_blk(width, 1), rows_blk(width, 2), rows_blk(LANES), rows_blk(LANES), gate_r],
        out_specs=[rows_blk(width), per_chunk(dk, dk), per_chunk(CHUNK, CHUNK), per_chunk(CHUNK, CHUNK),
                   per_chunk(CHUNK, CHUNK), rows_blk(2 * width), rows_blk(width)],
        out_shape=[jax.ShapeDtypeStruct((t_len, width), F32),
                   jax.ShapeDtypeStruct((GDN_HEADS, n_chunks, dk, dk), F32), scores, scores, scores,
                   jax.ShapeDtypeStruct((t_len, 2 * width), F32), jax.ShapeDtypeStruct((t_len, width), F32)],
        scratch_shapes=[pltpu.VMEM((GDN_HEADS, dk, dk), F32)],
        compiler_params=_params("arbitrary"),
    )(gact, gact, gact, beta_c, gam_c, gam_r)


def _gdn_bwd_call(gact, beta_c, gam_c, gam_r, saved, d_o, t_len, scatter=()):
    n_chunks = t_len // CHUNK
    dk = GDN_HEAD_DIM
    width = GDN_HEADS * dk
    cps, steps, rows_blk, gate_r, per_chunk = _gdn_specs(t_len, n_chunks, True)
    nx = len(scatter)
    n_in = 13

    def body(*refs):
        q_ref, k_ref, v_ref, b_ref, gc_ref, gr_ref = refs[:6]
        saved_refs, do_ref = refs[6:12], refs[12]
        d_ref, dgate_ref = refs[n_in + nx:n_in + 2 + nx]
        dstate_ref = refs[n_in + 2 + 2 * nx]
        copies = lambda: _direct_copies(refs[n_in:n_in + nx], refs[n_in + 2 + nx:n_in + 2 + 2 * nx],
                                        *refs[n_in + 3 + 2 * nx:], (True,) * nx)
        if nx:
            pl.when(pl.program_id(0) == 0)(lambda: _start_all(copies()))
        row = lax.broadcasted_iota(jnp.int32, (CHUNK, CHUNK), 0)
        col = lax.broadcasted_iota(jnp.int32, (CHUNK, CHUNK), 1)
        incl, strict = row >= col, row > col
        ng = GDN_BWD_GROUP
        nb = GDN_HEADS * ng
        upper = jnp.broadcast_to((row <= col).astype(F32), (nb, CHUNK, CHUNK))
        ones = jnp.ones((nb, CHUNK, LANES), F32)
        last_row = lax.broadcasted_iota(jnp.int32, (CHUNK, 1), 0) == CHUNK - 1
        lane_ids = lax.broadcasted_iota(jnp.int32, (1, LANES), 1)
        rsum = lambda m: jnp.sum(m, axis=-1, keepdims=True)
        total = lambda m: jnp.sum(rsum(m), axis=1, keepdims=True)
        of_chunk = lambda x, c: jnp.stack([x[h * ng + c] for h in range(GDN_HEADS)])

        @pl.when(pl.program_id(0) == 0)
        def _():
            dstate_ref[...] = jnp.zeros_like(dstate_ref)

        for c0 in range(cps - ng, -1, -ng):
            group(c0, q_ref, k_ref, v_ref, b_ref, gc_ref, gr_ref, saved_refs, do_ref, d_ref, dgate_ref, dstate_ref,
                  incl, strict, upper, ones, last_row, lane_ids, rsum, total, of_chunk)
        if nx:
            pl.when(pl.program_id(0) == steps - 1)(lambda: _wait_all(copies()))

    def group(c0, q_ref, k_ref, v_ref, b_ref, gc_ref, gr_ref, saved_refs, do_ref, d_ref, dgate_ref, dstate_ref,
              incl, strict, upper, ones, last_row, lane_ids, rsum, total, of_chunk):
        ng = GDN_BWD_GROUP
        nb = GDN_HEADS * ng
        rows = pl.ds(c0 * CHUNK, ng * CHUNK)
        s_ref, t_ref, a_ref, p_ref, uw_ref, vn_ref = saved_refs
        _, q, k, v, b, gc, dm, kb, vb, e, _, _, gl, eg = _chunk_terms(
            q_ref, k_ref, v_ref, b_ref, gc_ref, gr_ref, c0, incl, strict, ng, scores=False)
        s = s_ref[:, c0:c0 + ng].reshape(nb, dk, dk)
        tm = t_ref[:, c0:c0 + ng].reshape(nb, CHUNK, CHUNK)
        a = a_ref[:, c0:c0 + ng].reshape(nb, CHUNK, CHUNK)
        p = p_ref[:, c0:c0 + ng].reshape(nb, CHUNK, CHUNK)
        d_out = _heads_of(do_ref, rows).reshape(nb, CHUNK, dk)
        vn = _heads_of(vn_ref, rows).reshape(nb, CHUNK, dk)
        uw = jnp.stack([uw_ref[rows, h * 2 * dk:(h + 1) * 2 * dk] for h in range(GDN_HEADS)]).reshape(nb, CHUNK, 2 * dk)
        u, w = uw[:, :, :dk], uw[:, :, dk:]
        el = jnp.exp(gl)
        kbe = kb * e
        qe = q * e
        kd = k * eg
        bm, bm_nt, bm_tn = (functools.partial(f, right_low=False) for f in (_bm, _bm_nt, _bm_tn))
        pt_do = _bm_tn(p, d_out)
        qet_do = _bm_tn(qe, d_out)

        ds = dstate_ref[...]
        d_vn_c, ds_c = [None] * ng, [None] * ng
        for c in range(ng - 1, -1, -1):
            ds_c[c] = ds
            d_vn_c[c] = of_chunk(pt_do, c) + bm(of_chunk(kd, c), ds)
            ds = of_chunk(el, c) * ds + of_chunk(qet_do, c) - bm_tn(of_chunk(w, c), d_vn_c[c])
        dstate_ref[...] = ds
        by_chunk = lambda xs: jnp.stack([xs[c][h] for h in range(GDN_HEADS) for c in range(ng)])
        d_vn, ds = by_chunk(d_vn_c), by_chunk(ds_c)

        on_s = bm_nt(jnp.concatenate([d_out, d_vn], axis=1), s)
        d_qe, d_w = on_s[:, :CHUNK], -on_s[:, CHUNK:]
        d_p = jnp.where(incl, _bm_nt(d_out, vn), 0.0)
        d_kd = _bm_nt(vn, ds)
        d_both = _bm_tn(tm, jnp.concatenate([d_vn, d_w], axis=2))
        d_vb, d_kbe = d_both[:, :, :dk], d_both[:, :, dk:]
        d_a = -jnp.where(strict, _bm_nt(d_both, uw), 0.0)
        m = d_a * dm
        n = d_p * dm
        on_k = bm(jnp.concatenate([m, n], axis=1), k)
        d_kb = on_k[:, :CHUNK] + d_kbe * e
        d_q = on_k[:, CHUNK:] + d_qe * e
        d_k = (bm_tn(jnp.concatenate([m, n], axis=1), jnp.concatenate([kb, q], axis=1))
               + d_kd * eg + b * d_kb)
        d_v = b * d_vb
        r = d_a * a + d_p * p
        kd_term = rsum(d_kd * kd)
        d_gl = total(ds * s) * el + jnp.sum(kd_term, axis=1, keepdims=True)
        d_gam = (rsum(r) - _times_exact(r, ones, _BTN)[:, :, 0:1] + rsum(d_qe * qe) + rsum(d_kbe * kbe) - kd_term
                 + jnp.where(last_row, d_gl, 0.0))
        d_beta = rsum(d_kb * k) + rsum(d_vb * v)
        d_g = _exact_times(upper, d_gam * ones, _BNN)[:, :, 0:1]
        per_head = lambda x: x.reshape(GDN_HEADS, ng * CHUNK, x.shape[-1])
        d_q, d_k, d_v, d_beta, d_g = (per_head(x) for x in (d_q, d_k, d_v, d_beta, d_g))
        gates = jnp.zeros((ng * CHUNK, LANES), F32)
        for h in range(GDN_HEADS):
            lanes = slice(h * dk, (h + 1) * dk)
            d_ref[0, rows, lanes] = d_q[h]
            d_ref[1, rows, lanes] = d_k[h]
            d_ref[2, rows, lanes] = d_v[h]
            gates = gates + (jnp.where(lane_ids == h, d_beta[h], 0.0)
                             + jnp.where(lane_ids == GDN_HEADS + h, d_g[h], 0.0))
        dgate_ref[rows, :] = gates

    d_spec = pl.BlockSpec((3, cps * CHUNK, width), lambda g: (0, steps - 1 - g, 0))
    return pl.pallas_call(
        body, name="gdn_bwd",
        grid=(steps,),
        in_specs=[rows_blk(width, 0), rows_blk(width, 1), rows_blk(width, 2), rows_blk(LANES), rows_blk(LANES), gate_r,
                  per_chunk(dk, dk), per_chunk(CHUNK, CHUNK), per_chunk(CHUNK, CHUNK), per_chunk(CHUNK, CHUNK),
                  rows_blk(2 * width), rows_blk(width), rows_blk(width)] + [_HBM] * nx,
        out_specs=[d_spec, rows_blk(LANES)] + [_HBM] * nx,
        out_shape=[jax.ShapeDtypeStruct((3, t_len, width), F32),
                   jax.ShapeDtypeStruct((t_len, LANES), F32)] + _direct_out_shapes(scatter, (True,) * nx),
        scratch_shapes=[pltpu.VMEM((GDN_HEADS, dk, dk), F32)] + (_direct_semaphores(nx) if nx else []),
        compiler_params=_params("arbitrary"),
    )(gact, gact, gact, beta_c, gam_c, gam_r, *saved, d_o, *scatter)


def _group_sums(x, group):
    rows, width = x.shape
    lane = lax.broadcasted_iota(jnp.int32, (1, LANES), 1)
    out = []
    for t in range(width // LANES):
        seg = x[:, t * LANES:(t + 1) * LANES]
        if group == LANES:
            out.append(jnp.broadcast_to(jnp.sum(seg, axis=-1, keepdims=True), (rows, LANES)))
        else:
            low = jnp.sum(jnp.where(lane < group, seg, 0.0), axis=-1, keepdims=True)
            high = jnp.sum(jnp.where(lane < group, 0.0, seg), axis=-1, keepdims=True)
            out.append(jnp.where(lane < group, low, high))
    return jnp.concatenate(out, axis=1)


def _post_call(o_sb, o_gd, proj_gates, x, target, w_out, sbw, gdw, fw, tm=256):
    t_len, d = x.shape
    half = 512
    sb_blocks = half // LANES

    def body(osb_ref, ogd_ref, zsb_ref, zgd_ref, x_ref, tg_ref, wo_ref, sbw_ref, gdw_ref, fw_ref,
             dx2_ref, dosb_ref, dogd_ref, dz_ref, loss_ref, gfw_ref, gsb_ref, ggd_ref, gwo_ref):
        step = pl.program_id(0)

        @pl.when(step == 0)
        def _():
            loss_ref[...] = jnp.zeros_like(loss_ref)
            gfw_ref[...] = jnp.zeros_like(gfw_ref)
            gsb_ref[...] = jnp.zeros_like(gsb_ref)
            ggd_ref[...] = jnp.zeros_like(ggd_ref)
            gwo_ref[...] = jnp.zeros_like(gwo_ref)

        def head_forward(o, z, w, head_dim):
            r = lax.rsqrt(_group_sums(o * o, head_dim) * (1.0 / head_dim) + EPS)
            nrm = o * r * w
            sg = _sigmoid(z)
            return r, nrm, sg, nrm * (z * sg)

        def head_backward(d_m, o, z, w, head_dim, r, nrm, sg):
            d_n = d_m * (z * sg)
            d_z = d_m * nrm * (sg * (1.0 + z * (1.0 - sg)))
            dnw = d_n * w
            d_o = r * dnw - o * (r * r * r) * (_group_sums(dnw * o, head_dim) * (1.0 / head_dim))
            return d_o, d_z, jnp.sum(d_n * o * r, axis=0, keepdims=True)

        osb = jnp.concatenate([osb_ref[j] for j in range(sb_blocks)], axis=1)
        ogd, zsb, zgd = ogd_ref[...], zsb_ref[...], zgd_ref[...]
        sbw_v, gdw_v = sbw_ref[...], gdw_ref[...]
        r_sb, n_sb, sg_sb, m_sb = head_forward(osb, zsb, sbw_v, SB_HEAD_DIM)
        r_gd, n_gd, sg_gd, m_gd = head_forward(ogd, zgd, gdw_v, GDN_HEAD_DIM)
        mixed = jnp.concatenate([m_sb, m_gd], axis=1).astype(MXU_DTYPE)
        wo = wo_ref[...]
        x2 = x_ref[...] + jnp.dot(mixed, wo, preferred_element_type=F32)
        r2 = lax.rsqrt(jnp.mean(x2 * x2, axis=-1, keepdims=True) + EPS)
        fw_v = fw_ref[...]
        err = x2 * r2 * fw_v - tg_ref[...]
        loss_ref[...] += 0.5 * jnp.sum(jnp.sum(err * err, axis=-1, keepdims=True) * (1.0 / d))
        dy = err * (1.0 / d)
        gg = dy * fw_v
        dx2 = r2 * gg - x2 * ((r2 * r2 * r2) * jnp.mean(gg * x2, axis=-1, keepdims=True))
        gfw_ref[...] += jnp.sum(dy * x2 * r2, axis=0, keepdims=True)
        dx2_ref[...] = dx2
        dx2b = dx2.astype(MXU_DTYPE)
        d_mixed = lax.dot_general(dx2b, wo, _NT, preferred_element_type=F32)
        gwo_ref[...] += lax.dot_general(mixed, dx2b, _TN, preferred_element_type=F32)
        d_osb, d_zsb, gsb = head_backward(d_mixed[:, :half], osb, zsb, sbw_v, SB_HEAD_DIM, r_sb, n_sb, sg_sb)
        d_ogd, d_zgd, ggd = head_backward(d_mixed[:, half:], ogd, zgd, gdw_v, GDN_HEAD_DIM, r_gd, n_gd, sg_gd)
        for j in range(sb_blocks):
            dosb_ref[j] = d_osb[:, j * LANES:(j + 1) * LANES]
        dogd_ref[...] = d_ogd
        dz_ref[0] = d_zsb
        dz_ref[1] = d_zgd
        gsb_ref[...] += gsb
        ggd_ref[...] += ggd

    row_blk = lambda w: pl.BlockSpec((tm, w), lambda i: (i, 0))
    blocks_blk = pl.BlockSpec((sb_blocks, tm, LANES), lambda i: (0, i, 0))
    fixed = lambda r, w: pl.BlockSpec((r, w), lambda i: (0, 0))
    return pl.pallas_call(
        body, name="post",
        grid=(t_len // tm,),
        in_specs=[blocks_blk, row_blk(half),
                  pl.BlockSpec((tm, half), lambda i: (i, 0)),
                  pl.BlockSpec((tm, half), lambda i: (i, 1)),
                  row_blk(d), row_blk(d), fixed(d, d), fixed(1, half), fixed(1, half), fixed(1, d)],
        out_specs=[row_blk(d), blocks_blk, row_blk(half),
                   pl.BlockSpec((2, tm, half), lambda i: (DPROJ_GATE_SLOT // 2, i, 0)),
                   fixed(1, LANES), fixed(1, d), fixed(1, half), fixed(1, half), fixed(d, d)],
        out_shape=[jax.ShapeDtypeStruct((t_len, d), F32), jax.ShapeDtypeStruct((sb_blocks, t_len, LANES), F32),
                   jax.ShapeDtypeStruct((t_len, half), F32),
                   jax.ShapeDtypeStruct((len(DPROJ_PIECE_OF_SLOT), t_len, half), F32),
                     jax.ShapeDtypeStruct((1, LANES), F32), jax.ShapeDtypeStruct((1, d), F32),
                     jax.ShapeDtypeStruct((1, half), F32), jax.ShapeDtypeStruct((1, half), F32),
                     jax.ShapeDtypeStruct((d, d), F32)],
        compiler_params=_params("arbitrary"),
    )(o_sb, o_gd, proj_gates, proj_gates, x, target, w_out, sbw, gdw, fw)


def _piece_of_slot(s):
    return jnp.where(s < DPROJ_GDN_SLOT, s, jnp.where(s < DPROJ_GATE_SLOT, s + 1,
                                                     jnp.where(s == DPROJ_GATE_SLOT, 3, 7)))


def _gw_in_call(h_t, dproj8):
    d, t_len = h_t.shape
    n_piece, _, pw = dproj8.shape

    def body(ht_ref, dp_ref, gw_ref):
        gw_ref[...] = jnp.dot(ht_ref[...], dp_ref[0].astype(MXU_DTYPE), preferred_element_type=F32)

    return pl.pallas_call(
        body, name="gw_in",
        grid=(n_piece,),
        in_specs=[pl.BlockSpec((d, t_len), lambda s: (0, 0)),
                  pl.BlockSpec((1, t_len, pw), lambda s: (s, 0, 0))],
        out_specs=pl.BlockSpec((d, pw), lambda s: (0, _piece_of_slot(s))),
        out_shape=jax.ShapeDtypeStruct((d, n_piece * pw), F32),
        compiler_params=_params("arbitrary"),
    )(h_t, dproj8)


def _slot_of_piece(p):
    return jnp.where(p < DPROJ_GDN_SLOT, p, jnp.where(p == 3, DPROJ_GATE_SLOT, jnp.where(p < 7, p - 1, 7)))


def _gw_in_shards_call(h_t, dproj8, dsmall, out_dtype):
    d, t_len = h_t.shape
    n_piece, _, pw = dproj8.shape
    ns = dsmall.shape[1]
    n_pairs = N_DEV // 2

    def body(ht_ref, dp_ref, ds_ref, chip_ref, prev_ref, gates_ref, send_ref, recv_ref, send_sems, recv_sems):
        p = pl.program_id(0)
        x_pos, y_pos, c = lax.axis_index("x"), lax.axis_index("y"), lax.axis_index("c")
        to_sibling = lambda pair: pltpu.make_async_remote_copy(
            src_ref=send_ref.at[pair], dst_ref=recv_ref.at[pair], send_sem=send_sems.at[pair],
            recv_sem=recv_sems.at[pair], device_id=(x_pos, y_pos, 1 - c), device_id_type=_MESH)

        @pl.when(p == 0)
        def _():
            gates_ref[...] = jnp.dot(ht_ref[...], ds_ref[...].astype(MXU_DTYPE), preferred_element_type=F32)

        def emit(s, tail):
            x = jnp.concatenate([prev_ref[...], tail], axis=1)
            y = x if s == 0 else pltpu.roll(x, SHARD_PAD - s, axis=1)
            shard = y[:, :SHARD_COLS].astype(out_dtype)

            @pl.when(c == s % 2)
            def _():
                chip_ref[s // 2] = shard

            @pl.when(c != s % 2)
            def _():
                send_ref[s // 2] = shard
                to_sibling(s // 2).start()

        @pl.when(p < n_piece)
        def _():
            cur = jnp.dot(ht_ref[...], dp_ref[0].astype(MXU_DTYPE), preferred_element_type=F32)
            for s in range(n_piece - 1):
                pl.when(p == s + 1)(functools.partial(emit, s, cur[:, :SHARD_PAD - pw]))
            prev_ref[...] = cur

        @pl.when(p == n_piece)
        def _():
            emit(n_piece - 1, gates_ref[...])
            for pair in range(n_pairs):
                to_sibling(pair).wait_send()
            for pair in range(n_pairs):
                to_sibling(pair).wait_recv()
                chip_ref[pair] = (chip_ref[pair].astype(F32) + recv_ref[pair].astype(F32)).astype(out_dtype)

    shards_of_side = lambda: pltpu.VMEM((n_pairs, d, SHARD_COLS), out_dtype)
    return pl.pallas_call(
        body, name="gw_in",
        grid=(n_piece + 1,),
        in_specs=[pl.BlockSpec((d, t_len), lambda p: (0, 0)),
                  pl.BlockSpec((1, t_len, pw), lambda p: (_slot_of_piece(jnp.minimum(p, n_piece - 1)), 0, 0)),
                  pl.BlockSpec((t_len, ns), lambda p: (0, 0))],
        out_specs=pl.BlockSpec((n_pairs, d, SHARD_COLS), lambda p: (0, 0, 0)),
        out_shape=jax.ShapeDtypeStruct((n_pairs, d, SHARD_COLS), out_dtype),
        scratch_shapes=[pltpu.VMEM((d, pw), F32), pltpu.VMEM((d, ns), F32), shards_of_side(), shards_of_side(),
                        pltpu.SemaphoreType.DMA((n_pairs,)), pltpu.SemaphoreType.DMA((n_pairs,))],
        compiler_params=_params("arbitrary"),
    )(h_t, dproj8, dsmall)


def _gw_small_call(h_t, dsmall, tm=512):
    d, t_len = h_t.shape
    ns = dsmall.shape[1]

    def body(ht_ref, dp_ref, gw_ref):
        @pl.when(pl.program_id(0) == 0)
        def _():
            gw_ref[...] = jnp.zeros_like(gw_ref)

        gw_ref[...] += jnp.dot(ht_ref[...], dp_ref[...].astype(MXU_DTYPE), preferred_element_type=F32)

    return pl.pallas_call(
        body, name="gw_small",
        grid=(t_len // tm,),
        in_specs=[pl.BlockSpec((d, tm), lambda t: (0, t)),
                  pl.BlockSpec((tm, ns), lambda t: (t, 0))],
        out_specs=pl.BlockSpec((d, ns), lambda t: (0, 0)),
        out_shape=jax.ShapeDtypeStruct((d, ns), F32),
        compiler_params=_params("arbitrary"),
    )(h_t, dsmall)


def _dx_call(dproj8, dsmall, w_main, w_small, x, r, dx2, norm_w, chip_scatter=(), peer_scatter=(), tm=256):
    t_len, d = x.shape
    n_piece, _, pw = dproj8.shape
    ns = dsmall.shape[1]
    nx = len(chip_scatter)
    n_peer = len(peer_scatter)
    steps = t_len // tm
    n_in = 8 + nx + n_peer
    n_out = 2 + nx + n_peer

    def body(*refs):
        dp_ref, ds_ref, wm_ref, ws_ref, x_ref, r_ref, dx2_ref, nw_ref = refs[:8]
        gx_ref, gnw_ref = refs[n_in:n_in + 2]
        scratch = refs[n_in + n_out:]
        copies = lambda: _chip_copies(refs[8:8 + nx], refs[n_in + 2:n_in + 2 + nx], *scratch[:3])
        if nx:
            pl.when(pl.program_id(0) == 0)(lambda: _start_all(copies()))

        @pl.when(pl.program_id(0) == 0)
        def _():
            gnw_ref[...] = jnp.zeros_like(gnw_ref)

        dh = lax.dot_general(ds_ref[...].astype(MXU_DTYPE), ws_ref[...], _NT, preferred_element_type=F32)
        for s, p in enumerate(DPROJ_PIECE_OF_SLOT):
            dh = dh + lax.dot_general(dp_ref[s].astype(MXU_DTYPE), wm_ref[:, p * pw:(p + 1) * pw], _NT,
                                      preferred_element_type=F32)
        xv, rv = x_ref[...], r_ref[...]
        dn = dh * nw_ref[...]
        gx_ref[...] = dx2_ref[...] + rv * dn - xv * ((rv * rv * rv) * jnp.mean(dn * xv, axis=-1, keepdims=True))
        gnw_ref[...] += jnp.sum(dh * xv * rv, axis=0, keepdims=True)

        @pl.when(pl.program_id(0) == steps - 1)
        def _():
            if n_peer:
                small_ref, parts_ref = refs[8 + nx:n_in]
                small_buf = scratch[6]
                small_buf[...] = small_ref[...]
                small_buf[0:1, :] = gnw_ref[...]
                peer_copies = _direct_copies([small_buf, parts_ref], refs[n_in + 2 + nx:n_in + n_out], *scratch[3:6],
                                             [False, True])
                _start_all(peer_copies)
            if nx:
                _wait_all(copies())
            if n_peer:
                _wait_all(peer_copies)

    assert n_peer in (0, 2) and (nx == 1 or not n_peer)
    peer_in_specs = [pl.BlockSpec(peer_scatter[0].shape, lambda i: (0, 0)), _HBM] if n_peer else []
    peer_scratch = _direct_semaphores(n_peer) + [pltpu.VMEM(peer_scatter[0].shape, F32)] if n_peer else []
    return pl.pallas_call(
        body, name="dx",
        grid=(steps,),
        in_specs=[pl.BlockSpec((n_piece, tm, pw), lambda i: (0, i, 0)),
                  pl.BlockSpec((tm, ns), lambda i: (i, 0)),
                  pl.BlockSpec((d, n_piece * pw), lambda i: (0, 0)),
                  pl.BlockSpec((d, ns), lambda i: (0, 0)),
                  pl.BlockSpec((tm, d), lambda i: (i, 0)),
                  pl.BlockSpec((tm, 1), lambda i: (i, 0)),
                  pl.BlockSpec((tm, d), lambda i: (i, 0)),
                  pl.BlockSpec((1, d), lambda i: (0, 0))] + [_HBM] * nx + peer_in_specs,
        out_specs=[pl.BlockSpec((tm, d), lambda i: (i, 0)),
                   pl.BlockSpec((1, d), lambda i: (0, 0))] + [_HBM] * (nx + n_peer),
        out_shape=[jax.ShapeDtypeStruct((t_len, d), F32), jax.ShapeDtypeStruct((1, d), F32)]
                  + [jax.ShapeDtypeStruct(a.shape, a.dtype) for a in chip_scatter]
                  + (_direct_out_shapes(peer_scatter, [False, True]) if n_peer else []),
        scratch_shapes=(_chip_semaphores(nx) if nx else []) + peer_scratch,
        compiler_params=_params("arbitrary"),
    )(dproj8, dsmall, w_main, w_small, x, r, dx2, norm_w, *chip_scatter, *peer_scatter)


def _direct_out_shapes(srcs, per_peer):
    return [jax.ShapeDtypeStruct(s.shape if pp else (N_DEV,) + s.shape, s.dtype) for s, pp in zip(srcs, per_peer)]


def _direct_semaphores(n):
    return [pltpu.SemaphoreType.DMA((n * (N_DEV - 1),)), pltpu.SemaphoreType.DMA((n * (N_DEV - 1),)),
            pltpu.SemaphoreType.DMA((n,))]


def _direct_copies(src_refs, out_refs, send_sems, recv_sems, local_sems, per_peer):
    x, y, c = lax.axis_index("x"), lax.axis_index("y"), lax.axis_index("c")
    me = 4 * x + 2 * y + c
    local, remote = [], []
    for a in range(len(src_refs)):
        mine = src_refs[a].at[me] if per_peer[a] else src_refs[a]
        local.append(pltpu.make_async_copy(mine, out_refs[a].at[me], local_sems.at[a]))
    for k in range(1, N_DEV):
        kx, ky, kc = (k >> 2) & 1, (k >> 1) & 1, k & 1
        px = 1 - x if kx else x
        py = 1 - y if ky else y
        pc = 1 - c if kc else c
        peer = 4 * px + 2 * py + pc
        for a in range(len(src_refs)):
            sem = a * (N_DEV - 1) + (k - 1)
            remote.append(pltpu.make_async_remote_copy(
                src_ref=src_refs[a].at[peer] if per_peer[a] else src_refs[a], dst_ref=out_refs[a].at[me],
                send_sem=send_sems.at[sem], recv_sem=recv_sems.at[sem],
                device_id=(px, py, pc), device_id_type=pl.DeviceIdType.MESH))
    return local, remote


def _start_all(copies):
    local, remote = copies
    for cp in local + remote:
        cp.start()


def _wait_all(copies):
    local, remote = copies
    for cp in remote:
        cp.wait_send()
    for cp in remote:
        cp.wait_recv()
    for cp in local:
        cp.wait()


N_CHIPS = 4
_HBM = pl.BlockSpec(memory_space=pl.ANY)
_MESH = pl.DeviceIdType.MESH


def _gather_call(name, srcs):
    n = len(srcs)
    per = N_DEV - 1

    def body(*refs):
        src_refs, out_refs = refs[:n], refs[n:2 * n]
        send_sems, recv_sems, local_sems = refs[2 * n:]
        x, y, c = lax.axis_index("x"), lax.axis_index("y"), lax.axis_index("c")
        me, sibling = (x, y, c), (x, y, 1 - c)
        x_nbr, y_nbr, diagonal = (1 - x, y), (x, 1 - y), (1 - x, 1 - y)
        held = ((1 - x) * c + x * (1 - c), y * c + (1 - y) * (1 - c))
        onward = (x * c + (1 - x) * (1 - c), (1 - y) * c + y * (1 - c))
        slot = lambda px, py, pc: 4 * px + 2 * py + pc

        def copy(a, k, block, to, from_src=False):
            rows = out_refs[a].at[slot(*block)]
            return pltpu.make_async_remote_copy(
                src_ref=src_refs[a] if from_src else rows, dst_ref=rows,
                send_sem=send_sems.at[a * per + k], recv_sem=recv_sems.at[a * per + k],
                device_id=to, device_id_type=_MESH)

        local = [pltpu.make_async_copy(src_refs[a], out_refs[a].at[slot(*me)], local_sems.at[a]) for a in range(n)]
        started = []

        def start(cp):
            cp.start()
            started.append(cp)

        for cp in local:
            cp.start()
        for a in range(n):
            start(copy(a, 0, me, sibling, True))
            start(copy(a, 1, me, (*x_nbr, c), True))
            start(copy(a, 2, me, (*y_nbr, c), True))
        for a in range(n):
            copy(a, 1, (*x_nbr, c), me).wait_recv()
            copy(a, 2, (*y_nbr, c), me).wait_recv()
            start(copy(a, 3, (*held, c), (*onward, c)))
            start(copy(a, 4, (*x_nbr, c), sibling))
            start(copy(a, 5, (*y_nbr, c), sibling))
        for a in range(n):
            copy(a, 3, (*diagonal, c), me).wait_recv()
            start(copy(a, 6, (*diagonal, c), sibling))
        for a in range(n):
            copy(a, 0, sibling, me).wait_recv()
            for k, chip in ((4, x_nbr), (5, y_nbr), (6, diagonal)):
                copy(a, k, (*chip, 1 - c), me).wait_recv()
        for cp in started:
            cp.wait_send()
        for cp in local:
            cp.wait()

    return pl.pallas_call(
        body, name=name,
        in_specs=[_HBM] * n, out_specs=[_HBM] * n,
        out_shape=[jax.ShapeDtypeStruct((N_DEV,) + s.shape, s.dtype) for s in srcs],
        scratch_shapes=[pltpu.SemaphoreType.DMA((n * per,)), pltpu.SemaphoreType.DMA((n * per,)),
                        pltpu.SemaphoreType.DMA((n,))],
    )(*srcs)


def _chip_semaphores(n):
    per = N_CHIPS - 1
    return [pltpu.SemaphoreType.DMA((n * per,)), pltpu.SemaphoreType.DMA((n * per,)), pltpu.SemaphoreType.DMA((n,))]


def _chip_copies(src_refs, out_refs, send_sems, recv_sems, local_sems):
    per = N_CHIPS - 1
    x, y, c = lax.axis_index("x"), lax.axis_index("y"), lax.axis_index("c")
    mine = 2 * x + y
    chips = [(1 - x, y), (x, 1 - y), (1 - x, 1 - y)]
    n = len(src_refs)
    local = [pltpu.make_async_copy(src_refs[a].at[mine], out_refs[a].at[mine], local_sems.at[a]) for a in range(n)]
    remote = []
    for a in range(n):
        for j, (px, py) in enumerate(chips):
            remote.append(pltpu.make_async_remote_copy(
                src_ref=src_refs[a].at[2 * px + py], dst_ref=out_refs[a].at[mine],
                send_sem=send_sems.at[a * per + j], recv_sem=recv_sems.at[a * per + j],
                device_id=(px, py, c), device_id_type=_MESH))
    return local, remote


def _adam_call(name, parts, w, m, v, tr):
    rows, cols = w.shape
    n_slots = parts.shape[0]

    def body(p_ref, w_ref, m_ref, v_ref, g_ref, d_ref, nm_ref, nv_ref):
        g = p_ref[0].astype(F32)
        for s in range(1, n_slots):
            g = g + p_ref[s].astype(F32)
        m_new = ADAM_B1 * m_ref[...] + (1.0 - ADAM_B1) * g
        v_new = ADAM_B2 * v_ref[...] + (1.0 - ADAM_B2) * (g * g)
        m_hat = m_new / (1.0 - ADAM_B1 ** ADAM_STEP)
        v_hat = v_new / (1.0 - ADAM_B2 ** ADAM_STEP)
        g_ref[...] = g
        d_ref[...] = -ADAM_LR * (m_hat / (jnp.sqrt(v_hat) + ADAM_EPS) + ADAM_WD * w_ref[...])
        nm_ref[...] = m_new
        nv_ref[...] = v_new

    blk = pl.BlockSpec((tr, cols), lambda i: (i, 0))
    return pl.pallas_call(
        body, name=name,
        grid=(rows // tr,),
        in_specs=[pl.BlockSpec((n_slots, tr, cols), lambda i: (0, i, 0)), blk, blk, blk],
        out_specs=[blk] * 4,
        out_shape=[jax.ShapeDtypeStruct((rows, cols), F32)] * 4,
        compiler_params=_params("arbitrary"),
    )(parts, w, m, v)


def _columns_to_rows_call(name, w_t, rows, dtype):
    row_tiles = rows // LANES
    cols = w_t.shape[0] // row_tiles
    whole = cols // LANES * LANES

    def body(w_ref, out_ref):
        diagonal = (lax.broadcasted_iota(jnp.int32, (LANES, LANES), 0)
                    == lax.broadcasted_iota(jnp.int32, (LANES, LANES), 1))
        for a in range(row_tiles):
            out_ref[a * LANES:(a + 1) * LANES, :whole] = (
                w_ref[pl.ds(a, whole, stride=row_tiles), :].T.astype(dtype))
            for c in range(whole, cols):
                column = w_ref[pl.ds(c * row_tiles + a, 1), :]
                upright = jnp.sum(jnp.where(diagonal, column, 0.0), axis=1, keepdims=True)
                out_ref[a * LANES:(a + 1) * LANES, c:c + 1] = upright.astype(dtype)

    vm = pl.BlockSpec(memory_space=pltpu.VMEM)
    return pl.pallas_call(
        body, name=name,
        in_specs=[vm], out_specs=vm,
        out_shape=jax.ShapeDtypeStruct((rows, cols), dtype),
        compiler_params=pltpu.CompilerParams(vmem_limit_bytes=VMEM_LIMIT_BYTES),
    )(w_t)


def _adam_columns_call(name, parts, w_t, m_t, v_t):
    n_slots, rows, cols = parts.shape
    row_tiles = rows // LANES
    cols_pad = -(-cols // LANES) * LANES

    def body(p_ref, w_ref, m_ref, v_ref, *out_refs):
        for a in range(row_tiles):
            g = p_ref[0, a * LANES:(a + 1) * LANES, :].astype(F32)
            for s in range(1, n_slots):
                g = g + p_ref[s, a * LANES:(a + 1) * LANES, :].astype(F32)
            g = jnp.concatenate([g, jnp.zeros((LANES, cols_pad - cols), F32)], axis=1).T[:cols]
            column_rows = pl.ds(a, cols, stride=row_tiles)
            results = (g,) + _adamw(g, w_ref[column_rows, :], m_ref[column_rows, :], v_ref[column_rows, :])
            for out_ref, val in zip(out_refs, results):
                out_ref[column_rows, :] = val

    vm = pl.BlockSpec(memory_space=pltpu.VMEM)
    return pl.pallas_call(
        body, name=name,
        in_specs=[vm] * 4, out_specs=[vm] * 4,
        out_shape=[jax.ShapeDtypeStruct(w_t.shape, F32)] * 4,
        compiler_params=pltpu.CompilerParams(vmem_limit_bytes=VMEM_LIMIT_BYTES),
    )(parts, w_t, m_t, v_t)


N_PIECES = 8
PIECE = 512
SHARD_COLS = 513
SHARD_PAD = 640
RELAYOUT_ROWS = 256


def _from_shards_call(shards):
    _, d, _ = shards.shape
    tr = RELAYOUT_ROWS

    def body(p_ref, m_ref, s_ref):
        lane = lax.broadcasted_iota(jnp.int32, (tr, SHARD_PAD), 1)
        pad = jnp.zeros((tr, SHARD_PAD - SHARD_COLS), p_ref.dtype)
        sh = [jnp.concatenate([p_ref[s], pad], axis=1) for s in range(N_DEV)]
        for p in range(N_PIECES):
            y = sh[p] if p == 0 else pltpu.roll(sh[p], p, axis=1)
            if p > 0:
                y = jnp.where(lane < p, pltpu.roll(sh[p - 1], SHARD_PAD - (SHARD_COLS - p), axis=1), y)
            m_ref[:, p * PIECE:(p + 1) * PIECE] = y[:, :PIECE].astype(m_ref.dtype)
        first_gate = N_PIECES * PIECE - (N_DEV - 1) * SHARD_COLS
        s_ref[...] = pltpu.roll(sh[N_DEV - 1], SHARD_PAD - first_gate, axis=1)[:, :LANES].astype(s_ref.dtype)

    return pl.pallas_call(
        body, name="w_in_from_shards",
        grid=(d // tr,),
        in_specs=[pl.BlockSpec((N_DEV, tr, SHARD_COLS), lambda i: (0, i, 0))],
        out_specs=[pl.BlockSpec((tr, N_PIECES * PIECE), lambda i: (i, 0)), pl.BlockSpec((tr, LANES), lambda i: (i, 0))],
        out_shape=[jax.ShapeDtypeStruct((d, N_PIECES * PIECE), shards.dtype),
                   jax.ShapeDtypeStruct((d, LANES), shards.dtype)],
        compiler_params=_params("arbitrary"),
    )(shards)


def _adamw(g, w, m, v):
    m_new = ADAM_B1 * m + (1.0 - ADAM_B1) * g
    v_new = ADAM_B2 * v + (1.0 - ADAM_B2) * (g * g)
    m_hat = m_new / (1.0 - ADAM_B1 ** ADAM_STEP)
    v_hat = v_new / (1.0 - ADAM_B2 ** ADAM_STEP)
    return -ADAM_LR * (m_hat / (jnp.sqrt(v_hat) + ADAM_EPS) + ADAM_WD * w), m_new, v_new


def _adam_small_call(parts, ws, ms, vs):
    n = len(ws)
    n_slots = parts.shape[0]

    def body(*refs):
        p_ref = refs[0]
        w_refs, m_refs, v_refs = refs[1:1 + n], refs[1 + n:1 + 2 * n], refs[1 + 2 * n:1 + 3 * n]
        loss_ref = refs[1 + 3 * n]
        outs = refs[2 + 3 * n:]
        g_all = p_ref[0]
        for s in range(1, n_slots):
            g_all = g_all + p_ref[s]
        loss_ref[...] = g_all[n:n + 1, 0:1]
        for r in range(n):
            size = w_refs[r].shape[1]
            g = g_all[r:r + 1, :size]
            delta, m_new, v_new = _adamw(g, w_refs[r][...], m_refs[r][...], v_refs[r][...])
            for kind, val in enumerate((g, delta, m_new, v_new)):
                outs[kind * n + r][...] = val

    vm = pl.BlockSpec(memory_space=pltpu.VMEM)
    shapes = [jax.ShapeDtypeStruct(w.shape, F32) for w in ws]
    return pl.pallas_call(
        body, name="adam_small",
        in_specs=[vm] * (1 + 3 * n), out_specs=[vm] * (1 + 4 * n),
        out_shape=[jax.ShapeDtypeStruct((1, 1), F32)] + shapes * 4,
    )(parts, *ws, *ms, *vs)


_SMALL_ROWS = ("norm1_w", "final_norm_w", "sb_norm_w", "gdn_norm_w", "gdn_A_log", "gdn_dt_bias", "loss")


def _pack_small(vals, width):
    rows = [jnp.pad(a.reshape(1, -1).astype(F32), ((0, 0), (0, width - a.size))) for a in vals]
    rows += [jnp.zeros((1, width), F32)] * (8 - len(rows))
    return jnp.concatenate(rows, axis=0)


def _device_step(x2d, tgt, w_main, w_small, w_out_full, conv_full, norm1_w, sb_norm_w, gdn_A_log, gdn_dt_bias,
                 gdn_norm_w, final_norm_w, distributed=False):
    t_len, d = x2d.shape
    n_chunks = t_len // CHUNK
    w_main, w_small, w_out_full = (a.astype(MXU_DTYPE) for a in (w_main, w_small, w_out_full))
    w_small_t = w_small[:, :2 * GDN_HEADS].T

    pad_lanes = lambda a, lo: jnp.pad(a.reshape(1, -1), ((0, 0), (lo, LANES - lo - a.size)))
    alog_l, dtb_l = pad_lanes(gdn_A_log, GDN_HEADS), pad_lanes(gdn_dt_bias, GDN_HEADS)
    alog_c, dtb_c = alog_l[:, :8].T, dtb_l[:, :8].T
    sbw = jnp.tile(sb_norm_w, (1, 512 // SB_HEAD_DIM))
    gdw = jnp.tile(gdn_norm_w, (1, 512 // GDN_HEAD_DIM))
    fw = final_norm_w.reshape(1, d)

    if distributed:
        proj_cols, proj_gates, ps, pst, h_t, r1, w_out_g, conv_g = _inproj_call(
            x2d, norm1_w, w_main, w_small, w_small_t, gather=(w_out_full, conv_full))
        w_out_full = w_out_g.reshape(d, d)
        conv_full = conv_g.transpose(1, 0, 2).reshape(CONV_WIDTH, N_DEV * conv_g.shape[2])
    else:
        proj_cols, proj_gates, ps, pst, h_t, r1 = _inproj_call(x2d, norm1_w, w_main, w_small, w_small_t)
    o_sb, sp_total, sb_blocks_run = _sb_fwd_call(proj_cols, t_len)
    gact = _gdn_prep_call(proj_cols, conv_full, t_len, after=sp_total)
    beta_l, gcol_l, grow = _gdn_gates_call(ps, pst, alog_l, dtb_l, alog_c, dtb_c, t_len)
    gam_r = grow[GDN_HEADS:2 * GDN_HEADS].reshape(GDN_HEADS, n_chunks, 1, CHUNK)
    o_gd, *gdn_saved = _gdn_fwd_call(gact, beta_l, gcol_l, gam_r, t_len)

    (dx2, d_osb, d_ogd, dproj8, loss_p, g_fw, g_sbw, g_gdw, g_wout) = _post_call(
        o_sb, o_gd, proj_gates, x2d, tgt, w_out_full, sbw, gdw, fw)

    dproj8 = _sb_bwd_call(proj_cols, sp_total, sb_blocks_run, d_osb, dproj8, t_len)
    if distributed:
        d_gact3, d_gates, g_wout = _gdn_bwd_call(gact, beta_l, gcol_l, gam_r, gdn_saved, d_ogd, t_len,
                                                 scatter=(g_wout.reshape(N_DEV, d // N_DEV, d),))
    else:
        d_gact3, d_gates = _gdn_bwd_call(gact, beta_l, gcol_l, gam_r, gdn_saved, d_ogd, t_len)
    dproj8, g_conv = _gdn_prep_bwd_call(proj_cols, conv_full, d_gact3, dproj8, t_len)
    dsmall, g_alog, g_dtb = _gdn_gates_bwd_call(ps, alog_l, dtb_l, d_gates, t_len)

    if distributed:
        chip_partials = _gw_in_shards_call(h_t, dproj8, dsmall, WIRE_DTYPE)
        fold = lambda a, group: a.reshape(-1, group).sum(axis=0)
        small_g = _pack_small([jnp.zeros((d,), F32), g_fw, fold(g_sbw, SB_HEAD_DIM), fold(g_gdw, GDN_HEAD_DIM),
                               g_alog[0, GDN_HEADS:2 * GDN_HEADS], g_dtb[0, GDN_HEADS:2 * GDN_HEADS],
                               loss_p[0, :1]], d)
        conv_cols = g_conv.shape[1] // N_DEV
        g_conv_parts = g_conv.reshape(CONV_WIDTH, N_DEV, conv_cols).transpose(1, 0, 2)
        grad_x, _, g_w_in, p_small, p_conv = _dx_call(dproj8, dsmall, w_main, w_small, x2d, r1, dx2, norm1_w,
                                                      chip_scatter=(chip_partials,),
                                                      peer_scatter=(small_g, g_conv_parts))
        return grad_x, g_w_in, g_wout, p_small, p_conv
    else:
        grad_x, g_n1 = _dx_call(dproj8, dsmall, w_main, w_small, x2d, r1, dx2, norm1_w)
        g_w_in = (_gw_in_call(h_t, dproj8), _gw_small_call(h_t, dsmall))
    return (loss_p, grad_x, g_n1, g_w_in, g_sbw, g_conv, g_alog, g_dtb, g_gdw, g_wout, g_fw)


def kernel(x, norm1_w, w_in, sb_norm_w, gdn_conv_w, gdn_A_log, gdn_dt_bias, gdn_norm_w, w_out, final_norm_w, loss_target, m_norm1_w, m_w_in, m_sb_norm_w, m_gdn_conv_w, m_gdn_A_log, m_gdn_dt_bias, m_gdn_norm_w, m_w_out, m_final_norm_w, v_norm1_w, v_w_in, v_sb_norm_w, v_gdn_conv_w, v_gdn_A_log, v_gdn_dt_bias, v_gdn_norm_w, v_w_out, v_final_norm_w):
    d = x.shape[2]
    shard_cols = w_in.shape[2]

    columns = lambda a: a.transpose(2, 0, 1).reshape(shard_cols * d // LANES, LANES)
    from_columns = lambda a: a.reshape(shard_cols, d // LANES, LANES).transpose(1, 2, 0).reshape(1, d, shard_cols)
    (w_in_g,) = _gather_call("gather_weights", [_columns_to_rows_call("w_in_to_wire", columns(w_in), d, WIRE_DTYPE)])
    w_main, w_small = _from_shards_call(w_in_g)

    grad_x, p_w_in, p_wout, p_small, p_conv = _device_step(
        x[0], loss_target[0], w_main, w_small, w_out[0].astype(WIRE_DTYPE), gdn_conv_w[0], norm1_w, sb_norm_w,
        gdn_A_log, gdn_dt_bias, gdn_norm_w, final_norm_w, distributed=True)

    r_w_in = [from_columns(a) for a in _adam_columns_call("adam_w_in", p_w_in, columns(w_in), columns(m_w_in),
                                                          columns(v_w_in))]
    r_wout = _adam_call("adam_w_out", p_wout, w_out[0], m_w_out[0], v_w_out[0], d // N_DEV)
    r_conv = _adam_call("adam_conv", p_conv, gdn_conv_w[0], m_gdn_conv_w[0], v_gdn_conv_w[0], CONV_WIDTH)

    row = lambda a: a.reshape(1, -1)
    n_small = len(_SMALL_ROWS) - 1
    r_small = _adam_small_call(
        p_small,
        [norm1_w, row(final_norm_w), sb_norm_w, gdn_norm_w, gdn_A_log, gdn_dt_bias],
        [m_norm1_w, row(m_final_norm_w), m_sb_norm_w, m_gdn_norm_w, m_gdn_A_log, m_gdn_dt_bias],
        [v_norm1_w, row(v_final_norm_w), v_sb_norm_w, v_gdn_norm_w, v_gdn_A_log, v_gdn_dt_bias])

    def small_out(kind, name):
        out = r_small[1 + kind * n_small + _SMALL_ROWS.index(name)]
        return out.reshape(final_norm_w.shape) if name == "final_norm_w" else out

    def outputs(kind):
        return (small_out(kind, "norm1_w"), r_w_in[kind], small_out(kind, "sb_norm_w"), r_conv[kind][None],
                small_out(kind, "gdn_A_log"), small_out(kind, "gdn_dt_bias"), small_out(kind, "gdn_norm_w"),
                r_wout[kind][None], small_out(kind, "final_norm_w"))

    return (r_small[0][0, 0], grad_x[None], *outputs(0), *outputs(1), *outputs(2), *outputs(3))
```

```python
import functools

import jax
import jax.numpy as jnp
from jax import lax
from jax.experimental import pallas as pl
from jax.experimental.pallas import tpu as pltpu

F32 = jnp.float32
MXU_DTYPE = jnp.bfloat16
WIRE_DTYPE = jnp.bfloat16
EXACT = lax.Precision.HIGHEST
EPS = 1e-6
N_DEV = 8
SB_HEAD_DIM = 64
GDN_HEAD_DIM = 128
GDN_HEADS = 4
GDN_CHUNKS_PER_STEP = 4
GDN_BWD_GROUP = 1
CHUNK = 64
CONV_WIDTH = 4
LANES = 128
SB_BLOCK = 128
SB_BQ = 256
VMEM_LIMIT_BYTES = 56 * 1024 * 1024

PIECE_COLS = 512
PROJ_PIECE_KINDS = ("heads", "heads", "heads", "gate", "heads", "heads", "heads", "gate")
SB_FIRST_BLOCK, GDN_FIRST_BLOCK = 0, 12

DPROJ_PIECE_OF_SLOT = (0, 1, 2, 4, 5, 6, 3, 7)
DPROJ_SB_SLOT, DPROJ_GDN_SLOT, DPROJ_GATE_SLOT = 0, 3, 6

ADAM_LR = 0.001
ADAM_B1 = 0.9
ADAM_B2 = 0.999
ADAM_EPS = 1e-08
ADAM_WD = 0.01
ADAM_STEP = 10

_NN = (((1,), (0,)), ((), ()))
_NT = (((1,), (1,)), ((), ()))
_TN = (((0,), (0,)), ((), ()))
_BNN = (((2,), (1,)), ((0,), (0,)))
_BNT = (((2,), (2,)), ((0,), (0,)))
_BTN = (((1,), (1,)), ((0,), (0,)))


def _mx(a, b):
    return jnp.dot(a, b, precision=EXACT, preferred_element_type=F32)


def _split(x):
    hi = x.astype(MXU_DTYPE)
    return hi, (x - hi.astype(F32)).astype(MXU_DTYPE)


def _m3_general(a, b, dims, right_low=True):
    ah, al = _split(a)
    bh, bl = _split(b)
    dot = lambda x, y: lax.dot_general(x, y, dims, preferred_element_type=F32)
    (contract, _), (batch, _) = dims
    free = [ax for ax in range(a.ndim) if ax not in contract and ax not in batch][0]
    m = a.shape[free]
    both = dot(jnp.concatenate([ah, al], axis=free), bh)
    out_axis = len(batch)
    hi_part = lax.slice_in_dim(both, 0, m, axis=out_axis)
    lo_part = lax.slice_in_dim(both, m, 2 * m, axis=out_axis)
    return hi_part + (dot(ah, bl) + lo_part if right_low else lo_part)


def _times_exact(a, b_exact, dims):
    ah, al = _split(a)
    (contract, _), (batch, _) = dims
    free = [ax for ax in range(a.ndim) if ax not in contract and ax not in batch][0]
    m = a.shape[free]
    both = lax.dot_general(jnp.concatenate([ah, al], axis=free), b_exact.astype(MXU_DTYPE), dims,
                           preferred_element_type=F32)
    out_axis = len(batch)
    return lax.slice_in_dim(both, 0, m, axis=out_axis) + lax.slice_in_dim(both, m, 2 * m, axis=out_axis)


def _exact_times(a_exact, b, dims):
    bh, bl = _split(b)
    n = b.shape[-1]
    both = lax.dot_general(a_exact.astype(MXU_DTYPE), jnp.concatenate([bh, bl], axis=-1), dims,
                           preferred_element_type=F32)
    return both[..., :n] + both[..., n:]


def _sigmoid(z):
    return 1.0 / (1.0 + jnp.exp(-z))


def _softplus(z):
    return jnp.maximum(z, 0.0) + jnp.log(1.0 + jnp.exp(-jnp.abs(z)))


def _params(*semantics):
    return pltpu.CompilerParams(dimension_semantics=semantics, vmem_limit_bytes=VMEM_LIMIT_BYTES)


def _inproj_call(x, norm_w, w_main, w_small, w_small_t, gather=(), tm=256):
    t_len, d = x.shape
    n = w_main.shape[1]
    ns = w_small.shape[1]
    nst = w_small_t.shape[0]
    ng = len(gather)
    steps = t_len // tm

    blocks_per_piece = PIECE_COLS // LANES
    n_gate_cols = PIECE_COLS * PROJ_PIECE_KINDS.count("gate")
    n_col_blocks = blocks_per_piece * PROJ_PIECE_KINDS.count("heads")

    def body(*refs):
        x_ref, nw_ref, wm_ref, ws_ref, wst_ref = refs[:5]
        cols_ref, pz_ref, ps_ref, pst_ref, ht_ref, r_ref = refs[5 + ng:11 + ng]
        copies = lambda: _direct_copies(refs[5:5 + ng], refs[11 + ng:11 + 2 * ng], *refs[11 + 2 * ng:], (False,) * ng)
        if ng:
            pl.when(pl.program_id(0) == 0)(lambda: _start_all(copies()))
        xv = x_ref[...]
        r = lax.rsqrt(jnp.mean(xv * xv, axis=-1, keepdims=True) + EPS)
        h = xv * r * nw_ref[...]
        hb = h.astype(MXU_DTYPE)
        n_block = n_gate = 0
        for piece, kind in enumerate(PROJ_PIECE_KINDS):
            out = jnp.dot(hb, wm_ref[:, piece * PIECE_COLS:(piece + 1) * PIECE_COLS], preferred_element_type=F32)
            if kind == "gate":
                pz_ref[:, n_gate * PIECE_COLS:(n_gate + 1) * PIECE_COLS] = out
                n_gate += 1
            else:
                for j in range(blocks_per_piece):
                    cols_ref[n_block + j] = out[:, j * LANES:(j + 1) * LANES]
                n_block += blocks_per_piece
        ps_ref[...] = jnp.dot(hb, ws_ref[...], preferred_element_type=F32)
        pst_ref[...] = lax.dot_general(wst_ref[...], hb, _NT, preferred_element_type=F32)
        ht_ref[...] = h.T.astype(MXU_DTYPE)
        r_ref[...] = r
        if ng:
            pl.when(pl.program_id(0) == steps - 1)(lambda: _wait_all(copies()))

    return pl.pallas_call(
        body, name="inproj",
        grid=(steps,),
        in_specs=[pl.BlockSpec((tm, d), lambda i: (i, 0)),
                  pl.BlockSpec((1, d), lambda i: (0, 0)),
                  pl.BlockSpec((d, n), lambda i: (0, 0)),
                  pl.BlockSpec((d, ns), lambda i: (0, 0)),
                  pl.BlockSpec((nst, d), lambda i: (0, 0))] + [_HBM] * ng,
        out_specs=[pl.BlockSpec((n_col_blocks, tm, LANES), lambda i: (0, i, 0)),
                   pl.BlockSpec((tm, n_gate_cols), lambda i: (i, 0)),
                   pl.BlockSpec((tm, ns), lambda i: (i, 0)),
                   pl.BlockSpec((nst, tm), lambda i: (0, i)),
                   pl.BlockSpec((d, tm), lambda i: (0, i)),
                   pl.BlockSpec((tm, 1), lambda i: (i, 0))] + [_HBM] * ng,
        out_shape=[jax.ShapeDtypeStruct((n_col_blocks, t_len, LANES), F32),
                   jax.ShapeDtypeStruct((t_len, n_gate_cols), F32),
                   jax.ShapeDtypeStruct((t_len, ns), F32),
                   jax.ShapeDtypeStruct((nst, t_len), F32),
                   jax.ShapeDtypeStruct((d, t_len), MXU_DTYPE),
                   jax.ShapeDtypeStruct((t_len, 1), F32)] + _direct_out_shapes(gather, (False,) * ng),
        scratch_shapes=_direct_semaphores(ng) if ng else [],
        compiler_params=_params("arbitrary"),
    )(x, norm_w, w_main, w_small, w_small_t, *gather)


def _running_sum_mm(x, tri):
    hi = x.astype(MXU_DTYPE)
    lo = (x - hi.astype(F32)).astype(MXU_DTYPE)
    return jnp.dot(hi, tri, preferred_element_type=F32) + jnp.dot(lo, tri, preferred_element_type=F32)


def _col_block(t_len, first):
    return pl.BlockSpec((1, t_len, LANES), lambda p: (first + p, 0, 0))


def _sb_iotas():
    row_i = lax.broadcasted_iota(jnp.int32, (SB_BQ, SB_BLOCK), 0)
    col_i = lax.broadcasted_iota(jnp.int32, (SB_BQ, SB_BLOCK), 1)
    sq_r = lax.broadcasted_iota(jnp.int32, (SB_BLOCK, SB_BLOCK), 0)
    sq_c = lax.broadcasted_iota(jnp.int32, (SB_BLOCK, SB_BLOCK), 1)
    return row_i, col_i, sq_r, sq_c


SB_DIAG_BLOCKS = SB_BQ // SB_BLOCK
SB_EXP_FLOOR = -110.0


def _sb_keys_descending(qi, tile, carry, z_bounds, n_heads, has_free):
    group = SB_DIAG_BLOCKS
    n_free = group * qi
    diag = list(range(group - 1, -1, -1))
    carry = tile([n_free + j for j in diag], [True] * group, carry, [j * SB_BLOCK for j in diag])

    def largest_exponent(c):
        worst = jnp.max(z_bounds[0] - c[1])
        for h in range(1, n_heads):
            worst = jnp.maximum(worst, jnp.max(z_bounds[h] - c[1 + h]))
        return worst

    always = group if has_free else 0

    def cond(state):
        return (state[0] < n_free) & ((state[1] > SB_EXP_FLOOR) | (state[0] < always))

    def body(state):
        first = n_free - 1 - state[0]
        c = tile([first - j for j in range(group)], [False] * group, state[2:])
        return (state[0] + group, largest_exponent(c), *c)

    out = lax.while_loop(cond, body, (jnp.int32(0), largest_exponent(carry), *carry))
    return out[2:], out[0]


def _sb_keys_ascending(qi, n_run, tile, carry, has_free):
    group = SB_DIAG_BLOCKS
    n_free = group * qi
    diag = list(range(group))
    kjs, los, masked = [n_free + j for j in diag], [j * SB_BLOCK for j in diag], [True] * group
    if has_free:
        early = lambda s: [n_free - n_run + group * s + j for j in range(group)]
        carry = lax.fori_loop(0, n_run // group - 1, lambda s, c: tile(early(s), [False] * group, c), carry)
        kjs, los, masked = [n_free - group + j for j in range(group)] + kjs, [0] * group + los, [False] * group + masked
    return tile(kjs, masked, carry, los)


def _sb_fwd_call(cols, t_len):
    nq = t_len // SB_BQ
    scale = float(SB_HEAD_DIM) ** -0.5
    n_pairs = 512 // LANES
    per_pair = LANES // SB_HEAD_DIM

    def body(q_blk, k_blk, v_blk, o_blk, st_ref, nrun_ref):
        q_ref, k_ref, v_ref, o_ref = q_blk.at[0], k_blk.at[0], v_blk.at[0], o_blk.at[0]
        lane = lax.broadcasted_iota(jnp.int32, (1, LANES), 1)
        row_i, col_i, sq_r, sq_c = _sb_iotas()
        ge = (sq_r >= sq_c).astype(MXU_DTYPE)
        hms = [((lane // SB_HEAD_DIM) == hh).astype(F32) for hh in range(per_pair)]
        k_sq = k_ref[...] * k_ref[...]
        k_norms = [jnp.sqrt(jnp.max(jnp.sum(k_sq * hm, axis=-1, keepdims=True))) * (1.02 * scale) for hm in hms]

        def q_block(qi, has_free):
            r0 = qi * SB_BQ if isinstance(qi, int) else pl.multiple_of(qi * SB_BQ, SB_BQ)
            rows = pl.ds(r0, SB_BQ)
            q_all = q_ref[rows, :]
            qms = [(q_all * (hm * scale)).astype(MXU_DTYPE) for hm in hms]
            z_bounds = [jnp.sqrt(jnp.sum(q_all * q_all * hm, axis=-1, keepdims=True)) * kn
                        for hm, kn in zip(hms, k_norms)]

            def tile(kjs, masked, kc, los=None):
                heads = range(per_pair)
                los = los or [0] * len(kjs)
                pairs = [(t, h) for t in range(len(kjs)) for h in heads]
                add_rows = lambda full, lo, part: full + part if lo == 0 else jnp.concatenate(
                    [full[:lo], full[lo:] + part], axis=0)
                acc, cs = kc[0], list(kc[1:])
                s0s = [kj * SB_BLOCK if isinstance(kj, int) else pl.multiple_of(kj * SB_BLOCK, SB_BLOCK) for kj in kjs]
                kbs = [k_ref[pl.ds(s0, SB_BLOCK), :].astype(MXU_DTYPE) for s0 in s0s]
                v_alls = [v_ref[pl.ds(s0, SB_BLOCK), :] for s0 in s0s]
                vms = {(t, h): (v_alls[t] * hms[h]).astype(MXU_DTYPE) for t, h in pairs}
                zs = {(t, h): lax.dot_general(qms[h][los[t]:], kbs[t], _NT, preferred_element_type=F32)
                      for t, h in pairs}
                masks = [(col_i[lo:] + s0) < (row_i[lo:] + r0) if m else None for m, lo, s0 in zip(masked, los, s0s)]
                keep = lambda t, a: a if masks[t] is None else jnp.where(masks[t], a, 0.0)
                sps = {(t, h): keep(t, _softplus(zs[t, h])) for t, h in pairs}
                sums = {p: _running_sum_mm(sps[p], ge) for p in pairs}
                mass = {}
                for t, h in pairs:
                    mass[t, h] = cs[h] if t == 0 else add_rows(
                        mass[t - 1, h], los[t - 1], jnp.sum(sps[t - 1, h], axis=-1, keepdims=True))
                ws = {(t, h): keep(t, jnp.exp(zs[t, h] - (sums[t, h] + mass[t, h][los[t]:]))) for t, h in pairs}
                for t, h in pairs:
                    acc = add_rows(acc, los[t], jnp.dot(ws[t, h].astype(MXU_DTYPE), vms[t, h],
                                                        preferred_element_type=F32))
                last = len(kjs) - 1
                cs = [add_rows(mass[last, h], los[last], jnp.sum(sps[last, h], axis=-1, keepdims=True)) for h in heads]
                return (acc, *cs)

            zero_col = jnp.zeros((SB_BQ, 1), F32)
            out, n_run = _sb_keys_descending(
                qi, tile, (jnp.zeros((SB_BQ, LANES), F32),) + (zero_col,) * per_pair, z_bounds, per_pair, has_free)
            o_ref[rows, :] = out[0]
            masses = jnp.zeros((SB_BQ, LANES), F32)
            for hh in range(per_pair):
                masses = jnp.where(lane == hh, out[1 + hh], masses)
            st_ref[rows, :] = masses
            nrun_ref[pl.program_id(0), qi] = n_run

        q_block(0, False)
        lax.fori_loop(1, nq, lambda qi, carry: (q_block(qi, True), carry)[1], 0)

    return pl.pallas_call(
        body, name="sb_fwd",
        grid=(n_pairs,),
        in_specs=[_col_block(t_len, SB_FIRST_BLOCK), _col_block(t_len, SB_FIRST_BLOCK + n_pairs),
                  _col_block(t_len, SB_FIRST_BLOCK + 2 * n_pairs)],
        out_specs=[_col_block(t_len, 0),
                   pl.BlockSpec((t_len, LANES), lambda p: (0, p)),
                   pl.BlockSpec(memory_space=pltpu.SMEM)],
        out_shape=[jax.ShapeDtypeStruct((n_pairs, t_len, LANES), F32),
                   jax.ShapeDtypeStruct((t_len, n_pairs * LANES), F32),
                   jax.ShapeDtypeStruct((n_pairs, nq), jnp.int32)],
        compiler_params=_params("arbitrary"),
    )(cols, cols, cols)


def _sb_bwd_call(cols, sp_total, n_run_all, d_o, dproj, t_len):
    nq = t_len // SB_BQ
    scale = float(SB_HEAD_DIM) ** -0.5
    n_pairs = 512 // LANES
    per_pair = LANES // SB_HEAD_DIM

    def body(q_blk, k_blk, v_blk, st_ref, nrun_ref, do_blk, dproj_in_ref, d_ref):
        q_ref, k_ref, v_ref, do_ref = q_blk.at[0], k_blk.at[0], v_blk.at[0], do_blk.at[0]
        lane = lax.broadcasted_iota(jnp.int32, (1, LANES), 1)
        row_i, col_i, sq_r, sq_c = _sb_iotas()
        lt = (sq_r < sq_c).astype(MXU_DTYPE)
        le = (sq_r <= sq_c).astype(MXU_DTYPE)
        hms = [((lane // SB_HEAD_DIM) == hh).astype(F32) for hh in range(per_pair)]
        d_ref[1] = jnp.zeros((t_len, LANES), F32)
        d_ref[2] = jnp.zeros((t_len, LANES), F32)

        def q_block(qi, has_free):
            r0 = qi * SB_BQ if isinstance(qi, int) else pl.multiple_of(qi * SB_BQ, SB_BQ)
            rows = pl.ds(r0, SB_BQ)
            q_all, do_all = q_ref[rows, :], do_ref[rows, :]
            qms = [(q_all * (hm * scale)).astype(MXU_DTYPE) for hm in hms]
            doms = [(do_all * hm).astype(MXU_DTYPE) for hm in hms]
            masses = st_ref[rows, :]
            totals = [jnp.sum(jnp.where(lane == hh, masses, 0.0), axis=-1, keepdims=True) for hh in range(per_pair)]

            def tile(kjs, masked, kc, los=None):
                heads = range(per_pair)
                los = los or [0] * len(kjs)
                pairs = [(t, h) for t in range(len(kjs)) for h in heads]
                add_rows = lambda full, lo, part: full + part if lo == 0 else jnp.concatenate(
                    [full[:lo], full[lo:] + part], axis=0)
                rsum = lambda a: jnp.sum(a, axis=-1, keepdims=True)
                dq, cls, gls = kc[0], list(kc[1:1 + per_pair]), list(kc[1 + per_pair:])
                s0s = [kj * SB_BLOCK if isinstance(kj, int) else pl.multiple_of(kj * SB_BLOCK, SB_BLOCK) for kj in kjs]
                k_alls = [k_ref[pl.ds(s0, SB_BLOCK), :] for s0 in s0s]
                v_alls = [v_ref[pl.ds(s0, SB_BLOCK), :] for s0 in s0s]
                kbs = [k_all.astype(MXU_DTYPE) for k_all in k_alls]
                vms = {(t, h): (v_alls[t] * hms[h]).astype(MXU_DTYPE) for t, h in pairs}
                kms = {(t, h): (k_alls[t] * (hms[h] * scale)).astype(MXU_DTYPE) for t, h in pairs}
                q_live = {(t, h): qms[h][los[t]:] for t, h in pairs}
                do_live = {(t, h): doms[h][los[t]:] for t, h in pairs}
                zs = {p: lax.dot_general(q_live[p], kbs[p[0]], _NT, preferred_element_type=F32) for p in pairs}
                das = {p: lax.dot_general(do_live[p], vms[p], _NT, preferred_element_type=F32) for p in pairs}
                masks = [(col_i[lo:] + s0) < (row_i[lo:] + r0) if m else None for m, lo, s0 in zip(masked, los, s0s)]
                keep = lambda t, a: a if masks[t] is None else jnp.where(masks[t], a, 0.0)
                sp_alls = {p: _softplus(zs[p]) for p in pairs}
                sps = {(t, h): keep(t, sp_alls[t, h]) for t, h in pairs}
                lefts = {p: _running_sum_mm(sps[p], lt) for p in pairs}
                cl = {}
                for t, h in pairs:
                    cl[t, h] = cls[h] if t == 0 else add_rows(cl[t - 1, h], los[t - 1], rsum(sps[t - 1, h]))
                ws = {(t, h): keep(t, jnp.exp(zs[t, h] - ((totals[h] - cl[t, h])[los[t]:] - lefts[t, h])))
                      for t, h in pairs}
                gs = {p: das[p] * ws[p] for p in pairs}
                g_sums = {p: _running_sum_mm(gs[p], le) for p in pairs}
                gl = {}
                for t, h in pairs:
                    gl[t, h] = gls[h] if t == 0 else add_rows(gl[t - 1, h], los[t - 1], rsum(gs[t - 1, h]))
                dzs = {(t, h): keep(t, gs[t, h] - jnp.exp(zs[t, h] - sp_alls[t, h]) * (gl[t, h][los[t]:] + g_sums[t, h])
                               ).astype(MXU_DTYPE) for t, h in pairs}
                for t in range(len(kjs)):
                    dk_t = jnp.zeros((SB_BLOCK, LANES), F32)
                    dv_t = jnp.zeros((SB_BLOCK, LANES), F32)
                    for h in heads:
                        dq = add_rows(dq, los[t], jnp.dot(dzs[t, h], kms[t, h], preferred_element_type=F32))
                        dk_t = dk_t + lax.dot_general(dzs[t, h], q_live[t, h], _TN, preferred_element_type=F32)
                        dv_t = dv_t + lax.dot_general(ws[t, h].astype(MXU_DTYPE), do_live[t, h], _TN,
                                                      preferred_element_type=F32)
                    d_ref[1, pl.ds(s0s[t], SB_BLOCK), :] += dk_t
                    d_ref[2, pl.ds(s0s[t], SB_BLOCK), :] += dv_t
                last = len(kjs) - 1
                cls = [add_rows(cl[last, h], los[last], rsum(sps[last, h])) for h in heads]
                gls = [add_rows(gl[last, h], los[last], rsum(gs[last, h])) for h in heads]
                return (dq, *cls, *gls)

            zero_col = jnp.zeros((SB_BQ, 1), F32)
            out = _sb_keys_ascending(qi, nrun_ref[pl.program_id(0), qi], tile,
                                     (jnp.zeros((SB_BQ, LANES), F32),) + (zero_col,) * (2 * per_pair), has_free)
            d_ref[0, rows, :] = out[0]

        q_block(0, False)
        lax.fori_loop(1, nq, lambda qi, carry: (q_block(qi, True), carry)[1], 0)

    return pl.pallas_call(
        body, name="sb_bwd",
        grid=(n_pairs,),
        in_specs=[_col_block(t_len, SB_FIRST_BLOCK), _col_block(t_len, SB_FIRST_BLOCK + n_pairs),
                  _col_block(t_len, SB_FIRST_BLOCK + 2 * n_pairs),
                  pl.BlockSpec((t_len, LANES), lambda p: (0, p)),
                  pl.BlockSpec(memory_space=pltpu.SMEM), _col_block(t_len, 0), _HBM],
        out_specs=pl.BlockSpec((3, t_len, LANES), lambda p: (DPROJ_SB_SLOT // 3, 0, p)),
        out_shape=jax.ShapeDtypeStruct(dproj.shape, dproj.dtype),
        input_output_aliases={6: 0},
        compiler_params=_params("arbitrary"),
    )(cols, cols, cols, sp_total, n_run_all, d_o, dproj)


def _conv_taps(xin, rows, t_len):
    taps = []
    for i in range(CONV_WIDTH):
        shift = CONV_WIDTH - 1 - i
        if shift == 0:
            taps.append(xin)
        else:
            taps.append(jnp.where(rows >= shift, pltpu.roll(xin, shift, axis=0), 0.0))
    return taps


def _gdn_prep_body_common(x_ref, w_ref, t_len):
    j = pl.program_id(0)
    xin = x_ref[...]
    rows = lax.broadcasted_iota(jnp.int32, (t_len, LANES), 0)
    taps = _conv_taps(xin, rows, t_len)
    pre = taps[0] * w_ref[0:1, :]
    for i in range(1, CONV_WIDTH):
        pre = pre + taps[i] * w_ref[i:i + 1, :]
    sg = _sigmoid(pre)
    act = pre * sg
    is_qk = j < 2 * GDN_HEADS
    nrm = jnp.where(is_qk, lax.rsqrt(jnp.sum(act * act, axis=-1, keepdims=True) + EPS), 1.0)
    sc = jnp.where(j < GDN_HEADS, float(GDN_HEAD_DIM) ** -0.5, 1.0)
    return j, rows, taps, pre, sg, act, is_qk, nrm, sc


def _gdn_prep_call(cols, conv_w, t_len, after):
    def body(x_blk, w_ref, after_ref, out_ref):
        _, _, _, _, _, act, _, nrm, sc = _gdn_prep_body_common(x_blk.at[0], w_ref, t_len)
        out_ref[...] = act * nrm * sc

    return pl.pallas_call(
        body, name="gdn_prep",
        grid=(3 * GDN_HEADS,),
        in_specs=[_col_block(t_len, GDN_FIRST_BLOCK),
                  pl.BlockSpec((CONV_WIDTH, LANES), lambda j: (0, j)),
                  pl.BlockSpec(memory_space=pl.ANY)],
        out_specs=pl.BlockSpec((t_len, LANES), lambda j: (0, j)),
        out_shape=jax.ShapeDtypeStruct((t_len, 3 * 512), F32),
        compiler_params=_params("arbitrary"),
    )(cols, conv_w, after)


def _gdn_prep_bwd_call(cols, conv_w, d_act3, dproj, t_len):
    def body(x_blk, w_ref, d_ref, dproj_in_ref, dx_ref, dw_ref):
        _, rows, taps, pre, sg, act, is_qk, nrm, sc = _gdn_prep_body_common(x_blk.at[0], w_ref, t_len)
        d_out = d_ref[0]
        dn = d_out * sc
        d_norm = nrm * dn - act * (nrm * nrm * nrm) * jnp.sum(dn * act, axis=-1, keepdims=True)
        d_act = jnp.where(is_qk, d_norm, d_out)
        d_pre = d_act * sg * (1.0 + pre * (1.0 - sg))
        dx = d_pre * w_ref[CONV_WIDTH - 1:CONV_WIDTH, :]
        dw_ref[CONV_WIDTH - 1:CONV_WIDTH, :] = jnp.sum(d_pre * taps[CONV_WIDTH - 1], axis=0, keepdims=True)
        for i in range(CONV_WIDTH - 1):
            shift = CONV_WIDTH - 1 - i
            up = jnp.where(rows < t_len - shift, pltpu.roll(d_pre, t_len - shift, axis=0), 0.0)
            dx = dx + up * w_ref[i:i + 1, :]
            dw_ref[i:i + 1, :] = jnp.sum(d_pre * taps[i], axis=0, keepdims=True)
        dx_ref[0] = dx

    return pl.pallas_call(
        body, name="gdn_prep_bwd",
        grid=(3 * GDN_HEADS,),
        in_specs=[_col_block(t_len, GDN_FIRST_BLOCK),
                  pl.BlockSpec((CONV_WIDTH, LANES), lambda j: (0, j)),
                  pl.BlockSpec((1, t_len, LANES), lambda j: (j // GDN_HEADS, 0, j % GDN_HEADS)), _HBM],
        out_specs=[pl.BlockSpec((1, t_len, LANES), lambda j: (DPROJ_GDN_SLOT + j // GDN_HEADS, 0, j % GDN_HEADS)),
                   pl.BlockSpec((CONV_WIDTH, LANES), lambda j: (0, j))],
        out_shape=[jax.ShapeDtypeStruct(dproj.shape, dproj.dtype),
                   jax.ShapeDtypeStruct((CONV_WIDTH, 3 * 512), F32)],
        input_output_aliases={3: 0},
        compiler_params=_params("arbitrary"),
    )(cols, conv_w, d_act3, dproj)


def _chunk_cumsum_matrix():
    r = lax.broadcasted_iota(jnp.int32, (LANES, LANES), 0)
    c = lax.broadcasted_iota(jnp.int32, (LANES, LANES), 1)
    return ((r <= c) & ((r // CHUNK) == (c // CHUNK))).astype(F32)


def _gdn_gates_call(ps, pst, alog_l, dtb_l, alog_c, dtb_c, t_len):
    def body(ps_ref, pst_ref, al_ref, dl_ref, ac_ref, dc_ref, beta_ref, gcol_ref, grow_ref):
        upper = _chunk_cumsum_matrix()
        lower = upper.T
        psv = ps_ref[...]
        beta_ref[...] = _sigmoid(psv)
        g_l = -jnp.exp(al_ref[...]) * _softplus(psv + dl_ref[...])
        g_r = -jnp.exp(ac_ref[...]) * _softplus(pst_ref[...] + dc_ref[...])
        for w in range(t_len // LANES):
            sl = slice(w * LANES, (w + 1) * LANES)
            gcol_ref[sl, :] = _mx(lower, g_l[sl, :])
            grow_ref[:, sl] = _mx(g_r[:, sl], upper)

    vm = pl.BlockSpec(memory_space=pltpu.VMEM)
    return pl.pallas_call(
        body, name="gdn_gates",
        in_specs=[vm] * 6, out_specs=[vm] * 3,
        out_shape=[jax.ShapeDtypeStruct((t_len, LANES), F32),
                   jax.ShapeDtypeStruct((t_len, LANES), F32),
                   jax.ShapeDtypeStruct((8, t_len), F32)],
        compiler_params=pltpu.CompilerParams(vmem_limit_bytes=VMEM_LIMIT_BYTES),
    )(ps, pst, alog_l, dtb_l, alog_c, dtb_c)


def _gdn_gates_bwd_call(ps, alog_l, dtb_l, d_l, t_len):
    def body(ps_ref, al_ref, dl_ref, d_ref, dps_ref, gal_ref, gdt_ref):
        lane = lax.broadcasted_iota(jnp.int32, (1, LANES), 1)
        psv = ps_ref[...]
        dv = d_ref[...]
        beta = _sigmoid(psv)
        ea = jnp.exp(al_ref[...])
        arg = psv + dl_ref[...]
        g = -ea * _softplus(arg)
        d_a = dv * (-ea) * _sigmoid(arg)
        is_a = (lane >= GDN_HEADS) & (lane < 2 * GDN_HEADS)
        dps_ref[...] = jnp.where(lane < GDN_HEADS, dv * beta * (1.0 - beta), jnp.where(is_a, d_a, 0.0))
        gdt_ref[...] = jnp.where(is_a, jnp.sum(d_a, axis=0, keepdims=True), 0.0)
        gal_ref[...] = jnp.where(is_a, jnp.sum(dv * g, axis=0, keepdims=True), 0.0)

    vm = pl.BlockSpec(memory_space=pltpu.VMEM)
    return pl.pallas_call(
        body, name="gdn_gates_bwd",
        in_specs=[vm] * 4, out_specs=[vm] * 3,
        out_shape=[jax.ShapeDtypeStruct((t_len, LANES), F32),
                   jax.ShapeDtypeStruct((1, LANES), F32),
                   jax.ShapeDtypeStruct((1, LANES), F32)],
        compiler_params=pltpu.CompilerParams(vmem_limit_bytes=VMEM_LIMIT_BYTES),
    )(ps, alog_l, dtb_l, d_l)


def _bm(a, b, right_low=True):
    return _m3_general(a, b, _BNN, right_low)


def _bm_nt(a, b, right_low=True):
    return _m3_general(a, b, _BNT, right_low)


def _bm_tn(a, b, right_low=True):
    return _m3_general(a, b, _BTN, right_low)


def _heads_of(ref, rows):
    return jnp.stack([ref[rows, h * GDN_HEAD_DIM:(h + 1) * GDN_HEAD_DIM] for h in range(GDN_HEADS)])


def _chunk_terms(q_ref, k_ref, v_ref, b_ref, gc_ref, gr_ref, c, incl, strict, n=1, scores=True):
    r0 = c * CHUNK if isinstance(c, int) else pl.multiple_of(c * CHUNK, CHUNK)
    rows = pl.ds(r0, n * CHUNK)
    per_chunk = lambda x: x.reshape(GDN_HEADS * n, CHUNK, x.shape[-1])
    q, k, v = (per_chunk(_heads_of(ref, rows)) for ref in (q_ref, k_ref, v_ref))
    lane_ids = lax.broadcasted_iota(jnp.int32, (1, LANES), 1)
    pick = lambda slab, first: jnp.stack([jnp.sum(jnp.where(lane_ids == first + h, slab, 0.0), axis=-1, keepdims=True)
                                          for h in range(GDN_HEADS)])
    b = per_chunk(pick(b_ref[rows, :], 0))
    gc = per_chunk(pick(gc_ref[rows, :], GDN_HEADS))
    gr = gr_ref[:, c] if n == 1 else gr_ref[:, c:c + n].reshape(GDN_HEADS * n, 1, CHUNK)
    dm = jnp.where(incl, jnp.exp(jnp.where(incl, gc - gr, 0.0)), 0.0)
    kb = k * b
    vb = v * b
    e = jnp.exp(gc)
    a = p = None
    if scores:
        kk_qk = _bm_nt(jnp.concatenate([kb, q], axis=1), k)
        a = jnp.where(strict, kk_qk[:, :CHUNK] * dm, 0.0)
        p = jnp.where(incl, kk_qk[:, CHUNK:] * dm, 0.0)
    gl = gc[:, CHUNK - 1:CHUNK, :]
    eg = jnp.exp(gl - gc)
    return rows, q, k, v, b, gc, dm, kb, vb, e, a, p, gl, eg


def _unit_lower_inverse(a, eye):
    x = -a
    tm = eye + x
    xp = _bm(x, x)
    for _ in range(4):
        both = _bm(jnp.concatenate([xp, tm], axis=1), xp)
        tm = tm + both[:, CHUNK:]
        xp = both[:, :CHUNK]
    return tm + _bm(tm, xp)


def _gdn_specs(t_len, n_chunks, reverse):
    cps = GDN_CHUNKS_PER_STEP
    steps = n_chunks // cps
    at = (lambda g: steps - 1 - g) if reverse else (lambda g: g)
    rows_blk = lambda width, part=0: pl.BlockSpec((cps * CHUNK, width), lambda g: (at(g), part))
    gate_r = pl.BlockSpec((GDN_HEADS, cps, 1, CHUNK), lambda g: (0, at(g), 0, 0))
    per_chunk = lambda r, c: pl.BlockSpec((GDN_HEADS, cps, r, c), lambda g: (0, at(g), 0, 0))
    return cps, steps, rows_blk, gate_r, per_chunk


def _gdn_fwd_call(gact, beta_c, gam_c, gam_r, t_len):
    n_chunks = t_len // CHUNK
    dk = GDN_HEAD_DIM
    width = GDN_HEADS * dk
    cps, steps, rows_blk, gate_r, per_chunk = _gdn_specs(t_len, n_chunks, False)

    def body(q_ref, k_ref, v_ref, b_ref, gc_ref, gr_ref, o_ref, s_ref, t_ref, a_ref, p_ref, uw_ref, vn_ref, state_ref):
        row = lax.broadcasted_iota(jnp.int32, (CHUNK, CHUNK), 0)
        col = lax.broadcasted_iota(jnp.int32, (CHUNK, CHUNK), 1)
        incl, strict = row >= col, row > col
        eye = (row == col).astype(F32)

        @pl.when(pl.program_id(0) == 0)
        def _():
            state_ref[...] = jnp.zeros_like(state_ref)

        _, q, k, v, b, gc, dm, kb, vb, e, a, p, gl, eg = _chunk_terms(
            q_ref, k_ref, v_ref, b_ref, gc_ref, gr_ref, 0, incl, strict, cps)
        tm = _unit_lower_inverse(a, eye)
        uw = _bm(tm, jnp.concatenate([vb, kb * e], axis=2))
        w_qe = jnp.concatenate([uw[:, :, dk:], q * e], axis=1)
        u, kd, decay = uw[:, :, :dk], k * eg, jnp.exp(gl)
        per_chunk_block = lambda x: x.reshape(GDN_HEADS, cps, CHUNK, CHUNK)
        t_ref[...], a_ref[...], p_ref[...] = per_chunk_block(tm), per_chunk_block(a), per_chunk_block(p)
        uw_heads = uw.reshape(GDN_HEADS, cps * CHUNK, 2 * dk)
        for h in range(GDN_HEADS):
            uw_ref[:, h * 2 * dk:(h + 1) * 2 * dk] = uw_heads[h]

        of_chunk = lambda x, c: jnp.stack([x[h * cps + c] for h in range(GDN_HEADS)])
        s = state_ref[...]
        for c in range(cps):
            ws_qs = _bm(of_chunk(w_qe, c), s)
            vn = of_chunk(u, c) - ws_qs[:, :CHUNK]
            o = ws_qs[:, CHUNK:] + _bm(of_chunk(p, c), vn)
            for h in range(GDN_HEADS):
                o_ref[c * CHUNK:(c + 1) * CHUNK, h * dk:(h + 1) * dk] = o[h]
                vn_ref[c * CHUNK:(c + 1) * CHUNK, h * dk:(h + 1) * dk] = vn[h]
            s_ref[:, c] = s
            s = s * of_chunk(decay, c) + _bm_tn(of_chunk(kd, c), vn)
        state_ref[...] = s

    scores = jax.ShapeDtypeStruct((GDN_HEADS, n_chunks, CHUNK, CHUNK), F32)
    return pl.pallas_call(
        body, name="gdn_fwd",
        grid=(steps,),
        in_specs=[rows_blk(width, 0), rows_blk(width, 1), rows_blk(width, 2), rows_blk(LANES), rows_blk(LANES), gate_r],
        out_specs=[rows_blk(width), per_chunk(dk, dk), per_chunk(CHUNK, CHUNK), per_chunk(CHUNK, CHUNK),
                   per_chunk(CHUNK, CHUNK), rows_blk(2 * width), rows_blk(width)],
        out_shape=[jax.ShapeDtypeStruct((t_len, width), F32),
                   jax.ShapeDtypeStruct((GDN_HEADS, n_chunks, dk, dk), F32), scores, scores, scores,
                   jax.ShapeDtypeStruct((t_len, 2 * width), F32), jax.ShapeDtypeStruct((t_len, width), F32)],
        scratch_shapes=[pltpu.VMEM((GDN_HEADS, dk, dk), F32)],
        compiler_params=_params("arbitrary"),
    )(gact, gact, gact, beta_c, gam_c, gam_r)


def _gdn_bwd_call(gact, beta_c, gam_c, gam_r, saved, d_o, t_len, scatter=()):
    n_chunks = t_len // CHUNK
    dk = GDN_HEAD_DIM
    width = GDN_HEADS * dk
    cps, steps, rows_blk, gate_r, per_chunk = _gdn_specs(t_len, n_chunks, True)
    nx = len(scatter)
    n_in = 13

    def body(*refs):
        q_ref, k_ref, v_ref, b_ref, gc_ref, gr_ref = refs[:6]
        saved_refs, do_ref = refs[6:12], refs[12]
        d_ref, dgate_ref = refs[n_in + nx:n_in + 2 + nx]
        dstate_ref = refs[n_in + 2 + 2 * nx]
        copies = lambda: _direct_copies(refs[n_in:n_in + nx], refs[n_in + 2 + nx:n_in + 2 + 2 * nx],
                                        *refs[n_in + 3 + 2 * nx:], (True,) * nx)
        if nx:
            pl.when(pl.program_id(0) == 0)(lambda: _start_all(copies()))
        row = lax.broadcasted_iota(jnp.int32, (CHUNK, CHUNK), 0)
        col = lax.broadcasted_iota(jnp.int32, (CHUNK, CHUNK), 1)
        incl, strict = row >= col, row > col
        ng = GDN_BWD_GROUP
        nb = GDN_HEADS * ng
        upper = jnp.broadcast_to((row <= col).astype(F32), (nb, CHUNK, CHUNK))
        ones = jnp.ones((nb, CHUNK, LANES), F32)
        last_row = lax.broadcasted_iota(jnp.int32, (CHUNK, 1), 0) == CHUNK - 1
        lane_ids = lax.broadcasted_iota(jnp.int32, (1, LANES), 1)
        rsum = lambda m: jnp.sum(m, axis=-1, keepdims=True)
        total = lambda m: jnp.sum(rsum(m), axis=1, keepdims=True)
        of_chunk = lambda x, c: jnp.stack([x[h * ng + c] for h in range(GDN_HEADS)])

        @pl.when(pl.program_id(0) == 0)
        def _():
            dstate_ref[...] = jnp.zeros_like(dstate_ref)

        for c0 in range(cps - ng, -1, -ng):
            group(c0, q_ref, k_ref, v_ref, b_ref, gc_ref, gr_ref, saved_refs, do_ref, d_ref, dgate_ref, dstate_ref,
                  incl, strict, upper, ones, last_row, lane_ids, rsum, total, of_chunk)
        if nx:
            pl.when(pl.program_id(0) == steps - 1)(lambda: _wait_all(copies()))

    def group(c0, q_ref, k_ref, v_ref, b_ref, gc_ref, gr_ref, saved_refs, do_ref, d_ref, dgate_ref, dstate_ref,
              incl, strict, upper, ones, last_row, lane_ids, rsum, total, of_chunk):
        ng = GDN_BWD_GROUP
        nb = GDN_HEADS * ng
        rows = pl.ds(c0 * CHUNK, ng * CHUNK)
        s_ref, t_ref, a_ref, p_ref, uw_ref, vn_ref = saved_refs
        _, q, k, v, b, gc, dm, kb, vb, e, _, _, gl, eg = _chunk_terms(
            q_ref, k_ref, v_ref, b_ref, gc_ref, gr_ref, c0, incl, strict, ng, scores=False)
        s = s_ref[:, c0:c0 + ng].reshape(nb, dk, dk)
        tm = t_ref[:, c0:c0 + ng].reshape(nb, CHUNK, CHUNK)
        a = a_ref[:, c0:c0 + ng].reshape(nb, CHUNK, CHUNK)
        p = p_ref[:, c0:c0 + ng].reshape(nb, CHUNK, CHUNK)
        d_out = _heads_of(do_ref, rows).reshape(nb, CHUNK, dk)
        vn = _heads_of(vn_ref, rows).reshape(nb, CHUNK, dk)
        uw = jnp.stack([uw_ref[rows, h * 2 * dk:(h + 1) * 2 * dk] for h in range(GDN_HEADS)]).reshape(nb, CHUNK, 2 * dk)
        u, w = uw[:, :, :dk], uw[:, :, dk:]
        el = jnp.exp(gl)
        kbe = kb * e
        qe = q * e
        kd = k * eg
        bm, bm_nt, bm_tn = (functools.partial(f, right_low=False) for f in (_bm, _bm_nt, _bm_tn))
        pt_do = _bm_tn(p, d_out)
        qet_do = _bm_tn(qe, d_out)

        ds = dstate_ref[...]
        d_vn_c, ds_c = [None] * ng, [None] * ng
        for c in range(ng - 1, -1, -1):
            ds_c[c] = ds
            d_vn_c[c] = of_chunk(pt_do, c) + bm(of_chunk(kd, c), ds)
            ds = of_chunk(el, c) * ds + of_chunk(qet_do, c) - bm_tn(of_chunk(w, c), d_vn_c[c])
        dstate_ref[...] = ds
        by_chunk = lambda xs: jnp.stack([xs[c][h] for h in range(GDN_HEADS) for c in range(ng)])
        d_vn, ds = by_chunk(d_vn_c), by_chunk(ds_c)

        on_s = bm_nt(jnp.concatenate([d_out, d_vn], axis=1), s)
        d_qe, d_w = on_s[:, :CHUNK], -on_s[:, CHUNK:]
        d_p = jnp.where(incl, _bm_nt(d_out, vn), 0.0)
        d_kd = _bm_nt(vn, ds)
        d_both = _bm_tn(tm, jnp.concatenate([d_vn, d_w], axis=2))
        d_vb, d_kbe = d_both[:, :, :dk], d_both[:, :, dk:]
        d_a = -jnp.where(strict, _bm_nt(d_both, uw), 0.0)
        m = d_a * dm
        n = d_p * dm
        on_k = bm(jnp.concatenate([m, n], axis=1), k)
        d_kb = on_k[:, :CHUNK] + d_kbe * e
        d_q = on_k[:, CHUNK:] + d_qe * e
        d_k = (bm_tn(jnp.concatenate([m, n], axis=1), jnp.concatenate([kb, q], axis=1))
               + d_kd * eg + b * d_kb)
        d_v = b * d_vb
        r = d_a * a + d_p * p
        kd_term = rsum(d_kd * kd)
        d_gl = total(ds * s) * el + jnp.sum(kd_term, axis=1, keepdims=True)
        d_gam = (rsum(r) - _times_exact(r, ones, _BTN)[:, :, 0:1] + rsum(d_qe * qe) + rsum(d_kbe * kbe) - kd_term
                 + jnp.where(last_row, d_gl, 0.0))
        d_beta = rsum(d_kb * k) + rsum(d_vb * v)
        d_g = _exact_times(upper, d_gam * ones, _BNN)[:, :, 0:1]
        per_head = lambda x: x.reshape(GDN_HEADS, ng * CHUNK, x.shape[-1])
        d_q, d_k, d_v, d_beta, d_g = (per_head(x) for x in (d_q, d_k, d_v, d_beta, d_g))
        gates = jnp.zeros((ng * CHUNK, LANES), F32)
        for h in range(GDN_HEADS):
            lanes = slice(h * dk, (h + 1) * dk)
            d_ref[0, rows, lanes] = d_q[h]
            d_ref[1, rows, lanes] = d_k[h]
            d_ref[2, rows, lanes] = d_v[h]
            gates = gates + (jnp.where(lane_ids == h, d_beta[h], 0.0)
                             + jnp.where(lane_ids == GDN_HEADS + h, d_g[h], 0.0))
        dgate_ref[rows, :] = gates

    d_spec = pl.BlockSpec((3, cps * CHUNK, width), lambda g: (0, steps - 1 - g, 0))
    return pl.pallas_call(
        body, name="gdn_bwd",
        grid=(steps,),
        in_specs=[rows_blk(width, 0), rows_blk(width, 1), rows_blk(width, 2), rows_blk(LANES), rows_blk(LANES), gate_r,
                  per_chunk(dk, dk), per_chunk(CHUNK, CHUNK), per_chunk(CHUNK, CHUNK), per_chunk(CHUNK, CHUNK),
                  rows_blk(2 * width), rows_blk(width), rows_blk(width)] + [_HBM] * nx,
        out_specs=[d_spec, rows_blk(LANES)] + [_HBM] * nx,
        out_shape=[jax.ShapeDtypeStruct((3, t_len, width), F32),
                   jax.ShapeDtypeStruct((t_len, LANES), F32)] + _direct_out_shapes(scatter, (True,) * nx),
        scratch_shapes=[pltpu.VMEM((GDN_HEADS, dk, dk), F32)] + (_direct_semaphores(nx) if nx else []),
        compiler_params=_params("arbitrary"),
    )(gact, gact, gact, beta_c, gam_c, gam_r, *saved, d_o, *scatter)


def _group_sums(x, group):
    rows, width = x.shape
    lane = lax.broadcasted_iota(jnp.int32, (1, LANES), 1)
    out = []
    for t in range(width // LANES):
        seg = x[:, t * LANES:(t + 1) * LANES]
        if group == LANES:
            out.append(jnp.broadcast_to(jnp.sum(seg, axis=-1, keepdims=True), (rows, LANES)))
        else:
            low = jnp.sum(jnp.where(lane < group, seg, 0.0), axis=-1, keepdims=True)
            high = jnp.sum(jnp.where(lane < group, 0.0, seg), axis=-1, keepdims=True)
            out.append(jnp.where(lane < group, low, high))
    return jnp.concatenate(out, axis=1)


def _post_call(o_sb, o_gd, proj_gates, x, target, w_out, sbw, gdw, fw, tm=256):
    t_len, d = x.shape
    half = 512
    sb_blocks = half // LANES

    def body(osb_ref, ogd_ref, zsb_ref, zgd_ref, x_ref, tg_ref, wo_ref, sbw_ref, gdw_ref, fw_ref,
             dx2_ref, dosb_ref, dogd_ref, dz_ref, loss_ref, gfw_ref, gsb_ref, ggd_ref, gwo_ref):
        step = pl.program_id(0)

        @pl.when(step == 0)
        def _():
            loss_ref[...] = jnp.zeros_like(loss_ref)
            gfw_ref[...] = jnp.zeros_like(gfw_ref)
            gsb_ref[...] = jnp.zeros_like(gsb_ref)
            ggd_ref[...] = jnp.zeros_like(ggd_ref)
            gwo_ref[...] = jnp.zeros_like(gwo_ref)

        def head_forward(o, z, w, head_dim):
            r = lax.rsqrt(_group_sums(o * o, head_dim) * (1.0 / head_dim) + EPS)
            nrm = o * r * w
            sg = _sigmoid(z)
            return r, nrm, sg, nrm * (z * sg)

        def head_backward(d_m, o, z, w, head_dim, r, nrm, sg):
            d_n = d_m * (z * sg)
            d_z = d_m * nrm * (sg * (1.0 + z * (1.0 - sg)))
            dnw = d_n * w
            d_o = r * dnw - o * (r * r * r) * (_group_sums(dnw * o, head_dim) * (1.0 / head_dim))
            return d_o, d_z, jnp.sum(d_n * o * r, axis=0, keepdims=True)

        osb = jnp.concatenate([osb_ref[j] for j in range(sb_blocks)], axis=1)
        ogd, zsb, zgd = ogd_ref[...], zsb_ref[...], zgd_ref[...]
        sbw_v, gdw_v = sbw_ref[...], gdw_ref[...]
        r_sb, n_sb, sg_sb, m_sb = head_forward(osb, zsb, sbw_v, SB_HEAD_DIM)
        r_gd, n_gd, sg_gd, m_gd = head_forward(ogd, zgd, gdw_v, GDN_HEAD_DIM)
        mixed = jnp.concatenate([m_sb, m_gd], axis=1).astype(MXU_DTYPE)
        wo = wo_ref[...]
        x2 = x_ref[...] + jnp.dot(mixed, wo, preferred_element_type=F32)
        r2 = lax.rsqrt(jnp.mean(x2 * x2, axis=-1, keepdims=True) + EPS)
        fw_v = fw_ref[...]
        err = x2 * r2 * fw_v - tg_ref[...]
        loss_ref[...] += 0.5 * jnp.sum(jnp.sum(err * err, axis=-1, keepdims=True) * (1.0 / d))
        dy = err * (1.0 / d)
        gg = dy * fw_v
        dx2 = r2 * gg - x2 * ((r2 * r2 * r2) * jnp.mean(gg * x2, axis=-1, keepdims=True))
        gfw_ref[...] += jnp.sum(dy * x2 * r2, axis=0, keepdims=True)
        dx2_ref[...] = dx2
        dx2b = dx2.astype(MXU_DTYPE)
        d_mixed = lax.dot_general(dx2b, wo, _NT, preferred_element_type=F32)
        gwo_ref[...] += lax.dot_general(mixed, dx2b, _TN, preferred_element_type=F32)
        d_osb, d_zsb, gsb = head_backward(d_mixed[:, :half], osb, zsb, sbw_v, SB_HEAD_DIM, r_sb, n_sb, sg_sb)
        d_ogd, d_zgd, ggd = head_backward(d_mixed[:, half:], ogd, zgd, gdw_v, GDN_HEAD_DIM, r_gd, n_gd, sg_gd)
        for j in range(sb_blocks):
            dosb_ref[j] = d_osb[:, j * LANES:(j + 1) * LANES]
        dogd_ref[...] = d_ogd
        dz_ref[0] = d_zsb
        dz_ref[1] = d_zgd
        gsb_ref[...] += gsb
        ggd_ref[...] += ggd

    row_blk = lambda w: pl.BlockSpec((tm, w), lambda i: (i, 0))
    blocks_blk = pl.BlockSpec((sb_blocks, tm, LANES), lambda i: (0, i, 0))
    fixed = lambda r, w: pl.BlockSpec((r, w), lambda i: (0, 0))
    return pl.pallas_call(
        body, name="post",
        grid=(t_len // tm,),
        in_specs=[blocks_blk, row_blk(half),
                  pl.BlockSpec((tm, half), lambda i: (i, 0)),
                  pl.BlockSpec((tm, half), lambda i: (i, 1)),
                  row_blk(d), row_blk(d), fixed(d, d), fixed(1, half), fixed(1, half), fixed(1, d)],
        out_specs=[row_blk(d), blocks_blk, row_blk(half),
                   pl.BlockSpec((2, tm, half), lambda i: (DPROJ_GATE_SLOT // 2, i, 0)),
                   fixed(1, LANES), fixed(1, d), fixed(1, half), fixed(1, half), fixed(d, d)],
        out_shape=[jax.ShapeDtypeStruct((t_len, d), F32), jax.ShapeDtypeStruct((sb_blocks, t_len, LANES), F32),
                   jax.ShapeDtypeStruct((t_len, half), F32),
                   jax.ShapeDtypeStruct((len(DPROJ_PIECE_OF_SLOT), t_len, half), F32),
                     jax.ShapeDtypeStruct((1, LANES), F32), jax.ShapeDtypeStruct((1, d), F32),
                     jax.ShapeDtypeStruct((1, half), F32), jax.ShapeDtypeStruct((1, half), F32),
                     jax.ShapeDtypeStruct((d, d), F32)],
        compiler_params=_params("arbitrary"),
    )(o_sb, o_gd, proj_gates, proj_gates, x, target, w_out, sbw, gdw, fw)


def _piece_of_slot(s):
    return jnp.where(s < DPROJ_GDN_SLOT, s, jnp.where(s < DPROJ_GATE_SLOT, s + 1,
                                                     jnp.where(s == DPROJ_GATE_SLOT, 3, 7)))


def _gw_in_call(h_t, dproj8):
    d, t_len = h_t.shape
    n_piece, _, pw = dproj8.shape

    def body(ht_ref, dp_ref, gw_ref):
        gw_ref[...] = jnp.dot(ht_ref[...], dp_ref[0].astype(MXU_DTYPE), preferred_element_type=F32)

    return pl.pallas_call(
        body, name="gw_in",
        grid=(n_piece,),
        in_specs=[pl.BlockSpec((d, t_len), lambda s: (0, 0)),
                  pl.BlockSpec((1, t_len, pw), lambda s: (s, 0, 0))],
        out_specs=pl.BlockSpec((d, pw), lambda s: (0, _piece_of_slot(s))),
        out_shape=jax.ShapeDtypeStruct((d, n_piece * pw), F32),
        compiler_params=_params("arbitrary"),
    )(h_t, dproj8)


def _slot_of_piece(p):
    return jnp.where(p < DPROJ_GDN_SLOT, p, jnp.where(p == 3, DPROJ_GATE_SLOT, jnp.where(p < 7, p - 1, 7)))


def _gw_in_shards_call(h_t, dproj8, dsmall, out_dtype):
    d, t_len = h_t.shape
    n_piece, _, pw = dproj8.shape
    ns = dsmall.shape[1]
    n_pairs = N_DEV // 2

    def body(ht_ref, dp_ref, ds_ref, chip_ref, prev_ref, gates_ref, send_ref, recv_ref, send_sems, recv_sems):
        p = pl.program_id(0)
        x_pos, y_pos, c = lax.axis_index("x"), lax.axis_index("y"), lax.axis_index("c")
        to_sibling = lambda pair: pltpu.make_async_remote_copy(
            src_ref=send_ref.at[pair], dst_ref=recv_ref.at[pair], send_sem=send_sems.at[pair],
            recv_sem=recv_sems.at[pair], device_id=(x_pos, y_pos, 1 - c), device_id_type=_MESH)

        @pl.when(p == 0)
        def _():
            gates_ref[...] = jnp.dot(ht_ref[...], ds_ref[...].astype(MXU_DTYPE), preferred_element_type=F32)

        def emit(s, tail):
            x = jnp.concatenate([prev_ref[...], tail], axis=1)
            y = x if s == 0 else pltpu.roll(x, SHARD_PAD - s, axis=1)
            shard = y[:, :SHARD_COLS].astype(out_dtype)

            @pl.when(c == s % 2)
            def _():
                chip_ref[s // 2] = shard

            @pl.when(c != s % 2)
            def _():
                send_ref[s // 2] = shard
                to_sibling(s // 2).start()

        @pl.when(p < n_piece)
        def _():
            cur = jnp.dot(ht_ref[...], dp_ref[0].astype(MXU_DTYPE), preferred_element_type=F32)
            for s in range(n_piece - 1):
                pl.when(p == s + 1)(functools.partial(emit, s, cur[:, :SHARD_PAD - pw]))
            prev_ref[...] = cur

        @pl.when(p == n_piece)
        def _():
            emit(n_piece - 1, gates_ref[...])
            for pair in range(n_pairs):
                to_sibling(pair).wait_send()
            for pair in range(n_pairs):
                to_sibling(pair).wait_recv()
                chip_ref[pair] = (chip_ref[pair].astype(F32) + recv_ref[pair].astype(F32)).astype(out_dtype)

    shards_of_side = lambda: pltpu.VMEM((n_pairs, d, SHARD_COLS), out_dtype)
    return pl.pallas_call(
        body, name="gw_in",
        grid=(n_piece + 1,),
        in_specs=[pl.BlockSpec((d, t_len), lambda p: (0, 0)),
                  pl.BlockSpec((1, t_len, pw), lambda p: (_slot_of_piece(jnp.minimum(p, n_piece - 1)), 0, 0)),
                  pl.BlockSpec((t_len, ns), lambda p: (0, 0))],
        out_specs=pl.BlockSpec((n_pairs, d, SHARD_COLS), lambda p: (0, 0, 0)),
        out_shape=jax.ShapeDtypeStruct((n_pairs, d, SHARD_COLS), out_dtype),
        scratch_shapes=[pltpu.VMEM((d, pw), F32), pltpu.VMEM((d, ns), F32), shards_of_side(), shards_of_side(),
                        pltpu.SemaphoreType.DMA((n_pairs,)), pltpu.SemaphoreType.DMA((n_pairs,))],
        compiler_params=_params("arbitrary"),
    )(h_t, dproj8, dsmall)


def _gw_small_call(h_t, dsmall, tm=512):
    d, t_len = h_t.shape
    ns = dsmall.shape[1]

    def body(ht_ref, dp_ref, gw_ref):
        @pl.when(pl.program_id(0) == 0)
        def _():
            gw_ref[...] = jnp.zeros_like(gw_ref)

        gw_ref[...] += jnp.dot(ht_ref[...], dp_ref[...].astype(MXU_DTYPE), preferred_element_type=F32)

    return pl.pallas_call(
        body, name="gw_small",
        grid=(t_len // tm,),
        in_specs=[pl.BlockSpec((d, tm), lambda t: (0, t)),
                  pl.BlockSpec((tm, ns), lambda t: (t, 0))],
        out_specs=pl.BlockSpec((d, ns), lambda t: (0, 0)),
        out_shape=jax.ShapeDtypeStruct((d, ns), F32),
        compiler_params=_params("arbitrary"),
    )(h_t, dsmall)


def _dx_call(dproj8, dsmall, w_main, w_small, x, r, dx2, norm_w, chip_scatter=(), peer_scatter=(), tm=256):
    t_len, d = x.shape
    n_piece, _, pw = dproj8.shape
    ns = dsmall.shape[1]
    nx = len(chip_scatter)
    n_peer = len(peer_scatter)
    steps = t_len // tm
    n_in = 8 + nx + n_peer
    n_out = 2 + nx + n_peer

    def body(*refs):
        dp_ref, ds_ref, wm_ref, ws_ref, x_ref, r_ref, dx2_ref, nw_ref = refs[:8]
        gx_ref, gnw_ref = refs[n_in:n_in + 2]
        scratch = refs[n_in + n_out:]
        copies = lambda: _chip_copies(refs[8:8 + nx], refs[n_in + 2:n_in + 2 + nx], *scratch[:3])
        if nx:
            pl.when(pl.program_id(0) == 0)(lambda: _start_all(copies()))

        @pl.when(pl.program_id(0) == 0)
        def _():
            gnw_ref[...] = jnp.zeros_like(gnw_ref)

        dh = lax.dot_general(ds_ref[...].astype(MXU_DTYPE), ws_ref[...], _NT, preferred_element_type=F32)
        for s, p in enumerate(DPROJ_PIECE_OF_SLOT):
            dh = dh + lax.dot_general(dp_ref[s].astype(MXU_DTYPE), wm_ref[:, p * pw:(p + 1) * pw], _NT,
                                      preferred_element_type=F32)
        xv, rv = x_ref[...], r_ref[...]
        dn = dh * nw_ref[...]
        gx_ref[...] = dx2_ref[...] + rv * dn - xv * ((rv * rv * rv) * jnp.mean(dn * xv, axis=-1, keepdims=True))
        gnw_ref[...] += jnp.sum(dh * xv * rv, axis=0, keepdims=True)

        @pl.when(pl.program_id(0) == steps - 1)
        def _():
            if n_peer:
                small_ref, parts_ref = refs[8 + nx:n_in]
                small_buf = scratch[6]
                small_buf[...] = small_ref[...]
                small_buf[0:1, :] = gnw_ref[...]
                peer_copies = _direct_copies([small_buf, parts_ref], refs[n_in + 2 + nx:n_in + n_out], *scratch[3:6],
                                             [False, True])
                _start_all(peer_copies)
            if nx:
                _wait_all(copies())
            if n_peer:
                _wait_all(peer_copies)

    assert n_peer in (0, 2) and (nx == 1 or not n_peer)
    peer_in_specs = [pl.BlockSpec(peer_scatter[0].shape, lambda i: (0, 0)), _HBM] if n_peer else []
    peer_scratch = _direct_semaphores(n_peer) + [pltpu.VMEM(peer_scatter[0].shape, F32)] if n_peer else []
    return pl.pallas_call(
        body, name="dx",
        grid=(steps,),
        in_specs=[pl.BlockSpec((n_piece, tm, pw), lambda i: (0, i, 0)),
                  pl.BlockSpec((tm, ns), lambda i: (i, 0)),
                  pl.BlockSpec((d, n_piece * pw), lambda i: (0, 0)),
                  pl.BlockSpec((d, ns), lambda i: (0, 0)),
                  pl.BlockSpec((tm, d), lambda i: (i, 0)),
                  pl.BlockSpec((tm, 1), lambda i: (i, 0)),
                  pl.BlockSpec((tm, d), lambda i: (i, 0)),
                  pl.BlockSpec((1, d), lambda i: (0, 0))] + [_HBM] * nx + peer_in_specs,
        out_specs=[pl.BlockSpec((tm, d), lambda i: (i, 0)),
                   pl.BlockSpec((1, d), lambda i: (0, 0))] + [_HBM] * (nx + n_peer),
        out_shape=[jax.ShapeDtypeStruct((t_len, d), F32), jax.ShapeDtypeStruct((1, d), F32)]
                  + [jax.ShapeDtypeStruct(a.shape, a.dtype) for a in chip_scatter]
                  + (_direct_out_shapes(peer_scatter, [False, True]) if n_peer else []),
        scratch_shapes=(_chip_semaphores(nx) if nx else []) + peer_scratch,
        compiler_params=_params("arbitrary"),
    )(dproj8, dsmall, w_main, w_small, x, r, dx2, norm_w, *chip_scatter, *peer_scatter)


def _direct_out_shapes(srcs, per_peer):
    return [jax.ShapeDtypeStruct(s.shape if pp else (N_DEV,) + s.shape, s.dtype) for s, pp in zip(srcs, per_peer)]


def _direct_semaphores(n):
    return [pltpu.SemaphoreType.DMA((n * (N_DEV - 1),)), pltpu.SemaphoreType.DMA((n * (N_DEV - 1),)),
            pltpu.SemaphoreType.DMA((n,))]


def _direct_copies(src_refs, out_refs, send_sems, recv_sems, local_sems, per_peer):
    x, y, c = lax.axis_index("x"), lax.axis_index("y"), lax.axis_index("c")
    me = 4 * x + 2 * y + c
    local, remote = [], []
    for a in range(len(src_refs)):
        mine = src_refs[a].at[me] if per_peer[a] else src_refs[a]
        local.append(pltpu.make_async_copy(mine, out_refs[a].at[me], local_sems.at[a]))
    for k in range(1, N_DEV):
        kx, ky, kc = (k >> 2) & 1, (k >> 1) & 1, k & 1
        px = 1 - x if kx else x
        py = 1 - y if ky else y
        pc = 1 - c if kc else c
        peer = 4 * px + 2 * py + pc
        for a in range(len(src_refs)):
            sem = a * (N_DEV - 1) + (k - 1)
            remote.append(pltpu.make_async_remote_copy(
                src_ref=src_refs[a].at[peer] if per_peer[a] else src_refs[a], dst_ref=out_refs[a].at[me],
                send_sem=send_sems.at[sem], recv_sem=recv_sems.at[sem],
                device_id=(px, py, pc), device_id_type=pl.DeviceIdType.MESH))
    return local, remote


def _start_all(copies):
    local, remote = copies
    for cp in local + remote:
        cp.start()


def _wait_all(copies):
    local, remote = copies
    for cp in remote:
        cp.wait_send()
    for cp in remote:
        cp.wait_recv()
    for cp in local:
        cp.wait()


N_CHIPS = 4
_HBM = pl.BlockSpec(memory_space=pl.ANY)
_MESH = pl.DeviceIdType.MESH


def _gather_call(name, srcs):
    n = len(srcs)
    per = N_DEV - 1

    def body(*refs):
        src_refs, out_refs = refs[:n], refs[n:2 * n]
        send_sems, recv_sems, local_sems = refs[2 * n:]
        x, y, c = lax.axis_index("x"), lax.axis_index("y"), lax.axis_index("c")
        me, sibling = (x, y, c), (x, y, 1 - c)
        x_nbr, y_nbr, diagonal = (1 - x, y), (x, 1 - y), (1 - x, 1 - y)
        held = ((1 - x) * c + x * (1 - c), y * c + (1 - y) * (1 - c))
        onward = (x * c + (1 - x) * (1 - c), (1 - y) * c + y * (1 - c))
        slot = lambda px, py, pc: 4 * px + 2 * py + pc

        def copy(a, k, block, to, from_src=False):
            rows = out_refs[a].at[slot(*block)]
            return pltpu.make_async_remote_copy(
                src_ref=src_refs[a] if from_src else rows, dst_ref=rows,
                send_sem=send_sems.at[a * per + k], recv_sem=recv_sems.at[a * per + k],
                device_id=to, device_id_type=_MESH)

        local = [pltpu.make_async_copy(src_refs[a], out_refs[a].at[slot(*me)], local_sems.at[a]) for a in range(n)]
        started = []

        def start(cp):
            cp.start()
            started.append(cp)

        for cp in local:
            cp.start()
        for a in range(n):
            start(copy(a, 0, me, sibling, True))
            start(copy(a, 1, me, (*x_nbr, c), True))
            start(copy(a, 2, me, (*y_nbr, c), True))
        for a in range(n):
            copy(a, 1, (*x_nbr, c), me).wait_recv()
            copy(a, 2, (*y_nbr, c), me).wait_recv()
            start(copy(a, 3, (*held, c), (*onward, c)))
            start(copy(a, 4, (*x_nbr, c), sibling))
            start(copy(a, 5, (*y_nbr, c), sibling))
        for a in range(n):
            copy(a, 3, (*diagonal, c), me).wait_recv()
            start(copy(a, 6, (*diagonal, c), sibling))
        for a in range(n):
            copy(a, 0, sibling, me).wait_recv()
            for k, chip in ((4, x_nbr), (5, y_nbr), (6, diagonal)):
                copy(a, k, (*chip, 1 - c), me).wait_recv()
        for cp in started:
            cp.wait_send()
        for cp in local:
            cp.wait()

    return pl.pallas_call(
        body, name=name,
        in_specs=[_HBM] * n, out_specs=[_HBM] * n,
        out_shape=[jax.ShapeDtypeStruct((N_DEV,) + s.shape, s.dtype) for s in srcs],
        scratch_shapes=[pltpu.SemaphoreType.DMA((n * per,)), pltpu.SemaphoreType.DMA((n * per,)),
                        pltpu.SemaphoreType.DMA((n,))],
    )(*srcs)


def _chip_semaphores(n):
    per = N_CHIPS - 1
    return [pltpu.SemaphoreType.DMA((n * per,)), pltpu.SemaphoreType.DMA((n * per,)), pltpu.SemaphoreType.DMA((n,))]


def _chip_copies(src_refs, out_refs, send_sems, recv_sems, local_sems):
    per = N_CHIPS - 1
    x, y, c = lax.axis_index("x"), lax.axis_index("y"), lax.axis_index("c")
    mine = 2 * x + y
    chips = [(1 - x, y), (x, 1 - y), (1 - x, 1 - y)]
    n = len(src_refs)
    local = [pltpu.make_async_copy(src_refs[a].at[mine], out_refs[a].at[mine], local_sems.at[a]) for a in range(n)]
    remote = []
    for a in range(n):
        for j, (px, py) in enumerate(chips):
            remote.append(pltpu.make_async_remote_copy(
                src_ref=src_refs[a].at[2 * px + py], dst_ref=out_refs[a].at[mine],
                send_sem=send_sems.at[a * per + j], recv_sem=recv_sems.at[a * per + j],
                device_id=(px, py, c), device_id_type=_MESH))
    return local, remote


def _adam_call(name, parts, w, m, v, tr):
    rows, cols = w.shape
    n_slots = parts.shape[0]

    def body(p_ref, w_ref, m_ref, v_ref, g_ref, d_ref, nm_ref, nv_ref):
        g = p_ref[0].astype(F32)
        for s in range(1, n_slots):
            g = g + p_ref[s].astype(F32)
        m_new = ADAM_B1 * m_ref[...] + (1.0 - ADAM_B1) * g
        v_new = ADAM_B2 * v_ref[...] + (1.0 - ADAM_B2) * (g * g)
        m_hat = m_new / (1.0 - ADAM_B1 ** ADAM_STEP)
        v_hat = v_new / (1.0 - ADAM_B2 ** ADAM_STEP)
        g_ref[...] = g
        d_ref[...] = -ADAM_LR * (m_hat / (jnp.sqrt(v_hat) + ADAM_EPS) + ADAM_WD * w_ref[...])
        nm_ref[...] = m_new
        nv_ref[...] = v_new

    blk = pl.BlockSpec((tr, cols), lambda i: (i, 0))
    return pl.pallas_call(
        body, name=name,
        grid=(rows // tr,),
        in_specs=[pl.BlockSpec((n_slots, tr, cols), lambda i: (0, i, 0)), blk, blk, blk],
        out_specs=[blk] * 4,
        out_shape=[jax.ShapeDtypeStruct((rows, cols), F32)] * 4,
        compiler_params=_params("arbitrary"),
    )(parts, w, m, v)


def _columns_to_rows_call(name, w_t, rows, dtype):
    row_tiles = rows // LANES
    cols = w_t.shape[0] // row_tiles
    whole = cols // LANES * LANES

    def body(w_ref, out_ref):
        diagonal = (lax.broadcasted_iota(jnp.int32, (LANES, LANES), 0)
                    == lax.broadcasted_iota(jnp.int32, (LANES, LANES), 1))
        for a in range(row_tiles):
            out_ref[a * LANES:(a + 1) * LANES, :whole] = (
                w_ref[pl.ds(a, whole, stride=row_tiles), :].T.astype(dtype))
            for c in range(whole, cols):
                column = w_ref[pl.ds(c * row_tiles + a, 1), :]
                upright = jnp.sum(jnp.where(diagonal, column, 0.0), axis=1, keepdims=True)
                out_ref[a * LANES:(a + 1) * LANES, c:c + 1] = upright.astype(dtype)

    vm = pl.BlockSpec(memory_space=pltpu.VMEM)
    return pl.pallas_call(
        body, name=name,
        in_specs=[vm], out_specs=vm,
        out_shape=jax.ShapeDtypeStruct((rows, cols), dtype),
        compiler_params=pltpu.CompilerParams(vmem_limit_bytes=VMEM_LIMIT_BYTES),
    )(w_t)


def _adam_columns_call(name, parts, w_t, m_t, v_t):
    n_slots, rows, cols = parts.shape
    row_tiles = rows // LANES
    whole_groups = cols // LANES
    rest = cols - whole_groups * LANES

    def body(p_ref, w_ref, m_ref, v_ref, *out_refs):
        def update(a, n_cols, column_rows):
            g = p_ref[0, a * LANES:(a + 1) * LANES, 0:n_cols].astype(F32)
            for s in range(1, n_slots):
                g = g + p_ref[s, a * LANES:(a + 1) * LANES, 0:n_cols].astype(F32)
            if n_cols == LANES:
                g = g.T
            else:
                diagonal = (lax.broadcasted_iota(jnp.int32, (LANES, LANES), 0)
                            == lax.broadcasted_iota(jnp.int32, (LANES, LANES), 1))
                g = jnp.concatenate([jnp.sum(jnp.where(diagonal, g[:, c:c + 1], 0.0), axis=0, keepdims=True)
                                     for c in range(n_cols)], axis=0)
            results = (g,) + _adamw(g, w_ref[column_rows, :], m_ref[column_rows, :], v_ref[column_rows, :])
            for out_ref, val in zip(out_refs, results):
                out_ref[column_rows, :] = val

        @pl.when(pl.program_id(0) < whole_groups)
        def _():
            for a in range(row_tiles):
                update(a, LANES, pl.ds(a, LANES, stride=row_tiles))

        @pl.when(pl.program_id(0) == whole_groups)
        def _():
            for a in range(row_tiles):
                update(a, rest, pl.ds(a, rest, stride=row_tiles))

    group = pl.BlockSpec((LANES * row_tiles, LANES), lambda j: (j, 0))
    return pl.pallas_call(
        body, name=name,
        grid=(whole_groups + 1,),
        in_specs=[pl.BlockSpec((n_slots, rows, LANES), lambda j: (0, 0, j)), group, group, group],
        out_specs=[group] * 4,
        out_shape=[jax.ShapeDtypeStruct(w_t.shape, F32)] * 4,
        compiler_params=_params("arbitrary"),
    )(parts, w_t, m_t, v_t)


N_PIECES = 8
PIECE = 512
SHARD_COLS = 513
SHARD_PAD = 640
RELAYOUT_ROWS = 256


def _from_shards_call(shards):
    _, d, _ = shards.shape
    tr = RELAYOUT_ROWS

    def body(p_ref, m_ref, s_ref):
        lane = lax.broadcasted_iota(jnp.int32, (tr, SHARD_PAD), 1)
        pad = jnp.zeros((tr, SHARD_PAD - SHARD_COLS), p_ref.dtype)
        sh = [jnp.concatenate([p_ref[s], pad], axis=1) for s in range(N_DEV)]
        for p in range(N_PIECES):
            y = sh[p] if p == 0 else pltpu.roll(sh[p], p, axis=1)
            if p > 0:
                y = jnp.where(lane < p, pltpu.roll(sh[p - 1], SHARD_PAD - (SHARD_COLS - p), axis=1), y)
            m_ref[:, p * PIECE:(p + 1) * PIECE] = y[:, :PIECE].astype(m_ref.dtype)
        first_gate = N_PIECES * PIECE - (N_DEV - 1) * SHARD_COLS
        s_ref[...] = pltpu.roll(sh[N_DEV - 1], SHARD_PAD - first_gate, axis=1)[:, :LANES].astype(s_ref.dtype)

    return pl.pallas_call(
        body, name="w_in_from_shards",
        grid=(d // tr,),
        in_specs=[pl.BlockSpec((N_DEV, tr, SHARD_COLS), lambda i: (0, i, 0))],
        out_specs=[pl.BlockSpec((tr, N_PIECES * PIECE), lambda i: (i, 0)), pl.BlockSpec((tr, LANES), lambda i: (i, 0))],
        out_shape=[jax.ShapeDtypeStruct((d, N_PIECES * PIECE), shards.dtype),
                   jax.ShapeDtypeStruct((d, LANES), shards.dtype)],
        compiler_params=_params("arbitrary"),
    )(shards)


def _adamw(g, w, m, v):
    m_new = ADAM_B1 * m + (1.0 - ADAM_B1) * g
    v_new = ADAM_B2 * v + (1.0 - ADAM_B2) * (g * g)
    m_hat = m_new / (1.0 - ADAM_B1 ** ADAM_STEP)
    v_hat = v_new / (1.0 - ADAM_B2 ** ADAM_STEP)
    return -ADAM_LR * (m_hat / (jnp.sqrt(v_hat) + ADAM_EPS) + ADAM_WD * w), m_new, v_new


def _adam_small_call(parts, ws, ms, vs):
    n = len(ws)
    n_slots = parts.shape[0]

    def body(*refs):
        p_ref = refs[0]
        w_refs, m_refs, v_refs = refs[1:1 + n], refs[1 + n:1 + 2 * n], refs[1 + 2 * n:1 + 3 * n]
        loss_ref = refs[1 + 3 * n]
        outs = refs[2 + 3 * n:]
        g_all = p_ref[0]
        for s in range(1, n_slots):
            g_all = g_all + p_ref[s]
        loss_ref[...] = g_all[n:n + 1, 0:1]
        for r in range(n):
            size = w_refs[r].shape[1]
            g = g_all[r:r + 1, :size]
            delta, m_new, v_new = _adamw(g, w_refs[r][...], m_refs[r][...], v_refs[r][...])
            for kind, val in enumerate((g, delta, m_new, v_new)):
                outs[kind * n + r][...] = val

    vm = pl.BlockSpec(memory_space=pltpu.VMEM)
    shapes = [jax.ShapeDtypeStruct(w.shape, F32) for w in ws]
    return pl.pallas_call(
        body, name="adam_small",
        in_specs=[vm] * (1 + 3 * n), out_specs=[vm] * (1 + 4 * n),
        out_shape=[jax.ShapeDtypeStruct((1, 1), F32)] + shapes * 4,
    )(parts, *ws, *ms, *vs)


_SMALL_ROWS = ("norm1_w", "final_norm_w", "sb_norm_w", "gdn_norm_w", "gdn_A_log", "gdn_dt_bias", "loss")


def _pack_small(vals, width):
    rows = [jnp.pad(a.reshape(1, -1).astype(F32), ((0, 0), (0, width - a.size))) for a in vals]
    rows += [jnp.zeros((1, width), F32)] * (8 - len(rows))
    return jnp.concatenate(rows, axis=0)


def _device_step(x2d, tgt, w_main, w_small, w_out_full, conv_full, norm1_w, sb_norm_w, gdn_A_log, gdn_dt_bias,
                 gdn_norm_w, final_norm_w, distributed=False):
    t_len, d = x2d.shape
    n_chunks = t_len // CHUNK
    w_main, w_small, w_out_full = (a.astype(MXU_DTYPE) for a in (w_main, w_small, w_out_full))
    w_small_t = w_small[:, :2 * GDN_HEADS].T

    pad_lanes = lambda a, lo: jnp.pad(a.reshape(1, -1), ((0, 0), (lo, LANES - lo - a.size)))
    alog_l, dtb_l = pad_lanes(gdn_A_log, GDN_HEADS), pad_lanes(gdn_dt_bias, GDN_HEADS)
    alog_c, dtb_c = alog_l[:, :8].T, dtb_l[:, :8].T
    sbw = jnp.tile(sb_norm_w, (1, 512 // SB_HEAD_DIM))
    gdw = jnp.tile(gdn_norm_w, (1, 512 // GDN_HEAD_DIM))
    fw = final_norm_w.reshape(1, d)

    if distributed:
        proj_cols, proj_gates, ps, pst, h_t, r1, w_out_g, conv_g = _inproj_call(
            x2d, norm1_w, w_main, w_small, w_small_t, gather=(w_out_full, conv_full))
        w_out_full = w_out_g.reshape(d, d)
        conv_full = conv_g.transpose(1, 0, 2).reshape(CONV_WIDTH, N_DEV * conv_g.shape[2])
    else:
        proj_cols, proj_gates, ps, pst, h_t, r1 = _inproj_call(x2d, norm1_w, w_main, w_small, w_small_t)
    o_sb, sp_total, sb_blocks_run = _sb_fwd_call(proj_cols, t_len)
    gact = _gdn_prep_call(proj_cols, conv_full, t_len, after=sp_total)
    beta_l, gcol_l, grow = _gdn_gates_call(ps, pst, alog_l, dtb_l, alog_c, dtb_c, t_len)
    gam_r = grow[GDN_HEADS:2 * GDN_HEADS].reshape(GDN_HEADS, n_chunks, 1, CHUNK)
    o_gd, *gdn_saved = _gdn_fwd_call(gact, beta_l, gcol_l, gam_r, t_len)

    (dx2, d_osb, d_ogd, dproj8, loss_p, g_fw, g_sbw, g_gdw, g_wout) = _post_call(
        o_sb, o_gd, proj_gates, x2d, tgt, w_out_full, sbw, gdw, fw)

    dproj8 = _sb_bwd_call(proj_cols, sp_total, sb_blocks_run, d_osb, dproj8, t_len)
    if distributed:
        d_gact3, d_gates, g_wout = _gdn_bwd_call(gact, beta_l, gcol_l, gam_r, gdn_saved, d_ogd, t_len,
                                                 scatter=(g_wout.reshape(N_DEV, d // N_DEV, d),))
    else:
        d_gact3, d_gates = _gdn_bwd_call(gact, beta_l, gcol_l, gam_r, gdn_saved, d_ogd, t_len)
    dproj8, g_conv = _gdn_prep_bwd_call(proj_cols, conv_full, d_gact3, dproj8, t_len)
    dsmall, g_alog, g_dtb = _gdn_gates_bwd_call(ps, alog_l, dtb_l, d_gates, t_len)

    if distributed:
        chip_partials = _gw_in_shards_call(h_t, dproj8, dsmall, WIRE_DTYPE)
        fold = lambda a, group: a.reshape(-1, group).sum(axis=0)
        small_g = _pack_small([jnp.zeros((d,), F32), g_fw, fold(g_sbw, SB_HEAD_DIM), fold(g_gdw, GDN_HEAD_DIM),
                               g_alog[0, GDN_HEADS:2 * GDN_HEADS], g_dtb[0, GDN_HEADS:2 * GDN_HEADS],
                               loss_p[0, :1]], d)
        conv_cols = g_conv.shape[1] // N_DEV
        g_conv_parts = g_conv.reshape(CONV_WIDTH, N_DEV, conv_cols).transpose(1, 0, 2)
        grad_x, _, g_w_in, p_small, p_conv = _dx_call(dproj8, dsmall, w_main, w_small, x2d, r1, dx2, norm1_w,
                                                      chip_scatter=(chip_partials,),
                                                      peer_scatter=(small_g, g_conv_parts))
        return grad_x, g_w_in, g_wout, p_small, p_conv
    else:
        grad_x, g_n1 = _dx_call(dproj8, dsmall, w_main, w_small, x2d, r1, dx2, norm1_w)
        g_w_in = (_gw_in_call(h_t, dproj8), _gw_small_call(h_t, dsmall))
    return (loss_p, grad_x, g_n1, g_w_in, g_sbw, g_conv, g_alog, g_dtb, g_gdw, g_wout, g_fw)


def kernel(x, norm1_w, w_in, sb_norm_w, gdn_conv_w, gdn_A_log, gdn_dt_bias, gdn_norm_w, w_out, final_norm_w, loss_target, m_norm1_w, m_w_in, m_sb_norm_w, m_gdn_conv_w, m_gdn_A_log, m_gdn_dt_bias, m_gdn_norm_w, m_w_out, m_final_norm_w, v_norm1_w, v_w_in, v_sb_norm_w, v_gdn_conv_w, v_gdn_A_log, v_gdn_dt_bias, v_gdn_norm_w, v_w_out, v_final_norm_w):
    d = x.shape[2]
    shard_cols = w_in.shape[2]

    columns = lambda a: a.transpose(2, 0, 1).reshape(shard_cols * d // LANES, LANES)
    from_columns = lambda a: a.reshape(shard_cols, d // LANES, LANES).transpose(1, 2, 0).reshape(1, d, shard_cols)
    (w_in_g,) = _gather_call("gather_weights", [_columns_to_rows_call("w_in_to_wire", columns(w_in), d, WIRE_DTYPE)])
    w_main, w_small = _from_shards_call(w_in_g)

    grad_x, p_w_in, p_wout, p_small, p_conv = _device_step(
        x[0], loss_target[0], w_main, w_small, w_out[0].astype(WIRE_DTYPE), gdn_conv_w[0], norm1_w, sb_norm_w,
        gdn_A_log, gdn_dt_bias, gdn_norm_w, final_norm_w, distributed=True)

    r_w_in = [from_columns(a) for a in _adam_columns_call("adam_w_in", p_w_in, columns(w_in), columns(m_w_in),
                                                          columns(v_w_in))]
    r_wout = _adam_call("adam_w_out", p_wout, w_out[0], m_w_out[0], v_w_out[0], d // N_DEV)
    r_conv = _adam_call("adam_conv", p_conv, gdn_conv_w[0], m_gdn_conv_w[0], v_gdn_conv_w[0], CONV_WIDTH)

    row = lambda a: a.reshape(1, -1)
    n_small = len(_SMALL_ROWS) - 1
    r_small = _adam_small_call(
        p_small,
        [norm1_w, row(final_norm_w), sb_norm_w, gdn_norm_w, gdn_A_log, gdn_dt_bias],
        [m_norm1_w, row(m_final_norm_w), m_sb_norm_w, m_gdn_norm_w, m_gdn_A_log, m_gdn_dt_bias],
        [v_norm1_w, row(v_final_norm_w), v_sb_norm_w, v_gdn_norm_w, v_gdn_A_log, v_gdn_dt_bias])

    def small_out(kind, name):
        out = r_small[1 + kind * n_small + _SMALL_ROWS.index(name)]
        return out.reshape(final_norm_w.shape) if name == "final_norm_w" else out

    def outputs(kind):
        return (small_out(kind, "norm1_w"), r_w_in[kind], small_out(kind, "sb_norm_w"), r_conv[kind][None],
                small_out(kind, "gdn_A_log"), small_out(kind, "gdn_dt_bias"), small_out(kind, "gdn_norm_w"),
                r_wout[kind][None], small_out(kind, "final_norm_w"))

    return (r_small[0][0, 0], grad_x[None], *outputs(0), *outputs(1), *outputs(2), *outputs(3))
```

```python
import functools

import jax
import jax.numpy as jnp
from jax import lax
from jax.experimental import pallas as pl
from jax.experimental.pallas import tpu as pltpu

F32 = jnp.float32
MXU_DTYPE = jnp.bfloat16
WIRE_DTYPE = jnp.bfloat16
EXACT = lax.Precision.HIGHEST
EPS = 1e-6
N_DEV = 8
SB_HEAD_DIM = 64
GDN_HEAD_DIM = 128
GDN_HEADS = 4
GDN_CHUNKS_PER_STEP = 4
GDN_BWD_GROUP = 1
CHUNK = 64
CONV_WIDTH = 4
LANES = 128
SB_BLOCK = 128
SB_BQ = 256
VMEM_LIMIT_BYTES = 56 * 1024 * 1024

PIECE_COLS = 512
PROJ_PIECE_KINDS = ("heads", "heads", "heads", "gate", "heads", "heads", "heads", "gate")
SB_FIRST_BLOCK, GDN_FIRST_BLOCK = 0, 12

DPROJ_PIECE_OF_SLOT = (0, 1, 2, 4, 5, 6, 3, 7)
DPROJ_SB_SLOT, DPROJ_GDN_SLOT, DPROJ_GATE_SLOT = 0, 3, 6

ADAM_LR = 0.001
ADAM_B1 = 0.9
ADAM_B2 = 0.999
ADAM_EPS = 1e-08
ADAM_WD = 0.01
ADAM_STEP = 10

_NN = (((1,), (0,)), ((), ()))
_NT = (((1,), (1,)), ((), ()))
_TN = (((0,), (0,)), ((), ()))
_BNN = (((2,), (1,)), ((0,), (0,)))
_BNT = (((2,), (2,)), ((0,), (0,)))
_BTN = (((1,), (1,)), ((0,), (0,)))


def _mx(a, b):
    return jnp.dot(a, b, precision=EXACT, preferred_element_type=F32)


def _split(x):
    hi = x.astype(MXU_DTYPE)
    return hi, (x - hi.astype(F32)).astype(MXU_DTYPE)


def _m3_general(a, b, dims, right_low=True):
    ah, al = _split(a)
    bh, bl = _split(b)
    dot = lambda x, y: lax.dot_general(x, y, dims, preferred_element_type=F32)
    (contract, _), (batch, _) = dims
    free = [ax for ax in range(a.ndim) if ax not in contract and ax not in batch][0]
    m = a.shape[free]
    both = dot(jnp.concatenate([ah, al], axis=free), bh)
    out_axis = len(batch)
    hi_part = lax.slice_in_dim(both, 0, m, axis=out_axis)
    lo_part = lax.slice_in_dim(both, m, 2 * m, axis=out_axis)
    return hi_part + (dot(ah, bl) + lo_part if right_low else lo_part)


def _times_exact(a, b_exact, dims):
    ah, al = _split(a)
    (contract, _), (batch, _) = dims
    free = [ax for ax in range(a.ndim) if ax not in contract and ax not in batch][0]
    m = a.shape[free]
    both = lax.dot_general(jnp.concatenate([ah, al], axis=free), b_exact.astype(MXU_DTYPE), dims,
                           preferred_element_type=F32)
    out_axis = len(batch)
    return lax.slice_in_dim(both, 0, m, axis=out_axis) + lax.slice_in_dim(both, m, 2 * m, axis=out_axis)


def _exact_times(a_exact, b, dims):
    bh, bl = _split(b)
    n = b.shape[-1]
    both = lax.dot_general(a_exact.astype(MXU_DTYPE), jnp.concatenate([bh, bl], axis=-1), dims,
                           preferred_element_type=F32)
    return both[..., :n] + both[..., n:]


def _sigmoid(z):
    return 1.0 / (1.0 + jnp.exp(-z))


def _softplus(z):
    return jnp.maximum(z, 0.0) + jnp.log(1.0 + jnp.exp(-jnp.abs(z)))


def _params(*semantics):
    return pltpu.CompilerParams(dimension_semantics=semantics, vmem_limit_bytes=VMEM_LIMIT_BYTES)


def _inproj_call(x, norm_w, w_main, w_small, w_small_t, gather=(), tm=256):
    t_len, d = x.shape
    n = w_main.shape[1]
    ns = w_small.shape[1]
    nst = w_small_t.shape[0]
    ng = len(gather)
    steps = t_len // tm

    blocks_per_piece = PIECE_COLS // LANES
    n_gate_cols = PIECE_COLS * PROJ_PIECE_KINDS.count("gate")
    n_col_blocks = blocks_per_piece * PROJ_PIECE_KINDS.count("heads")

    def body(*refs):
        x_ref, nw_ref, wm_ref, ws_ref, wst_ref = refs[:5]
        cols_ref, pz_ref, ps_ref, pst_ref, ht_ref, r_ref = refs[5 + ng:11 + ng]
        copies = lambda: _direct_copies(refs[5:5 + ng], refs[11 + ng:11 + 2 * ng], *refs[11 + 2 * ng:], (False,) * ng)
        if ng:
            pl.when(pl.program_id(0) == 0)(lambda: _start_all(copies()))
        xv = x_ref[...]
        r = lax.rsqrt(jnp.mean(xv * xv, axis=-1, keepdims=True) + EPS)
        h = xv * r * nw_ref[...]
        hb = h.astype(MXU_DTYPE)
        n_block = n_gate = 0
        for piece, kind in enumerate(PROJ_PIECE_KINDS):
            out = jnp.dot(hb, wm_ref[:, piece * PIECE_COLS:(piece + 1) * PIECE_COLS], preferred_element_type=F32)
            if kind == "gate":
                pz_ref[:, n_gate * PIECE_COLS:(n_gate + 1) * PIECE_COLS] = out
                n_gate += 1
            else:
                for j in range(blocks_per_piece):
                    cols_ref[n_block + j] = out[:, j * LANES:(j + 1) * LANES]
                n_block += blocks_per_piece
        ps_ref[...] = jnp.dot(hb, ws_ref[...], preferred_element_type=F32)
        pst_ref[...] = lax.dot_general(wst_ref[...], hb, _NT, preferred_element_type=F32)
        ht_ref[...] = h.T.astype(MXU_DTYPE)
        r_ref[...] = r
        if ng:
            pl.when(pl.program_id(0) == steps - 1)(lambda: _wait_all(copies()))

    return pl.pallas_call(
        body, name="inproj",
        grid=(steps,),
        in_specs=[pl.BlockSpec((tm, d), lambda i: (i, 0)),
                  pl.BlockSpec((1, d), lambda i: (0, 0)),
                  pl.BlockSpec((d, n), lambda i: (0, 0)),
                  pl.BlockSpec((d, ns), lambda i: (0, 0)),
                  pl.BlockSpec((nst, d), lambda i: (0, 0))] + [_HBM] * ng,
        out_specs=[pl.BlockSpec((n_col_blocks, tm, LANES), lambda i: (0, i, 0)),
                   pl.BlockSpec((tm, n_gate_cols), lambda i: (i, 0)),
                   pl.BlockSpec((tm, ns), lambda i: (i, 0)),
                   pl.BlockSpec((nst, tm), lambda i: (0, i)),
                   pl.BlockSpec((d, tm), lambda i: (0, i)),
                   pl.BlockSpec((tm, 1), lambda i: (i, 0))] + [_HBM] * ng,
        out_shape=[jax.ShapeDtypeStruct((n_col_blocks, t_len, LANES), F32),
                   jax.ShapeDtypeStruct((t_len, n_gate_cols), F32),
                   jax.ShapeDtypeStruct((t_len, ns), F32),
                   jax.ShapeDtypeStruct((nst, t_len), F32),
                   jax.ShapeDtypeStruct((d, t_len), MXU_DTYPE),
                   jax.ShapeDtypeStruct((t_len, 1), F32)] + _direct_out_shapes(gather, (False,) * ng),
        scratch_shapes=_direct_semaphores(ng) if ng else [],
        compiler_params=_params("arbitrary"),
    )(x, norm_w, w_main, w_small, w_small_t, *gather)


def _running_sum_mm(x, tri):
    hi = x.astype(MXU_DTYPE)
    lo = (x - hi.astype(F32)).astype(MXU_DTYPE)
    return jnp.dot(hi, tri, preferred_element_type=F32) + jnp.dot(lo, tri, preferred_element_type=F32)


def _col_block(t_len, first):
    return pl.BlockSpec((1, t_len, LANES), lambda p: (first + p, 0, 0))


def _sb_iotas():
    row_i = lax.broadcasted_iota(jnp.int32, (SB_BQ, SB_BLOCK), 0)
    col_i = lax.broadcasted_iota(jnp.int32, (SB_BQ, SB_BLOCK), 1)
    sq_r = lax.broadcasted_iota(jnp.int32, (SB_BLOCK, SB_BLOCK), 0)
    sq_c = lax.broadcasted_iota(jnp.int32, (SB_BLOCK, SB_BLOCK), 1)
    return row_i, col_i, sq_r, sq_c


SB_DIAG_BLOCKS = SB_BQ // SB_BLOCK
SB_EXP_FLOOR = -110.0


def _sb_keys_descending(qi, tile, carry, z_bounds, n_heads, has_free):
    group = SB_DIAG_BLOCKS
    n_free = group * qi
    diag = list(range(group - 1, -1, -1))
    carry = tile([n_free + j for j in diag], [True] * group, carry, [j * SB_BLOCK for j in diag])

    def largest_exponent(c):
        worst = jnp.max(z_bounds[0] - c[1])
        for h in range(1, n_heads):
            worst = jnp.maximum(worst, jnp.max(z_bounds[h] - c[1 + h]))
        return worst

    always = group if has_free else 0

    def cond(state):
        return (state[0] < n_free) & ((state[1] > SB_EXP_FLOOR) | (state[0] < always))

    def body(state):
        first = n_free - 1 - state[0]
        c = tile([first - j for j in range(group)], [False] * group, state[2:])
        return (state[0] + group, largest_exponent(c), *c)

    out = lax.while_loop(cond, body, (jnp.int32(0), largest_exponent(carry), *carry))
    return out[2:], out[0]


def _sb_keys_ascending(qi, n_run, tile, carry, has_free):
    group = SB_DIAG_BLOCKS
    n_free = group * qi
    diag = list(range(group))
    kjs, los, masked = [n_free + j for j in diag], [j * SB_BLOCK for j in diag], [True] * group
    if has_free:
        early = lambda s: [n_free - n_run + group * s + j for j in range(group)]
        carry = lax.fori_loop(0, n_run // group - 1, lambda s, c: tile(early(s), [False] * group, c), carry)
        kjs, los, masked = [n_free - group + j for j in range(group)] + kjs, [0] * group + los, [False] * group + masked
    return tile(kjs, masked, carry, los)


def _sb_fwd_call(cols, t_len):
    nq = t_len // SB_BQ
    scale = float(SB_HEAD_DIM) ** -0.5
    n_pairs = 512 // LANES
    per_pair = LANES // SB_HEAD_DIM

    def body(q_blk, k_blk, v_blk, o_blk, st_ref, nrun_ref):
        q_ref, k_ref, v_ref, o_ref = q_blk.at[0], k_blk.at[0], v_blk.at[0], o_blk.at[0]
        lane = lax.broadcasted_iota(jnp.int32, (1, LANES), 1)
        row_i, col_i, sq_r, sq_c = _sb_iotas()
        ge = (sq_r >= sq_c).astype(MXU_DTYPE)
        hms = [((lane // SB_HEAD_DIM) == hh).astype(F32) for hh in range(per_pair)]
        k_sq = k_ref[...] * k_ref[...]
        k_norms = [jnp.sqrt(jnp.max(jnp.sum(k_sq * hm, axis=-1, keepdims=True))) * (1.02 * scale) for hm in hms]

        def q_block(qi, has_free):
            r0 = qi * SB_BQ if isinstance(qi, int) else pl.multiple_of(qi * SB_BQ, SB_BQ)
            rows = pl.ds(r0, SB_BQ)
            q_all = q_ref[rows, :]
            qms = [(q_all * (hm * scale)).astype(MXU_DTYPE) for hm in hms]
            z_bounds = [jnp.sqrt(jnp.sum(q_all * q_all * hm, axis=-1, keepdims=True)) * kn
                        for hm, kn in zip(hms, k_norms)]

            def tile(kjs, masked, kc, los=None):
                heads = range(per_pair)
                los = los or [0] * len(kjs)
                pairs = [(t, h) for t in range(len(kjs)) for h in heads]
                add_rows = lambda full, lo, part: full + part if lo == 0 else jnp.concatenate(
                    [full[:lo], full[lo:] + part], axis=0)
                acc, cs = kc[0], list(kc[1:])
                s0s = [kj * SB_BLOCK if isinstance(kj, int) else pl.multiple_of(kj * SB_BLOCK, SB_BLOCK) for kj in kjs]
                kbs = [k_ref[pl.ds(s0, SB_BLOCK), :].astype(MXU_DTYPE) for s0 in s0s]
                v_alls = [v_ref[pl.ds(s0, SB_BLOCK), :] for s0 in s0s]
                vms = {(t, h): (v_alls[t] * hms[h]).astype(MXU_DTYPE) for t, h in pairs}
                zs = {(t, h): lax.dot_general(qms[h][los[t]:], kbs[t], _NT, preferred_element_type=F32)
                      for t, h in pairs}
                masks = [(col_i[lo:] + s0) < (row_i[lo:] + r0) if m else None for m, lo, s0 in zip(masked, los, s0s)]
                keep = lambda t, a: a if masks[t] is None else jnp.where(masks[t], a, 0.0)
                sps = {(t, h): keep(t, _softplus(zs[t, h])) for t, h in pairs}
                sums = {p: _running_sum_mm(sps[p], ge) for p in pairs}
                mass = {}
                for t, h in pairs:
                    mass[t, h] = cs[h] if t == 0 else add_rows(
                        mass[t - 1, h], los[t - 1], jnp.sum(sps[t - 1, h], axis=-1, keepdims=True))
                ws = {(t, h): keep(t, jnp.exp(zs[t, h] - (sums[t, h] + mass[t, h][los[t]:]))) for t, h in pairs}
                for t, h in pairs:
                    acc = add_rows(acc, los[t], jnp.dot(ws[t, h].astype(MXU_DTYPE), vms[t, h],
                                                        preferred_element_type=F32))
                last = len(kjs) - 1
                cs = [add_rows(mass[last, h], los[last], jnp.sum(sps[last, h], axis=-1, keepdims=True)) for h in heads]
                return (acc, *cs)

            zero_col = jnp.zeros((SB_BQ, 1), F32)
            out, n_run = _sb_keys_descending(
                qi, tile, (jnp.zeros((SB_BQ, LANES), F32),) + (zero_col,) * per_pair, z_bounds, per_pair, has_free)
            o_ref[rows, :] = out[0]
            masses = jnp.zeros((SB_BQ, LANES), F32)
            for hh in range(per_pair):
                masses = jnp.where(lane == hh, out[1 + hh], masses)
            st_ref[rows, :] = masses
            nrun_ref[pl.program_id(0), qi] = n_run

        q_block(0, False)
        lax.fori_loop(1, nq, lambda qi, carry: (q_block(qi, True), carry)[1], 0)

    return pl.pallas_call(
        body, name="sb_fwd",
        grid=(n_pairs,),
        in_specs=[_col_block(t_len, SB_FIRST_BLOCK), _col_block(t_len, SB_FIRST_BLOCK + n_pairs),
                  _col_block(t_len, SB_FIRST_BLOCK + 2 * n_pairs)],
        out_specs=[_col_block(t_len, 0),
                   pl.BlockSpec((t_len, LANES), lambda p: (0, p)),
                   pl.BlockSpec(memory_space=pltpu.SMEM)],
        out_shape=[jax.ShapeDtypeStruct((n_pairs, t_len, LANES), F32),
                   jax.ShapeDtypeStruct((t_len, n_pairs * LANES), F32),
                   jax.ShapeDtypeStruct((n_pairs, nq), jnp.int32)],
        compiler_params=_params("arbitrary"),
    )(cols, cols, cols)


def _sb_bwd_call(cols, sp_total, n_run_all, d_o, dproj, t_len):
    nq = t_len // SB_BQ
    scale = float(SB_HEAD_DIM) ** -0.5
    n_pairs = 512 // LANES
    per_pair = LANES // SB_HEAD_DIM

    def body(q_blk, k_blk, v_blk, st_ref, nrun_ref, do_blk, dproj_in_ref, d_ref):
        q_ref, k_ref, v_ref, do_ref = q_blk.at[0], k_blk.at[0], v_blk.at[0], do_blk.at[0]
        lane = lax.broadcasted_iota(jnp.int32, (1, LANES), 1)
        row_i, col_i, sq_r, sq_c = _sb_iotas()
        lt = (sq_r < sq_c).astype(MXU_DTYPE)
        le = (sq_r <= sq_c).astype(MXU_DTYPE)
        hms = [((lane // SB_HEAD_DIM) == hh).astype(F32) for hh in range(per_pair)]
        d_ref[1] = jnp.zeros((t_len, LANES), F32)
        d_ref[2] = jnp.zeros((t_len, LANES), F32)

        def q_block(qi, has_free):
            r0 = qi * SB_BQ if isinstance(qi, int) else pl.multiple_of(qi * SB_BQ, SB_BQ)
            rows = pl.ds(r0, SB_BQ)
            q_all, do_all = q_ref[rows, :], do_ref[rows, :]
            qms = [(q_all * (hm * scale)).astype(MXU_DTYPE) for hm in hms]
            doms = [(do_all * hm).astype(MXU_DTYPE) for hm in hms]
            masses = st_ref[rows, :]
            totals = [jnp.sum(jnp.where(lane == hh, masses, 0.0), axis=-1, keepdims=True) for hh in range(per_pair)]

            def tile(kjs, masked, kc, los=None):
                heads = range(per_pair)
                los = los or [0] * len(kjs)
                pairs = [(t, h) for t in range(len(kjs)) for h in heads]
                add_rows = lambda full, lo, part: full + part if lo == 0 else jnp.concatenate(
                    [full[:lo], full[lo:] + part], axis=0)
                rsum = lambda a: jnp.sum(a, axis=-1, keepdims=True)
                dq, cls, gls = kc[0], list(kc[1:1 + per_pair]), list(kc[1 + per_pair:])
                s0s = [kj * SB_BLOCK if isinstance(kj, int) else pl.multiple_of(kj * SB_BLOCK, SB_BLOCK) for kj in kjs]
                k_alls = [k_ref[pl.ds(s0, SB_BLOCK), :] for s0 in s0s]
                v_alls = [v_ref[pl.ds(s0, SB_BLOCK), :] for s0 in s0s]
                kbs = [k_all.astype(MXU_DTYPE) for k_all in k_alls]
                vms = {(t, h): (v_alls[t] * hms[h]).astype(MXU_DTYPE) for t, h in pairs}
                kms = {(t, h): (k_alls[t] * (hms[h] * scale)).astype(MXU_DTYPE) for t, h in pairs}
                q_live = {(t, h): qms[h][los[t]:] for t, h in pairs}
                do_live = {(t, h): doms[h][los[t]:] for t, h in pairs}
                zs = {p: lax.dot_general(q_live[p], kbs[p[0]], _NT, preferred_element_type=F32) for p in pairs}
                das = {p: lax.dot_general(do_live[p], vms[p], _NT, preferred_element_type=F32) for p in pairs}
                masks = [(col_i[lo:] + s0) < (row_i[lo:] + r0) if m else None for m, lo, s0 in zip(masked, los, s0s)]
                keep = lambda t, a: a if masks[t] is None else jnp.where(masks[t], a, 0.0)
                sp_alls = {p: _softplus(zs[p]) for p in pairs}
                sps = {(t, h): keep(t, sp_alls[t, h]) for t, h in pairs}
                lefts = {p: _running_sum_mm(sps[p], lt) for p in pairs}
                cl = {}
                for t, h in pairs:
                    cl[t, h] = cls[h] if t == 0 else add_rows(cl[t - 1, h], los[t - 1], rsum(sps[t - 1, h]))
                ws = {(t, h): keep(t, jnp.exp(zs[t, h] - ((totals[h] - cl[t, h])[los[t]:] - lefts[t, h])))
                      for t, h in pairs}
                gs = {p: das[p] * ws[p] for p in pairs}
                g_sums = {p: _running_sum_mm(gs[p], le) for p in pairs}
                gl = {}
                for t, h in pairs:
                    gl[t, h] = gls[h] if t == 0 else add_rows(gl[t - 1, h], los[t - 1], rsum(gs[t - 1, h]))
                dzs = {(t, h): keep(t, gs[t, h] - jnp.exp(zs[t, h] - sp_alls[t, h]) * (gl[t, h][los[t]:] + g_sums[t, h])
                               ).astype(MXU_DTYPE) for t, h in pairs}
                for t in range(len(kjs)):
                    dk_t = jnp.zeros((SB_BLOCK, LANES), F32)
                    dv_t = jnp.zeros((SB_BLOCK, LANES), F32)
                    for h in heads:
                        dq = add_rows(dq, los[t], jnp.dot(dzs[t, h], kms[t, h], preferred_element_type=F32))
                        dk_t = dk_t + lax.dot_general(dzs[t, h], q_live[t, h], _TN, preferred_element_type=F32)
                        dv_t = dv_t + lax.dot_general(ws[t, h].astype(MXU_DTYPE), do_live[t, h], _TN,
                                                      preferred_element_type=F32)
                    d_ref[1, pl.ds(s0s[t], SB_BLOCK), :] += dk_t
                    d_ref[2, pl.ds(s0s[t], SB_BLOCK), :] += dv_t
                last = len(kjs) - 1
                cls = [add_rows(cl[last, h], los[last], rsum(sps[last, h])) for h in heads]
                gls = [add_rows(gl[last, h], los[last], rsum(gs[last, h])) for h in heads]
                return (dq, *cls, *gls)

            zero_col = jnp.zeros((SB_BQ, 1), F32)
            out = _sb_keys_ascending(qi, nrun_ref[pl.program_id(0), qi], tile,
                                     (jnp.zeros((SB_BQ, LANES), F32),) + (zero_col,) * (2 * per_pair), has_free)
            d_ref[0, rows, :] = out[0]

        q_block(0, False)
        lax.fori_loop(1, nq, lambda qi, carry: (q_block(qi, True), carry)[1], 0)

    return pl.pallas_call(
        body, name="sb_bwd",
        grid=(n_pairs,),
        in_specs=[_col_block(t_len, SB_FIRST_BLOCK), _col_block(t_len, SB_FIRST_BLOCK + n_pairs),
                  _col_block(t_len, SB_FIRST_BLOCK + 2 * n_pairs),
                  pl.BlockSpec((t_len, LANES), lambda p: (0, p)),
                  pl.BlockSpec(memory_space=pltpu.SMEM), _col_block(t_len, 0), _HBM],
        out_specs=pl.BlockSpec((3, t_len, LANES), lambda p: (DPROJ_SB_SLOT // 3, 0, p)),
        out_shape=jax.ShapeDtypeStruct(dproj.shape, dproj.dtype),
        input_output_aliases={6: 0},
        compiler_params=_params("arbitrary"),
    )(cols, cols, cols, sp_total, n_run_all, d_o, dproj)


def _conv_taps(xin, rows, t_len):
    taps = []
    for i in range(CONV_WIDTH):
        shift = CONV_WIDTH - 1 - i
        if shift == 0:
            taps.append(xin)
        else:
            taps.append(jnp.where(rows >= shift, pltpu.roll(xin, shift, axis=0), 0.0))
    return taps


def _gdn_prep_body_common(x_ref, w_ref, t_len):
    j = pl.program_id(0)
    xin = x_ref[...]
    rows = lax.broadcasted_iota(jnp.int32, (t_len, LANES), 0)
    taps = _conv_taps(xin, rows, t_len)
    pre = taps[0] * w_ref[0:1, :]
    for i in range(1, CONV_WIDTH):
        pre = pre + taps[i] * w_ref[i:i + 1, :]
    sg = _sigmoid(pre)
    act = pre * sg
    is_qk = j < 2 * GDN_HEADS
    nrm = jnp.where(is_qk, lax.rsqrt(jnp.sum(act * act, axis=-1, keepdims=True) + EPS), 1.0)
    sc = jnp.where(j < GDN_HEADS, float(GDN_HEAD_DIM) ** -0.5, 1.0)
    return j, rows, taps, pre, sg, act, is_qk, nrm, sc


def _gdn_prep_call(cols, conv_w, t_len, after):
    def body(x_blk, w_ref, after_ref, out_ref):
        _, _, _, _, _, act, _, nrm, sc = _gdn_prep_body_common(x_blk.at[0], w_ref, t_len)
        out_ref[...] = act * nrm * sc

    return pl.pallas_call(
        body, name="gdn_prep",
        grid=(3 * GDN_HEADS,),
        in_specs=[_col_block(t_len, GDN_FIRST_BLOCK),
                  pl.BlockSpec((CONV_WIDTH, LANES), lambda j: (0, j)),
                  pl.BlockSpec(memory_space=pl.ANY)],
        out_specs=pl.BlockSpec((t_len, LANES), lambda j: (0, j)),
        out_shape=jax.ShapeDtypeStruct((t_len, 3 * 512), F32),
        compiler_params=_params("arbitrary"),
    )(cols, conv_w, after)


def _gdn_prep_bwd_call(cols, conv_w, d_act3, dproj, t_len):
    def body(x_blk, w_ref, d_ref, dproj_in_ref, dx_ref, dw_ref):
        _, rows, taps, pre, sg, act, is_qk, nrm, sc = _gdn_prep_body_common(x_blk.at[0], w_ref, t_len)
        d_out = d_ref[0]
        dn = d_out * sc
        d_norm = nrm * dn - act * (nrm * nrm * nrm) * jnp.sum(dn * act, axis=-1, keepdims=True)
        d_act = jnp.where(is_qk, d_norm, d_out)
        d_pre = d_act * sg * (1.0 + pre * (1.0 - sg))
        dx = d_pre * w_ref[CONV_WIDTH - 1:CONV_WIDTH, :]
        dw_ref[CONV_WIDTH - 1:CONV_WIDTH, :] = jnp.sum(d_pre * taps[CONV_WIDTH - 1], axis=0, keepdims=True)
        for i in range(CONV_WIDTH - 1):
            shift = CONV_WIDTH - 1 - i
            up = jnp.where(rows < t_len - shift, pltpu.roll(d_pre, t_len - shift, axis=0), 0.0)
            dx = dx + up * w_ref[i:i + 1, :]
            dw_ref[i:i + 1, :] = jnp.sum(d_pre * taps[i], axis=0, keepdims=True)
        dx_ref[0] = dx

    return pl.pallas_call(
        body, name="gdn_prep_bwd",
        grid=(3 * GDN_HEADS,),
        in_specs=[_col_block(t_len, GDN_FIRST_BLOCK),
                  pl.BlockSpec((CONV_WIDTH, LANES), lambda j: (0, j)),
                  pl.BlockSpec((1, t_len, LANES), lambda j: (j // GDN_HEADS, 0, j % GDN_HEADS)), _HBM],
        out_specs=[pl.BlockSpec((1, t_len, LANES), lambda j: (DPROJ_GDN_SLOT + j // GDN_HEADS, 0, j % GDN_HEADS)),
                   pl.BlockSpec((CONV_WIDTH, LANES), lambda j: (0, j))],
        out_shape=[jax.ShapeDtypeStruct(dproj.shape, dproj.dtype),
                   jax.ShapeDtypeStruct((CONV_WIDTH, 3 * 512), F32)],
        input_output_aliases={3: 0},
        compiler_params=_params("arbitrary"),
    )(cols, conv_w, d_act3, dproj)


def _chunk_cumsum_matrix():
    r = lax.broadcasted_iota(jnp.int32, (LANES, LANES), 0)
    c = lax.broadcasted_iota(jnp.int32, (LANES, LANES), 1)
    return ((r <= c) & ((r // CHUNK) == (c // CHUNK))).astype(F32)


def _gdn_gates_call(ps, pst, alog_l, dtb_l, alog_c, dtb_c, t_len):
    def body(ps_ref, pst_ref, al_ref, dl_ref, ac_ref, dc_ref, beta_ref, gcol_ref, grow_ref):
        upper = _chunk_cumsum_matrix()
        lower = upper.T
        psv = ps_ref[...]
        beta_ref[...] = _sigmoid(psv)
        g_l = -jnp.exp(al_ref[...]) * _softplus(psv + dl_ref[...])
        g_r = -jnp.exp(ac_ref[...]) * _softplus(pst_ref[...] + dc_ref[...])
        for w in range(t_len // LANES):
            sl = slice(w * LANES, (w + 1) * LANES)
            gcol_ref[sl, :] = _mx(lower, g_l[sl, :])
            grow_ref[:, sl] = _mx(g_r[:, sl], upper)

    vm = pl.BlockSpec(memory_space=pltpu.VMEM)
    return pl.pallas_call(
        body, name="gdn_gates",
        in_specs=[vm] * 6, out_specs=[vm] * 3,
        out_shape=[jax.ShapeDtypeStruct((t_len, LANES), F32),
                   jax.ShapeDtypeStruct((t_len, LANES), F32),
                   jax.ShapeDtypeStruct((8, t_len), F32)],
        compiler_params=pltpu.CompilerParams(vmem_limit_bytes=VMEM_LIMIT_BYTES),
    )(ps, pst, alog_l, dtb_l, alog_c, dtb_c)


def _gdn_gates_bwd_call(ps, alog_l, dtb_l, d_l, t_len):
    def body(ps_ref, al_ref, dl_ref, d_ref, dps_ref, gal_ref, gdt_ref):
        lane = lax.broadcasted_iota(jnp.int32, (1, LANES), 1)
        psv = ps_ref[...]
        dv = d_ref[...]
        beta = _sigmoid(psv)
        ea = jnp.exp(al_ref[...])
        arg = psv + dl_ref[...]
        g = -ea * _softplus(arg)
        d_a = dv * (-ea) * _sigmoid(arg)
        is_a = (lane >= GDN_HEADS) & (lane < 2 * GDN_HEADS)
        dps_ref[...] = jnp.where(lane < GDN_HEADS, dv * beta * (1.0 - beta), jnp.where(is_a, d_a, 0.0))
        gdt_ref[...] = jnp.where(is_a, jnp.sum(d_a, axis=0, keepdims=True), 0.0)
        gal_ref[...] = jnp.where(is_a, jnp.sum(dv * g, axis=0, keepdims=True), 0.0)

    vm = pl.BlockSpec(memory_space=pltpu.VMEM)
    return pl.pallas_call(
        body, name="gdn_gates_bwd",
        in_specs=[vm] * 4, out_specs=[vm] * 3,
        out_shape=[jax.ShapeDtypeStruct((t_len, LANES), F32),
                   jax.ShapeDtypeStruct((1, LANES), F32),
                   jax.ShapeDtypeStruct((1, LANES), F32)],
        compiler_params=pltpu.CompilerParams(vmem_limit_bytes=VMEM_LIMIT_BYTES),
    )(ps, alog_l, dtb_l, d_l)


def _bm(a, b, right_low=True):
    return _m3_general(a, b, _BNN, right_low)


def _bm_nt(a, b, right_low=True):
    return _m3_general(a, b, _BNT, right_low)


def _bm_tn(a, b, right_low=True):
    return _m3_general(a, b, _BTN, right_low)


def _heads_of(ref, rows):
    return jnp.stack([ref[rows, h * GDN_HEAD_DIM:(h + 1) * GDN_HEAD_DIM] for h in range(GDN_HEADS)])


def _chunk_terms(q_ref, k_ref, v_ref, b_ref, gc_ref, gr_ref, c, incl, strict, n=1, scores=True):
    r0 = c * CHUNK if isinstance(c, int) else pl.multiple_of(c * CHUNK, CHUNK)
    rows = pl.ds(r0, n * CHUNK)
    per_chunk = lambda x: x.reshape(GDN_HEADS * n, CHUNK, x.shape[-1])
    q, k, v = (per_chunk(_heads_of(ref, rows)) for ref in (q_ref, k_ref, v_ref))
    lane_ids = lax.broadcasted_iota(jnp.int32, (1, LANES), 1)
    pick = lambda slab, first: jnp.stack([jnp.sum(jnp.where(lane_ids == first + h, slab, 0.0), axis=-1, keepdims=True)
                                          for h in range(GDN_HEADS)])
    b = per_chunk(pick(b_ref[rows, :], 0))
    gc = per_chunk(pick(gc_ref[rows, :], GDN_HEADS))
    gr = gr_ref[:, c] if n == 1 else gr_ref[:, c:c + n].reshape(GDN_HEADS * n, 1, CHUNK)
    dm = jnp.where(incl, jnp.exp(jnp.where(incl, gc - gr, 0.0)), 0.0)
    kb = k * b
    vb = v * b
    e = jnp.exp(gc)
    a = p = None
    if scores:
        kk_qk = _bm_nt(jnp.concatenate([kb, q], axis=1), k)
        a = jnp.where(strict, kk_qk[:, :CHUNK] * dm, 0.0)
        p = jnp.where(incl, kk_qk[:, CHUNK:] * dm, 0.0)
    gl = gc[:, CHUNK - 1:CHUNK, :]
    eg = jnp.exp(gl - gc)
    return rows, q, k, v, b, gc, dm, kb, vb, e, a, p, gl, eg


def _unit_lower_inverse(a, eye):
    x = -a
    tm = eye + x
    xp = _bm(x, x)
    for _ in range(4):
        both = _bm(jnp.concatenate([xp, tm], axis=1), xp)
        tm = tm + both[:, CHUNK:]
        xp = both[:, :CHUNK]
    return tm + _bm(tm, xp)


def _gdn_specs(t_len, n_chunks, reverse):
    cps = GDN_CHUNKS_PER_STEP
    steps = n_chunks // cps
    at = (lambda g: steps - 1 - g) if reverse else (lambda g: g)
    rows_blk = lambda width, part=0: pl.BlockSpec((cps * CHUNK, width), lambda g: (at(g), part))
    gate_r = pl.BlockSpec((GDN_HEADS, cps, 1, CHUNK), lambda g: (0, at(g), 0, 0))
    per_chunk = lambda r, c: pl.BlockSpec((GDN_HEADS, cps, r, c), lambda g: (0, at(g), 0, 0))
    return cps, steps, rows_blk, gate_r, per_chunk


def _gdn_fwd_call(gact, beta_c, gam_c, gam_r, t_len):
    n_chunks = t_len // CHUNK
    dk = GDN_HEAD_DIM
    width = GDN_HEADS * dk
    cps, steps, rows_blk, gate_r, per_chunk = _gdn_specs(t_len, n_chunks, False)

    def body(q_ref, k_ref, v_ref, b_ref, gc_ref, gr_ref, o_ref, s_ref, t_ref, a_ref, p_ref, uw_ref, vn_ref, state_ref):
        row = lax.broadcasted_iota(jnp.int32, (CHUNK, CHUNK), 0)
        col = lax.broadcasted_iota(jnp.int32, (CHUNK, CHUNK), 1)
        incl, strict = row >= col, row > col
        eye = (row == col).astype(F32)

        @pl.when(pl.program_id(0) == 0)
        def _():
            state_ref[...] = jnp.zeros_like(state_ref)

        _, q, k, v, b, gc, dm, kb, vb, e, a, p, gl, eg = _chunk_terms(
            q_ref, k_ref, v_ref, b_ref, gc_ref, gr_ref, 0, incl, strict, cps)
        tm = _unit_lower_inverse(a, eye)
        uw = _bm(tm, jnp.concatenate([vb, kb * e], axis=2))
        w_qe = jnp.concatenate([uw[:, :, dk:], q * e], axis=1)
        u, kd, decay = uw[:, :, :dk], k * eg, jnp.exp(gl)
        per_chunk_block = lambda x: x.reshape(GDN_HEADS, cps, CHUNK, CHUNK)
        t_ref[...], a_ref[...], p_ref[...] = per_chunk_block(tm), per_chunk_block(a), per_chunk_block(p)
        uw_heads = uw.reshape(GDN_HEADS, cps * CHUNK, 2 * dk)
        for h in range(GDN_HEADS):
            uw_ref[:, h * 2 * dk:(h + 1) * 2 * dk] = uw_heads[h]

        of_chunk = lambda x, c: jnp.stack([x[h * cps + c] for h in range(GDN_HEADS)])
        s = state_ref[...]
        for c in range(cps):
            ws_qs = _bm(of_chunk(w_qe, c), s)
            vn = of_chunk(u, c) - ws_qs[:, :CHUNK]
            o = ws_qs[:, CHUNK:] + _bm(of_chunk(p, c), vn)
            for h in range(GDN_HEADS):
                o_ref[c * CHUNK:(c + 1) * CHUNK, h * dk:(h + 1) * dk] = o[h]
                vn_ref[c * CHUNK:(c + 1) * CHUNK, h * dk:(h + 1) * dk] = vn[h]
            s_ref[:, c] = s
            s = s * of_chunk(decay, c) + _bm_tn(of_chunk(kd, c), vn)
        state_ref[...] = s

    scores = jax.ShapeDtypeStruct((GDN_HEADS, n_chunks, CHUNK, CHUNK), F32)
    return pl.pallas_call(
        body, name="gdn_fwd",
        grid=(steps,),
        in_specs=[rows_blk(width, 0), rows_blk(width, 1), rows_blk(width, 2), rows_blk(LANES), rows_blk(LANES), gate_r],
        out_specs=[rows_blk(width), per_chunk(dk, dk), per_chunk(CHUNK, CHUNK), per_chunk(CHUNK, CHUNK),
                   per_chunk(CHUNK, CHUNK), rows_blk(2 * width), rows_blk(width)],
        out_shape=[jax.ShapeDtypeStruct((t_len, width), F32),
                   jax.ShapeDtypeStruct((GDN_HEADS, n_chunks, dk, dk), F32), scores, scores, scores,
                   jax.ShapeDtypeStruct((t_len, 2 * width), F32), jax.ShapeDtypeStruct((t_len, width), F32)],
        scratch_shapes=[pltpu.VMEM((GDN_HEADS, dk, dk), F32)],
        compiler_params=_params("arbitrary"),
    )(gact, gact, gact, beta_c, gam_c, gam_r)


def _gdn_bwd_call(gact, beta_c, gam_c, gam_r, saved, d_o, t_len, scatter=()):
    n_chunks = t_len // CHUNK
    dk = GDN_HEAD_DIM
    width = GDN_HEADS * dk
    cps, steps, rows_blk, gate_r, per_chunk = _gdn_specs(t_len, n_chunks, True)
    nx = len(scatter)
    n_in = 13

    def body(*refs):
        q_ref, k_ref, v_ref, b_ref, gc_ref, gr_ref = refs[:6]
        saved_refs, do_ref = refs[6:12], refs[12]
        d_ref, dgate_ref = refs[n_in + nx:n_in + 2 + nx]
        dstate_ref = refs[n_in + 2 + 2 * nx]
        copies = lambda: _direct_copies(refs[n_in:n_in + nx], refs[n_in + 2 + nx:n_in + 2 + 2 * nx],
                                        *refs[n_in + 3 + 2 * nx:], (True,) * nx)
        if nx:
            pl.when(pl.program_id(0) == 0)(lambda: _start_all(copies()))
        row = lax.broadcasted_iota(jnp.int32, (CHUNK, CHUNK), 0)
        col = lax.broadcasted_iota(jnp.int32, (CHUNK, CHUNK), 1)
        incl, strict = row >= col, row > col
        ng = GDN_BWD_GROUP
        nb = GDN_HEADS * ng
        upper = jnp.broadcast_to((row <= col).astype(F32), (nb, CHUNK, CHUNK))
        ones = jnp.ones((nb, CHUNK, LANES), F32)
        last_row = lax.broadcasted_iota(jnp.int32, (CHUNK, 1), 0) == CHUNK - 1
        lane_ids = lax.broadcasted_iota(jnp.int32, (1, LANES), 1)
        rsum = lambda m: jnp.sum(m, axis=-1, keepdims=True)
        total = lambda m: jnp.sum(rsum(m), axis=1, keepdims=True)
        of_chunk = lambda x, c: jnp.stack([x[h * ng + c] for h in range(GDN_HEADS)])

        @pl.when(pl.program_id(0) == 0)
        def _():
            dstate_ref[...] = jnp.zeros_like(dstate_ref)

        for c0 in range(cps - ng, -1, -ng):
            group(c0, q_ref, k_ref, v_ref, b_ref, gc_ref, gr_ref, saved_refs, do_ref, d_ref, dgate_ref, dstate_ref,
                  incl, strict, upper, ones, last_row, lane_ids, rsum, total, of_chunk)
        if nx:
            pl.when(pl.program_id(0) == steps - 1)(lambda: _wait_all(copies()))

    def group(c0, q_ref, k_ref, v_ref, b_ref, gc_ref, gr_ref, saved_refs, do_ref, d_ref, dgate_ref, dstate_ref,
              incl, strict, upper, ones, last_row, lane_ids, rsum, total, of_chunk):
        ng = GDN_BWD_GROUP
        nb = GDN_HEADS * ng
        rows = pl.ds(c0 * CHUNK, ng * CHUNK)
        s_ref, t_ref, a_ref, p_ref, uw_ref, vn_ref = saved_refs
        _, q, k, v, b, gc, dm, kb, vb, e, _, _, gl, eg = _chunk_terms(
            q_ref, k_ref, v_ref, b_ref, gc_ref, gr_ref, c0, incl, strict, ng, scores=False)
        s = s_ref[:, c0:c0 + ng].reshape(nb, dk, dk)
        tm = t_ref[:, c0:c0 + ng].reshape(nb, CHUNK, CHUNK)
        a = a_ref[:, c0:c0 + ng].reshape(nb, CHUNK, CHUNK)
        p = p_ref[:, c0:c0 + ng].reshape(nb, CHUNK, CHUNK)
        d_out = _heads_of(do_ref, rows).reshape(nb, CHUNK, dk)
        vn = _heads_of(vn_ref, rows).reshape(nb, CHUNK, dk)
        uw = jnp.stack([uw_ref[rows, h * 2 * dk:(h + 1) * 2 * dk] for h in range(GDN_HEADS)]).reshape(nb, CHUNK, 2 * dk)
        u, w = uw[:, :, :dk], uw[:, :, dk:]
        el = jnp.exp(gl)
        kbe = kb * e
        qe = q * e
        kd = k * eg
        bm, bm_nt, bm_tn = (functools.partial(f, right_low=False) for f in (_bm, _bm_nt, _bm_tn))
        pt_do = _bm_tn(p, d_out)
        qet_do = _bm_tn(qe, d_out)

        ds = dstate_ref[...]
        d_vn_c, ds_c = [None] * ng, [None] * ng
        for c in range(ng - 1, -1, -1):
            ds_c[c] = ds
            d_vn_c[c] = of_chunk(pt_do, c) + bm(of_chunk(kd, c), ds)
            ds = of_chunk(el, c) * ds + of_chunk(qet_do, c) - bm_tn(of_chunk(w, c), d_vn_c[c])
        dstate_ref[...] = ds
        by_chunk = lambda xs: jnp.stack([xs[c][h] for h in range(GDN_HEADS) for c in range(ng)])
        d_vn, ds = by_chunk(d_vn_c), by_chunk(ds_c)

        on_s = bm_nt(jnp.concatenate([d_out, d_vn], axis=1), s)
        d_qe, d_w = on_s[:, :CHUNK], -on_s[:, CHUNK:]
        d_p = jnp.where(incl, _bm_nt(d_out, vn), 0.0)
        d_kd = _bm_nt(vn, ds)
        d_both = _bm_tn(tm, jnp.concatenate([d_vn, d_w], axis=2))
        d_vb, d_kbe = d_both[:, :, :dk], d_both[:, :, dk:]
        d_a = -jnp.where(strict, _bm_nt(d_both, uw), 0.0)
        m = d_a * dm
        n = d_p * dm
        on_k = bm(jnp.concatenate([m, n], axis=1), k)
        d_kb = on_k[:, :CHUNK] + d_kbe * e
        d_q = on_k[:, CHUNK:] + d_qe * e
        d_k = (bm_tn(jnp.concatenate([m, n], axis=1), jnp.concatenate([kb, q], axis=1))
               + d_kd * eg + b * d_kb)
        d_v = b * d_vb
        r = d_a * a + d_p * p
        kd_term = rsum(d_kd * kd)
        d_gl = total(ds * s) * el + jnp.sum(kd_term, axis=1, keepdims=True)
        d_gam = (rsum(r) - _times_exact(r, ones, _BTN)[:, :, 0:1] + rsum(d_qe * qe) + rsum(d_kbe * kbe) - kd_term
                 + jnp.where(last_row, d_gl, 0.0))
        d_beta = rsum(d_kb * k) + rsum(d_vb * v)
        d_g = _exact_times(upper, d_gam * ones, _BNN)[:, :, 0:1]
        per_head = lambda x: x.reshape(GDN_HEADS, ng * CHUNK, x.shape[-1])
        d_q, d_k, d_v, d_beta, d_g = (per_head(x) for x in (d_q, d_k, d_v, d_beta, d_g))
        gates = jnp.zeros((ng * CHUNK, LANES), F32)
        for h in range(GDN_HEADS):
            lanes = slice(h * dk, (h + 1) * dk)
            d_ref[0, rows, lanes] = d_q[h]
            d_ref[1, rows, lanes] = d_k[h]
            d_ref[2, rows, lanes] = d_v[h]
            gates = gates + (jnp.where(lane_ids == h, d_beta[h], 0.0)
                             + jnp.where(lane_ids == GDN_HEADS + h, d_g[h], 0.0))
        dgate_ref[rows, :] = gates

    d_spec = pl.BlockSpec((3, cps * CHUNK, width), lambda g: (0, steps - 1 - g, 0))
    return pl.pallas_call(
        body, name="gdn_bwd",
        grid=(steps,),
        in_specs=[rows_blk(width, 0), rows_blk(width, 1), rows_blk(width, 2), rows_blk(LANES), rows_blk(LANES), gate_r,
                  per_chunk(dk, dk), per_chunk(CHUNK, CHUNK), per_chunk(CHUNK, CHUNK), per_chunk(CHUNK, CHUNK),
                  rows_blk(2 * width), rows_blk(width), rows_blk(width)] + [_HBM] * nx,
        out_specs=[d_spec, rows_blk(LANES)] + [_HBM] * nx,
        out_shape=[jax.ShapeDtypeStruct((3, t_len, width), F32),
                   jax.ShapeDtypeStruct((t_len, LANES), F32)] + _direct_out_shapes(scatter, (True,) * nx),
        scratch_shapes=[pltpu.VMEM((GDN_HEADS, dk, dk), F32)] + (_direct_semaphores(nx) if nx else []),
        compiler_params=_params("arbitrary"),
    )(gact, gact, gact, beta_c, gam_c, gam_r, *saved, d_o, *scatter)


def _group_sums(x, group):
    rows, width = x.shape
    lane = lax.broadcasted_iota(jnp.int32, (1, LANES), 1)
    out = []
    for t in range(width // LANES):
        seg = x[:, t * LANES:(t + 1) * LANES]
        if group == LANES:
            out.append(jnp.broadcast_to(jnp.sum(seg, axis=-1, keepdims=True), (rows, LANES)))
        else:
            low = jnp.sum(jnp.where(lane < group, seg, 0.0), axis=-1, keepdims=True)
            high = jnp.sum(jnp.where(lane < group, 0.0, seg), axis=-1, keepdims=True)
            out.append(jnp.where(lane < group, low, high))
    return jnp.concatenate(out, axis=1)


def _post_call(o_sb, o_gd, proj_gates, x, target, w_out, sbw, gdw, fw, tm=256):
    t_len, d = x.shape
    half = 512
    sb_blocks = half // LANES

    def body(osb_ref, ogd_ref, zsb_ref, zgd_ref, x_ref, tg_ref, wo_ref, sbw_ref, gdw_ref, fw_ref,
             dx2_ref, dosb_ref, dogd_ref, dz_ref, loss_ref, gfw_ref, gsb_ref, ggd_ref, gwo_ref):
        step = pl.program_id(0)

        @pl.when(step == 0)
        def _():
            loss_ref[...] = jnp.zeros_like(loss_ref)
            gfw_ref[...] = jnp.zeros_like(gfw_ref)
            gsb_ref[...] = jnp.zeros_like(gsb_ref)
            ggd_ref[...] = jnp.zeros_like(ggd_ref)
            gwo_ref[...] = jnp.zeros_like(gwo_ref)

        def head_forward(o, z, w, head_dim):
            r = lax.rsqrt(_group_sums(o * o, head_dim) * (1.0 / head_dim) + EPS)
            nrm = o * r * w
            sg = _sigmoid(z)
            return r, nrm, sg, nrm * (z * sg)

        def head_backward(d_m, o, z, w, head_dim, r, nrm, sg):
            d_n = d_m * (z * sg)
            d_z = d_m * nrm * (sg * (1.0 + z * (1.0 - sg)))
            dnw = d_n * w
            d_o = r * dnw - o * (r * r * r) * (_group_sums(dnw * o, head_dim) * (1.0 / head_dim))
            return d_o, d_z, jnp.sum(d_n * o * r, axis=0, keepdims=True)

        osb = jnp.concatenate([osb_ref[j] for j in range(sb_blocks)], axis=1)
        ogd, zsb, zgd = ogd_ref[...], zsb_ref[...], zgd_ref[...]
        sbw_v, gdw_v = sbw_ref[...], gdw_ref[...]
        r_sb, n_sb, sg_sb, m_sb = head_forward(osb, zsb, sbw_v, SB_HEAD_DIM)
        r_gd, n_gd, sg_gd, m_gd = head_forward(ogd, zgd, gdw_v, GDN_HEAD_DIM)
        mixed = jnp.concatenate([m_sb, m_gd], axis=1).astype(MXU_DTYPE)
        wo = wo_ref[...]
        x2 = x_ref[...] + jnp.dot(mixed, wo, preferred_element_type=F32)
        r2 = lax.rsqrt(jnp.mean(x2 * x2, axis=-1, keepdims=True) + EPS)
        fw_v = fw_ref[...]
        err = x2 * r2 * fw_v - tg_ref[...]
        loss_ref[...] += 0.5 * jnp.sum(jnp.sum(err * err, axis=-1, keepdims=True) * (1.0 / d))
        dy = err * (1.0 / d)
        gg = dy * fw_v
        dx2 = r2 * gg - x2 * ((r2 * r2 * r2) * jnp.mean(gg * x2, axis=-1, keepdims=True))
        gfw_ref[...] += jnp.sum(dy * x2 * r2, axis=0, keepdims=True)
        dx2_ref[...] = dx2
        dx2b = dx2.astype(MXU_DTYPE)
        d_mixed = lax.dot_general(dx2b, wo, _NT, preferred_element_type=F32)
        gwo_ref[...] += lax.dot_general(mixed, dx2b, _TN, preferred_element_type=F32)
        d_osb, d_zsb, gsb = head_backward(d_mixed[:, :half], osb, zsb, sbw_v, SB_HEAD_DIM, r_sb, n_sb, sg_sb)
        d_ogd, d_zgd, ggd = head_backward(d_mixed[:, half:], ogd, zgd, gdw_v, GDN_HEAD_DIM, r_gd, n_gd, sg_gd)
        for j in range(sb_blocks):
            dosb_ref[j] = d_osb[:, j * LANES:(j + 1) * LANES]
        dogd_ref[...] = d_ogd
        dz_ref[0] = d_zsb
        dz_ref[1] = d_zgd
        gsb_ref[...] += gsb
        ggd_ref[...] += ggd

    row_blk = lambda w: pl.BlockSpec((tm, w), lambda i: (i, 0))
    blocks_blk = pl.BlockSpec((sb_blocks, tm, LANES), lambda i: (0, i, 0))
    fixed = lambda r, w: pl.BlockSpec((r, w), lambda i: (0, 0))
    return pl.pallas_call(
        body, name="post",
        grid=(t_len // tm,),
        in_specs=[blocks_blk, row_blk(half),
                  pl.BlockSpec((tm, half), lambda i: (i, 0)),
                  pl.BlockSpec((tm, half), lambda i: (i, 1)),
                  row_blk(d), row_blk(d), fixed(d, d), fixed(1, half), fixed(1, half), fixed(1, d)],
        out_specs=[row_blk(d), blocks_blk, row_blk(half),
                   pl.BlockSpec((2, tm, half), lambda i: (DPROJ_GATE_SLOT // 2, i, 0)),
                   fixed(1, LANES), fixed(1, d), fixed(1, half), fixed(1, half), fixed(d, d)],
        out_shape=[jax.ShapeDtypeStruct((t_len, d), F32), jax.ShapeDtypeStruct((sb_blocks, t_len, LANES), F32),
                   jax.ShapeDtypeStruct((t_len, half), F32),
                   jax.ShapeDtypeStruct((len(DPROJ_PIECE_OF_SLOT), t_len, half), F32),
                     jax.ShapeDtypeStruct((1, LANES), F32), jax.ShapeDtypeStruct((1, d), F32),
                     jax.ShapeDtypeStruct((1, half), F32), jax.ShapeDtypeStruct((1, half), F32),
                     jax.ShapeDtypeStruct((d, d), F32)],
        compiler_params=_params("arbitrary"),
    )(o_sb, o_gd, proj_gates, proj_gates, x, target, w_out, sbw, gdw, fw)


def _piece_of_slot(s):
    return jnp.where(s < DPROJ_GDN_SLOT, s, jnp.where(s < DPROJ_GATE_SLOT, s + 1,
                                                     jnp.where(s == DPROJ_GATE_SLOT, 3, 7)))


def _gw_in_call(h_t, dproj8):
    d, t_len = h_t.shape
    n_piece, _, pw = dproj8.shape

    def body(ht_ref, dp_ref, gw_ref):
        gw_ref[...] = jnp.dot(ht_ref[...], dp_ref[0].astype(MXU_DTYPE), preferred_element_type=F32)

    return pl.pallas_call(
        body, name="gw_in",
        grid=(n_piece,),
        in_specs=[pl.BlockSpec((d, t_len), lambda s: (0, 0)),
                  pl.BlockSpec((1, t_len, pw), lambda s: (s, 0, 0))],
        out_specs=pl.BlockSpec((d, pw), lambda s: (0, _piece_of_slot(s))),
        out_shape=jax.ShapeDtypeStruct((d, n_piece * pw), F32),
        compiler_params=_params("arbitrary"),
    )(h_t, dproj8)


def _slot_of_piece(p):
    return jnp.where(p < DPROJ_GDN_SLOT, p, jnp.where(p == 3, DPROJ_GATE_SLOT, jnp.where(p < 7, p - 1, 7)))


def _gw_in_shards_call(h_t, dproj8, dsmall, out_dtype):
    d, t_len = h_t.shape
    n_piece, _, pw = dproj8.shape
    ns = dsmall.shape[1]
    n_pairs = N_DEV // 2

    def body(ht_ref, dp_ref, ds_ref, chip_ref, prev_ref, gates_ref, send_ref, recv_ref, send_sems, recv_sems):
        p = pl.program_id(0)
        x_pos, y_pos, c = lax.axis_index("x"), lax.axis_index("y"), lax.axis_index("c")
        to_sibling = lambda pair: pltpu.make_async_remote_copy(
            src_ref=send_ref.at[pair], dst_ref=recv_ref.at[pair], send_sem=send_sems.at[pair],
            recv_sem=recv_sems.at[pair], device_id=(x_pos, y_pos, 1 - c), device_id_type=_MESH)

        @pl.when(p == 0)
        def _():
            gates_ref[...] = jnp.dot(ht_ref[...], ds_ref[...].astype(MXU_DTYPE), preferred_element_type=F32)

        def emit(s, tail):
            x = jnp.concatenate([prev_ref[...], tail], axis=1)
            y = x if s == 0 else pltpu.roll(x, SHARD_PAD - s, axis=1)
            shard = y[:, :SHARD_COLS].astype(out_dtype)

            @pl.when(c == s % 2)
            def _():
                chip_ref[s // 2] = shard

            @pl.when(c != s % 2)
            def _():
                send_ref[s // 2] = shard
                to_sibling(s // 2).start()

        @pl.when(p < n_piece)
        def _():
            cur = jnp.dot(ht_ref[...], dp_ref[0].astype(MXU_DTYPE), preferred_element_type=F32)
            for s in range(n_piece - 1):
                pl.when(p == s + 1)(functools.partial(emit, s, cur[:, :SHARD_PAD - pw]))
            prev_ref[...] = cur

        @pl.when(p == n_piece)
        def _():
            emit(n_piece - 1, gates_ref[...])
            for pair in range(n_pairs):
                to_sibling(pair).wait_send()
            for pair in range(n_pairs):
                to_sibling(pair).wait_recv()
                chip_ref[pair] = (chip_ref[pair].astype(F32) + recv_ref[pair].astype(F32)).astype(out_dtype)

    shards_of_side = lambda: pltpu.VMEM((n_pairs, d, SHARD_COLS), out_dtype)
    return pl.pallas_call(
        body, name="gw_in",
        grid=(n_piece + 1,),
        in_specs=[pl.BlockSpec((d, t_len), lambda p: (0, 0)),
                  pl.BlockSpec((1, t_len, pw), lambda p: (_slot_of_piece(jnp.minimum(p, n_piece - 1)), 0, 0)),
                  pl.BlockSpec((t_len, ns), lambda p: (0, 0))],
        out_specs=pl.BlockSpec((n_pairs, d, SHARD_COLS), lambda p: (0, 0, 0)),
        out_shape=jax.ShapeDtypeStruct((n_pairs, d, SHARD_COLS), out_dtype),
        scratch_shapes=[pltpu.VMEM((d, pw), F32), pltpu.VMEM((d, ns), F32), shards_of_side(), shards_of_side(),
                        pltpu.SemaphoreType.DMA((n_pairs,)), pltpu.SemaphoreType.DMA((n_pairs,))],
        compiler_params=_params("arbitrary"),
    )(h_t, dproj8, dsmall)


def _gw_small_call(h_t, dsmall, tm=512):
    d, t_len = h_t.shape
    ns = dsmall.shape[1]

    def body(ht_ref, dp_ref, gw_ref):
        @pl.when(pl.program_id(0) == 0)
        def _():
            gw_ref[...] = jnp.zeros_like(gw_ref)

        gw_ref[...] += jnp.dot(ht_ref[...], dp_ref[...].astype(MXU_DTYPE), preferred_element_type=F32)

    return pl.pallas_call(
        body, name="gw_small",
        grid=(t_len // tm,),
        in_specs=[pl.BlockSpec((d, tm), lambda t: (0, t)),
                  pl.BlockSpec((tm, ns), lambda t: (t, 0))],
        out_specs=pl.BlockSpec((d, ns), lambda t: (0, 0)),
        out_shape=jax.ShapeDtypeStruct((d, ns), F32),
        compiler_params=_params("arbitrary"),
    )(h_t, dsmall)


def _dx_call(dproj8, dsmall, w_main, w_small, x, r, dx2, norm_w, chip_scatter=(), peer_scatter=(), tm=256):
    t_len, d = x.shape
    n_piece, _, pw = dproj8.shape
    ns = dsmall.shape[1]
    nx = len(chip_scatter)
    n_peer = len(peer_scatter)
    steps = t_len // tm
    n_in = 8 + nx + n_peer
    n_out = 2 + nx + n_peer + nx
    reduce_step = steps // 2

    def body(*refs):
        dp_ref, ds_ref, wm_ref, ws_ref, x_ref, r_ref, dx2_ref, nw_ref = refs[:8]
        gx_ref, gnw_ref = refs[n_in:n_in + 2]
        scratch = refs[n_in + n_out:]
        chip = lambda: _chip_reduce_copies(refs[8], refs[n_in + 2], refs[n_in + n_out - 1], scratch[-2], scratch[-1],
                                           *scratch[:3])
        if nx:
            @pl.when(pl.program_id(0) == 0)
            def _():
                cp = chip()
                for name in ("to_reduce", "direct", "keep", "load_own"):
                    cp[name].start()

        @pl.when(pl.program_id(0) == 0)
        def _():
            gnw_ref[...] = jnp.zeros_like(gnw_ref)

        dh = lax.dot_general(ds_ref[...].astype(MXU_DTYPE), ws_ref[...], _NT, preferred_element_type=F32)
        for s, p in enumerate(DPROJ_PIECE_OF_SLOT):
            dh = dh + lax.dot_general(dp_ref[s].astype(MXU_DTYPE), wm_ref[:, p * pw:(p + 1) * pw], _NT,
                                      preferred_element_type=F32)
        xv, rv = x_ref[...], r_ref[...]
        dn = dh * nw_ref[...]
        gx_ref[...] = dx2_ref[...] + rv * dn - xv * ((rv * rv * rv) * jnp.mean(dn * xv, axis=-1, keepdims=True))
        gnw_ref[...] += jnp.sum(dh * xv * rv, axis=0, keepdims=True)

        if nx:
            @pl.when(pl.program_id(0) == reduce_step)
            def _():
                cp = chip()
                own_buf, got_buf = scratch[-2], scratch[-1]
                cp["to_reduce"].wait_recv()
                cp["load_got"].start()
                cp["load_own"].wait()
                cp["load_got"].wait()
                own_buf[...] = (own_buf[...].astype(F32) + got_buf[...].astype(F32)).astype(own_buf.dtype)
                cp["reduced"].start()

        @pl.when(pl.program_id(0) == steps - 1)
        def _():
            if n_peer:
                small_ref, parts_ref = refs[8 + nx:n_in]
                small_buf = scratch[6]
                small_buf[...] = small_ref[...]
                small_buf[0:1, :] = gnw_ref[...]
                peer_copies = _direct_copies([small_buf, parts_ref], refs[n_in + 2 + nx:n_in + 2 + nx + n_peer],
                                             *scratch[3:6], [False, True])
                _start_all(peer_copies)
            if nx:
                cp = chip()
                for name in ("to_reduce", "direct", "reduced"):
                    cp[name].wait_send()
                cp["direct"].wait_recv()
                cp["reduced"].wait_recv()
                cp["keep"].wait()
            if n_peer:
                _wait_all(peer_copies)

    assert n_peer in (0, 2) and (nx == 1 or not n_peer)
    peer_in_specs = [pl.BlockSpec(peer_scatter[0].shape, lambda i: (0, 0)), _HBM] if n_peer else []
    peer_scratch = _direct_semaphores(n_peer) + [pltpu.VMEM(peer_scatter[0].shape, F32)] if n_peer else []
    return pl.pallas_call(
        body, name="dx",
        grid=(steps,),
        in_specs=[pl.BlockSpec((n_piece, tm, pw), lambda i: (0, i, 0)),
                  pl.BlockSpec((tm, ns), lambda i: (i, 0)),
                  pl.BlockSpec((d, n_piece * pw), lambda i: (0, 0)),
                  pl.BlockSpec((d, ns), lambda i: (0, 0)),
                  pl.BlockSpec((tm, d), lambda i: (i, 0)),
                  pl.BlockSpec((tm, 1), lambda i: (i, 0)),
                  pl.BlockSpec((tm, d), lambda i: (i, 0)),
                  pl.BlockSpec((1, d), lambda i: (0, 0))] + [_HBM] * nx + peer_in_specs,
        out_specs=[pl.BlockSpec((tm, d), lambda i: (i, 0)),
                   pl.BlockSpec((1, d), lambda i: (0, 0))] + [_HBM] * (nx + n_peer + nx),
        out_shape=[jax.ShapeDtypeStruct((t_len, d), F32), jax.ShapeDtypeStruct((1, d), F32)]
                  + [jax.ShapeDtypeStruct((N_CHIPS - 1,) + a.shape[1:], a.dtype) for a in chip_scatter]
                  + (_direct_out_shapes(peer_scatter, [False, True]) if n_peer else [])
                  + [jax.ShapeDtypeStruct(a.shape[1:], a.dtype) for a in chip_scatter],
        scratch_shapes=([pltpu.SemaphoreType.DMA((3,))] * 3 if nx else []) + peer_scratch
                       + [pltpu.VMEM(a.shape[1:], a.dtype) for a in chip_scatter for _ in range(2)],
        compiler_params=_params("arbitrary"),
    )(dproj8, dsmall, w_main, w_small, x, r, dx2, norm_w, *chip_scatter, *peer_scatter)


def _direct_out_shapes(srcs, per_peer):
    return [jax.ShapeDtypeStruct(s.shape if pp else (N_DEV,) + s.shape, s.dtype) for s, pp in zip(srcs, per_peer)]


def _direct_semaphores(n):
    return [pltpu.SemaphoreType.DMA((n * (N_DEV - 1),)), pltpu.SemaphoreType.DMA((n * (N_DEV - 1),)),
            pltpu.SemaphoreType.DMA((n,))]


def _direct_copies(src_refs, out_refs, send_sems, recv_sems, local_sems, per_peer):
    x, y, c = lax.axis_index("x"), lax.axis_index("y"), lax.axis_index("c")
    me = 4 * x + 2 * y + c
    local, remote = [], []
    for a in range(len(src_refs)):
        mine = src_refs[a].at[me] if per_peer[a] else src_refs[a]
        local.append(pltpu.make_async_copy(mine, out_refs[a].at[me], local_sems.at[a]))
    for k in range(1, N_DEV):
        kx, ky, kc = (k >> 2) & 1, (k >> 1) & 1, k & 1
        px = 1 - x if kx else x
        py = 1 - y if ky else y
        pc = 1 - c if kc else c
        peer = 4 * px + 2 * py + pc
        for a in range(len(src_refs)):
            sem = a * (N_DEV - 1) + (k - 1)
            remote.append(pltpu.make_async_remote_copy(
                src_ref=src_refs[a].at[peer] if per_peer[a] else src_refs[a], dst_ref=out_refs[a].at[me],
                send_sem=send_sems.at[sem], recv_sem=recv_sems.at[sem],
                device_id=(px, py, pc), device_id_type=pl.DeviceIdType.MESH))
    return local, remote


def _start_all(copies):
    local, remote = copies
    for cp in local + remote:
        cp.start()


def _wait_all(copies):
    local, remote = copies
    for cp in remote:
        cp.wait_send()
    for cp in remote:
        cp.wait_recv()
    for cp in local:
        cp.wait()


N_CHIPS = 4
_HBM = pl.BlockSpec(memory_space=pl.ANY)
_MESH = pl.DeviceIdType.MESH


def _gather_call(name, srcs):
    n = len(srcs)
    per = N_DEV - 1

    def body(*refs):
        src_refs, out_refs = refs[:n], refs[n:2 * n]
        send_sems, recv_sems, local_sems = refs[2 * n:]
        x, y, c = lax.axis_index("x"), lax.axis_index("y"), lax.axis_index("c")
        me, sibling = (x, y, c), (x, y, 1 - c)
        x_nbr, y_nbr, diagonal = (1 - x, y), (x, 1 - y), (1 - x, 1 - y)
        held = ((1 - x) * c + x * (1 - c), y * c + (1 - y) * (1 - c))
        onward = (x * c + (1 - x) * (1 - c), (1 - y) * c + y * (1 - c))
        slot = lambda px, py, pc: 4 * px + 2 * py + pc

        def copy(a, k, block, to, from_src=False):
            rows = out_refs[a].at[slot(*block)]
            return pltpu.make_async_remote_copy(
                src_ref=src_refs[a] if from_src else rows, dst_ref=rows,
                send_sem=send_sems.at[a * per + k], recv_sem=recv_sems.at[a * per + k],
                device_id=to, device_id_type=_MESH)

        local = [pltpu.make_async_copy(src_refs[a], out_refs[a].at[slot(*me)], local_sems.at[a]) for a in range(n)]
        started = []

        def start(cp):
            cp.start()
            started.append(cp)

        for cp in local:
            cp.start()
        for a in range(n):
            start(copy(a, 0, me, sibling, True))
            start(copy(a, 1, me, (*x_nbr, c), True))
            start(copy(a, 2, me, (*y_nbr, c), True))
        for a in range(n):
            copy(a, 1, (*x_nbr, c), me).wait_recv()
            copy(a, 2, (*y_nbr, c), me).wait_recv()
            start(copy(a, 3, (*held, c), (*onward, c)))
            start(copy(a, 4, (*x_nbr, c), sibling))
            start(copy(a, 5, (*y_nbr, c), sibling))
        for a in range(n):
            copy(a, 3, (*diagonal, c), me).wait_recv()
            start(copy(a, 6, (*diagonal, c), sibling))
        for a in range(n):
            copy(a, 0, sibling, me).wait_recv()
            for k, chip in ((4, x_nbr), (5, y_nbr), (6, diagonal)):
                copy(a, k, (*chip, 1 - c), me).wait_recv()
        for cp in started:
            cp.wait_send()
        for cp in local:
            cp.wait()

    return pl.pallas_call(
        body, name=name,
        in_specs=[_HBM] * n, out_specs=[_HBM] * n,
        out_shape=[jax.ShapeDtypeStruct((N_DEV,) + s.shape, s.dtype) for s in srcs],
        scratch_shapes=[pltpu.SemaphoreType.DMA((n * per,)), pltpu.SemaphoreType.DMA((n * per,)),
                        pltpu.SemaphoreType.DMA((n,))],
    )(*srcs)


def _chip_reduce_copies(src_ref, out_ref, stage_ref, own_buf, got_buf, send_sems, recv_sems, local_sems):
    x, y, c = lax.axis_index("x"), lax.axis_index("y"), lax.axis_index("c")
    via = (c * (1 - x) + (1 - c) * x, c * y + (1 - c) * (1 - y))
    other = (c * x + (1 - c) * (1 - x), c * (1 - y) + (1 - c) * y)
    block = lambda chip: src_ref.at[2 * chip[0] + chip[1]]
    remote = lambda k, src, dst, chip: pltpu.make_async_remote_copy(
        src_ref=src, dst_ref=dst, send_sem=send_sems.at[k], recv_sem=recv_sems.at[k],
        device_id=(chip[0], chip[1], c), device_id_type=_MESH)
    local = lambda k, src, dst: pltpu.make_async_copy(src, dst, local_sems.at[k])
    return dict(keep=local(0, block((x, y)), out_ref.at[0]),
                to_reduce=remote(0, block((1 - x, 1 - y)), stage_ref, via),
                direct=remote(1, block(via), out_ref.at[1], via),
                load_own=local(1, block(other), own_buf),
                load_got=local(2, stage_ref, got_buf),
                reduced=remote(2, own_buf, out_ref.at[2], other))


def _adam_call(name, parts, w, m, v, tr):
    rows, cols = w.shape
    n_slots = parts.shape[0]

    def body(p_ref, w_ref, m_ref, v_ref, g_ref, d_ref, nm_ref, nv_ref):
        g = p_ref[0].astype(F32)
        for s in range(1, n_slots):
            g = g + p_ref[s].astype(F32)
        m_new = ADAM_B1 * m_ref[...] + (1.0 - ADAM_B1) * g
        v_new = ADAM_B2 * v_ref[...] + (1.0 - ADAM_B2) * (g * g)
        m_hat = m_new / (1.0 - ADAM_B1 ** ADAM_STEP)
        v_hat = v_new / (1.0 - ADAM_B2 ** ADAM_STEP)
        g_ref[...] = g
        d_ref[...] = -ADAM_LR * (m_hat / (jnp.sqrt(v_hat) + ADAM_EPS) + ADAM_WD * w_ref[...])
        nm_ref[...] = m_new
        nv_ref[...] = v_new

    blk = pl.BlockSpec((tr, cols), lambda i: (i, 0))
    return pl.pallas_call(
        body, name=name,
        grid=(rows // tr,),
        in_specs=[pl.BlockSpec((n_slots, tr, cols), lambda i: (0, i, 0)), blk, blk, blk],
        out_specs=[blk] * 4,
        out_shape=[jax.ShapeDtypeStruct((rows, cols), F32)] * 4,
        compiler_params=_params("arbitrary"),
    )(parts, w, m, v)


def _columns_to_rows_call(name, w_t, rows, dtype):
    row_tiles = rows // LANES
    cols = w_t.shape[0] // row_tiles
    whole = cols // LANES * LANES

    def body(w_ref, out_ref):
        diagonal = (lax.broadcasted_iota(jnp.int32, (LANES, LANES), 0)
                    == lax.broadcasted_iota(jnp.int32, (LANES, LANES), 1))
        for a in range(row_tiles):
            out_ref[a * LANES:(a + 1) * LANES, :whole] = (
                w_ref[pl.ds(a, whole, stride=row_tiles), :].T.astype(dtype))
            for c in range(whole, cols):
                column = w_ref[pl.ds(c * row_tiles + a, 1), :]
                upright = jnp.sum(jnp.where(diagonal, column, 0.0), axis=1, keepdims=True)
                out_ref[a * LANES:(a + 1) * LANES, c:c + 1] = upright.astype(dtype)

    vm = pl.BlockSpec(memory_space=pltpu.VMEM)
    return pl.pallas_call(
        body, name=name,
        in_specs=[vm], out_specs=vm,
        out_shape=jax.ShapeDtypeStruct((rows, cols), dtype),
        compiler_params=pltpu.CompilerParams(vmem_limit_bytes=VMEM_LIMIT_BYTES),
    )(w_t)


def _adam_columns_call(name, parts, w_t, m_t, v_t):
    n_slots, rows, cols = parts.shape
    row_tiles = rows // LANES
    cols_pad = -(-cols // LANES) * LANES

    def body(p_ref, w_ref, m_ref, v_ref, *out_refs):
        for a in range(row_tiles):
            g = p_ref[0, a * LANES:(a + 1) * LANES, :].astype(F32)
            for s in range(1, n_slots):
                g = g + p_ref[s, a * LANES:(a + 1) * LANES, :].astype(F32)
            g = jnp.concatenate([g, jnp.zeros((LANES, cols_pad - cols), F32)], axis=1).T[:cols]
            column_rows = pl.ds(a, cols, stride=row_tiles)
            results = (g,) + _adamw(g, w_ref[column_rows, :], m_ref[column_rows, :], v_ref[column_rows, :])
            for out_ref, val in zip(out_refs, results):
                out_ref[column_rows, :] = val

    vm = pl.BlockSpec(memory_space=pltpu.VMEM)
    return pl.pallas_call(
        body, name=name,
        in_specs=[vm] * 4, out_specs=[vm] * 4,
        out_shape=[jax.ShapeDtypeStruct(w_t.shape, F32)] * 4,
        compiler_params=pltpu.CompilerParams(vmem_limit_bytes=VMEM_LIMIT_BYTES),
    )(parts, w_t, m_t, v_t)


N_PIECES = 8
PIECE = 512
SHARD_COLS = 513
SHARD_PAD = 640
RELAYOUT_ROWS = 256


def _from_shards_call(shards):
    _, d, _ = shards.shape
    tr = RELAYOUT_ROWS

    def body(p_ref, m_ref, s_ref):
        lane = lax.broadcasted_iota(jnp.int32, (tr, SHARD_PAD), 1)
        pad = jnp.zeros((tr, SHARD_PAD - SHARD_COLS), p_ref.dtype)
        sh = [jnp.concatenate([p_ref[s], pad], axis=1) for s in range(N_DEV)]
        for p in range(N_PIECES):
            y = sh[p] if p == 0 else pltpu.roll(sh[p], p, axis=1)
            if p > 0:
                y = jnp.where(lane < p, pltpu.roll(sh[p - 1], SHARD_PAD - (SHARD_COLS - p), axis=1), y)
            m_ref[:, p * PIECE:(p + 1) * PIECE] = y[:, :PIECE].astype(m_ref.dtype)
        first_gate = N_PIECES * PIECE - (N_DEV - 1) * SHARD_COLS
        s_ref[...] = pltpu.roll(sh[N_DEV - 1], SHARD_PAD - first_gate, axis=1)[:, :LANES].astype(s_ref.dtype)

    return pl.pallas_call(
        body, name="w_in_from_shards",
        grid=(d // tr,),
        in_specs=[pl.BlockSpec((N_DEV, tr, SHARD_COLS), lambda i: (0, i, 0))],
        out_specs=[pl.BlockSpec((tr, N_PIECES * PIECE), lambda i: (i, 0)), pl.BlockSpec((tr, LANES), lambda i: (i, 0))],
        out_shape=[jax.ShapeDtypeStruct((d, N_PIECES * PIECE), shards.dtype),
                   jax.ShapeDtypeStruct((d, LANES), shards.dtype)],
        compiler_params=_params("arbitrary"),
    )(shards)


def _adamw(g, w, m, v):
    m_new = ADAM_B1 * m + (1.0 - ADAM_B1) * g
    v_new = ADAM_B2 * v + (1.0 - ADAM_B2) * (g * g)
    m_hat = m_new / (1.0 - ADAM_B1 ** ADAM_STEP)
    v_hat = v_new / (1.0 - ADAM_B2 ** ADAM_STEP)
    return -ADAM_LR * (m_hat / (jnp.sqrt(v_hat) + ADAM_EPS) + ADAM_WD * w), m_new, v_new


def _adam_small_call(parts, ws, ms, vs):
    n = len(ws)
    n_slots = parts.shape[0]

    def body(*refs):
        p_ref = refs[0]
        w_refs, m_refs, v_refs = refs[1:1 + n], refs[1 + n:1 + 2 * n], refs[1 + 2 * n:1 + 3 * n]
        loss_ref = refs[1 + 3 * n]
        outs = refs[2 + 3 * n:]
        g_all = p_ref[0]
        for s in range(1, n_slots):
            g_all = g_all + p_ref[s]
        loss_ref[...] = g_all[n:n + 1, 0:1]
        for r in range(n):
            size = w_refs[r].shape[1]
            g = g_all[r:r + 1, :size]
            delta, m_new, v_new = _adamw(g, w_refs[r][...], m_refs[r][...], v_refs[r][...])
            for kind, val in enumerate((g, delta, m_new, v_new)):
                outs[kind * n + r][...] = val

    vm = pl.BlockSpec(memory_space=pltpu.VMEM)
    shapes = [jax.ShapeDtypeStruct(w.shape, F32) for w in ws]
    return pl.pallas_call(
        body, name="adam_small",
        in_specs=[vm] * (1 + 3 * n), out_specs=[vm] * (1 + 4 * n),
        out_shape=[jax.ShapeDtypeStruct((1, 1), F32)] + shapes * 4,
    )(parts, *ws, *ms, *vs)


_SMALL_ROWS = ("norm1_w", "final_norm_w", "sb_norm_w", "gdn_norm_w", "gdn_A_log", "gdn_dt_bias", "loss")


def _pack_small(vals, width):
    rows = [jnp.pad(a.reshape(1, -1).astype(F32), ((0, 0), (0, width - a.size))) for a in vals]
    rows += [jnp.zeros((1, width), F32)] * (8 - len(rows))
    return jnp.concatenate(rows, axis=0)


def _device_step(x2d, tgt, w_main, w_small, w_out_full, conv_full, norm1_w, sb_norm_w, gdn_A_log, gdn_dt_bias,
                 gdn_norm_w, final_norm_w, distributed=False):
    t_len, d = x2d.shape
    n_chunks = t_len // CHUNK
    w_main, w_small, w_out_full = (a.astype(MXU_DTYPE) for a in (w_main, w_small, w_out_full))
    w_small_t = w_small[:, :2 * GDN_HEADS].T

    pad_lanes = lambda a, lo: jnp.pad(a.reshape(1, -1), ((0, 0), (lo, LANES - lo - a.size)))
    alog_l, dtb_l = pad_lanes(gdn_A_log, GDN_HEADS), pad_lanes(gdn_dt_bias, GDN_HEADS)
    alog_c, dtb_c = alog_l[:, :8].T, dtb_l[:, :8].T
    sbw = jnp.tile(sb_norm_w, (1, 512 // SB_HEAD_DIM))
    gdw = jnp.tile(gdn_norm_w, (1, 512 // GDN_HEAD_DIM))
    fw = final_norm_w.reshape(1, d)

    if distributed:
        proj_cols, proj_gates, ps, pst, h_t, r1, w_out_g, conv_g = _inproj_call(
            x2d, norm1_w, w_main, w_small, w_small_t, gather=(w_out_full, conv_full))
        w_out_full = w_out_g.reshape(d, d)
        conv_full = conv_g.transpose(1, 0, 2).reshape(CONV_WIDTH, N_DEV * conv_g.shape[2])
    else:
        proj_cols, proj_gates, ps, pst, h_t, r1 = _inproj_call(x2d, norm1_w, w_main, w_small, w_small_t)
    o_sb, sp_total, sb_blocks_run = _sb_fwd_call(proj_cols, t_len)
    gact = _gdn_prep_call(proj_cols, conv_full, t_len, after=sp_total)
    beta_l, gcol_l, grow = _gdn_gates_call(ps, pst, alog_l, dtb_l, alog_c, dtb_c, t_len)
    gam_r = grow[GDN_HEADS:2 * GDN_HEADS].reshape(GDN_HEADS, n_chunks, 1, CHUNK)
    o_gd, *gdn_saved = _gdn_fwd_call(gact, beta_l, gcol_l, gam_r, t_len)

    (dx2, d_osb, d_ogd, dproj8, loss_p, g_fw, g_sbw, g_gdw, g_wout) = _post_call(
        o_sb, o_gd, proj_gates, x2d, tgt, w_out_full, sbw, gdw, fw)

    dproj8 = _sb_bwd_call(proj_cols, sp_total, sb_blocks_run, d_osb, dproj8, t_len)
    if distributed:
        d_gact3, d_gates, g_wout = _gdn_bwd_call(gact, beta_l, gcol_l, gam_r, gdn_saved, d_ogd, t_len,
                                                 scatter=(g_wout.reshape(N_DEV, d // N_DEV, d),))
    else:
        d_gact3, d_gates = _gdn_bwd_call(gact, beta_l, gcol_l, gam_r, gdn_saved, d_ogd, t_len)
    dproj8, g_conv = _gdn_prep_bwd_call(proj_cols, conv_full, d_gact3, dproj8, t_len)
    dsmall, g_alog, g_dtb = _gdn_gates_bwd_call(ps, alog_l, dtb_l, d_gates, t_len)

    if distributed:
        chip_partials = _gw_in_shards_call(h_t, dproj8, dsmall, WIRE_DTYPE)
        fold = lambda a, group: a.reshape(-1, group).sum(axis=0)
        small_g = _pack_small([jnp.zeros((d,), F32), g_fw, fold(g_sbw, SB_HEAD_DIM), fold(g_gdw, GDN_HEAD_DIM),
                               g_alog[0, GDN_HEADS:2 * GDN_HEADS], g_dtb[0, GDN_HEADS:2 * GDN_HEADS],
                               loss_p[0, :1]], d)
        conv_cols = g_conv.shape[1] // N_DEV
        g_conv_parts = g_conv.reshape(CONV_WIDTH, N_DEV, conv_cols).transpose(1, 0, 2)
        grad_x, _, g_w_in, p_small, p_conv, _ = _dx_call(dproj8, dsmall, w_main, w_small, x2d, r1, dx2, norm1_w,
                                                         chip_scatter=(chip_partials,),
                                                         peer_scatter=(small_g, g_conv_parts))
        return grad_x, g_w_in, g_wout, p_small, p_conv
    else:
        grad_x, g_n1 = _dx_call(dproj8, dsmall, w_main, w_small, x2d, r1, dx2, norm1_w)
        g_w_in = (_gw_in_call(h_t, dproj8), _gw_small_call(h_t, dsmall))
    return (loss_p, grad_x, g_n1, g_w_in, g_sbw, g_conv, g_alog, g_dtb, g_gdw, g_wout, g_fw)


def kernel(x, norm1_w, w_in, sb_norm_w, gdn_conv_w, gdn_A_log, gdn_dt_bias, gdn_norm_w, w_out, final_norm_w, loss_target, m_norm1_w, m_w_in, m_sb_norm_w, m_gdn_conv_w, m_gdn_A_log, m_gdn_dt_bias, m_gdn_norm_w, m_w_out, m_final_norm_w, v_norm1_w, v_w_in, v_sb_norm_w, v_gdn_conv_w, v_gdn_A_log, v_gdn_dt_bias, v_gdn_norm_w, v_w_out, v_final_norm_w):
    d = x.shape[2]
    shard_cols = w_in.shape[2]

    columns = lambda a: a.transpose(2, 0, 1).reshape(shard_cols * d // LANES, LANES)
    from_columns = lambda a: a.reshape(shard_cols, d // LANES, LANES).transpose(1, 2, 0).reshape(1, d, shard_cols)
    (w_in_g,) = _gather_call("gather_weights", [_columns_to_rows_call("w_in_to_wire", columns(w_in), d, WIRE_DTYPE)])
    w_main, w_small = _from_shards_call(w_in_g)

    grad_x, p_w_in, p_wout, p_small, p_conv = _device_step(
        x[0], loss_target[0], w_main, w_small, w_out[0].astype(WIRE_DTYPE), gdn_conv_w[0], norm1_w, sb_norm_w,
        gdn_A_log, gdn_dt_bias, gdn_norm_w, final_norm_w, distributed=True)

    r_w_in = [from_columns(a) for a in _adam_columns_call("adam_w_in", p_w_in, columns(w_in), columns(m_w_in),
                                                          columns(v_w_in))]
    r_wout = _adam_call("adam_w_out", p_wout, w_out[0], m_w_out[0], v_w_out[0], d // N_DEV)
    r_conv = _adam_call("adam_conv", p_conv, gdn_conv_w[0], m_gdn_conv_w[0], v_gdn_conv_w[0], CONV_WIDTH)

    row = lambda a: a.reshape(1, -1)
    n_small = len(_SMALL_ROWS) - 1
    r_small = _adam_small_call(
        p_small,
        [norm1_w, row(final_norm_w), sb_norm_w, gdn_norm_w, gdn_A_log, gdn_dt_bias],
        [m_norm1_w, row(m_final_norm_w), m_sb_norm_w, m_gdn_norm_w, m_gdn_A_log, m_gdn_dt_bias],
        [v_norm1_w, row(v_final_norm_w), v_sb_norm_w, v_gdn_norm_w, v_gdn_A_log, v_gdn_dt_bias])

    def small_out(kind, name):
        out = r_small[1 + kind * n_small + _SMALL_ROWS.index(name)]
        return out.reshape(final_norm_w.shape) if name == "final_norm_w" else out

    def outputs(kind):
        return (small_out(kind, "norm1_w"), r_w_in[kind], small_out(kind, "sb_norm_w"), r_conv[kind][None],
                small_out(kind, "gdn_A_log"), small_out(kind, "gdn_dt_bias"), small_out(kind, "gdn_norm_w"),
                r_wout[kind][None], small_out(kind, "final_norm_w"))

    return (r_small[0][0, 0], grad_x[None], *outputs(0), *outputs(1), *outputs(2), *outputs(3))
```

```python
import functools

import jax
import jax.numpy as jnp
from jax import lax
from jax.experimental import pallas as pl
from jax.experimental.pallas import tpu as pltpu

F32 = jnp.float32
MXU_DTYPE = jnp.bfloat16
WIRE_DTYPE = jnp.bfloat16
EXACT = lax.Precision.HIGHEST
EPS = 1e-6
N_DEV = 8
SB_HEAD_DIM = 64
GDN_HEAD_DIM = 128
GDN_HEADS = 4
GDN_CHUNKS_PER_STEP = 4
GDN_BWD_GROUP = 1
CHUNK = 64
CONV_WIDTH = 4
LANES = 128
SB_BLOCK = 128
SB_BQ = 256
VMEM_LIMIT_BYTES = 56 * 1024 * 1024

PIECE_COLS = 512
PROJ_PIECE_KINDS = ("heads", "heads", "heads", "gate", "heads", "heads", "heads", "gate")
SB_FIRST_BLOCK, GDN_FIRST_BLOCK = 0, 12

DPROJ_PIECE_OF_SLOT = (0, 1, 2, 4, 5, 6, 3, 7)
DPROJ_SB_SLOT, DPROJ_GDN_SLOT, DPROJ_GATE_SLOT = 0, 3, 6

ADAM_LR = 0.001
ADAM_B1 = 0.9
ADAM_B2 = 0.999
ADAM_EPS = 1e-08
ADAM_WD = 0.01
ADAM_STEP = 10

_NN = (((1,), (0,)), ((), ()))
_NT = (((1,), (1,)), ((), ()))
_TN = (((0,), (0,)), ((), ()))
_BNN = (((2,), (1,)), ((0,), (0,)))
_BNT = (((2,), (2,)), ((0,), (0,)))
_BTN = (((1,), (1,)), ((0,), (0,)))


def _mx(a, b):
    return jnp.dot(a, b, precision=EXACT, preferred_element_type=F32)


def _split(x):
    hi = x.astype(MXU_DTYPE)
    return hi, (x - hi.astype(F32)).astype(MXU_DTYPE)


def _m3_general(a, b, dims, right_low=True):
    ah, al = _split(a)
    bh, bl = _split(b)
    dot = lambda x, y: lax.dot_general(x, y, dims, preferred_element_type=F32)
    (contract, _), (batch, _) = dims
    free = [ax for ax in range(a.ndim) if ax not in contract and ax not in batch][0]
    m = a.shape[free]
    both = dot(jnp.concatenate([ah, al], axis=free), bh)
    out_axis = len(batch)
    hi_part = lax.slice_in_dim(both, 0, m, axis=out_axis)
    lo_part = lax.slice_in_dim(both, m, 2 * m, axis=out_axis)
    return hi_part + (dot(ah, bl) + lo_part if right_low else lo_part)


def _times_exact(a, b_exact, dims):
    ah, al = _split(a)
    (contract, _), (batch, _) = dims
    free = [ax for ax in range(a.ndim) if ax not in contract and ax not in batch][0]
    m = a.shape[free]
    both = lax.dot_general(jnp.concatenate([ah, al], axis=free), b_exact.astype(MXU_DTYPE), dims,
                           preferred_element_type=F32)
    out_axis = len(batch)
    return lax.slice_in_dim(both, 0, m, axis=out_axis) + lax.slice_in_dim(both, m, 2 * m, axis=out_axis)


def _exact_times(a_exact, b, dims):
    bh, bl = _split(b)
    n = b.shape[-1]
    both = lax.dot_general(a_exact.astype(MXU_DTYPE), jnp.concatenate([bh, bl], axis=-1), dims,
                           preferred_element_type=F32)
    return both[..., :n] + both[..., n:]


def _sigmoid(z):
    return 1.0 / (1.0 + jnp.exp(-z))


def _softplus(z):
    return jnp.maximum(z, 0.0) + jnp.log(1.0 + jnp.exp(-jnp.abs(z)))


def _params(*semantics):
    return pltpu.CompilerParams(dimension_semantics=semantics, vmem_limit_bytes=VMEM_LIMIT_BYTES)


def _inproj_call(x, norm_w, w_main, w_small, w_small_t, gather=(), tm=256):
    t_len, d = x.shape
    n = w_main.shape[1]
    ns = w_small.shape[1]
    nst = w_small_t.shape[0]
    ng = len(gather)
    steps = t_len // tm

    blocks_per_piece = PIECE_COLS // LANES
    n_gate_cols = PIECE_COLS * PROJ_PIECE_KINDS.count("gate")
    n_col_blocks = blocks_per_piece * PROJ_PIECE_KINDS.count("heads")

    def body(*refs):
        x_ref, nw_ref, wm_ref, ws_ref, wst_ref = refs[:5]
        cols_ref, pz_ref, ps_ref, pst_ref, ht_ref, r_ref = refs[5 + ng:11 + ng]
        copies = lambda: _direct_copies(refs[5:5 + ng], refs[11 + ng:11 + 2 * ng], *refs[11 + 2 * ng:], (False,) * ng)
        if ng:
            pl.when(pl.program_id(0) == 0)(lambda: _start_all(copies()))
        xv = x_ref[...]
        r = lax.rsqrt(jnp.mean(xv * xv, axis=-1, keepdims=True) + EPS)
        h = xv * r * nw_ref[...]
        hb = h.astype(MXU_DTYPE)
        n_block = n_gate = 0
        for piece, kind in enumerate(PROJ_PIECE_KINDS):
            out = jnp.dot(hb, wm_ref[:, piece * PIECE_COLS:(piece + 1) * PIECE_COLS], preferred_element_type=F32)
            if kind == "gate":
                pz_ref[:, n_gate * PIECE_COLS:(n_gate + 1) * PIECE_COLS] = out
                n_gate += 1
            else:
                for j in range(blocks_per_piece):
                    cols_ref[n_block + j] = out[:, j * LANES:(j + 1) * LANES]
                n_block += blocks_per_piece
        ps_ref[...] = jnp.dot(hb, ws_ref[...], preferred_element_type=F32)
        pst_ref[...] = lax.dot_general(wst_ref[...], hb, _NT, preferred_element_type=F32)
        ht_ref[...] = h.T.astype(MXU_DTYPE)
        r_ref[...] = r
        if ng:
            pl.when(pl.program_id(0) == steps - 1)(lambda: _wait_all(copies()))

    return pl.pallas_call(
        body, name="inproj",
        grid=(steps,),
        in_specs=[pl.BlockSpec((tm, d), lambda i: (i, 0)),
                  pl.BlockSpec((1, d), lambda i: (0, 0)),
                  pl.BlockSpec((d, n), lambda i: (0, 0)),
                  pl.BlockSpec((d, ns), lambda i: (0, 0)),
                  pl.BlockSpec((nst, d), lambda i: (0, 0))] + [_HBM] * ng,
        out_specs=[pl.BlockSpec((n_col_blocks, tm, LANES), lambda i: (0, i, 0)),
                   pl.BlockSpec((tm, n_gate_cols), lambda i: (i, 0)),
                   pl.BlockSpec((tm, ns), lambda i: (i, 0)),
                   pl.BlockSpec((nst, tm), lambda i: (0, i)),
                   pl.BlockSpec((d, tm), lambda i: (0, i)),
                   pl.BlockSpec((tm, 1), lambda i: (i, 0))] + [_HBM] * ng,
        out_shape=[jax.ShapeDtypeStruct((n_col_blocks, t_len, LANES), F32),
                   jax.ShapeDtypeStruct((t_len, n_gate_cols), F32),
                   jax.ShapeDtypeStruct((t_len, ns), F32),
                   jax.ShapeDtypeStruct((nst, t_len), F32),
                   jax.ShapeDtypeStruct((d, t_len), MXU_DTYPE),
                   jax.ShapeDtypeStruct((t_len, 1), F32)] + _direct_out_shapes(gather, (False,) * ng),
        scratch_shapes=_direct_semaphores(ng) if ng else [],
        compiler_params=_params("arbitrary"),
    )(x, norm_w, w_main, w_small, w_small_t, *gather)


def _running_sum_mm(x, tri):
    hi = x.astype(MXU_DTYPE)
    lo = (x - hi.astype(F32)).astype(MXU_DTYPE)
    return jnp.dot(hi, tri, preferred_element_type=F32) + jnp.dot(lo, tri, preferred_element_type=F32)


def _col_block(t_len, first):
    return pl.BlockSpec((1, t_len, LANES), lambda p: (first + p, 0, 0))


def _sb_iotas():
    row_i = lax.broadcasted_iota(jnp.int32, (SB_BQ, SB_BLOCK), 0)
    col_i = lax.broadcasted_iota(jnp.int32, (SB_BQ, SB_BLOCK), 1)
    sq_r = lax.broadcasted_iota(jnp.int32, (SB_BLOCK, SB_BLOCK), 0)
    sq_c = lax.broadcasted_iota(jnp.int32, (SB_BLOCK, SB_BLOCK), 1)
    return row_i, col_i, sq_r, sq_c


SB_DIAG_BLOCKS = SB_BQ // SB_BLOCK
SB_EXP_FLOOR = -110.0


def _sb_keys_descending(qi, tile, carry, z_bounds, n_heads, has_free):
    group = SB_DIAG_BLOCKS
    n_free = group * qi
    diag = list(range(group - 1, -1, -1))
    carry = tile([n_free + j for j in diag], [True] * group, carry, [j * SB_BLOCK for j in diag])

    def largest_exponent(c):
        worst = jnp.max(z_bounds[0] - c[1])
        for h in range(1, n_heads):
            worst = jnp.maximum(worst, jnp.max(z_bounds[h] - c[1 + h]))
        return worst

    always = group if has_free else 0

    def cond(state):
        return (state[0] < n_free) & ((state[1] > SB_EXP_FLOOR) | (state[0] < always))

    def body(state):
        first = n_free - 1 - state[0]
        c = tile([first - j for j in range(group)], [False] * group, state[2:])
        return (state[0] + group, largest_exponent(c), *c)

    out = lax.while_loop(cond, body, (jnp.int32(0), largest_exponent(carry), *carry))
    return out[2:], out[0]


def _sb_keys_ascending(qi, n_run, tile, carry, has_free):
    group = SB_DIAG_BLOCKS
    n_free = group * qi
    diag = list(range(group))
    kjs, los, masked = [n_free + j for j in diag], [j * SB_BLOCK for j in diag], [True] * group
    if has_free:
        early = lambda s: [n_free - n_run + group * s + j for j in range(group)]
        carry = lax.fori_loop(0, n_run // group - 1, lambda s, c: tile(early(s), [False] * group, c), carry)
        kjs, los, masked = [n_free - group + j for j in range(group)] + kjs, [0] * group + los, [False] * group + masked
    return tile(kjs, masked, carry, los)


def _sb_fwd_call(cols, t_len):
    nq = t_len // SB_BQ
    scale = float(SB_HEAD_DIM) ** -0.5
    n_pairs = 512 // LANES
    per_pair = LANES // SB_HEAD_DIM

    def body(q_blk, k_blk, v_blk, o_blk, st_ref, nrun_ref):
        q_ref, k_ref, v_ref, o_ref = q_blk.at[0], k_blk.at[0], v_blk.at[0], o_blk.at[0]
        lane = lax.broadcasted_iota(jnp.int32, (1, LANES), 1)
        row_i, col_i, sq_r, sq_c = _sb_iotas()
        ge = (sq_r >= sq_c).astype(MXU_DTYPE)
        hms = [((lane // SB_HEAD_DIM) == hh).astype(F32) for hh in range(per_pair)]
        k_sq = k_ref[...] * k_ref[...]
        k_norms = [jnp.sqrt(jnp.max(jnp.sum(k_sq * hm, axis=-1, keepdims=True))) * (1.02 * scale) for hm in hms]

        def q_block(qi, has_free):
            r0 = qi * SB_BQ if isinstance(qi, int) else pl.multiple_of(qi * SB_BQ, SB_BQ)
            rows = pl.ds(r0, SB_BQ)
            q_all = q_ref[rows, :]
            qms = [(q_all * (hm * scale)).astype(MXU_DTYPE) for hm in hms]
            z_bounds = [jnp.sqrt(jnp.sum(q_all * q_all * hm, axis=-1, keepdims=True)) * kn
                        for hm, kn in zip(hms, k_norms)]

            def tile(kjs, masked, kc, los=None):
                heads = range(per_pair)
                los = los or [0] * len(kjs)
                pairs = [(t, h) for t in range(len(kjs)) for h in heads]
                add_rows = lambda full, lo, part: full + part if lo == 0 else jnp.concatenate(
                    [full[:lo], full[lo:] + part], axis=0)
                acc, cs = kc[0], list(kc[1:])
                s0s = [kj * SB_BLOCK if isinstance(kj, int) else pl.multiple_of(kj * SB_BLOCK, SB_BLOCK) for kj in kjs]
                kbs = [k_ref[pl.ds(s0, SB_BLOCK), :].astype(MXU_DTYPE) for s0 in s0s]
                v_alls = [v_ref[pl.ds(s0, SB_BLOCK), :] for s0 in s0s]
                vms = {(t, h): (v_alls[t] * hms[h]).astype(MXU_DTYPE) for t, h in pairs}
                zs = {(t, h): lax.dot_general(qms[h][los[t]:], kbs[t], _NT, preferred_element_type=F32)
                      for t, h in pairs}
                masks = [(col_i[lo:] + s0) < (row_i[lo:] + r0) if m else None for m, lo, s0 in zip(masked, los, s0s)]
                keep = lambda t, a: a if masks[t] is None else jnp.where(masks[t], a, 0.0)
                sps = {(t, h): keep(t, _softplus(zs[t, h])) for t, h in pairs}
                sums = {p: _running_sum_mm(sps[p], ge) for p in pairs}
                mass = {}
                for t, h in pairs:
                    mass[t, h] = cs[h] if t == 0 else add_rows(
                        mass[t - 1, h], los[t - 1], jnp.sum(sps[t - 1, h], axis=-1, keepdims=True))
                ws = {(t, h): keep(t, jnp.exp(zs[t, h] - (sums[t, h] + mass[t, h][los[t]:]))) for t, h in pairs}
                for t, h in pairs:
                    acc = add_rows(acc, los[t], jnp.dot(ws[t, h].astype(MXU_DTYPE), vms[t, h],
                                                        preferred_element_type=F32))
                last = len(kjs) - 1
                cs = [add_rows(mass[last, h], los[last], jnp.sum(sps[last, h], axis=-1, keepdims=True)) for h in heads]
                return (acc, *cs)

            zero_col = jnp.zeros((SB_BQ, 1), F32)
            out, n_run = _sb_keys_descending(
                qi, tile, (jnp.zeros((SB_BQ, LANES), F32),) + (zero_col,) * per_pair, z_bounds, per_pair, has_free)
            o_ref[rows, :] = out[0]
            masses = jnp.zeros((SB_BQ, LANES), F32)
            for hh in range(per_pair):
                masses = jnp.where(lane == hh, out[1 + hh], masses)
            st_ref[rows, :] = masses
            nrun_ref[pl.program_id(0), qi] = n_run

        q_block(0, False)
        lax.fori_loop(1, nq, lambda qi, carry: (q_block(qi, True), carry)[1], 0)

    return pl.pallas_call(
        body, name="sb_fwd",
        grid=(n_pairs,),
        in_specs=[_col_block(t_len, SB_FIRST_BLOCK), _col_block(t_len, SB_FIRST_BLOCK + n_pairs),
                  _col_block(t_len, SB_FIRST_BLOCK + 2 * n_pairs)],
        out_specs=[_col_block(t_len, 0),
                   pl.BlockSpec((t_len, LANES), lambda p: (0, p)),
                   pl.BlockSpec(memory_space=pltpu.SMEM)],
        out_shape=[jax.ShapeDtypeStruct((n_pairs, t_len, LANES), F32),
                   jax.ShapeDtypeStruct((t_len, n_pairs * LANES), F32),
                   jax.ShapeDtypeStruct((n_pairs, nq), jnp.int32)],
        compiler_params=_params("arbitrary"),
    )(cols, cols, cols)


def _sb_bwd_call(cols, sp_total, n_run_all, d_o, dproj, t_len):
    nq = t_len // SB_BQ
    scale = float(SB_HEAD_DIM) ** -0.5
    n_pairs = 512 // LANES
    per_pair = LANES // SB_HEAD_DIM

    def body(q_blk, k_blk, v_blk, st_ref, nrun_ref, do_blk, dproj_in_ref, d_ref):
        q_ref, k_ref, v_ref, do_ref = q_blk.at[0], k_blk.at[0], v_blk.at[0], do_blk.at[0]
        lane = lax.broadcasted_iota(jnp.int32, (1, LANES), 1)
        row_i, col_i, sq_r, sq_c = _sb_iotas()
        lt = (sq_r < sq_c).astype(MXU_DTYPE)
        le = (sq_r <= sq_c).astype(MXU_DTYPE)
        hms = [((lane // SB_HEAD_DIM) == hh).astype(F32) for hh in range(per_pair)]
        d_ref[1] = jnp.zeros((t_len, LANES), F32)
        d_ref[2] = jnp.zeros((t_len, LANES), F32)

        def q_block(qi, has_free):
            r0 = qi * SB_BQ if isinstance(qi, int) else pl.multiple_of(qi * SB_BQ, SB_BQ)
            rows = pl.ds(r0, SB_BQ)
            q_all, do_all = q_ref[rows, :], do_ref[rows, :]
            qms = [(q_all * (hm * scale)).astype(MXU_DTYPE) for hm in hms]
            doms = [(do_all * hm).astype(MXU_DTYPE) for hm in hms]
            masses = st_ref[rows, :]
            totals = [jnp.sum(jnp.where(lane == hh, masses, 0.0), axis=-1, keepdims=True) for hh in range(per_pair)]

            def tile(kjs, masked, kc, los=None):
                heads = range(per_pair)
                los = los or [0] * len(kjs)
                pairs = [(t, h) for t in range(len(kjs)) for h in heads]
                add_rows = lambda full, lo, part: full + part if lo == 0 else jnp.concatenate(
                    [full[:lo], full[lo:] + part], axis=0)
                rsum = lambda a: jnp.sum(a, axis=-1, keepdims=True)
                dq, cls, gls = kc[0], list(kc[1:1 + per_pair]), list(kc[1 + per_pair:])
                s0s = [kj * SB_BLOCK if isinstance(kj, int) else pl.multiple_of(kj * SB_BLOCK, SB_BLOCK) for kj in kjs]
                k_alls = [k_ref[pl.ds(s0, SB_BLOCK), :] for s0 in s0s]
                v_alls = [v_ref[pl.ds(s0, SB_BLOCK), :] for s0 in s0s]
                kbs = [k_all.astype(MXU_DTYPE) for k_all in k_alls]
                vms = {(t, h): (v_alls[t] * hms[h]).astype(MXU_DTYPE) for t, h in pairs}
                kms = {(t, h): (k_alls[t] * (hms[h] * scale)).astype(MXU_DTYPE) for t, h in pairs}
                q_live = {(t, h): qms[h][los[t]:] for t, h in pairs}
                do_live = {(t, h): doms[h][los[t]:] for t, h in pairs}
                zs = {p: lax.dot_general(q_live[p], kbs[p[0]], _NT, preferred_element_type=F32) for p in pairs}
                das = {p: lax.dot_general(do_live[p], vms[p], _NT, preferred_element_type=F32) for p in pairs}
                masks = [(col_i[lo:] + s0) < (row_i[lo:] + r0) if m else None for m, lo, s0 in zip(masked, los, s0s)]
                keep = lambda t, a: a if masks[t] is None else jnp.where(masks[t], a, 0.0)
                sp_alls = {p: _softplus(zs[p]) for p in pairs}
                sps = {(t, h): keep(t, sp_alls[t, h]) for t, h in pairs}
                lefts = {p: _running_sum_mm(sps[p], lt) for p in pairs}
                cl = {}
                for t, h in pairs:
                    cl[t, h] = cls[h] if t == 0 else add_rows(cl[t - 1, h], los[t - 1], rsum(sps[t - 1, h]))
                ws = {(t, h): keep(t, jnp.exp(zs[t, h] - ((totals[h] - cl[t, h])[los[t]:] - lefts[t, h])))
                      for t, h in pairs}
                gs = {p: das[p] * ws[p] for p in pairs}
                g_sums = {p: _running_sum_mm(gs[p], le) for p in pairs}
                gl = {}
                for t, h in pairs:
                    gl[t, h] = gls[h] if t == 0 else add_rows(gl[t - 1, h], los[t - 1], rsum(gs[t - 1, h]))
                dzs = {(t, h): keep(t, gs[t, h] - jnp.exp(zs[t, h] - sp_alls[t, h]) * (gl[t, h][los[t]:] + g_sums[t, h])
                               ).astype(MXU_DTYPE) for t, h in pairs}
                for t in range(len(kjs)):
                    dk_t = jnp.zeros((SB_BLOCK, LANES), F32)
                    dv_t = jnp.zeros((SB_BLOCK, LANES), F32)
                    for h in heads:
                        dq = add_rows(dq, los[t], jnp.dot(dzs[t, h], kms[t, h], preferred_element_type=F32))
                        dk_t = dk_t + lax.dot_general(dzs[t, h], q_live[t, h], _TN, preferred_element_type=F32)
                        dv_t = dv_t + lax.dot_general(ws[t, h].astype(MXU_DTYPE), do_live[t, h], _TN,
                                                      preferred_element_type=F32)
                    d_ref[1, pl.ds(s0s[t], SB_BLOCK), :] += dk_t
                    d_ref[2, pl.ds(s0s[t], SB_BLOCK), :] += dv_t
                last = len(kjs) - 1
                cls = [add_rows(cl[last, h], los[last], rsum(sps[last, h])) for h in heads]
                gls = [add_rows(gl[last, h], los[last], rsum(gs[last, h])) for h in heads]
                return (dq, *cls, *gls)

            zero_col = jnp.zeros((SB_BQ, 1), F32)
            out = _sb_keys_ascending(qi, nrun_ref[pl.program_id(0), qi], tile,
                                     (jnp.zeros((SB_BQ, LANES), F32),) + (zero_col,) * (2 * per_pair), has_free)
            d_ref[0, rows, :] = out[0]

        q_block(0, False)
        lax.fori_loop(1, nq, lambda qi, carry: (q_block(qi, True), carry)[1], 0)

    return pl.pallas_call(
        body, name="sb_bwd",
        grid=(n_pairs,),
        in_specs=[_col_block(t_len, SB_FIRST_BLOCK), _col_block(t_len, SB_FIRST_BLOCK + n_pairs),
                  _col_block(t_len, SB_FIRST_BLOCK + 2 * n_pairs),
                  pl.BlockSpec((t_len, LANES), lambda p: (0, p)),
                  pl.BlockSpec(memory_space=pltpu.SMEM), _col_block(t_len, 0), _HBM],
        out_specs=pl.BlockSpec((3, t_len, LANES), lambda p: (DPROJ_SB_SLOT // 3, 0, p)),
        out_shape=jax.ShapeDtypeStruct(dproj.shape, dproj.dtype),
        input_output_aliases={6: 0},
        compiler_params=_params("arbitrary"),
    )(cols, cols, cols, sp_total, n_run_all, d_o, dproj)


def _conv_taps(xin, rows, t_len):
    taps = []
    for i in range(CONV_WIDTH):
        shift = CONV_WIDTH - 1 - i
        if shift == 0:
            taps.append(xin)
        else:
            taps.append(jnp.where(rows >= shift, pltpu.roll(xin, shift, axis=0), 0.0))
    return taps


def _gdn_prep_body_common(x_ref, w_ref, t_len):
    j = pl.program_id(0)
    xin = x_ref[...]
    rows = lax.broadcasted_iota(jnp.int32, (t_len, LANES), 0)
    taps = _conv_taps(xin, rows, t_len)
    pre = taps[0] * w_ref[0:1, :]
    for i in range(1, CONV_WIDTH):
        pre = pre + taps[i] * w_ref[i:i + 1, :]
    sg = _sigmoid(pre)
    act = pre * sg
    is_qk = j < 2 * GDN_HEADS
    nrm = jnp.where(is_qk, lax.rsqrt(jnp.sum(act * act, axis=-1, keepdims=True) + EPS), 1.0)
    sc = jnp.where(j < GDN_HEADS, float(GDN_HEAD_DIM) ** -0.5, 1.0)
    return j, rows, taps, pre, sg, act, is_qk, nrm, sc


def _gdn_prep_call(cols, conv_w, t_len, after):
    def body(x_blk, w_ref, after_ref, out_ref):
        _, _, _, _, _, act, _, nrm, sc = _gdn_prep_body_common(x_blk.at[0], w_ref, t_len)
        out_ref[...] = act * nrm * sc

    return pl.pallas_call(
        body, name="gdn_prep",
        grid=(3 * GDN_HEADS,),
        in_specs=[_col_block(t_len, GDN_FIRST_BLOCK),
                  pl.BlockSpec((CONV_WIDTH, LANES), lambda j: (0, j)),
                  pl.BlockSpec(memory_space=pl.ANY)],
        out_specs=pl.BlockSpec((t_len, LANES), lambda j: (0, j)),
        out_shape=jax.ShapeDtypeStruct((t_len, 3 * 512), F32),
        compiler_params=_params("arbitrary"),
    )(cols, conv_w, after)


def _gdn_prep_bwd_call(cols, conv_w, d_act3, dproj, t_len):
    def body(x_blk, w_ref, d_ref, dproj_in_ref, dx_ref, dw_ref):
        _, rows, taps, pre, sg, act, is_qk, nrm, sc = _gdn_prep_body_common(x_blk.at[0], w_ref, t_len)
        d_out = d_ref[0]
        dn = d_out * sc
        d_norm = nrm * dn - act * (nrm * nrm * nrm) * jnp.sum(dn * act, axis=-1, keepdims=True)
        d_act = jnp.where(is_qk, d_norm, d_out)
        d_pre = d_act * sg * (1.0 + pre * (1.0 - sg))
        dx = d_pre * w_ref[CONV_WIDTH - 1:CONV_WIDTH, :]
        dw_ref[CONV_WIDTH - 1:CONV_WIDTH, :] = jnp.sum(d_pre * taps[CONV_WIDTH - 1], axis=0, keepdims=True)
        for i in range(CONV_WIDTH - 1):
            shift = CONV_WIDTH - 1 - i
            up = jnp.where(rows < t_len - shift, pltpu.roll(d_pre, t_len - shift, axis=0), 0.0)
            dx = dx + up * w_ref[i:i + 1, :]
            dw_ref[i:i + 1, :] = jnp.sum(d_pre * taps[i], axis=0, keepdims=True)
        dx_ref[0] = dx

    return pl.pallas_call(
        body, name="gdn_prep_bwd",
        grid=(3 * GDN_HEADS,),
        in_specs=[_col_block(t_len, GDN_FIRST_BLOCK),
                  pl.BlockSpec((CONV_WIDTH, LANES), lambda j: (0, j)),
                  pl.BlockSpec((1, t_len, LANES), lambda j: (j // GDN_HEADS, 0, j % GDN_HEADS)), _HBM],
        out_specs=[pl.BlockSpec((1, t_len, LANES), lambda j: (DPROJ_GDN_SLOT + j // GDN_HEADS, 0, j % GDN_HEADS)),
                   pl.BlockSpec((CONV_WIDTH, LANES), lambda j: (0, j))],
        out_shape=[jax.ShapeDtypeStruct(dproj.shape, dproj.dtype),
                   jax.ShapeDtypeStruct((CONV_WIDTH, 3 * 512), F32)],
        input_output_aliases={3: 0},
        compiler_params=_params("arbitrary"),
    )(cols, conv_w, d_act3, dproj)


def _chunk_cumsum_matrix():
    r = lax.broadcasted_iota(jnp.int32, (LANES, LANES), 0)
    c = lax.broadcasted_iota(jnp.int32, (LANES, LANES), 1)
    return ((r <= c) & ((r // CHUNK) == (c // CHUNK))).astype(F32)


def _gdn_gates_call(ps, pst, alog_l, dtb_l, alog_c, dtb_c, t_len):
    def body(ps_ref, pst_ref, al_ref, dl_ref, ac_ref, dc_ref, beta_ref, gcol_ref, grow_ref):
        upper = _chunk_cumsum_matrix()
        lower = upper.T
        psv = ps_ref[...]
        beta_ref[...] = _sigmoid(psv)
        g_l = -jnp.exp(al_ref[...]) * _softplus(psv + dl_ref[...])
        g_r = -jnp.exp(ac_ref[...]) * _softplus(pst_ref[...] + dc_ref[...])
        for w in range(t_len // LANES):
            sl = slice(w * LANES, (w + 1) * LANES)
            gcol_ref[sl, :] = _mx(lower, g_l[sl, :])
            grow_ref[:, sl] = _mx(g_r[:, sl], upper)

    vm = pl.BlockSpec(memory_space=pltpu.VMEM)
    return pl.pallas_call(
        body, name="gdn_gates",
        in_specs=[vm] * 6, out_specs=[vm] * 3,
        out_shape=[jax.ShapeDtypeStruct((t_len, LANES), F32),
                   jax.ShapeDtypeStruct((t_len, LANES), F32),
                   jax.ShapeDtypeStruct((8, t_len), F32)],
        compiler_params=pltpu.CompilerParams(vmem_limit_bytes=VMEM_LIMIT_BYTES),
    )(ps, pst, alog_l, dtb_l, alog_c, dtb_c)


def _gdn_gates_bwd_call(ps, alog_l, dtb_l, d_l, t_len):
    def body(ps_ref, al_ref, dl_ref, d_ref, dps_ref, gal_ref, gdt_ref):
        lane = lax.broadcasted_iota(jnp.int32, (1, LANES), 1)
        psv = ps_ref[...]
        dv = d_ref[...]
        beta = _sigmoid(psv)
        ea = jnp.exp(al_ref[...])
        arg = psv + dl_ref[...]
        g = -ea * _softplus(arg)
        d_a = dv * (-ea) * _sigmoid(arg)
        is_a = (lane >= GDN_HEADS) & (lane < 2 * GDN_HEADS)
        dps_ref[...] = jnp.where(lane < GDN_HEADS, dv * beta * (1.0 - beta), jnp.where(is_a, d_a, 0.0))
        gdt_ref[...] = jnp.where(is_a, jnp.sum(d_a, axis=0, keepdims=True), 0.0)
        gal_ref[...] = jnp.where(is_a, jnp.sum(dv * g, axis=0, keepdims=True), 0.0)

    vm = pl.BlockSpec(memory_space=pltpu.VMEM)
    return pl.pallas_call(
        body, name="gdn_gates_bwd",
        in_specs=[vm] * 4, out_specs=[vm] * 3,
        out_shape=[jax.ShapeDtypeStruct((t_len, LANES), F32),
                   jax.ShapeDtypeStruct((1, LANES), F32),
                   jax.ShapeDtypeStruct((1, LANES), F32)],
        compiler_params=pltpu.CompilerParams(vmem_limit_bytes=VMEM_LIMIT_BYTES),
    )(ps, alog_l, dtb_l, d_l)


def _bm(a, b, right_low=True):
    return _m3_general(a, b, _BNN, right_low)


def _bm_nt(a, b, right_low=True):
    return _m3_general(a, b, _BNT, right_low)


def _bm_tn(a, b, right_low=True):
    return _m3_general(a, b, _BTN, right_low)


def _heads_of(ref, rows):
    return jnp.stack([ref[rows, h * GDN_HEAD_DIM:(h + 1) * GDN_HEAD_DIM] for h in range(GDN_HEADS)])


def _chunk_terms(q_ref, k_ref, v_ref, b_ref, gc_ref, gr_ref, c, incl, strict, n=1, scores=True):
    r0 = c * CHUNK if isinstance(c, int) else pl.multiple_of(c * CHUNK, CHUNK)
    rows = pl.ds(r0, n * CHUNK)
    per_chunk = lambda x: x.reshape(GDN_HEADS * n, CHUNK, x.shape[-1])
    q, k, v = (per_chunk(_heads_of(ref, rows)) for ref in (q_ref, k_ref, v_ref))
    lane_ids = lax.broadcasted_iota(jnp.int32, (1, LANES), 1)
    pick = lambda slab, first: jnp.stack([jnp.sum(jnp.where(lane_ids == first + h, slab, 0.0), axis=-1, keepdims=True)
                                          for h in range(GDN_HEADS)])
    b = per_chunk(pick(b_ref[rows, :], 0))
    gc = per_chunk(pick(gc_ref[rows, :], GDN_HEADS))
    gr = gr_ref[:, c] if n == 1 else gr_ref[:, c:c + n].reshape(GDN_HEADS * n, 1, CHUNK)
    dm = jnp.where(incl, jnp.exp(jnp.where(incl, gc - gr, 0.0)), 0.0)
    kb = k * b
    vb = v * b
    e = jnp.exp(gc)
    a = p = None
    if scores:
        kk_qk = _bm_nt(jnp.concatenate([kb, q], axis=1), k)
        a = jnp.where(strict, kk_qk[:, :CHUNK] * dm, 0.0)
        p = jnp.where(incl, kk_qk[:, CHUNK:] * dm, 0.0)
    gl = gc[:, CHUNK - 1:CHUNK, :]
    eg = jnp.exp(gl - gc)
    return rows, q, k, v, b, gc, dm, kb, vb, e, a, p, gl, eg


def _unit_lower_inverse(a, eye):
    x = -a
    tm = eye + x
    xp = _bm(x, x)
    for _ in range(4):
        both = _bm(jnp.concatenate([xp, tm], axis=1), xp)
        tm = tm + both[:, CHUNK:]
        xp = both[:, :CHUNK]
    return tm + _bm(tm, xp)


def _gdn_specs(t_len, n_chunks, reverse):
    cps = GDN_CHUNKS_PER_STEP
    steps = n_chunks // cps
    at = (lambda g: steps - 1 - g) if reverse else (lambda g: g)
    rows_blk = lambda width, part=0: pl.BlockSpec((cps * CHUNK, width), lambda g: (at(g), part))
    gate_r = pl.BlockSpec((GDN_HEADS, cps, 1, CHUNK), lambda g: (0, at(g), 0, 0))
    per_chunk = lambda r, c: pl.BlockSpec((GDN_HEADS, cps, r, c), lambda g: (0, at(g), 0, 0))
    return cps, steps, rows_blk, gate_r, per_chunk


def _gdn_fwd_call(gact, beta_c, gam_c, gam_r, t_len):
    n_chunks = t_len // CHUNK
    dk = GDN_HEAD_DIM
    width = GDN_HEADS * dk
    cps, steps, rows_blk, gate_r, per_chunk = _gdn_specs(t_len, n_chunks, False)

    def body(q_ref, k_ref, v_ref, b_ref, gc_ref, gr_ref, o_ref, s_ref, t_ref, a_ref, p_ref, uw_ref, vn_ref, state_ref):
        row = lax.broadcasted_iota(jnp.int32, (CHUNK, CHUNK), 0)
        col = lax.broadcasted_iota(jnp.int32, (CHUNK, CHUNK), 1)
        incl, strict = row >= col, row > col
        eye = (row == col).astype(F32)

        @pl.when(pl.program_id(0) == 0)
        def _():
            state_ref[...] = jnp.zeros_like(state_ref)

        _, q, k, v, b, gc, dm, kb, vb, e, a, p, gl, eg = _chunk_terms(
            q_ref, k_ref, v_ref, b_ref, gc_ref, gr_ref, 0, incl, strict, cps)
        tm = _unit_lower_inverse(a, eye)
        uw = _bm(tm, jnp.concatenate([vb, kb * e], axis=2))
        w_qe = jnp.concatenate([uw[:, :, dk:], q * e], axis=1)
        u, kd, decay = uw[:, :, :dk], k * eg, jnp.exp(gl)
        per_chunk_block = lambda x: x.reshape(GDN_HEADS, cps, CHUNK, CHUNK)
        t_ref[...], a_ref[...], p_ref[...] = per_chunk_block(tm), per_chunk_block(a), per_chunk_block(p)
        uw_heads = uw.reshape(GDN_HEADS, cps * CHUNK, 2 * dk)
        for h in range(GDN_HEADS):
            uw_ref[:, h * 2 * dk:(h + 1) * 2 * dk] = uw_heads[h]

        of_chunk = lambda x, c: jnp.stack([x[h * cps + c] for h in range(GDN_HEADS)])
        s = state_ref[...]
        for c in range(cps):
            ws_qs = _bm(of_chunk(w_qe, c), s)
            vn = of_chunk(u, c) - ws_qs[:, :CHUNK]
            o = ws_qs[:, CHUNK:] + _bm(of_chunk(p, c), vn)
            for h in range(GDN_HEADS):
                o_ref[c * CHUNK:(c + 1) * CHUNK, h * dk:(h + 1) * dk] = o[h]
                vn_ref[c * CHUNK:(c + 1) * CHUNK, h * dk:(h + 1) * dk] = vn[h]
            s_ref[:, c] = s
            s = s * of_chunk(decay, c) + _bm_tn(of_chunk(kd, c), vn)
        state_ref[...] = s

    scores = jax.ShapeDtypeStruct((GDN_HEADS, n_chunks, CHUNK, CHUNK), F32)
    return pl.pallas_call(
        body, name="gdn_fwd",
        grid=(steps,),
        in_specs=[rows_blk(width, 0), rows_blk(width, 1), rows_blk(width, 2), rows_blk(LANES), rows_blk(LANES), gate_r],
        out_specs=[rows_blk(width), per_chunk(dk, dk), per_chunk(CHUNK, CHUNK), per_chunk(CHUNK, CHUNK),
                   per_chunk(CHUNK, CHUNK), rows_blk(2 * width), rows_blk(width)],
        out_shape=[jax.ShapeDtypeStruct((t_len, width), F32),
                   jax.ShapeDtypeStruct((GDN_HEADS, n_chunks, dk, dk), F32), scores, scores, scores,
                   jax.ShapeDtypeStruct((t_len, 2 * width), F32), jax.ShapeDtypeStruct((t_len, width), F32)],
        scratch_shapes=[pltpu.VMEM((GDN_HEADS, dk, dk), F32)],
        compiler_params=_params("arbitrary"),
    )(gact, gact, gact, beta_c, gam_c, gam_r)


def _gdn_bwd_call(gact, beta_c, gam_c, gam_r, saved, d_o, t_len, scatter=()):
    n_chunks = t_len // CHUNK
    dk = GDN_HEAD_DIM
    width = GDN_HEADS * dk
    cps, steps, rows_blk, gate_r, per_chunk = _gdn_specs(t_len, n_chunks, True)
    nx = len(scatter)
    n_in = 13

    def body(*refs):
        q_ref, k_ref, v_ref, b_ref, gc_ref, gr_ref = refs[:6]
        saved_refs, do_ref = refs[6:12], refs[12]
        d_ref, dgate_ref = refs[n_in + nx:n_in + 2 + nx]
        dstate_ref = refs[n_in + 2 + 2 * nx]
        copies = lambda: _direct_copies(refs[n_in:n_in + nx], refs[n_in + 2 + nx:n_in + 2 + 2 * nx],
                                        *refs[n_in + 3 + 2 * nx:], (True,) * nx)
        if nx:
            pl.when(pl.program_id(0) == 0)(lambda: _start_all(copies()))
        row = lax.broadcasted_iota(jnp.int32, (CHUNK, CHUNK), 0)
        col = lax.broadcasted_iota(jnp.int32, (CHUNK, CHUNK), 1)
        incl, strict = row >= col, row > col
        ng = GDN_BWD_GROUP
        nb = GDN_HEADS * ng
        upper = jnp.broadcast_to((row <= col).astype(F32), (nb, CHUNK, CHUNK))
        ones = jnp.ones((nb, CHUNK, LANES), F32)
        last_row = lax.broadcasted_iota(jnp.int32, (CHUNK, 1), 0) == CHUNK - 1
        lane_ids = lax.broadcasted_iota(jnp.int32, (1, LANES), 1)
        rsum = lambda m: jnp.sum(m, axis=-1, keepdims=True)
        total = lambda m: jnp.sum(rsum(m), axis=1, keepdims=True)
        of_chunk = lambda x, c: jnp.stack([x[h * ng + c] for h in range(GDN_HEADS)])

        @pl.when(pl.program_id(0) == 0)
        def _():
            dstate_ref[...] = jnp.zeros_like(dstate_ref)

        for c0 in range(cps - ng, -1, -ng):
            group(c0, q_ref, k_ref, v_ref, b_ref, gc_ref, gr_ref, saved_refs, do_ref, d_ref, dgate_ref, dstate_ref,
                  incl, strict, upper, ones, last_row, lane_ids, rsum, total, of_chunk)
        if nx:
            pl.when(pl.program_id(0) == steps - 1)(lambda: _wait_all(copies()))

    def group(c0, q_ref, k_ref, v_ref, b_ref, gc_ref, gr_ref, saved_refs, do_ref, d_ref, dgate_ref, dstate_ref,
              incl, strict, upper, ones, last_row, lane_ids, rsum, total, of_chunk):
        ng = GDN_BWD_GROUP
        nb = GDN_HEADS * ng
        rows = pl.ds(c0 * CHUNK, ng * CHUNK)
        s_ref, t_ref, a_ref, p_ref, uw_ref, vn_ref = saved_refs
        _, q, k, v, b, gc, dm, kb, vb, e, _, _, gl, eg = _chunk_terms(
            q_ref, k_ref, v_ref, b_ref, gc_ref, gr_ref, c0, incl, strict, ng, scores=False)
        s = s_ref[:, c0:c0 + ng].reshape(nb, dk, dk)
        tm = t_ref[:, c0:c0 + ng].reshape(nb, CHUNK, CHUNK)
        a = a_ref[:, c0:c0 + ng].reshape(nb, CHUNK, CHUNK)
        p = p_ref[:, c0:c0 + ng].reshape(nb, CHUNK, CHUNK)
        d_out = _heads_of(do_ref, rows).reshape(nb, CHUNK, dk)
        vn = _heads_of(vn_ref, rows).reshape(nb, CHUNK, dk)
        uw = jnp.stack([uw_ref[rows, h * 2 * dk:(h + 1) * 2 * dk] for h in range(GDN_HEADS)]).reshape(nb, CHUNK, 2 * dk)
        u, w = uw[:, :, :dk], uw[:, :, dk:]
        el = jnp.exp(gl)
        kbe = kb * e
        qe = q * e
        kd = k * eg
        bm, bm_nt, bm_tn = (functools.partial(f, right_low=False) for f in (_bm, _bm_nt, _bm_tn))
        pt_do = _bm_tn(p, d_out)
        qet_do = _bm_tn(qe, d_out)

        ds = dstate_ref[...]
        d_vn_c, ds_c = [None] * ng, [None] * ng
        for c in range(ng - 1, -1, -1):
            ds_c[c] = ds
            d_vn_c[c] = of_chunk(pt_do, c) + bm(of_chunk(kd, c), ds)
            ds = of_chunk(el, c) * ds + of_chunk(qet_do, c) - bm_tn(of_chunk(w, c), d_vn_c[c])
        dstate_ref[...] = ds
        by_chunk = lambda xs: jnp.stack([xs[c][h] for h in range(GDN_HEADS) for c in range(ng)])
        d_vn, ds = by_chunk(d_vn_c), by_chunk(ds_c)

        on_s = bm_nt(jnp.concatenate([d_out, d_vn], axis=1), s)
        d_qe, d_w = on_s[:, :CHUNK], -on_s[:, CHUNK:]
        d_p = jnp.where(incl, _bm_nt(d_out, vn), 0.0)
        d_kd = _bm_nt(vn, ds)
        d_both = _bm_tn(tm, jnp.concatenate([d_vn, d_w], axis=2))
        d_vb, d_kbe = d_both[:, :, :dk], d_both[:, :, dk:]
        d_a = -jnp.where(strict, _bm_nt(d_both, uw), 0.0)
        m = d_a * dm
        n = d_p * dm
        on_k = bm(jnp.concatenate([m, n], axis=1), k)
        d_kb = on_k[:, :CHUNK] + d_kbe * e
        d_q = on_k[:, CHUNK:] + d_qe * e
        d_k = (bm_tn(jnp.concatenate([m, n], axis=1), jnp.concatenate([kb, q], axis=1))
               + d_kd * eg + b * d_kb)
        d_v = b * d_vb
        r = d_a * a + d_p * p
        kd_term = rsum(d_kd * kd)
        d_gl = total(ds * s) * el + jnp.sum(kd_term, axis=1, keepdims=True)
        d_gam = (rsum(r) - _times_exact(r, ones, _BTN)[:, :, 0:1] + rsum(d_qe * qe) + rsum(d_kbe * kbe) - kd_term
                 + jnp.where(last_row, d_gl, 0.0))
        d_beta = rsum(d_kb * k) + rsum(d_vb * v)
        d_g = _exact_times(upper, d_gam * ones, _BNN)[:, :, 0:1]
        per_head = lambda x: x.reshape(GDN_HEADS, ng * CHUNK, x.shape[-1])
        d_q, d_k, d_v, d_beta, d_g = (per_head(x) for x in (d_q, d_k, d_v, d_beta, d_g))
        gates = jnp.zeros((ng * CHUNK, LANES), F32)
        for h in range(GDN_HEADS):
            lanes = slice(h * dk, (h + 1) * dk)
            d_ref[0, rows, lanes] = d_q[h]
            d_ref[1, rows, lanes] = d_k[h]
            d_ref[2, rows, lanes] = d_v[h]
            gates = gates + (jnp.where(lane_ids == h, d_beta[h], 0.0)
                             + jnp.where(lane_ids == GDN_HEADS + h, d_g[h], 0.0))
        dgate_ref[rows, :] = gates

    d_spec = pl.BlockSpec((3, cps * CHUNK, width), lambda g: (0, steps - 1 - g, 0))
    return pl.pallas_call(
        body, name="gdn_bwd",
        grid=(steps,),
        in_specs=[rows_blk(width, 0), rows_blk(width, 1), rows_blk(width, 2), rows_blk(LANES), rows_blk(LANES), gate_r,
                  per_chunk(dk, dk), per_chunk(CHUNK, CHUNK), per_chunk(CHUNK, CHUNK), per_chunk(CHUNK, CHUNK),
                  rows_blk(2 * width), rows_blk(width), rows_blk(width)] + [_HBM] * nx,
        out_specs=[d_spec, rows_blk(LANES)] + [_HBM] * nx,
        out_shape=[jax.ShapeDtypeStruct((3, t_len, width), F32),
                   jax.ShapeDtypeStruct((t_len, LANES), F32)] + _direct_out_shapes(scatter, (True,) * nx),
        scratch_shapes=[pltpu.VMEM((GDN_HEADS, dk, dk), F32)] + (_direct_semaphores(nx) if nx else []),
        compiler_params=_params("arbitrary"),
    )(gact, gact, gact, beta_c, gam_c, gam_r, *saved, d_o, *scatter)


def _group_sums(x, group):
    rows, width = x.shape
    lane = lax.broadcasted_iota(jnp.int32, (1, LANES), 1)
    out = []
    for t in range(width // LANES):
        seg = x[:, t * LANES:(t + 1) * LANES]
        if group == LANES:
            out.append(jnp.broadcast_to(jnp.sum(seg, axis=-1, keepdims=True), (rows, LANES)))
        else:
            low = jnp.sum(jnp.where(lane < group, seg, 0.0), axis=-1, keepdims=True)
            high = jnp.sum(jnp.where(lane < group, 0.0, seg), axis=-1, keepdims=True)
            out.append(jnp.where(lane < group, low, high))
    return jnp.concatenate(out, axis=1)


def _post_call(o_sb, o_gd, proj_gates, x, target, w_out, sbw, gdw, fw, tm=256):
    t_len, d = x.shape
    half = 512
    sb_blocks = half // LANES

    def body(osb_ref, ogd_ref, zsb_ref, zgd_ref, x_ref, tg_ref, wo_ref, sbw_ref, gdw_ref, fw_ref,
             dx2_ref, dosb_ref, dogd_ref, dz_ref, loss_ref, gfw_ref, gsb_ref, ggd_ref, gwo_ref):
        step = pl.program_id(0)

        @pl.when(step == 0)
        def _():
            loss_ref[...] = jnp.zeros_like(loss_ref)
            gfw_ref[...] = jnp.zeros_like(gfw_ref)
            gsb_ref[...] = jnp.zeros_like(gsb_ref)
            ggd_ref[...] = jnp.zeros_like(ggd_ref)
            gwo_ref[...] = jnp.zeros_like(gwo_ref)

        def head_forward(o, z, w, head_dim):
            r = lax.rsqrt(_group_sums(o * o, head_dim) * (1.0 / head_dim) + EPS)
            nrm = o * r * w
            sg = _sigmoid(z)
            return r, nrm, sg, nrm * (z * sg)

        def head_backward(d_m, o, z, w, head_dim, r, nrm, sg):
            d_n = d_m * (z * sg)
            d_z = d_m * nrm * (sg * (1.0 + z * (1.0 - sg)))
            dnw = d_n * w
            d_o = r * dnw - o * (r * r * r) * (_group_sums(dnw * o, head_dim) * (1.0 / head_dim))
            return d_o, d_z, jnp.sum(d_n * o * r, axis=0, keepdims=True)

        osb = jnp.concatenate([osb_ref[j] for j in range(sb_blocks)], axis=1)
        ogd, zsb, zgd = ogd_ref[...], zsb_ref[...], zgd_ref[...]
        sbw_v, gdw_v = sbw_ref[...], gdw_ref[...]
        r_sb, n_sb, sg_sb, m_sb = head_forward(osb, zsb, sbw_v, SB_HEAD_DIM)
        r_gd, n_gd, sg_gd, m_gd = head_forward(ogd, zgd, gdw_v, GDN_HEAD_DIM)
        mixed = jnp.concatenate([m_sb, m_gd], axis=1).astype(MXU_DTYPE)
        wo = wo_ref[...]
        x2 = x_ref[...] + jnp.dot(mixed, wo, preferred_element_type=F32)
        r2 = lax.rsqrt(jnp.mean(x2 * x2, axis=-1, keepdims=True) + EPS)
        fw_v = fw_ref[...]
        err = x2 * r2 * fw_v - tg_ref[...]
        loss_ref[...] += 0.5 * jnp.sum(jnp.sum(err * err, axis=-1, keepdims=True) * (1.0 / d))
        dy = err * (1.0 / d)
        gg = dy * fw_v
        dx2 = r2 * gg - x2 * ((r2 * r2 * r2) * jnp.mean(gg * x2, axis=-1, keepdims=True))
        gfw_ref[...] += jnp.sum(dy * x2 * r2, axis=0, keepdims=True)
        dx2_ref[...] = dx2
        dx2b = dx2.astype(MXU_DTYPE)
        d_mixed = lax.dot_general(dx2b, wo, _NT, preferred_element_type=F32)
        gwo_ref[...] += lax.dot_general(mixed, dx2b, _TN, preferred_element_type=F32)
        d_osb, d_zsb, gsb = head_backward(d_mixed[:, :half], osb, zsb, sbw_v, SB_HEAD_DIM, r_sb, n_sb, sg_sb)
        d_ogd, d_zgd, ggd = head_backward(d_mixed[:, half:], ogd, zgd, gdw_v, GDN_HEAD_DIM, r_gd, n_gd, sg_gd)
        for j in range(sb_blocks):
            dosb_ref[j] = d_osb[:, j * LANES:(j + 1) * LANES]
        dogd_ref[...] = d_ogd
        dz_ref[0] = d_zsb
        dz_ref[1] = d_zgd
        gsb_ref[...] += gsb
        ggd_ref[...] += ggd

    row_blk = lambda w: pl.BlockSpec((tm, w), lambda i: (i, 0))
    blocks_blk = pl.BlockSpec((sb_blocks, tm, LANES), lambda i: (0, i, 0))
    fixed = lambda r, w: pl.BlockSpec((r, w), lambda i: (0, 0))
    return pl.pallas_call(
        body, name="post",
        grid=(t_len // tm,),
        in_specs=[blocks_blk, row_blk(half),
                  pl.BlockSpec((tm, half), lambda i: (i, 0)),
                  pl.BlockSpec((tm, half), lambda i: (i, 1)),
                  row_blk(d), row_blk(d), fixed(d, d), fixed(1, half), fixed(1, half), fixed(1, d)],
        out_specs=[row_blk(d), blocks_blk, row_blk(half),
                   pl.BlockSpec((2, tm, half), lambda i: (DPROJ_GATE_SLOT // 2, i, 0)),
                   fixed(1, LANES), fixed(1, d), fixed(1, half), fixed(1, half), fixed(d, d)],
        out_shape=[jax.ShapeDtypeStruct((t_len, d), F32), jax.ShapeDtypeStruct((sb_blocks, t_len, LANES), F32),
                   jax.ShapeDtypeStruct((t_len, half), F32),
                   jax.ShapeDtypeStruct((len(DPROJ_PIECE_OF_SLOT), t_len, half), F32),
                     jax.ShapeDtypeStruct((1, LANES), F32), jax.ShapeDtypeStruct((1, d), F32),
                     jax.ShapeDtypeStruct((1, half), F32), jax.ShapeDtypeStruct((1, half), F32),
                     jax.ShapeDtypeStruct((d, d), F32)],
        compiler_params=_params("arbitrary"),
    )(o_sb, o_gd, proj_gates, proj_gates, x, target, w_out, sbw, gdw, fw)


def _piece_of_slot(s):
    return jnp.where(s < DPROJ_GDN_SLOT, s, jnp.where(s < DPROJ_GATE_SLOT, s + 1,
                                                     jnp.where(s == DPROJ_GATE_SLOT, 3, 7)))


def _gw_in_call(h_t, dproj8):
    d, t_len = h_t.shape
    n_piece, _, pw = dproj8.shape

    def body(ht_ref, dp_ref, gw_ref):
        gw_ref[...] = jnp.dot(ht_ref[...], dp_ref[0].astype(MXU_DTYPE), preferred_element_type=F32)

    return pl.pallas_call(
        body, name="gw_in",
        grid=(n_piece,),
        in_specs=[pl.BlockSpec((d, t_len), lambda s: (0, 0)),
                  pl.BlockSpec((1, t_len, pw), lambda s: (s, 0, 0))],
        out_specs=pl.BlockSpec((d, pw), lambda s: (0, _piece_of_slot(s))),
        out_shape=jax.ShapeDtypeStruct((d, n_piece * pw), F32),
        compiler_params=_params("arbitrary"),
    )(h_t, dproj8)


def _slot_of_piece(p):
    return jnp.where(p < DPROJ_GDN_SLOT, p, jnp.where(p == 3, DPROJ_GATE_SLOT, jnp.where(p < 7, p - 1, 7)))


def _gw_in_shards_call(h_t, dproj8, dsmall, out_dtype):
    d, t_len = h_t.shape
    n_piece, _, pw = dproj8.shape
    ns = dsmall.shape[1]
    n_pairs = N_DEV // 2

    def body(ht_ref, dp_ref, ds_ref, chip_ref, prev_ref, gates_ref, send_ref, recv_ref, send_sems, recv_sems):
        p = pl.program_id(0)
        x_pos, y_pos, c = lax.axis_index("x"), lax.axis_index("y"), lax.axis_index("c")
        to_sibling = lambda pair: pltpu.make_async_remote_copy(
            src_ref=send_ref.at[pair], dst_ref=recv_ref.at[pair], send_sem=send_sems.at[pair],
            recv_sem=recv_sems.at[pair], device_id=(x_pos, y_pos, 1 - c), device_id_type=_MESH)

        @pl.when(p == 0)
        def _():
            gates_ref[...] = jnp.dot(ht_ref[...], ds_ref[...].astype(MXU_DTYPE), preferred_element_type=F32)

        def emit(s, tail):
            x = jnp.concatenate([prev_ref[...], tail], axis=1)
            y = x if s == 0 else pltpu.roll(x, SHARD_PAD - s, axis=1)
            shard = y[:, :SHARD_COLS].astype(out_dtype)

            @pl.when(c == s % 2)
            def _():
                chip_ref[s // 2] = shard

            @pl.when(c != s % 2)
            def _():
                send_ref[s // 2] = shard
                to_sibling(s // 2).start()

        @pl.when(p < n_piece)
        def _():
            cur = jnp.dot(ht_ref[...], dp_ref[0].astype(MXU_DTYPE), preferred_element_type=F32)
            for s in range(n_piece - 1):
                pl.when(p == s + 1)(functools.partial(emit, s, cur[:, :SHARD_PAD - pw]))
            prev_ref[...] = cur

        @pl.when(p == n_piece)
        def _():
            emit(n_piece - 1, gates_ref[...])
            for pair in range(n_pairs):
                to_sibling(pair).wait_send()
            for pair in range(n_pairs):
                to_sibling(pair).wait_recv()
                chip_ref[pair] = (chip_ref[pair].astype(F32) + recv_ref[pair].astype(F32)).astype(out_dtype)

    shards_of_side = lambda: pltpu.VMEM((n_pairs, d, SHARD_COLS), out_dtype)
    return pl.pallas_call(
        body, name="gw_in",
        grid=(n_piece + 1,),
        in_specs=[pl.BlockSpec((d, t_len), lambda p: (0, 0)),
                  pl.BlockSpec((1, t_len, pw), lambda p: (_slot_of_piece(jnp.minimum(p, n_piece - 1)), 0, 0)),
                  pl.BlockSpec((t_len, ns), lambda p: (0, 0))],
        out_specs=pl.BlockSpec((n_pairs, d, SHARD_COLS), lambda p: (0, 0, 0)),
        out_shape=jax.ShapeDtypeStruct((n_pairs, d, SHARD_COLS), out_dtype),
        scratch_shapes=[pltpu.VMEM((d, pw), F32), pltpu.VMEM((d, ns), F32), shards_of_side(), shards_of_side(),
                        pltpu.SemaphoreType.DMA((n_pairs,)), pltpu.SemaphoreType.DMA((n_pairs,))],
        compiler_params=_params("arbitrary"),
    )(h_t, dproj8, dsmall)


def _gw_small_call(h_t, dsmall, tm=512):
    d, t_len = h_t.shape
    ns = dsmall.shape[1]

    def body(ht_ref, dp_ref, gw_ref):
        @pl.when(pl.program_id(0) == 0)
        def _():
            gw_ref[...] = jnp.zeros_like(gw_ref)

        gw_ref[...] += jnp.dot(ht_ref[...], dp_ref[...].astype(MXU_DTYPE), preferred_element_type=F32)

    return pl.pallas_call(
        body, name="gw_small",
        grid=(t_len // tm,),
        in_specs=[pl.BlockSpec((d, tm), lambda t: (0, t)),
                  pl.BlockSpec((tm, ns), lambda t: (t, 0))],
        out_specs=pl.BlockSpec((d, ns), lambda t: (0, 0)),
        out_shape=jax.ShapeDtypeStruct((d, ns), F32),
        compiler_params=_params("arbitrary"),
    )(h_t, dsmall)


def _dx_call(dproj8, dsmall, w_main, w_small, x, r, dx2, norm_w, chip_scatter=(), peer_scatter=(), tm=256):
    t_len, d = x.shape
    n_piece, _, pw = dproj8.shape
    ns = dsmall.shape[1]
    nx = len(chip_scatter)
    n_peer = len(peer_scatter)
    steps = t_len // tm
    n_in = 8 + nx + n_peer
    n_out = 2 + nx + n_peer + nx
    reduce_step = steps // 2

    def body(*refs):
        dp_ref, ds_ref, wm_ref, ws_ref, x_ref, r_ref, dx2_ref, nw_ref = refs[:8]
        gx_ref, gnw_ref = refs[n_in:n_in + 2]
        scratch = refs[n_in + n_out:]
        chip = lambda: _chip_reduce_copies(refs[8], refs[n_in + 2], refs[n_in + n_out - 1], scratch[-2], scratch[-1],
                                           *scratch[:3])
        if nx:
            @pl.when(pl.program_id(0) == 0)
            def _():
                cp = chip()
                for name in ("to_reduce", "keep", "load_own"):
                    cp[name].start()

        @pl.when(pl.program_id(0) == 0)
        def _():
            gnw_ref[...] = jnp.zeros_like(gnw_ref)

        dh = lax.dot_general(ds_ref[...].astype(MXU_DTYPE), ws_ref[...], _NT, preferred_element_type=F32)
        for s, p in enumerate(DPROJ_PIECE_OF_SLOT):
            dh = dh + lax.dot_general(dp_ref[s].astype(MXU_DTYPE), wm_ref[:, p * pw:(p + 1) * pw], _NT,
                                      preferred_element_type=F32)
        xv, rv = x_ref[...], r_ref[...]
        dn = dh * nw_ref[...]
        gx_ref[...] = dx2_ref[...] + rv * dn - xv * ((rv * rv * rv) * jnp.mean(dn * xv, axis=-1, keepdims=True))
        gnw_ref[...] += jnp.sum(dh * xv * rv, axis=0, keepdims=True)

        if nx:
            @pl.when(pl.program_id(0) == reduce_step)
            def _():
                cp = chip()
                own_buf, got_buf = scratch[-2], scratch[-1]
                cp["direct"].start()
                cp["to_reduce"].wait_recv()
                cp["load_got"].start()
                cp["load_own"].wait()
                cp["load_got"].wait()
                own_buf[...] = (own_buf[...].astype(F32) + got_buf[...].astype(F32)).astype(own_buf.dtype)
                cp["reduced"].start()

        @pl.when(pl.program_id(0) == steps - 1)
        def _():
            if n_peer:
                small_ref, parts_ref = refs[8 + nx:n_in]
                small_buf = scratch[6]
                small_buf[...] = small_ref[...]
                small_buf[0:1, :] = gnw_ref[...]
                peer_copies = _direct_copies([small_buf, parts_ref], refs[n_in + 2 + nx:n_in + 2 + nx + n_peer],
                                             *scratch[3:6], [False, True])
                _start_all(peer_copies)
            if nx:
                cp = chip()
                for name in ("to_reduce", "direct", "reduced"):
                    cp[name].wait_send()
                cp["direct"].wait_recv()
                cp["reduced"].wait_recv()
                cp["keep"].wait()
            if n_peer:
                _wait_all(peer_copies)

    assert n_peer in (0, 2) and (nx == 1 or not n_peer)
    peer_in_specs = [pl.BlockSpec(peer_scatter[0].shape, lambda i: (0, 0)), _HBM] if n_peer else []
    peer_scratch = _direct_semaphores(n_peer) + [pltpu.VMEM(peer_scatter[0].shape, F32)] if n_peer else []
    return pl.pallas_call(
        body, name="dx",
        grid=(steps,),
        in_specs=[pl.BlockSpec((n_piece, tm, pw), lambda i: (0, i, 0)),
                  pl.BlockSpec((tm, ns), lambda i: (i, 0)),
                  pl.BlockSpec((d, n_piece * pw), lambda i: (0, 0)),
                  pl.BlockSpec((d, ns), lambda i: (0, 0)),
                  pl.BlockSpec((tm, d), lambda i: (i, 0)),
                  pl.BlockSpec((tm, 1), lambda i: (i, 0)),
                  pl.BlockSpec((tm, d), lambda i: (i, 0)),
                  pl.BlockSpec((1, d), lambda i: (0, 0))] + [_HBM] * nx + peer_in_specs,
        out_specs=[pl.BlockSpec((tm, d), lambda i: (i, 0)),
                   pl.BlockSpec((1, d), lambda i: (0, 0))] + [_HBM] * (nx + n_peer + nx),
        out_shape=[jax.ShapeDtypeStruct((t_len, d), F32), jax.ShapeDtypeStruct((1, d), F32)]
                  + [jax.ShapeDtypeStruct((N_CHIPS - 1,) + a.shape[1:], a.dtype) for a in chip_scatter]
                  + (_direct_out_shapes(peer_scatter, [False, True]) if n_peer else [])
                  + [jax.ShapeDtypeStruct(a.shape[1:], a.dtype) for a in chip_scatter],
        scratch_shapes=([pltpu.SemaphoreType.DMA((3,))] * 3 if nx else []) + peer_scratch
                       + [pltpu.VMEM(a.shape[1:], a.dtype) for a in chip_scatter for _ in range(2)],
        compiler_params=_params("arbitrary"),
    )(dproj8, dsmall, w_main, w_small, x, r, dx2, norm_w, *chip_scatter, *peer_scatter)


def _direct_out_shapes(srcs, per_peer):
    return [jax.ShapeDtypeStruct(s.shape if pp else (N_DEV,) + s.shape, s.dtype) for s, pp in zip(srcs, per_peer)]


def _direct_semaphores(n):
    return [pltpu.SemaphoreType.DMA((n * (N_DEV - 1),)), pltpu.SemaphoreType.DMA((n * (N_DEV - 1),)),
            pltpu.SemaphoreType.DMA((n,))]


def _direct_copies(src_refs, out_refs, send_sems, recv_sems, local_sems, per_peer):
    x, y, c = lax.axis_index("x"), lax.axis_index("y"), lax.axis_index("c")
    me = 4 * x + 2 * y + c
    local, remote = [], []
    for a in range(len(src_refs)):
        mine = src_refs[a].at[me] if per_peer[a] else src_refs[a]
        local.append(pltpu.make_async_copy(mine, out_refs[a].at[me], local_sems.at[a]))
    for k in range(1, N_DEV):
        kx, ky, kc = (k >> 2) & 1, (k >> 1) & 1, k & 1
        px = 1 - x if kx else x
        py = 1 - y if ky else y
        pc = 1 - c if kc else c
        peer = 4 * px + 2 * py + pc
        for a in range(len(src_refs)):
            sem = a * (N_DEV - 1) + (k - 1)
            remote.append(pltpu.make_async_remote_copy(
                src_ref=src_refs[a].at[peer] if per_peer[a] else src_refs[a], dst_ref=out_refs[a].at[me],
                send_sem=send_sems.at[sem], recv_sem=recv_sems.at[sem],
                device_id=(px, py, pc), device_id_type=pl.DeviceIdType.MESH))
    return local, remote


def _start_all(copies):
    local, remote = copies
    for cp in local + remote:
        cp.start()


def _wait_all(copies):
    local, remote = copies
    for cp in remote:
        cp.wait_send()
    for cp in remote:
        cp.wait_recv()
    for cp in local:
        cp.wait()


N_CHIPS = 4
_HBM = pl.BlockSpec(memory_space=pl.ANY)
_MESH = pl.DeviceIdType.MESH


def _gather_call(name, srcs):
    n = len(srcs)
    per = N_DEV - 1

    def body(*refs):
        src_refs, out_refs = refs[:n], refs[n:2 * n]
        send_sems, recv_sems, local_sems = refs[2 * n:]
        x, y, c = lax.axis_index("x"), lax.axis_index("y"), lax.axis_index("c")
        me, sibling = (x, y, c), (x, y, 1 - c)
        x_nbr, y_nbr, diagonal = (1 - x, y), (x, 1 - y), (1 - x, 1 - y)
        held = ((1 - x) * c + x * (1 - c), y * c + (1 - y) * (1 - c))
        onward = (x * c + (1 - x) * (1 - c), (1 - y) * c + y * (1 - c))
        slot = lambda px, py, pc: 4 * px + 2 * py + pc

        def copy(a, k, block, to, from_src=False):
            rows = out_refs[a].at[slot(*block)]
            return pltpu.make_async_remote_copy(
                src_ref=src_refs[a] if from_src else rows, dst_ref=rows,
                send_sem=send_sems.at[a * per + k], recv_sem=recv_sems.at[a * per + k],
                device_id=to, device_id_type=_MESH)

        local = [pltpu.make_async_copy(src_refs[a], out_refs[a].at[slot(*me)], local_sems.at[a]) for a in range(n)]
        started = []

        def start(cp):
            cp.start()
            started.append(cp)

        for cp in local:
            cp.start()
        for a in range(n):
            start(copy(a, 0, me, sibling, True))
            start(copy(a, 1, me, (*x_nbr, c), True))
            start(copy(a, 2, me, (*y_nbr, c), True))
        for a in range(n):
            copy(a, 1, (*x_nbr, c), me).wait_recv()
            copy(a, 2, (*y_nbr, c), me).wait_recv()
            start(copy(a, 3, (*held, c), (*onward, c)))
            start(copy(a, 4, (*x_nbr, c), sibling))
            start(copy(a, 5, (*y_nbr, c), sibling))
        for a in range(n):
            copy(a, 3, (*diagonal, c), me).wait_recv()
            start(copy(a, 6, (*diagonal, c), sibling))
        for a in range(n):
            copy(a, 0, sibling, me).wait_recv()
            for k, chip in ((4, x_nbr), (5, y_nbr), (6, diagonal)):
                copy(a, k, (*chip, 1 - c), me).wait_recv()
        for cp in started:
            cp.wait_send()
        for cp in local:
            cp.wait()

    return pl.pallas_call(
        body, name=name,
        in_specs=[_HBM] * n, out_specs=[_HBM] * n,
        out_shape=[jax.ShapeDtypeStruct((N_DEV,) + s.shape, s.dtype) for s in srcs],
        scratch_shapes=[pltpu.SemaphoreType.DMA((n * per,)), pltpu.SemaphoreType.DMA((n * per,)),
                        pltpu.SemaphoreType.DMA((n,))],
    )(*srcs)


def _chip_reduce_copies(src_ref, out_ref, stage_ref, own_buf, got_buf, send_sems, recv_sems, local_sems):
    x, y, c = lax.axis_index("x"), lax.axis_index("y"), lax.axis_index("c")
    via = (c * (1 - x) + (1 - c) * x, c * y + (1 - c) * (1 - y))
    other = (c * x + (1 - c) * (1 - x), c * (1 - y) + (1 - c) * y)
    block = lambda chip: src_ref.at[2 * chip[0] + chip[1]]
    remote = lambda k, src, dst, chip: pltpu.make_async_remote_copy(
        src_ref=src, dst_ref=dst, send_sem=send_sems.at[k], recv_sem=recv_sems.at[k],
        device_id=(chip[0], chip[1], c), device_id_type=_MESH)
    local = lambda k, src, dst: pltpu.make_async_copy(src, dst, local_sems.at[k])
    return dict(keep=local(0, block((x, y)), out_ref.at[0]),
                to_reduce=remote(0, block((1 - x, 1 - y)), stage_ref, via),
                direct=remote(1, block(via), out_ref.at[1], via),
                load_own=local(1, block(other), own_buf),
                load_got=local(2, stage_ref, got_buf),
                reduced=remote(2, own_buf, out_ref.at[2], other))


def _adam_call(name, parts, w, m, v, tr):
    rows, cols = w.shape
    n_slots = parts.shape[0]

    def body(p_ref, w_ref, m_ref, v_ref, g_ref, d_ref, nm_ref, nv_ref):
        g = p_ref[0].astype(F32)
        for s in range(1, n_slots):
            g = g + p_ref[s].astype(F32)
        m_new = ADAM_B1 * m_ref[...] + (1.0 - ADAM_B1) * g
        v_new = ADAM_B2 * v_ref[...] + (1.0 - ADAM_B2) * (g * g)
        m_hat = m_new / (1.0 - ADAM_B1 ** ADAM_STEP)
        v_hat = v_new / (1.0 - ADAM_B2 ** ADAM_STEP)
        g_ref[...] = g
        d_ref[...] = -ADAM_LR * (m_hat / (jnp.sqrt(v_hat) + ADAM_EPS) + ADAM_WD * w_ref[...])
        nm_ref[...] = m_new
        nv_ref[...] = v_new

    blk = pl.BlockSpec((tr, cols), lambda i: (i, 0))
    return pl.pallas_call(
        body, name=name,
        grid=(rows // tr,),
        in_specs=[pl.BlockSpec((n_slots, tr, cols), lambda i: (0, i, 0)), blk, blk, blk],
        out_specs=[blk] * 4,
        out_shape=[jax.ShapeDtypeStruct((rows, cols), F32)] * 4,
        compiler_params=_params("arbitrary"),
    )(parts, w, m, v)


def _columns_to_rows_call(name, w_t, rows, dtype):
    row_tiles = rows // LANES
    cols = w_t.shape[0] // row_tiles
    whole = cols // LANES * LANES

    def body(w_ref, out_ref):
        diagonal = (lax.broadcasted_iota(jnp.int32, (LANES, LANES), 0)
                    == lax.broadcasted_iota(jnp.int32, (LANES, LANES), 1))
        for a in range(row_tiles):
            out_ref[a * LANES:(a + 1) * LANES, :whole] = (
                w_ref[pl.ds(a, whole, stride=row_tiles), :].T.astype(dtype))
            for c in range(whole, cols):
                column = w_ref[pl.ds(c * row_tiles + a, 1), :]
                upright = jnp.sum(jnp.where(diagonal, column, 0.0), axis=1, keepdims=True)
                out_ref[a * LANES:(a + 1) * LANES, c:c + 1] = upright.astype(dtype)

    vm = pl.BlockSpec(memory_space=pltpu.VMEM)
    return pl.pallas_call(
        body, name=name,
        in_specs=[vm], out_specs=vm,
        out_shape=jax.ShapeDtypeStruct((rows, cols), dtype),
        compiler_params=pltpu.CompilerParams(vmem_limit_bytes=VMEM_LIMIT_BYTES),
    )(w_t)


def _adam_columns_call(name, parts, w_t, m_t, v_t):
    n_slots, rows, cols = parts.shape
    row_tiles = rows // LANES
    cols_pad = -(-cols // LANES) * LANES

    def body(p_ref, w_ref, m_ref, v_ref, *out_refs):
        for a in range(row_tiles):
            g = p_ref[0, a * LANES:(a + 1) * LANES, :].astype(F32)
            for s in range(1, n_slots):
                g = g + p_ref[s, a * LANES:(a + 1) * LANES, :].astype(F32)
            g = jnp.concatenate([g, jnp.zeros((LANES, cols_pad - cols), F32)], axis=1).T[:cols]
            column_rows = pl.ds(a, cols, stride=row_tiles)
            results = (g,) + _adamw(g, w_ref[column_rows, :], m_ref[column_rows, :], v_ref[column_rows, :])
            for out_ref, val in zip(out_refs, results):
                out_ref[column_rows, :] = val

    vm = pl.BlockSpec(memory_space=pltpu.VMEM)
    return pl.pallas_call(
        body, name=name,
        in_specs=[vm] * 4, out_specs=[vm] * 4,
        out_shape=[jax.ShapeDtypeStruct(w_t.shape, F32)] * 4,
        compiler_params=pltpu.CompilerParams(vmem_limit_bytes=VMEM_LIMIT_BYTES),
    )(parts, w_t, m_t, v_t)


N_PIECES = 8
PIECE = 512
SHARD_COLS = 513
SHARD_PAD = 640
RELAYOUT_ROWS = 256


def _from_shards_call(shards):
    _, d, _ = shards.shape
    tr = RELAYOUT_ROWS

    def body(p_ref, m_ref, s_ref):
        lane = lax.broadcasted_iota(jnp.int32, (tr, SHARD_PAD), 1)
        pad = jnp.zeros((tr, SHARD_PAD - SHARD_COLS), p_ref.dtype)
        sh = [jnp.concatenate([p_ref[s], pad], axis=1) for s in range(N_DEV)]
        for p in range(N_PIECES):
            y = sh[p] if p == 0 else pltpu.roll(sh[p], p, axis=1)
            if p > 0:
                y = jnp.where(lane < p, pltpu.roll(sh[p - 1], SHARD_PAD - (SHARD_COLS - p), axis=1), y)
            m_ref[:, p * PIECE:(p + 1) * PIECE] = y[:, :PIECE].astype(m_ref.dtype)
        first_gate = N_PIECES * PIECE - (N_DEV - 1) * SHARD_COLS
        s_ref[...] = pltpu.roll(sh[N_DEV - 1], SHARD_PAD - first_gate, axis=1)[:, :LANES].astype(s_ref.dtype)

    return pl.pallas_call(
        body, name="w_in_from_shards",
        grid=(d // tr,),
        in_specs=[pl.BlockSpec((N_DEV, tr, SHARD_COLS), lambda i: (0, i, 0))],
        out_specs=[pl.BlockSpec((tr, N_PIECES * PIECE), lambda i: (i, 0)), pl.BlockSpec((tr, LANES), lambda i: (i, 0))],
        out_shape=[jax.ShapeDtypeStruct((d, N_PIECES * PIECE), shards.dtype),
                   jax.ShapeDtypeStruct((d, LANES), shards.dtype)],
        compiler_params=_params("arbitrary"),
    )(shards)


def _adamw(g, w, m, v):
    m_new = ADAM_B1 * m + (1.0 - ADAM_B1) * g
    v_new = ADAM_B2 * v + (1.0 - ADAM_B2) * (g * g)
    m_hat = m_new / (1.0 - ADAM_B1 ** ADAM_STEP)
    v_hat = v_new / (1.0 - ADAM_B2 ** ADAM_STEP)
    return -ADAM_LR * (m_hat / (jnp.sqrt(v_hat) + ADAM_EPS) + ADAM_WD * w), m_new, v_new


def _adam_small_call(parts, ws, ms, vs):
    n = len(ws)
    n_slots = parts.shape[0]

    def body(*refs):
        p_ref = refs[0]
        w_refs, m_refs, v_refs = refs[1:1 + n], refs[1 + n:1 + 2 * n], refs[1 + 2 * n:1 + 3 * n]
        loss_ref = refs[1 + 3 * n]
        outs = refs[2 + 3 * n:]
        g_all = p_ref[0]
        for s in range(1, n_slots):
            g_all = g_all + p_ref[s]
        loss_ref[...] = g_all[n:n + 1, 0:1]
        for r in range(n):
            size = w_refs[r].shape[1]
            g = g_all[r:r + 1, :size]
            delta, m_new, v_new = _adamw(g, w_refs[r][...], m_refs[r][...], v_refs[r][...])
            for kind, val in enumerate((g, delta, m_new, v_new)):
                outs[kind * n + r][...] = val

    vm = pl.BlockSpec(memory_space=pltpu.VMEM)
    shapes = [jax.ShapeDtypeStruct(w.shape, F32) for w in ws]
    return pl.pallas_call(
        body, name="adam_small",
        in_specs=[vm] * (1 + 3 * n), out_specs=[vm] * (1 + 4 * n),
        out_shape=[jax.ShapeDtypeStruct((1, 1), F32)] + shapes * 4,
    )(parts, *ws, *ms, *vs)


_SMALL_ROWS = ("norm1_w", "final_norm_w", "sb_norm_w", "gdn_norm_w", "gdn_A_log", "gdn_dt_bias", "loss")


def _pack_small(vals, width):
    rows = [jnp.pad(a.reshape(1, -1).astype(F32), ((0, 0), (0, width - a.size))) for a in vals]
    rows += [jnp.zeros((1, width), F32)] * (8 - len(rows))
    return jnp.concatenate(rows, axis=0)


def _device_step(x2d, tgt, w_main, w_small, w_out_full, conv_full, norm1_w, sb_norm_w, gdn_A_log, gdn_dt_bias,
                 gdn_norm_w, final_norm_w, distributed=False):
    t_len, d = x2d.shape
    n_chunks = t_len // CHUNK
    w_main, w_small, w_out_full = (a.astype(MXU_DTYPE) for a in (w_main, w_small, w_out_full))
    w_small_t = w_small[:, :2 * GDN_HEADS].T

    pad_lanes = lambda a, lo: jnp.pad(a.reshape(1, -1), ((0, 0), (lo, LANES - lo - a.size)))
    alog_l, dtb_l = pad_lanes(gdn_A_log, GDN_HEADS), pad_lanes(gdn_dt_bias, GDN_HEADS)
    alog_c, dtb_c = alog_l[:, :8].T, dtb_l[:, :8].T
    sbw = jnp.tile(sb_norm_w, (1, 512 // SB_HEAD_DIM))
    gdw = jnp.tile(gdn_norm_w, (1, 512 // GDN_HEAD_DIM))
    fw = final_norm_w.reshape(1, d)

    if distributed:
        proj_cols, proj_gates, ps, pst, h_t, r1, w_out_g, conv_g = _inproj_call(
            x2d, norm1_w, w_main, w_small, w_small_t, gather=(w_out_full, conv_full))
        w_out_full = w_out_g.reshape(d, d)
        conv_full = conv_g.transpose(1, 0, 2).reshape(CONV_WIDTH, N_DEV * conv_g.shape[2])
    else:
        proj_cols, proj_gates, ps, pst, h_t, r1 = _inproj_call(x2d, norm1_w, w_main, w_small, w_small_t)
    o_sb, sp_total, sb_blocks_run = _sb_fwd_call(proj_cols, t_len)
    gact = _gdn_prep_call(proj_cols, conv_full, t_len, after=sp_total)
    beta_l, gcol_l, grow = _gdn_gates_call(ps, pst, alog_l, dtb_l, alog_c, dtb_c, t_len)
    gam_r = grow[GDN_HEADS:2 * GDN_HEADS].reshape(GDN_HEADS, n_chunks, 1, CHUNK)
    o_gd, *gdn_saved = _gdn_fwd_call(gact, beta_l, gcol_l, gam_r, t_len)

    (dx2, d_osb, d_ogd, dproj8, loss_p, g_fw, g_sbw, g_gdw, g_wout) = _post_call(
        o_sb, o_gd, proj_gates, x2d, tgt, w_out_full, sbw, gdw, fw)

    dproj8 = _sb_bwd_call(proj_cols, sp_total, sb_blocks_run, d_osb, dproj8, t_len)
    if distributed:
        d_gact3, d_gates, g_wout = _gdn_bwd_call(gact, beta_l, gcol_l, gam_r, gdn_saved, d_ogd, t_len,
                                                 scatter=(g_wout.reshape(N_DEV, d // N_DEV, d),))
    else:
        d_gact3, d_gates = _gdn_bwd_call(gact, beta_l, gcol_l, gam_r, gdn_saved, d_ogd, t_len)
    dproj8, g_conv = _gdn_prep_bwd_call(proj_cols, conv_full, d_gact3, dproj8, t_len)
    dsmall, g_alog, g_dtb = _gdn_gates_bwd_call(ps, alog_l, dtb_l, d_gates, t_len)

    if distributed:
        chip_partials = _gw_in_shards_call(h_t, dproj8, dsmall, WIRE_DTYPE)
        fold = lambda a, group: a.reshape(-1, group).sum(axis=0)
        small_g = _pack_small([jnp.zeros((d,), F32), g_fw, fold(g_sbw, SB_HEAD_DIM), fold(g_gdw, GDN_HEAD_DIM),
                               g_alog[0, GDN_HEADS:2 * GDN_HEADS], g_dtb[0, GDN_HEADS:2 * GDN_HEADS],
                               loss_p[0, :1]], d)
        conv_cols = g_conv.shape[1] // N_DEV
        g_conv_parts = g_conv.reshape(CONV_WIDTH, N_DEV, conv_cols).transpose(1, 0, 2)
        grad_x, _, g_w_in, p_small, p_conv, _ = _dx_call(dproj8, dsmall, w_main, w_small, x2d, r1, dx2, norm1_w,
                                                         chip_scatter=(chip_partials,),
                                                         peer_scatter=(small_g, g_conv_parts))
        return grad_x, g_w_in, g_wout, p_small, p_conv
    else:
        grad_x, g_n1 = _dx_call(dproj8, dsmall, w_main, w_small, x2d, r1, dx2, norm1_w)
        g_w_in = (_gw_in_call(h_t, dproj8), _gw_small_call(h_t, dsmall))
    return (loss_p, grad_x, g_n1, g_w_in, g_sbw, g_conv, g_alog, g_dtb, g_gdw, g_wout, g_fw)


def kernel(x, norm1_w, w_in, sb_norm_w, gdn_conv_w, gdn_A_log, gdn_dt_bias, gdn_norm_w, w_out, final_norm_w, loss_target, m_norm1_w, m_w_in, m_sb_norm_w, m_gdn_conv_w, m_gdn_A_log, m_gdn_dt_bias, m_gdn_norm_w, m_w_out, m_final_norm_w, v_norm1_w, v_w_in, v_sb_norm_w, v_gdn_conv_w, v_gdn_A_log, v_gdn_dt_bias, v_gdn_norm_w, v_w_out, v_final_norm_w):
    d = x.shape[2]
    shard_cols = w_in.shape[2]

    columns = lambda a: a.transpose(2, 0, 1).reshape(shard_cols * d // LANES, LANES)
    from_columns = lambda a: a.reshape(shard_cols, d // LANES, LANES).transpose(1, 2, 0).reshape(1, d, shard_cols)
    (w_in_g,) = _gather_call("gather_weights", [_columns_to_rows_call("w_in_to_wire", columns(w_in), d, WIRE_DTYPE)])
    w_main, w_small = _from_shards_call(w_in_g)

    grad_x, p_w_in, p_wout, p_small, p_conv = _device_step(
        x[0], loss_target[0], w_main, w_small, w_out[0].astype(WIRE_DTYPE), gdn_conv_w[0], norm1_w, sb_norm_w,
        gdn_A_log, gdn_dt_bias, gdn_norm_w, final_norm_w, distributed=True)

    r_w_in = [from_columns(a) for a in _adam_columns_call("adam_w_in", p_w_in, columns(w_in), columns(m_w_in),
                                                          columns(v_w_in))]
    r_wout = _adam_call("adam_w_out", p_wout, w_out[0], m_w_out[0], v_w_out[0], d // N_DEV)
    r_conv = _adam_call("adam_conv", p_conv, gdn_conv_w[0], m_gdn_conv_w[0], v_gdn_conv_w[0], CONV_WIDTH)

    row = lambda a: a.reshape(1, -1)
    n_small = len(_SMALL_ROWS) - 1
    r_small = _adam_small_call(
        p_small,
        [norm1_w, row(final_norm_w), sb_norm_w, gdn_norm_w, gdn_A_log, gdn_dt_bias],
        [m_norm1_w, row(m_final_norm_w), m_sb_norm_w, m_gdn_norm_w, m_gdn_A_log, m_gdn_dt_bias],
        [v_norm1_w, row(v_final_norm_w), v_sb_norm_w, v_gdn_norm_w, v_gdn_A_log, v_gdn_dt_bias])

    def small_out(kind, name):
        out = r_small[1 + kind * n_small + _SMALL_ROWS.index(name)]
        return out.reshape(final_norm_w.shape) if name == "final_norm_w" else out

    def outputs(kind):
        return (small_out(kind, "norm1_w"), r_w_in[kind], small_out(kind, "sb_norm_w"), r_conv[kind][None],
                small_out(kind, "gdn_A_log"), small_out(kind, "gdn_dt_bias"), small_out(kind, "gdn_norm_w"),
                r_wout[kind][None], small_out(kind, "final_norm_w"))

    return (r_small[0][0, 0], grad_x[None], *outputs(0), *outputs(1), *outputs(2), *outputs(3))
```

```python
import functools

import jax
import jax.numpy as jnp
from jax import lax
from jax.experimental import pallas as pl
from jax.experimental.pallas import tpu as pltpu

F32 = jnp.float32
MXU_DTYPE = jnp.bfloat16
WIRE_DTYPE = jnp.bfloat16
EXACT = lax.Precision.HIGHEST
EPS = 1e-6
N_DEV = 8
SB_HEAD_DIM = 64
GDN_HEAD_DIM = 128
GDN_HEADS = 4
GDN_CHUNKS_PER_STEP = 4
GDN_BWD_GROUP = 1
CHUNK = 64
CONV_WIDTH = 4
LANES = 128
SB_BLOCK = 128
SB_BQ = 256
VMEM_LIMIT_BYTES = 56 * 1024 * 1024

PIECE_COLS = 512
PROJ_PIECE_KINDS = ("heads", "heads", "heads", "gate", "heads", "heads", "heads", "gate")
SB_FIRST_BLOCK, GDN_FIRST_BLOCK = 0, 12

DPROJ_PIECE_OF_SLOT = (0, 1, 2, 4, 5, 6, 3, 7)
DPROJ_SB_SLOT, DPROJ_GDN_SLOT, DPROJ_GATE_SLOT = 0, 3, 6

ADAM_LR = 0.001
ADAM_B1 = 0.9
ADAM_B2 = 0.999
ADAM_EPS = 1e-08
ADAM_WD = 0.01
ADAM_STEP = 10

_NN = (((1,), (0,)), ((), ()))
_NT = (((1,), (1,)), ((), ()))
_TN = (((0,), (0,)), ((), ()))
_BNN = (((2,), (1,)), ((0,), (0,)))
_BNT = (((2,), (2,)), ((0,), (0,)))
_BTN = (((1,), (1,)), ((0,), (0,)))


def _mx(a, b):
    return jnp.dot(a, b, precision=EXACT, preferred_element_type=F32)


def _split(x):
    hi = x.astype(MXU_DTYPE)
    return hi, (x - hi.astype(F32)).astype(MXU_DTYPE)


def _m3_general(a, b, dims, right_low=True):
    ah, al = _split(a)
    bh, bl = _split(b)
    dot = lambda x, y: lax.dot_general(x, y, dims, preferred_element_type=F32)
    (contract, _), (batch, _) = dims
    free = [ax for ax in range(a.ndim) if ax not in contract and ax not in batch][0]
    m = a.shape[free]
    both = dot(jnp.concatenate([ah, al], axis=free), bh)
    out_axis = len(batch)
    hi_part = lax.slice_in_dim(both, 0, m, axis=out_axis)
    lo_part = lax.slice_in_dim(both, m, 2 * m, axis=out_axis)
    return hi_part + (dot(ah, bl) + lo_part if right_low else lo_part)


def _times_exact(a, b_exact, dims):
    ah, al = _split(a)
    (contract, _), (batch, _) = dims
    free = [ax for ax in range(a.ndim) if ax not in contract and ax not in batch][0]
    m = a.shape[free]
    both = lax.dot_general(jnp.concatenate([ah, al], axis=free), b_exact.astype(MXU_DTYPE), dims,
                           preferred_element_type=F32)
    out_axis = len(batch)
    return lax.slice_in_dim(both, 0, m, axis=out_axis) + lax.slice_in_dim(both, m, 2 * m, axis=out_axis)


def _exact_times(a_exact, b, dims):
    bh, bl = _split(b)
    n = b.shape[-1]
    both = lax.dot_general(a_exact.astype(MXU_DTYPE), jnp.concatenate([bh, bl], axis=-1), dims,
                           preferred_element_type=F32)
    return both[..., :n] + both[..., n:]


def _sigmoid(z):
    return 1.0 / (1.0 + jnp.exp(-z))


def _softplus(z):
    return jnp.maximum(z, 0.0) + jnp.log(1.0 + jnp.exp(-jnp.abs(z)))


def _params(*semantics):
    return pltpu.CompilerParams(dimension_semantics=semantics, vmem_limit_bytes=VMEM_LIMIT_BYTES)


def _inproj_call(x, norm_w, w_main, w_small, w_small_t, gather=(), tm=256):
    t_len, d = x.shape
    n = w_main.shape[1]
    ns = w_small.shape[1]
    nst = w_small_t.shape[0]
    ng = len(gather)
    steps = t_len // tm

    blocks_per_piece = PIECE_COLS // LANES
    n_gate_cols = PIECE_COLS * PROJ_PIECE_KINDS.count("gate")
    n_col_blocks = blocks_per_piece * PROJ_PIECE_KINDS.count("heads")

    def body(*refs):
        x_ref, nw_ref, wm_ref, ws_ref, wst_ref = refs[:5]
        cols_ref, pz_ref, ps_ref, pst_ref, ht_ref, r_ref = refs[5 + ng:11 + ng]
        copies = lambda: _direct_copies(refs[5:5 + ng], refs[11 + ng:11 + 2 * ng], *refs[11 + 2 * ng:], (False,) * ng)
        if ng:
            pl.when(pl.program_id(0) == 0)(lambda: _start_all(copies()))
        xv = x_ref[...]
        r = lax.rsqrt(jnp.mean(xv * xv, axis=-1, keepdims=True) + EPS)
        h = xv * r * nw_ref[...]
        hb = h.astype(MXU_DTYPE)
        n_block = n_gate = 0
        for piece, kind in enumerate(PROJ_PIECE_KINDS):
            out = jnp.dot(hb, wm_ref[:, piece * PIECE_COLS:(piece + 1) * PIECE_COLS], preferred_element_type=F32)
            if kind == "gate":
                pz_ref[:, n_gate * PIECE_COLS:(n_gate + 1) * PIECE_COLS] = out
                n_gate += 1
            else:
                for j in range(blocks_per_piece):
                    cols_ref[n_block + j] = out[:, j * LANES:(j + 1) * LANES]
                n_block += blocks_per_piece
        ps_ref[...] = jnp.dot(hb, ws_ref[...], preferred_element_type=F32)
        pst_ref[...] = lax.dot_general(wst_ref[...], hb, _NT, preferred_element_type=F32)
        ht_ref[...] = h.T.astype(MXU_DTYPE)
        r_ref[...] = r
        if ng:
            pl.when(pl.program_id(0) == steps - 1)(lambda: _wait_all(copies()))

    return pl.pallas_call(
        body, name="inproj",
        grid=(steps,),
        in_specs=[pl.BlockSpec((tm, d), lambda i: (i, 0)),
                  pl.BlockSpec((1, d), lambda i: (0, 0)),
                  pl.BlockSpec((d, n), lambda i: (0, 0)),
                  pl.BlockSpec((d, ns), lambda i: (0, 0)),
                  pl.BlockSpec((nst, d), lambda i: (0, 0))] + [_HBM] * ng,
        out_specs=[pl.BlockSpec((n_col_blocks, tm, LANES), lambda i: (0, i, 0)),
                   pl.BlockSpec((tm, n_gate_cols), lambda i: (i, 0)),
                   pl.BlockSpec((tm, ns), lambda i: (i, 0)),
                   pl.BlockSpec((nst, tm), lambda i: (0, i)),
                   pl.BlockSpec((d, tm), lambda i: (0, i)),
                   pl.BlockSpec((tm, 1), lambda i: (i, 0))] + [_HBM] * ng,
        out_shape=[jax.ShapeDtypeStruct((n_col_blocks, t_len, LANES), F32),
                   jax.ShapeDtypeStruct((t_len, n_gate_cols), F32),
                   jax.ShapeDtypeStruct((t_len, ns), F32),
                   jax.ShapeDtypeStruct((nst, t_len), F32),
                   jax.ShapeDtypeStruct((d, t_len), MXU_DTYPE),
                   jax.ShapeDtypeStruct((t_len, 1), F32)] + _direct_out_shapes(gather, (False,) * ng),
        scratch_shapes=_direct_semaphores(ng) if ng else [],
        compiler_params=_params("arbitrary"),
    )(x, norm_w, w_main, w_small, w_small_t, *gather)


def _running_sum_mm(x, tri):
    hi = x.astype(MXU_DTYPE)
    lo = (x - hi.astype(F32)).astype(MXU_DTYPE)
    return jnp.dot(hi, tri, preferred_element_type=F32) + jnp.dot(lo, tri, preferred_element_type=F32)


def _col_block(t_len, first):
    return pl.BlockSpec((1, t_len, LANES), lambda p: (first + p, 0, 0))


def _sb_iotas():
    row_i = lax.broadcasted_iota(jnp.int32, (SB_BQ, SB_BLOCK), 0)
    col_i = lax.broadcasted_iota(jnp.int32, (SB_BQ, SB_BLOCK), 1)
    sq_r = lax.broadcasted_iota(jnp.int32, (SB_BLOCK, SB_BLOCK), 0)
    sq_c = lax.broadcasted_iota(jnp.int32, (SB_BLOCK, SB_BLOCK), 1)
    return row_i, col_i, sq_r, sq_c


SB_DIAG_BLOCKS = SB_BQ // SB_BLOCK
SB_EXP_FLOOR = -110.0


def _sb_keys_descending(qi, tile, carry, z_bounds, n_heads, has_free):
    group = SB_DIAG_BLOCKS
    n_free = group * qi
    diag = list(range(group - 1, -1, -1))
    carry = tile([n_free + j for j in diag], [True] * group, carry, [j * SB_BLOCK for j in diag])

    def largest_exponent(c):
        worst = jnp.max(z_bounds[0] - c[1])
        for h in range(1, n_heads):
            worst = jnp.maximum(worst, jnp.max(z_bounds[h] - c[1 + h]))
        return worst

    always = group if has_free else 0

    def cond(state):
        return (state[0] < n_free) & ((state[1] > SB_EXP_FLOOR) | (state[0] < always))

    def body(state):
        first = n_free - 1 - state[0]
        c = tile([first - j for j in range(group)], [False] * group, state[2:])
        return (state[0] + group, largest_exponent(c), *c)

    out = lax.while_loop(cond, body, (jnp.int32(0), largest_exponent(carry), *carry))
    return out[2:], out[0]


def _sb_keys_ascending(qi, n_run, tile, carry, has_free):
    group = SB_DIAG_BLOCKS
    n_free = group * qi
    diag = list(range(group))
    kjs, los, masked = [n_free + j for j in diag], [j * SB_BLOCK for j in diag], [True] * group
    if has_free:
        early = lambda s: [n_free - n_run + group * s + j for j in range(group)]
        carry = lax.fori_loop(0, n_run // group - 1, lambda s, c: tile(early(s), [False] * group, c), carry)
        kjs, los, masked = [n_free - group + j for j in range(group)] + kjs, [0] * group + los, [False] * group + masked
    return tile(kjs, masked, carry, los)


def _sb_fwd_call(cols, t_len):
    nq = t_len // SB_BQ
    scale = float(SB_HEAD_DIM) ** -0.5
    n_pairs = 512 // LANES
    per_pair = LANES // SB_HEAD_DIM

    def body(q_blk, k_blk, v_blk, o_blk, st_ref, nrun_ref):
        q_ref, k_ref, v_ref, o_ref = q_blk.at[0], k_blk.at[0], v_blk.at[0], o_blk.at[0]
        lane = lax.broadcasted_iota(jnp.int32, (1, LANES), 1)
        row_i, col_i, sq_r, sq_c = _sb_iotas()
        ge = (sq_r >= sq_c).astype(MXU_DTYPE)
        hms = [((lane // SB_HEAD_DIM) == hh).astype(F32) for hh in range(per_pair)]
        k_sq = k_ref[...] * k_ref[...]
        k_norms = [jnp.sqrt(jnp.max(jnp.sum(k_sq * hm, axis=-1, keepdims=True))) * (1.02 * scale) for hm in hms]

        def q_block(qi, has_free):
            r0 = qi * SB_BQ if isinstance(qi, int) else pl.multiple_of(qi * SB_BQ, SB_BQ)
            rows = pl.ds(r0, SB_BQ)
            q_all = q_ref[rows, :]
            qms = [(q_all * (hm * scale)).astype(MXU_DTYPE) for hm in hms]
            z_bounds = [jnp.sqrt(jnp.sum(q_all * q_all * hm, axis=-1, keepdims=True)) * kn
                        for hm, kn in zip(hms, k_norms)]

            def tile(kjs, masked, kc, los=None):
                heads = range(per_pair)
                los = los or [0] * len(kjs)
                pairs = [(t, h) for t in range(len(kjs)) for h in heads]
                add_rows = lambda full, lo, part: full + part if lo == 0 else jnp.concatenate(
                    [full[:lo], full[lo:] + part], axis=0)
                acc, cs = kc[0], list(kc[1:])
                s0s = [kj * SB_BLOCK if isinstance(kj, int) else pl.multiple_of(kj * SB_BLOCK, SB_BLOCK) for kj in kjs]
                kbs = [k_ref[pl.ds(s0, SB_BLOCK), :].astype(MXU_DTYPE) for s0 in s0s]
                v_alls = [v_ref[pl.ds(s0, SB_BLOCK), :] for s0 in s0s]
                vms = {(t, h): (v_alls[t] * hms[h]).astype(MXU_DTYPE) for t, h in pairs}
                zs = {(t, h): lax.dot_general(qms[h][los[t]:], kbs[t], _NT, preferred_element_type=F32)
                      for t, h in pairs}
                masks = [(col_i[lo:] + s0) < (row_i[lo:] + r0) if m else None for m, lo, s0 in zip(masked, los, s0s)]
                keep = lambda t, a: a if masks[t] is None else jnp.where(masks[t], a, 0.0)
                sps = {(t, h): keep(t, _softplus(zs[t, h])) for t, h in pairs}
                sums = {p: _running_sum_mm(sps[p], ge) for p in pairs}
                mass = {}
                for t, h in pairs:
                    mass[t, h] = cs[h] if t == 0 else add_rows(
                        mass[t - 1, h], los[t - 1], jnp.sum(sps[t - 1, h], axis=-1, keepdims=True))
                ws = {(t, h): keep(t, jnp.exp(zs[t, h] - (sums[t, h] + mass[t, h][los[t]:]))) for t, h in pairs}
                for t, h in pairs:
                    acc = add_rows(acc, los[t], jnp.dot(ws[t, h].astype(MXU_DTYPE), vms[t, h],
                                                        preferred_element_type=F32))
                last = len(kjs) - 1
                cs = [add_rows(mass[last, h], los[last], jnp.sum(sps[last, h], axis=-1, keepdims=True)) for h in heads]
                return (acc, *cs)

            zero_col = jnp.zeros((SB_BQ, 1), F32)
            out, n_run = _sb_keys_descending(
                qi, tile, (jnp.zeros((SB_BQ, LANES), F32),) + (zero_col,) * per_pair, z_bounds, per_pair, has_free)
            o_ref[rows, :] = out[0]
            masses = jnp.zeros((SB_BQ, LANES), F32)
            for hh in range(per_pair):
                masses = jnp.where(lane == hh, out[1 + hh], masses)
            st_ref[rows, :] = masses
            nrun_ref[pl.program_id(0), qi] = n_run

        q_block(0, False)
        lax.fori_loop(1, nq, lambda qi, carry: (q_block(qi, True), carry)[1], 0)

    return pl.pallas_call(
        body, name="sb_fwd",
        grid=(n_pairs,),
        in_specs=[_col_block(t_len, SB_FIRST_BLOCK), _col_block(t_len, SB_FIRST_BLOCK + n_pairs),
                  _col_block(t_len, SB_FIRST_BLOCK + 2 * n_pairs)],
        out_specs=[_col_block(t_len, 0),
                   pl.BlockSpec((t_len, LANES), lambda p: (0, p)),
                   pl.BlockSpec(memory_space=pltpu.SMEM)],
        out_shape=[jax.ShapeDtypeStruct((n_pairs, t_len, LANES), F32),
                   jax.ShapeDtypeStruct((t_len, n_pairs * LANES), F32),
                   jax.ShapeDtypeStruct((n_pairs, nq), jnp.int32)],
        compiler_params=_params("arbitrary"),
    )(cols, cols, cols)


def _sb_bwd_call(cols, sp_total, n_run_all, d_o, dproj, t_len):
    nq = t_len // SB_BQ
    scale = float(SB_HEAD_DIM) ** -0.5
    n_pairs = 512 // LANES
    per_pair = LANES // SB_HEAD_DIM

    def body(q_blk, k_blk, v_blk, st_ref, nrun_ref, do_blk, dproj_in_ref, d_ref):
        q_ref, k_ref, v_ref, do_ref = q_blk.at[0], k_blk.at[0], v_blk.at[0], do_blk.at[0]
        lane = lax.broadcasted_iota(jnp.int32, (1, LANES), 1)
        row_i, col_i, sq_r, sq_c = _sb_iotas()
        lt = (sq_r < sq_c).astype(MXU_DTYPE)
        le = (sq_r <= sq_c).astype(MXU_DTYPE)
        hms = [((lane // SB_HEAD_DIM) == hh).astype(F32) for hh in range(per_pair)]
        d_ref[1] = jnp.zeros((t_len, LANES), F32)
        d_ref[2] = jnp.zeros((t_len, LANES), F32)

        def q_block(qi, has_free):
            r0 = qi * SB_BQ if isinstance(qi, int) else pl.multiple_of(qi * SB_BQ, SB_BQ)
            rows = pl.ds(r0, SB_BQ)
            q_all, do_all = q_ref[rows, :], do_ref[rows, :]
            qms = [(q_all * (hm * scale)).astype(MXU_DTYPE) for hm in hms]
            doms = [(do_all * hm).astype(MXU_DTYPE) for hm in hms]
            masses = st_ref[rows, :]
            totals = [jnp.sum(jnp.where(lane == hh, masses, 0.0), axis=-1, keepdims=True) for hh in range(per_pair)]

            def tile(kjs, masked, kc, los=None):
                heads = range(per_pair)
                los = los or [0] * len(kjs)
                pairs = [(t, h) for t in range(len(kjs)) for h in heads]
                add_rows = lambda full, lo, part: full + part if lo == 0 else jnp.concatenate(
                    [full[:lo], full[lo:] + part], axis=0)
                rsum = lambda a: jnp.sum(a, axis=-1, keepdims=True)
                dq, cls, gls = kc[0], list(kc[1:1 + per_pair]), list(kc[1 + per_pair:])
                s0s = [kj * SB_BLOCK if isinstance(kj, int) else pl.multiple_of(kj * SB_BLOCK, SB_BLOCK) for kj in kjs]
                k_alls = [k_ref[pl.ds(s0, SB_BLOCK), :] for s0 in s0s]
                v_alls = [v_ref[pl.ds(s0, SB_BLOCK), :] for s0 in s0s]
                kbs = [k_all.astype(MXU_DTYPE) for k_all in k_alls]
                vms = {(t, h): (v_alls[t] * hms[h]).astype(MXU_DTYPE) for t, h in pairs}
                kms = {(t, h): (k_alls[t] * (hms[h] * scale)).astype(MXU_DTYPE) for t, h in pairs}
                q_live = {(t, h): qms[h][los[t]:] for t, h in pairs}
                do_live = {(t, h): doms[h][los[t]:] for t, h in pairs}
                zs = {p: lax.dot_general(q_live[p], kbs[p[0]], _NT, preferred_element_type=F32) for p in pairs}
                das = {p: lax.dot_general(do_live[p], vms[p], _NT, preferred_element_type=F32) for p in pairs}
                masks = [(col_i[lo:] + s0) < (row_i[lo:] + r0) if m else None for m, lo, s0 in zip(masked, los, s0s)]
                keep = lambda t, a: a if masks[t] is None else jnp.where(masks[t], a, 0.0)
                sp_alls = {p: _softplus(zs[p]) for p in pairs}
                sps = {(t, h): keep(t, sp_alls[t, h]) for t, h in pairs}
                lefts = {p: _running_sum_mm(sps[p], lt) for p in pairs}
                cl = {}
                for t, h in pairs:
                    cl[t, h] = cls[h] if t == 0 else add_rows(cl[t - 1, h], los[t - 1], rsum(sps[t - 1, h]))
                ws = {(t, h): keep(t, jnp.exp(zs[t, h] - ((totals[h] - cl[t, h])[los[t]:] - lefts[t, h])))
                      for t, h in pairs}
                gs = {p: das[p] * ws[p] for p in pairs}
                g_sums = {p: _running_sum_mm(gs[p], le) for p in pairs}
                gl = {}
                for t, h in pairs:
                    gl[t, h] = gls[h] if t == 0 else add_rows(gl[t - 1, h], los[t - 1], rsum(gs[t - 1, h]))
                dzs = {(t, h): keep(t, gs[t, h] - jnp.exp(zs[t, h] - sp_alls[t, h]) * (gl[t, h][los[t]:] + g_sums[t, h])
                               ).astype(MXU_DTYPE) for t, h in pairs}
                for t in range(len(kjs)):
                    dk_t = jnp.zeros((SB_BLOCK, LANES), F32)
                    dv_t = jnp.zeros((SB_BLOCK, LANES), F32)
                    for h in heads:
                        dq = add_rows(dq, los[t], jnp.dot(dzs[t, h], kms[t, h], preferred_element_type=F32))
                        dk_t = dk_t + lax.dot_general(dzs[t, h], q_live[t, h], _TN, preferred_element_type=F32)
                        dv_t = dv_t + lax.dot_general(ws[t, h].astype(MXU_DTYPE), do_live[t, h], _TN,
                                                      preferred_element_type=F32)
                    d_ref[1, pl.ds(s0s[t], SB_BLOCK), :] += dk_t
                    d_ref[2, pl.ds(s0s[t], SB_BLOCK), :] += dv_t
                last = len(kjs) - 1
                cls = [add_rows(cl[last, h], los[last], rsum(sps[last, h])) for h in heads]
                gls = [add_rows(gl[last, h], los[last], rsum(gs[last, h])) for h in heads]
                return (dq, *cls, *gls)

            zero_col = jnp.zeros((SB_BQ, 1), F32)
            out = _sb_keys_ascending(qi, nrun_ref[pl.program_id(0), qi], tile,
                                     (jnp.zeros((SB_BQ, LANES), F32),) + (zero_col,) * (2 * per_pair), has_free)
            d_ref[0, rows, :] = out[0]

        q_block(0, False)
        lax.fori_loop(1, nq, lambda qi, carry: (q_block(qi, True), carry)[1], 0)

    return pl.pallas_call(
        body, name="sb_bwd",
        grid=(n_pairs,),
        in_specs=[_col_block(t_len, SB_FIRST_BLOCK), _col_block(t_len, SB_FIRST_BLOCK + n_pairs),
                  _col_block(t_len, SB_FIRST_BLOCK + 2 * n_pairs),
                  pl.BlockSpec((t_len, LANES), lambda p: (0, p)),
                  pl.BlockSpec(memory_space=pltpu.SMEM), _col_block(t_len, 0), _HBM],
        out_specs=pl.BlockSpec((3, t_len, LANES), lambda p: (DPROJ_SB_SLOT // 3, 0, p)),
        out_shape=jax.ShapeDtypeStruct(dproj.shape, dproj.dtype),
        input_output_aliases={6: 0},
        compiler_params=_params("arbitrary"),
    )(cols, cols, cols, sp_total, n_run_all, d_o, dproj)


def _conv_taps(xin, rows, t_len):
    taps = []
    for i in range(CONV_WIDTH):
        shift = CONV_WIDTH - 1 - i
        if shift == 0:
            taps.append(xin)
        else:
            taps.append(jnp.where(rows >= shift, pltpu.roll(xin, shift, axis=0), 0.0))
    return taps


def _gdn_prep_body_common(x_ref, w_ref, t_len):
    j = pl.program_id(0)
    xin = x_ref[...]
    rows = lax.broadcasted_iota(jnp.int32, (t_len, LANES), 0)
    taps = _conv_taps(xin, rows, t_len)
    pre = taps[0] * w_ref[0:1, :]
    for i in range(1, CONV_WIDTH):
        pre = pre + taps[i] * w_ref[i:i + 1, :]
    sg = _sigmoid(pre)
    act = pre * sg
    is_qk = j < 2 * GDN_HEADS
    nrm = jnp.where(is_qk, lax.rsqrt(jnp.sum(act * act, axis=-1, keepdims=True) + EPS), 1.0)
    sc = jnp.where(j < GDN_HEADS, float(GDN_HEAD_DIM) ** -0.5, 1.0)
    return j, rows, taps, pre, sg, act, is_qk, nrm, sc


def _gdn_prep_call(cols, conv_w, t_len, after):
    def body(x_blk, w_ref, after_ref, out_ref):
        _, _, _, _, _, act, _, nrm, sc = _gdn_prep_body_common(x_blk.at[0], w_ref, t_len)
        out_ref[...] = act * nrm * sc

    return pl.pallas_call(
        body, name="gdn_prep",
        grid=(3 * GDN_HEADS,),
        in_specs=[_col_block(t_len, GDN_FIRST_BLOCK),
                  pl.BlockSpec((CONV_WIDTH, LANES), lambda j: (0, j)),
                  pl.BlockSpec(memory_space=pl.ANY)],
        out_specs=pl.BlockSpec((t_len, LANES), lambda j: (0, j)),
        out_shape=jax.ShapeDtypeStruct((t_len, 3 * 512), F32),
        compiler_params=_params("arbitrary"),
    )(cols, conv_w, after)


def _gdn_prep_bwd_call(cols, conv_w, d_act3, dproj, t_len):
    def body(x_blk, w_ref, d_ref, dproj_in_ref, dx_ref, dw_ref):
        _, rows, taps, pre, sg, act, is_qk, nrm, sc = _gdn_prep_body_common(x_blk.at[0], w_ref, t_len)
        d_out = d_ref[0]
        dn = d_out * sc
        d_norm = nrm * dn - act * (nrm * nrm * nrm) * jnp.sum(dn * act, axis=-1, keepdims=True)
        d_act = jnp.where(is_qk, d_norm, d_out)
        d_pre = d_act * sg * (1.0 + pre * (1.0 - sg))
        dx = d_pre * w_ref[CONV_WIDTH - 1:CONV_WIDTH, :]
        dw_ref[CONV_WIDTH - 1:CONV_WIDTH, :] = jnp.sum(d_pre * taps[CONV_WIDTH - 1], axis=0, keepdims=True)
        for i in range(CONV_WIDTH - 1):
            shift = CONV_WIDTH - 1 - i
            up = jnp.where(rows < t_len - shift, pltpu.roll(d_pre, t_len - shift, axis=0), 0.0)
            dx = dx + up * w_ref[i:i + 1, :]
            dw_ref[i:i + 1, :] = jnp.sum(d_pre * taps[i], axis=0, keepdims=True)
        dx_ref[0] = dx

    return pl.pallas_call(
        body, name="gdn_prep_bwd",
        grid=(3 * GDN_HEADS,),
        in_specs=[_col_block(t_len, GDN_FIRST_BLOCK),
                  pl.BlockSpec((CONV_WIDTH, LANES), lambda j: (0, j)),
                  pl.BlockSpec((1, t_len, LANES), lambda j: (j // GDN_HEADS, 0, j % GDN_HEADS)), _HBM],
        out_specs=[pl.BlockSpec((1, t_len, LANES), lambda j: (DPROJ_GDN_SLOT + j // GDN_HEADS, 0, j % GDN_HEADS)),
                   pl.BlockSpec((CONV_WIDTH, LANES), lambda j: (0, j))],
        out_shape=[jax.ShapeDtypeStruct(dproj.shape, dproj.dtype),
                   jax.ShapeDtypeStruct((CONV_WIDTH, 3 * 512), F32)],
        input_output_aliases={3: 0},
        compiler_params=_params("arbitrary"),
    )(cols, conv_w, d_act3, dproj)


def _chunk_cumsum_matrix():
    r = lax.broadcasted_iota(jnp.int32, (LANES, LANES), 0)
    c = lax.broadcasted_iota(jnp.int32, (LANES, LANES), 1)
    return ((r <= c) & ((r // CHUNK) == (c // CHUNK))).astype(F32)


def _gdn_gates_call(ps, pst, alog_l, dtb_l, alog_c, dtb_c, t_len):
    def body(ps_ref, pst_ref, al_ref, dl_ref, ac_ref, dc_ref, beta_ref, gcol_ref, grow_ref):
        upper = _chunk_cumsum_matrix()
        lower = upper.T
        psv = ps_ref[...]
        beta_ref[...] = _sigmoid(psv)
        g_l = -jnp.exp(al_ref[...]) * _softplus(psv + dl_ref[...])
        g_r = -jnp.exp(ac_ref[...]) * _softplus(pst_ref[...] + dc_ref[...])
        for w in range(t_len // LANES):
            sl = slice(w * LANES, (w + 1) * LANES)
            gcol_ref[sl, :] = _mx(lower, g_l[sl, :])
            grow_ref[:, sl] = _mx(g_r[:, sl], upper)

    vm = pl.BlockSpec(memory_space=pltpu.VMEM)
    return pl.pallas_call(
        body, name="gdn_gates",
        in_specs=[vm] * 6, out_specs=[vm] * 3,
        out_shape=[jax.ShapeDtypeStruct((t_len, LANES), F32),
                   jax.ShapeDtypeStruct((t_len, LANES), F32),
                   jax.ShapeDtypeStruct((8, t_len), F32)],
        compiler_params=pltpu.CompilerParams(vmem_limit_bytes=VMEM_LIMIT_BYTES),
    )(ps, pst, alog_l, dtb_l, alog_c, dtb_c)


def _gdn_gates_bwd_call(ps, alog_l, dtb_l, d_l, t_len):
    def body(ps_ref, al_ref, dl_ref, d_ref, dps_ref, gal_ref, gdt_ref):
        lane = lax.broadcasted_iota(jnp.int32, (1, LANES), 1)
        psv = ps_ref[...]
        dv = d_ref[...]
        beta = _sigmoid(psv)
        ea = jnp.exp(al_ref[...])
        arg = psv + dl_ref[...]
        g = -ea * _softplus(arg)
        d_a = dv * (-ea) * _sigmoid(arg)
        is_a = (lane >= GDN_HEADS) & (lane < 2 * GDN_HEADS)
        dps_ref[...] = jnp.where(lane < GDN_HEADS, dv * beta * (1.0 - beta), jnp.where(is_a, d_a, 0.0))
        gdt_ref[...] = jnp.where(is_a, jnp.sum(d_a, axis=0, keepdims=True), 0.0)
        gal_ref[...] = jnp.where(is_a, jnp.sum(dv * g, axis=0, keepdims=True), 0.0)

    vm = pl.BlockSpec(memory_space=pltpu.VMEM)
    return pl.pallas_call(
        body, name="gdn_gates_bwd",
        in_specs=[vm] * 4, out_specs=[vm] * 3,
        out_shape=[jax.ShapeDtypeStruct((t_len, LANES), F32),
                   jax.ShapeDtypeStruct((1, LANES), F32),
                   jax.ShapeDtypeStruct((1, LANES), F32)],
        compiler_params=pltpu.CompilerParams(vmem_limit_bytes=VMEM_LIMIT_BYTES),
    )(ps, alog_l, dtb_l, d_l)


def _bm(a, b, right_low=True):
    return _m3_general(a, b, _BNN, right_low)


def _bm_nt(a, b, right_low=True):
    return _m3_general(a, b, _BNT, right_low)


def _bm_tn(a, b, right_low=True):
    return _m3_general(a, b, _BTN, right_low)


def _heads_of(ref, rows):
    return jnp.stack([ref[rows, h * GDN_HEAD_DIM:(h + 1) * GDN_HEAD_DIM] for h in range(GDN_HEADS)])


def _chunk_terms(q_ref, k_ref, v_ref, b_ref, gc_ref, gr_ref, c, incl, strict, n=1, scores=True):
    r0 = c * CHUNK if isinstance(c, int) else pl.multiple_of(c * CHUNK, CHUNK)
    rows = pl.ds(r0, n * CHUNK)
    per_chunk = lambda x: x.reshape(GDN_HEADS * n, CHUNK, x.shape[-1])
    q, k, v = (per_chunk(_heads_of(ref, rows)) for ref in (q_ref, k_ref, v_ref))
    lane_ids = lax.broadcasted_iota(jnp.int32, (1, LANES), 1)
    pick = lambda slab, first: jnp.stack([jnp.sum(jnp.where(lane_ids == first + h, slab, 0.0), axis=-1, keepdims=True)
                                          for h in range(GDN_HEADS)])
    b = per_chunk(pick(b_ref[rows, :], 0))
    gc = per_chunk(pick(gc_ref[rows, :], GDN_HEADS))
    gr = gr_ref[:, c] if n == 1 else gr_ref[:, c:c + n].reshape(GDN_HEADS * n, 1, CHUNK)
    dm = jnp.where(incl, jnp.exp(jnp.where(incl, gc - gr, 0.0)), 0.0)
    kb = k * b
    vb = v * b
    e = jnp.exp(gc)
    a = p = None
    if scores:
        kk_qk = _bm_nt(jnp.concatenate([kb, q], axis=1), k)
        a = jnp.where(strict, kk_qk[:, :CHUNK] * dm, 0.0)
        p = jnp.where(incl, kk_qk[:, CHUNK:] * dm, 0.0)
    gl = gc[:, CHUNK - 1:CHUNK, :]
    eg = jnp.exp(gl - gc)
    return rows, q, k, v, b, gc, dm, kb, vb, e, a, p, gl, eg


def _unit_lower_inverse(a, eye):
    x = -a
    tm = eye + x
    xp = _bm(x, x)
    for _ in range(4):
        both = _bm(jnp.concatenate([xp, tm], axis=1), xp)
        tm = tm + both[:, CHUNK:]
        xp = both[:, :CHUNK]
    return tm + _bm(tm, xp)


def _gdn_specs(t_len, n_chunks, reverse):
    cps = GDN_CHUNKS_PER_STEP
    steps = n_chunks // cps
    at = (lambda g: steps - 1 - g) if reverse else (lambda g: g)
    rows_blk = lambda width, part=0: pl.BlockSpec((cps * CHUNK, width), lambda g: (at(g), part))
    gate_r = pl.BlockSpec((GDN_HEADS, cps, 1, CHUNK), lambda g: (0, at(g), 0, 0))
    per_chunk = lambda r, c: pl.BlockSpec((GDN_HEADS, cps, r, c), lambda g: (0, at(g), 0, 0))
    return cps, steps, rows_blk, gate_r, per_chunk


def _gdn_fwd_call(gact, beta_c, gam_c, gam_r, t_len):
    n_chunks = t_len // CHUNK
    dk = GDN_HEAD_DIM
    width = GDN_HEADS * dk
    cps, steps, rows_blk, gate_r, per_chunk = _gdn_specs(t_len, n_chunks, False)

    def body(q_ref, k_ref, v_ref, b_ref, gc_ref, gr_ref, o_ref, s_ref, t_ref, a_ref, p_ref, uw_ref, vn_ref, state_ref):
        row = lax.broadcasted_iota(jnp.int32, (CHUNK, CHUNK), 0)
        col = lax.broadcasted_iota(jnp.int32, (CHUNK, CHUNK), 1)
        incl, strict = row >= col, row > col
        eye = (row == col).astype(F32)

        @pl.when(pl.program_id(0) == 0)
        def _():
            state_ref[...] = jnp.zeros_like(state_ref)

        _, q, k, v, b, gc, dm, kb, vb, e, a, p, gl, eg = _chunk_terms(
            q_ref, k_ref, v_ref, b_ref, gc_ref, gr_ref, 0, incl, strict, cps)
        tm = _unit_lower_inverse(a, eye)
        uw = _bm(tm, jnp.concatenate([vb, kb * e], axis=2))
        w_qe = jnp.concatenate([uw[:, :, dk:], q * e], axis=1)
        u, kd, decay = uw[:, :, :dk], k * eg, jnp.exp(gl)
        per_chunk_block = lambda x: x.reshape(GDN_HEADS, cps, CHUNK, CHUNK)
        t_ref[...], a_ref[...], p_ref[...] = per_chunk_block(tm), per_chunk_block(a), per_chunk_block(p)
        uw_heads = uw.reshape(GDN_HEADS, cps * CHUNK, 2 * dk)
        for h in range(GDN_HEADS):
            uw_ref[:, h * 2 * dk:(h + 1) * 2 * dk] = uw_heads[h]

        of_chunk = lambda x, c: jnp.stack([x[h * cps + c] for h in range(GDN_HEADS)])
        s = state_ref[...]
        for c in range(cps):
            ws_qs = _bm(of_chunk(w_qe, c), s)
            vn = of_chunk(u, c) - ws_qs[:, :CHUNK]
            o = ws_qs[:, CHUNK:] + _bm(of_chunk(p, c), vn)
            for h in range(GDN_HEADS):
                o_ref[c * CHUNK:(c + 1) * CHUNK, h * dk:(h + 1) * dk] = o[h]
                vn_ref[c * CHUNK:(c + 1) * CHUNK, h * dk:(h + 1) * dk] = vn[h]
            s_ref[:, c] = s
            s = s * of_chunk(decay, c) + _bm_tn(of_chunk(kd, c), vn)
        state_ref[...] = s

    scores = jax.ShapeDtypeStruct((GDN_HEADS, n_chunks, CHUNK, CHUNK), F32)
    return pl.pallas_call(
        body, name="gdn_fwd",
        grid=(steps,),
        in_specs=[rows_blk(width, 0), rows_blk(width, 1), rows_blk(width, 2), rows_blk(LANES), rows_blk(LANES), gate_r],
        out_specs=[rows_blk(width), per_chunk(dk, dk), per_chunk(CHUNK, CHUNK), per_chunk(CHUNK, CHUNK),
                   per_chunk(CHUNK, CHUNK), rows_blk(2 * width), rows_blk(width)],
        out_shape=[jax.ShapeDtypeStruct((t_len, width), F32),
                   jax.ShapeDtypeStruct((GDN_HEADS, n_chunks, dk, dk), F32), scores, scores, scores,
                   jax.ShapeDtypeStruct((t_len, 2 * width), F32), jax.ShapeDtypeStruct((t_len, width), F32)],
        scratch_shapes=[pltpu.VMEM((GDN_HEADS, dk, dk), F32)],
        compiler_params=_params("arbitrary"),
    )(gact, gact, gact, beta_c, gam_c, gam_r)


def _gdn_bwd_call(gact, beta_c, gam_c, gam_r, saved, d_o, t_len, scatter=()):
    n_chunks = t_len // CHUNK
    dk = GDN_HEAD_DIM
    width = GDN_HEADS * dk
    cps, steps, rows_blk, gate_r, per_chunk = _gdn_specs(t_len, n_chunks, True)
    nx = len(scatter)
    n_in = 13

    def body(*refs):
        q_ref, k_ref, v_ref, b_ref, gc_ref, gr_ref = refs[:6]
        saved_refs, do_ref = refs[6:12], refs[12]
        d_ref, dgate_ref = refs[n_in + nx:n_in + 2 + nx]
        dstate_ref = refs[n_in + 2 + 2 * nx]
        copies = lambda: _direct_copies(refs[n_in:n_in + nx], refs[n_in + 2 + nx:n_in + 2 + 2 * nx],
                                        *refs[n_in + 3 + 2 * nx:], (True,) * nx)
        if nx:
            pl.when(pl.program_id(0) == 0)(lambda: _start_all(copies()))
        row = lax.broadcasted_iota(jnp.int32, (CHUNK, CHUNK), 0)
        col = lax.broadcasted_iota(jnp.int32, (CHUNK, CHUNK), 1)
        incl, strict = row >= col, row > col
        ng = GDN_BWD_GROUP
        nb = GDN_HEADS * ng
        upper = jnp.broadcast_to((row <= col).astype(F32), (nb, CHUNK, CHUNK))
        ones = jnp.ones((nb, CHUNK, LANES), F32)
        last_row = lax.broadcasted_iota(jnp.int32, (CHUNK, 1), 0) == CHUNK - 1
        lane_ids = lax.broadcasted_iota(jnp.int32, (1, LANES), 1)
        rsum = lambda m: jnp.sum(m, axis=-1, keepdims=True)
        total = lambda m: jnp.sum(rsum(m), axis=1, keepdims=True)
        of_chunk = lambda x, c: jnp.stack([x[h * ng + c] for h in range(GDN_HEADS)])

        @pl.when(pl.program_id(0) == 0)
        def _():
            dstate_ref[...] = jnp.zeros_like(dstate_ref)

        for c0 in range(cps - ng, -1, -ng):
            group(c0, q_ref, k_ref, v_ref, b_ref, gc_ref, gr_ref, saved_refs, do_ref, d_ref, dgate_ref, dstate_ref,
                  incl, strict, upper, ones, last_row, lane_ids, rsum, total, of_chunk)
        if nx:
            pl.when(pl.program_id(0) == steps - 1)(lambda: _wait_all(copies()))

    def group(c0, q_ref, k_ref, v_ref, b_ref, gc_ref, gr_ref, saved_refs, do_ref, d_ref, dgate_ref, dstate_ref,
              incl, strict, upper, ones, last_row, lane_ids, rsum, total, of_chunk):
        ng = GDN_BWD_GROUP
        nb = GDN_HEADS * ng
        rows = pl.ds(c0 * CHUNK, ng * CHUNK)
        s_ref, t_ref, a_ref, p_ref, uw_ref, vn_ref = saved_refs
        _, q, k, v, b, gc, dm, kb, vb, e, _, _, gl, eg = _chunk_terms(
            q_ref, k_ref, v_ref, b_ref, gc_ref, gr_ref, c0, incl, strict, ng, scores=False)
        s = s_ref[:, c0:c0 + ng].reshape(nb, dk, dk)
        tm = t_ref[:, c0:c0 + ng].reshape(nb, CHUNK, CHUNK)
        a = a_ref[:, c0:c0 + ng].reshape(nb, CHUNK, CHUNK)
        p = p_ref[:, c0:c0 + ng].reshape(nb, CHUNK, CHUNK)
        d_out = _heads_of(do_ref, rows).reshape(nb, CHUNK, dk)
        vn = _heads_of(vn_ref, rows).reshape(nb, CHUNK, dk)
        uw = jnp.stack([uw_ref[rows, h * 2 * dk:(h + 1) * 2 * dk] for h in range(GDN_HEADS)]).reshape(nb, CHUNK, 2 * dk)
        u, w = uw[:, :, :dk], uw[:, :, dk:]
        el = jnp.exp(gl)
        kbe = kb * e
        qe = q * e
        kd = k * eg
        bm, bm_nt, bm_tn = (functools.partial(f, right_low=False) for f in (_bm, _bm_nt, _bm_tn))
        pt_do = _bm_tn(p, d_out)
        qet_do = _bm_tn(qe, d_out)

        ds = dstate_ref[...]
        d_vn_c, ds_c = [None] * ng, [None] * ng
        for c in range(ng - 1, -1, -1):
            ds_c[c] = ds
            d_vn_c[c] = of_chunk(pt_do, c) + bm(of_chunk(kd, c), ds)
            ds = of_chunk(el, c) * ds + of_chunk(qet_do, c) - bm_tn(of_chunk(w, c), d_vn_c[c])
        dstate_ref[...] = ds
        by_chunk = lambda xs: jnp.stack([xs[c][h] for h in range(GDN_HEADS) for c in range(ng)])
        d_vn, ds = by_chunk(d_vn_c), by_chunk(ds_c)

        on_s = bm_nt(jnp.concatenate([d_out, d_vn], axis=1), s)
        d_qe, d_w = on_s[:, :CHUNK], -on_s[:, CHUNK:]
        d_p = jnp.where(incl, _bm_nt(d_out, vn), 0.0)
        d_kd = _bm_nt(vn, ds)
        d_both = _bm_tn(tm, jnp.concatenate([d_vn, d_w], axis=2))
        d_vb, d_kbe = d_both[:, :, :dk], d_both[:, :, dk:]
        d_a = -jnp.where(strict, _bm_nt(d_both, uw), 0.0)
        m = d_a * dm
        n = d_p * dm
        on_k = bm(jnp.concatenate([m, n], axis=1), k)
        d_kb = on_k[:, :CHUNK] + d_kbe * e
        d_q = on_k[:, CHUNK:] + d_qe * e
        d_k = (bm_tn(jnp.concatenate([m, n], axis=1), jnp.concatenate([kb, q], axis=1))
               + d_kd * eg + b * d_kb)
        d_v = b * d_vb
        r = d_a * a + d_p * p
        kd_term = rsum(d_kd * kd)
        d_gl = total(ds * s) * el + jnp.sum(kd_term, axis=1, keepdims=True)
        d_gam = (rsum(r) - _times_exact(r, ones, _BTN)[:, :, 0:1] + rsum(d_qe * qe) + rsum(d_kbe * kbe) - kd_term
                 + jnp.where(last_row, d_gl, 0.0))
        d_beta = rsum(d_kb * k) + rsum(d_vb * v)
        d_g = _exact_times(upper, d_gam * ones, _BNN)[:, :, 0:1]
        per_head = lambda x: x.reshape(GDN_HEADS, ng * CHUNK, x.shape[-1])
        d_q, d_k, d_v, d_beta, d_g = (per_head(x) for x in (d_q, d_k, d_v, d_beta, d_g))
        gates = jnp.zeros((ng * CHUNK, LANES), F32)
        for h in range(GDN_HEADS):
            lanes = slice(h * dk, (h + 1) * dk)
            d_ref[0, rows, lanes] = d_q[h]
            d_ref[1, rows, lanes] = d_k[h]
            d_ref[2, rows, lanes] = d_v[h]
            gates = gates + (jnp.where(lane_ids == h, d_beta[h], 0.0)
                             + jnp.where(lane_ids == GDN_HEADS + h, d_g[h], 0.0))
        dgate_ref[rows, :] = gates

    d_spec = pl.BlockSpec((3, cps * CHUNK, width), lambda g: (0, steps - 1 - g, 0))
    return pl.pallas_call(
        body, name="gdn_bwd",
        grid=(steps,),
        in_specs=[rows_blk(width, 0), rows_blk(width, 1), rows_blk(width, 2), rows_blk(LANES), rows_blk(LANES), gate_r,
                  per_chunk(dk, dk), per_chunk(CHUNK, CHUNK), per_chunk(CHUNK, CHUNK), per_chunk(CHUNK, CHUNK),
                  rows_blk(2 * width), rows_blk(width), rows_blk(width)] + [_HBM] * nx,
        out_specs=[d_spec, rows_blk(LANES)] + [_HBM] * nx,
        out_shape=[jax.ShapeDtypeStruct((3, t_len, width), F32),
                   jax.ShapeDtypeStruct((t_len, LANES), F32)] + _direct_out_shapes(scatter, (True,) * nx),
        scratch_shapes=[pltpu.VMEM((GDN_HEADS, dk, dk), F32)] + (_direct_semaphores(nx) if nx else []),
        compiler_params=_params("arbitrary"),
    )(gact, gact, gact, beta_c, gam_c, gam_r, *saved, d_o, *scatter)


def _group_sums(x, group):
    rows, width = x.shape
    lane = lax.broadcasted_iota(jnp.int32, (1, LANES), 1)
    out = []
    for t in range(width // LANES):
        seg = x[:, t * LANES:(t + 1) * LANES]
        if group == LANES:
            out.append(jnp.broadcast_to(jnp.sum(seg, axis=-1, keepdims=True), (rows, LANES)))
        else:
            low = jnp.sum(jnp.where(lane < group, seg, 0.0), axis=-1, keepdims=True)
            high = jnp.sum(jnp.where(lane < group, 0.0, seg), axis=-1, keepdims=True)
            out.append(jnp.where(lane < group, low, high))
    return jnp.concatenate(out, axis=1)


def _post_call(o_sb, o_gd, proj_gates, x, target, w_out, sbw, gdw, fw, tm=256):
    t_len, d = x.shape
    half = 512
    sb_blocks = half // LANES

    def body(osb_ref, ogd_ref, zsb_ref, zgd_ref, x_ref, tg_ref, wo_ref, sbw_ref, gdw_ref, fw_ref,
             dx2_ref, dosb_ref, dogd_ref, dz_ref, loss_ref, gfw_ref, gsb_ref, ggd_ref, gwo_ref):
        step = pl.program_id(0)

        @pl.when(step == 0)
        def _():
            loss_ref[...] = jnp.zeros_like(loss_ref)
            gfw_ref[...] = jnp.zeros_like(gfw_ref)
            gsb_ref[...] = jnp.zeros_like(gsb_ref)
            ggd_ref[...] = jnp.zeros_like(ggd_ref)
            gwo_ref[...] = jnp.zeros_like(gwo_ref)

        def head_forward(o, z, w, head_dim):
            r = lax.rsqrt(_group_sums(o * o, head_dim) * (1.0 / head_dim) + EPS)
            nrm = o * r * w
            sg = _sigmoid(z)
            return r, nrm, sg, nrm * (z * sg)

        def head_backward(d_m, o, z, w, head_dim, r, nrm, sg):
            d_n = d_m * (z * sg)
            d_z = d_m * nrm * (sg * (1.0 + z * (1.0 - sg)))
            dnw = d_n * w
            d_o = r * dnw - o * (r * r * r) * (_group_sums(dnw * o, head_dim) * (1.0 / head_dim))
            return d_o, d_z, jnp.sum(d_n * o * r, axis=0, keepdims=True)

        osb = jnp.concatenate([osb_ref[j] for j in range(sb_blocks)], axis=1)
        ogd, zsb, zgd = ogd_ref[...], zsb_ref[...], zgd_ref[...]
        sbw_v, gdw_v = sbw_ref[...], gdw_ref[...]
        r_sb, n_sb, sg_sb, m_sb = head_forward(osb, zsb, sbw_v, SB_HEAD_DIM)
        r_gd, n_gd, sg_gd, m_gd = head_forward(ogd, zgd, gdw_v, GDN_HEAD_DIM)
        mixed = jnp.concatenate([m_sb, m_gd], axis=1).astype(MXU_DTYPE)
        wo = wo_ref[...]
        x2 = x_ref[...] + jnp.dot(mixed, wo, preferred_element_type=F32)
        r2 = lax.rsqrt(jnp.mean(x2 * x2, axis=-1, keepdims=True) + EPS)
        fw_v = fw_ref[...]
        err = x2 * r2 * fw_v - tg_ref[...]
        loss_ref[...] += 0.5 * jnp.sum(jnp.sum(err * err, axis=-1, keepdims=True) * (1.0 / d))
        dy = err * (1.0 / d)
        gg = dy * fw_v
        dx2 = r2 * gg - x2 * ((r2 * r2 * r2) * jnp.mean(gg * x2, axis=-1, keepdims=True))
        gfw_ref[...] += jnp.sum(dy * x2 * r2, axis=0, keepdims=True)
        dx2_ref[...] = dx2
        dx2b = dx2.astype(MXU_DTYPE)
        d_mixed = lax.dot_general(dx2b, wo, _NT, preferred_element_type=F32)
        gwo_ref[...] += lax.dot_general(mixed, dx2b, _TN, preferred_element_type=F32)
        d_osb, d_zsb, gsb = head_backward(d_mixed[:, :half], osb, zsb, sbw_v, SB_HEAD_DIM, r_sb, n_sb, sg_sb)
        d_ogd, d_zgd, ggd = head_backward(d_mixed[:, half:], ogd, zgd, gdw_v, GDN_HEAD_DIM, r_gd, n_gd, sg_gd)
        for j in range(sb_blocks):
            dosb_ref[j] = d_osb[:, j * LANES:(j + 1) * LANES]
        dogd_ref[...] = d_ogd
        dz_ref[0] = d_zsb
        dz_ref[1] = d_zgd
        gsb_ref[...] += gsb
        ggd_ref[...] += ggd

    row_blk = lambda w: pl.BlockSpec((tm, w), lambda i: (i, 0))
    blocks_blk = pl.BlockSpec((sb_blocks, tm, LANES), lambda i: (0, i, 0))
    fixed = lambda r, w: pl.BlockSpec((r, w), lambda i: (0, 0))
    return pl.pallas_call(
        body, name="post",
        grid=(t_len // tm,),
        in_specs=[blocks_blk, row_blk(half),
                  pl.BlockSpec((tm, half), lambda i: (i, 0)),
                  pl.BlockSpec((tm, half), lambda i: (i, 1)),
                  row_blk(d), row_blk(d), fixed(d, d), fixed(1, half), fixed(1, half), fixed(1, d)],
        out_specs=[row_blk(d), blocks_blk, row_blk(half),
                   pl.BlockSpec((2, tm, half), lambda i: (DPROJ_GATE_SLOT // 2, i, 0)),
                   fixed(1, LANES), fixed(1, d), fixed(1, half), fixed(1, half), fixed(d, d)],
        out_shape=[jax.ShapeDtypeStruct((t_len, d), F32), jax.ShapeDtypeStruct((sb_blocks, t_len, LANES), F32),
                   jax.ShapeDtypeStruct((t_len, half), F32),
                   jax.ShapeDtypeStruct((len(DPROJ_PIECE_OF_SLOT), t_len, half), F32),
                     jax.ShapeDtypeStruct((1, LANES), F32), jax.ShapeDtypeStruct((1, d), F32),
                     jax.ShapeDtypeStruct((1, half), F32), jax.ShapeDtypeStruct((1, half), F32),
                     jax.ShapeDtypeStruct((d, d), F32)],
        compiler_params=_params("arbitrary"),
    )(o_sb, o_gd, proj_gates, proj_gates, x, target, w_out, sbw, gdw, fw)


def _piece_of_slot(s):
    return jnp.where(s < DPROJ_GDN_SLOT, s, jnp.where(s < DPROJ_GATE_SLOT, s + 1,
                                                     jnp.where(s == DPROJ_GATE_SLOT, 3, 7)))


def _gw_in_call(h_t, dproj8):
    d, t_len = h_t.shape
    n_piece, _, pw = dproj8.shape

    def body(ht_ref, dp_ref, gw_ref):
        gw_ref[...] = jnp.dot(ht_ref[...], dp_ref[0].astype(MXU_DTYPE), preferred_element_type=F32)

    return pl.pallas_call(
        body, name="gw_in",
        grid=(n_piece,),
        in_specs=[pl.BlockSpec((d, t_len), lambda s: (0, 0)),
                  pl.BlockSpec((1, t_len, pw), lambda s: (s, 0, 0))],
        out_specs=pl.BlockSpec((d, pw), lambda s: (0, _piece_of_slot(s))),
        out_shape=jax.ShapeDtypeStruct((d, n_piece * pw), F32),
        compiler_params=_params("arbitrary"),
    )(h_t, dproj8)


def _slot_of_piece(p):
    return jnp.where(p < DPROJ_GDN_SLOT, p, jnp.where(p == 3, DPROJ_GATE_SLOT, jnp.where(p < 7, p - 1, 7)))


def _gw_in_shards_call(h_t, dproj8, dsmall, out_dtype):
    d, t_len = h_t.shape
    n_piece, _, pw = dproj8.shape
    ns = dsmall.shape[1]
    n_pairs = N_DEV // 2

    def body(ht_ref, dp_ref, ds_ref, chip_ref, prev_ref, gates_ref, send_ref, recv_ref, send_sems, recv_sems):
        p = pl.program_id(0)
        x_pos, y_pos, c = lax.axis_index("x"), lax.axis_index("y"), lax.axis_index("c")
        to_sibling = lambda pair: pltpu.make_async_remote_copy(
            src_ref=send_ref.at[pair], dst_ref=recv_ref.at[pair], send_sem=send_sems.at[pair],
            recv_sem=recv_sems.at[pair], device_id=(x_pos, y_pos, 1 - c), device_id_type=_MESH)

        @pl.when(p == 0)
        def _():
            gates_ref[...] = jnp.dot(ht_ref[...], ds_ref[...].astype(MXU_DTYPE), preferred_element_type=F32)

        def emit(s, tail):
            x = jnp.concatenate([prev_ref[...], tail], axis=1)
            y = x if s == 0 else pltpu.roll(x, SHARD_PAD - s, axis=1)
            shard = y[:, :SHARD_COLS].astype(out_dtype)

            @pl.when(c == s % 2)
            def _():
                chip_ref[s // 2] = shard

            @pl.when(c != s % 2)
            def _():
                send_ref[s // 2] = shard
                to_sibling(s // 2).start()

        @pl.when(p < n_piece)
        def _():
            cur = jnp.dot(ht_ref[...], dp_ref[0].astype(MXU_DTYPE), preferred_element_type=F32)
            for s in range(n_piece - 1):
                pl.when(p == s + 1)(functools.partial(emit, s, cur[:, :SHARD_PAD - pw]))
            prev_ref[...] = cur

        @pl.when(p == n_piece)
        def _():
            emit(n_piece - 1, gates_ref[...])
            for pair in range(n_pairs):
                to_sibling(pair).wait_send()
            for pair in range(n_pairs):
                to_sibling(pair).wait_recv()
                chip_ref[pair] = (chip_ref[pair].astype(F32) + recv_ref[pair].astype(F32)).astype(out_dtype)

    shards_of_side = lambda: pltpu.VMEM((n_pairs, d, SHARD_COLS), out_dtype)
    return pl.pallas_call(
        body, name="gw_in",
        grid=(n_piece + 1,),
        in_specs=[pl.BlockSpec((d, t_len), lambda p: (0, 0)),
                  pl.BlockSpec((1, t_len, pw), lambda p: (_slot_of_piece(jnp.minimum(p, n_piece - 1)), 0, 0)),
                  pl.BlockSpec((t_len, ns), lambda p: (0, 0))],
        out_specs=pl.BlockSpec((n_pairs, d, SHARD_COLS), lambda p: (0, 0, 0)),
        out_shape=jax.ShapeDtypeStruct((n_pairs, d, SHARD_COLS), out_dtype),
        scratch_shapes=[pltpu.VMEM((d, pw), F32), pltpu.VMEM((d, ns), F32), shards_of_side(), shards_of_side(),
                        pltpu.SemaphoreType.DMA((n_pairs,)), pltpu.SemaphoreType.DMA((n_pairs,))],
        compiler_params=_params("arbitrary"),
    )(h_t, dproj8, dsmall)


def _gw_small_call(h_t, dsmall, tm=512):
    d, t_len = h_t.shape
    ns = dsmall.shape[1]

    def body(ht_ref, dp_ref, gw_ref):
        @pl.when(pl.program_id(0) == 0)
        def _():
            gw_ref[...] = jnp.zeros_like(gw_ref)

        gw_ref[...] += jnp.dot(ht_ref[...], dp_ref[...].astype(MXU_DTYPE), preferred_element_type=F32)

    return pl.pallas_call(
        body, name="gw_small",
        grid=(t_len // tm,),
        in_specs=[pl.BlockSpec((d, tm), lambda t: (0, t)),
                  pl.BlockSpec((tm, ns), lambda t: (t, 0))],
        out_specs=pl.BlockSpec((d, ns), lambda t: (0, 0)),
        out_shape=jax.ShapeDtypeStruct((d, ns), F32),
        compiler_params=_params("arbitrary"),
    )(h_t, dsmall)


def _dx_call(dproj8, dsmall, w_main, w_small, x, r, dx2, norm_w, chip_scatter=(), peer_scatter=(), tm=256):
    t_len, d = x.shape
    n_piece, _, pw = dproj8.shape
    ns = dsmall.shape[1]
    nx = len(chip_scatter)
    n_peer = len(peer_scatter)
    steps = t_len // tm
    n_in = 8 + nx + n_peer
    n_out = 2 + nx + n_peer + nx
    reduce_step = steps // 2

    def body(*refs):
        dp_ref, ds_ref, wm_ref, ws_ref, x_ref, r_ref, dx2_ref, nw_ref = refs[:8]
        gx_ref, gnw_ref = refs[n_in:n_in + 2]
        scratch = refs[n_in + n_out:]
        chip = lambda: _chip_reduce_copies(refs[8], refs[n_in + 2], refs[n_in + n_out - 1], scratch[-2], scratch[-1],
                                           *scratch[:3])
        if nx:
            @pl.when(pl.program_id(0) == 0)
            def _():
                cp = chip()
                for name in ("to_reduce", "direct", "keep", "load_own"):
                    for part in cp[name]:
                        part.start()

        @pl.when(pl.program_id(0) == 0)
        def _():
            gnw_ref[...] = jnp.zeros_like(gnw_ref)

        dh = lax.dot_general(ds_ref[...].astype(MXU_DTYPE), ws_ref[...], _NT, preferred_element_type=F32)
        for s, p in enumerate(DPROJ_PIECE_OF_SLOT):
            dh = dh + lax.dot_general(dp_ref[s].astype(MXU_DTYPE), wm_ref[:, p * pw:(p + 1) * pw], _NT,
                                      preferred_element_type=F32)
        xv, rv = x_ref[...], r_ref[...]
        dn = dh * nw_ref[...]
        gx_ref[...] = dx2_ref[...] + rv * dn - xv * ((rv * rv * rv) * jnp.mean(dn * xv, axis=-1, keepdims=True))
        gnw_ref[...] += jnp.sum(dh * xv * rv, axis=0, keepdims=True)

        if nx:
            @pl.when(pl.program_id(0) == reduce_step)
            def _():
                cp = chip()
                own_buf, got_buf = scratch[-2], scratch[-1]
                for part in cp["to_reduce"]:
                    part.wait_recv()
                cp["load_got"][0].start()
                cp["load_own"][0].wait()
                cp["load_got"][0].wait()
                own_buf[...] = (own_buf[...].astype(F32) + got_buf[...].astype(F32)).astype(own_buf.dtype)
                for part in cp["reduced"]:
                    part.start()

        @pl.when(pl.program_id(0) == steps - 1)
        def _():
            if n_peer:
                small_ref, parts_ref = refs[8 + nx:n_in]
                small_buf = scratch[6]
                small_buf[...] = small_ref[...]
                small_buf[0:1, :] = gnw_ref[...]
                peer_copies = _direct_copies([small_buf, parts_ref], refs[n_in + 2 + nx:n_in + 2 + nx + n_peer],
                                             *scratch[3:6], [False, True])
                _start_all(peer_copies)
            if nx:
                cp = chip()
                for name in ("to_reduce", "direct", "reduced"):
                    for part in cp[name]:
                        part.wait_send()
                for part in cp["direct"] + cp["reduced"]:
                    part.wait_recv()
                cp["keep"][0].wait()
            if n_peer:
                _wait_all(peer_copies)

    assert n_peer in (0, 2) and (nx == 1 or not n_peer)
    peer_in_specs = [pl.BlockSpec(peer_scatter[0].shape, lambda i: (0, 0)), _HBM] if n_peer else []
    peer_scratch = _direct_semaphores(n_peer) + [pltpu.VMEM(peer_scatter[0].shape, F32)] if n_peer else []
    return pl.pallas_call(
        body, name="dx",
        grid=(steps,),
        in_specs=[pl.BlockSpec((n_piece, tm, pw), lambda i: (0, i, 0)),
                  pl.BlockSpec((tm, ns), lambda i: (i, 0)),
                  pl.BlockSpec((d, n_piece * pw), lambda i: (0, 0)),
                  pl.BlockSpec((d, ns), lambda i: (0, 0)),
                  pl.BlockSpec((tm, d), lambda i: (i, 0)),
                  pl.BlockSpec((tm, 1), lambda i: (i, 0)),
                  pl.BlockSpec((tm, d), lambda i: (i, 0)),
                  pl.BlockSpec((1, d), lambda i: (0, 0))] + [_HBM] * nx + peer_in_specs,
        out_specs=[pl.BlockSpec((tm, d), lambda i: (i, 0)),
                   pl.BlockSpec((1, d), lambda i: (0, 0))] + [_HBM] * (nx + n_peer + nx),
        out_shape=[jax.ShapeDtypeStruct((t_len, d), F32), jax.ShapeDtypeStruct((1, d), F32)]
                  + [jax.ShapeDtypeStruct((N_CHIPS - 1,) + a.shape[1:], a.dtype) for a in chip_scatter]
                  + (_direct_out_shapes(peer_scatter, [False, True]) if n_peer else [])
                  + [jax.ShapeDtypeStruct(a.shape[1:], a.dtype) for a in chip_scatter],
        scratch_shapes=([pltpu.SemaphoreType.DMA((3 * CHIP_COPY_PARTS,))] * 2 + [pltpu.SemaphoreType.DMA((3,))]
                        if nx else []) + peer_scratch
                       + [pltpu.VMEM(a.shape[1:], a.dtype) for a in chip_scatter for _ in range(2)],
        compiler_params=_params("arbitrary"),
    )(dproj8, dsmall, w_main, w_small, x, r, dx2, norm_w, *chip_scatter, *peer_scatter)


def _direct_out_shapes(srcs, per_peer):
    return [jax.ShapeDtypeStruct(s.shape if pp else (N_DEV,) + s.shape, s.dtype) for s, pp in zip(srcs, per_peer)]


def _direct_semaphores(n):
    return [pltpu.SemaphoreType.DMA((n * (N_DEV - 1),)), pltpu.SemaphoreType.DMA((n * (N_DEV - 1),)),
            pltpu.SemaphoreType.DMA((n,))]


def _direct_copies(src_refs, out_refs, send_sems, recv_sems, local_sems, per_peer):
    x, y, c = lax.axis_index("x"), lax.axis_index("y"), lax.axis_index("c")
    me = 4 * x + 2 * y + c
    local, remote = [], []
    for a in range(len(src_refs)):
        mine = src_refs[a].at[me] if per_peer[a] else src_refs[a]
        local.append(pltpu.make_async_copy(mine, out_refs[a].at[me], local_sems.at[a]))
    for k in range(1, N_DEV):
        kx, ky, kc = (k >> 2) & 1, (k >> 1) & 1, k & 1
        px = 1 - x if kx else x
        py = 1 - y if ky else y
        pc = 1 - c if kc else c
        peer = 4 * px + 2 * py + pc
        for a in range(len(src_refs)):
            sem = a * (N_DEV - 1) + (k - 1)
            remote.append(pltpu.make_async_remote_copy(
                src_ref=src_refs[a].at[peer] if per_peer[a] else src_refs[a], dst_ref=out_refs[a].at[me],
                send_sem=send_sems.at[sem], recv_sem=recv_sems.at[sem],
                device_id=(px, py, pc), device_id_type=pl.DeviceIdType.MESH))
    return local, remote


def _start_all(copies):
    local, remote = copies
    for cp in local + remote:
        cp.start()


def _wait_all(copies):
    local, remote = copies
    for cp in remote:
        cp.wait_send()
    for cp in remote:
        cp.wait_recv()
    for cp in local:
        cp.wait()


N_CHIPS = 4
CHIP_COPY_PARTS = 4
_HBM = pl.BlockSpec(memory_space=pl.ANY)
_MESH = pl.DeviceIdType.MESH


def _gather_call(name, srcs):
    n = len(srcs)
    per = N_DEV - 1

    def body(*refs):
        src_refs, out_refs = refs[:n], refs[n:2 * n]
        send_sems, recv_sems, local_sems = refs[2 * n:]
        x, y, c = lax.axis_index("x"), lax.axis_index("y"), lax.axis_index("c")
        me, sibling = (x, y, c), (x, y, 1 - c)
        x_nbr, y_nbr, diagonal = (1 - x, y), (x, 1 - y), (1 - x, 1 - y)
        held = ((1 - x) * c + x * (1 - c), y * c + (1 - y) * (1 - c))
        onward = (x * c + (1 - x) * (1 - c), (1 - y) * c + y * (1 - c))
        slot = lambda px, py, pc: 4 * px + 2 * py + pc

        def copy(a, k, block, to, from_src=False):
            rows = out_refs[a].at[slot(*block)]
            return pltpu.make_async_remote_copy(
                src_ref=src_refs[a] if from_src else rows, dst_ref=rows,
                send_sem=send_sems.at[a * per + k], recv_sem=recv_sems.at[a * per + k],
                device_id=to, device_id_type=_MESH)

        local = [pltpu.make_async_copy(src_refs[a], out_refs[a].at[slot(*me)], local_sems.at[a]) for a in range(n)]
        started = []

        def start(cp):
            cp.start()
            started.append(cp)

        for cp in local:
            cp.start()
        for a in range(n):
            start(copy(a, 0, me, sibling, True))
            start(copy(a, 1, me, (*x_nbr, c), True))
            start(copy(a, 2, me, (*y_nbr, c), True))
        for a in range(n):
            copy(a, 1, (*x_nbr, c), me).wait_recv()
            copy(a, 2, (*y_nbr, c), me).wait_recv()
            start(copy(a, 3, (*held, c), (*onward, c)))
            start(copy(a, 4, (*x_nbr, c), sibling))
            start(copy(a, 5, (*y_nbr, c), sibling))
        for a in range(n):
            copy(a, 3, (*diagonal, c), me).wait_recv()
            start(copy(a, 6, (*diagonal, c), sibling))
        for a in range(n):
            copy(a, 0, sibling, me).wait_recv()
            for k, chip in ((4, x_nbr), (5, y_nbr), (6, diagonal)):
                copy(a, k, (*chip, 1 - c), me).wait_recv()
        for cp in started:
            cp.wait_send()
        for cp in local:
            cp.wait()

    return pl.pallas_call(
        body, name=name,
        in_specs=[_HBM] * n, out_specs=[_HBM] * n,
        out_shape=[jax.ShapeDtypeStruct((N_DEV,) + s.shape, s.dtype) for s in srcs],
        scratch_shapes=[pltpu.SemaphoreType.DMA((n * per,)), pltpu.SemaphoreType.DMA((n * per,)),
                        pltpu.SemaphoreType.DMA((n,))],
    )(*srcs)


def _chip_reduce_copies(src_ref, out_ref, stage_ref, own_buf, got_buf, send_sems, recv_sems, local_sems):
    x, y, c = lax.axis_index("x"), lax.axis_index("y"), lax.axis_index("c")
    via = (c * (1 - x) + (1 - c) * x, c * y + (1 - c) * (1 - y))
    other = (c * x + (1 - c) * (1 - x), c * (1 - y) + (1 - c) * y)
    block = lambda chip: src_ref.at[2 * chip[0] + chip[1]]
    rows = own_buf.shape[0] // CHIP_COPY_PARTS

    def remote(k, src, dst, chip):
        return [pltpu.make_async_remote_copy(
            src_ref=src.at[pl.ds(h * rows, rows)], dst_ref=dst.at[pl.ds(h * rows, rows)],
            send_sem=send_sems.at[k * CHIP_COPY_PARTS + h], recv_sem=recv_sems.at[k * CHIP_COPY_PARTS + h],
            device_id=(chip[0], chip[1], c), device_id_type=_MESH) for h in range(CHIP_COPY_PARTS)]

    local = lambda k, src, dst: [pltpu.make_async_copy(src, dst, local_sems.at[k])]
    return dict(keep=local(0, block((x, y)), out_ref.at[0]),
                to_reduce=remote(0, block((1 - x, 1 - y)), stage_ref, via),
                direct=remote(1, block(via), out_ref.at[1], via),
                load_own=local(1, block(other), own_buf),
                load_got=local(2, stage_ref, got_buf),
                reduced=remote(2, own_buf, out_ref.at[2], other))


def _adam_call(name, parts, w, m, v, tr):
    rows, cols = w.shape
    n_slots = parts.shape[0]

    def body(p_ref, w_ref, m_ref, v_ref, g_ref, d_ref, nm_ref, nv_ref):
        g = p_ref[0].astype(F32)
        for s in range(1, n_slots):
            g = g + p_ref[s].astype(F32)
        m_new = ADAM_B1 * m_ref[...] + (1.0 - ADAM_B1) * g
        v_new = ADAM_B2 * v_ref[...] + (1.0 - ADAM_B2) * (g * g)
        m_hat = m_new / (1.0 - ADAM_B1 ** ADAM_STEP)
        v_hat = v_new / (1.0 - ADAM_B2 ** ADAM_STEP)
        g_ref[...] = g
        d_ref[...] = -ADAM_LR * (m_hat / (jnp.sqrt(v_hat) + ADAM_EPS) + ADAM_WD * w_ref[...])
        nm_ref[...] = m_new
        nv_ref[...] = v_new

    blk = pl.BlockSpec((tr, cols), lambda i: (i, 0))
    return pl.pallas_call(
        body, name=name,
        grid=(rows // tr,),
        in_specs=[pl.BlockSpec((n_slots, tr, cols), lambda i: (0, i, 0)), blk, blk, blk],
        out_specs=[blk] * 4,
        out_shape=[jax.ShapeDtypeStruct((rows, cols), F32)] * 4,
        compiler_params=_params("arbitrary"),
    )(parts, w, m, v)


def _columns_to_rows_call(name, w_t, rows, dtype):
    row_tiles = rows // LANES
    cols = w_t.shape[0] // row_tiles
    whole = cols // LANES * LANES

    def body(w_ref, out_ref):
        diagonal = (lax.broadcasted_iota(jnp.int32, (LANES, LANES), 0)
                    == lax.broadcasted_iota(jnp.int32, (LANES, LANES), 1))
        for a in range(row_tiles):
            out_ref[a * LANES:(a + 1) * LANES, :whole] = (
                w_ref[pl.ds(a, whole, stride=row_tiles), :].T.astype(dtype))
            for c in range(whole, cols):
                column = w_ref[pl.ds(c * row_tiles + a, 1), :]
                upright = jnp.sum(jnp.where(diagonal, column, 0.0), axis=1, keepdims=True)
                out_ref[a * LANES:(a + 1) * LANES, c:c + 1] = upright.astype(dtype)

    vm = pl.BlockSpec(memory_space=pltpu.VMEM)
    return pl.pallas_call(
        body, name=name,
        in_specs=[vm], out_specs=vm,
        out_shape=jax.ShapeDtypeStruct((rows, cols), dtype),
        compiler_params=pltpu.CompilerParams(vmem_limit_bytes=VMEM_LIMIT_BYTES),
    )(w_t)


def _adam_columns_call(name, parts, w_t, m_t, v_t):
    n_slots, rows, cols = parts.shape
    row_tiles = rows // LANES
    cols_pad = -(-cols // LANES) * LANES

    def body(p_ref, w_ref, m_ref, v_ref, *out_refs):
        for a in range(row_tiles):
            g = p_ref[0, a * LANES:(a + 1) * LANES, :].astype(F32)
            for s in range(1, n_slots):
                g = g + p_ref[s, a * LANES:(a + 1) * LANES, :].astype(F32)
            g = jnp.concatenate([g, jnp.zeros((LANES, cols_pad - cols), F32)], axis=1).T[:cols]
            column_rows = pl.ds(a, cols, stride=row_tiles)
            results = (g,) + _adamw(g, w_ref[column_rows, :], m_ref[column_rows, :], v_ref[column_rows, :])
            for out_ref, val in zip(out_refs, results):
                out_ref[column_rows, :] = val

    vm = pl.BlockSpec(memory_space=pltpu.VMEM)
    return pl.pallas_call(
        body, name=name,
        in_specs=[vm] * 4, out_specs=[vm] * 4,
        out_shape=[jax.ShapeDtypeStruct(w_t.shape, F32)] * 4,
        compiler_params=pltpu.CompilerParams(vmem_limit_bytes=VMEM_LIMIT_BYTES),
    )(parts, w_t, m_t, v_t)


N_PIECES = 8
PIECE = 512
SHARD_COLS = 513
SHARD_PAD = 640
RELAYOUT_ROWS = 256


def _from_shards_call(shards):
    _, d, _ = shards.shape
    tr = RELAYOUT_ROWS

    def body(p_ref, m_ref, s_ref):
        lane = lax.broadcasted_iota(jnp.int32, (tr, SHARD_PAD), 1)
        pad = jnp.zeros((tr, SHARD_PAD - SHARD_COLS), p_ref.dtype)
        sh = [jnp.concatenate([p_ref[s], pad], axis=1) for s in range(N_DEV)]
        for p in range(N_PIECES):
            y = sh[p] if p == 0 else pltpu.roll(sh[p], p, axis=1)
            if p > 0:
                y = jnp.where(lane < p, pltpu.roll(sh[p - 1], SHARD_PAD - (SHARD_COLS - p), axis=1), y)
            m_ref[:, p * PIECE:(p + 1) * PIECE] = y[:, :PIECE].astype(m_ref.dtype)
        first_gate = N_PIECES * PIECE - (N_DEV - 1) * SHARD_COLS
        s_ref[...] = pltpu.roll(sh[N_DEV - 1], SHARD_PAD - first_gate, axis=1)[:, :LANES].astype(s_ref.dtype)

    return pl.pallas_call(
        body, name="w_in_from_shards",
        grid=(d // tr,),
        in_specs=[pl.BlockSpec((N_DEV, tr, SHARD_COLS), lambda i: (0, i, 0))],
        out_specs=[pl.BlockSpec((tr, N_PIECES * PIECE), lambda i: (i, 0)), pl.BlockSpec((tr, LANES), lambda i: (i, 0))],
        out_shape=[jax.ShapeDtypeStruct((d, N_PIECES * PIECE), shards.dtype),
                   jax.ShapeDtypeStruct((d, LANES), shards.dtype)],
        compiler_params=_params("arbitrary"),
    )(shards)


def _adamw(g, w, m, v):
    m_new = ADAM_B1 * m + (1.0 - ADAM_B1) * g
    v_new = ADAM_B2 * v + (1.0 - ADAM_B2) * (g * g)
    m_hat = m_new / (1.0 - ADAM_B1 ** ADAM_STEP)
    v_hat = v_new / (1.0 - ADAM_B2 ** ADAM_STEP)
    return -ADAM_LR * (m_hat / (jnp.sqrt(v_hat) + ADAM_EPS) + ADAM_WD * w), m_new, v_new


def _adam_small_call(parts, ws, ms, vs):
    n = len(ws)
    n_slots = parts.shape[0]

    def body(*refs):
        p_ref = refs[0]
        w_refs, m_refs, v_refs = refs[1:1 + n], refs[1 + n:1 + 2 * n], refs[1 + 2 * n:1 + 3 * n]
        loss_ref = refs[1 + 3 * n]
        outs = refs[2 + 3 * n:]
        g_all = p_ref[0]
        for s in range(1, n_slots):
            g_all = g_all + p_ref[s]
        loss_ref[...] = g_all[n:n + 1, 0:1]
        for r in range(n):
            size = w_refs[r].shape[1]
            g = g_all[r:r + 1, :size]
            delta, m_new, v_new = _adamw(g, w_refs[r][...], m_refs[r][...], v_refs[r][...])
            for kind, val in enumerate((g, delta, m_new, v_new)):
                outs[kind * n + r][...] = val

    vm = pl.BlockSpec(memory_space=pltpu.VMEM)
    shapes = [jax.ShapeDtypeStruct(w.shape, F32) for w in ws]
    return pl.pallas_call(
        body, name="adam_small",
        in_specs=[vm] * (1 + 3 * n), out_specs=[vm] * (1 + 4 * n),
        out_shape=[jax.ShapeDtypeStruct((1, 1), F32)] + shapes * 4,
    )(parts, *ws, *ms, *vs)


_SMALL_ROWS = ("norm1_w", "final_norm_w", "sb_norm_w", "gdn_norm_w", "gdn_A_log", "gdn_dt_bias", "loss")


def _pack_small(vals, width):
    rows = [jnp.pad(a.reshape(1, -1).astype(F32), ((0, 0), (0, width - a.size))) for a in vals]
    rows += [jnp.zeros((1, width), F32)] * (8 - len(rows))
    return jnp.concatenate(rows, axis=0)


def _device_step(x2d, tgt, w_main, w_small, w_out_full, conv_full, norm1_w, sb_norm_w, gdn_A_log, gdn_dt_bias,
                 gdn_norm_w, final_norm_w, distributed=False):
    t_len, d = x2d.shape
    n_chunks = t_len // CHUNK
    w_main, w_small, w_out_full = (a.astype(MXU_DTYPE) for a in (w_main, w_small, w_out_full))
    w_small_t = w_small[:, :2 * GDN_HEADS].T

    pad_lanes = lambda a, lo: jnp.pad(a.reshape(1, -1), ((0, 0), (lo, LANES - lo - a.size)))
    alog_l, dtb_l = pad_lanes(gdn_A_log, GDN_HEADS), pad_lanes(gdn_dt_bias, GDN_HEADS)
    alog_c, dtb_c = alog_l[:, :8].T, dtb_l[:, :8].T
    sbw = jnp.tile(sb_norm_w, (1, 512 // SB_HEAD_DIM))
    gdw = jnp.tile(gdn_norm_w, (1, 512 // GDN_HEAD_DIM))
    fw = final_norm_w.reshape(1, d)

    if distributed:
        proj_cols, proj_gates, ps, pst, h_t, r1, w_out_g, conv_g = _inproj_call(
            x2d, norm1_w, w_main, w_small, w_small_t, gather=(w_out_full, conv_full))
        w_out_full = w_out_g.reshape(d, d)
        conv_full = conv_g.transpose(1, 0, 2).reshape(CONV_WIDTH, N_DEV * conv_g.shape[2])
    else:
        proj_cols, proj_gates, ps, pst, h_t, r1 = _inproj_call(x2d, norm1_w, w_main, w_small, w_small_t)
    o_sb, sp_total, sb_blocks_run = _sb_fwd_call(proj_cols, t_len)
    gact = _gdn_prep_call(proj_cols, conv_full, t_len, after=sp_total)
    beta_l, gcol_l, grow = _gdn_gates_call(ps, pst, alog_l, dtb_l, alog_c, dtb_c, t_len)
    gam_r = grow[GDN_HEADS:2 * GDN_HEADS].reshape(GDN_HEADS, n_chunks, 1, CHUNK)
    o_gd, *gdn_saved = _gdn_fwd_call(gact, beta_l, gcol_l, gam_r, t_len)

    (dx2, d_osb, d_ogd, dproj8, loss_p, g_fw, g_sbw, g_gdw, g_wout) = _post_call(
        o_sb, o_gd, proj_gates, x2d, tgt, w_out_full, sbw, gdw, fw)

    dproj8 = _sb_bwd_call(proj_cols, sp_total, sb_blocks_run, d_osb, dproj8, t_len)
    if distributed:
        d_gact3, d_gates, g_wout = _gdn_bwd_call(gact, beta_l, gcol_l, gam_r, gdn_saved, d_ogd, t_len,
                                                 scatter=(g_wout.reshape(N_DEV, d // N_DEV, d),))
    else:
        d_gact3, d_gates = _gdn_bwd_call(gact, beta_l, gcol_l, gam_r, gdn_saved, d_ogd, t_len)
    dproj8, g_conv = _gdn_prep_bwd_call(proj_cols, conv_full, d_gact3, dproj8, t_len)
    dsmall, g_alog, g_dtb = _gdn_gates_bwd_call(ps, alog_l, dtb_l, d_gates, t_len)

    if distributed:
        chip_partials = _gw_in_shards_call(h_t, dproj8, dsmall, WIRE_DTYPE)
        fold = lambda a, group: a.reshape(-1, group).sum(axis=0)
        small_g = _pack_small([jnp.zeros((d,), F32), g_fw, fold(g_sbw, SB_HEAD_DIM), fold(g_gdw, GDN_HEAD_DIM),
                               g_alog[0, GDN_HEADS:2 * GDN_HEADS], g_dtb[0, GDN_HEADS:2 * GDN_HEADS],
                               loss_p[0, :1]], d)
        conv_cols = g_conv.shape[1] // N_DEV
        g_conv_parts = g_conv.reshape(CONV_WIDTH, N_DEV, conv_cols).transpose(1, 0, 2)
        grad_x, _, g_w_in, p_small, p_conv, _ = _dx_call(dproj8, dsmall, w_main, w_small, x2d, r1, dx2, norm1_w,
                                                         chip_scatter=(chip_partials,),
                                                         peer_scatter=(small_g, g_conv_parts))
        return grad_x, g_w_in, g_wout, p_small, p_conv
    else:
        grad_x, g_n1 = _dx_call(dproj8, dsmall, w_main, w_small, x2d, r1, dx2, norm1_w)
        g_w_in = (_gw_in_call(h_t, dproj8), _gw_small_call(h_t, dsmall))
    return (loss_p, grad_x, g_n1, g_w_in, g_sbw, g_conv, g_alog, g_dtb, g_gdw, g_wout, g_fw)


def kernel(x, norm1_w, w_in, sb_norm_w, gdn_conv_w, gdn_A_log, gdn_dt_bias, gdn_norm_w, w_out, final_norm_w, loss_target, m_norm1_w, m_w_in, m_sb_norm_w, m_gdn_conv_w, m_gdn_A_log, m_gdn_dt_bias, m_gdn_norm_w, m_w_out, m_final_norm_w, v_norm1_w, v_w_in, v_sb_norm_w, v_gdn_conv_w, v_gdn_A_log, v_gdn_dt_bias, v_gdn_norm_w, v_w_out, v_final_norm_w):
    d = x.shape[2]
    shard_cols = w_in.shape[2]

    columns = lambda a: a.transpose(2, 0, 1).reshape(shard_cols * d // LANES, LANES)
    from_columns = lambda a: a.reshape(shard_cols, d // LANES, LANES).transpose(1, 2, 0).reshape(1, d, shard_cols)
    (w_in_g,) = _gather_call("gather_weights", [_columns_to_rows_call("w_in_to_wire", columns(w_in), d, WIRE_DTYPE)])
    w_main, w_small = _from_shards_call(w_in_g)

    grad_x, p_w_in, p_wout, p_small, p_conv = _device_step(
        x[0], loss_target[0], w_main, w_small, w_out[0].astype(WIRE_DTYPE), gdn_conv_w[0], norm1_w, sb_norm_w,
        gdn_A_log, gdn_dt_bias, gdn_norm_w, final_norm_w, distributed=True)

    r_w_in = [from_columns(a) for a in _adam_columns_call("adam_w_in", p_w_in, columns(w_in), columns(m_w_in),
                                                          columns(v_w_in))]
    r_wout = _adam_call("adam_w_out", p_wout, w_out[0], m_w_out[0], v_w_out[0], d // N_DEV)
    r_conv = _adam_call("adam_conv", p_conv, gdn_conv_w[0], m_gdn_conv_w[0], v_gdn_conv_w[0], CONV_WIDTH)

    row = lambda a: a.reshape(1, -1)
    n_small = len(_SMALL_ROWS) - 1
    r_small = _adam_small_call(
        p_small,
        [norm1_w, row(final_norm_w), sb_norm_w, gdn_norm_w, gdn_A_log, gdn_dt_bias],
        [m_norm1_w, row(m_final_norm_w), m_sb_norm_w, m_gdn_norm_w, m_gdn_A_log, m_gdn_dt_bias],
        [v_norm1_w, row(v_final_norm_w), v_sb_norm_w, v_gdn_norm_w, v_gdn_A_log, v_gdn_dt_bias])

    def small_out(kind, name):
        out = r_small[1 + kind * n_small + _SMALL_ROWS.index(name)]
        return out.reshape(final_norm_w.shape) if name == "final_norm_w" else out

    def outputs(kind):
        return (small_out(kind, "norm1_w"), r_w_in[kind], small_out(kind, "sb_norm_w"), r_conv[kind][None],
                small_out(kind, "gdn_A_log"), small_out(kind, "gdn_dt_bias"), small_out(kind, "gdn_norm_w"),
                r_wout[kind][None], small_out(kind, "final_norm_w"))

    return (r_small[0][0, 0], grad_x[None], *outputs(0), *outputs(1), *outputs(2), *outputs(3))
```
